```python
import math
import jax
import jax.numpy as jnp
from jax import lax
import numpy as np

D_MODEL = 1024
BATCH = 8
SEQ = 8192
DEPTH = 2

GRID_W = 64
CTX_LEN = 256
EPS = 1e-6
ROPE_BASE = 10000.0

SSM_HEADS = 16
SSM_HEAD_DIM = 64
SSM_INNER = SSM_HEADS * SSM_HEAD_DIM
SSM_GROUPS = 2
SSM_STATE = 128
SSM_CONV = 5
SSM_CHUNK = 128
SSM_CONV_DIM = SSM_INNER + 2 * SSM_GROUPS * SSM_STATE

SWA_Q_HEADS = 8
SWA_KV_HEADS = 2
SWA_HEAD_DIM = 128
SWA_WINDOW = 128
SWA_BLOCK = 128

MLA_HEADS = 8
MLA_Q_RANK = 384
MLA_KV_RANK = 256
MLA_NOPE = 128
MLA_ROPE = 64
MLA_QK = MLA_NOPE + MLA_ROPE
MLA_V = 128
MLA_Q_BLOCK = 128

N_BRANCH = 3
FFN_HIDDEN = -(-8 * D_MODEL // (3 * 256)) * 256

IN_WIDTHS = (
    SSM_CONV_DIM,
    2 * SSM_HEADS,
    SWA_KV_HEADS * SWA_HEAD_DIM,
    SWA_KV_HEADS * SWA_HEAD_DIM,
    MLA_KV_RANK,
    MLA_ROPE,
    SSM_INNER,
    SWA_Q_HEADS * SWA_HEAD_DIM,
    MLA_Q_RANK,
    N_BRANCH * D_MODEL,
)
N_KV_SPLITS = 6
KV_COLS = sum(IN_WIDTHS[:N_KV_SPLITS])
IN_COLS = sum(IN_WIDTHS)

kernel_name = 'hybrid_ssd_swa_mla_dit_block'

F32 = jnp.float32


def rms_norm(x, g):
    xf = x.astype(F32)
    y = xf * lax.rsqrt(jnp.mean(xf * xf, axis=-1, keepdims=True) + EPS)
    return (y * g.astype(F32)).astype(x.dtype)


def modulate(x, g, shift, scale):
    return rms_norm(x, g) * (1.0 + scale[:, None, :]) + shift[:, None, :]


def split_cols(u, widths):
    offs = np.cumsum(widths)[:-1].tolist()
    return jnp.split(u, offs, axis=-1)


def axial_rope_tables(rows, rot_dim):
    n_freq = rot_dim // 4
    inv = jnp.power(ROPE_BASE, -jnp.arange(n_freq, dtype=F32) / n_freq)
    r, col = jnp.meshgrid(jnp.arange(rows, dtype=F32), jnp.arange(GRID_W, dtype=F32), indexing='ij')
    ang = jnp.stack([r.reshape(-1)[:, None] * inv, col.reshape(-1)[:, None] * inv], axis=1)
    return jnp.cos(ang), jnp.sin(ang)


def apply_axial_rope(x, cos, sin):
    shp = x.shape
    xr = x.astype(F32).reshape(shp[:-1] + (2, 2, shp[-1] // 4))
    x1, x2 = xr[..., 0, :], xr[..., 1, :]
    c = cos[None, :, None]
    s = sin[None, :, None]
    out = jnp.stack([x1 * c - x2 * s, x2 * c + x1 * s], axis=-2)
    return out.reshape(shp).astype(x.dtype)


def centred_dwconv(u, w, b):
    k, ch = w.shape
    out = lax.conv_general_dilated(u, w[:, None, :].astype(u.dtype), window_strides=(1,),
                                   padding=[(k // 2, k // 2)], dimension_numbers=('NWC', 'WIO', 'NWC'),
                                   feature_group_count=ch)
    return out + b


def ssd_scan(xh, dt, a, bm, cm, h0, with_y):
    bsz, L, H, P = xh.shape
    G, N = bm.shape[-2:]
    hpg = H // G
    Q = SSM_CHUNK
    nc = L // Q
    x = xh.astype(F32).reshape(bsz, nc, Q, G, hpg, P)
    dtc = dt.astype(F32).reshape(bsz, nc, Q, G, hpg)
    bc = bm.astype(F32).reshape(bsz, nc, Q, G, N)
    cc = cm.astype(F32).reshape(bsz, nc, Q, G, N)
    acs = jnp.cumsum(dtc * a.astype(F32).reshape(G, hpg), axis=2)
    xdt = x * dtc[..., None]
    decay_end = jnp.exp(acs[:, :, -1:] - acs)
    states = jnp.einsum('bcjgn,bcjghp->bcghpn', bc, xdt * decay_end[..., None])
    chunk_decay = jnp.exp(acs[:, :, -1])

    def step(h, inp):
        s, d = inp
        return h * d[..., None, None] + s, h

    h_t, h_in = lax.scan(step, h0.astype(F32).reshape(bsz, G, hpg, P, N),
                         (jnp.moveaxis(states, 1, 0), jnp.moveaxis(chunk_decay, 1, 0)))
    h_t = h_t.reshape(bsz, H, P, N)
    if not with_y:
        return None, h_t
    h_in = jnp.moveaxis(h_in, 0, 1)
    acs_t = jnp.moveaxis(acs, 2, -1)
    lower = jnp.tril(jnp.ones((Q, Q), dtype=bool))
    seg = jnp.exp(jnp.where(lower, acs_t[..., :, None] - acs_t[..., None, :], -jnp.inf))
    cb = jnp.einsum('bcign,bcjgn->bcgij', cc, bc)
    y_diag = jnp.einsum('bcghij,bcjghp->bcighp', cb[:, :, :, None] * seg, xdt)
    y_off = jnp.einsum('bcign,bcghpn->bcighp', cc, h_in) * jnp.exp(acs)[..., None]
    y = (y_diag + y_off).reshape(bsz, L, H, P)
    return y.astype(xh.dtype), h_t


def ssd_branch(xbc_l, dt_l, z_l, xbc_c, dt_c, z_c, conv_w, conv_b, dt_bias, a_log, d_skip, norm_g):
    a = -jnp.exp(a_log.astype(F32))

    def prep(xbc, dt):
        bsz, n = xbc.shape[:2]
        u = jax.nn.silu(centred_dwconv(xbc, conv_w, conv_b))
        xs, bm, cm = jnp.split(u, [SSM_INNER, SSM_INNER + SSM_GROUPS * SSM_STATE], axis=-1)
        dts = jax.nn.softplus(dt.reshape(bsz, n, 2, SSM_HEADS).astype(F32) + dt_bias.astype(F32))
        return (xs.reshape(bsz, n, SSM_HEADS, SSM_HEAD_DIM), bm.reshape(bsz, n, SSM_GROUPS, SSM_STATE),
                cm.reshape(bsz, n, SSM_GROUPS, SSM_STATE), dts)

    def rev(t):
        return jnp.flip(t, axis=1)

    def bidir(xs, bm, cm, dts, h_f, h_b, with_y):
        y_f, hf = ssd_scan(xs, dts[:, :, 0], a[0], bm, cm, h_f, with_y)
        y_b, hb = ssd_scan(rev(xs), rev(dts[:, :, 1]), a[1], rev(bm), rev(cm), h_b, with_y)
        y = y_f + rev(y_b) + d_skip[:, None] * xs if with_y else None
        return y, hf, hb

    def gated_out(y, z):
        return rms_norm(y.reshape(z.shape) * jax.nn.silu(z), norm_g)

    with_ctx = z_c is not None
    xs_c, b_c, c_c, dts_c = prep(xbc_c, dt_c)
    h0 = jnp.zeros((xs_c.shape[0], SSM_HEADS, SSM_HEAD_DIM, SSM_STATE), F32)
    y_c, hc_f, hc_b = bidir(xs_c, b_c, c_c, dts_c, h0, h0, with_ctx)
    xs_l, b_l, c_l, dts_l = prep(xbc_l, dt_l)
    y_l, _, _ = bidir(xs_l, b_l, c_l, dts_l, hc_f, hc_b, True)
    out_l = gated_out(y_l, z_l)
    out_c = gated_out(y_c, z_c) if with_ctx else None
    return out_l, out_c


def swa_branch(q_l, k_l, v_l, q_c, k_c, v_c, q_g, k_g, sink, rope):
    bsz, L = k_l.shape[:2]
    n_ctx = k_c.shape[1]
    grp = SWA_Q_HEADS // SWA_KV_HEADS
    blk = SWA_BLOCK
    nb = L // blk
    scale = SWA_HEAD_DIM ** -0.5

    def q_heads(q):
        return rms_norm(q.reshape(q.shape[0], q.shape[1], SWA_Q_HEADS, SWA_HEAD_DIM), q_g)

    def kv_heads(k, v):
        shp = (k.shape[0], k.shape[1], SWA_KV_HEADS, SWA_HEAD_DIM)
        return rms_norm(k.reshape(shp), k_g), v.reshape(shp)

    kc, vc = kv_heads(k_c, v_c)
    kl, vl = kv_heads(k_l, v_l)
    kl = apply_axial_rope(kl, *rope)
    ql = apply_axial_rope(q_heads(q_l), *rope)
    sink_f = sink.astype(F32).reshape(SWA_KV_HEADS, grp)

    qb = ql.reshape(bsz, nb, blk, SWA_KV_HEADS, grp, SWA_HEAD_DIM)

    def band(t):
        tp = jnp.pad(t, ((0, 0), (blk, blk), (0, 0), (0, 0))).reshape(bsz, nb + 2, blk, SWA_KV_HEADS, SWA_HEAD_DIM)
        return jnp.concatenate([tp[:, :-2], tp[:, 1:-1], tp[:, 2:]], axis=2)

    kb, vb = band(kl), band(vl)
    s_win = jnp.einsum('bnqhgd,bnkhd->bnhgqk', qb, kb).astype(F32) * scale
    q_idx = jnp.arange(blk)[:, None] + blk
    k_idx = jnp.arange(3 * blk)[None, :]
    k_pos = (jnp.arange(nb)[:, None, None] - 1) * blk + k_idx
    mask = (jnp.abs(k_idx - q_idx) <= SWA_WINDOW)[None] & (k_pos >= 0) & (k_pos < L)
    s_win = jnp.where(mask[None, :, None, None], s_win, -jnp.inf)
    s_ctx = jnp.einsum('bnqhgd,bkhd->bnhgqk', qb, kc).astype(F32) * scale
    s_sink = jnp.broadcast_to(sink_f[None, None, :, :, None, None], s_win.shape[:-1] + (1,))
    p = jax.nn.softmax(jnp.concatenate([s_win, s_ctx, s_sink], axis=-1), axis=-1).astype(vb.dtype)
    o = (jnp.einsum('bnhgqk,bnkhd->bnqhgd', p[..., :3 * blk], vb)
         + jnp.einsum('bnhgqk,bkhd->bnqhgd', p[..., 3 * blk:3 * blk + n_ctx], vc))
    out_l = o.reshape(bsz, L, SWA_Q_HEADS * SWA_HEAD_DIM)
    if q_c is None:
        return out_l, None
    qc = q_heads(q_c).reshape(bsz, n_ctx, SWA_KV_HEADS, grp, SWA_HEAD_DIM)
    s_c = jnp.einsum('bqhgd,bkhd->bhgqk', qc, kc).astype(F32) * scale
    s_cs = jnp.broadcast_to(sink_f[None, :, :, None, None], s_c.shape[:-1] + (1,))
    pc = jax.nn.softmax(jnp.concatenate([s_c, s_cs], axis=-1), axis=-1).astype(vc.dtype)
    out_c = jnp.einsum('bhgqk,bkhd->bqhgd', pc[..., :n_ctx], vc).reshape(bsz, n_ctx, SWA_Q_HEADS * SWA_HEAD_DIM)
    return out_l, out_c


def mla_keys(ckv, kr, kv_lat_g, w_ukv, k_g, rope):
    bsz, n = ckv.shape[:2]
    kv = (rms_norm(ckv, kv_lat_g) @ w_ukv).reshape(bsz, n, MLA_HEADS, MLA_NOPE + MLA_V)
    k_nope = rms_norm(kv[..., :MLA_NOPE], k_g[:MLA_NOPE])
    k_rope = rms_norm(kr[:, :, None, :], k_g[MLA_NOPE:])
    if rope is not None:
        k_rope = apply_axial_rope(k_rope, *rope)
    k = jnp.concatenate([k_nope, jnp.broadcast_to(k_rope, (bsz, n, MLA_HEADS, MLA_ROPE))], axis=-1)
    return k, kv[..., MLA_NOPE:]


def mla_queries(cq, q_lat_g, w_uq, q_g, rope):
    bsz, n = cq.shape[:2]
    q = (rms_norm(cq, q_lat_g) @ w_uq).reshape(bsz, n, MLA_HEADS, MLA_QK)
    q_nope = rms_norm(q[..., :MLA_NOPE], q_g[:MLA_NOPE])
    q_rope = rms_norm(q[..., MLA_NOPE:], q_g[MLA_NOPE:])
    if rope is not None:
        q_rope = apply_axial_rope(q_rope, *rope)
    return jnp.concatenate([q_nope, q_rope], axis=-1)


def mla_branch(cq_l, ckv_l, kr_l, cq_c, ckv_c, kr_c, q_lat_g, kv_lat_g, w_uq, w_ukv, q_g, k_g, rope):
    bsz, L = cq_l.shape[:2]
    n_ctx = ckv_c.shape[1]
    scale = MLA_QK ** -0.5
    k_c, v_c = mla_keys(ckv_c, kr_c, kv_lat_g, w_ukv, k_g, None)
    k_l, v_l = mla_keys(ckv_l, kr_l, kv_lat_g, w_ukv, k_g, rope)
    q_l = mla_queries(cq_l, q_lat_g, w_uq, q_g, rope)
    k_all = jnp.concatenate([k_l, k_c], axis=1)
    v_all = jnp.concatenate([v_l, v_c], axis=1)
    nb = L // MLA_Q_BLOCK
    qb = jnp.moveaxis(q_l.reshape(bsz, nb, MLA_Q_BLOCK, MLA_HEADS, MLA_QK), 1, 0)

    def attend(qblk):
        s = jnp.einsum('bqhd,bkhd->bhqk', qblk, k_all).astype(F32) * scale
        p = jax.nn.softmax(s, axis=-1).astype(v_all.dtype)
        return jnp.einsum('bhqk,bkhd->bqhd', p, v_all)

    out_l = jnp.moveaxis(lax.map(attend, qb), 0, 1).reshape(bsz, L, MLA_HEADS * MLA_V)
    if cq_c is None:
        return out_l, None
    q_c = mla_queries(cq_c, q_lat_g, w_uq, q_g, None)
    s = jnp.einsum('bqhd,bkhd->bhqk', q_c, k_c).astype(F32) * scale
    p = jax.nn.softmax(s, axis=-1).astype(v_c.dtype)
    out_c = jnp.einsum('bhqk,bkhd->bqhd', p, v_c).reshape(bsz, n_ctx, MLA_HEADS * MLA_V)
    return out_l, out_c


def merge_branches(gates, y_ssm, y_swa, y_mla, w_p_ssm, w_p_swa, w_p_mla, w_o):
    g = jax.nn.sigmoid(gates.astype(F32)).astype(gates.dtype)
    g_ssm, g_swa, g_mla = jnp.split(g, N_BRANCH, axis=-1)
    merged = g_ssm * (y_ssm @ w_p_ssm) + g_swa * (y_swa @ w_p_swa) + g_mla * (y_mla @ w_p_mla)
    return merged @ w_o


def swiglu(h, w_in, w_out):
    g, u = jnp.split(h @ w_in, 2, axis=-1)
    return (jax.nn.silu(g) * u) @ w_out


def _fwd_setup_inputs(seed: int = 0) -> dict:
    key = jax.random.key(seed)
    ks = iter(jax.random.split(key, 40))

    def nrm(shape, s):
        return jax.random.normal(next(ks), shape, F32) * s

    def gain(shape):
        return 1.0 + nrm(shape, 0.02)

    x = nrm((BATCH, SEQ, D_MODEL), 1.0)
    c = nrm((BATCH, D_MODEL), 1.0)
    ctx = nrm((BATCH, CTX_LEN, D_MODEL), 1.0)
    c_ctx = nrm((D_MODEL,), 1.0)
    w_mod = nrm((DEPTH, D_MODEL, 6 * D_MODEL), 0.5 * D_MODEL ** -0.5)
    b_mod = nrm((DEPTH, 6 * D_MODEL), 0.01)
    norm1_g = gain((DEPTH, D_MODEL))
    norm2_g = gain((DEPTH, D_MODEL))
    w_in = nrm((DEPTH, D_MODEL, IN_COLS), D_MODEL ** -0.5)
    ssm_conv_w = nrm((DEPTH, SSM_CONV, SSM_CONV_DIM), SSM_CONV ** -0.5)
    ssm_conv_b = nrm((DEPTH, SSM_CONV_DIM), 0.01)
    dt0 = jnp.exp(jax.random.uniform(next(ks), (DEPTH, 2, SSM_HEADS), F32, math.log(1e-3), math.log(1e-1)))
    ssm_dt_bias = dt0 + jnp.log(-jnp.expm1(-dt0))
    ssm_a_log = jnp.log(jax.random.uniform(next(ks), (DEPTH, 2, SSM_HEADS), F32, 1.0, 16.0))
    ssm_d = gain((DEPTH, SSM_HEADS))
    ssm_norm_g = gain((DEPTH, SSM_INNER))
    swa_q_norm_g = gain((DEPTH, SWA_HEAD_DIM))
    swa_k_norm_g = gain((DEPTH, SWA_HEAD_DIM))
    swa_sink = nrm((DEPTH, SWA_Q_HEADS), 0.5)
    mla_q_lat_g = gain((DEPTH, MLA_Q_RANK))
    mla_kv_lat_g = gain((DEPTH, MLA_KV_RANK))
    w_mla_uq = nrm((DEPTH, MLA_Q_RANK, MLA_HEADS * MLA_QK), MLA_Q_RANK ** -0.5)
    w_mla_ukv = nrm((DEPTH, MLA_KV_RANK, MLA_HEADS * (MLA_NOPE + MLA_V)), MLA_KV_RANK ** -0.5)
    mla_q_norm_g = gain((DEPTH, MLA_QK))
    mla_k_norm_g = gain((DEPTH, MLA_QK))
    w_p_ssm = nrm((DEPTH, SSM_INNER, D_MODEL), SSM_INNER ** -0.5)
    w_p_swa = nrm((DEPTH, SWA_Q_HEADS * SWA_HEAD_DIM, D_MODEL), (SWA_Q_HEADS * SWA_HEAD_DIM) ** -0.5)
    w_p_mla = nrm((DEPTH, MLA_HEADS * MLA_V, D_MODEL), (MLA_HEADS * MLA_V) ** -0.5)
    w_out = nrm((DEPTH, D_MODEL, D_MODEL), D_MODEL ** -0.5)
    w_ffn_in = nrm((DEPTH, D_MODEL, 2 * FFN_HIDDEN), D_MODEL ** -0.5)
    w_ffn_out = nrm((DEPTH, FFN_HIDDEN, D_MODEL), FFN_HIDDEN ** -0.5)
    return {'x': x, 'c': c, 'ctx': ctx, 'c_ctx': c_ctx, 'w_mod': w_mod, 'b_mod': b_mod,
            'norm1_g': norm1_g, 'norm2_g': norm2_g, 'w_in': w_in,
            'ssm_conv_w': ssm_conv_w, 'ssm_conv_b': ssm_conv_b, 'ssm_dt_bias': ssm_dt_bias,
            'ssm_a_log': ssm_a_log, 'ssm_d': ssm_d, 'ssm_norm_g': ssm_norm_g,
            'swa_q_norm_g': swa_q_norm_g, 'swa_k_norm_g': swa_k_norm_g, 'swa_sink': swa_sink,
            'mla_q_lat_g': mla_q_lat_g, 'mla_kv_lat_g': mla_kv_lat_g, 'w_mla_uq': w_mla_uq,
            'w_mla_ukv': w_mla_ukv, 'mla_q_norm_g': mla_q_norm_g, 'mla_k_norm_g': mla_k_norm_g,
            'w_p_ssm': w_p_ssm, 'w_p_swa': w_p_swa, 'w_p_mla': w_p_mla, 'w_out': w_out,
            'w_ffn_in': w_ffn_in, 'w_ffn_out': w_ffn_out}


def _fwd_reference(x, c, ctx, c_ctx, w_mod, b_mod, norm1_g, norm2_g, w_in,
              ssm_conv_w, ssm_conv_b, ssm_dt_bias, ssm_a_log, ssm_d, ssm_norm_g,
              swa_q_norm_g, swa_k_norm_g, swa_sink,
              mla_q_lat_g, mla_kv_lat_g, w_mla_uq, w_mla_ukv, mla_q_norm_g, mla_k_norm_g,
              w_p_ssm, w_p_swa, w_p_mla, w_out, w_ffn_in, w_ffn_out):
    n_lat = x.shape[1]
    rows = n_lat // GRID_W
    rope_swa = axial_rope_tables(rows, SWA_HEAD_DIM)
    rope_mla = axial_rope_tables(rows, MLA_ROPE)
    silu_c = jax.nn.silu(c)
    silu_cc = jax.nn.silu(c_ctx)[None]
    x_l, x_c = x, ctx
    for i in range(DEPTH):
        last = i == DEPTH - 1
        sh1, sc1, gt1, sh2, sc2, gt2 = jnp.split(silu_c @ w_mod[i] + b_mod[i], 6, axis=-1)
        csh1, csc1, cgt1, csh2, csc2, cgt2 = jnp.split(silu_cc @ w_mod[i] + b_mod[i], 6, axis=-1)

        h_l = modulate(x_l, norm1_g[i], sh1, sc1)
        h_c = modulate(x_c, norm1_g[i], csh1, csc1)
        (xbc_l, dt_l, k_l, v_l, ckv_l, kr_l, z_l, q_l, cq_l, gates_l) = split_cols(h_l @ w_in[i], IN_WIDTHS)
        if last:
            (xbc_c, dt_c, k_c, v_c, ckv_c, kr_c) = split_cols(h_c @ w_in[i][:, :KV_COLS], IN_WIDTHS[:N_KV_SPLITS])
            z_c = q_c = cq_c = gates_c = None
        else:
            (xbc_c, dt_c, k_c, v_c, ckv_c, kr_c, z_c, q_c, cq_c, gates_c) = split_cols(h_c @ w_in[i], IN_WIDTHS)

        y_ssm_l, y_ssm_c = ssd_branch(xbc_l, dt_l, z_l, xbc_c, dt_c, z_c, ssm_conv_w[i], ssm_conv_b[i],
                                      ssm_dt_bias[i], ssm_a_log[i], ssm_d[i], ssm_norm_g[i])
        y_swa_l, y_swa_c = swa_branch(q_l, k_l, v_l, q_c, k_c, v_c, swa_q_norm_g[i], swa_k_norm_g[i],
                                      swa_sink[i], rope_swa)
        y_mla_l, y_mla_c = mla_branch(cq_l, ckv_l, kr_l, cq_c, ckv_c, kr_c, mla_q_lat_g[i], mla_kv_lat_g[i],
                                      w_mla_uq[i], w_mla_ukv[i], mla_q_norm_g[i], mla_k_norm_g[i], rope_mla)

        x_l = x_l + gt1[:, None] * merge_branches(gates_l, y_ssm_l, y_swa_l, y_mla_l,
                                                  w_p_ssm[i], w_p_swa[i], w_p_mla[i], w_out[i])
        x_l = x_l + gt2[:, None] * swiglu(modulate(x_l, norm2_g[i], sh2, sc2), w_ffn_in[i], w_ffn_out[i])

        if not last:
            x_c = x_c + cgt1[:, None] * merge_branches(gates_c, y_ssm_c, y_swa_c, y_mla_c,
                                                      w_p_ssm[i], w_p_swa[i], w_p_mla[i], w_out[i])
            x_c = x_c + cgt2[:, None] * swiglu(modulate(x_c, norm2_g[i], csh2, csc2), w_ffn_in[i], w_ffn_out[i])
    return x_l


import jax as _jax
import jax.numpy as _jnp

TWIN_FORMAT = 'train_step'
FWD_PARAMS = ['x', 'c', 'ctx', 'c_ctx', 'w_mod', 'b_mod', 'norm1_g', 'norm2_g', 'w_in', 'ssm_conv_w', 'ssm_conv_b', 'ssm_dt_bias', 'ssm_a_log', 'ssm_d', 'ssm_norm_g', 'swa_q_norm_g', 'swa_k_norm_g', 'swa_sink', 'mla_q_lat_g', 'mla_kv_lat_g', 'w_mla_uq', 'w_mla_ukv', 'mla_q_norm_g', 'mla_k_norm_g', 'w_p_ssm', 'w_p_swa', 'w_p_mla', 'w_out', 'w_ffn_in', 'w_ffn_out']
TWIN_WEIGHTS = ['c_ctx', 'w_mod', 'b_mod', 'norm1_g', 'norm2_g', 'w_in', 'ssm_conv_w', 'ssm_conv_b', 'ssm_dt_bias', 'ssm_a_log', 'ssm_d', 'ssm_norm_g', 'swa_q_norm_g', 'swa_k_norm_g', 'swa_sink', 'mla_q_lat_g', 'mla_kv_lat_g', 'w_mla_uq', 'w_mla_ukv', 'mla_q_norm_g', 'mla_k_norm_g', 'w_p_ssm', 'w_p_swa', 'w_p_mla', 'w_out', 'w_ffn_in', 'w_ffn_out']
TWIN_DIFF_INPUT = 'x'
TWIN_INPUTS = ['x', 'c', 'ctx', 'c_ctx', 'w_mod', 'b_mod', 'norm1_g', 'norm2_g', 'w_in', 'ssm_conv_w', 'ssm_conv_b', 'ssm_dt_bias', 'ssm_a_log', 'ssm_d', 'ssm_norm_g', 'swa_q_norm_g', 'swa_k_norm_g', 'swa_sink', 'mla_q_lat_g', 'mla_kv_lat_g', 'w_mla_uq', 'w_mla_ukv', 'mla_q_norm_g', 'mla_k_norm_g', 'w_p_ssm', 'w_p_swa', 'w_p_mla', 'w_out', 'w_ffn_in', 'w_ffn_out', 'loss_target', 'm_c_ctx', 'm_w_mod', 'm_b_mod', 'm_norm1_g', 'm_norm2_g', 'm_w_in', 'm_ssm_conv_w', 'm_ssm_conv_b', 'm_ssm_dt_bias', 'm_ssm_a_log', 'm_ssm_d', 'm_ssm_norm_g', 'm_swa_q_norm_g', 'm_swa_k_norm_g', 'm_swa_sink', 'm_mla_q_lat_g', 'm_mla_kv_lat_g', 'm_w_mla_uq', 'm_w_mla_ukv', 'm_mla_q_norm_g', 'm_mla_k_norm_g', 'm_w_p_ssm', 'm_w_p_swa', 'm_w_p_mla', 'm_w_out', 'm_w_ffn_in', 'm_w_ffn_out', 'v_c_ctx', 'v_w_mod', 'v_b_mod', 'v_norm1_g', 'v_norm2_g', 'v_w_in', 'v_ssm_conv_w', 'v_ssm_conv_b', 'v_ssm_dt_bias', 'v_ssm_a_log', 'v_ssm_d', 'v_ssm_norm_g', 'v_swa_q_norm_g', 'v_swa_k_norm_g', 'v_swa_sink', 'v_mla_q_lat_g', 'v_mla_kv_lat_g', 'v_w_mla_uq', 'v_w_mla_ukv', 'v_mla_q_norm_g', 'v_mla_k_norm_g', 'v_w_p_ssm', 'v_w_p_swa', 'v_w_p_mla', 'v_w_out', 'v_w_ffn_in', 'v_w_ffn_out']
TWIN_OUTPUTS = ['loss', 'grad_x', 'grad_c_ctx', 'grad_w_mod', 'grad_b_mod', 'grad_norm1_g', 'grad_norm2_g', 'grad_w_in', 'grad_ssm_conv_w', 'grad_ssm_conv_b', 'grad_ssm_dt_bias', 'grad_ssm_a_log', 'grad_ssm_d', 'grad_ssm_norm_g', 'grad_swa_q_norm_g', 'grad_swa_k_norm_g', 'grad_swa_sink', 'grad_mla_q_lat_g', 'grad_mla_kv_lat_g', 'grad_w_mla_uq', 'grad_w_mla_ukv', 'grad_mla_q_norm_g', 'grad_mla_k_norm_g', 'grad_w_p_ssm', 'grad_w_p_swa', 'grad_w_p_mla', 'grad_w_out', 'grad_w_ffn_in', 'grad_w_ffn_out', 'delta_c_ctx', 'delta_w_mod', 'delta_b_mod', 'delta_norm1_g', 'delta_norm2_g', 'delta_w_in', 'delta_ssm_conv_w', 'delta_ssm_conv_b', 'delta_ssm_dt_bias', 'delta_ssm_a_log', 'delta_ssm_d', 'delta_ssm_norm_g', 'delta_swa_q_norm_g', 'delta_swa_k_norm_g', 'delta_swa_sink', 'delta_mla_q_lat_g', 'delta_mla_kv_lat_g', 'delta_w_mla_uq', 'delta_w_mla_ukv', 'delta_mla_q_norm_g', 'delta_mla_k_norm_g', 'delta_w_p_ssm', 'delta_w_p_swa', 'delta_w_p_mla', 'delta_w_out', 'delta_w_ffn_in', 'delta_w_ffn_out', 'new_m_c_ctx', 'new_m_w_mod', 'new_m_b_mod', 'new_m_norm1_g', 'new_m_norm2_g', 'new_m_w_in', 'new_m_ssm_conv_w', 'new_m_ssm_conv_b', 'new_m_ssm_dt_bias', 'new_m_ssm_a_log', 'new_m_ssm_d', 'new_m_ssm_norm_g', 'new_m_swa_q_norm_g', 'new_m_swa_k_norm_g', 'new_m_swa_sink', 'new_m_mla_q_lat_g', 'new_m_mla_kv_lat_g', 'new_m_w_mla_uq', 'new_m_w_mla_ukv', 'new_m_mla_q_norm_g', 'new_m_mla_k_norm_g', 'new_m_w_p_ssm', 'new_m_w_p_swa', 'new_m_w_p_mla', 'new_m_w_out', 'new_m_w_ffn_in', 'new_m_w_ffn_out', 'new_v_c_ctx', 'new_v_w_mod', 'new_v_b_mod', 'new_v_norm1_g', 'new_v_norm2_g', 'new_v_w_in', 'new_v_ssm_conv_w', 'new_v_ssm_conv_b', 'new_v_ssm_dt_bias', 'new_v_ssm_a_log', 'new_v_ssm_d', 'new_v_ssm_norm_g', 'new_v_swa_q_norm_g', 'new_v_swa_k_norm_g', 'new_v_swa_sink', 'new_v_mla_q_lat_g', 'new_v_mla_kv_lat_g', 'new_v_w_mla_uq', 'new_v_w_mla_ukv', 'new_v_mla_q_norm_g', 'new_v_mla_k_norm_g', 'new_v_w_p_ssm', 'new_v_w_p_swa', 'new_v_w_p_mla', 'new_v_w_out', 'new_v_w_ffn_in', 'new_v_w_ffn_out']
TWIN_LEAF_KINDS = {'loss': 'loss', 'grad_x': 'grad_x', 'grad_c_ctx': 'grad_w', 'grad_w_mod': 'grad_w', 'grad_b_mod': 'grad_w', 'grad_norm1_g': 'grad_w', 'grad_norm2_g': 'grad_w', 'grad_w_in': 'grad_w', 'grad_ssm_conv_w': 'grad_w', 'grad_ssm_conv_b': 'grad_w', 'grad_ssm_dt_bias': 'grad_w', 'grad_ssm_a_log': 'grad_w', 'grad_ssm_d': 'grad_w', 'grad_ssm_norm_g': 'grad_w', 'grad_swa_q_norm_g': 'grad_w', 'grad_swa_k_norm_g': 'grad_w', 'grad_swa_sink': 'grad_w', 'grad_mla_q_lat_g': 'grad_w', 'grad_mla_kv_lat_g': 'grad_w', 'grad_w_mla_uq': 'grad_w', 'grad_w_mla_ukv': 'grad_w', 'grad_mla_q_norm_g': 'grad_w', 'grad_mla_k_norm_g': 'grad_w', 'grad_w_p_ssm': 'grad_w', 'grad_w_p_swa': 'grad_w', 'grad_w_p_mla': 'grad_w', 'grad_w_out': 'grad_w', 'grad_w_ffn_in': 'grad_w', 'grad_w_ffn_out': 'grad_w', 'delta_c_ctx': 'delta_w', 'delta_w_mod': 'delta_w', 'delta_b_mod': 'delta_w', 'delta_norm1_g': 'delta_w', 'delta_norm2_g': 'delta_w', 'delta_w_in': 'delta_w', 'delta_ssm_conv_w': 'delta_w', 'delta_ssm_conv_b': 'delta_w', 'delta_ssm_dt_bias': 'delta_w', 'delta_ssm_a_log': 'delta_w', 'delta_ssm_d': 'delta_w', 'delta_ssm_norm_g': 'delta_w', 'delta_swa_q_norm_g': 'delta_w', 'delta_swa_k_norm_g': 'delta_w', 'delta_swa_sink': 'delta_w', 'delta_mla_q_lat_g': 'delta_w', 'delta_mla_kv_lat_g': 'delta_w', 'delta_w_mla_uq': 'delta_w', 'delta_w_mla_ukv': 'delta_w', 'delta_mla_q_norm_g': 'delta_w', 'delta_mla_k_norm_g': 'delta_w', 'delta_w_p_ssm': 'delta_w', 'delta_w_p_swa': 'delta_w', 'delta_w_p_mla': 'delta_w', 'delta_w_out': 'delta_w', 'delta_w_ffn_in': 'delta_w', 'delta_w_ffn_out': 'delta_w', 'new_m_c_ctx': 'new_m', 'new_m_w_mod': 'new_m', 'new_m_b_mod': 'new_m', 'new_m_norm1_g': 'new_m', 'new_m_norm2_g': 'new_m', 'new_m_w_in': 'new_m', 'new_m_ssm_conv_w': 'new_m', 'new_m_ssm_conv_b': 'new_m', 'new_m_ssm_dt_bias': 'new_m', 'new_m_ssm_a_log': 'new_m', 'new_m_ssm_d': 'new_m', 'new_m_ssm_norm_g': 'new_m', 'new_m_swa_q_norm_g': 'new_m', 'new_m_swa_k_norm_g': 'new_m', 'new_m_swa_sink': 'new_m', 'new_m_mla_q_lat_g': 'new_m', 'new_m_mla_kv_lat_g': 'new_m', 'new_m_w_mla_uq': 'new_m', 'new_m_w_mla_ukv': 'new_m', 'new_m_mla_q_norm_g': 'new_m', 'new_m_mla_k_norm_g': 'new_m', 'new_m_w_p_ssm': 'new_m', 'new_m_w_p_swa': 'new_m', 'new_m_w_p_mla': 'new_m', 'new_m_w_out': 'new_m', 'new_m_w_ffn_in': 'new_m', 'new_m_w_ffn_out': 'new_m', 'new_v_c_ctx': 'new_v', 'new_v_w_mod': 'new_v', 'new_v_b_mod': 'new_v', 'new_v_norm1_g': 'new_v', 'new_v_norm2_g': 'new_v', 'new_v_w_in': 'new_v', 'new_v_ssm_conv_w': 'new_v', 'new_v_ssm_conv_b': 'new_v', 'new_v_ssm_dt_bias': 'new_v', 'new_v_ssm_a_log': 'new_v', 'new_v_ssm_d': 'new_v', 'new_v_ssm_norm_g': 'new_v', 'new_v_swa_q_norm_g': 'new_v', 'new_v_swa_k_norm_g': 'new_v', 'new_v_swa_sink': 'new_v', 'new_v_mla_q_lat_g': 'new_v', 'new_v_mla_kv_lat_g': 'new_v', 'new_v_w_mla_uq': 'new_v', 'new_v_w_mla_ukv': 'new_v', 'new_v_mla_q_norm_g': 'new_v', 'new_v_mla_k_norm_g': 'new_v', 'new_v_w_p_ssm': 'new_v', 'new_v_w_p_swa': 'new_v', 'new_v_w_p_mla': 'new_v', 'new_v_w_out': 'new_v', 'new_v_w_ffn_in': 'new_v', 'new_v_w_ffn_out': 'new_v'}


def _forward(args):
    return _fwd_reference(*[args[k] for k in FWD_PARAMS])


def _output_shape():
    def fwd():
        inp = _fwd_setup_inputs(0)
        return _fwd_reference(*[inp[k] for k in FWD_PARAMS])
    out = _jax.eval_shape(fwd)
    return out.shape, out.dtype

N_MICROBATCH = 1
ADAM_LR = 0.001
ADAM_B1 = 0.9
ADAM_B2 = 0.999
ADAM_EPS = 1e-08
ADAM_WD = 0.01
ADAM_STEP = 10
PER_EXAMPLE_BATCH_AXIS = {'x': 0, 'c': 0, 'ctx': 0, 'loss_target': 0}
SHARED_INPUTS = []
_WEIGHT_DTYPES = {'c_ctx': _jnp.float32, 'w_mod': _jnp.float32, 'b_mod': _jnp.float32, 'norm1_g': _jnp.float32, 'norm2_g': _jnp.float32, 'w_in': _jnp.float32, 'ssm_conv_w': _jnp.float32, 'ssm_conv_b': _jnp.float32, 'ssm_dt_bias': _jnp.float32, 'ssm_a_log': _jnp.float32, 'ssm_d': _jnp.float32, 'ssm_norm_g': _jnp.float32, 'swa_q_norm_g': _jnp.float32, 'swa_k_norm_g': _jnp.float32, 'swa_sink': _jnp.float32, 'mla_q_lat_g': _jnp.float32, 'mla_kv_lat_g': _jnp.float32, 'w_mla_uq': _jnp.float32, 'w_mla_ukv': _jnp.float32, 'mla_q_norm_g': _jnp.float32, 'mla_k_norm_g': _jnp.float32, 'w_p_ssm': _jnp.float32, 'w_p_swa': _jnp.float32, 'w_p_mla': _jnp.float32, 'w_out': _jnp.float32, 'w_ffn_in': _jnp.float32, 'w_ffn_out': _jnp.float32}
MOMENT_SCALE = {'c_ctx': 2.004470e-01, 'w_mod': 1.270525e+00, 'b_mod': 3.341235e+00, 'norm1_g': 1.341449e-01, 'norm2_g': 6.295871e+00, 'w_in': 8.911489e-02, 'ssm_conv_w': 1.201412e-01, 'ssm_conv_b': 2.947083e-01, 'ssm_dt_bias': 2.744419e-01, 'ssm_a_log': 7.221322e-01, 'ssm_d': 4.075934e-01, 'ssm_norm_g': 2.795180e+00, 'swa_q_norm_g': 7.445151e-02, 'swa_k_norm_g': 7.102320e-02, 'swa_sink': 1.787594e-02, 'mla_q_lat_g': 1.559806e-02, 'mla_kv_lat_g': 6.413609e-01, 'w_mla_uq': 7.151893e-03, 'w_mla_ukv': 1.032262e-01, 'mla_q_norm_g': 2.357422e-02, 'mla_k_norm_g': 2.373455e-02, 'w_p_ssm': 2.114275e-01, 'w_p_swa': 1.539257e-01, 'w_p_mla': 1.440291e-01, 'w_out': 2.336330e-01, 'w_ffn_in': 9.003715e-02, 'w_ffn_out': 1.167769e-01}


def _to_microbatches(a, axis):
    t = _jnp.moveaxis(a, axis, 0)
    t = t.reshape((N_MICROBATCH, t.shape[0] // N_MICROBATCH) + t.shape[1:])
    return _jnp.moveaxis(t, 1, axis + 1)


def setup_inputs(seed: int = 0) -> dict:
    inp = _fwd_setup_inputs(seed)
    key = _jax.random.fold_in(_jax.random.key(seed), 7919)
    shape, _ = _output_shape()
    out = dict(inp)
    out["loss_target"] = _jax.random.normal(_jax.random.fold_in(key, 0), shape, _jnp.float32)
    for i, name in enumerate(TWIN_WEIGHTS):
        w = inp[name].astype(_jnp.float32)
        if MOMENT_SCALE is None:
            s = _jnp.sqrt(_jnp.mean(_jnp.square(w)) + 1e-30)
        else:
            s = MOMENT_SCALE[name]
        km, kv = _jax.random.split(_jax.random.fold_in(key, i + 1))
        out[name] = w
        out["m_" + name] = s * _jax.random.normal(km, w.shape, _jnp.float32)
        out["v_" + name] = (s * s) * _jax.random.uniform(kv, w.shape, _jnp.float32, 0.5, 1.5)
    if N_MICROBATCH > 1:
        for name, axis in PER_EXAMPLE_BATCH_AXIS.items():
            out[name] = _to_microbatches(out[name], axis)
    return {'x': out['x'], 'c': out['c'], 'ctx': out['ctx'], 'c_ctx': out['c_ctx'], 'w_mod': out['w_mod'], 'b_mod': out['b_mod'], 'norm1_g': out['norm1_g'], 'norm2_g': out['norm2_g'], 'w_in': out['w_in'], 'ssm_conv_w': out['ssm_conv_w'], 'ssm_conv_b': out['ssm_conv_b'], 'ssm_dt_bias': out['ssm_dt_bias'], 'ssm_a_log': out['ssm_a_log'], 'ssm_d': out['ssm_d'], 'ssm_norm_g': out['ssm_norm_g'], 'swa_q_norm_g': out['swa_q_norm_g'], 'swa_k_norm_g': out['swa_k_norm_g'], 'swa_sink': out['swa_sink'], 'mla_q_lat_g': out['mla_q_lat_g'], 'mla_kv_lat_g': out['mla_kv_lat_g'], 'w_mla_uq': out['w_mla_uq'], 'w_mla_ukv': out['w_mla_ukv'], 'mla_q_norm_g': out['mla_q_norm_g'], 'mla_k_norm_g': out['mla_k_norm_g'], 'w_p_ssm': out['w_p_ssm'], 'w_p_swa': out['w_p_swa'], 'w_p_mla': out['w_p_mla'], 'w_out': out['w_out'], 'w_ffn_in': out['w_ffn_in'], 'w_ffn_out': out['w_ffn_out'], 'loss_target': out['loss_target'], 'm_c_ctx': out['m_c_ctx'], 'm_w_mod': out['m_w_mod'], 'm_b_mod': out['m_b_mod'], 'm_norm1_g': out['m_norm1_g'], 'm_norm2_g': out['m_norm2_g'], 'm_w_in': out['m_w_in'], 'm_ssm_conv_w': out['m_ssm_conv_w'], 'm_ssm_conv_b': out['m_ssm_conv_b'], 'm_ssm_dt_bias': out['m_ssm_dt_bias'], 'm_ssm_a_log': out['m_ssm_a_log'], 'm_ssm_d': out['m_ssm_d'], 'm_ssm_norm_g': out['m_ssm_norm_g'], 'm_swa_q_norm_g': out['m_swa_q_norm_g'], 'm_swa_k_norm_g': out['m_swa_k_norm_g'], 'm_swa_sink': out['m_swa_sink'], 'm_mla_q_lat_g': out['m_mla_q_lat_g'], 'm_mla_kv_lat_g': out['m_mla_kv_lat_g'], 'm_w_mla_uq': out['m_w_mla_uq'], 'm_w_mla_ukv': out['m_w_mla_ukv'], 'm_mla_q_norm_g': out['m_mla_q_norm_g'], 'm_mla_k_norm_g': out['m_mla_k_norm_g'], 'm_w_p_ssm': out['m_w_p_ssm'], 'm_w_p_swa': out['m_w_p_swa'], 'm_w_p_mla': out['m_w_p_mla'], 'm_w_out': out['m_w_out'], 'm_w_ffn_in': out['m_w_ffn_in'], 'm_w_ffn_out': out['m_w_ffn_out'], 'v_c_ctx': out['v_c_ctx'], 'v_w_mod': out['v_w_mod'], 'v_b_mod': out['v_b_mod'], 'v_norm1_g': out['v_norm1_g'], 'v_norm2_g': out['v_norm2_g'], 'v_w_in': out['v_w_in'], 'v_ssm_conv_w': out['v_ssm_conv_w'], 'v_ssm_conv_b': out['v_ssm_conv_b'], 'v_ssm_dt_bias': out['v_ssm_dt_bias'], 'v_ssm_a_log': out['v_ssm_a_log'], 'v_ssm_d': out['v_ssm_d'], 'v_ssm_norm_g': out['v_ssm_norm_g'], 'v_swa_q_norm_g': out['v_swa_q_norm_g'], 'v_swa_k_norm_g': out['v_swa_k_norm_g'], 'v_swa_sink': out['v_swa_sink'], 'v_mla_q_lat_g': out['v_mla_q_lat_g'], 'v_mla_kv_lat_g': out['v_mla_kv_lat_g'], 'v_w_mla_uq': out['v_w_mla_uq'], 'v_w_mla_ukv': out['v_w_mla_ukv'], 'v_mla_q_norm_g': out['v_mla_q_norm_g'], 'v_mla_k_norm_g': out['v_mla_k_norm_g'], 'v_w_p_ssm': out['v_w_p_ssm'], 'v_w_p_swa': out['v_w_p_swa'], 'v_w_p_mla': out['v_w_p_mla'], 'v_w_out': out['v_w_out'], 'v_w_ffn_in': out['v_w_ffn_in'], 'v_w_ffn_out': out['v_w_ffn_out']}


def _loss(weights, diff, rest, loss_target):
    with _jax.named_scope("forward"):
        args = {**rest, TWIN_DIFF_INPUT: diff, **{k: w.astype(_WEIGHT_DTYPES[k]) for k, w in weights.items()}}
        y = _forward(args)
    with _jax.named_scope("loss_head"):
        err = _jnp.square(y.astype(_jnp.float32) - loss_target)
        return 0.5 * _jnp.sum(_jnp.mean(err, axis=-1)) if err.ndim else 0.5 * err


def _adamw(w, g, m, v):
    m = ADAM_B1 * m + (1.0 - ADAM_B1) * g
    v = ADAM_B2 * v + (1.0 - ADAM_B2) * _jnp.square(g)
    m_hat = m / (1.0 - ADAM_B1 ** ADAM_STEP)
    v_hat = v / (1.0 - ADAM_B2 ** ADAM_STEP)
    delta = -ADAM_LR * (m_hat / (_jnp.sqrt(v_hat) + ADAM_EPS) + ADAM_WD * w)
    return delta, m, v


def reference(x, c, ctx, c_ctx, w_mod, b_mod, norm1_g, norm2_g, w_in, ssm_conv_w, ssm_conv_b, ssm_dt_bias, ssm_a_log, ssm_d, ssm_norm_g, swa_q_norm_g, swa_k_norm_g, swa_sink, mla_q_lat_g, mla_kv_lat_g, w_mla_uq, w_mla_ukv, mla_q_norm_g, mla_k_norm_g, w_p_ssm, w_p_swa, w_p_mla, w_out, w_ffn_in, w_ffn_out, loss_target, m_c_ctx, m_w_mod, m_b_mod, m_norm1_g, m_norm2_g, m_w_in, m_ssm_conv_w, m_ssm_conv_b, m_ssm_dt_bias, m_ssm_a_log, m_ssm_d, m_ssm_norm_g, m_swa_q_norm_g, m_swa_k_norm_g, m_swa_sink, m_mla_q_lat_g, m_mla_kv_lat_g, m_w_mla_uq, m_w_mla_ukv, m_mla_q_norm_g, m_mla_k_norm_g, m_w_p_ssm, m_w_p_swa, m_w_p_mla, m_w_out, m_w_ffn_in, m_w_ffn_out, v_c_ctx, v_w_mod, v_b_mod, v_norm1_g, v_norm2_g, v_w_in, v_ssm_conv_w, v_ssm_conv_b, v_ssm_dt_bias, v_ssm_a_log, v_ssm_d, v_ssm_norm_g, v_swa_q_norm_g, v_swa_k_norm_g, v_swa_sink, v_mla_q_lat_g, v_mla_kv_lat_g, v_w_mla_uq, v_w_mla_ukv, v_mla_q_norm_g, v_mla_k_norm_g, v_w_p_ssm, v_w_p_swa, v_w_p_mla, v_w_out, v_w_ffn_in, v_w_ffn_out):
    given = dict(x=x, c=c, ctx=ctx, c_ctx=c_ctx, w_mod=w_mod, b_mod=b_mod, norm1_g=norm1_g, norm2_g=norm2_g, w_in=w_in, ssm_conv_w=ssm_conv_w, ssm_conv_b=ssm_conv_b, ssm_dt_bias=ssm_dt_bias, ssm_a_log=ssm_a_log, ssm_d=ssm_d, ssm_norm_g=ssm_norm_g, swa_q_norm_g=swa_q_norm_g, swa_k_norm_g=swa_k_norm_g, swa_sink=swa_sink, mla_q_lat_g=mla_q_lat_g, mla_kv_lat_g=mla_kv_lat_g, w_mla_uq=w_mla_uq, w_mla_ukv=w_mla_ukv, mla_q_norm_g=mla_q_norm_g, mla_k_norm_g=mla_k_norm_g, w_p_ssm=w_p_ssm, w_p_swa=w_p_swa, w_p_mla=w_p_mla, w_out=w_out, w_ffn_in=w_ffn_in, w_ffn_out=w_ffn_out, loss_target=loss_target, m_c_ctx=m_c_ctx, m_w_mod=m_w_mod, m_b_mod=m_b_mod, m_norm1_g=m_norm1_g, m_norm2_g=m_norm2_g, m_w_in=m_w_in, m_ssm_conv_w=m_ssm_conv_w, m_ssm_conv_b=m_ssm_conv_b, m_ssm_dt_bias=m_ssm_dt_bias, m_ssm_a_log=m_ssm_a_log, m_ssm_d=m_ssm_d, m_ssm_norm_g=m_ssm_norm_g, m_swa_q_norm_g=m_swa_q_norm_g, m_swa_k_norm_g=m_swa_k_norm_g, m_swa_sink=m_swa_sink, m_mla_q_lat_g=m_mla_q_lat_g, m_mla_kv_lat_g=m_mla_kv_lat_g, m_w_mla_uq=m_w_mla_uq, m_w_mla_ukv=m_w_mla_ukv, m_mla_q_norm_g=m_mla_q_norm_g, m_mla_k_norm_g=m_mla_k_norm_g, m_w_p_ssm=m_w_p_ssm, m_w_p_swa=m_w_p_swa, m_w_p_mla=m_w_p_mla, m_w_out=m_w_out, m_w_ffn_in=m_w_ffn_in, m_w_ffn_out=m_w_ffn_out, v_c_ctx=v_c_ctx, v_w_mod=v_w_mod, v_b_mod=v_b_mod, v_norm1_g=v_norm1_g, v_norm2_g=v_norm2_g, v_w_in=v_w_in, v_ssm_conv_w=v_ssm_conv_w, v_ssm_conv_b=v_ssm_conv_b, v_ssm_dt_bias=v_ssm_dt_bias, v_ssm_a_log=v_ssm_a_log, v_ssm_d=v_ssm_d, v_ssm_norm_g=v_ssm_norm_g, v_swa_q_norm_g=v_swa_q_norm_g, v_swa_k_norm_g=v_swa_k_norm_g, v_swa_sink=v_swa_sink, v_mla_q_lat_g=v_mla_q_lat_g, v_mla_kv_lat_g=v_mla_kv_lat_g, v_w_mla_uq=v_w_mla_uq, v_w_mla_ukv=v_w_mla_ukv, v_mla_q_norm_g=v_mla_q_norm_g, v_mla_k_norm_g=v_mla_k_norm_g, v_w_p_ssm=v_w_p_ssm, v_w_p_swa=v_w_p_swa, v_w_p_mla=v_w_p_mla, v_w_out=v_w_out, v_w_ffn_in=v_w_ffn_in, v_w_ffn_out=v_w_ffn_out)
    weights = {n: given[n] for n in TWIN_WEIGHTS}
    shared = {n: given[n] for n in SHARED_INPUTS}
    per_example = {n: given[n] for n in ['x', 'c', 'ctx']}
    grad_fn = _jax.value_and_grad(_loss, argnums=(0, 1))

    def one_microbatch(ex, loss_target):
        ex = dict(ex)
        diff = ex.pop(TWIN_DIFF_INPUT)
        return grad_fn(weights, diff, {**shared, **ex}, loss_target)

    if N_MICROBATCH == 1:
        loss, (grad_w, grad_x) = one_microbatch(per_example, given["loss_target"])
    else:
        def body(carry, xs):
            loss_sum, grad_sum = carry
            l_k, (gw_k, gx_k) = one_microbatch(xs[0], xs[1])
            with _jax.named_scope("update"):
                return (loss_sum + l_k, _jax.tree.map(_jnp.add, grad_sum, gw_k)), gx_k

        init = (_jnp.zeros((), _jnp.float32), _jax.tree.map(_jnp.zeros_like, weights))
        (loss, grad_w), grad_x = _jax.lax.scan(body, init, (per_example, given["loss_target"]))
    with _jax.named_scope("update"):
        delta_w, new_m, new_v = {}, {}, {}
        for n in TWIN_WEIGHTS:
            delta_w[n], new_m[n], new_v[n] = _adamw(weights[n], grad_w[n], given["m_" + n], given["v_" + n])
    return (loss, grad_x, *[grad_w[n] for n in TWIN_WEIGHTS], *[delta_w[n] for n in TWIN_WEIGHTS],
            *[new_m[n] for n in TWIN_WEIGHTS], *[new_v[n] for n in TWIN_WEIGHTS])
```

```python
import functools
import math

import jax
import jax.numpy as jnp
from jax import lax
from jax.experimental import pallas as pl
from jax.experimental.pallas import tpu as pltpu

F32 = jnp.float32
BF16 = jnp.bfloat16
MESH = pl.DeviceIdType.MESH

D = 1024
NCTX = 256
EPS = 1e-6
ROPE_BASE = 10000.0
GRID_W = 64
DEPTH = 2
Q = 128
N_HEADS_SSM = 16
SWA_HQ, SWA_HKV, SWA_DH, SWA_WIN = 8, 2, 128, 128
MLA_H, MLA_NOPE, MLA_ROPE, MLA_V = 8, 128, 64, 128
MLA_QRANK, MLA_KVRANK = 384, 256
FFN = 2816
RT = 256
VMEM_LIMIT = 56 << 20
NEG = -1e30

C_G1, C_G2, C_G3, C_Z, C_Q, C_XS, C_B, C_C, C_K, C_V, C_CKV, C_MISC, C_PAD, C_CQ = (
    0, 1024, 2048, 3072, 4096, 5120, 6144, 6400, 6656, 6912, 7168, 7424, 7552, 7680)
UW = 8064
DT_LANE = 64

ADAM_LR, ADAM_B1, ADAM_B2, ADAM_EPS, ADAM_WD, ADAM_STEP = 0.001, 0.9, 0.999, 1e-08, 0.01, 10


def _cp(sem):
    return pltpu.CompilerParams(dimension_semantics=sem, vmem_limit_bytes=VMEM_LIMIT)


def _pick(n, cands):
    for c in cands:
        if n % c == 0:
            return c
    return n


_TN = (1536, 1408, 1152, 1024, 896, 768, 512, 384, 256, 128)


def mm(a, b, out_dtype, name):
    m, k = a.shape
    _, n = b.shape
    tm = _pick(m, (768, 512, 256, 128, 8))
    tn = _pick(n, _TN)
    tk = k if k <= 2048 else _pick(k, (1408, 1152, 1024, 896, 768, 512))
    nk = k // tk

    def body(a_ref, b_ref, o_ref, *acc):
        p = jnp.dot(a_ref[...].astype(BF16), b_ref[...].astype(BF16), preferred_element_type=F32)
        if nk == 1:
            o_ref[...] = p.astype(out_dtype)
        else:
            kk = pl.program_id(2)

            @pl.when(kk == 0)
            def _():
                acc[0][...] = p

            @pl.when(kk > 0)
            def _():
                acc[0][...] += p

            @pl.when(kk == nk - 1)
            def _():
                o_ref[...] = acc[0][...].astype(out_dtype)

    return pl.pallas_call(
        body, out_shape=jax.ShapeDtypeStruct((m, n), out_dtype), grid=(m // tm, n // tn, nk),
        in_specs=[pl.BlockSpec((tm, tk), lambda i, j, kk: (i, kk)), pl.BlockSpec((tk, tn), lambda i, j, kk: (kk, j))],
        out_specs=pl.BlockSpec((tm, tn), lambda i, j, kk: (i, j)),
        scratch_shapes=[] if nk == 1 else [pltpu.VMEM((tm, tn), F32)],
        name=name, compiler_params=_cp(("parallel", "parallel", "arbitrary")))(a, b)


def mm_tn(a, b, name):
    t, ka = a.shape
    _, nb = b.shape
    ta = _pick(ka, (1024, 1408, 768, 512, 384, 256, 128))
    tb = _pick(nb, _TN)
    tt = _pick(t, (768, 512, 256, 128, 8))
    nt = t // tt

    def body(a_ref, b_ref, o_ref):
        p = lax.dot_general(a_ref[...].astype(BF16), b_ref[...].astype(BF16), (((0,), (0,)), ((), ())),
                            preferred_element_type=F32)
        s = pl.program_id(2)

        @pl.when(s == 0)
        def _():
            o_ref[...] = p

        @pl.when(s > 0)
        def _():
            o_ref[...] += p

    return pl.pallas_call(
        body, out_shape=jax.ShapeDtypeStruct((ka, nb), F32), grid=(ka // ta, nb // tb, nt),
        in_specs=[pl.BlockSpec((tt, ta), lambda i, j, s: (s, i)), pl.BlockSpec((tt, tb), lambda i, j, s: (s, j))],
        out_specs=pl.BlockSpec((ta, tb), lambda i, j, s: (i, j)),
        name=name, compiler_params=_cp(("parallel", "parallel", "arbitrary")))(a, b)


def _rms(x, g, n=None):
    n = x.shape[-1] if n is None else n
    r = lax.rsqrt(jnp.sum(x * x, axis=-1, keepdims=True) * (1.0 / n) + EPS)
    return x * r * g


def _silu(x):
    return x * jax.nn.sigmoid(x)


def _modulate(x, g, sc, sh):
    return _rms(x, g) * (1.0 + sc) + sh


def _swap(x, s):
    ax = x.ndim - 1
    w = x.shape[ax]
    lane = lax.broadcasted_iota(jnp.int32, x.shape, ax)
    lo = (lane & s) == 0
    return jnp.where(lo, pltpu.roll(x, w - s, ax), pltpu.roll(x, s, ax))


@functools.partial(jax.custom_vjp, nondiff_argnums=(3,))
def _rope(x, cos, sin, s):
    return x * cos + _swap(x, s) * sin


def _rope_fwd(x, cos, sin, s):
    return _rope(x, cos, sin, s), (cos, sin)


def _rope_bwd(s, res, g):
    cos, sin = res
    return g * cos - _swap(g, s) * sin, jnp.zeros_like(cos), jnp.zeros_like(sin)


_rope.defvjp(_rope_fwd, _rope_bwd)


@jax.custom_vjp
def _softplus(x):
    return jnp.maximum(x, 0.0) + jnp.log(1.0 + jnp.exp(-jnp.abs(x)))


def _softplus_fwd(x):
    return _softplus(x), x


def _softplus_bwd(x, g):
    return (g * jax.nn.sigmoid(x),)


_softplus.defvjp(_softplus_fwd, _softplus_bwd)


def _d(a, b, dims):
    return lax.dot_general(a.astype(BF16), b.astype(BF16), (dims, ((), ())), preferred_element_type=F32)


@jax.custom_vjp
def bdot(a, b):
    return _d(a, b, ((1,), (0,)))


bdot.defvjp(lambda a, b: (bdot(a, b), (a, b)),
            lambda r, g: (_d(g, r[1], ((1,), (1,))), _d(r[0], g, ((0,), (0,)))))


@jax.custom_vjp
def bdot_nt(a, b):
    return _d(a, b, ((1,), (1,)))


bdot_nt.defvjp(lambda a, b: (bdot_nt(a, b), (a, b)),
               lambda r, g: (_d(g, r[1], ((1,), (0,))), _d(g, r[0], ((0,), (0,)))))


@jax.custom_vjp
def bdot_tn(a, b):
    return _d(a, b, ((0,), (0,)))


bdot_tn.defvjp(lambda a, b: (bdot_tn(a, b), (a, b)),
               lambda r, g: (_d(r[1], g, ((1,), (1,))), _d(r[0], g, ((1,), (0,)))))


def _tri(rev):
    i = lax.broadcasted_iota(jnp.int32, (Q, Q), 0)
    j = lax.broadcasted_iota(jnp.int32, (Q, Q), 1)
    return (i <= j) if rev else (i >= j)


def _split3(a):
    hi = a.astype(BF16)
    r = a - hi.astype(F32)
    mid = r.astype(BF16)
    lo = (r - mid.astype(F32)).astype(BF16)
    return hi, mid, lo


def _cum_cols_impl(a, rev):
    t = _tri(rev).astype(BF16)
    return sum(jnp.dot(t, p, preferred_element_type=F32) for p in _split3(a))


def _cum_rows_impl(a, rev):
    t = _tri(not rev).astype(BF16)
    return sum(jnp.dot(p, t, preferred_element_type=F32) for p in _split3(a))


@functools.partial(jax.custom_vjp, nondiff_argnums=(1,))
def cum_cols(a, rev):
    return _cum_cols_impl(a, rev)


cum_cols.defvjp(lambda a, rev: (_cum_cols_impl(a, rev), None), lambda rev, _, g: (_cum_cols_impl(g, not rev),))


@functools.partial(jax.custom_vjp, nondiff_argnums=(1,))
def cum_rows(a, rev):
    return _cum_rows_impl(a, rev)


cum_rows.defvjp(lambda a, rev: (_cum_rows_impl(a, rev), None), lambda rev, _, g: (_cum_rows_impl(g, not rev),))


def _rs(w, cb=0):
    return pl.BlockSpec((RT, w), lambda i: (i, cb))


def _ps(shape):
    nd = len(shape)
    return pl.BlockSpec(shape, lambda i: (0,) * nd)


def _gs(w, cb, nlat):
    return pl.BlockSpec((1, 1, w), lambda i: (i // nlat, 0, cb))


def _rowcall(name, body, n, ins, outs, scratch=()):
    return pl.pallas_call(
        body, out_shape=[o[0] for o in outs], grid=(n // RT,), in_specs=[s for _, s in ins],
        out_specs=[s for _, s in outs], scratch_shapes=list(scratch), name=name,
        compiler_params=_cp(("arbitrary",)))(*[a for a, _ in ins])


def _acc(ref, val, first):
    @pl.when(first)
    def _():
        ref[...] = val

    @pl.when(jnp.logical_not(first))
    def _():
        ref[...] += val


def _sd(shape, dt):
    return jax.ShapeDtypeStruct(shape, dt)


def resid_mod_fwd(name, xp, o, mod_gt, gt_i, mod_n, sh_i, sc_i, norm_g, nlat):
    n = xp.shape[0]
    has_res = o is not None

    def body(*refs):
        if has_res:
            xp_ref, o_ref, gt_ref, sh_ref, sc_ref, g_ref, xn_ref, h_ref = refs
            xn = xp_ref[...] + gt_ref[0] * o_ref[...]
            xn_ref[...] = xn
        else:
            xp_ref, sh_ref, sc_ref, g_ref, h_ref = refs
            xn = xp_ref[...]
        h_ref[...] = _modulate(xn, g_ref[...], sc_ref[0], sh_ref[0]).astype(BF16)

    ins = [(xp, _rs(D))]
    if has_res:
        ins += [(o, _rs(D)), (mod_gt, _gs(D, gt_i, nlat))]
    ins += [(mod_n, _gs(D, sh_i, nlat)), (mod_n, _gs(D, sc_i, nlat)), (norm_g, _ps((1, D)))]
    outs = ([(_sd((n, D), F32), _rs(D))] if has_res else []) + [(_sd((n, D), BF16), _rs(D))]
    r = _rowcall(name, body, n, ins, outs)
    return (r[0], r[1]) if has_res else (xp, r[0])


def resid_mod_bwd(name, xn, dxn, dh, o, mod_gt, gt_i, mod_n, sh_i, sc_i, norm_g, nlat):
    n = xn.shape[0]
    has_res = o is not None

    def body(*refs):
        i = pl.program_id(0)
        if has_res:
            (xn_ref, dxn_ref, dh_ref, o_ref, gt_ref, sh_ref, sc_ref, g_ref,
             dx_ref, do_ref, dgt_ref, dsh_ref, dsc_ref, dg_ref) = refs
        else:
            xn_ref, dxn_ref, dh_ref, sh_ref, sc_ref, g_ref, dx_ref, dsh_ref, dsc_ref, dg_ref = refs
        _, vjp = jax.vjp(_modulate, xn_ref[...], g_ref[...], sc_ref[0], sh_ref[0])
        dx, dg, dsc, dsh = vjp(dh_ref[...])
        dx = dx + dxn_ref[...]
        dx_ref[...] = dx
        gfirst = (i == 0) | (i == nlat)
        _acc(dg_ref, dg, i == 0)
        _acc(dsh_ref, dsh[None], gfirst)
        _acc(dsc_ref, dsc[None], gfirst)
        if has_res:
            do_ref[...] = (gt_ref[0] * dx).astype(BF16)
            _acc(dgt_ref, jnp.sum(dx * o_ref[...], axis=0, keepdims=True)[None], gfirst)

    ins = [(xn, _rs(D)), (dxn, _rs(D)), (dh, _rs(D))]
    if has_res:
        ins += [(o, _rs(D)), (mod_gt, _gs(D, gt_i, nlat))]
    ins += [(mod_n, _gs(D, sh_i, nlat)), (mod_n, _gs(D, sc_i, nlat)), (norm_g, _ps((1, D)))]
    gacc = (_sd((2, 1, D), F32), _gs(D, 0, nlat))
    outs = [(_sd((n, D), F32), _rs(D))]
    if has_res:
        outs += [(_sd((n, D), BF16), _rs(D)), gacc]
    outs += [gacc, gacc, (_sd((1, D), F32), _ps((1, D)))]
    r = _rowcall(name, body, n, ins, outs)
    if has_res:
        return r
    return r[0], None, None, r[1], r[2], r[3]


def resid_loss(name, xp, o, mod_gt, gt_i, target, nlat):
    n = xp.shape[0]

    def body(xp_ref, o_ref, gt_ref, t_ref, loss_ref, dx_ref, do_ref, dgt_ref):
        i = pl.program_id(0)
        gt = gt_ref[0]

        @pl.when(i < nlat)
        def _():
            err = xp_ref[...] + gt * o_ref[...] - t_ref[...]
            dx = err * (1.0 / D)
            dx_ref[...] = dx
            do_ref[...] = (gt * dx).astype(BF16)
            _acc(loss_ref, jnp.full((1, 128), 0.5 / D, F32) * jnp.sum(err * err), i == 0)
            _acc(dgt_ref, jnp.sum(dx * o_ref[...], axis=0, keepdims=True)[None], i == 0)

        @pl.when(i >= nlat)
        def _():
            dx_ref[...] = jnp.zeros((RT, D), F32)
            do_ref[...] = jnp.zeros((RT, D), BF16)
            dgt_ref[...] = jnp.zeros((1, 1, D), F32)

    tgt_spec = pl.BlockSpec((RT, D), lambda i: (jnp.minimum(i, nlat - 1), 0))
    ins = [(xp, _rs(D)), (o, _rs(D)), (mod_gt, _gs(D, gt_i, nlat)), (target, tgt_spec)]
    outs = [(_sd((1, 128), F32), _ps((1, 128))), (_sd((n, D), F32), _rs(D)), (_sd((n, D), BF16), _rs(D)),
            (_sd((2, 1, D), F32), _gs(D, 0, nlat))]
    return _rowcall(name, body, n, ins, outs)


def mod_fwd(name, c8, w_mod, b_mod):
    tn = 1536

    def body(c_ref, w_ref, b_ref, o_ref, s_ref):
        s = _silu(c_ref[...]).astype(BF16)
        s_ref[...] = s
        o_ref[...] = jnp.dot(s, w_ref[...], preferred_element_type=F32) + b_ref[...]

    return pl.pallas_call(
        body, out_shape=[_sd((8, 6 * D), F32), _sd((8, D), BF16)], grid=(6 * D // tn,),
        in_specs=[pl.BlockSpec((8, D), lambda j: (0, 0)), pl.BlockSpec((D, tn), lambda j: (0, j)),
                  pl.BlockSpec((1, tn), lambda j: (0, j))],
        out_specs=[pl.BlockSpec((8, tn), lambda j: (0, j)), pl.BlockSpec((8, D), lambda j: (0, 0))],
        name=name, compiler_params=_cp(("arbitrary",)))(c8, w_mod, b_mod)


def mod_small_bwd(name, c8, dsilu, dmod8):
    def body(c_ref, ds_ref, dm_ref, dc_ref, db_ref):
        _, vjp = jax.vjp(_silu, c_ref[...])
        dc_ref[...] = vjp(ds_ref[...])[0]
        db_ref[...] = jnp.sum(dm_ref[...], axis=0, keepdims=True)

    return pl.pallas_call(
        body, out_shape=[_sd((8, D), F32), _sd((1, 6 * D), F32)], grid=(1,),
        in_specs=[pl.BlockSpec((8, D), lambda j: (0, 0)), pl.BlockSpec((8, D), lambda j: (0, 0)),
                  pl.BlockSpec((8, 6 * D), lambda j: (0, 0))],
        out_specs=[pl.BlockSpec((8, D), lambda j: (0, 0)), pl.BlockSpec((1, 6 * D), lambda j: (0, 0))],
        name=name, compiler_params=_cp(("arbitrary",)))(c8, dsilu, dmod8)


def _conv_taps(x, nlat):
    n = x.shape[0]
    r = lax.broadcasted_iota(jnp.int32, x.shape, 0)
    lo = jnp.where(r < nlat, 0, nlat)
    hi = jnp.where(r < nlat, nlat, n)
    taps = []
    for o in (-2, -1, 0, 1, 2):
        xs = x if o == 0 else pltpu.roll(x, (-o) % n, 0)
        t = r + o
        taps.append(jnp.where((t >= lo) & (t < hi), xs, 0.0))
    return taps


def conv_fwd(name, u, w, b, nlat_rows):
    n = u.shape[0]

    def body(x_ref, w_ref, b_ref, o_ref):
        taps = _conv_taps(x_ref[...], nlat_rows)
        wv = w_ref[...]
        pre = b_ref[...] + sum(taps[k] * wv[k:k + 1, :] for k in range(5))
        o_ref[...] = _silu(pre)

    return pl.pallas_call(
        body, out_shape=_sd((n, 1536), F32), grid=(12,),
        in_specs=[pl.BlockSpec((n, 128), lambda j: (0, C_XS // 128 + j)), pl.BlockSpec((5, 128), lambda j: (0, j)),
                  pl.BlockSpec((1, 128), lambda j: (0, j))],
        out_specs=pl.BlockSpec((n, 128), lambda j: (0, j)),
        name=name, compiler_params=_cp(("parallel",)))(u, w, b)


def conv_bwd(name, u, dact, w, b, nlat_rows):
    n = u.shape[0]

    def body(x_ref, da_ref, w_ref, b_ref, dx_ref, dw_ref, db_ref):
        taps = _conv_taps(x_ref[...], nlat_rows)
        wv = w_ref[...]
        pre = b_ref[...] + sum(taps[k] * wv[k:k + 1, :] for k in range(5))
        s = jax.nn.sigmoid(pre)
        dpre = da_ref[...] * (s * (1.0 + pre * (1.0 - s)))
        db_ref[...] = jnp.sum(dpre, axis=0, keepdims=True)
        rows = lax.broadcasted_iota(jnp.int32, (5, 128), 0)
        dw = jnp.zeros((5, 128), F32)
        for k in range(5):
            dw = dw + jnp.where(rows == k, jnp.sum(dpre * taps[k], axis=0, keepdims=True), 0.0)
        dw_ref[...] = dw
        r = lax.broadcasted_iota(jnp.int32, dpre.shape, 0)
        lo = jnp.where(r < nlat_rows, 0, nlat_rows)
        hi = jnp.where(r < nlat_rows, nlat_rows, n)
        dx = jnp.zeros_like(dpre)
        for k in range(5):
            o = k - 2
            ds = dpre if o == 0 else pltpu.roll(dpre, o % n, 0)
            t = r - o
            dx = dx + jnp.where((t >= lo) & (t < hi), ds, 0.0) * wv[k:k + 1, :]
        dx_ref[...] = dx.astype(BF16)

    return pl.pallas_call(
        body, out_shape=[_sd((n, 1536), BF16), _sd((5, 1536), F32), _sd((1, 1536), F32)], grid=(12,),
        in_specs=[pl.BlockSpec((n, 128), lambda j: (0, C_XS // 128 + j)), pl.BlockSpec((n, 128), lambda j: (0, j)),
                  pl.BlockSpec((5, 128), lambda j: (0, j)), pl.BlockSpec((1, 128), lambda j: (0, j))],
        out_specs=[pl.BlockSpec((n, 128), lambda j: (0, j)), pl.BlockSpec((5, 128), lambda j: (0, j)),
                   pl.BlockSpec((1, 128), lambda j: (0, j))],
        name=name, compiler_params=_cp(("parallel",)))(u, dact, w, b)


def _ssd_chunk(rev, dirn, g, x4, bm, cm, misc, dtrow, bias_c, alog_c, bias_r, alog_r, h4):
    dt_c = _softplus(misc + bias_c)
    a_c = dt_c * (-jnp.exp(alog_c))
    dt_r = _softplus(dtrow + bias_r)
    a_r = dt_r * (-jnp.exp(alog_r))
    cs_c = cum_cols(a_c, rev)
    cs_r = cum_rows(a_r, rev)
    tot_c = jnp.sum(a_c, axis=0, keepdims=True)
    cb = bdot_nt(cm, bm)
    tri = _tri(rev)
    lane = lax.broadcasted_iota(jnp.int32, (1, 128), 1)
    row16 = lax.broadcasted_iota(jnp.int32, (16, 1), 0)
    prow = lax.broadcasted_iota(jnp.int32, (128, 1), 0)
    ys, hs = [], []
    for p in range(4):
        ydiag = 0.0
        wst = 0.0
        eoff = 0.0
        hscale = 0.0
        for e in range(2):
            hg = 8 * g + 2 * p + e
            oh_c = (lane == DT_LANE + 16 * dirn + hg).astype(F32)
            dt_h = jnp.sum(dt_c * oh_c, axis=1, keepdims=True)
            cs_h = jnp.sum(cs_c * oh_c, axis=1, keepdims=True)
            tot_h = jnp.sum(tot_c * oh_c, axis=1, keepdims=True)
            csr_h = jnp.sum(cs_r * (row16 == hg).astype(F32), axis=0, keepdims=True)
            seg = jnp.exp(jnp.where(tri, cs_h - csr_h, -jnp.inf))
            hm = ((lane < 64) if e == 0 else (lane >= 64)).astype(F32)
            ydiag = ydiag + bdot(cb * seg, x4[p] * (dt_h * hm))
            wst = wst + (dt_h * jnp.exp(tot_h - cs_h)) * hm
            eoff = eoff + jnp.exp(cs_h) * hm
            hscale = hscale + jnp.exp(tot_h) * ((prow < 64) if e == 0 else (prow >= 64)).astype(F32)
        ys.append(ydiag + bdot_nt(cm, h4[p]) * eoff)
        hs.append(h4[p] * hscale + bdot_tn(x4[p] * wst, bm))
    return ys, hs


def _ssd_specs(nlat_chunks, rev, dirn, bwd):
    nc = nlat_chunks + 2

    def chunk(s):
        if bwd:
            s = nc - 1 - s
        return (nlat_chunks + 1 - s) if rev else (s + nlat_chunks) % nc

    def step(s):
        return (nc - 1 - s) if bwd else s

    return dict(
        x=pl.BlockSpec((Q, 512), lambda g, s: (chunk(s), g)),
        b=pl.BlockSpec((Q, 128), lambda g, s: (chunk(s), 8 + g)),
        c=pl.BlockSpec((Q, 128), lambda g, s: (chunk(s), 10 + g)),
        misc=pl.BlockSpec((Q, 128), lambda g, s: (chunk(s), C_MISC // 128)),
        dtrow=pl.BlockSpec((16, Q), lambda g, s: (dirn, chunk(s))),
        p_c=pl.BlockSpec((1, 128), lambda g, s: (0, 0)),
        p_r=pl.BlockSpec((16, 1), lambda g, s: (dirn, 0)),
        y=pl.BlockSpec((Q, 512), lambda g, s: (chunk(s), g)),
        hsave=pl.BlockSpec((1, 1, 512, 128), lambda g, s: (g, step(s), 0, 0)),
        bc_out=pl.BlockSpec((Q, 128), lambda g, s: (chunk(s), g)),
        misc_out=pl.BlockSpec((1, Q, 128), lambda g, s: (g, chunk(s), 0)),
        dtrow_out=pl.BlockSpec((1, 16, Q), lambda g, s: (g, 0, chunk(s))),
        pacc_c=pl.BlockSpec((1, 128), lambda g, s: (0, 0)),
        pacc_r=pl.BlockSpec((16, 1), lambda g, s: (0, 0)),
    )


def ssd_fwd(name, xbc, u, dtrow, bias_c, alog_c, bias_r, alog_r, nlat_chunks, rev, dirn):
    n = xbc.shape[0]
    nc = nlat_chunks + 2
    sp = _ssd_specs(nlat_chunks, rev, dirn, False)

    def body(x_ref, b_ref, c_ref, m_ref, r_ref, bc_ref, ac_ref, br_ref, ar_ref, y_ref, hs_ref, h_s):
        g = pl.program_id(0)
        s = pl.program_id(1)

        @pl.when(s == 0)
        def _():
            h_s[...] = jnp.zeros((512, 128), F32)

        hs_ref[0, 0] = h_s[...]
        x4 = [x_ref[:, 128 * p:128 * p + 128] for p in range(4)]
        h4 = [h_s[128 * p:128 * p + 128, :] for p in range(4)]
        ys, hs = _ssd_chunk(rev, dirn, g, x4, b_ref[...], c_ref[...], m_ref[...], r_ref[...],
                            bc_ref[...], ac_ref[...], br_ref[...], ar_ref[...], h4)
        for p in range(4):
            y_ref[:, 128 * p:128 * p + 128] = ys[p]
            h_s[128 * p:128 * p + 128, :] = hs[p]

    return pl.pallas_call(
        body, out_shape=[_sd((n, 1024), F32), _sd((2, nc, 512, 128), F32)], grid=(2, nc),
        in_specs=[sp["x"], sp["b"], sp["c"], sp["misc"], sp["dtrow"], sp["p_c"], sp["p_c"], sp["p_r"], sp["p_r"]],
        out_specs=[sp["y"], sp["hsave"]], scratch_shapes=[pltpu.VMEM((512, 128), F32)],
        name=name, compiler_params=_cp(("arbitrary", "arbitrary")))(
            xbc, xbc, xbc, u, dtrow, bias_c, alog_c, bias_r, alog_r)


def ssd_bwd(name, xbc, u, dtrow, bias_c, alog_c, bias_r, alog_r, hsave, dy, acc, nlat_chunks, rev, dirn):
    n = xbc.shape[0]
    sp = _ssd_specs(nlat_chunks, rev, dirn, True)

    def body(x_ref, b_ref, c_ref, m_ref, r_ref, bc_ref, ac_ref, br_ref, ar_ref, hs_ref, dy_ref, ax_ref, ab_ref, acc_ref,
             dx_ref, db_ref, dc_ref, dm_ref, dr_ref, dbc_ref, dac_ref, dbr_ref, dar_ref, dh_s):
        g = pl.program_id(0)
        s = pl.program_id(1)

        @pl.when(s == 0)
        def _():
            dh_s[...] = jnp.zeros((512, 128), F32)

        x4 = [x_ref[:, 128 * p:128 * p + 128] for p in range(4)]
        h4 = [hs_ref[0, 0, 128 * p:128 * p + 128, :] for p in range(4)]
        fn = functools.partial(_ssd_chunk, rev, dirn, g)
        _, vjp = jax.vjp(fn, x4, b_ref[...], c_ref[...], m_ref[...], r_ref[...],
                         bc_ref[...], ac_ref[...], br_ref[...], ar_ref[...], h4)
        dys = [dy_ref[:, 128 * p:128 * p + 128] for p in range(4)]
        dhs = [dh_s[128 * p:128 * p + 128, :] for p in range(4)]
        dx4, db, dc, dm, dr, dbc, dac, dbr, dar, dh4 = vjp((dys, dhs))
        for p in range(4):
            dx_ref[:, 128 * p:128 * p + 128] = dx4[p] + ax_ref[:, 128 * p:128 * p + 128]
            dh_s[128 * p:128 * p + 128, :] = dh4[p]
        db_ref[...] = db + ab_ref[...]
        dc_ref[...] = dc + acc_ref[...]
        dm_ref[0] = dm
        dr_ref[0] = dr
        first = (g == 0) & (s == 0)
        _acc(dbc_ref, dbc, first)
        _acc(dac_ref, dac, first)
        _acc(dbr_ref, dbr, first)
        _acc(dar_ref, dar, first)

    ax, ab, ac = acc
    return pl.pallas_call(
        body,
        out_shape=[_sd((n, 1024), F32), _sd((n, 256), F32), _sd((n, 256), F32), _sd((2, n, 128), F32),
                   _sd((2, 16, n), F32), _sd((1, 128), F32), _sd((1, 128), F32), _sd((16, 1), F32), _sd((16, 1), F32)],
        grid=(2, nlat_chunks + 2),
        in_specs=[sp["x"], sp["b"], sp["c"], sp["misc"], sp["dtrow"], sp["p_c"], sp["p_c"], sp["p_r"], sp["p_r"],
                  sp["hsave"], sp["y"], sp["y"], sp["bc_out"], sp["bc_out"]],
        out_specs=[sp["y"], sp["bc_out"], sp["bc_out"], sp["misc_out"], sp["dtrow_out"],
                   sp["pacc_c"], sp["pacc_c"], sp["pacc_r"], sp["pacc_r"]],
        scratch_shapes=[pltpu.VMEM((512, 128), F32)],
        name=name, compiler_params=_cp(("arbitrary", "arbitrary")))(
            xbc, xbc, xbc, u, dtrow, bias_c, alog_c, bias_r, alog_r, hsave, dy, ax, ab, ac)


def _ssd_out(yf, yb, xs, z, g, dexp):
    return _rms((yf + yb + dexp * xs) * _silu(z), g)


def ssd_out_fwd(name, yf, yb, xbc, u, g, dexp):
    n = yf.shape[0]

    def body(yf_ref, yb_ref, xs_ref, z_ref, g_ref, d_ref, o_ref):
        o_ref[...] = _ssd_out(yf_ref[...], yb_ref[...], xs_ref[...], z_ref[...], g_ref[...], d_ref[...]).astype(BF16)

    return _rowcall(name, body, n,
                    [(yf, _rs(D)), (yb, _rs(D)), (xbc, _rs(D, 0)), (u, _rs(D, C_Z // D)), (g, _ps((1, D))), (dexp, _ps((1, D)))],
                    [(_sd((n, D), BF16), _rs(D))])[0]


def ssd_out_bwd(name, yf, yb, xbc, u, g, dexp, dys):
    n = yf.shape[0]

    def body(yf_ref, yb_ref, xs_ref, z_ref, g_ref, d_ref, dys_ref, dy_ref, dxs_ref, dz_ref, dg_ref, dd_ref):
        i = pl.program_id(0)
        _, vjp = jax.vjp(_ssd_out, yf_ref[...], yb_ref[...], xs_ref[...], z_ref[...], g_ref[...], d_ref[...])
        dyf, _, dxs, dz, dg, dd = vjp(dys_ref[...])
        dy_ref[...] = dyf
        dxs_ref[...] = dxs
        dz_ref[...] = dz.astype(BF16)
        _acc(dg_ref, dg, i == 0)
        _acc(dd_ref, dd, i == 0)

    return _rowcall(name, body, n,
                    [(yf, _rs(D)), (yb, _rs(D)), (xbc, _rs(D, 0)), (u, _rs(D, C_Z // D)), (g, _ps((1, D))), (dexp, _ps((1, D))),
                     (dys, _rs(D))],
                    [(_sd((n, D), F32), _rs(D)), (_sd((n, D), F32), _rs(D)), (_sd((n, D), BF16), _rs(D)),
                     (_sd((1, D), F32), _ps((1, D))), (_sd((1, D), F32), _ps((1, D)))])


def _normrope(x, g, cos, sin, s, n=None):
    return _rope(_rms(x, g, n), cos, sin, s)


def swa_prep_fwd(name, u, gq, gk, cos, sin):
    n = u.shape[0]

    def body(q_ref, k_ref, gq_ref, gk_ref, cos_ref, sin_ref, qs_ref, ks_ref):
        cs, sn = cos_ref[...], sin_ref[...]
        for h in range(SWA_HQ):
            sl = slice(128 * h, 128 * h + 128)
            qs_ref[:, sl] = _normrope(q_ref[:, sl], gq_ref[...], cs, sn, 32).astype(BF16)
        for h in range(SWA_HKV):
            sl = slice(128 * h, 128 * h + 128)
            ks_ref[:, sl] = _normrope(k_ref[:, sl], gk_ref[...], cs, sn, 32).astype(BF16)

    return _rowcall(name, body, n,
                    [(u, _rs(1024, C_Q // 1024)), (u, _rs(256, C_K // 256)), (gq, _ps((1, 128))), (gk, _ps((1, 128))),
                     (cos, _rs(128)), (sin, _rs(128))],
                    [(_sd((n, 1024), BF16), _rs(1024)), (_sd((n, 256), BF16), _rs(256))])


def swa_prep_bwd(name, u, gq, gk, cos, sin, dqs, dks, dv):
    n = u.shape[0]

    def body(q_ref, k_ref, gq_ref, gk_ref, cos_ref, sin_ref, dqs_ref, dks_ref, dv_ref,
             dq_ref, dk_ref, dvo_ref, dgq_ref, dgk_ref):
        i = pl.program_id(0)
        cs, sn = cos_ref[...], sin_ref[...]
        fn = lambda x, g: _normrope(x, g, cs, sn, 32)
        dgq = jnp.zeros((1, 128), F32)
        dgk = jnp.zeros((1, 128), F32)
        for h in range(SWA_HQ):
            sl = slice(128 * h, 128 * h + 128)
            _, vjp = jax.vjp(fn, q_ref[:, sl], gq_ref[...])
            dx, dg = vjp(dqs_ref[:, sl])
            dq_ref[:, sl] = dx.astype(BF16)
            dgq = dgq + dg
        for h in range(SWA_HKV):
            sl = slice(128 * h, 128 * h + 128)
            _, vjp = jax.vjp(fn, k_ref[:, sl], gk_ref[...])
            dx, dg = vjp(dks_ref[:, sl])
            dk_ref[:, sl] = dx.astype(BF16)
            dgk = dgk + dg
        dvo_ref[...] = dv_ref[...].astype(BF16)
        _acc(dgq_ref, dgq, i == 0)
        _acc(dgk_ref, dgk, i == 0)

    return _rowcall(name, body, n,
                    [(u, _rs(1024, C_Q // 1024)), (u, _rs(256, C_K // 256)), (gq, _ps((1, 128))), (gk, _ps((1, 128))),
                     (cos, _rs(128)), (sin, _rs(128)), (dqs, _rs(1024)), (dks, _rs(256)), (dv, _rs(256))],
                    [(_sd((n, 1024), BF16), _rs(1024)), (_sd((n, 256), BF16), _rs(256)), (_sd((n, 256), BF16), _rs(256)),
                     (_sd((1, 128), F32), _ps((1, 128))), (_sd((1, 128), F32), _ps((1, 128)))])


def lat_norm_fwd(name, u, g_kv, g_q):
    n = u.shape[0]

    def body(ckv_ref, cq_ref, gkv_ref, gq_ref, okv_ref, oq_ref):
        okv_ref[...] = _rms(ckv_ref[...], gkv_ref[...]).astype(BF16)
        oq_ref[...] = _rms(cq_ref[...], gq_ref[...]).astype(BF16)

    return _rowcall(name, body, n,
                    [(u, _rs(256, C_CKV // 256)), (u, _rs(384, C_CQ // 384)), (g_kv, _ps((1, 256))), (g_q, _ps((1, 384)))],
                    [(_sd((n, 256), BF16), _rs(256)), (_sd((n, 384), BF16), _rs(384))])


def lat_norm_bwd(name, u, g_kv, g_q, dkvn, dqn):
    n = u.shape[0]

    def body(ckv_ref, cq_ref, gkv_ref, gq_ref, dkvn_ref, dqn_ref, dckv_ref, dcq_ref, dgkv_ref, dgq_ref):
        i = pl.program_id(0)
        _, vjp = jax.vjp(_rms, ckv_ref[...], gkv_ref[...])
        dx, dg = vjp(dkvn_ref[...])
        dckv_ref[...] = dx.astype(BF16)
        _acc(dgkv_ref, dg, i == 0)
        _, vjp = jax.vjp(_rms, cq_ref[...], gq_ref[...])
        dx, dg = vjp(dqn_ref[...])
        dcq_ref[...] = dx.astype(BF16)
        _acc(dgq_ref, dg, i == 0)

    return _rowcall(name, body, n,
                    [(u, _rs(256, C_CKV // 256)), (u, _rs(384, C_CQ // 384)), (g_kv, _ps((1, 256))), (g_q, _ps((1, 384))),
                     (dkvn, _rs(256)), (dqn, _rs(384))],
                    [(_sd((n, 256), BF16), _rs(256)), (_sd((n, 384), BF16), _rs(384)),
                     (_sd((1, 256), F32), _ps((1, 256))), (_sd((1, 384), F32), _ps((1, 384)))])


def _lane_lt64(x):
    return (lax.broadcasted_iota(jnp.int32, (1, 128), 1) < 64).astype(F32) * x


def _mla_krope(misc, g, cos, sin):
    return _normrope(_lane_lt64(misc), g, cos, sin, 16, MLA_ROPE)


def mla_prep_fwd(name, kv, qp, u, qg, kg, cos, sin):
    n = kv.shape[0]

    def body(kv_ref, q_ref, m_ref, qg_ref, kg_ref, cos_ref, sin_ref, km_ref, qm_ref):
        cs, sn = cos_ref[...], sin_ref[...]
        kr = _mla_krope(m_ref[...], kg_ref[:, 128:256], cs, sn).astype(BF16)
        for h in range(MLA_H):
            km_ref[:, 256 * h:256 * h + 128] = _rms(kv_ref[:, 128 * h:128 * h + 128], kg_ref[:, 0:128]).astype(BF16)
            km_ref[:, 256 * h + 128:256 * h + 256] = kr
            qm_ref[:, 256 * h:256 * h + 128] = _rms(q_ref[:, 256 * h:256 * h + 128], qg_ref[:, 0:128]).astype(BF16)
            qm_ref[:, 256 * h + 128:256 * h + 256] = _normrope(
                q_ref[:, 256 * h + 128:256 * h + 256], qg_ref[:, 128:256], cs, sn, 16, MLA_ROPE).astype(BF16)

    return _rowcall(name, body, n,
                    [(kv, _rs(1024, 0)), (qp, _rs(2048)), (u, _rs(128, C_MISC // 128)), (qg, _ps((1, 256))), (kg, _ps((1, 256))),
                     (cos, _rs(128)), (sin, _rs(128))],
                    [(_sd((n, 2048), BF16), _rs(2048)), (_sd((n, 2048), BF16), _rs(2048))])


def mla_prep_bwd(name, kv, qp, u, qg, kg, cos, sin, dkm, dqm, dv):
    n = kv.shape[0]

    def body(kv_ref, q_ref, m_ref, qg_ref, kg_ref, cos_ref, sin_ref, dkm_ref, dqm_ref, dv_ref,
             dkv_ref, dq_ref, dkr_ref, dqg_ref, dkg_ref):
        i = pl.program_id(0)
        cs, sn = cos_ref[...], sin_ref[...]
        fr = lambda x, g: _normrope(x, g, cs, sn, 16, MLA_ROPE)
        dkg_n = jnp.zeros((1, 128), F32)
        dqg_n = jnp.zeros((1, 128), F32)
        dqg_r = jnp.zeros((1, 128), F32)
        dkr_sum = jnp.zeros((RT, 128), F32)
        for h in range(MLA_H):
            _, vjp = jax.vjp(_rms, kv_ref[:, 128 * h:128 * h + 128], kg_ref[:, 0:128])
            dx, dg = vjp(dkm_ref[:, 256 * h:256 * h + 128])
            dkv_ref[:, 128 * h:128 * h + 128] = dx.astype(BF16)
            dkg_n = dkg_n + dg
            dkr_sum = dkr_sum + dkm_ref[:, 256 * h + 128:256 * h + 256]
            _, vjp = jax.vjp(_rms, q_ref[:, 256 * h:256 * h + 128], qg_ref[:, 0:128])
            dx, dg = vjp(dqm_ref[:, 256 * h:256 * h + 128])
            dq_ref[:, 256 * h:256 * h + 128] = dx.astype(BF16)
            dqg_n = dqg_n + dg
            _, vjp = jax.vjp(fr, q_ref[:, 256 * h + 128:256 * h + 256], qg_ref[:, 128:256])
            dx, dg = vjp(dqm_ref[:, 256 * h + 128:256 * h + 256])
            dq_ref[:, 256 * h + 128:256 * h + 256] = dx.astype(BF16)
            dqg_r = dqg_r + dg
        _, vjp = jax.vjp(lambda m, g: _mla_krope(m, g, cs, sn), m_ref[...], kg_ref[:, 128:256])
        dm, dkg_r = vjp(dkr_sum)
        dkr_ref[...] = dm
        dkv_ref[:, 1024:2048] = dv_ref[...].astype(BF16)
        _acc(dqg_ref.at[:, 0:128], dqg_n, i == 0)
        _acc(dqg_ref.at[:, 128:256], dqg_r, i == 0)
        _acc(dkg_ref.at[:, 0:128], dkg_n, i == 0)
        _acc(dkg_ref.at[:, 128:256], dkg_r, i == 0)

    return _rowcall(name, body, n,
                    [(kv, _rs(1024, 0)), (qp, _rs(2048)), (u, _rs(128, C_MISC // 128)), (qg, _ps((1, 256))), (kg, _ps((1, 256))),
                     (cos, _rs(128)), (sin, _rs(128)), (dkm, _rs(2048)), (dqm, _rs(2048)), (dv, _rs(1024))],
                    [(_sd((n, 2048), BF16), _rs(2048)), (_sd((n, 2048), BF16), _rs(2048)), (_sd((n, 128), F32), _rs(128)),
                     (_sd((1, 256), F32), _ps((1, 256))), (_sd((1, 256), F32), _ps((1, 256)))])


def misc_combine(name, dkr, dm_f, dm_b, drow_t):
    n = dkr.shape[0]

    def body(a_ref, f_ref, b_ref, r_ref, o_ref):
        o_ref[...] = (a_ref[...] + f_ref[0] + f_ref[1] + b_ref[0] + b_ref[1] + r_ref[...]).astype(BF16)

    g2 = pl.BlockSpec((2, RT, 128), lambda i: (0, i, 0))
    return _rowcall(name, body, n, [(dkr, _rs(128)), (dm_f, g2), (dm_b, g2), (drow_t, _rs(128))],
                    [(_sd((n, 128), BF16), _rs(128))])[0]


def _merge(g1, g2, g3, p1, p2, p3):
    return jax.nn.sigmoid(g1) * p1 + jax.nn.sigmoid(g2) * p2 + jax.nn.sigmoid(g3) * p3


def merge_fwd(name, u, p1, p2, p3):
    n = u.shape[0]

    def body(g1, g2, g3, a, b, c, o_ref):
        o_ref[...] = _merge(g1[...], g2[...], g3[...], a[...], b[...], c[...]).astype(BF16)

    return _rowcall(name, body, n, [(u, _rs(D, 0)), (u, _rs(D, 1)), (u, _rs(D, 2)), (p1, _rs(D)), (p2, _rs(D)), (p3, _rs(D))],
                    [(_sd((n, D), BF16), _rs(D))])[0]


def merge_bwd(name, u, p1, p2, p3, dm):
    n = u.shape[0]

    def body(g1, g2, g3, a, b, c, dm_ref, d1, d2, d3, dg_ref):
        _, vjp = jax.vjp(_merge, g1[...], g2[...], g3[...], a[...], b[...], c[...])
        r = vjp(dm_ref[...])
        for k in range(3):
            dg_ref[:, D * k:D * k + D] = r[k].astype(BF16)
        d1[...] = r[3].astype(BF16)
        d2[...] = r[4].astype(BF16)
        d3[...] = r[5].astype(BF16)

    return _rowcall(name, body, n,
                    [(u, _rs(D, 0)), (u, _rs(D, 1)), (u, _rs(D, 2)), (p1, _rs(D)), (p2, _rs(D)), (p3, _rs(D)), (dm, _rs(D))],
                    [(_sd((n, D), BF16), _rs(D))] * 3 + [(_sd((n, 3 * D), BF16), _rs(3 * D))])


def _swiglu(g, u):
    return _silu(g) * u


def swiglu_fwd(name, gu):
    n = gu.shape[0]

    def body(g_ref, u_ref, o_ref):
        o_ref[...] = _swiglu(g_ref[...], u_ref[...]).astype(BF16)

    return _rowcall(name, body, n, [(gu, _rs(FFN, 0)), (gu, _rs(FFN, 1))], [(_sd((n, FFN), BF16), _rs(FFN))])[0]


def swiglu_bwd(name, gu, da):
    n = gu.shape[0]

    def body(g_ref, u_ref, da_ref, o_ref):
        _, vjp = jax.vjp(_swiglu, g_ref[...], u_ref[...])
        dg, du = vjp(da_ref[...])
        o_ref[:, 0:FFN] = dg.astype(BF16)
        o_ref[:, FFN:2 * FFN] = du.astype(BF16)

    return _rowcall(name, body, n, [(gu, _rs(FFN, 0)), (gu, _rs(FFN, 1)), (da, _rs(FFN))],
                    [(_sd((n, 2 * FFN), BF16), _rs(2 * FFN))])[0]


def _band_mask(tq, tk, i, kb):
    qp = i * tq + lax.broadcasted_iota(jnp.int32, (tq, tk), 0)
    kp = kb * tk + lax.broadcasted_iota(jnp.int32, (tq, tk), 1)
    return jnp.abs(qp - kp) <= SWA_WIN


def flash_fwd(name, qa, ka, va, *, w, vw, hq, grp, vcol0, scale, nlat, tq, tk, band, sink, ctx_q, prev=None):
    n = qa.shape[0]
    cblk = nlat // NCTX
    band = band and not ctx_q
    if ctx_q:
        tq = tk = NCTX
        grid = (hq, 1, 1)
        qmap = lambda h, i, kk: (cblk, h)
        kmap = lambda h, i, kk: (cblk, h // grp)
        vmap = lambda h, i, kk: (cblk, vcol0 + h // grp)
        omap = lambda h, i, kk: (cblk, h)
        lmap = lambda h, i, kk: (h, cblk, 0)
    else:
        nb = nlat // tk
        nk = 3 if band else nb
        grid = (hq, nlat // tq, nk)
        kb_of = (lambda i, kk: jnp.clip(i + kk - 1, 0, nb - 1)) if band else (lambda i, kk: kk)
        qmap = lambda h, i, kk: (i, h)
        kmap = lambda h, i, kk: (kb_of(i, kk), h // grp)
        vmap = lambda h, i, kk: (kb_of(i, kk), vcol0 + h // grp)
        omap = lambda h, i, kk: (i, h)
        lmap = lambda h, i, kk: (h, i, 0)
    nk = grid[2]
    extra = not ctx_q
    has_sink = sink is not None

    def body(*refs):
        refs = list(refs)
        q_ref, k_ref, v_ref = refs[:3]
        pos = 3
        if extra:
            ke_ref, ve_ref = refs[pos:pos + 2]
            pos += 2
        if has_sink:
            s_ref = refs[pos]
            pos += 1
        if prev is not None:
            pos += 2
        o_ref, l_ref, m_s, l_s, a_s = refs[pos:pos + 5]
        i = pl.program_id(1)
        kk = pl.program_id(2)
        q = q_ref[...]

        def step(kblk, vblk, mask):
            s = _d(q, kblk, ((1,), (1,))) * scale
            if mask is not None:
                s = jnp.where(mask, s, NEG)
            m_new = jnp.maximum(m_s[...], jnp.max(s, axis=1, keepdims=True))
            alpha = jnp.exp(m_s[...] - m_new)
            p = jnp.exp(s - m_new)
            l_s[...] = alpha * l_s[...] + jnp.sum(p, axis=1, keepdims=True)
            a_s[...] = alpha * a_s[...] + _d(p, vblk, ((1,), (0,)))
            m_s[...] = m_new

        @pl.when(kk == 0)
        def _():
            if has_sink:
                sv = jnp.max(s_ref[0], axis=1, keepdims=True)
                m_s[...] = jnp.zeros((tq, 1), F32) + sv
                l_s[...] = jnp.ones((tq, 1), F32)
            else:
                m_s[...] = jnp.full((tq, 1), NEG, F32)
                l_s[...] = jnp.zeros((tq, 1), F32)
            a_s[...] = jnp.zeros((tq, vw), F32)
            if extra:
                step(ke_ref[...], ve_ref[...], None)

        if band:
            kb = i + kk - 1

            @pl.when((kb >= 0) & (kb < nlat // tk))
            def _():
                step(k_ref[...], v_ref[...], _band_mask(tq, tk, i, kb))
        else:
            step(k_ref[...], v_ref[...], None)

        @pl.when(kk == nk - 1)
        def _():
            o_ref[...] = (a_s[...] / l_s[...]).astype(BF16)
            l_ref[0] = m_s[...] + jnp.log(l_s[...])

    ins = [(qa, pl.BlockSpec((tq, w), qmap)), (ka, pl.BlockSpec((tk, w), kmap)), (va, pl.BlockSpec((tk, vw), vmap))]
    if extra:
        ins += [(ka, pl.BlockSpec((NCTX, w), lambda h, i, kk: (cblk, h // grp))),
                (va, pl.BlockSpec((NCTX, vw), lambda h, i, kk: (cblk, vcol0 + h // grp)))]
    if has_sink:
        ins += [(sink, pl.BlockSpec((1, 1, 128), lambda h, i, kk: (h, 0, 0)))]
    aliases = {}
    if prev is not None:
        any_spec = pl.BlockSpec(memory_space=pl.ANY)
        aliases = {len(ins): 0, len(ins) + 1: 1}
        ins += [(prev[0], any_spec), (prev[1], any_spec)]
    return pl.pallas_call(
        body, out_shape=[_sd((n, hq * vw), BF16), _sd((hq, n, 1), F32)], grid=grid,
        in_specs=[s for _, s in ins],
        out_specs=[pl.BlockSpec((tq, vw), omap), pl.BlockSpec((1, tq, 1), lmap)],
        scratch_shapes=[pltpu.VMEM((tq, 1), F32), pltpu.VMEM((tq, 1), F32), pltpu.VMEM((tq, vw), F32)],
        input_output_aliases=aliases, name=name,
        compiler_params=_cp(("parallel", "parallel", "arbitrary")))(*[a for a, _ in ins])


def flash_dq(name, qa, ka, va, oa, doa, lse, *, w, vw, hq, grp, vcol0, scale, nlat, tq, tk, band, sink, ctx_q, prev=None):
    n = qa.shape[0]
    cblk = nlat // NCTX
    band = band and not ctx_q
    if ctx_q:
        tq = tk = NCTX
        grid = (hq, 1, 1)
        qmap = lambda h, i, kk: (cblk, h)
        kmap = lambda h, i, kk: (cblk, h // grp)
        vmap = lambda h, i, kk: (cblk, vcol0 + h // grp)
        lmap = lambda h, i, kk: (h, cblk, 0)
    else:
        nb = nlat // tk
        grid = (hq, nlat // tq, 3 if band else nb)
        kb_of = (lambda i, kk: jnp.clip(i + kk - 1, 0, nb - 1)) if band else (lambda i, kk: kk)
        qmap = lambda h, i, kk: (i, h)
        kmap = lambda h, i, kk: (kb_of(i, kk), h // grp)
        vmap = lambda h, i, kk: (kb_of(i, kk), vcol0 + h // grp)
        lmap = lambda h, i, kk: (h, i, 0)
    nk = grid[2]
    nq = grid[1]
    extra = not ctx_q
    has_sink = sink is not None

    def body(*refs):
        refs = list(refs)
        q_ref, k_ref, v_ref, o_ref, do_ref, l_ref = refs[:6]
        pos = 6
        if extra:
            ke_ref, ve_ref = refs[pos:pos + 2]
            pos += 2
        if has_sink:
            s_ref = refs[pos]
            pos += 1
        if prev is not None:
            pos += 2
        dq_ref, dl_ref, ds_ref, acc_s, dl_s = refs[pos:pos + 5]
        i = pl.program_id(1)
        kk = pl.program_id(2)
        q = q_ref[...]
        do = do_ref[...]
        lse_v = l_ref[0]

        def step(kblk, vblk, mask):
            s = _d(q, kblk, ((1,), (1,))) * scale
            if mask is not None:
                s = jnp.where(mask, s, NEG)
            p = jnp.exp(s - lse_v)
            dp = _d(do, vblk, ((1,), (1,)))
            ds = p * (dp - dl_s[...]) * scale
            acc_s[...] += _d(ds, kblk, ((1,), (0,)))

        @pl.when(kk == 0)
        def _():
            delta = jnp.sum(do * o_ref[...].astype(F32), axis=1, keepdims=True)
            dl_s[...] = delta
            acc_s[...] = jnp.zeros((tq, w), F32)
            if has_sink:
                sv = jnp.max(s_ref[0], axis=1, keepdims=True)
                dsk = jnp.sum(-jnp.exp(sv - lse_v) * delta, axis=0, keepdims=True)
                _acc(ds_ref, jnp.zeros((1, 1, 128), F32) + dsk, i == 0)
            else:
                ds_ref[...] = jnp.zeros((1, 1, 128), F32)
            if extra:
                step(ke_ref[...], ve_ref[...], None)

        if band:
            kb = i + kk - 1

            @pl.when((kb >= 0) & (kb < nlat // tk))
            def _():
                step(k_ref[...], v_ref[...], _band_mask(tq, tk, i, kb))
        else:
            step(k_ref[...], v_ref[...], None)

        @pl.when(kk == nk - 1)
        def _():
            dq_ref[...] = acc_s[...]
            dl_ref[0] = dl_s[...]

    ins = [(qa, pl.BlockSpec((tq, w), qmap)), (ka, pl.BlockSpec((tk, w), kmap)), (va, pl.BlockSpec((tk, vw), vmap)),
           (oa, pl.BlockSpec((tq, vw), qmap)), (doa, pl.BlockSpec((tq, vw), qmap)), (lse, pl.BlockSpec((1, tq, 1), lmap))]
    if extra:
        ins += [(ka, pl.BlockSpec((NCTX, w), lambda h, i, kk: (cblk, h // grp))),
                (va, pl.BlockSpec((NCTX, vw), lambda h, i, kk: (cblk, vcol0 + h // grp)))]
    if has_sink:
        ins += [(sink, pl.BlockSpec((1, 1, 128), lambda h, i, kk: (h, 0, 0)))]
    aliases = {}
    if prev is not None:
        any_spec = pl.BlockSpec(memory_space=pl.ANY)
        aliases = {len(ins): 0, len(ins) + 1: 1}
        ins += [(prev[0], any_spec), (prev[1], any_spec)]
    del nq
    return pl.pallas_call(
        body, out_shape=[_sd((n, hq * w), F32), _sd((hq, n, 1), F32), _sd((hq, 1, 128), F32)], grid=grid,
        in_specs=[s for _, s in ins],
        out_specs=[pl.BlockSpec((tq, w), qmap), pl.BlockSpec((1, tq, 1), lmap),
                   pl.BlockSpec((1, 1, 128), lambda h, i, kk: (h, 0, 0))],
        scratch_shapes=[pltpu.VMEM((tq, w), F32), pltpu.VMEM((tq, 1), F32)],
        input_output_aliases=aliases, name=name,
        compiler_params=_cp(("parallel", "arbitrary", "arbitrary")))(*[a for a, _ in ins])


def flash_dkv(name, qa, ka, va, doa, lse, delta, *, w, vw, hkv, grp, vcol0, scale, nlat, tq, tk, band, ctx_k, prev=None):
    n = qa.shape[0]
    cblk = nlat // NCTX
    nqb = nlat // tq
    band = band and not ctx_k
    if ctx_k:
        tk = NCTX
        nqs = nqb
        grid = (hkv, 1, grp * nqs)
        kmap = lambda hk, j, t: (cblk, hk)
        vmap = lambda hk, j, t: (cblk, vcol0 + hk)
        dvmap = lambda hk, j, t: (cblk, hk)
        qb_of = lambda j, t: t % nqs
    else:
        nqs = 3 if band else nqb
        grid = (hkv, nlat // tk, grp * nqs)
        kmap = lambda hk, j, t: (j, hk)
        vmap = lambda hk, j, t: (j, vcol0 + hk)
        dvmap = lambda hk, j, t: (j, hk)
        qb_of = (lambda j, t: jnp.clip(j + t % nqs - 1, 0, nqb - 1)) if band else (lambda j, t: t % nqs)
    qmap = lambda hk, j, t: (qb_of(j, t), hk * grp + t // nqs)
    lmap = lambda hk, j, t: (hk * grp + t // nqs, qb_of(j, t), 0)

    def body(*refs):
        refs = list(refs)
        q_ref, k_ref, v_ref, do_ref, l_ref, dl_ref = refs[:6]
        pos = 6
        if ctx_k:
            qe_ref, doe_ref, le_ref, dle_ref = refs[pos:pos + 4]
            pos += 4
        if prev is not None:
            pos += 2
        dk_ref, dv_ref = refs[pos:pos + 2]
        j = pl.program_id(1)
        t = pl.program_id(2)
        kblk = k_ref[...]
        vblk = v_ref[...]

        def contrib(q, do, lse_v, dl_v, mask):
            s = _d(q, kblk, ((1,), (1,))) * scale
            if mask is not None:
                s = jnp.where(mask, s, NEG)
            p = jnp.exp(s - lse_v)
            dp = _d(do, vblk, ((1,), (1,)))
            ds = p * (dp - dl_v) * scale
            return _d(ds, q, ((0,), (0,))), _d(p, do, ((0,), (0,)))

        @pl.when(t == 0)
        def _():
            dk = jnp.zeros((tk, w), F32)
            dv = jnp.zeros((tk, vw), F32)
            if ctx_k:
                for gi in range(grp):
                    a, b = contrib(qe_ref[:, w * gi:w * gi + w], doe_ref[:, vw * gi:vw * gi + vw], le_ref[gi], dle_ref[gi], None)
                    dk = dk + a
                    dv = dv + b
            dk_ref[...] = dk
            dv_ref[...] = dv

        def add(mask):
            a, b = contrib(q_ref[...], do_ref[...], l_ref[0], dl_ref[0], mask)
            dk_ref[...] += a
            dv_ref[...] += b

        if band:
            qb = j + t % nqs - 1

            @pl.when((qb >= 0) & (qb < nqb))
            def _():
                add(_band_mask(tq, tk, qb, j))
        else:
            add(None)

    ins = [(qa, pl.BlockSpec((tq, w), qmap)), (ka, pl.BlockSpec((tk, w), kmap)), (va, pl.BlockSpec((tk, vw), vmap)),
           (doa, pl.BlockSpec((tq, vw), qmap)), (lse, pl.BlockSpec((1, tq, 1), lmap)), (delta, pl.BlockSpec((1, tq, 1), lmap))]
    if ctx_k:
        ins += [(qa, pl.BlockSpec((NCTX, grp * w), lambda hk, j, t: (cblk, hk))),
                (doa, pl.BlockSpec((NCTX, grp * vw), lambda hk, j, t: (cblk, hk))),
                (lse, pl.BlockSpec((grp, NCTX, 1), lambda hk, j, t: (hk, cblk, 0))),
                (delta, pl.BlockSpec((grp, NCTX, 1), lambda hk, j, t: (hk, cblk, 0)))]
    aliases = {}
    if prev is not None:
        any_spec = pl.BlockSpec(memory_space=pl.ANY)
        aliases = {len(ins): 0, len(ins) + 1: 1}
        ins += [(prev[0], any_spec), (prev[1], any_spec)]
    return pl.pallas_call(
        body, out_shape=[_sd((n, hkv * w), F32), _sd((n, hkv * vw), F32)], grid=grid,
        in_specs=[s for _, s in ins],
        out_specs=[pl.BlockSpec((tk, w), kmap), pl.BlockSpec((tk, vw), dvmap)],
        input_output_aliases=aliases, name=name,
        compiler_params=_cp(("parallel", "parallel", "arbitrary")))(*[a for a, _ in ins])


def attention_fwd(tag, qa, ka, va, sink, cfg, nlat):
    o, lse = flash_fwd(tag + "_fwd_lat", qa, ka, va, sink=sink, ctx_q=False, nlat=nlat, **cfg)
    return flash_fwd(tag + "_fwd_ctx", qa, ka, va, sink=sink, ctx_q=True, nlat=nlat, prev=(o, lse), **cfg)


def attention_bwd(tag, qa, ka, va, oa, doa, lse, sink, cfg, nlat):
    dq, delta, ds1 = flash_dq(tag + "_dq_lat", qa, ka, va, oa, doa, lse, sink=sink, ctx_q=False, nlat=nlat, **cfg)
    dq, delta, ds2 = flash_dq(tag + "_dq_ctx", qa, ka, va, oa, doa, lse, sink=sink, ctx_q=True, nlat=nlat,
                              prev=(dq, delta), **cfg)
    kc = {k: v for k, v in cfg.items() if k != "hq"}
    kc["hkv"] = cfg["hq"] // cfg["grp"]
    dk, dv = flash_dkv(tag + "_dkv_lat", qa, ka, va, doa, lse, delta, ctx_k=False, nlat=nlat, **kc)
    dk, dv = flash_dkv(tag + "_dkv_ctx", qa, ka, va, doa, lse, delta, ctx_k=True, nlat=nlat, prev=(dk, dv), **kc)
    return dq, dk, dv, ds1 + ds2


def adamw(name, w, g, m, v):
    r, c = w.shape
    tr = _pick(r, (256, 128, 64, 32, 16, 8))
    bc1 = 1.0 - ADAM_B1 ** ADAM_STEP
    bc2 = 1.0 - ADAM_B2 ** ADAM_STEP

    def body(w_ref, g_ref, m_ref, v_ref, d_ref, nm_ref, nv_ref):
        gv = g_ref[...]
        nm = ADAM_B1 * m_ref[...] + (1.0 - ADAM_B1) * gv
        nv = ADAM_B2 * v_ref[...] + (1.0 - ADAM_B2) * (gv * gv)
        d_ref[...] = -ADAM_LR * ((nm / bc1) / (jnp.sqrt(nv / bc2) + ADAM_EPS) + ADAM_WD * w_ref[...])
        nm_ref[...] = nm
        nv_ref[...] = nv

    spec = pl.BlockSpec((tr, c), lambda i: (i, 0))
    return pl.pallas_call(body, out_shape=[_sd((r, c), F32)] * 3, grid=(r // tr,), in_specs=[spec] * 4, out_specs=[spec] * 3,
                          name=name, compiler_params=_cp(("parallel",)))(w, g, m, v)


def _coords():
    return lax.axis_index("x"), lax.axis_index("y"), lax.axis_index("c")


_ANY = pl.BlockSpec(memory_space=pl.ANY)


def gather_chips(name, a):
    def body(a_ref, o_ref, send_sems, recv_sems, loc_sem):
        x, y, c = _coords()
        me = 2 * x + y
        peers = [(1 - x, y), (x, 1 - y), (1 - x, 1 - y)]
        mine = pltpu.make_async_copy(a_ref, o_ref.at[me], loc_sem)
        mine.start()
        sends = [pltpu.make_async_remote_copy(a_ref, o_ref.at[me], send_sems.at[k], recv_sems.at[k],
                                              device_id=(px, py, c), device_id_type=MESH)
                 for k, (px, py) in enumerate(peers)]
        for cp in sends:
            cp.start()
        for k, (px, py) in enumerate(peers):
            pltpu.make_async_remote_copy(a_ref, o_ref.at[2 * px + py], send_sems.at[k], recv_sems.at[k],
                                         device_id=(px, py, c), device_id_type=MESH).wait_recv()
        for cp in sends:
            cp.wait_send()
        mine.wait()

    return pl.pallas_call(
        body, out_shape=_sd((4,) + a.shape, a.dtype), in_specs=[_ANY], out_specs=_ANY,
        scratch_shapes=[pltpu.SemaphoreType.DMA((3,)), pltpu.SemaphoreType.DMA((3,)), pltpu.SemaphoreType.DMA],
        name=name, compiler_params=pltpu.CompilerParams(has_side_effects=True))(a)


def scatter_chips(name, a):
    def body(a_ref, o_ref, send_sems, recv_sems, loc_sem):
        x, y, c = _coords()
        me = 2 * x + y
        peers = [(1 - x, y), (x, 1 - y), (1 - x, 1 - y)]
        mine = pltpu.make_async_copy(a_ref.at[me], o_ref.at[me], loc_sem)
        mine.start()
        sends = [pltpu.make_async_remote_copy(a_ref.at[2 * px + py], o_ref.at[me], send_sems.at[k], recv_sems.at[k],
                                              device_id=(px, py, c), device_id_type=MESH)
                 for k, (px, py) in enumerate(peers)]
        for cp in sends:
            cp.start()
        for k, (px, py) in enumerate(peers):
            pltpu.make_async_remote_copy(a_ref.at[me], o_ref.at[2 * px + py], send_sems.at[k], recv_sems.at[k],
                                         device_id=(px, py, c), device_id_type=MESH).wait_recv()
        for cp in sends:
            cp.wait_send()
        mine.wait()

    return pl.pallas_call(
        body, out_shape=_sd(a.shape, a.dtype), in_specs=[_ANY], out_specs=_ANY,
        scratch_shapes=[pltpu.SemaphoreType.DMA((3,)), pltpu.SemaphoreType.DMA((3,)), pltpu.SemaphoreType.DMA],
        name=name, compiler_params=pltpu.CompilerParams(has_side_effects=True))(a)


def sibling_swap(name, a):
    def body(a_ref, o_ref, send_sem, recv_sem):
        x, y, c = _coords()
        cp = pltpu.make_async_remote_copy(a_ref, o_ref, send_sem, recv_sem, device_id=(x, y, 1 - c), device_id_type=MESH)
        cp.start()
        cp.wait()

    return pl.pallas_call(
        body, out_shape=_sd(a.shape, a.dtype), in_specs=[_ANY], out_specs=_ANY,
        scratch_shapes=[pltpu.SemaphoreType.DMA, pltpu.SemaphoreType.DMA],
        name=name, compiler_params=pltpu.CompilerParams(has_side_effects=True))(a)


def gather_all(name, a):
    def body(a_ref, o_ref, send_sems, recv_sems, loc_sem):
        x, y, c = _coords()
        me = 4 * x + 2 * y + c
        flips = [(fx, fy, fc) for fx in (0, 1) for fy in (0, 1) for fc in (0, 1) if fx + fy + fc > 0]
        peers = [(x ^ fx, y ^ fy, c ^ fc) for fx, fy, fc in flips]
        mine = pltpu.make_async_copy(a_ref, o_ref.at[me], loc_sem)
        mine.start()
        sends = [pltpu.make_async_remote_copy(a_ref, o_ref.at[me], send_sems.at[k], recv_sems.at[k],
                                              device_id=p, device_id_type=MESH) for k, p in enumerate(peers)]
        for cp in sends:
            cp.start()
        for k, (px, py, pc) in enumerate(peers):
            pltpu.make_async_remote_copy(a_ref, o_ref.at[4 * px + 2 * py + pc], send_sems.at[k], recv_sems.at[k],
                                         device_id=(px, py, pc), device_id_type=MESH).wait_recv()
        for cp in sends:
            cp.wait_send()
        mine.wait()

    return pl.pallas_call(
        body, out_shape=_sd((8,) + a.shape, a.dtype), in_specs=[_ANY], out_specs=_ANY,
        scratch_shapes=[pltpu.SemaphoreType.DMA((7,)), pltpu.SemaphoreType.DMA((7,)), pltpu.SemaphoreType.DMA],
        name=name, compiler_params=pltpu.CompilerParams(has_side_effects=True))(a)


def sum_blocks(name, a):
    k, r, c = a.shape
    tr = _pick(r, (256, 128, 64, 32, 16, 8))

    def body(a_ref, o_ref):
        acc = a_ref[0].astype(F32)
        for s in range(1, k):
            acc = acc + a_ref[s].astype(F32)
        o_ref[...] = acc

    return pl.pallas_call(body, out_shape=_sd((r, c), F32), grid=(r // tr,),
                          in_specs=[pl.BlockSpec((k, tr, c), lambda i: (0, i, 0))], out_specs=pl.BlockSpec((tr, c), lambda i: (i, 0)),
                          name=name, compiler_params=_cp(("parallel",)))(a)


def add_ordered(name, a, b):
    r, c = a.shape
    tr = _pick(r, (256, 128, 64, 32, 16, 8))

    def body(a_ref, b_ref, o_ref):
        o_ref[...] = a_ref[...] + b_ref[...]

    spec = pl.BlockSpec((tr, c), lambda i: (i, 0))
    return pl.pallas_call(body, out_shape=_sd((r, c), F32), grid=(r // tr,), in_specs=[spec, spec], out_specs=spec,
                          name=name, compiler_params=_cp(("parallel",)))(a, b)


BIG = ("w_mod", "w_in", "w_mla_uq", "w_mla_ukv", "w_p_ssm", "w_p_swa", "w_p_mla", "w_out", "w_ffn_in", "w_ffn_out")
COL_SHARDED = ("w_mod", "w_in", "w_mla_uq", "w_mla_ukv", "w_ffn_in")
SMALL = ("c_ctx", "b_mod", "norm1_g", "norm2_g", "ssm_conv_w", "ssm_conv_b", "ssm_dt_bias", "ssm_a_log", "ssm_d",
         "ssm_norm_g", "swa_q_norm_g", "swa_k_norm_g", "swa_sink", "mla_q_lat_g", "mla_kv_lat_g", "mla_q_norm_g",
         "mla_k_norm_g")
WEIGHTS = ("c_ctx", "w_mod", "b_mod", "norm1_g", "norm2_g", "w_in", "ssm_conv_w", "ssm_conv_b", "ssm_dt_bias", "ssm_a_log",
           "ssm_d", "ssm_norm_g", "swa_q_norm_g", "swa_k_norm_g", "swa_sink", "mla_q_lat_g", "mla_kv_lat_g", "w_mla_uq",
           "w_mla_ukv", "mla_q_norm_g", "mla_k_norm_g", "w_p_ssm", "w_p_swa", "w_p_mla", "w_out", "w_ffn_in", "w_ffn_out")


def pack_w_in(w):
    z = lambda k: jnp.zeros((w.shape[0], k), w.dtype)
    return jnp.concatenate([w[:, 4832:7904], w[:, 2400:3424], w[:, 3424:4448], w[:, 0:1536], w[:, 1568:1824], w[:, 1824:2080],
                            w[:, 2080:2336], w[:, 2336:2400], w[:, 1536:1568], z(32), z(128), w[:, 4448:4832]], axis=1)


def unpack_w_in(g):
    return jnp.concatenate([g[:, 5120:6656], g[:, 7488:7520], g[:, 6656:6912], g[:, 6912:7168], g[:, 7168:7424], g[:, 7424:7488],
                            g[:, 3072:4096], g[:, 4096:5120], g[:, 7680:8064], g[:, 0:3072]], axis=1)


def pack_ukv(w):
    return w.reshape(MLA_KVRANK, MLA_H, 2, 128).transpose(0, 2, 1, 3).reshape(MLA_KVRANK, 2048)


def unpack_ukv(g):
    return g.reshape(MLA_KVRANK, 2, MLA_H, 128).transpose(0, 2, 1, 3).reshape(MLA_KVRANK, 2048)


def pack_uq(w):
    return jnp.pad(w.reshape(MLA_QRANK, MLA_H, 192), ((0, 0), (0, 0), (0, 64))).reshape(MLA_QRANK, 2048)


def unpack_uq(g):
    return g.reshape(MLA_QRANK, MLA_H, 256)[:, :, :192].reshape(MLA_QRANK, 1536)


def rope_tables(nlat):
    t = jnp.arange(nlat, dtype=jnp.int32)
    r = (t // GRID_W).astype(F32)[:, None]
    col = (t % GRID_W).astype(F32)[:, None]

    def tab(nf, pad):
        inv = jnp.power(ROPE_BASE, -jnp.arange(nf, dtype=F32) / nf)
        ar, ac = r * inv, col * inv
        cos = jnp.concatenate([jnp.cos(ar), jnp.cos(ar), jnp.cos(ac), jnp.cos(ac), jnp.ones((nlat, pad), F32)], axis=1)
        sin = jnp.concatenate([-jnp.sin(ar), jnp.sin(ar), -jnp.sin(ac), jnp.sin(ac), jnp.zeros((nlat, pad), F32)], axis=1)
        cos = jnp.concatenate([cos, jnp.ones((NCTX, 128), F32)], axis=0)
        sin = jnp.concatenate([sin, jnp.zeros((NCTX, 128), F32)], axis=0)
        return cos, sin

    return tab(32, 0), tab(16, 64)


def _lanes(v, start, width=128):
    return jnp.zeros((1, width), F32).at[0, start:start + v.shape[0]].set(v)


def layer_fwd(i, xin, h, mod, p, tabs, nlat):
    t = "l%d_" % i
    n = xin.shape[0]
    (cos_s, sin_s), (cos_m, sin_m) = tabs
    u = mm(h, p["w_in"], F32, t + "in_proj")
    xbc = conv_fwd(t + "conv", u, p["conv_w"], p["conv_b"], nlat)
    dtrow = jnp.transpose(u[:, C_MISC + DT_LANE:C_MISC + DT_LANE + 32])
    nlc = nlat // Q
    yf, hs_f = ssd_fwd(t + "ssd_f", xbc, u, dtrow, p["bias_c"], p["alog_c"], p["bias_r"], p["alog_r"], nlc, False, 0)
    yb, hs_b = ssd_fwd(t + "ssd_b", xbc, u, dtrow, p["bias_c"], p["alog_c"], p["bias_r"], p["alog_r"], nlc, True, 1)
    ys = ssd_out_fwd(t + "ssd_out", yf, yb, xbc, u, p["ssm_norm_g"], p["d_exp"])
    qs, ks = swa_prep_fwd(t + "swa_prep", u, p["swa_q_g"], p["swa_k_g"], cos_s, sin_s)
    o_swa, lse_swa = attention_fwd(t + "swa", qs, ks, u, p["sink"], p["swa_cfg"], nlat)
    ckv_n, cq_n = lat_norm_fwd(t + "lat_norm", u, p["kv_lat_g"], p["q_lat_g"])
    kv = mm(ckv_n, p["w_ukv"], F32, t + "ukv")
    qp = mm(cq_n, p["w_uq"], F32, t + "uq")
    km, qm = mla_prep_fwd(t + "mla_prep", kv, qp, u, p["mla_q_g"], p["mla_k_g"], cos_m, sin_m)
    o_mla, lse_mla = attention_fwd(t + "mla", qm, km, kv, None, p["mla_cfg"], nlat)
    p1 = mm(ys, p["w_p_ssm"], F32, t + "p_ssm")
    p2 = mm(o_swa, p["w_p_swa"], F32, t + "p_swa")
    p3 = mm(o_mla, p["w_p_mla"], F32, t + "p_mla")
    merged = merge_fwd(t + "merge", u, p1, p2, p3)
    o = mm(merged, p["w_out"], F32, t + "out_proj")
    x1, h2 = resid_mod_fwd(t + "res1", xin, o, mod, 2, mod, 3, 4, p["norm2_g"], nlat // RT)
    gu = mm(h2, p["w_ffn_in"], F32, t + "ffn_in")
    a = swiglu_fwd(t + "swiglu", gu)
    f = mm(a, p["w_ffn_out"], F32, t + "ffn_out")
    saved = dict(xin=xin, h=h, u=u, xbc=xbc, dtrow=dtrow, yf=yf, yb=yb, hs_f=hs_f, hs_b=hs_b, ys=ys, qs=qs, ks=ks,
                 o_swa=o_swa, lse_swa=lse_swa, ckv_n=ckv_n, cq_n=cq_n, kv=kv, qp=qp, km=km, qm=qm, o_mla=o_mla,
                 lse_mla=lse_mla, p1=p1, p2=p2, p3=p3, merged=merged, o=o, x1=x1, h2=h2, gu=gu, a=a, f=f)
    del n
    return x1, f, saved


def layer_bwd(i, dx2, df, dgt2, sv, mod, p, tabs, nlat):
    t = "l%db_" % i
    (cos_s, sin_s), (cos_m, sin_m) = tabs
    g = {}
    nt = nlat // RT
    nlc = nlat // Q
    g["w_ffn_out"] = mm_tn(sv["a"], df, t + "wg_ffn_out")
    da = mm(df, p["w_ffn_out_t"], F32, t + "dg_ffn_out")
    dgu = swiglu_bwd(t + "swiglu", sv["gu"], da)
    g["w_ffn_in"] = mm_tn(sv["h2"], dgu, t + "wg_ffn_in")
    dh2 = mm(dgu, p["w_ffn_in_t"], F32, t + "dg_ffn_in")
    dx1, do, dgt1, dsh2, dsc2, g["norm2_g"] = resid_mod_bwd(t + "res1", sv["x1"], dx2, dh2, sv["o"], mod, 2, mod, 3, 4,
                                                              p["norm2_g"], nt)
    g["w_out"] = mm_tn(sv["merged"], do, t + "wg_out")
    dmerged = mm(do, p["w_out_t"], F32, t + "dg_out")
    dp1, dp2, dp3, dgates = merge_bwd(t + "merge", sv["u"], sv["p1"], sv["p2"], sv["p3"], dmerged)
    g["w_p_ssm"] = mm_tn(sv["ys"], dp1, t + "wg_p_ssm")
    g["w_p_swa"] = mm_tn(sv["o_swa"], dp2, t + "wg_p_swa")
    g["w_p_mla"] = mm_tn(sv["o_mla"], dp3, t + "wg_p_mla")
    dys = mm(dp1, p["w_p_ssm_t"], F32, t + "dg_p_ssm")
    do_swa = mm(dp2, p["w_p_swa_t"], F32, t + "dg_p_swa")
    do_mla = mm(dp3, p["w_p_mla_t"], F32, t + "dg_p_mla")
    dqm, dkm, dv_mla, _ = attention_bwd(t + "mla", sv["qm"], sv["km"], sv["kv"], sv["o_mla"], do_mla, sv["lse_mla"], None,
                                        p["mla_cfg"], nlat)
    dkv, dqp, dkr, g["mla_q_g"], g["mla_k_g"] = mla_prep_bwd(t + "mla_prep", sv["kv"], sv["qp"], sv["u"], p["mla_q_g"],
                                                             p["mla_k_g"], cos_m, sin_m, dkm, dqm, dv_mla)
    g["w_ukv"] = mm_tn(sv["ckv_n"], dkv, t + "wg_ukv")
    g["w_uq"] = mm_tn(sv["cq_n"], dqp, t + "wg_uq")
    dckv_n = mm(dkv, p["w_ukv_t"], F32, t + "dg_ukv")
    dcq_n = mm(dqp, p["w_uq_t"], F32, t + "dg_uq")
    dckv, dcq, g["kv_lat_g"], g["q_lat_g"] = lat_norm_bwd(t + "lat_norm", sv["u"], p["kv_lat_g"], p["q_lat_g"], dckv_n, dcq_n)
    dqs, dks, dv_swa, g["sink"] = attention_bwd(t + "swa", sv["qs"], sv["ks"], sv["u"], sv["o_swa"], do_swa, sv["lse_swa"],
                                                p["sink"], p["swa_cfg"], nlat)
    dq, dk, dv, g["swa_q_g"], g["swa_k_g"] = swa_prep_bwd(t + "swa_prep", sv["u"], p["swa_q_g"], p["swa_k_g"], cos_s, sin_s,
                                                          dqs, dks, dv_swa)
    dy, dxs_skip, dz, g["ssm_norm_g"], g["d_exp"] = ssd_out_bwd(t + "ssd_out", sv["yf"], sv["yb"], sv["xbc"], sv["u"],
                                                                 p["ssm_norm_g"], p["d_exp"], dys)
    n = dy.shape[0]
    zbc = jnp.zeros((n, 256), F32)
    r_f = ssd_bwd(t + "ssd_f", sv["xbc"], sv["u"], sv["dtrow"], p["bias_c"], p["alog_c"], p["bias_r"], p["alog_r"],
                  sv["hs_f"], dy, (dxs_skip, zbc, zbc), nlc, False, 0)
    r_b = ssd_bwd(t + "ssd_b", sv["xbc"], sv["u"], sv["dtrow"], p["bias_c"], p["alog_c"], p["bias_r"], p["alog_r"],
                  sv["hs_b"], dy, (r_f[0], r_f[1], r_f[2]), nlc, True, 1)
    dact = jnp.concatenate([r_b[0], r_b[1], r_b[2]], axis=1)
    dxbc, g["conv_w"], g["conv_b"] = conv_bwd(t + "conv", sv["u"], dact, p["conv_w"], p["conv_b"], nlat)
    drow = jnp.concatenate([r_f[4][0] + r_f[4][1], r_b[4][0] + r_b[4][1]], axis=0)
    drow_t = jnp.pad(jnp.transpose(drow), ((0, 0), (DT_LANE, 128 - DT_LANE - 32)))
    dmisc = misc_combine(t + "misc", dkr, r_f[3], r_b[3], drow_t)
    g["bias_c"] = r_f[5] + r_b[5]
    g["alog_c"] = r_f[6] + r_b[6]
    g["bias_r"] = jnp.concatenate([r_f[7], r_b[7]], axis=0)
    g["alog_r"] = jnp.concatenate([r_f[8], r_b[8]], axis=0)
    du = jnp.concatenate([dgates, dz, dq, dxbc, dk, dv, dckv, dmisc, jnp.zeros((n, 128), BF16), dcq], axis=1)
    g["w_in"] = mm_tn(sv["h"], du, t + "wg_in")
    dh = mm(du, p["w_in_t"], F32, t + "dg_in")
    g["mod"] = (dgt1, dsh2, dsc2, dgt2)
    return dx1, dh, g


def local_step(x, c, ctx, target, c_ctx, W, nlat):
    xin = jnp.concatenate([x, ctx], axis=0)
    n = xin.shape[0]
    nt = nlat // RT
    tabs = rope_tables(nlat)
    c8 = jnp.zeros((8, D), F32).at[0].set(c[0]).at[1].set(c_ctx)
    mods, silus = [], []
    for i in range(DEPTH):
        m8, s8 = mod_fwd("l%d_mod" % i, c8, W[i]["w_mod"], W[i]["b_mod"])
        mods.append(m8[0:2].reshape(2, 1, 6 * D))
        silus.append(s8)
    saved = []
    _, h = resid_mod_fwd("l0_norm1", xin, None, None, 0, mods[0], 0, 1, W[0]["norm1_g"], nt)
    xcur = xin
    for i in range(DEPTH):
        x1, f, sv = layer_fwd(i, xcur, h, mods[i], W[i], tabs, nlat)
        saved.append(sv)
        if i + 1 < DEPTH:
            xcur, h = resid_mod_fwd("l%d_res2" % i, x1, f, mods[i], 5, mods[i + 1], 0, 1, W[i + 1]["norm1_g"], nt)
    loss_v, dx2, df, dgt2 = resid_loss("loss", x1, f, mods[DEPTH - 1], 5, target, nt)
    grads = [None] * DEPTH
    for i in reversed(range(DEPTH)):
        dx1, dh, g = layer_bwd(i, dx2, df, dgt2, saved[i], mods[i], W[i], tabs, nlat)
        if i > 0:
            sv = saved[i]
            dx2, df, dgt2, dsh1, dsc1, g["norm1_g"] = resid_mod_bwd(
                "l%db_res2" % (i - 1), sv["xin"], dx1, dh, saved[i - 1]["f"], mods[i - 1], 5, mods[i], 0, 1,
                W[i]["norm1_g"], nt)
        else:
            dxin, _, _, dsh1, dsc1, g["norm1_g"] = resid_mod_bwd("l0b_norm1", saved[0]["xin"], dx1, dh, None, None, 0,
                                                                  mods[0], 0, 1, W[0]["norm1_g"], nt)
        dgt1, dsh2, dsc2, dgt2_i = g.pop("mod")
        dmod = jnp.concatenate([dsh1, dsc1, dgt1, dsh2, dsc2, dgt2_i], axis=2).reshape(2, 6 * D)
        dmod8 = jnp.zeros((8, 6 * D), F32).at[0:2].set(dmod)
        g["w_mod"] = mm_tn(silus[i], dmod8, "l%db_wg_mod" % i)
        dsilu = mm(dmod8, W[i]["w_mod_t"], F32, "l%db_dg_mod" % i)
        dc8, g["b_mod"] = mod_small_bwd("l%db_mod_small" % i, c8, dsilu, dmod8)
        g["c8"] = dc8
        grads[i] = g
    del n
    return loss_v[0, 0], dxin, grads


def _big_shapes():
    return dict(w_mod=(2, 1024, 1536), w_in=(2, 1024, 1976), w_mla_uq=(2, 384, 384), w_mla_ukv=(2, 256, 512),
                w_p_ssm=(2, 256, 1024), w_p_swa=(2, 256, 1024), w_p_mla=(2, 256, 1024), w_out=(2, 256, 1024),
                w_ffn_in=(2, 1024, 1408), w_ffn_out=(2, 704, 1024))


def _pack_big(d, dtype):
    return jnp.concatenate([d[k].astype(dtype).reshape(-1, 1024) for k in BIG], axis=0)


def _unpack_big(buf, lead):
    out = {}
    r0 = 0
    for k in BIG:
        sh = _big_shapes()[k]
        rows = sh[0] * sh[1] * sh[2] // 1024
        out[k] = buf[..., r0:r0 + rows, :].reshape(lead + sh)
        r0 += rows
    return out


def _full_from_chips(k, a):
    if k in COL_SHARDED:
        return a.transpose(1, 2, 0, 3).reshape(2, a.shape[2], 4 * a.shape[3])
    return a.transpose(1, 0, 2, 3).reshape(2, 4 * a.shape[2], a.shape[3])


def _chips_from_full(k, a):
    if k in COL_SHARDED:
        return a.reshape(a.shape[0], 4, a.shape[1] // 4).transpose(1, 0, 2)
    return a.reshape(4, a.shape[0] // 4, a.shape[1])


def _small_sizes():
    return dict(c_ctx=1024, b_mod=2 * 6144, norm1_g=2048, norm2_g=2048, ssm_conv_w=2 * 5 * 1536, ssm_conv_b=2 * 1536,
                ssm_dt_bias=64, ssm_a_log=64, ssm_d=32, ssm_norm_g=2048, swa_q_norm_g=256, swa_k_norm_g=256, swa_sink=16,
                mla_q_lat_g=768, mla_kv_lat_g=512, mla_q_norm_g=384, mla_k_norm_g=384)


def _pack_small(d):
    parts = []
    for k in SMALL:
        v = d[k].astype(F32).reshape(-1)
        parts.append(jnp.pad(v, (0, (-v.shape[0]) % 1024)))
    return jnp.concatenate(parts).reshape(-1, 128)


def _unpack_small(buf, shapes):
    flat = buf.reshape(-1)
    out = {}
    o = 0
    for k in SMALL:
        sz = _small_sizes()[k]
        out[k] = flat[o:o + sz].reshape(shapes[k])
        o += sz + (-sz) % 1024
    return out


def big_grads(grads):
    gfull = {k: [] for k in BIG}
    for i in range(DEPTH):
        g = grads[i]
        gfull["w_mod"].append(g["w_mod"])
        gfull["w_in"].append(unpack_w_in(g["w_in"]))
        gfull["w_mla_uq"].append(unpack_uq(g["w_uq"]))
        gfull["w_mla_ukv"].append(unpack_ukv(g["w_ukv"]))
        for k in ("w_p_ssm", "w_p_swa", "w_p_mla", "w_out", "w_ffn_in", "w_ffn_out"):
            gfull[k].append(g[k])
    return gfull


def small_grads(grads):
    gs = {}
    gs["c_ctx"] = sum(grads[i]["c8"][1] for i in range(DEPTH))
    st = lambda f: jnp.stack([f(grads[i]) for i in range(DEPTH)])
    gs["b_mod"] = st(lambda g: g["b_mod"][0])
    gs["norm1_g"] = st(lambda g: g["norm1_g"][0])
    gs["norm2_g"] = st(lambda g: g["norm2_g"][0])
    gs["ssm_conv_w"] = st(lambda g: g["conv_w"])
    gs["ssm_conv_b"] = st(lambda g: g["conv_b"][0])
    gs["ssm_dt_bias"] = st(lambda g: (g["bias_c"][0, DT_LANE:DT_LANE + 32] + g["bias_r"][:, 0]).reshape(2, 16))
    gs["ssm_a_log"] = st(lambda g: (g["alog_c"][0, DT_LANE:DT_LANE + 32] + g["alog_r"][:, 0]).reshape(2, 16))
    gs["ssm_d"] = st(lambda g: g["d_exp"].reshape(16, 64).sum(axis=1))
    gs["ssm_norm_g"] = st(lambda g: g["ssm_norm_g"][0])
    gs["swa_q_norm_g"] = st(lambda g: g["swa_q_g"][0])
    gs["swa_k_norm_g"] = st(lambda g: g["swa_k_g"][0])
    gs["swa_sink"] = st(lambda g: g["sink"][:, 0, 0])
    gs["mla_q_lat_g"] = st(lambda g: g["q_lat_g"][0])
    gs["mla_kv_lat_g"] = st(lambda g: g["kv_lat_g"][0])
    gs["mla_q_norm_g"] = st(lambda g: g["mla_q_g"][0, :192])
    gs["mla_k_norm_g"] = st(lambda g: g["mla_k_g"][0, :192])
    return gs


def layer_params(i, full, conv_full, sm, nlat):
    p = {}
    p["w_mod"] = full["w_mod"][i]
    p["w_in"] = pack_w_in(full["w_in"][i])
    p["w_uq"] = pack_uq(full["w_mla_uq"][i])
    p["w_ukv"] = pack_ukv(full["w_mla_ukv"][i])
    for k in ("w_p_ssm", "w_p_swa", "w_p_mla", "w_out", "w_ffn_in", "w_ffn_out"):
        p[k] = full[k][i]
    for k in ("w_mod", "w_in", "w_uq", "w_ukv", "w_p_ssm", "w_p_swa", "w_p_mla", "w_out", "w_ffn_in", "w_ffn_out"):
        p[k + "_t"] = jnp.transpose(p[k])
    p["b_mod"] = sm["b_mod"][i][None]
    p["norm1_g"] = sm["norm1_g"][i][None]
    p["norm2_g"] = sm["norm2_g"][i][None]
    p["conv_w"] = conv_full[i]
    p["conv_b"] = sm["ssm_conv_b"][i][None]
    bias = sm["ssm_dt_bias"][i].reshape(32)
    alog = sm["ssm_a_log"][i].reshape(32)
    p["bias_c"] = _lanes(bias, DT_LANE)
    p["alog_c"] = _lanes(alog, DT_LANE)
    p["bias_r"] = bias[:, None]
    p["alog_r"] = alog[:, None]
    p["d_exp"] = jnp.repeat(sm["ssm_d"][i], 64)[None]
    p["ssm_norm_g"] = sm["ssm_norm_g"][i][None]
    p["swa_q_g"] = sm["swa_q_norm_g"][i][None]
    p["swa_k_g"] = sm["swa_k_norm_g"][i][None]
    p["sink"] = jnp.broadcast_to(sm["swa_sink"][i][:, None, None], (SWA_HQ, 1, 128))
    p["q_lat_g"] = sm["mla_q_lat_g"][i][None]
    p["kv_lat_g"] = sm["mla_kv_lat_g"][i][None]
    p["mla_q_g"] = _lanes(sm["mla_q_norm_g"][i], 0, 256)
    p["mla_k_g"] = _lanes(sm["mla_k_norm_g"][i], 0, 256)
    p["swa_cfg"] = dict(w=128, vw=128, hq=SWA_HQ, grp=SWA_HQ // SWA_HKV, vcol0=C_V // 128, scale=SWA_DH ** -0.5,
                        tq=256, tk=256, band=True)
    p["mla_cfg"] = dict(w=256, vw=128, hq=MLA_H, grp=1, vcol0=8, scale=(MLA_NOPE + MLA_ROPE) ** -0.5,
                        tq=min(512, nlat), tk=min(1024, nlat), band=False)
    return p


def kernel(x, c, ctx, c_ctx, w_mod, b_mod, norm1_g, norm2_g, w_in, ssm_conv_w, ssm_conv_b, ssm_dt_bias, ssm_a_log, ssm_d, ssm_norm_g, swa_q_norm_g, swa_k_norm_g, swa_sink, mla_q_lat_g, mla_kv_lat_g, w_mla_uq, w_mla_ukv, mla_q_norm_g, mla_k_norm_g, w_p_ssm, w_p_swa, w_p_mla, w_out, w_ffn_in, w_ffn_out, loss_target, m_c_ctx, m_w_mod, m_b_mod, m_norm1_g, m_norm2_g, m_w_in, m_ssm_conv_w, m_ssm_conv_b, m_ssm_dt_bias, m_ssm_a_log, m_ssm_d, m_ssm_norm_g, m_swa_q_norm_g, m_swa_k_norm_g, m_swa_sink, m_mla_q_lat_g, m_mla_kv_lat_g, m_w_mla_uq, m_w_mla_ukv, m_mla_q_norm_g, m_mla_k_norm_g, m_w_p_ssm, m_w_p_swa, m_w_p_mla, m_w_out, m_w_ffn_in, m_w_ffn_out, v_c_ctx, v_w_mod, v_b_mod, v_norm1_g, v_norm2_g, v_w_in, v_ssm_conv_w, v_ssm_conv_b, v_ssm_dt_bias, v_ssm_a_log, v_ssm_d, v_ssm_norm_g, v_swa_q_norm_g, v_swa_k_norm_g, v_swa_sink, v_mla_q_lat_g, v_mla_kv_lat_g, v_w_mla_uq, v_w_mla_ukv, v_mla_q_norm_g, v_mla_k_norm_g, v_w_p_ssm, v_w_p_swa, v_w_p_mla, v_w_out, v_w_ffn_in, v_w_ffn_out):
    loc = dict(locals())
    w = {k: loc[k] for k in WEIGHTS}
    m = {k: loc["m_" + k] for k in WEIGHTS}
    v = {k: loc["v_" + k] for k in WEIGHTS}
    nlat = x.shape[1]

    gathered = _unpack_big(gather_chips("gather_weights", _pack_big(w, BF16)), (4,))
    full = {k: _full_from_chips(k, gathered[k]) for k in BIG}
    conv_sh = jnp.pad(ssm_conv_w.reshape(10, 384), ((0, 6), (0, 0)))
    conv_full = gather_chips("gather_conv", conv_sh)[:, :10].reshape(4, 2, 5, 384).transpose(1, 2, 0, 3).reshape(2, 5, 1536)

    W = [layer_params(i, full, conv_full, w, nlat) for i in range(DEPTH)]

    loss_loc, dx, grads = local_step(x[0], c, ctx[0], loss_target[0], c_ctx, W, nlat)

    gfull = big_grads(grads)
    by_chip = {k: jnp.stack([_chips_from_full(k, a) for a in gfull[k]], axis=1) for k in BIG}
    send = jnp.concatenate([by_chip[k].astype(BF16).reshape(4, -1, 1024) for k in BIG], axis=1)
    recv = scatter_chips("scatter_grads", send)
    mine = sum_blocks("sum_chips", recv)
    other = sibling_swap("swap_cores", mine)
    is_south = lax.axis_index("c") == 0
    south = jnp.where(is_south, mine, other)
    north = jnp.where(is_south, other, mine)
    gbig = _unpack_big(add_ordered("sum_cores", south, north), ())

    gs = small_grads(grads)
    small_all = gather_all("gather_small", _pack_small(gs))
    small_sum = sum_blocks("sum_small", small_all)
    full_shapes = {k: (w[k].shape if k != "ssm_conv_w" else (2, 5, 1536)) for k in SMALL}
    gsmall = _unpack_small(small_sum, full_shapes)
    chip = 2 * lax.axis_index("x") + lax.axis_index("y")
    gsmall["ssm_conv_w"] = lax.dynamic_slice_in_dim(gsmall["ssm_conv_w"], chip * 384, 384, axis=2)

    grad = {**gbig, **gsmall}
    delta, new_m, new_v = {}, {}, {}
    sm = {k: _pack_small_local(d) for k, d in (("w", w), ("g", grad), ("m", m), ("v", v))}
    r = adamw("adamw_small", sm["w"], sm["g"], sm["m"], sm["v"])
    shapes = {k: w[k].shape for k in SMALL}
    for dst, buf in zip((delta, new_m, new_v), r):
        dst.update(_unpack_small_local(buf, shapes))
    for k in BIG:
        sh = w[k].shape
        r = adamw("adamw_" + k, *[a[k].reshape(sh[0] * sh[1], sh[2]) for a in (w, grad, m, v)])
        for dst, buf in zip((delta, new_m, new_v), r):
            dst[k] = buf.reshape(sh)

    loss = lax.psum(loss_loc, ("x", "y", "c"))
    return (loss, dx[None, :nlat], *[grad[k] for k in WEIGHTS], *[delta[k] for k in WEIGHTS],
            *[new_m[k] for k in WEIGHTS], *[new_v[k] for k in WEIGHTS])


def _pack_small_local(d):
    parts = []
    for k in SMALL:
        a = d[k].astype(F32).reshape(-1)
        parts.append(jnp.pad(a, (0, (-a.shape[0]) % 1024)))
    return jnp.concatenate(parts).reshape(-1, 128)


def _unpack_small_local(buf, shapes):
    flat = buf.reshape(-1)
    out = {}
    o = 0
    for k in SMALL:
        sz = math.prod(shapes[k])
        out[k] = flat[o:o + sz].reshape(shapes[k])
        o += sz + (-sz) % 1024
    return out
```

```python
import functools
import math

import jax
import jax.numpy as jnp
from jax import lax
from jax.experimental import pallas as pl
from jax.experimental.pallas import tpu as pltpu

F32 = jnp.float32
BF16 = jnp.bfloat16
MESH = pl.DeviceIdType.MESH

D = 1024
NCTX = 256
EPS = 1e-6
ROPE_BASE = 10000.0
GRID_W = 64
DEPTH = 2
Q = 128
N_HEADS_SSM = 16
SWA_HQ, SWA_HKV, SWA_DH, SWA_WIN = 8, 2, 128, 128
MLA_H, MLA_NOPE, MLA_ROPE, MLA_V = 8, 128, 64, 128
MLA_QRANK, MLA_KVRANK = 384, 256
FFN = 2816
RT = 256
VMEM_LIMIT = 56 << 20
NEG = -1e30
LOG2E = 1.4426950408889634

C_G1, C_G2, C_G3, C_Z, C_Q, C_XS, C_B, C_C, C_K, C_V, C_CKV, C_MISC, C_PAD, C_CQ = (
    0, 1024, 2048, 3072, 4096, 5120, 6144, 6400, 6656, 6912, 7168, 7424, 7552, 7680)
UW = 8064
DT_LANE = 64

ADAM_LR, ADAM_B1, ADAM_B2, ADAM_EPS, ADAM_WD, ADAM_STEP = 0.001, 0.9, 0.999, 1e-08, 0.01, 10


def _cp(sem):
    return pltpu.CompilerParams(dimension_semantics=sem, vmem_limit_bytes=VMEM_LIMIT)


def _pick(n, cands):
    for c in cands:
        if n % c == 0:
            return c
    return n


_TN = (1536, 1408, 1152, 1024, 896, 768, 512, 384, 256, 128)


def mm(a, b, out_dtype, name):
    m, k = a.shape
    _, n = b.shape
    tm = _pick(m, (768, 512, 256, 128, 8))
    tn = _pick(n, _TN)
    tk = k if k <= 2048 else _pick(k, (1408, 1152, 1024, 896, 768, 512))
    nk = k // tk

    def body(a_ref, b_ref, o_ref, *acc):
        p = jnp.dot(a_ref[...].astype(BF16), b_ref[...].astype(BF16), preferred_element_type=F32)
        if nk == 1:
            o_ref[...] = p.astype(out_dtype)
        else:
            kk = pl.program_id(2)

            @pl.when(kk == 0)
            def _():
                acc[0][...] = p

            @pl.when(kk > 0)
            def _():
                acc[0][...] += p

            @pl.when(kk == nk - 1)
            def _():
                o_ref[...] = acc[0][...].astype(out_dtype)

    return pl.pallas_call(
        body, out_shape=jax.ShapeDtypeStruct((m, n), out_dtype), grid=(m // tm, n // tn, nk),
        in_specs=[pl.BlockSpec((tm, tk), lambda i, j, kk: (i, kk)), pl.BlockSpec((tk, tn), lambda i, j, kk: (kk, j))],
        out_specs=pl.BlockSpec((tm, tn), lambda i, j, kk: (i, j)),
        scratch_shapes=[] if nk == 1 else [pltpu.VMEM((tm, tn), F32)],
        name=name, compiler_params=_cp(("parallel", "parallel", "arbitrary")))(a, b)


def mm_tn(a, b, name):
    t, ka = a.shape
    _, nb = b.shape
    ta = _pick(ka, (1024, 1408, 768, 512, 384, 256, 128))
    tb = _pick(nb, _TN)
    tt = _pick(t, (768, 512, 256, 128, 8))
    nt = t // tt

    def body(a_ref, b_ref, o_ref):
        p = lax.dot_general(a_ref[...].astype(BF16), b_ref[...].astype(BF16), (((0,), (0,)), ((), ())),
                            preferred_element_type=F32)
        s = pl.program_id(2)

        @pl.when(s == 0)
        def _():
            o_ref[...] = p

        @pl.when(s > 0)
        def _():
            o_ref[...] += p

    return pl.pallas_call(
        body, out_shape=jax.ShapeDtypeStruct((ka, nb), F32), grid=(ka // ta, nb // tb, nt),
        in_specs=[pl.BlockSpec((tt, ta), lambda i, j, s: (s, i)), pl.BlockSpec((tt, tb), lambda i, j, s: (s, j))],
        out_specs=pl.BlockSpec((ta, tb), lambda i, j, s: (i, j)),
        name=name, compiler_params=_cp(("parallel", "parallel", "arbitrary")))(a, b)


def _rms(x, g, n=None):
    n = x.shape[-1] if n is None else n
    r = lax.rsqrt(jnp.sum(x * x, axis=-1, keepdims=True) * (1.0 / n) + EPS)
    return x * r * g


def _silu(x):
    return x * jax.nn.sigmoid(x)


def _modulate(x, g, sc, sh):
    return _rms(x, g) * (1.0 + sc) + sh


def _swap(x, s):
    ax = x.ndim - 1
    w = x.shape[ax]
    lane = lax.broadcasted_iota(jnp.int32, x.shape, ax)
    lo = (lane & s) == 0
    return jnp.where(lo, pltpu.roll(x, w - s, ax), pltpu.roll(x, s, ax))


@functools.partial(jax.custom_vjp, nondiff_argnums=(3,))
def _rope(x, cos, sin, s):
    return x * cos + _swap(x, s) * sin


def _rope_fwd(x, cos, sin, s):
    return _rope(x, cos, sin, s), (cos, sin)


def _rope_bwd(s, res, g):
    cos, sin = res
    return g * cos - _swap(g, s) * sin, jnp.zeros_like(cos), jnp.zeros_like(sin)


_rope.defvjp(_rope_fwd, _rope_bwd)


@jax.custom_vjp
def _softplus(x):
    return jnp.maximum(x, 0.0) + jnp.log(1.0 + jnp.exp(-jnp.abs(x)))


def _softplus_fwd(x):
    return _softplus(x), x


def _softplus_bwd(x, g):
    return (g * jax.nn.sigmoid(x),)


_softplus.defvjp(_softplus_fwd, _softplus_bwd)


def _d(a, b, dims):
    return lax.dot_general(a.astype(BF16), b.astype(BF16), (dims, ((), ())), preferred_element_type=F32)


@jax.custom_vjp
def bdot(a, b):
    return _d(a, b, ((1,), (0,)))


bdot.defvjp(lambda a, b: (bdot(a, b), (a, b)),
            lambda r, g: (_d(g, r[1], ((1,), (1,))), _d(r[0], g, ((0,), (0,)))))


@jax.custom_vjp
def bdot_nt(a, b):
    return _d(a, b, ((1,), (1,)))


bdot_nt.defvjp(lambda a, b: (bdot_nt(a, b), (a, b)),
               lambda r, g: (_d(g, r[1], ((1,), (0,))), _d(g, r[0], ((0,), (0,)))))


@jax.custom_vjp
def bdot_tn(a, b):
    return _d(a, b, ((0,), (0,)))


bdot_tn.defvjp(lambda a, b: (bdot_tn(a, b), (a, b)),
               lambda r, g: (_d(r[1], g, ((1,), (1,))), _d(r[0], g, ((1,), (0,)))))


def _tri(rev):
    i = lax.broadcasted_iota(jnp.int32, (Q, Q), 0)
    j = lax.broadcasted_iota(jnp.int32, (Q, Q), 1)
    return (i <= j) if rev else (i >= j)


def _split3(a):
    hi = a.astype(BF16)
    r = a - hi.astype(F32)
    mid = r.astype(BF16)
    lo = (r - mid.astype(F32)).astype(BF16)
    return hi, mid, lo


def _cum_cols_impl(a, rev):
    t = _tri(rev).astype(BF16)
    return sum(jnp.dot(t, p, preferred_element_type=F32) for p in _split3(a))


def _cum_rows_impl(a, rev):
    t = _tri(not rev).astype(BF16)
    return sum(jnp.dot(p, t, preferred_element_type=F32) for p in _split3(a))


@functools.partial(jax.custom_vjp, nondiff_argnums=(1,))
def cum_cols(a, rev):
    return _cum_cols_impl(a, rev)


cum_cols.defvjp(lambda a, rev: (_cum_cols_impl(a, rev), None), lambda rev, _, g: (_cum_cols_impl(g, not rev),))


@functools.partial(jax.custom_vjp, nondiff_argnums=(1,))
def cum_rows(a, rev):
    return _cum_rows_impl(a, rev)


cum_rows.defvjp(lambda a, rev: (_cum_rows_impl(a, rev), None), lambda rev, _, g: (_cum_rows_impl(g, not rev),))


def _rs(w, cb=0):
    return pl.BlockSpec((RT, w), lambda i: (i, cb))


def _ps(shape):
    nd = len(shape)
    return pl.BlockSpec(shape, lambda i: (0,) * nd)


def _gs(w, cb, nlat):
    return pl.BlockSpec((1, 1, w), lambda i: (i // nlat, 0, cb))


def _rowcall(name, body, n, ins, outs, scratch=()):
    return pl.pallas_call(
        body, out_shape=[o[0] for o in outs], grid=(n // RT,), in_specs=[s for _, s in ins],
        out_specs=[s for _, s in outs], scratch_shapes=list(scratch), name=name,
        compiler_params=_cp(("arbitrary",)))(*[a for a, _ in ins])


def _acc(ref, val, first):
    @pl.when(first)
    def _():
        ref[...] = val

    @pl.when(jnp.logical_not(first))
    def _():
        ref[...] += val


def _sd(shape, dt):
    return jax.ShapeDtypeStruct(shape, dt)


def resid_mod_fwd(name, xp, o, mod_gt, gt_i, mod_n, sh_i, sc_i, norm_g, nlat):
    n = xp.shape[0]
    has_res = o is not None

    def body(*refs):
        if has_res:
            xp_ref, o_ref, gt_ref, sh_ref, sc_ref, g_ref, xn_ref, h_ref = refs
            xn = xp_ref[...] + gt_ref[0] * o_ref[...]
            xn_ref[...] = xn
        else:
            xp_ref, sh_ref, sc_ref, g_ref, h_ref = refs
            xn = xp_ref[...]
        h_ref[...] = _modulate(xn, g_ref[...], sc_ref[0], sh_ref[0]).astype(BF16)

    ins = [(xp, _rs(D))]
    if has_res:
        ins += [(o, _rs(D)), (mod_gt, _gs(D, gt_i, nlat))]
    ins += [(mod_n, _gs(D, sh_i, nlat)), (mod_n, _gs(D, sc_i, nlat)), (norm_g, _ps((1, D)))]
    outs = ([(_sd((n, D), F32), _rs(D))] if has_res else []) + [(_sd((n, D), BF16), _rs(D))]
    r = _rowcall(name, body, n, ins, outs)
    return (r[0], r[1]) if has_res else (xp, r[0])


def resid_mod_bwd(name, xn, dxn, dh, o, mod_gt, gt_i, mod_n, sh_i, sc_i, norm_g, nlat):
    n = xn.shape[0]
    has_res = o is not None

    def body(*refs):
        i = pl.program_id(0)
        if has_res:
            (xn_ref, dxn_ref, dh_ref, o_ref, gt_ref, sh_ref, sc_ref, g_ref,
             dx_ref, do_ref, dgt_ref, dsh_ref, dsc_ref, dg_ref) = refs
        else:
            xn_ref, dxn_ref, dh_ref, sh_ref, sc_ref, g_ref, dx_ref, dsh_ref, dsc_ref, dg_ref = refs
        _, vjp = jax.vjp(_modulate, xn_ref[...], g_ref[...], sc_ref[0], sh_ref[0])
        dx, dg, dsc, dsh = vjp(dh_ref[...])
        dx = dx + dxn_ref[...]
        dx_ref[...] = dx
        gfirst = (i == 0) | (i == nlat)
        _acc(dg_ref, dg, i == 0)
        _acc(dsh_ref, dsh[None], gfirst)
        _acc(dsc_ref, dsc[None], gfirst)
        if has_res:
            do_ref[...] = (gt_ref[0] * dx).astype(BF16)
            _acc(dgt_ref, jnp.sum(dx * o_ref[...], axis=0, keepdims=True)[None], gfirst)

    ins = [(xn, _rs(D)), (dxn, _rs(D)), (dh, _rs(D))]
    if has_res:
        ins += [(o, _rs(D)), (mod_gt, _gs(D, gt_i, nlat))]
    ins += [(mod_n, _gs(D, sh_i, nlat)), (mod_n, _gs(D, sc_i, nlat)), (norm_g, _ps((1, D)))]
    gacc = (_sd((2, 1, D), F32), _gs(D, 0, nlat))
    outs = [(_sd((n, D), F32), _rs(D))]
    if has_res:
        outs += [(_sd((n, D), BF16), _rs(D)), gacc]
    outs += [gacc, gacc, (_sd((1, D), F32), _ps((1, D)))]
    r = _rowcall(name, body, n, ins, outs)
    if has_res:
        return r
    return r[0], None, None, r[1], r[2], r[3]


def resid_loss(name, xp, o, mod_gt, gt_i, target, nlat):
    n = xp.shape[0]

    def body(xp_ref, o_ref, gt_ref, t_ref, loss_ref, dx_ref, do_ref, dgt_ref):
        i = pl.program_id(0)
        gt = gt_ref[0]

        @pl.when(i < nlat)
        def _():
            err = xp_ref[...] + gt * o_ref[...] - t_ref[...]
            dx = err * (1.0 / D)
            dx_ref[...] = dx
            do_ref[...] = (gt * dx).astype(BF16)
            _acc(loss_ref, jnp.full((1, 128), 0.5 / D, F32) * jnp.sum(err * err), i == 0)
            _acc(dgt_ref, jnp.sum(dx * o_ref[...], axis=0, keepdims=True)[None], i == 0)

        @pl.when(i >= nlat)
        def _():
            dx_ref[...] = jnp.zeros((RT, D), F32)
            do_ref[...] = jnp.zeros((RT, D), BF16)
            dgt_ref[...] = jnp.zeros((1, 1, D), F32)

    tgt_spec = pl.BlockSpec((RT, D), lambda i: (jnp.minimum(i, nlat - 1), 0))
    ins = [(xp, _rs(D)), (o, _rs(D)), (mod_gt, _gs(D, gt_i, nlat)), (target, tgt_spec)]
    outs = [(_sd((1, 128), F32), _ps((1, 128))), (_sd((n, D), F32), _rs(D)), (_sd((n, D), BF16), _rs(D)),
            (_sd((2, 1, D), F32), _gs(D, 0, nlat))]
    return _rowcall(name, body, n, ins, outs)


def mod_fwd(name, c8, w_mod, b_mod):
    tn = 1536

    def body(c_ref, w_ref, b_ref, o_ref, s_ref):
        s = _silu(c_ref[...]).astype(BF16)
        s_ref[...] = s
        o_ref[...] = jnp.dot(s, w_ref[...], preferred_element_type=F32) + b_ref[...]

    return pl.pallas_call(
        body, out_shape=[_sd((8, 6 * D), F32), _sd((8, D), BF16)], grid=(6 * D // tn,),
        in_specs=[pl.BlockSpec((8, D), lambda j: (0, 0)), pl.BlockSpec((D, tn), lambda j: (0, j)),
                  pl.BlockSpec((1, tn), lambda j: (0, j))],
        out_specs=[pl.BlockSpec((8, tn), lambda j: (0, j)), pl.BlockSpec((8, D), lambda j: (0, 0))],
        name=name, compiler_params=_cp(("arbitrary",)))(c8, w_mod, b_mod)


def mod_small_bwd(name, c8, dsilu, dmod8):
    def body(c_ref, ds_ref, dm_ref, dc_ref, db_ref):
        _, vjp = jax.vjp(_silu, c_ref[...])
        dc_ref[...] = vjp(ds_ref[...])[0]
        db_ref[...] = jnp.sum(dm_ref[...], axis=0, keepdims=True)

    return pl.pallas_call(
        body, out_shape=[_sd((8, D), F32), _sd((1, 6 * D), F32)], grid=(1,),
        in_specs=[pl.BlockSpec((8, D), lambda j: (0, 0)), pl.BlockSpec((8, D), lambda j: (0, 0)),
                  pl.BlockSpec((8, 6 * D), lambda j: (0, 0))],
        out_specs=[pl.BlockSpec((8, D), lambda j: (0, 0)), pl.BlockSpec((1, 6 * D), lambda j: (0, 0))],
        name=name, compiler_params=_cp(("arbitrary",)))(c8, dsilu, dmod8)


def _conv_taps(x, nlat):
    n = x.shape[0]
    r = lax.broadcasted_iota(jnp.int32, x.shape, 0)
    lo = jnp.where(r < nlat, 0, nlat)
    hi = jnp.where(r < nlat, nlat, n)
    taps = []
    for o in (-2, -1, 0, 1, 2):
        xs = x if o == 0 else pltpu.roll(x, (-o) % n, 0)
        t = r + o
        taps.append(jnp.where((t >= lo) & (t < hi), xs, 0.0))
    return taps


def conv_fwd(name, u, w, b, nlat_rows):
    n = u.shape[0]

    def body(x_ref, w_ref, b_ref, o_ref):
        taps = _conv_taps(x_ref[...], nlat_rows)
        wv = w_ref[...]
        pre = b_ref[...] + sum(taps[k] * wv[k:k + 1, :] for k in range(5))
        o_ref[...] = _silu(pre)

    return pl.pallas_call(
        body, out_shape=_sd((n, 1536), F32), grid=(12,),
        in_specs=[pl.BlockSpec((n, 128), lambda j: (0, C_XS // 128 + j)), pl.BlockSpec((5, 128), lambda j: (0, j)),
                  pl.BlockSpec((1, 128), lambda j: (0, j))],
        out_specs=pl.BlockSpec((n, 128), lambda j: (0, j)),
        name=name, compiler_params=_cp(("parallel",)))(u, w, b)


def conv_bwd(name, u, dact, w, b, nlat_rows):
    n = u.shape[0]

    def body(x_ref, da_ref, w_ref, b_ref, dx_ref, dw_ref, db_ref):
        taps = _conv_taps(x_ref[...], nlat_rows)
        wv = w_ref[...]
        pre = b_ref[...] + sum(taps[k] * wv[k:k + 1, :] for k in range(5))
        s = jax.nn.sigmoid(pre)
        dpre = da_ref[...] * (s * (1.0 + pre * (1.0 - s)))
        db_ref[...] = jnp.sum(dpre, axis=0, keepdims=True)
        rows = lax.broadcasted_iota(jnp.int32, (5, 128), 0)
        dw = jnp.zeros((5, 128), F32)
        for k in range(5):
            dw = dw + jnp.where(rows == k, jnp.sum(dpre * taps[k], axis=0, keepdims=True), 0.0)
        dw_ref[...] = dw
        r = lax.broadcasted_iota(jnp.int32, dpre.shape, 0)
        lo = jnp.where(r < nlat_rows, 0, nlat_rows)
        hi = jnp.where(r < nlat_rows, nlat_rows, n)
        dx = jnp.zeros_like(dpre)
        for k in range(5):
            o = k - 2
            ds = dpre if o == 0 else pltpu.roll(dpre, o % n, 0)
            t = r - o
            dx = dx + jnp.where((t >= lo) & (t < hi), ds, 0.0) * wv[k:k + 1, :]
        dx_ref[...] = dx.astype(BF16)

    return pl.pallas_call(
        body, out_shape=[_sd((n, 1536), BF16), _sd((5, 1536), F32), _sd((1, 1536), F32)], grid=(12,),
        in_specs=[pl.BlockSpec((n, 128), lambda j: (0, C_XS // 128 + j)), pl.BlockSpec((n, 128), lambda j: (0, j)),
                  pl.BlockSpec((5, 128), lambda j: (0, j)), pl.BlockSpec((1, 128), lambda j: (0, j))],
        out_specs=[pl.BlockSpec((n, 128), lambda j: (0, j)), pl.BlockSpec((5, 128), lambda j: (0, j)),
                   pl.BlockSpec((1, 128), lambda j: (0, j))],
        name=name, compiler_params=_cp(("parallel",)))(u, dact, w, b)


def _ssd_chunk(rev, dirn, g, x4, bm, cm, misc, dtrow, bias_c, alog_c, bias_r, alog_r, h4):
    dt_c = _softplus(misc + bias_c)
    a_c = dt_c * (-jnp.exp(alog_c))
    dt_r = _softplus(dtrow + bias_r)
    a_r = dt_r * (-jnp.exp(alog_r))
    cs_c = cum_cols(a_c, rev)
    cs_r = cum_rows(a_r, rev)
    tot_c = jnp.sum(a_c, axis=0, keepdims=True)
    cb = bdot_nt(cm, bm)
    tri = _tri(rev)
    lane = lax.broadcasted_iota(jnp.int32, (1, 128), 1)
    row16 = lax.broadcasted_iota(jnp.int32, (16, 1), 0)
    prow = lax.broadcasted_iota(jnp.int32, (128, 1), 0)
    ys, hs = [], []
    for p in range(4):
        ydiag = 0.0
        wst = 0.0
        eoff = 0.0
        hscale = 0.0
        for e in range(2):
            hg = 8 * g + 2 * p + e
            oh_c = (lane == DT_LANE + 16 * dirn + hg).astype(F32)
            dt_h = jnp.sum(dt_c * oh_c, axis=1, keepdims=True)
            cs_h = jnp.sum(cs_c * oh_c, axis=1, keepdims=True)
            tot_h = jnp.sum(tot_c * oh_c, axis=1, keepdims=True)
            csr_h = jnp.sum(cs_r * (row16 == hg).astype(F32), axis=0, keepdims=True)
            seg = jnp.exp(jnp.where(tri, cs_h - csr_h, -jnp.inf))
            hm = ((lane < 64) if e == 0 else (lane >= 64)).astype(F32)
            ydiag = ydiag + bdot(cb * seg, x4[p] * (dt_h * hm))
            wst = wst + (dt_h * jnp.exp(tot_h - cs_h)) * hm
            eoff = eoff + jnp.exp(cs_h) * hm
            hscale = hscale + jnp.exp(tot_h) * ((prow < 64) if e == 0 else (prow >= 64)).astype(F32)
        ys.append(ydiag + bdot_nt(cm, h4[p]) * eoff)
        hs.append(h4[p] * hscale + bdot_tn(x4[p] * wst, bm))
    return ys, hs


def _ssd_specs(nlat_chunks, rev, dirn, bwd):
    nc = nlat_chunks + 2

    def chunk(s):
        if bwd:
            s = nc - 1 - s
        return (nlat_chunks + 1 - s) if rev else (s + nlat_chunks) % nc

    def step(s):
        return (nc - 1 - s) if bwd else s

    return dict(
        x=pl.BlockSpec((Q, 512), lambda g, s: (chunk(s), g)),
        b=pl.BlockSpec((Q, 128), lambda g, s: (chunk(s), 8 + g)),
        c=pl.BlockSpec((Q, 128), lambda g, s: (chunk(s), 10 + g)),
        misc=pl.BlockSpec((Q, 128), lambda g, s: (chunk(s), C_MISC // 128)),
        dtrow=pl.BlockSpec((16, Q), lambda g, s: (dirn, chunk(s))),
        p_c=pl.BlockSpec((1, 128), lambda g, s: (0, 0)),
        p_r=pl.BlockSpec((16, 1), lambda g, s: (dirn, 0)),
        y=pl.BlockSpec((Q, 512), lambda g, s: (chunk(s), g)),
        hsave=pl.BlockSpec((1, 1, 512, 128), lambda g, s: (g, step(s), 0, 0)),
        bc_out=pl.BlockSpec((Q, 128), lambda g, s: (chunk(s), g)),
        misc_out=pl.BlockSpec((1, Q, 128), lambda g, s: (g, chunk(s), 0)),
        dtrow_out=pl.BlockSpec((1, 16, Q), lambda g, s: (g, 0, chunk(s))),
        pacc_c=pl.BlockSpec((1, 128), lambda g, s: (0, 0)),
        pacc_r=pl.BlockSpec((16, 1), lambda g, s: (0, 0)),
    )


def ssd_fwd(name, xbc, u, dtrow, bias_c, alog_c, bias_r, alog_r, nlat_chunks, rev, dirn):
    n = xbc.shape[0]
    nc = nlat_chunks + 2
    sp = _ssd_specs(nlat_chunks, rev, dirn, False)

    def body(x_ref, b_ref, c_ref, m_ref, r_ref, bc_ref, ac_ref, br_ref, ar_ref, y_ref, hs_ref, h_s):
        g = pl.program_id(0)
        s = pl.program_id(1)

        @pl.when(s == 0)
        def _():
            h_s[...] = jnp.zeros((512, 128), F32)

        hs_ref[0, 0] = h_s[...]
        x4 = [x_ref[:, 128 * p:128 * p + 128] for p in range(4)]
        h4 = [h_s[128 * p:128 * p + 128, :] for p in range(4)]
        ys, hs = _ssd_chunk(rev, dirn, g, x4, b_ref[...], c_ref[...], m_ref[...], r_ref[...],
                            bc_ref[...], ac_ref[...], br_ref[...], ar_ref[...], h4)
        for p in range(4):
            y_ref[:, 128 * p:128 * p + 128] = ys[p]
            h_s[128 * p:128 * p + 128, :] = hs[p]

    return pl.pallas_call(
        body, out_shape=[_sd((n, 1024), F32), _sd((2, nc, 512, 128), F32)], grid=(2, nc),
        in_specs=[sp["x"], sp["b"], sp["c"], sp["misc"], sp["dtrow"], sp["p_c"], sp["p_c"], sp["p_r"], sp["p_r"]],
        out_specs=[sp["y"], sp["hsave"]], scratch_shapes=[pltpu.VMEM((512, 128), F32)],
        name=name, compiler_params=_cp(("arbitrary", "arbitrary")))(
            xbc, xbc, xbc, u, dtrow, bias_c, alog_c, bias_r, alog_r)


def ssd_bwd(name, xbc, u, dtrow, bias_c, alog_c, bias_r, alog_r, hsave, dy, acc, nlat_chunks, rev, dirn):
    n = xbc.shape[0]
    sp = _ssd_specs(nlat_chunks, rev, dirn, True)

    def body(x_ref, b_ref, c_ref, m_ref, r_ref, bc_ref, ac_ref, br_ref, ar_ref, hs_ref, dy_ref, ax_ref, ab_ref, acc_ref,
             dx_ref, db_ref, dc_ref, dm_ref, dr_ref, dbc_ref, dac_ref, dbr_ref, dar_ref, dh_s):
        g = pl.program_id(0)
        s = pl.program_id(1)

        @pl.when(s == 0)
        def _():
            dh_s[...] = jnp.zeros((512, 128), F32)

        x4 = [x_ref[:, 128 * p:128 * p + 128] for p in range(4)]
        h4 = [hs_ref[0, 0, 128 * p:128 * p + 128, :] for p in range(4)]
        fn = functools.partial(_ssd_chunk, rev, dirn, g)
        _, vjp = jax.vjp(fn, x4, b_ref[...], c_ref[...], m_ref[...], r_ref[...],
                         bc_ref[...], ac_ref[...], br_ref[...], ar_ref[...], h4)
        dys = [dy_ref[:, 128 * p:128 * p + 128] for p in range(4)]
        dhs = [dh_s[128 * p:128 * p + 128, :] for p in range(4)]
        dx4, db, dc, dm, dr, dbc, dac, dbr, dar, dh4 = vjp((dys, dhs))
        for p in range(4):
            dx_ref[:, 128 * p:128 * p + 128] = dx4[p] + ax_ref[:, 128 * p:128 * p + 128]
            dh_s[128 * p:128 * p + 128, :] = dh4[p]
        db_ref[...] = db + ab_ref[...]
        dc_ref[...] = dc + acc_ref[...]
        dm_ref[0] = dm
        dr_ref[0] = dr
        first = (g == 0) & (s == 0)
        _acc(dbc_ref, dbc, first)
        _acc(dac_ref, dac, first)
        _acc(dbr_ref, dbr, first)
        _acc(dar_ref, dar, first)

    ax, ab, ac = acc
    return pl.pallas_call(
        body,
        out_shape=[_sd((n, 1024), F32), _sd((n, 256), F32), _sd((n, 256), F32), _sd((2, n, 128), F32),
                   _sd((2, 16, n), F32), _sd((1, 128), F32), _sd((1, 128), F32), _sd((16, 1), F32), _sd((16, 1), F32)],
        grid=(2, nlat_chunks + 2),
        in_specs=[sp["x"], sp["b"], sp["c"], sp["misc"], sp["dtrow"], sp["p_c"], sp["p_c"], sp["p_r"], sp["p_r"],
                  sp["hsave"], sp["y"], sp["y"], sp["bc_out"], sp["bc_out"]],
        out_specs=[sp["y"], sp["bc_out"], sp["bc_out"], sp["misc_out"], sp["dtrow_out"],
                   sp["pacc_c"], sp["pacc_c"], sp["pacc_r"], sp["pacc_r"]],
        scratch_shapes=[pltpu.VMEM((512, 128), F32)],
        name=name, compiler_params=_cp(("arbitrary", "arbitrary")))(
            xbc, xbc, xbc, u, dtrow, bias_c, alog_c, bias_r, alog_r, hsave, dy, ax, ab, ac)


def _ssd_out(yf, yb, xs, z, g, dexp):
    return _rms((yf + yb + dexp * xs) * _silu(z), g)


def ssd_out_fwd(name, yf, yb, xbc, u, g, dexp):
    n = yf.shape[0]

    def body(yf_ref, yb_ref, xs_ref, z_ref, g_ref, d_ref, o_ref):
        o_ref[...] = _ssd_out(yf_ref[...], yb_ref[...], xs_ref[...], z_ref[...], g_ref[...], d_ref[...]).astype(BF16)

    return _rowcall(name, body, n,
                    [(yf, _rs(D)), (yb, _rs(D)), (xbc, _rs(D, 0)), (u, _rs(D, C_Z // D)), (g, _ps((1, D))), (dexp, _ps((1, D)))],
                    [(_sd((n, D), BF16), _rs(D))])[0]


def ssd_out_bwd(name, yf, yb, xbc, u, g, dexp, dys):
    n = yf.shape[0]

    def body(yf_ref, yb_ref, xs_ref, z_ref, g_ref, d_ref, dys_ref, dy_ref, dxs_ref, dz_ref, dg_ref, dd_ref):
        i = pl.program_id(0)
        _, vjp = jax.vjp(_ssd_out, yf_ref[...], yb_ref[...], xs_ref[...], z_ref[...], g_ref[...], d_ref[...])
        dyf, _, dxs, dz, dg, dd = vjp(dys_ref[...])
        dy_ref[...] = dyf
        dxs_ref[...] = dxs
        dz_ref[...] = dz.astype(BF16)
        _acc(dg_ref, dg, i == 0)
        _acc(dd_ref, dd, i == 0)

    return _rowcall(name, body, n,
                    [(yf, _rs(D)), (yb, _rs(D)), (xbc, _rs(D, 0)), (u, _rs(D, C_Z // D)), (g, _ps((1, D))), (dexp, _ps((1, D))),
                     (dys, _rs(D))],
                    [(_sd((n, D), F32), _rs(D)), (_sd((n, D), F32), _rs(D)), (_sd((n, D), BF16), _rs(D)),
                     (_sd((1, D), F32), _ps((1, D))), (_sd((1, D), F32), _ps((1, D)))])


def _normrope(x, g, cos, sin, s, n=None):
    return _rope(_rms(x, g, n), cos, sin, s)


def swa_prep_fwd(name, u, gq, gk, cos, sin):
    n = u.shape[0]

    def body(q_ref, k_ref, gq_ref, gk_ref, cos_ref, sin_ref, qs_ref, ks_ref):
        cs, sn = cos_ref[...], sin_ref[...]
        for h in range(SWA_HQ):
            sl = slice(128 * h, 128 * h + 128)
            qs_ref[:, sl] = _normrope(q_ref[:, sl], gq_ref[...], cs, sn, 32).astype(BF16)
        for h in range(SWA_HKV):
            sl = slice(128 * h, 128 * h + 128)
            ks_ref[:, sl] = _normrope(k_ref[:, sl], gk_ref[...], cs, sn, 32).astype(BF16)

    return _rowcall(name, body, n,
                    [(u, _rs(1024, C_Q // 1024)), (u, _rs(256, C_K // 256)), (gq, _ps((1, 128))), (gk, _ps((1, 128))),
                     (cos, _rs(128)), (sin, _rs(128))],
                    [(_sd((n, 1024), BF16), _rs(1024)), (_sd((n, 256), BF16), _rs(256))])


def swa_prep_bwd(name, u, gq, gk, cos, sin, dqs, dks, dv):
    n = u.shape[0]

    def body(q_ref, k_ref, gq_ref, gk_ref, cos_ref, sin_ref, dqs_ref, dks_ref, dv_ref,
             dq_ref, dk_ref, dvo_ref, dgq_ref, dgk_ref):
        i = pl.program_id(0)
        cs, sn = cos_ref[...], sin_ref[...]
        fn = lambda x, g: _normrope(x, g, cs, sn, 32)
        dgq = jnp.zeros((1, 128), F32)
        dgk = jnp.zeros((1, 128), F32)
        for h in range(SWA_HQ):
            sl = slice(128 * h, 128 * h + 128)
            _, vjp = jax.vjp(fn, q_ref[:, sl], gq_ref[...])
            dx, dg = vjp(dqs_ref[:, sl])
            dq_ref[:, sl] = dx.astype(BF16)
            dgq = dgq + dg
        for h in range(SWA_HKV):
            sl = slice(128 * h, 128 * h + 128)
            _, vjp = jax.vjp(fn, k_ref[:, sl], gk_ref[...])
            dx, dg = vjp(dks_ref[:, sl])
            dk_ref[:, sl] = dx.astype(BF16)
            dgk = dgk + dg
        dvo_ref[...] = dv_ref[...].astype(BF16)
        _acc(dgq_ref, dgq, i == 0)
        _acc(dgk_ref, dgk, i == 0)

    return _rowcall(name, body, n,
                    [(u, _rs(1024, C_Q // 1024)), (u, _rs(256, C_K // 256)), (gq, _ps((1, 128))), (gk, _ps((1, 128))),
                     (cos, _rs(128)), (sin, _rs(128)), (dqs, _rs(1024)), (dks, _rs(256)), (dv, _rs(256))],
                    [(_sd((n, 1024), BF16), _rs(1024)), (_sd((n, 256), BF16), _rs(256)), (_sd((n, 256), BF16), _rs(256)),
                     (_sd((1, 128), F32), _ps((1, 128))), (_sd((1, 128), F32), _ps((1, 128)))])


def lat_norm_fwd(name, u, g_kv, g_q):
    n = u.shape[0]

    def body(ckv_ref, cq_ref, gkv_ref, gq_ref, okv_ref, oq_ref):
        okv_ref[...] = _rms(ckv_ref[...], gkv_ref[...]).astype(BF16)
        oq_ref[...] = _rms(cq_ref[...], gq_ref[...]).astype(BF16)

    return _rowcall(name, body, n,
                    [(u, _rs(256, C_CKV // 256)), (u, _rs(384, C_CQ // 384)), (g_kv, _ps((1, 256))), (g_q, _ps((1, 384)))],
                    [(_sd((n, 256), BF16), _rs(256)), (_sd((n, 384), BF16), _rs(384))])


def lat_norm_bwd(name, u, g_kv, g_q, dkvn, dqn):
    n = u.shape[0]

    def body(ckv_ref, cq_ref, gkv_ref, gq_ref, dkvn_ref, dqn_ref, dckv_ref, dcq_ref, dgkv_ref, dgq_ref):
        i = pl.program_id(0)
        _, vjp = jax.vjp(_rms, ckv_ref[...], gkv_ref[...])
        dx, dg = vjp(dkvn_ref[...])
        dckv_ref[...] = dx.astype(BF16)
        _acc(dgkv_ref, dg, i == 0)
        _, vjp = jax.vjp(_rms, cq_ref[...], gq_ref[...])
        dx, dg = vjp(dqn_ref[...])
        dcq_ref[...] = dx.astype(BF16)
        _acc(dgq_ref, dg, i == 0)

    return _rowcall(name, body, n,
                    [(u, _rs(256, C_CKV // 256)), (u, _rs(384, C_CQ // 384)), (g_kv, _ps((1, 256))), (g_q, _ps((1, 384))),
                     (dkvn, _rs(256)), (dqn, _rs(384))],
                    [(_sd((n, 256), BF16), _rs(256)), (_sd((n, 384), BF16), _rs(384)),
                     (_sd((1, 256), F32), _ps((1, 256))), (_sd((1, 384), F32), _ps((1, 384)))])


def _lane_lt64(x):
    return (lax.broadcasted_iota(jnp.int32, (1, 128), 1) < 64).astype(F32) * x


def _mla_krope(misc, g, cos, sin):
    return _normrope(_lane_lt64(misc), g, cos, sin, 16, MLA_ROPE)


def mla_prep_fwd(name, kv, qp, u, qg, kg, cos, sin):
    n = kv.shape[0]

    def body(kv_ref, q_ref, m_ref, qg_ref, kg_ref, cos_ref, sin_ref, km_ref, qm_ref):
        cs, sn = cos_ref[...], sin_ref[...]
        kr = _mla_krope(m_ref[...], kg_ref[:, 128:256], cs, sn).astype(BF16)
        for h in range(MLA_H):
            km_ref[:, 256 * h:256 * h + 128] = _rms(kv_ref[:, 128 * h:128 * h + 128], kg_ref[:, 0:128]).astype(BF16)
            km_ref[:, 256 * h + 128:256 * h + 256] = kr
            qm_ref[:, 256 * h:256 * h + 128] = _rms(q_ref[:, 256 * h:256 * h + 128], qg_ref[:, 0:128]).astype(BF16)
            qm_ref[:, 256 * h + 128:256 * h + 256] = _normrope(
                q_ref[:, 256 * h + 128:256 * h + 256], qg_ref[:, 128:256], cs, sn, 16, MLA_ROPE).astype(BF16)

    return _rowcall(name, body, n,
                    [(kv, _rs(1024, 0)), (qp, _rs(2048)), (u, _rs(128, C_MISC // 128)), (qg, _ps((1, 256))), (kg, _ps((1, 256))),
                     (cos, _rs(128)), (sin, _rs(128))],
                    [(_sd((n, 2048), BF16), _rs(2048)), (_sd((n, 2048), BF16), _rs(2048))])


def mla_prep_bwd(name, kv, qp, u, qg, kg, cos, sin, dkm, dqm, dv):
    n = kv.shape[0]

    def body(kv_ref, q_ref, m_ref, qg_ref, kg_ref, cos_ref, sin_ref, dkm_ref, dqm_ref, dv_ref,
             dkv_ref, dq_ref, dkr_ref, dqg_ref, dkg_ref):
        i = pl.program_id(0)
        cs, sn = cos_ref[...], sin_ref[...]
        fr = lambda x, g: _normrope(x, g, cs, sn, 16, MLA_ROPE)
        dkg_n = jnp.zeros((1, 128), F32)
        dqg_n = jnp.zeros((1, 128), F32)
        dqg_r = jnp.zeros((1, 128), F32)
        dkr_sum = jnp.zeros((RT, 128), F32)
        for h in range(MLA_H):
            _, vjp = jax.vjp(_rms, kv_ref[:, 128 * h:128 * h + 128], kg_ref[:, 0:128])
            dx, dg = vjp(dkm_ref[:, 256 * h:256 * h + 128])
            dkv_ref[:, 128 * h:128 * h + 128] = dx.astype(BF16)
            dkg_n = dkg_n + dg
            dkr_sum = dkr_sum + dkm_ref[:, 256 * h + 128:256 * h + 256]
            _, vjp = jax.vjp(_rms, q_ref[:, 256 * h:256 * h + 128], qg_ref[:, 0:128])
            dx, dg = vjp(dqm_ref[:, 256 * h:256 * h + 128])
            dq_ref[:, 256 * h:256 * h + 128] = dx.astype(BF16)
            dqg_n = dqg_n + dg
            _, vjp = jax.vjp(fr, q_ref[:, 256 * h + 128:256 * h + 256], qg_ref[:, 128:256])
            dx, dg = vjp(dqm_ref[:, 256 * h + 128:256 * h + 256])
            dq_ref[:, 256 * h + 128:256 * h + 256] = dx.astype(BF16)
            dqg_r = dqg_r + dg
        _, vjp = jax.vjp(lambda m, g: _mla_krope(m, g, cs, sn), m_ref[...], kg_ref[:, 128:256])
        dm, dkg_r = vjp(dkr_sum)
        dkr_ref[...] = dm
        dkv_ref[:, 1024:2048] = dv_ref[...].astype(BF16)
        _acc(dqg_ref.at[:, 0:128], dqg_n, i == 0)
        _acc(dqg_ref.at[:, 128:256], dqg_r, i == 0)
        _acc(dkg_ref.at[:, 0:128], dkg_n, i == 0)
        _acc(dkg_ref.at[:, 128:256], dkg_r, i == 0)

    return _rowcall(name, body, n,
                    [(kv, _rs(1024, 0)), (qp, _rs(2048)), (u, _rs(128, C_MISC // 128)), (qg, _ps((1, 256))), (kg, _ps((1, 256))),
                     (cos, _rs(128)), (sin, _rs(128)), (dkm, _rs(2048)), (dqm, _rs(2048)), (dv, _rs(1024))],
                    [(_sd((n, 2048), BF16), _rs(2048)), (_sd((n, 2048), BF16), _rs(2048)), (_sd((n, 128), F32), _rs(128)),
                     (_sd((1, 256), F32), _ps((1, 256))), (_sd((1, 256), F32), _ps((1, 256)))])


def misc_combine(name, dkr, dm_f, dm_b, drow_t):
    n = dkr.shape[0]

    def body(a_ref, f_ref, b_ref, r_ref, o_ref):
        o_ref[...] = (a_ref[...] + f_ref[0] + f_ref[1] + b_ref[0] + b_ref[1] + r_ref[...]).astype(BF16)

    g2 = pl.BlockSpec((2, RT, 128), lambda i: (0, i, 0))
    return _rowcall(name, body, n, [(dkr, _rs(128)), (dm_f, g2), (dm_b, g2), (drow_t, _rs(128))],
                    [(_sd((n, 128), BF16), _rs(128))])[0]


def _merge(g1, g2, g3, p1, p2, p3):
    return jax.nn.sigmoid(g1) * p1 + jax.nn.sigmoid(g2) * p2 + jax.nn.sigmoid(g3) * p3


def merge_fwd(name, u, p1, p2, p3):
    n = u.shape[0]

    def body(g1, g2, g3, a, b, c, o_ref):
        o_ref[...] = _merge(g1[...], g2[...], g3[...], a[...], b[...], c[...]).astype(BF16)

    return _rowcall(name, body, n, [(u, _rs(D, 0)), (u, _rs(D, 1)), (u, _rs(D, 2)), (p1, _rs(D)), (p2, _rs(D)), (p3, _rs(D))],
                    [(_sd((n, D), BF16), _rs(D))])[0]


def merge_bwd(name, u, p1, p2, p3, dm):
    n = u.shape[0]

    def body(g1, g2, g3, a, b, c, dm_ref, d1, d2, d3, dg_ref):
        _, vjp = jax.vjp(_merge, g1[...], g2[...], g3[...], a[...], b[...], c[...])
        r = vjp(dm_ref[...])
        for k in range(3):
            dg_ref[:, D * k:D * k + D] = r[k].astype(BF16)
        d1[...] = r[3].astype(BF16)
        d2[...] = r[4].astype(BF16)
        d3[...] = r[5].astype(BF16)

    return _rowcall(name, body, n,
                    [(u, _rs(D, 0)), (u, _rs(D, 1)), (u, _rs(D, 2)), (p1, _rs(D)), (p2, _rs(D)), (p3, _rs(D)), (dm, _rs(D))],
                    [(_sd((n, D), BF16), _rs(D))] * 3 + [(_sd((n, 3 * D), BF16), _rs(3 * D))])


def _swiglu(g, u):
    return _silu(g) * u


def swiglu_fwd(name, gu):
    n = gu.shape[0]

    def body(g_ref, u_ref, o_ref):
        o_ref[...] = _swiglu(g_ref[...], u_ref[...]).astype(BF16)

    return _rowcall(name, body, n, [(gu, _rs(FFN, 0)), (gu, _rs(FFN, 1))], [(_sd((n, FFN), BF16), _rs(FFN))])[0]


def swiglu_bwd(name, gu, da):
    n = gu.shape[0]

    def body(g_ref, u_ref, da_ref, o_ref):
        _, vjp = jax.vjp(_swiglu, g_ref[...], u_ref[...])
        dg, du = vjp(da_ref[...])
        o_ref[:, 0:FFN] = dg.astype(BF16)
        o_ref[:, FFN:2 * FFN] = du.astype(BF16)

    return _rowcall(name, body, n, [(gu, _rs(FFN, 0)), (gu, _rs(FFN, 1)), (da, _rs(FFN))],
                    [(_sd((n, 2 * FFN), BF16), _rs(2 * FFN))])[0]


def _band_mask(tq, tk, i, kb):
    qp = i * tq + lax.broadcasted_iota(jnp.int32, (tq, tk), 0)
    kp = kb * tk + lax.broadcasted_iota(jnp.int32, (tq, tk), 1)
    return jnp.abs(qp - kp) <= SWA_WIN


def flash_fwd(name, qa, ka, va, *, w, vw, hq, grp, vcol0, scale, nlat, tq, tk, band, sink, ctx_q, prev=None):
    n = qa.shape[0]
    cblk = nlat // NCTX
    band = band and not ctx_q
    if ctx_q:
        tq = tk = NCTX
        grid = (hq, 1, 1)
        qmap = lambda h, i, kk: (cblk, h)
        kmap = lambda h, i, kk: (cblk, h // grp)
        vmap = lambda h, i, kk: (cblk, vcol0 + h // grp)
        omap = lambda h, i, kk: (cblk, h)
        lmap = lambda h, i, kk: (h, cblk, 0)
    else:
        nb = nlat // tk
        nk = 3 if band else nb
        grid = (hq, nlat // tq, nk)
        kb_of = (lambda i, kk: jnp.clip(i + kk - 1, 0, nb - 1)) if band else (lambda i, kk: kk)
        qmap = lambda h, i, kk: (i, h)
        kmap = lambda h, i, kk: (kb_of(i, kk), h // grp)
        vmap = lambda h, i, kk: (kb_of(i, kk), vcol0 + h // grp)
        omap = lambda h, i, kk: (i, h)
        lmap = lambda h, i, kk: (h, i, 0)
    nk = grid[2]
    extra = not ctx_q
    has_sink = sink is not None

    def body(*refs):
        refs = list(refs)
        q_ref, k_ref, v_ref = refs[:3]
        pos = 3
        if extra:
            ke_ref, ve_ref = refs[pos:pos + 2]
            pos += 2
        if has_sink:
            s_ref = refs[pos]
            pos += 1
        if prev is not None:
            pos += 2
        o_ref, l_ref, m_s, l_s, a_s = refs[pos:pos + 5]
        i = pl.program_id(1)
        kk = pl.program_id(2)
        q = q_ref[...]

        def step(kblk, vblk, mask):
            s = _d(q, kblk, ((1,), (1,))) * (scale * LOG2E)
            if mask is not None:
                s = jnp.where(mask, s, NEG)
            m_new = jnp.maximum(m_s[...], jnp.max(s, axis=1, keepdims=True))
            alpha = jnp.exp2(m_s[...] - m_new)
            p = jnp.exp2(s - m_new)
            l_s[...] = alpha * l_s[...] + jnp.sum(p, axis=1, keepdims=True)
            a_s[...] = alpha * a_s[...] + _d(p, vblk, ((1,), (0,)))
            m_s[...] = m_new

        @pl.when(kk == 0)
        def _():
            if has_sink:
                sv = jnp.max(s_ref[0], axis=1, keepdims=True) * LOG2E
                m_s[...] = jnp.zeros((tq, 1), F32) + sv
                l_s[...] = jnp.ones((tq, 1), F32)
            else:
                m_s[...] = jnp.full((tq, 1), NEG, F32)
                l_s[...] = jnp.zeros((tq, 1), F32)
            a_s[...] = jnp.zeros((tq, vw), F32)
            if extra:
                step(ke_ref[...], ve_ref[...], None)

        if band:
            kb = i + kk - 1

            @pl.when((kb >= 0) & (kb < nlat // tk))
            def _():
                step(k_ref[...], v_ref[...], _band_mask(tq, tk, i, kb))
        else:
            step(k_ref[...], v_ref[...], None)

        @pl.when(kk == nk - 1)
        def _():
            o_ref[...] = (a_s[...] / l_s[...]).astype(BF16)
            l_ref[0] = m_s[...] + jnp.log2(l_s[...])

    ins = [(qa, pl.BlockSpec((tq, w), qmap)), (ka, pl.BlockSpec((tk, w), kmap)), (va, pl.BlockSpec((tk, vw), vmap))]
    if extra:
        ins += [(ka, pl.BlockSpec((NCTX, w), lambda h, i, kk: (cblk, h // grp))),
                (va, pl.BlockSpec((NCTX, vw), lambda h, i, kk: (cblk, vcol0 + h // grp)))]
    if has_sink:
        ins += [(sink, pl.BlockSpec((1, 1, 128), lambda h, i, kk: (h, 0, 0)))]
    aliases = {}
    if prev is not None:
        any_spec = pl.BlockSpec(memory_space=pl.ANY)
        aliases = {len(ins): 0, len(ins) + 1: 1}
        ins += [(prev[0], any_spec), (prev[1], any_spec)]
    return pl.pallas_call(
        body, out_shape=[_sd((n, hq * vw), BF16), _sd((hq, n, 1), F32)], grid=grid,
        in_specs=[s for _, s in ins],
        out_specs=[pl.BlockSpec((tq, vw), omap), pl.BlockSpec((1, tq, 1), lmap)],
        scratch_shapes=[pltpu.VMEM((tq, 1), F32), pltpu.VMEM((tq, 1), F32), pltpu.VMEM((tq, vw), F32)],
        input_output_aliases=aliases, name=name,
        compiler_params=_cp(("parallel", "parallel", "arbitrary")))(*[a for a, _ in ins])


def flash_dq(name, qa, ka, va, oa, doa, lse, *, w, vw, hq, grp, vcol0, scale, nlat, tq, tk, band, sink, ctx_q, prev=None):
    n = qa.shape[0]
    cblk = nlat // NCTX
    band = band and not ctx_q
    if ctx_q:
        tq = tk = NCTX
        grid = (hq, 1, 1)
        qmap = lambda h, i, kk: (cblk, h)
        kmap = lambda h, i, kk: (cblk, h // grp)
        vmap = lambda h, i, kk: (cblk, vcol0 + h // grp)
        lmap = lambda h, i, kk: (h, cblk, 0)
    else:
        nb = nlat // tk
        grid = (hq, nlat // tq, 3 if band else nb)
        kb_of = (lambda i, kk: jnp.clip(i + kk - 1, 0, nb - 1)) if band else (lambda i, kk: kk)
        qmap = lambda h, i, kk: (i, h)
        kmap = lambda h, i, kk: (kb_of(i, kk), h // grp)
        vmap = lambda h, i, kk: (kb_of(i, kk), vcol0 + h // grp)
        lmap = lambda h, i, kk: (h, i, 0)
    nk = grid[2]
    nq = grid[1]
    extra = not ctx_q
    has_sink = sink is not None

    def body(*refs):
        refs = list(refs)
        q_ref, k_ref, v_ref, o_ref, do_ref, l_ref = refs[:6]
        pos = 6
        if extra:
            ke_ref, ve_ref = refs[pos:pos + 2]
            pos += 2
        if has_sink:
            s_ref = refs[pos]
            pos += 1
        if prev is not None:
            pos += 2
        dq_ref, dl_ref, ds_ref, acc_s, dl_s = refs[pos:pos + 5]
        i = pl.program_id(1)
        kk = pl.program_id(2)
        q = q_ref[...]
        do = do_ref[...]
        lse_v = l_ref[0]

        def step(kblk, vblk, mask):
            s = _d(q, kblk, ((1,), (1,))) * (scale * LOG2E)
            if mask is not None:
                s = jnp.where(mask, s, NEG)
            p = jnp.exp2(s - lse_v)
            dp = _d(do, vblk, ((1,), (1,)))
            ds = p * (dp - dl_s[...]) * scale
            acc_s[...] += _d(ds, kblk, ((1,), (0,)))

        @pl.when(kk == 0)
        def _():
            delta = jnp.sum(do * o_ref[...].astype(F32), axis=1, keepdims=True)
            dl_s[...] = delta
            acc_s[...] = jnp.zeros((tq, w), F32)
            if has_sink:
                sv = jnp.max(s_ref[0], axis=1, keepdims=True) * LOG2E
                dsk = jnp.sum(-jnp.exp2(sv - lse_v) * delta, axis=0, keepdims=True)
                _acc(ds_ref, jnp.zeros((1, 1, 128), F32) + dsk, i == 0)
            else:
                ds_ref[...] = jnp.zeros((1, 1, 128), F32)
            if extra:
                step(ke_ref[...], ve_ref[...], None)

        if band:
            kb = i + kk - 1

            @pl.when((kb >= 0) & (kb < nlat // tk))
            def _():
                step(k_ref[...], v_ref[...], _band_mask(tq, tk, i, kb))
        else:
            step(k_ref[...], v_ref[...], None)

        @pl.when(kk == nk - 1)
        def _():
            dq_ref[...] = acc_s[...]
            dl_ref[0] = dl_s[...]

    ins = [(qa, pl.BlockSpec((tq, w), qmap)), (ka, pl.BlockSpec((tk, w), kmap)), (va, pl.BlockSpec((tk, vw), vmap)),
           (oa, pl.BlockSpec((tq, vw), qmap)), (doa, pl.BlockSpec((tq, vw), qmap)), (lse, pl.BlockSpec((1, tq, 1), lmap))]
    if extra:
        ins += [(ka, pl.BlockSpec((NCTX, w), lambda h, i, kk: (cblk, h // grp))),
                (va, pl.BlockSpec((NCTX, vw), lambda h, i, kk: (cblk, vcol0 + h // grp)))]
    if has_sink:
        ins += [(sink, pl.BlockSpec((1, 1, 128), lambda h, i, kk: (h, 0, 0)))]
    aliases = {}
    if prev is not None:
        any_spec = pl.BlockSpec(memory_space=pl.ANY)
        aliases = {len(ins): 0, len(ins) + 1: 1}
        ins += [(prev[0], any_spec), (prev[1], any_spec)]
    del nq
    return pl.pallas_call(
        body, out_shape=[_sd((n, hq * w), F32), _sd((hq, n, 1), F32), _sd((hq, 1, 128), F32)], grid=grid,
        in_specs=[s for _, s in ins],
        out_specs=[pl.BlockSpec((tq, w), qmap), pl.BlockSpec((1, tq, 1), lmap),
                   pl.BlockSpec((1, 1, 128), lambda h, i, kk: (h, 0, 0))],
        scratch_shapes=[pltpu.VMEM((tq, w), F32), pltpu.VMEM((tq, 1), F32)],
        input_output_aliases=aliases, name=name,
        compiler_params=_cp(("parallel", "arbitrary", "arbitrary")))(*[a for a, _ in ins])


def flash_dkv(name, qa, ka, va, doa, lse, delta, *, w, vw, hkv, grp, vcol0, scale, nlat, tq, tk, band, ctx_k, prev=None):
    n = qa.shape[0]
    cblk = nlat // NCTX
    nqb = nlat // tq
    band = band and not ctx_k
    if ctx_k:
        tk = NCTX
        nqs = nqb
        grid = (hkv, 1, grp * nqs)
        kmap = lambda hk, j, t: (cblk, hk)
        vmap = lambda hk, j, t: (cblk, vcol0 + hk)
        dvmap = lambda hk, j, t: (cblk, hk)
        qb_of = lambda j, t: t % nqs
    else:
        nqs = 3 if band else nqb
        grid = (hkv, nlat // tk, grp * nqs)
        kmap = lambda hk, j, t: (j, hk)
        vmap = lambda hk, j, t: (j, vcol0 + hk)
        dvmap = lambda hk, j, t: (j, hk)
        qb_of = (lambda j, t: jnp.clip(j + t % nqs - 1, 0, nqb - 1)) if band else (lambda j, t: t % nqs)
    qmap = lambda hk, j, t: (qb_of(j, t), hk * grp + t // nqs)
    lmap = lambda hk, j, t: (hk * grp + t // nqs, qb_of(j, t), 0)

    def body(*refs):
        refs = list(refs)
        q_ref, k_ref, v_ref, do_ref, l_ref, dl_ref = refs[:6]
        pos = 6
        if ctx_k:
            qe_ref, doe_ref, le_ref, dle_ref = refs[pos:pos + 4]
            pos += 4
        if prev is not None:
            pos += 2
        dk_ref, dv_ref = refs[pos:pos + 2]
        j = pl.program_id(1)
        t = pl.program_id(2)
        kblk = k_ref[...]
        vblk = v_ref[...]

        def contrib(q, do, lse_v, dl_v, mask):
            s = _d(q, kblk, ((1,), (1,))) * (scale * LOG2E)
            if mask is not None:
                s = jnp.where(mask, s, NEG)
            p = jnp.exp2(s - lse_v)
            dp = _d(do, vblk, ((1,), (1,)))
            ds = p * (dp - dl_v) * scale
            return _d(ds, q, ((0,), (0,))), _d(p, do, ((0,), (0,)))

        @pl.when(t == 0)
        def _():
            dk = jnp.zeros((tk, w), F32)
            dv = jnp.zeros((tk, vw), F32)
            if ctx_k:
                for gi in range(grp):
                    a, b = contrib(qe_ref[:, w * gi:w * gi + w], doe_ref[:, vw * gi:vw * gi + vw], le_ref[gi], dle_ref[gi], None)
                    dk = dk + a
                    dv = dv + b
            dk_ref[...] = dk
            dv_ref[...] = dv

        def add(mask):
            a, b = contrib(q_ref[...], do_ref[...], l_ref[0], dl_ref[0], mask)
            dk_ref[...] += a
            dv_ref[...] += b

        if band:
            qb = j + t % nqs - 1

            @pl.when((qb >= 0) & (qb < nqb))
            def _():
                add(_band_mask(tq, tk, qb, j))
        else:
            add(None)

    ins = [(qa, pl.BlockSpec((tq, w), qmap)), (ka, pl.BlockSpec((tk, w), kmap)), (va, pl.BlockSpec((tk, vw), vmap)),
           (doa, pl.BlockSpec((tq, vw), qmap)), (lse, pl.BlockSpec((1, tq, 1), lmap)), (delta, pl.BlockSpec((1, tq, 1), lmap))]
    if ctx_k:
        ins += [(qa, pl.BlockSpec((NCTX, grp * w), lambda hk, j, t: (cblk, hk))),
                (doa, pl.BlockSpec((NCTX, grp * vw), lambda hk, j, t: (cblk, hk))),
                (lse, pl.BlockSpec((grp, NCTX, 1), lambda hk, j, t: (hk, cblk, 0))),
                (delta, pl.BlockSpec((grp, NCTX, 1), lambda hk, j, t: (hk, cblk, 0)))]
    aliases = {}
    if prev is not None:
        any_spec = pl.BlockSpec(memory_space=pl.ANY)
        aliases = {len(ins): 0, len(ins) + 1: 1}
        ins += [(prev[0], any_spec), (prev[1], any_spec)]
    return pl.pallas_call(
        body, out_shape=[_sd((n, hkv * w), F32), _sd((n, hkv * vw), F32)], grid=grid,
        in_specs=[s for _, s in ins],
        out_specs=[pl.BlockSpec((tk, w), kmap), pl.BlockSpec((tk, vw), dvmap)],
        input_output_aliases=aliases, name=name,
        compiler_params=_cp(("parallel", "parallel", "arbitrary")))(*[a for a, _ in ins])


def attention_fwd(tag, qa, ka, va, sink, cfg, nlat):
    o, lse = flash_fwd(tag + "_fwd_lat", qa, ka, va, sink=sink, ctx_q=False, nlat=nlat, **cfg)
    return flash_fwd(tag + "_fwd_ctx", qa, ka, va, sink=sink, ctx_q=True, nlat=nlat, prev=(o, lse), **cfg)


def attention_bwd(tag, qa, ka, va, oa, doa, lse, sink, cfg, nlat):
    dq, delta, ds1 = flash_dq(tag + "_dq_lat", qa, ka, va, oa, doa, lse, sink=sink, ctx_q=False, nlat=nlat, **cfg)
    dq, delta, ds2 = flash_dq(tag + "_dq_ctx", qa, ka, va, oa, doa, lse, sink=sink, ctx_q=True, nlat=nlat,
                              prev=(dq, delta), **cfg)
    kc = {k: v for k, v in cfg.items() if k != "hq"}
    kc["hkv"] = cfg["hq"] // cfg["grp"]
    dk, dv = flash_dkv(tag + "_dkv_lat", qa, ka, va, doa, lse, delta, ctx_k=False, nlat=nlat, **kc)
    dk, dv = flash_dkv(tag + "_dkv_ctx", qa, ka, va, doa, lse, delta, ctx_k=True, nlat=nlat, prev=(dk, dv), **kc)
    return dq, dk, dv, ds1 + ds2


SWA_T = 512


def _swa_window(t, nlat):
    t = min(t, nlat)
    return t, min(t + 2 * SWA_WIN, nlat)


def _win_start(i, t, wlen, nlat):
    return pl.multiple_of(jnp.clip(i * t - SWA_WIN, 0, nlat - wlen), 128)


def _win_mask(rows, cols, row0, col0):
    rp = row0 + lax.broadcasted_iota(jnp.int32, (rows, cols), 0)
    cp = col0 + lax.broadcasted_iota(jnp.int32, (rows, cols), 1)
    return jnp.abs(rp - cp) <= SWA_WIN


def swa_fwd_lat(name, qs, ks, u, sink, nlat):
    n = qs.shape[0]
    tq, wlen = _swa_window(SWA_T, nlat)
    grp = SWA_HQ // SWA_HKV
    scale = SWA_DH ** -0.5
    vcol0 = C_V // 128

    def body(q_ref, k_ref, v_ref, s_ref, o_ref, l_ref):
        i = pl.program_id(1)
        ws = _win_start(i, tq, wlen, nlat)
        q = q_ref[...]
        s1 = _d(q, k_ref[pl.ds(ws, wlen), :], ((1,), (1,))) * (scale * LOG2E)
        s1 = jnp.where(_win_mask(tq, wlen, i * tq, ws), s1, NEG)
        s2 = _d(q, k_ref[pl.ds(nlat, NCTX), :], ((1,), (1,))) * (scale * LOG2E)
        sv = jnp.max(s_ref[0], axis=1, keepdims=True) * LOG2E
        m = jnp.maximum(jnp.maximum(jnp.max(s1, axis=1, keepdims=True), jnp.max(s2, axis=1, keepdims=True)), sv)
        p1 = jnp.exp2(s1 - m)
        p2 = jnp.exp2(s2 - m)
        l = jnp.sum(p1, axis=1, keepdims=True) + jnp.sum(p2, axis=1, keepdims=True) + jnp.exp2(sv - m)
        acc = _d(p1, v_ref[pl.ds(ws, wlen), :], ((1,), (0,))) + _d(p2, v_ref[pl.ds(nlat, NCTX), :], ((1,), (0,)))
        o_ref[...] = (acc / l).astype(BF16)
        l_ref[0] = m + jnp.log2(l)

    return pl.pallas_call(
        body, out_shape=[_sd((n, SWA_HQ * 128), BF16), _sd((SWA_HQ, n, 1), F32)], grid=(SWA_HQ, nlat // tq),
        in_specs=[pl.BlockSpec((tq, 128), lambda h, i: (i, h)), pl.BlockSpec((n, 128), lambda h, i: (0, h // grp)),
                  pl.BlockSpec((n, 128), lambda h, i: (0, vcol0 + h // grp)), pl.BlockSpec((1, 1, 128), lambda h, i: (h, 0, 0))],
        out_specs=[pl.BlockSpec((tq, 128), lambda h, i: (i, h)), pl.BlockSpec((1, tq, 1), lambda h, i: (h, i, 0))],
        name=name, compiler_params=_cp(("parallel", "arbitrary")))(qs, ks, u, sink)


def swa_dq_lat(name, qs, ks, u, o, do, lse, sink, nlat):
    n = qs.shape[0]
    tq, wlen = _swa_window(SWA_T, nlat)
    grp = SWA_HQ // SWA_HKV
    scale = SWA_DH ** -0.5
    vcol0 = C_V // 128

    def body(q_ref, k_ref, v_ref, s_ref, o_ref, do_ref, l_ref, dq_ref, dl_ref, ds_ref):
        i = pl.program_id(1)
        ws = _win_start(i, tq, wlen, nlat)
        q = q_ref[...]
        do = do_ref[...]
        lse_v = l_ref[0]
        delta = jnp.sum(do.astype(F32) * o_ref[...].astype(F32), axis=1, keepdims=True)
        kw = k_ref[pl.ds(ws, wlen), :]
        kc = k_ref[pl.ds(nlat, NCTX), :]
        s1 = _d(q, kw, ((1,), (1,))) * (scale * LOG2E)
        s1 = jnp.where(_win_mask(tq, wlen, i * tq, ws), s1, NEG)
        s2 = _d(q, kc, ((1,), (1,))) * (scale * LOG2E)
        ds1 = jnp.exp2(s1 - lse_v) * (_d(do, v_ref[pl.ds(ws, wlen), :], ((1,), (1,))) - delta) * scale
        ds2 = jnp.exp2(s2 - lse_v) * (_d(do, v_ref[pl.ds(nlat, NCTX), :], ((1,), (1,))) - delta) * scale
        dq_ref[...] = _d(ds1, kw, ((1,), (0,))) + _d(ds2, kc, ((1,), (0,)))
        dl_ref[0] = delta
        sv = jnp.max(s_ref[0], axis=1, keepdims=True) * LOG2E
        dsk = jnp.sum(-jnp.exp2(sv - lse_v) * delta, axis=0, keepdims=True)
        _acc(ds_ref, jnp.zeros((1, 1, 128), F32) + dsk, i == 0)

    qspec = pl.BlockSpec((tq, 128), lambda h, i: (i, h))
    lspec = pl.BlockSpec((1, tq, 1), lambda h, i: (h, i, 0))
    return pl.pallas_call(
        body, out_shape=[_sd((n, SWA_HQ * 128), F32), _sd((SWA_HQ, n, 1), F32), _sd((SWA_HQ, 1, 128), F32)],
        grid=(SWA_HQ, nlat // tq),
        in_specs=[qspec, pl.BlockSpec((n, 128), lambda h, i: (0, h // grp)),
                  pl.BlockSpec((n, 128), lambda h, i: (0, vcol0 + h // grp)), pl.BlockSpec((1, 1, 128), lambda h, i: (h, 0, 0)),
                  qspec, qspec, lspec],
        out_specs=[qspec, lspec, pl.BlockSpec((1, 1, 128), lambda h, i: (h, 0, 0))],
        name=name, compiler_params=_cp(("parallel", "arbitrary")))(qs, ks, u, sink, o, do, lse)


def swa_dkv_lat(name, qs, ks, u, do, lse_row, delta_row, nlat):
    n = qs.shape[0]
    tk, wlen = _swa_window(SWA_T, nlat)
    grp = SWA_HQ // SWA_HKV
    scale = SWA_DH ** -0.5
    vcol0 = C_V // 128

    def body(q_ref, k_ref, v_ref, do_ref, l_ref, dl_ref, dk_ref, dv_ref):
        j = pl.program_id(1)
        ws = _win_start(j, tk, wlen, nlat)
        k = k_ref[...]
        v = v_ref[...]
        mask = _win_mask(tk, wlen, j * tk, ws)
        dk = jnp.zeros((tk, 128), F32)
        dv = jnp.zeros((tk, 128), F32)
        for gi in range(grp):
            qw = q_ref[pl.ds(ws, wlen), 128 * gi:128 * gi + 128]
            dow = do_ref[pl.ds(ws, wlen), 128 * gi:128 * gi + 128]
            st = jnp.where(mask, _d(k, qw, ((1,), (1,))) * (scale * LOG2E), NEG)
            pt = jnp.exp2(st - l_ref[gi, :, pl.ds(ws, wlen)])
            dv = dv + _d(pt, dow, ((1,), (0,)))
            dst = pt * (_d(v, dow, ((1,), (1,))) - dl_ref[gi, :, pl.ds(ws, wlen)]) * scale
            dk = dk + _d(dst, qw, ((1,), (0,)))
        dk_ref[...] = dk
        dv_ref[...] = dv

    rspec = pl.BlockSpec((grp, 1, n), lambda hk, j: (hk, 0, 0))
    return pl.pallas_call(
        body, out_shape=[_sd((n, SWA_HKV * 128), F32), _sd((n, SWA_HKV * 128), F32)], grid=(SWA_HKV, nlat // tk),
        in_specs=[pl.BlockSpec((n, grp * 128), lambda hk, j: (0, hk)), pl.BlockSpec((tk, 128), lambda hk, j: (j, hk)),
                  pl.BlockSpec((tk, 128), lambda hk, j: (j, vcol0 + hk)), pl.BlockSpec((n, grp * 128), lambda hk, j: (0, hk)),
                  rspec, rspec],
        out_specs=[pl.BlockSpec((tk, 128), lambda hk, j: (j, hk)), pl.BlockSpec((tk, 128), lambda hk, j: (j, hk))],
        name=name, compiler_params=_cp(("parallel", "arbitrary")))(qs, ks, u, do, lse_row, delta_row)


def swa_attention_fwd(tag, qs, ks, u, sink, cfg, nlat):
    o, lse = swa_fwd_lat(tag + "_fwd_lat", qs, ks, u, sink, nlat)
    return flash_fwd(tag + "_fwd_ctx", qs, ks, u, sink=sink, ctx_q=True, nlat=nlat, prev=(o, lse), **cfg)


def swa_attention_bwd(tag, qs, ks, u, o, do, lse, sink, cfg, nlat):
    n = qs.shape[0]
    dq, delta, ds1 = swa_dq_lat(tag + "_dq_lat", qs, ks, u, o, do, lse, sink, nlat)
    dq, delta, ds2 = flash_dq(tag + "_dq_ctx", qs, ks, u, o, do, lse, sink=sink, ctx_q=True, nlat=nlat, prev=(dq, delta), **cfg)
    dk, dv = swa_dkv_lat(tag + "_dkv_lat", qs, ks, u, do, lse.reshape(SWA_HQ, 1, n), delta.reshape(SWA_HQ, 1, n), nlat)
    kc = {k: v for k, v in cfg.items() if k != "hq"}
    kc["hkv"] = SWA_HKV
    kc["tq"] = min(1024, nlat)
    dk, dv = flash_dkv(tag + "_dkv_ctx", qs, ks, u, do, lse, delta, ctx_k=True, nlat=nlat, prev=(dk, dv), **kc)
    return dq, dk, dv, ds1 + ds2


def adamw(name, w, g, m, v):
    r, c = w.shape
    tr = _pick(r, (256, 128, 64, 32, 16, 8))
    bc1 = 1.0 - ADAM_B1 ** ADAM_STEP
    bc2 = 1.0 - ADAM_B2 ** ADAM_STEP

    def body(w_ref, g_ref, m_ref, v_ref, d_ref, nm_ref, nv_ref):
        gv = g_ref[...]
        nm = ADAM_B1 * m_ref[...] + (1.0 - ADAM_B1) * gv
        nv = ADAM_B2 * v_ref[...] + (1.0 - ADAM_B2) * (gv * gv)
        d_ref[...] = -ADAM_LR * ((nm / bc1) / (jnp.sqrt(nv / bc2) + ADAM_EPS) + ADAM_WD * w_ref[...])
        nm_ref[...] = nm
        nv_ref[...] = nv

    spec = pl.BlockSpec((tr, c), lambda i: (i, 0))
    return pl.pallas_call(body, out_shape=[_sd((r, c), F32)] * 3, grid=(r // tr,), in_specs=[spec] * 4, out_specs=[spec] * 3,
                          name=name, compiler_params=_cp(("parallel",)))(w, g, m, v)


def _coords():
    return lax.axis_index("x"), lax.axis_index("y"), lax.axis_index("c")


_ANY = pl.BlockSpec(memory_space=pl.ANY)


def gather_chips(name, a):
    def body(a_ref, o_ref, send_sems, recv_sems, loc_sem):
        x, y, c = _coords()
        me = 2 * x + y
        peers = [(1 - x, y), (x, 1 - y), (1 - x, 1 - y)]
        mine = pltpu.make_async_copy(a_ref, o_ref.at[me], loc_sem)
        mine.start()
        sends = [pltpu.make_async_remote_copy(a_ref, o_ref.at[me], send_sems.at[k], recv_sems.at[k],
                                              device_id=(px, py, c), device_id_type=MESH)
                 for k, (px, py) in enumerate(peers)]
        for cp in sends:
            cp.start()
        for k, (px, py) in enumerate(peers):
            pltpu.make_async_remote_copy(a_ref, o_ref.at[2 * px + py], send_sems.at[k], recv_sems.at[k],
                                         device_id=(px, py, c), device_id_type=MESH).wait_recv()
        for cp in sends:
            cp.wait_send()
        mine.wait()

    return pl.pallas_call(
        body, out_shape=_sd((4,) + a.shape, a.dtype), in_specs=[_ANY], out_specs=_ANY,
        scratch_shapes=[pltpu.SemaphoreType.DMA((3,)), pltpu.SemaphoreType.DMA((3,)), pltpu.SemaphoreType.DMA],
        name=name, compiler_params=pltpu.CompilerParams(has_side_effects=True))(a)


def scatter_chips(name, a):
    def body(a_ref, o_ref, send_sems, recv_sems, loc_sem):
        x, y, c = _coords()
        me = 2 * x + y
        peers = [(1 - x, y), (x, 1 - y), (1 - x, 1 - y)]
        mine = pltpu.make_async_copy(a_ref.at[me], o_ref.at[me], loc_sem)
        mine.start()
        sends = [pltpu.make_async_remote_copy(a_ref.at[2 * px + py], o_ref.at[me], send_sems.at[k], recv_sems.at[k],
                                              device_id=(px, py, c), device_id_type=MESH)
                 for k, (px, py) in enumerate(peers)]
        for cp in sends:
            cp.start()
        for k, (px, py) in enumerate(peers):
            pltpu.make_async_remote_copy(a_ref.at[me], o_ref.at[2 * px + py], send_sems.at[k], recv_sems.at[k],
                                         device_id=(px, py, c), device_id_type=MESH).wait_recv()
        for cp in sends:
            cp.wait_send()
        mine.wait()

    return pl.pallas_call(
        body, out_shape=_sd(a.shape, a.dtype), in_specs=[_ANY], out_specs=_ANY,
        scratch_shapes=[pltpu.SemaphoreType.DMA((3,)), pltpu.SemaphoreType.DMA((3,)), pltpu.SemaphoreType.DMA],
        name=name, compiler_params=pltpu.CompilerParams(has_side_effects=True))(a)


def sibling_swap(name, a):
    def body(a_ref, o_ref, send_sem, recv_sem):
        x, y, c = _coords()
        cp = pltpu.make_async_remote_copy(a_ref, o_ref, send_sem, recv_sem, device_id=(x, y, 1 - c), device_id_type=MESH)
        cp.start()
        cp.wait()

    return pl.pallas_call(
        body, out_shape=_sd(a.shape, a.dtype), in_specs=[_ANY], out_specs=_ANY,
        scratch_shapes=[pltpu.SemaphoreType.DMA, pltpu.SemaphoreType.DMA],
        name=name, compiler_params=pltpu.CompilerParams(has_side_effects=True))(a)


def gather_all(name, a):
    def body(a_ref, o_ref, send_sems, recv_sems, loc_sem):
        x, y, c = _coords()
        me = 4 * x + 2 * y + c
        flips = [(fx, fy, fc) for fx in (0, 1) for fy in (0, 1) for fc in (0, 1) if fx + fy + fc > 0]
        peers = [(x ^ fx, y ^ fy, c ^ fc) for fx, fy, fc in flips]
        mine = pltpu.make_async_copy(a_ref, o_ref.at[me], loc_sem)
        mine.start()
        sends = [pltpu.make_async_remote_copy(a_ref, o_ref.at[me], send_sems.at[k], recv_sems.at[k],
                                              device_id=p, device_id_type=MESH) for k, p in enumerate(peers)]
        for cp in sends:
            cp.start()
        for k, (px, py, pc) in enumerate(peers):
            pltpu.make_async_remote_copy(a_ref, o_ref.at[4 * px + 2 * py + pc], send_sems.at[k], recv_sems.at[k],
                                         device_id=(px, py, pc), device_id_type=MESH).wait_recv()
        for cp in sends:
            cp.wait_send()
        mine.wait()

    return pl.pallas_call(
        body, out_shape=_sd((8,) + a.shape, a.dtype), in_specs=[_ANY], out_specs=_ANY,
        scratch_shapes=[pltpu.SemaphoreType.DMA((7,)), pltpu.SemaphoreType.DMA((7,)), pltpu.SemaphoreType.DMA],
        name=name, compiler_params=pltpu.CompilerParams(has_side_effects=True))(a)


def sum_blocks(name, a):
    k, r, c = a.shape
    tr = _pick(r, (256, 128, 64, 32, 16, 8))

    def body(a_ref, o_ref):
        acc = a_ref[0].astype(F32)
        for s in range(1, k):
            acc = acc + a_ref[s].astype(F32)
        o_ref[...] = acc

    return pl.pallas_call(body, out_shape=_sd((r, c), F32), grid=(r // tr,),
                          in_specs=[pl.BlockSpec((k, tr, c), lambda i: (0, i, 0))], out_specs=pl.BlockSpec((tr, c), lambda i: (i, 0)),
                          name=name, compiler_params=_cp(("parallel",)))(a)


def add_pair(name, a, b):
    r, c = a.shape
    tr = _pick(r, (256, 128, 64, 32, 16, 8))

    def body(a_ref, b_ref, o_ref):
        o_ref[...] = a_ref[...] + b_ref[...]

    spec = pl.BlockSpec((tr, c), lambda i: (i, 0))
    return pl.pallas_call(body, out_shape=_sd((r, c), F32), grid=(r // tr,), in_specs=[spec, spec], out_specs=spec,
                          name=name, compiler_params=_cp(("parallel",)))(a, b)


BIG = ("w_mod", "w_in", "w_mla_uq", "w_mla_ukv", "w_p_ssm", "w_p_swa", "w_p_mla", "w_out", "w_ffn_in", "w_ffn_out")
COL_SHARDED = ("w_mod", "w_in", "w_mla_uq", "w_mla_ukv", "w_ffn_in")
SMALL = ("c_ctx", "b_mod", "norm1_g", "norm2_g", "ssm_conv_w", "ssm_conv_b", "ssm_dt_bias", "ssm_a_log", "ssm_d",
         "ssm_norm_g", "swa_q_norm_g", "swa_k_norm_g", "swa_sink", "mla_q_lat_g", "mla_kv_lat_g", "mla_q_norm_g",
         "mla_k_norm_g")
WEIGHTS = ("c_ctx", "w_mod", "b_mod", "norm1_g", "norm2_g", "w_in", "ssm_conv_w", "ssm_conv_b", "ssm_dt_bias", "ssm_a_log",
           "ssm_d", "ssm_norm_g", "swa_q_norm_g", "swa_k_norm_g", "swa_sink", "mla_q_lat_g", "mla_kv_lat_g", "w_mla_uq",
           "w_mla_ukv", "mla_q_norm_g", "mla_k_norm_g", "w_p_ssm", "w_p_swa", "w_p_mla", "w_out", "w_ffn_in", "w_ffn_out")


def pack_w_in(w):
    z = lambda k: jnp.zeros((w.shape[0], k), w.dtype)
    return jnp.concatenate([w[:, 4832:7904], w[:, 2400:3424], w[:, 3424:4448], w[:, 0:1536], w[:, 1568:1824], w[:, 1824:2080],
                            w[:, 2080:2336], w[:, 2336:2400], w[:, 1536:1568], z(32), z(128), w[:, 4448:4832]], axis=1)


def unpack_w_in(g):
    return jnp.concatenate([g[:, 5120:6656], g[:, 7488:7520], g[:, 6656:6912], g[:, 6912:7168], g[:, 7168:7424], g[:, 7424:7488],
                            g[:, 3072:4096], g[:, 4096:5120], g[:, 7680:8064], g[:, 0:3072]], axis=1)


def pack_ukv(w):
    return w.reshape(MLA_KVRANK, MLA_H, 2, 128).transpose(0, 2, 1, 3).reshape(MLA_KVRANK, 2048)


def unpack_ukv(g):
    return g.reshape(MLA_KVRANK, 2, MLA_H, 128).transpose(0, 2, 1, 3).reshape(MLA_KVRANK, 2048)


def pack_uq(w):
    return jnp.pad(w.reshape(MLA_QRANK, MLA_H, 192), ((0, 0), (0, 0), (0, 64))).reshape(MLA_QRANK, 2048)


def unpack_uq(g):
    return g.reshape(MLA_QRANK, MLA_H, 256)[:, :, :192].reshape(MLA_QRANK, 1536)


def rope_tables(nlat):
    t = jnp.arange(nlat, dtype=jnp.int32)
    r = (t // GRID_W).astype(F32)[:, None]
    col = (t % GRID_W).astype(F32)[:, None]

    def tab(nf, pad):
        inv = jnp.power(ROPE_BASE, -jnp.arange(nf, dtype=F32) / nf)
        ar, ac = r * inv, col * inv
        cos = jnp.concatenate([jnp.cos(ar), jnp.cos(ar), jnp.cos(ac), jnp.cos(ac), jnp.ones((nlat, pad), F32)], axis=1)
        sin = jnp.concatenate([-jnp.sin(ar), jnp.sin(ar), -jnp.sin(ac), jnp.sin(ac), jnp.zeros((nlat, pad), F32)], axis=1)
        cos = jnp.concatenate([cos, jnp.ones((NCTX, 128), F32)], axis=0)
        sin = jnp.concatenate([sin, jnp.zeros((NCTX, 128), F32)], axis=0)
        return cos, sin

    return tab(32, 0), tab(16, 64)


def _lanes(v, start, width=128):
    return jnp.zeros((1, width), F32).at[0, start:start + v.shape[0]].set(v)


def layer_fwd(i, xin, h, mod, p, tabs, nlat):
    t = "l%d_" % i
    n = xin.shape[0]
    (cos_s, sin_s), (cos_m, sin_m) = tabs
    u = mm(h, p["w_in"], F32, t + "in_proj")
    xbc = conv_fwd(t + "conv", u, p["conv_w"], p["conv_b"], nlat)
    dtrow = jnp.transpose(u[:, C_MISC + DT_LANE:C_MISC + DT_LANE + 32])
    nlc = nlat // Q
    yf, hs_f = ssd_fwd(t + "ssd_f", xbc, u, dtrow, p["bias_c"], p["alog_c"], p["bias_r"], p["alog_r"], nlc, False, 0)
    yb, hs_b = ssd_fwd(t + "ssd_b", xbc, u, dtrow, p["bias_c"], p["alog_c"], p["bias_r"], p["alog_r"], nlc, True, 1)
    ys = ssd_out_fwd(t + "ssd_out", yf, yb, xbc, u, p["ssm_norm_g"], p["d_exp"])
    qs, ks = swa_prep_fwd(t + "swa_prep", u, p["swa_q_g"], p["swa_k_g"], cos_s, sin_s)
    o_swa, lse_swa = swa_attention_fwd(t + "swa", qs, ks, u, p["sink"], p["swa_cfg"], nlat)
    ckv_n, cq_n = lat_norm_fwd(t + "lat_norm", u, p["kv_lat_g"], p["q_lat_g"])
    kv = mm(ckv_n, p["w_ukv"], F32, t + "ukv")
    qp = mm(cq_n, p["w_uq"], F32, t + "uq")
    km, qm = mla_prep_fwd(t + "mla_prep", kv, qp, u, p["mla_q_g"], p["mla_k_g"], cos_m, sin_m)
    o_mla, lse_mla = attention_fwd(t + "mla", qm, km, kv, None, p["mla_cfg"], nlat)
    p1 = mm(ys, p["w_p_ssm"], F32, t + "p_ssm")
    p2 = mm(o_swa, p["w_p_swa"], F32, t + "p_swa")
    p3 = mm(o_mla, p["w_p_mla"], F32, t + "p_mla")
    merged = merge_fwd(t + "merge", u, p1, p2, p3)
    o = mm(merged, p["w_out"], F32, t + "out_proj")
    x1, h2 = resid_mod_fwd(t + "res1", xin, o, mod, 2, mod, 3, 4, p["norm2_g"], nlat // RT)
    gu = mm(h2, p["w_ffn_in"], F32, t + "ffn_in")
    a = swiglu_fwd(t + "swiglu", gu)
    f = mm(a, p["w_ffn_out"], F32, t + "ffn_out")
    saved = dict(xin=xin, h=h, u=u, xbc=xbc, dtrow=dtrow, yf=yf, yb=yb, hs_f=hs_f, hs_b=hs_b, ys=ys, qs=qs, ks=ks,
                 o_swa=o_swa, lse_swa=lse_swa, ckv_n=ckv_n, cq_n=cq_n, kv=kv, qp=qp, km=km, qm=qm, o_mla=o_mla,
                 lse_mla=lse_mla, p1=p1, p2=p2, p3=p3, merged=merged, o=o, x1=x1, h2=h2, gu=gu, a=a, f=f)
    del n
    return x1, f, saved


def layer_bwd(i, dx2, df, dgt2, sv, mod, p, tabs, nlat):
    t = "l%db_" % i
    (cos_s, sin_s), (cos_m, sin_m) = tabs
    g = {}
    nt = nlat // RT
    nlc = nlat // Q
    g["w_ffn_out"] = mm_tn(sv["a"], df, t + "wg_ffn_out")
    da = mm(df, p["w_ffn_out_t"], F32, t + "dg_ffn_out")
    dgu = swiglu_bwd(t + "swiglu", sv["gu"], da)
    g["w_ffn_in"] = mm_tn(sv["h2"], dgu, t + "wg_ffn_in")
    dh2 = mm(dgu, p["w_ffn_in_t"], F32, t + "dg_ffn_in")
    dx1, do, dgt1, dsh2, dsc2, g["norm2_g"] = resid_mod_bwd(t + "res1", sv["x1"], dx2, dh2, sv["o"], mod, 2, mod, 3, 4,
                                                              p["norm2_g"], nt)
    g["w_out"] = mm_tn(sv["merged"], do, t + "wg_out")
    dmerged = mm(do, p["w_out_t"], F32, t + "dg_out")
    dp1, dp2, dp3, dgates = merge_bwd(t + "merge", sv["u"], sv["p1"], sv["p2"], sv["p3"], dmerged)
    g["w_p_ssm"] = mm_tn(sv["ys"], dp1, t + "wg_p_ssm")
    g["w_p_swa"] = mm_tn(sv["o_swa"], dp2, t + "wg_p_swa")
    g["w_p_mla"] = mm_tn(sv["o_mla"], dp3, t + "wg_p_mla")
    dys = mm(dp1, p["w_p_ssm_t"], F32, t + "dg_p_ssm")
    do_swa = mm(dp2, p["w_p_swa_t"], BF16, t + "dg_p_swa")
    do_mla = mm(dp3, p["w_p_mla_t"], F32, t + "dg_p_mla")
    dqm, dkm, dv_mla, _ = attention_bwd(t + "mla", sv["qm"], sv["km"], sv["kv"], sv["o_mla"], do_mla, sv["lse_mla"], None,
                                        p["mla_cfg"], nlat)
    dkv, dqp, dkr, g["mla_q_g"], g["mla_k_g"] = mla_prep_bwd(t + "mla_prep", sv["kv"], sv["qp"], sv["u"], p["mla_q_g"],
                                                             p["mla_k_g"], cos_m, sin_m, dkm, dqm, dv_mla)
    g["w_ukv"] = mm_tn(sv["ckv_n"], dkv, t + "wg_ukv")
    g["w_uq"] = mm_tn(sv["cq_n"], dqp, t + "wg_uq")
    dckv_n = mm(dkv, p["w_ukv_t"], F32, t + "dg_ukv")
    dcq_n = mm(dqp, p["w_uq_t"], F32, t + "dg_uq")
    dckv, dcq, g["kv_lat_g"], g["q_lat_g"] = lat_norm_bwd(t + "lat_norm", sv["u"], p["kv_lat_g"], p["q_lat_g"], dckv_n, dcq_n)
    dqs, dks, dv_swa, g["sink"] = swa_attention_bwd(t + "swa", sv["qs"], sv["ks"], sv["u"], sv["o_swa"], do_swa, sv["lse_swa"],
                                                p["sink"], p["swa_cfg"], nlat)
    dq, dk, dv, g["swa_q_g"], g["swa_k_g"] = swa_prep_bwd(t + "swa_prep", sv["u"], p["swa_q_g"], p["swa_k_g"], cos_s, sin_s,
                                                          dqs, dks, dv_swa)
    dy, dxs_skip, dz, g["ssm_norm_g"], g["d_exp"] = ssd_out_bwd(t + "ssd_out", sv["yf"], sv["yb"], sv["xbc"], sv["u"],
                                                                 p["ssm_norm_g"], p["d_exp"], dys)
    n = dy.shape[0]
    zbc = jnp.zeros((n, 256), F32)
    r_f = ssd_bwd(t + "ssd_f", sv["xbc"], sv["u"], sv["dtrow"], p["bias_c"], p["alog_c"], p["bias_r"], p["alog_r"],
                  sv["hs_f"], dy, (dxs_skip, zbc, zbc), nlc, False, 0)
    r_b = ssd_bwd(t + "ssd_b", sv["xbc"], sv["u"], sv["dtrow"], p["bias_c"], p["alog_c"], p["bias_r"], p["alog_r"],
                  sv["hs_b"], dy, (r_f[0], r_f[1], r_f[2]), nlc, True, 1)
    dact = jnp.concatenate([r_b[0], r_b[1], r_b[2]], axis=1)
    dxbc, g["conv_w"], g["conv_b"] = conv_bwd(t + "conv", sv["u"], dact, p["conv_w"], p["conv_b"], nlat)
    drow = jnp.concatenate([r_f[4][0] + r_f[4][1], r_b[4][0] + r_b[4][1]], axis=0)
    drow_t = jnp.pad(jnp.transpose(drow), ((0, 0), (DT_LANE, 128 - DT_LANE - 32)))
    dmisc = misc_combine(t + "misc", dkr, r_f[3], r_b[3], drow_t)
    g["bias_c"] = r_f[5] + r_b[5]
    g["alog_c"] = r_f[6] + r_b[6]
    g["bias_r"] = jnp.concatenate([r_f[7], r_b[7]], axis=0)
    g["alog_r"] = jnp.concatenate([r_f[8], r_b[8]], axis=0)
    du = jnp.concatenate([dgates, dz, dq, dxbc, dk, dv, dckv, dmisc, jnp.zeros((n, 128), BF16), dcq], axis=1)
    g["w_in"] = mm_tn(sv["h"], du, t + "wg_in")
    dh = mm(du, p["w_in_t"], F32, t + "dg_in")
    g["mod"] = (dgt1, dsh2, dsc2, dgt2)
    return dx1, dh, g


def local_step(x, c, ctx, target, c_ctx, W, nlat):
    xin = jnp.concatenate([x, ctx], axis=0)
    n = xin.shape[0]
    nt = nlat // RT
    tabs = rope_tables(nlat)
    c8 = jnp.zeros((8, D), F32).at[0].set(c[0]).at[1].set(c_ctx)
    mods, silus = [], []
    for i in range(DEPTH):
        m8, s8 = mod_fwd("l%d_mod" % i, c8, W[i]["w_mod"], W[i]["b_mod"])
        mods.append(m8[0:2].reshape(2, 1, 6 * D))
        silus.append(s8)
    saved = []
    _, h = resid_mod_fwd("l0_norm1", xin, None, None, 0, mods[0], 0, 1, W[0]["norm1_g"], nt)
    xcur = xin
    for i in range(DEPTH):
        x1, f, sv = layer_fwd(i, xcur, h, mods[i], W[i], tabs, nlat)
        saved.append(sv)
        if i + 1 < DEPTH:
            xcur, h = resid_mod_fwd("l%d_res2" % i, x1, f, mods[i], 5, mods[i + 1], 0, 1, W[i + 1]["norm1_g"], nt)
    loss_v, dx2, df, dgt2 = resid_loss("loss", x1, f, mods[DEPTH - 1], 5, target, nt)
    grads = [None] * DEPTH
    for i in reversed(range(DEPTH)):
        dx1, dh, g = layer_bwd(i, dx2, df, dgt2, saved[i], mods[i], W[i], tabs, nlat)
        if i > 0:
            sv = saved[i]
            dx2, df, dgt2, dsh1, dsc1, g["norm1_g"] = resid_mod_bwd(
                "l%db_res2" % (i - 1), sv["xin"], dx1, dh, saved[i - 1]["f"], mods[i - 1], 5, mods[i], 0, 1,
                W[i]["norm1_g"], nt)
        else:
            dxin, _, _, dsh1, dsc1, g["norm1_g"] = resid_mod_bwd("l0b_norm1", saved[0]["xin"], dx1, dh, None, None, 0,
                                                                  mods[0], 0, 1, W[0]["norm1_g"], nt)
        dgt1, dsh2, dsc2, dgt2_i = g.pop("mod")
        dmod = jnp.concatenate([dsh1, dsc1, dgt1, dsh2, dsc2, dgt2_i], axis=2).reshape(2, 6 * D)
        dmod8 = jnp.zeros((8, 6 * D), F32).at[0:2].set(dmod)
        g["w_mod"] = mm_tn(silus[i], dmod8, "l%db_wg_mod" % i)
        dsilu = mm(dmod8, W[i]["w_mod_t"], F32, "l%db_dg_mod" % i)
        dc8, g["b_mod"] = mod_small_bwd("l%db_mod_small" % i, c8, dsilu, dmod8)
        g["c8"] = dc8
        grads[i] = g
    del n
    return loss_v[0, 0], dxin, grads


def _big_shapes():
    return dict(w_mod=(2, 1024, 1536), w_in=(2, 1024, 1976), w_mla_uq=(2, 384, 384), w_mla_ukv=(2, 256, 512),
                w_p_ssm=(2, 256, 1024), w_p_swa=(2, 256, 1024), w_p_mla=(2, 256, 1024), w_out=(2, 256, 1024),
                w_ffn_in=(2, 1024, 1408), w_ffn_out=(2, 704, 1024))


def _pack_big(d, dtype):
    return jnp.concatenate([d[k].astype(dtype).reshape(-1, 1024) for k in BIG], axis=0)


def _unpack_big(buf, lead):
    out = {}
    r0 = 0
    for k in BIG:
        sh = _big_shapes()[k]
        rows = sh[0] * sh[1] * sh[2] // 1024
        out[k] = buf[..., r0:r0 + rows, :].reshape(lead + sh)
        r0 += rows
    return out


def _full_from_chips(k, a):
    if k in COL_SHARDED:
        return a.transpose(1, 2, 0, 3).reshape(2, a.shape[2], 4 * a.shape[3])
    return a.transpose(1, 0, 2, 3).reshape(2, 4 * a.shape[2], a.shape[3])


def _chips_from_full(k, a):
    if k in COL_SHARDED:
        return a.reshape(a.shape[0], 4, a.shape[1] // 4).transpose(1, 0, 2)
    return a.reshape(4, a.shape[0] // 4, a.shape[1])


def _small_sizes():
    return dict(c_ctx=1024, b_mod=2 * 6144, norm1_g=2048, norm2_g=2048, ssm_conv_w=2 * 5 * 1536, ssm_conv_b=2 * 1536,
                ssm_dt_bias=64, ssm_a_log=64, ssm_d=32, ssm_norm_g=2048, swa_q_norm_g=256, swa_k_norm_g=256, swa_sink=16,
                mla_q_lat_g=768, mla_kv_lat_g=512, mla_q_norm_g=384, mla_k_norm_g=384)


def _pack_small(d):
    parts = []
    for k in SMALL:
        v = d[k].astype(F32).reshape(-1)
        parts.append(jnp.pad(v, (0, (-v.shape[0]) % 1024)))
    return jnp.concatenate(parts).reshape(-1, 128)


def _unpack_small(buf, shapes):
    flat = buf.reshape(-1)
    out = {}
    o = 0
    for k in SMALL:
        sz = _small_sizes()[k]
        out[k] = flat[o:o + sz].reshape(shapes[k])
        o += sz + (-sz) % 1024
    return out


def big_grads(grads):
    gfull = {k: [] for k in BIG}
    for i in range(DEPTH):
        g = grads[i]
        gfull["w_mod"].append(g["w_mod"])
        gfull["w_in"].append(unpack_w_in(g["w_in"]))
        gfull["w_mla_uq"].append(unpack_uq(g["w_uq"]))
        gfull["w_mla_ukv"].append(unpack_ukv(g["w_ukv"]))
        for k in ("w_p_ssm", "w_p_swa", "w_p_mla", "w_out", "w_ffn_in", "w_ffn_out"):
            gfull[k].append(g[k])
    return gfull


def small_grads(grads):
    gs = {}
    gs["c_ctx"] = sum(grads[i]["c8"][1] for i in range(DEPTH))
    st = lambda f: jnp.stack([f(grads[i]) for i in range(DEPTH)])
    gs["b_mod"] = st(lambda g: g["b_mod"][0])
    gs["norm1_g"] = st(lambda g: g["norm1_g"][0])
    gs["norm2_g"] = st(lambda g: g["norm2_g"][0])
    gs["ssm_conv_w"] = st(lambda g: g["conv_w"])
    gs["ssm_conv_b"] = st(lambda g: g["conv_b"][0])
    gs["ssm_dt_bias"] = st(lambda g: (g["bias_c"][0, DT_LANE:DT_LANE + 32] + g["bias_r"][:, 0]).reshape(2, 16))
    gs["ssm_a_log"] = st(lambda g: (g["alog_c"][0, DT_LANE:DT_LANE + 32] + g["alog_r"][:, 0]).reshape(2, 16))
    gs["ssm_d"] = st(lambda g: g["d_exp"].reshape(16, 64).sum(axis=1))
    gs["ssm_norm_g"] = st(lambda g: g["ssm_norm_g"][0])
    gs["swa_q_norm_g"] = st(lambda g: g["swa_q_g"][0])
    gs["swa_k_norm_g"] = st(lambda g: g["swa_k_g"][0])
    gs["swa_sink"] = st(lambda g: g["sink"][:, 0, 0])
    gs["mla_q_lat_g"] = st(lambda g: g["q_lat_g"][0])
    gs["mla_kv_lat_g"] = st(lambda g: g["kv_lat_g"][0])
    gs["mla_q_norm_g"] = st(lambda g: g["mla_q_g"][0, :192])
    gs["mla_k_norm_g"] = st(lambda g: g["mla_k_g"][0, :192])
    return gs


def layer_params(i, full, conv_full, sm, nlat):
    p = {}
    p["w_mod"] = full["w_mod"][i]
    p["w_in"] = pack_w_in(full["w_in"][i])
    p["w_uq"] = pack_uq(full["w_mla_uq"][i])
    p["w_ukv"] = pack_ukv(full["w_mla_ukv"][i])
    for k in ("w_p_ssm", "w_p_swa", "w_p_mla", "w_out", "w_ffn_in", "w_ffn_out"):
        p[k] = full[k][i]
    for k in ("w_mod", "w_in", "w_uq", "w_ukv", "w_p_ssm", "w_p_swa", "w_p_mla", "w_out", "w_ffn_in", "w_ffn_out"):
        p[k + "_t"] = jnp.transpose(p[k])
    p["b_mod"] = sm["b_mod"][i][None]
    p["norm1_g"] = sm["norm1_g"][i][None]
    p["norm2_g"] = sm["norm2_g"][i][None]
    p["conv_w"] = conv_full[i]
    p["conv_b"] = sm["ssm_conv_b"][i][None]
    bias = sm["ssm_dt_bias"][i].reshape(32)
    alog = sm["ssm_a_log"][i].reshape(32)
    p["bias_c"] = _lanes(bias, DT_LANE)
    p["alog_c"] = _lanes(alog, DT_LANE)
    p["bias_r"] = bias[:, None]
    p["alog_r"] = alog[:, None]
    p["d_exp"] = jnp.repeat(sm["ssm_d"][i], 64)[None]
    p["ssm_norm_g"] = sm["ssm_norm_g"][i][None]
    p["swa_q_g"] = sm["swa_q_norm_g"][i][None]
    p["swa_k_g"] = sm["swa_k_norm_g"][i][None]
    p["sink"] = jnp.broadcast_to(sm["swa_sink"][i][:, None, None], (SWA_HQ, 1, 128))
    p["q_lat_g"] = sm["mla_q_lat_g"][i][None]
    p["kv_lat_g"] = sm["mla_kv_lat_g"][i][None]
    p["mla_q_g"] = _lanes(sm["mla_q_norm_g"][i], 0, 256)
    p["mla_k_g"] = _lanes(sm["mla_k_norm_g"][i], 0, 256)
    p["swa_cfg"] = dict(w=128, vw=128, hq=SWA_HQ, grp=SWA_HQ // SWA_HKV, vcol0=C_V // 128, scale=SWA_DH ** -0.5,
                        tq=256, tk=256, band=True)
    p["mla_cfg"] = dict(w=256, vw=128, hq=MLA_H, grp=1, vcol0=8, scale=(MLA_NOPE + MLA_ROPE) ** -0.5,
                        tq=min(1024, nlat), tk=min(1024, nlat), band=False)
    return p


def kernel(x, c, ctx, c_ctx, w_mod, b_mod, norm1_g, norm2_g, w_in, ssm_conv_w, ssm_conv_b, ssm_dt_bias, ssm_a_log, ssm_d, ssm_norm_g, swa_q_norm_g, swa_k_norm_g, swa_sink, mla_q_lat_g, mla_kv_lat_g, w_mla_uq, w_mla_ukv, mla_q_norm_g, mla_k_norm_g, w_p_ssm, w_p_swa, w_p_mla, w_out, w_ffn_in, w_ffn_out, loss_target, m_c_ctx, m_w_mod, m_b_mod, m_norm1_g, m_norm2_g, m_w_in, m_ssm_conv_w, m_ssm_conv_b, m_ssm_dt_bias, m_ssm_a_log, m_ssm_d, m_ssm_norm_g, m_swa_q_norm_g, m_swa_k_norm_g, m_swa_sink, m_mla_q_lat_g, m_mla_kv_lat_g, m_w_mla_uq, m_w_mla_ukv, m_mla_q_norm_g, m_mla_k_norm_g, m_w_p_ssm, m_w_p_swa, m_w_p_mla, m_w_out, m_w_ffn_in, m_w_ffn_out, v_c_ctx, v_w_mod, v_b_mod, v_norm1_g, v_norm2_g, v_w_in, v_ssm_conv_w, v_ssm_conv_b, v_ssm_dt_bias, v_ssm_a_log, v_ssm_d, v_ssm_norm_g, v_swa_q_norm_g, v_swa_k_norm_g, v_swa_sink, v_mla_q_lat_g, v_mla_kv_lat_g, v_w_mla_uq, v_w_mla_ukv, v_mla_q_norm_g, v_mla_k_norm_g, v_w_p_ssm, v_w_p_swa, v_w_p_mla, v_w_out, v_w_ffn_in, v_w_ffn_out):
    loc = dict(locals())
    w = {k: loc[k] for k in WEIGHTS}
    m = {k: loc["m_" + k] for k in WEIGHTS}
    v = {k: loc["v_" + k] for k in WEIGHTS}
    nlat = x.shape[1]

    gathered = _unpack_big(gather_chips("gather_weights", _pack_big(w, BF16)), (4,))
    full = {k: _full_from_chips(k, gathered[k]) for k in BIG}
    conv_sh = jnp.pad(ssm_conv_w.reshape(10, 384), ((0, 6), (0, 0)))
    conv_full = gather_chips("gather_conv", conv_sh)[:, :10].reshape(4, 2, 5, 384).transpose(1, 2, 0, 3).reshape(2, 5, 1536)

    W = [layer_params(i, full, conv_full, w, nlat) for i in range(DEPTH)]

    loss_loc, dx, grads = local_step(x[0], c, ctx[0], loss_target[0], c_ctx, W, nlat)

    gfull = big_grads(grads)
    by_chip = {k: jnp.stack([_chips_from_full(k, a) for a in gfull[k]], axis=1) for k in BIG}
    send = jnp.concatenate([by_chip[k].astype(BF16).reshape(4, -1, 1024) for k in BIG], axis=1)
    recv = scatter_chips("scatter_grads", send)
    mine = sum_blocks("sum_chips", recv)
    other = sibling_swap("swap_cores", mine)
    gbig = _unpack_big(add_pair("sum_cores", mine, other), ())

    gs = small_grads(grads)
    small_all = gather_all("gather_small", _pack_small(gs))
    small_sum = sum_blocks("sum_small", small_all)
    full_shapes = {k: (w[k].shape if k != "ssm_conv_w" else (2, 5, 1536)) for k in SMALL}
    gsmall = _unpack_small(small_sum, full_shapes)
    chip = 2 * lax.axis_index("x") + lax.axis_index("y")
    gsmall["ssm_conv_w"] = lax.dynamic_slice_in_dim(gsmall["ssm_conv_w"], chip * 384, 384, axis=2)

    grad = {**gbig, **gsmall}
    delta, new_m, new_v = {}, {}, {}
    sm = {k: _pack_small_local(d) for k, d in (("w", w), ("g", grad), ("m", m), ("v", v))}
    r = adamw("adamw_small", sm["w"], sm["g"], sm["m"], sm["v"])
    shapes = {k: w[k].shape for k in SMALL}
    for dst, buf in zip((delta, new_m, new_v), r):
        dst.update(_unpack_small_local(buf, shapes))
    for k in BIG:
        sh = w[k].shape
        r = adamw("adamw_" + k, *[a[k].reshape(sh[0] * sh[1], sh[2]) for a in (w, grad, m, v)])
        for dst, buf in zip((delta, new_m, new_v), r):
            dst[k] = buf.reshape(sh)

    loss = lax.psum(loss_loc, ("x", "y", "c"))
    return (loss, dx[None, :nlat], *[grad[k] for k in WEIGHTS], *[delta[k] for k in WEIGHTS],
            *[new_m[k] for k in WEIGHTS], *[new_v[k] for k in WEIGHTS])


def _pack_small_local(d):
    parts = []
    for k in SMALL:
        a = d[k].astype(F32).reshape(-1)
        parts.append(jnp.pad(a, (0, (-a.shape[0]) % 1024)))
    return jnp.concatenate(parts).reshape(-1, 128)


def _unpack_small_local(buf, shapes):
    flat = buf.reshape(-1)
    out = {}
    o = 0
    for k in SMALL:
        sz = math.prod(shapes[k])
        out[k] = flat[o:o + sz].reshape(shapes[k])
        o += sz + (-sz) % 1024
    return out
```

```python
import functools
import math

import jax
import jax.numpy as jnp
from jax import lax
from jax.experimental import pallas as pl
from jax.experimental.pallas import tpu as pltpu

F32 = jnp.float32
BF16 = jnp.bfloat16
MESH = pl.DeviceIdType.MESH

D = 1024
NCTX = 256
EPS = 1e-6
ROPE_BASE = 10000.0
GRID_W = 64
DEPTH = 2
Q = 128
N_HEADS_SSM = 16
SWA_HQ, SWA_HKV, SWA_DH, SWA_WIN = 8, 2, 128, 128
MLA_H, MLA_NOPE, MLA_ROPE, MLA_V = 8, 128, 64, 128
MLA_QRANK, MLA_KVRANK = 384, 256
FFN = 2816
RT = 256
VMEM_LIMIT = 56 << 20
NEG = -1e30
LOG2E = 1.4426950408889634

C_G1, C_G2, C_G3, C_Z, C_Q, C_XS, C_B, C_C, C_K, C_V, C_CKV, C_MISC, C_PAD, C_CQ = (
    0, 1024, 2048, 3072, 4096, 5120, 6144, 6400, 6656, 6912, 7168, 7424, 7552, 7680)
UW = 8064
DT_LANE = 64

ADAM_LR, ADAM_B1, ADAM_B2, ADAM_EPS, ADAM_WD, ADAM_STEP = 0.001, 0.9, 0.999, 1e-08, 0.01, 10


def _cp(sem):
    return pltpu.CompilerParams(dimension_semantics=sem, vmem_limit_bytes=VMEM_LIMIT)


def _pick(n, cands):
    for c in cands:
        if n % c == 0:
            return c
    return n


_TN = (1536, 1408, 1152, 1024, 896, 768, 512, 384, 256, 128)


def mm(a, b, out_dtype, name):
    m, k = a.shape
    _, n = b.shape
    tm = _pick(m, (768, 512, 256, 128, 8))
    tn = _pick(n, _TN)
    tk = k if k <= 2048 else _pick(k, (1408, 1152, 1024, 896, 768, 512))
    nk = k // tk

    def body(a_ref, b_ref, o_ref, *acc):
        p = jnp.dot(a_ref[...].astype(BF16), b_ref[...].astype(BF16), preferred_element_type=F32)
        if nk == 1:
            o_ref[...] = p.astype(out_dtype)
        else:
            kk = pl.program_id(2)

            @pl.when(kk == 0)
            def _():
                acc[0][...] = p

            @pl.when(kk > 0)
            def _():
                acc[0][...] += p

            @pl.when(kk == nk - 1)
            def _():
                o_ref[...] = acc[0][...].astype(out_dtype)

    return pl.pallas_call(
        body, out_shape=jax.ShapeDtypeStruct((m, n), out_dtype), grid=(m // tm, n // tn, nk),
        in_specs=[pl.BlockSpec((tm, tk), lambda i, j, kk: (i, kk)), pl.BlockSpec((tk, tn), lambda i, j, kk: (kk, j))],
        out_specs=pl.BlockSpec((tm, tn), lambda i, j, kk: (i, j)),
        scratch_shapes=[] if nk == 1 else [pltpu.VMEM((tm, tn), F32)],
        name=name, compiler_params=_cp(("parallel", "parallel", "arbitrary")))(a, b)


def mm_tn(a, b, name):
    t, ka = a.shape
    _, nb = b.shape
    ta = _pick(ka, (1024, 1408, 768, 512, 384, 256, 128))
    tb = _pick(nb, _TN)
    tt = _pick(t, (768, 512, 256, 128, 8))
    nt = t // tt

    def body(a_ref, b_ref, o_ref):
        p = lax.dot_general(a_ref[...].astype(BF16), b_ref[...].astype(BF16), (((0,), (0,)), ((), ())),
                            preferred_element_type=F32)
        s = pl.program_id(2)

        @pl.when(s == 0)
        def _():
            o_ref[...] = p

        @pl.when(s > 0)
        def _():
            o_ref[...] += p

    return pl.pallas_call(
        body, out_shape=jax.ShapeDtypeStruct((ka, nb), F32), grid=(ka // ta, nb // tb, nt),
        in_specs=[pl.BlockSpec((tt, ta), lambda i, j, s: (s, i)), pl.BlockSpec((tt, tb), lambda i, j, s: (s, j))],
        out_specs=pl.BlockSpec((ta, tb), lambda i, j, s: (i, j)),
        name=name, compiler_params=_cp(("parallel", "parallel", "arbitrary")))(a, b)


def _rms(x, g, n=None):
    n = x.shape[-1] if n is None else n
    r = lax.rsqrt(jnp.sum(x * x, axis=-1, keepdims=True) * (1.0 / n) + EPS)
    return x * r * g


def _silu(x):
    return x * jax.nn.sigmoid(x)


def _modulate(x, g, sc, sh):
    return _rms(x, g) * (1.0 + sc) + sh


def _swap(x, s):
    ax = x.ndim - 1
    w = x.shape[ax]
    lane = lax.broadcasted_iota(jnp.int32, x.shape, ax)
    lo = (lane & s) == 0
    return jnp.where(lo, pltpu.roll(x, w - s, ax), pltpu.roll(x, s, ax))


@functools.partial(jax.custom_vjp, nondiff_argnums=(3,))
def _rope(x, cos, sin, s):
    return x * cos + _swap(x, s) * sin


def _rope_fwd(x, cos, sin, s):
    return _rope(x, cos, sin, s), (cos, sin)


def _rope_bwd(s, res, g):
    cos, sin = res
    return g * cos - _swap(g, s) * sin, jnp.zeros_like(cos), jnp.zeros_like(sin)


_rope.defvjp(_rope_fwd, _rope_bwd)


@jax.custom_vjp
def _softplus(x):
    return jnp.maximum(x, 0.0) + jnp.log(1.0 + jnp.exp(-jnp.abs(x)))


def _softplus_fwd(x):
    return _softplus(x), x


def _softplus_bwd(x, g):
    return (g * jax.nn.sigmoid(x),)


_softplus.defvjp(_softplus_fwd, _softplus_bwd)


def _d(a, b, dims):
    return lax.dot_general(a.astype(BF16), b.astype(BF16), (dims, ((), ())), preferred_element_type=F32)


@jax.custom_vjp
def bdot(a, b):
    return _d(a, b, ((1,), (0,)))


bdot.defvjp(lambda a, b: (bdot(a, b), (a, b)),
            lambda r, g: (_d(g, r[1], ((1,), (1,))), _d(r[0], g, ((0,), (0,)))))


@jax.custom_vjp
def bdot_nt(a, b):
    return _d(a, b, ((1,), (1,)))


bdot_nt.defvjp(lambda a, b: (bdot_nt(a, b), (a, b)),
               lambda r, g: (_d(g, r[1], ((1,), (0,))), _d(g, r[0], ((0,), (0,)))))


@jax.custom_vjp
def bdot_tn(a, b):
    return _d(a, b, ((0,), (0,)))


bdot_tn.defvjp(lambda a, b: (bdot_tn(a, b), (a, b)),
               lambda r, g: (_d(r[1], g, ((1,), (1,))), _d(r[0], g, ((1,), (0,)))))


def _tri(rev):
    i = lax.broadcasted_iota(jnp.int32, (Q, Q), 0)
    j = lax.broadcasted_iota(jnp.int32, (Q, Q), 1)
    return (i <= j) if rev else (i >= j)


def _split3(a):
    hi = a.astype(BF16)
    r = a - hi.astype(F32)
    mid = r.astype(BF16)
    lo = (r - mid.astype(F32)).astype(BF16)
    return hi, mid, lo


def _cum_cols_impl(a, rev):
    t = _tri(rev).astype(BF16)
    return sum(jnp.dot(t, p, preferred_element_type=F32) for p in _split3(a))


def _cum_rows_impl(a, rev):
    t = _tri(not rev).astype(BF16)
    return sum(jnp.dot(p, t, preferred_element_type=F32) for p in _split3(a))


@functools.partial(jax.custom_vjp, nondiff_argnums=(1,))
def cum_cols(a, rev):
    return _cum_cols_impl(a, rev)


cum_cols.defvjp(lambda a, rev: (_cum_cols_impl(a, rev), None), lambda rev, _, g: (_cum_cols_impl(g, not rev),))


@functools.partial(jax.custom_vjp, nondiff_argnums=(1,))
def cum_rows(a, rev):
    return _cum_rows_impl(a, rev)


cum_rows.defvjp(lambda a, rev: (_cum_rows_impl(a, rev), None), lambda rev, _, g: (_cum_rows_impl(g, not rev),))


def _rs(w, cb=0):
    return pl.BlockSpec((RT, w), lambda i: (i, cb))


def _ps(shape):
    nd = len(shape)
    return pl.BlockSpec(shape, lambda i: (0,) * nd)


def _gs(w, cb, nlat):
    return pl.BlockSpec((1, 1, w), lambda i: (i // nlat, 0, cb))


def _rowcall(name, body, n, ins, outs, scratch=()):
    return pl.pallas_call(
        body, out_shape=[o[0] for o in outs], grid=(n // RT,), in_specs=[s for _, s in ins],
        out_specs=[s for _, s in outs], scratch_shapes=list(scratch), name=name,
        compiler_params=_cp(("arbitrary",)))(*[a for a, _ in ins])


def _acc(ref, val, first):
    @pl.when(first)
    def _():
        ref[...] = val

    @pl.when(jnp.logical_not(first))
    def _():
        ref[...] += val


def _sd(shape, dt):
    return jax.ShapeDtypeStruct(shape, dt)


def resid_mod_fwd(name, xp, o, mod_gt, gt_i, mod_n, sh_i, sc_i, norm_g, nlat):
    n = xp.shape[0]
    has_res = o is not None

    def body(*refs):
        if has_res:
            xp_ref, o_ref, gt_ref, sh_ref, sc_ref, g_ref, xn_ref, h_ref = refs
            xn = xp_ref[...] + gt_ref[0] * o_ref[...]
            xn_ref[...] = xn
        else:
            xp_ref, sh_ref, sc_ref, g_ref, h_ref = refs
            xn = xp_ref[...]
        h_ref[...] = _modulate(xn, g_ref[...], sc_ref[0], sh_ref[0]).astype(BF16)

    ins = [(xp, _rs(D))]
    if has_res:
        ins += [(o, _rs(D)), (mod_gt, _gs(D, gt_i, nlat))]
    ins += [(mod_n, _gs(D, sh_i, nlat)), (mod_n, _gs(D, sc_i, nlat)), (norm_g, _ps((1, D)))]
    outs = ([(_sd((n, D), F32), _rs(D))] if has_res else []) + [(_sd((n, D), BF16), _rs(D))]
    r = _rowcall(name, body, n, ins, outs)
    return (r[0], r[1]) if has_res else (xp, r[0])


def resid_mod_bwd(name, xn, dxn, dh, o, mod_gt, gt_i, mod_n, sh_i, sc_i, norm_g, nlat):
    n = xn.shape[0]
    has_res = o is not None

    def body(*refs):
        i = pl.program_id(0)
        if has_res:
            (xn_ref, dxn_ref, dh_ref, o_ref, gt_ref, sh_ref, sc_ref, g_ref,
             dx_ref, do_ref, dgt_ref, dsh_ref, dsc_ref, dg_ref) = refs
        else:
            xn_ref, dxn_ref, dh_ref, sh_ref, sc_ref, g_ref, dx_ref, dsh_ref, dsc_ref, dg_ref = refs
        _, vjp = jax.vjp(_modulate, xn_ref[...], g_ref[...], sc_ref[0], sh_ref[0])
        dx, dg, dsc, dsh = vjp(dh_ref[...])
        dx = dx + dxn_ref[...]
        dx_ref[...] = dx
        gfirst = (i == 0) | (i == nlat)
        _acc(dg_ref, dg, i == 0)
        _acc(dsh_ref, dsh[None], gfirst)
        _acc(dsc_ref, dsc[None], gfirst)
        if has_res:
            do_ref[...] = (gt_ref[0] * dx).astype(BF16)
            _acc(dgt_ref, jnp.sum(dx * o_ref[...], axis=0, keepdims=True)[None], gfirst)

    ins = [(xn, _rs(D)), (dxn, _rs(D)), (dh, _rs(D))]
    if has_res:
        ins += [(o, _rs(D)), (mod_gt, _gs(D, gt_i, nlat))]
    ins += [(mod_n, _gs(D, sh_i, nlat)), (mod_n, _gs(D, sc_i, nlat)), (norm_g, _ps((1, D)))]
    gacc = (_sd((2, 1, D), F32), _gs(D, 0, nlat))
    outs = [(_sd((n, D), F32), _rs(D))]
    if has_res:
        outs += [(_sd((n, D), BF16), _rs(D)), gacc]
    outs += [gacc, gacc, (_sd((1, D), F32), _ps((1, D)))]
    r = _rowcall(name, body, n, ins, outs)
    if has_res:
        return r
    return r[0], None, None, r[1], r[2], r[3]


def resid_loss(name, xp, o, mod_gt, gt_i, target, nlat):
    n = xp.shape[0]

    def body(xp_ref, o_ref, gt_ref, t_ref, loss_ref, dx_ref, do_ref, dgt_ref):
        i = pl.program_id(0)
        gt = gt_ref[0]

        @pl.when(i < nlat)
        def _():
            err = xp_ref[...] + gt * o_ref[...] - t_ref[...]
            dx = err * (1.0 / D)
            dx_ref[...] = dx
            do_ref[...] = (gt * dx).astype(BF16)
            _acc(loss_ref, jnp.full((1, 128), 0.5 / D, F32) * jnp.sum(err * err), i == 0)
            _acc(dgt_ref, jnp.sum(dx * o_ref[...], axis=0, keepdims=True)[None], i == 0)

        @pl.when(i >= nlat)
        def _():
            dx_ref[...] = jnp.zeros((RT, D), F32)
            do_ref[...] = jnp.zeros((RT, D), BF16)
            dgt_ref[...] = jnp.zeros((1, 1, D), F32)

    tgt_spec = pl.BlockSpec((RT, D), lambda i: (jnp.minimum(i, nlat - 1), 0))
    ins = [(xp, _rs(D)), (o, _rs(D)), (mod_gt, _gs(D, gt_i, nlat)), (target, tgt_spec)]
    outs = [(_sd((1, 128), F32), _ps((1, 128))), (_sd((n, D), F32), _rs(D)), (_sd((n, D), BF16), _rs(D)),
            (_sd((2, 1, D), F32), _gs(D, 0, nlat))]
    return _rowcall(name, body, n, ins, outs)


def mod_fwd(name, c8, w_mod, b_mod):
    tn = 1536

    def body(c_ref, w_ref, b_ref, o_ref, s_ref):
        s = _silu(c_ref[...]).astype(BF16)
        s_ref[...] = s
        o_ref[...] = jnp.dot(s, w_ref[...], preferred_element_type=F32) + b_ref[...]

    return pl.pallas_call(
        body, out_shape=[_sd((8, 6 * D), F32), _sd((8, D), BF16)], grid=(6 * D // tn,),
        in_specs=[pl.BlockSpec((8, D), lambda j: (0, 0)), pl.BlockSpec((D, tn), lambda j: (0, j)),
                  pl.BlockSpec((1, tn), lambda j: (0, j))],
        out_specs=[pl.BlockSpec((8, tn), lambda j: (0, j)), pl.BlockSpec((8, D), lambda j: (0, 0))],
        name=name, compiler_params=_cp(("arbitrary",)))(c8, w_mod, b_mod)


def mod_small_bwd(name, c8, dsilu, dmod8):
    def body(c_ref, ds_ref, dm_ref, dc_ref, db_ref):
        _, vjp = jax.vjp(_silu, c_ref[...])
        dc_ref[...] = vjp(ds_ref[...])[0]
        db_ref[...] = jnp.sum(dm_ref[...], axis=0, keepdims=True)

    return pl.pallas_call(
        body, out_shape=[_sd((8, D), F32), _sd((1, 6 * D), F32)], grid=(1,),
        in_specs=[pl.BlockSpec((8, D), lambda j: (0, 0)), pl.BlockSpec((8, D), lambda j: (0, 0)),
                  pl.BlockSpec((8, 6 * D), lambda j: (0, 0))],
        out_specs=[pl.BlockSpec((8, D), lambda j: (0, 0)), pl.BlockSpec((1, 6 * D), lambda j: (0, 0))],
        name=name, compiler_params=_cp(("arbitrary",)))(c8, dsilu, dmod8)


def _conv_taps(x, nlat):
    n = x.shape[0]
    r = lax.broadcasted_iota(jnp.int32, x.shape, 0)
    lo = jnp.where(r < nlat, 0, nlat)
    hi = jnp.where(r < nlat, nlat, n)
    taps = []
    for o in (-2, -1, 0, 1, 2):
        xs = x if o == 0 else pltpu.roll(x, (-o) % n, 0)
        t = r + o
        taps.append(jnp.where((t >= lo) & (t < hi), xs, 0.0))
    return taps


def conv_fwd(name, u, w, b, nlat_rows):
    n = u.shape[0]

    def body(x_ref, w_ref, b_ref, o_ref):
        taps = _conv_taps(x_ref[...], nlat_rows)
        wv = w_ref[...]
        pre = b_ref[...] + sum(taps[k] * wv[k:k + 1, :] for k in range(5))
        o_ref[...] = _silu(pre)

    return pl.pallas_call(
        body, out_shape=_sd((n, 1536), F32), grid=(12,),
        in_specs=[pl.BlockSpec((n, 128), lambda j: (0, C_XS // 128 + j)), pl.BlockSpec((5, 128), lambda j: (0, j)),
                  pl.BlockSpec((1, 128), lambda j: (0, j))],
        out_specs=pl.BlockSpec((n, 128), lambda j: (0, j)),
        name=name, compiler_params=_cp(("parallel",)))(u, w, b)


def conv_bwd(name, u, dact, w, b, nlat_rows):
    n = u.shape[0]

    def body(x_ref, da_ref, w_ref, b_ref, dx_ref, dw_ref, db_ref):
        taps = _conv_taps(x_ref[...], nlat_rows)
        wv = w_ref[...]
        pre = b_ref[...] + sum(taps[k] * wv[k:k + 1, :] for k in range(5))
        s = jax.nn.sigmoid(pre)
        dpre = da_ref[...] * (s * (1.0 + pre * (1.0 - s)))
        db_ref[...] = jnp.sum(dpre, axis=0, keepdims=True)
        rows = lax.broadcasted_iota(jnp.int32, (5, 128), 0)
        dw = jnp.zeros((5, 128), F32)
        for k in range(5):
            dw = dw + jnp.where(rows == k, jnp.sum(dpre * taps[k], axis=0, keepdims=True), 0.0)
        dw_ref[...] = dw
        r = lax.broadcasted_iota(jnp.int32, dpre.shape, 0)
        lo = jnp.where(r < nlat_rows, 0, nlat_rows)
        hi = jnp.where(r < nlat_rows, nlat_rows, n)
        dx = jnp.zeros_like(dpre)
        for k in range(5):
            o = k - 2
            ds = dpre if o == 0 else pltpu.roll(dpre, o % n, 0)
            t = r - o
            dx = dx + jnp.where((t >= lo) & (t < hi), ds, 0.0) * wv[k:k + 1, :]
        dx_ref[...] = dx.astype(BF16)

    return pl.pallas_call(
        body, out_shape=[_sd((n, 1536), BF16), _sd((5, 1536), F32), _sd((1, 1536), F32)], grid=(12,),
        in_specs=[pl.BlockSpec((n, 128), lambda j: (0, C_XS // 128 + j)), pl.BlockSpec((n, 128), lambda j: (0, j)),
                  pl.BlockSpec((5, 128), lambda j: (0, j)), pl.BlockSpec((1, 128), lambda j: (0, j))],
        out_specs=[pl.BlockSpec((n, 128), lambda j: (0, j)), pl.BlockSpec((5, 128), lambda j: (0, j)),
                   pl.BlockSpec((1, 128), lambda j: (0, j))],
        name=name, compiler_params=_cp(("parallel",)))(u, dact, w, b)


def _ssd_chunk(rev, dirn, g, x4, bm, cm, misc, dtrow, bias_c, alog_c, bias_r, alog_r, h4):
    dt_c = _softplus(misc + bias_c)
    a_c = dt_c * (-jnp.exp(alog_c))
    dt_r = _softplus(dtrow + bias_r)
    a_r = dt_r * (-jnp.exp(alog_r))
    cs_c = cum_cols(a_c, rev)
    cs_r = cum_rows(a_r, rev)
    tot_c = jnp.sum(a_c, axis=0, keepdims=True)
    cb = bdot_nt(cm, bm)
    tri = _tri(rev)
    lane = lax.broadcasted_iota(jnp.int32, (1, 128), 1)
    row16 = lax.broadcasted_iota(jnp.int32, (16, 1), 0)
    prow = lax.broadcasted_iota(jnp.int32, (128, 1), 0)
    ys, hs = [], []
    for p in range(4):
        ydiag = 0.0
        wst = 0.0
        eoff = 0.0
        hscale = 0.0
        for e in range(2):
            hg = 8 * g + 2 * p + e
            oh_c = (lane == DT_LANE + 16 * dirn + hg).astype(F32)
            dt_h = jnp.sum(dt_c * oh_c, axis=1, keepdims=True)
            cs_h = jnp.sum(cs_c * oh_c, axis=1, keepdims=True)
            tot_h = jnp.sum(tot_c * oh_c, axis=1, keepdims=True)
            csr_h = jnp.sum(cs_r * (row16 == hg).astype(F32), axis=0, keepdims=True)
            seg = jnp.exp(jnp.where(tri, cs_h - csr_h, -jnp.inf))
            hm = ((lane < 64) if e == 0 else (lane >= 64)).astype(F32)
            ydiag = ydiag + bdot(cb * seg, x4[p] * (dt_h * hm))
            wst = wst + (dt_h * jnp.exp(tot_h - cs_h)) * hm
            eoff = eoff + jnp.exp(cs_h) * hm
            hscale = hscale + jnp.exp(tot_h) * ((prow < 64) if e == 0 else (prow >= 64)).astype(F32)
        ys.append(ydiag + bdot_nt(cm, h4[p]) * eoff)
        hs.append(h4[p] * hscale + bdot_tn(x4[p] * wst, bm))
    return ys, hs


def _ssd_specs(nlat_chunks, rev, dirn, bwd):
    nc = nlat_chunks + 2

    def chunk(s):
        if bwd:
            s = nc - 1 - s
        return (nlat_chunks + 1 - s) if rev else (s + nlat_chunks) % nc

    def step(s):
        return (nc - 1 - s) if bwd else s

    return dict(
        x=pl.BlockSpec((Q, 512), lambda g, s: (chunk(s), g)),
        b=pl.BlockSpec((Q, 128), lambda g, s: (chunk(s), 8 + g)),
        c=pl.BlockSpec((Q, 128), lambda g, s: (chunk(s), 10 + g)),
        misc=pl.BlockSpec((Q, 128), lambda g, s: (chunk(s), C_MISC // 128)),
        dtrow=pl.BlockSpec((16, Q), lambda g, s: (dirn, chunk(s))),
        p_c=pl.BlockSpec((1, 128), lambda g, s: (0, 0)),
        p_r=pl.BlockSpec((16, 1), lambda g, s: (dirn, 0)),
        y=pl.BlockSpec((Q, 512), lambda g, s: (chunk(s), g)),
        hsave=pl.BlockSpec((1, 1, 512, 128), lambda g, s: (g, step(s), 0, 0)),
        bc_out=pl.BlockSpec((Q, 128), lambda g, s: (chunk(s), g)),
        misc_out=pl.BlockSpec((1, Q, 128), lambda g, s: (g, chunk(s), 0)),
        dtrow_out=pl.BlockSpec((1, 16, Q), lambda g, s: (g, 0, chunk(s))),
        pacc_c=pl.BlockSpec((1, 128), lambda g, s: (0, 0)),
        pacc_r=pl.BlockSpec((16, 1), lambda g, s: (0, 0)),
    )


def ssd_fwd(name, xbc, u, dtrow, bias_c, alog_c, bias_r, alog_r, nlat_chunks, rev, dirn):
    n = xbc.shape[0]
    nc = nlat_chunks + 2
    sp = _ssd_specs(nlat_chunks, rev, dirn, False)

    def body(x_ref, b_ref, c_ref, m_ref, r_ref, bc_ref, ac_ref, br_ref, ar_ref, y_ref, hs_ref, h_s):
        g = pl.program_id(0)
        s = pl.program_id(1)

        @pl.when(s == 0)
        def _():
            h_s[...] = jnp.zeros((512, 128), F32)

        hs_ref[0, 0] = h_s[...]
        x4 = [x_ref[:, 128 * p:128 * p + 128] for p in range(4)]
        h4 = [h_s[128 * p:128 * p + 128, :] for p in range(4)]
        ys, hs = _ssd_chunk(rev, dirn, g, x4, b_ref[...], c_ref[...], m_ref[...], r_ref[...],
                            bc_ref[...], ac_ref[...], br_ref[...], ar_ref[...], h4)
        for p in range(4):
            y_ref[:, 128 * p:128 * p + 128] = ys[p]
            h_s[128 * p:128 * p + 128, :] = hs[p]

    return pl.pallas_call(
        body, out_shape=[_sd((n, 1024), F32), _sd((2, nc, 512, 128), F32)], grid=(2, nc),
        in_specs=[sp["x"], sp["b"], sp["c"], sp["misc"], sp["dtrow"], sp["p_c"], sp["p_c"], sp["p_r"], sp["p_r"]],
        out_specs=[sp["y"], sp["hsave"]], scratch_shapes=[pltpu.VMEM((512, 128), F32)],
        name=name, compiler_params=_cp(("arbitrary", "arbitrary")))(
            xbc, xbc, xbc, u, dtrow, bias_c, alog_c, bias_r, alog_r)


def ssd_bwd(name, xbc, u, dtrow, bias_c, alog_c, bias_r, alog_r, hsave, dy, acc, nlat_chunks, rev, dirn):
    n = xbc.shape[0]
    sp = _ssd_specs(nlat_chunks, rev, dirn, True)

    def body(x_ref, b_ref, c_ref, m_ref, r_ref, bc_ref, ac_ref, br_ref, ar_ref, hs_ref, dy_ref, ax_ref, ab_ref, acc_ref,
             dx_ref, db_ref, dc_ref, dm_ref, dr_ref, dbc_ref, dac_ref, dbr_ref, dar_ref, dh_s):
        g = pl.program_id(0)
        s = pl.program_id(1)

        @pl.when(s == 0)
        def _():
            dh_s[...] = jnp.zeros((512, 128), F32)

        x4 = [x_ref[:, 128 * p:128 * p + 128] for p in range(4)]
        h4 = [hs_ref[0, 0, 128 * p:128 * p + 128, :] for p in range(4)]
        fn = functools.partial(_ssd_chunk, rev, dirn, g)
        _, vjp = jax.vjp(fn, x4, b_ref[...], c_ref[...], m_ref[...], r_ref[...],
                         bc_ref[...], ac_ref[...], br_ref[...], ar_ref[...], h4)
        dys = [dy_ref[:, 128 * p:128 * p + 128] for p in range(4)]
        dhs = [dh_s[128 * p:128 * p + 128, :] for p in range(4)]
        dx4, db, dc, dm, dr, dbc, dac, dbr, dar, dh4 = vjp((dys, dhs))
        for p in range(4):
            dx_ref[:, 128 * p:128 * p + 128] = dx4[p] + ax_ref[:, 128 * p:128 * p + 128]
            dh_s[128 * p:128 * p + 128, :] = dh4[p]
        db_ref[...] = db + ab_ref[...]
        dc_ref[...] = dc + acc_ref[...]
        dm_ref[0] = dm
        dr_ref[0] = dr
        first = (g == 0) & (s == 0)
        _acc(dbc_ref, dbc, first)
        _acc(dac_ref, dac, first)
        _acc(dbr_ref, dbr, first)
        _acc(dar_ref, dar, first)

    ax, ab, ac = acc
    return pl.pallas_call(
        body,
        out_shape=[_sd((n, 1024), F32), _sd((n, 256), F32), _sd((n, 256), F32), _sd((2, n, 128), F32),
                   _sd((2, 16, n), F32), _sd((1, 128), F32), _sd((1, 128), F32), _sd((16, 1), F32), _sd((16, 1), F32)],
        grid=(2, nlat_chunks + 2),
        in_specs=[sp["x"], sp["b"], sp["c"], sp["misc"], sp["dtrow"], sp["p_c"], sp["p_c"], sp["p_r"], sp["p_r"],
                  sp["hsave"], sp["y"], sp["y"], sp["bc_out"], sp["bc_out"]],
        out_specs=[sp["y"], sp["bc_out"], sp["bc_out"], sp["misc_out"], sp["dtrow_out"],
                   sp["pacc_c"], sp["pacc_c"], sp["pacc_r"], sp["pacc_r"]],
        scratch_shapes=[pltpu.VMEM((512, 128), F32)],
        name=name, compiler_params=_cp(("arbitrary", "arbitrary")))(
            xbc, xbc, xbc, u, dtrow, bias_c, alog_c, bias_r, alog_r, hsave, dy, ax, ab, ac)


def _ssd_out(yf, yb, xs, z, g, dexp):
    return _rms((yf + yb + dexp * xs) * _silu(z), g)


def ssd_out_fwd(name, yf, yb, xbc, u, g, dexp):
    n = yf.shape[0]

    def body(yf_ref, yb_ref, xs_ref, z_ref, g_ref, d_ref, o_ref):
        o_ref[...] = _ssd_out(yf_ref[...], yb_ref[...], xs_ref[...], z_ref[...], g_ref[...], d_ref[...]).astype(BF16)

    return _rowcall(name, body, n,
                    [(yf, _rs(D)), (yb, _rs(D)), (xbc, _rs(D, 0)), (u, _rs(D, C_Z // D)), (g, _ps((1, D))), (dexp, _ps((1, D)))],
                    [(_sd((n, D), BF16), _rs(D))])[0]


def ssd_out_bwd(name, yf, yb, xbc, u, g, dexp, dys):
    n = yf.shape[0]

    def body(yf_ref, yb_ref, xs_ref, z_ref, g_ref, d_ref, dys_ref, dy_ref, dxs_ref, dz_ref, dg_ref, dd_ref):
        i = pl.program_id(0)
        _, vjp = jax.vjp(_ssd_out, yf_ref[...], yb_ref[...], xs_ref[...], z_ref[...], g_ref[...], d_ref[...])
        dyf, _, dxs, dz, dg, dd = vjp(dys_ref[...])
        dy_ref[...] = dyf
        dxs_ref[...] = dxs
        dz_ref[...] = dz.astype(BF16)
        _acc(dg_ref, dg, i == 0)
        _acc(dd_ref, dd, i == 0)

    return _rowcall(name, body, n,
                    [(yf, _rs(D)), (yb, _rs(D)), (xbc, _rs(D, 0)), (u, _rs(D, C_Z // D)), (g, _ps((1, D))), (dexp, _ps((1, D))),
                     (dys, _rs(D))],
                    [(_sd((n, D), F32), _rs(D)), (_sd((n, D), F32), _rs(D)), (_sd((n, D), BF16), _rs(D)),
                     (_sd((1, D), F32), _ps((1, D))), (_sd((1, D), F32), _ps((1, D)))])


def _normrope(x, g, cos, sin, s, n=None):
    return _rope(_rms(x, g, n), cos, sin, s)


def swa_prep_fwd(name, u, gq, gk, cos, sin):
    n = u.shape[0]

    def body(q_ref, k_ref, gq_ref, gk_ref, cos_ref, sin_ref, qs_ref, ks_ref):
        cs, sn = cos_ref[...], sin_ref[...]
        for h in range(SWA_HQ):
            sl = slice(128 * h, 128 * h + 128)
            qs_ref[:, sl] = _normrope(q_ref[:, sl], gq_ref[...], cs, sn, 32).astype(BF16)
        for h in range(SWA_HKV):
            sl = slice(128 * h, 128 * h + 128)
            ks_ref[:, sl] = _normrope(k_ref[:, sl], gk_ref[...], cs, sn, 32).astype(BF16)

    return _rowcall(name, body, n,
                    [(u, _rs(1024, C_Q // 1024)), (u, _rs(256, C_K // 256)), (gq, _ps((1, 128))), (gk, _ps((1, 128))),
                     (cos, _rs(128)), (sin, _rs(128))],
                    [(_sd((n, 1024), BF16), _rs(1024)), (_sd((n, 256), BF16), _rs(256))])


def swa_prep_bwd(name, u, gq, gk, cos, sin, dqs, dks, dv):
    n = u.shape[0]

    def body(q_ref, k_ref, gq_ref, gk_ref, cos_ref, sin_ref, dqs_ref, dks_ref, dv_ref,
             dq_ref, dk_ref, dvo_ref, dgq_ref, dgk_ref):
        i = pl.program_id(0)
        cs, sn = cos_ref[...], sin_ref[...]
        fn = lambda x, g: _normrope(x, g, cs, sn, 32)
        dgq = jnp.zeros((1, 128), F32)
        dgk = jnp.zeros((1, 128), F32)
        for h in range(SWA_HQ):
            sl = slice(128 * h, 128 * h + 128)
            _, vjp = jax.vjp(fn, q_ref[:, sl], gq_ref[...])
            dx, dg = vjp(dqs_ref[:, sl])
            dq_ref[:, sl] = dx.astype(BF16)
            dgq = dgq + dg
        for h in range(SWA_HKV):
            sl = slice(128 * h, 128 * h + 128)
            _, vjp = jax.vjp(fn, k_ref[:, sl], gk_ref[...])
            dx, dg = vjp(dks_ref[:, sl])
            dk_ref[:, sl] = dx.astype(BF16)
            dgk = dgk + dg
        dvo_ref[...] = dv_ref[...].astype(BF16)
        _acc(dgq_ref, dgq, i == 0)
        _acc(dgk_ref, dgk, i == 0)

    return _rowcall(name, body, n,
                    [(u, _rs(1024, C_Q // 1024)), (u, _rs(256, C_K // 256)), (gq, _ps((1, 128))), (gk, _ps((1, 128))),
                     (cos, _rs(128)), (sin, _rs(128)), (dqs, _rs(1024)), (dks, _rs(256)), (dv, _rs(256))],
                    [(_sd((n, 1024), BF16), _rs(1024)), (_sd((n, 256), BF16), _rs(256)), (_sd((n, 256), BF16), _rs(256)),
                     (_sd((1, 128), F32), _ps((1, 128))), (_sd((1, 128), F32), _ps((1, 128)))])


def lat_norm_fwd(name, u, g_kv, g_q):
    n = u.shape[0]

    def body(ckv_ref, cq_ref, gkv_ref, gq_ref, okv_ref, oq_ref):
        okv_ref[...] = _rms(ckv_ref[...], gkv_ref[...]).astype(BF16)
        oq_ref[...] = _rms(cq_ref[...], gq_ref[...]).astype(BF16)

    return _rowcall(name, body, n,
                    [(u, _rs(256, C_CKV // 256)), (u, _rs(384, C_CQ // 384)), (g_kv, _ps((1, 256))), (g_q, _ps((1, 384)))],
                    [(_sd((n, 256), BF16), _rs(256)), (_sd((n, 384), BF16), _rs(384))])


def lat_norm_bwd(name, u, g_kv, g_q, dkvn, dqn):
    n = u.shape[0]

    def body(ckv_ref, cq_ref, gkv_ref, gq_ref, dkvn_ref, dqn_ref, dckv_ref, dcq_ref, dgkv_ref, dgq_ref):
        i = pl.program_id(0)
        _, vjp = jax.vjp(_rms, ckv_ref[...], gkv_ref[...])
        dx, dg = vjp(dkvn_ref[...])
        dckv_ref[...] = dx.astype(BF16)
        _acc(dgkv_ref, dg, i == 0)
        _, vjp = jax.vjp(_rms, cq_ref[...], gq_ref[...])
        dx, dg = vjp(dqn_ref[...])
        dcq_ref[...] = dx.astype(BF16)
        _acc(dgq_ref, dg, i == 0)

    return _rowcall(name, body, n,
                    [(u, _rs(256, C_CKV // 256)), (u, _rs(384, C_CQ // 384)), (g_kv, _ps((1, 256))), (g_q, _ps((1, 384))),
                     (dkvn, _rs(256)), (dqn, _rs(384))],
                    [(_sd((n, 256), BF16), _rs(256)), (_sd((n, 384), BF16), _rs(384)),
                     (_sd((1, 256), F32), _ps((1, 256))), (_sd((1, 384), F32), _ps((1, 384)))])


def _lane_lt64(x):
    return (lax.broadcasted_iota(jnp.int32, (1, 128), 1) < 64).astype(F32) * x


def _mla_krope(misc, g, cos, sin):
    return _normrope(_lane_lt64(misc), g, cos, sin, 16, MLA_ROPE)


def mla_prep_fwd(name, kv, qp, u, qg, kg, cos, sin):
    n = kv.shape[0]

    def body(kv_ref, v_ref, q_ref, m_ref, qg_ref, kg_ref, cos_ref, sin_ref, km_ref, qm_ref, vm_ref):
        cs, sn = cos_ref[...], sin_ref[...]
        vm_ref[...] = v_ref[...].astype(BF16)
        kr = _mla_krope(m_ref[...], kg_ref[:, 128:256], cs, sn).astype(BF16)
        for h in range(MLA_H):
            km_ref[:, 256 * h:256 * h + 128] = _rms(kv_ref[:, 128 * h:128 * h + 128], kg_ref[:, 0:128]).astype(BF16)
            km_ref[:, 256 * h + 128:256 * h + 256] = kr
            qm_ref[:, 256 * h:256 * h + 128] = _rms(q_ref[:, 256 * h:256 * h + 128], qg_ref[:, 0:128]).astype(BF16)
            qm_ref[:, 256 * h + 128:256 * h + 256] = _normrope(
                q_ref[:, 256 * h + 128:256 * h + 256], qg_ref[:, 128:256], cs, sn, 16, MLA_ROPE).astype(BF16)

    return _rowcall(name, body, n,
                    [(kv, _rs(1024, 0)), (kv, _rs(1024, 1)), (qp, _rs(2048)), (u, _rs(128, C_MISC // 128)), (qg, _ps((1, 256))),
                     (kg, _ps((1, 256))), (cos, _rs(128)), (sin, _rs(128))],
                    [(_sd((n, 2048), BF16), _rs(2048)), (_sd((n, 2048), BF16), _rs(2048)), (_sd((n, 1024), BF16), _rs(1024))])


def mla_prep_bwd(name, kv, qp, u, qg, kg, cos, sin, dkm, dqm, dv):
    n = kv.shape[0]

    def body(kv_ref, q_ref, m_ref, qg_ref, kg_ref, cos_ref, sin_ref, dkm_ref, dqm_ref, dv_ref,
             dkv_ref, dq_ref, dkr_ref, dqg_ref, dkg_ref):
        i = pl.program_id(0)
        cs, sn = cos_ref[...], sin_ref[...]
        fr = lambda x, g: _normrope(x, g, cs, sn, 16, MLA_ROPE)
        dkg_n = jnp.zeros((1, 128), F32)
        dqg_n = jnp.zeros((1, 128), F32)
        dqg_r = jnp.zeros((1, 128), F32)
        dkr_sum = jnp.zeros((RT, 128), F32)
        for h in range(MLA_H):
            _, vjp = jax.vjp(_rms, kv_ref[:, 128 * h:128 * h + 128], kg_ref[:, 0:128])
            dx, dg = vjp(dkm_ref[:, 256 * h:256 * h + 128])
            dkv_ref[:, 128 * h:128 * h + 128] = dx.astype(BF16)
            dkg_n = dkg_n + dg
            dkr_sum = dkr_sum + dkm_ref[:, 256 * h + 128:256 * h + 256]
            _, vjp = jax.vjp(_rms, q_ref[:, 256 * h:256 * h + 128], qg_ref[:, 0:128])
            dx, dg = vjp(dqm_ref[:, 256 * h:256 * h + 128])
            dq_ref[:, 256 * h:256 * h + 128] = dx.astype(BF16)
            dqg_n = dqg_n + dg
            _, vjp = jax.vjp(fr, q_ref[:, 256 * h + 128:256 * h + 256], qg_ref[:, 128:256])
            dx, dg = vjp(dqm_ref[:, 256 * h + 128:256 * h + 256])
            dq_ref[:, 256 * h + 128:256 * h + 256] = dx.astype(BF16)
            dqg_r = dqg_r + dg
        _, vjp = jax.vjp(lambda m, g: _mla_krope(m, g, cs, sn), m_ref[...], kg_ref[:, 128:256])
        dm, dkg_r = vjp(dkr_sum)
        dkr_ref[...] = dm
        dkv_ref[:, 1024:2048] = dv_ref[...].astype(BF16)
        _acc(dqg_ref.at[:, 0:128], dqg_n, i == 0)
        _acc(dqg_ref.at[:, 128:256], dqg_r, i == 0)
        _acc(dkg_ref.at[:, 0:128], dkg_n, i == 0)
        _acc(dkg_ref.at[:, 128:256], dkg_r, i == 0)

    return _rowcall(name, body, n,
                    [(kv, _rs(1024, 0)), (qp, _rs(2048)), (u, _rs(128, C_MISC // 128)), (qg, _ps((1, 256))), (kg, _ps((1, 256))),
                     (cos, _rs(128)), (sin, _rs(128)), (dkm, _rs(2048)), (dqm, _rs(2048)), (dv, _rs(1024))],
                    [(_sd((n, 2048), BF16), _rs(2048)), (_sd((n, 2048), BF16), _rs(2048)), (_sd((n, 128), F32), _rs(128)),
                     (_sd((1, 256), F32), _ps((1, 256))), (_sd((1, 256), F32), _ps((1, 256)))])


def misc_combine(name, dkr, dm_f, dm_b, drow_t):
    n = dkr.shape[0]

    def body(a_ref, f_ref, b_ref, r_ref, o_ref):
        o_ref[...] = (a_ref[...] + f_ref[0] + f_ref[1] + b_ref[0] + b_ref[1] + r_ref[...]).astype(BF16)

    g2 = pl.BlockSpec((2, RT, 128), lambda i: (0, i, 0))
    return _rowcall(name, body, n, [(dkr, _rs(128)), (dm_f, g2), (dm_b, g2), (drow_t, _rs(128))],
                    [(_sd((n, 128), BF16), _rs(128))])[0]


def _merge(g1, g2, g3, p1, p2, p3):
    return jax.nn.sigmoid(g1) * p1 + jax.nn.sigmoid(g2) * p2 + jax.nn.sigmoid(g3) * p3


def merge_fwd(name, u, p1, p2, p3):
    n = u.shape[0]

    def body(g1, g2, g3, a, b, c, o_ref):
        o_ref[...] = _merge(g1[...], g2[...], g3[...], a[...], b[...], c[...]).astype(BF16)

    return _rowcall(name, body, n, [(u, _rs(D, 0)), (u, _rs(D, 1)), (u, _rs(D, 2)), (p1, _rs(D)), (p2, _rs(D)), (p3, _rs(D))],
                    [(_sd((n, D), BF16), _rs(D))])[0]


def merge_bwd(name, u, p1, p2, p3, dm):
    n = u.shape[0]

    def body(g1, g2, g3, a, b, c, dm_ref, d1, d2, d3, dg_ref):
        _, vjp = jax.vjp(_merge, g1[...], g2[...], g3[...], a[...], b[...], c[...])
        r = vjp(dm_ref[...])
        for k in range(3):
            dg_ref[:, D * k:D * k + D] = r[k].astype(BF16)
        d1[...] = r[3].astype(BF16)
        d2[...] = r[4].astype(BF16)
        d3[...] = r[5].astype(BF16)

    return _rowcall(name, body, n,
                    [(u, _rs(D, 0)), (u, _rs(D, 1)), (u, _rs(D, 2)), (p1, _rs(D)), (p2, _rs(D)), (p3, _rs(D)), (dm, _rs(D))],
                    [(_sd((n, D), BF16), _rs(D))] * 3 + [(_sd((n, 3 * D), BF16), _rs(3 * D))])


def _swiglu(g, u):
    return _silu(g) * u


def swiglu_fwd(name, gu):
    n = gu.shape[0]

    def body(g_ref, u_ref, o_ref):
        o_ref[...] = _swiglu(g_ref[...], u_ref[...]).astype(BF16)

    return _rowcall(name, body, n, [(gu, _rs(FFN, 0)), (gu, _rs(FFN, 1))], [(_sd((n, FFN), BF16), _rs(FFN))])[0]


def swiglu_bwd(name, gu, da):
    n = gu.shape[0]

    def body(g_ref, u_ref, da_ref, o_ref):
        _, vjp = jax.vjp(_swiglu, g_ref[...], u_ref[...])
        dg, du = vjp(da_ref[...])
        o_ref[:, 0:FFN] = dg.astype(BF16)
        o_ref[:, FFN:2 * FFN] = du.astype(BF16)

    return _rowcall(name, body, n, [(gu, _rs(FFN, 0)), (gu, _rs(FFN, 1)), (da, _rs(FFN))],
                    [(_sd((n, 2 * FFN), BF16), _rs(2 * FFN))])[0]


FLASH_ROWS = 256


def _fold_lanes(x, op):
    acc = x[:, 0:128]
    for b in range(1, x.shape[1] // 128):
        acc = op(acc, x[:, 128 * b:128 * b + 128])
    return acc


def _band_mask(tq, tk, i, kb):
    qp = i * tq + lax.broadcasted_iota(jnp.int32, (tq, tk), 0)
    kp = kb * tk + lax.broadcasted_iota(jnp.int32, (tq, tk), 1)
    return jnp.abs(qp - kp) <= SWA_WIN


def flash_fwd(name, qa, ka, va, *, w, vw, hq, grp, vcol0, scale, nlat, tq, tk, band, sink, ctx_q, prev=None):
    n = qa.shape[0]
    cblk = nlat // NCTX
    band = band and not ctx_q
    assert not band, "latent rows of a banded attention go through swa_fwd_lat"
    if ctx_q:
        tq = tk = NCTX
        grid = (hq, 1, 1)
        qmap = lambda h, i, kk: (cblk, h)
        kmap = lambda h, i, kk: (cblk, h // grp)
        vmap = lambda h, i, kk: (cblk, vcol0 + h // grp)
        omap = lambda h, i, kk: (cblk, h)
        lmap = lambda h, i, kk: (h, cblk, 0)
    else:
        nb = nlat // tk
        nk = 3 if band else nb
        grid = (hq, nlat // tq, nk)
        kb_of = (lambda i, kk: jnp.clip(i + kk - 1, 0, nb - 1)) if band else (lambda i, kk: kk)
        qmap = lambda h, i, kk: (i, h)
        kmap = lambda h, i, kk: (kb_of(i, kk), h // grp)
        vmap = lambda h, i, kk: (kb_of(i, kk), vcol0 + h // grp)
        omap = lambda h, i, kk: (i, h)
        lmap = lambda h, i, kk: (h, i, 0)
    nk = grid[2]
    extra = not ctx_q
    has_sink = sink is not None

    def body(*refs):
        refs = list(refs)
        q_ref, k_ref, v_ref = refs[:3]
        pos = 3
        if extra:
            ke_ref, ve_ref = refs[pos:pos + 2]
            pos += 2
        if has_sink:
            s_ref = refs[pos]
            pos += 1
        if prev is not None:
            pos += 2
        o_ref, l_ref, m_s, l_s, a_s = refs[pos:pos + 5]
        kk = pl.program_id(2)
        tr = min(tq, FLASH_ROWS)

        def step(kblk, vblk):
            for r in range(tq // tr):
                rows = slice(r * tr, (r + 1) * tr)
                s = _d(q_ref[rows, :], kblk, ((1,), (1,))) * (scale * LOG2E)
                m_prev = m_s[rows, :]
                m_new = jnp.maximum(m_prev, jnp.max(_fold_lanes(s, jnp.maximum), axis=1, keepdims=True))
                alpha = jnp.exp2(m_prev - m_new)
                p = jnp.exp2(s - m_new)
                l_s[rows, :] = alpha * l_s[rows, :] + _fold_lanes(p, jnp.add)
                a_s[rows, :] = alpha * a_s[rows, :] + _d(p, vblk, ((1,), (0,)))
                m_s[rows, :] = m_new

        @pl.when(kk == 0)
        def _():
            if has_sink:
                sv = jnp.max(s_ref[0], axis=1, keepdims=True) * LOG2E
                m_s[...] = jnp.zeros((tq, 1), F32) + sv
                l_s[...] = (lax.broadcasted_iota(jnp.int32, (tq, 128), 1) == 0).astype(F32)
            else:
                m_s[...] = jnp.full((tq, 1), NEG, F32)
                l_s[...] = jnp.zeros((tq, 128), F32)
            a_s[...] = jnp.zeros((tq, vw), F32)
            if extra:
                step(ke_ref[...], ve_ref[...])

        step(k_ref[...], v_ref[...])

        @pl.when(kk == nk - 1)
        def _():
            l = jnp.sum(l_s[...], axis=1, keepdims=True)
            o_ref[...] = (a_s[...] / l).astype(BF16)
            l_ref[0] = m_s[...] + jnp.log2(l)

    ins = [(qa, pl.BlockSpec((tq, w), qmap)), (ka, pl.BlockSpec((tk, w), kmap)), (va, pl.BlockSpec((tk, vw), vmap))]
    if extra:
        ins += [(ka, pl.BlockSpec((NCTX, w), lambda h, i, kk: (cblk, h // grp))),
                (va, pl.BlockSpec((NCTX, vw), lambda h, i, kk: (cblk, vcol0 + h // grp)))]
    if has_sink:
        ins += [(sink, pl.BlockSpec((1, 1, 128), lambda h, i, kk: (h, 0, 0)))]
    aliases = {}
    if prev is not None:
        any_spec = pl.BlockSpec(memory_space=pl.ANY)
        aliases = {len(ins): 0, len(ins) + 1: 1}
        ins += [(prev[0], any_spec), (prev[1], any_spec)]
    return pl.pallas_call(
        body, out_shape=[_sd((n, hq * vw), BF16), _sd((hq, n, 1), F32)], grid=grid,
        in_specs=[s for _, s in ins],
        out_specs=[pl.BlockSpec((tq, vw), omap), pl.BlockSpec((1, tq, 1), lmap)],
        scratch_shapes=[pltpu.VMEM((tq, 1), F32), pltpu.VMEM((tq, 128), F32), pltpu.VMEM((tq, vw), F32)],
        input_output_aliases=aliases, name=name,
        compiler_params=_cp(("parallel", "parallel", "arbitrary")))(*[a for a, _ in ins])


def flash_dq(name, qa, ka, va, oa, doa, lse, *, w, vw, hq, grp, vcol0, scale, nlat, tq, tk, band, sink, ctx_q, prev=None):
    n = qa.shape[0]
    cblk = nlat // NCTX
    band = band and not ctx_q
    if ctx_q:
        tq = tk = NCTX
        grid = (hq, 1, 1)
        qmap = lambda h, i, kk: (cblk, h)
        kmap = lambda h, i, kk: (cblk, h // grp)
        vmap = lambda h, i, kk: (cblk, vcol0 + h // grp)
        lmap = lambda h, i, kk: (h, cblk, 0)
    else:
        nb = nlat // tk
        grid = (hq, nlat // tq, 3 if band else nb)
        kb_of = (lambda i, kk: jnp.clip(i + kk - 1, 0, nb - 1)) if band else (lambda i, kk: kk)
        qmap = lambda h, i, kk: (i, h)
        kmap = lambda h, i, kk: (kb_of(i, kk), h // grp)
        vmap = lambda h, i, kk: (kb_of(i, kk), vcol0 + h // grp)
        lmap = lambda h, i, kk: (h, i, 0)
    nk = grid[2]
    nq = grid[1]
    extra = not ctx_q
    has_sink = sink is not None

    def body(*refs):
        refs = list(refs)
        q_ref, k_ref, v_ref, o_ref, do_ref, l_ref = refs[:6]
        pos = 6
        if extra:
            ke_ref, ve_ref = refs[pos:pos + 2]
            pos += 2
        if has_sink:
            s_ref = refs[pos]
            pos += 1
        if prev is not None:
            pos += 2
        dq_ref, dl_ref, ds_ref, acc_s, dl_s = refs[pos:pos + 5]
        i = pl.program_id(1)
        kk = pl.program_id(2)
        q = q_ref[...]
        do = do_ref[...]
        lse_v = l_ref[0]

        def step(kblk, vblk, mask):
            s = _d(q, kblk, ((1,), (1,))) * (scale * LOG2E)
            if mask is not None:
                s = jnp.where(mask, s, NEG)
            p = jnp.exp2(s - lse_v)
            dp = _d(do, vblk, ((1,), (1,)))
            ds = p * (dp - dl_s[...]) * scale
            acc_s[...] += _d(ds, kblk, ((1,), (0,)))

        @pl.when(kk == 0)
        def _():
            delta = jnp.sum(do * o_ref[...].astype(F32), axis=1, keepdims=True)
            dl_s[...] = delta
            acc_s[...] = jnp.zeros((tq, w), F32)
            if has_sink:
                sv = jnp.max(s_ref[0], axis=1, keepdims=True) * LOG2E
                dsk = jnp.sum(-jnp.exp2(sv - lse_v) * delta, axis=0, keepdims=True)
                _acc(ds_ref, jnp.zeros((1, 1, 128), F32) + dsk, i == 0)
            else:
                ds_ref[...] = jnp.zeros((1, 1, 128), F32)
            if extra:
                step(ke_ref[...], ve_ref[...], None)

        if band:
            kb = i + kk - 1

            @pl.when((kb >= 0) & (kb < nlat // tk))
            def _():
                step(k_ref[...], v_ref[...], _band_mask(tq, tk, i, kb))
        else:
            step(k_ref[...], v_ref[...], None)

        @pl.when(kk == nk - 1)
        def _():
            dq_ref[...] = acc_s[...]
            dl_ref[0] = dl_s[...]

    ins = [(qa, pl.BlockSpec((tq, w), qmap)), (ka, pl.BlockSpec((tk, w), kmap)), (va, pl.BlockSpec((tk, vw), vmap)),
           (oa, pl.BlockSpec((tq, vw), qmap)), (doa, pl.BlockSpec((tq, vw), qmap)), (lse, pl.BlockSpec((1, tq, 1), lmap))]
    if extra:
        ins += [(ka, pl.BlockSpec((NCTX, w), lambda h, i, kk: (cblk, h // grp))),
                (va, pl.BlockSpec((NCTX, vw), lambda h, i, kk: (cblk, vcol0 + h // grp)))]
    if has_sink:
        ins += [(sink, pl.BlockSpec((1, 1, 128), lambda h, i, kk: (h, 0, 0)))]
    aliases = {}
    if prev is not None:
        any_spec = pl.BlockSpec(memory_space=pl.ANY)
        aliases = {len(ins): 0, len(ins) + 1: 1}
        ins += [(prev[0], any_spec), (prev[1], any_spec)]
    del nq
    return pl.pallas_call(
        body, out_shape=[_sd((n, hq * w), F32), _sd((hq, n, 1), F32), _sd((hq, 1, 128), F32)], grid=grid,
        in_specs=[s for _, s in ins],
        out_specs=[pl.BlockSpec((tq, w), qmap), pl.BlockSpec((1, tq, 1), lmap),
                   pl.BlockSpec((1, 1, 128), lambda h, i, kk: (h, 0, 0))],
        scratch_shapes=[pltpu.VMEM((tq, w), F32), pltpu.VMEM((tq, 1), F32)],
        input_output_aliases=aliases, name=name,
        compiler_params=_cp(("parallel", "arbitrary", "arbitrary")))(*[a for a, _ in ins])


def flash_dkv(name, qa, ka, va, doa, lse, delta, *, w, vw, hkv, grp, vcol0, scale, nlat, tq, tk, band, ctx_k, prev=None):
    n = qa.shape[0]
    cblk = nlat // NCTX
    nqb = nlat // tq
    band = band and not ctx_k
    if ctx_k:
        tk = NCTX
        nqs = nqb
        grid = (hkv, 1, grp * nqs)
        kmap = lambda hk, j, t: (cblk, hk)
        vmap = lambda hk, j, t: (cblk, vcol0 + hk)
        dvmap = lambda hk, j, t: (cblk, hk)
        qb_of = lambda j, t: t % nqs
    else:
        nqs = 3 if band else nqb
        grid = (hkv, nlat // tk, grp * nqs)
        kmap = lambda hk, j, t: (j, hk)
        vmap = lambda hk, j, t: (j, vcol0 + hk)
        dvmap = lambda hk, j, t: (j, hk)
        qb_of = (lambda j, t: jnp.clip(j + t % nqs - 1, 0, nqb - 1)) if band else (lambda j, t: t % nqs)
    qmap = lambda hk, j, t: (qb_of(j, t), hk * grp + t // nqs)
    lmap = lambda hk, j, t: (hk * grp + t // nqs, qb_of(j, t), 0)

    def body(*refs):
        refs = list(refs)
        q_ref, k_ref, v_ref, do_ref, l_ref, dl_ref = refs[:6]
        pos = 6
        if ctx_k:
            qe_ref, doe_ref, le_ref, dle_ref = refs[pos:pos + 4]
            pos += 4
        if prev is not None:
            pos += 2
        dk_ref, dv_ref = refs[pos:pos + 2]
        j = pl.program_id(1)
        t = pl.program_id(2)
        kblk = k_ref[...]
        vblk = v_ref[...]

        def contrib(q, do, lse_v, dl_v, mask):
            s = _d(q, kblk, ((1,), (1,))) * (scale * LOG2E)
            if mask is not None:
                s = jnp.where(mask, s, NEG)
            p = jnp.exp2(s - lse_v)
            dp = _d(do, vblk, ((1,), (1,)))
            ds = p * (dp - dl_v) * scale
            return _d(ds, q, ((0,), (0,))), _d(p, do, ((0,), (0,)))

        @pl.when(t == 0)
        def _():
            dk = jnp.zeros((tk, w), F32)
            dv = jnp.zeros((tk, vw), F32)
            if ctx_k:
                for gi in range(grp):
                    a, b = contrib(qe_ref[:, w * gi:w * gi + w], doe_ref[:, vw * gi:vw * gi + vw], le_ref[gi], dle_ref[gi], None)
                    dk = dk + a
                    dv = dv + b
            dk_ref[...] = dk
            dv_ref[...] = dv

        def add(mask):
            a, b = contrib(q_ref[...], do_ref[...], l_ref[0], dl_ref[0], mask)
            dk_ref[...] += a
            dv_ref[...] += b

        if band:
            qb = j + t % nqs - 1

            @pl.when((qb >= 0) & (qb < nqb))
            def _():
                add(_band_mask(tq, tk, qb, j))
        else:
            add(None)

    ins = [(qa, pl.BlockSpec((tq, w), qmap)), (ka, pl.BlockSpec((tk, w), kmap)), (va, pl.BlockSpec((tk, vw), vmap)),
           (doa, pl.BlockSpec((tq, vw), qmap)), (lse, pl.BlockSpec((1, tq, 1), lmap)), (delta, pl.BlockSpec((1, tq, 1), lmap))]
    if ctx_k:
        ins += [(qa, pl.BlockSpec((NCTX, grp * w), lambda hk, j, t: (cblk, hk))),
                (doa, pl.BlockSpec((NCTX, grp * vw), lambda hk, j, t: (cblk, hk))),
                (lse, pl.BlockSpec((grp, NCTX, 1), lambda hk, j, t: (hk, cblk, 0))),
                (delta, pl.BlockSpec((grp, NCTX, 1), lambda hk, j, t: (hk, cblk, 0)))]
    aliases = {}
    if prev is not None:
        any_spec = pl.BlockSpec(memory_space=pl.ANY)
        aliases = {len(ins): 0, len(ins) + 1: 1}
        ins += [(prev[0], any_spec), (prev[1], any_spec)]
    return pl.pallas_call(
        body, out_shape=[_sd((n, hkv * w), F32), _sd((n, hkv * vw), F32)], grid=grid,
        in_specs=[s for _, s in ins],
        out_specs=[pl.BlockSpec((tk, w), kmap), pl.BlockSpec((tk, vw), dvmap)],
        input_output_aliases=aliases, name=name,
        compiler_params=_cp(("parallel", "parallel", "arbitrary")))(*[a for a, _ in ins])


def mla_fwd(name, qm, km, vm, nlat):
    n = qm.shape[0]
    t = NCTX
    nlt = nlat // t
    c = (MLA_NOPE + MLA_ROPE) ** -0.5 * LOG2E

    def body(q_ref, k_ref, v_ref, o_ref, l_ref):
        i = pl.program_id(1)

        def run(k, v):
            s = _d(q_ref[...], k, ((1,), (1,))) * c
            m = jnp.max(_fold_lanes(s, jnp.maximum), axis=1, keepdims=True)
            p = jnp.exp2(s - m)
            l = jnp.sum(_fold_lanes(p, jnp.add), axis=1, keepdims=True)
            o_ref[...] = (_d(p, v, ((1,), (0,))) / l).astype(BF16)
            l_ref[0] = m + jnp.log2(l)

        @pl.when(i < nlt)
        def _():
            run(k_ref[...], v_ref[...])

        @pl.when(i == nlt)
        def _():
            run(k_ref[nlat:n, :], v_ref[nlat:n, :])

    return pl.pallas_call(
        body, out_shape=[_sd((n, MLA_H * 128), BF16), _sd((MLA_H, n, 1), F32)], grid=(MLA_H, n // t),
        in_specs=[pl.BlockSpec((t, 256), lambda h, i: (i, h)), pl.BlockSpec((n, 256), lambda h, i: (0, h)),
                  pl.BlockSpec((n, 128), lambda h, i: (0, h))],
        out_specs=[pl.BlockSpec((t, 128), lambda h, i: (i, h)), pl.BlockSpec((1, t, 1), lambda h, i: (h, i, 0))],
        name=name, compiler_params=_cp(("parallel", "arbitrary")))(qm, km, vm)


def mla_dq(name, qm, km, vm, o, do, lse, nlat):
    n = qm.shape[0]
    t = NCTX
    nlt = nlat // t
    scale = (MLA_NOPE + MLA_ROPE) ** -0.5

    def body(q_ref, k_ref, v_ref, o_ref, do_ref, l_ref, dq_ref, dl_ref):
        i = pl.program_id(1)
        do = do_ref[...]
        delta = jnp.sum(do.astype(F32) * o_ref[...].astype(F32), axis=1, keepdims=True)
        dl_ref[0] = delta

        def run(k, v):
            s = _d(q_ref[...], k, ((1,), (1,))) * (scale * LOG2E)
            ds = jnp.exp2(s - l_ref[0]) * (_d(do, v, ((1,), (1,))) - delta) * scale
            dq_ref[...] = _d(ds, k, ((1,), (0,)))

        @pl.when(i < nlt)
        def _():
            run(k_ref[...], v_ref[...])

        @pl.when(i == nlt)
        def _():
            run(k_ref[nlat:n, :], v_ref[nlat:n, :])

    qspec = pl.BlockSpec((t, 256), lambda h, i: (i, h))
    ospec = pl.BlockSpec((t, 128), lambda h, i: (i, h))
    lspec = pl.BlockSpec((1, t, 1), lambda h, i: (h, i, 0))
    return pl.pallas_call(
        body, out_shape=[_sd((n, MLA_H * 256), F32), _sd((MLA_H, n, 1), F32)], grid=(MLA_H, n // t),
        in_specs=[qspec, pl.BlockSpec((n, 256), lambda h, i: (0, h)), pl.BlockSpec((n, 128), lambda h, i: (0, h)),
                  ospec, ospec, lspec],
        out_specs=[qspec, lspec],
        name=name, compiler_params=_cp(("parallel", "arbitrary")))(qm, km, vm, o, do, lse)


def mla_dkv(name, qm, km, vm, do, lse_row, delta_row, nlat):
    n = qm.shape[0]
    t = NCTX
    nlt = nlat // t
    scale = (MLA_NOPE + MLA_ROPE) ** -0.5

    def body(q_ref, k_ref, v_ref, do_ref, l_ref, dl_ref, dk_ref, dv_ref):
        j = pl.program_id(1)

        def run(q, do, lrow, drow):
            st = _d(k_ref[...], q, ((1,), (1,))) * (scale * LOG2E)
            pt = jnp.exp2(st - lrow)
            dv_ref[...] = _d(pt, do, ((1,), (0,)))
            dst = pt * (_d(v_ref[...], do, ((1,), (1,))) - drow) * scale
            dk_ref[...] = _d(dst, q, ((1,), (0,)))

        @pl.when(j < nlt)
        def _():
            run(q_ref[0:nlat, :], do_ref[0:nlat, :], l_ref[0, :, 0:nlat], dl_ref[0, :, 0:nlat])

        @pl.when(j == nlt)
        def _():
            run(q_ref[...], do_ref[...], l_ref[0], dl_ref[0])

    rspec = pl.BlockSpec((1, 1, n), lambda h, j: (h, 0, 0))
    return pl.pallas_call(
        body, out_shape=[_sd((n, MLA_H * 256), F32), _sd((n, MLA_H * 128), F32)], grid=(MLA_H, n // t),
        in_specs=[pl.BlockSpec((n, 256), lambda h, j: (0, h)), pl.BlockSpec((t, 256), lambda h, j: (j, h)),
                  pl.BlockSpec((t, 128), lambda h, j: (j, h)), pl.BlockSpec((n, 128), lambda h, j: (0, h)), rspec, rspec],
        out_specs=[pl.BlockSpec((t, 256), lambda h, j: (j, h)), pl.BlockSpec((t, 128), lambda h, j: (j, h))],
        name=name, compiler_params=_cp(("parallel", "arbitrary")))(qm, km, vm, do, lse_row, delta_row)


def mla_attention_bwd(tag, qm, km, vm, o, do, lse, nlat):
    n = qm.shape[0]
    dq, delta = mla_dq(tag + "_dq", qm, km, vm, o, do, lse, nlat)
    dk, dv = mla_dkv(tag + "_dkv", qm, km, vm, do, lse.reshape(MLA_H, 1, n), delta.reshape(MLA_H, 1, n), nlat)
    return dq, dk, dv


SWA_T = 512


def _swa_window(t, nlat):
    t = min(t, nlat)
    return t, min(t + 2 * SWA_WIN, nlat)


def _win_start(i, t, wlen, nlat):
    return pl.multiple_of(jnp.clip(i * t - SWA_WIN, 0, nlat - wlen), 128)


def _win_mask(rows, cols, row0, col0):
    rp = row0 + lax.broadcasted_iota(jnp.int32, (rows, cols), 0)
    cp = col0 + lax.broadcasted_iota(jnp.int32, (rows, cols), 1)
    return jnp.abs(rp - cp) <= SWA_WIN


def swa_fwd_lat(name, qs, ks, u, sink, nlat):
    n = qs.shape[0]
    tq, wlen = _swa_window(SWA_T, nlat)
    grp = SWA_HQ // SWA_HKV
    scale = SWA_DH ** -0.5
    vcol0 = C_V // 128

    def body(q_ref, k_ref, v_ref, s_ref, o_ref, l_ref):
        i = pl.program_id(1)
        ws = _win_start(i, tq, wlen, nlat)
        q = q_ref[...]
        s1 = _d(q, k_ref[pl.ds(ws, wlen), :], ((1,), (1,))) * (scale * LOG2E)
        s1 = jnp.where(_win_mask(tq, wlen, i * tq, ws), s1, NEG)
        s2 = _d(q, k_ref[pl.ds(nlat, NCTX), :], ((1,), (1,))) * (scale * LOG2E)
        sv = jnp.max(s_ref[0], axis=1, keepdims=True) * LOG2E
        m = jnp.maximum(jnp.maximum(jnp.max(s1, axis=1, keepdims=True), jnp.max(s2, axis=1, keepdims=True)), sv)
        p1 = jnp.exp2(s1 - m)
        p2 = jnp.exp2(s2 - m)
        l = jnp.sum(p1, axis=1, keepdims=True) + jnp.sum(p2, axis=1, keepdims=True) + jnp.exp2(sv - m)
        acc = _d(p1, v_ref[pl.ds(ws, wlen), :], ((1,), (0,))) + _d(p2, v_ref[pl.ds(nlat, NCTX), :], ((1,), (0,)))
        o_ref[...] = (acc / l).astype(BF16)
        l_ref[0] = m + jnp.log2(l)

    return pl.pallas_call(
        body, out_shape=[_sd((n, SWA_HQ * 128), BF16), _sd((SWA_HQ, n, 1), F32)], grid=(SWA_HQ, nlat // tq),
        in_specs=[pl.BlockSpec((tq, 128), lambda h, i: (i, h)), pl.BlockSpec((n, 128), lambda h, i: (0, h // grp)),
                  pl.BlockSpec((n, 128), lambda h, i: (0, vcol0 + h // grp)), pl.BlockSpec((1, 1, 128), lambda h, i: (h, 0, 0))],
        out_specs=[pl.BlockSpec((tq, 128), lambda h, i: (i, h)), pl.BlockSpec((1, tq, 1), lambda h, i: (h, i, 0))],
        name=name, compiler_params=_cp(("parallel", "arbitrary")))(qs, ks, u, sink)


def swa_dq_lat(name, qs, ks, u, o, do, lse, sink, nlat):
    n = qs.shape[0]
    tq, wlen = _swa_window(SWA_T, nlat)
    grp = SWA_HQ // SWA_HKV
    scale = SWA_DH ** -0.5
    vcol0 = C_V // 128

    def body(q_ref, k_ref, v_ref, s_ref, o_ref, do_ref, l_ref, dq_ref, dl_ref, ds_ref):
        i = pl.program_id(1)
        ws = _win_start(i, tq, wlen, nlat)
        q = q_ref[...]
        do = do_ref[...]
        lse_v = l_ref[0]
        delta = jnp.sum(do.astype(F32) * o_ref[...].astype(F32), axis=1, keepdims=True)
        kw = k_ref[pl.ds(ws, wlen), :]
        kc = k_ref[pl.ds(nlat, NCTX), :]
        s1 = _d(q, kw, ((1,), (1,))) * (scale * LOG2E)
        s1 = jnp.where(_win_mask(tq, wlen, i * tq, ws), s1, NEG)
        s2 = _d(q, kc, ((1,), (1,))) * (scale * LOG2E)
        ds1 = jnp.exp2(s1 - lse_v) * (_d(do, v_ref[pl.ds(ws, wlen), :], ((1,), (1,))) - delta) * scale
        ds2 = jnp.exp2(s2 - lse_v) * (_d(do, v_ref[pl.ds(nlat, NCTX), :], ((1,), (1,))) - delta) * scale
        dq_ref[...] = _d(ds1, kw, ((1,), (0,))) + _d(ds2, kc, ((1,), (0,)))
        dl_ref[0] = delta
        sv = jnp.max(s_ref[0], axis=1, keepdims=True) * LOG2E
        dsk = jnp.sum(-jnp.exp2(sv - lse_v) * delta, axis=0, keepdims=True)
        _acc(ds_ref, jnp.zeros((1, 1, 128), F32) + dsk, i == 0)

    qspec = pl.BlockSpec((tq, 128), lambda h, i: (i, h))
    lspec = pl.BlockSpec((1, tq, 1), lambda h, i: (h, i, 0))
    return pl.pallas_call(
        body, out_shape=[_sd((n, SWA_HQ * 128), F32), _sd((SWA_HQ, n, 1), F32), _sd((SWA_HQ, 1, 128), F32)],
        grid=(SWA_HQ, nlat // tq),
        in_specs=[qspec, pl.BlockSpec((n, 128), lambda h, i: (0, h // grp)),
                  pl.BlockSpec((n, 128), lambda h, i: (0, vcol0 + h // grp)), pl.BlockSpec((1, 1, 128), lambda h, i: (h, 0, 0)),
                  qspec, qspec, lspec],
        out_specs=[qspec, lspec, pl.BlockSpec((1, 1, 128), lambda h, i: (h, 0, 0))],
        name=name, compiler_params=_cp(("parallel", "arbitrary")))(qs, ks, u, sink, o, do, lse)


def swa_dkv_lat(name, qs, ks, u, do, lse_row, delta_row, nlat):
    n = qs.shape[0]
    tk, wlen = _swa_window(SWA_T, nlat)
    grp = SWA_HQ // SWA_HKV
    scale = SWA_DH ** -0.5
    vcol0 = C_V // 128

    def body(q_ref, k_ref, v_ref, do_ref, l_ref, dl_ref, dk_ref, dv_ref):
        j = pl.program_id(1)
        ws = _win_start(j, tk, wlen, nlat)
        k = k_ref[...]
        v = v_ref[...]
        mask = _win_mask(tk, wlen, j * tk, ws)
        dk = jnp.zeros((tk, 128), F32)
        dv = jnp.zeros((tk, 128), F32)
        for gi in range(grp):
            qw = q_ref[pl.ds(ws, wlen), 128 * gi:128 * gi + 128]
            dow = do_ref[pl.ds(ws, wlen), 128 * gi:128 * gi + 128]
            st = jnp.where(mask, _d(k, qw, ((1,), (1,))) * (scale * LOG2E), NEG)
            pt = jnp.exp2(st - l_ref[gi, :, pl.ds(ws, wlen)])
            dv = dv + _d(pt, dow, ((1,), (0,)))
            dst = pt * (_d(v, dow, ((1,), (1,))) - dl_ref[gi, :, pl.ds(ws, wlen)]) * scale
            dk = dk + _d(dst, qw, ((1,), (0,)))
        dk_ref[...] = dk
        dv_ref[...] = dv

    rspec = pl.BlockSpec((grp, 1, n), lambda hk, j: (hk, 0, 0))
    return pl.pallas_call(
        body, out_shape=[_sd((n, SWA_HKV * 128), F32), _sd((n, SWA_HKV * 128), F32)], grid=(SWA_HKV, nlat // tk),
        in_specs=[pl.BlockSpec((n, grp * 128), lambda hk, j: (0, hk)), pl.BlockSpec((tk, 128), lambda hk, j: (j, hk)),
                  pl.BlockSpec((tk, 128), lambda hk, j: (j, vcol0 + hk)), pl.BlockSpec((n, grp * 128), lambda hk, j: (0, hk)),
                  rspec, rspec],
        out_specs=[pl.BlockSpec((tk, 128), lambda hk, j: (j, hk)), pl.BlockSpec((tk, 128), lambda hk, j: (j, hk))],
        name=name, compiler_params=_cp(("parallel", "arbitrary")))(qs, ks, u, do, lse_row, delta_row)


def swa_attention_fwd(tag, qs, ks, u, sink, cfg, nlat):
    o, lse = swa_fwd_lat(tag + "_fwd_lat", qs, ks, u, sink, nlat)
    return flash_fwd(tag + "_fwd_ctx", qs, ks, u, sink=sink, ctx_q=True, nlat=nlat, prev=(o, lse), **cfg)


def swa_attention_bwd(tag, qs, ks, u, o, do, lse, sink, cfg, nlat):
    n = qs.shape[0]
    dq, delta, ds1 = swa_dq_lat(tag + "_dq_lat", qs, ks, u, o, do, lse, sink, nlat)
    dq, delta, ds2 = flash_dq(tag + "_dq_ctx", qs, ks, u, o, do, lse, sink=sink, ctx_q=True, nlat=nlat, prev=(dq, delta), **cfg)
    dk, dv = swa_dkv_lat(tag + "_dkv_lat", qs, ks, u, do, lse.reshape(SWA_HQ, 1, n), delta.reshape(SWA_HQ, 1, n), nlat)
    kc = {k: v for k, v in cfg.items() if k != "hq"}
    kc["hkv"] = SWA_HKV
    kc["tq"] = min(1024, nlat)
    dk, dv = flash_dkv(tag + "_dkv_ctx", qs, ks, u, do, lse, delta, ctx_k=True, nlat=nlat, prev=(dk, dv), **kc)
    return dq, dk, dv, ds1 + ds2


def adamw(name, w, g, m, v):
    r, c = w.shape
    tr = _pick(r, (256, 128, 64, 32, 16, 8))
    bc1 = 1.0 - ADAM_B1 ** ADAM_STEP
    bc2 = 1.0 - ADAM_B2 ** ADAM_STEP

    def body(w_ref, g_ref, m_ref, v_ref, d_ref, nm_ref, nv_ref):
        gv = g_ref[...]
        nm = ADAM_B1 * m_ref[...] + (1.0 - ADAM_B1) * gv
        nv = ADAM_B2 * v_ref[...] + (1.0 - ADAM_B2) * (gv * gv)
        d_ref[...] = -ADAM_LR * ((nm / bc1) / (jnp.sqrt(nv / bc2) + ADAM_EPS) + ADAM_WD * w_ref[...])
        nm_ref[...] = nm
        nv_ref[...] = nv

    spec = pl.BlockSpec((tr, c), lambda i: (i, 0))
    return pl.pallas_call(body, out_shape=[_sd((r, c), F32)] * 3, grid=(r // tr,), in_specs=[spec] * 4, out_specs=[spec] * 3,
                          name=name, compiler_params=_cp(("parallel",)))(w, g, m, v)


def _coords():
    return lax.axis_index("x"), lax.axis_index("y"), lax.axis_index("c")


_ANY = pl.BlockSpec(memory_space=pl.ANY)


def gather_chips(name, a):
    def body(a_ref, o_ref, send_sems, recv_sems, loc_sem):
        x, y, c = _coords()
        me = 2 * x + y
        peers = [(1 - x, y), (x, 1 - y), (1 - x, 1 - y)]
        mine = pltpu.make_async_copy(a_ref, o_ref.at[me], loc_sem)
        mine.start()
        sends = [pltpu.make_async_remote_copy(a_ref, o_ref.at[me], send_sems.at[k], recv_sems.at[k],
                                              device_id=(px, py, c), device_id_type=MESH)
                 for k, (px, py) in enumerate(peers)]
        for cp in sends:
            cp.start()
        for k, (px, py) in enumerate(peers):
            pltpu.make_async_remote_copy(a_ref, o_ref.at[2 * px + py], send_sems.at[k], recv_sems.at[k],
                                         device_id=(px, py, c), device_id_type=MESH).wait_recv()
        for cp in sends:
            cp.wait_send()
        mine.wait()

    return pl.pallas_call(
        body, out_shape=_sd((4,) + a.shape, a.dtype), in_specs=[_ANY], out_specs=_ANY,
        scratch_shapes=[pltpu.SemaphoreType.DMA((3,)), pltpu.SemaphoreType.DMA((3,)), pltpu.SemaphoreType.DMA],
        name=name, compiler_params=pltpu.CompilerParams(has_side_effects=True))(a)


def scatter_chips(name, a):
    def body(a_ref, o_ref, send_sems, recv_sems, loc_sem):
        x, y, c = _coords()
        me = 2 * x + y
        peers = [(1 - x, y), (x, 1 - y), (1 - x, 1 - y)]
        mine = pltpu.make_async_copy(a_ref.at[me], o_ref.at[me], loc_sem)
        mine.start()
        sends = [pltpu.make_async_remote_copy(a_ref.at[2 * px + py], o_ref.at[me], send_sems.at[k], recv_sems.at[k],
                                              device_id=(px, py, c), device_id_type=MESH)
                 for k, (px, py) in enumerate(peers)]
        for cp in sends:
            cp.start()
        for k, (px, py) in enumerate(peers):
            pltpu.make_async_remote_copy(a_ref.at[me], o_ref.at[2 * px + py], send_sems.at[k], recv_sems.at[k],
                                         device_id=(px, py, c), device_id_type=MESH).wait_recv()
        for cp in sends:
            cp.wait_send()
        mine.wait()

    return pl.pallas_call(
        body, out_shape=_sd(a.shape, a.dtype), in_specs=[_ANY], out_specs=_ANY,
        scratch_shapes=[pltpu.SemaphoreType.DMA((3,)), pltpu.SemaphoreType.DMA((3,)), pltpu.SemaphoreType.DMA],
        name=name, compiler_params=pltpu.CompilerParams(has_side_effects=True))(a)


def sibling_swap(name, a):
    def body(a_ref, o_ref, send_sem, recv_sem):
        x, y, c = _coords()
        cp = pltpu.make_async_remote_copy(a_ref, o_ref, send_sem, recv_sem, device_id=(x, y, 1 - c), device_id_type=MESH)
        cp.start()
        cp.wait()

    return pl.pallas_call(
        body, out_shape=_sd(a.shape, a.dtype), in_specs=[_ANY], out_specs=_ANY,
        scratch_shapes=[pltpu.SemaphoreType.DMA, pltpu.SemaphoreType.DMA],
        name=name, compiler_params=pltpu.CompilerParams(has_side_effects=True))(a)


def gather_all(name, a):
    def body(a_ref, o_ref, send_sems, recv_sems, loc_sem):
        x, y, c = _coords()
        me = 4 * x + 2 * y + c
        flips = [(fx, fy, fc) for fx in (0, 1) for fy in (0, 1) for fc in (0, 1) if fx + fy + fc > 0]
        peers = [(x ^ fx, y ^ fy, c ^ fc) for fx, fy, fc in flips]
        mine = pltpu.make_async_copy(a_ref, o_ref.at[me], loc_sem)
        mine.start()
        sends = [pltpu.make_async_remote_copy(a_ref, o_ref.at[me], send_sems.at[k], recv_sems.at[k],
                                              device_id=p, device_id_type=MESH) for k, p in enumerate(peers)]
        for cp in sends:
            cp.start()
        for k, (px, py, pc) in enumerate(peers):
            pltpu.make_async_remote_copy(a_ref, o_ref.at[4 * px + 2 * py + pc], send_sems.at[k], recv_sems.at[k],
                                         device_id=(px, py, pc), device_id_type=MESH).wait_recv()
        for cp in sends:
            cp.wait_send()
        mine.wait()

    return pl.pallas_call(
        body, out_shape=_sd((8,) + a.shape, a.dtype), in_specs=[_ANY], out_specs=_ANY,
        scratch_shapes=[pltpu.SemaphoreType.DMA((7,)), pltpu.SemaphoreType.DMA((7,)), pltpu.SemaphoreType.DMA],
        name=name, compiler_params=pltpu.CompilerParams(has_side_effects=True))(a)


def sum_blocks(name, a):
    k, r, c = a.shape
    tr = _pick(r, (256, 128, 64, 32, 16, 8))

    def body(a_ref, o_ref):
        acc = a_ref[0].astype(F32)
        for s in range(1, k):
            acc = acc + a_ref[s].astype(F32)
        o_ref[...] = acc

    return pl.pallas_call(body, out_shape=_sd((r, c), F32), grid=(r // tr,),
                          in_specs=[pl.BlockSpec((k, tr, c), lambda i: (0, i, 0))], out_specs=pl.BlockSpec((tr, c), lambda i: (i, 0)),
                          name=name, compiler_params=_cp(("parallel",)))(a)


def add_pair(name, a, b):
    r, c = a.shape
    tr = _pick(r, (256, 128, 64, 32, 16, 8))

    def body(a_ref, b_ref, o_ref):
        o_ref[...] = a_ref[...] + b_ref[...]

    spec = pl.BlockSpec((tr, c), lambda i: (i, 0))
    return pl.pallas_call(body, out_shape=_sd((r, c), F32), grid=(r // tr,), in_specs=[spec, spec], out_specs=spec,
                          name=name, compiler_params=_cp(("parallel",)))(a, b)


BIG = ("w_mod", "w_in", "w_mla_uq", "w_mla_ukv", "w_p_ssm", "w_p_swa", "w_p_mla", "w_out", "w_ffn_in", "w_ffn_out")
COL_SHARDED = ("w_mod", "w_in", "w_mla_uq", "w_mla_ukv", "w_ffn_in")
SMALL = ("c_ctx", "b_mod", "norm1_g", "norm2_g", "ssm_conv_w", "ssm_conv_b", "ssm_dt_bias", "ssm_a_log", "ssm_d",
         "ssm_norm_g", "swa_q_norm_g", "swa_k_norm_g", "swa_sink", "mla_q_lat_g", "mla_kv_lat_g", "mla_q_norm_g",
         "mla_k_norm_g")
WEIGHTS = ("c_ctx", "w_mod", "b_mod", "norm1_g", "norm2_g", "w_in", "ssm_conv_w", "ssm_conv_b", "ssm_dt_bias", "ssm_a_log",
           "ssm_d", "ssm_norm_g", "swa_q_norm_g", "swa_k_norm_g", "swa_sink", "mla_q_lat_g", "mla_kv_lat_g", "w_mla_uq",
           "w_mla_ukv", "mla_q_norm_g", "mla_k_norm_g", "w_p_ssm", "w_p_swa", "w_p_mla", "w_out", "w_ffn_in", "w_ffn_out")


def pack_w_in(w):
    z = lambda k: jnp.zeros((w.shape[0], k), w.dtype)
    return jnp.concatenate([w[:, 4832:7904], w[:, 2400:3424], w[:, 3424:4448], w[:, 0:1536], w[:, 1568:1824], w[:, 1824:2080],
                            w[:, 2080:2336], w[:, 2336:2400], w[:, 1536:1568], z(32), z(128), w[:, 4448:4832]], axis=1)


def unpack_w_in(g):
    return jnp.concatenate([g[:, 5120:6656], g[:, 7488:7520], g[:, 6656:6912], g[:, 6912:7168], g[:, 7168:7424], g[:, 7424:7488],
                            g[:, 3072:4096], g[:, 4096:5120], g[:, 7680:8064], g[:, 0:3072]], axis=1)


def pack_ukv(w):
    return w.reshape(MLA_KVRANK, MLA_H, 2, 128).transpose(0, 2, 1, 3).reshape(MLA_KVRANK, 2048)


def unpack_ukv(g):
    return g.reshape(MLA_KVRANK, 2, MLA_H, 128).transpose(0, 2, 1, 3).reshape(MLA_KVRANK, 2048)


def pack_uq(w):
    return jnp.pad(w.reshape(MLA_QRANK, MLA_H, 192), ((0, 0), (0, 0), (0, 64))).reshape(MLA_QRANK, 2048)


def unpack_uq(g):
    return g.reshape(MLA_QRANK, MLA_H, 256)[:, :, :192].reshape(MLA_QRANK, 1536)


def rope_tables(nlat):
    t = jnp.arange(nlat, dtype=jnp.int32)
    r = (t // GRID_W).astype(F32)[:, None]
    col = (t % GRID_W).astype(F32)[:, None]

    def tab(nf, pad):
        inv = jnp.power(ROPE_BASE, -jnp.arange(nf, dtype=F32) / nf)
        ar, ac = r * inv, col * inv
        cos = jnp.concatenate([jnp.cos(ar), jnp.cos(ar), jnp.cos(ac), jnp.cos(ac), jnp.ones((nlat, pad), F32)], axis=1)
        sin = jnp.concatenate([-jnp.sin(ar), jnp.sin(ar), -jnp.sin(ac), jnp.sin(ac), jnp.zeros((nlat, pad), F32)], axis=1)
        cos = jnp.concatenate([cos, jnp.ones((NCTX, 128), F32)], axis=0)
        sin = jnp.concatenate([sin, jnp.zeros((NCTX, 128), F32)], axis=0)
        return cos, sin

    return tab(32, 0), tab(16, 64)


def _lanes(v, start, width=128):
    return jnp.zeros((1, width), F32).at[0, start:start + v.shape[0]].set(v)


def layer_fwd(i, xin, h, mod, p, tabs, nlat):
    t = "l%d_" % i
    n = xin.shape[0]
    (cos_s, sin_s), (cos_m, sin_m) = tabs
    u = mm(h, p["w_in"], F32, t + "in_proj")
    xbc = conv_fwd(t + "conv", u, p["conv_w"], p["conv_b"], nlat)
    dtrow = jnp.transpose(u[:, C_MISC + DT_LANE:C_MISC + DT_LANE + 32])
    nlc = nlat // Q
    yf, hs_f = ssd_fwd(t + "ssd_f", xbc, u, dtrow, p["bias_c"], p["alog_c"], p["bias_r"], p["alog_r"], nlc, False, 0)
    yb, hs_b = ssd_fwd(t + "ssd_b", xbc, u, dtrow, p["bias_c"], p["alog_c"], p["bias_r"], p["alog_r"], nlc, True, 1)
    ys = ssd_out_fwd(t + "ssd_out", yf, yb, xbc, u, p["ssm_norm_g"], p["d_exp"])
    qs, ks = swa_prep_fwd(t + "swa_prep", u, p["swa_q_g"], p["swa_k_g"], cos_s, sin_s)
    o_swa, lse_swa = swa_attention_fwd(t + "swa", qs, ks, u, p["sink"], p["swa_cfg"], nlat)
    ckv_n, cq_n = lat_norm_fwd(t + "lat_norm", u, p["kv_lat_g"], p["q_lat_g"])
    kv = mm(ckv_n, p["w_ukv"], F32, t + "ukv")
    qp = mm(cq_n, p["w_uq"], F32, t + "uq")
    km, qm, vm = mla_prep_fwd(t + "mla_prep", kv, qp, u, p["mla_q_g"], p["mla_k_g"], cos_m, sin_m)
    o_mla, lse_mla = mla_fwd(t + "mla_fwd", qm, km, vm, nlat)
    p1 = mm(ys, p["w_p_ssm"], F32, t + "p_ssm")
    p2 = mm(o_swa, p["w_p_swa"], F32, t + "p_swa")
    p3 = mm(o_mla, p["w_p_mla"], F32, t + "p_mla")
    merged = merge_fwd(t + "merge", u, p1, p2, p3)
    o = mm(merged, p["w_out"], F32, t + "out_proj")
    x1, h2 = resid_mod_fwd(t + "res1", xin, o, mod, 2, mod, 3, 4, p["norm2_g"], nlat // RT)
    gu = mm(h2, p["w_ffn_in"], F32, t + "ffn_in")
    a = swiglu_fwd(t + "swiglu", gu)
    f = mm(a, p["w_ffn_out"], F32, t + "ffn_out")
    saved = dict(xin=xin, h=h, u=u, xbc=xbc, dtrow=dtrow, yf=yf, yb=yb, hs_f=hs_f, hs_b=hs_b, ys=ys, qs=qs, ks=ks,
                 o_swa=o_swa, lse_swa=lse_swa, ckv_n=ckv_n, cq_n=cq_n, kv=kv, qp=qp, km=km, qm=qm, vm=vm, o_mla=o_mla,
                 lse_mla=lse_mla, p1=p1, p2=p2, p3=p3, merged=merged, o=o, x1=x1, h2=h2, gu=gu, a=a, f=f)
    del n
    return x1, f, saved


def layer_bwd(i, dx2, df, dgt2, sv, mod, p, tabs, nlat):
    t = "l%db_" % i
    (cos_s, sin_s), (cos_m, sin_m) = tabs
    g = {}
    nt = nlat // RT
    nlc = nlat // Q
    g["w_ffn_out"] = mm_tn(sv["a"], df, t + "wg_ffn_out")
    da = mm(df, p["w_ffn_out_t"], F32, t + "dg_ffn_out")
    dgu = swiglu_bwd(t + "swiglu", sv["gu"], da)
    g["w_ffn_in"] = mm_tn(sv["h2"], dgu, t + "wg_ffn_in")
    dh2 = mm(dgu, p["w_ffn_in_t"], F32, t + "dg_ffn_in")
    dx1, do, dgt1, dsh2, dsc2, g["norm2_g"] = resid_mod_bwd(t + "res1", sv["x1"], dx2, dh2, sv["o"], mod, 2, mod, 3, 4,
                                                              p["norm2_g"], nt)
    g["w_out"] = mm_tn(sv["merged"], do, t + "wg_out")
    dmerged = mm(do, p["w_out_t"], F32, t + "dg_out")
    dp1, dp2, dp3, dgates = merge_bwd(t + "merge", sv["u"], sv["p1"], sv["p2"], sv["p3"], dmerged)
    g["w_p_ssm"] = mm_tn(sv["ys"], dp1, t + "wg_p_ssm")
    g["w_p_swa"] = mm_tn(sv["o_swa"], dp2, t + "wg_p_swa")
    g["w_p_mla"] = mm_tn(sv["o_mla"], dp3, t + "wg_p_mla")
    dys = mm(dp1, p["w_p_ssm_t"], F32, t + "dg_p_ssm")
    do_swa = mm(dp2, p["w_p_swa_t"], BF16, t + "dg_p_swa")
    do_mla = mm(dp3, p["w_p_mla_t"], BF16, t + "dg_p_mla")
    dqm, dkm, dv_mla = mla_attention_bwd(t + "mla", sv["qm"], sv["km"], sv["vm"], sv["o_mla"], do_mla, sv["lse_mla"], nlat)
    dkv, dqp, dkr, g["mla_q_g"], g["mla_k_g"] = mla_prep_bwd(t + "mla_prep", sv["kv"], sv["qp"], sv["u"], p["mla_q_g"],
                                                             p["mla_k_g"], cos_m, sin_m, dkm, dqm, dv_mla)
    g["w_ukv"] = mm_tn(sv["ckv_n"], dkv, t + "wg_ukv")
    g["w_uq"] = mm_tn(sv["cq_n"], dqp, t + "wg_uq")
    dckv_n = mm(dkv, p["w_ukv_t"], F32, t + "dg_ukv")
    dcq_n = mm(dqp, p["w_uq_t"], F32, t + "dg_uq")
    dckv, dcq, g["kv_lat_g"], g["q_lat_g"] = lat_norm_bwd(t + "lat_norm", sv["u"], p["kv_lat_g"], p["q_lat_g"], dckv_n, dcq_n)
    dqs, dks, dv_swa, g["sink"] = swa_attention_bwd(t + "swa", sv["qs"], sv["ks"], sv["u"], sv["o_swa"], do_swa, sv["lse_swa"],
                                                p["sink"], p["swa_cfg"], nlat)
    dq, dk, dv, g["swa_q_g"], g["swa_k_g"] = swa_prep_bwd(t + "swa_prep", sv["u"], p["swa_q_g"], p["swa_k_g"], cos_s, sin_s,
                                                          dqs, dks, dv_swa)
    dy, dxs_skip, dz, g["ssm_norm_g"], g["d_exp"] = ssd_out_bwd(t + "ssd_out", sv["yf"], sv["yb"], sv["xbc"], sv["u"],
                                                                 p["ssm_norm_g"], p["d_exp"], dys)
    n = dy.shape[0]
    zbc = jnp.zeros((n, 256), F32)
    r_f = ssd_bwd(t + "ssd_f", sv["xbc"], sv["u"], sv["dtrow"], p["bias_c"], p["alog_c"], p["bias_r"], p["alog_r"],
                  sv["hs_f"], dy, (dxs_skip, zbc, zbc), nlc, False, 0)
    r_b = ssd_bwd(t + "ssd_b", sv["xbc"], sv["u"], sv["dtrow"], p["bias_c"], p["alog_c"], p["bias_r"], p["alog_r"],
                  sv["hs_b"], dy, (r_f[0], r_f[1], r_f[2]), nlc, True, 1)
    dact = jnp.concatenate([r_b[0], r_b[1], r_b[2]], axis=1)
    dxbc, g["conv_w"], g["conv_b"] = conv_bwd(t + "conv", sv["u"], dact, p["conv_w"], p["conv_b"], nlat)
    drow = jnp.concatenate([r_f[4][0] + r_f[4][1], r_b[4][0] + r_b[4][1]], axis=0)
    drow_t = jnp.pad(jnp.transpose(drow), ((0, 0), (DT_LANE, 128 - DT_LANE - 32)))
    dmisc = misc_combine(t + "misc", dkr, r_f[3], r_b[3], drow_t)
    g["bias_c"] = r_f[5] + r_b[5]
    g["alog_c"] = r_f[6] + r_b[6]
    g["bias_r"] = jnp.concatenate([r_f[7], r_b[7]], axis=0)
    g["alog_r"] = jnp.concatenate([r_f[8], r_b[8]], axis=0)
    du = jnp.concatenate([dgates, dz, dq, dxbc, dk, dv, dckv, dmisc, jnp.zeros((n, 128), BF16), dcq], axis=1)
    g["w_in"] = mm_tn(sv["h"], du, t + "wg_in")
    dh = mm(du, p["w_in_t"], F32, t + "dg_in")
    g["mod"] = (dgt1, dsh2, dsc2, dgt2)
    return dx1, dh, g


def local_step(x, c, ctx, target, c_ctx, W, nlat):
    xin = jnp.concatenate([x, ctx], axis=0)
    n = xin.shape[0]
    nt = nlat // RT
    tabs = rope_tables(nlat)
    c8 = jnp.zeros((8, D), F32).at[0].set(c[0]).at[1].set(c_ctx)
    mods, silus = [], []
    for i in range(DEPTH):
        m8, s8 = mod_fwd("l%d_mod" % i, c8, W[i]["w_mod"], W[i]["b_mod"])
        mods.append(m8[0:2].reshape(2, 1, 6 * D))
        silus.append(s8)
    saved = []
    _, h = resid_mod_fwd("l0_norm1", xin, None, None, 0, mods[0], 0, 1, W[0]["norm1_g"], nt)
    xcur = xin
    for i in range(DEPTH):
        x1, f, sv = layer_fwd(i, xcur, h, mods[i], W[i], tabs, nlat)
        saved.append(sv)
        if i + 1 < DEPTH:
            xcur, h = resid_mod_fwd("l%d_res2" % i, x1, f, mods[i], 5, mods[i + 1], 0, 1, W[i + 1]["norm1_g"], nt)
    loss_v, dx2, df, dgt2 = resid_loss("loss", x1, f, mods[DEPTH - 1], 5, target, nt)
    grads = [None] * DEPTH
    for i in reversed(range(DEPTH)):
        dx1, dh, g = layer_bwd(i, dx2, df, dgt2, saved[i], mods[i], W[i], tabs, nlat)
        if i > 0:
            sv = saved[i]
            dx2, df, dgt2, dsh1, dsc1, g["norm1_g"] = resid_mod_bwd(
                "l%db_res2" % (i - 1), sv["xin"], dx1, dh, saved[i - 1]["f"], mods[i - 1], 5, mods[i], 0, 1,
                W[i]["norm1_g"], nt)
        else:
            dxin, _, _, dsh1, dsc1, g["norm1_g"] = resid_mod_bwd("l0b_norm1", saved[0]["xin"], dx1, dh, None, None, 0,
                                                                  mods[0], 0, 1, W[0]["norm1_g"], nt)
        dgt1, dsh2, dsc2, dgt2_i = g.pop("mod")
        dmod = jnp.concatenate([dsh1, dsc1, dgt1, dsh2, dsc2, dgt2_i], axis=2).reshape(2, 6 * D)
        dmod8 = jnp.zeros((8, 6 * D), F32).at[0:2].set(dmod)
        g["w_mod"] = mm_tn(silus[i], dmod8, "l%db_wg_mod" % i)
        dsilu = mm(dmod8, W[i]["w_mod_t"], F32, "l%db_dg_mod" % i)
        dc8, g["b_mod"] = mod_small_bwd("l%db_mod_small" % i, c8, dsilu, dmod8)
        g["c8"] = dc8
        grads[i] = g
    del n
    return loss_v[0, 0], dxin, grads


def _big_shapes():
    return dict(w_mod=(2, 1024, 1536), w_in=(2, 1024, 1976), w_mla_uq=(2, 384, 384), w_mla_ukv=(2, 256, 512),
                w_p_ssm=(2, 256, 1024), w_p_swa=(2, 256, 1024), w_p_mla=(2, 256, 1024), w_out=(2, 256, 1024),
                w_ffn_in=(2, 1024, 1408), w_ffn_out=(2, 704, 1024))


def _pack_big(d, dtype):
    return jnp.concatenate([d[k].astype(dtype).reshape(-1, 1024) for k in BIG], axis=0)


def _unpack_big(buf, lead):
    out = {}
    r0 = 0
    for k in BIG:
        sh = _big_shapes()[k]
        rows = sh[0] * sh[1] * sh[2] // 1024
        out[k] = buf[..., r0:r0 + rows, :].reshape(lead + sh)
        r0 += rows
    return out


def _full_from_chips(k, a):
    if k in COL_SHARDED:
        return a.transpose(1, 2, 0, 3).reshape(2, a.shape[2], 4 * a.shape[3])
    return a.transpose(1, 0, 2, 3).reshape(2, 4 * a.shape[2], a.shape[3])


def _chips_from_full(k, a):
    if k in COL_SHARDED:
        return a.reshape(a.shape[0], 4, a.shape[1] // 4).transpose(1, 0, 2)
    return a.reshape(4, a.shape[0] // 4, a.shape[1])


def _small_sizes():
    return dict(c_ctx=1024, b_mod=2 * 6144, norm1_g=2048, norm2_g=2048, ssm_conv_w=2 * 5 * 1536, ssm_conv_b=2 * 1536,
                ssm_dt_bias=64, ssm_a_log=64, ssm_d=32, ssm_norm_g=2048, swa_q_norm_g=256, swa_k_norm_g=256, swa_sink=16,
                mla_q_lat_g=768, mla_kv_lat_g=512, mla_q_norm_g=384, mla_k_norm_g=384)


def _pack_small(d):
    parts = []
    for k in SMALL:
        v = d[k].astype(F32).reshape(-1)
        parts.append(jnp.pad(v, (0, (-v.shape[0]) % 1024)))
    return jnp.concatenate(parts).reshape(-1, 128)


def _unpack_small(buf, shapes):
    flat = buf.reshape(-1)
    out = {}
    o = 0
    for k in SMALL:
        sz = _small_sizes()[k]
        out[k] = flat[o:o + sz].reshape(shapes[k])
        o += sz + (-sz) % 1024
    return out


def big_grads(grads):
    gfull = {k: [] for k in BIG}
    for i in range(DEPTH):
        g = grads[i]
        gfull["w_mod"].append(g["w_mod"])
        gfull["w_in"].append(unpack_w_in(g["w_in"]))
        gfull["w_mla_uq"].append(unpack_uq(g["w_uq"]))
        gfull["w_mla_ukv"].append(unpack_ukv(g["w_ukv"]))
        for k in ("w_p_ssm", "w_p_swa", "w_p_mla", "w_out", "w_ffn_in", "w_ffn_out"):
            gfull[k].append(g[k])
    return gfull


def small_grads(grads):
    gs = {}
    gs["c_ctx"] = sum(grads[i]["c8"][1] for i in range(DEPTH))
    st = lambda f: jnp.stack([f(grads[i]) for i in range(DEPTH)])
    gs["b_mod"] = st(lambda g: g["b_mod"][0])
    gs["norm1_g"] = st(lambda g: g["norm1_g"][0])
    gs["norm2_g"] = st(lambda g: g["norm2_g"][0])
    gs["ssm_conv_w"] = st(lambda g: g["conv_w"])
    gs["ssm_conv_b"] = st(lambda g: g["conv_b"][0])
    gs["ssm_dt_bias"] = st(lambda g: (g["bias_c"][0, DT_LANE:DT_LANE + 32] + g["bias_r"][:, 0]).reshape(2, 16))
    gs["ssm_a_log"] = st(lambda g: (g["alog_c"][0, DT_LANE:DT_LANE + 32] + g["alog_r"][:, 0]).reshape(2, 16))
    gs["ssm_d"] = st(lambda g: g["d_exp"].reshape(16, 64).sum(axis=1))
    gs["ssm_norm_g"] = st(lambda g: g["ssm_norm_g"][0])
    gs["swa_q_norm_g"] = st(lambda g: g["swa_q_g"][0])
    gs["swa_k_norm_g"] = st(lambda g: g["swa_k_g"][0])
    gs["swa_sink"] = st(lambda g: g["sink"][:, 0, 0])
    gs["mla_q_lat_g"] = st(lambda g: g["q_lat_g"][0])
    gs["mla_kv_lat_g"] = st(lambda g: g["kv_lat_g"][0])
    gs["mla_q_norm_g"] = st(lambda g: g["mla_q_g"][0, :192])
    gs["mla_k_norm_g"] = st(lambda g: g["mla_k_g"][0, :192])
    return gs


def layer_params(i, full, conv_full, sm, nlat):
    p = {}
    p["w_mod"] = full["w_mod"][i]
    p["w_in"] = pack_w_in(full["w_in"][i])
    p["w_uq"] = pack_uq(full["w_mla_uq"][i])
    p["w_ukv"] = pack_ukv(full["w_mla_ukv"][i])
    for k in ("w_p_ssm", "w_p_swa", "w_p_mla", "w_out", "w_ffn_in", "w_ffn_out"):
        p[k] = full[k][i]
    for k in ("w_mod", "w_in", "w_uq", "w_ukv", "w_p_ssm", "w_p_swa", "w_p_mla", "w_out", "w_ffn_in", "w_ffn_out"):
        p[k + "_t"] = jnp.transpose(p[k])
    p["b_mod"] = sm["b_mod"][i][None]
    p["norm1_g"] = sm["norm1_g"][i][None]
    p["norm2_g"] = sm["norm2_g"][i][None]
    p["conv_w"] = conv_full[i]
    p["conv_b"] = sm["ssm_conv_b"][i][None]
    bias = sm["ssm_dt_bias"][i].reshape(32)
    alog = sm["ssm_a_log"][i].reshape(32)
    p["bias_c"] = _lanes(bias, DT_LANE)
    p["alog_c"] = _lanes(alog, DT_LANE)
    p["bias_r"] = bias[:, None]
    p["alog_r"] = alog[:, None]
    p["d_exp"] = jnp.repeat(sm["ssm_d"][i], 64)[None]
    p["ssm_norm_g"] = sm["ssm_norm_g"][i][None]
    p["swa_q_g"] = sm["swa_q_norm_g"][i][None]
    p["swa_k_g"] = sm["swa_k_norm_g"][i][None]
    p["sink"] = jnp.broadcast_to(sm["swa_sink"][i][:, None, None], (SWA_HQ, 1, 128))
    p["q_lat_g"] = sm["mla_q_lat_g"][i][None]
    p["kv_lat_g"] = sm["mla_kv_lat_g"][i][None]
    p["mla_q_g"] = _lanes(sm["mla_q_norm_g"][i], 0, 256)
    p["mla_k_g"] = _lanes(sm["mla_k_norm_g"][i], 0, 256)
    p["swa_cfg"] = dict(w=128, vw=128, hq=SWA_HQ, grp=SWA_HQ // SWA_HKV, vcol0=C_V // 128, scale=SWA_DH ** -0.5,
                        tq=256, tk=256, band=True)
    return p


def kernel(x, c, ctx, c_ctx, w_mod, b_mod, norm1_g, norm2_g, w_in, ssm_conv_w, ssm_conv_b, ssm_dt_bias, ssm_a_log, ssm_d, ssm_norm_g, swa_q_norm_g, swa_k_norm_g, swa_sink, mla_q_lat_g, mla_kv_lat_g, w_mla_uq, w_mla_ukv, mla_q_norm_g, mla_k_norm_g, w_p_ssm, w_p_swa, w_p_mla, w_out, w_ffn_in, w_ffn_out, loss_target, m_c_ctx, m_w_mod, m_b_mod, m_norm1_g, m_norm2_g, m_w_in, m_ssm_conv_w, m_ssm_conv_b, m_ssm_dt_bias, m_ssm_a_log, m_ssm_d, m_ssm_norm_g, m_swa_q_norm_g, m_swa_k_norm_g, m_swa_sink, m_mla_q_lat_g, m_mla_kv_lat_g, m_w_mla_uq, m_w_mla_ukv, m_mla_q_norm_g, m_mla_k_norm_g, m_w_p_ssm, m_w_p_swa, m_w_p_mla, m_w_out, m_w_ffn_in, m_w_ffn_out, v_c_ctx, v_w_mod, v_b_mod, v_norm1_g, v_norm2_g, v_w_in, v_ssm_conv_w, v_ssm_conv_b, v_ssm_dt_bias, v_ssm_a_log, v_ssm_d, v_ssm_norm_g, v_swa_q_norm_g, v_swa_k_norm_g, v_swa_sink, v_mla_q_lat_g, v_mla_kv_lat_g, v_w_mla_uq, v_w_mla_ukv, v_mla_q_norm_g, v_mla_k_norm_g, v_w_p_ssm, v_w_p_swa, v_w_p_mla, v_w_out, v_w_ffn_in, v_w_ffn_out):
    loc = dict(locals())
    w = {k: loc[k] for k in WEIGHTS}
    m = {k: loc["m_" + k] for k in WEIGHTS}
    v = {k: loc["v_" + k] for k in WEIGHTS}
    nlat = x.shape[1]

    gathered = _unpack_big(gather_chips("gather_weights", _pack_big(w, BF16)), (4,))
    full = {k: _full_from_chips(k, gathered[k]) for k in BIG}
    conv_sh = jnp.pad(ssm_conv_w.reshape(10, 384), ((0, 6), (0, 0)))
    conv_full = gather_chips("gather_conv", conv_sh)[:, :10].reshape(4, 2, 5, 384).transpose(1, 2, 0, 3).reshape(2, 5, 1536)

    W = [layer_params(i, full, conv_full, w, nlat) for i in range(DEPTH)]

    loss_loc, dx, grads = local_step(x[0], c, ctx[0], loss_target[0], c_ctx, W, nlat)

    gfull = big_grads(grads)
    by_chip = {k: jnp.stack([_chips_from_full(k, a) for a in gfull[k]], axis=1) for k in BIG}
    send = jnp.concatenate([by_chip[k].astype(BF16).reshape(4, -1, 1024) for k in BIG], axis=1)
    recv = scatter_chips("scatter_grads", send)
    mine = sum_blocks("sum_chips", recv)
    other = sibling_swap("swap_cores", mine)
    gbig = _unpack_big(add_pair("sum_cores", mine, other), ())

    gs = small_grads(grads)
    small_all = gather_all("gather_small", _pack_small(gs))
    small_sum = sum_blocks("sum_small", small_all)
    full_shapes = {k: (w[k].shape if k != "ssm_conv_w" else (2, 5, 1536)) for k in SMALL}
    gsmall = _unpack_small(small_sum, full_shapes)
    chip = 2 * lax.axis_index("x") + lax.axis_index("y")
    gsmall["ssm_conv_w"] = lax.dynamic_slice_in_dim(gsmall["ssm_conv_w"], chip * 384, 384, axis=2)

    grad = {**gbig, **gsmall}
    delta, new_m, new_v = {}, {}, {}
    sm = {k: _pack_small_local(d) for k, d in (("w", w), ("g", grad), ("m", m), ("v", v))}
    r = adamw("adamw_small", sm["w"], sm["g"], sm["m"], sm["v"])
    shapes = {k: w[k].shape for k in SMALL}
    for dst, buf in zip((delta, new_m, new_v), r):
        dst.update(_unpack_small_local(buf, shapes))
    for k in BIG:
        sh = w[k].shape
        r = adamw("adamw_" + k, *[a[k].reshape(sh[0] * sh[1], sh[2]) for a in (w, grad, m, v)])
        for dst, buf in zip((delta, new_m, new_v), r):
            dst[k] = buf.reshape(sh)

    loss = lax.psum(loss_loc, ("x", "y", "c"))
    return (loss, dx[None, :nlat], *[grad[k] for k in WEIGHTS], *[delta[k] for k in WEIGHTS],
            *[new_m[k] for k in WEIGHTS], *[new_v[k] for k in WEIGHTS])


def _pack_small_local(d):
    parts = []
    for k in SMALL:
        a = d[k].astype(F32).reshape(-1)
        parts.append(jnp.pad(a, (0, (-a.shape[0]) % 1024)))
    return jnp.concatenate(parts).reshape(-1, 128)


def _unpack_small_local(buf, shapes):
    flat = buf.reshape(-1)
    out = {}
    o = 0
    for k in SMALL:
        sz = math.prod(shapes[k])
        out[k] = flat[o:o + sz].reshape(shapes[k])
        o += sz + (-sz) % 1024
    return out
```

```python
import functools
import math

import jax
import jax.numpy as jnp
from jax import lax
from jax.experimental import pallas as pl
from jax.experimental.pallas import tpu as pltpu

F32 = jnp.float32
BF16 = jnp.bfloat16
MESH = pl.DeviceIdType.MESH

D = 1024
NCTX = 256
EPS = 1e-6
ROPE_BASE = 10000.0
GRID_W = 64
DEPTH = 2
Q = 128
N_HEADS_SSM = 16
SWA_HQ, SWA_HKV, SWA_DH, SWA_WIN = 8, 2, 128, 128
MLA_H, MLA_NOPE, MLA_ROPE, MLA_V = 8, 128, 64, 128
MLA_QRANK, MLA_KVRANK = 384, 256
FFN = 2816
RT = 256
VMEM_LIMIT = 56 << 20
NEG = -1e30
LOG2E = 1.4426950408889634

C_G1, C_G2, C_G3, C_Z, C_Q, C_XS, C_B, C_C, C_K, C_V, C_CKV, C_MISC, C_PAD, C_CQ = (
    0, 1024, 2048, 3072, 4096, 5120, 6144, 6400, 6656, 6912, 7168, 7424, 7552, 7680)
UW = 8064
DT_LANE = 64

ADAM_LR, ADAM_B1, ADAM_B2, ADAM_EPS, ADAM_WD, ADAM_STEP = 0.001, 0.9, 0.999, 1e-08, 0.01, 10


def _cp(sem):
    return pltpu.CompilerParams(dimension_semantics=sem, vmem_limit_bytes=VMEM_LIMIT)


def _pick(n, cands):
    for c in cands:
        if n % c == 0:
            return c
    return n


_TN = (1536, 1408, 1152, 1024, 896, 768, 512, 384, 256, 128)


def mm(a, b, out_dtype, name, trans_b=False):
    m, k = a.shape
    n = b.shape[0] if trans_b else b.shape[1]
    tm = _pick(m, (768, 512, 256, 128, 8))
    tn = _pick(n, _TN)
    tk = k if k <= 2048 else _pick(k, (1408, 1152, 1024, 896, 768, 512))
    nk = k // tk
    b_spec = (pl.BlockSpec((tn, tk), lambda i, j, kk: (j, kk)) if trans_b
              else pl.BlockSpec((tk, tn), lambda i, j, kk: (kk, j)))

    def body(a_ref, b_ref, o_ref, *acc):
        p = _d(a_ref[...], b_ref[...], ((1,), (1 if trans_b else 0,)))
        if nk == 1:
            o_ref[...] = p.astype(out_dtype)
        else:
            kk = pl.program_id(2)

            @pl.when(kk == 0)
            def _():
                acc[0][...] = p

            @pl.when(kk > 0)
            def _():
                acc[0][...] += p

            @pl.when(kk == nk - 1)
            def _():
                o_ref[...] = acc[0][...].astype(out_dtype)

    return pl.pallas_call(
        body, out_shape=jax.ShapeDtypeStruct((m, n), out_dtype), grid=(m // tm, n // tn, nk),
        in_specs=[pl.BlockSpec((tm, tk), lambda i, j, kk: (i, kk)), b_spec],
        out_specs=pl.BlockSpec((tm, tn), lambda i, j, kk: (i, j)),
        scratch_shapes=[] if nk == 1 else [pltpu.VMEM((tm, tn), F32)],
        name=name, compiler_params=_cp(("parallel", "parallel", "arbitrary")))(a, b)


def mm_tn(a, b, name, out_dtype=BF16):
    t, ka = a.shape
    _, nb = b.shape
    ta = _pick(ka, (1024, 1408, 768, 512, 384, 256, 128))
    tb = _pick(nb, _TN)
    tt = _pick(t, (768, 512, 256, 128, 8))
    nt = t // tt

    def body(a_ref, b_ref, o_ref, acc):
        p = _d(a_ref[...], b_ref[...], ((0,), (0,)))
        s = pl.program_id(2)

        @pl.when(s == 0)
        def _():
            acc[...] = p

        @pl.when(s > 0)
        def _():
            acc[...] += p

        @pl.when(s == nt - 1)
        def _():
            o_ref[...] = acc[...].astype(out_dtype)

    return pl.pallas_call(
        body, out_shape=jax.ShapeDtypeStruct((ka, nb), out_dtype), grid=(ka // ta, nb // tb, nt),
        in_specs=[pl.BlockSpec((tt, ta), lambda i, j, s: (s, i)), pl.BlockSpec((tt, tb), lambda i, j, s: (s, j))],
        out_specs=pl.BlockSpec((ta, tb), lambda i, j, s: (i, j)), scratch_shapes=[pltpu.VMEM((ta, tb), F32)],
        name=name, compiler_params=_cp(("parallel", "parallel", "arbitrary")))(a, b)


def _rms(x, g, n=None):
    n = x.shape[-1] if n is None else n
    r = lax.rsqrt(jnp.sum(x * x, axis=-1, keepdims=True) * (1.0 / n) + EPS)
    return x * r * g


def _silu(x):
    return x * jax.nn.sigmoid(x)


def _modulate(x, g, sc, sh):
    return _rms(x, g) * (1.0 + sc) + sh


def _swap(x, s):
    ax = x.ndim - 1
    w = x.shape[ax]
    lane = lax.broadcasted_iota(jnp.int32, x.shape, ax)
    lo = (lane & s) == 0
    return jnp.where(lo, pltpu.roll(x, w - s, ax), pltpu.roll(x, s, ax))


@functools.partial(jax.custom_vjp, nondiff_argnums=(3,))
def _rope(x, cos, sin, s):
    return x * cos + _swap(x, s) * sin


def _rope_fwd(x, cos, sin, s):
    return _rope(x, cos, sin, s), (cos, sin)


def _rope_bwd(s, res, g):
    cos, sin = res
    return g * cos - _swap(g, s) * sin, jnp.zeros_like(cos), jnp.zeros_like(sin)


_rope.defvjp(_rope_fwd, _rope_bwd)


@jax.custom_vjp
def _softplus(x):
    return jnp.maximum(x, 0.0) + jnp.log(1.0 + jnp.exp(-jnp.abs(x)))


def _softplus_fwd(x):
    return _softplus(x), x


def _softplus_bwd(x, g):
    return (g * jax.nn.sigmoid(x),)


_softplus.defvjp(_softplus_fwd, _softplus_bwd)


def _d(a, b, dims):
    return lax.dot_general(a.astype(BF16), b.astype(BF16), (dims, ((), ())), preferred_element_type=F32)


@jax.custom_vjp
def bdot(a, b):
    return _d(a, b, ((1,), (0,)))


bdot.defvjp(lambda a, b: (bdot(a, b), (a, b)),
            lambda r, g: (_d(g, r[1], ((1,), (1,))), _d(r[0], g, ((0,), (0,)))))


@jax.custom_vjp
def bdot_nt(a, b):
    return _d(a, b, ((1,), (1,)))


bdot_nt.defvjp(lambda a, b: (bdot_nt(a, b), (a, b)),
               lambda r, g: (_d(g, r[1], ((1,), (0,))), _d(g, r[0], ((0,), (0,)))))


@jax.custom_vjp
def bdot_tn(a, b):
    return _d(a, b, ((0,), (0,)))


bdot_tn.defvjp(lambda a, b: (bdot_tn(a, b), (a, b)),
               lambda r, g: (_d(r[1], g, ((1,), (1,))), _d(r[0], g, ((1,), (0,)))))


def _tri(rev):
    i = lax.broadcasted_iota(jnp.int32, (Q, Q), 0)
    j = lax.broadcasted_iota(jnp.int32, (Q, Q), 1)
    return (i <= j) if rev else (i >= j)


def _split3(a):
    hi = a.astype(BF16)
    r = a - hi.astype(F32)
    mid = r.astype(BF16)
    lo = (r - mid.astype(F32)).astype(BF16)
    return hi, mid, lo


def _cum_cols_impl(a, rev):
    t = _tri(rev).astype(BF16)
    return sum(jnp.dot(t, p, preferred_element_type=F32) for p in _split3(a))


def _cum_rows_impl(a, rev):
    t = _tri(not rev).astype(BF16)
    return sum(jnp.dot(p, t, preferred_element_type=F32) for p in _split3(a))


@functools.partial(jax.custom_vjp, nondiff_argnums=(1,))
def cum_cols(a, rev):
    return _cum_cols_impl(a, rev)


cum_cols.defvjp(lambda a, rev: (_cum_cols_impl(a, rev), None), lambda rev, _, g: (_cum_cols_impl(g, not rev),))


@functools.partial(jax.custom_vjp, nondiff_argnums=(1,))
def cum_rows(a, rev):
    return _cum_rows_impl(a, rev)


cum_rows.defvjp(lambda a, rev: (_cum_rows_impl(a, rev), None), lambda rev, _, g: (_cum_rows_impl(g, not rev),))


def _rs(w, cb=0):
    return pl.BlockSpec((RT, w), lambda i: (i, cb))


def _ps(shape):
    nd = len(shape)
    return pl.BlockSpec(shape, lambda i: (0,) * nd)


def _gs(w, cb, nlat):
    return pl.BlockSpec((1, 1, w), lambda i: (i // nlat, 0, cb))


def _rowcall(name, body, n, ins, outs, scratch=()):
    return pl.pallas_call(
        body, out_shape=[o[0] for o in outs], grid=(n // RT,), in_specs=[s for _, s in ins],
        out_specs=[s for _, s in outs], scratch_shapes=list(scratch), name=name,
        compiler_params=_cp(("arbitrary",)))(*[a for a, _ in ins])


def _acc(ref, val, first):
    @pl.when(first)
    def _():
        ref[...] = val

    @pl.when(jnp.logical_not(first))
    def _():
        ref[...] += val


def _sd(shape, dt):
    return jax.ShapeDtypeStruct(shape, dt)


def resid_mod_fwd(name, xp, o, mod_gt, gt_i, mod_n, sh_i, sc_i, norm_g, nlat):
    n = xp.shape[0]
    has_res = o is not None

    def body(*refs):
        if has_res:
            xp_ref, o_ref, gt_ref, sh_ref, sc_ref, g_ref, xn_ref, h_ref = refs
            xn = xp_ref[...] + gt_ref[0] * o_ref[...]
            xn_ref[...] = xn
        else:
            xp_ref, sh_ref, sc_ref, g_ref, h_ref = refs
            xn = xp_ref[...]
        h_ref[...] = _modulate(xn, g_ref[...], sc_ref[0], sh_ref[0]).astype(BF16)

    ins = [(xp, _rs(D))]
    if has_res:
        ins += [(o, _rs(D)), (mod_gt, _gs(D, gt_i, nlat))]
    ins += [(mod_n, _gs(D, sh_i, nlat)), (mod_n, _gs(D, sc_i, nlat)), (norm_g, _ps((1, D)))]
    outs = ([(_sd((n, D), F32), _rs(D))] if has_res else []) + [(_sd((n, D), BF16), _rs(D))]
    r = _rowcall(name, body, n, ins, outs)
    return (r[0], r[1]) if has_res else (xp, r[0])


def resid_mod_bwd(name, xn, dxn, dh, o, mod_gt, gt_i, mod_n, sh_i, sc_i, norm_g, nlat):
    n = xn.shape[0]
    has_res = o is not None

    def body(*refs):
        i = pl.program_id(0)
        if has_res:
            (xn_ref, dxn_ref, dh_ref, o_ref, gt_ref, sh_ref, sc_ref, g_ref,
             dx_ref, do_ref, dgt_ref, dsh_ref, dsc_ref, dg_ref) = refs
        else:
            xn_ref, dxn_ref, dh_ref, sh_ref, sc_ref, g_ref, dx_ref, dsh_ref, dsc_ref, dg_ref = refs
        _, vjp = jax.vjp(_modulate, xn_ref[...], g_ref[...], sc_ref[0], sh_ref[0])
        dx, dg, dsc, dsh = vjp(dh_ref[...])
        dx = dx + dxn_ref[...]
        dx_ref[...] = dx
        gfirst = (i == 0) | (i == nlat)
        _acc(dg_ref, dg, i == 0)
        _acc(dsh_ref, dsh[None], gfirst)
        _acc(dsc_ref, dsc[None], gfirst)
        if has_res:
            do_ref[...] = (gt_ref[0] * dx).astype(BF16)
            _acc(dgt_ref, jnp.sum(dx * o_ref[...], axis=0, keepdims=True)[None], gfirst)

    ins = [(xn, _rs(D)), (dxn, _rs(D)), (dh, _rs(D))]
    if has_res:
        ins += [(o, _rs(D)), (mod_gt, _gs(D, gt_i, nlat))]
    ins += [(mod_n, _gs(D, sh_i, nlat)), (mod_n, _gs(D, sc_i, nlat)), (norm_g, _ps((1, D)))]
    gacc = (_sd((2, 1, D), F32), _gs(D, 0, nlat))
    outs = [(_sd((n, D), F32), _rs(D))]
    if has_res:
        outs += [(_sd((n, D), BF16), _rs(D)), gacc]
    outs += [gacc, gacc, (_sd((1, D), F32), _ps((1, D)))]
    r = _rowcall(name, body, n, ins, outs)
    if has_res:
        return r
    return r[0], None, None, r[1], r[2], r[3]


def resid_loss(name, xp, o, mod_gt, gt_i, target, nlat):
    n = xp.shape[0]

    def body(xp_ref, o_ref, gt_ref, t_ref, loss_ref, dx_ref, do_ref, dgt_ref):
        i = pl.program_id(0)
        gt = gt_ref[0]

        @pl.when(i < nlat)
        def _():
            err = xp_ref[...] + gt * o_ref[...] - t_ref[...]
            dx = err * (1.0 / D)
            dx_ref[...] = dx
            do_ref[...] = (gt * dx).astype(BF16)
            _acc(loss_ref, jnp.full((1, 128), 0.5 / D, F32) * jnp.sum(err * err), i == 0)
            _acc(dgt_ref, jnp.sum(dx * o_ref[...], axis=0, keepdims=True)[None], i == 0)

        @pl.when(i >= nlat)
        def _():
            dx_ref[...] = jnp.zeros((RT, D), F32)
            do_ref[...] = jnp.zeros((RT, D), BF16)
            dgt_ref[...] = jnp.zeros((1, 1, D), F32)

    tgt_spec = pl.BlockSpec((RT, D), lambda i: (jnp.minimum(i, nlat - 1), 0))
    ins = [(xp, _rs(D)), (o, _rs(D)), (mod_gt, _gs(D, gt_i, nlat)), (target, tgt_spec)]
    outs = [(_sd((1, 128), F32), _ps((1, 128))), (_sd((n, D), F32), _rs(D)), (_sd((n, D), BF16), _rs(D)),
            (_sd((2, 1, D), F32), _gs(D, 0, nlat))]
    return _rowcall(name, body, n, ins, outs)


def mod_fwd(name, c8, w_mod, b_mod):
    tn = 1536

    def body(c_ref, w_ref, b_ref, o_ref, s_ref):
        s = _silu(c_ref[...]).astype(BF16)
        s_ref[...] = s
        o_ref[...] = jnp.dot(s, w_ref[...], preferred_element_type=F32) + b_ref[...]

    return pl.pallas_call(
        body, out_shape=[_sd((8, 6 * D), F32), _sd((8, D), BF16)], grid=(6 * D // tn,),
        in_specs=[pl.BlockSpec((8, D), lambda j: (0, 0)), pl.BlockSpec((D, tn), lambda j: (0, j)),
                  pl.BlockSpec((1, tn), lambda j: (0, j))],
        out_specs=[pl.BlockSpec((8, tn), lambda j: (0, j)), pl.BlockSpec((8, D), lambda j: (0, 0))],
        name=name, compiler_params=_cp(("arbitrary",)))(c8, w_mod, b_mod)


def mod_small_bwd(name, c8, dsilu, dmod8):
    def body(c_ref, ds_ref, dm_ref, dc_ref, db_ref):
        _, vjp = jax.vjp(_silu, c_ref[...])
        dc_ref[...] = vjp(ds_ref[...])[0]
        db_ref[...] = jnp.sum(dm_ref[...], axis=0, keepdims=True)

    return pl.pallas_call(
        body, out_shape=[_sd((8, D), F32), _sd((1, 6 * D), F32)], grid=(1,),
        in_specs=[pl.BlockSpec((8, D), lambda j: (0, 0)), pl.BlockSpec((8, D), lambda j: (0, 0)),
                  pl.BlockSpec((8, 6 * D), lambda j: (0, 0))],
        out_specs=[pl.BlockSpec((8, D), lambda j: (0, 0)), pl.BlockSpec((1, 6 * D), lambda j: (0, 0))],
        name=name, compiler_params=_cp(("arbitrary",)))(c8, dsilu, dmod8)


def _conv_taps(x, nlat):
    n = x.shape[0]
    r = lax.broadcasted_iota(jnp.int32, x.shape, 0)
    lo = jnp.where(r < nlat, 0, nlat)
    hi = jnp.where(r < nlat, nlat, n)
    taps = []
    for o in (-2, -1, 0, 1, 2):
        xs = x if o == 0 else pltpu.roll(x, (-o) % n, 0)
        t = r + o
        taps.append(jnp.where((t >= lo) & (t < hi), xs, 0.0))
    return taps


def conv_fwd(name, u, w, b, nlat_rows):
    n = u.shape[0]

    def body(x_ref, w_ref, b_ref, o_ref):
        taps = _conv_taps(x_ref[...], nlat_rows)
        wv = w_ref[...]
        pre = b_ref[...] + sum(taps[k] * wv[k:k + 1, :] for k in range(5))
        o_ref[...] = _silu(pre)

    return pl.pallas_call(
        body, out_shape=_sd((n, 1536), F32), grid=(12,),
        in_specs=[pl.BlockSpec((n, 128), lambda j: (0, C_XS // 128 + j)), pl.BlockSpec((5, 128), lambda j: (0, j)),
                  pl.BlockSpec((1, 128), lambda j: (0, j))],
        out_specs=pl.BlockSpec((n, 128), lambda j: (0, j)),
        name=name, compiler_params=_cp(("parallel",)))(u, w, b)


def conv_bwd(name, u, dact, w, b, nlat_rows):
    n = u.shape[0]

    def body(x_ref, da_ref, w_ref, b_ref, dx_ref, dw_ref, db_ref):
        taps = _conv_taps(x_ref[...], nlat_rows)
        wv = w_ref[...]
        pre = b_ref[...] + sum(taps[k] * wv[k:k + 1, :] for k in range(5))
        s = jax.nn.sigmoid(pre)
        dpre = da_ref[...] * (s * (1.0 + pre * (1.0 - s)))
        db_ref[...] = jnp.sum(dpre, axis=0, keepdims=True)
        rows = lax.broadcasted_iota(jnp.int32, (5, 128), 0)
        dw = jnp.zeros((5, 128), F32)
        for k in range(5):
            dw = dw + jnp.where(rows == k, jnp.sum(dpre * taps[k], axis=0, keepdims=True), 0.0)
        dw_ref[...] = dw
        r = lax.broadcasted_iota(jnp.int32, dpre.shape, 0)
        lo = jnp.where(r < nlat_rows, 0, nlat_rows)
        hi = jnp.where(r < nlat_rows, nlat_rows, n)
        dx = jnp.zeros_like(dpre)
        for k in range(5):
            o = k - 2
            ds = dpre if o == 0 else pltpu.roll(dpre, o % n, 0)
            t = r - o
            dx = dx + jnp.where((t >= lo) & (t < hi), ds, 0.0) * wv[k:k + 1, :]
        dx_ref[...] = dx.astype(BF16)

    return pl.pallas_call(
        body, out_shape=[_sd((n, 1536), BF16), _sd((5, 1536), F32), _sd((1, 1536), F32)], grid=(12,),
        in_specs=[pl.BlockSpec((n, 128), lambda j: (0, C_XS // 128 + j)), pl.BlockSpec((n, 128), lambda j: (0, j)),
                  pl.BlockSpec((5, 128), lambda j: (0, j)), pl.BlockSpec((1, 128), lambda j: (0, j))],
        out_specs=[pl.BlockSpec((n, 128), lambda j: (0, j)), pl.BlockSpec((5, 128), lambda j: (0, j)),
                   pl.BlockSpec((1, 128), lambda j: (0, j))],
        name=name, compiler_params=_cp(("parallel",)))(u, dact, w, b)


def _ssd_chunk(rev, dirn, g, x4, bm, cm, misc, dtrow, bias_c, alog_c, bias_r, alog_r, h4):
    dt_c = _softplus(misc + bias_c)
    a_c = dt_c * (-jnp.exp(alog_c))
    dt_r = _softplus(dtrow + bias_r)
    a_r = dt_r * (-jnp.exp(alog_r))
    cs_c = cum_cols(a_c, rev)
    cs_r = cum_rows(a_r, rev)
    tot_c = jnp.sum(a_c, axis=0, keepdims=True)
    cb = bdot_nt(cm, bm)
    tri = _tri(rev)
    lane = lax.broadcasted_iota(jnp.int32, (1, 128), 1)
    row16 = lax.broadcasted_iota(jnp.int32, (16, 1), 0)
    prow = lax.broadcasted_iota(jnp.int32, (128, 1), 0)
    ys, hs = [], []
    for p in range(4):
        ydiag = 0.0
        wst = 0.0
        eoff = 0.0
        hscale = 0.0
        for e in range(2):
            hg = 8 * g + 2 * p + e
            oh_c = (lane == DT_LANE + 16 * dirn + hg).astype(F32)
            dt_h = jnp.sum(dt_c * oh_c, axis=1, keepdims=True)
            cs_h = jnp.sum(cs_c * oh_c, axis=1, keepdims=True)
            tot_h = jnp.sum(tot_c * oh_c, axis=1, keepdims=True)
            csr_h = jnp.sum(cs_r * (row16 == hg).astype(F32), axis=0, keepdims=True)
            seg = jnp.exp(jnp.where(tri, cs_h - csr_h, -jnp.inf))
            hm = ((lane < 64) if e == 0 else (lane >= 64)).astype(F32)
            ydiag = ydiag + bdot(cb * seg, x4[p] * (dt_h * hm))
            wst = wst + (dt_h * jnp.exp(tot_h - cs_h)) * hm
            eoff = eoff + jnp.exp(cs_h) * hm
            hscale = hscale + jnp.exp(tot_h) * ((prow < 64) if e == 0 else (prow >= 64)).astype(F32)
        ys.append(ydiag + bdot_nt(cm, h4[p]) * eoff)
        hs.append(h4[p] * hscale + bdot_tn(x4[p] * wst, bm))
    return ys, hs


def _ssd_specs(nlat_chunks, rev, dirn, bwd):
    nc = nlat_chunks + 2

    def chunk(s):
        if bwd:
            s = nc - 1 - s
        return (nlat_chunks + 1 - s) if rev else (s + nlat_chunks) % nc

    def step(s):
        return (nc - 1 - s) if bwd else s

    return dict(
        x=pl.BlockSpec((Q, 512), lambda g, s: (chunk(s), g)),
        b=pl.BlockSpec((Q, 128), lambda g, s: (chunk(s), 8 + g)),
        c=pl.BlockSpec((Q, 128), lambda g, s: (chunk(s), 10 + g)),
        misc=pl.BlockSpec((Q, 128), lambda g, s: (chunk(s), C_MISC // 128)),
        dtrow=pl.BlockSpec((16, Q), lambda g, s: (dirn, chunk(s))),
        p_c=pl.BlockSpec((1, 128), lambda g, s: (0, 0)),
        p_r=pl.BlockSpec((16, 1), lambda g, s: (dirn, 0)),
        y=pl.BlockSpec((Q, 512), lambda g, s: (chunk(s), g)),
        hsave=pl.BlockSpec((1, 1, 512, 128), lambda g, s: (g, step(s), 0, 0)),
        bc_out=pl.BlockSpec((Q, 128), lambda g, s: (chunk(s), g)),
        misc_out=pl.BlockSpec((1, Q, 128), lambda g, s: (g, chunk(s), 0)),
        dtrow_out=pl.BlockSpec((1, 16, Q), lambda g, s: (g, 0, chunk(s))),
        pacc_c=pl.BlockSpec((1, 128), lambda g, s: (0, 0)),
        pacc_r=pl.BlockSpec((16, 1), lambda g, s: (0, 0)),
    )


def ssd_fwd(name, xbc, u, dtrow, bias_c, alog_c, bias_r, alog_r, nlat_chunks, rev, dirn):
    n = xbc.shape[0]
    nc = nlat_chunks + 2
    sp = _ssd_specs(nlat_chunks, rev, dirn, False)

    def body(x_ref, b_ref, c_ref, m_ref, r_ref, bc_ref, ac_ref, br_ref, ar_ref, y_ref, hs_ref, h_s):
        g = pl.program_id(0)
        s = pl.program_id(1)

        @pl.when(s == 0)
        def _():
            h_s[...] = jnp.zeros((512, 128), F32)

        hs_ref[0, 0] = h_s[...]
        x4 = [x_ref[:, 128 * p:128 * p + 128] for p in range(4)]
        h4 = [h_s[128 * p:128 * p + 128, :] for p in range(4)]
        ys, hs = _ssd_chunk(rev, dirn, g, x4, b_ref[...], c_ref[...], m_ref[...], r_ref[...],
                            bc_ref[...], ac_ref[...], br_ref[...], ar_ref[...], h4)
        for p in range(4):
            y_ref[:, 128 * p:128 * p + 128] = ys[p]
            h_s[128 * p:128 * p + 128, :] = hs[p]

    return pl.pallas_call(
        body, out_shape=[_sd((n, 1024), F32), _sd((2, nc, 512, 128), F32)], grid=(2, nc),
        in_specs=[sp["x"], sp["b"], sp["c"], sp["misc"], sp["dtrow"], sp["p_c"], sp["p_c"], sp["p_r"], sp["p_r"]],
        out_specs=[sp["y"], sp["hsave"]], scratch_shapes=[pltpu.VMEM((512, 128), F32)],
        name=name, compiler_params=_cp(("arbitrary", "arbitrary")))(
            xbc, xbc, xbc, u, dtrow, bias_c, alog_c, bias_r, alog_r)


def ssd_bwd(name, xbc, u, dtrow, bias_c, alog_c, bias_r, alog_r, hsave, dy, acc, nlat_chunks, rev, dirn):
    n = xbc.shape[0]
    sp = _ssd_specs(nlat_chunks, rev, dirn, True)

    def body(x_ref, b_ref, c_ref, m_ref, r_ref, bc_ref, ac_ref, br_ref, ar_ref, hs_ref, dy_ref, ax_ref, ab_ref, acc_ref,
             dx_ref, db_ref, dc_ref, dm_ref, dr_ref, dbc_ref, dac_ref, dbr_ref, dar_ref, dh_s):
        g = pl.program_id(0)
        s = pl.program_id(1)

        @pl.when(s == 0)
        def _():
            dh_s[...] = jnp.zeros((512, 128), F32)

        x4 = [x_ref[:, 128 * p:128 * p + 128] for p in range(4)]
        h4 = [hs_ref[0, 0, 128 * p:128 * p + 128, :] for p in range(4)]
        fn = functools.partial(_ssd_chunk, rev, dirn, g)
        _, vjp = jax.vjp(fn, x4, b_ref[...], c_ref[...], m_ref[...], r_ref[...],
                         bc_ref[...], ac_ref[...], br_ref[...], ar_ref[...], h4)
        dys = [dy_ref[:, 128 * p:128 * p + 128] for p in range(4)]
        dhs = [dh_s[128 * p:128 * p + 128, :] for p in range(4)]
        dx4, db, dc, dm, dr, dbc, dac, dbr, dar, dh4 = vjp((dys, dhs))
        for p in range(4):
            dx_ref[:, 128 * p:128 * p + 128] = dx4[p] + ax_ref[:, 128 * p:128 * p + 128]
            dh_s[128 * p:128 * p + 128, :] = dh4[p]
        db_ref[...] = db + ab_ref[...]
        dc_ref[...] = dc + acc_ref[...]
        dm_ref[0] = dm
        dr_ref[0] = dr
        first = (g == 0) & (s == 0)
        _acc(dbc_ref, dbc, first)
        _acc(dac_ref, dac, first)
        _acc(dbr_ref, dbr, first)
        _acc(dar_ref, dar, first)

    ax, ab, ac = acc
    return pl.pallas_call(
        body,
        out_shape=[_sd((n, 1024), F32), _sd((n, 256), F32), _sd((n, 256), F32), _sd((2, n, 128), F32),
                   _sd((2, 16, n), F32), _sd((1, 128), F32), _sd((1, 128), F32), _sd((16, 1), F32), _sd((16, 1), F32)],
        grid=(2, nlat_chunks + 2),
        in_specs=[sp["x"], sp["b"], sp["c"], sp["misc"], sp["dtrow"], sp["p_c"], sp["p_c"], sp["p_r"], sp["p_r"],
                  sp["hsave"], sp["y"], sp["y"], sp["bc_out"], sp["bc_out"]],
        out_specs=[sp["y"], sp["bc_out"], sp["bc_out"], sp["misc_out"], sp["dtrow_out"],
                   sp["pacc_c"], sp["pacc_c"], sp["pacc_r"], sp["pacc_r"]],
        scratch_shapes=[pltpu.VMEM((512, 128), F32)],
        name=name, compiler_params=_cp(("arbitrary", "arbitrary")))(
            xbc, xbc, xbc, u, dtrow, bias_c, alog_c, bias_r, alog_r, hsave, dy, ax, ab, ac)


def _ssd_out(yf, yb, xs, z, g, dexp):
    return _rms((yf + yb + dexp * xs) * _silu(z), g)


def ssd_out_fwd(name, yf, yb, xbc, u, g, dexp):
    n = yf.shape[0]

    def body(yf_ref, yb_ref, xs_ref, z_ref, g_ref, d_ref, o_ref):
        o_ref[...] = _ssd_out(yf_ref[...], yb_ref[...], xs_ref[...], z_ref[...], g_ref[...], d_ref[...]).astype(BF16)

    return _rowcall(name, body, n,
                    [(yf, _rs(D)), (yb, _rs(D)), (xbc, _rs(D, 0)), (u, _rs(D, C_Z // D)), (g, _ps((1, D))), (dexp, _ps((1, D)))],
                    [(_sd((n, D), BF16), _rs(D))])[0]


def ssd_out_bwd(name, yf, yb, xbc, u, g, dexp, dys):
    n = yf.shape[0]

    def body(yf_ref, yb_ref, xs_ref, z_ref, g_ref, d_ref, dys_ref, dy_ref, dxs_ref, dz_ref, dg_ref, dd_ref):
        i = pl.program_id(0)
        _, vjp = jax.vjp(_ssd_out, yf_ref[...], yb_ref[...], xs_ref[...], z_ref[...], g_ref[...], d_ref[...])
        dyf, _, dxs, dz, dg, dd = vjp(dys_ref[...])
        dy_ref[...] = dyf
        dxs_ref[...] = dxs
        dz_ref[...] = dz.astype(BF16)
        _acc(dg_ref, dg, i == 0)
        _acc(dd_ref, dd, i == 0)

    return _rowcall(name, body, n,
                    [(yf, _rs(D)), (yb, _rs(D)), (xbc, _rs(D, 0)), (u, _rs(D, C_Z // D)), (g, _ps((1, D))), (dexp, _ps((1, D))),
                     (dys, _rs(D))],
                    [(_sd((n, D), F32), _rs(D)), (_sd((n, D), F32), _rs(D)), (_sd((n, D), BF16), _rs(D)),
                     (_sd((1, D), F32), _ps((1, D))), (_sd((1, D), F32), _ps((1, D)))])


def _normrope(x, g, cos, sin, s, n=None):
    return _rope(_rms(x, g, n), cos, sin, s)


def swa_prep_fwd(name, u, gq, gk, cos, sin):
    n = u.shape[0]

    def body(q_ref, k_ref, gq_ref, gk_ref, cos_ref, sin_ref, qs_ref, ks_ref):
        cs, sn = cos_ref[...], sin_ref[...]
        for h in range(SWA_HQ):
            sl = slice(128 * h, 128 * h + 128)
            qs_ref[:, sl] = _normrope(q_ref[:, sl], gq_ref[...], cs, sn, 32).astype(BF16)
        for h in range(SWA_HKV):
            sl = slice(128 * h, 128 * h + 128)
            ks_ref[:, sl] = _normrope(k_ref[:, sl], gk_ref[...], cs, sn, 32).astype(BF16)

    return _rowcall(name, body, n,
                    [(u, _rs(1024, C_Q // 1024)), (u, _rs(256, C_K // 256)), (gq, _ps((1, 128))), (gk, _ps((1, 128))),
                     (cos, _rs(128)), (sin, _rs(128))],
                    [(_sd((n, 1024), BF16), _rs(1024)), (_sd((n, 256), BF16), _rs(256))])


def swa_prep_bwd(name, u, gq, gk, cos, sin, dqs, dks, dv):
    n = u.shape[0]

    def body(q_ref, k_ref, gq_ref, gk_ref, cos_ref, sin_ref, dqs_ref, dks_ref, dv_ref,
             dq_ref, dk_ref, dvo_ref, dgq_ref, dgk_ref):
        i = pl.program_id(0)
        cs, sn = cos_ref[...], sin_ref[...]
        fn = lambda x, g: _normrope(x, g, cs, sn, 32)
        dgq = jnp.zeros((1, 128), F32)
        dgk = jnp.zeros((1, 128), F32)
        for h in range(SWA_HQ):
            sl = slice(128 * h, 128 * h + 128)
            _, vjp = jax.vjp(fn, q_ref[:, sl], gq_ref[...])
            dx, dg = vjp(dqs_ref[:, sl])
            dq_ref[:, sl] = dx.astype(BF16)
            dgq = dgq + dg
        for h in range(SWA_HKV):
            sl = slice(128 * h, 128 * h + 128)
            _, vjp = jax.vjp(fn, k_ref[:, sl], gk_ref[...])
            dx, dg = vjp(dks_ref[:, sl])
            dk_ref[:, sl] = dx.astype(BF16)
            dgk = dgk + dg
        dvo_ref[...] = dv_ref[...].astype(BF16)
        _acc(dgq_ref, dgq, i == 0)
        _acc(dgk_ref, dgk, i == 0)

    return _rowcall(name, body, n,
                    [(u, _rs(1024, C_Q // 1024)), (u, _rs(256, C_K // 256)), (gq, _ps((1, 128))), (gk, _ps((1, 128))),
                     (cos, _rs(128)), (sin, _rs(128)), (dqs, _rs(1024)), (dks, _rs(256)), (dv, _rs(256))],
                    [(_sd((n, 1024), BF16), _rs(1024)), (_sd((n, 256), BF16), _rs(256)), (_sd((n, 256), BF16), _rs(256)),
                     (_sd((1, 128), F32), _ps((1, 128))), (_sd((1, 128), F32), _ps((1, 128)))])


def lat_norm_fwd(name, u, g_kv, g_q):
    n = u.shape[0]

    def body(ckv_ref, cq_ref, gkv_ref, gq_ref, okv_ref, oq_ref):
        okv_ref[...] = _rms(ckv_ref[...], gkv_ref[...]).astype(BF16)
        oq_ref[...] = _rms(cq_ref[...], gq_ref[...]).astype(BF16)

    return _rowcall(name, body, n,
                    [(u, _rs(256, C_CKV // 256)), (u, _rs(384, C_CQ // 384)), (g_kv, _ps((1, 256))), (g_q, _ps((1, 384)))],
                    [(_sd((n, 256), BF16), _rs(256)), (_sd((n, 384), BF16), _rs(384))])


def lat_norm_bwd(name, u, g_kv, g_q, dkvn, dqn):
    n = u.shape[0]

    def body(ckv_ref, cq_ref, gkv_ref, gq_ref, dkvn_ref, dqn_ref, dckv_ref, dcq_ref, dgkv_ref, dgq_ref):
        i = pl.program_id(0)
        _, vjp = jax.vjp(_rms, ckv_ref[...], gkv_ref[...])
        dx, dg = vjp(dkvn_ref[...])
        dckv_ref[...] = dx.astype(BF16)
        _acc(dgkv_ref, dg, i == 0)
        _, vjp = jax.vjp(_rms, cq_ref[...], gq_ref[...])
        dx, dg = vjp(dqn_ref[...])
        dcq_ref[...] = dx.astype(BF16)
        _acc(dgq_ref, dg, i == 0)

    return _rowcall(name, body, n,
                    [(u, _rs(256, C_CKV // 256)), (u, _rs(384, C_CQ // 384)), (g_kv, _ps((1, 256))), (g_q, _ps((1, 384))),
                     (dkvn, _rs(256)), (dqn, _rs(384))],
                    [(_sd((n, 256), BF16), _rs(256)), (_sd((n, 384), BF16), _rs(384)),
                     (_sd((1, 256), F32), _ps((1, 256))), (_sd((1, 384), F32), _ps((1, 384)))])


def _lane_lt64(x):
    return (lax.broadcasted_iota(jnp.int32, (1, 128), 1) < 64).astype(F32) * x


def _mla_krope(misc, g, cos, sin):
    return _normrope(_lane_lt64(misc), g, cos, sin, 16, MLA_ROPE)


def mla_prep_fwd(name, kv, qp, u, qg, kg, cos, sin):
    n = kv.shape[0]

    def body(kv_ref, v_ref, q_ref, m_ref, qg_ref, kg_ref, cos_ref, sin_ref, km_ref, qm_ref, vm_ref):
        cs, sn = cos_ref[...], sin_ref[...]
        vm_ref[...] = v_ref[...].astype(BF16)
        kr = _mla_krope(m_ref[...], kg_ref[:, 128:256], cs, sn).astype(BF16)
        for h in range(MLA_H):
            km_ref[:, 256 * h:256 * h + 128] = _rms(kv_ref[:, 128 * h:128 * h + 128], kg_ref[:, 0:128]).astype(BF16)
            km_ref[:, 256 * h + 128:256 * h + 256] = kr
            qm_ref[:, 256 * h:256 * h + 128] = _rms(q_ref[:, 256 * h:256 * h + 128], qg_ref[:, 0:128]).astype(BF16)
            qm_ref[:, 256 * h + 128:256 * h + 256] = _normrope(
                q_ref[:, 256 * h + 128:256 * h + 256], qg_ref[:, 128:256], cs, sn, 16, MLA_ROPE).astype(BF16)

    return _rowcall(name, body, n,
                    [(kv, _rs(1024, 0)), (kv, _rs(1024, 1)), (qp, _rs(2048)), (u, _rs(128, C_MISC // 128)), (qg, _ps((1, 256))),
                     (kg, _ps((1, 256))), (cos, _rs(128)), (sin, _rs(128))],
                    [(_sd((n, 2048), BF16), _rs(2048)), (_sd((n, 2048), BF16), _rs(2048)), (_sd((n, 1024), BF16), _rs(1024))])


def mla_prep_bwd(name, kv, qp, u, qg, kg, cos, sin, dkm, dqm, dv):
    n = kv.shape[0]

    def body(kv_ref, q_ref, m_ref, qg_ref, kg_ref, cos_ref, sin_ref, dkm_ref, dqm_ref, dv_ref,
             dkv_ref, dq_ref, dkr_ref, dqg_ref, dkg_ref):
        i = pl.program_id(0)
        cs, sn = cos_ref[...], sin_ref[...]
        fr = lambda x, g: _normrope(x, g, cs, sn, 16, MLA_ROPE)
        dkg_n = jnp.zeros((1, 128), F32)
        dqg_n = jnp.zeros((1, 128), F32)
        dqg_r = jnp.zeros((1, 128), F32)
        dkr_sum = jnp.zeros((RT, 128), F32)
        for h in range(MLA_H):
            _, vjp = jax.vjp(_rms, kv_ref[:, 128 * h:128 * h + 128], kg_ref[:, 0:128])
            dx, dg = vjp(dkm_ref[:, 256 * h:256 * h + 128])
            dkv_ref[:, 128 * h:128 * h + 128] = dx.astype(BF16)
            dkg_n = dkg_n + dg
            dkr_sum = dkr_sum + dkm_ref[:, 256 * h + 128:256 * h + 256]
            _, vjp = jax.vjp(_rms, q_ref[:, 256 * h:256 * h + 128], qg_ref[:, 0:128])
            dx, dg = vjp(dqm_ref[:, 256 * h:256 * h + 128])
            dq_ref[:, 256 * h:256 * h + 128] = dx.astype(BF16)
            dqg_n = dqg_n + dg
            _, vjp = jax.vjp(fr, q_ref[:, 256 * h + 128:256 * h + 256], qg_ref[:, 128:256])
            dx, dg = vjp(dqm_ref[:, 256 * h + 128:256 * h + 256])
            dq_ref[:, 256 * h + 128:256 * h + 256] = dx.astype(BF16)
            dqg_r = dqg_r + dg
        _, vjp = jax.vjp(lambda m, g: _mla_krope(m, g, cs, sn), m_ref[...], kg_ref[:, 128:256])
        dm, dkg_r = vjp(dkr_sum)
        dkr_ref[...] = dm
        dkv_ref[:, 1024:2048] = dv_ref[...].astype(BF16)
        _acc(dqg_ref.at[:, 0:128], dqg_n, i == 0)
        _acc(dqg_ref.at[:, 128:256], dqg_r, i == 0)
        _acc(dkg_ref.at[:, 0:128], dkg_n, i == 0)
        _acc(dkg_ref.at[:, 128:256], dkg_r, i == 0)

    return _rowcall(name, body, n,
                    [(kv, _rs(1024, 0)), (qp, _rs(2048)), (u, _rs(128, C_MISC // 128)), (qg, _ps((1, 256))), (kg, _ps((1, 256))),
                     (cos, _rs(128)), (sin, _rs(128)), (dkm, _rs(2048)), (dqm, _rs(2048)), (dv, _rs(1024))],
                    [(_sd((n, 2048), BF16), _rs(2048)), (_sd((n, 2048), BF16), _rs(2048)), (_sd((n, 128), F32), _rs(128)),
                     (_sd((1, 256), F32), _ps((1, 256))), (_sd((1, 256), F32), _ps((1, 256)))])


def misc_combine(name, dkr, dm_f, dm_b, drow_t):
    n = dkr.shape[0]

    def body(a_ref, f_ref, b_ref, r_ref, o_ref):
        o_ref[...] = (a_ref[...] + f_ref[0] + f_ref[1] + b_ref[0] + b_ref[1] + r_ref[...]).astype(BF16)

    g2 = pl.BlockSpec((2, RT, 128), lambda i: (0, i, 0))
    return _rowcall(name, body, n, [(dkr, _rs(128)), (dm_f, g2), (dm_b, g2), (drow_t, _rs(128))],
                    [(_sd((n, 128), BF16), _rs(128))])[0]


def _merge(g1, g2, g3, p1, p2, p3):
    return jax.nn.sigmoid(g1) * p1 + jax.nn.sigmoid(g2) * p2 + jax.nn.sigmoid(g3) * p3


def merge_fwd(name, u, p1, p2, p3):
    n = u.shape[0]

    def body(g1, g2, g3, a, b, c, o_ref):
        o_ref[...] = _merge(g1[...], g2[...], g3[...], a[...], b[...], c[...]).astype(BF16)

    return _rowcall(name, body, n, [(u, _rs(D, 0)), (u, _rs(D, 1)), (u, _rs(D, 2)), (p1, _rs(D)), (p2, _rs(D)), (p3, _rs(D))],
                    [(_sd((n, D), BF16), _rs(D))])[0]


def merge_bwd(name, u, p1, p2, p3, dm):
    n = u.shape[0]

    def body(g1, g2, g3, a, b, c, dm_ref, d1, d2, d3, dg_ref):
        _, vjp = jax.vjp(_merge, g1[...], g2[...], g3[...], a[...], b[...], c[...])
        r = vjp(dm_ref[...])
        for k in range(3):
            dg_ref[:, D * k:D * k + D] = r[k].astype(BF16)
        d1[...] = r[3].astype(BF16)
        d2[...] = r[4].astype(BF16)
        d3[...] = r[5].astype(BF16)

    return _rowcall(name, body, n,
                    [(u, _rs(D, 0)), (u, _rs(D, 1)), (u, _rs(D, 2)), (p1, _rs(D)), (p2, _rs(D)), (p3, _rs(D)), (dm, _rs(D))],
                    [(_sd((n, D), BF16), _rs(D))] * 3 + [(_sd((n, 3 * D), BF16), _rs(3 * D))])


def _swiglu(g, u):
    return _silu(g) * u


def swiglu_fwd(name, gu):
    n = gu.shape[0]

    def body(g_ref, u_ref, o_ref):
        o_ref[...] = _swiglu(g_ref[...], u_ref[...]).astype(BF16)

    return _rowcall(name, body, n, [(gu, _rs(FFN, 0)), (gu, _rs(FFN, 1))], [(_sd((n, FFN), BF16), _rs(FFN))])[0]


def swiglu_bwd(name, gu, da):
    n = gu.shape[0]

    def body(g_ref, u_ref, da_ref, o_ref):
        _, vjp = jax.vjp(_swiglu, g_ref[...], u_ref[...])
        dg, du = vjp(da_ref[...])
        o_ref[:, 0:FFN] = dg.astype(BF16)
        o_ref[:, FFN:2 * FFN] = du.astype(BF16)

    return _rowcall(name, body, n, [(gu, _rs(FFN, 0)), (gu, _rs(FFN, 1)), (da, _rs(FFN))],
                    [(_sd((n, 2 * FFN), BF16), _rs(2 * FFN))])[0]


FLASH_ROWS = 256


def _fold_lanes(x, op):
    acc = x[:, 0:128]
    for b in range(1, x.shape[1] // 128):
        acc = op(acc, x[:, 128 * b:128 * b + 128])
    return acc


def _band_mask(tq, tk, i, kb):
    qp = i * tq + lax.broadcasted_iota(jnp.int32, (tq, tk), 0)
    kp = kb * tk + lax.broadcasted_iota(jnp.int32, (tq, tk), 1)
    return jnp.abs(qp - kp) <= SWA_WIN


def flash_fwd(name, qa, ka, va, *, w, vw, hq, grp, vcol0, scale, nlat, tq, tk, band, sink, ctx_q, prev=None):
    n = qa.shape[0]
    cblk = nlat // NCTX
    band = band and not ctx_q
    assert not band, "latent rows of a banded attention go through swa_fwd_lat"
    if ctx_q:
        tq = tk = NCTX
        grid = (hq, 1, 1)
        qmap = lambda h, i, kk: (cblk, h)
        kmap = lambda h, i, kk: (cblk, h // grp)
        vmap = lambda h, i, kk: (cblk, vcol0 + h // grp)
        omap = lambda h, i, kk: (cblk, h)
        lmap = lambda h, i, kk: (h, cblk, 0)
    else:
        nb = nlat // tk
        nk = 3 if band else nb
        grid = (hq, nlat // tq, nk)
        kb_of = (lambda i, kk: jnp.clip(i + kk - 1, 0, nb - 1)) if band else (lambda i, kk: kk)
        qmap = lambda h, i, kk: (i, h)
        kmap = lambda h, i, kk: (kb_of(i, kk), h // grp)
        vmap = lambda h, i, kk: (kb_of(i, kk), vcol0 + h // grp)
        omap = lambda h, i, kk: (i, h)
        lmap = lambda h, i, kk: (h, i, 0)
    nk = grid[2]
    extra = not ctx_q
    has_sink = sink is not None

    def body(*refs):
        refs = list(refs)
        q_ref, k_ref, v_ref = refs[:3]
        pos = 3
        if extra:
            ke_ref, ve_ref = refs[pos:pos + 2]
            pos += 2
        if has_sink:
            s_ref = refs[pos]
            pos += 1
        if prev is not None:
            pos += 2
        o_ref, l_ref, m_s, l_s, a_s = refs[pos:pos + 5]
        kk = pl.program_id(2)
        tr = min(tq, FLASH_ROWS)

        def step(kblk, vblk):
            for r in range(tq // tr):
                rows = slice(r * tr, (r + 1) * tr)
                s = _d(q_ref[rows, :], kblk, ((1,), (1,))) * (scale * LOG2E)
                m_prev = m_s[rows, :]
                m_new = jnp.maximum(m_prev, jnp.max(_fold_lanes(s, jnp.maximum), axis=1, keepdims=True))
                alpha = jnp.exp2(m_prev - m_new)
                p = jnp.exp2(s - m_new)
                l_s[rows, :] = alpha * l_s[rows, :] + _fold_lanes(p, jnp.add)
                a_s[rows, :] = alpha * a_s[rows, :] + _d(p, vblk, ((1,), (0,)))
                m_s[rows, :] = m_new

        @pl.when(kk == 0)
        def _():
            if has_sink:
                sv = jnp.max(s_ref[0], axis=1, keepdims=True) * LOG2E
                m_s[...] = jnp.zeros((tq, 1), F32) + sv
                l_s[...] = (lax.broadcasted_iota(jnp.int32, (tq, 128), 1) == 0).astype(F32)
            else:
                m_s[...] = jnp.full((tq, 1), NEG, F32)
                l_s[...] = jnp.zeros((tq, 128), F32)
            a_s[...] = jnp.zeros((tq, vw), F32)
            if extra:
                step(ke_ref[...], ve_ref[...])

        step(k_ref[...], v_ref[...])

        @pl.when(kk == nk - 1)
        def _():
            l = jnp.sum(l_s[...], axis=1, keepdims=True)
            o_ref[...] = (a_s[...] / l).astype(BF16)
            l_ref[0] = m_s[...] + jnp.log2(l)

    ins = [(qa, pl.BlockSpec((tq, w), qmap)), (ka, pl.BlockSpec((tk, w), kmap)), (va, pl.BlockSpec((tk, vw), vmap))]
    if extra:
        ins += [(ka, pl.BlockSpec((NCTX, w), lambda h, i, kk: (cblk, h // grp))),
                (va, pl.BlockSpec((NCTX, vw), lambda h, i, kk: (cblk, vcol0 + h // grp)))]
    if has_sink:
        ins += [(sink, pl.BlockSpec((1, 1, 128), lambda h, i, kk: (h, 0, 0)))]
    aliases = {}
    if prev is not None:
        any_spec = pl.BlockSpec(memory_space=pl.ANY)
        aliases = {len(ins): 0, len(ins) + 1: 1}
        ins += [(prev[0], any_spec), (prev[1], any_spec)]
    return pl.pallas_call(
        body, out_shape=[_sd((n, hq * vw), BF16), _sd((hq, n, 1), F32)], grid=grid,
        in_specs=[s for _, s in ins],
        out_specs=[pl.BlockSpec((tq, vw), omap), pl.BlockSpec((1, tq, 1), lmap)],
        scratch_shapes=[pltpu.VMEM((tq, 1), F32), pltpu.VMEM((tq, 128), F32), pltpu.VMEM((tq, vw), F32)],
        input_output_aliases=aliases, name=name,
        compiler_params=_cp(("parallel", "parallel", "arbitrary")))(*[a for a, _ in ins])


def flash_dq(name, qa, ka, va, oa, doa, lse, *, w, vw, hq, grp, vcol0, scale, nlat, tq, tk, band, sink, ctx_q, prev=None):
    n = qa.shape[0]
    cblk = nlat // NCTX
    band = band and not ctx_q
    if ctx_q:
        tq = tk = NCTX
        grid = (hq, 1, 1)
        qmap = lambda h, i, kk: (cblk, h)
        kmap = lambda h, i, kk: (cblk, h // grp)
        vmap = lambda h, i, kk: (cblk, vcol0 + h // grp)
        lmap = lambda h, i, kk: (h, cblk, 0)
    else:
        nb = nlat // tk
        grid = (hq, nlat // tq, 3 if band else nb)
        kb_of = (lambda i, kk: jnp.clip(i + kk - 1, 0, nb - 1)) if band else (lambda i, kk: kk)
        qmap = lambda h, i, kk: (i, h)
        kmap = lambda h, i, kk: (kb_of(i, kk), h // grp)
        vmap = lambda h, i, kk: (kb_of(i, kk), vcol0 + h // grp)
        lmap = lambda h, i, kk: (h, i, 0)
    nk = grid[2]
    nq = grid[1]
    extra = not ctx_q
    has_sink = sink is not None

    def body(*refs):
        refs = list(refs)
        q_ref, k_ref, v_ref, o_ref, do_ref, l_ref = refs[:6]
        pos = 6
        if extra:
            ke_ref, ve_ref = refs[pos:pos + 2]
            pos += 2
        if has_sink:
            s_ref = refs[pos]
            pos += 1
        if prev is not None:
            pos += 2
        dq_ref, dl_ref, ds_ref, acc_s, dl_s = refs[pos:pos + 5]
        i = pl.program_id(1)
        kk = pl.program_id(2)
        q = q_ref[...]
        do = do_ref[...]
        lse_v = l_ref[0]

        def step(kblk, vblk, mask):
            s = _d(q, kblk, ((1,), (1,))) * (scale * LOG2E)
            if mask is not None:
                s = jnp.where(mask, s, NEG)
            p = jnp.exp2(s - lse_v)
            dp = _d(do, vblk, ((1,), (1,)))
            ds = p * (dp - dl_s[...]) * scale
            acc_s[...] += _d(ds, kblk, ((1,), (0,)))

        @pl.when(kk == 0)
        def _():
            delta = jnp.sum(do * o_ref[...].astype(F32), axis=1, keepdims=True)
            dl_s[...] = delta
            acc_s[...] = jnp.zeros((tq, w), F32)
            if has_sink:
                sv = jnp.max(s_ref[0], axis=1, keepdims=True) * LOG2E
                dsk = jnp.sum(-jnp.exp2(sv - lse_v) * delta, axis=0, keepdims=True)
                _acc(ds_ref, jnp.zeros((1, 1, 128), F32) + dsk, i == 0)
            else:
                ds_ref[...] = jnp.zeros((1, 1, 128), F32)
            if extra:
                step(ke_ref[...], ve_ref[...], None)

        if band:
            kb = i + kk - 1

            @pl.when((kb >= 0) & (kb < nlat // tk))
            def _():
                step(k_ref[...], v_ref[...], _band_mask(tq, tk, i, kb))
        else:
            step(k_ref[...], v_ref[...], None)

        @pl.when(kk == nk - 1)
        def _():
            dq_ref[...] = acc_s[...]
            dl_ref[0] = dl_s[...]

    ins = [(qa, pl.BlockSpec((tq, w), qmap)), (ka, pl.BlockSpec((tk, w), kmap)), (va, pl.BlockSpec((tk, vw), vmap)),
           (oa, pl.BlockSpec((tq, vw), qmap)), (doa, pl.BlockSpec((tq, vw), qmap)), (lse, pl.BlockSpec((1, tq, 1), lmap))]
    if extra:
        ins += [(ka, pl.BlockSpec((NCTX, w), lambda h, i, kk: (cblk, h // grp))),
                (va, pl.BlockSpec((NCTX, vw), lambda h, i, kk: (cblk, vcol0 + h // grp)))]
    if has_sink:
        ins += [(sink, pl.BlockSpec((1, 1, 128), lambda h, i, kk: (h, 0, 0)))]
    aliases = {}
    if prev is not None:
        any_spec = pl.BlockSpec(memory_space=pl.ANY)
        aliases = {len(ins): 0, len(ins) + 1: 1}
        ins += [(prev[0], any_spec), (prev[1], any_spec)]
    del nq
    return pl.pallas_call(
        body, out_shape=[_sd((n, hq * w), F32), _sd((hq, n, 1), F32), _sd((hq, 1, 128), F32)], grid=grid,
        in_specs=[s for _, s in ins],
        out_specs=[pl.BlockSpec((tq, w), qmap), pl.BlockSpec((1, tq, 1), lmap),
                   pl.BlockSpec((1, 1, 128), lambda h, i, kk: (h, 0, 0))],
        scratch_shapes=[pltpu.VMEM((tq, w), F32), pltpu.VMEM((tq, 1), F32)],
        input_output_aliases=aliases, name=name,
        compiler_params=_cp(("parallel", "arbitrary", "arbitrary")))(*[a for a, _ in ins])


def flash_dkv(name, qa, ka, va, doa, lse, delta, *, w, vw, hkv, grp, vcol0, scale, nlat, tq, tk, band, ctx_k, prev=None):
    n = qa.shape[0]
    cblk = nlat // NCTX
    nqb = nlat // tq
    band = band and not ctx_k
    if ctx_k:
        tk = NCTX
        nqs = nqb
        grid = (hkv, 1, grp * nqs)
        kmap = lambda hk, j, t: (cblk, hk)
        vmap = lambda hk, j, t: (cblk, vcol0 + hk)
        dvmap = lambda hk, j, t: (cblk, hk)
        qb_of = lambda j, t: t % nqs
    else:
        nqs = 3 if band else nqb
        grid = (hkv, nlat // tk, grp * nqs)
        kmap = lambda hk, j, t: (j, hk)
        vmap = lambda hk, j, t: (j, vcol0 + hk)
        dvmap = lambda hk, j, t: (j, hk)
        qb_of = (lambda j, t: jnp.clip(j + t % nqs - 1, 0, nqb - 1)) if band else (lambda j, t: t % nqs)
    qmap = lambda hk, j, t: (qb_of(j, t), hk * grp + t // nqs)
    lmap = lambda hk, j, t: (hk * grp + t // nqs, qb_of(j, t), 0)

    def body(*refs):
        refs = list(refs)
        q_ref, k_ref, v_ref, do_ref, l_ref, dl_ref = refs[:6]
        pos = 6
        if ctx_k:
            qe_ref, doe_ref, le_ref, dle_ref = refs[pos:pos + 4]
            pos += 4
        if prev is not None:
            pos += 2
        dk_ref, dv_ref = refs[pos:pos + 2]
        j = pl.program_id(1)
        t = pl.program_id(2)
        kblk = k_ref[...]
        vblk = v_ref[...]

        def contrib(q, do, lse_v, dl_v, mask):
            s = _d(q, kblk, ((1,), (1,))) * (scale * LOG2E)
            if mask is not None:
                s = jnp.where(mask, s, NEG)
            p = jnp.exp2(s - lse_v)
            dp = _d(do, vblk, ((1,), (1,)))
            ds = p * (dp - dl_v) * scale
            return _d(ds, q, ((0,), (0,))), _d(p, do, ((0,), (0,)))

        @pl.when(t == 0)
        def _():
            dk = jnp.zeros((tk, w), F32)
            dv = jnp.zeros((tk, vw), F32)
            if ctx_k:
                for gi in range(grp):
                    a, b = contrib(qe_ref[:, w * gi:w * gi + w], doe_ref[:, vw * gi:vw * gi + vw], le_ref[gi], dle_ref[gi], None)
                    dk = dk + a
                    dv = dv + b
            dk_ref[...] = dk
            dv_ref[...] = dv

        def add(mask):
            a, b = contrib(q_ref[...], do_ref[...], l_ref[0], dl_ref[0], mask)
            dk_ref[...] += a
            dv_ref[...] += b

        if band:
            qb = j + t % nqs - 1

            @pl.when((qb >= 0) & (qb < nqb))
            def _():
                add(_band_mask(tq, tk, qb, j))
        else:
            add(None)

    ins = [(qa, pl.BlockSpec((tq, w), qmap)), (ka, pl.BlockSpec((tk, w), kmap)), (va, pl.BlockSpec((tk, vw), vmap)),
           (doa, pl.BlockSpec((tq, vw), qmap)), (lse, pl.BlockSpec((1, tq, 1), lmap)), (delta, pl.BlockSpec((1, tq, 1), lmap))]
    if ctx_k:
        ins += [(qa, pl.BlockSpec((NCTX, grp * w), lambda hk, j, t: (cblk, hk))),
                (doa, pl.BlockSpec((NCTX, grp * vw), lambda hk, j, t: (cblk, hk))),
                (lse, pl.BlockSpec((grp, NCTX, 1), lambda hk, j, t: (hk, cblk, 0))),
                (delta, pl.BlockSpec((grp, NCTX, 1), lambda hk, j, t: (hk, cblk, 0)))]
    aliases = {}
    if prev is not None:
        any_spec = pl.BlockSpec(memory_space=pl.ANY)
        aliases = {len(ins): 0, len(ins) + 1: 1}
        ins += [(prev[0], any_spec), (prev[1], any_spec)]
    return pl.pallas_call(
        body, out_shape=[_sd((n, hkv * w), F32), _sd((n, hkv * vw), F32)], grid=grid,
        in_specs=[s for _, s in ins],
        out_specs=[pl.BlockSpec((tk, w), kmap), pl.BlockSpec((tk, vw), dvmap)],
        input_output_aliases=aliases, name=name,
        compiler_params=_cp(("parallel", "parallel", "arbitrary")))(*[a for a, _ in ins])


def mla_fwd(name, qm, km, vm, nlat):
    n = qm.shape[0]
    t = NCTX
    nlt = nlat // t
    c = (MLA_NOPE + MLA_ROPE) ** -0.5 * LOG2E

    def body(q_ref, k_ref, v_ref, o_ref, l_ref):
        i = pl.program_id(1)

        def run(k, v):
            s = _d(q_ref[...], k, ((1,), (1,))) * c
            m = jnp.max(_fold_lanes(s, jnp.maximum), axis=1, keepdims=True)
            p = jnp.exp2(s - m)
            l = jnp.sum(_fold_lanes(p, jnp.add), axis=1, keepdims=True)
            o_ref[...] = (_d(p, v, ((1,), (0,))) / l).astype(BF16)
            l_ref[0] = m + jnp.log2(l)

        @pl.when(i < nlt)
        def _():
            run(k_ref[...], v_ref[...])

        @pl.when(i == nlt)
        def _():
            run(k_ref[nlat:n, :], v_ref[nlat:n, :])

    return pl.pallas_call(
        body, out_shape=[_sd((n, MLA_H * 128), BF16), _sd((MLA_H, n, 1), F32)], grid=(MLA_H, n // t),
        in_specs=[pl.BlockSpec((t, 256), lambda h, i: (i, h)), pl.BlockSpec((n, 256), lambda h, i: (0, h)),
                  pl.BlockSpec((n, 128), lambda h, i: (0, h))],
        out_specs=[pl.BlockSpec((t, 128), lambda h, i: (i, h)), pl.BlockSpec((1, t, 1), lambda h, i: (h, i, 0))],
        name=name, compiler_params=_cp(("parallel", "arbitrary")))(qm, km, vm)


def mla_dq(name, qm, km, vm, o, do, lse, nlat):
    n = qm.shape[0]
    t = NCTX
    nlt = nlat // t
    scale = (MLA_NOPE + MLA_ROPE) ** -0.5

    def body(q_ref, k_ref, v_ref, o_ref, do_ref, l_ref, dq_ref, dl_ref):
        i = pl.program_id(1)
        do = do_ref[...]
        delta = jnp.sum(do.astype(F32) * o_ref[...].astype(F32), axis=1, keepdims=True)
        dl_ref[0] = delta

        def run(k, v):
            s = _d(q_ref[...], k, ((1,), (1,))) * (scale * LOG2E)
            ds = jnp.exp2(s - l_ref[0]) * (_d(do, v, ((1,), (1,))) - delta) * scale
            dq_ref[...] = _d(ds, k, ((1,), (0,)))

        @pl.when(i < nlt)
        def _():
            run(k_ref[...], v_ref[...])

        @pl.when(i == nlt)
        def _():
            run(k_ref[nlat:n, :], v_ref[nlat:n, :])

    qspec = pl.BlockSpec((t, 256), lambda h, i: (i, h))
    ospec = pl.BlockSpec((t, 128), lambda h, i: (i, h))
    lspec = pl.BlockSpec((1, t, 1), lambda h, i: (h, i, 0))
    return pl.pallas_call(
        body, out_shape=[_sd((n, MLA_H * 256), F32), _sd((MLA_H, n, 1), F32)], grid=(MLA_H, n // t),
        in_specs=[qspec, pl.BlockSpec((n, 256), lambda h, i: (0, h)), pl.BlockSpec((n, 128), lambda h, i: (0, h)),
                  ospec, ospec, lspec],
        out_specs=[qspec, lspec],
        name=name, compiler_params=_cp(("parallel", "arbitrary")))(qm, km, vm, o, do, lse)


def mla_dkv(name, qm, km, vm, do, lse_row, delta_row, nlat):
    n = qm.shape[0]
    t = NCTX
    nlt = nlat // t
    scale = (MLA_NOPE + MLA_ROPE) ** -0.5

    def body(q_ref, k_ref, v_ref, do_ref, l_ref, dl_ref, dk_ref, dv_ref):
        j = pl.program_id(1)

        def run(q, do, lrow, drow):
            st = _d(k_ref[...], q, ((1,), (1,))) * (scale * LOG2E)
            pt = jnp.exp2(st - lrow)
            dv_ref[...] = _d(pt, do, ((1,), (0,)))
            dst = pt * (_d(v_ref[...], do, ((1,), (1,))) - drow) * scale
            dk_ref[...] = _d(dst, q, ((1,), (0,)))

        @pl.when(j < nlt)
        def _():
            run(q_ref[0:nlat, :], do_ref[0:nlat, :], l_ref[0, :, 0:nlat], dl_ref[0, :, 0:nlat])

        @pl.when(j == nlt)
        def _():
            run(q_ref[...], do_ref[...], l_ref[0], dl_ref[0])

    rspec = pl.BlockSpec((1, 1, n), lambda h, j: (h, 0, 0))
    return pl.pallas_call(
        body, out_shape=[_sd((n, MLA_H * 256), F32), _sd((n, MLA_H * 128), F32)], grid=(MLA_H, n // t),
        in_specs=[pl.BlockSpec((n, 256), lambda h, j: (0, h)), pl.BlockSpec((t, 256), lambda h, j: (j, h)),
                  pl.BlockSpec((t, 128), lambda h, j: (j, h)), pl.BlockSpec((n, 128), lambda h, j: (0, h)), rspec, rspec],
        out_specs=[pl.BlockSpec((t, 256), lambda h, j: (j, h)), pl.BlockSpec((t, 128), lambda h, j: (j, h))],
        name=name, compiler_params=_cp(("parallel", "arbitrary")))(qm, km, vm, do, lse_row, delta_row)


def mla_attention_bwd(tag, qm, km, vm, o, do, lse, nlat):
    n = qm.shape[0]
    dq, delta = mla_dq(tag + "_dq", qm, km, vm, o, do, lse, nlat)
    dk, dv = mla_dkv(tag + "_dkv", qm, km, vm, do, lse.reshape(MLA_H, 1, n), delta.reshape(MLA_H, 1, n), nlat)
    return dq, dk, dv


SWA_T = 512


def _swa_window(t, nlat):
    t = min(t, nlat)
    return t, min(t + 2 * SWA_WIN, nlat)


def _win_start(i, t, wlen, nlat):
    return pl.multiple_of(jnp.clip(i * t - SWA_WIN, 0, nlat - wlen), 128)


def _win_mask(rows, cols, row0, col0):
    rp = row0 + lax.broadcasted_iota(jnp.int32, (rows, cols), 0)
    cp = col0 + lax.broadcasted_iota(jnp.int32, (rows, cols), 1)
    return jnp.abs(rp - cp) <= SWA_WIN


def swa_fwd_lat(name, qs, ks, u, sink, nlat):
    n = qs.shape[0]
    tq, wlen = _swa_window(SWA_T, nlat)
    grp = SWA_HQ // SWA_HKV
    scale = SWA_DH ** -0.5
    vcol0 = C_V // 128

    def body(q_ref, k_ref, v_ref, s_ref, o_ref, l_ref):
        i = pl.program_id(1)
        ws = _win_start(i, tq, wlen, nlat)
        q = q_ref[...]
        s1 = _d(q, k_ref[pl.ds(ws, wlen), :], ((1,), (1,))) * (scale * LOG2E)
        s1 = jnp.where(_win_mask(tq, wlen, i * tq, ws), s1, NEG)
        s2 = _d(q, k_ref[pl.ds(nlat, NCTX), :], ((1,), (1,))) * (scale * LOG2E)
        sv = jnp.max(s_ref[0], axis=1, keepdims=True) * LOG2E
        m = jnp.maximum(jnp.maximum(jnp.max(s1, axis=1, keepdims=True), jnp.max(s2, axis=1, keepdims=True)), sv)
        p1 = jnp.exp2(s1 - m)
        p2 = jnp.exp2(s2 - m)
        l = jnp.sum(p1, axis=1, keepdims=True) + jnp.sum(p2, axis=1, keepdims=True) + jnp.exp2(sv - m)
        acc = _d(p1, v_ref[pl.ds(ws, wlen), :], ((1,), (0,))) + _d(p2, v_ref[pl.ds(nlat, NCTX), :], ((1,), (0,)))
        o_ref[...] = (acc / l).astype(BF16)
        l_ref[0] = m + jnp.log2(l)

    return pl.pallas_call(
        body, out_shape=[_sd((n, SWA_HQ * 128), BF16), _sd((SWA_HQ, n, 1), F32)], grid=(SWA_HQ, nlat // tq),
        in_specs=[pl.BlockSpec((tq, 128), lambda h, i: (i, h)), pl.BlockSpec((n, 128), lambda h, i: (0, h // grp)),
                  pl.BlockSpec((n, 128), lambda h, i: (0, vcol0 + h // grp)), pl.BlockSpec((1, 1, 128), lambda h, i: (h, 0, 0))],
        out_specs=[pl.BlockSpec((tq, 128), lambda h, i: (i, h)), pl.BlockSpec((1, tq, 1), lambda h, i: (h, i, 0))],
        name=name, compiler_params=_cp(("parallel", "arbitrary")))(qs, ks, u, sink)


def swa_dq_lat(name, qs, ks, u, o, do, lse, sink, nlat):
    n = qs.shape[0]
    tq, wlen = _swa_window(SWA_T, nlat)
    grp = SWA_HQ // SWA_HKV
    scale = SWA_DH ** -0.5
    vcol0 = C_V // 128

    def body(q_ref, k_ref, v_ref, s_ref, o_ref, do_ref, l_ref, dq_ref, dl_ref, ds_ref):
        i = pl.program_id(1)
        ws = _win_start(i, tq, wlen, nlat)
        q = q_ref[...]
        do = do_ref[...]
        lse_v = l_ref[0]
        delta = jnp.sum(do.astype(F32) * o_ref[...].astype(F32), axis=1, keepdims=True)
        kw = k_ref[pl.ds(ws, wlen), :]
        kc = k_ref[pl.ds(nlat, NCTX), :]
        s1 = _d(q, kw, ((1,), (1,))) * (scale * LOG2E)
        s1 = jnp.where(_win_mask(tq, wlen, i * tq, ws), s1, NEG)
        s2 = _d(q, kc, ((1,), (1,))) * (scale * LOG2E)
        ds1 = jnp.exp2(s1 - lse_v) * (_d(do, v_ref[pl.ds(ws, wlen), :], ((1,), (1,))) - delta) * scale
        ds2 = jnp.exp2(s2 - lse_v) * (_d(do, v_ref[pl.ds(nlat, NCTX), :], ((1,), (1,))) - delta) * scale
        dq_ref[...] = _d(ds1, kw, ((1,), (0,))) + _d(ds2, kc, ((1,), (0,)))
        dl_ref[0] = delta
        sv = jnp.max(s_ref[0], axis=1, keepdims=True) * LOG2E
        dsk = jnp.sum(-jnp.exp2(sv - lse_v) * delta, axis=0, keepdims=True)
        _acc(ds_ref, jnp.zeros((1, 1, 128), F32) + dsk, i == 0)

    qspec = pl.BlockSpec((tq, 128), lambda h, i: (i, h))
    lspec = pl.BlockSpec((1, tq, 1), lambda h, i: (h, i, 0))
    return pl.pallas_call(
        body, out_shape=[_sd((n, SWA_HQ * 128), F32), _sd((SWA_HQ, n, 1), F32), _sd((SWA_HQ, 1, 128), F32)],
        grid=(SWA_HQ, nlat // tq),
        in_specs=[qspec, pl.BlockSpec((n, 128), lambda h, i: (0, h // grp)),
                  pl.BlockSpec((n, 128), lambda h, i: (0, vcol0 + h // grp)), pl.BlockSpec((1, 1, 128), lambda h, i: (h, 0, 0)),
                  qspec, qspec, lspec],
        out_specs=[qspec, lspec, pl.BlockSpec((1, 1, 128), lambda h, i: (h, 0, 0))],
        name=name, compiler_params=_cp(("parallel", "arbitrary")))(qs, ks, u, sink, o, do, lse)


def swa_dkv_lat(name, qs, ks, u, do, lse_row, delta_row, nlat):
    n = qs.shape[0]
    tk, wlen = _swa_window(SWA_T, nlat)
    grp = SWA_HQ // SWA_HKV
    scale = SWA_DH ** -0.5
    vcol0 = C_V // 128

    def body(q_ref, k_ref, v_ref, do_ref, l_ref, dl_ref, dk_ref, dv_ref):
        j = pl.program_id(1)
        ws = _win_start(j, tk, wlen, nlat)
        k = k_ref[...]
        v = v_ref[...]
        mask = _win_mask(tk, wlen, j * tk, ws)
        dk = jnp.zeros((tk, 128), F32)
        dv = jnp.zeros((tk, 128), F32)
        for gi in range(grp):
            qw = q_ref[pl.ds(ws, wlen), 128 * gi:128 * gi + 128]
            dow = do_ref[pl.ds(ws, wlen), 128 * gi:128 * gi + 128]
            st = jnp.where(mask, _d(k, qw, ((1,), (1,))) * (scale * LOG2E), NEG)
            pt = jnp.exp2(st - l_ref[gi, :, pl.ds(ws, wlen)])
            dv = dv + _d(pt, dow, ((1,), (0,)))
            dst = pt * (_d(v, dow, ((1,), (1,))) - dl_ref[gi, :, pl.ds(ws, wlen)]) * scale
            dk = dk + _d(dst, qw, ((1,), (0,)))
        dk_ref[...] = dk
        dv_ref[...] = dv

    rspec = pl.BlockSpec((grp, 1, n), lambda hk, j: (hk, 0, 0))
    return pl.pallas_call(
        body, out_shape=[_sd((n, SWA_HKV * 128), F32), _sd((n, SWA_HKV * 128), F32)], grid=(SWA_HKV, nlat // tk),
        in_specs=[pl.BlockSpec((n, grp * 128), lambda hk, j: (0, hk)), pl.BlockSpec((tk, 128), lambda hk, j: (j, hk)),
                  pl.BlockSpec((tk, 128), lambda hk, j: (j, vcol0 + hk)), pl.BlockSpec((n, grp * 128), lambda hk, j: (0, hk)),
                  rspec, rspec],
        out_specs=[pl.BlockSpec((tk, 128), lambda hk, j: (j, hk)), pl.BlockSpec((tk, 128), lambda hk, j: (j, hk))],
        name=name, compiler_params=_cp(("parallel", "arbitrary")))(qs, ks, u, do, lse_row, delta_row)


def swa_attention_fwd(tag, qs, ks, u, sink, cfg, nlat):
    o, lse = swa_fwd_lat(tag + "_fwd_lat", qs, ks, u, sink, nlat)
    return flash_fwd(tag + "_fwd_ctx", qs, ks, u, sink=sink, ctx_q=True, nlat=nlat, prev=(o, lse), **cfg)


def swa_attention_bwd(tag, qs, ks, u, o, do, lse, sink, cfg, nlat):
    n = qs.shape[0]
    dq, delta, ds1 = swa_dq_lat(tag + "_dq_lat", qs, ks, u, o, do, lse, sink, nlat)
    dq, delta, ds2 = flash_dq(tag + "_dq_ctx", qs, ks, u, o, do, lse, sink=sink, ctx_q=True, nlat=nlat, prev=(dq, delta), **cfg)
    dk, dv = swa_dkv_lat(tag + "_dkv_lat", qs, ks, u, do, lse.reshape(SWA_HQ, 1, n), delta.reshape(SWA_HQ, 1, n), nlat)
    kc = {k: v for k, v in cfg.items() if k != "hq"}
    kc["hkv"] = SWA_HKV
    kc["tq"] = min(1024, nlat)
    dk, dv = flash_dkv(tag + "_dkv_ctx", qs, ks, u, do, lse, delta, ctx_k=True, nlat=nlat, prev=(dk, dv), **kc)
    return dq, dk, dv, ds1 + ds2


def adamw(name, w, g, m, v):
    r, c = w.shape
    tr = _pick(r, (256, 128, 64, 32, 16, 8))
    bc1 = 1.0 - ADAM_B1 ** ADAM_STEP
    bc2 = 1.0 - ADAM_B2 ** ADAM_STEP

    def body(w_ref, g_ref, m_ref, v_ref, d_ref, nm_ref, nv_ref):
        gv = g_ref[...]
        nm = ADAM_B1 * m_ref[...] + (1.0 - ADAM_B1) * gv
        nv = ADAM_B2 * v_ref[...] + (1.0 - ADAM_B2) * (gv * gv)
        d_ref[...] = -ADAM_LR * ((nm / bc1) / (jnp.sqrt(nv / bc2) + ADAM_EPS) + ADAM_WD * w_ref[...])
        nm_ref[...] = nm
        nv_ref[...] = nv

    spec = pl.BlockSpec((tr, c), lambda i: (i, 0))
    return pl.pallas_call(body, out_shape=[_sd((r, c), F32)] * 3, grid=(r // tr,), in_specs=[spec] * 4, out_specs=[spec] * 3,
                          name=name, compiler_params=_cp(("parallel",)))(w, g, m, v)


def _coords():
    return lax.axis_index("x"), lax.axis_index("y"), lax.axis_index("c")


_ANY = pl.BlockSpec(memory_space=pl.ANY)


def gather_chips(name, a):
    r = a.shape[0]
    half = r // 2

    def body(a_ref, o_ref, ici_send, ici_recv, d2d_send, d2d_recv, loc_sem):
        x, y, c = _coords()
        me = 2 * x + y
        peers = [(1 - x, y), (x, 1 - y), (1 - x, 1 - y)]
        my_rows = pl.ds(c * half, half)
        sib_rows = pl.ds((1 - c) * half, half)
        mine = pltpu.make_async_copy(a_ref, o_ref.at[me], loc_sem)
        mine.start()
        sends = [pltpu.make_async_remote_copy(a_ref.at[my_rows], o_ref.at[me, my_rows], ici_send.at[k], ici_recv.at[k],
                                              device_id=(px, py, c), device_id_type=MESH)
                 for k, (px, py) in enumerate(peers)]
        for cp in sends:
            cp.start()
        passed = []
        for k, (px, py) in enumerate(peers):
            s = 2 * px + py
            pltpu.make_async_remote_copy(a_ref.at[my_rows], o_ref.at[s, my_rows], ici_send.at[k], ici_recv.at[k],
                                         device_id=(px, py, c), device_id_type=MESH).wait_recv()
            fw = pltpu.make_async_remote_copy(o_ref.at[s, my_rows], o_ref.at[s, my_rows], d2d_send.at[k], d2d_recv.at[k],
                                              device_id=(x, y, 1 - c), device_id_type=MESH)
            fw.start()
            passed.append(fw)
        for k, (px, py) in enumerate(peers):
            s = 2 * px + py
            pltpu.make_async_remote_copy(o_ref.at[s, sib_rows], o_ref.at[s, sib_rows], d2d_send.at[k], d2d_recv.at[k],
                                         device_id=(x, y, 1 - c), device_id_type=MESH).wait_recv()
        for cp in sends + passed:
            cp.wait_send()
        mine.wait()

    return pl.pallas_call(
        body, out_shape=_sd((4,) + a.shape, a.dtype), in_specs=[_ANY], out_specs=_ANY,
        scratch_shapes=[pltpu.SemaphoreType.DMA((3,)), pltpu.SemaphoreType.DMA((3,)), pltpu.SemaphoreType.DMA((3,)),
                        pltpu.SemaphoreType.DMA((3,)), pltpu.SemaphoreType.DMA],
        name=name, compiler_params=pltpu.CompilerParams(has_side_effects=True))(a)


def pair_split(name, a):
    k4, r, cdim = a.shape
    half = r // 2

    def body(a_ref, own_ref, got_ref, send_sem, recv_sem, loc_sem):
        x, y, c = _coords()
        my_rows = pl.ds(c * half, half)
        sib_rows = pl.ds((1 - c) * half, half)
        mine = pltpu.make_async_copy(a_ref.at[:, my_rows], own_ref, loc_sem)
        mine.start()
        cp = pltpu.make_async_remote_copy(a_ref.at[:, sib_rows], got_ref, send_sem, recv_sem,
                                          device_id=(x, y, 1 - c), device_id_type=MESH)
        cp.start()
        cp.wait()
        mine.wait()

    return pl.pallas_call(
        body, out_shape=[_sd((k4, half, cdim), a.dtype), _sd((k4, half, cdim), a.dtype)], in_specs=[_ANY],
        out_specs=[_ANY, _ANY],
        scratch_shapes=[pltpu.SemaphoreType.DMA, pltpu.SemaphoreType.DMA, pltpu.SemaphoreType.DMA],
        name=name, compiler_params=pltpu.CompilerParams(has_side_effects=True))(a)


def scatter_chips(name, a):
    def body(a_ref, o_ref, send_sems, recv_sems, loc_sem):
        x, y, c = _coords()
        me = 2 * x + y
        peers = [(1 - x, y), (x, 1 - y), (1 - x, 1 - y)]
        mine = pltpu.make_async_copy(a_ref.at[me], o_ref.at[me], loc_sem)
        mine.start()
        sends = [pltpu.make_async_remote_copy(a_ref.at[2 * px + py], o_ref.at[me], send_sems.at[k], recv_sems.at[k],
                                              device_id=(px, py, c), device_id_type=MESH)
                 for k, (px, py) in enumerate(peers)]
        for cp in sends:
            cp.start()
        for k, (px, py) in enumerate(peers):
            pltpu.make_async_remote_copy(a_ref.at[me], o_ref.at[2 * px + py], send_sems.at[k], recv_sems.at[k],
                                         device_id=(px, py, c), device_id_type=MESH).wait_recv()
        for cp in sends:
            cp.wait_send()
        mine.wait()

    return pl.pallas_call(
        body, out_shape=_sd(a.shape, a.dtype), in_specs=[_ANY], out_specs=_ANY,
        scratch_shapes=[pltpu.SemaphoreType.DMA((3,)), pltpu.SemaphoreType.DMA((3,)), pltpu.SemaphoreType.DMA],
        name=name, compiler_params=pltpu.CompilerParams(has_side_effects=True))(a)


def pair_join(name, a):
    half, cdim = a.shape

    def body(a_ref, o_ref, send_sem, recv_sem, loc_sem):
        x, y, c = _coords()
        my_rows = pl.ds(c * half, half)
        sib_rows = pl.ds((1 - c) * half, half)
        mine = pltpu.make_async_copy(a_ref, o_ref.at[my_rows], loc_sem)
        mine.start()
        cp = pltpu.make_async_remote_copy(a_ref, o_ref.at[my_rows], send_sem, recv_sem, device_id=(x, y, 1 - c),
                                          device_id_type=MESH)
        cp.start()
        cp.wait_send()
        pltpu.make_async_remote_copy(a_ref, o_ref.at[sib_rows], send_sem, recv_sem, device_id=(x, y, 1 - c),
                                     device_id_type=MESH).wait_recv()
        mine.wait()

    return pl.pallas_call(
        body, out_shape=_sd((2 * half, cdim), a.dtype), in_specs=[_ANY], out_specs=_ANY,
        scratch_shapes=[pltpu.SemaphoreType.DMA, pltpu.SemaphoreType.DMA, pltpu.SemaphoreType.DMA],
        name=name, compiler_params=pltpu.CompilerParams(has_side_effects=True))(a)


def add_cast(name, a, b, dtype):
    k, r, c = a.shape
    tr = _pick(r, (1384, 1024, 512, 256, 128, 64, 32, 16, 8))

    def body(a_ref, b_ref, o_ref):
        o_ref[...] = (a_ref[...].astype(F32) + b_ref[...].astype(F32)).astype(dtype)

    spec = pl.BlockSpec((1, tr, c), lambda s, i: (s, i, 0))
    return pl.pallas_call(body, out_shape=_sd((k, r, c), dtype), grid=(k, r // tr), in_specs=[spec, spec], out_specs=spec,
                          name=name, compiler_params=_cp(("parallel", "parallel")))(a, b)


def gather_all(name, a):
    def body(a_ref, o_ref, send_sems, recv_sems, loc_sem):
        x, y, c = _coords()
        me = 4 * x + 2 * y + c
        flips = [(fx, fy, fc) for fx in (0, 1) for fy in (0, 1) for fc in (0, 1) if fx + fy + fc > 0]
        peers = [(x ^ fx, y ^ fy, c ^ fc) for fx, fy, fc in flips]
        mine = pltpu.make_async_copy(a_ref, o_ref.at[me], loc_sem)
        mine.start()
        sends = [pltpu.make_async_remote_copy(a_ref, o_ref.at[me], send_sems.at[k], recv_sems.at[k],
                                              device_id=p, device_id_type=MESH) for k, p in enumerate(peers)]
        for cp in sends:
            cp.start()
        for k, (px, py, pc) in enumerate(peers):
            pltpu.make_async_remote_copy(a_ref, o_ref.at[4 * px + 2 * py + pc], send_sems.at[k], recv_sems.at[k],
                                         device_id=(px, py, pc), device_id_type=MESH).wait_recv()
        for cp in sends:
            cp.wait_send()
        mine.wait()

    return pl.pallas_call(
        body, out_shape=_sd((8,) + a.shape, a.dtype), in_specs=[_ANY], out_specs=_ANY,
        scratch_shapes=[pltpu.SemaphoreType.DMA((7,)), pltpu.SemaphoreType.DMA((7,)), pltpu.SemaphoreType.DMA],
        name=name, compiler_params=pltpu.CompilerParams(has_side_effects=True))(a)


def sum_blocks(name, a):
    k, r, c = a.shape
    tr = _pick(r, (1384, 256, 128, 64, 32, 16, 8))

    def body(a_ref, o_ref):
        acc = a_ref[0].astype(F32)
        for s in range(1, k):
            acc = acc + a_ref[s].astype(F32)
        o_ref[...] = acc

    return pl.pallas_call(body, out_shape=_sd((r, c), F32), grid=(r // tr,),
                          in_specs=[pl.BlockSpec((k, tr, c), lambda i: (0, i, 0))], out_specs=pl.BlockSpec((tr, c), lambda i: (i, 0)),
                          name=name, compiler_params=_cp(("parallel",)))(a)


BIG = ("w_mod", "w_in", "w_mla_uq", "w_mla_ukv", "w_p_ssm", "w_p_swa", "w_p_mla", "w_out", "w_ffn_in", "w_ffn_out")
COL_SHARDED = ("w_mod", "w_in", "w_mla_uq", "w_mla_ukv", "w_ffn_in")
SMALL = ("c_ctx", "b_mod", "norm1_g", "norm2_g", "ssm_conv_w", "ssm_conv_b", "ssm_dt_bias", "ssm_a_log", "ssm_d",
         "ssm_norm_g", "swa_q_norm_g", "swa_k_norm_g", "swa_sink", "mla_q_lat_g", "mla_kv_lat_g", "mla_q_norm_g",
         "mla_k_norm_g")
WEIGHTS = ("c_ctx", "w_mod", "b_mod", "norm1_g", "norm2_g", "w_in", "ssm_conv_w", "ssm_conv_b", "ssm_dt_bias", "ssm_a_log",
           "ssm_d", "ssm_norm_g", "swa_q_norm_g", "swa_k_norm_g", "swa_sink", "mla_q_lat_g", "mla_kv_lat_g", "w_mla_uq",
           "w_mla_ukv", "mla_q_norm_g", "mla_k_norm_g", "w_p_ssm", "w_p_swa", "w_p_mla", "w_out", "w_ffn_in", "w_ffn_out")


def pack_w_in(w):
    z = lambda k: jnp.zeros((w.shape[0], k), w.dtype)
    return jnp.concatenate([w[:, 4832:7904], w[:, 2400:3424], w[:, 3424:4448], w[:, 0:1536], w[:, 1568:1824], w[:, 1824:2080],
                            w[:, 2080:2336], w[:, 2336:2400], w[:, 1536:1568], z(32), z(128), w[:, 4448:4832]], axis=1)


def unpack_w_in(g):
    return jnp.concatenate([g[:, 5120:6656], g[:, 7488:7520], g[:, 6656:6912], g[:, 6912:7168], g[:, 7168:7424], g[:, 7424:7488],
                            g[:, 3072:4096], g[:, 4096:5120], g[:, 7680:8064], g[:, 0:3072]], axis=1)


def pack_ukv(w):
    return w.reshape(MLA_KVRANK, MLA_H, 2, 128).transpose(0, 2, 1, 3).reshape(MLA_KVRANK, 2048)


def unpack_ukv(g):
    return g.reshape(MLA_KVRANK, 2, MLA_H, 128).transpose(0, 2, 1, 3).reshape(MLA_KVRANK, 2048)


def pack_uq(w):
    return jnp.pad(w.reshape(MLA_QRANK, MLA_H, 192), ((0, 0), (0, 0), (0, 64))).reshape(MLA_QRANK, 2048)


def unpack_uq(g):
    return g.reshape(MLA_QRANK, MLA_H, 256)[:, :, :192].reshape(MLA_QRANK, 1536)


def rope_tables(nlat):
    t = jnp.arange(nlat, dtype=jnp.int32)
    r = (t // GRID_W).astype(F32)[:, None]
    col = (t % GRID_W).astype(F32)[:, None]

    def tab(nf, pad):
        inv = jnp.power(ROPE_BASE, -jnp.arange(nf, dtype=F32) / nf)
        ar, ac = r * inv, col * inv
        cos = jnp.concatenate([jnp.cos(ar), jnp.cos(ar), jnp.cos(ac), jnp.cos(ac), jnp.ones((nlat, pad), F32)], axis=1)
        sin = jnp.concatenate([-jnp.sin(ar), jnp.sin(ar), -jnp.sin(ac), jnp.sin(ac), jnp.zeros((nlat, pad), F32)], axis=1)
        cos = jnp.concatenate([cos, jnp.ones((NCTX, 128), F32)], axis=0)
        sin = jnp.concatenate([sin, jnp.zeros((NCTX, 128), F32)], axis=0)
        return cos, sin

    return tab(32, 0), tab(16, 64)


def _lanes(v, start, width=128):
    return jnp.zeros((1, width), F32).at[0, start:start + v.shape[0]].set(v)


def layer_fwd(i, xin, h, mod, p, tabs, nlat):
    t = "l%d_" % i
    n = xin.shape[0]
    (cos_s, sin_s), (cos_m, sin_m) = tabs
    u = mm(h, p["w_in"], F32, t + "in_proj")
    xbc = conv_fwd(t + "conv", u, p["conv_w"], p["conv_b"], nlat)
    dtrow = jnp.transpose(u[:, C_MISC + DT_LANE:C_MISC + DT_LANE + 32])
    nlc = nlat // Q
    yf, hs_f = ssd_fwd(t + "ssd_f", xbc, u, dtrow, p["bias_c"], p["alog_c"], p["bias_r"], p["alog_r"], nlc, False, 0)
    yb, hs_b = ssd_fwd(t + "ssd_b", xbc, u, dtrow, p["bias_c"], p["alog_c"], p["bias_r"], p["alog_r"], nlc, True, 1)
    ys = ssd_out_fwd(t + "ssd_out", yf, yb, xbc, u, p["ssm_norm_g"], p["d_exp"])
    qs, ks = swa_prep_fwd(t + "swa_prep", u, p["swa_q_g"], p["swa_k_g"], cos_s, sin_s)
    o_swa, lse_swa = swa_attention_fwd(t + "swa", qs, ks, u, p["sink"], p["swa_cfg"], nlat)
    ckv_n, cq_n = lat_norm_fwd(t + "lat_norm", u, p["kv_lat_g"], p["q_lat_g"])
    kv = mm(ckv_n, p["w_ukv"], F32, t + "ukv")
    qp = mm(cq_n, p["w_uq"], F32, t + "uq")
    km, qm, vm = mla_prep_fwd(t + "mla_prep", kv, qp, u, p["mla_q_g"], p["mla_k_g"], cos_m, sin_m)
    o_mla, lse_mla = mla_fwd(t + "mla_fwd", qm, km, vm, nlat)
    p1 = mm(ys, p["w_p_ssm"], F32, t + "p_ssm")
    p2 = mm(o_swa, p["w_p_swa"], F32, t + "p_swa")
    p3 = mm(o_mla, p["w_p_mla"], F32, t + "p_mla")
    merged = merge_fwd(t + "merge", u, p1, p2, p3)
    o = mm(merged, p["w_out"], F32, t + "out_proj")
    x1, h2 = resid_mod_fwd(t + "res1", xin, o, mod, 2, mod, 3, 4, p["norm2_g"], nlat // RT)
    gu = mm(h2, p["w_ffn_in"], F32, t + "ffn_in")
    a = swiglu_fwd(t + "swiglu", gu)
    f = mm(a, p["w_ffn_out"], F32, t + "ffn_out")
    saved = dict(xin=xin, h=h, u=u, xbc=xbc, dtrow=dtrow, yf=yf, yb=yb, hs_f=hs_f, hs_b=hs_b, ys=ys, qs=qs, ks=ks,
                 o_swa=o_swa, lse_swa=lse_swa, ckv_n=ckv_n, cq_n=cq_n, kv=kv, qp=qp, km=km, qm=qm, vm=vm, o_mla=o_mla,
                 lse_mla=lse_mla, p1=p1, p2=p2, p3=p3, merged=merged, o=o, x1=x1, h2=h2, gu=gu, a=a, f=f)
    del n
    return x1, f, saved


def layer_bwd(i, dx2, df, dgt2, sv, mod, p, tabs, nlat):
    t = "l%db_" % i
    (cos_s, sin_s), (cos_m, sin_m) = tabs
    g = {}
    nt = nlat // RT
    nlc = nlat // Q
    g["w_ffn_out"] = mm_tn(sv["a"], df, t + "wg_ffn_out")
    da = mm(df, p["w_ffn_out"], F32, t + "dg_ffn_out", trans_b=True)
    dgu = swiglu_bwd(t + "swiglu", sv["gu"], da)
    g["w_ffn_in"] = mm_tn(sv["h2"], dgu, t + "wg_ffn_in")
    dh2 = mm(dgu, p["w_ffn_in"], F32, t + "dg_ffn_in", trans_b=True)
    dx1, do, dgt1, dsh2, dsc2, g["norm2_g"] = resid_mod_bwd(t + "res1", sv["x1"], dx2, dh2, sv["o"], mod, 2, mod, 3, 4,
                                                              p["norm2_g"], nt)
    g["w_out"] = mm_tn(sv["merged"], do, t + "wg_out")
    dmerged = mm(do, p["w_out"], F32, t + "dg_out", trans_b=True)
    dp1, dp2, dp3, dgates = merge_bwd(t + "merge", sv["u"], sv["p1"], sv["p2"], sv["p3"], dmerged)
    g["w_p_ssm"] = mm_tn(sv["ys"], dp1, t + "wg_p_ssm")
    g["w_p_swa"] = mm_tn(sv["o_swa"], dp2, t + "wg_p_swa")
    g["w_p_mla"] = mm_tn(sv["o_mla"], dp3, t + "wg_p_mla")
    dys = mm(dp1, p["w_p_ssm"], F32, t + "dg_p_ssm", trans_b=True)
    do_swa = mm(dp2, p["w_p_swa"], BF16, t + "dg_p_swa", trans_b=True)
    do_mla = mm(dp3, p["w_p_mla"], BF16, t + "dg_p_mla", trans_b=True)
    dqm, dkm, dv_mla = mla_attention_bwd(t + "mla", sv["qm"], sv["km"], sv["vm"], sv["o_mla"], do_mla, sv["lse_mla"], nlat)
    dkv, dqp, dkr, g["mla_q_g"], g["mla_k_g"] = mla_prep_bwd(t + "mla_prep", sv["kv"], sv["qp"], sv["u"], p["mla_q_g"],
                                                             p["mla_k_g"], cos_m, sin_m, dkm, dqm, dv_mla)
    g["w_ukv"] = mm_tn(sv["ckv_n"], dkv, t + "wg_ukv")
    g["w_uq"] = mm_tn(sv["cq_n"], dqp, t + "wg_uq")
    dckv_n = mm(dkv, p["w_ukv"], F32, t + "dg_ukv", trans_b=True)
    dcq_n = mm(dqp, p["w_uq"], F32, t + "dg_uq", trans_b=True)
    dckv, dcq, g["kv_lat_g"], g["q_lat_g"] = lat_norm_bwd(t + "lat_norm", sv["u"], p["kv_lat_g"], p["q_lat_g"], dckv_n, dcq_n)
    dqs, dks, dv_swa, g["sink"] = swa_attention_bwd(t + "swa", sv["qs"], sv["ks"], sv["u"], sv["o_swa"], do_swa, sv["lse_swa"],
                                                p["sink"], p["swa_cfg"], nlat)
    dq, dk, dv, g["swa_q_g"], g["swa_k_g"] = swa_prep_bwd(t + "swa_prep", sv["u"], p["swa_q_g"], p["swa_k_g"], cos_s, sin_s,
                                                          dqs, dks, dv_swa)
    dy, dxs_skip, dz, g["ssm_norm_g"], g["d_exp"] = ssd_out_bwd(t + "ssd_out", sv["yf"], sv["yb"], sv["xbc"], sv["u"],
                                                                 p["ssm_norm_g"], p["d_exp"], dys)
    n = dy.shape[0]
    zbc = jnp.zeros((n, 256), F32)
    r_f = ssd_bwd(t + "ssd_f", sv["xbc"], sv["u"], sv["dtrow"], p["bias_c"], p["alog_c"], p["bias_r"], p["alog_r"],
                  sv["hs_f"], dy, (dxs_skip, zbc, zbc), nlc, False, 0)
    r_b = ssd_bwd(t + "ssd_b", sv["xbc"], sv["u"], sv["dtrow"], p["bias_c"], p["alog_c"], p["bias_r"], p["alog_r"],
                  sv["hs_b"], dy, (r_f[0], r_f[1], r_f[2]), nlc, True, 1)
    dact = jnp.concatenate([r_b[0], r_b[1], r_b[2]], axis=1)
    dxbc, g["conv_w"], g["conv_b"] = conv_bwd(t + "conv", sv["u"], dact, p["conv_w"], p["conv_b"], nlat)
    drow = jnp.concatenate([r_f[4][0] + r_f[4][1], r_b[4][0] + r_b[4][1]], axis=0)
    drow_t = jnp.pad(jnp.transpose(drow), ((0, 0), (DT_LANE, 128 - DT_LANE - 32)))
    dmisc = misc_combine(t + "misc", dkr, r_f[3], r_b[3], drow_t)
    g["bias_c"] = r_f[5] + r_b[5]
    g["alog_c"] = r_f[6] + r_b[6]
    g["bias_r"] = jnp.concatenate([r_f[7], r_b[7]], axis=0)
    g["alog_r"] = jnp.concatenate([r_f[8], r_b[8]], axis=0)
    du = jnp.concatenate([dgates, dz, dq, dxbc, dk, dv, dckv, dmisc, jnp.zeros((n, 128), BF16), dcq], axis=1)
    g["w_in"] = mm_tn(sv["h"], du, t + "wg_in")
    dh = mm(du, p["w_in"], F32, t + "dg_in", trans_b=True)
    g["mod"] = (dgt1, dsh2, dsc2, dgt2)
    return dx1, dh, g


def local_step(x, c, ctx, target, c_ctx, W, nlat):
    xin = jnp.concatenate([x, ctx], axis=0)
    n = xin.shape[0]
    nt = nlat // RT
    tabs = rope_tables(nlat)
    c8 = jnp.zeros((8, D), F32).at[0].set(c[0]).at[1].set(c_ctx)
    mods, silus = [], []
    for i in range(DEPTH):
        m8, s8 = mod_fwd("l%d_mod" % i, c8, W[i]["w_mod"], W[i]["b_mod"])
        mods.append(m8[0:2].reshape(2, 1, 6 * D))
        silus.append(s8)
    saved = []
    _, h = resid_mod_fwd("l0_norm1", xin, None, None, 0, mods[0], 0, 1, W[0]["norm1_g"], nt)
    xcur = xin
    for i in range(DEPTH):
        x1, f, sv = layer_fwd(i, xcur, h, mods[i], W[i], tabs, nlat)
        saved.append(sv)
        if i + 1 < DEPTH:
            xcur, h = resid_mod_fwd("l%d_res2" % i, x1, f, mods[i], 5, mods[i + 1], 0, 1, W[i + 1]["norm1_g"], nt)
    loss_v, dx2, df, dgt2 = resid_loss("loss", x1, f, mods[DEPTH - 1], 5, target, nt)
    grads = [None] * DEPTH
    for i in reversed(range(DEPTH)):
        dx1, dh, g = layer_bwd(i, dx2, df, dgt2, saved[i], mods[i], W[i], tabs, nlat)
        if i > 0:
            sv = saved[i]
            dx2, df, dgt2, dsh1, dsc1, g["norm1_g"] = resid_mod_bwd(
                "l%db_res2" % (i - 1), sv["xin"], dx1, dh, saved[i - 1]["f"], mods[i - 1], 5, mods[i], 0, 1,
                W[i]["norm1_g"], nt)
        else:
            dxin, _, _, dsh1, dsc1, g["norm1_g"] = resid_mod_bwd("l0b_norm1", saved[0]["xin"], dx1, dh, None, None, 0,
                                                                  mods[0], 0, 1, W[0]["norm1_g"], nt)
        dgt1, dsh2, dsc2, dgt2_i = g.pop("mod")
        dmod = jnp.concatenate([dsh1, dsc1, dgt1, dsh2, dsc2, dgt2_i], axis=2).reshape(2, 6 * D)
        dmod8 = jnp.zeros((8, 6 * D), F32).at[0:2].set(dmod)
        g["w_mod"] = mm_tn(silus[i], dmod8, "l%db_wg_mod" % i)
        dsilu = mm(dmod8, W[i]["w_mod"], F32, "l%db_dg_mod" % i, trans_b=True)
        dc8, g["b_mod"] = mod_small_bwd("l%db_mod_small" % i, c8, dsilu, dmod8)
        g["c8"] = dc8
        grads[i] = g
    del n
    return loss_v[0, 0], dxin, grads


def _big_shapes():
    return dict(w_mod=(2, 1024, 1536), w_in=(2, 1024, 1976), w_mla_uq=(2, 384, 384), w_mla_ukv=(2, 256, 512),
                w_p_ssm=(2, 256, 1024), w_p_swa=(2, 256, 1024), w_p_mla=(2, 256, 1024), w_out=(2, 256, 1024),
                w_ffn_in=(2, 1024, 1408), w_ffn_out=(2, 704, 1024))


def _pack_big(d, dtype):
    return jnp.concatenate([d[k].astype(dtype).reshape(-1, 1024) for k in BIG], axis=0)


def _unpack_big(buf, lead):
    out = {}
    r0 = 0
    for k in BIG:
        sh = _big_shapes()[k]
        rows = sh[0] * sh[1] * sh[2] // 1024
        out[k] = buf[..., r0:r0 + rows, :].reshape(lead + sh)
        r0 += rows
    return out


def _full_from_chips(k, a):
    if k in COL_SHARDED:
        return a.transpose(1, 2, 0, 3).reshape(2, a.shape[2], 4 * a.shape[3])
    return a.transpose(1, 0, 2, 3).reshape(2, 4 * a.shape[2], a.shape[3])


def _chips_from_full(k, a):
    if k in COL_SHARDED:
        return a.reshape(a.shape[0], 4, a.shape[1] // 4).transpose(1, 0, 2)
    return a.reshape(4, a.shape[0] // 4, a.shape[1])


def _small_sizes():
    return dict(c_ctx=1024, b_mod=2 * 6144, norm1_g=2048, norm2_g=2048, ssm_conv_w=2 * 5 * 1536, ssm_conv_b=2 * 1536,
                ssm_dt_bias=64, ssm_a_log=64, ssm_d=32, ssm_norm_g=2048, swa_q_norm_g=256, swa_k_norm_g=256, swa_sink=16,
                mla_q_lat_g=768, mla_kv_lat_g=512, mla_q_norm_g=384, mla_k_norm_g=384)


def _pack_small(d):
    parts = []
    for k in SMALL:
        v = d[k].astype(F32).reshape(-1)
        parts.append(jnp.pad(v, (0, (-v.shape[0]) % 1024)))
    return jnp.concatenate(parts).reshape(-1, 128)


def _unpack_small(buf, shapes):
    flat = buf.reshape(-1)
    out = {}
    o = 0
    for k in SMALL:
        sz = _small_sizes()[k]
        out[k] = flat[o:o + sz].reshape(shapes[k])
        o += sz + (-sz) % 1024
    return out


def big_grads(grads):
    gfull = {k: [] for k in BIG}
    for i in range(DEPTH):
        g = grads[i]
        gfull["w_mod"].append(g["w_mod"])
        gfull["w_in"].append(unpack_w_in(g["w_in"]))
        gfull["w_mla_uq"].append(unpack_uq(g["w_uq"]))
        gfull["w_mla_ukv"].append(unpack_ukv(g["w_ukv"]))
        for k in ("w_p_ssm", "w_p_swa", "w_p_mla", "w_out", "w_ffn_in", "w_ffn_out"):
            gfull[k].append(g[k])
    return gfull


def small_grads(grads):
    gs = {}
    gs["c_ctx"] = sum(grads[i]["c8"][1] for i in range(DEPTH))
    st = lambda f: jnp.stack([f(grads[i]) for i in range(DEPTH)])
    gs["b_mod"] = st(lambda g: g["b_mod"][0])
    gs["norm1_g"] = st(lambda g: g["norm1_g"][0])
    gs["norm2_g"] = st(lambda g: g["norm2_g"][0])
    gs["ssm_conv_w"] = st(lambda g: g["conv_w"])
    gs["ssm_conv_b"] = st(lambda g: g["conv_b"][0])
    gs["ssm_dt_bias"] = st(lambda g: (g["bias_c"][0, DT_LANE:DT_LANE + 32] + g["bias_r"][:, 0]).reshape(2, 16))
    gs["ssm_a_log"] = st(lambda g: (g["alog_c"][0, DT_LANE:DT_LANE + 32] + g["alog_r"][:, 0]).reshape(2, 16))
    gs["ssm_d"] = st(lambda g: g["d_exp"].reshape(16, 64).sum(axis=1))
    gs["ssm_norm_g"] = st(lambda g: g["ssm_norm_g"][0])
    gs["swa_q_norm_g"] = st(lambda g: g["swa_q_g"][0])
    gs["swa_k_norm_g"] = st(lambda g: g["swa_k_g"][0])
    gs["swa_sink"] = st(lambda g: g["sink"][:, 0, 0])
    gs["mla_q_lat_g"] = st(lambda g: g["q_lat_g"][0])
    gs["mla_kv_lat_g"] = st(lambda g: g["kv_lat_g"][0])
    gs["mla_q_norm_g"] = st(lambda g: g["mla_q_g"][0, :192])
    gs["mla_k_norm_g"] = st(lambda g: g["mla_k_g"][0, :192])
    return gs


def layer_params(i, full, conv_full, sm, nlat):
    p = {}
    p["w_mod"] = full["w_mod"][i]
    p["w_in"] = pack_w_in(full["w_in"][i])
    p["w_uq"] = pack_uq(full["w_mla_uq"][i])
    p["w_ukv"] = pack_ukv(full["w_mla_ukv"][i])
    for k in ("w_p_ssm", "w_p_swa", "w_p_mla", "w_out", "w_ffn_in", "w_ffn_out"):
        p[k] = full[k][i]
    p["b_mod"] = sm["b_mod"][i][None]
    p["norm1_g"] = sm["norm1_g"][i][None]
    p["norm2_g"] = sm["norm2_g"][i][None]
    p["conv_w"] = conv_full[i]
    p["conv_b"] = sm["ssm_conv_b"][i][None]
    bias = sm["ssm_dt_bias"][i].reshape(32)
    alog = sm["ssm_a_log"][i].reshape(32)
    p["bias_c"] = _lanes(bias, DT_LANE)
    p["alog_c"] = _lanes(alog, DT_LANE)
    p["bias_r"] = bias[:, None]
    p["alog_r"] = alog[:, None]
    p["d_exp"] = jnp.repeat(sm["ssm_d"][i], 64)[None]
    p["ssm_norm_g"] = sm["ssm_norm_g"][i][None]
    p["swa_q_g"] = sm["swa_q_norm_g"][i][None]
    p["swa_k_g"] = sm["swa_k_norm_g"][i][None]
    p["sink"] = jnp.broadcast_to(sm["swa_sink"][i][:, None, None], (SWA_HQ, 1, 128))
    p["q_lat_g"] = sm["mla_q_lat_g"][i][None]
    p["kv_lat_g"] = sm["mla_kv_lat_g"][i][None]
    p["mla_q_g"] = _lanes(sm["mla_q_norm_g"][i], 0, 256)
    p["mla_k_g"] = _lanes(sm["mla_k_norm_g"][i], 0, 256)
    p["swa_cfg"] = dict(w=128, vw=128, hq=SWA_HQ, grp=SWA_HQ // SWA_HKV, vcol0=C_V // 128, scale=SWA_DH ** -0.5,
                        tq=256, tk=256, band=True)
    return p


def kernel(x, c, ctx, c_ctx, w_mod, b_mod, norm1_g, norm2_g, w_in, ssm_conv_w, ssm_conv_b, ssm_dt_bias, ssm_a_log, ssm_d, ssm_norm_g, swa_q_norm_g, swa_k_norm_g, swa_sink, mla_q_lat_g, mla_kv_lat_g, w_mla_uq, w_mla_ukv, mla_q_norm_g, mla_k_norm_g, w_p_ssm, w_p_swa, w_p_mla, w_out, w_ffn_in, w_ffn_out, loss_target, m_c_ctx, m_w_mod, m_b_mod, m_norm1_g, m_norm2_g, m_w_in, m_ssm_conv_w, m_ssm_conv_b, m_ssm_dt_bias, m_ssm_a_log, m_ssm_d, m_ssm_norm_g, m_swa_q_norm_g, m_swa_k_norm_g, m_swa_sink, m_mla_q_lat_g, m_mla_kv_lat_g, m_w_mla_uq, m_w_mla_ukv, m_mla_q_norm_g, m_mla_k_norm_g, m_w_p_ssm, m_w_p_swa, m_w_p_mla, m_w_out, m_w_ffn_in, m_w_ffn_out, v_c_ctx, v_w_mod, v_b_mod, v_norm1_g, v_norm2_g, v_w_in, v_ssm_conv_w, v_ssm_conv_b, v_ssm_dt_bias, v_ssm_a_log, v_ssm_d, v_ssm_norm_g, v_swa_q_norm_g, v_swa_k_norm_g, v_swa_sink, v_mla_q_lat_g, v_mla_kv_lat_g, v_w_mla_uq, v_w_mla_ukv, v_mla_q_norm_g, v_mla_k_norm_g, v_w_p_ssm, v_w_p_swa, v_w_p_mla, v_w_out, v_w_ffn_in, v_w_ffn_out):
    loc = dict(locals())
    w = {k: loc[k] for k in WEIGHTS}
    m = {k: loc["m_" + k] for k in WEIGHTS}
    v = {k: loc["v_" + k] for k in WEIGHTS}
    nlat = x.shape[1]

    gathered = _unpack_big(gather_chips("gather_weights", _pack_big(w, BF16)), (4,))
    full = {k: _full_from_chips(k, gathered[k]) for k in BIG}
    conv_sh = jnp.pad(ssm_conv_w.reshape(10, 384), ((0, 6), (0, 0)))
    conv_full = gather_chips("gather_conv", conv_sh)[:, :10].reshape(4, 2, 5, 384).transpose(1, 2, 0, 3).reshape(2, 5, 1536)

    W = [layer_params(i, full, conv_full, w, nlat) for i in range(DEPTH)]

    loss_loc, dx, grads = local_step(x[0], c, ctx[0], loss_target[0], c_ctx, W, nlat)

    gfull = big_grads(grads)
    by_chip = {k: jnp.stack([_chips_from_full(k, a) for a in gfull[k]], axis=1) for k in BIG}
    send = jnp.concatenate([by_chip[k].astype(BF16).reshape(4, -1, 1024) for k in BIG], axis=1)
    own, got = pair_split("pair_split", send)
    pair = add_cast("pair_sum", own, got, BF16)
    recv = scatter_chips("scatter_grads", pair)
    mine = sum_blocks("sum_chips", recv)
    gbig = _unpack_big(pair_join("join_cores", mine), ())

    gs = small_grads(grads)
    small_all = gather_all("gather_small", _pack_small(gs))
    small_sum = sum_blocks("sum_small", small_all)
    full_shapes = {k: (w[k].shape if k != "ssm_conv_w" else (2, 5, 1536)) for k in SMALL}
    gsmall = _unpack_small(small_sum, full_shapes)
    chip = 2 * lax.axis_index("x") + lax.axis_index("y")
    gsmall["ssm_conv_w"] = lax.dynamic_slice_in_dim(gsmall["ssm_conv_w"], chip * 384, 384, axis=2)

    grad = {**gbig, **gsmall}
    delta, new_m, new_v = {}, {}, {}
    sm = {k: _pack_small_local(d) for k, d in (("w", w), ("g", grad), ("m", m), ("v", v))}
    r = adamw("adamw_small", sm["w"], sm["g"], sm["m"], sm["v"])
    shapes = {k: w[k].shape for k in SMALL}
    for dst, buf in zip((delta, new_m, new_v), r):
        dst.update(_unpack_small_local(buf, shapes))
    for k in BIG:
        sh = w[k].shape
        r = adamw("adamw_" + k, *[a[k].reshape(sh[0] * sh[1], sh[2]) for a in (w, grad, m, v)])
        for dst, buf in zip((delta, new_m, new_v), r):
            dst[k] = buf.reshape(sh)

    loss = lax.psum(loss_loc, ("x", "y", "c"))
    return (loss, dx[None, :nlat], *[grad[k] for k in WEIGHTS], *[delta[k] for k in WEIGHTS],
            *[new_m[k] for k in WEIGHTS], *[new_v[k] for k in WEIGHTS])


def _pack_small_local(d):
    parts = []
    for k in SMALL:
        a = d[k].astype(F32).reshape(-1)
        parts.append(jnp.pad(a, (0, (-a.shape[0]) % 1024)))
    return jnp.concatenate(parts).reshape(-1, 128)


def _unpack_small_local(buf, shapes):
    flat = buf.reshape(-1)
    out = {}
    o = 0
    for k in SMALL:
        sz = math.prod(shapes[k])
        out[k] = flat[o:o + sz].reshape(shapes[k])
        o += sz + (-sz) % 1024
    return out
```

```python
import functools
import math

import jax
import jax.numpy as jnp
from jax import lax
from jax.experimental import pallas as pl
from jax.experimental.pallas import tpu as pltpu

F32 = jnp.float32
BF16 = jnp.bfloat16
MESH = pl.DeviceIdType.MESH

D = 1024
NCTX = 256
EPS = 1e-6
ROPE_BASE = 10000.0
GRID_W = 64
DEPTH = 2
Q = 128
N_HEADS_SSM = 16
SWA_HQ, SWA_HKV, SWA_DH, SWA_WIN = 8, 2, 128, 128
MLA_H, MLA_NOPE, MLA_ROPE, MLA_V = 8, 128, 64, 128
MLA_QRANK, MLA_KVRANK = 384, 256
FFN = 2816
RT = 256
VMEM_LIMIT = 56 << 20
NEG = -1e30
LOG2E = 1.4426950408889634

C_G1, C_G2, C_G3, C_Z, C_Q, C_XS, C_B, C_C, C_K, C_V, C_CKV, C_MISC, C_PAD, C_CQ = (
    0, 1024, 2048, 3072, 4096, 5120, 6144, 6400, 6656, 6912, 7168, 7424, 7552, 7680)
UW = 8064
DT_LANE = 64

ADAM_LR, ADAM_B1, ADAM_B2, ADAM_EPS, ADAM_WD, ADAM_STEP = 0.001, 0.9, 0.999, 1e-08, 0.01, 10


def _cp(sem):
    return pltpu.CompilerParams(dimension_semantics=sem, vmem_limit_bytes=VMEM_LIMIT)


def _pick(n, cands):
    for c in cands:
        if n % c == 0:
            return c
    return n


_TN = (1536, 1408, 1152, 1024, 896, 768, 512, 384, 256, 128)


def mm(a, b, out_dtype, name, trans_b=False):
    m, k = a.shape
    n = b.shape[0] if trans_b else b.shape[1]
    tm = _pick(m, (768, 512, 256, 128, 8))
    tn = _pick(n, _TN)
    tk = k if k <= 2048 else _pick(k, (1408, 1152, 1024, 896, 768, 512))
    nk = k // tk
    b_spec = (pl.BlockSpec((tn, tk), lambda i, j, kk: (j, kk)) if trans_b
              else pl.BlockSpec((tk, tn), lambda i, j, kk: (kk, j)))

    def body(a_ref, b_ref, o_ref, *acc):
        p = _d(a_ref[...], b_ref[...], ((1,), (1 if trans_b else 0,)))
        if nk == 1:
            o_ref[...] = p.astype(out_dtype)
        else:
            kk = pl.program_id(2)

            @pl.when(kk == 0)
            def _():
                acc[0][...] = p

            @pl.when(kk > 0)
            def _():
                acc[0][...] += p

            @pl.when(kk == nk - 1)
            def _():
                o_ref[...] = acc[0][...].astype(out_dtype)

    return pl.pallas_call(
        body, out_shape=jax.ShapeDtypeStruct((m, n), out_dtype), grid=(m // tm, n // tn, nk),
        in_specs=[pl.BlockSpec((tm, tk), lambda i, j, kk: (i, kk)), b_spec],
        out_specs=pl.BlockSpec((tm, tn), lambda i, j, kk: (i, j)),
        scratch_shapes=[] if nk == 1 else [pltpu.VMEM((tm, tn), F32)],
        name=name, compiler_params=_cp(("parallel", "parallel", "arbitrary")))(a, b)


def mm_tn(a, b, name, out_dtype=BF16):
    t, ka = a.shape
    _, nb = b.shape
    ta = _pick(ka, (1024, 1408, 768, 512, 384, 256, 128))
    tb = _pick(nb, _TN)
    tt = _pick(t, (768, 512, 256, 128, 8))
    nt = t // tt

    def body(a_ref, b_ref, o_ref, acc):
        p = _d(a_ref[...], b_ref[...], ((0,), (0,)))
        s = pl.program_id(2)

        @pl.when(s == 0)
        def _():
            acc[...] = p

        @pl.when(s > 0)
        def _():
            acc[...] += p

        @pl.when(s == nt - 1)
        def _():
            o_ref[...] = acc[...].astype(out_dtype)

    return pl.pallas_call(
        body, out_shape=jax.ShapeDtypeStruct((ka, nb), out_dtype), grid=(ka // ta, nb // tb, nt),
        in_specs=[pl.BlockSpec((tt, ta), lambda i, j, s: (s, i)), pl.BlockSpec((tt, tb), lambda i, j, s: (s, j))],
        out_specs=pl.BlockSpec((ta, tb), lambda i, j, s: (i, j)), scratch_shapes=[pltpu.VMEM((ta, tb), F32)],
        name=name, compiler_params=_cp(("parallel", "parallel", "arbitrary")))(a, b)


def _rms(x, g, n=None):
    n = x.shape[-1] if n is None else n
    r = lax.rsqrt(jnp.sum(x * x, axis=-1, keepdims=True) * (1.0 / n) + EPS)
    return x * r * g


def _silu(x):
    return x * jax.nn.sigmoid(x)


def _modulate(x, g, sc, sh):
    return _rms(x, g) * (1.0 + sc) + sh


def _swap(x, s):
    ax = x.ndim - 1
    w = x.shape[ax]
    lane = lax.broadcasted_iota(jnp.int32, x.shape, ax)
    lo = (lane & s) == 0
    return jnp.where(lo, pltpu.roll(x, w - s, ax), pltpu.roll(x, s, ax))


@functools.partial(jax.custom_vjp, nondiff_argnums=(3,))
def _rope(x, cos, sin, s):
    return x * cos + _swap(x, s) * sin


def _rope_fwd(x, cos, sin, s):
    return _rope(x, cos, sin, s), (cos, sin)


def _rope_bwd(s, res, g):
    cos, sin = res
    return g * cos - _swap(g, s) * sin, jnp.zeros_like(cos), jnp.zeros_like(sin)


_rope.defvjp(_rope_fwd, _rope_bwd)


@jax.custom_vjp
def _softplus(x):
    return jnp.maximum(x, 0.0) + jnp.log(1.0 + jnp.exp(-jnp.abs(x)))


def _softplus_fwd(x):
    return _softplus(x), x


def _softplus_bwd(x, g):
    return (g * jax.nn.sigmoid(x),)


_softplus.defvjp(_softplus_fwd, _softplus_bwd)


def _d(a, b, dims):
    return lax.dot_general(a.astype(BF16), b.astype(BF16), (dims, ((), ())), preferred_element_type=F32)


@jax.custom_vjp
def bdot(a, b):
    return _d(a, b, ((1,), (0,)))


bdot.defvjp(lambda a, b: (bdot(a, b), (a, b)),
            lambda r, g: (_d(g, r[1], ((1,), (1,))), _d(r[0], g, ((0,), (0,)))))


@jax.custom_vjp
def bdot_nt(a, b):
    return _d(a, b, ((1,), (1,)))


bdot_nt.defvjp(lambda a, b: (bdot_nt(a, b), (a, b)),
               lambda r, g: (_d(g, r[1], ((1,), (0,))), _d(g, r[0], ((0,), (0,)))))


@jax.custom_vjp
def bdot_tn(a, b):
    return _d(a, b, ((0,), (0,)))


bdot_tn.defvjp(lambda a, b: (bdot_tn(a, b), (a, b)),
               lambda r, g: (_d(r[1], g, ((1,), (1,))), _d(r[0], g, ((1,), (0,)))))


def _tri(rev):
    i = lax.broadcasted_iota(jnp.int32, (Q, Q), 0)
    j = lax.broadcasted_iota(jnp.int32, (Q, Q), 1)
    return (i <= j) if rev else (i >= j)


def _split3(a):
    hi = a.astype(BF16)
    r = a - hi.astype(F32)
    mid = r.astype(BF16)
    lo = (r - mid.astype(F32)).astype(BF16)
    return hi, mid, lo


def _cum_cols_impl(a, rev):
    t = _tri(rev).astype(BF16)
    return sum(jnp.dot(t, p, preferred_element_type=F32) for p in _split3(a))


def _cum_rows_impl(a, rev):
    t = _tri(not rev).astype(BF16)
    return sum(jnp.dot(p, t, preferred_element_type=F32) for p in _split3(a))


@functools.partial(jax.custom_vjp, nondiff_argnums=(1,))
def cum_cols(a, rev):
    return _cum_cols_impl(a, rev)


cum_cols.defvjp(lambda a, rev: (_cum_cols_impl(a, rev), None), lambda rev, _, g: (_cum_cols_impl(g, not rev),))


@functools.partial(jax.custom_vjp, nondiff_argnums=(1,))
def cum_rows(a, rev):
    return _cum_rows_impl(a, rev)


cum_rows.defvjp(lambda a, rev: (_cum_rows_impl(a, rev), None), lambda rev, _, g: (_cum_rows_impl(g, not rev),))


def _rs(w, cb=0):
    return pl.BlockSpec((RT, w), lambda i: (i, cb))


def _ps(shape):
    nd = len(shape)
    return pl.BlockSpec(shape, lambda i: (0,) * nd)


def _gs(w, cb, nlat):
    return pl.BlockSpec((1, 1, w), lambda i: (i // nlat, 0, cb))


def _rowcall(name, body, n, ins, outs, scratch=()):
    return pl.pallas_call(
        body, out_shape=[o[0] for o in outs], grid=(n // RT,), in_specs=[s for _, s in ins],
        out_specs=[s for _, s in outs], scratch_shapes=list(scratch), name=name,
        compiler_params=_cp(("arbitrary",)))(*[a for a, _ in ins])


def _acc(ref, val, first):
    @pl.when(first)
    def _():
        ref[...] = val

    @pl.when(jnp.logical_not(first))
    def _():
        ref[...] += val


def _sd(shape, dt):
    return jax.ShapeDtypeStruct(shape, dt)


def resid_mod_fwd(name, xp, o, mod_gt, gt_i, mod_n, sh_i, sc_i, norm_g, nlat):
    n = xp.shape[0]
    has_res = o is not None

    def body(*refs):
        if has_res:
            xp_ref, o_ref, gt_ref, sh_ref, sc_ref, g_ref, xn_ref, h_ref = refs
            xn = xp_ref[...] + gt_ref[0] * o_ref[...]
            xn_ref[...] = xn
        else:
            xp_ref, sh_ref, sc_ref, g_ref, h_ref = refs
            xn = xp_ref[...]
        h_ref[...] = _modulate(xn, g_ref[...], sc_ref[0], sh_ref[0]).astype(BF16)

    ins = [(xp, _rs(D))]
    if has_res:
        ins += [(o, _rs(D)), (mod_gt, _gs(D, gt_i, nlat))]
    ins += [(mod_n, _gs(D, sh_i, nlat)), (mod_n, _gs(D, sc_i, nlat)), (norm_g, _ps((1, D)))]
    outs = ([(_sd((n, D), F32), _rs(D))] if has_res else []) + [(_sd((n, D), BF16), _rs(D))]
    r = _rowcall(name, body, n, ins, outs)
    return (r[0], r[1]) if has_res else (xp, r[0])


def resid_mod_bwd(name, xn, dxn, dh, o, mod_gt, gt_i, mod_n, sh_i, sc_i, norm_g, nlat):
    n = xn.shape[0]
    has_res = o is not None

    def body(*refs):
        i = pl.program_id(0)
        if has_res:
            (xn_ref, dxn_ref, dh_ref, o_ref, gt_ref, sh_ref, sc_ref, g_ref,
             dx_ref, do_ref, dgt_ref, dsh_ref, dsc_ref, dg_ref) = refs
        else:
            xn_ref, dxn_ref, dh_ref, sh_ref, sc_ref, g_ref, dx_ref, dsh_ref, dsc_ref, dg_ref = refs
        _, vjp = jax.vjp(_modulate, xn_ref[...], g_ref[...], sc_ref[0], sh_ref[0])
        dx, dg, dsc, dsh = vjp(dh_ref[...])
        dx = dx + dxn_ref[...]
        dx_ref[...] = dx
        gfirst = (i == 0) | (i == nlat)
        _acc(dg_ref, dg, i == 0)
        _acc(dsh_ref, dsh[None], gfirst)
        _acc(dsc_ref, dsc[None], gfirst)
        if has_res:
            do_ref[...] = (gt_ref[0] * dx).astype(BF16)
            _acc(dgt_ref, jnp.sum(dx * o_ref[...], axis=0, keepdims=True)[None], gfirst)

    ins = [(xn, _rs(D)), (dxn, _rs(D)), (dh, _rs(D))]
    if has_res:
        ins += [(o, _rs(D)), (mod_gt, _gs(D, gt_i, nlat))]
    ins += [(mod_n, _gs(D, sh_i, nlat)), (mod_n, _gs(D, sc_i, nlat)), (norm_g, _ps((1, D)))]
    gacc = (_sd((2, 1, D), F32), _gs(D, 0, nlat))
    outs = [(_sd((n, D), F32), _rs(D))]
    if has_res:
        outs += [(_sd((n, D), BF16), _rs(D)), gacc]
    outs += [gacc, gacc, (_sd((1, D), F32), _ps((1, D)))]
    r = _rowcall(name, body, n, ins, outs)
    if has_res:
        return r
    return r[0], None, None, r[1], r[2], r[3]


def resid_loss(name, xp, o, mod_gt, gt_i, target, nlat):
    n = xp.shape[0]

    def body(xp_ref, o_ref, gt_ref, t_ref, loss_ref, dx_ref, do_ref, dgt_ref):
        i = pl.program_id(0)
        gt = gt_ref[0]

        @pl.when(i < nlat)
        def _():
            err = xp_ref[...] + gt * o_ref[...] - t_ref[...]
            dx = err * (1.0 / D)
            dx_ref[...] = dx
            do_ref[...] = (gt * dx).astype(BF16)
            _acc(loss_ref, jnp.full((1, 128), 0.5 / D, F32) * jnp.sum(err * err), i == 0)
            _acc(dgt_ref, jnp.sum(dx * o_ref[...], axis=0, keepdims=True)[None], i == 0)

        @pl.when(i >= nlat)
        def _():
            dx_ref[...] = jnp.zeros((RT, D), F32)
            do_ref[...] = jnp.zeros((RT, D), BF16)
            dgt_ref[...] = jnp.zeros((1, 1, D), F32)

    tgt_spec = pl.BlockSpec((RT, D), lambda i: (jnp.minimum(i, nlat - 1), 0))
    ins = [(xp, _rs(D)), (o, _rs(D)), (mod_gt, _gs(D, gt_i, nlat)), (target, tgt_spec)]
    outs = [(_sd((1, 128), F32), _ps((1, 128))), (_sd((n, D), F32), _rs(D)), (_sd((n, D), BF16), _rs(D)),
            (_sd((2, 1, D), F32), _gs(D, 0, nlat))]
    return _rowcall(name, body, n, ins, outs)


def mod_fwd(name, c8, w_mod, b_mod):
    tn = 1536

    def body(c_ref, w_ref, b_ref, o_ref, s_ref):
        s = _silu(c_ref[...]).astype(BF16)
        s_ref[...] = s
        o_ref[...] = jnp.dot(s, w_ref[...], preferred_element_type=F32) + b_ref[...]

    return pl.pallas_call(
        body, out_shape=[_sd((8, 6 * D), F32), _sd((8, D), BF16)], grid=(6 * D // tn,),
        in_specs=[pl.BlockSpec((8, D), lambda j: (0, 0)), pl.BlockSpec((D, tn), lambda j: (0, j)),
                  pl.BlockSpec((1, tn), lambda j: (0, j))],
        out_specs=[pl.BlockSpec((8, tn), lambda j: (0, j)), pl.BlockSpec((8, D), lambda j: (0, 0))],
        name=name, compiler_params=_cp(("arbitrary",)))(c8, w_mod, b_mod)


def mod_small_bwd(name, c8, dsilu, dmod8):
    def body(c_ref, ds_ref, dm_ref, dc_ref, db_ref):
        _, vjp = jax.vjp(_silu, c_ref[...])
        dc_ref[...] = vjp(ds_ref[...])[0]
        db_ref[...] = jnp.sum(dm_ref[...], axis=0, keepdims=True)

    return pl.pallas_call(
        body, out_shape=[_sd((8, D), F32), _sd((1, 6 * D), F32)], grid=(1,),
        in_specs=[pl.BlockSpec((8, D), lambda j: (0, 0)), pl.BlockSpec((8, D), lambda j: (0, 0)),
                  pl.BlockSpec((8, 6 * D), lambda j: (0, 0))],
        out_specs=[pl.BlockSpec((8, D), lambda j: (0, 0)), pl.BlockSpec((1, 6 * D), lambda j: (0, 0))],
        name=name, compiler_params=_cp(("arbitrary",)))(c8, dsilu, dmod8)


def _conv_taps(x, nlat):
    n = x.shape[0]
    r = lax.broadcasted_iota(jnp.int32, x.shape, 0)
    lo = jnp.where(r < nlat, 0, nlat)
    hi = jnp.where(r < nlat, nlat, n)
    taps = []
    for o in (-2, -1, 0, 1, 2):
        xs = x if o == 0 else pltpu.roll(x, (-o) % n, 0)
        t = r + o
        taps.append(jnp.where((t >= lo) & (t < hi), xs, 0.0))
    return taps


def conv_fwd(name, u, w, b, nlat_rows):
    n = u.shape[0]

    def body(x_ref, w_ref, b_ref, o_ref):
        taps = _conv_taps(x_ref[...], nlat_rows)
        wv = w_ref[...]
        pre = b_ref[...] + sum(taps[k] * wv[k:k + 1, :] for k in range(5))
        o_ref[...] = _silu(pre)

    return pl.pallas_call(
        body, out_shape=_sd((n, 1536), F32), grid=(12,),
        in_specs=[pl.BlockSpec((n, 128), lambda j: (0, C_XS // 128 + j)), pl.BlockSpec((5, 128), lambda j: (0, j)),
                  pl.BlockSpec((1, 128), lambda j: (0, j))],
        out_specs=pl.BlockSpec((n, 128), lambda j: (0, j)),
        name=name, compiler_params=_cp(("parallel",)))(u, w, b)


def conv_bwd(name, u, dact, w, b, nlat_rows):
    n = u.shape[0]

    def body(x_ref, da_ref, w_ref, b_ref, dx_ref, dw_ref, db_ref):
        taps = _conv_taps(x_ref[...], nlat_rows)
        wv = w_ref[...]
        pre = b_ref[...] + sum(taps[k] * wv[k:k + 1, :] for k in range(5))
        s = jax.nn.sigmoid(pre)
        dpre = da_ref[...] * (s * (1.0 + pre * (1.0 - s)))
        db_ref[...] = jnp.sum(dpre, axis=0, keepdims=True)
        rows = lax.broadcasted_iota(jnp.int32, (5, 128), 0)
        dw = jnp.zeros((5, 128), F32)
        for k in range(5):
            dw = dw + jnp.where(rows == k, jnp.sum(dpre * taps[k], axis=0, keepdims=True), 0.0)
        dw_ref[...] = dw
        r = lax.broadcasted_iota(jnp.int32, dpre.shape, 0)
        lo = jnp.where(r < nlat_rows, 0, nlat_rows)
        hi = jnp.where(r < nlat_rows, nlat_rows, n)
        dx = jnp.zeros_like(dpre)
        for k in range(5):
            o = k - 2
            ds = dpre if o == 0 else pltpu.roll(dpre, o % n, 0)
            t = r - o
            dx = dx + jnp.where((t >= lo) & (t < hi), ds, 0.0) * wv[k:k + 1, :]
        dx_ref[...] = dx.astype(BF16)

    return pl.pallas_call(
        body, out_shape=[_sd((n, 1536), BF16), _sd((5, 1536), F32), _sd((1, 1536), F32)], grid=(12,),
        in_specs=[pl.BlockSpec((n, 128), lambda j: (0, C_XS // 128 + j)), pl.BlockSpec((n, 128), lambda j: (0, j)),
                  pl.BlockSpec((5, 128), lambda j: (0, j)), pl.BlockSpec((1, 128), lambda j: (0, j))],
        out_specs=[pl.BlockSpec((n, 128), lambda j: (0, j)), pl.BlockSpec((5, 128), lambda j: (0, j)),
                   pl.BlockSpec((1, 128), lambda j: (0, j))],
        name=name, compiler_params=_cp(("parallel",)))(u, dact, w, b)


def _ssd_chunk(rev, dirn, g, x4, bm, cm, misc, dtrow, bias_c, alog_c, bias_r, alog_r, h4):
    dt_c = _softplus(misc + bias_c)
    a_c = dt_c * (-jnp.exp(alog_c))
    dt_r = _softplus(dtrow + bias_r)
    a_r = dt_r * (-jnp.exp(alog_r))
    cs_c = cum_cols(a_c, rev)
    cs_r = cum_rows(a_r, rev)
    tot_c = jnp.sum(a_c, axis=0, keepdims=True)
    cb = bdot_nt(cm, bm)
    tri = _tri(rev)
    lane = lax.broadcasted_iota(jnp.int32, (1, 128), 1)
    row16 = lax.broadcasted_iota(jnp.int32, (16, 1), 0)
    prow = lax.broadcasted_iota(jnp.int32, (128, 1), 0)
    ys, hs = [], []
    for p in range(4):
        ydiag = 0.0
        wst = 0.0
        eoff = 0.0
        hscale = 0.0
        for e in range(2):
            hg = 8 * g + 2 * p + e
            oh_c = (lane == DT_LANE + 16 * dirn + hg).astype(F32)
            dt_h = jnp.sum(dt_c * oh_c, axis=1, keepdims=True)
            cs_h = jnp.sum(cs_c * oh_c, axis=1, keepdims=True)
            tot_h = jnp.sum(tot_c * oh_c, axis=1, keepdims=True)
            csr_h = jnp.sum(cs_r * (row16 == hg).astype(F32), axis=0, keepdims=True)
            seg = jnp.exp(jnp.where(tri, cs_h - csr_h, -jnp.inf))
            hm = ((lane < 64) if e == 0 else (lane >= 64)).astype(F32)
            ydiag = ydiag + bdot(cb * seg, x4[p] * (dt_h * hm))
            wst = wst + (dt_h * jnp.exp(tot_h - cs_h)) * hm
            eoff = eoff + jnp.exp(cs_h) * hm
            hscale = hscale + jnp.exp(tot_h) * ((prow < 64) if e == 0 else (prow >= 64)).astype(F32)
        ys.append(ydiag + bdot_nt(cm, h4[p]) * eoff)
        hs.append(h4[p] * hscale + bdot_tn(x4[p] * wst, bm))
    return ys, hs


def _ssd_specs(nlat_chunks, rev, dirn, bwd):
    nc = nlat_chunks + 2

    def chunk(s):
        if bwd:
            s = nc - 1 - s
        return (nlat_chunks + 1 - s) if rev else (s + nlat_chunks) % nc

    def step(s):
        return (nc - 1 - s) if bwd else s

    return dict(
        x=pl.BlockSpec((Q, 512), lambda g, s: (chunk(s), g)),
        b=pl.BlockSpec((Q, 128), lambda g, s: (chunk(s), 8 + g)),
        c=pl.BlockSpec((Q, 128), lambda g, s: (chunk(s), 10 + g)),
        misc=pl.BlockSpec((Q, 128), lambda g, s: (chunk(s), C_MISC // 128)),
        dtrow=pl.BlockSpec((16, Q), lambda g, s: (dirn, chunk(s))),
        p_c=pl.BlockSpec((1, 128), lambda g, s: (0, 0)),
        p_r=pl.BlockSpec((16, 1), lambda g, s: (dirn, 0)),
        y=pl.BlockSpec((Q, 512), lambda g, s: (chunk(s), g)),
        hsave=pl.BlockSpec((1, 1, 512, 128), lambda g, s: (g, step(s), 0, 0)),
        bc_out=pl.BlockSpec((Q, 128), lambda g, s: (chunk(s), g)),
        misc_out=pl.BlockSpec((1, Q, 128), lambda g, s: (g, chunk(s), 0)),
        dtrow_out=pl.BlockSpec((1, 16, Q), lambda g, s: (g, 0, chunk(s))),
        pacc_c=pl.BlockSpec((1, 128), lambda g, s: (0, 0)),
        pacc_r=pl.BlockSpec((16, 1), lambda g, s: (0, 0)),
    )


def ssd_fwd(name, xbc, u, dtrow, bias_c, alog_c, bias_r, alog_r, nlat_chunks, rev, dirn):
    n = xbc.shape[0]
    nc = nlat_chunks + 2
    sp = _ssd_specs(nlat_chunks, rev, dirn, False)

    def body(x_ref, b_ref, c_ref, m_ref, r_ref, bc_ref, ac_ref, br_ref, ar_ref, y_ref, hs_ref, h_s):
        g = pl.program_id(0)
        s = pl.program_id(1)

        @pl.when(s == 0)
        def _():
            h_s[...] = jnp.zeros((512, 128), F32)

        hs_ref[0, 0] = h_s[...]
        x4 = [x_ref[:, 128 * p:128 * p + 128] for p in range(4)]
        h4 = [h_s[128 * p:128 * p + 128, :] for p in range(4)]
        ys, hs = _ssd_chunk(rev, dirn, g, x4, b_ref[...], c_ref[...], m_ref[...], r_ref[...],
                            bc_ref[...], ac_ref[...], br_ref[...], ar_ref[...], h4)
        for p in range(4):
            y_ref[:, 128 * p:128 * p + 128] = ys[p]
            h_s[128 * p:128 * p + 128, :] = hs[p]

    return pl.pallas_call(
        body, out_shape=[_sd((n, 1024), F32), _sd((2, nc, 512, 128), F32)], grid=(2, nc),
        in_specs=[sp["x"], sp["b"], sp["c"], sp["misc"], sp["dtrow"], sp["p_c"], sp["p_c"], sp["p_r"], sp["p_r"]],
        out_specs=[sp["y"], sp["hsave"]], scratch_shapes=[pltpu.VMEM((512, 128), F32)],
        name=name, compiler_params=_cp(("arbitrary", "arbitrary")))(
            xbc, xbc, xbc, u, dtrow, bias_c, alog_c, bias_r, alog_r)


def ssd_bwd(name, xbc, u, dtrow, bias_c, alog_c, bias_r, alog_r, hsave, dy, acc, nlat_chunks, rev, dirn):
    n = xbc.shape[0]
    sp = _ssd_specs(nlat_chunks, rev, dirn, True)

    def body(x_ref, b_ref, c_ref, m_ref, r_ref, bc_ref, ac_ref, br_ref, ar_ref, hs_ref, dy_ref, ax_ref, ab_ref, acc_ref,
             dx_ref, db_ref, dc_ref, dm_ref, dr_ref, dbc_ref, dac_ref, dbr_ref, dar_ref, dh_s):
        g = pl.program_id(0)
        s = pl.program_id(1)

        @pl.when(s == 0)
        def _():
            dh_s[...] = jnp.zeros((512, 128), F32)

        x4 = [x_ref[:, 128 * p:128 * p + 128] for p in range(4)]
        h4 = [hs_ref[0, 0, 128 * p:128 * p + 128, :] for p in range(4)]
        fn = functools.partial(_ssd_chunk, rev, dirn, g)
        _, vjp = jax.vjp(fn, x4, b_ref[...], c_ref[...], m_ref[...], r_ref[...],
                         bc_ref[...], ac_ref[...], br_ref[...], ar_ref[...], h4)
        dys = [dy_ref[:, 128 * p:128 * p + 128] for p in range(4)]
        dhs = [dh_s[128 * p:128 * p + 128, :] for p in range(4)]
        dx4, db, dc, dm, dr, dbc, dac, dbr, dar, dh4 = vjp((dys, dhs))
        for p in range(4):
            dx_ref[:, 128 * p:128 * p + 128] = dx4[p] + ax_ref[:, 128 * p:128 * p + 128]
            dh_s[128 * p:128 * p + 128, :] = dh4[p]
        db_ref[...] = db + ab_ref[...]
        dc_ref[...] = dc + acc_ref[...]
        dm_ref[0] = dm
        dr_ref[0] = dr
        first = (g == 0) & (s == 0)
        _acc(dbc_ref, dbc, first)
        _acc(dac_ref, dac, first)
        _acc(dbr_ref, dbr, first)
        _acc(dar_ref, dar, first)

    ax, ab, ac = acc
    return pl.pallas_call(
        body,
        out_shape=[_sd((n, 1024), F32), _sd((n, 256), F32), _sd((n, 256), F32), _sd((2, n, 128), F32),
                   _sd((2, 16, n), F32), _sd((1, 128), F32), _sd((1, 128), F32), _sd((16, 1), F32), _sd((16, 1), F32)],
        grid=(2, nlat_chunks + 2),
        in_specs=[sp["x"], sp["b"], sp["c"], sp["misc"], sp["dtrow"], sp["p_c"], sp["p_c"], sp["p_r"], sp["p_r"],
                  sp["hsave"], sp["y"], sp["y"], sp["bc_out"], sp["bc_out"]],
        out_specs=[sp["y"], sp["bc_out"], sp["bc_out"], sp["misc_out"], sp["dtrow_out"],
                   sp["pacc_c"], sp["pacc_c"], sp["pacc_r"], sp["pacc_r"]],
        scratch_shapes=[pltpu.VMEM((512, 128), F32)],
        name=name, compiler_params=_cp(("arbitrary", "arbitrary")))(
            xbc, xbc, xbc, u, dtrow, bias_c, alog_c, bias_r, alog_r, hsave, dy, ax, ab, ac)


def _ssd_out(yf, yb, xs, z, g, dexp):
    return _rms((yf + yb + dexp * xs) * _silu(z), g)


def ssd_out_fwd(name, yf, yb, xbc, u, g, dexp):
    n = yf.shape[0]

    def body(yf_ref, yb_ref, xs_ref, z_ref, g_ref, d_ref, o_ref):
        o_ref[...] = _ssd_out(yf_ref[...], yb_ref[...], xs_ref[...], z_ref[...], g_ref[...], d_ref[...]).astype(BF16)

    return _rowcall(name, body, n,
                    [(yf, _rs(D)), (yb, _rs(D)), (xbc, _rs(D, 0)), (u, _rs(D, C_Z // D)), (g, _ps((1, D))), (dexp, _ps((1, D)))],
                    [(_sd((n, D), BF16), _rs(D))])[0]


def ssd_out_bwd(name, yf, yb, xbc, u, g, dexp, dys):
    n = yf.shape[0]

    def body(yf_ref, yb_ref, xs_ref, z_ref, g_ref, d_ref, dys_ref, dy_ref, dxs_ref, dz_ref, dg_ref, dd_ref):
        i = pl.program_id(0)
        _, vjp = jax.vjp(_ssd_out, yf_ref[...], yb_ref[...], xs_ref[...], z_ref[...], g_ref[...], d_ref[...])
        dyf, _, dxs, dz, dg, dd = vjp(dys_ref[...])
        dy_ref[...] = dyf
        dxs_ref[...] = dxs
        dz_ref[...] = dz.astype(BF16)
        _acc(dg_ref, dg, i == 0)
        _acc(dd_ref, dd, i == 0)

    return _rowcall(name, body, n,
                    [(yf, _rs(D)), (yb, _rs(D)), (xbc, _rs(D, 0)), (u, _rs(D, C_Z // D)), (g, _ps((1, D))), (dexp, _ps((1, D))),
                     (dys, _rs(D))],
                    [(_sd((n, D), F32), _rs(D)), (_sd((n, D), F32), _rs(D)), (_sd((n, D), BF16), _rs(D)),
                     (_sd((1, D), F32), _ps((1, D))), (_sd((1, D), F32), _ps((1, D)))])


def _normrope(x, g, cos, sin, s, n=None):
    return _rope(_rms(x, g, n), cos, sin, s)


def swa_prep_fwd(name, u, gq, gk, cos, sin):
    n = u.shape[0]

    def body(q_ref, k_ref, gq_ref, gk_ref, cos_ref, sin_ref, qs_ref, ks_ref):
        cs, sn = cos_ref[...], sin_ref[...]
        for h in range(SWA_HQ):
            sl = slice(128 * h, 128 * h + 128)
            qs_ref[:, sl] = _normrope(q_ref[:, sl], gq_ref[...], cs, sn, 32).astype(BF16)
        for h in range(SWA_HKV):
            sl = slice(128 * h, 128 * h + 128)
            ks_ref[:, sl] = _normrope(k_ref[:, sl], gk_ref[...], cs, sn, 32).astype(BF16)

    return _rowcall(name, body, n,
                    [(u, _rs(1024, C_Q // 1024)), (u, _rs(256, C_K // 256)), (gq, _ps((1, 128))), (gk, _ps((1, 128))),
                     (cos, _rs(128)), (sin, _rs(128))],
                    [(_sd((n, 1024), BF16), _rs(1024)), (_sd((n, 256), BF16), _rs(256))])


def swa_prep_bwd(name, u, gq, gk, cos, sin, dqs, dks, dv):
    n = u.shape[0]

    def body(q_ref, k_ref, gq_ref, gk_ref, cos_ref, sin_ref, dqs_ref, dks_ref, dv_ref,
             dq_ref, dk_ref, dvo_ref, dgq_ref, dgk_ref):
        i = pl.program_id(0)
        cs, sn = cos_ref[...], sin_ref[...]
        fn = lambda x, g: _normrope(x, g, cs, sn, 32)
        dgq = jnp.zeros((1, 128), F32)
        dgk = jnp.zeros((1, 128), F32)
        for h in range(SWA_HQ):
            sl = slice(128 * h, 128 * h + 128)
            _, vjp = jax.vjp(fn, q_ref[:, sl], gq_ref[...])
            dx, dg = vjp(dqs_ref[:, sl])
            dq_ref[:, sl] = dx.astype(BF16)
            dgq = dgq + dg
        for h in range(SWA_HKV):
            sl = slice(128 * h, 128 * h + 128)
            _, vjp = jax.vjp(fn, k_ref[:, sl], gk_ref[...])
            dx, dg = vjp(dks_ref[:, sl])
            dk_ref[:, sl] = dx.astype(BF16)
            dgk = dgk + dg
        dvo_ref[...] = dv_ref[...].astype(BF16)
        _acc(dgq_ref, dgq, i == 0)
        _acc(dgk_ref, dgk, i == 0)

    return _rowcall(name, body, n,
                    [(u, _rs(1024, C_Q // 1024)), (u, _rs(256, C_K // 256)), (gq, _ps((1, 128))), (gk, _ps((1, 128))),
                     (cos, _rs(128)), (sin, _rs(128)), (dqs, _rs(1024)), (dks, _rs(256)), (dv, _rs(256))],
                    [(_sd((n, 1024), BF16), _rs(1024)), (_sd((n, 256), BF16), _rs(256)), (_sd((n, 256), BF16), _rs(256)),
                     (_sd((1, 128), F32), _ps((1, 128))), (_sd((1, 128), F32), _ps((1, 128)))])


def lat_norm_fwd(name, u, g_kv, g_q):
    n = u.shape[0]

    def body(ckv_ref, cq_ref, gkv_ref, gq_ref, okv_ref, oq_ref):
        okv_ref[...] = _rms(ckv_ref[...], gkv_ref[...]).astype(BF16)
        oq_ref[...] = _rms(cq_ref[...], gq_ref[...]).astype(BF16)

    return _rowcall(name, body, n,
                    [(u, _rs(256, C_CKV // 256)), (u, _rs(384, C_CQ // 384)), (g_kv, _ps((1, 256))), (g_q, _ps((1, 384)))],
                    [(_sd((n, 256), BF16), _rs(256)), (_sd((n, 384), BF16), _rs(384))])


def lat_norm_bwd(name, u, g_kv, g_q, dkvn, dqn):
    n = u.shape[0]

    def body(ckv_ref, cq_ref, gkv_ref, gq_ref, dkvn_ref, dqn_ref, dckv_ref, dcq_ref, dgkv_ref, dgq_ref):
        i = pl.program_id(0)
        _, vjp = jax.vjp(_rms, ckv_ref[...], gkv_ref[...])
        dx, dg = vjp(dkvn_ref[...])
        dckv_ref[...] = dx.astype(BF16)
        _acc(dgkv_ref, dg, i == 0)
        _, vjp = jax.vjp(_rms, cq_ref[...], gq_ref[...])
        dx, dg = vjp(dqn_ref[...])
        dcq_ref[...] = dx.astype(BF16)
        _acc(dgq_ref, dg, i == 0)

    return _rowcall(name, body, n,
                    [(u, _rs(256, C_CKV // 256)), (u, _rs(384, C_CQ // 384)), (g_kv, _ps((1, 256))), (g_q, _ps((1, 384))),
                     (dkvn, _rs(256)), (dqn, _rs(384))],
                    [(_sd((n, 256), BF16), _rs(256)), (_sd((n, 384), BF16), _rs(384)),
                     (_sd((1, 256), F32), _ps((1, 256))), (_sd((1, 384), F32), _ps((1, 384)))])


def _lane_lt64(x):
    return (lax.broadcasted_iota(jnp.int32, (1, 128), 1) < 64).astype(F32) * x


def _mla_krope(misc, g, cos, sin):
    return _normrope(_lane_lt64(misc), g, cos, sin, 16, MLA_ROPE)


def mla_prep_fwd(name, kv, qp, u, qg, kg, cos, sin):
    n = kv.shape[0]

    def body(kv_ref, v_ref, q_ref, m_ref, qg_ref, kg_ref, cos_ref, sin_ref, km_ref, qm_ref, vm_ref):
        cs, sn = cos_ref[...], sin_ref[...]
        vm_ref[...] = v_ref[...].astype(BF16)
        kr = _mla_krope(m_ref[...], kg_ref[:, 128:256], cs, sn).astype(BF16)
        for h in range(MLA_H):
            km_ref[:, 256 * h:256 * h + 128] = _rms(kv_ref[:, 128 * h:128 * h + 128], kg_ref[:, 0:128]).astype(BF16)
            km_ref[:, 256 * h + 128:256 * h + 256] = kr
            qm_ref[:, 256 * h:256 * h + 128] = _rms(q_ref[:, 256 * h:256 * h + 128], qg_ref[:, 0:128]).astype(BF16)
            qm_ref[:, 256 * h + 128:256 * h + 256] = _normrope(
                q_ref[:, 256 * h + 128:256 * h + 256], qg_ref[:, 128:256], cs, sn, 16, MLA_ROPE).astype(BF16)

    return _rowcall(name, body, n,
                    [(kv, _rs(1024, 0)), (kv, _rs(1024, 1)), (qp, _rs(2048)), (u, _rs(128, C_MISC // 128)), (qg, _ps((1, 256))),
                     (kg, _ps((1, 256))), (cos, _rs(128)), (sin, _rs(128))],
                    [(_sd((n, 2048), BF16), _rs(2048)), (_sd((n, 2048), BF16), _rs(2048)), (_sd((n, 1024), BF16), _rs(1024))])


def mla_prep_bwd(name, kv, qp, u, qg, kg, cos, sin, dkm, dqm, dv):
    n = kv.shape[0]

    def body(kv_ref, q_ref, m_ref, qg_ref, kg_ref, cos_ref, sin_ref, dkm_ref, dqm_ref, dv_ref,
             dkv_ref, dq_ref, dkr_ref, dqg_ref, dkg_ref):
        i = pl.program_id(0)
        cs, sn = cos_ref[...], sin_ref[...]
        fr = lambda x, g: _normrope(x, g, cs, sn, 16, MLA_ROPE)
        dkg_n = jnp.zeros((1, 128), F32)
        dqg_n = jnp.zeros((1, 128), F32)
        dqg_r = jnp.zeros((1, 128), F32)
        dkr_sum = jnp.zeros((RT, 128), F32)
        for h in range(MLA_H):
            _, vjp = jax.vjp(_rms, kv_ref[:, 128 * h:128 * h + 128], kg_ref[:, 0:128])
            dx, dg = vjp(dkm_ref[:, 256 * h:256 * h + 128])
            dkv_ref[:, 128 * h:128 * h + 128] = dx.astype(BF16)
            dkg_n = dkg_n + dg
            dkr_sum = dkr_sum + dkm_ref[:, 256 * h + 128:256 * h + 256]
            _, vjp = jax.vjp(_rms, q_ref[:, 256 * h:256 * h + 128], qg_ref[:, 0:128])
            dx, dg = vjp(dqm_ref[:, 256 * h:256 * h + 128])
            dq_ref[:, 256 * h:256 * h + 128] = dx.astype(BF16)
            dqg_n = dqg_n + dg
            _, vjp = jax.vjp(fr, q_ref[:, 256 * h + 128:256 * h + 256], qg_ref[:, 128:256])
            dx, dg = vjp(dqm_ref[:, 256 * h + 128:256 * h + 256])
            dq_ref[:, 256 * h + 128:256 * h + 256] = dx.astype(BF16)
            dqg_r = dqg_r + dg
        _, vjp = jax.vjp(lambda m, g: _mla_krope(m, g, cs, sn), m_ref[...], kg_ref[:, 128:256])
        dm, dkg_r = vjp(dkr_sum)
        dkr_ref[...] = dm
        dkv_ref[:, 1024:2048] = dv_ref[...].astype(BF16)
        _acc(dqg_ref.at[:, 0:128], dqg_n, i == 0)
        _acc(dqg_ref.at[:, 128:256], dqg_r, i == 0)
        _acc(dkg_ref.at[:, 0:128], dkg_n, i == 0)
        _acc(dkg_ref.at[:, 128:256], dkg_r, i == 0)

    return _rowcall(name, body, n,
                    [(kv, _rs(1024, 0)), (qp, _rs(2048)), (u, _rs(128, C_MISC // 128)), (qg, _ps((1, 256))), (kg, _ps((1, 256))),
                     (cos, _rs(128)), (sin, _rs(128)), (dkm, _rs(2048)), (dqm, _rs(2048)), (dv, _rs(1024))],
                    [(_sd((n, 2048), BF16), _rs(2048)), (_sd((n, 2048), BF16), _rs(2048)), (_sd((n, 128), F32), _rs(128)),
                     (_sd((1, 256), F32), _ps((1, 256))), (_sd((1, 256), F32), _ps((1, 256)))])


def misc_combine(name, dkr, dm_f, dm_b, drow_t):
    n = dkr.shape[0]

    def body(a_ref, f_ref, b_ref, r_ref, o_ref):
        o_ref[...] = (a_ref[...] + f_ref[0] + f_ref[1] + b_ref[0] + b_ref[1] + r_ref[...]).astype(BF16)

    g2 = pl.BlockSpec((2, RT, 128), lambda i: (0, i, 0))
    return _rowcall(name, body, n, [(dkr, _rs(128)), (dm_f, g2), (dm_b, g2), (drow_t, _rs(128))],
                    [(_sd((n, 128), BF16), _rs(128))])[0]


def _merge(g1, g2, g3, p1, p2, p3):
    return jax.nn.sigmoid(g1) * p1 + jax.nn.sigmoid(g2) * p2 + jax.nn.sigmoid(g3) * p3


def merge_fwd(name, u, p1, p2, p3):
    n = u.shape[0]

    def body(g1, g2, g3, a, b, c, o_ref):
        o_ref[...] = _merge(g1[...], g2[...], g3[...], a[...], b[...], c[...]).astype(BF16)

    return _rowcall(name, body, n, [(u, _rs(D, 0)), (u, _rs(D, 1)), (u, _rs(D, 2)), (p1, _rs(D)), (p2, _rs(D)), (p3, _rs(D))],
                    [(_sd((n, D), BF16), _rs(D))])[0]


def merge_bwd(name, u, p1, p2, p3, dm):
    n = u.shape[0]

    def body(g1, g2, g3, a, b, c, dm_ref, d1, d2, d3, dg_ref):
        _, vjp = jax.vjp(_merge, g1[...], g2[...], g3[...], a[...], b[...], c[...])
        r = vjp(dm_ref[...])
        for k in range(3):
            dg_ref[:, D * k:D * k + D] = r[k].astype(BF16)
        d1[...] = r[3].astype(BF16)
        d2[...] = r[4].astype(BF16)
        d3[...] = r[5].astype(BF16)

    return _rowcall(name, body, n,
                    [(u, _rs(D, 0)), (u, _rs(D, 1)), (u, _rs(D, 2)), (p1, _rs(D)), (p2, _rs(D)), (p3, _rs(D)), (dm, _rs(D))],
                    [(_sd((n, D), BF16), _rs(D))] * 3 + [(_sd((n, 3 * D), BF16), _rs(3 * D))])


def _swiglu(g, u):
    return _silu(g) * u


def swiglu_fwd(name, gu):
    n = gu.shape[0]

    def body(g_ref, u_ref, o_ref):
        o_ref[...] = _swiglu(g_ref[...], u_ref[...]).astype(BF16)

    return _rowcall(name, body, n, [(gu, _rs(FFN, 0)), (gu, _rs(FFN, 1))], [(_sd((n, FFN), BF16), _rs(FFN))])[0]


def swiglu_bwd(name, gu, da):
    n = gu.shape[0]

    def body(g_ref, u_ref, da_ref, o_ref):
        _, vjp = jax.vjp(_swiglu, g_ref[...], u_ref[...])
        dg, du = vjp(da_ref[...])
        o_ref[:, 0:FFN] = dg.astype(BF16)
        o_ref[:, FFN:2 * FFN] = du.astype(BF16)

    return _rowcall(name, body, n, [(gu, _rs(FFN, 0)), (gu, _rs(FFN, 1)), (da, _rs(FFN))],
                    [(_sd((n, 2 * FFN), BF16), _rs(2 * FFN))])[0]


FLASH_ROWS = 256


def _fold_lanes(x, op):
    acc = x[:, 0:128]
    for b in range(1, x.shape[1] // 128):
        acc = op(acc, x[:, 128 * b:128 * b + 128])
    return acc


def _band_mask(tq, tk, i, kb):
    qp = i * tq + lax.broadcasted_iota(jnp.int32, (tq, tk), 0)
    kp = kb * tk + lax.broadcasted_iota(jnp.int32, (tq, tk), 1)
    return jnp.abs(qp - kp) <= SWA_WIN


def flash_fwd(name, qa, ka, va, *, w, vw, hq, grp, vcol0, scale, nlat, tq, tk, band, sink, ctx_q, prev=None):
    n = qa.shape[0]
    cblk = nlat // NCTX
    band = band and not ctx_q
    assert not band, "latent rows of a banded attention go through swa_fwd_lat"
    if ctx_q:
        tq = tk = NCTX
        grid = (hq, 1, 1)
        qmap = lambda h, i, kk: (cblk, h)
        kmap = lambda h, i, kk: (cblk, h // grp)
        vmap = lambda h, i, kk: (cblk, vcol0 + h // grp)
        omap = lambda h, i, kk: (cblk, h)
        lmap = lambda h, i, kk: (h, cblk, 0)
    else:
        nb = nlat // tk
        nk = 3 if band else nb
        grid = (hq, nlat // tq, nk)
        kb_of = (lambda i, kk: jnp.clip(i + kk - 1, 0, nb - 1)) if band else (lambda i, kk: kk)
        qmap = lambda h, i, kk: (i, h)
        kmap = lambda h, i, kk: (kb_of(i, kk), h // grp)
        vmap = lambda h, i, kk: (kb_of(i, kk), vcol0 + h // grp)
        omap = lambda h, i, kk: (i, h)
        lmap = lambda h, i, kk: (h, i, 0)
    nk = grid[2]
    extra = not ctx_q
    has_sink = sink is not None

    def body(*refs):
        refs = list(refs)
        q_ref, k_ref, v_ref = refs[:3]
        pos = 3
        if extra:
            ke_ref, ve_ref = refs[pos:pos + 2]
            pos += 2
        if has_sink:
            s_ref = refs[pos]
            pos += 1
        if prev is not None:
            pos += 2
        o_ref, l_ref, m_s, l_s, a_s = refs[pos:pos + 5]
        kk = pl.program_id(2)
        tr = min(tq, FLASH_ROWS)

        def step(kblk, vblk):
            for r in range(tq // tr):
                rows = slice(r * tr, (r + 1) * tr)
                s = _d(q_ref[rows, :], kblk, ((1,), (1,))) * (scale * LOG2E)
                m_prev = m_s[rows, :]
                m_new = jnp.maximum(m_prev, jnp.max(_fold_lanes(s, jnp.maximum), axis=1, keepdims=True))
                alpha = jnp.exp2(m_prev - m_new)
                p = jnp.exp2(s - m_new)
                l_s[rows, :] = alpha * l_s[rows, :] + _fold_lanes(p, jnp.add)
                a_s[rows, :] = alpha * a_s[rows, :] + _d(p, vblk, ((1,), (0,)))
                m_s[rows, :] = m_new

        @pl.when(kk == 0)
        def _():
            if has_sink:
                sv = jnp.max(s_ref[0], axis=1, keepdims=True) * LOG2E
                m_s[...] = jnp.zeros((tq, 1), F32) + sv
                l_s[...] = (lax.broadcasted_iota(jnp.int32, (tq, 128), 1) == 0).astype(F32)
            else:
                m_s[...] = jnp.full((tq, 1), NEG, F32)
                l_s[...] = jnp.zeros((tq, 128), F32)
            a_s[...] = jnp.zeros((tq, vw), F32)
            if extra:
                step(ke_ref[...], ve_ref[...])

        step(k_ref[...], v_ref[...])

        @pl.when(kk == nk - 1)
        def _():
            l = jnp.sum(l_s[...], axis=1, keepdims=True)
            o_ref[...] = (a_s[...] / l).astype(BF16)
            l_ref[0] = m_s[...] + jnp.log2(l)

    ins = [(qa, pl.BlockSpec((tq, w), qmap)), (ka, pl.BlockSpec((tk, w), kmap)), (va, pl.BlockSpec((tk, vw), vmap))]
    if extra:
        ins += [(ka, pl.BlockSpec((NCTX, w), lambda h, i, kk: (cblk, h // grp))),
                (va, pl.BlockSpec((NCTX, vw), lambda h, i, kk: (cblk, vcol0 + h // grp)))]
    if has_sink:
        ins += [(sink, pl.BlockSpec((1, 1, 128), lambda h, i, kk: (h, 0, 0)))]
    aliases = {}
    if prev is not None:
        any_spec = pl.BlockSpec(memory_space=pl.ANY)
        aliases = {len(ins): 0, len(ins) + 1: 1}
        ins += [(prev[0], any_spec), (prev[1], any_spec)]
    return pl.pallas_call(
        body, out_shape=[_sd((n, hq * vw), BF16), _sd((hq, n, 1), F32)], grid=grid,
        in_specs=[s for _, s in ins],
        out_specs=[pl.BlockSpec((tq, vw), omap), pl.BlockSpec((1, tq, 1), lmap)],
        scratch_shapes=[pltpu.VMEM((tq, 1), F32), pltpu.VMEM((tq, 128), F32), pltpu.VMEM((tq, vw), F32)],
        input_output_aliases=aliases, name=name,
        compiler_params=_cp(("parallel", "parallel", "arbitrary")))(*[a for a, _ in ins])


def flash_dq(name, qa, ka, va, oa, doa, lse, *, w, vw, hq, grp, vcol0, scale, nlat, tq, tk, band, sink, ctx_q, prev=None):
    n = qa.shape[0]
    cblk = nlat // NCTX
    band = band and not ctx_q
    if ctx_q:
        tq = tk = NCTX
        grid = (hq, 1, 1)
        qmap = lambda h, i, kk: (cblk, h)
        kmap = lambda h, i, kk: (cblk, h // grp)
        vmap = lambda h, i, kk: (cblk, vcol0 + h // grp)
        lmap = lambda h, i, kk: (h, cblk, 0)
    else:
        nb = nlat // tk
        grid = (hq, nlat // tq, 3 if band else nb)
        kb_of = (lambda i, kk: jnp.clip(i + kk - 1, 0, nb - 1)) if band else (lambda i, kk: kk)
        qmap = lambda h, i, kk: (i, h)
        kmap = lambda h, i, kk: (kb_of(i, kk), h // grp)
        vmap = lambda h, i, kk: (kb_of(i, kk), vcol0 + h // grp)
        lmap = lambda h, i, kk: (h, i, 0)
    nk = grid[2]
    nq = grid[1]
    extra = not ctx_q
    has_sink = sink is not None

    def body(*refs):
        refs = list(refs)
        q_ref, k_ref, v_ref, o_ref, do_ref, l_ref = refs[:6]
        pos = 6
        if extra:
            ke_ref, ve_ref = refs[pos:pos + 2]
            pos += 2
        if has_sink:
            s_ref = refs[pos]
            pos += 1
        if prev is not None:
            pos += 2
        dq_ref, dl_ref, ds_ref, acc_s, dl_s = refs[pos:pos + 5]
        i = pl.program_id(1)
        kk = pl.program_id(2)
        q = q_ref[...]
        do = do_ref[...]
        lse_v = l_ref[0]

        def step(kblk, vblk, mask):
            s = _d(q, kblk, ((1,), (1,))) * (scale * LOG2E)
            if mask is not None:
                s = jnp.where(mask, s, NEG)
            p = jnp.exp2(s - lse_v)
            dp = _d(do, vblk, ((1,), (1,)))
            ds = p * (dp - dl_s[...]) * scale
            acc_s[...] += _d(ds, kblk, ((1,), (0,)))

        @pl.when(kk == 0)
        def _():
            delta = jnp.sum(do * o_ref[...].astype(F32), axis=1, keepdims=True)
            dl_s[...] = delta
            acc_s[...] = jnp.zeros((tq, w), F32)
            if has_sink:
                sv = jnp.max(s_ref[0], axis=1, keepdims=True) * LOG2E
                dsk = jnp.sum(-jnp.exp2(sv - lse_v) * delta, axis=0, keepdims=True)
                _acc(ds_ref, jnp.zeros((1, 1, 128), F32) + dsk, i == 0)
            else:
                ds_ref[...] = jnp.zeros((1, 1, 128), F32)
            if extra:
                step(ke_ref[...], ve_ref[...], None)

        if band:
            kb = i + kk - 1

            @pl.when((kb >= 0) & (kb < nlat // tk))
            def _():
                step(k_ref[...], v_ref[...], _band_mask(tq, tk, i, kb))
        else:
            step(k_ref[...], v_ref[...], None)

        @pl.when(kk == nk - 1)
        def _():
            dq_ref[...] = acc_s[...]
            dl_ref[0] = dl_s[...]

    ins = [(qa, pl.BlockSpec((tq, w), qmap)), (ka, pl.BlockSpec((tk, w), kmap)), (va, pl.BlockSpec((tk, vw), vmap)),
           (oa, pl.BlockSpec((tq, vw), qmap)), (doa, pl.BlockSpec((tq, vw), qmap)), (lse, pl.BlockSpec((1, tq, 1), lmap))]
    if extra:
        ins += [(ka, pl.BlockSpec((NCTX, w), lambda h, i, kk: (cblk, h // grp))),
                (va, pl.BlockSpec((NCTX, vw), lambda h, i, kk: (cblk, vcol0 + h // grp)))]
    if has_sink:
        ins += [(sink, pl.BlockSpec((1, 1, 128), lambda h, i, kk: (h, 0, 0)))]
    aliases = {}
    if prev is not None:
        any_spec = pl.BlockSpec(memory_space=pl.ANY)
        aliases = {len(ins): 0, len(ins) + 1: 1}
        ins += [(prev[0], any_spec), (prev[1], any_spec)]
    del nq
    return pl.pallas_call(
        body, out_shape=[_sd((n, hq * w), F32), _sd((hq, n, 1), F32), _sd((hq, 1, 128), F32)], grid=grid,
        in_specs=[s for _, s in ins],
        out_specs=[pl.BlockSpec((tq, w), qmap), pl.BlockSpec((1, tq, 1), lmap),
                   pl.BlockSpec((1, 1, 128), lambda h, i, kk: (h, 0, 0))],
        scratch_shapes=[pltpu.VMEM((tq, w), F32), pltpu.VMEM((tq, 1), F32)],
        input_output_aliases=aliases, name=name,
        compiler_params=_cp(("parallel", "arbitrary", "arbitrary")))(*[a for a, _ in ins])


def flash_dkv(name, qa, ka, va, doa, lse, delta, *, w, vw, hkv, grp, vcol0, scale, nlat, tq, tk, band, ctx_k, prev=None):
    n = qa.shape[0]
    cblk = nlat // NCTX
    nqb = nlat // tq
    band = band and not ctx_k
    if ctx_k:
        tk = NCTX
        nqs = nqb
        grid = (hkv, 1, grp * nqs)
        kmap = lambda hk, j, t: (cblk, hk)
        vmap = lambda hk, j, t: (cblk, vcol0 + hk)
        dvmap = lambda hk, j, t: (cblk, hk)
        qb_of = lambda j, t: t % nqs
    else:
        nqs = 3 if band else nqb
        grid = (hkv, nlat // tk, grp * nqs)
        kmap = lambda hk, j, t: (j, hk)
        vmap = lambda hk, j, t: (j, vcol0 + hk)
        dvmap = lambda hk, j, t: (j, hk)
        qb_of = (lambda j, t: jnp.clip(j + t % nqs - 1, 0, nqb - 1)) if band else (lambda j, t: t % nqs)
    qmap = lambda hk, j, t: (qb_of(j, t), hk * grp + t // nqs)
    lmap = lambda hk, j, t: (hk * grp + t // nqs, qb_of(j, t), 0)

    def body(*refs):
        refs = list(refs)
        q_ref, k_ref, v_ref, do_ref, l_ref, dl_ref = refs[:6]
        pos = 6
        if ctx_k:
            qe_ref, doe_ref, le_ref, dle_ref = refs[pos:pos + 4]
            pos += 4
        if prev is not None:
            pos += 2
        dk_ref, dv_ref = refs[pos:pos + 2]
        j = pl.program_id(1)
        t = pl.program_id(2)
        kblk = k_ref[...]
        vblk = v_ref[...]

        def contrib(q, do, lse_v, dl_v, mask):
            s = _d(q, kblk, ((1,), (1,))) * (scale * LOG2E)
            if mask is not None:
                s = jnp.where(mask, s, NEG)
            p = jnp.exp2(s - lse_v)
            dp = _d(do, vblk, ((1,), (1,)))
            ds = p * (dp - dl_v) * scale
            return _d(ds, q, ((0,), (0,))), _d(p, do, ((0,), (0,)))

        @pl.when(t == 0)
        def _():
            dk = jnp.zeros((tk, w), F32)
            dv = jnp.zeros((tk, vw), F32)
            if ctx_k:
                for gi in range(grp):
                    a, b = contrib(qe_ref[:, w * gi:w * gi + w], doe_ref[:, vw * gi:vw * gi + vw], le_ref[gi], dle_ref[gi], None)
                    dk = dk + a
                    dv = dv + b
            dk_ref[...] = dk
            dv_ref[...] = dv

        def add(mask):
            a, b = contrib(q_ref[...], do_ref[...], l_ref[0], dl_ref[0], mask)
            dk_ref[...] += a
            dv_ref[...] += b

        if band:
            qb = j + t % nqs - 1

            @pl.when((qb >= 0) & (qb < nqb))
            def _():
                add(_band_mask(tq, tk, qb, j))
        else:
            add(None)

    ins = [(qa, pl.BlockSpec((tq, w), qmap)), (ka, pl.BlockSpec((tk, w), kmap)), (va, pl.BlockSpec((tk, vw), vmap)),
           (doa, pl.BlockSpec((tq, vw), qmap)), (lse, pl.BlockSpec((1, tq, 1), lmap)), (delta, pl.BlockSpec((1, tq, 1), lmap))]
    if ctx_k:
        ins += [(qa, pl.BlockSpec((NCTX, grp * w), lambda hk, j, t: (cblk, hk))),
                (doa, pl.BlockSpec((NCTX, grp * vw), lambda hk, j, t: (cblk, hk))),
                (lse, pl.BlockSpec((grp, NCTX, 1), lambda hk, j, t: (hk, cblk, 0))),
                (delta, pl.BlockSpec((grp, NCTX, 1), lambda hk, j, t: (hk, cblk, 0)))]
    aliases = {}
    if prev is not None:
        any_spec = pl.BlockSpec(memory_space=pl.ANY)
        aliases = {len(ins): 0, len(ins) + 1: 1}
        ins += [(prev[0], any_spec), (prev[1], any_spec)]
    return pl.pallas_call(
        body, out_shape=[_sd((n, hkv * w), F32), _sd((n, hkv * vw), F32)], grid=grid,
        in_specs=[s for _, s in ins],
        out_specs=[pl.BlockSpec((tk, w), kmap), pl.BlockSpec((tk, vw), dvmap)],
        input_output_aliases=aliases, name=name,
        compiler_params=_cp(("parallel", "parallel", "arbitrary")))(*[a for a, _ in ins])


def mla_fwd(name, qm, km, vm, nlat):
    n = qm.shape[0]
    t = NCTX
    nlt = nlat // t
    c = (MLA_NOPE + MLA_ROPE) ** -0.5 * LOG2E

    def body(q_ref, k_ref, v_ref, o_ref, l_ref):
        i = pl.program_id(1)

        def run(k, v):
            s = _d(q_ref[...], k, ((1,), (1,))) * c
            m = jnp.max(_fold_lanes(s, jnp.maximum), axis=1, keepdims=True)
            p = jnp.exp2(s - m)
            l = jnp.sum(_fold_lanes(p, jnp.add), axis=1, keepdims=True)
            o_ref[...] = (_d(p, v, ((1,), (0,))) / l).astype(BF16)
            l_ref[0] = m + jnp.log2(l)

        @pl.when(i < nlt)
        def _():
            run(k_ref[...], v_ref[...])

        @pl.when(i == nlt)
        def _():
            run(k_ref[nlat:n, :], v_ref[nlat:n, :])

    return pl.pallas_call(
        body, out_shape=[_sd((n, MLA_H * 128), BF16), _sd((MLA_H, n, 1), F32)], grid=(MLA_H, n // t),
        in_specs=[pl.BlockSpec((t, 256), lambda h, i: (i, h)), pl.BlockSpec((n, 256), lambda h, i: (0, h)),
                  pl.BlockSpec((n, 128), lambda h, i: (0, h))],
        out_specs=[pl.BlockSpec((t, 128), lambda h, i: (i, h)), pl.BlockSpec((1, t, 1), lambda h, i: (h, i, 0))],
        name=name, compiler_params=_cp(("parallel", "arbitrary")))(qm, km, vm)


def mla_dq(name, qm, km, vm, o, do, lse, nlat):
    n = qm.shape[0]
    t = NCTX
    nlt = nlat // t
    scale = (MLA_NOPE + MLA_ROPE) ** -0.5

    def body(q_ref, k_ref, v_ref, o_ref, do_ref, l_ref, dq_ref, dl_ref):
        i = pl.program_id(1)
        do = do_ref[...]
        delta = jnp.sum(do.astype(F32) * o_ref[...].astype(F32), axis=1, keepdims=True)
        dl_ref[0] = delta

        def run(k, v):
            s = _d(q_ref[...], k, ((1,), (1,))) * (scale * LOG2E)
            ds = jnp.exp2(s - l_ref[0]) * (_d(do, v, ((1,), (1,))) - delta) * scale
            dq_ref[...] = _d(ds, k, ((1,), (0,)))

        @pl.when(i < nlt)
        def _():
            run(k_ref[...], v_ref[...])

        @pl.when(i == nlt)
        def _():
            run(k_ref[nlat:n, :], v_ref[nlat:n, :])

    qspec = pl.BlockSpec((t, 256), lambda h, i: (i, h))
    ospec = pl.BlockSpec((t, 128), lambda h, i: (i, h))
    lspec = pl.BlockSpec((1, t, 1), lambda h, i: (h, i, 0))
    return pl.pallas_call(
        body, out_shape=[_sd((n, MLA_H * 256), F32), _sd((MLA_H, n, 1), F32)], grid=(MLA_H, n // t),
        in_specs=[qspec, pl.BlockSpec((n, 256), lambda h, i: (0, h)), pl.BlockSpec((n, 128), lambda h, i: (0, h)),
                  ospec, ospec, lspec],
        out_specs=[qspec, lspec],
        name=name, compiler_params=_cp(("parallel", "arbitrary")))(qm, km, vm, o, do, lse)


def mla_dkv(name, qm, km, vm, do, lse_row, delta_row, nlat):
    n = qm.shape[0]
    t = NCTX
    nlt = nlat // t
    scale = (MLA_NOPE + MLA_ROPE) ** -0.5

    def body(q_ref, k_ref, v_ref, do_ref, l_ref, dl_ref, dk_ref, dv_ref):
        j = pl.program_id(1)

        def run(q, do, lrow, drow):
            st = _d(k_ref[...], q, ((1,), (1,))) * (scale * LOG2E)
            pt = jnp.exp2(st - lrow)
            dv_ref[...] = _d(pt, do, ((1,), (0,)))
            dst = pt * (_d(v_ref[...], do, ((1,), (1,))) - drow) * scale
            dk_ref[...] = _d(dst, q, ((1,), (0,)))

        @pl.when(j < nlt)
        def _():
            run(q_ref[0:nlat, :], do_ref[0:nlat, :], l_ref[0, :, 0:nlat], dl_ref[0, :, 0:nlat])

        @pl.when(j == nlt)
        def _():
            run(q_ref[...], do_ref[...], l_ref[0], dl_ref[0])

    rspec = pl.BlockSpec((1, 1, n), lambda h, j: (h, 0, 0))
    return pl.pallas_call(
        body, out_shape=[_sd((n, MLA_H * 256), F32), _sd((n, MLA_H * 128), F32)], grid=(MLA_H, n // t),
        in_specs=[pl.BlockSpec((n, 256), lambda h, j: (0, h)), pl.BlockSpec((t, 256), lambda h, j: (j, h)),
                  pl.BlockSpec((t, 128), lambda h, j: (j, h)), pl.BlockSpec((n, 128), lambda h, j: (0, h)), rspec, rspec],
        out_specs=[pl.BlockSpec((t, 256), lambda h, j: (j, h)), pl.BlockSpec((t, 128), lambda h, j: (j, h))],
        name=name, compiler_params=_cp(("parallel", "arbitrary")))(qm, km, vm, do, lse_row, delta_row)


def mla_attention_bwd(tag, qm, km, vm, o, do, lse, nlat):
    n = qm.shape[0]
    dq, delta = mla_dq(tag + "_dq", qm, km, vm, o, do, lse, nlat)
    dk, dv = mla_dkv(tag + "_dkv", qm, km, vm, do, lse.reshape(MLA_H, 1, n), delta.reshape(MLA_H, 1, n), nlat)
    return dq, dk, dv


SWA_T = 512


def _swa_window(t, nlat):
    t = min(t, nlat)
    return t, min(t + 2 * SWA_WIN, nlat)


def _win_start(i, t, wlen, nlat):
    return pl.multiple_of(jnp.clip(i * t - SWA_WIN, 0, nlat - wlen), 128)


def _win_mask(rows, cols, row0, col0):
    rp = row0 + lax.broadcasted_iota(jnp.int32, (rows, cols), 0)
    cp = col0 + lax.broadcasted_iota(jnp.int32, (rows, cols), 1)
    return jnp.abs(rp - cp) <= SWA_WIN


def swa_fwd_lat(name, qs, ks, u, sink, nlat):
    n = qs.shape[0]
    tq, wlen = _swa_window(SWA_T, nlat)
    grp = SWA_HQ // SWA_HKV
    scale = SWA_DH ** -0.5
    vcol0 = C_V // 128

    def body(q_ref, k_ref, v_ref, s_ref, o_ref, l_ref):
        i = pl.program_id(1)
        ws = _win_start(i, tq, wlen, nlat)
        q = q_ref[...]
        s1 = _d(q, k_ref[pl.ds(ws, wlen), :], ((1,), (1,))) * (scale * LOG2E)
        s1 = jnp.where(_win_mask(tq, wlen, i * tq, ws), s1, NEG)
        s2 = _d(q, k_ref[pl.ds(nlat, NCTX), :], ((1,), (1,))) * (scale * LOG2E)
        sv = jnp.max(s_ref[0], axis=1, keepdims=True) * LOG2E
        m = jnp.maximum(jnp.maximum(jnp.max(s1, axis=1, keepdims=True), jnp.max(s2, axis=1, keepdims=True)), sv)
        p1 = jnp.exp2(s1 - m)
        p2 = jnp.exp2(s2 - m)
        l = jnp.sum(p1, axis=1, keepdims=True) + jnp.sum(p2, axis=1, keepdims=True) + jnp.exp2(sv - m)
        acc = _d(p1, v_ref[pl.ds(ws, wlen), :], ((1,), (0,))) + _d(p2, v_ref[pl.ds(nlat, NCTX), :], ((1,), (0,)))
        o_ref[...] = (acc / l).astype(BF16)
        l_ref[0] = m + jnp.log2(l)

    return pl.pallas_call(
        body, out_shape=[_sd((n, SWA_HQ * 128), BF16), _sd((SWA_HQ, n, 1), F32)], grid=(SWA_HQ, nlat // tq),
        in_specs=[pl.BlockSpec((tq, 128), lambda h, i: (i, h)), pl.BlockSpec((n, 128), lambda h, i: (0, h // grp)),
                  pl.BlockSpec((n, 128), lambda h, i: (0, vcol0 + h // grp)), pl.BlockSpec((1, 1, 128), lambda h, i: (h, 0, 0))],
        out_specs=[pl.BlockSpec((tq, 128), lambda h, i: (i, h)), pl.BlockSpec((1, tq, 1), lambda h, i: (h, i, 0))],
        name=name, compiler_params=_cp(("parallel", "arbitrary")))(qs, ks, u, sink)


def swa_dq_lat(name, qs, ks, u, o, do, lse, sink, nlat):
    n = qs.shape[0]
    tq, wlen = _swa_window(SWA_T, nlat)
    grp = SWA_HQ // SWA_HKV
    scale = SWA_DH ** -0.5
    vcol0 = C_V // 128

    def body(q_ref, k_ref, v_ref, s_ref, o_ref, do_ref, l_ref, dq_ref, dl_ref, ds_ref):
        i = pl.program_id(1)
        ws = _win_start(i, tq, wlen, nlat)
        q = q_ref[...]
        do = do_ref[...]
        lse_v = l_ref[0]
        delta = jnp.sum(do.astype(F32) * o_ref[...].astype(F32), axis=1, keepdims=True)
        kw = k_ref[pl.ds(ws, wlen), :]
        kc = k_ref[pl.ds(nlat, NCTX), :]
        s1 = _d(q, kw, ((1,), (1,))) * (scale * LOG2E)
        s1 = jnp.where(_win_mask(tq, wlen, i * tq, ws), s1, NEG)
        s2 = _d(q, kc, ((1,), (1,))) * (scale * LOG2E)
        ds1 = jnp.exp2(s1 - lse_v) * (_d(do, v_ref[pl.ds(ws, wlen), :], ((1,), (1,))) - delta) * scale
        ds2 = jnp.exp2(s2 - lse_v) * (_d(do, v_ref[pl.ds(nlat, NCTX), :], ((1,), (1,))) - delta) * scale
        dq_ref[...] = _d(ds1, kw, ((1,), (0,))) + _d(ds2, kc, ((1,), (0,)))
        dl_ref[0] = delta
        sv = jnp.max(s_ref[0], axis=1, keepdims=True) * LOG2E
        dsk = jnp.sum(-jnp.exp2(sv - lse_v) * delta, axis=0, keepdims=True)
        _acc(ds_ref, jnp.zeros((1, 1, 128), F32) + dsk, i == 0)

    qspec = pl.BlockSpec((tq, 128), lambda h, i: (i, h))
    lspec = pl.BlockSpec((1, tq, 1), lambda h, i: (h, i, 0))
    return pl.pallas_call(
        body, out_shape=[_sd((n, SWA_HQ * 128), F32), _sd((SWA_HQ, n, 1), F32), _sd((SWA_HQ, 1, 128), F32)],
        grid=(SWA_HQ, nlat // tq),
        in_specs=[qspec, pl.BlockSpec((n, 128), lambda h, i: (0, h // grp)),
                  pl.BlockSpec((n, 128), lambda h, i: (0, vcol0 + h // grp)), pl.BlockSpec((1, 1, 128), lambda h, i: (h, 0, 0)),
                  qspec, qspec, lspec],
        out_specs=[qspec, lspec, pl.BlockSpec((1, 1, 128), lambda h, i: (h, 0, 0))],
        name=name, compiler_params=_cp(("parallel", "arbitrary")))(qs, ks, u, sink, o, do, lse)


def swa_dkv_lat(name, qs, ks, u, do, lse_row, delta_row, nlat):
    n = qs.shape[0]
    tk, wlen = _swa_window(SWA_T, nlat)
    grp = SWA_HQ // SWA_HKV
    scale = SWA_DH ** -0.5
    vcol0 = C_V // 128

    def body(q_ref, k_ref, v_ref, do_ref, l_ref, dl_ref, dk_ref, dv_ref):
        j = pl.program_id(1)
        ws = _win_start(j, tk, wlen, nlat)
        k = k_ref[...]
        v = v_ref[...]
        mask = _win_mask(tk, wlen, j * tk, ws)
        dk = jnp.zeros((tk, 128), F32)
        dv = jnp.zeros((tk, 128), F32)
        for gi in range(grp):
            qw = q_ref[pl.ds(ws, wlen), 128 * gi:128 * gi + 128]
            dow = do_ref[pl.ds(ws, wlen), 128 * gi:128 * gi + 128]
            st = jnp.where(mask, _d(k, qw, ((1,), (1,))) * (scale * LOG2E), NEG)
            pt = jnp.exp2(st - l_ref[gi, :, pl.ds(ws, wlen)])
            dv = dv + _d(pt, dow, ((1,), (0,)))
            dst = pt * (_d(v, dow, ((1,), (1,))) - dl_ref[gi, :, pl.ds(ws, wlen)]) * scale
            dk = dk + _d(dst, qw, ((1,), (0,)))
        dk_ref[...] = dk
        dv_ref[...] = dv

    rspec = pl.BlockSpec((grp, 1, n), lambda hk, j: (hk, 0, 0))
    return pl.pallas_call(
        body, out_shape=[_sd((n, SWA_HKV * 128), F32), _sd((n, SWA_HKV * 128), F32)], grid=(SWA_HKV, nlat // tk),
        in_specs=[pl.BlockSpec((n, grp * 128), lambda hk, j: (0, hk)), pl.BlockSpec((tk, 128), lambda hk, j: (j, hk)),
                  pl.BlockSpec((tk, 128), lambda hk, j: (j, vcol0 + hk)), pl.BlockSpec((n, grp * 128), lambda hk, j: (0, hk)),
                  rspec, rspec],
        out_specs=[pl.BlockSpec((tk, 128), lambda hk, j: (j, hk)), pl.BlockSpec((tk, 128), lambda hk, j: (j, hk))],
        name=name, compiler_params=_cp(("parallel", "arbitrary")))(qs, ks, u, do, lse_row, delta_row)


def swa_attention_fwd(tag, qs, ks, u, sink, cfg, nlat):
    o, lse = swa_fwd_lat(tag + "_fwd_lat", qs, ks, u, sink, nlat)
    return flash_fwd(tag + "_fwd_ctx", qs, ks, u, sink=sink, ctx_q=True, nlat=nlat, prev=(o, lse), **cfg)


def swa_attention_bwd(tag, qs, ks, u, o, do, lse, sink, cfg, nlat):
    n = qs.shape[0]
    dq, delta, ds1 = swa_dq_lat(tag + "_dq_lat", qs, ks, u, o, do, lse, sink, nlat)
    dq, delta, ds2 = flash_dq(tag + "_dq_ctx", qs, ks, u, o, do, lse, sink=sink, ctx_q=True, nlat=nlat, prev=(dq, delta), **cfg)
    dk, dv = swa_dkv_lat(tag + "_dkv_lat", qs, ks, u, do, lse.reshape(SWA_HQ, 1, n), delta.reshape(SWA_HQ, 1, n), nlat)
    kc = {k: v for k, v in cfg.items() if k != "hq"}
    kc["hkv"] = SWA_HKV
    kc["tq"] = min(1024, nlat)
    dk, dv = flash_dkv(tag + "_dkv_ctx", qs, ks, u, do, lse, delta, ctx_k=True, nlat=nlat, prev=(dk, dv), **kc)
    return dq, dk, dv, ds1 + ds2


def adamw(name, w, g, m, v):
    r, c = w.shape
    tr = _pick(r, (256, 128, 64, 32, 16, 8))
    bc1 = 1.0 - ADAM_B1 ** ADAM_STEP
    bc2 = 1.0 - ADAM_B2 ** ADAM_STEP

    def body(w_ref, g_ref, m_ref, v_ref, d_ref, nm_ref, nv_ref):
        gv = g_ref[...]
        nm = ADAM_B1 * m_ref[...] + (1.0 - ADAM_B1) * gv
        nv = ADAM_B2 * v_ref[...] + (1.0 - ADAM_B2) * (gv * gv)
        d_ref[...] = -ADAM_LR * ((nm / bc1) / (jnp.sqrt(nv / bc2) + ADAM_EPS) + ADAM_WD * w_ref[...])
        nm_ref[...] = nm
        nv_ref[...] = nv

    spec = pl.BlockSpec((tr, c), lambda i: (i, 0))
    return pl.pallas_call(body, out_shape=[_sd((r, c), F32)] * 3, grid=(r // tr,), in_specs=[spec] * 4, out_specs=[spec] * 3,
                          name=name, compiler_params=_cp(("parallel",)))(w, g, m, v)


def _coords():
    return lax.axis_index("x"), lax.axis_index("y"), lax.axis_index("c")


_ANY = pl.BlockSpec(memory_space=pl.ANY)


def gather_chips(name, a):
    r = a.shape[0]
    half = r // 2

    def body(a_ref, o_ref, ici_send, ici_recv, d2d_send, d2d_recv, loc_sem):
        x, y, c = _coords()
        me = 2 * x + y
        peers = [(1 - x, y), (x, 1 - y), (1 - x, 1 - y)]
        my_rows = pl.ds(c * half, half)
        sib_rows = pl.ds((1 - c) * half, half)
        mine = pltpu.make_async_copy(a_ref, o_ref.at[me], loc_sem)
        mine.start()
        sends = [pltpu.make_async_remote_copy(a_ref.at[my_rows], o_ref.at[me, my_rows], ici_send.at[k], ici_recv.at[k],
                                              device_id=(px, py, c), device_id_type=MESH)
                 for k, (px, py) in enumerate(peers)]
        for cp in sends:
            cp.start()
        passed = []
        for k, (px, py) in enumerate(peers):
            s = 2 * px + py
            pltpu.make_async_remote_copy(a_ref.at[my_rows], o_ref.at[s, my_rows], ici_send.at[k], ici_recv.at[k],
                                         device_id=(px, py, c), device_id_type=MESH).wait_recv()
            fw = pltpu.make_async_remote_copy(o_ref.at[s, my_rows], o_ref.at[s, my_rows], d2d_send.at[k], d2d_recv.at[k],
                                              device_id=(x, y, 1 - c), device_id_type=MESH)
            fw.start()
            passed.append(fw)
        for k, (px, py) in enumerate(peers):
            s = 2 * px + py
            pltpu.make_async_remote_copy(o_ref.at[s, sib_rows], o_ref.at[s, sib_rows], d2d_send.at[k], d2d_recv.at[k],
                                         device_id=(x, y, 1 - c), device_id_type=MESH).wait_recv()
        for cp in sends + passed:
            cp.wait_send()
        mine.wait()

    return pl.pallas_call(
        body, out_shape=_sd((4,) + a.shape, a.dtype), in_specs=[_ANY], out_specs=_ANY,
        scratch_shapes=[pltpu.SemaphoreType.DMA((3,)), pltpu.SemaphoreType.DMA((3,)), pltpu.SemaphoreType.DMA((3,)),
                        pltpu.SemaphoreType.DMA((3,)), pltpu.SemaphoreType.DMA],
        name=name, compiler_params=pltpu.CompilerParams(has_side_effects=True))(a)


def pair_split(name, a):
    k4, r, cdim = a.shape
    half = r // 2

    def body(a_ref, own_ref, got_ref, send_sem, recv_sem, loc_sem):
        x, y, c = _coords()
        my_rows = pl.ds(c * half, half)
        sib_rows = pl.ds((1 - c) * half, half)
        mine = pltpu.make_async_copy(a_ref.at[:, my_rows], own_ref, loc_sem)
        mine.start()
        cp = pltpu.make_async_remote_copy(a_ref.at[:, sib_rows], got_ref, send_sem, recv_sem,
                                          device_id=(x, y, 1 - c), device_id_type=MESH)
        cp.start()
        cp.wait()
        mine.wait()

    return pl.pallas_call(
        body, out_shape=[_sd((k4, half, cdim), a.dtype), _sd((k4, half, cdim), a.dtype)], in_specs=[_ANY],
        out_specs=[_ANY, _ANY],
        scratch_shapes=[pltpu.SemaphoreType.DMA, pltpu.SemaphoreType.DMA, pltpu.SemaphoreType.DMA],
        name=name, compiler_params=pltpu.CompilerParams(has_side_effects=True))(a)


def scatter_chips(name, a):
    def body(a_ref, o_ref, send_sems, recv_sems, loc_sem):
        x, y, c = _coords()
        me = 2 * x + y
        peers = [(1 - x, y), (x, 1 - y), (1 - x, 1 - y)]
        mine = pltpu.make_async_copy(a_ref.at[me], o_ref.at[me], loc_sem)
        mine.start()
        sends = [pltpu.make_async_remote_copy(a_ref.at[2 * px + py], o_ref.at[me], send_sems.at[k], recv_sems.at[k],
                                              device_id=(px, py, c), device_id_type=MESH)
                 for k, (px, py) in enumerate(peers)]
        for cp in sends:
            cp.start()
        for k, (px, py) in enumerate(peers):
            pltpu.make_async_remote_copy(a_ref.at[me], o_ref.at[2 * px + py], send_sems.at[k], recv_sems.at[k],
                                         device_id=(px, py, c), device_id_type=MESH).wait_recv()
        for cp in sends:
            cp.wait_send()
        mine.wait()

    return pl.pallas_call(
        body, out_shape=_sd(a.shape, a.dtype), in_specs=[_ANY], out_specs=_ANY,
        scratch_shapes=[pltpu.SemaphoreType.DMA((3,)), pltpu.SemaphoreType.DMA((3,)), pltpu.SemaphoreType.DMA],
        name=name, compiler_params=pltpu.CompilerParams(has_side_effects=True))(a)


def pair_join(name, a):
    half, cdim = a.shape

    def body(a_ref, o_ref, send_sem, recv_sem, loc_sem):
        x, y, c = _coords()
        my_rows = pl.ds(c * half, half)
        sib_rows = pl.ds((1 - c) * half, half)
        mine = pltpu.make_async_copy(a_ref, o_ref.at[my_rows], loc_sem)
        mine.start()
        cp = pltpu.make_async_remote_copy(a_ref, o_ref.at[my_rows], send_sem, recv_sem, device_id=(x, y, 1 - c),
                                          device_id_type=MESH)
        cp.start()
        cp.wait_send()
        pltpu.make_async_remote_copy(a_ref, o_ref.at[sib_rows], send_sem, recv_sem, device_id=(x, y, 1 - c),
                                     device_id_type=MESH).wait_recv()
        mine.wait()

    return pl.pallas_call(
        body, out_shape=_sd((2 * half, cdim), a.dtype), in_specs=[_ANY], out_specs=_ANY,
        scratch_shapes=[pltpu.SemaphoreType.DMA, pltpu.SemaphoreType.DMA, pltpu.SemaphoreType.DMA],
        name=name, compiler_params=pltpu.CompilerParams(has_side_effects=True))(a)


def add_cast(name, a, b, dtype):
    k, r, c = a.shape
    tr = _pick(r, (1024, 512, 256, 128, 64, 32, 16, 8))

    def body(a_ref, b_ref, o_ref):
        o_ref[...] = (a_ref[...].astype(F32) + b_ref[...].astype(F32)).astype(dtype)

    spec = pl.BlockSpec((1, tr, c), lambda s, i: (s, i, 0))
    return pl.pallas_call(body, out_shape=_sd((k, r, c), dtype), grid=(k, r // tr), in_specs=[spec, spec], out_specs=spec,
                          name=name, compiler_params=_cp(("parallel", "parallel")))(a, b)


def gather_all(name, a):
    def body(a_ref, o_ref, send_sems, recv_sems, loc_sem):
        x, y, c = _coords()
        me = 4 * x + 2 * y + c
        flips = [(fx, fy, fc) for fx in (0, 1) for fy in (0, 1) for fc in (0, 1) if fx + fy + fc > 0]
        peers = [(x ^ fx, y ^ fy, c ^ fc) for fx, fy, fc in flips]
        mine = pltpu.make_async_copy(a_ref, o_ref.at[me], loc_sem)
        mine.start()
        sends = [pltpu.make_async_remote_copy(a_ref, o_ref.at[me], send_sems.at[k], recv_sems.at[k],
                                              device_id=p, device_id_type=MESH) for k, p in enumerate(peers)]
        for cp in sends:
            cp.start()
        for k, (px, py, pc) in enumerate(peers):
            pltpu.make_async_remote_copy(a_ref, o_ref.at[4 * px + 2 * py + pc], send_sems.at[k], recv_sems.at[k],
                                         device_id=(px, py, pc), device_id_type=MESH).wait_recv()
        for cp in sends:
            cp.wait_send()
        mine.wait()

    return pl.pallas_call(
        body, out_shape=_sd((8,) + a.shape, a.dtype), in_specs=[_ANY], out_specs=_ANY,
        scratch_shapes=[pltpu.SemaphoreType.DMA((7,)), pltpu.SemaphoreType.DMA((7,)), pltpu.SemaphoreType.DMA],
        name=name, compiler_params=pltpu.CompilerParams(has_side_effects=True))(a)


def sum_blocks(name, a):
    k, r, c = a.shape
    tr = _pick(r, (1024, 256, 128, 64, 32, 16, 8))

    def body(a_ref, o_ref):
        acc = a_ref[0].astype(F32)
        for s in range(1, k):
            acc = acc + a_ref[s].astype(F32)
        o_ref[...] = acc

    return pl.pallas_call(body, out_shape=_sd((r, c), F32), grid=(r // tr,),
                          in_specs=[pl.BlockSpec((k, tr, c), lambda i: (0, i, 0))], out_specs=pl.BlockSpec((tr, c), lambda i: (i, 0)),
                          name=name, compiler_params=_cp(("parallel",)))(a)


BIG = ("w_mod", "w_in", "w_mla_uq", "w_mla_ukv", "w_p_ssm", "w_p_swa", "w_p_mla", "w_out", "w_ffn_in", "w_ffn_out")
COL_SHARDED = ("w_mod", "w_in", "w_mla_uq", "w_mla_ukv", "w_ffn_in")
SMALL = ("c_ctx", "b_mod", "norm1_g", "norm2_g", "ssm_conv_w", "ssm_conv_b", "ssm_dt_bias", "ssm_a_log", "ssm_d",
         "ssm_norm_g", "swa_q_norm_g", "swa_k_norm_g", "swa_sink", "mla_q_lat_g", "mla_kv_lat_g", "mla_q_norm_g",
         "mla_k_norm_g")
WEIGHTS = ("c_ctx", "w_mod", "b_mod", "norm1_g", "norm2_g", "w_in", "ssm_conv_w", "ssm_conv_b", "ssm_dt_bias", "ssm_a_log",
           "ssm_d", "ssm_norm_g", "swa_q_norm_g", "swa_k_norm_g", "swa_sink", "mla_q_lat_g", "mla_kv_lat_g", "w_mla_uq",
           "w_mla_ukv", "mla_q_norm_g", "mla_k_norm_g", "w_p_ssm", "w_p_swa", "w_p_mla", "w_out", "w_ffn_in", "w_ffn_out")


def pack_w_in(w):
    z = lambda k: jnp.zeros((w.shape[0], k), w.dtype)
    return jnp.concatenate([w[:, 4832:7904], w[:, 2400:3424], w[:, 3424:4448], w[:, 0:1536], w[:, 1568:1824], w[:, 1824:2080],
                            w[:, 2080:2336], w[:, 2336:2400], w[:, 1536:1568], z(32), z(128), w[:, 4448:4832]], axis=1)


def unpack_w_in(g):
    return jnp.concatenate([g[:, 5120:6656], g[:, 7488:7520], g[:, 6656:6912], g[:, 6912:7168], g[:, 7168:7424], g[:, 7424:7488],
                            g[:, 3072:4096], g[:, 4096:5120], g[:, 7680:8064], g[:, 0:3072]], axis=1)


def pack_ukv(w):
    return w.reshape(MLA_KVRANK, MLA_H, 2, 128).transpose(0, 2, 1, 3).reshape(MLA_KVRANK, 2048)


def unpack_ukv(g):
    return g.reshape(MLA_KVRANK, 2, MLA_H, 128).transpose(0, 2, 1, 3).reshape(MLA_KVRANK, 2048)


def pack_uq(w):
    return jnp.pad(w.reshape(MLA_QRANK, MLA_H, 192), ((0, 0), (0, 0), (0, 64))).reshape(MLA_QRANK, 2048)


def unpack_uq(g):
    return g.reshape(MLA_QRANK, MLA_H, 256)[:, :, :192].reshape(MLA_QRANK, 1536)


def rope_tables(nlat):
    t = jnp.arange(nlat, dtype=jnp.int32)
    r = (t // GRID_W).astype(F32)[:, None]
    col = (t % GRID_W).astype(F32)[:, None]

    def tab(nf, pad):
        inv = jnp.power(ROPE_BASE, -jnp.arange(nf, dtype=F32) / nf)
        ar, ac = r * inv, col * inv
        cos = jnp.concatenate([jnp.cos(ar), jnp.cos(ar), jnp.cos(ac), jnp.cos(ac), jnp.ones((nlat, pad), F32)], axis=1)
        sin = jnp.concatenate([-jnp.sin(ar), jnp.sin(ar), -jnp.sin(ac), jnp.sin(ac), jnp.zeros((nlat, pad), F32)], axis=1)
        cos = jnp.concatenate([cos, jnp.ones((NCTX, 128), F32)], axis=0)
        sin = jnp.concatenate([sin, jnp.zeros((NCTX, 128), F32)], axis=0)
        return cos, sin

    return tab(32, 0), tab(16, 64)


def _lanes(v, start, width=128):
    return jnp.zeros((1, width), F32).at[0, start:start + v.shape[0]].set(v)


def layer_fwd(i, xin, h, mod, p, tabs, nlat):
    t = "l%d_" % i
    n = xin.shape[0]
    (cos_s, sin_s), (cos_m, sin_m) = tabs
    u = mm(h, p["w_in"], F32, t + "in_proj")
    xbc = conv_fwd(t + "conv", u, p["conv_w"], p["conv_b"], nlat)
    dtrow = jnp.transpose(u[:, C_MISC + DT_LANE:C_MISC + DT_LANE + 32])
    nlc = nlat // Q
    yf, hs_f = ssd_fwd(t + "ssd_f", xbc, u, dtrow, p["bias_c"], p["alog_c"], p["bias_r"], p["alog_r"], nlc, False, 0)
    yb, hs_b = ssd_fwd(t + "ssd_b", xbc, u, dtrow, p["bias_c"], p["alog_c"], p["bias_r"], p["alog_r"], nlc, True, 1)
    ys = ssd_out_fwd(t + "ssd_out", yf, yb, xbc, u, p["ssm_norm_g"], p["d_exp"])
    qs, ks = swa_prep_fwd(t + "swa_prep", u, p["swa_q_g"], p["swa_k_g"], cos_s, sin_s)
    o_swa, lse_swa = swa_attention_fwd(t + "swa", qs, ks, u, p["sink"], p["swa_cfg"], nlat)
    ckv_n, cq_n = lat_norm_fwd(t + "lat_norm", u, p["kv_lat_g"], p["q_lat_g"])
    kv = mm(ckv_n, p["w_ukv"], F32, t + "ukv")
    qp = mm(cq_n, p["w_uq"], F32, t + "uq")
    km, qm, vm = mla_prep_fwd(t + "mla_prep", kv, qp, u, p["mla_q_g"], p["mla_k_g"], cos_m, sin_m)
    o_mla, lse_mla = mla_fwd(t + "mla_fwd", qm, km, vm, nlat)
    p1 = mm(ys, p["w_p_ssm"], F32, t + "p_ssm")
    p2 = mm(o_swa, p["w_p_swa"], F32, t + "p_swa")
    p3 = mm(o_mla, p["w_p_mla"], F32, t + "p_mla")
    merged = merge_fwd(t + "merge", u, p1, p2, p3)
    o = mm(merged, p["w_out"], F32, t + "out_proj")
    x1, h2 = resid_mod_fwd(t + "res1", xin, o, mod, 2, mod, 3, 4, p["norm2_g"], nlat // RT)
    gu = mm(h2, p["w_ffn_in"], F32, t + "ffn_in")
    a = swiglu_fwd(t + "swiglu", gu)
    f = mm(a, p["w_ffn_out"], F32, t + "ffn_out")
    saved = dict(xin=xin, h=h, u=u, xbc=xbc, dtrow=dtrow, yf=yf, yb=yb, hs_f=hs_f, hs_b=hs_b, ys=ys, qs=qs, ks=ks,
                 o_swa=o_swa, lse_swa=lse_swa, ckv_n=ckv_n, cq_n=cq_n, kv=kv, qp=qp, km=km, qm=qm, vm=vm, o_mla=o_mla,
                 lse_mla=lse_mla, p1=p1, p2=p2, p3=p3, merged=merged, o=o, x1=x1, h2=h2, gu=gu, a=a, f=f)
    del n
    return x1, f, saved


def layer_bwd(i, dx2, df, dgt2, sv, mod, p, tabs, nlat):
    t = "l%db_" % i
    (cos_s, sin_s), (cos_m, sin_m) = tabs
    g = {}
    nt = nlat // RT
    nlc = nlat // Q
    g["w_ffn_out"] = mm_tn(sv["a"], df, t + "wg_ffn_out")
    da = mm(df, p["w_ffn_out"], F32, t + "dg_ffn_out", trans_b=True)
    dgu = swiglu_bwd(t + "swiglu", sv["gu"], da)
    g["w_ffn_in"] = mm_tn(sv["h2"], dgu, t + "wg_ffn_in")
    dh2 = mm(dgu, p["w_ffn_in"], F32, t + "dg_ffn_in", trans_b=True)
    dx1, do, dgt1, dsh2, dsc2, g["norm2_g"] = resid_mod_bwd(t + "res1", sv["x1"], dx2, dh2, sv["o"], mod, 2, mod, 3, 4,
                                                              p["norm2_g"], nt)
    g["w_out"] = mm_tn(sv["merged"], do, t + "wg_out")
    dmerged = mm(do, p["w_out"], F32, t + "dg_out", trans_b=True)
    dp1, dp2, dp3, dgates = merge_bwd(t + "merge", sv["u"], sv["p1"], sv["p2"], sv["p3"], dmerged)
    g["w_p_ssm"] = mm_tn(sv["ys"], dp1, t + "wg_p_ssm")
    g["w_p_swa"] = mm_tn(sv["o_swa"], dp2, t + "wg_p_swa")
    g["w_p_mla"] = mm_tn(sv["o_mla"], dp3, t + "wg_p_mla")
    dys = mm(dp1, p["w_p_ssm"], F32, t + "dg_p_ssm", trans_b=True)
    do_swa = mm(dp2, p["w_p_swa"], BF16, t + "dg_p_swa", trans_b=True)
    do_mla = mm(dp3, p["w_p_mla"], BF16, t + "dg_p_mla", trans_b=True)
    dqm, dkm, dv_mla = mla_attention_bwd(t + "mla", sv["qm"], sv["km"], sv["vm"], sv["o_mla"], do_mla, sv["lse_mla"], nlat)
    dkv, dqp, dkr, g["mla_q_g"], g["mla_k_g"] = mla_prep_bwd(t + "mla_prep", sv["kv"], sv["qp"], sv["u"], p["mla_q_g"],
                                                             p["mla_k_g"], cos_m, sin_m, dkm, dqm, dv_mla)
    g["w_ukv"] = mm_tn(sv["ckv_n"], dkv, t + "wg_ukv")
    g["w_uq"] = mm_tn(sv["cq_n"], dqp, t + "wg_uq")
    dckv_n = mm(dkv, p["w_ukv"], F32, t + "dg_ukv", trans_b=True)
    dcq_n = mm(dqp, p["w_uq"], F32, t + "dg_uq", trans_b=True)
    dckv, dcq, g["kv_lat_g"], g["q_lat_g"] = lat_norm_bwd(t + "lat_norm", sv["u"], p["kv_lat_g"], p["q_lat_g"], dckv_n, dcq_n)
    dqs, dks, dv_swa, g["sink"] = swa_attention_bwd(t + "swa", sv["qs"], sv["ks"], sv["u"], sv["o_swa"], do_swa, sv["lse_swa"],
                                                p["sink"], p["swa_cfg"], nlat)
    dq, dk, dv, g["swa_q_g"], g["swa_k_g"] = swa_prep_bwd(t + "swa_prep", sv["u"], p["swa_q_g"], p["swa_k_g"], cos_s, sin_s,
                                                          dqs, dks, dv_swa)
    dy, dxs_skip, dz, g["ssm_norm_g"], g["d_exp"] = ssd_out_bwd(t + "ssd_out", sv["yf"], sv["yb"], sv["xbc"], sv["u"],
                                                                 p["ssm_norm_g"], p["d_exp"], dys)
    n = dy.shape[0]
    zbc = jnp.zeros((n, 256), F32)
    r_f = ssd_bwd(t + "ssd_f", sv["xbc"], sv["u"], sv["dtrow"], p["bias_c"], p["alog_c"], p["bias_r"], p["alog_r"],
                  sv["hs_f"], dy, (dxs_skip, zbc, zbc), nlc, False, 0)
    r_b = ssd_bwd(t + "ssd_b", sv["xbc"], sv["u"], sv["dtrow"], p["bias_c"], p["alog_c"], p["bias_r"], p["alog_r"],
                  sv["hs_b"], dy, (r_f[0], r_f[1], r_f[2]), nlc, True, 1)
    dact = jnp.concatenate([r_b[0], r_b[1], r_b[2]], axis=1)
    dxbc, g["conv_w"], g["conv_b"] = conv_bwd(t + "conv", sv["u"], dact, p["conv_w"], p["conv_b"], nlat)
    drow = jnp.concatenate([r_f[4][0] + r_f[4][1], r_b[4][0] + r_b[4][1]], axis=0)
    drow_t = jnp.pad(jnp.transpose(drow), ((0, 0), (DT_LANE, 128 - DT_LANE - 32)))
    dmisc = misc_combine(t + "misc", dkr, r_f[3], r_b[3], drow_t)
    g["bias_c"] = r_f[5] + r_b[5]
    g["alog_c"] = r_f[6] + r_b[6]
    g["bias_r"] = jnp.concatenate([r_f[7], r_b[7]], axis=0)
    g["alog_r"] = jnp.concatenate([r_f[8], r_b[8]], axis=0)
    du = jnp.concatenate([dgates, dz, dq, dxbc, dk, dv, dckv, dmisc, jnp.zeros((n, 128), BF16), dcq], axis=1)
    g["w_in"] = mm_tn(sv["h"], du, t + "wg_in")
    dh = mm(du, p["w_in"], F32, t + "dg_in", trans_b=True)
    g["mod"] = (dgt1, dsh2, dsc2, dgt2)
    return dx1, dh, g


def local_step(x, c, ctx, target, c_ctx, W, nlat):
    xin = jnp.concatenate([x, ctx], axis=0)
    n = xin.shape[0]
    nt = nlat // RT
    tabs = rope_tables(nlat)
    c8 = jnp.zeros((8, D), F32).at[0].set(c[0]).at[1].set(c_ctx)
    mods, silus = [], []
    for i in range(DEPTH):
        m8, s8 = mod_fwd("l%d_mod" % i, c8, W[i]["w_mod"], W[i]["b_mod"])
        mods.append(m8[0:2].reshape(2, 1, 6 * D))
        silus.append(s8)
    saved = []
    _, h = resid_mod_fwd("l0_norm1", xin, None, None, 0, mods[0], 0, 1, W[0]["norm1_g"], nt)
    xcur = xin
    for i in range(DEPTH):
        x1, f, sv = layer_fwd(i, xcur, h, mods[i], W[i], tabs, nlat)
        saved.append(sv)
        if i + 1 < DEPTH:
            xcur, h = resid_mod_fwd("l%d_res2" % i, x1, f, mods[i], 5, mods[i + 1], 0, 1, W[i + 1]["norm1_g"], nt)
    loss_v, dx2, df, dgt2 = resid_loss("loss", x1, f, mods[DEPTH - 1], 5, target, nt)
    grads = [None] * DEPTH
    for i in reversed(range(DEPTH)):
        dx1, dh, g = layer_bwd(i, dx2, df, dgt2, saved[i], mods[i], W[i], tabs, nlat)
        if i > 0:
            sv = saved[i]
            dx2, df, dgt2, dsh1, dsc1, g["norm1_g"] = resid_mod_bwd(
                "l%db_res2" % (i - 1), sv["xin"], dx1, dh, saved[i - 1]["f"], mods[i - 1], 5, mods[i], 0, 1,
                W[i]["norm1_g"], nt)
        else:
            dxin, _, _, dsh1, dsc1, g["norm1_g"] = resid_mod_bwd("l0b_norm1", saved[0]["xin"], dx1, dh, None, None, 0,
                                                                  mods[0], 0, 1, W[0]["norm1_g"], nt)
        dgt1, dsh2, dsc2, dgt2_i = g.pop("mod")
        dmod = jnp.concatenate([dsh1, dsc1, dgt1, dsh2, dsc2, dgt2_i], axis=2).reshape(2, 6 * D)
        dmod8 = jnp.zeros((8, 6 * D), F32).at[0:2].set(dmod)
        g["w_mod"] = mm_tn(silus[i], dmod8, "l%db_wg_mod" % i)
        dsilu = mm(dmod8, W[i]["w_mod"], F32, "l%db_dg_mod" % i, trans_b=True)
        dc8, g["b_mod"] = mod_small_bwd("l%db_mod_small" % i, c8, dsilu, dmod8)
        g["c8"] = dc8
        grads[i] = g
    del n
    return loss_v[0, 0], dxin, grads


def _big_shapes():
    return dict(w_mod=(2, 1024, 1536), w_in=(2, 1024, 1976), w_mla_uq=(2, 384, 384), w_mla_ukv=(2, 256, 512),
                w_p_ssm=(2, 256, 1024), w_p_swa=(2, 256, 1024), w_p_mla=(2, 256, 1024), w_out=(2, 256, 1024),
                w_ffn_in=(2, 1024, 1408), w_ffn_out=(2, 704, 1024))


PACK_ROWS = 14336


def _pack_big(d, dtype):
    parts = [d[k].astype(dtype).reshape(-1, 1024) for k in BIG]
    used = sum(p.shape[0] for p in parts)
    return jnp.concatenate(parts + [jnp.zeros((PACK_ROWS - used, 1024), dtype)], axis=0)


def _unpack_big(buf, lead):
    out = {}
    r0 = 0
    for k in BIG:
        sh = _big_shapes()[k]
        rows = sh[0] * sh[1] * sh[2] // 1024
        out[k] = buf[..., r0:r0 + rows, :].reshape(lead + sh)
        r0 += rows
    return out


def _full_from_chips(k, a):
    if k in COL_SHARDED:
        return a.transpose(1, 2, 0, 3).reshape(2, a.shape[2], 4 * a.shape[3])
    return a.transpose(1, 0, 2, 3).reshape(2, 4 * a.shape[2], a.shape[3])


def _chips_from_full(k, a):
    if k in COL_SHARDED:
        return a.reshape(a.shape[0], 4, a.shape[1] // 4).transpose(1, 0, 2)
    return a.reshape(4, a.shape[0] // 4, a.shape[1])


def _small_sizes():
    return dict(c_ctx=1024, b_mod=2 * 6144, norm1_g=2048, norm2_g=2048, ssm_conv_w=2 * 5 * 1536, ssm_conv_b=2 * 1536,
                ssm_dt_bias=64, ssm_a_log=64, ssm_d=32, ssm_norm_g=2048, swa_q_norm_g=256, swa_k_norm_g=256, swa_sink=16,
                mla_q_lat_g=768, mla_kv_lat_g=512, mla_q_norm_g=384, mla_k_norm_g=384)


def _pack_small(d):
    parts = []
    for k in SMALL:
        v = d[k].astype(F32).reshape(-1)
        parts.append(jnp.pad(v, (0, (-v.shape[0]) % 1024)))
    return jnp.concatenate(parts).reshape(-1, 128)


def _unpack_small(buf, shapes):
    flat = buf.reshape(-1)
    out = {}
    o = 0
    for k in SMALL:
        sz = _small_sizes()[k]
        out[k] = flat[o:o + sz].reshape(shapes[k])
        o += sz + (-sz) % 1024
    return out


def big_grads(grads):
    gfull = {k: [] for k in BIG}
    for i in range(DEPTH):
        g = grads[i]
        gfull["w_mod"].append(g["w_mod"])
        gfull["w_in"].append(unpack_w_in(g["w_in"]))
        gfull["w_mla_uq"].append(unpack_uq(g["w_uq"]))
        gfull["w_mla_ukv"].append(unpack_ukv(g["w_ukv"]))
        for k in ("w_p_ssm", "w_p_swa", "w_p_mla", "w_out", "w_ffn_in", "w_ffn_out"):
            gfull[k].append(g[k])
    return gfull


def small_grads(grads):
    gs = {}
    gs["c_ctx"] = sum(grads[i]["c8"][1] for i in range(DEPTH))
    st = lambda f: jnp.stack([f(grads[i]) for i in range(DEPTH)])
    gs["b_mod"] = st(lambda g: g["b_mod"][0])
    gs["norm1_g"] = st(lambda g: g["norm1_g"][0])
    gs["norm2_g"] = st(lambda g: g["norm2_g"][0])
    gs["ssm_conv_w"] = st(lambda g: g["conv_w"])
    gs["ssm_conv_b"] = st(lambda g: g["conv_b"][0])
    gs["ssm_dt_bias"] = st(lambda g: (g["bias_c"][0, DT_LANE:DT_LANE + 32] + g["bias_r"][:, 0]).reshape(2, 16))
    gs["ssm_a_log"] = st(lambda g: (g["alog_c"][0, DT_LANE:DT_LANE + 32] + g["alog_r"][:, 0]).reshape(2, 16))
    gs["ssm_d"] = st(lambda g: g["d_exp"].reshape(16, 64).sum(axis=1))
    gs["ssm_norm_g"] = st(lambda g: g["ssm_norm_g"][0])
    gs["swa_q_norm_g"] = st(lambda g: g["swa_q_g"][0])
    gs["swa_k_norm_g"] = st(lambda g: g["swa_k_g"][0])
    gs["swa_sink"] = st(lambda g: g["sink"][:, 0, 0])
    gs["mla_q_lat_g"] = st(lambda g: g["q_lat_g"][0])
    gs["mla_kv_lat_g"] = st(lambda g: g["kv_lat_g"][0])
    gs["mla_q_norm_g"] = st(lambda g: g["mla_q_g"][0, :192])
    gs["mla_k_norm_g"] = st(lambda g: g["mla_k_g"][0, :192])
    return gs


def layer_params(i, full, conv_full, sm, nlat):
    p = {}
    p["w_mod"] = full["w_mod"][i]
    p["w_in"] = pack_w_in(full["w_in"][i])
    p["w_uq"] = pack_uq(full["w_mla_uq"][i])
    p["w_ukv"] = pack_ukv(full["w_mla_ukv"][i])
    for k in ("w_p_ssm", "w_p_swa", "w_p_mla", "w_out", "w_ffn_in", "w_ffn_out"):
        p[k] = full[k][i]
    p["b_mod"] = sm["b_mod"][i][None]
    p["norm1_g"] = sm["norm1_g"][i][None]
    p["norm2_g"] = sm["norm2_g"][i][None]
    p["conv_w"] = conv_full[i]
    p["conv_b"] = sm["ssm_conv_b"][i][None]
    bias = sm["ssm_dt_bias"][i].reshape(32)
    alog = sm["ssm_a_log"][i].reshape(32)
    p["bias_c"] = _lanes(bias, DT_LANE)
    p["alog_c"] = _lanes(alog, DT_LANE)
    p["bias_r"] = bias[:, None]
    p["alog_r"] = alog[:, None]
    p["d_exp"] = jnp.repeat(sm["ssm_d"][i], 64)[None]
    p["ssm_norm_g"] = sm["ssm_norm_g"][i][None]
    p["swa_q_g"] = sm["swa_q_norm_g"][i][None]
    p["swa_k_g"] = sm["swa_k_norm_g"][i][None]
    p["sink"] = jnp.broadcast_to(sm["swa_sink"][i][:, None, None], (SWA_HQ, 1, 128))
    p["q_lat_g"] = sm["mla_q_lat_g"][i][None]
    p["kv_lat_g"] = sm["mla_kv_lat_g"][i][None]
    p["mla_q_g"] = _lanes(sm["mla_q_norm_g"][i], 0, 256)
    p["mla_k_g"] = _lanes(sm["mla_k_norm_g"][i], 0, 256)
    p["swa_cfg"] = dict(w=128, vw=128, hq=SWA_HQ, grp=SWA_HQ // SWA_HKV, vcol0=C_V // 128, scale=SWA_DH ** -0.5,
                        tq=256, tk=256, band=True)
    return p


def kernel(x, c, ctx, c_ctx, w_mod, b_mod, norm1_g, norm2_g, w_in, ssm_conv_w, ssm_conv_b, ssm_dt_bias, ssm_a_log, ssm_d, ssm_norm_g, swa_q_norm_g, swa_k_norm_g, swa_sink, mla_q_lat_g, mla_kv_lat_g, w_mla_uq, w_mla_ukv, mla_q_norm_g, mla_k_norm_g, w_p_ssm, w_p_swa, w_p_mla, w_out, w_ffn_in, w_ffn_out, loss_target, m_c_ctx, m_w_mod, m_b_mod, m_norm1_g, m_norm2_g, m_w_in, m_ssm_conv_w, m_ssm_conv_b, m_ssm_dt_bias, m_ssm_a_log, m_ssm_d, m_ssm_norm_g, m_swa_q_norm_g, m_swa_k_norm_g, m_swa_sink, m_mla_q_lat_g, m_mla_kv_lat_g, m_w_mla_uq, m_w_mla_ukv, m_mla_q_norm_g, m_mla_k_norm_g, m_w_p_ssm, m_w_p_swa, m_w_p_mla, m_w_out, m_w_ffn_in, m_w_ffn_out, v_c_ctx, v_w_mod, v_b_mod, v_norm1_g, v_norm2_g, v_w_in, v_ssm_conv_w, v_ssm_conv_b, v_ssm_dt_bias, v_ssm_a_log, v_ssm_d, v_ssm_norm_g, v_swa_q_norm_g, v_swa_k_norm_g, v_swa_sink, v_mla_q_lat_g, v_mla_kv_lat_g, v_w_mla_uq, v_w_mla_ukv, v_mla_q_norm_g, v_mla_k_norm_g, v_w_p_ssm, v_w_p_swa, v_w_p_mla, v_w_out, v_w_ffn_in, v_w_ffn_out):
    loc = dict(locals())
    w = {k: loc[k] for k in WEIGHTS}
    m = {k: loc["m_" + k] for k in WEIGHTS}
    v = {k: loc["v_" + k] for k in WEIGHTS}
    nlat = x.shape[1]

    gathered = _unpack_big(gather_chips("gather_weights", _pack_big(w, BF16)), (4,))
    full = {k: _full_from_chips(k, gathered[k]) for k in BIG}
    conv_sh = jnp.pad(ssm_conv_w.reshape(10, 384), ((0, 6), (0, 0)))
    conv_full = gather_chips("gather_conv", conv_sh)[:, :10].reshape(4, 2, 5, 384).transpose(1, 2, 0, 3).reshape(2, 5, 1536)

    W = [layer_params(i, full, conv_full, w, nlat) for i in range(DEPTH)]

    loss_loc, dx, grads = local_step(x[0], c, ctx[0], loss_target[0], c_ctx, W, nlat)

    gfull = big_grads(grads)
    by_chip = {k: jnp.stack([_chips_from_full(k, a) for a in gfull[k]], axis=1) for k in BIG}
    parts = [by_chip[k].astype(BF16).reshape(4, -1, 1024) for k in BIG]
    used = sum(p.shape[1] for p in parts)
    send = jnp.concatenate(parts + [jnp.zeros((4, PACK_ROWS - used, 1024), BF16)], axis=1)
    own, got = pair_split("pair_split", send)
    pair = add_cast("pair_sum", own, got, BF16)
    recv = scatter_chips("scatter_grads", pair)
    mine = sum_blocks("sum_chips", recv)
    gbig = _unpack_big(pair_join("join_cores", mine), ())

    gs = small_grads(grads)
    small_all = gather_all("gather_small", _pack_small(gs))
    small_sum = sum_blocks("sum_small", small_all)
    full_shapes = {k: (w[k].shape if k != "ssm_conv_w" else (2, 5, 1536)) for k in SMALL}
    gsmall = _unpack_small(small_sum, full_shapes)
    chip = 2 * lax.axis_index("x") + lax.axis_index("y")
    gsmall["ssm_conv_w"] = lax.dynamic_slice_in_dim(gsmall["ssm_conv_w"], chip * 384, 384, axis=2)

    grad = {**gbig, **gsmall}
    delta, new_m, new_v = {}, {}, {}
    sm = {k: _pack_small_local(d) for k, d in (("w", w), ("g", grad), ("m", m), ("v", v))}
    r = adamw("adamw_small", sm["w"], sm["g"], sm["m"], sm["v"])
    shapes = {k: w[k].shape for k in SMALL}
    for dst, buf in zip((delta, new_m, new_v), r):
        dst.update(_unpack_small_local(buf, shapes))
    for k in BIG:
        sh = w[k].shape
        r = adamw("adamw_" + k, *[a[k].reshape(sh[0] * sh[1], sh[2]) for a in (w, grad, m, v)])
        for dst, buf in zip((delta, new_m, new_v), r):
            dst[k] = buf.reshape(sh)

    loss = lax.psum(loss_loc, ("x", "y", "c"))
    return (loss, dx[None, :nlat], *[grad[k] for k in WEIGHTS], *[delta[k] for k in WEIGHTS],
            *[new_m[k] for k in WEIGHTS], *[new_v[k] for k in WEIGHTS])


def _pack_small_local(d):
    parts = []
    for k in SMALL:
        a = d[k].astype(F32).reshape(-1)
        parts.append(jnp.pad(a, (0, (-a.shape[0]) % 1024)))
    return jnp.concatenate(parts).reshape(-1, 128)


def _unpack_small_local(buf, shapes):
    flat = buf.reshape(-1)
    out = {}
    o = 0
    for k in SMALL:
        sz = math.prod(shapes[k])
        out[k] = flat[o:o + sz].reshape(shapes[k])
        o += sz + (-sz) % 1024
    return out
```

```python
import functools
import math

import jax
import jax.numpy as jnp
from jax import lax
from jax.experimental import pallas as pl
from jax.experimental.pallas import tpu as pltpu

F32 = jnp.float32
BF16 = jnp.bfloat16
MESH = pl.DeviceIdType.MESH

D = 1024
NCTX = 256
EPS = 1e-6
ROPE_BASE = 10000.0
GRID_W = 64
DEPTH = 2
Q = 128
N_HEADS_SSM = 16
SWA_HQ, SWA_HKV, SWA_DH, SWA_WIN = 8, 2, 128, 128
MLA_H, MLA_NOPE, MLA_ROPE, MLA_V = 8, 128, 64, 128
MLA_QRANK, MLA_KVRANK = 384, 256
FFN = 2816
RT = 256
VMEM_LIMIT = 56 << 20
NEG = -1e30
LOG2E = 1.4426950408889634

C_G1, C_G2, C_G3, C_Z, C_Q, C_XS, C_B, C_C, C_K, C_V, C_CKV, C_MISC, C_PAD, C_CQ = (
    0, 1024, 2048, 3072, 4096, 5120, 6144, 6400, 6656, 6912, 7168, 7424, 7552, 7680)
UW = 8064
DT_LANE = 64

ADAM_LR, ADAM_B1, ADAM_B2, ADAM_EPS, ADAM_WD, ADAM_STEP = 0.001, 0.9, 0.999, 1e-08, 0.01, 10


def _cp(sem):
    return pltpu.CompilerParams(dimension_semantics=sem, vmem_limit_bytes=VMEM_LIMIT)


def _pick(n, cands):
    for c in cands:
        if n % c == 0:
            return c
    return n


_TN = (1536, 1408, 1152, 1024, 896, 768, 512, 384, 256, 128)


def mm(a, b, out_dtype, name, trans_b=False):
    m, k = a.shape
    n = b.shape[0] if trans_b else b.shape[1]
    tm = _pick(m, (768, 512, 256, 128, 8))
    tn = _pick(n, _TN)
    tk = k if k <= 2048 else _pick(k, (1408, 1152, 1024, 896, 768, 512))
    nk = k // tk
    b_spec = (pl.BlockSpec((tn, tk), lambda i, j, kk: (j, kk)) if trans_b
              else pl.BlockSpec((tk, tn), lambda i, j, kk: (kk, j)))

    def body(a_ref, b_ref, o_ref, *acc):
        p = _d(a_ref[...], b_ref[...], ((1,), (1 if trans_b else 0,)))
        if nk == 1:
            o_ref[...] = p.astype(out_dtype)
        else:
            kk = pl.program_id(2)

            @pl.when(kk == 0)
            def _():
                acc[0][...] = p

            @pl.when(kk > 0)
            def _():
                acc[0][...] += p

            @pl.when(kk == nk - 1)
            def _():
                o_ref[...] = acc[0][...].astype(out_dtype)

    return pl.pallas_call(
        body, out_shape=jax.ShapeDtypeStruct((m, n), out_dtype), grid=(m // tm, n // tn, nk),
        in_specs=[pl.BlockSpec((tm, tk), lambda i, j, kk: (i, kk)), b_spec],
        out_specs=pl.BlockSpec((tm, tn), lambda i, j, kk: (i, j)),
        scratch_shapes=[] if nk == 1 else [pltpu.VMEM((tm, tn), F32)],
        name=name, compiler_params=_cp(("parallel", "parallel", "arbitrary")))(a, b)


def mm_tn(a, b, name, out_dtype=BF16):
    t, ka = a.shape
    _, nb = b.shape
    ta = _pick(ka, (1024, 1408, 768, 512, 384, 256, 128))
    tb = _pick(nb, _TN)
    tt = _pick(t, (768, 512, 256, 128, 8))
    nt = t // tt

    def body(a_ref, b_ref, o_ref, acc):
        p = _d(a_ref[...], b_ref[...], ((0,), (0,)))
        s = pl.program_id(2)

        @pl.when(s == 0)
        def _():
            acc[...] = p

        @pl.when(s > 0)
        def _():
            acc[...] += p

        @pl.when(s == nt - 1)
        def _():
            o_ref[...] = acc[...].astype(out_dtype)

    return pl.pallas_call(
        body, out_shape=jax.ShapeDtypeStruct((ka, nb), out_dtype), grid=(ka // ta, nb // tb, nt),
        in_specs=[pl.BlockSpec((tt, ta), lambda i, j, s: (s, i)), pl.BlockSpec((tt, tb), lambda i, j, s: (s, j))],
        out_specs=pl.BlockSpec((ta, tb), lambda i, j, s: (i, j)), scratch_shapes=[pltpu.VMEM((ta, tb), F32)],
        name=name, compiler_params=_cp(("parallel", "parallel", "arbitrary")))(a, b)


def _rms(x, g, n=None):
    n = x.shape[-1] if n is None else n
    r = lax.rsqrt(jnp.sum(x * x, axis=-1, keepdims=True) * (1.0 / n) + EPS)
    return x * r * g


def _silu(x):
    return x * jax.nn.sigmoid(x)


def _modulate(x, g, sc, sh):
    return _rms(x, g) * (1.0 + sc) + sh


def _swap(x, s):
    ax = x.ndim - 1
    w = x.shape[ax]
    lane = lax.broadcasted_iota(jnp.int32, x.shape, ax)
    lo = (lane & s) == 0
    return jnp.where(lo, pltpu.roll(x, w - s, ax), pltpu.roll(x, s, ax))


@functools.partial(jax.custom_vjp, nondiff_argnums=(3,))
def _rope(x, cos, sin, s):
    return x * cos + _swap(x, s) * sin


def _rope_fwd(x, cos, sin, s):
    return _rope(x, cos, sin, s), (cos, sin)


def _rope_bwd(s, res, g):
    cos, sin = res
    return g * cos - _swap(g, s) * sin, jnp.zeros_like(cos), jnp.zeros_like(sin)


_rope.defvjp(_rope_fwd, _rope_bwd)


@jax.custom_vjp
def _softplus(x):
    return jnp.maximum(x, 0.0) + jnp.log(1.0 + jnp.exp(-jnp.abs(x)))


def _softplus_fwd(x):
    return _softplus(x), x


def _softplus_bwd(x, g):
    return (g * jax.nn.sigmoid(x),)


_softplus.defvjp(_softplus_fwd, _softplus_bwd)


def _d(a, b, dims):
    return lax.dot_general(a.astype(BF16), b.astype(BF16), (dims, ((), ())), preferred_element_type=F32)


@jax.custom_vjp
def bdot(a, b):
    return _d(a, b, ((1,), (0,)))


bdot.defvjp(lambda a, b: (bdot(a, b), (a, b)),
            lambda r, g: (_d(g, r[1], ((1,), (1,))), _d(r[0], g, ((0,), (0,)))))


@jax.custom_vjp
def bdot_nt(a, b):
    return _d(a, b, ((1,), (1,)))


bdot_nt.defvjp(lambda a, b: (bdot_nt(a, b), (a, b)),
               lambda r, g: (_d(g, r[1], ((1,), (0,))), _d(g, r[0], ((0,), (0,)))))


@jax.custom_vjp
def bdot_tn(a, b):
    return _d(a, b, ((0,), (0,)))


bdot_tn.defvjp(lambda a, b: (bdot_tn(a, b), (a, b)),
               lambda r, g: (_d(r[1], g, ((1,), (1,))), _d(r[0], g, ((1,), (0,)))))


def _tri(rev):
    i = lax.broadcasted_iota(jnp.int32, (Q, Q), 0)
    j = lax.broadcasted_iota(jnp.int32, (Q, Q), 1)
    return (i <= j) if rev else (i >= j)


def _split3(a):
    hi = a.astype(BF16)
    r = a - hi.astype(F32)
    mid = r.astype(BF16)
    lo = (r - mid.astype(F32)).astype(BF16)
    return hi, mid, lo


def _cum_cols_impl(a, rev):
    t = _tri(rev).astype(BF16)
    return sum(jnp.dot(t, p, preferred_element_type=F32) for p in _split3(a))


def _cum_rows_impl(a, rev):
    t = _tri(not rev).astype(BF16)
    return sum(jnp.dot(p, t, preferred_element_type=F32) for p in _split3(a))


@functools.partial(jax.custom_vjp, nondiff_argnums=(1,))
def cum_cols(a, rev):
    return _cum_cols_impl(a, rev)


cum_cols.defvjp(lambda a, rev: (_cum_cols_impl(a, rev), None), lambda rev, _, g: (_cum_cols_impl(g, not rev),))


@functools.partial(jax.custom_vjp, nondiff_argnums=(1,))
def cum_rows(a, rev):
    return _cum_rows_impl(a, rev)


cum_rows.defvjp(lambda a, rev: (_cum_rows_impl(a, rev), None), lambda rev, _, g: (_cum_rows_impl(g, not rev),))


def _rs(w, cb=0):
    return pl.BlockSpec((RT, w), lambda i: (i, cb))


def _ps(shape):
    nd = len(shape)
    return pl.BlockSpec(shape, lambda i: (0,) * nd)


def _gs(w, cb, nlat):
    return pl.BlockSpec((1, 1, w), lambda i: (i // nlat, 0, cb))


def _rowcall(name, body, n, ins, outs, scratch=()):
    return pl.pallas_call(
        body, out_shape=[o[0] for o in outs], grid=(n // RT,), in_specs=[s for _, s in ins],
        out_specs=[s for _, s in outs], scratch_shapes=list(scratch), name=name,
        compiler_params=_cp(("arbitrary",)))(*[a for a, _ in ins])


def _acc(ref, val, first):
    @pl.when(first)
    def _():
        ref[...] = val

    @pl.when(jnp.logical_not(first))
    def _():
        ref[...] += val


def _sd(shape, dt):
    return jax.ShapeDtypeStruct(shape, dt)


def resid_mod_fwd(name, xp, o, mod_gt, gt_i, mod_n, sh_i, sc_i, norm_g, nlat):
    n = xp.shape[0]
    has_res = o is not None

    def body(*refs):
        if has_res:
            xp_ref, o_ref, gt_ref, sh_ref, sc_ref, g_ref, xn_ref, h_ref = refs
            xn = xp_ref[...] + gt_ref[0] * o_ref[...]
            xn_ref[...] = xn
        else:
            xp_ref, sh_ref, sc_ref, g_ref, h_ref = refs
            xn = xp_ref[...]
        h_ref[...] = _modulate(xn, g_ref[...], sc_ref[0], sh_ref[0]).astype(BF16)

    ins = [(xp, _rs(D))]
    if has_res:
        ins += [(o, _rs(D)), (mod_gt, _gs(D, gt_i, nlat))]
    ins += [(mod_n, _gs(D, sh_i, nlat)), (mod_n, _gs(D, sc_i, nlat)), (norm_g, _ps((1, D)))]
    outs = ([(_sd((n, D), F32), _rs(D))] if has_res else []) + [(_sd((n, D), BF16), _rs(D))]
    r = _rowcall(name, body, n, ins, outs)
    return (r[0], r[1]) if has_res else (xp, r[0])


def resid_mod_bwd(name, xn, dxn, dh, o, mod_gt, gt_i, mod_n, sh_i, sc_i, norm_g, nlat):
    n = xn.shape[0]
    has_res = o is not None

    def body(*refs):
        i = pl.program_id(0)
        if has_res:
            (xn_ref, dxn_ref, dh_ref, o_ref, gt_ref, sh_ref, sc_ref, g_ref,
             dx_ref, do_ref, dgt_ref, dsh_ref, dsc_ref, dg_ref) = refs
        else:
            xn_ref, dxn_ref, dh_ref, sh_ref, sc_ref, g_ref, dx_ref, dsh_ref, dsc_ref, dg_ref = refs
        _, vjp = jax.vjp(_modulate, xn_ref[...], g_ref[...], sc_ref[0], sh_ref[0])
        dx, dg, dsc, dsh = vjp(dh_ref[...])
        dx = dx + dxn_ref[...]
        dx_ref[...] = dx
        gfirst = (i == 0) | (i == nlat)
        _acc(dg_ref, dg, i == 0)
        _acc(dsh_ref, dsh[None], gfirst)
        _acc(dsc_ref, dsc[None], gfirst)
        if has_res:
            do_ref[...] = (gt_ref[0] * dx).astype(BF16)
            _acc(dgt_ref, jnp.sum(dx * o_ref[...], axis=0, keepdims=True)[None], gfirst)

    ins = [(xn, _rs(D)), (dxn, _rs(D)), (dh, _rs(D))]
    if has_res:
        ins += [(o, _rs(D)), (mod_gt, _gs(D, gt_i, nlat))]
    ins += [(mod_n, _gs(D, sh_i, nlat)), (mod_n, _gs(D, sc_i, nlat)), (norm_g, _ps((1, D)))]
    gacc = (_sd((2, 1, D), F32), _gs(D, 0, nlat))
    outs = [(_sd((n, D), F32), _rs(D))]
    if has_res:
        outs += [(_sd((n, D), BF16), _rs(D)), gacc]
    outs += [gacc, gacc, (_sd((1, D), F32), _ps((1, D)))]
    r = _rowcall(name, body, n, ins, outs)
    if has_res:
        return r
    return r[0], None, None, r[1], r[2], r[3]


def resid_loss(name, xp, o, mod_gt, gt_i, target, nlat):
    n = xp.shape[0]

    def body(xp_ref, o_ref, gt_ref, t_ref, loss_ref, dx_ref, do_ref, dgt_ref):
        i = pl.program_id(0)
        gt = gt_ref[0]

        @pl.when(i < nlat)
        def _():
            err = xp_ref[...] + gt * o_ref[...] - t_ref[...]
            dx = err * (1.0 / D)
            dx_ref[...] = dx
            do_ref[...] = (gt * dx).astype(BF16)
            _acc(loss_ref, jnp.full((1, 128), 0.5 / D, F32) * jnp.sum(err * err), i == 0)
            _acc(dgt_ref, jnp.sum(dx * o_ref[...], axis=0, keepdims=True)[None], i == 0)

        @pl.when(i >= nlat)
        def _():
            dx_ref[...] = jnp.zeros((RT, D), F32)
            do_ref[...] = jnp.zeros((RT, D), BF16)
            dgt_ref[...] = jnp.zeros((1, 1, D), F32)

    tgt_spec = pl.BlockSpec((RT, D), lambda i: (jnp.minimum(i, nlat - 1), 0))
    ins = [(xp, _rs(D)), (o, _rs(D)), (mod_gt, _gs(D, gt_i, nlat)), (target, tgt_spec)]
    outs = [(_sd((1, 128), F32), _ps((1, 128))), (_sd((n, D), F32), _rs(D)), (_sd((n, D), BF16), _rs(D)),
            (_sd((2, 1, D), F32), _gs(D, 0, nlat))]
    return _rowcall(name, body, n, ins, outs)


def mod_fwd(name, c8, w_mod, b_mod):
    tn = 1536

    def body(c_ref, w_ref, b_ref, o_ref, s_ref):
        s = _silu(c_ref[...]).astype(BF16)
        s_ref[...] = s
        o_ref[...] = jnp.dot(s, w_ref[...], preferred_element_type=F32) + b_ref[...]

    return pl.pallas_call(
        body, out_shape=[_sd((8, 6 * D), F32), _sd((8, D), BF16)], grid=(6 * D // tn,),
        in_specs=[pl.BlockSpec((8, D), lambda j: (0, 0)), pl.BlockSpec((D, tn), lambda j: (0, j)),
                  pl.BlockSpec((1, tn), lambda j: (0, j))],
        out_specs=[pl.BlockSpec((8, tn), lambda j: (0, j)), pl.BlockSpec((8, D), lambda j: (0, 0))],
        name=name, compiler_params=_cp(("arbitrary",)))(c8, w_mod, b_mod)


def mod_small_bwd(name, c8, dsilu, dmod8):
    def body(c_ref, ds_ref, dm_ref, dc_ref, db_ref):
        _, vjp = jax.vjp(_silu, c_ref[...])
        dc_ref[...] = vjp(ds_ref[...])[0]
        db_ref[...] = jnp.sum(dm_ref[...], axis=0, keepdims=True)

    return pl.pallas_call(
        body, out_shape=[_sd((8, D), F32), _sd((1, 6 * D), F32)], grid=(1,),
        in_specs=[pl.BlockSpec((8, D), lambda j: (0, 0)), pl.BlockSpec((8, D), lambda j: (0, 0)),
                  pl.BlockSpec((8, 6 * D), lambda j: (0, 0))],
        out_specs=[pl.BlockSpec((8, D), lambda j: (0, 0)), pl.BlockSpec((1, 6 * D), lambda j: (0, 0))],
        name=name, compiler_params=_cp(("arbitrary",)))(c8, dsilu, dmod8)


def _conv_taps(x, nlat):
    n = x.shape[0]
    r = lax.broadcasted_iota(jnp.int32, x.shape, 0)
    lo = jnp.where(r < nlat, 0, nlat)
    hi = jnp.where(r < nlat, nlat, n)
    taps = []
    for o in (-2, -1, 0, 1, 2):
        xs = x if o == 0 else pltpu.roll(x, (-o) % n, 0)
        t = r + o
        taps.append(jnp.where((t >= lo) & (t < hi), xs, 0.0))
    return taps


def conv_fwd(name, u, w, b, nlat_rows):
    n = u.shape[0]

    def body(x_ref, w_ref, b_ref, o_ref):
        taps = _conv_taps(x_ref[...], nlat_rows)
        wv = w_ref[...]
        pre = b_ref[...] + sum(taps[k] * wv[k:k + 1, :] for k in range(5))
        o_ref[...] = _silu(pre)

    return pl.pallas_call(
        body, out_shape=_sd((n, 1536), F32), grid=(12,),
        in_specs=[pl.BlockSpec((n, 128), lambda j: (0, C_XS // 128 + j)), pl.BlockSpec((5, 128), lambda j: (0, j)),
                  pl.BlockSpec((1, 128), lambda j: (0, j))],
        out_specs=pl.BlockSpec((n, 128), lambda j: (0, j)),
        name=name, compiler_params=_cp(("parallel",)))(u, w, b)


def conv_bwd(name, u, dact, w, b, nlat_rows):
    n = u.shape[0]

    def body(x_ref, da_ref, w_ref, b_ref, dx_ref, dw_ref, db_ref):
        taps = _conv_taps(x_ref[...], nlat_rows)
        wv = w_ref[...]
        pre = b_ref[...] + sum(taps[k] * wv[k:k + 1, :] for k in range(5))
        s = jax.nn.sigmoid(pre)
        dpre = da_ref[...] * (s * (1.0 + pre * (1.0 - s)))
        db_ref[...] = jnp.sum(dpre, axis=0, keepdims=True)
        rows = lax.broadcasted_iota(jnp.int32, (5, 128), 0)
        dw = jnp.zeros((5, 128), F32)
        for k in range(5):
            dw = dw + jnp.where(rows == k, jnp.sum(dpre * taps[k], axis=0, keepdims=True), 0.0)
        dw_ref[...] = dw
        r = lax.broadcasted_iota(jnp.int32, dpre.shape, 0)
        lo = jnp.where(r < nlat_rows, 0, nlat_rows)
        hi = jnp.where(r < nlat_rows, nlat_rows, n)
        dx = jnp.zeros_like(dpre)
        for k in range(5):
            o = k - 2
            ds = dpre if o == 0 else pltpu.roll(dpre, o % n, 0)
            t = r - o
            dx = dx + jnp.where((t >= lo) & (t < hi), ds, 0.0) * wv[k:k + 1, :]
        dx_ref[...] = dx.astype(BF16)

    return pl.pallas_call(
        body, out_shape=[_sd((n, 1536), BF16), _sd((5, 1536), F32), _sd((1, 1536), F32)], grid=(12,),
        in_specs=[pl.BlockSpec((n, 128), lambda j: (0, C_XS // 128 + j)), pl.BlockSpec((n, 128), lambda j: (0, j)),
                  pl.BlockSpec((5, 128), lambda j: (0, j)), pl.BlockSpec((1, 128), lambda j: (0, j))],
        out_specs=[pl.BlockSpec((n, 128), lambda j: (0, j)), pl.BlockSpec((5, 128), lambda j: (0, j)),
                   pl.BlockSpec((1, 128), lambda j: (0, j))],
        name=name, compiler_params=_cp(("parallel",)))(u, dact, w, b)


def _ssd_chunk(rev, dirn, g, x4, bm, cm, misc, dtrow, bias_c, alog_c, bias_r, alog_r, h4):
    dt_c = _softplus(misc + bias_c)
    a_c = dt_c * (-jnp.exp(alog_c))
    dt_r = _softplus(dtrow + bias_r)
    a_r = dt_r * (-jnp.exp(alog_r))
    cs_c = cum_cols(a_c, rev)
    cs_r = cum_rows(a_r, rev)
    tot_c = jnp.sum(a_c, axis=0, keepdims=True)
    cb = bdot_nt(cm, bm)
    tri = _tri(rev)
    lane = lax.broadcasted_iota(jnp.int32, (1, 128), 1)
    row16 = lax.broadcasted_iota(jnp.int32, (16, 1), 0)
    prow = lax.broadcasted_iota(jnp.int32, (128, 1), 0)
    ys, hs = [], []
    for p in range(4):
        ydiag = 0.0
        wst = 0.0
        eoff = 0.0
        hscale = 0.0
        for e in range(2):
            hg = 8 * g + 2 * p + e
            oh_c = (lane == DT_LANE + 16 * dirn + hg).astype(F32)
            dt_h = jnp.sum(dt_c * oh_c, axis=1, keepdims=True)
            cs_h = jnp.sum(cs_c * oh_c, axis=1, keepdims=True)
            tot_h = jnp.sum(tot_c * oh_c, axis=1, keepdims=True)
            csr_h = jnp.sum(cs_r * (row16 == hg).astype(F32), axis=0, keepdims=True)
            seg = jnp.exp(jnp.where(tri, cs_h - csr_h, -jnp.inf))
            hm = ((lane < 64) if e == 0 else (lane >= 64)).astype(F32)
            ydiag = ydiag + bdot(cb * seg, x4[p] * (dt_h * hm))
            wst = wst + (dt_h * jnp.exp(tot_h - cs_h)) * hm
            eoff = eoff + jnp.exp(cs_h) * hm
            hscale = hscale + jnp.exp(tot_h) * ((prow < 64) if e == 0 else (prow >= 64)).astype(F32)
        ys.append(ydiag + bdot_nt(cm, h4[p]) * eoff)
        hs.append(h4[p] * hscale + bdot_tn(x4[p] * wst, bm))
    return ys, hs


def _ssd_specs(nlat_chunks, rev, dirn, bwd):
    nc = nlat_chunks + 2

    def chunk(s):
        if bwd:
            s = nc - 1 - s
        return (nlat_chunks + 1 - s) if rev else (s + nlat_chunks) % nc

    def step(s):
        return (nc - 1 - s) if bwd else s

    return dict(
        x=pl.BlockSpec((Q, 512), lambda g, s: (chunk(s), g)),
        b=pl.BlockSpec((Q, 128), lambda g, s: (chunk(s), 8 + g)),
        c=pl.BlockSpec((Q, 128), lambda g, s: (chunk(s), 10 + g)),
        misc=pl.BlockSpec((Q, 128), lambda g, s: (chunk(s), C_MISC // 128)),
        dtrow=pl.BlockSpec((16, Q), lambda g, s: (dirn, chunk(s))),
        p_c=pl.BlockSpec((1, 128), lambda g, s: (0, 0)),
        p_r=pl.BlockSpec((16, 1), lambda g, s: (dirn, 0)),
        y=pl.BlockSpec((Q, 512), lambda g, s: (chunk(s), g)),
        hsave=pl.BlockSpec((1, 1, 512, 128), lambda g, s: (g, step(s), 0, 0)),
        bc_out=pl.BlockSpec((Q, 128), lambda g, s: (chunk(s), g)),
        misc_out=pl.BlockSpec((1, Q, 128), lambda g, s: (g, chunk(s), 0)),
        dtrow_out=pl.BlockSpec((1, 16, Q), lambda g, s: (g, 0, chunk(s))),
        pacc_c=pl.BlockSpec((1, 128), lambda g, s: (0, 0)),
        pacc_r=pl.BlockSpec((16, 1), lambda g, s: (0, 0)),
    )


def ssd_fwd(name, xbc, u, dtrow, bias_c, alog_c, bias_r, alog_r, nlat_chunks, rev, dirn):
    n = xbc.shape[0]
    nc = nlat_chunks + 2
    sp = _ssd_specs(nlat_chunks, rev, dirn, False)

    def body(x_ref, b_ref, c_ref, m_ref, r_ref, bc_ref, ac_ref, br_ref, ar_ref, y_ref, hs_ref, h_s):
        g = pl.program_id(0)
        s = pl.program_id(1)

        @pl.when(s == 0)
        def _():
            h_s[...] = jnp.zeros((512, 128), F32)

        hs_ref[0, 0] = h_s[...]
        x4 = [x_ref[:, 128 * p:128 * p + 128] for p in range(4)]
        h4 = [h_s[128 * p:128 * p + 128, :] for p in range(4)]
        ys, hs = _ssd_chunk(rev, dirn, g, x4, b_ref[...], c_ref[...], m_ref[...], r_ref[...],
                            bc_ref[...], ac_ref[...], br_ref[...], ar_ref[...], h4)
        for p in range(4):
            y_ref[:, 128 * p:128 * p + 128] = ys[p]
            h_s[128 * p:128 * p + 128, :] = hs[p]

    return pl.pallas_call(
        body, out_shape=[_sd((n, 1024), F32), _sd((2, nc, 512, 128), F32)], grid=(2, nc),
        in_specs=[sp["x"], sp["b"], sp["c"], sp["misc"], sp["dtrow"], sp["p_c"], sp["p_c"], sp["p_r"], sp["p_r"]],
        out_specs=[sp["y"], sp["hsave"]], scratch_shapes=[pltpu.VMEM((512, 128), F32)],
        name=name, compiler_params=_cp(("arbitrary", "arbitrary")))(
            xbc, xbc, xbc, u, dtrow, bias_c, alog_c, bias_r, alog_r)


def ssd_bwd(name, xbc, u, dtrow, bias_c, alog_c, bias_r, alog_r, hsave, dy, acc, nlat_chunks, rev, dirn):
    n = xbc.shape[0]
    sp = _ssd_specs(nlat_chunks, rev, dirn, True)

    def body(x_ref, b_ref, c_ref, m_ref, r_ref, bc_ref, ac_ref, br_ref, ar_ref, hs_ref, dy_ref, ax_ref, ab_ref, acc_ref,
             dx_ref, db_ref, dc_ref, dm_ref, dr_ref, dbc_ref, dac_ref, dbr_ref, dar_ref, dh_s):
        g = pl.program_id(0)
        s = pl.program_id(1)

        @pl.when(s == 0)
        def _():
            dh_s[...] = jnp.zeros((512, 128), F32)

        x4 = [x_ref[:, 128 * p:128 * p + 128] for p in range(4)]
        h4 = [hs_ref[0, 0, 128 * p:128 * p + 128, :] for p in range(4)]
        fn = functools.partial(_ssd_chunk, rev, dirn, g)
        _, vjp = jax.vjp(fn, x4, b_ref[...], c_ref[...], m_ref[...], r_ref[...],
                         bc_ref[...], ac_ref[...], br_ref[...], ar_ref[...], h4)
        dys = [dy_ref[:, 128 * p:128 * p + 128] for p in range(4)]
        dhs = [dh_s[128 * p:128 * p + 128, :] for p in range(4)]
        dx4, db, dc, dm, dr, dbc, dac, dbr, dar, dh4 = vjp((dys, dhs))
        for p in range(4):
            dx_ref[:, 128 * p:128 * p + 128] = dx4[p] + ax_ref[:, 128 * p:128 * p + 128]
            dh_s[128 * p:128 * p + 128, :] = dh4[p]
        db_ref[...] = db + ab_ref[...]
        dc_ref[...] = dc + acc_ref[...]
        dm_ref[0] = dm
        dr_ref[0] = dr
        first = (g == 0) & (s == 0)
        _acc(dbc_ref, dbc, first)
        _acc(dac_ref, dac, first)
        _acc(dbr_ref, dbr, first)
        _acc(dar_ref, dar, first)

    ax, ab, ac = acc
    return pl.pallas_call(
        body,
        out_shape=[_sd((n, 1024), F32), _sd((n, 256), F32), _sd((n, 256), F32), _sd((2, n, 128), F32),
                   _sd((2, 16, n), F32), _sd((1, 128), F32), _sd((1, 128), F32), _sd((16, 1), F32), _sd((16, 1), F32)],
        grid=(2, nlat_chunks + 2),
        in_specs=[sp["x"], sp["b"], sp["c"], sp["misc"], sp["dtrow"], sp["p_c"], sp["p_c"], sp["p_r"], sp["p_r"],
                  sp["hsave"], sp["y"], sp["y"], sp["bc_out"], sp["bc_out"]],
        out_specs=[sp["y"], sp["bc_out"], sp["bc_out"], sp["misc_out"], sp["dtrow_out"],
                   sp["pacc_c"], sp["pacc_c"], sp["pacc_r"], sp["pacc_r"]],
        scratch_shapes=[pltpu.VMEM((512, 128), F32)],
        name=name, compiler_params=_cp(("arbitrary", "arbitrary")))(
            xbc, xbc, xbc, u, dtrow, bias_c, alog_c, bias_r, alog_r, hsave, dy, ax, ab, ac)


def _ssd_out(yf, yb, xs, z, g, dexp):
    return _rms((yf + yb + dexp * xs) * _silu(z), g)


def ssd_out_fwd(name, yf, yb, xbc, u, g, dexp):
    n = yf.shape[0]

    def body(yf_ref, yb_ref, xs_ref, z_ref, g_ref, d_ref, o_ref):
        o_ref[...] = _ssd_out(yf_ref[...], yb_ref[...], xs_ref[...], z_ref[...], g_ref[...], d_ref[...]).astype(BF16)

    return _rowcall(name, body, n,
                    [(yf, _rs(D)), (yb, _rs(D)), (xbc, _rs(D, 0)), (u, _rs(D, C_Z // D)), (g, _ps((1, D))), (dexp, _ps((1, D)))],
                    [(_sd((n, D), BF16), _rs(D))])[0]


def ssd_out_bwd(name, yf, yb, xbc, u, g, dexp, dys):
    n = yf.shape[0]

    def body(yf_ref, yb_ref, xs_ref, z_ref, g_ref, d_ref, dys_ref, dy_ref, dxs_ref, dz_ref, dg_ref, dd_ref):
        i = pl.program_id(0)
        _, vjp = jax.vjp(_ssd_out, yf_ref[...], yb_ref[...], xs_ref[...], z_ref[...], g_ref[...], d_ref[...])
        dyf, _, dxs, dz, dg, dd = vjp(dys_ref[...])
        dy_ref[...] = dyf
        dxs_ref[...] = dxs
        dz_ref[...] = dz.astype(BF16)
        _acc(dg_ref, dg, i == 0)
        _acc(dd_ref, dd, i == 0)

    return _rowcall(name, body, n,
                    [(yf, _rs(D)), (yb, _rs(D)), (xbc, _rs(D, 0)), (u, _rs(D, C_Z // D)), (g, _ps((1, D))), (dexp, _ps((1, D))),
                     (dys, _rs(D))],
                    [(_sd((n, D), F32), _rs(D)), (_sd((n, D), F32), _rs(D)), (_sd((n, D), BF16), _rs(D)),
                     (_sd((1, D), F32), _ps((1, D))), (_sd((1, D), F32), _ps((1, D)))])


def _normrope(x, g, cos, sin, s, n=None):
    return _rope(_rms(x, g, n), cos, sin, s)


def swa_prep_fwd(name, u, gq, gk, cos, sin):
    n = u.shape[0]

    def body(q_ref, k_ref, gq_ref, gk_ref, cos_ref, sin_ref, qs_ref, ks_ref):
        cs, sn = cos_ref[...], sin_ref[...]
        for h in range(SWA_HQ):
            sl = slice(128 * h, 128 * h + 128)
            qs_ref[:, sl] = _normrope(q_ref[:, sl], gq_ref[...], cs, sn, 32).astype(BF16)
        for h in range(SWA_HKV):
            sl = slice(128 * h, 128 * h + 128)
            ks_ref[:, sl] = _normrope(k_ref[:, sl], gk_ref[...], cs, sn, 32).astype(BF16)

    return _rowcall(name, body, n,
                    [(u, _rs(1024, C_Q // 1024)), (u, _rs(256, C_K // 256)), (gq, _ps((1, 128))), (gk, _ps((1, 128))),
                     (cos, _rs(128)), (sin, _rs(128))],
                    [(_sd((n, 1024), BF16), _rs(1024)), (_sd((n, 256), BF16), _rs(256))])


def swa_prep_bwd(name, u, gq, gk, cos, sin, dqs, dks, dv):
    n = u.shape[0]

    def body(q_ref, k_ref, gq_ref, gk_ref, cos_ref, sin_ref, dqs_ref, dks_ref, dv_ref,
             dq_ref, dk_ref, dvo_ref, dgq_ref, dgk_ref):
        i = pl.program_id(0)
        cs, sn = cos_ref[...], sin_ref[...]
        fn = lambda x, g: _normrope(x, g, cs, sn, 32)
        dgq = jnp.zeros((1, 128), F32)
        dgk = jnp.zeros((1, 128), F32)
        for h in range(SWA_HQ):
            sl = slice(128 * h, 128 * h + 128)
            _, vjp = jax.vjp(fn, q_ref[:, sl], gq_ref[...])
            dx, dg = vjp(dqs_ref[:, sl])
            dq_ref[:, sl] = dx.astype(BF16)
            dgq = dgq + dg
        for h in range(SWA_HKV):
            sl = slice(128 * h, 128 * h + 128)
            _, vjp = jax.vjp(fn, k_ref[:, sl], gk_ref[...])
            dx, dg = vjp(dks_ref[:, sl])
            dk_ref[:, sl] = dx.astype(BF16)
            dgk = dgk + dg
        dvo_ref[...] = dv_ref[...].astype(BF16)
        _acc(dgq_ref, dgq, i == 0)
        _acc(dgk_ref, dgk, i == 0)

    return _rowcall(name, body, n,
                    [(u, _rs(1024, C_Q // 1024)), (u, _rs(256, C_K // 256)), (gq, _ps((1, 128))), (gk, _ps((1, 128))),
                     (cos, _rs(128)), (sin, _rs(128)), (dqs, _rs(1024)), (dks, _rs(256)), (dv, _rs(256))],
                    [(_sd((n, 1024), BF16), _rs(1024)), (_sd((n, 256), BF16), _rs(256)), (_sd((n, 256), BF16), _rs(256)),
                     (_sd((1, 128), F32), _ps((1, 128))), (_sd((1, 128), F32), _ps((1, 128)))])


def lat_norm_fwd(name, u, g_kv, g_q):
    n = u.shape[0]

    def body(ckv_ref, cq_ref, gkv_ref, gq_ref, okv_ref, oq_ref):
        okv_ref[...] = _rms(ckv_ref[...], gkv_ref[...]).astype(BF16)
        oq_ref[...] = _rms(cq_ref[...], gq_ref[...]).astype(BF16)

    return _rowcall(name, body, n,
                    [(u, _rs(256, C_CKV // 256)), (u, _rs(384, C_CQ // 384)), (g_kv, _ps((1, 256))), (g_q, _ps((1, 384)))],
                    [(_sd((n, 256), BF16), _rs(256)), (_sd((n, 384), BF16), _rs(384))])


def lat_norm_bwd(name, u, g_kv, g_q, dkvn, dqn):
    n = u.shape[0]

    def body(ckv_ref, cq_ref, gkv_ref, gq_ref, dkvn_ref, dqn_ref, dckv_ref, dcq_ref, dgkv_ref, dgq_ref):
        i = pl.program_id(0)
        _, vjp = jax.vjp(_rms, ckv_ref[...], gkv_ref[...])
        dx, dg = vjp(dkvn_ref[...])
        dckv_ref[...] = dx.astype(BF16)
        _acc(dgkv_ref, dg, i == 0)
        _, vjp = jax.vjp(_rms, cq_ref[...], gq_ref[...])
        dx, dg = vjp(dqn_ref[...])
        dcq_ref[...] = dx.astype(BF16)
        _acc(dgq_ref, dg, i == 0)

    return _rowcall(name, body, n,
                    [(u, _rs(256, C_CKV // 256)), (u, _rs(384, C_CQ // 384)), (g_kv, _ps((1, 256))), (g_q, _ps((1, 384))),
                     (dkvn, _rs(256)), (dqn, _rs(384))],
                    [(_sd((n, 256), BF16), _rs(256)), (_sd((n, 384), BF16), _rs(384)),
                     (_sd((1, 256), F32), _ps((1, 256))), (_sd((1, 384), F32), _ps((1, 384)))])


def _lane_lt64(x):
    return (lax.broadcasted_iota(jnp.int32, (1, 128), 1) < 64).astype(F32) * x


def _mla_krope(misc, g, cos, sin):
    return _normrope(_lane_lt64(misc), g, cos, sin, 16, MLA_ROPE)


def mla_prep_fwd(name, kv, qp, u, qg, kg, cos, sin):
    n = kv.shape[0]

    def body(kv_ref, v_ref, q_ref, m_ref, qg_ref, kg_ref, cos_ref, sin_ref, km_ref, qm_ref, vm_ref):
        cs, sn = cos_ref[...], sin_ref[...]
        vm_ref[...] = v_ref[...].astype(BF16)
        kr = _mla_krope(m_ref[...], kg_ref[:, 128:256], cs, sn).astype(BF16)
        for h in range(MLA_H):
            km_ref[:, 256 * h:256 * h + 128] = _rms(kv_ref[:, 128 * h:128 * h + 128], kg_ref[:, 0:128]).astype(BF16)
            km_ref[:, 256 * h + 128:256 * h + 256] = kr
            qm_ref[:, 256 * h:256 * h + 128] = _rms(q_ref[:, 256 * h:256 * h + 128], qg_ref[:, 0:128]).astype(BF16)
            qm_ref[:, 256 * h + 128:256 * h + 256] = _normrope(
                q_ref[:, 256 * h + 128:256 * h + 256], qg_ref[:, 128:256], cs, sn, 16, MLA_ROPE).astype(BF16)

    return _rowcall(name, body, n,
                    [(kv, _rs(1024, 0)), (kv, _rs(1024, 1)), (qp, _rs(2048)), (u, _rs(128, C_MISC // 128)), (qg, _ps((1, 256))),
                     (kg, _ps((1, 256))), (cos, _rs(128)), (sin, _rs(128))],
                    [(_sd((n, 2048), BF16), _rs(2048)), (_sd((n, 2048), BF16), _rs(2048)), (_sd((n, 1024), BF16), _rs(1024))])


def mla_prep_bwd(name, kv, qp, u, qg, kg, cos, sin, dkm, dqm, dv):
    n = kv.shape[0]

    def body(kv_ref, q_ref, m_ref, qg_ref, kg_ref, cos_ref, sin_ref, dkm_ref, dqm_ref, dv_ref,
             dkv_ref, dq_ref, dkr_ref, dqg_ref, dkg_ref):
        i = pl.program_id(0)
        cs, sn = cos_ref[...], sin_ref[...]
        fr = lambda x, g: _normrope(x, g, cs, sn, 16, MLA_ROPE)
        dkg_n = jnp.zeros((1, 128), F32)
        dqg_n = jnp.zeros((1, 128), F32)
        dqg_r = jnp.zeros((1, 128), F32)
        dkr_sum = jnp.zeros((RT, 128), F32)
        for h in range(MLA_H):
            _, vjp = jax.vjp(_rms, kv_ref[:, 128 * h:128 * h + 128], kg_ref[:, 0:128])
            dx, dg = vjp(dkm_ref[:, 256 * h:256 * h + 128])
            dkv_ref[:, 128 * h:128 * h + 128] = dx.astype(BF16)
            dkg_n = dkg_n + dg
            dkr_sum = dkr_sum + dkm_ref[:, 256 * h + 128:256 * h + 256]
            _, vjp = jax.vjp(_rms, q_ref[:, 256 * h:256 * h + 128], qg_ref[:, 0:128])
            dx, dg = vjp(dqm_ref[:, 256 * h:256 * h + 128])
            dq_ref[:, 256 * h:256 * h + 128] = dx.astype(BF16)
            dqg_n = dqg_n + dg
            _, vjp = jax.vjp(fr, q_ref[:, 256 * h + 128:256 * h + 256], qg_ref[:, 128:256])
            dx, dg = vjp(dqm_ref[:, 256 * h + 128:256 * h + 256])
            dq_ref[:, 256 * h + 128:256 * h + 256] = dx.astype(BF16)
            dqg_r = dqg_r + dg
        _, vjp = jax.vjp(lambda m, g: _mla_krope(m, g, cs, sn), m_ref[...], kg_ref[:, 128:256])
        dm, dkg_r = vjp(dkr_sum)
        dkr_ref[...] = dm
        dkv_ref[:, 1024:2048] = dv_ref[...].astype(BF16)
        _acc(dqg_ref.at[:, 0:128], dqg_n, i == 0)
        _acc(dqg_ref.at[:, 128:256], dqg_r, i == 0)
        _acc(dkg_ref.at[:, 0:128], dkg_n, i == 0)
        _acc(dkg_ref.at[:, 128:256], dkg_r, i == 0)

    return _rowcall(name, body, n,
                    [(kv, _rs(1024, 0)), (qp, _rs(2048)), (u, _rs(128, C_MISC // 128)), (qg, _ps((1, 256))), (kg, _ps((1, 256))),
                     (cos, _rs(128)), (sin, _rs(128)), (dkm, _rs(2048)), (dqm, _rs(2048)), (dv, _rs(1024))],
                    [(_sd((n, 2048), BF16), _rs(2048)), (_sd((n, 2048), BF16), _rs(2048)), (_sd((n, 128), F32), _rs(128)),
                     (_sd((1, 256), F32), _ps((1, 256))), (_sd((1, 256), F32), _ps((1, 256)))])


def misc_combine(name, dkr, dm_f, dm_b, drow_t):
    n = dkr.shape[0]

    def body(a_ref, f_ref, b_ref, r_ref, o_ref):
        o_ref[...] = (a_ref[...] + f_ref[0] + f_ref[1] + b_ref[0] + b_ref[1] + r_ref[...]).astype(BF16)

    g2 = pl.BlockSpec((2, RT, 128), lambda i: (0, i, 0))
    return _rowcall(name, body, n, [(dkr, _rs(128)), (dm_f, g2), (dm_b, g2), (drow_t, _rs(128))],
                    [(_sd((n, 128), BF16), _rs(128))])[0]


def _merge(g1, g2, g3, p1, p2, p3):
    return jax.nn.sigmoid(g1) * p1 + jax.nn.sigmoid(g2) * p2 + jax.nn.sigmoid(g3) * p3


def merge_fwd(name, u, p1, p2, p3):
    n = u.shape[0]

    def body(g1, g2, g3, a, b, c, o_ref):
        o_ref[...] = _merge(g1[...], g2[...], g3[...], a[...], b[...], c[...]).astype(BF16)

    return _rowcall(name, body, n, [(u, _rs(D, 0)), (u, _rs(D, 1)), (u, _rs(D, 2)), (p1, _rs(D)), (p2, _rs(D)), (p3, _rs(D))],
                    [(_sd((n, D), BF16), _rs(D))])[0]


def merge_bwd(name, u, p1, p2, p3, dm):
    n = u.shape[0]

    def body(g1, g2, g3, a, b, c, dm_ref, d1, d2, d3, dg_ref):
        _, vjp = jax.vjp(_merge, g1[...], g2[...], g3[...], a[...], b[...], c[...])
        r = vjp(dm_ref[...])
        for k in range(3):
            dg_ref[:, D * k:D * k + D] = r[k].astype(BF16)
        d1[...] = r[3].astype(BF16)
        d2[...] = r[4].astype(BF16)
        d3[...] = r[5].astype(BF16)

    return _rowcall(name, body, n,
                    [(u, _rs(D, 0)), (u, _rs(D, 1)), (u, _rs(D, 2)), (p1, _rs(D)), (p2, _rs(D)), (p3, _rs(D)), (dm, _rs(D))],
                    [(_sd((n, D), BF16), _rs(D))] * 3 + [(_sd((n, 3 * D), BF16), _rs(3 * D))])


def _swiglu(g, u):
    return _silu(g) * u


def swiglu_fwd(name, gu):
    n = gu.shape[0]

    def body(g_ref, u_ref, o_ref):
        o_ref[...] = _swiglu(g_ref[...], u_ref[...]).astype(BF16)

    return _rowcall(name, body, n, [(gu, _rs(FFN, 0)), (gu, _rs(FFN, 1))], [(_sd((n, FFN), BF16), _rs(FFN))])[0]


def swiglu_bwd(name, gu, da):
    n = gu.shape[0]

    def body(g_ref, u_ref, da_ref, o_ref):
        _, vjp = jax.vjp(_swiglu, g_ref[...], u_ref[...])
        dg, du = vjp(da_ref[...])
        o_ref[:, 0:FFN] = dg.astype(BF16)
        o_ref[:, FFN:2 * FFN] = du.astype(BF16)

    return _rowcall(name, body, n, [(gu, _rs(FFN, 0)), (gu, _rs(FFN, 1)), (da, _rs(FFN))],
                    [(_sd((n, 2 * FFN), BF16), _rs(2 * FFN))])[0]


FLASH_ROWS = 256


def _fold_lanes(x, op):
    acc = x[:, 0:128]
    for b in range(1, x.shape[1] // 128):
        acc = op(acc, x[:, 128 * b:128 * b + 128])
    return acc


def _band_mask(tq, tk, i, kb):
    qp = i * tq + lax.broadcasted_iota(jnp.int32, (tq, tk), 0)
    kp = kb * tk + lax.broadcasted_iota(jnp.int32, (tq, tk), 1)
    return jnp.abs(qp - kp) <= SWA_WIN


def flash_fwd(name, qa, ka, va, *, w, vw, hq, grp, vcol0, scale, nlat, tq, tk, band, sink, ctx_q, prev=None):
    n = qa.shape[0]
    cblk = nlat // NCTX
    band = band and not ctx_q
    assert not band, "latent rows of a banded attention go through swa_fwd_lat"
    if ctx_q:
        tq = tk = NCTX
        grid = (hq, 1, 1)
        qmap = lambda h, i, kk: (cblk, h)
        kmap = lambda h, i, kk: (cblk, h // grp)
        vmap = lambda h, i, kk: (cblk, vcol0 + h // grp)
        omap = lambda h, i, kk: (cblk, h)
        lmap = lambda h, i, kk: (h, cblk, 0)
    else:
        nb = nlat // tk
        nk = 3 if band else nb
        grid = (hq, nlat // tq, nk)
        kb_of = (lambda i, kk: jnp.clip(i + kk - 1, 0, nb - 1)) if band else (lambda i, kk: kk)
        qmap = lambda h, i, kk: (i, h)
        kmap = lambda h, i, kk: (kb_of(i, kk), h // grp)
        vmap = lambda h, i, kk: (kb_of(i, kk), vcol0 + h // grp)
        omap = lambda h, i, kk: (i, h)
        lmap = lambda h, i, kk: (h, i, 0)
    nk = grid[2]
    extra = not ctx_q
    has_sink = sink is not None

    def body(*refs):
        refs = list(refs)
        q_ref, k_ref, v_ref = refs[:3]
        pos = 3
        if extra:
            ke_ref, ve_ref = refs[pos:pos + 2]
            pos += 2
        if has_sink:
            s_ref = refs[pos]
            pos += 1
        if prev is not None:
            pos += 2
        o_ref, l_ref, m_s, l_s, a_s = refs[pos:pos + 5]
        kk = pl.program_id(2)
        tr = min(tq, FLASH_ROWS)

        def step(kblk, vblk):
            for r in range(tq // tr):
                rows = slice(r * tr, (r + 1) * tr)
                s = _d(q_ref[rows, :], kblk, ((1,), (1,))) * (scale * LOG2E)
                m_prev = m_s[rows, :]
                m_new = jnp.maximum(m_prev, jnp.max(_fold_lanes(s, jnp.maximum), axis=1, keepdims=True))
                alpha = jnp.exp2(m_prev - m_new)
                p = jnp.exp2(s - m_new)
                l_s[rows, :] = alpha * l_s[rows, :] + _fold_lanes(p, jnp.add)
                a_s[rows, :] = alpha * a_s[rows, :] + _d(p, vblk, ((1,), (0,)))
                m_s[rows, :] = m_new

        @pl.when(kk == 0)
        def _():
            if has_sink:
                sv = jnp.max(s_ref[0], axis=1, keepdims=True) * LOG2E
                m_s[...] = jnp.zeros((tq, 1), F32) + sv
                l_s[...] = (lax.broadcasted_iota(jnp.int32, (tq, 128), 1) == 0).astype(F32)
            else:
                m_s[...] = jnp.full((tq, 1), NEG, F32)
                l_s[...] = jnp.zeros((tq, 128), F32)
            a_s[...] = jnp.zeros((tq, vw), F32)
            if extra:
                step(ke_ref[...], ve_ref[...])

        step(k_ref[...], v_ref[...])

        @pl.when(kk == nk - 1)
        def _():
            l = jnp.sum(l_s[...], axis=1, keepdims=True)
            o_ref[...] = (a_s[...] / l).astype(BF16)
            l_ref[0] = m_s[...] + jnp.log2(l)

    ins = [(qa, pl.BlockSpec((tq, w), qmap)), (ka, pl.BlockSpec((tk, w), kmap)), (va, pl.BlockSpec((tk, vw), vmap))]
    if extra:
        ins += [(ka, pl.BlockSpec((NCTX, w), lambda h, i, kk: (cblk, h // grp))),
                (va, pl.BlockSpec((NCTX, vw), lambda h, i, kk: (cblk, vcol0 + h // grp)))]
    if has_sink:
        ins += [(sink, pl.BlockSpec((1, 1, 128), lambda h, i, kk: (h, 0, 0)))]
    aliases = {}
    if prev is not None:
        any_spec = pl.BlockSpec(memory_space=pl.ANY)
        aliases = {len(ins): 0, len(ins) + 1: 1}
        ins += [(prev[0], any_spec), (prev[1], any_spec)]
    return pl.pallas_call(
        body, out_shape=[_sd((n, hq * vw), BF16), _sd((hq, n, 1), F32)], grid=grid,
        in_specs=[s for _, s in ins],
        out_specs=[pl.BlockSpec((tq, vw), omap), pl.BlockSpec((1, tq, 1), lmap)],
        scratch_shapes=[pltpu.VMEM((tq, 1), F32), pltpu.VMEM((tq, 128), F32), pltpu.VMEM((tq, vw), F32)],
        input_output_aliases=aliases, name=name,
        compiler_params=_cp(("parallel", "parallel", "arbitrary")))(*[a for a, _ in ins])


def flash_dq(name, qa, ka, va, oa, doa, lse, *, w, vw, hq, grp, vcol0, scale, nlat, tq, tk, band, sink, ctx_q, prev=None):
    n = qa.shape[0]
    cblk = nlat // NCTX
    band = band and not ctx_q
    if ctx_q:
        tq = tk = NCTX
        grid = (hq, 1, 1)
        qmap = lambda h, i, kk: (cblk, h)
        kmap = lambda h, i, kk: (cblk, h // grp)
        vmap = lambda h, i, kk: (cblk, vcol0 + h // grp)
        lmap = lambda h, i, kk: (h, cblk, 0)
    else:
        nb = nlat // tk
        grid = (hq, nlat // tq, 3 if band else nb)
        kb_of = (lambda i, kk: jnp.clip(i + kk - 1, 0, nb - 1)) if band else (lambda i, kk: kk)
        qmap = lambda h, i, kk: (i, h)
        kmap = lambda h, i, kk: (kb_of(i, kk), h // grp)
        vmap = lambda h, i, kk: (kb_of(i, kk), vcol0 + h // grp)
        lmap = lambda h, i, kk: (h, i, 0)
    nk = grid[2]
    nq = grid[1]
    extra = not ctx_q
    has_sink = sink is not None

    def body(*refs):
        refs = list(refs)
        q_ref, k_ref, v_ref, o_ref, do_ref, l_ref = refs[:6]
        pos = 6
        if extra:
            ke_ref, ve_ref = refs[pos:pos + 2]
            pos += 2
        if has_sink:
            s_ref = refs[pos]
            pos += 1
        if prev is not None:
            pos += 2
        dq_ref, dl_ref, ds_ref, acc_s, dl_s = refs[pos:pos + 5]
        i = pl.program_id(1)
        kk = pl.program_id(2)
        q = q_ref[...]
        do = do_ref[...]
        lse_v = l_ref[0]

        def step(kblk, vblk, mask):
            s = _d(q, kblk, ((1,), (1,))) * (scale * LOG2E)
            if mask is not None:
                s = jnp.where(mask, s, NEG)
            p = jnp.exp2(s - lse_v)
            dp = _d(do, vblk, ((1,), (1,)))
            ds = p * (dp - dl_s[...]) * scale
            acc_s[...] += _d(ds, kblk, ((1,), (0,)))

        @pl.when(kk == 0)
        def _():
            delta = jnp.sum(do * o_ref[...].astype(F32), axis=1, keepdims=True)
            dl_s[...] = delta
            acc_s[...] = jnp.zeros((tq, w), F32)
            if has_sink:
                sv = jnp.max(s_ref[0], axis=1, keepdims=True) * LOG2E
                dsk = jnp.sum(-jnp.exp2(sv - lse_v) * delta, axis=0, keepdims=True)
                _acc(ds_ref, jnp.zeros((1, 1, 128), F32) + dsk, i == 0)
            else:
                ds_ref[...] = jnp.zeros((1, 1, 128), F32)
            if extra:
                step(ke_ref[...], ve_ref[...], None)

        if band:
            kb = i + kk - 1

            @pl.when((kb >= 0) & (kb < nlat // tk))
            def _():
                step(k_ref[...], v_ref[...], _band_mask(tq, tk, i, kb))
        else:
            step(k_ref[...], v_ref[...], None)

        @pl.when(kk == nk - 1)
        def _():
            dq_ref[...] = acc_s[...]
            dl_ref[0] = dl_s[...]

    ins = [(qa, pl.BlockSpec((tq, w), qmap)), (ka, pl.BlockSpec((tk, w), kmap)), (va, pl.BlockSpec((tk, vw), vmap)),
           (oa, pl.BlockSpec((tq, vw), qmap)), (doa, pl.BlockSpec((tq, vw), qmap)), (lse, pl.BlockSpec((1, tq, 1), lmap))]
    if extra:
        ins += [(ka, pl.BlockSpec((NCTX, w), lambda h, i, kk: (cblk, h // grp))),
                (va, pl.BlockSpec((NCTX, vw), lambda h, i, kk: (cblk, vcol0 + h // grp)))]
    if has_sink:
        ins += [(sink, pl.BlockSpec((1, 1, 128), lambda h, i, kk: (h, 0, 0)))]
    aliases = {}
    if prev is not None:
        any_spec = pl.BlockSpec(memory_space=pl.ANY)
        aliases = {len(ins): 0, len(ins) + 1: 1}
        ins += [(prev[0], any_spec), (prev[1], any_spec)]
    del nq
    return pl.pallas_call(
        body, out_shape=[_sd((n, hq * w), F32), _sd((hq, n, 1), F32), _sd((hq, 1, 128), F32)], grid=grid,
        in_specs=[s for _, s in ins],
        out_specs=[pl.BlockSpec((tq, w), qmap), pl.BlockSpec((1, tq, 1), lmap),
                   pl.BlockSpec((1, 1, 128), lambda h, i, kk: (h, 0, 0))],
        scratch_shapes=[pltpu.VMEM((tq, w), F32), pltpu.VMEM((tq, 1), F32)],
        input_output_aliases=aliases, name=name,
        compiler_params=_cp(("parallel", "arbitrary", "arbitrary")))(*[a for a, _ in ins])


def flash_dkv(name, qa, ka, va, doa, lse, delta, *, w, vw, hkv, grp, vcol0, scale, nlat, tq, tk, band, ctx_k, prev=None):
    n = qa.shape[0]
    cblk = nlat // NCTX
    nqb = nlat // tq
    band = band and not ctx_k
    if ctx_k:
        tk = NCTX
        nqs = nqb
        grid = (hkv, 1, grp * nqs)
        kmap = lambda hk, j, t: (cblk, hk)
        vmap = lambda hk, j, t: (cblk, vcol0 + hk)
        dvmap = lambda hk, j, t: (cblk, hk)
        qb_of = lambda j, t: t % nqs
    else:
        nqs = 3 if band else nqb
        grid = (hkv, nlat // tk, grp * nqs)
        kmap = lambda hk, j, t: (j, hk)
        vmap = lambda hk, j, t: (j, vcol0 + hk)
        dvmap = lambda hk, j, t: (j, hk)
        qb_of = (lambda j, t: jnp.clip(j + t % nqs - 1, 0, nqb - 1)) if band else (lambda j, t: t % nqs)
    qmap = lambda hk, j, t: (qb_of(j, t), hk * grp + t // nqs)
    lmap = lambda hk, j, t: (hk * grp + t // nqs, qb_of(j, t), 0)

    def body(*refs):
        refs = list(refs)
        q_ref, k_ref, v_ref, do_ref, l_ref, dl_ref = refs[:6]
        pos = 6
        if ctx_k:
            qe_ref, doe_ref, le_ref, dle_ref = refs[pos:pos + 4]
            pos += 4
        if prev is not None:
            pos += 2
        dk_ref, dv_ref = refs[pos:pos + 2]
        j = pl.program_id(1)
        t = pl.program_id(2)
        kblk = k_ref[...]
        vblk = v_ref[...]

        def contrib(q, do, lse_v, dl_v, mask):
            s = _d(q, kblk, ((1,), (1,))) * (scale * LOG2E)
            if mask is not None:
                s = jnp.where(mask, s, NEG)
            p = jnp.exp2(s - lse_v)
            dp = _d(do, vblk, ((1,), (1,)))
            ds = p * (dp - dl_v) * scale
            return _d(ds, q, ((0,), (0,))), _d(p, do, ((0,), (0,)))

        @pl.when(t == 0)
        def _():
            dk = jnp.zeros((tk, w), F32)
            dv = jnp.zeros((tk, vw), F32)
            if ctx_k:
                for gi in range(grp):
                    a, b = contrib(qe_ref[:, w * gi:w * gi + w], doe_ref[:, vw * gi:vw * gi + vw], le_ref[gi], dle_ref[gi], None)
                    dk = dk + a
                    dv = dv + b
            dk_ref[...] = dk
            dv_ref[...] = dv

        def add(mask):
            a, b = contrib(q_ref[...], do_ref[...], l_ref[0], dl_ref[0], mask)
            dk_ref[...] += a
            dv_ref[...] += b

        if band:
            qb = j + t % nqs - 1

            @pl.when((qb >= 0) & (qb < nqb))
            def _():
                add(_band_mask(tq, tk, qb, j))
        else:
            add(None)

    ins = [(qa, pl.BlockSpec((tq, w), qmap)), (ka, pl.BlockSpec((tk, w), kmap)), (va, pl.BlockSpec((tk, vw), vmap)),
           (doa, pl.BlockSpec((tq, vw), qmap)), (lse, pl.BlockSpec((1, tq, 1), lmap)), (delta, pl.BlockSpec((1, tq, 1), lmap))]
    if ctx_k:
        ins += [(qa, pl.BlockSpec((NCTX, grp * w), lambda hk, j, t: (cblk, hk))),
                (doa, pl.BlockSpec((NCTX, grp * vw), lambda hk, j, t: (cblk, hk))),
                (lse, pl.BlockSpec((grp, NCTX, 1), lambda hk, j, t: (hk, cblk, 0))),
                (delta, pl.BlockSpec((grp, NCTX, 1), lambda hk, j, t: (hk, cblk, 0)))]
    aliases = {}
    if prev is not None:
        any_spec = pl.BlockSpec(memory_space=pl.ANY)
        aliases = {len(ins): 0, len(ins) + 1: 1}
        ins += [(prev[0], any_spec), (prev[1], any_spec)]
    return pl.pallas_call(
        body, out_shape=[_sd((n, hkv * w), F32), _sd((n, hkv * vw), F32)], grid=grid,
        in_specs=[s for _, s in ins],
        out_specs=[pl.BlockSpec((tk, w), kmap), pl.BlockSpec((tk, vw), dvmap)],
        input_output_aliases=aliases, name=name,
        compiler_params=_cp(("parallel", "parallel", "arbitrary")))(*[a for a, _ in ins])


def mla_fwd(name, qm, km, vm, nlat):
    n = qm.shape[0]
    t = NCTX
    nlt = nlat // t
    c = (MLA_NOPE + MLA_ROPE) ** -0.5 * LOG2E

    def body(q_ref, k_ref, v_ref, o_ref, l_ref):
        i = pl.program_id(1)

        def run(k, v):
            s = _d(q_ref[...], k, ((1,), (1,))) * c
            m = jnp.max(_fold_lanes(s, jnp.maximum), axis=1, keepdims=True)
            p = jnp.exp2(s - m)
            l = jnp.sum(_fold_lanes(p, jnp.add), axis=1, keepdims=True)
            o_ref[...] = (_d(p, v, ((1,), (0,))) / l).astype(BF16)
            l_ref[0] = m + jnp.log2(l)

        @pl.when(i < nlt)
        def _():
            run(k_ref[...], v_ref[...])

        @pl.when(i == nlt)
        def _():
            run(k_ref[nlat:n, :], v_ref[nlat:n, :])

    return pl.pallas_call(
        body, out_shape=[_sd((n, MLA_H * 128), BF16), _sd((MLA_H, n, 1), F32)], grid=(MLA_H, n // t),
        in_specs=[pl.BlockSpec((t, 256), lambda h, i: (i, h)), pl.BlockSpec((n, 256), lambda h, i: (0, h)),
                  pl.BlockSpec((n, 128), lambda h, i: (0, h))],
        out_specs=[pl.BlockSpec((t, 128), lambda h, i: (i, h)), pl.BlockSpec((1, t, 1), lambda h, i: (h, i, 0))],
        name=name, compiler_params=_cp(("parallel", "arbitrary")))(qm, km, vm)


def mla_dq(name, qm, km, vm, o, do, lse, nlat):
    n = qm.shape[0]
    t = NCTX
    nlt = nlat // t
    scale = (MLA_NOPE + MLA_ROPE) ** -0.5

    def body(q_ref, k_ref, v_ref, o_ref, do_ref, l_ref, dq_ref, dl_ref):
        i = pl.program_id(1)
        do = do_ref[...]
        delta = jnp.sum(do.astype(F32) * o_ref[...].astype(F32), axis=1, keepdims=True)
        dl_ref[0] = delta

        def run(k, v):
            s = _d(q_ref[...], k, ((1,), (1,))) * (scale * LOG2E)
            ds = jnp.exp2(s - l_ref[0]) * (_d(do, v, ((1,), (1,))) - delta) * scale
            dq_ref[...] = _d(ds, k, ((1,), (0,)))

        @pl.when(i < nlt)
        def _():
            run(k_ref[...], v_ref[...])

        @pl.when(i == nlt)
        def _():
            run(k_ref[nlat:n, :], v_ref[nlat:n, :])

    qspec = pl.BlockSpec((t, 256), lambda h, i: (i, h))
    ospec = pl.BlockSpec((t, 128), lambda h, i: (i, h))
    lspec = pl.BlockSpec((1, t, 1), lambda h, i: (h, i, 0))
    return pl.pallas_call(
        body, out_shape=[_sd((n, MLA_H * 256), F32), _sd((MLA_H, n, 1), F32)], grid=(MLA_H, n // t),
        in_specs=[qspec, pl.BlockSpec((n, 256), lambda h, i: (0, h)), pl.BlockSpec((n, 128), lambda h, i: (0, h)),
                  ospec, ospec, lspec],
        out_specs=[qspec, lspec],
        name=name, compiler_params=_cp(("parallel", "arbitrary")))(qm, km, vm, o, do, lse)


def mla_dkv(name, qm, km, vm, do, lse_row, delta_row, nlat):
    n = qm.shape[0]
    t = NCTX
    nlt = nlat // t
    scale = (MLA_NOPE + MLA_ROPE) ** -0.5

    def body(q_ref, k_ref, v_ref, do_ref, l_ref, dl_ref, dk_ref, dv_ref):
        j = pl.program_id(1)

        def run(q, do, lrow, drow):
            st = _d(k_ref[...], q, ((1,), (1,))) * (scale * LOG2E)
            pt = jnp.exp2(st - lrow)
            dv_ref[...] = _d(pt, do, ((1,), (0,)))
            dst = pt * (_d(v_ref[...], do, ((1,), (1,))) - drow) * scale
            dk_ref[...] = _d(dst, q, ((1,), (0,)))

        @pl.when(j < nlt)
        def _():
            run(q_ref[0:nlat, :], do_ref[0:nlat, :], l_ref[0, :, 0:nlat], dl_ref[0, :, 0:nlat])

        @pl.when(j == nlt)
        def _():
            run(q_ref[...], do_ref[...], l_ref[0], dl_ref[0])

    rspec = pl.BlockSpec((1, 1, n), lambda h, j: (h, 0, 0))
    return pl.pallas_call(
        body, out_shape=[_sd((n, MLA_H * 256), F32), _sd((n, MLA_H * 128), F32)], grid=(MLA_H, n // t),
        in_specs=[pl.BlockSpec((n, 256), lambda h, j: (0, h)), pl.BlockSpec((t, 256), lambda h, j: (j, h)),
                  pl.BlockSpec((t, 128), lambda h, j: (j, h)), pl.BlockSpec((n, 128), lambda h, j: (0, h)), rspec, rspec],
        out_specs=[pl.BlockSpec((t, 256), lambda h, j: (j, h)), pl.BlockSpec((t, 128), lambda h, j: (j, h))],
        name=name, compiler_params=_cp(("parallel", "arbitrary")))(qm, km, vm, do, lse_row, delta_row)


def mla_attention_bwd(tag, qm, km, vm, o, do, lse, nlat):
    n = qm.shape[0]
    dq, delta = mla_dq(tag + "_dq", qm, km, vm, o, do, lse, nlat)
    dk, dv = mla_dkv(tag + "_dkv", qm, km, vm, do, lse.reshape(MLA_H, 1, n), delta.reshape(MLA_H, 1, n), nlat)
    return dq, dk, dv


SWA_T = 512


def _swa_window(t, nlat):
    t = min(t, nlat)
    return t, min(t + 2 * SWA_WIN, nlat)


def _win_start(i, t, wlen, nlat):
    return pl.multiple_of(jnp.clip(i * t - SWA_WIN, 0, nlat - wlen), 128)


def _win_mask(rows, cols, row0, col0):
    rp = row0 + lax.broadcasted_iota(jnp.int32, (rows, cols), 0)
    cp = col0 + lax.broadcasted_iota(jnp.int32, (rows, cols), 1)
    return jnp.abs(rp - cp) <= SWA_WIN


def swa_fwd_lat(name, qs, ks, u, sink, nlat):
    n = qs.shape[0]
    tq, wlen = _swa_window(SWA_T, nlat)
    grp = SWA_HQ // SWA_HKV
    scale = SWA_DH ** -0.5
    vcol0 = C_V // 128

    def body(q_ref, k_ref, v_ref, s_ref, o_ref, l_ref):
        i = pl.program_id(1)
        ws = _win_start(i, tq, wlen, nlat)
        q = q_ref[...]
        s1 = _d(q, k_ref[pl.ds(ws, wlen), :], ((1,), (1,))) * (scale * LOG2E)
        s1 = jnp.where(_win_mask(tq, wlen, i * tq, ws), s1, NEG)
        s2 = _d(q, k_ref[pl.ds(nlat, NCTX), :], ((1,), (1,))) * (scale * LOG2E)
        sv = jnp.max(s_ref[0], axis=1, keepdims=True) * LOG2E
        m = jnp.maximum(jnp.maximum(jnp.max(s1, axis=1, keepdims=True), jnp.max(s2, axis=1, keepdims=True)), sv)
        p1 = jnp.exp2(s1 - m)
        p2 = jnp.exp2(s2 - m)
        l = jnp.sum(p1, axis=1, keepdims=True) + jnp.sum(p2, axis=1, keepdims=True) + jnp.exp2(sv - m)
        acc = _d(p1, v_ref[pl.ds(ws, wlen), :], ((1,), (0,))) + _d(p2, v_ref[pl.ds(nlat, NCTX), :], ((1,), (0,)))
        o_ref[...] = (acc / l).astype(BF16)
        l_ref[0] = m + jnp.log2(l)

    return pl.pallas_call(
        body, out_shape=[_sd((n, SWA_HQ * 128), BF16), _sd((SWA_HQ, n, 1), F32)], grid=(SWA_HQ, nlat // tq),
        in_specs=[pl.BlockSpec((tq, 128), lambda h, i: (i, h)), pl.BlockSpec((n, 128), lambda h, i: (0, h // grp)),
                  pl.BlockSpec((n, 128), lambda h, i: (0, vcol0 + h // grp)), pl.BlockSpec((1, 1, 128), lambda h, i: (h, 0, 0))],
        out_specs=[pl.BlockSpec((tq, 128), lambda h, i: (i, h)), pl.BlockSpec((1, tq, 1), lambda h, i: (h, i, 0))],
        name=name, compiler_params=_cp(("parallel", "arbitrary")))(qs, ks, u, sink)


def swa_dq_lat(name, qs, ks, u, o, do, lse, sink, nlat):
    n = qs.shape[0]
    tq, wlen = _swa_window(SWA_T, nlat)
    grp = SWA_HQ // SWA_HKV
    scale = SWA_DH ** -0.5
    vcol0 = C_V // 128

    def body(q_ref, k_ref, v_ref, s_ref, o_ref, do_ref, l_ref, dq_ref, dl_ref, ds_ref):
        i = pl.program_id(1)
        ws = _win_start(i, tq, wlen, nlat)
        q = q_ref[...]
        do = do_ref[...]
        lse_v = l_ref[0]
        delta = jnp.sum(do.astype(F32) * o_ref[...].astype(F32), axis=1, keepdims=True)
        kw = k_ref[pl.ds(ws, wlen), :]
        kc = k_ref[pl.ds(nlat, NCTX), :]
        s1 = _d(q, kw, ((1,), (1,))) * (scale * LOG2E)
        s1 = jnp.where(_win_mask(tq, wlen, i * tq, ws), s1, NEG)
        s2 = _d(q, kc, ((1,), (1,))) * (scale * LOG2E)
        ds1 = jnp.exp2(s1 - lse_v) * (_d(do, v_ref[pl.ds(ws, wlen), :], ((1,), (1,))) - delta) * scale
        ds2 = jnp.exp2(s2 - lse_v) * (_d(do, v_ref[pl.ds(nlat, NCTX), :], ((1,), (1,))) - delta) * scale
        dq_ref[...] = _d(ds1, kw, ((1,), (0,))) + _d(ds2, kc, ((1,), (0,)))
        dl_ref[0] = delta
        sv = jnp.max(s_ref[0], axis=1, keepdims=True) * LOG2E
        dsk = jnp.sum(-jnp.exp2(sv - lse_v) * delta, axis=0, keepdims=True)
        _acc(ds_ref, jnp.zeros((1, 1, 128), F32) + dsk, i == 0)

    qspec = pl.BlockSpec((tq, 128), lambda h, i: (i, h))
    lspec = pl.BlockSpec((1, tq, 1), lambda h, i: (h, i, 0))
    return pl.pallas_call(
        body, out_shape=[_sd((n, SWA_HQ * 128), F32), _sd((SWA_HQ, n, 1), F32), _sd((SWA_HQ, 1, 128), F32)],
        grid=(SWA_HQ, nlat // tq),
        in_specs=[qspec, pl.BlockSpec((n, 128), lambda h, i: (0, h // grp)),
                  pl.BlockSpec((n, 128), lambda h, i: (0, vcol0 + h // grp)), pl.BlockSpec((1, 1, 128), lambda h, i: (h, 0, 0)),
                  qspec, qspec, lspec],
        out_specs=[qspec, lspec, pl.BlockSpec((1, 1, 128), lambda h, i: (h, 0, 0))],
        name=name, compiler_params=_cp(("parallel", "arbitrary")))(qs, ks, u, sink, o, do, lse)


def swa_dkv_lat(name, qs, ks, u, do, lse_row, delta_row, nlat):
    n = qs.shape[0]
    tk, wlen = _swa_window(SWA_T, nlat)
    grp = SWA_HQ // SWA_HKV
    scale = SWA_DH ** -0.5
    vcol0 = C_V // 128

    def body(q_ref, k_ref, v_ref, do_ref, l_ref, dl_ref, dk_ref, dv_ref):
        j = pl.program_id(1)
        ws = _win_start(j, tk, wlen, nlat)
        k = k_ref[...]
        v = v_ref[...]
        mask = _win_mask(tk, wlen, j * tk, ws)
        dk = jnp.zeros((tk, 128), F32)
        dv = jnp.zeros((tk, 128), F32)
        for gi in range(grp):
            qw = q_ref[pl.ds(ws, wlen), 128 * gi:128 * gi + 128]
            dow = do_ref[pl.ds(ws, wlen), 128 * gi:128 * gi + 128]
            st = jnp.where(mask, _d(k, qw, ((1,), (1,))) * (scale * LOG2E), NEG)
            pt = jnp.exp2(st - l_ref[gi, :, pl.ds(ws, wlen)])
            dv = dv + _d(pt, dow, ((1,), (0,)))
            dst = pt * (_d(v, dow, ((1,), (1,))) - dl_ref[gi, :, pl.ds(ws, wlen)]) * scale
            dk = dk + _d(dst, qw, ((1,), (0,)))
        dk_ref[...] = dk
        dv_ref[...] = dv

    rspec = pl.BlockSpec((grp, 1, n), lambda hk, j: (hk, 0, 0))
    return pl.pallas_call(
        body, out_shape=[_sd((n, SWA_HKV * 128), F32), _sd((n, SWA_HKV * 128), F32)], grid=(SWA_HKV, nlat // tk),
        in_specs=[pl.BlockSpec((n, grp * 128), lambda hk, j: (0, hk)), pl.BlockSpec((tk, 128), lambda hk, j: (j, hk)),
                  pl.BlockSpec((tk, 128), lambda hk, j: (j, vcol0 + hk)), pl.BlockSpec((n, grp * 128), lambda hk, j: (0, hk)),
                  rspec, rspec],
        out_specs=[pl.BlockSpec((tk, 128), lambda hk, j: (j, hk)), pl.BlockSpec((tk, 128), lambda hk, j: (j, hk))],
        name=name, compiler_params=_cp(("parallel", "arbitrary")))(qs, ks, u, do, lse_row, delta_row)


def swa_attention_fwd(tag, qs, ks, u, sink, cfg, nlat):
    o, lse = swa_fwd_lat(tag + "_fwd_lat", qs, ks, u, sink, nlat)
    return flash_fwd(tag + "_fwd_ctx", qs, ks, u, sink=sink, ctx_q=True, nlat=nlat, prev=(o, lse), **cfg)


def swa_attention_bwd(tag, qs, ks, u, o, do, lse, sink, cfg, nlat):
    n = qs.shape[0]
    dq, delta, ds1 = swa_dq_lat(tag + "_dq_lat", qs, ks, u, o, do, lse, sink, nlat)
    dq, delta, ds2 = flash_dq(tag + "_dq_ctx", qs, ks, u, o, do, lse, sink=sink, ctx_q=True, nlat=nlat, prev=(dq, delta), **cfg)
    dk, dv = swa_dkv_lat(tag + "_dkv_lat", qs, ks, u, do, lse.reshape(SWA_HQ, 1, n), delta.reshape(SWA_HQ, 1, n), nlat)
    kc = {k: v for k, v in cfg.items() if k != "hq"}
    kc["hkv"] = SWA_HKV
    kc["tq"] = min(1024, nlat)
    dk, dv = flash_dkv(tag + "_dkv_ctx", qs, ks, u, do, lse, delta, ctx_k=True, nlat=nlat, prev=(dk, dv), **kc)
    return dq, dk, dv, ds1 + ds2


def adamw(name, w, g, m, v):
    r, c = w.shape
    tr = _pick(r, (256, 128, 64, 32, 16, 8))
    bc1 = 1.0 - ADAM_B1 ** ADAM_STEP
    bc2 = 1.0 - ADAM_B2 ** ADAM_STEP

    def body(w_ref, g_ref, m_ref, v_ref, d_ref, nm_ref, nv_ref):
        gv = g_ref[...]
        nm = ADAM_B1 * m_ref[...] + (1.0 - ADAM_B1) * gv
        nv = ADAM_B2 * v_ref[...] + (1.0 - ADAM_B2) * (gv * gv)
        d_ref[...] = -ADAM_LR * ((nm / bc1) / (jnp.sqrt(nv / bc2) + ADAM_EPS) + ADAM_WD * w_ref[...])
        nm_ref[...] = nm
        nv_ref[...] = nv

    spec = pl.BlockSpec((tr, c), lambda i: (i, 0))
    return pl.pallas_call(body, out_shape=[_sd((r, c), F32)] * 3, grid=(r // tr,), in_specs=[spec] * 4, out_specs=[spec] * 3,
                          name=name, compiler_params=_cp(("parallel",)))(w, g, m, v)


def _coords():
    return lax.axis_index("x"), lax.axis_index("y"), lax.axis_index("c")


_ANY = pl.BlockSpec(memory_space=pl.ANY)


def _per_core(fn):
    c = lax.axis_index("c")
    for cs in (0, 1):
        pl.when(c == cs)(functools.partial(fn, cs))


def gather_chips(name, a):
    r = a.shape[0]
    half = r // 2

    def body(a_ref, o_ref, ici_send, ici_recv, d2d_send, d2d_recv, loc_sem):
        _per_core(functools.partial(run, a_ref, o_ref, ici_send, ici_recv, d2d_send, d2d_recv, loc_sem))

    def run(a_ref, o_ref, ici_send, ici_recv, d2d_send, d2d_recv, loc_sem, c):
        x, y, _ = _coords()
        me = 2 * x + y
        peers = [(1 - x, y), (x, 1 - y), (1 - x, 1 - y)]
        my_rows = pl.ds(c * half, half)
        sib_rows = pl.ds((1 - c) * half, half)
        mine = pltpu.make_async_copy(a_ref, o_ref.at[me], loc_sem)
        mine.start()
        sends = [pltpu.make_async_remote_copy(a_ref.at[my_rows], o_ref.at[me, my_rows], ici_send.at[k], ici_recv.at[k],
                                              device_id=(px, py, c), device_id_type=MESH)
                 for k, (px, py) in enumerate(peers)]
        for cp in sends:
            cp.start()
        passed = []
        for k, (px, py) in enumerate(peers):
            s = 2 * px + py
            pltpu.make_async_remote_copy(a_ref.at[my_rows], o_ref.at[s, my_rows], ici_send.at[k], ici_recv.at[k],
                                         device_id=(px, py, c), device_id_type=MESH).wait_recv()
            fw = pltpu.make_async_remote_copy(o_ref.at[s, my_rows], o_ref.at[s, my_rows], d2d_send.at[k], d2d_recv.at[k],
                                              device_id=(x, y, 1 - c), device_id_type=MESH)
            fw.start()
            passed.append(fw)
        for k, (px, py) in enumerate(peers):
            s = 2 * px + py
            pltpu.make_async_remote_copy(o_ref.at[s, sib_rows], o_ref.at[s, sib_rows], d2d_send.at[k], d2d_recv.at[k],
                                         device_id=(x, y, 1 - c), device_id_type=MESH).wait_recv()
        for cp in sends + passed:
            cp.wait_send()
        mine.wait()

    return pl.pallas_call(
        body, out_shape=_sd((4,) + a.shape, a.dtype), in_specs=[_ANY], out_specs=_ANY,
        scratch_shapes=[pltpu.SemaphoreType.DMA((3,)), pltpu.SemaphoreType.DMA((3,)), pltpu.SemaphoreType.DMA((3,)),
                        pltpu.SemaphoreType.DMA((3,)), pltpu.SemaphoreType.DMA],
        name=name, compiler_params=pltpu.CompilerParams(has_side_effects=True))(a)


def pair_split(name, a):
    k4, r, cdim = a.shape
    half = r // 2

    def body(a_ref, own_ref, got_ref, send_sem, recv_sem, loc_sem):
        _per_core(functools.partial(run, a_ref, own_ref, got_ref, send_sem, recv_sem, loc_sem))

    def run(a_ref, own_ref, got_ref, send_sem, recv_sem, loc_sem, c):
        x, y, _ = _coords()
        my_rows = pl.ds(c * half, half)
        sib_rows = pl.ds((1 - c) * half, half)
        mine = pltpu.make_async_copy(a_ref.at[:, my_rows], own_ref, loc_sem)
        mine.start()
        cp = pltpu.make_async_remote_copy(a_ref.at[:, sib_rows], got_ref, send_sem, recv_sem,
                                          device_id=(x, y, 1 - c), device_id_type=MESH)
        cp.start()
        cp.wait()
        mine.wait()

    return pl.pallas_call(
        body, out_shape=[_sd((k4, half, cdim), a.dtype), _sd((k4, half, cdim), a.dtype)], in_specs=[_ANY],
        out_specs=[_ANY, _ANY],
        scratch_shapes=[pltpu.SemaphoreType.DMA, pltpu.SemaphoreType.DMA, pltpu.SemaphoreType.DMA],
        name=name, compiler_params=pltpu.CompilerParams(has_side_effects=True))(a)


def scatter_chips(name, a):
    def body(a_ref, o_ref, send_sems, recv_sems, loc_sem):
        x, y, c = _coords()
        me = 2 * x + y
        peers = [(1 - x, y), (x, 1 - y), (1 - x, 1 - y)]
        mine = pltpu.make_async_copy(a_ref.at[me], o_ref.at[me], loc_sem)
        mine.start()
        sends = [pltpu.make_async_remote_copy(a_ref.at[2 * px + py], o_ref.at[me], send_sems.at[k], recv_sems.at[k],
                                              device_id=(px, py, c), device_id_type=MESH)
                 for k, (px, py) in enumerate(peers)]
        for cp in sends:
            cp.start()
        for k, (px, py) in enumerate(peers):
            pltpu.make_async_remote_copy(a_ref.at[me], o_ref.at[2 * px + py], send_sems.at[k], recv_sems.at[k],
                                         device_id=(px, py, c), device_id_type=MESH).wait_recv()
        for cp in sends:
            cp.wait_send()
        mine.wait()

    return pl.pallas_call(
        body, out_shape=_sd(a.shape, a.dtype), in_specs=[_ANY], out_specs=_ANY,
        scratch_shapes=[pltpu.SemaphoreType.DMA((3,)), pltpu.SemaphoreType.DMA((3,)), pltpu.SemaphoreType.DMA],
        name=name, compiler_params=pltpu.CompilerParams(has_side_effects=True))(a)


def pair_join(name, a):
    half, cdim = a.shape

    def body(a_ref, o_ref, send_sem, recv_sem, loc_sem):
        _per_core(functools.partial(run, a_ref, o_ref, send_sem, recv_sem, loc_sem))

    def run(a_ref, o_ref, send_sem, recv_sem, loc_sem, c):
        x, y, _ = _coords()
        my_rows = pl.ds(c * half, half)
        sib_rows = pl.ds((1 - c) * half, half)
        mine = pltpu.make_async_copy(a_ref, o_ref.at[my_rows], loc_sem)
        mine.start()
        cp = pltpu.make_async_remote_copy(a_ref, o_ref.at[my_rows], send_sem, recv_sem, device_id=(x, y, 1 - c),
                                          device_id_type=MESH)
        cp.start()
        cp.wait_send()
        pltpu.make_async_remote_copy(a_ref, o_ref.at[sib_rows], send_sem, recv_sem, device_id=(x, y, 1 - c),
                                     device_id_type=MESH).wait_recv()
        mine.wait()

    return pl.pallas_call(
        body, out_shape=_sd((2 * half, cdim), a.dtype), in_specs=[_ANY], out_specs=_ANY,
        scratch_shapes=[pltpu.SemaphoreType.DMA, pltpu.SemaphoreType.DMA, pltpu.SemaphoreType.DMA],
        name=name, compiler_params=pltpu.CompilerParams(has_side_effects=True))(a)


def add_cast(name, a, b, dtype):
    k, r, c = a.shape
    tr = _pick(r, (1024, 512, 256, 128, 64, 32, 16, 8))

    def body(a_ref, b_ref, o_ref):
        o_ref[...] = (a_ref[...].astype(F32) + b_ref[...].astype(F32)).astype(dtype)

    spec = pl.BlockSpec((1, tr, c), lambda s, i: (s, i, 0))
    return pl.pallas_call(body, out_shape=_sd((k, r, c), dtype), grid=(k, r // tr), in_specs=[spec, spec], out_specs=spec,
                          name=name, compiler_params=_cp(("parallel", "parallel")))(a, b)


def gather_all(name, a):
    def body(a_ref, o_ref, send_sems, recv_sems, loc_sem):
        x, y, c = _coords()
        me = 4 * x + 2 * y + c
        flips = [(fx, fy, fc) for fx in (0, 1) for fy in (0, 1) for fc in (0, 1) if fx + fy + fc > 0]
        peers = [(x ^ fx, y ^ fy, c ^ fc) for fx, fy, fc in flips]
        mine = pltpu.make_async_copy(a_ref, o_ref.at[me], loc_sem)
        mine.start()
        sends = [pltpu.make_async_remote_copy(a_ref, o_ref.at[me], send_sems.at[k], recv_sems.at[k],
                                              device_id=p, device_id_type=MESH) for k, p in enumerate(peers)]
        for cp in sends:
            cp.start()
        for k, (px, py, pc) in enumerate(peers):
            pltpu.make_async_remote_copy(a_ref, o_ref.at[4 * px + 2 * py + pc], send_sems.at[k], recv_sems.at[k],
                                         device_id=(px, py, pc), device_id_type=MESH).wait_recv()
        for cp in sends:
            cp.wait_send()
        mine.wait()

    return pl.pallas_call(
        body, out_shape=_sd((8,) + a.shape, a.dtype), in_specs=[_ANY], out_specs=_ANY,
        scratch_shapes=[pltpu.SemaphoreType.DMA((7,)), pltpu.SemaphoreType.DMA((7,)), pltpu.SemaphoreType.DMA],
        name=name, compiler_params=pltpu.CompilerParams(has_side_effects=True))(a)


def sum_blocks(name, a):
    k, r, c = a.shape
    tr = _pick(r, (1024, 256, 128, 64, 32, 16, 8))

    def body(a_ref, o_ref):
        acc = a_ref[0].astype(F32)
        for s in range(1, k):
            acc = acc + a_ref[s].astype(F32)
        o_ref[...] = acc

    return pl.pallas_call(body, out_shape=_sd((r, c), F32), grid=(r // tr,),
                          in_specs=[pl.BlockSpec((k, tr, c), lambda i: (0, i, 0))], out_specs=pl.BlockSpec((tr, c), lambda i: (i, 0)),
                          name=name, compiler_params=_cp(("parallel",)))(a)


BIG = ("w_mod", "w_in", "w_mla_uq", "w_mla_ukv", "w_p_ssm", "w_p_swa", "w_p_mla", "w_out", "w_ffn_in", "w_ffn_out")
COL_SHARDED = ("w_mod", "w_in", "w_mla_uq", "w_mla_ukv", "w_ffn_in")
SMALL = ("c_ctx", "b_mod", "norm1_g", "norm2_g", "ssm_conv_w", "ssm_conv_b", "ssm_dt_bias", "ssm_a_log", "ssm_d",
         "ssm_norm_g", "swa_q_norm_g", "swa_k_norm_g", "swa_sink", "mla_q_lat_g", "mla_kv_lat_g", "mla_q_norm_g",
         "mla_k_norm_g")
WEIGHTS = ("c_ctx", "w_mod", "b_mod", "norm1_g", "norm2_g", "w_in", "ssm_conv_w", "ssm_conv_b", "ssm_dt_bias", "ssm_a_log",
           "ssm_d", "ssm_norm_g", "swa_q_norm_g", "swa_k_norm_g", "swa_sink", "mla_q_lat_g", "mla_kv_lat_g", "w_mla_uq",
           "w_mla_ukv", "mla_q_norm_g", "mla_k_norm_g", "w_p_ssm", "w_p_swa", "w_p_mla", "w_out", "w_ffn_in", "w_ffn_out")


def pack_w_in(w):
    z = lambda k: jnp.zeros((w.shape[0], k), w.dtype)
    return jnp.concatenate([w[:, 4832:7904], w[:, 2400:3424], w[:, 3424:4448], w[:, 0:1536], w[:, 1568:1824], w[:, 1824:2080],
                            w[:, 2080:2336], w[:, 2336:2400], w[:, 1536:1568], z(32), z(128), w[:, 4448:4832]], axis=1)


def unpack_w_in(g):
    return jnp.concatenate([g[:, 5120:6656], g[:, 7488:7520], g[:, 6656:6912], g[:, 6912:7168], g[:, 7168:7424], g[:, 7424:7488],
                            g[:, 3072:4096], g[:, 4096:5120], g[:, 7680:8064], g[:, 0:3072]], axis=1)


def pack_ukv(w):
    return w.reshape(MLA_KVRANK, MLA_H, 2, 128).transpose(0, 2, 1, 3).reshape(MLA_KVRANK, 2048)


def unpack_ukv(g):
    return g.reshape(MLA_KVRANK, 2, MLA_H, 128).transpose(0, 2, 1, 3).reshape(MLA_KVRANK, 2048)


def pack_uq(w):
    return jnp.pad(w.reshape(MLA_QRANK, MLA_H, 192), ((0, 0), (0, 0), (0, 64))).reshape(MLA_QRANK, 2048)


def unpack_uq(g):
    return g.reshape(MLA_QRANK, MLA_H, 256)[:, :, :192].reshape(MLA_QRANK, 1536)


def rope_tables(nlat):
    t = jnp.arange(nlat, dtype=jnp.int32)
    r = (t // GRID_W).astype(F32)[:, None]
    col = (t % GRID_W).astype(F32)[:, None]

    def tab(nf, pad):
        inv = jnp.power(ROPE_BASE, -jnp.arange(nf, dtype=F32) / nf)
        ar, ac = r * inv, col * inv
        cos = jnp.concatenate([jnp.cos(ar), jnp.cos(ar), jnp.cos(ac), jnp.cos(ac), jnp.ones((nlat, pad), F32)], axis=1)
        sin = jnp.concatenate([-jnp.sin(ar), jnp.sin(ar), -jnp.sin(ac), jnp.sin(ac), jnp.zeros((nlat, pad), F32)], axis=1)
        cos = jnp.concatenate([cos, jnp.ones((NCTX, 128), F32)], axis=0)
        sin = jnp.concatenate([sin, jnp.zeros((NCTX, 128), F32)], axis=0)
        return cos, sin

    return tab(32, 0), tab(16, 64)


def _lanes(v, start, width=128):
    return jnp.zeros((1, width), F32).at[0, start:start + v.shape[0]].set(v)


def layer_fwd(i, xin, h, mod, p, tabs, nlat):
    t = "l%d_" % i
    n = xin.shape[0]
    (cos_s, sin_s), (cos_m, sin_m) = tabs
    u = mm(h, p["w_in"], F32, t + "in_proj")
    xbc = conv_fwd(t + "conv", u, p["conv_w"], p["conv_b"], nlat)
    dtrow = jnp.transpose(u[:, C_MISC + DT_LANE:C_MISC + DT_LANE + 32])
    nlc = nlat // Q
    yf, hs_f = ssd_fwd(t + "ssd_f", xbc, u, dtrow, p["bias_c"], p["alog_c"], p["bias_r"], p["alog_r"], nlc, False, 0)
    yb, hs_b = ssd_fwd(t + "ssd_b", xbc, u, dtrow, p["bias_c"], p["alog_c"], p["bias_r"], p["alog_r"], nlc, True, 1)
    ys = ssd_out_fwd(t + "ssd_out", yf, yb, xbc, u, p["ssm_norm_g"], p["d_exp"])
    qs, ks = swa_prep_fwd(t + "swa_prep", u, p["swa_q_g"], p["swa_k_g"], cos_s, sin_s)
    o_swa, lse_swa = swa_attention_fwd(t + "swa", qs, ks, u, p["sink"], p["swa_cfg"], nlat)
    ckv_n, cq_n = lat_norm_fwd(t + "lat_norm", u, p["kv_lat_g"], p["q_lat_g"])
    kv = mm(ckv_n, p["w_ukv"], F32, t + "ukv")
    qp = mm(cq_n, p["w_uq"], F32, t + "uq")
    km, qm, vm = mla_prep_fwd(t + "mla_prep", kv, qp, u, p["mla_q_g"], p["mla_k_g"], cos_m, sin_m)
    o_mla, lse_mla = mla_fwd(t + "mla_fwd", qm, km, vm, nlat)
    p1 = mm(ys, p["w_p_ssm"], F32, t + "p_ssm")
    p2 = mm(o_swa, p["w_p_swa"], F32, t + "p_swa")
    p3 = mm(o_mla, p["w_p_mla"], F32, t + "p_mla")
    merged = merge_fwd(t + "merge", u, p1, p2, p3)
    o = mm(merged, p["w_out"], F32, t + "out_proj")
    x1, h2 = resid_mod_fwd(t + "res1", xin, o, mod, 2, mod, 3, 4, p["norm2_g"], nlat // RT)
    gu = mm(h2, p["w_ffn_in"], F32, t + "ffn_in")
    a = swiglu_fwd(t + "swiglu", gu)
    f = mm(a, p["w_ffn_out"], F32, t + "ffn_out")
    saved = dict(xin=xin, h=h, u=u, xbc=xbc, dtrow=dtrow, yf=yf, yb=yb, hs_f=hs_f, hs_b=hs_b, ys=ys, qs=qs, ks=ks,
                 o_swa=o_swa, lse_swa=lse_swa, ckv_n=ckv_n, cq_n=cq_n, kv=kv, qp=qp, km=km, qm=qm, vm=vm, o_mla=o_mla,
                 lse_mla=lse_mla, p1=p1, p2=p2, p3=p3, merged=merged, o=o, x1=x1, h2=h2, gu=gu, a=a, f=f)
    del n
    return x1, f, saved


def layer_bwd(i, dx2, df, dgt2, sv, mod, p, tabs, nlat):
    t = "l%db_" % i
    (cos_s, sin_s), (cos_m, sin_m) = tabs
    g = {}
    nt = nlat // RT
    nlc = nlat // Q
    g["w_ffn_out"] = mm_tn(sv["a"], df, t + "wg_ffn_out")
    da = mm(df, p["w_ffn_out"], F32, t + "dg_ffn_out", trans_b=True)
    dgu = swiglu_bwd(t + "swiglu", sv["gu"], da)
    g["w_ffn_in"] = mm_tn(sv["h2"], dgu, t + "wg_ffn_in")
    dh2 = mm(dgu, p["w_ffn_in"], F32, t + "dg_ffn_in", trans_b=True)
    dx1, do, dgt1, dsh2, dsc2, g["norm2_g"] = resid_mod_bwd(t + "res1", sv["x1"], dx2, dh2, sv["o"], mod, 2, mod, 3, 4,
                                                              p["norm2_g"], nt)
    g["w_out"] = mm_tn(sv["merged"], do, t + "wg_out")
    dmerged = mm(do, p["w_out"], F32, t + "dg_out", trans_b=True)
    dp1, dp2, dp3, dgates = merge_bwd(t + "merge", sv["u"], sv["p1"], sv["p2"], sv["p3"], dmerged)
    g["w_p_ssm"] = mm_tn(sv["ys"], dp1, t + "wg_p_ssm")
    g["w_p_swa"] = mm_tn(sv["o_swa"], dp2, t + "wg_p_swa")
    g["w_p_mla"] = mm_tn(sv["o_mla"], dp3, t + "wg_p_mla")
    dys = mm(dp1, p["w_p_ssm"], F32, t + "dg_p_ssm", trans_b=True)
    do_swa = mm(dp2, p["w_p_swa"], BF16, t + "dg_p_swa", trans_b=True)
    do_mla = mm(dp3, p["w_p_mla"], BF16, t + "dg_p_mla", trans_b=True)
    dqm, dkm, dv_mla = mla_attention_bwd(t + "mla", sv["qm"], sv["km"], sv["vm"], sv["o_mla"], do_mla, sv["lse_mla"], nlat)
    dkv, dqp, dkr, g["mla_q_g"], g["mla_k_g"] = mla_prep_bwd(t + "mla_prep", sv["kv"], sv["qp"], sv["u"], p["mla_q_g"],
                                                             p["mla_k_g"], cos_m, sin_m, dkm, dqm, dv_mla)
    g["w_ukv"] = mm_tn(sv["ckv_n"], dkv, t + "wg_ukv")
    g["w_uq"] = mm_tn(sv["cq_n"], dqp, t + "wg_uq")
    dckv_n = mm(dkv, p["w_ukv"], F32, t + "dg_ukv", trans_b=True)
    dcq_n = mm(dqp, p["w_uq"], F32, t + "dg_uq", trans_b=True)
    dckv, dcq, g["kv_lat_g"], g["q_lat_g"] = lat_norm_bwd(t + "lat_norm", sv["u"], p["kv_lat_g"], p["q_lat_g"], dckv_n, dcq_n)
    dqs, dks, dv_swa, g["sink"] = swa_attention_bwd(t + "swa", sv["qs"], sv["ks"], sv["u"], sv["o_swa"], do_swa, sv["lse_swa"],
                                                p["sink"], p["swa_cfg"], nlat)
    dq, dk, dv, g["swa_q_g"], g["swa_k_g"] = swa_prep_bwd(t + "swa_prep", sv["u"], p["swa_q_g"], p["swa_k_g"], cos_s, sin_s,
                                                          dqs, dks, dv_swa)
    dy, dxs_skip, dz, g["ssm_norm_g"], g["d_exp"] = ssd_out_bwd(t + "ssd_out", sv["yf"], sv["yb"], sv["xbc"], sv["u"],
                                                                 p["ssm_norm_g"], p["d_exp"], dys)
    n = dy.shape[0]
    zbc = jnp.zeros((n, 256), F32)
    r_f = ssd_bwd(t + "ssd_f", sv["xbc"], sv["u"], sv["dtrow"], p["bias_c"], p["alog_c"], p["bias_r"], p["alog_r"],
                  sv["hs_f"], dy, (dxs_skip, zbc, zbc), nlc, False, 0)
    r_b = ssd_bwd(t + "ssd_b", sv["xbc"], sv["u"], sv["dtrow"], p["bias_c"], p["alog_c"], p["bias_r"], p["alog_r"],
                  sv["hs_b"], dy, (r_f[0], r_f[1], r_f[2]), nlc, True, 1)
    dact = jnp.concatenate([r_b[0], r_b[1], r_b[2]], axis=1)
    dxbc, g["conv_w"], g["conv_b"] = conv_bwd(t + "conv", sv["u"], dact, p["conv_w"], p["conv_b"], nlat)
    drow = jnp.concatenate([r_f[4][0] + r_f[4][1], r_b[4][0] + r_b[4][1]], axis=0)
    drow_t = jnp.pad(jnp.transpose(drow), ((0, 0), (DT_LANE, 128 - DT_LANE - 32)))
    dmisc = misc_combine(t + "misc", dkr, r_f[3], r_b[3], drow_t)
    g["bias_c"] = r_f[5] + r_b[5]
    g["alog_c"] = r_f[6] + r_b[6]
    g["bias_r"] = jnp.concatenate([r_f[7], r_b[7]], axis=0)
    g["alog_r"] = jnp.concatenate([r_f[8], r_b[8]], axis=0)
    du = jnp.concatenate([dgates, dz, dq, dxbc, dk, dv, dckv, dmisc, jnp.zeros((n, 128), BF16), dcq], axis=1)
    g["w_in"] = mm_tn(sv["h"], du, t + "wg_in")
    dh = mm(du, p["w_in"], F32, t + "dg_in", trans_b=True)
    g["mod"] = (dgt1, dsh2, dsc2, dgt2)
    return dx1, dh, g


def local_step(x, c, ctx, target, c_ctx, W, nlat):
    xin = jnp.concatenate([x, ctx], axis=0)
    n = xin.shape[0]
    nt = nlat // RT
    tabs = rope_tables(nlat)
    c8 = jnp.zeros((8, D), F32).at[0].set(c[0]).at[1].set(c_ctx)
    mods, silus = [], []
    for i in range(DEPTH):
        m8, s8 = mod_fwd("l%d_mod" % i, c8, W[i]["w_mod"], W[i]["b_mod"])
        mods.append(m8[0:2].reshape(2, 1, 6 * D))
        silus.append(s8)
    saved = []
    _, h = resid_mod_fwd("l0_norm1", xin, None, None, 0, mods[0], 0, 1, W[0]["norm1_g"], nt)
    xcur = xin
    for i in range(DEPTH):
        x1, f, sv = layer_fwd(i, xcur, h, mods[i], W[i], tabs, nlat)
        saved.append(sv)
        if i + 1 < DEPTH:
            xcur, h = resid_mod_fwd("l%d_res2" % i, x1, f, mods[i], 5, mods[i + 1], 0, 1, W[i + 1]["norm1_g"], nt)
    loss_v, dx2, df, dgt2 = resid_loss("loss", x1, f, mods[DEPTH - 1], 5, target, nt)
    grads = [None] * DEPTH
    for i in reversed(range(DEPTH)):
        dx1, dh, g = layer_bwd(i, dx2, df, dgt2, saved[i], mods[i], W[i], tabs, nlat)
        if i > 0:
            sv = saved[i]
            dx2, df, dgt2, dsh1, dsc1, g["norm1_g"] = resid_mod_bwd(
                "l%db_res2" % (i - 1), sv["xin"], dx1, dh, saved[i - 1]["f"], mods[i - 1], 5, mods[i], 0, 1,
                W[i]["norm1_g"], nt)
        else:
            dxin, _, _, dsh1, dsc1, g["norm1_g"] = resid_mod_bwd("l0b_norm1", saved[0]["xin"], dx1, dh, None, None, 0,
                                                                  mods[0], 0, 1, W[0]["norm1_g"], nt)
        dgt1, dsh2, dsc2, dgt2_i = g.pop("mod")
        dmod = jnp.concatenate([dsh1, dsc1, dgt1, dsh2, dsc2, dgt2_i], axis=2).reshape(2, 6 * D)
        dmod8 = jnp.zeros((8, 6 * D), F32).at[0:2].set(dmod)
        g["w_mod"] = mm_tn(silus[i], dmod8, "l%db_wg_mod" % i)
        dsilu = mm(dmod8, W[i]["w_mod"], F32, "l%db_dg_mod" % i, trans_b=True)
        dc8, g["b_mod"] = mod_small_bwd("l%db_mod_small" % i, c8, dsilu, dmod8)
        g["c8"] = dc8
        grads[i] = g
    del n
    return loss_v[0, 0], dxin, grads


def _big_shapes():
    return dict(w_mod=(2, 1024, 1536), w_in=(2, 1024, 1976), w_mla_uq=(2, 384, 384), w_mla_ukv=(2, 256, 512),
                w_p_ssm=(2, 256, 1024), w_p_swa=(2, 256, 1024), w_p_mla=(2, 256, 1024), w_out=(2, 256, 1024),
                w_ffn_in=(2, 1024, 1408), w_ffn_out=(2, 704, 1024))


PACK_ROWS = 14336


def _pack_big(d, dtype):
    parts = [d[k].astype(dtype).reshape(-1, 1024) for k in BIG]
    used = sum(p.shape[0] for p in parts)
    return jnp.concatenate(parts + [jnp.zeros((PACK_ROWS - used, 1024), dtype)], axis=0)


def _unpack_big(buf, lead):
    out = {}
    r0 = 0
    for k in BIG:
        sh = _big_shapes()[k]
        rows = sh[0] * sh[1] * sh[2] // 1024
        out[k] = buf[..., r0:r0 + rows, :].reshape(lead + sh)
        r0 += rows
    return out


def _full_from_chips(k, a):
    if k in COL_SHARDED:
        return a.transpose(1, 2, 0, 3).reshape(2, a.shape[2], 4 * a.shape[3])
    return a.transpose(1, 0, 2, 3).reshape(2, 4 * a.shape[2], a.shape[3])


def _chips_from_full(k, a):
    if k in COL_SHARDED:
        return a.reshape(a.shape[0], 4, a.shape[1] // 4).transpose(1, 0, 2)
    return a.reshape(4, a.shape[0] // 4, a.shape[1])


def _small_sizes():
    return dict(c_ctx=1024, b_mod=2 * 6144, norm1_g=2048, norm2_g=2048, ssm_conv_w=2 * 5 * 1536, ssm_conv_b=2 * 1536,
                ssm_dt_bias=64, ssm_a_log=64, ssm_d=32, ssm_norm_g=2048, swa_q_norm_g=256, swa_k_norm_g=256, swa_sink=16,
                mla_q_lat_g=768, mla_kv_lat_g=512, mla_q_norm_g=384, mla_k_norm_g=384)


def _pack_small(d):
    parts = []
    for k in SMALL:
        v = d[k].astype(F32).reshape(-1)
        parts.append(jnp.pad(v, (0, (-v.shape[0]) % 1024)))
    return jnp.concatenate(parts).reshape(-1, 128)


def _unpack_small(buf, shapes):
    flat = buf.reshape(-1)
    out = {}
    o = 0
    for k in SMALL:
        sz = _small_sizes()[k]
        out[k] = flat[o:o + sz].reshape(shapes[k])
        o += sz + (-sz) % 1024
    return out


def big_grads(grads):
    gfull = {k: [] for k in BIG}
    for i in range(DEPTH):
        g = grads[i]
        gfull["w_mod"].append(g["w_mod"])
        gfull["w_in"].append(unpack_w_in(g["w_in"]))
        gfull["w_mla_uq"].append(unpack_uq(g["w_uq"]))
        gfull["w_mla_ukv"].append(unpack_ukv(g["w_ukv"]))
        for k in ("w_p_ssm", "w_p_swa", "w_p_mla", "w_out", "w_ffn_in", "w_ffn_out"):
            gfull[k].append(g[k])
    return gfull


def small_grads(grads):
    gs = {}
    gs["c_ctx"] = sum(grads[i]["c8"][1] for i in range(DEPTH))
    st = lambda f: jnp.stack([f(grads[i]) for i in range(DEPTH)])
    gs["b_mod"] = st(lambda g: g["b_mod"][0])
    gs["norm1_g"] = st(lambda g: g["norm1_g"][0])
    gs["norm2_g"] = st(lambda g: g["norm2_g"][0])
    gs["ssm_conv_w"] = st(lambda g: g["conv_w"])
    gs["ssm_conv_b"] = st(lambda g: g["conv_b"][0])
    gs["ssm_dt_bias"] = st(lambda g: (g["bias_c"][0, DT_LANE:DT_LANE + 32] + g["bias_r"][:, 0]).reshape(2, 16))
    gs["ssm_a_log"] = st(lambda g: (g["alog_c"][0, DT_LANE:DT_LANE + 32] + g["alog_r"][:, 0]).reshape(2, 16))
    gs["ssm_d"] = st(lambda g: g["d_exp"].reshape(16, 64).sum(axis=1))
    gs["ssm_norm_g"] = st(lambda g: g["ssm_norm_g"][0])
    gs["swa_q_norm_g"] = st(lambda g: g["swa_q_g"][0])
    gs["swa_k_norm_g"] = st(lambda g: g["swa_k_g"][0])
    gs["swa_sink"] = st(lambda g: g["sink"][:, 0, 0])
    gs["mla_q_lat_g"] = st(lambda g: g["q_lat_g"][0])
    gs["mla_kv_lat_g"] = st(lambda g: g["kv_lat_g"][0])
    gs["mla_q_norm_g"] = st(lambda g: g["mla_q_g"][0, :192])
    gs["mla_k_norm_g"] = st(lambda g: g["mla_k_g"][0, :192])
    return gs


def layer_params(i, full, conv_full, sm, nlat):
    p = {}
    p["w_mod"] = full["w_mod"][i]
    p["w_in"] = pack_w_in(full["w_in"][i])
    p["w_uq"] = pack_uq(full["w_mla_uq"][i])
    p["w_ukv"] = pack_ukv(full["w_mla_ukv"][i])
    for k in ("w_p_ssm", "w_p_swa", "w_p_mla", "w_out", "w_ffn_in", "w_ffn_out"):
        p[k] = full[k][i]
    p["b_mod"] = sm["b_mod"][i][None]
    p["norm1_g"] = sm["norm1_g"][i][None]
    p["norm2_g"] = sm["norm2_g"][i][None]
    p["conv_w"] = conv_full[i]
    p["conv_b"] = sm["ssm_conv_b"][i][None]
    bias = sm["ssm_dt_bias"][i].reshape(32)
    alog = sm["ssm_a_log"][i].reshape(32)
    p["bias_c"] = _lanes(bias, DT_LANE)
    p["alog_c"] = _lanes(alog, DT_LANE)
    p["bias_r"] = bias[:, None]
    p["alog_r"] = alog[:, None]
    p["d_exp"] = jnp.repeat(sm["ssm_d"][i], 64)[None]
    p["ssm_norm_g"] = sm["ssm_norm_g"][i][None]
    p["swa_q_g"] = sm["swa_q_norm_g"][i][None]
    p["swa_k_g"] = sm["swa_k_norm_g"][i][None]
    p["sink"] = jnp.broadcast_to(sm["swa_sink"][i][:, None, None], (SWA_HQ, 1, 128))
    p["q_lat_g"] = sm["mla_q_lat_g"][i][None]
    p["kv_lat_g"] = sm["mla_kv_lat_g"][i][None]
    p["mla_q_g"] = _lanes(sm["mla_q_norm_g"][i], 0, 256)
    p["mla_k_g"] = _lanes(sm["mla_k_norm_g"][i], 0, 256)
    p["swa_cfg"] = dict(w=128, vw=128, hq=SWA_HQ, grp=SWA_HQ // SWA_HKV, vcol0=C_V // 128, scale=SWA_DH ** -0.5,
                        tq=256, tk=256, band=True)
    return p


def kernel(x, c, ctx, c_ctx, w_mod, b_mod, norm1_g, norm2_g, w_in, ssm_conv_w, ssm_conv_b, ssm_dt_bias, ssm_a_log, ssm_d, ssm_norm_g, swa_q_norm_g, swa_k_norm_g, swa_sink, mla_q_lat_g, mla_kv_lat_g, w_mla_uq, w_mla_ukv, mla_q_norm_g, mla_k_norm_g, w_p_ssm, w_p_swa, w_p_mla, w_out, w_ffn_in, w_ffn_out, loss_target, m_c_ctx, m_w_mod, m_b_mod, m_norm1_g, m_norm2_g, m_w_in, m_ssm_conv_w, m_ssm_conv_b, m_ssm_dt_bias, m_ssm_a_log, m_ssm_d, m_ssm_norm_g, m_swa_q_norm_g, m_swa_k_norm_g, m_swa_sink, m_mla_q_lat_g, m_mla_kv_lat_g, m_w_mla_uq, m_w_mla_ukv, m_mla_q_norm_g, m_mla_k_norm_g, m_w_p_ssm, m_w_p_swa, m_w_p_mla, m_w_out, m_w_ffn_in, m_w_ffn_out, v_c_ctx, v_w_mod, v_b_mod, v_norm1_g, v_norm2_g, v_w_in, v_ssm_conv_w, v_ssm_conv_b, v_ssm_dt_bias, v_ssm_a_log, v_ssm_d, v_ssm_norm_g, v_swa_q_norm_g, v_swa_k_norm_g, v_swa_sink, v_mla_q_lat_g, v_mla_kv_lat_g, v_w_mla_uq, v_w_mla_ukv, v_mla_q_norm_g, v_mla_k_norm_g, v_w_p_ssm, v_w_p_swa, v_w_p_mla, v_w_out, v_w_ffn_in, v_w_ffn_out):
    loc = dict(locals())
    w = {k: loc[k] for k in WEIGHTS}
    m = {k: loc["m_" + k] for k in WEIGHTS}
    v = {k: loc["v_" + k] for k in WEIGHTS}
    nlat = x.shape[1]

    gathered = _unpack_big(gather_chips("gather_weights", _pack_big(w, BF16)), (4,))
    full = {k: _full_from_chips(k, gathered[k]) for k in BIG}
    conv_sh = jnp.pad(ssm_conv_w.reshape(10, 384), ((0, 6), (0, 0)))
    conv_full = gather_chips("gather_conv", conv_sh)[:, :10].reshape(4, 2, 5, 384).transpose(1, 2, 0, 3).reshape(2, 5, 1536)

    W = [layer_params(i, full, conv_full, w, nlat) for i in range(DEPTH)]

    loss_loc, dx, grads = local_step(x[0], c, ctx[0], loss_target[0], c_ctx, W, nlat)

    gfull = big_grads(grads)
    by_chip = {k: jnp.stack([_chips_from_full(k, a) for a in gfull[k]], axis=1) for k in BIG}
    parts = [by_chip[k].astype(BF16).reshape(4, -1, 1024) for k in BIG]
    used = sum(p.shape[1] for p in parts)
    send = jnp.concatenate(parts + [jnp.zeros((4, PACK_ROWS - used, 1024), BF16)], axis=1)
    own, got = pair_split("pair_split", send)
    pair = add_cast("pair_sum", own, got, BF16)
    recv = scatter_chips("scatter_grads", pair)
    mine = sum_blocks("sum_chips", recv)
    gbig = _unpack_big(pair_join("join_cores", mine), ())

    gs = small_grads(grads)
    small_all = gather_all("gather_small", _pack_small(gs))
    small_sum = sum_blocks("sum_small", small_all)
    full_shapes = {k: (w[k].shape if k != "ssm_conv_w" else (2, 5, 1536)) for k in SMALL}
    gsmall = _unpack_small(small_sum, full_shapes)
    chip = 2 * lax.axis_index("x") + lax.axis_index("y")
    gsmall["ssm_conv_w"] = lax.dynamic_slice_in_dim(gsmall["ssm_conv_w"], chip * 384, 384, axis=2)

    grad = {**gbig, **gsmall}
    delta, new_m, new_v = {}, {}, {}
    sm = {k: _pack_small_local(d) for k, d in (("w", w), ("g", grad), ("m", m), ("v", v))}
    r = adamw("adamw_small", sm["w"], sm["g"], sm["m"], sm["v"])
    shapes = {k: w[k].shape for k in SMALL}
    for dst, buf in zip((delta, new_m, new_v), r):
        dst.update(_unpack_small_local(buf, shapes))
    for k in BIG:
        sh = w[k].shape
        r = adamw("adamw_" + k, *[a[k].reshape(sh[0] * sh[1], sh[2]) for a in (w, grad, m, v)])
        for dst, buf in zip((delta, new_m, new_v), r):
            dst[k] = buf.reshape(sh)

    loss = lax.psum(loss_loc, ("x", "y", "c"))
    return (loss, dx[None, :nlat], *[grad[k] for k in WEIGHTS], *[delta[k] for k in WEIGHTS],
            *[new_m[k] for k in WEIGHTS], *[new_v[k] for k in WEIGHTS])


def _pack_small_local(d):
    parts = []
    for k in SMALL:
        a = d[k].astype(F32).reshape(-1)
        parts.append(jnp.pad(a, (0, (-a.shape[0]) % 1024)))
    return jnp.concatenate(parts).reshape(-1, 128)


def _unpack_small_local(buf, shapes):
    flat = buf.reshape(-1)
    out = {}
    o = 0
    for k in SMALL:
        sz = math.prod(shapes[k])
        out[k] = flat[o:o + sz].reshape(shapes[k])
        o += sz + (-sz) % 1024
    return out
```

```python
import functools
import math

import jax
import jax.numpy as jnp
from jax import lax
from jax.experimental import pallas as pl
from jax.experimental.pallas import tpu as pltpu

F32 = jnp.float32
BF16 = jnp.bfloat16
MESH = pl.DeviceIdType.MESH

D = 1024
NCTX = 256
EPS = 1e-6
ROPE_BASE = 10000.0
GRID_W = 64
DEPTH = 2
Q = 128
N_HEADS_SSM = 16
SWA_HQ, SWA_HKV, SWA_DH, SWA_WIN = 8, 2, 128, 128
MLA_H, MLA_NOPE, MLA_ROPE, MLA_V = 8, 128, 64, 128
MLA_QRANK, MLA_KVRANK = 384, 256
FFN = 2816
RT = 256
VMEM_LIMIT = 56 << 20
NEG = -1e30
LOG2E = 1.4426950408889634

C_G1, C_G2, C_G3, C_Z, C_Q, C_XS, C_B, C_C, C_K, C_V, C_CKV, C_MISC, C_PAD, C_CQ = (
    0, 1024, 2048, 3072, 4096, 5120, 6144, 6400, 6656, 6912, 7168, 7424, 7552, 7680)
UW = 8064
DT_LANE = 64

ADAM_LR, ADAM_B1, ADAM_B2, ADAM_EPS, ADAM_WD, ADAM_STEP = 0.001, 0.9, 0.999, 1e-08, 0.01, 10


def _cp(sem):
    return pltpu.CompilerParams(dimension_semantics=sem, vmem_limit_bytes=VMEM_LIMIT)


def _pick(n, cands):
    for c in cands:
        if n % c == 0:
            return c
    return n


_TN = (1536, 1408, 1152, 1024, 896, 768, 512, 384, 256, 128)


def mm(a, b, out_dtype, name, trans_b=False):
    m, k = a.shape
    n = b.shape[0] if trans_b else b.shape[1]
    tm = _pick(m, (768, 512, 256, 128, 8))
    tn = _pick(n, _TN)
    tk = k if k <= 2048 else _pick(k, (1408, 1152, 1024, 896, 768, 512))
    nk = k // tk
    b_spec = (pl.BlockSpec((tn, tk), lambda i, j, kk: (j, kk)) if trans_b
              else pl.BlockSpec((tk, tn), lambda i, j, kk: (kk, j)))

    def body(a_ref, b_ref, o_ref, *acc):
        p = _d(a_ref[...], b_ref[...], ((1,), (1 if trans_b else 0,)))
        if nk == 1:
            o_ref[...] = p.astype(out_dtype)
        else:
            kk = pl.program_id(2)

            @pl.when(kk == 0)
            def _():
                acc[0][...] = p

            @pl.when(kk > 0)
            def _():
                acc[0][...] += p

            @pl.when(kk == nk - 1)
            def _():
                o_ref[...] = acc[0][...].astype(out_dtype)

    return pl.pallas_call(
        body, out_shape=jax.ShapeDtypeStruct((m, n), out_dtype), grid=(m // tm, n // tn, nk),
        in_specs=[pl.BlockSpec((tm, tk), lambda i, j, kk: (i, kk)), b_spec],
        out_specs=pl.BlockSpec((tm, tn), lambda i, j, kk: (i, j)),
        scratch_shapes=[] if nk == 1 else [pltpu.VMEM((tm, tn), F32)],
        name=name, compiler_params=_cp(("parallel", "parallel", "arbitrary")))(a, b)


def mm_tn(a, b, name, out_dtype=BF16):
    t, ka = a.shape
    _, nb = b.shape
    ta = _pick(ka, (1024, 1408, 768, 512, 384, 256, 128))
    tb = _pick(nb, _TN)
    tt = _pick(t, (768, 512, 256, 128, 8))
    nt = t // tt

    def body(a_ref, b_ref, o_ref, acc):
        p = _d(a_ref[...], b_ref[...], ((0,), (0,)))
        s = pl.program_id(2)

        @pl.when(s == 0)
        def _():
            acc[...] = p

        @pl.when(s > 0)
        def _():
            acc[...] += p

        @pl.when(s == nt - 1)
        def _():
            o_ref[...] = acc[...].astype(out_dtype)

    return pl.pallas_call(
        body, out_shape=jax.ShapeDtypeStruct((ka, nb), out_dtype), grid=(ka // ta, nb // tb, nt),
        in_specs=[pl.BlockSpec((tt, ta), lambda i, j, s: (s, i)), pl.BlockSpec((tt, tb), lambda i, j, s: (s, j))],
        out_specs=pl.BlockSpec((ta, tb), lambda i, j, s: (i, j)), scratch_shapes=[pltpu.VMEM((ta, tb), F32)],
        name=name, compiler_params=_cp(("parallel", "parallel", "arbitrary")))(a, b)


def _rms(x, g, n=None):
    n = x.shape[-1] if n is None else n
    r = lax.rsqrt(jnp.sum(x * x, axis=-1, keepdims=True) * (1.0 / n) + EPS)
    return x * r * g


def _silu(x):
    return x * jax.nn.sigmoid(x)


def _modulate(x, g, sc, sh):
    return _rms(x, g) * (1.0 + sc) + sh


def _swap(x, s):
    ax = x.ndim - 1
    w = x.shape[ax]
    lane = lax.broadcasted_iota(jnp.int32, x.shape, ax)
    lo = (lane & s) == 0
    return jnp.where(lo, pltpu.roll(x, w - s, ax), pltpu.roll(x, s, ax))


@functools.partial(jax.custom_vjp, nondiff_argnums=(3,))
def _rope(x, cos, sin, s):
    return x * cos + _swap(x, s) * sin


def _rope_fwd(x, cos, sin, s):
    return _rope(x, cos, sin, s), (cos, sin)


def _rope_bwd(s, res, g):
    cos, sin = res
    return g * cos - _swap(g, s) * sin, jnp.zeros_like(cos), jnp.zeros_like(sin)


_rope.defvjp(_rope_fwd, _rope_bwd)


@jax.custom_vjp
def _softplus(x):
    return jnp.maximum(x, 0.0) + jnp.log(1.0 + jnp.exp(-jnp.abs(x)))


def _softplus_fwd(x):
    return _softplus(x), x


def _softplus_bwd(x, g):
    return (g * jax.nn.sigmoid(x),)


_softplus.defvjp(_softplus_fwd, _softplus_bwd)


def _d(a, b, dims):
    return lax.dot_general(a.astype(BF16), b.astype(BF16), (dims, ((), ())), preferred_element_type=F32)


@jax.custom_vjp
def bdot(a, b):
    return _d(a, b, ((1,), (0,)))


bdot.defvjp(lambda a, b: (bdot(a, b), (a, b)),
            lambda r, g: (_d(g, r[1], ((1,), (1,))), _d(r[0], g, ((0,), (0,)))))


@jax.custom_vjp
def bdot_nt(a, b):
    return _d(a, b, ((1,), (1,)))


bdot_nt.defvjp(lambda a, b: (bdot_nt(a, b), (a, b)),
               lambda r, g: (_d(g, r[1], ((1,), (0,))), _d(g, r[0], ((0,), (0,)))))


@jax.custom_vjp
def bdot_tn(a, b):
    return _d(a, b, ((0,), (0,)))


bdot_tn.defvjp(lambda a, b: (bdot_tn(a, b), (a, b)),
               lambda r, g: (_d(r[1], g, ((1,), (1,))), _d(r[0], g, ((1,), (0,)))))


def _tri(rev):
    i = lax.broadcasted_iota(jnp.int32, (Q, Q), 0)
    j = lax.broadcasted_iota(jnp.int32, (Q, Q), 1)
    return (i <= j) if rev else (i >= j)


def _split3(a):
    hi = a.astype(BF16)
    r = a - hi.astype(F32)
    mid = r.astype(BF16)
    lo = (r - mid.astype(F32)).astype(BF16)
    return hi, mid, lo


def _cum_cols_impl(a, rev):
    t = _tri(rev).astype(BF16)
    return sum(jnp.dot(t, p, preferred_element_type=F32) for p in _split3(a))


def _cum_rows_impl(a, rev):
    t = _tri(not rev).astype(BF16)
    return sum(jnp.dot(p, t, preferred_element_type=F32) for p in _split3(a))


@functools.partial(jax.custom_vjp, nondiff_argnums=(1,))
def cum_cols(a, rev):
    return _cum_cols_impl(a, rev)


cum_cols.defvjp(lambda a, rev: (_cum_cols_impl(a, rev), None), lambda rev, _, g: (_cum_cols_impl(g, not rev),))


@functools.partial(jax.custom_vjp, nondiff_argnums=(1,))
def cum_rows(a, rev):
    return _cum_rows_impl(a, rev)


cum_rows.defvjp(lambda a, rev: (_cum_rows_impl(a, rev), None), lambda rev, _, g: (_cum_rows_impl(g, not rev),))


def _rs(w, cb=0):
    return pl.BlockSpec((RT, w), lambda i: (i, cb))


def _ps(shape):
    nd = len(shape)
    return pl.BlockSpec(shape, lambda i: (0,) * nd)


def _gs(w, cb, nlat):
    return pl.BlockSpec((1, 1, w), lambda i: (i // nlat, 0, cb))


def _rowcall(name, body, n, ins, outs, scratch=()):
    return pl.pallas_call(
        body, out_shape=[o[0] for o in outs], grid=(n // RT,), in_specs=[s for _, s in ins],
        out_specs=[s for _, s in outs], scratch_shapes=list(scratch), name=name,
        compiler_params=_cp(("arbitrary",)))(*[a for a, _ in ins])


def _acc(ref, val, first):
    @pl.when(first)
    def _():
        ref[...] = val

    @pl.when(jnp.logical_not(first))
    def _():
        ref[...] += val


def _sd(shape, dt):
    return jax.ShapeDtypeStruct(shape, dt)


def resid_mod_fwd(name, xp, o, mod_gt, gt_i, mod_n, sh_i, sc_i, norm_g, nlat):
    n = xp.shape[0]
    has_res = o is not None

    def body(*refs):
        if has_res:
            xp_ref, o_ref, gt_ref, sh_ref, sc_ref, g_ref, xn_ref, h_ref = refs
            xn = xp_ref[...] + gt_ref[0] * o_ref[...]
            xn_ref[...] = xn
        else:
            xp_ref, sh_ref, sc_ref, g_ref, h_ref = refs
            xn = xp_ref[...]
        h_ref[...] = _modulate(xn, g_ref[...], sc_ref[0], sh_ref[0]).astype(BF16)

    ins = [(xp, _rs(D))]
    if has_res:
        ins += [(o, _rs(D)), (mod_gt, _gs(D, gt_i, nlat))]
    ins += [(mod_n, _gs(D, sh_i, nlat)), (mod_n, _gs(D, sc_i, nlat)), (norm_g, _ps((1, D)))]
    outs = ([(_sd((n, D), F32), _rs(D))] if has_res else []) + [(_sd((n, D), BF16), _rs(D))]
    r = _rowcall(name, body, n, ins, outs)
    return (r[0], r[1]) if has_res else (xp, r[0])


def resid_mod_bwd(name, xn, dxn, dh, o, mod_gt, gt_i, mod_n, sh_i, sc_i, norm_g, nlat):
    n = xn.shape[0]
    has_res = o is not None

    def body(*refs):
        i = pl.program_id(0)
        if has_res:
            (xn_ref, dxn_ref, dh_ref, o_ref, gt_ref, sh_ref, sc_ref, g_ref,
             dx_ref, do_ref, dgt_ref, dsh_ref, dsc_ref, dg_ref) = refs
        else:
            xn_ref, dxn_ref, dh_ref, sh_ref, sc_ref, g_ref, dx_ref, dsh_ref, dsc_ref, dg_ref = refs
        _, vjp = jax.vjp(_modulate, xn_ref[...], g_ref[...], sc_ref[0], sh_ref[0])
        dx, dg, dsc, dsh = vjp(dh_ref[...])
        dx = dx + dxn_ref[...]
        dx_ref[...] = dx
        gfirst = (i == 0) | (i == nlat)
        _acc(dg_ref, dg, i == 0)
        _acc(dsh_ref, dsh[None], gfirst)
        _acc(dsc_ref, dsc[None], gfirst)
        if has_res:
            do_ref[...] = (gt_ref[0] * dx).astype(BF16)
            _acc(dgt_ref, jnp.sum(dx * o_ref[...], axis=0, keepdims=True)[None], gfirst)

    ins = [(xn, _rs(D)), (dxn, _rs(D)), (dh, _rs(D))]
    if has_res:
        ins += [(o, _rs(D)), (mod_gt, _gs(D, gt_i, nlat))]
    ins += [(mod_n, _gs(D, sh_i, nlat)), (mod_n, _gs(D, sc_i, nlat)), (norm_g, _ps((1, D)))]
    gacc = (_sd((2, 1, D), F32), _gs(D, 0, nlat))
    outs = [(_sd((n, D), F32), _rs(D))]
    if has_res:
        outs += [(_sd((n, D), BF16), _rs(D)), gacc]
    outs += [gacc, gacc, (_sd((1, D), F32), _ps((1, D)))]
    r = _rowcall(name, body, n, ins, outs)
    if has_res:
        return r
    return r[0], None, None, r[1], r[2], r[3]


def resid_loss(name, xp, o, mod_gt, gt_i, target, nlat):
    n = xp.shape[0]

    def body(xp_ref, o_ref, gt_ref, t_ref, loss_ref, dx_ref, do_ref, dgt_ref):
        i = pl.program_id(0)
        gt = gt_ref[0]

        @pl.when(i < nlat)
        def _():
            err = xp_ref[...] + gt * o_ref[...] - t_ref[...]
            dx = err * (1.0 / D)
            dx_ref[...] = dx
            do_ref[...] = (gt * dx).astype(BF16)
            _acc(loss_ref, jnp.full((1, 128), 0.5 / D, F32) * jnp.sum(err * err), i == 0)
            _acc(dgt_ref, jnp.sum(dx * o_ref[...], axis=0, keepdims=True)[None], i == 0)

        @pl.when(i >= nlat)
        def _():
            dx_ref[...] = jnp.zeros((RT, D), F32)
            do_ref[...] = jnp.zeros((RT, D), BF16)
            dgt_ref[...] = jnp.zeros((1, 1, D), F32)

    tgt_spec = pl.BlockSpec((RT, D), lambda i: (jnp.minimum(i, nlat - 1), 0))
    ins = [(xp, _rs(D)), (o, _rs(D)), (mod_gt, _gs(D, gt_i, nlat)), (target, tgt_spec)]
    outs = [(_sd((1, 128), F32), _ps((1, 128))), (_sd((n, D), F32), _rs(D)), (_sd((n, D), BF16), _rs(D)),
            (_sd((2, 1, D), F32), _gs(D, 0, nlat))]
    return _rowcall(name, body, n, ins, outs)


def mod_fwd(name, c8, w_mod, b_mod):
    tn = 1536

    def body(c_ref, w_ref, b_ref, o_ref, s_ref):
        s = _silu(c_ref[...]).astype(BF16)
        s_ref[...] = s
        o_ref[...] = jnp.dot(s, w_ref[...], preferred_element_type=F32) + b_ref[...]

    return pl.pallas_call(
        body, out_shape=[_sd((8, 6 * D), F32), _sd((8, D), BF16)], grid=(6 * D // tn,),
        in_specs=[pl.BlockSpec((8, D), lambda j: (0, 0)), pl.BlockSpec((D, tn), lambda j: (0, j)),
                  pl.BlockSpec((1, tn), lambda j: (0, j))],
        out_specs=[pl.BlockSpec((8, tn), lambda j: (0, j)), pl.BlockSpec((8, D), lambda j: (0, 0))],
        name=name, compiler_params=_cp(("arbitrary",)))(c8, w_mod, b_mod)


def mod_small_bwd(name, c8, dsilu, dmod8):
    def body(c_ref, ds_ref, dm_ref, dc_ref, db_ref):
        _, vjp = jax.vjp(_silu, c_ref[...])
        dc_ref[...] = vjp(ds_ref[...])[0]
        db_ref[...] = jnp.sum(dm_ref[...], axis=0, keepdims=True)

    return pl.pallas_call(
        body, out_shape=[_sd((8, D), F32), _sd((1, 6 * D), F32)], grid=(1,),
        in_specs=[pl.BlockSpec((8, D), lambda j: (0, 0)), pl.BlockSpec((8, D), lambda j: (0, 0)),
                  pl.BlockSpec((8, 6 * D), lambda j: (0, 0))],
        out_specs=[pl.BlockSpec((8, D), lambda j: (0, 0)), pl.BlockSpec((1, 6 * D), lambda j: (0, 0))],
        name=name, compiler_params=_cp(("arbitrary",)))(c8, dsilu, dmod8)


def _conv_taps(x, nlat):
    n = x.shape[0]
    r = lax.broadcasted_iota(jnp.int32, x.shape, 0)
    lo = jnp.where(r < nlat, 0, nlat)
    hi = jnp.where(r < nlat, nlat, n)
    taps = []
    for o in (-2, -1, 0, 1, 2):
        xs = x if o == 0 else pltpu.roll(x, (-o) % n, 0)
        t = r + o
        taps.append(jnp.where((t >= lo) & (t < hi), xs, 0.0))
    return taps


def conv_fwd(name, u, w, b, nlat_rows):
    n = u.shape[0]

    def body(x_ref, w_ref, b_ref, o_ref):
        taps = _conv_taps(x_ref[...], nlat_rows)
        wv = w_ref[...]
        pre = b_ref[...] + sum(taps[k] * wv[k:k + 1, :] for k in range(5))
        o_ref[...] = _silu(pre)

    return pl.pallas_call(
        body, out_shape=_sd((n, 1536), F32), grid=(12,),
        in_specs=[pl.BlockSpec((n, 128), lambda j: (0, C_XS // 128 + j)), pl.BlockSpec((5, 128), lambda j: (0, j)),
                  pl.BlockSpec((1, 128), lambda j: (0, j))],
        out_specs=pl.BlockSpec((n, 128), lambda j: (0, j)),
        name=name, compiler_params=_cp(("parallel",)))(u, w, b)


def conv_bwd(name, u, dact, w, b, nlat_rows):
    n = u.shape[0]

    def body(x_ref, da_ref, w_ref, b_ref, dx_ref, dw_ref, db_ref):
        taps = _conv_taps(x_ref[...], nlat_rows)
        wv = w_ref[...]
        pre = b_ref[...] + sum(taps[k] * wv[k:k + 1, :] for k in range(5))
        s = jax.nn.sigmoid(pre)
        dpre = da_ref[...] * (s * (1.0 + pre * (1.0 - s)))
        db_ref[...] = jnp.sum(dpre, axis=0, keepdims=True)
        rows = lax.broadcasted_iota(jnp.int32, (5, 128), 0)
        dw = jnp.zeros((5, 128), F32)
        for k in range(5):
            dw = dw + jnp.where(rows == k, jnp.sum(dpre * taps[k], axis=0, keepdims=True), 0.0)
        dw_ref[...] = dw
        r = lax.broadcasted_iota(jnp.int32, dpre.shape, 0)
        lo = jnp.where(r < nlat_rows, 0, nlat_rows)
        hi = jnp.where(r < nlat_rows, nlat_rows, n)
        dx = jnp.zeros_like(dpre)
        for k in range(5):
            o = k - 2
            ds = dpre if o == 0 else pltpu.roll(dpre, o % n, 0)
            t = r - o
            dx = dx + jnp.where((t >= lo) & (t < hi), ds, 0.0) * wv[k:k + 1, :]
        dx_ref[...] = dx.astype(BF16)

    return pl.pallas_call(
        body, out_shape=[_sd((n, 1536), BF16), _sd((5, 1536), F32), _sd((1, 1536), F32)], grid=(12,),
        in_specs=[pl.BlockSpec((n, 128), lambda j: (0, C_XS // 128 + j)), pl.BlockSpec((n, 128), lambda j: (0, j)),
                  pl.BlockSpec((5, 128), lambda j: (0, j)), pl.BlockSpec((1, 128), lambda j: (0, j))],
        out_specs=[pl.BlockSpec((n, 128), lambda j: (0, j)), pl.BlockSpec((5, 128), lambda j: (0, j)),
                   pl.BlockSpec((1, 128), lambda j: (0, j))],
        name=name, compiler_params=_cp(("parallel",)))(u, dact, w, b)


def _ssd_chunk(rev, dirn, g, x4, bm, cm, misc, dtrow, bias_c, alog_c, bias_r, alog_r, h4):
    dt_c = _softplus(misc + bias_c)
    a_c = dt_c * (-jnp.exp(alog_c))
    dt_r = _softplus(dtrow + bias_r)
    a_r = dt_r * (-jnp.exp(alog_r))
    cs_c = cum_cols(a_c, rev)
    cs_r = cum_rows(a_r, rev)
    tot_c = jnp.sum(a_c, axis=0, keepdims=True)
    cb = bdot_nt(cm, bm)
    tri = _tri(rev)
    lane = lax.broadcasted_iota(jnp.int32, (1, 128), 1)
    row16 = lax.broadcasted_iota(jnp.int32, (16, 1), 0)
    prow = lax.broadcasted_iota(jnp.int32, (128, 1), 0)
    ys, hs = [], []
    for p in range(4):
        ydiag = 0.0
        wst = 0.0
        eoff = 0.0
        hscale = 0.0
        for e in range(2):
            hg = 8 * g + 2 * p + e
            oh_c = (lane == DT_LANE + 16 * dirn + hg).astype(F32)
            dt_h = jnp.sum(dt_c * oh_c, axis=1, keepdims=True)
            cs_h = jnp.sum(cs_c * oh_c, axis=1, keepdims=True)
            tot_h = jnp.sum(tot_c * oh_c, axis=1, keepdims=True)
            csr_h = jnp.sum(cs_r * (row16 == hg).astype(F32), axis=0, keepdims=True)
            seg = jnp.exp(jnp.where(tri, cs_h - csr_h, -jnp.inf))
            hm = ((lane < 64) if e == 0 else (lane >= 64)).astype(F32)
            ydiag = ydiag + bdot(cb * seg, x4[p] * (dt_h * hm))
            wst = wst + (dt_h * jnp.exp(tot_h - cs_h)) * hm
            eoff = eoff + jnp.exp(cs_h) * hm
            hscale = hscale + jnp.exp(tot_h) * ((prow < 64) if e == 0 else (prow >= 64)).astype(F32)
        ys.append(ydiag + bdot_nt(cm, h4[p]) * eoff)
        hs.append(h4[p] * hscale + bdot_tn(x4[p] * wst, bm))
    return ys, hs


def _ssd_specs(nlat_chunks, rev, dirn, bwd):
    nc = nlat_chunks + 2

    def chunk(s):
        if bwd:
            s = nc - 1 - s
        return (nlat_chunks + 1 - s) if rev else (s + nlat_chunks) % nc

    def step(s):
        return (nc - 1 - s) if bwd else s

    return dict(
        x=pl.BlockSpec((Q, 512), lambda g, s: (chunk(s), g)),
        b=pl.BlockSpec((Q, 128), lambda g, s: (chunk(s), 8 + g)),
        c=pl.BlockSpec((Q, 128), lambda g, s: (chunk(s), 10 + g)),
        misc=pl.BlockSpec((Q, 128), lambda g, s: (chunk(s), C_MISC // 128)),
        dtrow=pl.BlockSpec((16, Q), lambda g, s: (dirn, chunk(s))),
        p_c=pl.BlockSpec((1, 128), lambda g, s: (0, 0)),
        p_r=pl.BlockSpec((16, 1), lambda g, s: (dirn, 0)),
        y=pl.BlockSpec((Q, 512), lambda g, s: (chunk(s), g)),
        hsave=pl.BlockSpec((1, 1, 512, 128), lambda g, s: (g, step(s), 0, 0)),
        bc_out=pl.BlockSpec((Q, 128), lambda g, s: (chunk(s), g)),
        misc_out=pl.BlockSpec((1, Q, 128), lambda g, s: (g, chunk(s), 0)),
        dtrow_out=pl.BlockSpec((1, 16, Q), lambda g, s: (g, 0, chunk(s))),
        pacc_c=pl.BlockSpec((1, 128), lambda g, s: (0, 0)),
        pacc_r=pl.BlockSpec((16, 1), lambda g, s: (0, 0)),
    )


def ssd_fwd(name, xbc, u, dtrow, bias_c, alog_c, bias_r, alog_r, nlat_chunks, rev, dirn):
    n = xbc.shape[0]
    nc = nlat_chunks + 2
    sp = _ssd_specs(nlat_chunks, rev, dirn, False)

    def body(x_ref, b_ref, c_ref, m_ref, r_ref, bc_ref, ac_ref, br_ref, ar_ref, y_ref, hs_ref, h_s):
        g = pl.program_id(0)
        s = pl.program_id(1)

        @pl.when(s == 0)
        def _():
            h_s[...] = jnp.zeros((512, 128), F32)

        hs_ref[0, 0] = h_s[...]
        x4 = [x_ref[:, 128 * p:128 * p + 128] for p in range(4)]
        h4 = [h_s[128 * p:128 * p + 128, :] for p in range(4)]
        ys, hs = _ssd_chunk(rev, dirn, g, x4, b_ref[...], c_ref[...], m_ref[...], r_ref[...],
                            bc_ref[...], ac_ref[...], br_ref[...], ar_ref[...], h4)
        for p in range(4):
            y_ref[:, 128 * p:128 * p + 128] = ys[p]
            h_s[128 * p:128 * p + 128, :] = hs[p]

    return pl.pallas_call(
        body, out_shape=[_sd((n, 1024), F32), _sd((2, nc, 512, 128), F32)], grid=(2, nc),
        in_specs=[sp["x"], sp["b"], sp["c"], sp["misc"], sp["dtrow"], sp["p_c"], sp["p_c"], sp["p_r"], sp["p_r"]],
        out_specs=[sp["y"], sp["hsave"]], scratch_shapes=[pltpu.VMEM((512, 128), F32)],
        name=name, compiler_params=_cp(("arbitrary", "arbitrary")))(
            xbc, xbc, xbc, u, dtrow, bias_c, alog_c, bias_r, alog_r)


def ssd_bwd(name, xbc, u, dtrow, bias_c, alog_c, bias_r, alog_r, hsave, dy, acc, nlat_chunks, rev, dirn):
    n = xbc.shape[0]
    sp = _ssd_specs(nlat_chunks, rev, dirn, True)

    def body(x_ref, b_ref, c_ref, m_ref, r_ref, bc_ref, ac_ref, br_ref, ar_ref, hs_ref, dy_ref, ax_ref, ab_ref, acc_ref,
             dx_ref, db_ref, dc_ref, dm_ref, dr_ref, dbc_ref, dac_ref, dbr_ref, dar_ref, dh_s):
        g = pl.program_id(0)
        s = pl.program_id(1)

        @pl.when(s == 0)
        def _():
            dh_s[...] = jnp.zeros((512, 128), F32)

        x4 = [x_ref[:, 128 * p:128 * p + 128] for p in range(4)]
        h4 = [hs_ref[0, 0, 128 * p:128 * p + 128, :] for p in range(4)]
        fn = functools.partial(_ssd_chunk, rev, dirn, g)
        _, vjp = jax.vjp(fn, x4, b_ref[...], c_ref[...], m_ref[...], r_ref[...],
                         bc_ref[...], ac_ref[...], br_ref[...], ar_ref[...], h4)
        dys = [dy_ref[:, 128 * p:128 * p + 128] for p in range(4)]
        dhs = [dh_s[128 * p:128 * p + 128, :] for p in range(4)]
        dx4, db, dc, dm, dr, dbc, dac, dbr, dar, dh4 = vjp((dys, dhs))
        for p in range(4):
            dx_ref[:, 128 * p:128 * p + 128] = dx4[p] + ax_ref[:, 128 * p:128 * p + 128]
            dh_s[128 * p:128 * p + 128, :] = dh4[p]
        db_ref[...] = db + ab_ref[...]
        dc_ref[...] = dc + acc_ref[...]
        dm_ref[0] = dm
        dr_ref[0] = dr
        first = (g == 0) & (s == 0)
        _acc(dbc_ref, dbc, first)
        _acc(dac_ref, dac, first)
        _acc(dbr_ref, dbr, first)
        _acc(dar_ref, dar, first)

    ax, ab, ac = acc
    return pl.pallas_call(
        body,
        out_shape=[_sd((n, 1024), F32), _sd((n, 256), F32), _sd((n, 256), F32), _sd((2, n, 128), F32),
                   _sd((2, 16, n), F32), _sd((1, 128), F32), _sd((1, 128), F32), _sd((16, 1), F32), _sd((16, 1), F32)],
        grid=(2, nlat_chunks + 2),
        in_specs=[sp["x"], sp["b"], sp["c"], sp["misc"], sp["dtrow"], sp["p_c"], sp["p_c"], sp["p_r"], sp["p_r"],
                  sp["hsave"], sp["y"], sp["y"], sp["bc_out"], sp["bc_out"]],
        out_specs=[sp["y"], sp["bc_out"], sp["bc_out"], sp["misc_out"], sp["dtrow_out"],
                   sp["pacc_c"], sp["pacc_c"], sp["pacc_r"], sp["pacc_r"]],
        scratch_shapes=[pltpu.VMEM((512, 128), F32)],
        name=name, compiler_params=_cp(("arbitrary", "arbitrary")))(
            xbc, xbc, xbc, u, dtrow, bias_c, alog_c, bias_r, alog_r, hsave, dy, ax, ab, ac)


def _ssd_out(yf, yb, xs, z, g, dexp):
    return _rms((yf + yb + dexp * xs) * _silu(z), g)


def ssd_out_fwd(name, yf, yb, xbc, u, g, dexp):
    n = yf.shape[0]

    def body(yf_ref, yb_ref, xs_ref, z_ref, g_ref, d_ref, o_ref):
        o_ref[...] = _ssd_out(yf_ref[...], yb_ref[...], xs_ref[...], z_ref[...], g_ref[...], d_ref[...]).astype(BF16)

    return _rowcall(name, body, n,
                    [(yf, _rs(D)), (yb, _rs(D)), (xbc, _rs(D, 0)), (u, _rs(D, C_Z // D)), (g, _ps((1, D))), (dexp, _ps((1, D)))],
                    [(_sd((n, D), BF16), _rs(D))])[0]


def ssd_out_bwd(name, yf, yb, xbc, u, g, dexp, dys):
    n = yf.shape[0]

    def body(yf_ref, yb_ref, xs_ref, z_ref, g_ref, d_ref, dys_ref, dy_ref, dxs_ref, dz_ref, dg_ref, dd_ref):
        i = pl.program_id(0)
        _, vjp = jax.vjp(_ssd_out, yf_ref[...], yb_ref[...], xs_ref[...], z_ref[...], g_ref[...], d_ref[...])
        dyf, _, dxs, dz, dg, dd = vjp(dys_ref[...])
        dy_ref[...] = dyf
        dxs_ref[...] = dxs
        dz_ref[...] = dz.astype(BF16)
        _acc(dg_ref, dg, i == 0)
        _acc(dd_ref, dd, i == 0)

    return _rowcall(name, body, n,
                    [(yf, _rs(D)), (yb, _rs(D)), (xbc, _rs(D, 0)), (u, _rs(D, C_Z // D)), (g, _ps((1, D))), (dexp, _ps((1, D))),
                     (dys, _rs(D))],
                    [(_sd((n, D), F32), _rs(D)), (_sd((n, D), F32), _rs(D)), (_sd((n, D), BF16), _rs(D)),
                     (_sd((1, D), F32), _ps((1, D))), (_sd((1, D), F32), _ps((1, D)))])


def _normrope(x, g, cos, sin, s, n=None):
    return _rope(_rms(x, g, n), cos, sin, s)


def swa_prep_fwd(name, u, gq, gk, cos, sin):
    n = u.shape[0]

    def body(q_ref, k_ref, gq_ref, gk_ref, cos_ref, sin_ref, qs_ref, ks_ref):
        cs, sn = cos_ref[...], sin_ref[...]
        for h in range(SWA_HQ):
            sl = slice(128 * h, 128 * h + 128)
            qs_ref[:, sl] = _normrope(q_ref[:, sl], gq_ref[...], cs, sn, 32).astype(BF16)
        for h in range(SWA_HKV):
            sl = slice(128 * h, 128 * h + 128)
            ks_ref[:, sl] = _normrope(k_ref[:, sl], gk_ref[...], cs, sn, 32).astype(BF16)

    return _rowcall(name, body, n,
                    [(u, _rs(1024, C_Q // 1024)), (u, _rs(256, C_K // 256)), (gq, _ps((1, 128))), (gk, _ps((1, 128))),
                     (cos, _rs(128)), (sin, _rs(128))],
                    [(_sd((n, 1024), BF16), _rs(1024)), (_sd((n, 256), BF16), _rs(256))])


def swa_prep_bwd(name, u, gq, gk, cos, sin, dqs, dks, dv):
    n = u.shape[0]

    def body(q_ref, k_ref, gq_ref, gk_ref, cos_ref, sin_ref, dqs_ref, dks_ref, dv_ref,
             dq_ref, dk_ref, dvo_ref, dgq_ref, dgk_ref):
        i = pl.program_id(0)
        cs, sn = cos_ref[...], sin_ref[...]
        fn = lambda x, g: _normrope(x, g, cs, sn, 32)
        dgq = jnp.zeros((1, 128), F32)
        dgk = jnp.zeros((1, 128), F32)
        for h in range(SWA_HQ):
            sl = slice(128 * h, 128 * h + 128)
            _, vjp = jax.vjp(fn, q_ref[:, sl], gq_ref[...])
            dx, dg = vjp(dqs_ref[:, sl])
            dq_ref[:, sl] = dx.astype(BF16)
            dgq = dgq + dg
        for h in range(SWA_HKV):
            sl = slice(128 * h, 128 * h + 128)
            _, vjp = jax.vjp(fn, k_ref[:, sl], gk_ref[...])
            dx, dg = vjp(dks_ref[:, sl])
            dk_ref[:, sl] = dx.astype(BF16)
            dgk = dgk + dg
        dvo_ref[...] = dv_ref[...].astype(BF16)
        _acc(dgq_ref, dgq, i == 0)
        _acc(dgk_ref, dgk, i == 0)

    return _rowcall(name, body, n,
                    [(u, _rs(1024, C_Q // 1024)), (u, _rs(256, C_K // 256)), (gq, _ps((1, 128))), (gk, _ps((1, 128))),
                     (cos, _rs(128)), (sin, _rs(128)), (dqs, _rs(1024)), (dks, _rs(256)), (dv, _rs(256))],
                    [(_sd((n, 1024), BF16), _rs(1024)), (_sd((n, 256), BF16), _rs(256)), (_sd((n, 256), BF16), _rs(256)),
                     (_sd((1, 128), F32), _ps((1, 128))), (_sd((1, 128), F32), _ps((1, 128)))])


def lat_norm_fwd(name, u, g_kv, g_q):
    n = u.shape[0]

    def body(ckv_ref, cq_ref, gkv_ref, gq_ref, okv_ref, oq_ref):
        okv_ref[...] = _rms(ckv_ref[...], gkv_ref[...]).astype(BF16)
        oq_ref[...] = _rms(cq_ref[...], gq_ref[...]).astype(BF16)

    return _rowcall(name, body, n,
                    [(u, _rs(256, C_CKV // 256)), (u, _rs(384, C_CQ // 384)), (g_kv, _ps((1, 256))), (g_q, _ps((1, 384)))],
                    [(_sd((n, 256), BF16), _rs(256)), (_sd((n, 384), BF16), _rs(384))])


def lat_norm_bwd(name, u, g_kv, g_q, dkvn, dqn):
    n = u.shape[0]

    def body(ckv_ref, cq_ref, gkv_ref, gq_ref, dkvn_ref, dqn_ref, dckv_ref, dcq_ref, dgkv_ref, dgq_ref):
        i = pl.program_id(0)
        _, vjp = jax.vjp(_rms, ckv_ref[...], gkv_ref[...])
        dx, dg = vjp(dkvn_ref[...])
        dckv_ref[...] = dx.astype(BF16)
        _acc(dgkv_ref, dg, i == 0)
        _, vjp = jax.vjp(_rms, cq_ref[...], gq_ref[...])
        dx, dg = vjp(dqn_ref[...])
        dcq_ref[...] = dx.astype(BF16)
        _acc(dgq_ref, dg, i == 0)

    return _rowcall(name, body, n,
                    [(u, _rs(256, C_CKV // 256)), (u, _rs(384, C_CQ // 384)), (g_kv, _ps((1, 256))), (g_q, _ps((1, 384))),
                     (dkvn, _rs(256)), (dqn, _rs(384))],
                    [(_sd((n, 256), BF16), _rs(256)), (_sd((n, 384), BF16), _rs(384)),
                     (_sd((1, 256), F32), _ps((1, 256))), (_sd((1, 384), F32), _ps((1, 384)))])


def _lane_lt64(x):
    return (lax.broadcasted_iota(jnp.int32, (1, 128), 1) < 64).astype(F32) * x


def _mla_krope(misc, g, cos, sin):
    return _normrope(_lane_lt64(misc), g, cos, sin, 16, MLA_ROPE)


def mla_prep_fwd(name, kv, qp, u, qg, kg, cos, sin):
    n = kv.shape[0]

    def body(kv_ref, v_ref, q_ref, m_ref, qg_ref, kg_ref, cos_ref, sin_ref, km_ref, qm_ref, vm_ref):
        cs, sn = cos_ref[...], sin_ref[...]
        vm_ref[...] = v_ref[...].astype(BF16)
        kr = _mla_krope(m_ref[...], kg_ref[:, 128:256], cs, sn).astype(BF16)
        for h in range(MLA_H):
            km_ref[:, 256 * h:256 * h + 128] = _rms(kv_ref[:, 128 * h:128 * h + 128], kg_ref[:, 0:128]).astype(BF16)
            km_ref[:, 256 * h + 128:256 * h + 256] = kr
            qm_ref[:, 256 * h:256 * h + 128] = _rms(q_ref[:, 256 * h:256 * h + 128], qg_ref[:, 0:128]).astype(BF16)
            qm_ref[:, 256 * h + 128:256 * h + 256] = _normrope(
                q_ref[:, 256 * h + 128:256 * h + 256], qg_ref[:, 128:256], cs, sn, 16, MLA_ROPE).astype(BF16)

    return _rowcall(name, body, n,
                    [(kv, _rs(1024, 0)), (kv, _rs(1024, 1)), (qp, _rs(2048)), (u, _rs(128, C_MISC // 128)), (qg, _ps((1, 256))),
                     (kg, _ps((1, 256))), (cos, _rs(128)), (sin, _rs(128))],
                    [(_sd((n, 2048), BF16), _rs(2048)), (_sd((n, 2048), BF16), _rs(2048)), (_sd((n, 1024), BF16), _rs(1024))])


def mla_prep_bwd(name, kv, qp, u, qg, kg, cos, sin, dkm, dqm, dv):
    n = kv.shape[0]

    def body(kv_ref, q_ref, m_ref, qg_ref, kg_ref, cos_ref, sin_ref, dkm_ref, dqm_ref, dv_ref,
             dkv_ref, dq_ref, dkr_ref, dqg_ref, dkg_ref):
        i = pl.program_id(0)
        cs, sn = cos_ref[...], sin_ref[...]
        fr = lambda x, g: _normrope(x, g, cs, sn, 16, MLA_ROPE)
        dkg_n = jnp.zeros((1, 128), F32)
        dqg_n = jnp.zeros((1, 128), F32)
        dqg_r = jnp.zeros((1, 128), F32)
        dkr_sum = jnp.zeros((RT, 128), F32)
        for h in range(MLA_H):
            _, vjp = jax.vjp(_rms, kv_ref[:, 128 * h:128 * h + 128], kg_ref[:, 0:128])
            dx, dg = vjp(dkm_ref[:, 256 * h:256 * h + 128])
            dkv_ref[:, 128 * h:128 * h + 128] = dx.astype(BF16)
            dkg_n = dkg_n + dg
            dkr_sum = dkr_sum + dkm_ref[:, 256 * h + 128:256 * h + 256]
            _, vjp = jax.vjp(_rms, q_ref[:, 256 * h:256 * h + 128], qg_ref[:, 0:128])
            dx, dg = vjp(dqm_ref[:, 256 * h:256 * h + 128])
            dq_ref[:, 256 * h:256 * h + 128] = dx.astype(BF16)
            dqg_n = dqg_n + dg
            _, vjp = jax.vjp(fr, q_ref[:, 256 * h + 128:256 * h + 256], qg_ref[:, 128:256])
            dx, dg = vjp(dqm_ref[:, 256 * h + 128:256 * h + 256])
            dq_ref[:, 256 * h + 128:256 * h + 256] = dx.astype(BF16)
            dqg_r = dqg_r + dg
        _, vjp = jax.vjp(lambda m, g: _mla_krope(m, g, cs, sn), m_ref[...], kg_ref[:, 128:256])
        dm, dkg_r = vjp(dkr_sum)
        dkr_ref[...] = dm
        dkv_ref[:, 1024:2048] = dv_ref[...].astype(BF16)
        _acc(dqg_ref.at[:, 0:128], dqg_n, i == 0)
        _acc(dqg_ref.at[:, 128:256], dqg_r, i == 0)
        _acc(dkg_ref.at[:, 0:128], dkg_n, i == 0)
        _acc(dkg_ref.at[:, 128:256], dkg_r, i == 0)

    return _rowcall(name, body, n,
                    [(kv, _rs(1024, 0)), (qp, _rs(2048)), (u, _rs(128, C_MISC // 128)), (qg, _ps((1, 256))), (kg, _ps((1, 256))),
                     (cos, _rs(128)), (sin, _rs(128)), (dkm, _rs(2048)), (dqm, _rs(2048)), (dv, _rs(1024))],
                    [(_sd((n, 2048), BF16), _rs(2048)), (_sd((n, 2048), BF16), _rs(2048)), (_sd((n, 128), F32), _rs(128)),
                     (_sd((1, 256), F32), _ps((1, 256))), (_sd((1, 256), F32), _ps((1, 256)))])


def misc_combine(name, dkr, dm_f, dm_b, drow_t):
    n = dkr.shape[0]

    def body(a_ref, f_ref, b_ref, r_ref, o_ref):
        o_ref[...] = (a_ref[...] + f_ref[0] + f_ref[1] + b_ref[0] + b_ref[1] + r_ref[...]).astype(BF16)

    g2 = pl.BlockSpec((2, RT, 128), lambda i: (0, i, 0))
    return _rowcall(name, body, n, [(dkr, _rs(128)), (dm_f, g2), (dm_b, g2), (drow_t, _rs(128))],
                    [(_sd((n, 128), BF16), _rs(128))])[0]


def _merge(g1, g2, g3, p1, p2, p3):
    return jax.nn.sigmoid(g1) * p1 + jax.nn.sigmoid(g2) * p2 + jax.nn.sigmoid(g3) * p3


def merge_fwd(name, u, p1, p2, p3):
    n = u.shape[0]

    def body(g1, g2, g3, a, b, c, o_ref):
        o_ref[...] = _merge(g1[...], g2[...], g3[...], a[...], b[...], c[...]).astype(BF16)

    return _rowcall(name, body, n, [(u, _rs(D, 0)), (u, _rs(D, 1)), (u, _rs(D, 2)), (p1, _rs(D)), (p2, _rs(D)), (p3, _rs(D))],
                    [(_sd((n, D), BF16), _rs(D))])[0]


def merge_bwd(name, u, p1, p2, p3, dm):
    n = u.shape[0]

    def body(g1, g2, g3, a, b, c, dm_ref, d1, d2, d3, dg_ref):
        _, vjp = jax.vjp(_merge, g1[...], g2[...], g3[...], a[...], b[...], c[...])
        r = vjp(dm_ref[...])
        for k in range(3):
            dg_ref[:, D * k:D * k + D] = r[k].astype(BF16)
        d1[...] = r[3].astype(BF16)
        d2[...] = r[4].astype(BF16)
        d3[...] = r[5].astype(BF16)

    return _rowcall(name, body, n,
                    [(u, _rs(D, 0)), (u, _rs(D, 1)), (u, _rs(D, 2)), (p1, _rs(D)), (p2, _rs(D)), (p3, _rs(D)), (dm, _rs(D))],
                    [(_sd((n, D), BF16), _rs(D))] * 3 + [(_sd((n, 3 * D), BF16), _rs(3 * D))])


def _swiglu(g, u):
    return _silu(g) * u


def swiglu_fwd(name, gu):
    n = gu.shape[0]

    def body(g_ref, u_ref, o_ref):
        o_ref[...] = _swiglu(g_ref[...], u_ref[...]).astype(BF16)

    return _rowcall(name, body, n, [(gu, _rs(FFN, 0)), (gu, _rs(FFN, 1))], [(_sd((n, FFN), BF16), _rs(FFN))])[0]


def swiglu_bwd(name, gu, da):
    n = gu.shape[0]

    def body(g_ref, u_ref, da_ref, o_ref):
        _, vjp = jax.vjp(_swiglu, g_ref[...], u_ref[...])
        dg, du = vjp(da_ref[...])
        o_ref[:, 0:FFN] = dg.astype(BF16)
        o_ref[:, FFN:2 * FFN] = du.astype(BF16)

    return _rowcall(name, body, n, [(gu, _rs(FFN, 0)), (gu, _rs(FFN, 1)), (da, _rs(FFN))],
                    [(_sd((n, 2 * FFN), BF16), _rs(2 * FFN))])[0]


FLASH_ROWS = 256


def _fold_lanes(x, op):
    acc = x[:, 0:128]
    for b in range(1, x.shape[1] // 128):
        acc = op(acc, x[:, 128 * b:128 * b + 128])
    return acc


def _band_mask(tq, tk, i, kb):
    qp = i * tq + lax.broadcasted_iota(jnp.int32, (tq, tk), 0)
    kp = kb * tk + lax.broadcasted_iota(jnp.int32, (tq, tk), 1)
    return jnp.abs(qp - kp) <= SWA_WIN


def flash_fwd(name, qa, ka, va, *, w, vw, hq, grp, vcol0, scale, nlat, tq, tk, band, sink, ctx_q, prev=None):
    n = qa.shape[0]
    cblk = nlat // NCTX
    band = band and not ctx_q
    assert not band, "latent rows of a banded attention go through swa_fwd_lat"
    if ctx_q:
        tq = tk = NCTX
        grid = (hq, 1, 1)
        qmap = lambda h, i, kk: (cblk, h)
        kmap = lambda h, i, kk: (cblk, h // grp)
        vmap = lambda h, i, kk: (cblk, vcol0 + h // grp)
        omap = lambda h, i, kk: (cblk, h)
        lmap = lambda h, i, kk: (h, cblk, 0)
    else:
        nb = nlat // tk
        nk = 3 if band else nb
        grid = (hq, nlat // tq, nk)
        kb_of = (lambda i, kk: jnp.clip(i + kk - 1, 0, nb - 1)) if band else (lambda i, kk: kk)
        qmap = lambda h, i, kk: (i, h)
        kmap = lambda h, i, kk: (kb_of(i, kk), h // grp)
        vmap = lambda h, i, kk: (kb_of(i, kk), vcol0 + h // grp)
        omap = lambda h, i, kk: (i, h)
        lmap = lambda h, i, kk: (h, i, 0)
    nk = grid[2]
    extra = not ctx_q
    has_sink = sink is not None

    def body(*refs):
        refs = list(refs)
        q_ref, k_ref, v_ref = refs[:3]
        pos = 3
        if extra:
            ke_ref, ve_ref = refs[pos:pos + 2]
            pos += 2
        if has_sink:
            s_ref = refs[pos]
            pos += 1
        if prev is not None:
            pos += 2
        o_ref, l_ref, m_s, l_s, a_s = refs[pos:pos + 5]
        kk = pl.program_id(2)
        tr = min(tq, FLASH_ROWS)

        def step(kblk, vblk):
            for r in range(tq // tr):
                rows = slice(r * tr, (r + 1) * tr)
                s = _d(q_ref[rows, :], kblk, ((1,), (1,))) * (scale * LOG2E)
                m_prev = m_s[rows, :]
                m_new = jnp.maximum(m_prev, jnp.max(_fold_lanes(s, jnp.maximum), axis=1, keepdims=True))
                alpha = jnp.exp2(m_prev - m_new)
                p = jnp.exp2(s - m_new)
                l_s[rows, :] = alpha * l_s[rows, :] + _fold_lanes(p, jnp.add)
                a_s[rows, :] = alpha * a_s[rows, :] + _d(p, vblk, ((1,), (0,)))
                m_s[rows, :] = m_new

        @pl.when(kk == 0)
        def _():
            if has_sink:
                sv = jnp.max(s_ref[0], axis=1, keepdims=True) * LOG2E
                m_s[...] = jnp.zeros((tq, 1), F32) + sv
                l_s[...] = (lax.broadcasted_iota(jnp.int32, (tq, 128), 1) == 0).astype(F32)
            else:
                m_s[...] = jnp.full((tq, 1), NEG, F32)
                l_s[...] = jnp.zeros((tq, 128), F32)
            a_s[...] = jnp.zeros((tq, vw), F32)
            if extra:
                step(ke_ref[...], ve_ref[...])

        step(k_ref[...], v_ref[...])

        @pl.when(kk == nk - 1)
        def _():
            l = jnp.sum(l_s[...], axis=1, keepdims=True)
            o_ref[...] = (a_s[...] / l).astype(BF16)
            l_ref[0] = m_s[...] + jnp.log2(l)

    ins = [(qa, pl.BlockSpec((tq, w), qmap)), (ka, pl.BlockSpec((tk, w), kmap)), (va, pl.BlockSpec((tk, vw), vmap))]
    if extra:
        ins += [(ka, pl.BlockSpec((NCTX, w), lambda h, i, kk: (cblk, h // grp))),
                (va, pl.BlockSpec((NCTX, vw), lambda h, i, kk: (cblk, vcol0 + h // grp)))]
    if has_sink:
        ins += [(sink, pl.BlockSpec((1, 1, 128), lambda h, i, kk: (h, 0, 0)))]
    aliases = {}
    if prev is not None:
        any_spec = pl.BlockSpec(memory_space=pl.ANY)
        aliases = {len(ins): 0, len(ins) + 1: 1}
        ins += [(prev[0], any_spec), (prev[1], any_spec)]
    return pl.pallas_call(
        body, out_shape=[_sd((n, hq * vw), BF16), _sd((hq, n, 1), F32)], grid=grid,
        in_specs=[s for _, s in ins],
        out_specs=[pl.BlockSpec((tq, vw), omap), pl.BlockSpec((1, tq, 1), lmap)],
        scratch_shapes=[pltpu.VMEM((tq, 1), F32), pltpu.VMEM((tq, 128), F32), pltpu.VMEM((tq, vw), F32)],
        input_output_aliases=aliases, name=name,
        compiler_params=_cp(("parallel", "parallel", "arbitrary")))(*[a for a, _ in ins])


def flash_dq(name, qa, ka, va, oa, doa, lse, *, w, vw, hq, grp, vcol0, scale, nlat, tq, tk, band, sink, ctx_q, prev=None):
    n = qa.shape[0]
    cblk = nlat // NCTX
    band = band and not ctx_q
    if ctx_q:
        tq = tk = NCTX
        grid = (hq, 1, 1)
        qmap = lambda h, i, kk: (cblk, h)
        kmap = lambda h, i, kk: (cblk, h // grp)
        vmap = lambda h, i, kk: (cblk, vcol0 + h // grp)
        lmap = lambda h, i, kk: (h, cblk, 0)
    else:
        nb = nlat // tk
        grid = (hq, nlat // tq, 3 if band else nb)
        kb_of = (lambda i, kk: jnp.clip(i + kk - 1, 0, nb - 1)) if band else (lambda i, kk: kk)
        qmap = lambda h, i, kk: (i, h)
        kmap = lambda h, i, kk: (kb_of(i, kk), h // grp)
        vmap = lambda h, i, kk: (kb_of(i, kk), vcol0 + h // grp)
        lmap = lambda h, i, kk: (h, i, 0)
    nk = grid[2]
    nq = grid[1]
    extra = not ctx_q
    has_sink = sink is not None

    def body(*refs):
        refs = list(refs)
        q_ref, k_ref, v_ref, o_ref, do_ref, l_ref = refs[:6]
        pos = 6
        if extra:
            ke_ref, ve_ref = refs[pos:pos + 2]
            pos += 2
        if has_sink:
            s_ref = refs[pos]
            pos += 1
        if prev is not None:
            pos += 2
        dq_ref, dl_ref, ds_ref, acc_s, dl_s = refs[pos:pos + 5]
        i = pl.program_id(1)
        kk = pl.program_id(2)
        q = q_ref[...]
        do = do_ref[...]
        lse_v = l_ref[0]

        def step(kblk, vblk, mask):
            s = _d(q, kblk, ((1,), (1,))) * (scale * LOG2E)
            if mask is not None:
                s = jnp.where(mask, s, NEG)
            p = jnp.exp2(s - lse_v)
            dp = _d(do, vblk, ((1,), (1,)))
            ds = p * (dp - dl_s[...]) * scale
            acc_s[...] += _d(ds, kblk, ((1,), (0,)))

        @pl.when(kk == 0)
        def _():
            delta = jnp.sum(do * o_ref[...].astype(F32), axis=1, keepdims=True)
            dl_s[...] = delta
            acc_s[...] = jnp.zeros((tq, w), F32)
            if has_sink:
                sv = jnp.max(s_ref[0], axis=1, keepdims=True) * LOG2E
                dsk = jnp.sum(-jnp.exp2(sv - lse_v) * delta, axis=0, keepdims=True)
                _acc(ds_ref, jnp.zeros((1, 1, 128), F32) + dsk, i == 0)
            else:
                ds_ref[...] = jnp.zeros((1, 1, 128), F32)
            if extra:
                step(ke_ref[...], ve_ref[...], None)

        if band:
            kb = i + kk - 1

            @pl.when((kb >= 0) & (kb < nlat // tk))
            def _():
                step(k_ref[...], v_ref[...], _band_mask(tq, tk, i, kb))
        else:
            step(k_ref[...], v_ref[...], None)

        @pl.when(kk == nk - 1)
        def _():
            dq_ref[...] = acc_s[...]
            dl_ref[0] = dl_s[...]

    ins = [(qa, pl.BlockSpec((tq, w), qmap)), (ka, pl.BlockSpec((tk, w), kmap)), (va, pl.BlockSpec((tk, vw), vmap)),
           (oa, pl.BlockSpec((tq, vw), qmap)), (doa, pl.BlockSpec((tq, vw), qmap)), (lse, pl.BlockSpec((1, tq, 1), lmap))]
    if extra:
        ins += [(ka, pl.BlockSpec((NCTX, w), lambda h, i, kk: (cblk, h // grp))),
                (va, pl.BlockSpec((NCTX, vw), lambda h, i, kk: (cblk, vcol0 + h // grp)))]
    if has_sink:
        ins += [(sink, pl.BlockSpec((1, 1, 128), lambda h, i, kk: (h, 0, 0)))]
    aliases = {}
    if prev is not None:
        any_spec = pl.BlockSpec(memory_space=pl.ANY)
        aliases = {len(ins): 0, len(ins) + 1: 1}
        ins += [(prev[0], any_spec), (prev[1], any_spec)]
    del nq
    return pl.pallas_call(
        body, out_shape=[_sd((n, hq * w), F32), _sd((hq, n, 1), F32), _sd((hq, 1, 128), F32)], grid=grid,
        in_specs=[s for _, s in ins],
        out_specs=[pl.BlockSpec((tq, w), qmap), pl.BlockSpec((1, tq, 1), lmap),
                   pl.BlockSpec((1, 1, 128), lambda h, i, kk: (h, 0, 0))],
        scratch_shapes=[pltpu.VMEM((tq, w), F32), pltpu.VMEM((tq, 1), F32)],
        input_output_aliases=aliases, name=name,
        compiler_params=_cp(("parallel", "arbitrary", "arbitrary")))(*[a for a, _ in ins])


def flash_dkv(name, qa, ka, va, doa, lse, delta, *, w, vw, hkv, grp, vcol0, scale, nlat, tq, tk, band, ctx_k, prev=None):
    n = qa.shape[0]
    cblk = nlat // NCTX
    nqb = nlat // tq
    band = band and not ctx_k
    if ctx_k:
        tk = NCTX
        nqs = nqb
        grid = (hkv, 1, grp * nqs)
        kmap = lambda hk, j, t: (cblk, hk)
        vmap = lambda hk, j, t: (cblk, vcol0 + hk)
        dvmap = lambda hk, j, t: (cblk, hk)
        qb_of = lambda j, t: t % nqs
    else:
        nqs = 3 if band else nqb
        grid = (hkv, nlat // tk, grp * nqs)
        kmap = lambda hk, j, t: (j, hk)
        vmap = lambda hk, j, t: (j, vcol0 + hk)
        dvmap = lambda hk, j, t: (j, hk)
        qb_of = (lambda j, t: jnp.clip(j + t % nqs - 1, 0, nqb - 1)) if band else (lambda j, t: t % nqs)
    qmap = lambda hk, j, t: (qb_of(j, t), hk * grp + t // nqs)
    lmap = lambda hk, j, t: (hk * grp + t // nqs, qb_of(j, t), 0)

    def body(*refs):
        refs = list(refs)
        q_ref, k_ref, v_ref, do_ref, l_ref, dl_ref = refs[:6]
        pos = 6
        if ctx_k:
            qe_ref, doe_ref, le_ref, dle_ref = refs[pos:pos + 4]
            pos += 4
        if prev is not None:
            pos += 2
        dk_ref, dv_ref = refs[pos:pos + 2]
        j = pl.program_id(1)
        t = pl.program_id(2)
        kblk = k_ref[...]
        vblk = v_ref[...]

        def contrib(q, do, lse_v, dl_v, mask):
            s = _d(q, kblk, ((1,), (1,))) * (scale * LOG2E)
            if mask is not None:
                s = jnp.where(mask, s, NEG)
            p = jnp.exp2(s - lse_v)
            dp = _d(do, vblk, ((1,), (1,)))
            ds = p * (dp - dl_v) * scale
            return _d(ds, q, ((0,), (0,))), _d(p, do, ((0,), (0,)))

        @pl.when(t == 0)
        def _():
            dk = jnp.zeros((tk, w), F32)
            dv = jnp.zeros((tk, vw), F32)
            if ctx_k:
                for gi in range(grp):
                    a, b = contrib(qe_ref[:, w * gi:w * gi + w], doe_ref[:, vw * gi:vw * gi + vw], le_ref[gi], dle_ref[gi], None)
                    dk = dk + a
                    dv = dv + b
            dk_ref[...] = dk
            dv_ref[...] = dv

        def add(mask):
            a, b = contrib(q_ref[...], do_ref[...], l_ref[0], dl_ref[0], mask)
            dk_ref[...] += a
            dv_ref[...] += b

        if band:
            qb = j + t % nqs - 1

            @pl.when((qb >= 0) & (qb < nqb))
            def _():
                add(_band_mask(tq, tk, qb, j))
        else:
            add(None)

    ins = [(qa, pl.BlockSpec((tq, w), qmap)), (ka, pl.BlockSpec((tk, w), kmap)), (va, pl.BlockSpec((tk, vw), vmap)),
           (doa, pl.BlockSpec((tq, vw), qmap)), (lse, pl.BlockSpec((1, tq, 1), lmap)), (delta, pl.BlockSpec((1, tq, 1), lmap))]
    if ctx_k:
        ins += [(qa, pl.BlockSpec((NCTX, grp * w), lambda hk, j, t: (cblk, hk))),
                (doa, pl.BlockSpec((NCTX, grp * vw), lambda hk, j, t: (cblk, hk))),
                (lse, pl.BlockSpec((grp, NCTX, 1), lambda hk, j, t: (hk, cblk, 0))),
                (delta, pl.BlockSpec((grp, NCTX, 1), lambda hk, j, t: (hk, cblk, 0)))]
    aliases = {}
    if prev is not None:
        any_spec = pl.BlockSpec(memory_space=pl.ANY)
        aliases = {len(ins): 0, len(ins) + 1: 1}
        ins += [(prev[0], any_spec), (prev[1], any_spec)]
    return pl.pallas_call(
        body, out_shape=[_sd((n, hkv * w), F32), _sd((n, hkv * vw), F32)], grid=grid,
        in_specs=[s for _, s in ins],
        out_specs=[pl.BlockSpec((tk, w), kmap), pl.BlockSpec((tk, vw), dvmap)],
        input_output_aliases=aliases, name=name,
        compiler_params=_cp(("parallel", "parallel", "arbitrary")))(*[a for a, _ in ins])


def mla_fwd(name, qm, km, vm, nlat):
    n = qm.shape[0]
    t = NCTX
    nlt = nlat // t
    c = (MLA_NOPE + MLA_ROPE) ** -0.5 * LOG2E

    def body(q_ref, k_ref, v_ref, o_ref, l_ref):
        i = pl.program_id(1)

        def run(k, v):
            s = _d(q_ref[...], k, ((1,), (1,))) * c
            m = jnp.max(_fold_lanes(s, jnp.maximum), axis=1, keepdims=True)
            p = jnp.exp2(s - m)
            l = jnp.sum(_fold_lanes(p, jnp.add), axis=1, keepdims=True)
            o_ref[...] = (_d(p, v, ((1,), (0,))) / l).astype(BF16)
            l_ref[0] = m + jnp.log2(l)

        @pl.when(i < nlt)
        def _():
            run(k_ref[...], v_ref[...])

        @pl.when(i == nlt)
        def _():
            run(k_ref[nlat:n, :], v_ref[nlat:n, :])

    return pl.pallas_call(
        body, out_shape=[_sd((n, MLA_H * 128), BF16), _sd((MLA_H, n, 1), F32)], grid=(MLA_H, n // t),
        in_specs=[pl.BlockSpec((t, 256), lambda h, i: (i, h)), pl.BlockSpec((n, 256), lambda h, i: (0, h)),
                  pl.BlockSpec((n, 128), lambda h, i: (0, h))],
        out_specs=[pl.BlockSpec((t, 128), lambda h, i: (i, h)), pl.BlockSpec((1, t, 1), lambda h, i: (h, i, 0))],
        name=name, compiler_params=_cp(("parallel", "arbitrary")))(qm, km, vm)


def mla_dq(name, qm, km, vm, o, do, lse, nlat):
    n = qm.shape[0]
    t = NCTX
    nlt = nlat // t
    scale = (MLA_NOPE + MLA_ROPE) ** -0.5

    def body(q_ref, k_ref, v_ref, o_ref, do_ref, l_ref, dq_ref, dl_ref):
        i = pl.program_id(1)
        do = do_ref[...]
        delta = jnp.sum(do.astype(F32) * o_ref[...].astype(F32), axis=1, keepdims=True)
        dl_ref[0] = delta

        def run(k, v):
            s = _d(q_ref[...], k, ((1,), (1,))) * (scale * LOG2E)
            ds = jnp.exp2(s - l_ref[0]) * (_d(do, v, ((1,), (1,))) - delta) * scale
            dq_ref[...] = _d(ds, k, ((1,), (0,)))

        @pl.when(i < nlt)
        def _():
            run(k_ref[...], v_ref[...])

        @pl.when(i == nlt)
        def _():
            run(k_ref[nlat:n, :], v_ref[nlat:n, :])

    qspec = pl.BlockSpec((t, 256), lambda h, i: (i, h))
    ospec = pl.BlockSpec((t, 128), lambda h, i: (i, h))
    lspec = pl.BlockSpec((1, t, 1), lambda h, i: (h, i, 0))
    return pl.pallas_call(
        body, out_shape=[_sd((n, MLA_H * 256), F32), _sd((MLA_H, n, 1), F32)], grid=(MLA_H, n // t),
        in_specs=[qspec, pl.BlockSpec((n, 256), lambda h, i: (0, h)), pl.BlockSpec((n, 128), lambda h, i: (0, h)),
                  ospec, ospec, lspec],
        out_specs=[qspec, lspec],
        name=name, compiler_params=_cp(("parallel", "arbitrary")))(qm, km, vm, o, do, lse)


def mla_dkv(name, qm, km, vm, do, lse_row, delta_row, nlat):
    n = qm.shape[0]
    t = NCTX
    nlt = nlat // t
    scale = (MLA_NOPE + MLA_ROPE) ** -0.5

    def body(q_ref, k_ref, v_ref, do_ref, l_ref, dl_ref, dk_ref, dv_ref):
        j = pl.program_id(1)

        def run(q, do, lrow, drow):
            st = _d(k_ref[...], q, ((1,), (1,))) * (scale * LOG2E)
            pt = jnp.exp2(st - lrow)
            dv_ref[...] = _d(pt, do, ((1,), (0,)))
            dst = pt * (_d(v_ref[...], do, ((1,), (1,))) - drow) * scale
            dk_ref[...] = _d(dst, q, ((1,), (0,)))

        @pl.when(j < nlt)
        def _():
            run(q_ref[0:nlat, :], do_ref[0:nlat, :], l_ref[0, :, 0:nlat], dl_ref[0, :, 0:nlat])

        @pl.when(j == nlt)
        def _():
            run(q_ref[...], do_ref[...], l_ref[0], dl_ref[0])

    rspec = pl.BlockSpec((1, 1, n), lambda h, j: (h, 0, 0))
    return pl.pallas_call(
        body, out_shape=[_sd((n, MLA_H * 256), F32), _sd((n, MLA_H * 128), F32)], grid=(MLA_H, n // t),
        in_specs=[pl.BlockSpec((n, 256), lambda h, j: (0, h)), pl.BlockSpec((t, 256), lambda h, j: (j, h)),
                  pl.BlockSpec((t, 128), lambda h, j: (j, h)), pl.BlockSpec((n, 128), lambda h, j: (0, h)), rspec, rspec],
        out_specs=[pl.BlockSpec((t, 256), lambda h, j: (j, h)), pl.BlockSpec((t, 128), lambda h, j: (j, h))],
        name=name, compiler_params=_cp(("parallel", "arbitrary")))(qm, km, vm, do, lse_row, delta_row)


def mla_attention_bwd(tag, qm, km, vm, o, do, lse, nlat):
    n = qm.shape[0]
    dq, delta = mla_dq(tag + "_dq", qm, km, vm, o, do, lse, nlat)
    dk, dv = mla_dkv(tag + "_dkv", qm, km, vm, do, lse.reshape(MLA_H, 1, n), delta.reshape(MLA_H, 1, n), nlat)
    return dq, dk, dv


SWA_T = 512


def _swa_window(t, nlat):
    t = min(t, nlat)
    return t, min(t + 2 * SWA_WIN, nlat)


def _win_start(i, t, wlen, nlat):
    return pl.multiple_of(jnp.clip(i * t - SWA_WIN, 0, nlat - wlen), 128)


def _win_mask(rows, cols, row0, col0):
    rp = row0 + lax.broadcasted_iota(jnp.int32, (rows, cols), 0)
    cp = col0 + lax.broadcasted_iota(jnp.int32, (rows, cols), 1)
    return jnp.abs(rp - cp) <= SWA_WIN


def swa_fwd_lat(name, qs, ks, u, sink, nlat):
    n = qs.shape[0]
    tq, wlen = _swa_window(SWA_T, nlat)
    grp = SWA_HQ // SWA_HKV
    scale = SWA_DH ** -0.5
    vcol0 = C_V // 128

    def body(q_ref, k_ref, v_ref, s_ref, o_ref, l_ref):
        i = pl.program_id(1)
        ws = _win_start(i, tq, wlen, nlat)
        q = q_ref[...]
        s1 = _d(q, k_ref[pl.ds(ws, wlen), :], ((1,), (1,))) * (scale * LOG2E)
        s1 = jnp.where(_win_mask(tq, wlen, i * tq, ws), s1, NEG)
        s2 = _d(q, k_ref[pl.ds(nlat, NCTX), :], ((1,), (1,))) * (scale * LOG2E)
        sv = jnp.max(s_ref[0], axis=1, keepdims=True) * LOG2E
        m = jnp.maximum(jnp.maximum(jnp.max(s1, axis=1, keepdims=True), jnp.max(s2, axis=1, keepdims=True)), sv)
        p1 = jnp.exp2(s1 - m)
        p2 = jnp.exp2(s2 - m)
        l = jnp.sum(p1, axis=1, keepdims=True) + jnp.sum(p2, axis=1, keepdims=True) + jnp.exp2(sv - m)
        acc = _d(p1, v_ref[pl.ds(ws, wlen), :], ((1,), (0,))) + _d(p2, v_ref[pl.ds(nlat, NCTX), :], ((1,), (0,)))
        o_ref[...] = (acc / l).astype(BF16)
        l_ref[0] = m + jnp.log2(l)

    return pl.pallas_call(
        body, out_shape=[_sd((n, SWA_HQ * 128), BF16), _sd((SWA_HQ, n, 1), F32)], grid=(SWA_HQ, nlat // tq),
        in_specs=[pl.BlockSpec((tq, 128), lambda h, i: (i, h)), pl.BlockSpec((n, 128), lambda h, i: (0, h // grp)),
                  pl.BlockSpec((n, 128), lambda h, i: (0, vcol0 + h // grp)), pl.BlockSpec((1, 1, 128), lambda h, i: (h, 0, 0))],
        out_specs=[pl.BlockSpec((tq, 128), lambda h, i: (i, h)), pl.BlockSpec((1, tq, 1), lambda h, i: (h, i, 0))],
        name=name, compiler_params=_cp(("parallel", "arbitrary")))(qs, ks, u, sink)


def swa_dq_lat(name, qs, ks, u, o, do, lse, sink, nlat):
    n = qs.shape[0]
    tq, wlen = _swa_window(SWA_T, nlat)
    grp = SWA_HQ // SWA_HKV
    scale = SWA_DH ** -0.5
    vcol0 = C_V // 128

    def body(q_ref, k_ref, v_ref, s_ref, o_ref, do_ref, l_ref, dq_ref, dl_ref, ds_ref):
        i = pl.program_id(1)
        ws = _win_start(i, tq, wlen, nlat)
        q = q_ref[...]
        do = do_ref[...]
        lse_v = l_ref[0]
        delta = jnp.sum(do.astype(F32) * o_ref[...].astype(F32), axis=1, keepdims=True)
        kw = k_ref[pl.ds(ws, wlen), :]
        kc = k_ref[pl.ds(nlat, NCTX), :]
        s1 = _d(q, kw, ((1,), (1,))) * (scale * LOG2E)
        s1 = jnp.where(_win_mask(tq, wlen, i * tq, ws), s1, NEG)
        s2 = _d(q, kc, ((1,), (1,))) * (scale * LOG2E)
        ds1 = jnp.exp2(s1 - lse_v) * (_d(do, v_ref[pl.ds(ws, wlen), :], ((1,), (1,))) - delta) * scale
        ds2 = jnp.exp2(s2 - lse_v) * (_d(do, v_ref[pl.ds(nlat, NCTX), :], ((1,), (1,))) - delta) * scale
        dq_ref[...] = _d(ds1, kw, ((1,), (0,))) + _d(ds2, kc, ((1,), (0,)))
        dl_ref[0] = delta
        sv = jnp.max(s_ref[0], axis=1, keepdims=True) * LOG2E
        dsk = jnp.sum(-jnp.exp2(sv - lse_v) * delta, axis=0, keepdims=True)
        _acc(ds_ref, jnp.zeros((1, 1, 128), F32) + dsk, i == 0)

    qspec = pl.BlockSpec((tq, 128), lambda h, i: (i, h))
    lspec = pl.BlockSpec((1, tq, 1), lambda h, i: (h, i, 0))
    return pl.pallas_call(
        body, out_shape=[_sd((n, SWA_HQ * 128), F32), _sd((SWA_HQ, n, 1), F32), _sd((SWA_HQ, 1, 128), F32)],
        grid=(SWA_HQ, nlat // tq),
        in_specs=[qspec, pl.BlockSpec((n, 128), lambda h, i: (0, h // grp)),
                  pl.BlockSpec((n, 128), lambda h, i: (0, vcol0 + h // grp)), pl.BlockSpec((1, 1, 128), lambda h, i: (h, 0, 0)),
                  qspec, qspec, lspec],
        out_specs=[qspec, lspec, pl.BlockSpec((1, 1, 128), lambda h, i: (h, 0, 0))],
        name=name, compiler_params=_cp(("parallel", "arbitrary")))(qs, ks, u, sink, o, do, lse)


def swa_dkv_lat(name, qs, ks, u, do, lse_row, delta_row, nlat):
    n = qs.shape[0]
    tk, wlen = _swa_window(SWA_T, nlat)
    grp = SWA_HQ // SWA_HKV
    scale = SWA_DH ** -0.5
    vcol0 = C_V // 128

    def body(q_ref, k_ref, v_ref, do_ref, l_ref, dl_ref, dk_ref, dv_ref):
        j = pl.program_id(1)
        ws = _win_start(j, tk, wlen, nlat)
        k = k_ref[...]
        v = v_ref[...]
        mask = _win_mask(tk, wlen, j * tk, ws)
        dk = jnp.zeros((tk, 128), F32)
        dv = jnp.zeros((tk, 128), F32)
        for gi in range(grp):
            qw = q_ref[pl.ds(ws, wlen), 128 * gi:128 * gi + 128]
            dow = do_ref[pl.ds(ws, wlen), 128 * gi:128 * gi + 128]
            st = jnp.where(mask, _d(k, qw, ((1,), (1,))) * (scale * LOG2E), NEG)
            pt = jnp.exp2(st - l_ref[gi, :, pl.ds(ws, wlen)])
            dv = dv + _d(pt, dow, ((1,), (0,)))
            dst = pt * (_d(v, dow, ((1,), (1,))) - dl_ref[gi, :, pl.ds(ws, wlen)]) * scale
            dk = dk + _d(dst, qw, ((1,), (0,)))
        dk_ref[...] = dk
        dv_ref[...] = dv

    rspec = pl.BlockSpec((grp, 1, n), lambda hk, j: (hk, 0, 0))
    return pl.pallas_call(
        body, out_shape=[_sd((n, SWA_HKV * 128), F32), _sd((n, SWA_HKV * 128), F32)], grid=(SWA_HKV, nlat // tk),
        in_specs=[pl.BlockSpec((n, grp * 128), lambda hk, j: (0, hk)), pl.BlockSpec((tk, 128), lambda hk, j: (j, hk)),
                  pl.BlockSpec((tk, 128), lambda hk, j: (j, vcol0 + hk)), pl.BlockSpec((n, grp * 128), lambda hk, j: (0, hk)),
                  rspec, rspec],
        out_specs=[pl.BlockSpec((tk, 128), lambda hk, j: (j, hk)), pl.BlockSpec((tk, 128), lambda hk, j: (j, hk))],
        name=name, compiler_params=_cp(("parallel", "arbitrary")))(qs, ks, u, do, lse_row, delta_row)


def swa_attention_fwd(tag, qs, ks, u, sink, cfg, nlat):
    o, lse = swa_fwd_lat(tag + "_fwd_lat", qs, ks, u, sink, nlat)
    return flash_fwd(tag + "_fwd_ctx", qs, ks, u, sink=sink, ctx_q=True, nlat=nlat, prev=(o, lse), **cfg)


def swa_attention_bwd(tag, qs, ks, u, o, do, lse, sink, cfg, nlat):
    n = qs.shape[0]
    dq, delta, ds1 = swa_dq_lat(tag + "_dq_lat", qs, ks, u, o, do, lse, sink, nlat)
    dq, delta, ds2 = flash_dq(tag + "_dq_ctx", qs, ks, u, o, do, lse, sink=sink, ctx_q=True, nlat=nlat, prev=(dq, delta), **cfg)
    dk, dv = swa_dkv_lat(tag + "_dkv_lat", qs, ks, u, do, lse.reshape(SWA_HQ, 1, n), delta.reshape(SWA_HQ, 1, n), nlat)
    kc = {k: v for k, v in cfg.items() if k != "hq"}
    kc["hkv"] = SWA_HKV
    kc["tq"] = min(1024, nlat)
    dk, dv = flash_dkv(tag + "_dkv_ctx", qs, ks, u, do, lse, delta, ctx_k=True, nlat=nlat, prev=(dk, dv), **kc)
    return dq, dk, dv, ds1 + ds2


def adamw(name, w, g, m, v):
    r, c = w.shape
    tr = _pick(r, (256, 128, 64, 32, 16, 8))
    bc1 = 1.0 - ADAM_B1 ** ADAM_STEP
    bc2 = 1.0 - ADAM_B2 ** ADAM_STEP

    def body(w_ref, g_ref, m_ref, v_ref, d_ref, nm_ref, nv_ref):
        gv = g_ref[...]
        nm = ADAM_B1 * m_ref[...] + (1.0 - ADAM_B1) * gv
        nv = ADAM_B2 * v_ref[...] + (1.0 - ADAM_B2) * (gv * gv)
        d_ref[...] = -ADAM_LR * ((nm / bc1) / (jnp.sqrt(nv / bc2) + ADAM_EPS) + ADAM_WD * w_ref[...])
        nm_ref[...] = nm
        nv_ref[...] = nv

    spec = pl.BlockSpec((tr, c), lambda i: (i, 0))
    return pl.pallas_call(body, out_shape=[_sd((r, c), F32)] * 3, grid=(r // tr,), in_specs=[spec] * 4, out_specs=[spec] * 3,
                          name=name, compiler_params=_cp(("parallel",)))(w, g, m, v)


def _coords():
    return lax.axis_index("x"), lax.axis_index("y"), lax.axis_index("c")


_ANY = pl.BlockSpec(memory_space=pl.ANY)


def _chip():
    return 2 * lax.axis_index("x") + lax.axis_index("y")


def _per_core(fn):
    c = lax.axis_index("c")
    for cs in (0, 1):
        pl.when(c == cs)(functools.partial(fn, cs))


def gather_chips(name, a):
    r = a.shape[0]
    half = r // 2

    def body(a_ref, o_ref, ici_send, ici_recv, d2d_send, d2d_recv):
        _per_core(functools.partial(run, a_ref, o_ref, ici_send, ici_recv, d2d_send, d2d_recv))

    def run(a_ref, o_ref, ici_send, ici_recv, d2d_send, d2d_recv, c):
        x, y, _ = _coords()
        me = 2 * x + y
        peers = [(1 - x, y), (x, 1 - y), (1 - x, 1 - y)]
        my_rows = pl.ds(c * half, half)
        sib_rows = pl.ds((1 - c) * half, half)
        sends = [pltpu.make_async_remote_copy(a_ref.at[my_rows], o_ref.at[me, my_rows], ici_send.at[k], ici_recv.at[k],
                                              device_id=(px, py, c), device_id_type=MESH)
                 for k, (px, py) in enumerate(peers)]
        for cp in sends:
            cp.start()
        passed = []
        for k, (px, py) in enumerate(peers):
            s = 2 * px + py
            pltpu.make_async_remote_copy(a_ref.at[my_rows], o_ref.at[s, my_rows], ici_send.at[k], ici_recv.at[k],
                                         device_id=(px, py, c), device_id_type=MESH).wait_recv()
            fw = pltpu.make_async_remote_copy(o_ref.at[s, my_rows], o_ref.at[s, my_rows], d2d_send.at[k], d2d_recv.at[k],
                                              device_id=(x, y, 1 - c), device_id_type=MESH)
            fw.start()
            passed.append(fw)
        for k, (px, py) in enumerate(peers):
            s = 2 * px + py
            pltpu.make_async_remote_copy(o_ref.at[s, sib_rows], o_ref.at[s, sib_rows], d2d_send.at[k], d2d_recv.at[k],
                                         device_id=(x, y, 1 - c), device_id_type=MESH).wait_recv()
        for cp in sends + passed:
            cp.wait_send()

    out = pl.pallas_call(
        body, out_shape=_sd((4,) + a.shape, a.dtype), in_specs=[_ANY], out_specs=_ANY,
        scratch_shapes=[pltpu.SemaphoreType.DMA((3,)), pltpu.SemaphoreType.DMA((3,)), pltpu.SemaphoreType.DMA((3,)),
                        pltpu.SemaphoreType.DMA((3,))],
        name=name, compiler_params=pltpu.CompilerParams(has_side_effects=True))(a)
    return lax.dynamic_update_index_in_dim(out, a, _chip(), 0)


def pair_split(name, a):
    k4, r, cdim = a.shape
    half = r // 2

    def body(a_ref, got_ref, send_sem, recv_sem):
        _per_core(functools.partial(run, a_ref, got_ref, send_sem, recv_sem))

    def run(a_ref, got_ref, send_sem, recv_sem, c):
        x, y, _ = _coords()
        sib_rows = pl.ds((1 - c) * half, half)
        cp = pltpu.make_async_remote_copy(a_ref.at[:, sib_rows], got_ref, send_sem, recv_sem,
                                          device_id=(x, y, 1 - c), device_id_type=MESH)
        cp.start()
        cp.wait()

    got = pl.pallas_call(
        body, out_shape=_sd((k4, half, cdim), a.dtype), in_specs=[_ANY], out_specs=_ANY,
        scratch_shapes=[pltpu.SemaphoreType.DMA, pltpu.SemaphoreType.DMA],
        name=name, compiler_params=pltpu.CompilerParams(has_side_effects=True))(a)
    return lax.dynamic_slice_in_dim(a, lax.axis_index("c") * half, half, axis=1), got


def scatter_chips(name, a):
    def body(a_ref, o_ref, send_sems, recv_sems):
        x, y, c = _coords()
        me = 2 * x + y
        peers = [(1 - x, y), (x, 1 - y), (1 - x, 1 - y)]
        sends = [pltpu.make_async_remote_copy(a_ref.at[2 * px + py], o_ref.at[me], send_sems.at[k], recv_sems.at[k],
                                              device_id=(px, py, c), device_id_type=MESH)
                 for k, (px, py) in enumerate(peers)]
        for cp in sends:
            cp.start()
        for k, (px, py) in enumerate(peers):
            pltpu.make_async_remote_copy(a_ref.at[me], o_ref.at[2 * px + py], send_sems.at[k], recv_sems.at[k],
                                         device_id=(px, py, c), device_id_type=MESH).wait_recv()
        for cp in sends:
            cp.wait_send()

    out = pl.pallas_call(
        body, out_shape=_sd(a.shape, a.dtype), in_specs=[_ANY], out_specs=_ANY,
        scratch_shapes=[pltpu.SemaphoreType.DMA((3,)), pltpu.SemaphoreType.DMA((3,))],
        name=name, compiler_params=pltpu.CompilerParams(has_side_effects=True))(a)
    return lax.dynamic_update_index_in_dim(out, lax.dynamic_index_in_dim(a, _chip(), 0, keepdims=False), _chip(), 0)


def pair_join(name, a):
    half, cdim = a.shape

    def body(a_ref, o_ref, send_sem, recv_sem):
        _per_core(functools.partial(run, a_ref, o_ref, send_sem, recv_sem))

    def run(a_ref, o_ref, send_sem, recv_sem, c):
        x, y, _ = _coords()
        my_rows = pl.ds(c * half, half)
        sib_rows = pl.ds((1 - c) * half, half)
        cp = pltpu.make_async_remote_copy(a_ref, o_ref.at[my_rows], send_sem, recv_sem, device_id=(x, y, 1 - c),
                                          device_id_type=MESH)
        cp.start()
        cp.wait_send()
        pltpu.make_async_remote_copy(a_ref, o_ref.at[sib_rows], send_sem, recv_sem, device_id=(x, y, 1 - c),
                                     device_id_type=MESH).wait_recv()

    out = pl.pallas_call(
        body, out_shape=_sd((2 * half, cdim), a.dtype), in_specs=[_ANY], out_specs=_ANY,
        scratch_shapes=[pltpu.SemaphoreType.DMA, pltpu.SemaphoreType.DMA],
        name=name, compiler_params=pltpu.CompilerParams(has_side_effects=True))(a)
    return lax.dynamic_update_slice_in_dim(out, a, lax.axis_index("c") * half, axis=0)


def add_cast(name, a, b, dtype):
    k, r, c = a.shape
    tr = _pick(r, (1024, 512, 256, 128, 64, 32, 16, 8))

    def body(a_ref, b_ref, o_ref):
        o_ref[...] = (a_ref[...].astype(F32) + b_ref[...].astype(F32)).astype(dtype)

    spec = pl.BlockSpec((1, tr, c), lambda s, i: (s, i, 0))
    return pl.pallas_call(body, out_shape=_sd((k, r, c), dtype), grid=(k, r // tr), in_specs=[spec, spec], out_specs=spec,
                          name=name, compiler_params=_cp(("parallel", "parallel")))(a, b)


def gather_all(name, a):
    def body(a_ref, o_ref, send_sems, recv_sems, loc_sem):
        x, y, c = _coords()
        me = 4 * x + 2 * y + c
        flips = [(fx, fy, fc) for fx in (0, 1) for fy in (0, 1) for fc in (0, 1) if fx + fy + fc > 0]
        peers = [(x ^ fx, y ^ fy, c ^ fc) for fx, fy, fc in flips]
        mine = pltpu.make_async_copy(a_ref, o_ref.at[me], loc_sem)
        mine.start()
        sends = [pltpu.make_async_remote_copy(a_ref, o_ref.at[me], send_sems.at[k], recv_sems.at[k],
                                              device_id=p, device_id_type=MESH) for k, p in enumerate(peers)]
        for cp in sends:
            cp.start()
        for k, (px, py, pc) in enumerate(peers):
            pltpu.make_async_remote_copy(a_ref, o_ref.at[4 * px + 2 * py + pc], send_sems.at[k], recv_sems.at[k],
                                         device_id=(px, py, pc), device_id_type=MESH).wait_recv()
        for cp in sends:
            cp.wait_send()
        mine.wait()

    return pl.pallas_call(
        body, out_shape=_sd((8,) + a.shape, a.dtype), in_specs=[_ANY], out_specs=_ANY,
        scratch_shapes=[pltpu.SemaphoreType.DMA((7,)), pltpu.SemaphoreType.DMA((7,)), pltpu.SemaphoreType.DMA],
        name=name, compiler_params=pltpu.CompilerParams(has_side_effects=True))(a)


def sum_blocks(name, a):
    k, r, c = a.shape
    tr = _pick(r, (1024, 256, 128, 64, 32, 16, 8))

    def body(a_ref, o_ref):
        acc = a_ref[0].astype(F32)
        for s in range(1, k):
            acc = acc + a_ref[s].astype(F32)
        o_ref[...] = acc

    return pl.pallas_call(body, out_shape=_sd((r, c), F32), grid=(r // tr,),
                          in_specs=[pl.BlockSpec((k, tr, c), lambda i: (0, i, 0))], out_specs=pl.BlockSpec((tr, c), lambda i: (i, 0)),
                          name=name, compiler_params=_cp(("parallel",)))(a)


BIG = ("w_mod", "w_in", "w_mla_uq", "w_mla_ukv", "w_p_ssm", "w_p_swa", "w_p_mla", "w_out", "w_ffn_in", "w_ffn_out")
COL_SHARDED = ("w_mod", "w_in", "w_mla_uq", "w_mla_ukv", "w_ffn_in")
SMALL = ("c_ctx", "b_mod", "norm1_g", "norm2_g", "ssm_conv_w", "ssm_conv_b", "ssm_dt_bias", "ssm_a_log", "ssm_d",
         "ssm_norm_g", "swa_q_norm_g", "swa_k_norm_g", "swa_sink", "mla_q_lat_g", "mla_kv_lat_g", "mla_q_norm_g",
         "mla_k_norm_g")
WEIGHTS = ("c_ctx", "w_mod", "b_mod", "norm1_g", "norm2_g", "w_in", "ssm_conv_w", "ssm_conv_b", "ssm_dt_bias", "ssm_a_log",
           "ssm_d", "ssm_norm_g", "swa_q_norm_g", "swa_k_norm_g", "swa_sink", "mla_q_lat_g", "mla_kv_lat_g", "w_mla_uq",
           "w_mla_ukv", "mla_q_norm_g", "mla_k_norm_g", "w_p_ssm", "w_p_swa", "w_p_mla", "w_out", "w_ffn_in", "w_ffn_out")


def pack_w_in(w):
    z = lambda k: jnp.zeros((w.shape[0], k), w.dtype)
    return jnp.concatenate([w[:, 4832:7904], w[:, 2400:3424], w[:, 3424:4448], w[:, 0:1536], w[:, 1568:1824], w[:, 1824:2080],
                            w[:, 2080:2336], w[:, 2336:2400], w[:, 1536:1568], z(32), z(128), w[:, 4448:4832]], axis=1)


def unpack_w_in(g):
    return jnp.concatenate([g[:, 5120:6656], g[:, 7488:7520], g[:, 6656:6912], g[:, 6912:7168], g[:, 7168:7424], g[:, 7424:7488],
                            g[:, 3072:4096], g[:, 4096:5120], g[:, 7680:8064], g[:, 0:3072]], axis=1)


def pack_ukv(w):
    return w.reshape(MLA_KVRANK, MLA_H, 2, 128).transpose(0, 2, 1, 3).reshape(MLA_KVRANK, 2048)


def unpack_ukv(g):
    return g.reshape(MLA_KVRANK, 2, MLA_H, 128).transpose(0, 2, 1, 3).reshape(MLA_KVRANK, 2048)


def pack_uq(w):
    return jnp.pad(w.reshape(MLA_QRANK, MLA_H, 192), ((0, 0), (0, 0), (0, 64))).reshape(MLA_QRANK, 2048)


def unpack_uq(g):
    return g.reshape(MLA_QRANK, MLA_H, 256)[:, :, :192].reshape(MLA_QRANK, 1536)


def rope_tables(nlat):
    t = jnp.arange(nlat, dtype=jnp.int32)
    r = (t // GRID_W).astype(F32)[:, None]
    col = (t % GRID_W).astype(F32)[:, None]

    def tab(nf, pad):
        inv = jnp.power(ROPE_BASE, -jnp.arange(nf, dtype=F32) / nf)
        ar, ac = r * inv, col * inv
        cos = jnp.concatenate([jnp.cos(ar), jnp.cos(ar), jnp.cos(ac), jnp.cos(ac), jnp.ones((nlat, pad), F32)], axis=1)
        sin = jnp.concatenate([-jnp.sin(ar), jnp.sin(ar), -jnp.sin(ac), jnp.sin(ac), jnp.zeros((nlat, pad), F32)], axis=1)
        cos = jnp.concatenate([cos, jnp.ones((NCTX, 128), F32)], axis=0)
        sin = jnp.concatenate([sin, jnp.zeros((NCTX, 128), F32)], axis=0)
        return cos, sin

    return tab(32, 0), tab(16, 64)


def _lanes(v, start, width=128):
    return jnp.zeros((1, width), F32).at[0, start:start + v.shape[0]].set(v)


def layer_fwd(i, xin, h, mod, p, tabs, nlat):
    t = "l%d_" % i
    n = xin.shape[0]
    (cos_s, sin_s), (cos_m, sin_m) = tabs
    u = mm(h, p["w_in"], F32, t + "in_proj")
    xbc = conv_fwd(t + "conv", u, p["conv_w"], p["conv_b"], nlat)
    dtrow = jnp.transpose(u[:, C_MISC + DT_LANE:C_MISC + DT_LANE + 32])
    nlc = nlat // Q
    yf, hs_f = ssd_fwd(t + "ssd_f", xbc, u, dtrow, p["bias_c"], p["alog_c"], p["bias_r"], p["alog_r"], nlc, False, 0)
    yb, hs_b = ssd_fwd(t + "ssd_b", xbc, u, dtrow, p["bias_c"], p["alog_c"], p["bias_r"], p["alog_r"], nlc, True, 1)
    ys = ssd_out_fwd(t + "ssd_out", yf, yb, xbc, u, p["ssm_norm_g"], p["d_exp"])
    qs, ks = swa_prep_fwd(t + "swa_prep", u, p["swa_q_g"], p["swa_k_g"], cos_s, sin_s)
    o_swa, lse_swa = swa_attention_fwd(t + "swa", qs, ks, u, p["sink"], p["swa_cfg"], nlat)
    ckv_n, cq_n = lat_norm_fwd(t + "lat_norm", u, p["kv_lat_g"], p["q_lat_g"])
    kv = mm(ckv_n, p["w_ukv"], F32, t + "ukv")
    qp = mm(cq_n, p["w_uq"], F32, t + "uq")
    km, qm, vm = mla_prep_fwd(t + "mla_prep", kv, qp, u, p["mla_q_g"], p["mla_k_g"], cos_m, sin_m)
    o_mla, lse_mla = mla_fwd(t + "mla_fwd", qm, km, vm, nlat)
    p1 = mm(ys, p["w_p_ssm"], F32, t + "p_ssm")
    p2 = mm(o_swa, p["w_p_swa"], F32, t + "p_swa")
    p3 = mm(o_mla, p["w_p_mla"], F32, t + "p_mla")
    merged = merge_fwd(t + "merge", u, p1, p2, p3)
    o = mm(merged, p["w_out"], F32, t + "out_proj")
    x1, h2 = resid_mod_fwd(t + "res1", xin, o, mod, 2, mod, 3, 4, p["norm2_g"], nlat // RT)
    gu = mm(h2, p["w_ffn_in"], F32, t + "ffn_in")
    a = swiglu_fwd(t + "swiglu", gu)
    f = mm(a, p["w_ffn_out"], F32, t + "ffn_out")
    saved = dict(xin=xin, h=h, u=u, xbc=xbc, dtrow=dtrow, yf=yf, yb=yb, hs_f=hs_f, hs_b=hs_b, ys=ys, qs=qs, ks=ks,
                 o_swa=o_swa, lse_swa=lse_swa, ckv_n=ckv_n, cq_n=cq_n, kv=kv, qp=qp, km=km, qm=qm, vm=vm, o_mla=o_mla,
                 lse_mla=lse_mla, p1=p1, p2=p2, p3=p3, merged=merged, o=o, x1=x1, h2=h2, gu=gu, a=a, f=f)
    del n
    return x1, f, saved


def layer_bwd(i, dx2, df, dgt2, sv, mod, p, tabs, nlat):
    t = "l%db_" % i
    (cos_s, sin_s), (cos_m, sin_m) = tabs
    g = {}
    nt = nlat // RT
    nlc = nlat // Q
    g["w_ffn_out"] = mm_tn(sv["a"], df, t + "wg_ffn_out")
    da = mm(df, p["w_ffn_out"], F32, t + "dg_ffn_out", trans_b=True)
    dgu = swiglu_bwd(t + "swiglu", sv["gu"], da)
    g["w_ffn_in"] = mm_tn(sv["h2"], dgu, t + "wg_ffn_in")
    dh2 = mm(dgu, p["w_ffn_in"], F32, t + "dg_ffn_in", trans_b=True)
    dx1, do, dgt1, dsh2, dsc2, g["norm2_g"] = resid_mod_bwd(t + "res1", sv["x1"], dx2, dh2, sv["o"], mod, 2, mod, 3, 4,
                                                              p["norm2_g"], nt)
    g["w_out"] = mm_tn(sv["merged"], do, t + "wg_out")
    dmerged = mm(do, p["w_out"], F32, t + "dg_out", trans_b=True)
    dp1, dp2, dp3, dgates = merge_bwd(t + "merge", sv["u"], sv["p1"], sv["p2"], sv["p3"], dmerged)
    g["w_p_ssm"] = mm_tn(sv["ys"], dp1, t + "wg_p_ssm")
    g["w_p_swa"] = mm_tn(sv["o_swa"], dp2, t + "wg_p_swa")
    g["w_p_mla"] = mm_tn(sv["o_mla"], dp3, t + "wg_p_mla")
    dys = mm(dp1, p["w_p_ssm"], F32, t + "dg_p_ssm", trans_b=True)
    do_swa = mm(dp2, p["w_p_swa"], BF16, t + "dg_p_swa", trans_b=True)
    do_mla = mm(dp3, p["w_p_mla"], BF16, t + "dg_p_mla", trans_b=True)
    dqm, dkm, dv_mla = mla_attention_bwd(t + "mla", sv["qm"], sv["km"], sv["vm"], sv["o_mla"], do_mla, sv["lse_mla"], nlat)
    dkv, dqp, dkr, g["mla_q_g"], g["mla_k_g"] = mla_prep_bwd(t + "mla_prep", sv["kv"], sv["qp"], sv["u"], p["mla_q_g"],
                                                             p["mla_k_g"], cos_m, sin_m, dkm, dqm, dv_mla)
    g["w_ukv"] = mm_tn(sv["ckv_n"], dkv, t + "wg_ukv")
    g["w_uq"] = mm_tn(sv["cq_n"], dqp, t + "wg_uq")
    dckv_n = mm(dkv, p["w_ukv"], F32, t + "dg_ukv", trans_b=True)
    dcq_n = mm(dqp, p["w_uq"], F32, t + "dg_uq", trans_b=True)
    dckv, dcq, g["kv_lat_g"], g["q_lat_g"] = lat_norm_bwd(t + "lat_norm", sv["u"], p["kv_lat_g"], p["q_lat_g"], dckv_n, dcq_n)
    dqs, dks, dv_swa, g["sink"] = swa_attention_bwd(t + "swa", sv["qs"], sv["ks"], sv["u"], sv["o_swa"], do_swa, sv["lse_swa"],
                                                p["sink"], p["swa_cfg"], nlat)
    dq, dk, dv, g["swa_q_g"], g["swa_k_g"] = swa_prep_bwd(t + "swa_prep", sv["u"], p["swa_q_g"], p["swa_k_g"], cos_s, sin_s,
                                                          dqs, dks, dv_swa)
    dy, dxs_skip, dz, g["ssm_norm_g"], g["d_exp"] = ssd_out_bwd(t + "ssd_out", sv["yf"], sv["yb"], sv["xbc"], sv["u"],
                                                                 p["ssm_norm_g"], p["d_exp"], dys)
    n = dy.shape[0]
    zbc = jnp.zeros((n, 256), F32)
    r_f = ssd_bwd(t + "ssd_f", sv["xbc"], sv["u"], sv["dtrow"], p["bias_c"], p["alog_c"], p["bias_r"], p["alog_r"],
                  sv["hs_f"], dy, (dxs_skip, zbc, zbc), nlc, False, 0)
    r_b = ssd_bwd(t + "ssd_b", sv["xbc"], sv["u"], sv["dtrow"], p["bias_c"], p["alog_c"], p["bias_r"], p["alog_r"],
                  sv["hs_b"], dy, (r_f[0], r_f[1], r_f[2]), nlc, True, 1)
    dact = jnp.concatenate([r_b[0], r_b[1], r_b[2]], axis=1)
    dxbc, g["conv_w"], g["conv_b"] = conv_bwd(t + "conv", sv["u"], dact, p["conv_w"], p["conv_b"], nlat)
    drow = jnp.concatenate([r_f[4][0] + r_f[4][1], r_b[4][0] + r_b[4][1]], axis=0)
    drow_t = jnp.pad(jnp.transpose(drow), ((0, 0), (DT_LANE, 128 - DT_LANE - 32)))
    dmisc = misc_combine(t + "misc", dkr, r_f[3], r_b[3], drow_t)
    g["bias_c"] = r_f[5] + r_b[5]
    g["alog_c"] = r_f[6] + r_b[6]
    g["bias_r"] = jnp.concatenate([r_f[7], r_b[7]], axis=0)
    g["alog_r"] = jnp.concatenate([r_f[8], r_b[8]], axis=0)
    du = jnp.concatenate([dgates, dz, dq, dxbc, dk, dv, dckv, dmisc, jnp.zeros((n, 128), BF16), dcq], axis=1)
    g["w_in"] = mm_tn(sv["h"], du, t + "wg_in")
    dh = mm(du, p["w_in"], F32, t + "dg_in", trans_b=True)
    g["mod"] = (dgt1, dsh2, dsc2, dgt2)
    return dx1, dh, g


def local_step(x, c, ctx, target, c_ctx, W, nlat):
    xin = jnp.concatenate([x, ctx], axis=0)
    n = xin.shape[0]
    nt = nlat // RT
    tabs = rope_tables(nlat)
    c8 = jnp.zeros((8, D), F32).at[0].set(c[0]).at[1].set(c_ctx)
    mods, silus = [], []
    for i in range(DEPTH):
        m8, s8 = mod_fwd("l%d_mod" % i, c8, W[i]["w_mod"], W[i]["b_mod"])
        mods.append(m8[0:2].reshape(2, 1, 6 * D))
        silus.append(s8)
    saved = []
    _, h = resid_mod_fwd("l0_norm1", xin, None, None, 0, mods[0], 0, 1, W[0]["norm1_g"], nt)
    xcur = xin
    for i in range(DEPTH):
        x1, f, sv = layer_fwd(i, xcur, h, mods[i], W[i], tabs, nlat)
        saved.append(sv)
        if i + 1 < DEPTH:
            xcur, h = resid_mod_fwd("l%d_res2" % i, x1, f, mods[i], 5, mods[i + 1], 0, 1, W[i + 1]["norm1_g"], nt)
    loss_v, dx2, df, dgt2 = resid_loss("loss", x1, f, mods[DEPTH - 1], 5, target, nt)
    grads = [None] * DEPTH
    for i in reversed(range(DEPTH)):
        dx1, dh, g = layer_bwd(i, dx2, df, dgt2, saved[i], mods[i], W[i], tabs, nlat)
        if i > 0:
            sv = saved[i]
            dx2, df, dgt2, dsh1, dsc1, g["norm1_g"] = resid_mod_bwd(
                "l%db_res2" % (i - 1), sv["xin"], dx1, dh, saved[i - 1]["f"], mods[i - 1], 5, mods[i], 0, 1,
                W[i]["norm1_g"], nt)
        else:
            dxin, _, _, dsh1, dsc1, g["norm1_g"] = resid_mod_bwd("l0b_norm1", saved[0]["xin"], dx1, dh, None, None, 0,
                                                                  mods[0], 0, 1, W[0]["norm1_g"], nt)
        dgt1, dsh2, dsc2, dgt2_i = g.pop("mod")
        dmod = jnp.concatenate([dsh1, dsc1, dgt1, dsh2, dsc2, dgt2_i], axis=2).reshape(2, 6 * D)
        dmod8 = jnp.zeros((8, 6 * D), F32).at[0:2].set(dmod)
        g["w_mod"] = mm_tn(silus[i], dmod8, "l%db_wg_mod" % i)
        dsilu = mm(dmod8, W[i]["w_mod"], F32, "l%db_dg_mod" % i, trans_b=True)
        dc8, g["b_mod"] = mod_small_bwd("l%db_mod_small" % i, c8, dsilu, dmod8)
        g["c8"] = dc8
        grads[i] = g
    del n
    return loss_v[0, 0], dxin, grads


def _big_shapes():
    return dict(w_mod=(2, 1024, 1536), w_in=(2, 1024, 1976), w_mla_uq=(2, 384, 384), w_mla_ukv=(2, 256, 512),
                w_p_ssm=(2, 256, 1024), w_p_swa=(2, 256, 1024), w_p_mla=(2, 256, 1024), w_out=(2, 256, 1024),
                w_ffn_in=(2, 1024, 1408), w_ffn_out=(2, 704, 1024))


PACK_ROWS = 14336


def _pack_big(d, dtype):
    parts = [d[k].astype(dtype).reshape(-1, 1024) for k in BIG]
    used = sum(p.shape[0] for p in parts)
    return jnp.concatenate(parts + [jnp.zeros((PACK_ROWS - used, 1024), dtype)], axis=0)


def _unpack_big(buf, lead):
    out = {}
    r0 = 0
    for k in BIG:
        sh = _big_shapes()[k]
        rows = sh[0] * sh[1] * sh[2] // 1024
        out[k] = buf[..., r0:r0 + rows, :].reshape(lead + sh)
        r0 += rows
    return out


def _full_from_chips(k, a):
    if k in COL_SHARDED:
        return a.transpose(1, 2, 0, 3).reshape(2, a.shape[2], 4 * a.shape[3])
    return a.transpose(1, 0, 2, 3).reshape(2, 4 * a.shape[2], a.shape[3])


def _chips_from_full(k, a):
    if k in COL_SHARDED:
        return a.reshape(a.shape[0], 4, a.shape[1] // 4).transpose(1, 0, 2)
    return a.reshape(4, a.shape[0] // 4, a.shape[1])


def _small_sizes():
    return dict(c_ctx=1024, b_mod=2 * 6144, norm1_g=2048, norm2_g=2048, ssm_conv_w=2 * 5 * 1536, ssm_conv_b=2 * 1536,
                ssm_dt_bias=64, ssm_a_log=64, ssm_d=32, ssm_norm_g=2048, swa_q_norm_g=256, swa_k_norm_g=256, swa_sink=16,
                mla_q_lat_g=768, mla_kv_lat_g=512, mla_q_norm_g=384, mla_k_norm_g=384)


def _pack_small(d):
    parts = []
    for k in SMALL:
        v = d[k].astype(F32).reshape(-1)
        parts.append(jnp.pad(v, (0, (-v.shape[0]) % 1024)))
    return jnp.concatenate(parts).reshape(-1, 128)


def _unpack_small(buf, shapes):
    flat = buf.reshape(-1)
    out = {}
    o = 0
    for k in SMALL:
        sz = _small_sizes()[k]
        out[k] = flat[o:o + sz].reshape(shapes[k])
        o += sz + (-sz) % 1024
    return out


def big_grads(grads):
    gfull = {k: [] for k in BIG}
    for i in range(DEPTH):
        g = grads[i]
        gfull["w_mod"].append(g["w_mod"])
        gfull["w_in"].append(unpack_w_in(g["w_in"]))
        gfull["w_mla_uq"].append(unpack_uq(g["w_uq"]))
        gfull["w_mla_ukv"].append(unpack_ukv(g["w_ukv"]))
        for k in ("w_p_ssm", "w_p_swa", "w_p_mla", "w_out", "w_ffn_in", "w_ffn_out"):
            gfull[k].append(g[k])
    return gfull


def small_grads(grads):
    gs = {}
    gs["c_ctx"] = sum(grads[i]["c8"][1] for i in range(DEPTH))
    st = lambda f: jnp.stack([f(grads[i]) for i in range(DEPTH)])
    gs["b_mod"] = st(lambda g: g["b_mod"][0])
    gs["norm1_g"] = st(lambda g: g["norm1_g"][0])
    gs["norm2_g"] = st(lambda g: g["norm2_g"][0])
    gs["ssm_conv_w"] = st(lambda g: g["conv_w"])
    gs["ssm_conv_b"] = st(lambda g: g["conv_b"][0])
    gs["ssm_dt_bias"] = st(lambda g: (g["bias_c"][0, DT_LANE:DT_LANE + 32] + g["bias_r"][:, 0]).reshape(2, 16))
    gs["ssm_a_log"] = st(lambda g: (g["alog_c"][0, DT_LANE:DT_LANE + 32] + g["alog_r"][:, 0]).reshape(2, 16))
    gs["ssm_d"] = st(lambda g: g["d_exp"].reshape(16, 64).sum(axis=1))
    gs["ssm_norm_g"] = st(lambda g: g["ssm_norm_g"][0])
    gs["swa_q_norm_g"] = st(lambda g: g["swa_q_g"][0])
    gs["swa_k_norm_g"] = st(lambda g: g["swa_k_g"][0])
    gs["swa_sink"] = st(lambda g: g["sink"][:, 0, 0])
    gs["mla_q_lat_g"] = st(lambda g: g["q_lat_g"][0])
    gs["mla_kv_lat_g"] = st(lambda g: g["kv_lat_g"][0])
    gs["mla_q_norm_g"] = st(lambda g: g["mla_q_g"][0, :192])
    gs["mla_k_norm_g"] = st(lambda g: g["mla_k_g"][0, :192])
    return gs


def layer_params(i, full, conv_full, sm, nlat):
    p = {}
    p["w_mod"] = full["w_mod"][i]
    p["w_in"] = pack_w_in(full["w_in"][i])
    p["w_uq"] = pack_uq(full["w_mla_uq"][i])
    p["w_ukv"] = pack_ukv(full["w_mla_ukv"][i])
    for k in ("w_p_ssm", "w_p_swa", "w_p_mla", "w_out", "w_ffn_in", "w_ffn_out"):
        p[k] = full[k][i]
    p["b_mod"] = sm["b_mod"][i][None]
    p["norm1_g"] = sm["norm1_g"][i][None]
    p["norm2_g"] = sm["norm2_g"][i][None]
    p["conv_w"] = conv_full[i]
    p["conv_b"] = sm["ssm_conv_b"][i][None]
    bias = sm["ssm_dt_bias"][i].reshape(32)
    alog = sm["ssm_a_log"][i].reshape(32)
    p["bias_c"] = _lanes(bias, DT_LANE)
    p["alog_c"] = _lanes(alog, DT_LANE)
    p["bias_r"] = bias[:, None]
    p["alog_r"] = alog[:, None]
    p["d_exp"] = jnp.repeat(sm["ssm_d"][i], 64)[None]
    p["ssm_norm_g"] = sm["ssm_norm_g"][i][None]
    p["swa_q_g"] = sm["swa_q_norm_g"][i][None]
    p["swa_k_g"] = sm["swa_k_norm_g"][i][None]
    p["sink"] = jnp.broadcast_to(sm["swa_sink"][i][:, None, None], (SWA_HQ, 1, 128))
    p["q_lat_g"] = sm["mla_q_lat_g"][i][None]
    p["kv_lat_g"] = sm["mla_kv_lat_g"][i][None]
    p["mla_q_g"] = _lanes(sm["mla_q_norm_g"][i], 0, 256)
    p["mla_k_g"] = _lanes(sm["mla_k_norm_g"][i], 0, 256)
    p["swa_cfg"] = dict(w=128, vw=128, hq=SWA_HQ, grp=SWA_HQ // SWA_HKV, vcol0=C_V // 128, scale=SWA_DH ** -0.5,
                        tq=256, tk=256, band=True)
    return p


def kernel(x, c, ctx, c_ctx, w_mod, b_mod, norm1_g, norm2_g, w_in, ssm_conv_w, ssm_conv_b, ssm_dt_bias, ssm_a_log, ssm_d, ssm_norm_g, swa_q_norm_g, swa_k_norm_g, swa_sink, mla_q_lat_g, mla_kv_lat_g, w_mla_uq, w_mla_ukv, mla_q_norm_g, mla_k_norm_g, w_p_ssm, w_p_swa, w_p_mla, w_out, w_ffn_in, w_ffn_out, loss_target, m_c_ctx, m_w_mod, m_b_mod, m_norm1_g, m_norm2_g, m_w_in, m_ssm_conv_w, m_ssm_conv_b, m_ssm_dt_bias, m_ssm_a_log, m_ssm_d, m_ssm_norm_g, m_swa_q_norm_g, m_swa_k_norm_g, m_swa_sink, m_mla_q_lat_g, m_mla_kv_lat_g, m_w_mla_uq, m_w_mla_ukv, m_mla_q_norm_g, m_mla_k_norm_g, m_w_p_ssm, m_w_p_swa, m_w_p_mla, m_w_out, m_w_ffn_in, m_w_ffn_out, v_c_ctx, v_w_mod, v_b_mod, v_norm1_g, v_norm2_g, v_w_in, v_ssm_conv_w, v_ssm_conv_b, v_ssm_dt_bias, v_ssm_a_log, v_ssm_d, v_ssm_norm_g, v_swa_q_norm_g, v_swa_k_norm_g, v_swa_sink, v_mla_q_lat_g, v_mla_kv_lat_g, v_w_mla_uq, v_w_mla_ukv, v_mla_q_norm_g, v_mla_k_norm_g, v_w_p_ssm, v_w_p_swa, v_w_p_mla, v_w_out, v_w_ffn_in, v_w_ffn_out):
    loc = dict(locals())
    w = {k: loc[k] for k in WEIGHTS}
    m = {k: loc["m_" + k] for k in WEIGHTS}
    v = {k: loc["v_" + k] for k in WEIGHTS}
    nlat = x.shape[1]

    gathered = _unpack_big(gather_chips("gather_weights", _pack_big(w, BF16)), (4,))
    full = {k: _full_from_chips(k, gathered[k]) for k in BIG}
    conv_sh = jnp.pad(ssm_conv_w.reshape(10, 384), ((0, 6), (0, 0)))
    conv_full = gather_chips("gather_conv", conv_sh)[:, :10].reshape(4, 2, 5, 384).transpose(1, 2, 0, 3).reshape(2, 5, 1536)

    W = [layer_params(i, full, conv_full, w, nlat) for i in range(DEPTH)]

    loss_loc, dx, grads = local_step(x[0], c, ctx[0], loss_target[0], c_ctx, W, nlat)

    gfull = big_grads(grads)
    by_chip = {k: jnp.stack([_chips_from_full(k, a) for a in gfull[k]], axis=1) for k in BIG}
    parts = [by_chip[k].astype(BF16).reshape(4, -1, 1024) for k in BIG]
    used = sum(p.shape[1] for p in parts)
    send = jnp.concatenate(parts + [jnp.zeros((4, PACK_ROWS - used, 1024), BF16)], axis=1)
    own, got = pair_split("pair_split", send)
    pair = add_cast("pair_sum", own, got, BF16)
    recv = scatter_chips("scatter_grads", pair)
    mine = sum_blocks("sum_chips", recv)
    gbig = _unpack_big(pair_join("join_cores", mine), ())

    gs = small_grads(grads)
    small_all = gather_all("gather_small", _pack_small(gs))
    small_sum = sum_blocks("sum_small", small_all)
    full_shapes = {k: (w[k].shape if k != "ssm_conv_w" else (2, 5, 1536)) for k in SMALL}
    gsmall = _unpack_small(small_sum, full_shapes)
    chip = 2 * lax.axis_index("x") + lax.axis_index("y")
    gsmall["ssm_conv_w"] = lax.dynamic_slice_in_dim(gsmall["ssm_conv_w"], chip * 384, 384, axis=2)

    grad = {**gbig, **gsmall}
    delta, new_m, new_v = {}, {}, {}
    sm = {k: _pack_small_local(d) for k, d in (("w", w), ("g", grad), ("m", m), ("v", v))}
    r = adamw("adamw_small", sm["w"], sm["g"], sm["m"], sm["v"])
    shapes = {k: w[k].shape for k in SMALL}
    for dst, buf in zip((delta, new_m, new_v), r):
        dst.update(_unpack_small_local(buf, shapes))
    for k in BIG:
        sh = w[k].shape
        r = adamw("adamw_" + k, *[a[k].reshape(sh[0] * sh[1], sh[2]) for a in (w, grad, m, v)])
        for dst, buf in zip((delta, new_m, new_v), r):
            dst[k] = buf.reshape(sh)

    loss = lax.psum(loss_loc, ("x", "y", "c"))
    return (loss, dx[None, :nlat], *[grad[k] for k in WEIGHTS], *[delta[k] for k in WEIGHTS],
            *[new_m[k] for k in WEIGHTS], *[new_v[k] for k in WEIGHTS])


def _pack_small_local(d):
    parts = []
    for k in SMALL:
        a = d[k].astype(F32).reshape(-1)
        parts.append(jnp.pad(a, (0, (-a.shape[0]) % 1024)))
    return jnp.concatenate(parts).reshape(-1, 128)


def _unpack_small_local(buf, shapes):
    flat = buf.reshape(-1)
    out = {}
    o = 0
    for k in SMALL:
        sz = math.prod(shapes[k])
        out[k] = flat[o:o + sz].reshape(shapes[k])
        o += sz + (-sz) % 1024
    return out
```

```python
import functools
import math

import jax
import jax.numpy as jnp
from jax import lax
from jax.experimental import pallas as pl
from jax.experimental.pallas import tpu as pltpu

F32 = jnp.float32
BF16 = jnp.bfloat16
MESH = pl.DeviceIdType.MESH

D = 1024
NCTX = 256
EPS = 1e-6
ROPE_BASE = 10000.0
GRID_W = 64
DEPTH = 2
Q = 128
N_HEADS_SSM = 16
SWA_HQ, SWA_HKV, SWA_DH, SWA_WIN = 8, 2, 128, 128
MLA_H, MLA_NOPE, MLA_ROPE, MLA_V = 8, 128, 64, 128
MLA_QRANK, MLA_KVRANK = 384, 256
FFN = 2816
RT = 256
VMEM_LIMIT = 56 << 20
NEG = -1e30
LOG2E = 1.4426950408889634

C_G1, C_G2, C_G3, C_Z, C_Q, C_XS, C_B, C_C, C_K, C_V, C_CKV, C_MISC, C_PAD, C_CQ = (
    0, 1024, 2048, 3072, 4096, 5120, 6144, 6400, 6656, 6912, 7168, 7424, 7552, 7680)
UW = 8064
DT_LANE = 64

ADAM_LR, ADAM_B1, ADAM_B2, ADAM_EPS, ADAM_WD, ADAM_STEP = 0.001, 0.9, 0.999, 1e-08, 0.01, 10


def _cp(sem):
    return pltpu.CompilerParams(dimension_semantics=sem, vmem_limit_bytes=VMEM_LIMIT)


def _pick(n, cands):
    for c in cands:
        if n % c == 0:
            return c
    return n


_TN = (1536, 1408, 1152, 1024, 896, 768, 512, 384, 256, 128)


def mm(a, b, out_dtype, name, trans_b=False):
    m, k = a.shape
    n = b.shape[0] if trans_b else b.shape[1]
    tm = _pick(m, (768, 512, 256, 128, 8))
    tn = _pick(n, _TN)
    tk = k if k <= 2048 else _pick(k, (1408, 1152, 1024, 896, 768, 512))
    nk = k // tk
    b_spec = (pl.BlockSpec((tn, tk), lambda i, j, kk: (j, kk)) if trans_b
              else pl.BlockSpec((tk, tn), lambda i, j, kk: (kk, j)))

    def body(a_ref, b_ref, o_ref, *acc):
        p = _d(a_ref[...], b_ref[...], ((1,), (1 if trans_b else 0,)))
        if nk == 1:
            o_ref[...] = p.astype(out_dtype)
        else:
            kk = pl.program_id(2)

            @pl.when(kk == 0)
            def _():
                acc[0][...] = p

            @pl.when(kk > 0)
            def _():
                acc[0][...] += p

            @pl.when(kk == nk - 1)
            def _():
                o_ref[...] = acc[0][...].astype(out_dtype)

    return pl.pallas_call(
        body, out_shape=jax.ShapeDtypeStruct((m, n), out_dtype), grid=(m // tm, n // tn, nk),
        in_specs=[pl.BlockSpec((tm, tk), lambda i, j, kk: (i, kk)), b_spec],
        out_specs=pl.BlockSpec((tm, tn), lambda i, j, kk: (i, j)),
        scratch_shapes=[] if nk == 1 else [pltpu.VMEM((tm, tn), F32)],
        name=name, compiler_params=_cp(("parallel", "parallel", "arbitrary")))(a, b)


def mm_tn(a, b, name, out_dtype=BF16):
    t, ka = a.shape
    _, nb = b.shape
    ta = _pick(ka, (1024, 1408, 768, 512, 384, 256, 128))
    tb = _pick(nb, _TN)
    tt = _pick(t, (768, 512, 256, 128, 8))
    nt = t // tt

    def body(a_ref, b_ref, o_ref, acc):
        p = _d(a_ref[...], b_ref[...], ((0,), (0,)))
        s = pl.program_id(2)

        @pl.when(s == 0)
        def _():
            acc[...] = p

        @pl.when(s > 0)
        def _():
            acc[...] += p

        @pl.when(s == nt - 1)
        def _():
            o_ref[...] = acc[...].astype(out_dtype)

    return pl.pallas_call(
        body, out_shape=jax.ShapeDtypeStruct((ka, nb), out_dtype), grid=(ka // ta, nb // tb, nt),
        in_specs=[pl.BlockSpec((tt, ta), lambda i, j, s: (s, i)), pl.BlockSpec((tt, tb), lambda i, j, s: (s, j))],
        out_specs=pl.BlockSpec((ta, tb), lambda i, j, s: (i, j)), scratch_shapes=[pltpu.VMEM((ta, tb), F32)],
        name=name, compiler_params=_cp(("parallel", "parallel", "arbitrary")))(a, b)


def _rms(x, g, n=None):
    n = x.shape[-1] if n is None else n
    r = lax.rsqrt(jnp.sum(x * x, axis=-1, keepdims=True) * (1.0 / n) + EPS)
    return x * r * g


def _silu(x):
    return x * jax.nn.sigmoid(x)


def _modulate(x, g, sc, sh):
    return _rms(x, g) * (1.0 + sc) + sh


def _swap(x, s):
    ax = x.ndim - 1
    w = x.shape[ax]
    lane = lax.broadcasted_iota(jnp.int32, x.shape, ax)
    lo = (lane & s) == 0
    return jnp.where(lo, pltpu.roll(x, w - s, ax), pltpu.roll(x, s, ax))


@functools.partial(jax.custom_vjp, nondiff_argnums=(3,))
def _rope(x, cos, sin, s):
    return x * cos + _swap(x, s) * sin


def _rope_fwd(x, cos, sin, s):
    return _rope(x, cos, sin, s), (cos, sin)


def _rope_bwd(s, res, g):
    cos, sin = res
    return g * cos - _swap(g, s) * sin, jnp.zeros_like(cos), jnp.zeros_like(sin)


_rope.defvjp(_rope_fwd, _rope_bwd)


@jax.custom_vjp
def _softplus(x):
    return jnp.maximum(x, 0.0) + jnp.log(1.0 + jnp.exp(-jnp.abs(x)))


def _softplus_fwd(x):
    return _softplus(x), x


def _softplus_bwd(x, g):
    return (g * jax.nn.sigmoid(x),)


_softplus.defvjp(_softplus_fwd, _softplus_bwd)


def _d(a, b, dims):
    return lax.dot_general(a.astype(BF16), b.astype(BF16), (dims, ((), ())), preferred_element_type=F32)


@jax.custom_vjp
def bdot(a, b):
    return _d(a, b, ((1,), (0,)))


bdot.defvjp(lambda a, b: (bdot(a, b), (a, b)),
            lambda r, g: (_d(g, r[1], ((1,), (1,))), _d(r[0], g, ((0,), (0,)))))


@jax.custom_vjp
def bdot_nt(a, b):
    return _d(a, b, ((1,), (1,)))


bdot_nt.defvjp(lambda a, b: (bdot_nt(a, b), (a, b)),
               lambda r, g: (_d(g, r[1], ((1,), (0,))), _d(g, r[0], ((0,), (0,)))))


@jax.custom_vjp
def bdot_tn(a, b):
    return _d(a, b, ((0,), (0,)))


bdot_tn.defvjp(lambda a, b: (bdot_tn(a, b), (a, b)),
               lambda r, g: (_d(r[1], g, ((1,), (1,))), _d(r[0], g, ((1,), (0,)))))


def _tri(rev):
    i = lax.broadcasted_iota(jnp.int32, (Q, Q), 0)
    j = lax.broadcasted_iota(jnp.int32, (Q, Q), 1)
    return (i <= j) if rev else (i >= j)


def _split3(a):
    hi = a.astype(BF16)
    r = a - hi.astype(F32)
    mid = r.astype(BF16)
    lo = (r - mid.astype(F32)).astype(BF16)
    return hi, mid, lo


def _cum_cols_impl(a, rev):
    t = _tri(rev).astype(BF16)
    return sum(jnp.dot(t, p, preferred_element_type=F32) for p in _split3(a))


def _cum_rows_impl(a, rev):
    t = _tri(not rev).astype(BF16)
    return sum(jnp.dot(p, t, preferred_element_type=F32) for p in _split3(a))


@functools.partial(jax.custom_vjp, nondiff_argnums=(1,))
def cum_cols(a, rev):
    return _cum_cols_impl(a, rev)


cum_cols.defvjp(lambda a, rev: (_cum_cols_impl(a, rev), None), lambda rev, _, g: (_cum_cols_impl(g, not rev),))


@functools.partial(jax.custom_vjp, nondiff_argnums=(1,))
def cum_rows(a, rev):
    return _cum_rows_impl(a, rev)


cum_rows.defvjp(lambda a, rev: (_cum_rows_impl(a, rev), None), lambda rev, _, g: (_cum_rows_impl(g, not rev),))


def _rs(w, cb=0):
    return pl.BlockSpec((RT, w), lambda i: (i, cb))


def _ps(shape):
    nd = len(shape)
    return pl.BlockSpec(shape, lambda i: (0,) * nd)


def _gs(w, cb, nlat):
    return pl.BlockSpec((1, 1, w), lambda i: (i // nlat, 0, cb))


def _rowcall(name, body, n, ins, outs, scratch=()):
    return pl.pallas_call(
        body, out_shape=[o[0] for o in outs], grid=(n // RT,), in_specs=[s for _, s in ins],
        out_specs=[s for _, s in outs], scratch_shapes=list(scratch), name=name,
        compiler_params=_cp(("arbitrary",)))(*[a for a, _ in ins])


def _acc(ref, val, first):
    @pl.when(first)
    def _():
        ref[...] = val

    @pl.when(jnp.logical_not(first))
    def _():
        ref[...] += val


def _sd(shape, dt):
    return jax.ShapeDtypeStruct(shape, dt)


def resid_mod_fwd(name, xp, o, mod_gt, gt_i, mod_n, sh_i, sc_i, norm_g, nlat):
    n = xp.shape[0]
    has_res = o is not None

    def body(*refs):
        if has_res:
            xp_ref, o_ref, gt_ref, sh_ref, sc_ref, g_ref, xn_ref, h_ref = refs
            xn = xp_ref[...] + gt_ref[0] * o_ref[...]
            xn_ref[...] = xn
        else:
            xp_ref, sh_ref, sc_ref, g_ref, h_ref = refs
            xn = xp_ref[...]
        h_ref[...] = _modulate(xn, g_ref[...], sc_ref[0], sh_ref[0]).astype(BF16)

    ins = [(xp, _rs(D))]
    if has_res:
        ins += [(o, _rs(D)), (mod_gt, _gs(D, gt_i, nlat))]
    ins += [(mod_n, _gs(D, sh_i, nlat)), (mod_n, _gs(D, sc_i, nlat)), (norm_g, _ps((1, D)))]
    outs = ([(_sd((n, D), F32), _rs(D))] if has_res else []) + [(_sd((n, D), BF16), _rs(D))]
    r = _rowcall(name, body, n, ins, outs)
    return (r[0], r[1]) if has_res else (xp, r[0])


def resid_mod_bwd(name, xn, dxn, dh, o, mod_gt, gt_i, mod_n, sh_i, sc_i, norm_g, nlat):
    n = xn.shape[0]
    has_res = o is not None

    def body(*refs):
        i = pl.program_id(0)
        if has_res:
            (xn_ref, dxn_ref, dh_ref, o_ref, gt_ref, sh_ref, sc_ref, g_ref,
             dx_ref, do_ref, dgt_ref, dsh_ref, dsc_ref, dg_ref) = refs
        else:
            xn_ref, dxn_ref, dh_ref, sh_ref, sc_ref, g_ref, dx_ref, dsh_ref, dsc_ref, dg_ref = refs
        _, vjp = jax.vjp(_modulate, xn_ref[...], g_ref[...], sc_ref[0], sh_ref[0])
        dx, dg, dsc, dsh = vjp(dh_ref[...])
        dx = dx + dxn_ref[...]
        dx_ref[...] = dx
        gfirst = (i == 0) | (i == nlat)
        _acc(dg_ref, dg, i == 0)
        _acc(dsh_ref, dsh[None], gfirst)
        _acc(dsc_ref, dsc[None], gfirst)
        if has_res:
            do_ref[...] = (gt_ref[0] * dx).astype(BF16)
            _acc(dgt_ref, jnp.sum(dx * o_ref[...], axis=0, keepdims=True)[None], gfirst)

    ins = [(xn, _rs(D)), (dxn, _rs(D)), (dh, _rs(D))]
    if has_res:
        ins += [(o, _rs(D)), (mod_gt, _gs(D, gt_i, nlat))]
    ins += [(mod_n, _gs(D, sh_i, nlat)), (mod_n, _gs(D, sc_i, nlat)), (norm_g, _ps((1, D)))]
    gacc = (_sd((2, 1, D), F32), _gs(D, 0, nlat))
    outs = [(_sd((n, D), F32), _rs(D))]
    if has_res:
        outs += [(_sd((n, D), BF16), _rs(D)), gacc]
    outs += [gacc, gacc, (_sd((1, D), F32), _ps((1, D)))]
    r = _rowcall(name, body, n, ins, outs)
    if has_res:
        return r
    return r[0], None, None, r[1], r[2], r[3]


def resid_loss(name, xp, o, mod_gt, gt_i, target, nlat):
    n = xp.shape[0]

    def body(xp_ref, o_ref, gt_ref, t_ref, loss_ref, dx_ref, do_ref, dgt_ref):
        i = pl.program_id(0)
        gt = gt_ref[0]

        @pl.when(i < nlat)
        def _():
            err = xp_ref[...] + gt * o_ref[...] - t_ref[...]
            dx = err * (1.0 / D)
            dx_ref[...] = dx
            do_ref[...] = (gt * dx).astype(BF16)
            _acc(loss_ref, jnp.full((1, 128), 0.5 / D, F32) * jnp.sum(err * err), i == 0)
            _acc(dgt_ref, jnp.sum(dx * o_ref[...], axis=0, keepdims=True)[None], i == 0)

        @pl.when(i >= nlat)
        def _():
            dx_ref[...] = jnp.zeros((RT, D), F32)
            do_ref[...] = jnp.zeros((RT, D), BF16)
            dgt_ref[...] = jnp.zeros((1, 1, D), F32)

    tgt_spec = pl.BlockSpec((RT, D), lambda i: (jnp.minimum(i, nlat - 1), 0))
    ins = [(xp, _rs(D)), (o, _rs(D)), (mod_gt, _gs(D, gt_i, nlat)), (target, tgt_spec)]
    outs = [(_sd((1, 128), F32), _ps((1, 128))), (_sd((n, D), F32), _rs(D)), (_sd((n, D), BF16), _rs(D)),
            (_sd((2, 1, D), F32), _gs(D, 0, nlat))]
    return _rowcall(name, body, n, ins, outs)


def mod_fwd(name, c8, w_mod, b_mod):
    tn = 1536

    def body(c_ref, w_ref, b_ref, o_ref, s_ref):
        s = _silu(c_ref[...]).astype(BF16)
        s_ref[...] = s
        o_ref[...] = jnp.dot(s, w_ref[...], preferred_element_type=F32) + b_ref[...]

    return pl.pallas_call(
        body, out_shape=[_sd((8, 6 * D), F32), _sd((8, D), BF16)], grid=(6 * D // tn,),
        in_specs=[pl.BlockSpec((8, D), lambda j: (0, 0)), pl.BlockSpec((D, tn), lambda j: (0, j)),
                  pl.BlockSpec((1, tn), lambda j: (0, j))],
        out_specs=[pl.BlockSpec((8, tn), lambda j: (0, j)), pl.BlockSpec((8, D), lambda j: (0, 0))],
        name=name, compiler_params=_cp(("arbitrary",)))(c8, w_mod, b_mod)


def mod_small_bwd(name, c8, dsilu, dmod8):
    def body(c_ref, ds_ref, dm_ref, dc_ref, db_ref):
        _, vjp = jax.vjp(_silu, c_ref[...])
        dc_ref[...] = vjp(ds_ref[...])[0]
        db_ref[...] = jnp.sum(dm_ref[...], axis=0, keepdims=True)

    return pl.pallas_call(
        body, out_shape=[_sd((8, D), F32), _sd((1, 6 * D), F32)], grid=(1,),
        in_specs=[pl.BlockSpec((8, D), lambda j: (0, 0)), pl.BlockSpec((8, D), lambda j: (0, 0)),
                  pl.BlockSpec((8, 6 * D), lambda j: (0, 0))],
        out_specs=[pl.BlockSpec((8, D), lambda j: (0, 0)), pl.BlockSpec((1, 6 * D), lambda j: (0, 0))],
        name=name, compiler_params=_cp(("arbitrary",)))(c8, dsilu, dmod8)


def _conv_taps(x, nlat):
    n = x.shape[0]
    r = lax.broadcasted_iota(jnp.int32, x.shape, 0)
    lo = jnp.where(r < nlat, 0, nlat)
    hi = jnp.where(r < nlat, nlat, n)
    taps = []
    for o in (-2, -1, 0, 1, 2):
        xs = x if o == 0 else pltpu.roll(x, (-o) % n, 0)
        t = r + o
        taps.append(jnp.where((t >= lo) & (t < hi), xs, 0.0))
    return taps


def conv_fwd(name, u, w, b, nlat_rows):
    n = u.shape[0]

    def body(x_ref, w_ref, b_ref, o_ref):
        taps = _conv_taps(x_ref[...], nlat_rows)
        wv = w_ref[...]
        pre = b_ref[...] + sum(taps[k] * wv[k:k + 1, :] for k in range(5))
        o_ref[...] = _silu(pre)

    return pl.pallas_call(
        body, out_shape=_sd((n, 1536), F32), grid=(12,),
        in_specs=[pl.BlockSpec((n, 128), lambda j: (0, C_XS // 128 + j)), pl.BlockSpec((5, 128), lambda j: (0, j)),
                  pl.BlockSpec((1, 128), lambda j: (0, j))],
        out_specs=pl.BlockSpec((n, 128), lambda j: (0, j)),
        name=name, compiler_params=_cp(("parallel",)))(u, w, b)


def conv_bwd(name, u, dact, w, b, nlat_rows):
    n = u.shape[0]

    def body(x_ref, da_ref, w_ref, b_ref, dx_ref, dw_ref, db_ref):
        taps = _conv_taps(x_ref[...], nlat_rows)
        wv = w_ref[...]
        pre = b_ref[...] + sum(taps[k] * wv[k:k + 1, :] for k in range(5))
        s = jax.nn.sigmoid(pre)
        dpre = da_ref[...] * (s * (1.0 + pre * (1.0 - s)))
        db_ref[...] = jnp.sum(dpre, axis=0, keepdims=True)
        rows = lax.broadcasted_iota(jnp.int32, (5, 128), 0)
        dw = jnp.zeros((5, 128), F32)
        for k in range(5):
            dw = dw + jnp.where(rows == k, jnp.sum(dpre * taps[k], axis=0, keepdims=True), 0.0)
        dw_ref[...] = dw
        r = lax.broadcasted_iota(jnp.int32, dpre.shape, 0)
        lo = jnp.where(r < nlat_rows, 0, nlat_rows)
        hi = jnp.where(r < nlat_rows, nlat_rows, n)
        dx = jnp.zeros_like(dpre)
        for k in range(5):
            o = k - 2
            ds = dpre if o == 0 else pltpu.roll(dpre, o % n, 0)
            t = r - o
            dx = dx + jnp.where((t >= lo) & (t < hi), ds, 0.0) * wv[k:k + 1, :]
        dx_ref[...] = dx.astype(BF16)

    return pl.pallas_call(
        body, out_shape=[_sd((n, 1536), BF16), _sd((5, 1536), F32), _sd((1, 1536), F32)], grid=(12,),
        in_specs=[pl.BlockSpec((n, 128), lambda j: (0, C_XS // 128 + j)), pl.BlockSpec((n, 128), lambda j: (0, j)),
                  pl.BlockSpec((5, 128), lambda j: (0, j)), pl.BlockSpec((1, 128), lambda j: (0, j))],
        out_specs=[pl.BlockSpec((n, 128), lambda j: (0, j)), pl.BlockSpec((5, 128), lambda j: (0, j)),
                   pl.BlockSpec((1, 128), lambda j: (0, j))],
        name=name, compiler_params=_cp(("parallel",)))(u, dact, w, b)


def _ssd_chunk(rev, dirn, g, x4, bm, cm, misc, dtrow, bias_c, alog_c, bias_r, alog_r, h4):
    dt_c = _softplus(misc + bias_c)
    a_c = dt_c * (-jnp.exp(alog_c))
    dt_r = _softplus(dtrow + bias_r)
    a_r = dt_r * (-jnp.exp(alog_r))
    cs_c = cum_cols(a_c, rev)
    cs_r = cum_rows(a_r, rev)
    tot_c = jnp.sum(a_c, axis=0, keepdims=True)
    cb = bdot_nt(cm, bm)
    tri = _tri(rev)
    lane = lax.broadcasted_iota(jnp.int32, (1, 128), 1)
    row16 = lax.broadcasted_iota(jnp.int32, (16, 1), 0)
    prow = lax.broadcasted_iota(jnp.int32, (128, 1), 0)
    ys, hs = [], []
    for p in range(4):
        ydiag = 0.0
        wst = 0.0
        eoff = 0.0
        hscale = 0.0
        for e in range(2):
            hg = 8 * g + 2 * p + e
            oh_c = (lane == DT_LANE + 16 * dirn + hg).astype(F32)
            dt_h = jnp.sum(dt_c * oh_c, axis=1, keepdims=True)
            cs_h = jnp.sum(cs_c * oh_c, axis=1, keepdims=True)
            tot_h = jnp.sum(tot_c * oh_c, axis=1, keepdims=True)
            csr_h = jnp.sum(cs_r * (row16 == hg).astype(F32), axis=0, keepdims=True)
            seg = jnp.exp(jnp.where(tri, cs_h - csr_h, -jnp.inf))
            hm = ((lane < 64) if e == 0 else (lane >= 64)).astype(F32)
            ydiag = ydiag + bdot(cb * seg, x4[p] * (dt_h * hm))
            wst = wst + (dt_h * jnp.exp(tot_h - cs_h)) * hm
            eoff = eoff + jnp.exp(cs_h) * hm
            hscale = hscale + jnp.exp(tot_h) * ((prow < 64) if e == 0 else (prow >= 64)).astype(F32)
        ys.append(ydiag + bdot_nt(cm, h4[p]) * eoff)
        hs.append(h4[p] * hscale + bdot_tn(x4[p] * wst, bm))
    return ys, hs


def _ssd_specs(nlat_chunks, rev, dirn, bwd):
    nc = nlat_chunks + 2

    def chunk(s):
        if bwd:
            s = nc - 1 - s
        return (nlat_chunks + 1 - s) if rev else (s + nlat_chunks) % nc

    def step(s):
        return (nc - 1 - s) if bwd else s

    return dict(
        x=pl.BlockSpec((Q, 512), lambda g, s: (chunk(s), g)),
        b=pl.BlockSpec((Q, 128), lambda g, s: (chunk(s), 8 + g)),
        c=pl.BlockSpec((Q, 128), lambda g, s: (chunk(s), 10 + g)),
        misc=pl.BlockSpec((Q, 128), lambda g, s: (chunk(s), C_MISC // 128)),
        dtrow=pl.BlockSpec((16, Q), lambda g, s: (dirn, chunk(s))),
        p_c=pl.BlockSpec((1, 128), lambda g, s: (0, 0)),
        p_r=pl.BlockSpec((16, 1), lambda g, s: (dirn, 0)),
        y=pl.BlockSpec((Q, 512), lambda g, s: (chunk(s), g)),
        hsave=pl.BlockSpec((1, 1, 512, 128), lambda g, s: (g, step(s), 0, 0)),
        bc_out=pl.BlockSpec((Q, 128), lambda g, s: (chunk(s), g)),
        misc_out=pl.BlockSpec((1, Q, 128), lambda g, s: (g, chunk(s), 0)),
        dtrow_out=pl.BlockSpec((1, 16, Q), lambda g, s: (g, 0, chunk(s))),
        pacc_c=pl.BlockSpec((1, 128), lambda g, s: (0, 0)),
        pacc_r=pl.BlockSpec((16, 1), lambda g, s: (0, 0)),
    )


def ssd_fwd(name, xbc, u, dtrow, bias_c, alog_c, bias_r, alog_r, nlat_chunks, rev, dirn):
    n = xbc.shape[0]
    nc = nlat_chunks + 2
    sp = _ssd_specs(nlat_chunks, rev, dirn, False)

    def body(x_ref, b_ref, c_ref, m_ref, r_ref, bc_ref, ac_ref, br_ref, ar_ref, y_ref, hs_ref, h_s):
        g = pl.program_id(0)
        s = pl.program_id(1)

        @pl.when(s == 0)
        def _():
            h_s[...] = jnp.zeros((512, 128), F32)

        hs_ref[0, 0] = h_s[...]
        x4 = [x_ref[:, 128 * p:128 * p + 128] for p in range(4)]
        h4 = [h_s[128 * p:128 * p + 128, :] for p in range(4)]
        ys, hs = _ssd_chunk(rev, dirn, g, x4, b_ref[...], c_ref[...], m_ref[...], r_ref[...],
                            bc_ref[...], ac_ref[...], br_ref[...], ar_ref[...], h4)
        for p in range(4):
            y_ref[:, 128 * p:128 * p + 128] = ys[p]
            h_s[128 * p:128 * p + 128, :] = hs[p]

    return pl.pallas_call(
        body, out_shape=[_sd((n, 1024), F32), _sd((2, nc, 512, 128), F32)], grid=(2, nc),
        in_specs=[sp["x"], sp["b"], sp["c"], sp["misc"], sp["dtrow"], sp["p_c"], sp["p_c"], sp["p_r"], sp["p_r"]],
        out_specs=[sp["y"], sp["hsave"]], scratch_shapes=[pltpu.VMEM((512, 128), F32)],
        name=name, compiler_params=_cp(("arbitrary", "arbitrary")))(
            xbc, xbc, xbc, u, dtrow, bias_c, alog_c, bias_r, alog_r)


def ssd_bwd(name, xbc, u, dtrow, bias_c, alog_c, bias_r, alog_r, hsave, dy, acc, nlat_chunks, rev, dirn):
    n = xbc.shape[0]
    sp = _ssd_specs(nlat_chunks, rev, dirn, True)

    def body(x_ref, b_ref, c_ref, m_ref, r_ref, bc_ref, ac_ref, br_ref, ar_ref, hs_ref, dy_ref, ax_ref, ab_ref, acc_ref,
             dx_ref, db_ref, dc_ref, dm_ref, dr_ref, dbc_ref, dac_ref, dbr_ref, dar_ref, dh_s):
        g = pl.program_id(0)
        s = pl.program_id(1)

        @pl.when(s == 0)
        def _():
            dh_s[...] = jnp.zeros((512, 128), F32)

        x4 = [x_ref[:, 128 * p:128 * p + 128] for p in range(4)]
        h4 = [hs_ref[0, 0, 128 * p:128 * p + 128, :] for p in range(4)]
        fn = functools.partial(_ssd_chunk, rev, dirn, g)
        _, vjp = jax.vjp(fn, x4, b_ref[...], c_ref[...], m_ref[...], r_ref[...],
                         bc_ref[...], ac_ref[...], br_ref[...], ar_ref[...], h4)
        dys = [dy_ref[:, 128 * p:128 * p + 128] for p in range(4)]
        dhs = [dh_s[128 * p:128 * p + 128, :] for p in range(4)]
        dx4, db, dc, dm, dr, dbc, dac, dbr, dar, dh4 = vjp((dys, dhs))
        for p in range(4):
            dx_ref[:, 128 * p:128 * p + 128] = dx4[p] + ax_ref[:, 128 * p:128 * p + 128]
            dh_s[128 * p:128 * p + 128, :] = dh4[p]
        db_ref[...] = db + ab_ref[...]
        dc_ref[...] = dc + acc_ref[...]
        dm_ref[0] = dm
        dr_ref[0] = dr
        first = (g == 0) & (s == 0)
        _acc(dbc_ref, dbc, first)
        _acc(dac_ref, dac, first)
        _acc(dbr_ref, dbr, first)
        _acc(dar_ref, dar, first)

    ax, ab, ac = acc
    return pl.pallas_call(
        body,
        out_shape=[_sd((n, 1024), F32), _sd((n, 256), F32), _sd((n, 256), F32), _sd((2, n, 128), F32),
                   _sd((2, 16, n), F32), _sd((1, 128), F32), _sd((1, 128), F32), _sd((16, 1), F32), _sd((16, 1), F32)],
        grid=(2, nlat_chunks + 2),
        in_specs=[sp["x"], sp["b"], sp["c"], sp["misc"], sp["dtrow"], sp["p_c"], sp["p_c"], sp["p_r"], sp["p_r"],
                  sp["hsave"], sp["y"], sp["y"], sp["bc_out"], sp["bc_out"]],
        out_specs=[sp["y"], sp["bc_out"], sp["bc_out"], sp["misc_out"], sp["dtrow_out"],
                   sp["pacc_c"], sp["pacc_c"], sp["pacc_r"], sp["pacc_r"]],
        scratch_shapes=[pltpu.VMEM((512, 128), F32)],
        name=name, compiler_params=_cp(("arbitrary", "arbitrary")))(
            xbc, xbc, xbc, u, dtrow, bias_c, alog_c, bias_r, alog_r, hsave, dy, ax, ab, ac)


def _ssd_out(yf, yb, xs, z, g, dexp):
    return _rms((yf + yb + dexp * xs) * _silu(z), g)


def ssd_out_fwd(name, yf, yb, xbc, u, g, dexp):
    n = yf.shape[0]

    def body(yf_ref, yb_ref, xs_ref, z_ref, g_ref, d_ref, o_ref):
        o_ref[...] = _ssd_out(yf_ref[...], yb_ref[...], xs_ref[...], z_ref[...], g_ref[...], d_ref[...]).astype(BF16)

    return _rowcall(name, body, n,
                    [(yf, _rs(D)), (yb, _rs(D)), (xbc, _rs(D, 0)), (u, _rs(D, C_Z // D)), (g, _ps((1, D))), (dexp, _ps((1, D)))],
                    [(_sd((n, D), BF16), _rs(D))])[0]


def ssd_out_bwd(name, yf, yb, xbc, u, g, dexp, dys):
    n = yf.shape[0]

    def body(yf_ref, yb_ref, xs_ref, z_ref, g_ref, d_ref, dys_ref, dy_ref, dxs_ref, dz_ref, dg_ref, dd_ref):
        i = pl.program_id(0)
        _, vjp = jax.vjp(_ssd_out, yf_ref[...], yb_ref[...], xs_ref[...], z_ref[...], g_ref[...], d_ref[...])
        dyf, _, dxs, dz, dg, dd = vjp(dys_ref[...])
        dy_ref[...] = dyf
        dxs_ref[...] = dxs
        dz_ref[...] = dz.astype(BF16)
        _acc(dg_ref, dg, i == 0)
        _acc(dd_ref, dd, i == 0)

    return _rowcall(name, body, n,
                    [(yf, _rs(D)), (yb, _rs(D)), (xbc, _rs(D, 0)), (u, _rs(D, C_Z // D)), (g, _ps((1, D))), (dexp, _ps((1, D))),
                     (dys, _rs(D))],
                    [(_sd((n, D), F32), _rs(D)), (_sd((n, D), F32), _rs(D)), (_sd((n, D), BF16), _rs(D)),
                     (_sd((1, D), F32), _ps((1, D))), (_sd((1, D), F32), _ps((1, D)))])


def _normrope(x, g, cos, sin, s, n=None):
    return _rope(_rms(x, g, n), cos, sin, s)


def swa_prep_fwd(name, u, gq, gk, cos, sin):
    n = u.shape[0]

    def body(q_ref, k_ref, gq_ref, gk_ref, cos_ref, sin_ref, qs_ref, ks_ref):
        cs, sn = cos_ref[...], sin_ref[...]
        for h in range(SWA_HQ):
            sl = slice(128 * h, 128 * h + 128)
            qs_ref[:, sl] = _normrope(q_ref[:, sl], gq_ref[...], cs, sn, 32).astype(BF16)
        for h in range(SWA_HKV):
            sl = slice(128 * h, 128 * h + 128)
            ks_ref[:, sl] = _normrope(k_ref[:, sl], gk_ref[...], cs, sn, 32).astype(BF16)

    return _rowcall(name, body, n,
                    [(u, _rs(1024, C_Q // 1024)), (u, _rs(256, C_K // 256)), (gq, _ps((1, 128))), (gk, _ps((1, 128))),
                     (cos, _rs(128)), (sin, _rs(128))],
                    [(_sd((n, 1024), BF16), _rs(1024)), (_sd((n, 256), BF16), _rs(256))])


def swa_prep_bwd(name, u, gq, gk, cos, sin, dqs, dks, dv):
    n = u.shape[0]

    def body(q_ref, k_ref, gq_ref, gk_ref, cos_ref, sin_ref, dqs_ref, dks_ref, dv_ref,
             dq_ref, dk_ref, dvo_ref, dgq_ref, dgk_ref):
        i = pl.program_id(0)
        cs, sn = cos_ref[...], sin_ref[...]
        fn = lambda x, g: _normrope(x, g, cs, sn, 32)
        dgq = jnp.zeros((1, 128), F32)
        dgk = jnp.zeros((1, 128), F32)
        for h in range(SWA_HQ):
            sl = slice(128 * h, 128 * h + 128)
            _, vjp = jax.vjp(fn, q_ref[:, sl], gq_ref[...])
            dx, dg = vjp(dqs_ref[:, sl])
            dq_ref[:, sl] = dx.astype(BF16)
            dgq = dgq + dg
        for h in range(SWA_HKV):
            sl = slice(128 * h, 128 * h + 128)
            _, vjp = jax.vjp(fn, k_ref[:, sl], gk_ref[...])
            dx, dg = vjp(dks_ref[:, sl])
            dk_ref[:, sl] = dx.astype(BF16)
            dgk = dgk + dg
        dvo_ref[...] = dv_ref[...].astype(BF16)
        _acc(dgq_ref, dgq, i == 0)
        _acc(dgk_ref, dgk, i == 0)

    return _rowcall(name, body, n,
                    [(u, _rs(1024, C_Q // 1024)), (u, _rs(256, C_K // 256)), (gq, _ps((1, 128))), (gk, _ps((1, 128))),
                     (cos, _rs(128)), (sin, _rs(128)), (dqs, _rs(1024)), (dks, _rs(256)), (dv, _rs(256))],
                    [(_sd((n, 1024), BF16), _rs(1024)), (_sd((n, 256), BF16), _rs(256)), (_sd((n, 256), BF16), _rs(256)),
                     (_sd((1, 128), F32), _ps((1, 128))), (_sd((1, 128), F32), _ps((1, 128)))])


def lat_norm_fwd(name, u, g_kv, g_q):
    n = u.shape[0]

    def body(ckv_ref, cq_ref, gkv_ref, gq_ref, okv_ref, oq_ref):
        okv_ref[...] = _rms(ckv_ref[...], gkv_ref[...]).astype(BF16)
        oq_ref[...] = _rms(cq_ref[...], gq_ref[...]).astype(BF16)

    return _rowcall(name, body, n,
                    [(u, _rs(256, C_CKV // 256)), (u, _rs(384, C_CQ // 384)), (g_kv, _ps((1, 256))), (g_q, _ps((1, 384)))],
                    [(_sd((n, 256), BF16), _rs(256)), (_sd((n, 384), BF16), _rs(384))])


def lat_norm_bwd(name, u, g_kv, g_q, dkvn, dqn):
    n = u.shape[0]

    def body(ckv_ref, cq_ref, gkv_ref, gq_ref, dkvn_ref, dqn_ref, dckv_ref, dcq_ref, dgkv_ref, dgq_ref):
        i = pl.program_id(0)
        _, vjp = jax.vjp(_rms, ckv_ref[...], gkv_ref[...])
        dx, dg = vjp(dkvn_ref[...])
        dckv_ref[...] = dx.astype(BF16)
        _acc(dgkv_ref, dg, i == 0)
        _, vjp = jax.vjp(_rms, cq_ref[...], gq_ref[...])
        dx, dg = vjp(dqn_ref[...])
        dcq_ref[...] = dx.astype(BF16)
        _acc(dgq_ref, dg, i == 0)

    return _rowcall(name, body, n,
                    [(u, _rs(256, C_CKV // 256)), (u, _rs(384, C_CQ // 384)), (g_kv, _ps((1, 256))), (g_q, _ps((1, 384))),
                     (dkvn, _rs(256)), (dqn, _rs(384))],
                    [(_sd((n, 256), BF16), _rs(256)), (_sd((n, 384), BF16), _rs(384)),
                     (_sd((1, 256), F32), _ps((1, 256))), (_sd((1, 384), F32), _ps((1, 384)))])


def _lane_lt64(x):
    return (lax.broadcasted_iota(jnp.int32, (1, 128), 1) < 64).astype(F32) * x


def _mla_krope(misc, g, cos, sin):
    return _normrope(_lane_lt64(misc), g, cos, sin, 16, MLA_ROPE)


def mla_prep_fwd(name, kv, qp, u, qg, kg, cos, sin):
    n = kv.shape[0]

    def body(kv_ref, v_ref, q_ref, m_ref, qg_ref, kg_ref, cos_ref, sin_ref, km_ref, qm_ref, vm_ref):
        cs, sn = cos_ref[...], sin_ref[...]
        vm_ref[...] = v_ref[...].astype(BF16)
        kr = _mla_krope(m_ref[...], kg_ref[:, 128:256], cs, sn).astype(BF16)
        for h in range(MLA_H):
            km_ref[:, 256 * h:256 * h + 128] = _rms(kv_ref[:, 128 * h:128 * h + 128], kg_ref[:, 0:128]).astype(BF16)
            km_ref[:, 256 * h + 128:256 * h + 256] = kr
            qm_ref[:, 256 * h:256 * h + 128] = _rms(q_ref[:, 256 * h:256 * h + 128], qg_ref[:, 0:128]).astype(BF16)
            qm_ref[:, 256 * h + 128:256 * h + 256] = _normrope(
                q_ref[:, 256 * h + 128:256 * h + 256], qg_ref[:, 128:256], cs, sn, 16, MLA_ROPE).astype(BF16)

    return _rowcall(name, body, n,
                    [(kv, _rs(1024, 0)), (kv, _rs(1024, 1)), (qp, _rs(2048)), (u, _rs(128, C_MISC // 128)), (qg, _ps((1, 256))),
                     (kg, _ps((1, 256))), (cos, _rs(128)), (sin, _rs(128))],
                    [(_sd((n, 2048), BF16), _rs(2048)), (_sd((n, 2048), BF16), _rs(2048)), (_sd((n, 1024), BF16), _rs(1024))])


def mla_prep_bwd(name, kv, qp, u, qg, kg, cos, sin, dkm, dqm, dv):
    n = kv.shape[0]

    def body(kv_ref, q_ref, m_ref, qg_ref, kg_ref, cos_ref, sin_ref, dkm_ref, dqm_ref, dv_ref,
             dkv_ref, dq_ref, dkr_ref, dqg_ref, dkg_ref):
        i = pl.program_id(0)
        cs, sn = cos_ref[...], sin_ref[...]
        fr = lambda x, g: _normrope(x, g, cs, sn, 16, MLA_ROPE)
        dkg_n = jnp.zeros((1, 128), F32)
        dqg_n = jnp.zeros((1, 128), F32)
        dqg_r = jnp.zeros((1, 128), F32)
        dkr_sum = jnp.zeros((RT, 128), F32)
        for h in range(MLA_H):
            _, vjp = jax.vjp(_rms, kv_ref[:, 128 * h:128 * h + 128], kg_ref[:, 0:128])
            dx, dg = vjp(dkm_ref[:, 256 * h:256 * h + 128])
            dkv_ref[:, 128 * h:128 * h + 128] = dx.astype(BF16)
            dkg_n = dkg_n + dg
            dkr_sum = dkr_sum + dkm_ref[:, 256 * h + 128:256 * h + 256]
            _, vjp = jax.vjp(_rms, q_ref[:, 256 * h:256 * h + 128], qg_ref[:, 0:128])
            dx, dg = vjp(dqm_ref[:, 256 * h:256 * h + 128])
            dq_ref[:, 256 * h:256 * h + 128] = dx.astype(BF16)
            dqg_n = dqg_n + dg
            _, vjp = jax.vjp(fr, q_ref[:, 256 * h + 128:256 * h + 256], qg_ref[:, 128:256])
            dx, dg = vjp(dqm_ref[:, 256 * h + 128:256 * h + 256])
            dq_ref[:, 256 * h + 128:256 * h + 256] = dx.astype(BF16)
            dqg_r = dqg_r + dg
        _, vjp = jax.vjp(lambda m, g: _mla_krope(m, g, cs, sn), m_ref[...], kg_ref[:, 128:256])
        dm, dkg_r = vjp(dkr_sum)
        dkr_ref[...] = dm
        dkv_ref[:, 1024:2048] = dv_ref[...].astype(BF16)
        _acc(dqg_ref.at[:, 0:128], dqg_n, i == 0)
        _acc(dqg_ref.at[:, 128:256], dqg_r, i == 0)
        _acc(dkg_ref.at[:, 0:128], dkg_n, i == 0)
        _acc(dkg_ref.at[:, 128:256], dkg_r, i == 0)

    return _rowcall(name, body, n,
                    [(kv, _rs(1024, 0)), (qp, _rs(2048)), (u, _rs(128, C_MISC // 128)), (qg, _ps((1, 256))), (kg, _ps((1, 256))),
                     (cos, _rs(128)), (sin, _rs(128)), (dkm, _rs(2048)), (dqm, _rs(2048)), (dv, _rs(1024))],
                    [(_sd((n, 2048), BF16), _rs(2048)), (_sd((n, 2048), BF16), _rs(2048)), (_sd((n, 128), F32), _rs(128)),
                     (_sd((1, 256), F32), _ps((1, 256))), (_sd((1, 256), F32), _ps((1, 256)))])


def misc_combine(name, dkr, dm_f, dm_b, drow_t):
    n = dkr.shape[0]

    def body(a_ref, f_ref, b_ref, r_ref, o_ref):
        o_ref[...] = (a_ref[...] + f_ref[0] + f_ref[1] + b_ref[0] + b_ref[1] + r_ref[...]).astype(BF16)

    g2 = pl.BlockSpec((2, RT, 128), lambda i: (0, i, 0))
    return _rowcall(name, body, n, [(dkr, _rs(128)), (dm_f, g2), (dm_b, g2), (drow_t, _rs(128))],
                    [(_sd((n, 128), BF16), _rs(128))])[0]


def _merge(g1, g2, g3, p1, p2, p3):
    return jax.nn.sigmoid(g1) * p1 + jax.nn.sigmoid(g2) * p2 + jax.nn.sigmoid(g3) * p3


def merge_fwd(name, u, p1, p2, p3):
    n = u.shape[0]

    def body(g1, g2, g3, a, b, c, o_ref):
        o_ref[...] = _merge(g1[...], g2[...], g3[...], a[...], b[...], c[...]).astype(BF16)

    return _rowcall(name, body, n, [(u, _rs(D, 0)), (u, _rs(D, 1)), (u, _rs(D, 2)), (p1, _rs(D)), (p2, _rs(D)), (p3, _rs(D))],
                    [(_sd((n, D), BF16), _rs(D))])[0]


def merge_bwd(name, u, p1, p2, p3, dm):
    n = u.shape[0]

    def body(g1, g2, g3, a, b, c, dm_ref, d1, d2, d3, dg_ref):
        _, vjp = jax.vjp(_merge, g1[...], g2[...], g3[...], a[...], b[...], c[...])
        r = vjp(dm_ref[...])
        for k in range(3):
            dg_ref[:, D * k:D * k + D] = r[k].astype(BF16)
        d1[...] = r[3].astype(BF16)
        d2[...] = r[4].astype(BF16)
        d3[...] = r[5].astype(BF16)

    return _rowcall(name, body, n,
                    [(u, _rs(D, 0)), (u, _rs(D, 1)), (u, _rs(D, 2)), (p1, _rs(D)), (p2, _rs(D)), (p3, _rs(D)), (dm, _rs(D))],
                    [(_sd((n, D), BF16), _rs(D))] * 3 + [(_sd((n, 3 * D), BF16), _rs(3 * D))])


def _swiglu(g, u):
    return _silu(g) * u


def swiglu_fwd(name, gu):
    n = gu.shape[0]

    def body(g_ref, u_ref, o_ref):
        o_ref[...] = _swiglu(g_ref[...], u_ref[...]).astype(BF16)

    return _rowcall(name, body, n, [(gu, _rs(FFN, 0)), (gu, _rs(FFN, 1))], [(_sd((n, FFN), BF16), _rs(FFN))])[0]


def swiglu_bwd(name, gu, da):
    n = gu.shape[0]

    def body(g_ref, u_ref, da_ref, o_ref):
        _, vjp = jax.vjp(_swiglu, g_ref[...], u_ref[...])
        dg, du = vjp(da_ref[...])
        o_ref[:, 0:FFN] = dg.astype(BF16)
        o_ref[:, FFN:2 * FFN] = du.astype(BF16)

    return _rowcall(name, body, n, [(gu, _rs(FFN, 0)), (gu, _rs(FFN, 1)), (da, _rs(FFN))],
                    [(_sd((n, 2 * FFN), BF16), _rs(2 * FFN))])[0]


FLASH_ROWS = 256


def _fold_lanes(x, op):
    acc = x[:, 0:128]
    for b in range(1, x.shape[1] // 128):
        acc = op(acc, x[:, 128 * b:128 * b + 128])
    return acc


def _band_mask(tq, tk, i, kb):
    qp = i * tq + lax.broadcasted_iota(jnp.int32, (tq, tk), 0)
    kp = kb * tk + lax.broadcasted_iota(jnp.int32, (tq, tk), 1)
    return jnp.abs(qp - kp) <= SWA_WIN


def flash_fwd(name, qa, ka, va, *, w, vw, hq, grp, vcol0, scale, nlat, tq, tk, band, sink, ctx_q, prev=None):
    n = qa.shape[0]
    cblk = nlat // NCTX
    band = band and not ctx_q
    assert not band, "latent rows of a banded attention go through swa_fwd_lat"
    if ctx_q:
        tq = tk = NCTX
        grid = (hq, 1, 1)
        qmap = lambda h, i, kk: (cblk, h)
        kmap = lambda h, i, kk: (cblk, h // grp)
        vmap = lambda h, i, kk: (cblk, vcol0 + h // grp)
        omap = lambda h, i, kk: (cblk, h)
        lmap = lambda h, i, kk: (h, cblk, 0)
    else:
        nb = nlat // tk
        nk = 3 if band else nb
        grid = (hq, nlat // tq, nk)
        kb_of = (lambda i, kk: jnp.clip(i + kk - 1, 0, nb - 1)) if band else (lambda i, kk: kk)
        qmap = lambda h, i, kk: (i, h)
        kmap = lambda h, i, kk: (kb_of(i, kk), h // grp)
        vmap = lambda h, i, kk: (kb_of(i, kk), vcol0 + h // grp)
        omap = lambda h, i, kk: (i, h)
        lmap = lambda h, i, kk: (h, i, 0)
    nk = grid[2]
    extra = not ctx_q
    has_sink = sink is not None

    def body(*refs):
        refs = list(refs)
        q_ref, k_ref, v_ref = refs[:3]
        pos = 3
        if extra:
            ke_ref, ve_ref = refs[pos:pos + 2]
            pos += 2
        if has_sink:
            s_ref = refs[pos]
            pos += 1
        if prev is not None:
            pos += 2
        o_ref, l_ref, m_s, l_s, a_s = refs[pos:pos + 5]
        kk = pl.program_id(2)
        tr = min(tq, FLASH_ROWS)

        def step(kblk, vblk):
            for r in range(tq // tr):
                rows = slice(r * tr, (r + 1) * tr)
                s = _d(q_ref[rows, :], kblk, ((1,), (1,))) * (scale * LOG2E)
                m_prev = m_s[rows, :]
                m_new = jnp.maximum(m_prev, jnp.max(_fold_lanes(s, jnp.maximum), axis=1, keepdims=True))
                alpha = jnp.exp2(m_prev - m_new)
                p = jnp.exp2(s - m_new)
                l_s[rows, :] = alpha * l_s[rows, :] + _fold_lanes(p, jnp.add)
                a_s[rows, :] = alpha * a_s[rows, :] + _d(p, vblk, ((1,), (0,)))
                m_s[rows, :] = m_new

        @pl.when(kk == 0)
        def _():
            if has_sink:
                sv = jnp.max(s_ref[0], axis=1, keepdims=True) * LOG2E
                m_s[...] = jnp.zeros((tq, 1), F32) + sv
                l_s[...] = (lax.broadcasted_iota(jnp.int32, (tq, 128), 1) == 0).astype(F32)
            else:
                m_s[...] = jnp.full((tq, 1), NEG, F32)
                l_s[...] = jnp.zeros((tq, 128), F32)
            a_s[...] = jnp.zeros((tq, vw), F32)
            if extra:
                step(ke_ref[...], ve_ref[...])

        step(k_ref[...], v_ref[...])

        @pl.when(kk == nk - 1)
        def _():
            l = jnp.sum(l_s[...], axis=1, keepdims=True)
            o_ref[...] = (a_s[...] / l).astype(BF16)
            l_ref[0] = m_s[...] + jnp.log2(l)

    ins = [(qa, pl.BlockSpec((tq, w), qmap)), (ka, pl.BlockSpec((tk, w), kmap)), (va, pl.BlockSpec((tk, vw), vmap))]
    if extra:
        ins += [(ka, pl.BlockSpec((NCTX, w), lambda h, i, kk: (cblk, h // grp))),
                (va, pl.BlockSpec((NCTX, vw), lambda h, i, kk: (cblk, vcol0 + h // grp)))]
    if has_sink:
        ins += [(sink, pl.BlockSpec((1, 1, 128), lambda h, i, kk: (h, 0, 0)))]
    aliases = {}
    if prev is not None:
        any_spec = pl.BlockSpec(memory_space=pl.ANY)
        aliases = {len(ins): 0, len(ins) + 1: 1}
        ins += [(prev[0], any_spec), (prev[1], any_spec)]
    return pl.pallas_call(
        body, out_shape=[_sd((n, hq * vw), BF16), _sd((hq, n, 1), F32)], grid=grid,
        in_specs=[s for _, s in ins],
        out_specs=[pl.BlockSpec((tq, vw), omap), pl.BlockSpec((1, tq, 1), lmap)],
        scratch_shapes=[pltpu.VMEM((tq, 1), F32), pltpu.VMEM((tq, 128), F32), pltpu.VMEM((tq, vw), F32)],
        input_output_aliases=aliases, name=name,
        compiler_params=_cp(("parallel", "parallel", "arbitrary")))(*[a for a, _ in ins])


def flash_dq(name, qa, ka, va, oa, doa, lse, *, w, vw, hq, grp, vcol0, scale, nlat, tq, tk, band, sink, ctx_q, prev=None):
    n = qa.shape[0]
    cblk = nlat // NCTX
    band = band and not ctx_q
    if ctx_q:
        tq = tk = NCTX
        grid = (hq, 1, 1)
        qmap = lambda h, i, kk: (cblk, h)
        kmap = lambda h, i, kk: (cblk, h // grp)
        vmap = lambda h, i, kk: (cblk, vcol0 + h // grp)
        lmap = lambda h, i, kk: (h, cblk, 0)
    else:
        nb = nlat // tk
        grid = (hq, nlat // tq, 3 if band else nb)
        kb_of = (lambda i, kk: jnp.clip(i + kk - 1, 0, nb - 1)) if band else (lambda i, kk: kk)
        qmap = lambda h, i, kk: (i, h)
        kmap = lambda h, i, kk: (kb_of(i, kk), h // grp)
        vmap = lambda h, i, kk: (kb_of(i, kk), vcol0 + h // grp)
        lmap = lambda h, i, kk: (h, i, 0)
    nk = grid[2]
    nq = grid[1]
    extra = not ctx_q
    has_sink = sink is not None

    def body(*refs):
        refs = list(refs)
        q_ref, k_ref, v_ref, o_ref, do_ref, l_ref = refs[:6]
        pos = 6
        if extra:
            ke_ref, ve_ref = refs[pos:pos + 2]
            pos += 2
        if has_sink:
            s_ref = refs[pos]
            pos += 1
        if prev is not None:
            pos += 2
        dq_ref, dl_ref, ds_ref, acc_s, dl_s = refs[pos:pos + 5]
        i = pl.program_id(1)
        kk = pl.program_id(2)
        q = q_ref[...]
        do = do_ref[...]
        lse_v = l_ref[0]

        def step(kblk, vblk, mask):
            s = _d(q, kblk, ((1,), (1,))) * (scale * LOG2E)
            if mask is not None:
                s = jnp.where(mask, s, NEG)
            p = jnp.exp2(s - lse_v)
            dp = _d(do, vblk, ((1,), (1,)))
            ds = p * (dp - dl_s[...]) * scale
            acc_s[...] += _d(ds, kblk, ((1,), (0,)))

        @pl.when(kk == 0)
        def _():
            delta = jnp.sum(do * o_ref[...].astype(F32), axis=1, keepdims=True)
            dl_s[...] = delta
            acc_s[...] = jnp.zeros((tq, w), F32)
            if has_sink:
                sv = jnp.max(s_ref[0], axis=1, keepdims=True) * LOG2E
                dsk = jnp.sum(-jnp.exp2(sv - lse_v) * delta, axis=0, keepdims=True)
                _acc(ds_ref, jnp.zeros((1, 1, 128), F32) + dsk, i == 0)
            else:
                ds_ref[...] = jnp.zeros((1, 1, 128), F32)
            if extra:
                step(ke_ref[...], ve_ref[...], None)

        if band:
            kb = i + kk - 1

            @pl.when((kb >= 0) & (kb < nlat // tk))
            def _():
                step(k_ref[...], v_ref[...], _band_mask(tq, tk, i, kb))
        else:
            step(k_ref[...], v_ref[...], None)

        @pl.when(kk == nk - 1)
        def _():
            dq_ref[...] = acc_s[...]
            dl_ref[0] = dl_s[...]

    ins = [(qa, pl.BlockSpec((tq, w), qmap)), (ka, pl.BlockSpec((tk, w), kmap)), (va, pl.BlockSpec((tk, vw), vmap)),
           (oa, pl.BlockSpec((tq, vw), qmap)), (doa, pl.BlockSpec((tq, vw), qmap)), (lse, pl.BlockSpec((1, tq, 1), lmap))]
    if extra:
        ins += [(ka, pl.BlockSpec((NCTX, w), lambda h, i, kk: (cblk, h // grp))),
                (va, pl.BlockSpec((NCTX, vw), lambda h, i, kk: (cblk, vcol0 + h // grp)))]
    if has_sink:
        ins += [(sink, pl.BlockSpec((1, 1, 128), lambda h, i, kk: (h, 0, 0)))]
    aliases = {}
    if prev is not None:
        any_spec = pl.BlockSpec(memory_space=pl.ANY)
        aliases = {len(ins): 0, len(ins) + 1: 1}
        ins += [(prev[0], any_spec), (prev[1], any_spec)]
    del nq
    return pl.pallas_call(
        body, out_shape=[_sd((n, hq * w), F32), _sd((hq, n, 1), F32), _sd((hq, 1, 128), F32)], grid=grid,
        in_specs=[s for _, s in ins],
        out_specs=[pl.BlockSpec((tq, w), qmap), pl.BlockSpec((1, tq, 1), lmap),
                   pl.BlockSpec((1, 1, 128), lambda h, i, kk: (h, 0, 0))],
        scratch_shapes=[pltpu.VMEM((tq, w), F32), pltpu.VMEM((tq, 1), F32)],
        input_output_aliases=aliases, name=name,
        compiler_params=_cp(("parallel", "arbitrary", "arbitrary")))(*[a for a, _ in ins])


def flash_dkv(name, qa, ka, va, doa, lse, delta, *, w, vw, hkv, grp, vcol0, scale, nlat, tq, tk, band, ctx_k, prev=None):
    n = qa.shape[0]
    cblk = nlat // NCTX
    nqb = nlat // tq
    band = band and not ctx_k
    if ctx_k:
        tk = NCTX
        nqs = nqb
        grid = (hkv, 1, grp * nqs)
        kmap = lambda hk, j, t: (cblk, hk)
        vmap = lambda hk, j, t: (cblk, vcol0 + hk)
        dvmap = lambda hk, j, t: (cblk, hk)
        qb_of = lambda j, t: t % nqs
    else:
        nqs = 3 if band else nqb
        grid = (hkv, nlat // tk, grp * nqs)
        kmap = lambda hk, j, t: (j, hk)
        vmap = lambda hk, j, t: (j, vcol0 + hk)
        dvmap = lambda hk, j, t: (j, hk)
        qb_of = (lambda j, t: jnp.clip(j + t % nqs - 1, 0, nqb - 1)) if band else (lambda j, t: t % nqs)
    qmap = lambda hk, j, t: (qb_of(j, t), hk * grp + t // nqs)
    lmap = lambda hk, j, t: (hk * grp + t // nqs, qb_of(j, t), 0)

    def body(*refs):
        refs = list(refs)
        q_ref, k_ref, v_ref, do_ref, l_ref, dl_ref = refs[:6]
        pos = 6
        if ctx_k:
            qe_ref, doe_ref, le_ref, dle_ref = refs[pos:pos + 4]
            pos += 4
        if prev is not None:
            pos += 2
        dk_ref, dv_ref = refs[pos:pos + 2]
        j = pl.program_id(1)
        t = pl.program_id(2)
        kblk = k_ref[...]
        vblk = v_ref[...]

        def contrib(q, do, lse_v, dl_v, mask):
            s = _d(q, kblk, ((1,), (1,))) * (scale * LOG2E)
            if mask is not None:
                s = jnp.where(mask, s, NEG)
            p = jnp.exp2(s - lse_v)
            dp = _d(do, vblk, ((1,), (1,)))
            ds = p * (dp - dl_v) * scale
            return _d(ds, q, ((0,), (0,))), _d(p, do, ((0,), (0,)))

        @pl.when(t == 0)
        def _():
            dk = jnp.zeros((tk, w), F32)
            dv = jnp.zeros((tk, vw), F32)
            if ctx_k:
                for gi in range(grp):
                    a, b = contrib(qe_ref[:, w * gi:w * gi + w], doe_ref[:, vw * gi:vw * gi + vw], le_ref[gi], dle_ref[gi], None)
                    dk = dk + a
                    dv = dv + b
            dk_ref[...] = dk
            dv_ref[...] = dv

        def add(mask):
            a, b = contrib(q_ref[...], do_ref[...], l_ref[0], dl_ref[0], mask)
            dk_ref[...] += a
            dv_ref[...] += b

        if band:
            qb = j + t % nqs - 1

            @pl.when((qb >= 0) & (qb < nqb))
            def _():
                add(_band_mask(tq, tk, qb, j))
        else:
            add(None)

    ins = [(qa, pl.BlockSpec((tq, w), qmap)), (ka, pl.BlockSpec((tk, w), kmap)), (va, pl.BlockSpec((tk, vw), vmap)),
           (doa, pl.BlockSpec((tq, vw), qmap)), (lse, pl.BlockSpec((1, tq, 1), lmap)), (delta, pl.BlockSpec((1, tq, 1), lmap))]
    if ctx_k:
        ins += [(qa, pl.BlockSpec((NCTX, grp * w), lambda hk, j, t: (cblk, hk))),
                (doa, pl.BlockSpec((NCTX, grp * vw), lambda hk, j, t: (cblk, hk))),
                (lse, pl.BlockSpec((grp, NCTX, 1), lambda hk, j, t: (hk, cblk, 0))),
                (delta, pl.BlockSpec((grp, NCTX, 1), lambda hk, j, t: (hk, cblk, 0)))]
    aliases = {}
    if prev is not None:
        any_spec = pl.BlockSpec(memory_space=pl.ANY)
        aliases = {len(ins): 0, len(ins) + 1: 1}
        ins += [(prev[0], any_spec), (prev[1], any_spec)]
    return pl.pallas_call(
        body, out_shape=[_sd((n, hkv * w), F32), _sd((n, hkv * vw), F32)], grid=grid,
        in_specs=[s for _, s in ins],
        out_specs=[pl.BlockSpec((tk, w), kmap), pl.BlockSpec((tk, vw), dvmap)],
        input_output_aliases=aliases, name=name,
        compiler_params=_cp(("parallel", "parallel", "arbitrary")))(*[a for a, _ in ins])


def mla_fwd(name, qm, km, vm, nlat):
    n = qm.shape[0]
    t = NCTX
    nlt = nlat // t
    c = (MLA_NOPE + MLA_ROPE) ** -0.5 * LOG2E

    def body(q_ref, k_ref, v_ref, o_ref, l_ref):
        i = pl.program_id(1)

        def run(k, v):
            s = _d(q_ref[...], k, ((1,), (1,))) * c
            m = jnp.max(_fold_lanes(s, jnp.maximum), axis=1, keepdims=True)
            p = jnp.exp2(s - m)
            l = jnp.sum(_fold_lanes(p, jnp.add), axis=1, keepdims=True)
            o_ref[...] = (_d(p, v, ((1,), (0,))) / l).astype(BF16)
            l_ref[0] = m + jnp.log2(l)

        @pl.when(i < nlt)
        def _():
            run(k_ref[...], v_ref[...])

        @pl.when(i == nlt)
        def _():
            run(k_ref[nlat:n, :], v_ref[nlat:n, :])

    return pl.pallas_call(
        body, out_shape=[_sd((n, MLA_H * 128), BF16), _sd((MLA_H, n, 1), F32)], grid=(MLA_H, n // t),
        in_specs=[pl.BlockSpec((t, 256), lambda h, i: (i, h)), pl.BlockSpec((n, 256), lambda h, i: (0, h)),
                  pl.BlockSpec((n, 128), lambda h, i: (0, h))],
        out_specs=[pl.BlockSpec((t, 128), lambda h, i: (i, h)), pl.BlockSpec((1, t, 1), lambda h, i: (h, i, 0))],
        name=name, compiler_params=_cp(("parallel", "arbitrary")))(qm, km, vm)


def mla_dq(name, qm, km, vm, o, do, lse, nlat):
    n = qm.shape[0]
    t = NCTX
    nlt = nlat // t
    scale = (MLA_NOPE + MLA_ROPE) ** -0.5

    def body(q_ref, k_ref, v_ref, o_ref, do_ref, l_ref, dq_ref, dl_ref):
        i = pl.program_id(1)
        do = do_ref[...]
        delta = jnp.sum(do.astype(F32) * o_ref[...].astype(F32), axis=1, keepdims=True)
        dl_ref[0] = delta

        def run(k, v):
            s = _d(q_ref[...], k, ((1,), (1,))) * (scale * LOG2E)
            ds = jnp.exp2(s - l_ref[0]) * (_d(do, v, ((1,), (1,))) - delta) * scale
            dq_ref[...] = _d(ds, k, ((1,), (0,)))

        @pl.when(i < nlt)
        def _():
            run(k_ref[...], v_ref[...])

        @pl.when(i == nlt)
        def _():
            run(k_ref[nlat:n, :], v_ref[nlat:n, :])

    qspec = pl.BlockSpec((t, 256), lambda h, i: (i, h))
    ospec = pl.BlockSpec((t, 128), lambda h, i: (i, h))
    lspec = pl.BlockSpec((1, t, 1), lambda h, i: (h, i, 0))
    return pl.pallas_call(
        body, out_shape=[_sd((n, MLA_H * 256), F32), _sd((MLA_H, n, 1), F32)], grid=(MLA_H, n // t),
        in_specs=[qspec, pl.BlockSpec((n, 256), lambda h, i: (0, h)), pl.BlockSpec((n, 128), lambda h, i: (0, h)),
                  ospec, ospec, lspec],
        out_specs=[qspec, lspec],
        name=name, compiler_params=_cp(("parallel", "arbitrary")))(qm, km, vm, o, do, lse)


def mla_dkv(name, qm, km, vm, do, lse_row, delta_row, nlat):
    n = qm.shape[0]
    t = NCTX
    nlt = nlat // t
    scale = (MLA_NOPE + MLA_ROPE) ** -0.5

    def body(q_ref, k_ref, v_ref, do_ref, l_ref, dl_ref, dk_ref, dv_ref):
        j = pl.program_id(1)

        def run(q, do, lrow, drow):
            st = _d(k_ref[...], q, ((1,), (1,))) * (scale * LOG2E)
            pt = jnp.exp2(st - lrow)
            dv_ref[...] = _d(pt, do, ((1,), (0,)))
            dst = pt * (_d(v_ref[...], do, ((1,), (1,))) - drow) * scale
            dk_ref[...] = _d(dst, q, ((1,), (0,)))

        @pl.when(j < nlt)
        def _():
            run(q_ref[0:nlat, :], do_ref[0:nlat, :], l_ref[0, :, 0:nlat], dl_ref[0, :, 0:nlat])

        @pl.when(j == nlt)
        def _():
            run(q_ref[...], do_ref[...], l_ref[0], dl_ref[0])

    rspec = pl.BlockSpec((1, 1, n), lambda h, j: (h, 0, 0))
    return pl.pallas_call(
        body, out_shape=[_sd((n, MLA_H * 256), F32), _sd((n, MLA_H * 128), F32)], grid=(MLA_H, n // t),
        in_specs=[pl.BlockSpec((n, 256), lambda h, j: (0, h)), pl.BlockSpec((t, 256), lambda h, j: (j, h)),
                  pl.BlockSpec((t, 128), lambda h, j: (j, h)), pl.BlockSpec((n, 128), lambda h, j: (0, h)), rspec, rspec],
        out_specs=[pl.BlockSpec((t, 256), lambda h, j: (j, h)), pl.BlockSpec((t, 128), lambda h, j: (j, h))],
        name=name, compiler_params=_cp(("parallel", "arbitrary")))(qm, km, vm, do, lse_row, delta_row)


def mla_bwd(name, qm, km, vm, o, do, lse, nlat):
    n = qm.shape[0]
    t = NCTX
    nlt = nlat // t
    scale = (MLA_NOPE + MLA_ROPE) ** -0.5
    nchunk = 2 if (n // 128) % 2 == 0 else 1
    cw = n // nchunk

    def body(q_ref, k_ref, v_ref, o_ref, do_ref, l_ref, dq_ref, dkt_ref, dvt_ref):
        i = pl.program_id(1)

        @pl.when(i == 0)
        def _():
            dkt_ref[...] = jnp.zeros((256, n), F32)
            dvt_ref[...] = jnp.zeros((128, n), F32)

        q = q_ref[...]
        do = do_ref[...]
        delta = jnp.sum(do.astype(F32) * o_ref[...].astype(F32), axis=1, keepdims=True)

        def run(spans):
            dq = jnp.zeros((t, 256), F32)
            for a, b in spans:
                kc = k_ref[a:b, :]
                s = _d(q, kc, ((1,), (1,))) * (scale * LOG2E)
                p = jnp.exp2(s - l_ref[0])
                ds = (p * (_d(do, v_ref[a:b, :], ((1,), (1,))) - delta) * scale).astype(BF16)
                dq = dq + _d(ds, kc, ((1,), (0,)))
                dkt_ref[:, a:b] += _d(q, ds, ((0,), (0,)))
                dvt_ref[:, a:b] += _d(do, p, ((0,), (0,)))
            dq_ref[...] = dq

        @pl.when(i < nlt)
        def _():
            run([(c * cw, (c + 1) * cw) for c in range(nchunk)])

        @pl.when(i == nlt)
        def _():
            run([(nlat, n)])

    qspec = pl.BlockSpec((t, 256), lambda h, i: (i, h))
    ospec = pl.BlockSpec((t, 128), lambda h, i: (i, h))
    return pl.pallas_call(
        body, out_shape=[_sd((n, MLA_H * 256), F32), _sd((MLA_H * 256, n), F32), _sd((MLA_H * 128, n), F32)],
        grid=(MLA_H, n // t),
        in_specs=[qspec, pl.BlockSpec((n, 256), lambda h, i: (0, h)), pl.BlockSpec((n, 128), lambda h, i: (0, h)),
                  ospec, ospec, pl.BlockSpec((1, t, 1), lambda h, i: (h, i, 0))],
        out_specs=[qspec, pl.BlockSpec((256, n), lambda h, i: (h, 0)), pl.BlockSpec((128, n), lambda h, i: (h, 0))],
        name=name, compiler_params=_cp(("parallel", "arbitrary")))(qm, km, vm, o, do, lse)


def mla_attention_bwd(tag, qm, km, vm, o, do, lse, nlat):
    dq, dkt, dvt = mla_bwd(tag + "_bwd", qm, km, vm, o, do, lse, nlat)
    return dq, jnp.transpose(dkt), jnp.transpose(dvt)


SWA_T = 512


def _swa_window(t, nlat):
    t = min(t, nlat)
    return t, min(t + 2 * SWA_WIN, nlat)


def _win_start(i, t, wlen, nlat):
    return pl.multiple_of(jnp.clip(i * t - SWA_WIN, 0, nlat - wlen), 128)


def _win_mask(rows, cols, row0, col0):
    rp = row0 + lax.broadcasted_iota(jnp.int32, (rows, cols), 0)
    cp = col0 + lax.broadcasted_iota(jnp.int32, (rows, cols), 1)
    return jnp.abs(rp - cp) <= SWA_WIN


def swa_fwd_lat(name, qs, ks, u, sink, nlat):
    n = qs.shape[0]
    tq, wlen = _swa_window(SWA_T, nlat)
    grp = SWA_HQ // SWA_HKV
    scale = SWA_DH ** -0.5
    vcol0 = C_V // 128

    def body(q_ref, k_ref, v_ref, s_ref, o_ref, l_ref):
        i = pl.program_id(1)
        ws = _win_start(i, tq, wlen, nlat)
        q = q_ref[...]
        s1 = _d(q, k_ref[pl.ds(ws, wlen), :], ((1,), (1,))) * (scale * LOG2E)
        s1 = jnp.where(_win_mask(tq, wlen, i * tq, ws), s1, NEG)
        s2 = _d(q, k_ref[pl.ds(nlat, NCTX), :], ((1,), (1,))) * (scale * LOG2E)
        sv = jnp.max(s_ref[0], axis=1, keepdims=True) * LOG2E
        m = jnp.maximum(jnp.maximum(jnp.max(s1, axis=1, keepdims=True), jnp.max(s2, axis=1, keepdims=True)), sv)
        p1 = jnp.exp2(s1 - m)
        p2 = jnp.exp2(s2 - m)
        l = jnp.sum(p1, axis=1, keepdims=True) + jnp.sum(p2, axis=1, keepdims=True) + jnp.exp2(sv - m)
        acc = _d(p1, v_ref[pl.ds(ws, wlen), :], ((1,), (0,))) + _d(p2, v_ref[pl.ds(nlat, NCTX), :], ((1,), (0,)))
        o_ref[...] = (acc / l).astype(BF16)
        l_ref[0] = m + jnp.log2(l)

    return pl.pallas_call(
        body, out_shape=[_sd((n, SWA_HQ * 128), BF16), _sd((SWA_HQ, n, 1), F32)], grid=(SWA_HQ, nlat // tq),
        in_specs=[pl.BlockSpec((tq, 128), lambda h, i: (i, h)), pl.BlockSpec((n, 128), lambda h, i: (0, h // grp)),
                  pl.BlockSpec((n, 128), lambda h, i: (0, vcol0 + h // grp)), pl.BlockSpec((1, 1, 128), lambda h, i: (h, 0, 0))],
        out_specs=[pl.BlockSpec((tq, 128), lambda h, i: (i, h)), pl.BlockSpec((1, tq, 1), lambda h, i: (h, i, 0))],
        name=name, compiler_params=_cp(("parallel", "arbitrary")))(qs, ks, u, sink)


def swa_dq_lat(name, qs, ks, u, o, do, lse, sink, nlat):
    n = qs.shape[0]
    tq, wlen = _swa_window(SWA_T, nlat)
    grp = SWA_HQ // SWA_HKV
    scale = SWA_DH ** -0.5
    vcol0 = C_V // 128

    def body(q_ref, k_ref, v_ref, s_ref, o_ref, do_ref, l_ref, dq_ref, dl_ref, ds_ref):
        i = pl.program_id(1)
        ws = _win_start(i, tq, wlen, nlat)
        q = q_ref[...]
        do = do_ref[...]
        lse_v = l_ref[0]
        delta = jnp.sum(do.astype(F32) * o_ref[...].astype(F32), axis=1, keepdims=True)
        kw = k_ref[pl.ds(ws, wlen), :]
        kc = k_ref[pl.ds(nlat, NCTX), :]
        s1 = _d(q, kw, ((1,), (1,))) * (scale * LOG2E)
        s1 = jnp.where(_win_mask(tq, wlen, i * tq, ws), s1, NEG)
        s2 = _d(q, kc, ((1,), (1,))) * (scale * LOG2E)
        ds1 = jnp.exp2(s1 - lse_v) * (_d(do, v_ref[pl.ds(ws, wlen), :], ((1,), (1,))) - delta) * scale
        ds2 = jnp.exp2(s2 - lse_v) * (_d(do, v_ref[pl.ds(nlat, NCTX), :], ((1,), (1,))) - delta) * scale
        dq_ref[...] = _d(ds1, kw, ((1,), (0,))) + _d(ds2, kc, ((1,), (0,)))
        dl_ref[0] = delta
        sv = jnp.max(s_ref[0], axis=1, keepdims=True) * LOG2E
        dsk = jnp.sum(-jnp.exp2(sv - lse_v) * delta, axis=0, keepdims=True)
        _acc(ds_ref, jnp.zeros((1, 1, 128), F32) + dsk, i == 0)

    qspec = pl.BlockSpec((tq, 128), lambda h, i: (i, h))
    lspec = pl.BlockSpec((1, tq, 1), lambda h, i: (h, i, 0))
    return pl.pallas_call(
        body, out_shape=[_sd((n, SWA_HQ * 128), F32), _sd((SWA_HQ, n, 1), F32), _sd((SWA_HQ, 1, 128), F32)],
        grid=(SWA_HQ, nlat // tq),
        in_specs=[qspec, pl.BlockSpec((n, 128), lambda h, i: (0, h // grp)),
                  pl.BlockSpec((n, 128), lambda h, i: (0, vcol0 + h // grp)), pl.BlockSpec((1, 1, 128), lambda h, i: (h, 0, 0)),
                  qspec, qspec, lspec],
        out_specs=[qspec, lspec, pl.BlockSpec((1, 1, 128), lambda h, i: (h, 0, 0))],
        name=name, compiler_params=_cp(("parallel", "arbitrary")))(qs, ks, u, sink, o, do, lse)


def swa_dkv_lat(name, qs, ks, u, do, lse_row, delta_row, nlat):
    n = qs.shape[0]
    tk, wlen = _swa_window(SWA_T, nlat)
    grp = SWA_HQ // SWA_HKV
    scale = SWA_DH ** -0.5
    vcol0 = C_V // 128

    def body(q_ref, k_ref, v_ref, do_ref, l_ref, dl_ref, dk_ref, dv_ref):
        j = pl.program_id(1)
        ws = _win_start(j, tk, wlen, nlat)
        k = k_ref[...]
        v = v_ref[...]
        mask = _win_mask(tk, wlen, j * tk, ws)
        dk = jnp.zeros((tk, 128), F32)
        dv = jnp.zeros((tk, 128), F32)
        for gi in range(grp):
            qw = q_ref[pl.ds(ws, wlen), 128 * gi:128 * gi + 128]
            dow = do_ref[pl.ds(ws, wlen), 128 * gi:128 * gi + 128]
            st = jnp.where(mask, _d(k, qw, ((1,), (1,))) * (scale * LOG2E), NEG)
            pt = jnp.exp2(st - l_ref[gi, :, pl.ds(ws, wlen)])
            dv = dv + _d(pt, dow, ((1,), (0,)))
            dst = pt * (_d(v, dow, ((1,), (1,))) - dl_ref[gi, :, pl.ds(ws, wlen)]) * scale
            dk = dk + _d(dst, qw, ((1,), (0,)))
        dk_ref[...] = dk
        dv_ref[...] = dv

    rspec = pl.BlockSpec((grp, 1, n), lambda hk, j: (hk, 0, 0))
    return pl.pallas_call(
        body, out_shape=[_sd((n, SWA_HKV * 128), F32), _sd((n, SWA_HKV * 128), F32)], grid=(SWA_HKV, nlat // tk),
        in_specs=[pl.BlockSpec((n, grp * 128), lambda hk, j: (0, hk)), pl.BlockSpec((tk, 128), lambda hk, j: (j, hk)),
                  pl.BlockSpec((tk, 128), lambda hk, j: (j, vcol0 + hk)), pl.BlockSpec((n, grp * 128), lambda hk, j: (0, hk)),
                  rspec, rspec],
        out_specs=[pl.BlockSpec((tk, 128), lambda hk, j: (j, hk)), pl.BlockSpec((tk, 128), lambda hk, j: (j, hk))],
        name=name, compiler_params=_cp(("parallel", "arbitrary")))(qs, ks, u, do, lse_row, delta_row)


def swa_attention_fwd(tag, qs, ks, u, sink, cfg, nlat):
    o, lse = swa_fwd_lat(tag + "_fwd_lat", qs, ks, u, sink, nlat)
    return flash_fwd(tag + "_fwd_ctx", qs, ks, u, sink=sink, ctx_q=True, nlat=nlat, prev=(o, lse), **cfg)


def swa_attention_bwd(tag, qs, ks, u, o, do, lse, sink, cfg, nlat):
    n = qs.shape[0]
    dq, delta, ds1 = swa_dq_lat(tag + "_dq_lat", qs, ks, u, o, do, lse, sink, nlat)
    dq, delta, ds2 = flash_dq(tag + "_dq_ctx", qs, ks, u, o, do, lse, sink=sink, ctx_q=True, nlat=nlat, prev=(dq, delta), **cfg)
    dk, dv = swa_dkv_lat(tag + "_dkv_lat", qs, ks, u, do, lse.reshape(SWA_HQ, 1, n), delta.reshape(SWA_HQ, 1, n), nlat)
    kc = {k: v for k, v in cfg.items() if k != "hq"}
    kc["hkv"] = SWA_HKV
    kc["tq"] = min(1024, nlat)
    dk, dv = flash_dkv(tag + "_dkv_ctx", qs, ks, u, do, lse, delta, ctx_k=True, nlat=nlat, prev=(dk, dv), **kc)
    return dq, dk, dv, ds1 + ds2


def adamw(name, w, g, m, v):
    r, c = w.shape
    tr = _pick(r, (256, 128, 64, 32, 16, 8))
    bc1 = 1.0 - ADAM_B1 ** ADAM_STEP
    bc2 = 1.0 - ADAM_B2 ** ADAM_STEP

    def body(w_ref, g_ref, m_ref, v_ref, d_ref, nm_ref, nv_ref):
        gv = g_ref[...]
        nm = ADAM_B1 * m_ref[...] + (1.0 - ADAM_B1) * gv
        nv = ADAM_B2 * v_ref[...] + (1.0 - ADAM_B2) * (gv * gv)
        d_ref[...] = -ADAM_LR * ((nm / bc1) / (jnp.sqrt(nv / bc2) + ADAM_EPS) + ADAM_WD * w_ref[...])
        nm_ref[...] = nm
        nv_ref[...] = nv

    spec = pl.BlockSpec((tr, c), lambda i: (i, 0))
    return pl.pallas_call(body, out_shape=[_sd((r, c), F32)] * 3, grid=(r // tr,), in_specs=[spec] * 4, out_specs=[spec] * 3,
                          name=name, compiler_params=_cp(("parallel",)))(w, g, m, v)


def _coords():
    return lax.axis_index("x"), lax.axis_index("y"), lax.axis_index("c")


_ANY = pl.BlockSpec(memory_space=pl.ANY)


def _chip():
    return 2 * lax.axis_index("x") + lax.axis_index("y")


def _per_core(fn):
    c = lax.axis_index("c")
    for cs in (0, 1):
        pl.when(c == cs)(functools.partial(fn, cs))


def gather_chips(name, a):
    r = a.shape[0]
    half = r // 2

    def body(a_ref, o_ref, ici_send, ici_recv, d2d_send, d2d_recv):
        _per_core(functools.partial(run, a_ref, o_ref, ici_send, ici_recv, d2d_send, d2d_recv))

    def run(a_ref, o_ref, ici_send, ici_recv, d2d_send, d2d_recv, c):
        x, y, _ = _coords()
        me = 2 * x + y
        peers = [(1 - x, y), (x, 1 - y), (1 - x, 1 - y)]
        my_rows = pl.ds(c * half, half)
        sib_rows = pl.ds((1 - c) * half, half)
        sends = [pltpu.make_async_remote_copy(a_ref.at[my_rows], o_ref.at[me, my_rows], ici_send.at[k], ici_recv.at[k],
                                              device_id=(px, py, c), device_id_type=MESH)
                 for k, (px, py) in enumerate(peers)]
        for cp in sends:
            cp.start()
        passed = []
        for k, (px, py) in enumerate(peers):
            s = 2 * px + py
            pltpu.make_async_remote_copy(a_ref.at[my_rows], o_ref.at[s, my_rows], ici_send.at[k], ici_recv.at[k],
                                         device_id=(px, py, c), device_id_type=MESH).wait_recv()
            fw = pltpu.make_async_remote_copy(o_ref.at[s, my_rows], o_ref.at[s, my_rows], d2d_send.at[k], d2d_recv.at[k],
                                              device_id=(x, y, 1 - c), device_id_type=MESH)
            fw.start()
            passed.append(fw)
        for k, (px, py) in enumerate(peers):
            s = 2 * px + py
            pltpu.make_async_remote_copy(o_ref.at[s, sib_rows], o_ref.at[s, sib_rows], d2d_send.at[k], d2d_recv.at[k],
                                         device_id=(x, y, 1 - c), device_id_type=MESH).wait_recv()
        for cp in sends + passed:
            cp.wait_send()

    out = pl.pallas_call(
        body, out_shape=_sd((4,) + a.shape, a.dtype), in_specs=[_ANY], out_specs=_ANY,
        scratch_shapes=[pltpu.SemaphoreType.DMA((3,)), pltpu.SemaphoreType.DMA((3,)), pltpu.SemaphoreType.DMA((3,)),
                        pltpu.SemaphoreType.DMA((3,))],
        name=name, compiler_params=pltpu.CompilerParams(has_side_effects=True))(a)
    return lax.dynamic_update_index_in_dim(out, a, _chip(), 0)


def pair_split(name, a):
    k4, r, cdim = a.shape
    half = r // 2

    def body(a_ref, got_ref, send_sem, recv_sem):
        _per_core(functools.partial(run, a_ref, got_ref, send_sem, recv_sem))

    def run(a_ref, got_ref, send_sem, recv_sem, c):
        x, y, _ = _coords()
        sib_rows = pl.ds((1 - c) * half, half)
        cp = pltpu.make_async_remote_copy(a_ref.at[:, sib_rows], got_ref, send_sem, recv_sem,
                                          device_id=(x, y, 1 - c), device_id_type=MESH)
        cp.start()
        cp.wait()

    got = pl.pallas_call(
        body, out_shape=_sd((k4, half, cdim), a.dtype), in_specs=[_ANY], out_specs=_ANY,
        scratch_shapes=[pltpu.SemaphoreType.DMA, pltpu.SemaphoreType.DMA],
        name=name, compiler_params=pltpu.CompilerParams(has_side_effects=True))(a)
    return lax.dynamic_slice_in_dim(a, lax.axis_index("c") * half, half, axis=1), got


def scatter_chips(name, a):
    def body(a_ref, o_ref, send_sems, recv_sems):
        x, y, c = _coords()
        me = 2 * x + y
        peers = [(1 - x, y), (x, 1 - y), (1 - x, 1 - y)]
        sends = [pltpu.make_async_remote_copy(a_ref.at[2 * px + py], o_ref.at[me], send_sems.at[k], recv_sems.at[k],
                                              device_id=(px, py, c), device_id_type=MESH)
                 for k, (px, py) in enumerate(peers)]
        for cp in sends:
            cp.start()
        for k, (px, py) in enumerate(peers):
            pltpu.make_async_remote_copy(a_ref.at[me], o_ref.at[2 * px + py], send_sems.at[k], recv_sems.at[k],
                                         device_id=(px, py, c), device_id_type=MESH).wait_recv()
        for cp in sends:
            cp.wait_send()

    out = pl.pallas_call(
        body, out_shape=_sd(a.shape, a.dtype), in_specs=[_ANY], out_specs=_ANY,
        scratch_shapes=[pltpu.SemaphoreType.DMA((3,)), pltpu.SemaphoreType.DMA((3,))],
        name=name, compiler_params=pltpu.CompilerParams(has_side_effects=True))(a)
    return lax.dynamic_update_index_in_dim(out, lax.dynamic_index_in_dim(a, _chip(), 0, keepdims=False), _chip(), 0)


def pair_join(name, a):
    half, cdim = a.shape

    def body(a_ref, o_ref, send_sem, recv_sem):
        _per_core(functools.partial(run, a_ref, o_ref, send_sem, recv_sem))

    def run(a_ref, o_ref, send_sem, recv_sem, c):
        x, y, _ = _coords()
        my_rows = pl.ds(c * half, half)
        sib_rows = pl.ds((1 - c) * half, half)
        cp = pltpu.make_async_remote_copy(a_ref, o_ref.at[my_rows], send_sem, recv_sem, device_id=(x, y, 1 - c),
                                          device_id_type=MESH)
        cp.start()
        cp.wait_send()
        pltpu.make_async_remote_copy(a_ref, o_ref.at[sib_rows], send_sem, recv_sem, device_id=(x, y, 1 - c),
                                     device_id_type=MESH).wait_recv()

    out = pl.pallas_call(
        body, out_shape=_sd((2 * half, cdim), a.dtype), in_specs=[_ANY], out_specs=_ANY,
        scratch_shapes=[pltpu.SemaphoreType.DMA, pltpu.SemaphoreType.DMA],
        name=name, compiler_params=pltpu.CompilerParams(has_side_effects=True))(a)
    return lax.dynamic_update_slice_in_dim(out, a, lax.axis_index("c") * half, axis=0)


def add_cast(name, a, b, dtype):
    k, r, c = a.shape
    tr = _pick(r, (1024, 512, 256, 128, 64, 32, 16, 8))

    def body(a_ref, b_ref, o_ref):
        o_ref[...] = (a_ref[...].astype(F32) + b_ref[...].astype(F32)).astype(dtype)

    spec = pl.BlockSpec((1, tr, c), lambda s, i: (s, i, 0))
    return pl.pallas_call(body, out_shape=_sd((k, r, c), dtype), grid=(k, r // tr), in_specs=[spec, spec], out_specs=spec,
                          name=name, compiler_params=_cp(("parallel", "parallel")))(a, b)


def gather_all(name, a):
    def body(a_ref, o_ref, send_sems, recv_sems, loc_sem):
        x, y, c = _coords()
        me = 4 * x + 2 * y + c
        flips = [(fx, fy, fc) for fx in (0, 1) for fy in (0, 1) for fc in (0, 1) if fx + fy + fc > 0]
        peers = [(x ^ fx, y ^ fy, c ^ fc) for fx, fy, fc in flips]
        mine = pltpu.make_async_copy(a_ref, o_ref.at[me], loc_sem)
        mine.start()
        sends = [pltpu.make_async_remote_copy(a_ref, o_ref.at[me], send_sems.at[k], recv_sems.at[k],
                                              device_id=p, device_id_type=MESH) for k, p in enumerate(peers)]
        for cp in sends:
            cp.start()
        for k, (px, py, pc) in enumerate(peers):
            pltpu.make_async_remote_copy(a_ref, o_ref.at[4 * px + 2 * py + pc], send_sems.at[k], recv_sems.at[k],
                                         device_id=(px, py, pc), device_id_type=MESH).wait_recv()
        for cp in sends:
            cp.wait_send()
        mine.wait()

    return pl.pallas_call(
        body, out_shape=_sd((8,) + a.shape, a.dtype), in_specs=[_ANY], out_specs=_ANY,
        scratch_shapes=[pltpu.SemaphoreType.DMA((7,)), pltpu.SemaphoreType.DMA((7,)), pltpu.SemaphoreType.DMA],
        name=name, compiler_params=pltpu.CompilerParams(has_side_effects=True))(a)


def sum_blocks(name, a):
    k, r, c = a.shape
    tr = _pick(r, (1024, 256, 128, 64, 32, 16, 8))

    def body(a_ref, o_ref):
        acc = a_ref[0].astype(F32)
        for s in range(1, k):
            acc = acc + a_ref[s].astype(F32)
        o_ref[...] = acc

    return pl.pallas_call(body, out_shape=_sd((r, c), F32), grid=(r // tr,),
                          in_specs=[pl.BlockSpec((k, tr, c), lambda i: (0, i, 0))], out_specs=pl.BlockSpec((tr, c), lambda i: (i, 0)),
                          name=name, compiler_params=_cp(("parallel",)))(a)


BIG = ("w_mod", "w_in", "w_mla_uq", "w_mla_ukv", "w_p_ssm", "w_p_swa", "w_p_mla", "w_out", "w_ffn_in", "w_ffn_out")
COL_SHARDED = ("w_mod", "w_in", "w_mla_uq", "w_mla_ukv", "w_ffn_in")
SMALL = ("c_ctx", "b_mod", "norm1_g", "norm2_g", "ssm_conv_w", "ssm_conv_b", "ssm_dt_bias", "ssm_a_log", "ssm_d",
         "ssm_norm_g", "swa_q_norm_g", "swa_k_norm_g", "swa_sink", "mla_q_lat_g", "mla_kv_lat_g", "mla_q_norm_g",
         "mla_k_norm_g")
WEIGHTS = ("c_ctx", "w_mod", "b_mod", "norm1_g", "norm2_g", "w_in", "ssm_conv_w", "ssm_conv_b", "ssm_dt_bias", "ssm_a_log",
           "ssm_d", "ssm_norm_g", "swa_q_norm_g", "swa_k_norm_g", "swa_sink", "mla_q_lat_g", "mla_kv_lat_g", "w_mla_uq",
           "w_mla_ukv", "mla_q_norm_g", "mla_k_norm_g", "w_p_ssm", "w_p_swa", "w_p_mla", "w_out", "w_ffn_in", "w_ffn_out")


def pack_w_in(w):
    z = lambda k: jnp.zeros((w.shape[0], k), w.dtype)
    return jnp.concatenate([w[:, 4832:7904], w[:, 2400:3424], w[:, 3424:4448], w[:, 0:1536], w[:, 1568:1824], w[:, 1824:2080],
                            w[:, 2080:2336], w[:, 2336:2400], w[:, 1536:1568], z(32), z(128), w[:, 4448:4832]], axis=1)


def unpack_w_in(g):
    return jnp.concatenate([g[:, 5120:6656], g[:, 7488:7520], g[:, 6656:6912], g[:, 6912:7168], g[:, 7168:7424], g[:, 7424:7488],
                            g[:, 3072:4096], g[:, 4096:5120], g[:, 7680:8064], g[:, 0:3072]], axis=1)


def pack_ukv(w):
    return w.reshape(MLA_KVRANK, MLA_H, 2, 128).transpose(0, 2, 1, 3).reshape(MLA_KVRANK, 2048)


def unpack_ukv(g):
    return g.reshape(MLA_KVRANK, 2, MLA_H, 128).transpose(0, 2, 1, 3).reshape(MLA_KVRANK, 2048)


def pack_uq(w):
    return jnp.pad(w.reshape(MLA_QRANK, MLA_H, 192), ((0, 0), (0, 0), (0, 64))).reshape(MLA_QRANK, 2048)


def unpack_uq(g):
    return g.reshape(MLA_QRANK, MLA_H, 256)[:, :, :192].reshape(MLA_QRANK, 1536)


def rope_tables(nlat):
    t = jnp.arange(nlat, dtype=jnp.int32)
    r = (t // GRID_W).astype(F32)[:, None]
    col = (t % GRID_W).astype(F32)[:, None]

    def tab(nf, pad):
        inv = jnp.power(ROPE_BASE, -jnp.arange(nf, dtype=F32) / nf)
        ar, ac = r * inv, col * inv
        cos = jnp.concatenate([jnp.cos(ar), jnp.cos(ar), jnp.cos(ac), jnp.cos(ac), jnp.ones((nlat, pad), F32)], axis=1)
        sin = jnp.concatenate([-jnp.sin(ar), jnp.sin(ar), -jnp.sin(ac), jnp.sin(ac), jnp.zeros((nlat, pad), F32)], axis=1)
        cos = jnp.concatenate([cos, jnp.ones((NCTX, 128), F32)], axis=0)
        sin = jnp.concatenate([sin, jnp.zeros((NCTX, 128), F32)], axis=0)
        return cos, sin

    return tab(32, 0), tab(16, 64)


def _lanes(v, start, width=128):
    return jnp.zeros((1, width), F32).at[0, start:start + v.shape[0]].set(v)


def layer_fwd(i, xin, h, mod, p, tabs, nlat):
    t = "l%d_" % i
    n = xin.shape[0]
    (cos_s, sin_s), (cos_m, sin_m) = tabs
    u = mm(h, p["w_in"], F32, t + "in_proj")
    xbc = conv_fwd(t + "conv", u, p["conv_w"], p["conv_b"], nlat)
    dtrow = jnp.transpose(u[:, C_MISC + DT_LANE:C_MISC + DT_LANE + 32])
    nlc = nlat // Q
    yf, hs_f = ssd_fwd(t + "ssd_f", xbc, u, dtrow, p["bias_c"], p["alog_c"], p["bias_r"], p["alog_r"], nlc, False, 0)
    yb, hs_b = ssd_fwd(t + "ssd_b", xbc, u, dtrow, p["bias_c"], p["alog_c"], p["bias_r"], p["alog_r"], nlc, True, 1)
    ys = ssd_out_fwd(t + "ssd_out", yf, yb, xbc, u, p["ssm_norm_g"], p["d_exp"])
    qs, ks = swa_prep_fwd(t + "swa_prep", u, p["swa_q_g"], p["swa_k_g"], cos_s, sin_s)
    o_swa, lse_swa = swa_attention_fwd(t + "swa", qs, ks, u, p["sink"], p["swa_cfg"], nlat)
    ckv_n, cq_n = lat_norm_fwd(t + "lat_norm", u, p["kv_lat_g"], p["q_lat_g"])
    kv = mm(ckv_n, p["w_ukv"], F32, t + "ukv")
    qp = mm(cq_n, p["w_uq"], F32, t + "uq")
    km, qm, vm = mla_prep_fwd(t + "mla_prep", kv, qp, u, p["mla_q_g"], p["mla_k_g"], cos_m, sin_m)
    o_mla, lse_mla = mla_fwd(t + "mla_fwd", qm, km, vm, nlat)
    p1 = mm(ys, p["w_p_ssm"], F32, t + "p_ssm")
    p2 = mm(o_swa, p["w_p_swa"], F32, t + "p_swa")
    p3 = mm(o_mla, p["w_p_mla"], F32, t + "p_mla")
    merged = merge_fwd(t + "merge", u, p1, p2, p3)
    o = mm(merged, p["w_out"], F32, t + "out_proj")
    x1, h2 = resid_mod_fwd(t + "res1", xin, o, mod, 2, mod, 3, 4, p["norm2_g"], nlat // RT)
    gu = mm(h2, p["w_ffn_in"], F32, t + "ffn_in")
    a = swiglu_fwd(t + "swiglu", gu)
    f = mm(a, p["w_ffn_out"], F32, t + "ffn_out")
    saved = dict(xin=xin, h=h, u=u, xbc=xbc, dtrow=dtrow, yf=yf, yb=yb, hs_f=hs_f, hs_b=hs_b, ys=ys, qs=qs, ks=ks,
                 o_swa=o_swa, lse_swa=lse_swa, ckv_n=ckv_n, cq_n=cq_n, kv=kv, qp=qp, km=km, qm=qm, vm=vm, o_mla=o_mla,
                 lse_mla=lse_mla, p1=p1, p2=p2, p3=p3, merged=merged, o=o, x1=x1, h2=h2, gu=gu, a=a, f=f)
    del n
    return x1, f, saved


def layer_bwd(i, dx2, df, dgt2, sv, mod, p, tabs, nlat):
    t = "l%db_" % i
    (cos_s, sin_s), (cos_m, sin_m) = tabs
    g = {}
    nt = nlat // RT
    nlc = nlat // Q
    g["w_ffn_out"] = mm_tn(sv["a"], df, t + "wg_ffn_out")
    da = mm(df, p["w_ffn_out"], F32, t + "dg_ffn_out", trans_b=True)
    dgu = swiglu_bwd(t + "swiglu", sv["gu"], da)
    g["w_ffn_in"] = mm_tn(sv["h2"], dgu, t + "wg_ffn_in")
    dh2 = mm(dgu, p["w_ffn_in"], F32, t + "dg_ffn_in", trans_b=True)
    dx1, do, dgt1, dsh2, dsc2, g["norm2_g"] = resid_mod_bwd(t + "res1", sv["x1"], dx2, dh2, sv["o"], mod, 2, mod, 3, 4,
                                                              p["norm2_g"], nt)
    g["w_out"] = mm_tn(sv["merged"], do, t + "wg_out")
    dmerged = mm(do, p["w_out"], F32, t + "dg_out", trans_b=True)
    dp1, dp2, dp3, dgates = merge_bwd(t + "merge", sv["u"], sv["p1"], sv["p2"], sv["p3"], dmerged)
    g["w_p_ssm"] = mm_tn(sv["ys"], dp1, t + "wg_p_ssm")
    g["w_p_swa"] = mm_tn(sv["o_swa"], dp2, t + "wg_p_swa")
    g["w_p_mla"] = mm_tn(sv["o_mla"], dp3, t + "wg_p_mla")
    dys = mm(dp1, p["w_p_ssm"], F32, t + "dg_p_ssm", trans_b=True)
    do_swa = mm(dp2, p["w_p_swa"], BF16, t + "dg_p_swa", trans_b=True)
    do_mla = mm(dp3, p["w_p_mla"], BF16, t + "dg_p_mla", trans_b=True)
    dqm, dkm, dv_mla = mla_attention_bwd(t + "mla", sv["qm"], sv["km"], sv["vm"], sv["o_mla"], do_mla, sv["lse_mla"], nlat)
    dkv, dqp, dkr, g["mla_q_g"], g["mla_k_g"] = mla_prep_bwd(t + "mla_prep", sv["kv"], sv["qp"], sv["u"], p["mla_q_g"],
                                                             p["mla_k_g"], cos_m, sin_m, dkm, dqm, dv_mla)
    g["w_ukv"] = mm_tn(sv["ckv_n"], dkv, t + "wg_ukv")
    g["w_uq"] = mm_tn(sv["cq_n"], dqp, t + "wg_uq")
    dckv_n = mm(dkv, p["w_ukv"], F32, t + "dg_ukv", trans_b=True)
    dcq_n = mm(dqp, p["w_uq"], F32, t + "dg_uq", trans_b=True)
    dckv, dcq, g["kv_lat_g"], g["q_lat_g"] = lat_norm_bwd(t + "lat_norm", sv["u"], p["kv_lat_g"], p["q_lat_g"], dckv_n, dcq_n)
    dqs, dks, dv_swa, g["sink"] = swa_attention_bwd(t + "swa", sv["qs"], sv["ks"], sv["u"], sv["o_swa"], do_swa, sv["lse_swa"],
                                                p["sink"], p["swa_cfg"], nlat)
    dq, dk, dv, g["swa_q_g"], g["swa_k_g"] = swa_prep_bwd(t + "swa_prep", sv["u"], p["swa_q_g"], p["swa_k_g"], cos_s, sin_s,
                                                          dqs, dks, dv_swa)
    dy, dxs_skip, dz, g["ssm_norm_g"], g["d_exp"] = ssd_out_bwd(t + "ssd_out", sv["yf"], sv["yb"], sv["xbc"], sv["u"],
                                                                 p["ssm_norm_g"], p["d_exp"], dys)
    n = dy.shape[0]
    zbc = jnp.zeros((n, 256), F32)
    r_f = ssd_bwd(t + "ssd_f", sv["xbc"], sv["u"], sv["dtrow"], p["bias_c"], p["alog_c"], p["bias_r"], p["alog_r"],
                  sv["hs_f"], dy, (dxs_skip, zbc, zbc), nlc, False, 0)
    r_b = ssd_bwd(t + "ssd_b", sv["xbc"], sv["u"], sv["dtrow"], p["bias_c"], p["alog_c"], p["bias_r"], p["alog_r"],
                  sv["hs_b"], dy, (r_f[0], r_f[1], r_f[2]), nlc, True, 1)
    dact = jnp.concatenate([r_b[0], r_b[1], r_b[2]], axis=1)
    dxbc, g["conv_w"], g["conv_b"] = conv_bwd(t + "conv", sv["u"], dact, p["conv_w"], p["conv_b"], nlat)
    drow = jnp.concatenate([r_f[4][0] + r_f[4][1], r_b[4][0] + r_b[4][1]], axis=0)
    drow_t = jnp.pad(jnp.transpose(drow), ((0, 0), (DT_LANE, 128 - DT_LANE - 32)))
    dmisc = misc_combine(t + "misc", dkr, r_f[3], r_b[3], drow_t)
    g["bias_c"] = r_f[5] + r_b[5]
    g["alog_c"] = r_f[6] + r_b[6]
    g["bias_r"] = jnp.concatenate([r_f[7], r_b[7]], axis=0)
    g["alog_r"] = jnp.concatenate([r_f[8], r_b[8]], axis=0)
    du = jnp.concatenate([dgates, dz, dq, dxbc, dk, dv, dckv, dmisc, jnp.zeros((n, 128), BF16), dcq], axis=1)
    g["w_in"] = mm_tn(sv["h"], du, t + "wg_in")
    dh = mm(du, p["w_in"], F32, t + "dg_in", trans_b=True)
    g["mod"] = (dgt1, dsh2, dsc2, dgt2)
    return dx1, dh, g


def local_step(x, c, ctx, target, c_ctx, W, nlat):
    xin = jnp.concatenate([x, ctx], axis=0)
    n = xin.shape[0]
    nt = nlat // RT
    tabs = rope_tables(nlat)
    c8 = jnp.zeros((8, D), F32).at[0].set(c[0]).at[1].set(c_ctx)
    mods, silus = [], []
    for i in range(DEPTH):
        m8, s8 = mod_fwd("l%d_mod" % i, c8, W[i]["w_mod"], W[i]["b_mod"])
        mods.append(m8[0:2].reshape(2, 1, 6 * D))
        silus.append(s8)
    saved = []
    _, h = resid_mod_fwd("l0_norm1", xin, None, None, 0, mods[0], 0, 1, W[0]["norm1_g"], nt)
    xcur = xin
    for i in range(DEPTH):
        x1, f, sv = layer_fwd(i, xcur, h, mods[i], W[i], tabs, nlat)
        saved.append(sv)
        if i + 1 < DEPTH:
            xcur, h = resid_mod_fwd("l%d_res2" % i, x1, f, mods[i], 5, mods[i + 1], 0, 1, W[i + 1]["norm1_g"], nt)
    loss_v, dx2, df, dgt2 = resid_loss("loss", x1, f, mods[DEPTH - 1], 5, target, nt)
    grads = [None] * DEPTH
    for i in reversed(range(DEPTH)):
        dx1, dh, g = layer_bwd(i, dx2, df, dgt2, saved[i], mods[i], W[i], tabs, nlat)
        if i > 0:
            sv = saved[i]
            dx2, df, dgt2, dsh1, dsc1, g["norm1_g"] = resid_mod_bwd(
                "l%db_res2" % (i - 1), sv["xin"], dx1, dh, saved[i - 1]["f"], mods[i - 1], 5, mods[i], 0, 1,
                W[i]["norm1_g"], nt)
        else:
            dxin, _, _, dsh1, dsc1, g["norm1_g"] = resid_mod_bwd("l0b_norm1", saved[0]["xin"], dx1, dh, None, None, 0,
                                                                  mods[0], 0, 1, W[0]["norm1_g"], nt)
        dgt1, dsh2, dsc2, dgt2_i = g.pop("mod")
        dmod = jnp.concatenate([dsh1, dsc1, dgt1, dsh2, dsc2, dgt2_i], axis=2).reshape(2, 6 * D)
        dmod8 = jnp.zeros((8, 6 * D), F32).at[0:2].set(dmod)
        g["w_mod"] = mm_tn(silus[i], dmod8, "l%db_wg_mod" % i)
        dsilu = mm(dmod8, W[i]["w_mod"], F32, "l%db_dg_mod" % i, trans_b=True)
        dc8, g["b_mod"] = mod_small_bwd("l%db_mod_small" % i, c8, dsilu, dmod8)
        g["c8"] = dc8
        grads[i] = g
    del n
    return loss_v[0, 0], dxin, grads


def _big_shapes():
    return dict(w_mod=(2, 1024, 1536), w_in=(2, 1024, 1976), w_mla_uq=(2, 384, 384), w_mla_ukv=(2, 256, 512),
                w_p_ssm=(2, 256, 1024), w_p_swa=(2, 256, 1024), w_p_mla=(2, 256, 1024), w_out=(2, 256, 1024),
                w_ffn_in=(2, 1024, 1408), w_ffn_out=(2, 704, 1024))


PACK_ROWS = 14336


def _pack_big(d, dtype):
    parts = [d[k].astype(dtype).reshape(-1, 1024) for k in BIG]
    used = sum(p.shape[0] for p in parts)
    return jnp.concatenate(parts + [jnp.zeros((PACK_ROWS - used, 1024), dtype)], axis=0)


def _unpack_big(buf, lead):
    out = {}
    r0 = 0
    for k in BIG:
        sh = _big_shapes()[k]
        rows = sh[0] * sh[1] * sh[2] // 1024
        out[k] = buf[..., r0:r0 + rows, :].reshape(lead + sh)
        r0 += rows
    return out


def _full_from_chips(k, a):
    if k in COL_SHARDED:
        return a.transpose(1, 2, 0, 3).reshape(2, a.shape[2], 4 * a.shape[3])
    return a.transpose(1, 0, 2, 3).reshape(2, 4 * a.shape[2], a.shape[3])


def _chips_from_full(k, a):
    if k in COL_SHARDED:
        return a.reshape(a.shape[0], 4, a.shape[1] // 4).transpose(1, 0, 2)
    return a.reshape(4, a.shape[0] // 4, a.shape[1])


def _small_sizes():
    return dict(c_ctx=1024, b_mod=2 * 6144, norm1_g=2048, norm2_g=2048, ssm_conv_w=2 * 5 * 1536, ssm_conv_b=2 * 1536,
                ssm_dt_bias=64, ssm_a_log=64, ssm_d=32, ssm_norm_g=2048, swa_q_norm_g=256, swa_k_norm_g=256, swa_sink=16,
                mla_q_lat_g=768, mla_kv_lat_g=512, mla_q_norm_g=384, mla_k_norm_g=384)


def _pack_small(d):
    parts = []
    for k in SMALL:
        v = d[k].astype(F32).reshape(-1)
        parts.append(jnp.pad(v, (0, (-v.shape[0]) % 1024)))
    return jnp.concatenate(parts).reshape(-1, 128)


def _unpack_small(buf, shapes):
    flat = buf.reshape(-1)
    out = {}
    o = 0
    for k in SMALL:
        sz = _small_sizes()[k]
        out[k] = flat[o:o + sz].reshape(shapes[k])
        o += sz + (-sz) % 1024
    return out


def big_grads(grads):
    gfull = {k: [] for k in BIG}
    for i in range(DEPTH):
        g = grads[i]
        gfull["w_mod"].append(g["w_mod"])
        gfull["w_in"].append(unpack_w_in(g["w_in"]))
        gfull["w_mla_uq"].append(unpack_uq(g["w_uq"]))
        gfull["w_mla_ukv"].append(unpack_ukv(g["w_ukv"]))
        for k in ("w_p_ssm", "w_p_swa", "w_p_mla", "w_out", "w_ffn_in", "w_ffn_out"):
            gfull[k].append(g[k])
    return gfull


def small_grads(grads):
    gs = {}
    gs["c_ctx"] = sum(grads[i]["c8"][1] for i in range(DEPTH))
    st = lambda f: jnp.stack([f(grads[i]) for i in range(DEPTH)])
    gs["b_mod"] = st(lambda g: g["b_mod"][0])
    gs["norm1_g"] = st(lambda g: g["norm1_g"][0])
    gs["norm2_g"] = st(lambda g: g["norm2_g"][0])
    gs["ssm_conv_w"] = st(lambda g: g["conv_w"])
    gs["ssm_conv_b"] = st(lambda g: g["conv_b"][0])
    gs["ssm_dt_bias"] = st(lambda g: (g["bias_c"][0, DT_LANE:DT_LANE + 32] + g["bias_r"][:, 0]).reshape(2, 16))
    gs["ssm_a_log"] = st(lambda g: (g["alog_c"][0, DT_LANE:DT_LANE + 32] + g["alog_r"][:, 0]).reshape(2, 16))
    gs["ssm_d"] = st(lambda g: g["d_exp"].reshape(16, 64).sum(axis=1))
    gs["ssm_norm_g"] = st(lambda g: g["ssm_norm_g"][0])
    gs["swa_q_norm_g"] = st(lambda g: g["swa_q_g"][0])
    gs["swa_k_norm_g"] = st(lambda g: g["swa_k_g"][0])
    gs["swa_sink"] = st(lambda g: g["sink"][:, 0, 0])
    gs["mla_q_lat_g"] = st(lambda g: g["q_lat_g"][0])
    gs["mla_kv_lat_g"] = st(lambda g: g["kv_lat_g"][0])
    gs["mla_q_norm_g"] = st(lambda g: g["mla_q_g"][0, :192])
    gs["mla_k_norm_g"] = st(lambda g: g["mla_k_g"][0, :192])
    return gs


def layer_params(i, full, conv_full, sm, nlat):
    p = {}
    p["w_mod"] = full["w_mod"][i]
    p["w_in"] = pack_w_in(full["w_in"][i])
    p["w_uq"] = pack_uq(full["w_mla_uq"][i])
    p["w_ukv"] = pack_ukv(full["w_mla_ukv"][i])
    for k in ("w_p_ssm", "w_p_swa", "w_p_mla", "w_out", "w_ffn_in", "w_ffn_out"):
        p[k] = full[k][i]
    p["b_mod"] = sm["b_mod"][i][None]
    p["norm1_g"] = sm["norm1_g"][i][None]
    p["norm2_g"] = sm["norm2_g"][i][None]
    p["conv_w"] = conv_full[i]
    p["conv_b"] = sm["ssm_conv_b"][i][None]
    bias = sm["ssm_dt_bias"][i].reshape(32)
    alog = sm["ssm_a_log"][i].reshape(32)
    p["bias_c"] = _lanes(bias, DT_LANE)
    p["alog_c"] = _lanes(alog, DT_LANE)
    p["bias_r"] = bias[:, None]
    p["alog_r"] = alog[:, None]
    p["d_exp"] = jnp.repeat(sm["ssm_d"][i], 64)[None]
    p["ssm_norm_g"] = sm["ssm_norm_g"][i][None]
    p["swa_q_g"] = sm["swa_q_norm_g"][i][None]
    p["swa_k_g"] = sm["swa_k_norm_g"][i][None]
    p["sink"] = jnp.broadcast_to(sm["swa_sink"][i][:, None, None], (SWA_HQ, 1, 128))
    p["q_lat_g"] = sm["mla_q_lat_g"][i][None]
    p["kv_lat_g"] = sm["mla_kv_lat_g"][i][None]
    p["mla_q_g"] = _lanes(sm["mla_q_norm_g"][i], 0, 256)
    p["mla_k_g"] = _lanes(sm["mla_k_norm_g"][i], 0, 256)
    p["swa_cfg"] = dict(w=128, vw=128, hq=SWA_HQ, grp=SWA_HQ // SWA_HKV, vcol0=C_V // 128, scale=SWA_DH ** -0.5,
                        tq=256, tk=256, band=True)
    return p


def kernel(x, c, ctx, c_ctx, w_mod, b_mod, norm1_g, norm2_g, w_in, ssm_conv_w, ssm_conv_b, ssm_dt_bias, ssm_a_log, ssm_d, ssm_norm_g, swa_q_norm_g, swa_k_norm_g, swa_sink, mla_q_lat_g, mla_kv_lat_g, w_mla_uq, w_mla_ukv, mla_q_norm_g, mla_k_norm_g, w_p_ssm, w_p_swa, w_p_mla, w_out, w_ffn_in, w_ffn_out, loss_target, m_c_ctx, m_w_mod, m_b_mod, m_norm1_g, m_norm2_g, m_w_in, m_ssm_conv_w, m_ssm_conv_b, m_ssm_dt_bias, m_ssm_a_log, m_ssm_d, m_ssm_norm_g, m_swa_q_norm_g, m_swa_k_norm_g, m_swa_sink, m_mla_q_lat_g, m_mla_kv_lat_g, m_w_mla_uq, m_w_mla_ukv, m_mla_q_norm_g, m_mla_k_norm_g, m_w_p_ssm, m_w_p_swa, m_w_p_mla, m_w_out, m_w_ffn_in, m_w_ffn_out, v_c_ctx, v_w_mod, v_b_mod, v_norm1_g, v_norm2_g, v_w_in, v_ssm_conv_w, v_ssm_conv_b, v_ssm_dt_bias, v_ssm_a_log, v_ssm_d, v_ssm_norm_g, v_swa_q_norm_g, v_swa_k_norm_g, v_swa_sink, v_mla_q_lat_g, v_mla_kv_lat_g, v_w_mla_uq, v_w_mla_ukv, v_mla_q_norm_g, v_mla_k_norm_g, v_w_p_ssm, v_w_p_swa, v_w_p_mla, v_w_out, v_w_ffn_in, v_w_ffn_out):
    loc = dict(locals())
    w = {k: loc[k] for k in WEIGHTS}
    m = {k: loc["m_" + k] for k in WEIGHTS}
    v = {k: loc["v_" + k] for k in WEIGHTS}
    nlat = x.shape[1]

    gathered = _unpack_big(gather_chips("gather_weights", _pack_big(w, BF16)), (4,))
    full = {k: _full_from_chips(k, gathered[k]) for k in BIG}
    conv_sh = jnp.pad(ssm_conv_w.reshape(10, 384), ((0, 6), (0, 0)))
    conv_full = gather_chips("gather_conv", conv_sh)[:, :10].reshape(4, 2, 5, 384).transpose(1, 2, 0, 3).reshape(2, 5, 1536)

    W = [layer_params(i, full, conv_full, w, nlat) for i in range(DEPTH)]

    loss_loc, dx, grads = local_step(x[0], c, ctx[0], loss_target[0], c_ctx, W, nlat)

    gfull = big_grads(grads)
    by_chip = {k: jnp.stack([_chips_from_full(k, a) for a in gfull[k]], axis=1) for k in BIG}
    parts = [by_chip[k].astype(BF16).reshape(4, -1, 1024) for k in BIG]
    used = sum(p.shape[1] for p in parts)
    send = jnp.concatenate(parts + [jnp.zeros((4, PACK_ROWS - used, 1024), BF16)], axis=1)
    own, got = pair_split("pair_split", send)
    pair = add_cast("pair_sum", own, got, BF16)
    recv = scatter_chips("scatter_grads", pair)
    mine = sum_blocks("sum_chips", recv)
    gbig = _unpack_big(pair_join("join_cores", mine), ())

    gs = small_grads(grads)
    small_all = gather_all("gather_small", _pack_small(gs))
    small_sum = sum_blocks("sum_small", small_all)
    full_shapes = {k: (w[k].shape if k != "ssm_conv_w" else (2, 5, 1536)) for k in SMALL}
    gsmall = _unpack_small(small_sum, full_shapes)
    chip = 2 * lax.axis_index("x") + lax.axis_index("y")
    gsmall["ssm_conv_w"] = lax.dynamic_slice_in_dim(gsmall["ssm_conv_w"], chip * 384, 384, axis=2)

    grad = {**gbig, **gsmall}
    delta, new_m, new_v = {}, {}, {}
    sm = {k: _pack_small_local(d) for k, d in (("w", w), ("g", grad), ("m", m), ("v", v))}
    r = adamw("adamw_small", sm["w"], sm["g"], sm["m"], sm["v"])
    shapes = {k: w[k].shape for k in SMALL}
    for dst, buf in zip((delta, new_m, new_v), r):
        dst.update(_unpack_small_local(buf, shapes))
    for k in BIG:
        sh = w[k].shape
        r = adamw("adamw_" + k, *[a[k].reshape(sh[0] * sh[1], sh[2]) for a in (w, grad, m, v)])
        for dst, buf in zip((delta, new_m, new_v), r):
            dst[k] = buf.reshape(sh)

    loss = lax.psum(loss_loc, ("x", "y", "c"))
    return (loss, dx[None, :nlat], *[grad[k] for k in WEIGHTS], *[delta[k] for k in WEIGHTS],
            *[new_m[k] for k in WEIGHTS], *[new_v[k] for k in WEIGHTS])


def _pack_small_local(d):
    parts = []
    for k in SMALL:
        a = d[k].astype(F32).reshape(-1)
        parts.append(jnp.pad(a, (0, (-a.shape[0]) % 1024)))
    return jnp.concatenate(parts).reshape(-1, 128)


def _unpack_small_local(buf, shapes):
    flat = buf.reshape(-1)
    out = {}
    o = 0
    for k in SMALL:
        sz = math.prod(shapes[k])
        out[k] = flat[o:o + sz].reshape(shapes[k])
        o += sz + (-sz) % 1024
    return out
```

```python
import functools
import math

import jax
import jax.numpy as jnp
from jax import lax
from jax.experimental import pallas as pl
from jax.experimental.pallas import tpu as pltpu

F32 = jnp.float32
BF16 = jnp.bfloat16
MESH = pl.DeviceIdType.MESH

D = 1024
NCTX = 256
EPS = 1e-6
ROPE_BASE = 10000.0
GRID_W = 64
DEPTH = 2
Q = 128
N_HEADS_SSM = 16
SWA_HQ, SWA_HKV, SWA_DH, SWA_WIN = 8, 2, 128, 128
MLA_H, MLA_NOPE, MLA_ROPE, MLA_V = 8, 128, 64, 128
MLA_QRANK, MLA_KVRANK = 384, 256
FFN = 2816
RT = 256
VMEM_LIMIT = 56 << 20
NEG = -1e30
LOG2E = 1.4426950408889634

C_G1, C_G2, C_G3, C_Z, C_Q, C_XS, C_B, C_C, C_K, C_V, C_CKV, C_MISC, C_PAD, C_CQ = (
    0, 1024, 2048, 3072, 4096, 5120, 6144, 6400, 6656, 6912, 7168, 7424, 7552, 7680)
UW = 8064
DT_LANE = 64

ADAM_LR, ADAM_B1, ADAM_B2, ADAM_EPS, ADAM_WD, ADAM_STEP = 0.001, 0.9, 0.999, 1e-08, 0.01, 10


def _cp(sem):
    return pltpu.CompilerParams(dimension_semantics=sem, vmem_limit_bytes=VMEM_LIMIT)


def _pick(n, cands):
    for c in cands:
        if n % c == 0:
            return c
    return n


_TN = (1536, 1408, 1152, 1024, 896, 768, 512, 384, 256, 128)


def mm(a, b, out_dtype, name, trans_b=False):
    m, k = a.shape
    n = b.shape[0] if trans_b else b.shape[1]
    tm = _pick(m, (768, 512, 256, 128, 8))
    tn = _pick(n, _TN)
    tk = k if k <= 2048 else _pick(k, (1408, 1152, 1024, 896, 768, 512))
    nk = k // tk
    b_spec = (pl.BlockSpec((tn, tk), lambda i, j, kk: (j, kk)) if trans_b
              else pl.BlockSpec((tk, tn), lambda i, j, kk: (kk, j)))

    def body(a_ref, b_ref, o_ref, *acc):
        p = _d(a_ref[...], b_ref[...], ((1,), (1 if trans_b else 0,)))
        if nk == 1:
            o_ref[...] = p.astype(out_dtype)
        else:
            kk = pl.program_id(2)

            @pl.when(kk == 0)
            def _():
                acc[0][...] = p

            @pl.when(kk > 0)
            def _():
                acc[0][...] += p

            @pl.when(kk == nk - 1)
            def _():
                o_ref[...] = acc[0][...].astype(out_dtype)

    return pl.pallas_call(
        body, out_shape=jax.ShapeDtypeStruct((m, n), out_dtype), grid=(m // tm, n // tn, nk),
        in_specs=[pl.BlockSpec((tm, tk), lambda i, j, kk: (i, kk)), b_spec],
        out_specs=pl.BlockSpec((tm, tn), lambda i, j, kk: (i, j)),
        scratch_shapes=[] if nk == 1 else [pltpu.VMEM((tm, tn), F32)],
        name=name, compiler_params=_cp(("parallel", "parallel", "arbitrary")))(a, b)


def mm_tn(a, b, name, out_dtype=BF16):
    t, ka = a.shape
    _, nb = b.shape
    ta = _pick(ka, (1024, 1408, 768, 512, 384, 256, 128))
    tb = _pick(nb, _TN)
    tt = _pick(t, (768, 512, 256, 128, 8))
    nt = t // tt

    def body(a_ref, b_ref, o_ref, acc):
        p = _d(a_ref[...], b_ref[...], ((0,), (0,)))
        s = pl.program_id(2)

        @pl.when(s == 0)
        def _():
            acc[...] = p

        @pl.when(s > 0)
        def _():
            acc[...] += p

        @pl.when(s == nt - 1)
        def _():
            o_ref[...] = acc[...].astype(out_dtype)

    return pl.pallas_call(
        body, out_shape=jax.ShapeDtypeStruct((ka, nb), out_dtype), grid=(ka // ta, nb // tb, nt),
        in_specs=[pl.BlockSpec((tt, ta), lambda i, j, s: (s, i)), pl.BlockSpec((tt, tb), lambda i, j, s: (s, j))],
        out_specs=pl.BlockSpec((ta, tb), lambda i, j, s: (i, j)), scratch_shapes=[pltpu.VMEM((ta, tb), F32)],
        name=name, compiler_params=_cp(("parallel", "parallel", "arbitrary")))(a, b)


def _rms(x, g, n=None):
    n = x.shape[-1] if n is None else n
    r = lax.rsqrt(jnp.sum(x * x, axis=-1, keepdims=True) * (1.0 / n) + EPS)
    return x * r * g


def _silu(x):
    return x * jax.nn.sigmoid(x)


def _modulate(x, g, sc, sh):
    return _rms(x, g) * (1.0 + sc) + sh


def _swap(x, s):
    ax = x.ndim - 1
    w = x.shape[ax]
    lane = lax.broadcasted_iota(jnp.int32, x.shape, ax)
    lo = (lane & s) == 0
    return jnp.where(lo, pltpu.roll(x, w - s, ax), pltpu.roll(x, s, ax))


@functools.partial(jax.custom_vjp, nondiff_argnums=(3,))
def _rope(x, cos, sin, s):
    return x * cos + _swap(x, s) * sin


def _rope_fwd(x, cos, sin, s):
    return _rope(x, cos, sin, s), (cos, sin)


def _rope_bwd(s, res, g):
    cos, sin = res
    return g * cos - _swap(g, s) * sin, jnp.zeros_like(cos), jnp.zeros_like(sin)


_rope.defvjp(_rope_fwd, _rope_bwd)


@jax.custom_vjp
def _softplus(x):
    return jnp.maximum(x, 0.0) + jnp.log(1.0 + jnp.exp(-jnp.abs(x)))


def _softplus_fwd(x):
    return _softplus(x), x


def _softplus_bwd(x, g):
    return (g * jax.nn.sigmoid(x),)


_softplus.defvjp(_softplus_fwd, _softplus_bwd)


def _d(a, b, dims):
    return lax.dot_general(a.astype(BF16), b.astype(BF16), (dims, ((), ())), preferred_element_type=F32)


@jax.custom_vjp
def bdot(a, b):
    return _d(a, b, ((1,), (0,)))


bdot.defvjp(lambda a, b: (bdot(a, b), (a, b)),
            lambda r, g: (_d(g, r[1], ((1,), (1,))), _d(r[0], g, ((0,), (0,)))))


@jax.custom_vjp
def bdot_nt(a, b):
    return _d(a, b, ((1,), (1,)))


bdot_nt.defvjp(lambda a, b: (bdot_nt(a, b), (a, b)),
               lambda r, g: (_d(g, r[1], ((1,), (0,))), _d(g, r[0], ((0,), (0,)))))


@jax.custom_vjp
def bdot_tn(a, b):
    return _d(a, b, ((0,), (0,)))


bdot_tn.defvjp(lambda a, b: (bdot_tn(a, b), (a, b)),
               lambda r, g: (_d(r[1], g, ((1,), (1,))), _d(r[0], g, ((1,), (0,)))))


def _tri(rev):
    i = lax.broadcasted_iota(jnp.int32, (Q, Q), 0)
    j = lax.broadcasted_iota(jnp.int32, (Q, Q), 1)
    return (i <= j) if rev else (i >= j)


def _split3(a):
    hi = a.astype(BF16)
    r = a - hi.astype(F32)
    mid = r.astype(BF16)
    lo = (r - mid.astype(F32)).astype(BF16)
    return hi, mid, lo


def _cum_cols_impl(a, rev):
    t = _tri(rev).astype(BF16)
    return sum(jnp.dot(t, p, preferred_element_type=F32) for p in _split3(a))


def _cum_rows_impl(a, rev):
    t = _tri(not rev).astype(BF16)
    return sum(jnp.dot(p, t, preferred_element_type=F32) for p in _split3(a))


@functools.partial(jax.custom_vjp, nondiff_argnums=(1,))
def cum_cols(a, rev):
    return _cum_cols_impl(a, rev)


cum_cols.defvjp(lambda a, rev: (_cum_cols_impl(a, rev), None), lambda rev, _, g: (_cum_cols_impl(g, not rev),))


@functools.partial(jax.custom_vjp, nondiff_argnums=(1,))
def cum_rows(a, rev):
    return _cum_rows_impl(a, rev)


cum_rows.defvjp(lambda a, rev: (_cum_rows_impl(a, rev), None), lambda rev, _, g: (_cum_rows_impl(g, not rev),))


def _rs(w, cb=0):
    return pl.BlockSpec((RT, w), lambda i: (i, cb))


def _ps(shape):
    nd = len(shape)
    return pl.BlockSpec(shape, lambda i: (0,) * nd)


def _gs(w, cb, nlat):
    return pl.BlockSpec((1, 1, w), lambda i: (i // nlat, 0, cb))


def _rowcall(name, body, n, ins, outs, scratch=()):
    return pl.pallas_call(
        body, out_shape=[o[0] for o in outs], grid=(n // RT,), in_specs=[s for _, s in ins],
        out_specs=[s for _, s in outs], scratch_shapes=list(scratch), name=name,
        compiler_params=_cp(("arbitrary",)))(*[a for a, _ in ins])


def _acc(ref, val, first):
    @pl.when(first)
    def _():
        ref[...] = val

    @pl.when(jnp.logical_not(first))
    def _():
        ref[...] += val


def _sd(shape, dt):
    return jax.ShapeDtypeStruct(shape, dt)


def resid_mod_fwd(name, xp, o, mod_gt, gt_i, mod_n, sh_i, sc_i, norm_g, nlat):
    n = xp.shape[0]
    has_res = o is not None

    def body(*refs):
        if has_res:
            xp_ref, o_ref, gt_ref, sh_ref, sc_ref, g_ref, xn_ref, h_ref = refs
            xn = xp_ref[...] + gt_ref[0] * o_ref[...]
            xn_ref[...] = xn
        else:
            xp_ref, sh_ref, sc_ref, g_ref, h_ref = refs
            xn = xp_ref[...]
        h_ref[...] = _modulate(xn, g_ref[...], sc_ref[0], sh_ref[0]).astype(BF16)

    ins = [(xp, _rs(D))]
    if has_res:
        ins += [(o, _rs(D)), (mod_gt, _gs(D, gt_i, nlat))]
    ins += [(mod_n, _gs(D, sh_i, nlat)), (mod_n, _gs(D, sc_i, nlat)), (norm_g, _ps((1, D)))]
    outs = ([(_sd((n, D), F32), _rs(D))] if has_res else []) + [(_sd((n, D), BF16), _rs(D))]
    r = _rowcall(name, body, n, ins, outs)
    return (r[0], r[1]) if has_res else (xp, r[0])


def resid_mod_bwd(name, xn, dxn, dh, o, mod_gt, gt_i, mod_n, sh_i, sc_i, norm_g, nlat):
    n = xn.shape[0]
    has_res = o is not None

    def body(*refs):
        i = pl.program_id(0)
        if has_res:
            (xn_ref, dxn_ref, dh_ref, o_ref, gt_ref, sh_ref, sc_ref, g_ref,
             dx_ref, do_ref, dgt_ref, dsh_ref, dsc_ref, dg_ref) = refs
        else:
            xn_ref, dxn_ref, dh_ref, sh_ref, sc_ref, g_ref, dx_ref, dsh_ref, dsc_ref, dg_ref = refs
        _, vjp = jax.vjp(_modulate, xn_ref[...], g_ref[...], sc_ref[0], sh_ref[0])
        dx, dg, dsc, dsh = vjp(dh_ref[...])
        dx = dx + dxn_ref[...]
        dx_ref[...] = dx
        gfirst = (i == 0) | (i == nlat)
        _acc(dg_ref, dg, i == 0)
        _acc(dsh_ref, dsh[None], gfirst)
        _acc(dsc_ref, dsc[None], gfirst)
        if has_res:
            do_ref[...] = (gt_ref[0] * dx).astype(BF16)
            _acc(dgt_ref, jnp.sum(dx * o_ref[...], axis=0, keepdims=True)[None], gfirst)

    ins = [(xn, _rs(D)), (dxn, _rs(D)), (dh, _rs(D))]
    if has_res:
        ins += [(o, _rs(D)), (mod_gt, _gs(D, gt_i, nlat))]
    ins += [(mod_n, _gs(D, sh_i, nlat)), (mod_n, _gs(D, sc_i, nlat)), (norm_g, _ps((1, D)))]
    gacc = (_sd((2, 1, D), F32), _gs(D, 0, nlat))
    outs = [(_sd((n, D), F32), _rs(D))]
    if has_res:
        outs += [(_sd((n, D), BF16), _rs(D)), gacc]
    outs += [gacc, gacc, (_sd((1, D), F32), _ps((1, D)))]
    r = _rowcall(name, body, n, ins, outs)
    if has_res:
        return r
    return r[0], None, None, r[1], r[2], r[3]


def resid_loss(name, xp, o, mod_gt, gt_i, target, nlat):
    n = xp.shape[0]

    def body(xp_ref, o_ref, gt_ref, t_ref, loss_ref, dx_ref, do_ref, dgt_ref):
        i = pl.program_id(0)
        gt = gt_ref[0]

        @pl.when(i < nlat)
        def _():
            err = xp_ref[...] + gt * o_ref[...] - t_ref[...]
            dx = err * (1.0 / D)
            dx_ref[...] = dx
            do_ref[...] = (gt * dx).astype(BF16)
            _acc(loss_ref, jnp.full((1, 128), 0.5 / D, F32) * jnp.sum(err * err), i == 0)
            _acc(dgt_ref, jnp.sum(dx * o_ref[...], axis=0, keepdims=True)[None], i == 0)

        @pl.when(i >= nlat)
        def _():
            dx_ref[...] = jnp.zeros((RT, D), F32)
            do_ref[...] = jnp.zeros((RT, D), BF16)
            dgt_ref[...] = jnp.zeros((1, 1, D), F32)

    tgt_spec = pl.BlockSpec((RT, D), lambda i: (jnp.minimum(i, nlat - 1), 0))
    ins = [(xp, _rs(D)), (o, _rs(D)), (mod_gt, _gs(D, gt_i, nlat)), (target, tgt_spec)]
    outs = [(_sd((1, 128), F32), _ps((1, 128))), (_sd((n, D), F32), _rs(D)), (_sd((n, D), BF16), _rs(D)),
            (_sd((2, 1, D), F32), _gs(D, 0, nlat))]
    return _rowcall(name, body, n, ins, outs)


def mod_fwd(name, c8, w_mod, b_mod):
    tn = 1536

    def body(c_ref, w_ref, b_ref, o_ref, s_ref):
        s = _silu(c_ref[...]).astype(BF16)
        s_ref[...] = s
        o_ref[...] = jnp.dot(s, w_ref[...], preferred_element_type=F32) + b_ref[...]

    return pl.pallas_call(
        body, out_shape=[_sd((8, 6 * D), F32), _sd((8, D), BF16)], grid=(6 * D // tn,),
        in_specs=[pl.BlockSpec((8, D), lambda j: (0, 0)), pl.BlockSpec((D, tn), lambda j: (0, j)),
                  pl.BlockSpec((1, tn), lambda j: (0, j))],
        out_specs=[pl.BlockSpec((8, tn), lambda j: (0, j)), pl.BlockSpec((8, D), lambda j: (0, 0))],
        name=name, compiler_params=_cp(("arbitrary",)))(c8, w_mod, b_mod)


def mod_small_bwd(name, c8, dsilu, dmod8):
    def body(c_ref, ds_ref, dm_ref, dc_ref, db_ref):
        _, vjp = jax.vjp(_silu, c_ref[...])
        dc_ref[...] = vjp(ds_ref[...])[0]
        db_ref[...] = jnp.sum(dm_ref[...], axis=0, keepdims=True)

    return pl.pallas_call(
        body, out_shape=[_sd((8, D), F32), _sd((1, 6 * D), F32)], grid=(1,),
        in_specs=[pl.BlockSpec((8, D), lambda j: (0, 0)), pl.BlockSpec((8, D), lambda j: (0, 0)),
                  pl.BlockSpec((8, 6 * D), lambda j: (0, 0))],
        out_specs=[pl.BlockSpec((8, D), lambda j: (0, 0)), pl.BlockSpec((1, 6 * D), lambda j: (0, 0))],
        name=name, compiler_params=_cp(("arbitrary",)))(c8, dsilu, dmod8)


def _conv_taps(x, nlat):
    n = x.shape[0]
    r = lax.broadcasted_iota(jnp.int32, x.shape, 0)
    lo = jnp.where(r < nlat, 0, nlat)
    hi = jnp.where(r < nlat, nlat, n)
    taps = []
    for o in (-2, -1, 0, 1, 2):
        xs = x if o == 0 else pltpu.roll(x, (-o) % n, 0)
        t = r + o
        taps.append(jnp.where((t >= lo) & (t < hi), xs, 0.0))
    return taps


def conv_fwd(name, u, w, b, nlat_rows):
    n = u.shape[0]

    def body(x_ref, w_ref, b_ref, o_ref):
        taps = _conv_taps(x_ref[...], nlat_rows)
        wv = w_ref[...]
        pre = b_ref[...] + sum(taps[k] * wv[k:k + 1, :] for k in range(5))
        o_ref[...] = _silu(pre)

    return pl.pallas_call(
        body, out_shape=_sd((n, 1536), F32), grid=(12,),
        in_specs=[pl.BlockSpec((n, 128), lambda j: (0, C_XS // 128 + j)), pl.BlockSpec((5, 128), lambda j: (0, j)),
                  pl.BlockSpec((1, 128), lambda j: (0, j))],
        out_specs=pl.BlockSpec((n, 128), lambda j: (0, j)),
        name=name, compiler_params=_cp(("parallel",)))(u, w, b)


def conv_bwd(name, u, dact, w, b, nlat_rows):
    n = u.shape[0]

    def body(x_ref, da_ref, w_ref, b_ref, dx_ref, dw_ref, db_ref):
        taps = _conv_taps(x_ref[...], nlat_rows)
        wv = w_ref[...]
        pre = b_ref[...] + sum(taps[k] * wv[k:k + 1, :] for k in range(5))
        s = jax.nn.sigmoid(pre)
        dpre = da_ref[...] * (s * (1.0 + pre * (1.0 - s)))
        db_ref[...] = jnp.sum(dpre, axis=0, keepdims=True)
        rows = lax.broadcasted_iota(jnp.int32, (5, 128), 0)
        dw = jnp.zeros((5, 128), F32)
        for k in range(5):
            dw = dw + jnp.where(rows == k, jnp.sum(dpre * taps[k], axis=0, keepdims=True), 0.0)
        dw_ref[...] = dw
        r = lax.broadcasted_iota(jnp.int32, dpre.shape, 0)
        lo = jnp.where(r < nlat_rows, 0, nlat_rows)
        hi = jnp.where(r < nlat_rows, nlat_rows, n)
        dx = jnp.zeros_like(dpre)
        for k in range(5):
            o = k - 2
            ds = dpre if o == 0 else pltpu.roll(dpre, o % n, 0)
            t = r - o
            dx = dx + jnp.where((t >= lo) & (t < hi), ds, 0.0) * wv[k:k + 1, :]
        dx_ref[...] = dx.astype(BF16)

    return pl.pallas_call(
        body, out_shape=[_sd((n, 1536), BF16), _sd((5, 1536), F32), _sd((1, 1536), F32)], grid=(12,),
        in_specs=[pl.BlockSpec((n, 128), lambda j: (0, C_XS // 128 + j)), pl.BlockSpec((n, 128), lambda j: (0, j)),
                  pl.BlockSpec((5, 128), lambda j: (0, j)), pl.BlockSpec((1, 128), lambda j: (0, j))],
        out_specs=[pl.BlockSpec((n, 128), lambda j: (0, j)), pl.BlockSpec((5, 128), lambda j: (0, j)),
                   pl.BlockSpec((1, 128), lambda j: (0, j))],
        name=name, compiler_params=_cp(("parallel",)))(u, dact, w, b)


def _ssd_chunk(rev, dirn, g, x4, bm, cm, misc, dtrow, bias_c, alog_c, bias_r, alog_r, h4):
    dt_c = _softplus(misc + bias_c)
    a_c = dt_c * (-jnp.exp(alog_c))
    dt_r = _softplus(dtrow + bias_r)
    a_r = dt_r * (-jnp.exp(alog_r))
    cs_c = cum_cols(a_c, rev)
    cs_r = cum_rows(a_r, rev)
    tot_c = jnp.sum(a_c, axis=0, keepdims=True)
    cb = bdot_nt(cm, bm)
    tri = _tri(rev)
    lane = lax.broadcasted_iota(jnp.int32, (1, 128), 1)
    row16 = lax.broadcasted_iota(jnp.int32, (16, 1), 0)
    prow = lax.broadcasted_iota(jnp.int32, (128, 1), 0)
    ys, hs = [], []
    for p in range(4):
        ydiag = 0.0
        wst = 0.0
        eoff = 0.0
        hscale = 0.0
        for e in range(2):
            hg = 8 * g + 2 * p + e
            oh_c = (lane == DT_LANE + 16 * dirn + hg).astype(F32)
            dt_h = jnp.sum(dt_c * oh_c, axis=1, keepdims=True)
            cs_h = jnp.sum(cs_c * oh_c, axis=1, keepdims=True)
            tot_h = jnp.sum(tot_c * oh_c, axis=1, keepdims=True)
            csr_h = jnp.sum(cs_r * (row16 == hg).astype(F32), axis=0, keepdims=True)
            seg = jnp.exp(jnp.where(tri, cs_h - csr_h, -jnp.inf))
            hm = ((lane < 64) if e == 0 else (lane >= 64)).astype(F32)
            ydiag = ydiag + bdot(cb * seg, x4[p] * (dt_h * hm))
            wst = wst + (dt_h * jnp.exp(tot_h - cs_h)) * hm
            eoff = eoff + jnp.exp(cs_h) * hm
            hscale = hscale + jnp.exp(tot_h) * ((prow < 64) if e == 0 else (prow >= 64)).astype(F32)
        ys.append(ydiag + bdot_nt(cm, h4[p]) * eoff)
        hs.append(h4[p] * hscale + bdot_tn(x4[p] * wst, bm))
    return ys, hs


def _ssd_specs(nlat_chunks, rev, dirn, bwd):
    nc = nlat_chunks + 2

    def chunk(s):
        if bwd:
            s = nc - 1 - s
        return (nlat_chunks + 1 - s) if rev else (s + nlat_chunks) % nc

    def step(s):
        return (nc - 1 - s) if bwd else s

    return dict(
        x=pl.BlockSpec((Q, 512), lambda g, s: (chunk(s), g)),
        b=pl.BlockSpec((Q, 128), lambda g, s: (chunk(s), 8 + g)),
        c=pl.BlockSpec((Q, 128), lambda g, s: (chunk(s), 10 + g)),
        misc=pl.BlockSpec((Q, 128), lambda g, s: (chunk(s), C_MISC // 128)),
        dtrow=pl.BlockSpec((16, Q), lambda g, s: (dirn, chunk(s))),
        p_c=pl.BlockSpec((1, 128), lambda g, s: (0, 0)),
        p_r=pl.BlockSpec((16, 1), lambda g, s: (dirn, 0)),
        y=pl.BlockSpec((Q, 512), lambda g, s: (chunk(s), g)),
        hsave=pl.BlockSpec((1, 1, 512, 128), lambda g, s: (g, step(s), 0, 0)),
        bc_out=pl.BlockSpec((Q, 128), lambda g, s: (chunk(s), g)),
        misc_out=pl.BlockSpec((1, Q, 128), lambda g, s: (g, chunk(s), 0)),
        dtrow_out=pl.BlockSpec((1, 16, Q), lambda g, s: (g, 0, chunk(s))),
        pacc_c=pl.BlockSpec((1, 128), lambda g, s: (0, 0)),
        pacc_r=pl.BlockSpec((16, 1), lambda g, s: (0, 0)),
    )


def ssd_fwd(name, xbc, u, dtrow, bias_c, alog_c, bias_r, alog_r, nlat_chunks, rev, dirn):
    n = xbc.shape[0]
    nc = nlat_chunks + 2
    sp = _ssd_specs(nlat_chunks, rev, dirn, False)

    def body(x_ref, b_ref, c_ref, m_ref, r_ref, bc_ref, ac_ref, br_ref, ar_ref, y_ref, hs_ref, h_s):
        g = pl.program_id(0)
        s = pl.program_id(1)

        @pl.when(s == 0)
        def _():
            h_s[...] = jnp.zeros((512, 128), F32)

        hs_ref[0, 0] = h_s[...]
        x4 = [x_ref[:, 128 * p:128 * p + 128] for p in range(4)]
        h4 = [h_s[128 * p:128 * p + 128, :] for p in range(4)]
        ys, hs = _ssd_chunk(rev, dirn, g, x4, b_ref[...], c_ref[...], m_ref[...], r_ref[...],
                            bc_ref[...], ac_ref[...], br_ref[...], ar_ref[...], h4)
        for p in range(4):
            y_ref[:, 128 * p:128 * p + 128] = ys[p]
            h_s[128 * p:128 * p + 128, :] = hs[p]

    return pl.pallas_call(
        body, out_shape=[_sd((n, 1024), F32), _sd((2, nc, 512, 128), F32)], grid=(2, nc),
        in_specs=[sp["x"], sp["b"], sp["c"], sp["misc"], sp["dtrow"], sp["p_c"], sp["p_c"], sp["p_r"], sp["p_r"]],
        out_specs=[sp["y"], sp["hsave"]], scratch_shapes=[pltpu.VMEM((512, 128), F32)],
        name=name, compiler_params=_cp(("arbitrary", "arbitrary")))(
            xbc, xbc, xbc, u, dtrow, bias_c, alog_c, bias_r, alog_r)


def ssd_bwd(name, xbc, u, dtrow, bias_c, alog_c, bias_r, alog_r, hsave, dy, acc, nlat_chunks, rev, dirn):
    n = xbc.shape[0]
    sp = _ssd_specs(nlat_chunks, rev, dirn, True)

    def body(x_ref, b_ref, c_ref, m_ref, r_ref, bc_ref, ac_ref, br_ref, ar_ref, hs_ref, dy_ref, ax_ref, ab_ref, acc_ref,
             dx_ref, db_ref, dc_ref, dm_ref, dr_ref, dbc_ref, dac_ref, dbr_ref, dar_ref, dh_s):
        g = pl.program_id(0)
        s = pl.program_id(1)

        @pl.when(s == 0)
        def _():
            dh_s[...] = jnp.zeros((512, 128), F32)

        x4 = [x_ref[:, 128 * p:128 * p + 128] for p in range(4)]
        h4 = [hs_ref[0, 0, 128 * p:128 * p + 128, :] for p in range(4)]
        fn = functools.partial(_ssd_chunk, rev, dirn, g)
        _, vjp = jax.vjp(fn, x4, b_ref[...], c_ref[...], m_ref[...], r_ref[...],
                         bc_ref[...], ac_ref[...], br_ref[...], ar_ref[...], h4)
        dys = [dy_ref[:, 128 * p:128 * p + 128] for p in range(4)]
        dhs = [dh_s[128 * p:128 * p + 128, :] for p in range(4)]
        dx4, db, dc, dm, dr, dbc, dac, dbr, dar, dh4 = vjp((dys, dhs))
        for p in range(4):
            dx_ref[:, 128 * p:128 * p + 128] = dx4[p] + ax_ref[:, 128 * p:128 * p + 128]
            dh_s[128 * p:128 * p + 128, :] = dh4[p]
        db_ref[...] = db + ab_ref[...]
        dc_ref[...] = dc + acc_ref[...]
        dm_ref[0] = dm
        dr_ref[0] = dr
        first = (g == 0) & (s == 0)
        _acc(dbc_ref, dbc, first)
        _acc(dac_ref, dac, first)
        _acc(dbr_ref, dbr, first)
        _acc(dar_ref, dar, first)

    ax, ab, ac = acc
    return pl.pallas_call(
        body,
        out_shape=[_sd((n, 1024), F32), _sd((n, 256), F32), _sd((n, 256), F32), _sd((2, n, 128), F32),
                   _sd((2, 16, n), F32), _sd((1, 128), F32), _sd((1, 128), F32), _sd((16, 1), F32), _sd((16, 1), F32)],
        grid=(2, nlat_chunks + 2),
        in_specs=[sp["x"], sp["b"], sp["c"], sp["misc"], sp["dtrow"], sp["p_c"], sp["p_c"], sp["p_r"], sp["p_r"],
                  sp["hsave"], sp["y"], sp["y"], sp["bc_out"], sp["bc_out"]],
        out_specs=[sp["y"], sp["bc_out"], sp["bc_out"], sp["misc_out"], sp["dtrow_out"],
                   sp["pacc_c"], sp["pacc_c"], sp["pacc_r"], sp["pacc_r"]],
        scratch_shapes=[pltpu.VMEM((512, 128), F32)],
        name=name, compiler_params=_cp(("arbitrary", "arbitrary")))(
            xbc, xbc, xbc, u, dtrow, bias_c, alog_c, bias_r, alog_r, hsave, dy, ax, ab, ac)


def _ssd_out(yf, yb, xs, z, g, dexp):
    return _rms((yf + yb + dexp * xs) * _silu(z), g)


def ssd_out_fwd(name, yf, yb, xbc, u, g, dexp):
    n = yf.shape[0]

    def body(yf_ref, yb_ref, xs_ref, z_ref, g_ref, d_ref, o_ref):
        o_ref[...] = _ssd_out(yf_ref[...], yb_ref[...], xs_ref[...], z_ref[...], g_ref[...], d_ref[...]).astype(BF16)

    return _rowcall(name, body, n,
                    [(yf, _rs(D)), (yb, _rs(D)), (xbc, _rs(D, 0)), (u, _rs(D, C_Z // D)), (g, _ps((1, D))), (dexp, _ps((1, D)))],
                    [(_sd((n, D), BF16), _rs(D))])[0]


def ssd_out_bwd(name, yf, yb, xbc, u, g, dexp, dys):
    n = yf.shape[0]

    def body(yf_ref, yb_ref, xs_ref, z_ref, g_ref, d_ref, dys_ref, dy_ref, dxs_ref, dz_ref, dg_ref, dd_ref):
        i = pl.program_id(0)
        _, vjp = jax.vjp(_ssd_out, yf_ref[...], yb_ref[...], xs_ref[...], z_ref[...], g_ref[...], d_ref[...])
        dyf, _, dxs, dz, dg, dd = vjp(dys_ref[...])
        dy_ref[...] = dyf
        dxs_ref[...] = dxs
        dz_ref[...] = dz.astype(BF16)
        _acc(dg_ref, dg, i == 0)
        _acc(dd_ref, dd, i == 0)

    return _rowcall(name, body, n,
                    [(yf, _rs(D)), (yb, _rs(D)), (xbc, _rs(D, 0)), (u, _rs(D, C_Z // D)), (g, _ps((1, D))), (dexp, _ps((1, D))),
                     (dys, _rs(D))],
                    [(_sd((n, D), F32), _rs(D)), (_sd((n, D), F32), _rs(D)), (_sd((n, D), BF16), _rs(D)),
                     (_sd((1, D), F32), _ps((1, D))), (_sd((1, D), F32), _ps((1, D)))])


def _normrope(x, g, cos, sin, s, n=None):
    return _rope(_rms(x, g, n), cos, sin, s)


def swa_prep_fwd(name, u, gq, gk, cos, sin):
    n = u.shape[0]

    def body(q_ref, k_ref, gq_ref, gk_ref, cos_ref, sin_ref, qs_ref, ks_ref):
        cs, sn = cos_ref[...], sin_ref[...]
        for h in range(SWA_HQ):
            sl = slice(128 * h, 128 * h + 128)
            qs_ref[:, sl] = _normrope(q_ref[:, sl], gq_ref[...], cs, sn, 32).astype(BF16)
        for h in range(SWA_HKV):
            sl = slice(128 * h, 128 * h + 128)
            ks_ref[:, sl] = _normrope(k_ref[:, sl], gk_ref[...], cs, sn, 32).astype(BF16)

    return _rowcall(name, body, n,
                    [(u, _rs(1024, C_Q // 1024)), (u, _rs(256, C_K // 256)), (gq, _ps((1, 128))), (gk, _ps((1, 128))),
                     (cos, _rs(128)), (sin, _rs(128))],
                    [(_sd((n, 1024), BF16), _rs(1024)), (_sd((n, 256), BF16), _rs(256))])


def swa_prep_bwd(name, u, gq, gk, cos, sin, dqs, dks, dv):
    n = u.shape[0]

    def body(q_ref, k_ref, gq_ref, gk_ref, cos_ref, sin_ref, dqs_ref, dks_ref, dv_ref,
             dq_ref, dk_ref, dvo_ref, dgq_ref, dgk_ref):
        i = pl.program_id(0)
        cs, sn = cos_ref[...], sin_ref[...]
        fn = lambda x, g: _normrope(x, g, cs, sn, 32)
        dgq = jnp.zeros((1, 128), F32)
        dgk = jnp.zeros((1, 128), F32)
        for h in range(SWA_HQ):
            sl = slice(128 * h, 128 * h + 128)
            _, vjp = jax.vjp(fn, q_ref[:, sl], gq_ref[...])
            dx, dg = vjp(dqs_ref[:, sl])
            dq_ref[:, sl] = dx.astype(BF16)
            dgq = dgq + dg
        for h in range(SWA_HKV):
            sl = slice(128 * h, 128 * h + 128)
            _, vjp = jax.vjp(fn, k_ref[:, sl], gk_ref[...])
            dx, dg = vjp(dks_ref[:, sl])
            dk_ref[:, sl] = dx.astype(BF16)
            dgk = dgk + dg
        dvo_ref[...] = dv_ref[...].astype(BF16)
        _acc(dgq_ref, dgq, i == 0)
        _acc(dgk_ref, dgk, i == 0)

    return _rowcall(name, body, n,
                    [(u, _rs(1024, C_Q // 1024)), (u, _rs(256, C_K // 256)), (gq, _ps((1, 128))), (gk, _ps((1, 128))),
                     (cos, _rs(128)), (sin, _rs(128)), (dqs, _rs(1024)), (dks, _rs(256)), (dv, _rs(256))],
                    [(_sd((n, 1024), BF16), _rs(1024)), (_sd((n, 256), BF16), _rs(256)), (_sd((n, 256), BF16), _rs(256)),
                     (_sd((1, 128), F32), _ps((1, 128))), (_sd((1, 128), F32), _ps((1, 128)))])


def lat_norm_fwd(name, u, g_kv, g_q):
    n = u.shape[0]

    def body(ckv_ref, cq_ref, gkv_ref, gq_ref, okv_ref, oq_ref):
        okv_ref[...] = _rms(ckv_ref[...], gkv_ref[...]).astype(BF16)
        oq_ref[...] = _rms(cq_ref[...], gq_ref[...]).astype(BF16)

    return _rowcall(name, body, n,
                    [(u, _rs(256, C_CKV // 256)), (u, _rs(384, C_CQ // 384)), (g_kv, _ps((1, 256))), (g_q, _ps((1, 384)))],
                    [(_sd((n, 256), BF16), _rs(256)), (_sd((n, 384), BF16), _rs(384))])


def lat_norm_bwd(name, u, g_kv, g_q, dkvn, dqn):
    n = u.shape[0]

    def body(ckv_ref, cq_ref, gkv_ref, gq_ref, dkvn_ref, dqn_ref, dckv_ref, dcq_ref, dgkv_ref, dgq_ref):
        i = pl.program_id(0)
        _, vjp = jax.vjp(_rms, ckv_ref[...], gkv_ref[...])
        dx, dg = vjp(dkvn_ref[...])
        dckv_ref[...] = dx.astype(BF16)
        _acc(dgkv_ref, dg, i == 0)
        _, vjp = jax.vjp(_rms, cq_ref[...], gq_ref[...])
        dx, dg = vjp(dqn_ref[...])
        dcq_ref[...] = dx.astype(BF16)
        _acc(dgq_ref, dg, i == 0)

    return _rowcall(name, body, n,
                    [(u, _rs(256, C_CKV // 256)), (u, _rs(384, C_CQ // 384)), (g_kv, _ps((1, 256))), (g_q, _ps((1, 384))),
                     (dkvn, _rs(256)), (dqn, _rs(384))],
                    [(_sd((n, 256), BF16), _rs(256)), (_sd((n, 384), BF16), _rs(384)),
                     (_sd((1, 256), F32), _ps((1, 256))), (_sd((1, 384), F32), _ps((1, 384)))])


def _lane_lt64(x):
    return (lax.broadcasted_iota(jnp.int32, (1, 128), 1) < 64).astype(F32) * x


def _mla_krope(misc, g, cos, sin):
    return _normrope(_lane_lt64(misc), g, cos, sin, 16, MLA_ROPE)


def mla_prep_fwd(name, kv, qp, u, qg, kg, cos, sin):
    n = kv.shape[0]

    def body(kv_ref, v_ref, q_ref, m_ref, qg_ref, kg_ref, cos_ref, sin_ref, km_ref, qm_ref, vm_ref):
        cs, sn = cos_ref[...], sin_ref[...]
        vm_ref[...] = v_ref[...].astype(BF16)
        kr = _mla_krope(m_ref[...], kg_ref[:, 128:256], cs, sn).astype(BF16)
        for h in range(MLA_H):
            km_ref[:, 256 * h:256 * h + 128] = _rms(kv_ref[:, 128 * h:128 * h + 128], kg_ref[:, 0:128]).astype(BF16)
            km_ref[:, 256 * h + 128:256 * h + 256] = kr
            qm_ref[:, 256 * h:256 * h + 128] = _rms(q_ref[:, 256 * h:256 * h + 128], qg_ref[:, 0:128]).astype(BF16)
            qm_ref[:, 256 * h + 128:256 * h + 256] = _normrope(
                q_ref[:, 256 * h + 128:256 * h + 256], qg_ref[:, 128:256], cs, sn, 16, MLA_ROPE).astype(BF16)

    return _rowcall(name, body, n,
                    [(kv, _rs(1024, 0)), (kv, _rs(1024, 1)), (qp, _rs(2048)), (u, _rs(128, C_MISC // 128)), (qg, _ps((1, 256))),
                     (kg, _ps((1, 256))), (cos, _rs(128)), (sin, _rs(128))],
                    [(_sd((n, 2048), BF16), _rs(2048)), (_sd((n, 2048), BF16), _rs(2048)), (_sd((n, 1024), BF16), _rs(1024))])


def mla_prep_bwd(name, kv, qp, u, qg, kg, cos, sin, dkm, dqm, dv):
    n = kv.shape[0]

    def body(kv_ref, q_ref, m_ref, qg_ref, kg_ref, cos_ref, sin_ref, dkm_ref, dqm_ref, dv_ref,
             dkv_ref, dq_ref, dkr_ref, dqg_ref, dkg_ref):
        i = pl.program_id(0)
        cs, sn = cos_ref[...], sin_ref[...]
        fr = lambda x, g: _normrope(x, g, cs, sn, 16, MLA_ROPE)
        dkg_n = jnp.zeros((1, 128), F32)
        dqg_n = jnp.zeros((1, 128), F32)
        dqg_r = jnp.zeros((1, 128), F32)
        dkr_sum = jnp.zeros((RT, 128), F32)
        for h in range(MLA_H):
            _, vjp = jax.vjp(_rms, kv_ref[:, 128 * h:128 * h + 128], kg_ref[:, 0:128])
            dx, dg = vjp(dkm_ref[:, 256 * h:256 * h + 128])
            dkv_ref[:, 128 * h:128 * h + 128] = dx.astype(BF16)
            dkg_n = dkg_n + dg
            dkr_sum = dkr_sum + dkm_ref[:, 256 * h + 128:256 * h + 256]
            _, vjp = jax.vjp(_rms, q_ref[:, 256 * h:256 * h + 128], qg_ref[:, 0:128])
            dx, dg = vjp(dqm_ref[:, 256 * h:256 * h + 128])
            dq_ref[:, 256 * h:256 * h + 128] = dx.astype(BF16)
            dqg_n = dqg_n + dg
            _, vjp = jax.vjp(fr, q_ref[:, 256 * h + 128:256 * h + 256], qg_ref[:, 128:256])
            dx, dg = vjp(dqm_ref[:, 256 * h + 128:256 * h + 256])
            dq_ref[:, 256 * h + 128:256 * h + 256] = dx.astype(BF16)
            dqg_r = dqg_r + dg
        _, vjp = jax.vjp(lambda m, g: _mla_krope(m, g, cs, sn), m_ref[...], kg_ref[:, 128:256])
        dm, dkg_r = vjp(dkr_sum)
        dkr_ref[...] = dm
        dkv_ref[:, 1024:2048] = dv_ref[...].astype(BF16)
        _acc(dqg_ref.at[:, 0:128], dqg_n, i == 0)
        _acc(dqg_ref.at[:, 128:256], dqg_r, i == 0)
        _acc(dkg_ref.at[:, 0:128], dkg_n, i == 0)
        _acc(dkg_ref.at[:, 128:256], dkg_r, i == 0)

    return _rowcall(name, body, n,
                    [(kv, _rs(1024, 0)), (qp, _rs(2048)), (u, _rs(128, C_MISC // 128)), (qg, _ps((1, 256))), (kg, _ps((1, 256))),
                     (cos, _rs(128)), (sin, _rs(128)), (dkm, _rs(2048)), (dqm, _rs(2048)), (dv, _rs(1024))],
                    [(_sd((n, 2048), BF16), _rs(2048)), (_sd((n, 2048), BF16), _rs(2048)), (_sd((n, 128), F32), _rs(128)),
                     (_sd((1, 256), F32), _ps((1, 256))), (_sd((1, 256), F32), _ps((1, 256)))])


def misc_combine(name, dkr, dm_f, dm_b, drow_t):
    n = dkr.shape[0]

    def body(a_ref, f_ref, b_ref, r_ref, o_ref):
        o_ref[...] = (a_ref[...] + f_ref[0] + f_ref[1] + b_ref[0] + b_ref[1] + r_ref[...]).astype(BF16)

    g2 = pl.BlockSpec((2, RT, 128), lambda i: (0, i, 0))
    return _rowcall(name, body, n, [(dkr, _rs(128)), (dm_f, g2), (dm_b, g2), (drow_t, _rs(128))],
                    [(_sd((n, 128), BF16), _rs(128))])[0]


def _f32(ref):
    return ref[...].astype(F32)


def _merge(g1, g2, g3, p1, p2, p3):
    return jax.nn.sigmoid(g1) * p1 + jax.nn.sigmoid(g2) * p2 + jax.nn.sigmoid(g3) * p3


def merge_fwd(name, u, p1, p2, p3):
    n = u.shape[0]

    def body(g1, g2, g3, a, b, c, o_ref):
        o_ref[...] = _merge(g1[...], g2[...], g3[...], _f32(a), _f32(b), _f32(c)).astype(BF16)

    return _rowcall(name, body, n, [(u, _rs(D, 0)), (u, _rs(D, 1)), (u, _rs(D, 2)), (p1, _rs(D)), (p2, _rs(D)), (p3, _rs(D))],
                    [(_sd((n, D), BF16), _rs(D))])[0]


def merge_bwd(name, u, p1, p2, p3, dm):
    n = u.shape[0]

    def body(g1, g2, g3, a, b, c, dm_ref, d1, d2, d3, dg_ref):
        _, vjp = jax.vjp(_merge, g1[...], g2[...], g3[...], _f32(a), _f32(b), _f32(c))
        r = vjp(dm_ref[...])
        for k in range(3):
            dg_ref[:, D * k:D * k + D] = r[k].astype(BF16)
        d1[...] = r[3].astype(BF16)
        d2[...] = r[4].astype(BF16)
        d3[...] = r[5].astype(BF16)

    return _rowcall(name, body, n,
                    [(u, _rs(D, 0)), (u, _rs(D, 1)), (u, _rs(D, 2)), (p1, _rs(D)), (p2, _rs(D)), (p3, _rs(D)), (dm, _rs(D))],
                    [(_sd((n, D), BF16), _rs(D))] * 3 + [(_sd((n, 3 * D), BF16), _rs(3 * D))])


def _swiglu(g, u):
    return _silu(g) * u


def swiglu_fwd(name, gu):
    n = gu.shape[0]

    def body(g_ref, u_ref, o_ref):
        o_ref[...] = _swiglu(_f32(g_ref), _f32(u_ref)).astype(BF16)

    return _rowcall(name, body, n, [(gu, _rs(FFN, 0)), (gu, _rs(FFN, 1))], [(_sd((n, FFN), BF16), _rs(FFN))])[0]


def swiglu_bwd(name, gu, da):
    n = gu.shape[0]

    def body(g_ref, u_ref, da_ref, o_ref):
        _, vjp = jax.vjp(_swiglu, _f32(g_ref), _f32(u_ref))
        dg, du = vjp(da_ref[...])
        o_ref[:, 0:FFN] = dg.astype(BF16)
        o_ref[:, FFN:2 * FFN] = du.astype(BF16)

    return _rowcall(name, body, n, [(gu, _rs(FFN, 0)), (gu, _rs(FFN, 1)), (da, _rs(FFN))],
                    [(_sd((n, 2 * FFN), BF16), _rs(2 * FFN))])[0]


FLASH_ROWS = 256


def _fold_lanes(x, op):
    acc = x[:, 0:128]
    for b in range(1, x.shape[1] // 128):
        acc = op(acc, x[:, 128 * b:128 * b + 128])
    return acc


def _band_mask(tq, tk, i, kb):
    qp = i * tq + lax.broadcasted_iota(jnp.int32, (tq, tk), 0)
    kp = kb * tk + lax.broadcasted_iota(jnp.int32, (tq, tk), 1)
    return jnp.abs(qp - kp) <= SWA_WIN


def flash_fwd(name, qa, ka, va, *, w, vw, hq, grp, vcol0, scale, nlat, tq, tk, band, sink, ctx_q, prev=None):
    n = qa.shape[0]
    cblk = nlat // NCTX
    band = band and not ctx_q
    assert not band, "latent rows of a banded attention go through swa_fwd_lat"
    if ctx_q:
        tq = tk = NCTX
        grid = (hq, 1, 1)
        qmap = lambda h, i, kk: (cblk, h)
        kmap = lambda h, i, kk: (cblk, h // grp)
        vmap = lambda h, i, kk: (cblk, vcol0 + h // grp)
        omap = lambda h, i, kk: (cblk, h)
        lmap = lambda h, i, kk: (h, cblk, 0)
    else:
        nb = nlat // tk
        nk = 3 if band else nb
        grid = (hq, nlat // tq, nk)
        kb_of = (lambda i, kk: jnp.clip(i + kk - 1, 0, nb - 1)) if band else (lambda i, kk: kk)
        qmap = lambda h, i, kk: (i, h)
        kmap = lambda h, i, kk: (kb_of(i, kk), h // grp)
        vmap = lambda h, i, kk: (kb_of(i, kk), vcol0 + h // grp)
        omap = lambda h, i, kk: (i, h)
        lmap = lambda h, i, kk: (h, i, 0)
    nk = grid[2]
    extra = not ctx_q
    has_sink = sink is not None

    def body(*refs):
        refs = list(refs)
        q_ref, k_ref, v_ref = refs[:3]
        pos = 3
        if extra:
            ke_ref, ve_ref = refs[pos:pos + 2]
            pos += 2
        if has_sink:
            s_ref = refs[pos]
            pos += 1
        if prev is not None:
            pos += 2
        o_ref, l_ref, m_s, l_s, a_s = refs[pos:pos + 5]
        kk = pl.program_id(2)
        tr = min(tq, FLASH_ROWS)

        def step(kblk, vblk):
            for r in range(tq // tr):
                rows = slice(r * tr, (r + 1) * tr)
                s = _d(q_ref[rows, :], kblk, ((1,), (1,))) * (scale * LOG2E)
                m_prev = m_s[rows, :]
                m_new = jnp.maximum(m_prev, jnp.max(_fold_lanes(s, jnp.maximum), axis=1, keepdims=True))
                alpha = jnp.exp2(m_prev - m_new)
                p = jnp.exp2(s - m_new)
                l_s[rows, :] = alpha * l_s[rows, :] + _fold_lanes(p, jnp.add)
                a_s[rows, :] = alpha * a_s[rows, :] + _d(p, vblk, ((1,), (0,)))
                m_s[rows, :] = m_new

        @pl.when(kk == 0)
        def _():
            if has_sink:
                sv = jnp.max(s_ref[0], axis=1, keepdims=True) * LOG2E
                m_s[...] = jnp.zeros((tq, 1), F32) + sv
                l_s[...] = (lax.broadcasted_iota(jnp.int32, (tq, 128), 1) == 0).astype(F32)
            else:
                m_s[...] = jnp.full((tq, 1), NEG, F32)
                l_s[...] = jnp.zeros((tq, 128), F32)
            a_s[...] = jnp.zeros((tq, vw), F32)
            if extra:
                step(ke_ref[...], ve_ref[...])

        step(k_ref[...], v_ref[...])

        @pl.when(kk == nk - 1)
        def _():
            l = jnp.sum(l_s[...], axis=1, keepdims=True)
            o_ref[...] = (a_s[...] / l).astype(BF16)
            l_ref[0] = m_s[...] + jnp.log2(l)

    ins = [(qa, pl.BlockSpec((tq, w), qmap)), (ka, pl.BlockSpec((tk, w), kmap)), (va, pl.BlockSpec((tk, vw), vmap))]
    if extra:
        ins += [(ka, pl.BlockSpec((NCTX, w), lambda h, i, kk: (cblk, h // grp))),
                (va, pl.BlockSpec((NCTX, vw), lambda h, i, kk: (cblk, vcol0 + h // grp)))]
    if has_sink:
        ins += [(sink, pl.BlockSpec((1, 1, 128), lambda h, i, kk: (h, 0, 0)))]
    aliases = {}
    if prev is not None:
        any_spec = pl.BlockSpec(memory_space=pl.ANY)
        aliases = {len(ins): 0, len(ins) + 1: 1}
        ins += [(prev[0], any_spec), (prev[1], any_spec)]
    return pl.pallas_call(
        body, out_shape=[_sd((n, hq * vw), BF16), _sd((hq, n, 1), F32)], grid=grid,
        in_specs=[s for _, s in ins],
        out_specs=[pl.BlockSpec((tq, vw), omap), pl.BlockSpec((1, tq, 1), lmap)],
        scratch_shapes=[pltpu.VMEM((tq, 1), F32), pltpu.VMEM((tq, 128), F32), pltpu.VMEM((tq, vw), F32)],
        input_output_aliases=aliases, name=name,
        compiler_params=_cp(("parallel", "parallel", "arbitrary")))(*[a for a, _ in ins])


def flash_dq(name, qa, ka, va, oa, doa, lse, *, w, vw, hq, grp, vcol0, scale, nlat, tq, tk, band, sink, ctx_q, prev=None):
    n = qa.shape[0]
    cblk = nlat // NCTX
    band = band and not ctx_q
    if ctx_q:
        tq = tk = NCTX
        grid = (hq, 1, 1)
        qmap = lambda h, i, kk: (cblk, h)
        kmap = lambda h, i, kk: (cblk, h // grp)
        vmap = lambda h, i, kk: (cblk, vcol0 + h // grp)
        lmap = lambda h, i, kk: (h, cblk, 0)
    else:
        nb = nlat // tk
        grid = (hq, nlat // tq, 3 if band else nb)
        kb_of = (lambda i, kk: jnp.clip(i + kk - 1, 0, nb - 1)) if band else (lambda i, kk: kk)
        qmap = lambda h, i, kk: (i, h)
        kmap = lambda h, i, kk: (kb_of(i, kk), h // grp)
        vmap = lambda h, i, kk: (kb_of(i, kk), vcol0 + h // grp)
        lmap = lambda h, i, kk: (h, i, 0)
    nk = grid[2]
    nq = grid[1]
    extra = not ctx_q
    has_sink = sink is not None

    def body(*refs):
        refs = list(refs)
        q_ref, k_ref, v_ref, o_ref, do_ref, l_ref = refs[:6]
        pos = 6
        if extra:
            ke_ref, ve_ref = refs[pos:pos + 2]
            pos += 2
        if has_sink:
            s_ref = refs[pos]
            pos += 1
        if prev is not None:
            pos += 2
        dq_ref, dl_ref, ds_ref, acc_s, dl_s = refs[pos:pos + 5]
        i = pl.program_id(1)
        kk = pl.program_id(2)
        q = q_ref[...]
        do = do_ref[...]
        lse_v = l_ref[0]

        def step(kblk, vblk, mask):
            s = _d(q, kblk, ((1,), (1,))) * (scale * LOG2E)
            if mask is not None:
                s = jnp.where(mask, s, NEG)
            p = jnp.exp2(s - lse_v)
            dp = _d(do, vblk, ((1,), (1,)))
            ds = p * (dp - dl_s[...]) * scale
            acc_s[...] += _d(ds, kblk, ((1,), (0,)))

        @pl.when(kk == 0)
        def _():
            delta = jnp.sum(do * o_ref[...].astype(F32), axis=1, keepdims=True)
            dl_s[...] = delta
            acc_s[...] = jnp.zeros((tq, w), F32)
            if has_sink:
                sv = jnp.max(s_ref[0], axis=1, keepdims=True) * LOG2E
                dsk = jnp.sum(-jnp.exp2(sv - lse_v) * delta, axis=0, keepdims=True)
                _acc(ds_ref, jnp.zeros((1, 1, 128), F32) + dsk, i == 0)
            else:
                ds_ref[...] = jnp.zeros((1, 1, 128), F32)
            if extra:
                step(ke_ref[...], ve_ref[...], None)

        if band:
            kb = i + kk - 1

            @pl.when((kb >= 0) & (kb < nlat // tk))
            def _():
                step(k_ref[...], v_ref[...], _band_mask(tq, tk, i, kb))
        else:
            step(k_ref[...], v_ref[...], None)

        @pl.when(kk == nk - 1)
        def _():
            dq_ref[...] = acc_s[...]
            dl_ref[0] = dl_s[...]

    ins = [(qa, pl.BlockSpec((tq, w), qmap)), (ka, pl.BlockSpec((tk, w), kmap)), (va, pl.BlockSpec((tk, vw), vmap)),
           (oa, pl.BlockSpec((tq, vw), qmap)), (doa, pl.BlockSpec((tq, vw), qmap)), (lse, pl.BlockSpec((1, tq, 1), lmap))]
    if extra:
        ins += [(ka, pl.BlockSpec((NCTX, w), lambda h, i, kk: (cblk, h // grp))),
                (va, pl.BlockSpec((NCTX, vw), lambda h, i, kk: (cblk, vcol0 + h // grp)))]
    if has_sink:
        ins += [(sink, pl.BlockSpec((1, 1, 128), lambda h, i, kk: (h, 0, 0)))]
    aliases = {}
    if prev is not None:
        any_spec = pl.BlockSpec(memory_space=pl.ANY)
        aliases = {len(ins): 0, len(ins) + 1: 1}
        ins += [(prev[0], any_spec), (prev[1], any_spec)]
    del nq
    return pl.pallas_call(
        body, out_shape=[_sd((n, hq * w), F32), _sd((hq, n, 1), F32), _sd((hq, 1, 128), F32)], grid=grid,
        in_specs=[s for _, s in ins],
        out_specs=[pl.BlockSpec((tq, w), qmap), pl.BlockSpec((1, tq, 1), lmap),
                   pl.BlockSpec((1, 1, 128), lambda h, i, kk: (h, 0, 0))],
        scratch_shapes=[pltpu.VMEM((tq, w), F32), pltpu.VMEM((tq, 1), F32)],
        input_output_aliases=aliases, name=name,
        compiler_params=_cp(("parallel", "arbitrary", "arbitrary")))(*[a for a, _ in ins])


def flash_dkv(name, qa, ka, va, doa, lse, delta, *, w, vw, hkv, grp, vcol0, scale, nlat, tq, tk, band, ctx_k, prev=None):
    n = qa.shape[0]
    cblk = nlat // NCTX
    nqb = nlat // tq
    band = band and not ctx_k
    if ctx_k:
        tk = NCTX
        nqs = nqb
        grid = (hkv, 1, grp * nqs)
        kmap = lambda hk, j, t: (cblk, hk)
        vmap = lambda hk, j, t: (cblk, vcol0 + hk)
        dvmap = lambda hk, j, t: (cblk, hk)
        qb_of = lambda j, t: t % nqs
    else:
        nqs = 3 if band else nqb
        grid = (hkv, nlat // tk, grp * nqs)
        kmap = lambda hk, j, t: (j, hk)
        vmap = lambda hk, j, t: (j, vcol0 + hk)
        dvmap = lambda hk, j, t: (j, hk)
        qb_of = (lambda j, t: jnp.clip(j + t % nqs - 1, 0, nqb - 1)) if band else (lambda j, t: t % nqs)
    qmap = lambda hk, j, t: (qb_of(j, t), hk * grp + t // nqs)
    lmap = lambda hk, j, t: (hk * grp + t // nqs, qb_of(j, t), 0)

    def body(*refs):
        refs = list(refs)
        q_ref, k_ref, v_ref, do_ref, l_ref, dl_ref = refs[:6]
        pos = 6
        if ctx_k:
            qe_ref, doe_ref, le_ref, dle_ref = refs[pos:pos + 4]
            pos += 4
        if prev is not None:
            pos += 2
        dk_ref, dv_ref = refs[pos:pos + 2]
        j = pl.program_id(1)
        t = pl.program_id(2)
        kblk = k_ref[...]
        vblk = v_ref[...]

        def contrib(q, do, lse_v, dl_v, mask):
            s = _d(q, kblk, ((1,), (1,))) * (scale * LOG2E)
            if mask is not None:
                s = jnp.where(mask, s, NEG)
            p = jnp.exp2(s - lse_v)
            dp = _d(do, vblk, ((1,), (1,)))
            ds = p * (dp - dl_v) * scale
            return _d(ds, q, ((0,), (0,))), _d(p, do, ((0,), (0,)))

        @pl.when(t == 0)
        def _():
            dk = jnp.zeros((tk, w), F32)
            dv = jnp.zeros((tk, vw), F32)
            if ctx_k:
                for gi in range(grp):
                    a, b = contrib(qe_ref[:, w * gi:w * gi + w], doe_ref[:, vw * gi:vw * gi + vw], le_ref[gi], dle_ref[gi], None)
                    dk = dk + a
                    dv = dv + b
            dk_ref[...] = dk
            dv_ref[...] = dv

        def add(mask):
            a, b = contrib(q_ref[...], do_ref[...], l_ref[0], dl_ref[0], mask)
            dk_ref[...] += a
            dv_ref[...] += b

        if band:
            qb = j + t % nqs - 1

            @pl.when((qb >= 0) & (qb < nqb))
            def _():
                add(_band_mask(tq, tk, qb, j))
        else:
            add(None)

    ins = [(qa, pl.BlockSpec((tq, w), qmap)), (ka, pl.BlockSpec((tk, w), kmap)), (va, pl.BlockSpec((tk, vw), vmap)),
           (doa, pl.BlockSpec((tq, vw), qmap)), (lse, pl.BlockSpec((1, tq, 1), lmap)), (delta, pl.BlockSpec((1, tq, 1), lmap))]
    if ctx_k:
        ins += [(qa, pl.BlockSpec((NCTX, grp * w), lambda hk, j, t: (cblk, hk))),
                (doa, pl.BlockSpec((NCTX, grp * vw), lambda hk, j, t: (cblk, hk))),
                (lse, pl.BlockSpec((grp, NCTX, 1), lambda hk, j, t: (hk, cblk, 0))),
                (delta, pl.BlockSpec((grp, NCTX, 1), lambda hk, j, t: (hk, cblk, 0)))]
    aliases = {}
    if prev is not None:
        any_spec = pl.BlockSpec(memory_space=pl.ANY)
        aliases = {len(ins): 0, len(ins) + 1: 1}
        ins += [(prev[0], any_spec), (prev[1], any_spec)]
    return pl.pallas_call(
        body, out_shape=[_sd((n, hkv * w), F32), _sd((n, hkv * vw), F32)], grid=grid,
        in_specs=[s for _, s in ins],
        out_specs=[pl.BlockSpec((tk, w), kmap), pl.BlockSpec((tk, vw), dvmap)],
        input_output_aliases=aliases, name=name,
        compiler_params=_cp(("parallel", "parallel", "arbitrary")))(*[a for a, _ in ins])


def mla_fwd(name, qm, km, vm, nlat):
    n = qm.shape[0]
    t = NCTX
    nlt = nlat // t
    c = (MLA_NOPE + MLA_ROPE) ** -0.5 * LOG2E

    def body(q_ref, k_ref, v_ref, o_ref, l_ref):
        i = pl.program_id(1)

        def run(k, v):
            s = _d(q_ref[...], k, ((1,), (1,))) * c
            m = jnp.max(_fold_lanes(s, jnp.maximum), axis=1, keepdims=True)
            p = jnp.exp2(s - m)
            l = jnp.sum(_fold_lanes(p, jnp.add), axis=1, keepdims=True)
            o_ref[...] = (_d(p, v, ((1,), (0,))) / l).astype(BF16)
            l_ref[0] = m + jnp.log2(l)

        @pl.when(i < nlt)
        def _():
            run(k_ref[...], v_ref[...])

        @pl.when(i == nlt)
        def _():
            run(k_ref[nlat:n, :], v_ref[nlat:n, :])

    return pl.pallas_call(
        body, out_shape=[_sd((n, MLA_H * 128), BF16), _sd((MLA_H, n, 1), F32)], grid=(MLA_H, n // t),
        in_specs=[pl.BlockSpec((t, 256), lambda h, i: (i, h)), pl.BlockSpec((n, 256), lambda h, i: (0, h)),
                  pl.BlockSpec((n, 128), lambda h, i: (0, h))],
        out_specs=[pl.BlockSpec((t, 128), lambda h, i: (i, h)), pl.BlockSpec((1, t, 1), lambda h, i: (h, i, 0))],
        name=name, compiler_params=_cp(("parallel", "arbitrary")))(qm, km, vm)


def mla_bwd(name, qm, km, vm, o, do, lse, nlat):
    n = qm.shape[0]
    t = NCTX
    nlt = nlat // t
    scale = (MLA_NOPE + MLA_ROPE) ** -0.5
    nchunk = 2 if (n // 128) % 2 == 0 else 1
    cw = n // nchunk

    def body(q_ref, k_ref, v_ref, o_ref, do_ref, l_ref, dq_ref, dkt_ref, dvt_ref):
        i = pl.program_id(1)

        @pl.when(i == 0)
        def _():
            dkt_ref[...] = jnp.zeros((256, n), F32)
            dvt_ref[...] = jnp.zeros((128, n), F32)

        q = q_ref[...]
        do = do_ref[...]
        delta = jnp.sum(do.astype(F32) * o_ref[...].astype(F32), axis=1, keepdims=True)

        def run(spans):
            dq = jnp.zeros((t, 256), F32)
            for a, b in spans:
                kc = k_ref[a:b, :]
                s = _d(q, kc, ((1,), (1,))) * (scale * LOG2E)
                p = jnp.exp2(s - l_ref[0])
                ds = (p * (_d(do, v_ref[a:b, :], ((1,), (1,))) - delta) * scale).astype(BF16)
                dq = dq + _d(ds, kc, ((1,), (0,)))
                dkt_ref[:, a:b] += _d(q, ds, ((0,), (0,)))
                dvt_ref[:, a:b] += _d(do, p, ((0,), (0,)))
            dq_ref[...] = dq

        @pl.when(i < nlt)
        def _():
            run([(c * cw, (c + 1) * cw) for c in range(nchunk)])

        @pl.when(i == nlt)
        def _():
            run([(nlat, n)])

    qspec = pl.BlockSpec((t, 256), lambda h, i: (i, h))
    ospec = pl.BlockSpec((t, 128), lambda h, i: (i, h))
    return pl.pallas_call(
        body, out_shape=[_sd((n, MLA_H * 256), F32), _sd((MLA_H * 256, n), F32), _sd((MLA_H * 128, n), F32)],
        grid=(MLA_H, n // t),
        in_specs=[qspec, pl.BlockSpec((n, 256), lambda h, i: (0, h)), pl.BlockSpec((n, 128), lambda h, i: (0, h)),
                  ospec, ospec, pl.BlockSpec((1, t, 1), lambda h, i: (h, i, 0))],
        out_specs=[qspec, pl.BlockSpec((256, n), lambda h, i: (h, 0)), pl.BlockSpec((128, n), lambda h, i: (h, 0))],
        name=name, compiler_params=_cp(("parallel", "arbitrary")))(qm, km, vm, o, do, lse)


def mla_attention_bwd(tag, qm, km, vm, o, do, lse, nlat):
    dq, dkt, dvt = mla_bwd(tag + "_bwd", qm, km, vm, o, do, lse, nlat)
    return dq, jnp.transpose(dkt), jnp.transpose(dvt)


SWA_T = 512


def _swa_window(t, nlat):
    t = min(t, nlat)
    return t, min(t + 2 * SWA_WIN, nlat)


def _win_start(i, t, wlen, nlat):
    return pl.multiple_of(jnp.clip(i * t - SWA_WIN, 0, nlat - wlen), 128)


def _win_mask(rows, cols, row0, col0):
    rp = row0 + lax.broadcasted_iota(jnp.int32, (rows, cols), 0)
    cp = col0 + lax.broadcasted_iota(jnp.int32, (rows, cols), 1)
    return jnp.abs(rp - cp) <= SWA_WIN


def swa_fwd_lat(name, qs, ks, u, sink, nlat):
    n = qs.shape[0]
    tq, wlen = _swa_window(SWA_T, nlat)
    grp = SWA_HQ // SWA_HKV
    scale = SWA_DH ** -0.5
    vcol0 = C_V // 128

    def body(q_ref, k_ref, v_ref, s_ref, o_ref, l_ref):
        i = pl.program_id(1)
        ws = _win_start(i, tq, wlen, nlat)
        q = q_ref[...]
        s1 = _d(q, k_ref[pl.ds(ws, wlen), :], ((1,), (1,))) * (scale * LOG2E)
        s1 = jnp.where(_win_mask(tq, wlen, i * tq, ws), s1, NEG)
        s2 = _d(q, k_ref[pl.ds(nlat, NCTX), :], ((1,), (1,))) * (scale * LOG2E)
        sv = jnp.max(s_ref[0], axis=1, keepdims=True) * LOG2E
        m = jnp.maximum(jnp.maximum(jnp.max(s1, axis=1, keepdims=True), jnp.max(s2, axis=1, keepdims=True)), sv)
        p1 = jnp.exp2(s1 - m)
        p2 = jnp.exp2(s2 - m)
        l = jnp.sum(p1, axis=1, keepdims=True) + jnp.sum(p2, axis=1, keepdims=True) + jnp.exp2(sv - m)
        acc = _d(p1, v_ref[pl.ds(ws, wlen), :], ((1,), (0,))) + _d(p2, v_ref[pl.ds(nlat, NCTX), :], ((1,), (0,)))
        o_ref[...] = (acc / l).astype(BF16)
        l_ref[0] = m + jnp.log2(l)

    return pl.pallas_call(
        body, out_shape=[_sd((n, SWA_HQ * 128), BF16), _sd((SWA_HQ, n, 1), F32)], grid=(SWA_HQ, nlat // tq),
        in_specs=[pl.BlockSpec((tq, 128), lambda h, i: (i, h)), pl.BlockSpec((n, 128), lambda h, i: (0, h // grp)),
                  pl.BlockSpec((n, 128), lambda h, i: (0, vcol0 + h // grp)), pl.BlockSpec((1, 1, 128), lambda h, i: (h, 0, 0))],
        out_specs=[pl.BlockSpec((tq, 128), lambda h, i: (i, h)), pl.BlockSpec((1, tq, 1), lambda h, i: (h, i, 0))],
        name=name, compiler_params=_cp(("parallel", "arbitrary")))(qs, ks, u, sink)


def swa_dq_lat(name, qs, ks, u, o, do, lse, sink, nlat):
    n = qs.shape[0]
    tq, wlen = _swa_window(SWA_T, nlat)
    grp = SWA_HQ // SWA_HKV
    scale = SWA_DH ** -0.5
    vcol0 = C_V // 128

    def body(q_ref, k_ref, v_ref, s_ref, o_ref, do_ref, l_ref, dq_ref, dl_ref, ds_ref):
        i = pl.program_id(1)
        ws = _win_start(i, tq, wlen, nlat)
        q = q_ref[...]
        do = do_ref[...]
        lse_v = l_ref[0]
        delta = jnp.sum(do.astype(F32) * o_ref[...].astype(F32), axis=1, keepdims=True)
        kw = k_ref[pl.ds(ws, wlen), :]
        kc = k_ref[pl.ds(nlat, NCTX), :]
        s1 = _d(q, kw, ((1,), (1,))) * (scale * LOG2E)
        s1 = jnp.where(_win_mask(tq, wlen, i * tq, ws), s1, NEG)
        s2 = _d(q, kc, ((1,), (1,))) * (scale * LOG2E)
        ds1 = jnp.exp2(s1 - lse_v) * (_d(do, v_ref[pl.ds(ws, wlen), :], ((1,), (1,))) - delta) * scale
        ds2 = jnp.exp2(s2 - lse_v) * (_d(do, v_ref[pl.ds(nlat, NCTX), :], ((1,), (1,))) - delta) * scale
        dq_ref[...] = _d(ds1, kw, ((1,), (0,))) + _d(ds2, kc, ((1,), (0,)))
        dl_ref[0] = delta
        sv = jnp.max(s_ref[0], axis=1, keepdims=True) * LOG2E
        dsk = jnp.sum(-jnp.exp2(sv - lse_v) * delta, axis=0, keepdims=True)
        _acc(ds_ref, jnp.zeros((1, 1, 128), F32) + dsk, i == 0)

    qspec = pl.BlockSpec((tq, 128), lambda h, i: (i, h))
    lspec = pl.BlockSpec((1, tq, 1), lambda h, i: (h, i, 0))
    return pl.pallas_call(
        body, out_shape=[_sd((n, SWA_HQ * 128), F32), _sd((SWA_HQ, n, 1), F32), _sd((SWA_HQ, 1, 128), F32)],
        grid=(SWA_HQ, nlat // tq),
        in_specs=[qspec, pl.BlockSpec((n, 128), lambda h, i: (0, h // grp)),
                  pl.BlockSpec((n, 128), lambda h, i: (0, vcol0 + h // grp)), pl.BlockSpec((1, 1, 128), lambda h, i: (h, 0, 0)),
                  qspec, qspec, lspec],
        out_specs=[qspec, lspec, pl.BlockSpec((1, 1, 128), lambda h, i: (h, 0, 0))],
        name=name, compiler_params=_cp(("parallel", "arbitrary")))(qs, ks, u, sink, o, do, lse)


def swa_dkv_lat(name, qs, ks, u, do, lse_row, delta_row, nlat):
    n = qs.shape[0]
    tk, wlen = _swa_window(SWA_T, nlat)
    grp = SWA_HQ // SWA_HKV
    scale = SWA_DH ** -0.5
    vcol0 = C_V // 128

    def body(q_ref, k_ref, v_ref, do_ref, l_ref, dl_ref, dk_ref, dv_ref):
        j = pl.program_id(1)
        ws = _win_start(j, tk, wlen, nlat)
        k = k_ref[...]
        v = v_ref[...]
        mask = _win_mask(tk, wlen, j * tk, ws)
        dk = jnp.zeros((tk, 128), F32)
        dv = jnp.zeros((tk, 128), F32)
        for gi in range(grp):
            qw = q_ref[pl.ds(ws, wlen), 128 * gi:128 * gi + 128]
            dow = do_ref[pl.ds(ws, wlen), 128 * gi:128 * gi + 128]
            st = jnp.where(mask, _d(k, qw, ((1,), (1,))) * (scale * LOG2E), NEG)
            pt = jnp.exp2(st - l_ref[gi, :, pl.ds(ws, wlen)])
            dv = dv + _d(pt, dow, ((1,), (0,)))
            dst = pt * (_d(v, dow, ((1,), (1,))) - dl_ref[gi, :, pl.ds(ws, wlen)]) * scale
            dk = dk + _d(dst, qw, ((1,), (0,)))
        dk_ref[...] = dk
        dv_ref[...] = dv

    rspec = pl.BlockSpec((grp, 1, n), lambda hk, j: (hk, 0, 0))
    return pl.pallas_call(
        body, out_shape=[_sd((n, SWA_HKV * 128), F32), _sd((n, SWA_HKV * 128), F32)], grid=(SWA_HKV, nlat // tk),
        in_specs=[pl.BlockSpec((n, grp * 128), lambda hk, j: (0, hk)), pl.BlockSpec((tk, 128), lambda hk, j: (j, hk)),
                  pl.BlockSpec((tk, 128), lambda hk, j: (j, vcol0 + hk)), pl.BlockSpec((n, grp * 128), lambda hk, j: (0, hk)),
                  rspec, rspec],
        out_specs=[pl.BlockSpec((tk, 128), lambda hk, j: (j, hk)), pl.BlockSpec((tk, 128), lambda hk, j: (j, hk))],
        name=name, compiler_params=_cp(("parallel", "arbitrary")))(qs, ks, u, do, lse_row, delta_row)


def swa_attention_fwd(tag, qs, ks, u, sink, cfg, nlat):
    o, lse = swa_fwd_lat(tag + "_fwd_lat", qs, ks, u, sink, nlat)
    return flash_fwd(tag + "_fwd_ctx", qs, ks, u, sink=sink, ctx_q=True, nlat=nlat, prev=(o, lse), **cfg)


def swa_attention_bwd(tag, qs, ks, u, o, do, lse, sink, cfg, nlat):
    n = qs.shape[0]
    dq, delta, ds1 = swa_dq_lat(tag + "_dq_lat", qs, ks, u, o, do, lse, sink, nlat)
    dq, delta, ds2 = flash_dq(tag + "_dq_ctx", qs, ks, u, o, do, lse, sink=sink, ctx_q=True, nlat=nlat, prev=(dq, delta), **cfg)
    dk, dv = swa_dkv_lat(tag + "_dkv_lat", qs, ks, u, do, lse.reshape(SWA_HQ, 1, n), delta.reshape(SWA_HQ, 1, n), nlat)
    kc = {k: v for k, v in cfg.items() if k != "hq"}
    kc["hkv"] = SWA_HKV
    kc["tq"] = min(1024, nlat)
    dk, dv = flash_dkv(tag + "_dkv_ctx", qs, ks, u, do, lse, delta, ctx_k=True, nlat=nlat, prev=(dk, dv), **kc)
    return dq, dk, dv, ds1 + ds2


def adamw(name, w, g, m, v):
    r, c = w.shape
    tr = _pick(r, (256, 128, 64, 32, 16, 8))
    bc1 = 1.0 - ADAM_B1 ** ADAM_STEP
    bc2 = 1.0 - ADAM_B2 ** ADAM_STEP

    def body(w_ref, g_ref, m_ref, v_ref, d_ref, nm_ref, nv_ref):
        gv = g_ref[...]
        nm = ADAM_B1 * m_ref[...] + (1.0 - ADAM_B1) * gv
        nv = ADAM_B2 * v_ref[...] + (1.0 - ADAM_B2) * (gv * gv)
        d_ref[...] = -ADAM_LR * ((nm / bc1) / (jnp.sqrt(nv / bc2) + ADAM_EPS) + ADAM_WD * w_ref[...])
        nm_ref[...] = nm
        nv_ref[...] = nv

    spec = pl.BlockSpec((tr, c), lambda i: (i, 0))
    return pl.pallas_call(body, out_shape=[_sd((r, c), F32)] * 3, grid=(r // tr,), in_specs=[spec] * 4, out_specs=[spec] * 3,
                          name=name, compiler_params=_cp(("parallel",)))(w, g, m, v)


def _coords():
    return lax.axis_index("x"), lax.axis_index("y"), lax.axis_index("c")


_ANY = pl.BlockSpec(memory_space=pl.ANY)


def _chip():
    return 2 * lax.axis_index("x") + lax.axis_index("y")


def _per_core(fn):
    c = lax.axis_index("c")
    for cs in (0, 1):
        pl.when(c == cs)(functools.partial(fn, cs))


def gather_chips(name, a):
    r = a.shape[0]
    half = r // 2

    def body(a_ref, o_ref, ici_send, ici_recv, d2d_send, d2d_recv):
        _per_core(functools.partial(run, a_ref, o_ref, ici_send, ici_recv, d2d_send, d2d_recv))

    def run(a_ref, o_ref, ici_send, ici_recv, d2d_send, d2d_recv, c):
        x, y, _ = _coords()
        me = 2 * x + y
        peers = [(1 - x, y), (x, 1 - y), (1 - x, 1 - y)]
        my_rows = pl.ds(c * half, half)
        sib_rows = pl.ds((1 - c) * half, half)
        sends = [pltpu.make_async_remote_copy(a_ref.at[my_rows], o_ref.at[me, my_rows], ici_send.at[k], ici_recv.at[k],
                                              device_id=(px, py, c), device_id_type=MESH)
                 for k, (px, py) in enumerate(peers)]
        for cp in sends:
            cp.start()
        passed = []
        for k, (px, py) in enumerate(peers):
            s = 2 * px + py
            pltpu.make_async_remote_copy(a_ref.at[my_rows], o_ref.at[s, my_rows], ici_send.at[k], ici_recv.at[k],
                                         device_id=(px, py, c), device_id_type=MESH).wait_recv()
            fw = pltpu.make_async_remote_copy(o_ref.at[s, my_rows], o_ref.at[s, my_rows], d2d_send.at[k], d2d_recv.at[k],
                                              device_id=(x, y, 1 - c), device_id_type=MESH)
            fw.start()
            passed.append(fw)
        for k, (px, py) in enumerate(peers):
            s = 2 * px + py
            pltpu.make_async_remote_copy(o_ref.at[s, sib_rows], o_ref.at[s, sib_rows], d2d_send.at[k], d2d_recv.at[k],
                                         device_id=(x, y, 1 - c), device_id_type=MESH).wait_recv()
        for cp in sends + passed:
            cp.wait_send()

    out = pl.pallas_call(
        body, out_shape=_sd((4,) + a.shape, a.dtype), in_specs=[_ANY], out_specs=_ANY,
        scratch_shapes=[pltpu.SemaphoreType.DMA((3,)), pltpu.SemaphoreType.DMA((3,)), pltpu.SemaphoreType.DMA((3,)),
                        pltpu.SemaphoreType.DMA((3,))],
        name=name, compiler_params=pltpu.CompilerParams(has_side_effects=True))(a)
    return lax.dynamic_update_index_in_dim(out, a, _chip(), 0)


def pair_split(name, a):
    k4, r, cdim = a.shape
    half = r // 2

    def body(a_ref, got_ref, send_sem, recv_sem):
        _per_core(functools.partial(run, a_ref, got_ref, send_sem, recv_sem))

    def run(a_ref, got_ref, send_sem, recv_sem, c):
        x, y, _ = _coords()
        sib_rows = pl.ds((1 - c) * half, half)
        cp = pltpu.make_async_remote_copy(a_ref.at[:, sib_rows], got_ref, send_sem, recv_sem,
                                          device_id=(x, y, 1 - c), device_id_type=MESH)
        cp.start()
        cp.wait()

    got = pl.pallas_call(
        body, out_shape=_sd((k4, half, cdim), a.dtype), in_specs=[_ANY], out_specs=_ANY,
        scratch_shapes=[pltpu.SemaphoreType.DMA, pltpu.SemaphoreType.DMA],
        name=name, compiler_params=pltpu.CompilerParams(has_side_effects=True))(a)
    return lax.dynamic_slice_in_dim(a, lax.axis_index("c") * half, half, axis=1), got


def scatter_chips(name, a):
    def body(a_ref, o_ref, send_sems, recv_sems):
        x, y, c = _coords()
        me = 2 * x + y
        peers = [(1 - x, y), (x, 1 - y), (1 - x, 1 - y)]
        sends = [pltpu.make_async_remote_copy(a_ref.at[2 * px + py], o_ref.at[me], send_sems.at[k], recv_sems.at[k],
                                              device_id=(px, py, c), device_id_type=MESH)
                 for k, (px, py) in enumerate(peers)]
        for cp in sends:
            cp.start()
        for k, (px, py) in enumerate(peers):
            pltpu.make_async_remote_copy(a_ref.at[me], o_ref.at[2 * px + py], send_sems.at[k], recv_sems.at[k],
                                         device_id=(px, py, c), device_id_type=MESH).wait_recv()
        for cp in sends:
            cp.wait_send()

    out = pl.pallas_call(
        body, out_shape=_sd(a.shape, a.dtype), in_specs=[_ANY], out_specs=_ANY,
        scratch_shapes=[pltpu.SemaphoreType.DMA((3,)), pltpu.SemaphoreType.DMA((3,))],
        name=name, compiler_params=pltpu.CompilerParams(has_side_effects=True))(a)
    return lax.dynamic_update_index_in_dim(out, lax.dynamic_index_in_dim(a, _chip(), 0, keepdims=False), _chip(), 0)


def pair_join(name, a):
    half, cdim = a.shape

    def body(a_ref, o_ref, send_sem, recv_sem):
        _per_core(functools.partial(run, a_ref, o_ref, send_sem, recv_sem))

    def run(a_ref, o_ref, send_sem, recv_sem, c):
        x, y, _ = _coords()
        my_rows = pl.ds(c * half, half)
        sib_rows = pl.ds((1 - c) * half, half)
        cp = pltpu.make_async_remote_copy(a_ref, o_ref.at[my_rows], send_sem, recv_sem, device_id=(x, y, 1 - c),
                                          device_id_type=MESH)
        cp.start()
        cp.wait_send()
        pltpu.make_async_remote_copy(a_ref, o_ref.at[sib_rows], send_sem, recv_sem, device_id=(x, y, 1 - c),
                                     device_id_type=MESH).wait_recv()

    out = pl.pallas_call(
        body, out_shape=_sd((2 * half, cdim), a.dtype), in_specs=[_ANY], out_specs=_ANY,
        scratch_shapes=[pltpu.SemaphoreType.DMA, pltpu.SemaphoreType.DMA],
        name=name, compiler_params=pltpu.CompilerParams(has_side_effects=True))(a)
    return lax.dynamic_update_slice_in_dim(out, a, lax.axis_index("c") * half, axis=0)


def add_cast(name, a, b, dtype):
    k, r, c = a.shape
    tr = _pick(r, (1024, 512, 256, 128, 64, 32, 16, 8))

    def body(a_ref, b_ref, o_ref):
        o_ref[...] = (a_ref[...].astype(F32) + b_ref[...].astype(F32)).astype(dtype)

    spec = pl.BlockSpec((1, tr, c), lambda s, i: (s, i, 0))
    return pl.pallas_call(body, out_shape=_sd((k, r, c), dtype), grid=(k, r // tr), in_specs=[spec, spec], out_specs=spec,
                          name=name, compiler_params=_cp(("parallel", "parallel")))(a, b)


def gather_all(name, a):
    def body(a_ref, o_ref, send_sems, recv_sems, loc_sem):
        x, y, c = _coords()
        me = 4 * x + 2 * y + c
        flips = [(fx, fy, fc) for fx in (0, 1) for fy in (0, 1) for fc in (0, 1) if fx + fy + fc > 0]
        peers = [(x ^ fx, y ^ fy, c ^ fc) for fx, fy, fc in flips]
        mine = pltpu.make_async_copy(a_ref, o_ref.at[me], loc_sem)
        mine.start()
        sends = [pltpu.make_async_remote_copy(a_ref, o_ref.at[me], send_sems.at[k], recv_sems.at[k],
                                              device_id=p, device_id_type=MESH) for k, p in enumerate(peers)]
        for cp in sends:
            cp.start()
        for k, (px, py, pc) in enumerate(peers):
            pltpu.make_async_remote_copy(a_ref, o_ref.at[4 * px + 2 * py + pc], send_sems.at[k], recv_sems.at[k],
                                         device_id=(px, py, pc), device_id_type=MESH).wait_recv()
        for cp in sends:
            cp.wait_send()
        mine.wait()

    return pl.pallas_call(
        body, out_shape=_sd((8,) + a.shape, a.dtype), in_specs=[_ANY], out_specs=_ANY,
        scratch_shapes=[pltpu.SemaphoreType.DMA((7,)), pltpu.SemaphoreType.DMA((7,)), pltpu.SemaphoreType.DMA],
        name=name, compiler_params=pltpu.CompilerParams(has_side_effects=True))(a)


def sum_blocks(name, a):
    k, r, c = a.shape
    tr = _pick(r, (1024, 256, 128, 64, 32, 16, 8))

    def body(a_ref, o_ref):
        acc = a_ref[0].astype(F32)
        for s in range(1, k):
            acc = acc + a_ref[s].astype(F32)
        o_ref[...] = acc

    return pl.pallas_call(body, out_shape=_sd((r, c), F32), grid=(r // tr,),
                          in_specs=[pl.BlockSpec((k, tr, c), lambda i: (0, i, 0))], out_specs=pl.BlockSpec((tr, c), lambda i: (i, 0)),
                          name=name, compiler_params=_cp(("parallel",)))(a)


BIG = ("w_mod", "w_in", "w_mla_uq", "w_mla_ukv", "w_p_ssm", "w_p_swa", "w_p_mla", "w_out", "w_ffn_in", "w_ffn_out")
COL_SHARDED = ("w_mod", "w_in", "w_mla_uq", "w_mla_ukv", "w_ffn_in")
SMALL = ("c_ctx", "b_mod", "norm1_g", "norm2_g", "ssm_conv_w", "ssm_conv_b", "ssm_dt_bias", "ssm_a_log", "ssm_d",
         "ssm_norm_g", "swa_q_norm_g", "swa_k_norm_g", "swa_sink", "mla_q_lat_g", "mla_kv_lat_g", "mla_q_norm_g",
         "mla_k_norm_g")
WEIGHTS = ("c_ctx", "w_mod", "b_mod", "norm1_g", "norm2_g", "w_in", "ssm_conv_w", "ssm_conv_b", "ssm_dt_bias", "ssm_a_log",
           "ssm_d", "ssm_norm_g", "swa_q_norm_g", "swa_k_norm_g", "swa_sink", "mla_q_lat_g", "mla_kv_lat_g", "w_mla_uq",
           "w_mla_ukv", "mla_q_norm_g", "mla_k_norm_g", "w_p_ssm", "w_p_swa", "w_p_mla", "w_out", "w_ffn_in", "w_ffn_out")


def pack_w_in(w):
    z = lambda k: jnp.zeros((w.shape[0], k), w.dtype)
    return jnp.concatenate([w[:, 4832:7904], w[:, 2400:3424], w[:, 3424:4448], w[:, 0:1536], w[:, 1568:1824], w[:, 1824:2080],
                            w[:, 2080:2336], w[:, 2336:2400], w[:, 1536:1568], z(32), z(128), w[:, 4448:4832]], axis=1)


def unpack_w_in(g):
    return jnp.concatenate([g[:, 5120:6656], g[:, 7488:7520], g[:, 6656:6912], g[:, 6912:7168], g[:, 7168:7424], g[:, 7424:7488],
                            g[:, 3072:4096], g[:, 4096:5120], g[:, 7680:8064], g[:, 0:3072]], axis=1)


def pack_ukv(w):
    return w.reshape(MLA_KVRANK, MLA_H, 2, 128).transpose(0, 2, 1, 3).reshape(MLA_KVRANK, 2048)


def unpack_ukv(g):
    return g.reshape(MLA_KVRANK, 2, MLA_H, 128).transpose(0, 2, 1, 3).reshape(MLA_KVRANK, 2048)


def pack_uq(w):
    return jnp.pad(w.reshape(MLA_QRANK, MLA_H, 192), ((0, 0), (0, 0), (0, 64))).reshape(MLA_QRANK, 2048)


def unpack_uq(g):
    return g.reshape(MLA_QRANK, MLA_H, 256)[:, :, :192].reshape(MLA_QRANK, 1536)


def rope_tables(nlat):
    t = jnp.arange(nlat, dtype=jnp.int32)
    r = (t // GRID_W).astype(F32)[:, None]
    col = (t % GRID_W).astype(F32)[:, None]

    def tab(nf, pad):
        inv = jnp.power(ROPE_BASE, -jnp.arange(nf, dtype=F32) / nf)
        ar, ac = r * inv, col * inv
        cos = jnp.concatenate([jnp.cos(ar), jnp.cos(ar), jnp.cos(ac), jnp.cos(ac), jnp.ones((nlat, pad), F32)], axis=1)
        sin = jnp.concatenate([-jnp.sin(ar), jnp.sin(ar), -jnp.sin(ac), jnp.sin(ac), jnp.zeros((nlat, pad), F32)], axis=1)
        cos = jnp.concatenate([cos, jnp.ones((NCTX, 128), F32)], axis=0)
        sin = jnp.concatenate([sin, jnp.zeros((NCTX, 128), F32)], axis=0)
        return cos, sin

    return tab(32, 0), tab(16, 64)


def _lanes(v, start, width=128):
    return jnp.zeros((1, width), F32).at[0, start:start + v.shape[0]].set(v)


def layer_fwd(i, xin, h, mod, p, tabs, nlat):
    t = "l%d_" % i
    n = xin.shape[0]
    (cos_s, sin_s), (cos_m, sin_m) = tabs
    u = mm(h, p["w_in"], F32, t + "in_proj")
    xbc = conv_fwd(t + "conv", u, p["conv_w"], p["conv_b"], nlat)
    dtrow = jnp.transpose(u[:, C_MISC + DT_LANE:C_MISC + DT_LANE + 32])
    nlc = nlat // Q
    yf, hs_f = ssd_fwd(t + "ssd_f", xbc, u, dtrow, p["bias_c"], p["alog_c"], p["bias_r"], p["alog_r"], nlc, False, 0)
    yb, hs_b = ssd_fwd(t + "ssd_b", xbc, u, dtrow, p["bias_c"], p["alog_c"], p["bias_r"], p["alog_r"], nlc, True, 1)
    ys = ssd_out_fwd(t + "ssd_out", yf, yb, xbc, u, p["ssm_norm_g"], p["d_exp"])
    qs, ks = swa_prep_fwd(t + "swa_prep", u, p["swa_q_g"], p["swa_k_g"], cos_s, sin_s)
    o_swa, lse_swa = swa_attention_fwd(t + "swa", qs, ks, u, p["sink"], p["swa_cfg"], nlat)
    ckv_n, cq_n = lat_norm_fwd(t + "lat_norm", u, p["kv_lat_g"], p["q_lat_g"])
    kv = mm(ckv_n, p["w_ukv"], F32, t + "ukv")
    qp = mm(cq_n, p["w_uq"], F32, t + "uq")
    km, qm, vm = mla_prep_fwd(t + "mla_prep", kv, qp, u, p["mla_q_g"], p["mla_k_g"], cos_m, sin_m)
    o_mla, lse_mla = mla_fwd(t + "mla_fwd", qm, km, vm, nlat)
    p1 = mm(ys, p["w_p_ssm"], BF16, t + "p_ssm")
    p2 = mm(o_swa, p["w_p_swa"], BF16, t + "p_swa")
    p3 = mm(o_mla, p["w_p_mla"], BF16, t + "p_mla")
    merged = merge_fwd(t + "merge", u, p1, p2, p3)
    o = mm(merged, p["w_out"], F32, t + "out_proj")
    x1, h2 = resid_mod_fwd(t + "res1", xin, o, mod, 2, mod, 3, 4, p["norm2_g"], nlat // RT)
    gu = mm(h2, p["w_ffn_in"], BF16, t + "ffn_in")
    a = swiglu_fwd(t + "swiglu", gu)
    f = mm(a, p["w_ffn_out"], F32, t + "ffn_out")
    saved = dict(xin=xin, h=h, u=u, xbc=xbc, dtrow=dtrow, yf=yf, yb=yb, hs_f=hs_f, hs_b=hs_b, ys=ys, qs=qs, ks=ks,
                 o_swa=o_swa, lse_swa=lse_swa, ckv_n=ckv_n, cq_n=cq_n, kv=kv, qp=qp, km=km, qm=qm, vm=vm, o_mla=o_mla,
                 lse_mla=lse_mla, p1=p1, p2=p2, p3=p3, merged=merged, o=o, x1=x1, h2=h2, gu=gu, a=a, f=f)
    del n
    return x1, f, saved


def layer_bwd(i, dx2, df, dgt2, sv, mod, p, tabs, nlat):
    t = "l%db_" % i
    (cos_s, sin_s), (cos_m, sin_m) = tabs
    g = {}
    nt = nlat // RT
    nlc = nlat // Q
    g["w_ffn_out"] = mm_tn(sv["a"], df, t + "wg_ffn_out")
    da = mm(df, p["w_ffn_out"], F32, t + "dg_ffn_out", trans_b=True)
    dgu = swiglu_bwd(t + "swiglu", sv["gu"], da)
    g["w_ffn_in"] = mm_tn(sv["h2"], dgu, t + "wg_ffn_in")
    dh2 = mm(dgu, p["w_ffn_in"], F32, t + "dg_ffn_in", trans_b=True)
    dx1, do, dgt1, dsh2, dsc2, g["norm2_g"] = resid_mod_bwd(t + "res1", sv["x1"], dx2, dh2, sv["o"], mod, 2, mod, 3, 4,
                                                              p["norm2_g"], nt)
    g["w_out"] = mm_tn(sv["merged"], do, t + "wg_out")
    dmerged = mm(do, p["w_out"], F32, t + "dg_out", trans_b=True)
    dp1, dp2, dp3, dgates = merge_bwd(t + "merge", sv["u"], sv["p1"], sv["p2"], sv["p3"], dmerged)
    g["w_p_ssm"] = mm_tn(sv["ys"], dp1, t + "wg_p_ssm")
    g["w_p_swa"] = mm_tn(sv["o_swa"], dp2, t + "wg_p_swa")
    g["w_p_mla"] = mm_tn(sv["o_mla"], dp3, t + "wg_p_mla")
    dys = mm(dp1, p["w_p_ssm"], F32, t + "dg_p_ssm", trans_b=True)
    do_swa = mm(dp2, p["w_p_swa"], BF16, t + "dg_p_swa", trans_b=True)
    do_mla = mm(dp3, p["w_p_mla"], BF16, t + "dg_p_mla", trans_b=True)
    dqm, dkm, dv_mla = mla_attention_bwd(t + "mla", sv["qm"], sv["km"], sv["vm"], sv["o_mla"], do_mla, sv["lse_mla"], nlat)
    dkv, dqp, dkr, g["mla_q_g"], g["mla_k_g"] = mla_prep_bwd(t + "mla_prep", sv["kv"], sv["qp"], sv["u"], p["mla_q_g"],
                                                             p["mla_k_g"], cos_m, sin_m, dkm, dqm, dv_mla)
    g["w_ukv"] = mm_tn(sv["ckv_n"], dkv, t + "wg_ukv")
    g["w_uq"] = mm_tn(sv["cq_n"], dqp, t + "wg_uq")
    dckv_n = mm(dkv, p["w_ukv"], F32, t + "dg_ukv", trans_b=True)
    dcq_n = mm(dqp, p["w_uq"], F32, t + "dg_uq", trans_b=True)
    dckv, dcq, g["kv_lat_g"], g["q_lat_g"] = lat_norm_bwd(t + "lat_norm", sv["u"], p["kv_lat_g"], p["q_lat_g"], dckv_n, dcq_n)
    dqs, dks, dv_swa, g["sink"] = swa_attention_bwd(t + "swa", sv["qs"], sv["ks"], sv["u"], sv["o_swa"], do_swa, sv["lse_swa"],
                                                p["sink"], p["swa_cfg"], nlat)
    dq, dk, dv, g["swa_q_g"], g["swa_k_g"] = swa_prep_bwd(t + "swa_prep", sv["u"], p["swa_q_g"], p["swa_k_g"], cos_s, sin_s,
                                                          dqs, dks, dv_swa)
    dy, dxs_skip, dz, g["ssm_norm_g"], g["d_exp"] = ssd_out_bwd(t + "ssd_out", sv["yf"], sv["yb"], sv["xbc"], sv["u"],
                                                                 p["ssm_norm_g"], p["d_exp"], dys)
    n = dy.shape[0]
    zbc = jnp.zeros((n, 256), F32)
    r_f = ssd_bwd(t + "ssd_f", sv["xbc"], sv["u"], sv["dtrow"], p["bias_c"], p["alog_c"], p["bias_r"], p["alog_r"],
                  sv["hs_f"], dy, (dxs_skip, zbc, zbc), nlc, False, 0)
    r_b = ssd_bwd(t + "ssd_b", sv["xbc"], sv["u"], sv["dtrow"], p["bias_c"], p["alog_c"], p["bias_r"], p["alog_r"],
                  sv["hs_b"], dy, (r_f[0], r_f[1], r_f[2]), nlc, True, 1)
    dact = jnp.concatenate([r_b[0], r_b[1], r_b[2]], axis=1)
    dxbc, g["conv_w"], g["conv_b"] = conv_bwd(t + "conv", sv["u"], dact, p["conv_w"], p["conv_b"], nlat)
    drow = jnp.concatenate([r_f[4][0] + r_f[4][1], r_b[4][0] + r_b[4][1]], axis=0)
    drow_t = jnp.pad(jnp.transpose(drow), ((0, 0), (DT_LANE, 128 - DT_LANE - 32)))
    dmisc = misc_combine(t + "misc", dkr, r_f[3], r_b[3], drow_t)
    g["bias_c"] = r_f[5] + r_b[5]
    g["alog_c"] = r_f[6] + r_b[6]
    g["bias_r"] = jnp.concatenate([r_f[7], r_b[7]], axis=0)
    g["alog_r"] = jnp.concatenate([r_f[8], r_b[8]], axis=0)
    du = jnp.concatenate([dgates, dz, dq, dxbc, dk, dv, dckv, dmisc, jnp.zeros((n, 128), BF16), dcq], axis=1)
    g["w_in"] = mm_tn(sv["h"], du, t + "wg_in")
    dh = mm(du, p["w_in"], F32, t + "dg_in", trans_b=True)
    g["mod"] = (dgt1, dsh2, dsc2, dgt2)
    return dx1, dh, g


def local_step(x, c, ctx, target, c_ctx, W, nlat):
    xin = jnp.concatenate([x, ctx], axis=0)
    n = xin.shape[0]
    nt = nlat // RT
    tabs = rope_tables(nlat)
    c8 = jnp.zeros((8, D), F32).at[0].set(c[0]).at[1].set(c_ctx)
    mods, silus = [], []
    for i in range(DEPTH):
        m8, s8 = mod_fwd("l%d_mod" % i, c8, W[i]["w_mod"], W[i]["b_mod"])
        mods.append(m8[0:2].reshape(2, 1, 6 * D))
        silus.append(s8)
    saved = []
    _, h = resid_mod_fwd("l0_norm1", xin, None, None, 0, mods[0], 0, 1, W[0]["norm1_g"], nt)
    xcur = xin
    for i in range(DEPTH):
        x1, f, sv = layer_fwd(i, xcur, h, mods[i], W[i], tabs, nlat)
        saved.append(sv)
        if i + 1 < DEPTH:
            xcur, h = resid_mod_fwd("l%d_res2" % i, x1, f, mods[i], 5, mods[i + 1], 0, 1, W[i + 1]["norm1_g"], nt)
    loss_v, dx2, df, dgt2 = resid_loss("loss", x1, f, mods[DEPTH - 1], 5, target, nt)
    grads = [None] * DEPTH
    for i in reversed(range(DEPTH)):
        dx1, dh, g = layer_bwd(i, dx2, df, dgt2, saved[i], mods[i], W[i], tabs, nlat)
        if i > 0:
            sv = saved[i]
            dx2, df, dgt2, dsh1, dsc1, g["norm1_g"] = resid_mod_bwd(
                "l%db_res2" % (i - 1), sv["xin"], dx1, dh, saved[i - 1]["f"], mods[i - 1], 5, mods[i], 0, 1,
                W[i]["norm1_g"], nt)
        else:
            dxin, _, _, dsh1, dsc1, g["norm1_g"] = resid_mod_bwd("l0b_norm1", saved[0]["xin"], dx1, dh, None, None, 0,
                                                                  mods[0], 0, 1, W[0]["norm1_g"], nt)
        dgt1, dsh2, dsc2, dgt2_i = g.pop("mod")
        dmod = jnp.concatenate([dsh1, dsc1, dgt1, dsh2, dsc2, dgt2_i], axis=2).reshape(2, 6 * D)
        dmod8 = jnp.zeros((8, 6 * D), F32).at[0:2].set(dmod)
        g["w_mod"] = mm_tn(silus[i], dmod8, "l%db_wg_mod" % i)
        dsilu = mm(dmod8, W[i]["w_mod"], F32, "l%db_dg_mod" % i, trans_b=True)
        dc8, g["b_mod"] = mod_small_bwd("l%db_mod_small" % i, c8, dsilu, dmod8)
        g["c8"] = dc8
        grads[i] = g
    del n
    return loss_v[0, 0], dxin, grads


def _big_shapes():
    return dict(w_mod=(2, 1024, 1536), w_in=(2, 1024, 1976), w_mla_uq=(2, 384, 384), w_mla_ukv=(2, 256, 512),
                w_p_ssm=(2, 256, 1024), w_p_swa=(2, 256, 1024), w_p_mla=(2, 256, 1024), w_out=(2, 256, 1024),
                w_ffn_in=(2, 1024, 1408), w_ffn_out=(2, 704, 1024))


PACK_ROWS = 14336


def _pack_big(d, dtype):
    parts = [d[k].astype(dtype).reshape(-1, 1024) for k in BIG]
    used = sum(p.shape[0] for p in parts)
    return jnp.concatenate(parts + [jnp.zeros((PACK_ROWS - used, 1024), dtype)], axis=0)


def _unpack_big(buf, lead):
    out = {}
    r0 = 0
    for k in BIG:
        sh = _big_shapes()[k]
        rows = sh[0] * sh[1] * sh[2] // 1024
        out[k] = buf[..., r0:r0 + rows, :].reshape(lead + sh)
        r0 += rows
    return out


def _full_from_chips(k, a):
    if k in COL_SHARDED:
        return a.transpose(1, 2, 0, 3).reshape(2, a.shape[2], 4 * a.shape[3])
    return a.transpose(1, 0, 2, 3).reshape(2, 4 * a.shape[2], a.shape[3])


def _chips_from_full(k, a):
    if k in COL_SHARDED:
        return a.reshape(a.shape[0], 4, a.shape[1] // 4).transpose(1, 0, 2)
    return a.reshape(4, a.shape[0] // 4, a.shape[1])


def _small_sizes():
    return dict(c_ctx=1024, b_mod=2 * 6144, norm1_g=2048, norm2_g=2048, ssm_conv_w=2 * 5 * 1536, ssm_conv_b=2 * 1536,
                ssm_dt_bias=64, ssm_a_log=64, ssm_d=32, ssm_norm_g=2048, swa_q_norm_g=256, swa_k_norm_g=256, swa_sink=16,
                mla_q_lat_g=768, mla_kv_lat_g=512, mla_q_norm_g=384, mla_k_norm_g=384)


def _pack_small(d):
    parts = []
    for k in SMALL:
        v = d[k].astype(F32).reshape(-1)
        parts.append(jnp.pad(v, (0, (-v.shape[0]) % 1024)))
    return jnp.concatenate(parts).reshape(-1, 128)


def _unpack_small(buf, shapes):
    flat = buf.reshape(-1)
    out = {}
    o = 0
    for k in SMALL:
        sz = _small_sizes()[k]
        out[k] = flat[o:o + sz].reshape(shapes[k])
        o += sz + (-sz) % 1024
    return out


def big_grads(grads):
    gfull = {k: [] for k in BIG}
    for i in range(DEPTH):
        g = grads[i]
        gfull["w_mod"].append(g["w_mod"])
        gfull["w_in"].append(unpack_w_in(g["w_in"]))
        gfull["w_mla_uq"].append(unpack_uq(g["w_uq"]))
        gfull["w_mla_ukv"].append(unpack_ukv(g["w_ukv"]))
        for k in ("w_p_ssm", "w_p_swa", "w_p_mla", "w_out", "w_ffn_in", "w_ffn_out"):
            gfull[k].append(g[k])
    return gfull


def small_grads(grads):
    gs = {}
    gs["c_ctx"] = sum(grads[i]["c8"][1] for i in range(DEPTH))
    st = lambda f: jnp.stack([f(grads[i]) for i in range(DEPTH)])
    gs["b_mod"] = st(lambda g: g["b_mod"][0])
    gs["norm1_g"] = st(lambda g: g["norm1_g"][0])
    gs["norm2_g"] = st(lambda g: g["norm2_g"][0])
    gs["ssm_conv_w"] = st(lambda g: g["conv_w"])
    gs["ssm_conv_b"] = st(lambda g: g["conv_b"][0])
    gs["ssm_dt_bias"] = st(lambda g: (g["bias_c"][0, DT_LANE:DT_LANE + 32] + g["bias_r"][:, 0]).reshape(2, 16))
    gs["ssm_a_log"] = st(lambda g: (g["alog_c"][0, DT_LANE:DT_LANE + 32] + g["alog_r"][:, 0]).reshape(2, 16))
    gs["ssm_d"] = st(lambda g: g["d_exp"].reshape(16, 64).sum(axis=1))
    gs["ssm_norm_g"] = st(lambda g: g["ssm_norm_g"][0])
    gs["swa_q_norm_g"] = st(lambda g: g["swa_q_g"][0])
    gs["swa_k_norm_g"] = st(lambda g: g["swa_k_g"][0])
    gs["swa_sink"] = st(lambda g: g["sink"][:, 0, 0])
    gs["mla_q_lat_g"] = st(lambda g: g["q_lat_g"][0])
    gs["mla_kv_lat_g"] = st(lambda g: g["kv_lat_g"][0])
    gs["mla_q_norm_g"] = st(lambda g: g["mla_q_g"][0, :192])
    gs["mla_k_norm_g"] = st(lambda g: g["mla_k_g"][0, :192])
    return gs


def layer_params(i, full, conv_full, sm, nlat):
    p = {}
    p["w_mod"] = full["w_mod"][i]
    p["w_in"] = pack_w_in(full["w_in"][i])
    p["w_uq"] = pack_uq(full["w_mla_uq"][i])
    p["w_ukv"] = pack_ukv(full["w_mla_ukv"][i])
    for k in ("w_p_ssm", "w_p_swa", "w_p_mla", "w_out", "w_ffn_in", "w_ffn_out"):
        p[k] = full[k][i]
    p["b_mod"] = sm["b_mod"][i][None]
    p["norm1_g"] = sm["norm1_g"][i][None]
    p["norm2_g"] = sm["norm2_g"][i][None]
    p["conv_w"] = conv_full[i]
    p["conv_b"] = sm["ssm_conv_b"][i][None]
    bias = sm["ssm_dt_bias"][i].reshape(32)
    alog = sm["ssm_a_log"][i].reshape(32)
    p["bias_c"] = _lanes(bias, DT_LANE)
    p["alog_c"] = _lanes(alog, DT_LANE)
    p["bias_r"] = bias[:, None]
    p["alog_r"] = alog[:, None]
    p["d_exp"] = jnp.repeat(sm["ssm_d"][i], 64)[None]
    p["ssm_norm_g"] = sm["ssm_norm_g"][i][None]
    p["swa_q_g"] = sm["swa_q_norm_g"][i][None]
    p["swa_k_g"] = sm["swa_k_norm_g"][i][None]
    p["sink"] = jnp.broadcast_to(sm["swa_sink"][i][:, None, None], (SWA_HQ, 1, 128))
    p["q_lat_g"] = sm["mla_q_lat_g"][i][None]
    p["kv_lat_g"] = sm["mla_kv_lat_g"][i][None]
    p["mla_q_g"] = _lanes(sm["mla_q_norm_g"][i], 0, 256)
    p["mla_k_g"] = _lanes(sm["mla_k_norm_g"][i], 0, 256)
    p["swa_cfg"] = dict(w=128, vw=128, hq=SWA_HQ, grp=SWA_HQ // SWA_HKV, vcol0=C_V // 128, scale=SWA_DH ** -0.5,
                        tq=256, tk=256, band=True)
    return p


def kernel(x, c, ctx, c_ctx, w_mod, b_mod, norm1_g, norm2_g, w_in, ssm_conv_w, ssm_conv_b, ssm_dt_bias, ssm_a_log, ssm_d, ssm_norm_g, swa_q_norm_g, swa_k_norm_g, swa_sink, mla_q_lat_g, mla_kv_lat_g, w_mla_uq, w_mla_ukv, mla_q_norm_g, mla_k_norm_g, w_p_ssm, w_p_swa, w_p_mla, w_out, w_ffn_in, w_ffn_out, loss_target, m_c_ctx, m_w_mod, m_b_mod, m_norm1_g, m_norm2_g, m_w_in, m_ssm_conv_w, m_ssm_conv_b, m_ssm_dt_bias, m_ssm_a_log, m_ssm_d, m_ssm_norm_g, m_swa_q_norm_g, m_swa_k_norm_g, m_swa_sink, m_mla_q_lat_g, m_mla_kv_lat_g, m_w_mla_uq, m_w_mla_ukv, m_mla_q_norm_g, m_mla_k_norm_g, m_w_p_ssm, m_w_p_swa, m_w_p_mla, m_w_out, m_w_ffn_in, m_w_ffn_out, v_c_ctx, v_w_mod, v_b_mod, v_norm1_g, v_norm2_g, v_w_in, v_ssm_conv_w, v_ssm_conv_b, v_ssm_dt_bias, v_ssm_a_log, v_ssm_d, v_ssm_norm_g, v_swa_q_norm_g, v_swa_k_norm_g, v_swa_sink, v_mla_q_lat_g, v_mla_kv_lat_g, v_w_mla_uq, v_w_mla_ukv, v_mla_q_norm_g, v_mla_k_norm_g, v_w_p_ssm, v_w_p_swa, v_w_p_mla, v_w_out, v_w_ffn_in, v_w_ffn_out):
    loc = dict(locals())
    w = {k: loc[k] for k in WEIGHTS}
    m = {k: loc["m_" + k] for k in WEIGHTS}
    v = {k: loc["v_" + k] for k in WEIGHTS}
    nlat = x.shape[1]

    gathered = _unpack_big(gather_chips("gather_weights", _pack_big(w, BF16)), (4,))
    full = {k: _full_from_chips(k, gathered[k]) for k in BIG}
    conv_sh = jnp.pad(ssm_conv_w.reshape(10, 384), ((0, 6), (0, 0)))
    conv_full = gather_chips("gather_conv", conv_sh)[:, :10].reshape(4, 2, 5, 384).transpose(1, 2, 0, 3).reshape(2, 5, 1536)

    W = [layer_params(i, full, conv_full, w, nlat) for i in range(DEPTH)]

    loss_loc, dx, grads = local_step(x[0], c, ctx[0], loss_target[0], c_ctx, W, nlat)

    gfull = big_grads(grads)
    by_chip = {k: jnp.stack([_chips_from_full(k, a) for a in gfull[k]], axis=1) for k in BIG}
    parts = [by_chip[k].astype(BF16).reshape(4, -1, 1024) for k in BIG]
    used = sum(p.shape[1] for p in parts)
    send = jnp.concatenate(parts + [jnp.zeros((4, PACK_ROWS - used, 1024), BF16)], axis=1)
    own, got = pair_split("pair_split", send)
    pair = add_cast("pair_sum", own, got, BF16)
    recv = scatter_chips("scatter_grads", pair)
    mine = sum_blocks("sum_chips", recv)
    gbig = _unpack_big(pair_join("join_cores", mine), ())

    gs = small_grads(grads)
    small_all = gather_all("gather_small", _pack_small(gs))
    small_sum = sum_blocks("sum_small", small_all)
    full_shapes = {k: (w[k].shape if k != "ssm_conv_w" else (2, 5, 1536)) for k in SMALL}
    gsmall = _unpack_small(small_sum, full_shapes)
    chip = 2 * lax.axis_index("x") + lax.axis_index("y")
    gsmall["ssm_conv_w"] = lax.dynamic_slice_in_dim(gsmall["ssm_conv_w"], chip * 384, 384, axis=2)

    grad = {**gbig, **gsmall}
    delta, new_m, new_v = {}, {}, {}
    sm = {k: _pack_small_local(d) for k, d in (("w", w), ("g", grad), ("m", m), ("v", v))}
    r = adamw("adamw_small", sm["w"], sm["g"], sm["m"], sm["v"])
    shapes = {k: w[k].shape for k in SMALL}
    for dst, buf in zip((delta, new_m, new_v), r):
        dst.update(_unpack_small_local(buf, shapes))
    for k in BIG:
        sh = w[k].shape
        r = adamw("adamw_" + k, *[a[k].reshape(sh[0] * sh[1], sh[2]) for a in (w, grad, m, v)])
        for dst, buf in zip((delta, new_m, new_v), r):
            dst[k] = buf.reshape(sh)

    loss = lax.psum(loss_loc, ("x", "y", "c"))
    return (loss, dx[None, :nlat], *[grad[k] for k in WEIGHTS], *[delta[k] for k in WEIGHTS],
            *[new_m[k] for k in WEIGHTS], *[new_v[k] for k in WEIGHTS])


def _pack_small_local(d):
    parts = []
    for k in SMALL:
        a = d[k].astype(F32).reshape(-1)
        parts.append(jnp.pad(a, (0, (-a.shape[0]) % 1024)))
    return jnp.concatenate(parts).reshape(-1, 128)


def _unpack_small_local(buf, shapes):
    flat = buf.reshape(-1)
    out = {}
    o = 0
    for k in SMALL:
        sz = math.prod(shapes[k])
        out[k] = flat[o:o + sz].reshape(shapes[k])
        o += sz + (-sz) % 1024
    return out
```

```python
import functools
import math

import jax
import jax.numpy as jnp
from jax import lax
from jax.experimental import pallas as pl
from jax.experimental.pallas import tpu as pltpu

F32 = jnp.float32
BF16 = jnp.bfloat16
MESH = pl.DeviceIdType.MESH

D = 1024
NCTX = 256
EPS = 1e-6
ROPE_BASE = 10000.0
GRID_W = 64
DEPTH = 2
Q = 128
N_HEADS_SSM = 16
SWA_HQ, SWA_HKV, SWA_DH, SWA_WIN = 8, 2, 128, 128
MLA_H, MLA_NOPE, MLA_ROPE, MLA_V = 8, 128, 64, 128
MLA_QRANK, MLA_KVRANK = 384, 256
FFN = 2816
RT = 256
VMEM_LIMIT = 56 << 20
NEG = -1e30
LOG2E = 1.4426950408889634

C_G1, C_G2, C_G3, C_Z, C_Q, C_XS, C_B, C_C, C_K, C_V, C_CKV, C_MISC, C_PAD, C_CQ = (
    0, 1024, 2048, 3072, 4096, 5120, 6144, 6400, 6656, 6912, 7168, 7424, 7552, 7680)
UW = 8064
DT_LANE = 64

ADAM_LR, ADAM_B1, ADAM_B2, ADAM_EPS, ADAM_WD, ADAM_STEP = 0.001, 0.9, 0.999, 1e-08, 0.01, 10


def _cp(sem):
    return pltpu.CompilerParams(dimension_semantics=sem, vmem_limit_bytes=VMEM_LIMIT)


def _pick(n, cands):
    for c in cands:
        if n % c == 0:
            return c
    return n


_TN = (1536, 1408, 1152, 1024, 896, 768, 512, 384, 256, 128)


def mm(a, b, out_dtype, name, trans_b=False):
    m, k = a.shape
    n = b.shape[0] if trans_b else b.shape[1]
    tm = _pick(m, (768, 512, 256, 128, 8))
    tn = _pick(n, _TN)
    tk = k if k <= 2048 else _pick(k, (1408, 1152, 1024, 896, 768, 512))
    nk = k // tk
    b_spec = (pl.BlockSpec((tn, tk), lambda i, j, kk: (j, kk)) if trans_b
              else pl.BlockSpec((tk, tn), lambda i, j, kk: (kk, j)))

    def body(a_ref, b_ref, o_ref, *acc):
        p = _d(a_ref[...], b_ref[...], ((1,), (1 if trans_b else 0,)))
        if nk == 1:
            o_ref[...] = p.astype(out_dtype)
        else:
            kk = pl.program_id(2)

            @pl.when(kk == 0)
            def _():
                acc[0][...] = p

            @pl.when(kk > 0)
            def _():
                acc[0][...] += p

            @pl.when(kk == nk - 1)
            def _():
                o_ref[...] = acc[0][...].astype(out_dtype)

    return pl.pallas_call(
        body, out_shape=jax.ShapeDtypeStruct((m, n), out_dtype), grid=(m // tm, n // tn, nk),
        in_specs=[pl.BlockSpec((tm, tk), lambda i, j, kk: (i, kk)), b_spec],
        out_specs=pl.BlockSpec((tm, tn), lambda i, j, kk: (i, j)),
        scratch_shapes=[] if nk == 1 else [pltpu.VMEM((tm, tn), F32)],
        name=name, compiler_params=_cp(("parallel", "parallel", "arbitrary")))(a, b)


def mm_tn(a, b, name, out_dtype=BF16):
    t, ka = a.shape
    _, nb = b.shape
    ta = _pick(ka, (1024, 1408, 768, 512, 384, 256, 128))
    tb = _pick(nb, _TN)
    tt = _pick(t, (768, 512, 256, 128, 8))
    nt = t // tt

    def body(a_ref, b_ref, o_ref, acc):
        p = _d(a_ref[...], b_ref[...], ((0,), (0,)))
        s = pl.program_id(2)

        @pl.when(s == 0)
        def _():
            acc[...] = p

        @pl.when(s > 0)
        def _():
            acc[...] += p

        @pl.when(s == nt - 1)
        def _():
            o_ref[...] = acc[...].astype(out_dtype)

    return pl.pallas_call(
        body, out_shape=jax.ShapeDtypeStruct((ka, nb), out_dtype), grid=(ka // ta, nb // tb, nt),
        in_specs=[pl.BlockSpec((tt, ta), lambda i, j, s: (s, i)), pl.BlockSpec((tt, tb), lambda i, j, s: (s, j))],
        out_specs=pl.BlockSpec((ta, tb), lambda i, j, s: (i, j)), scratch_shapes=[pltpu.VMEM((ta, tb), F32)],
        name=name, compiler_params=_cp(("parallel", "parallel", "arbitrary")))(a, b)


def _rms(x, g, n=None):
    n = x.shape[-1] if n is None else n
    r = lax.rsqrt(jnp.sum(x * x, axis=-1, keepdims=True) * (1.0 / n) + EPS)
    return x * r * g


def _silu(x):
    return x * jax.nn.sigmoid(x)


def _modulate(x, g, sc, sh):
    return _rms(x, g) * (1.0 + sc) + sh


def _swap(x, s):
    ax = x.ndim - 1
    w = x.shape[ax]
    lane = lax.broadcasted_iota(jnp.int32, x.shape, ax)
    lo = (lane & s) == 0
    return jnp.where(lo, pltpu.roll(x, w - s, ax), pltpu.roll(x, s, ax))


@functools.partial(jax.custom_vjp, nondiff_argnums=(3,))
def _rope(x, cos, sin, s):
    return x * cos + _swap(x, s) * sin


def _rope_fwd(x, cos, sin, s):
    return _rope(x, cos, sin, s), (cos, sin)


def _rope_bwd(s, res, g):
    cos, sin = res
    return g * cos - _swap(g, s) * sin, jnp.zeros_like(cos), jnp.zeros_like(sin)


_rope.defvjp(_rope_fwd, _rope_bwd)


@jax.custom_vjp
def _softplus(x):
    return jnp.maximum(x, 0.0) + jnp.log(1.0 + jnp.exp(-jnp.abs(x)))


def _softplus_fwd(x):
    return _softplus(x), x


def _softplus_bwd(x, g):
    return (g * jax.nn.sigmoid(x),)


_softplus.defvjp(_softplus_fwd, _softplus_bwd)


def _d(a, b, dims):
    return lax.dot_general(a.astype(BF16), b.astype(BF16), (dims, ((), ())), preferred_element_type=F32)


@jax.custom_vjp
def bdot(a, b):
    return _d(a, b, ((1,), (0,)))


bdot.defvjp(lambda a, b: (bdot(a, b), (a, b)),
            lambda r, g: (_d(g, r[1], ((1,), (1,))), _d(r[0], g, ((0,), (0,)))))


@jax.custom_vjp
def bdot_nt(a, b):
    return _d(a, b, ((1,), (1,)))


bdot_nt.defvjp(lambda a, b: (bdot_nt(a, b), (a, b)),
               lambda r, g: (_d(g, r[1], ((1,), (0,))), _d(g, r[0], ((0,), (0,)))))


@jax.custom_vjp
def bdot_tn(a, b):
    return _d(a, b, ((0,), (0,)))


bdot_tn.defvjp(lambda a, b: (bdot_tn(a, b), (a, b)),
               lambda r, g: (_d(r[1], g, ((1,), (1,))), _d(r[0], g, ((1,), (0,)))))


def _tri(rev):
    i = lax.broadcasted_iota(jnp.int32, (Q, Q), 0)
    j = lax.broadcasted_iota(jnp.int32, (Q, Q), 1)
    return (i <= j) if rev else (i >= j)


def _split3(a):
    hi = a.astype(BF16)
    r = a - hi.astype(F32)
    mid = r.astype(BF16)
    lo = (r - mid.astype(F32)).astype(BF16)
    return hi, mid, lo


def _cum_cols_impl(a, rev):
    t = _tri(rev).astype(BF16)
    return sum(jnp.dot(t, p, preferred_element_type=F32) for p in _split3(a))


def _cum_rows_impl(a, rev):
    t = _tri(not rev).astype(BF16)
    return sum(jnp.dot(p, t, preferred_element_type=F32) for p in _split3(a))


@functools.partial(jax.custom_vjp, nondiff_argnums=(1,))
def cum_cols(a, rev):
    return _cum_cols_impl(a, rev)


cum_cols.defvjp(lambda a, rev: (_cum_cols_impl(a, rev), None), lambda rev, _, g: (_cum_cols_impl(g, not rev),))


@functools.partial(jax.custom_vjp, nondiff_argnums=(1,))
def cum_rows(a, rev):
    return _cum_rows_impl(a, rev)


cum_rows.defvjp(lambda a, rev: (_cum_rows_impl(a, rev), None), lambda rev, _, g: (_cum_rows_impl(g, not rev),))


def _rs(w, cb=0):
    return pl.BlockSpec((RT, w), lambda i: (i, cb))


def _ps(shape):
    nd = len(shape)
    return pl.BlockSpec(shape, lambda i: (0,) * nd)


def _gs(w, cb, nlat):
    return pl.BlockSpec((1, 1, w), lambda i: (i // nlat, 0, cb))


def _rowcall(name, body, n, ins, outs, scratch=()):
    return pl.pallas_call(
        body, out_shape=[o[0] for o in outs], grid=(n // RT,), in_specs=[s for _, s in ins],
        out_specs=[s for _, s in outs], scratch_shapes=list(scratch), name=name,
        compiler_params=_cp(("arbitrary",)))(*[a for a, _ in ins])


def _acc(ref, val, first):
    @pl.when(first)
    def _():
        ref[...] = val

    @pl.when(jnp.logical_not(first))
    def _():
        ref[...] += val


def _sd(shape, dt):
    return jax.ShapeDtypeStruct(shape, dt)


def resid_mod_fwd(name, xp, o, mod_gt, gt_i, mod_n, sh_i, sc_i, norm_g, nlat):
    n = xp.shape[0]
    has_res = o is not None

    def body(*refs):
        if has_res:
            xp_ref, o_ref, gt_ref, sh_ref, sc_ref, g_ref, xn_ref, h_ref = refs
            xn = xp_ref[...] + gt_ref[0] * o_ref[...]
            xn_ref[...] = xn
        else:
            xp_ref, sh_ref, sc_ref, g_ref, h_ref = refs
            xn = xp_ref[...]
        h_ref[...] = _modulate(xn, g_ref[...], sc_ref[0], sh_ref[0]).astype(BF16)

    ins = [(xp, _rs(D))]
    if has_res:
        ins += [(o, _rs(D)), (mod_gt, _gs(D, gt_i, nlat))]
    ins += [(mod_n, _gs(D, sh_i, nlat)), (mod_n, _gs(D, sc_i, nlat)), (norm_g, _ps((1, D)))]
    outs = ([(_sd((n, D), F32), _rs(D))] if has_res else []) + [(_sd((n, D), BF16), _rs(D))]
    r = _rowcall(name, body, n, ins, outs)
    return (r[0], r[1]) if has_res else (xp, r[0])


def resid_mod_bwd(name, xn, dxn, dh, o, mod_gt, gt_i, mod_n, sh_i, sc_i, norm_g, nlat):
    n = xn.shape[0]
    has_res = o is not None

    def body(*refs):
        i = pl.program_id(0)
        if has_res:
            (xn_ref, dxn_ref, dh_ref, o_ref, gt_ref, sh_ref, sc_ref, g_ref,
             dx_ref, do_ref, dgt_ref, dsh_ref, dsc_ref, dg_ref) = refs
        else:
            xn_ref, dxn_ref, dh_ref, sh_ref, sc_ref, g_ref, dx_ref, dsh_ref, dsc_ref, dg_ref = refs
        _, vjp = jax.vjp(_modulate, xn_ref[...], g_ref[...], sc_ref[0], sh_ref[0])
        dx, dg, dsc, dsh = vjp(dh_ref[...])
        dx = dx + dxn_ref[...]
        dx_ref[...] = dx
        gfirst = (i == 0) | (i == nlat)
        _acc(dg_ref, dg, i == 0)
        _acc(dsh_ref, dsh[None], gfirst)
        _acc(dsc_ref, dsc[None], gfirst)
        if has_res:
            do_ref[...] = (gt_ref[0] * dx).astype(BF16)
            _acc(dgt_ref, jnp.sum(dx * o_ref[...], axis=0, keepdims=True)[None], gfirst)

    ins = [(xn, _rs(D)), (dxn, _rs(D)), (dh, _rs(D))]
    if has_res:
        ins += [(o, _rs(D)), (mod_gt, _gs(D, gt_i, nlat))]
    ins += [(mod_n, _gs(D, sh_i, nlat)), (mod_n, _gs(D, sc_i, nlat)), (norm_g, _ps((1, D)))]
    gacc = (_sd((2, 1, D), F32), _gs(D, 0, nlat))
    outs = [(_sd((n, D), F32), _rs(D))]
    if has_res:
        outs += [(_sd((n, D), BF16), _rs(D)), gacc]
    outs += [gacc, gacc, (_sd((1, D), F32), _ps((1, D)))]
    r = _rowcall(name, body, n, ins, outs)
    if has_res:
        return r
    return r[0], None, None, r[1], r[2], r[3]


def resid_loss(name, xp, o, mod_gt, gt_i, target, nlat):
    n = xp.shape[0]

    def body(xp_ref, o_ref, gt_ref, t_ref, loss_ref, dx_ref, do_ref, dgt_ref):
        i = pl.program_id(0)
        gt = gt_ref[0]

        @pl.when(i < nlat)
        def _():
            err = xp_ref[...] + gt * o_ref[...] - t_ref[...]
            dx = err * (1.0 / D)
            dx_ref[...] = dx
            do_ref[...] = (gt * dx).astype(BF16)
            _acc(loss_ref, jnp.full((1, 128), 0.5 / D, F32) * jnp.sum(err * err), i == 0)
            _acc(dgt_ref, jnp.sum(dx * o_ref[...], axis=0, keepdims=True)[None], i == 0)

        @pl.when(i >= nlat)
        def _():
            dx_ref[...] = jnp.zeros((RT, D), F32)
            do_ref[...] = jnp.zeros((RT, D), BF16)
            dgt_ref[...] = jnp.zeros((1, 1, D), F32)

    tgt_spec = pl.BlockSpec((RT, D), lambda i: (jnp.minimum(i, nlat - 1), 0))
    ins = [(xp, _rs(D)), (o, _rs(D)), (mod_gt, _gs(D, gt_i, nlat)), (target, tgt_spec)]
    outs = [(_sd((1, 128), F32), _ps((1, 128))), (_sd((n, D), F32), _rs(D)), (_sd((n, D), BF16), _rs(D)),
            (_sd((2, 1, D), F32), _gs(D, 0, nlat))]
    return _rowcall(name, body, n, ins, outs)


def mod_fwd(name, c8, w_mod, b_mod):
    tn = 1536

    def body(c_ref, w_ref, b_ref, o_ref, s_ref):
        s = _silu(c_ref[...]).astype(BF16)
        s_ref[...] = s
        o_ref[...] = jnp.dot(s, w_ref[...], preferred_element_type=F32) + b_ref[...]

    return pl.pallas_call(
        body, out_shape=[_sd((8, 6 * D), F32), _sd((8, D), BF16)], grid=(6 * D // tn,),
        in_specs=[pl.BlockSpec((8, D), lambda j: (0, 0)), pl.BlockSpec((D, tn), lambda j: (0, j)),
                  pl.BlockSpec((1, tn), lambda j: (0, j))],
        out_specs=[pl.BlockSpec((8, tn), lambda j: (0, j)), pl.BlockSpec((8, D), lambda j: (0, 0))],
        name=name, compiler_params=_cp(("arbitrary",)))(c8, w_mod, b_mod)


def mod_small_bwd(name, c8, dsilu, dmod8):
    def body(c_ref, ds_ref, dm_ref, dc_ref, db_ref):
        _, vjp = jax.vjp(_silu, c_ref[...])
        dc_ref[...] = vjp(ds_ref[...])[0]
        db_ref[...] = jnp.sum(dm_ref[...], axis=0, keepdims=True)

    return pl.pallas_call(
        body, out_shape=[_sd((8, D), F32), _sd((1, 6 * D), F32)], grid=(1,),
        in_specs=[pl.BlockSpec((8, D), lambda j: (0, 0)), pl.BlockSpec((8, D), lambda j: (0, 0)),
                  pl.BlockSpec((8, 6 * D), lambda j: (0, 0))],
        out_specs=[pl.BlockSpec((8, D), lambda j: (0, 0)), pl.BlockSpec((1, 6 * D), lambda j: (0, 0))],
        name=name, compiler_params=_cp(("arbitrary",)))(c8, dsilu, dmod8)


def _conv_taps(x, nlat):
    n = x.shape[0]
    r = lax.broadcasted_iota(jnp.int32, x.shape, 0)
    lo = jnp.where(r < nlat, 0, nlat)
    hi = jnp.where(r < nlat, nlat, n)
    taps = []
    for o in (-2, -1, 0, 1, 2):
        xs = x if o == 0 else pltpu.roll(x, (-o) % n, 0)
        t = r + o
        taps.append(jnp.where((t >= lo) & (t < hi), xs, 0.0))
    return taps


def conv_fwd(name, u, w, b, nlat_rows):
    n = u.shape[0]

    def body(x_ref, w_ref, b_ref, o_ref):
        taps = _conv_taps(x_ref[...], nlat_rows)
        wv = w_ref[...]
        pre = b_ref[...] + sum(taps[k] * wv[k:k + 1, :] for k in range(5))
        o_ref[...] = _silu(pre)

    return pl.pallas_call(
        body, out_shape=_sd((n, 1536), F32), grid=(12,),
        in_specs=[pl.BlockSpec((n, 128), lambda j: (0, C_XS // 128 + j)), pl.BlockSpec((5, 128), lambda j: (0, j)),
                  pl.BlockSpec((1, 128), lambda j: (0, j))],
        out_specs=pl.BlockSpec((n, 128), lambda j: (0, j)),
        name=name, compiler_params=_cp(("parallel",)))(u, w, b)


def conv_bwd(name, u, dact, w, b, nlat_rows):
    n = u.shape[0]

    def body(x_ref, da_ref, w_ref, b_ref, dx_ref, dw_ref, db_ref):
        taps = _conv_taps(x_ref[...], nlat_rows)
        wv = w_ref[...]
        pre = b_ref[...] + sum(taps[k] * wv[k:k + 1, :] for k in range(5))
        s = jax.nn.sigmoid(pre)
        dpre = da_ref[...] * (s * (1.0 + pre * (1.0 - s)))
        db_ref[...] = jnp.sum(dpre, axis=0, keepdims=True)
        rows = lax.broadcasted_iota(jnp.int32, (5, 128), 0)
        dw = jnp.zeros((5, 128), F32)
        for k in range(5):
            dw = dw + jnp.where(rows == k, jnp.sum(dpre * taps[k], axis=0, keepdims=True), 0.0)
        dw_ref[...] = dw
        r = lax.broadcasted_iota(jnp.int32, dpre.shape, 0)
        lo = jnp.where(r < nlat_rows, 0, nlat_rows)
        hi = jnp.where(r < nlat_rows, nlat_rows, n)
        dx = jnp.zeros_like(dpre)
        for k in range(5):
            o = k - 2
            ds = dpre if o == 0 else pltpu.roll(dpre, o % n, 0)
            t = r - o
            dx = dx + jnp.where((t >= lo) & (t < hi), ds, 0.0) * wv[k:k + 1, :]
        dx_ref[...] = dx.astype(BF16)

    return pl.pallas_call(
        body, out_shape=[_sd((n, 1536), BF16), _sd((5, 1536), F32), _sd((1, 1536), F32)], grid=(12,),
        in_specs=[pl.BlockSpec((n, 128), lambda j: (0, C_XS // 128 + j)), pl.BlockSpec((n, 128), lambda j: (0, j)),
                  pl.BlockSpec((5, 128), lambda j: (0, j)), pl.BlockSpec((1, 128), lambda j: (0, j))],
        out_specs=[pl.BlockSpec((n, 128), lambda j: (0, j)), pl.BlockSpec((5, 128), lambda j: (0, j)),
                   pl.BlockSpec((1, 128), lambda j: (0, j))],
        name=name, compiler_params=_cp(("parallel",)))(u, dact, w, b)


def _ssd_chunk(rev, dirn, g, x4, bm, cm, misc, dtrow, bias_c, alog_c, bias_r, alog_r, h4):
    dt_c = _softplus(misc + bias_c)
    a_c = dt_c * (-jnp.exp(alog_c))
    dt_r = _softplus(dtrow + bias_r)
    a_r = dt_r * (-jnp.exp(alog_r))
    cs_c = cum_cols(a_c, rev)
    cs_r = cum_rows(a_r, rev)
    tot_c = jnp.sum(a_c, axis=0, keepdims=True)
    cb = bdot_nt(cm, bm)
    tri = _tri(rev)
    lane = lax.broadcasted_iota(jnp.int32, (1, 128), 1)
    row16 = lax.broadcasted_iota(jnp.int32, (16, 1), 0)
    prow = lax.broadcasted_iota(jnp.int32, (128, 1), 0)
    ys, hs = [], []
    for p in range(4):
        ydiag = 0.0
        wst = 0.0
        eoff = 0.0
        hscale = 0.0
        for e in range(2):
            hg = 8 * g + 2 * p + e
            oh_c = (lane == DT_LANE + 16 * dirn + hg).astype(F32)
            dt_h = jnp.sum(dt_c * oh_c, axis=1, keepdims=True)
            cs_h = jnp.sum(cs_c * oh_c, axis=1, keepdims=True)
            tot_h = jnp.sum(tot_c * oh_c, axis=1, keepdims=True)
            csr_h = jnp.sum(cs_r * (row16 == hg).astype(F32), axis=0, keepdims=True)
            seg = jnp.exp(jnp.where(tri, cs_h - csr_h, -jnp.inf))
            hm = ((lane < 64) if e == 0 else (lane >= 64)).astype(F32)
            ydiag = ydiag + bdot(cb * seg, x4[p] * (dt_h * hm))
            wst = wst + (dt_h * jnp.exp(tot_h - cs_h)) * hm
            eoff = eoff + jnp.exp(cs_h) * hm
            hscale = hscale + jnp.exp(tot_h) * ((prow < 64) if e == 0 else (prow >= 64)).astype(F32)
        ys.append(ydiag + bdot_nt(cm, h4[p]) * eoff)
        hs.append(h4[p] * hscale + bdot_tn(x4[p] * wst, bm))
    return ys, hs


def _ssd_specs(nlat_chunks, rev, dirn, bwd):
    nc = nlat_chunks + 2

    def chunk(s):
        if bwd:
            s = nc - 1 - s
        return (nlat_chunks + 1 - s) if rev else (s + nlat_chunks) % nc

    def step(s):
        return (nc - 1 - s) if bwd else s

    return dict(
        x=pl.BlockSpec((Q, 512), lambda g, s: (chunk(s), g)),
        b=pl.BlockSpec((Q, 128), lambda g, s: (chunk(s), 8 + g)),
        c=pl.BlockSpec((Q, 128), lambda g, s: (chunk(s), 10 + g)),
        misc=pl.BlockSpec((Q, 128), lambda g, s: (chunk(s), C_MISC // 128)),
        dtrow=pl.BlockSpec((16, Q), lambda g, s: (dirn, chunk(s))),
        p_c=pl.BlockSpec((1, 128), lambda g, s: (0, 0)),
        p_r=pl.BlockSpec((16, 1), lambda g, s: (dirn, 0)),
        y=pl.BlockSpec((Q, 512), lambda g, s: (chunk(s), g)),
        hsave=pl.BlockSpec((1, 1, 512, 128), lambda g, s: (g, step(s), 0, 0)),
        bc_out=pl.BlockSpec((Q, 128), lambda g, s: (chunk(s), g)),
        misc_out=pl.BlockSpec((1, Q, 128), lambda g, s: (g, chunk(s), 0)),
        dtrow_out=pl.BlockSpec((1, 16, Q), lambda g, s: (g, 0, chunk(s))),
        pacc_c=pl.BlockSpec((1, 128), lambda g, s: (0, 0)),
        pacc_r=pl.BlockSpec((16, 1), lambda g, s: (0, 0)),
    )


def ssd_fwd(name, xbc, u, dtrow, bias_c, alog_c, bias_r, alog_r, nlat_chunks, rev, dirn):
    n = xbc.shape[0]
    nc = nlat_chunks + 2
    sp = _ssd_specs(nlat_chunks, rev, dirn, False)

    def body(x_ref, b_ref, c_ref, m_ref, r_ref, bc_ref, ac_ref, br_ref, ar_ref, y_ref, hs_ref, h_s):
        g = pl.program_id(0)
        s = pl.program_id(1)

        @pl.when(s == 0)
        def _():
            h_s[...] = jnp.zeros((512, 128), F32)

        hs_ref[0, 0] = h_s[...]
        x4 = [x_ref[:, 128 * p:128 * p + 128] for p in range(4)]
        h4 = [h_s[128 * p:128 * p + 128, :] for p in range(4)]
        ys, hs = _ssd_chunk(rev, dirn, g, x4, b_ref[...], c_ref[...], m_ref[...], r_ref[...],
                            bc_ref[...], ac_ref[...], br_ref[...], ar_ref[...], h4)
        for p in range(4):
            y_ref[:, 128 * p:128 * p + 128] = ys[p]
            h_s[128 * p:128 * p + 128, :] = hs[p]

    return pl.pallas_call(
        body, out_shape=[_sd((n, 1024), F32), _sd((2, nc, 512, 128), F32)], grid=(2, nc),
        in_specs=[sp["x"], sp["b"], sp["c"], sp["misc"], sp["dtrow"], sp["p_c"], sp["p_c"], sp["p_r"], sp["p_r"]],
        out_specs=[sp["y"], sp["hsave"]], scratch_shapes=[pltpu.VMEM((512, 128), F32)],
        name=name, compiler_params=_cp(("arbitrary", "arbitrary")))(
            xbc, xbc, xbc, u, dtrow, bias_c, alog_c, bias_r, alog_r)


def ssd_bwd(name, xbc, u, dtrow, bias_c, alog_c, bias_r, alog_r, hsave, dy, acc, nlat_chunks, rev, dirn):
    n = xbc.shape[0]
    sp = _ssd_specs(nlat_chunks, rev, dirn, True)

    def body(x_ref, b_ref, c_ref, m_ref, r_ref, bc_ref, ac_ref, br_ref, ar_ref, hs_ref, dy_ref, ax_ref, ab_ref, acc_ref,
             dx_ref, db_ref, dc_ref, dm_ref, dr_ref, dbc_ref, dac_ref, dbr_ref, dar_ref, dh_s):
        g = pl.program_id(0)
        s = pl.program_id(1)

        @pl.when(s == 0)
        def _():
            dh_s[...] = jnp.zeros((512, 128), F32)

        x4 = [x_ref[:, 128 * p:128 * p + 128] for p in range(4)]
        h4 = [hs_ref[0, 0, 128 * p:128 * p + 128, :] for p in range(4)]
        fn = functools.partial(_ssd_chunk, rev, dirn, g)
        _, vjp = jax.vjp(fn, x4, b_ref[...], c_ref[...], m_ref[...], r_ref[...],
                         bc_ref[...], ac_ref[...], br_ref[...], ar_ref[...], h4)
        dys = [dy_ref[:, 128 * p:128 * p + 128] for p in range(4)]
        dhs = [dh_s[128 * p:128 * p + 128, :] for p in range(4)]
        dx4, db, dc, dm, dr, dbc, dac, dbr, dar, dh4 = vjp((dys, dhs))
        for p in range(4):
            dx_ref[:, 128 * p:128 * p + 128] = dx4[p] + ax_ref[:, 128 * p:128 * p + 128]
            dh_s[128 * p:128 * p + 128, :] = dh4[p]
        db_ref[...] = db + ab_ref[...]
        dc_ref[...] = dc + acc_ref[...]
        dm_ref[0] = dm
        dr_ref[0] = dr
        first = (g == 0) & (s == 0)
        _acc(dbc_ref, dbc, first)
        _acc(dac_ref, dac, first)
        _acc(dbr_ref, dbr, first)
        _acc(dar_ref, dar, first)

    ax, ab, ac = acc
    return pl.pallas_call(
        body,
        out_shape=[_sd((n, 1024), F32), _sd((n, 256), F32), _sd((n, 256), F32), _sd((2, n, 128), F32),
                   _sd((2, 16, n), F32), _sd((1, 128), F32), _sd((1, 128), F32), _sd((16, 1), F32), _sd((16, 1), F32)],
        grid=(2, nlat_chunks + 2),
        in_specs=[sp["x"], sp["b"], sp["c"], sp["misc"], sp["dtrow"], sp["p_c"], sp["p_c"], sp["p_r"], sp["p_r"],
                  sp["hsave"], sp["y"], sp["y"], sp["bc_out"], sp["bc_out"]],
        out_specs=[sp["y"], sp["bc_out"], sp["bc_out"], sp["misc_out"], sp["dtrow_out"],
                   sp["pacc_c"], sp["pacc_c"], sp["pacc_r"], sp["pacc_r"]],
        scratch_shapes=[pltpu.VMEM((512, 128), F32)],
        name=name, compiler_params=_cp(("arbitrary", "arbitrary")))(
            xbc, xbc, xbc, u, dtrow, bias_c, alog_c, bias_r, alog_r, hsave, dy, ax, ab, ac)


def _ssd_out(yf, yb, xs, z, g, dexp):
    return _rms((yf + yb + dexp * xs) * _silu(z), g)


def ssd_out_fwd(name, yf, yb, xbc, u, g, dexp):
    n = yf.shape[0]

    def body(yf_ref, yb_ref, xs_ref, z_ref, g_ref, d_ref, o_ref):
        o_ref[...] = _ssd_out(yf_ref[...], yb_ref[...], xs_ref[...], z_ref[...], g_ref[...], d_ref[...]).astype(BF16)

    return _rowcall(name, body, n,
                    [(yf, _rs(D)), (yb, _rs(D)), (xbc, _rs(D, 0)), (u, _rs(D, C_Z // D)), (g, _ps((1, D))), (dexp, _ps((1, D)))],
                    [(_sd((n, D), BF16), _rs(D))])[0]


def ssd_out_bwd(name, yf, yb, xbc, u, g, dexp, dys):
    n = yf.shape[0]

    def body(yf_ref, yb_ref, xs_ref, z_ref, g_ref, d_ref, dys_ref, dy_ref, dxs_ref, dz_ref, dg_ref, dd_ref):
        i = pl.program_id(0)
        _, vjp = jax.vjp(_ssd_out, yf_ref[...], yb_ref[...], xs_ref[...], z_ref[...], g_ref[...], d_ref[...])
        dyf, _, dxs, dz, dg, dd = vjp(dys_ref[...])
        dy_ref[...] = dyf
        dxs_ref[...] = dxs
        dz_ref[...] = dz.astype(BF16)
        _acc(dg_ref, dg, i == 0)
        _acc(dd_ref, dd, i == 0)

    return _rowcall(name, body, n,
                    [(yf, _rs(D)), (yb, _rs(D)), (xbc, _rs(D, 0)), (u, _rs(D, C_Z // D)), (g, _ps((1, D))), (dexp, _ps((1, D))),
                     (dys, _rs(D))],
                    [(_sd((n, D), F32), _rs(D)), (_sd((n, D), F32), _rs(D)), (_sd((n, D), BF16), _rs(D)),
                     (_sd((1, D), F32), _ps((1, D))), (_sd((1, D), F32), _ps((1, D)))])


def _normrope(x, g, cos, sin, s, n=None):
    return _rope(_rms(x, g, n), cos, sin, s)


def swa_prep_fwd(name, u, gq, gk, cos, sin):
    n = u.shape[0]

    def body(q_ref, k_ref, gq_ref, gk_ref, cos_ref, sin_ref, qs_ref, ks_ref):
        cs, sn = cos_ref[...], sin_ref[...]
        for h in range(SWA_HQ):
            sl = slice(128 * h, 128 * h + 128)
            qs_ref[:, sl] = _normrope(q_ref[:, sl], gq_ref[...], cs, sn, 32).astype(BF16)
        for h in range(SWA_HKV):
            sl = slice(128 * h, 128 * h + 128)
            ks_ref[:, sl] = _normrope(k_ref[:, sl], gk_ref[...], cs, sn, 32).astype(BF16)

    return _rowcall(name, body, n,
                    [(u, _rs(1024, C_Q // 1024)), (u, _rs(256, C_K // 256)), (gq, _ps((1, 128))), (gk, _ps((1, 128))),
                     (cos, _rs(128)), (sin, _rs(128))],
                    [(_sd((n, 1024), BF16), _rs(1024)), (_sd((n, 256), BF16), _rs(256))])


def swa_prep_bwd(name, u, gq, gk, cos, sin, dqs, dks, dv):
    n = u.shape[0]

    def body(q_ref, k_ref, gq_ref, gk_ref, cos_ref, sin_ref, dqs_ref, dks_ref, dv_ref,
             dq_ref, dk_ref, dvo_ref, dgq_ref, dgk_ref):
        i = pl.program_id(0)
        cs, sn = cos_ref[...], sin_ref[...]
        fn = lambda x, g: _normrope(x, g, cs, sn, 32)
        dgq = jnp.zeros((1, 128), F32)
        dgk = jnp.zeros((1, 128), F32)
        for h in range(SWA_HQ):
            sl = slice(128 * h, 128 * h + 128)
            _, vjp = jax.vjp(fn, q_ref[:, sl], gq_ref[...])
            dx, dg = vjp(dqs_ref[:, sl])
            dq_ref[:, sl] = dx.astype(BF16)
            dgq = dgq + dg
        for h in range(SWA_HKV):
            sl = slice(128 * h, 128 * h + 128)
            _, vjp = jax.vjp(fn, k_ref[:, sl], gk_ref[...])
            dx, dg = vjp(dks_ref[:, sl])
            dk_ref[:, sl] = dx.astype(BF16)
            dgk = dgk + dg
        dvo_ref[...] = dv_ref[...].astype(BF16)
        _acc(dgq_ref, dgq, i == 0)
        _acc(dgk_ref, dgk, i == 0)

    return _rowcall(name, body, n,
                    [(u, _rs(1024, C_Q // 1024)), (u, _rs(256, C_K // 256)), (gq, _ps((1, 128))), (gk, _ps((1, 128))),
                     (cos, _rs(128)), (sin, _rs(128)), (dqs, _rs(1024)), (dks, _rs(256)), (dv, _rs(256))],
                    [(_sd((n, 1024), BF16), _rs(1024)), (_sd((n, 256), BF16), _rs(256)), (_sd((n, 256), BF16), _rs(256)),
                     (_sd((1, 128), F32), _ps((1, 128))), (_sd((1, 128), F32), _ps((1, 128)))])


def lat_norm_fwd(name, u, g_kv, g_q):
    n = u.shape[0]

    def body(ckv_ref, cq_ref, gkv_ref, gq_ref, okv_ref, oq_ref):
        okv_ref[...] = _rms(ckv_ref[...], gkv_ref[...]).astype(BF16)
        oq_ref[...] = _rms(cq_ref[...], gq_ref[...]).astype(BF16)

    return _rowcall(name, body, n,
                    [(u, _rs(256, C_CKV // 256)), (u, _rs(384, C_CQ // 384)), (g_kv, _ps((1, 256))), (g_q, _ps((1, 384)))],
                    [(_sd((n, 256), BF16), _rs(256)), (_sd((n, 384), BF16), _rs(384))])


def lat_norm_bwd(name, u, g_kv, g_q, dkvn, dqn):
    n = u.shape[0]

    def body(ckv_ref, cq_ref, gkv_ref, gq_ref, dkvn_ref, dqn_ref, dckv_ref, dcq_ref, dgkv_ref, dgq_ref):
        i = pl.program_id(0)
        _, vjp = jax.vjp(_rms, ckv_ref[...], gkv_ref[...])
        dx, dg = vjp(dkvn_ref[...])
        dckv_ref[...] = dx.astype(BF16)
        _acc(dgkv_ref, dg, i == 0)
        _, vjp = jax.vjp(_rms, cq_ref[...], gq_ref[...])
        dx, dg = vjp(dqn_ref[...])
        dcq_ref[...] = dx.astype(BF16)
        _acc(dgq_ref, dg, i == 0)

    return _rowcall(name, body, n,
                    [(u, _rs(256, C_CKV // 256)), (u, _rs(384, C_CQ // 384)), (g_kv, _ps((1, 256))), (g_q, _ps((1, 384))),
                     (dkvn, _rs(256)), (dqn, _rs(384))],
                    [(_sd((n, 256), BF16), _rs(256)), (_sd((n, 384), BF16), _rs(384)),
                     (_sd((1, 256), F32), _ps((1, 256))), (_sd((1, 384), F32), _ps((1, 384)))])


def _lane_lt64(x):
    return (lax.broadcasted_iota(jnp.int32, (1, 128), 1) < 64).astype(F32) * x


def _mla_krope(misc, g, cos, sin):
    return _normrope(_lane_lt64(misc), g, cos, sin, 16, MLA_ROPE)


def mla_prep_fwd(name, kv, qp, u, qg, kg, cos, sin):
    n = kv.shape[0]

    def body(kv_ref, v_ref, q_ref, m_ref, qg_ref, kg_ref, cos_ref, sin_ref, km_ref, qm_ref, vm_ref):
        cs, sn = cos_ref[...], sin_ref[...]
        vm_ref[...] = v_ref[...].astype(BF16)
        kr = _mla_krope(m_ref[...], kg_ref[:, 128:256], cs, sn).astype(BF16)
        for h in range(MLA_H):
            km_ref[:, 256 * h:256 * h + 128] = _rms(kv_ref[:, 128 * h:128 * h + 128], kg_ref[:, 0:128]).astype(BF16)
            km_ref[:, 256 * h + 128:256 * h + 256] = kr
            qm_ref[:, 256 * h:256 * h + 128] = _rms(q_ref[:, 256 * h:256 * h + 128], qg_ref[:, 0:128]).astype(BF16)
            qm_ref[:, 256 * h + 128:256 * h + 256] = _normrope(
                q_ref[:, 256 * h + 128:256 * h + 256], qg_ref[:, 128:256], cs, sn, 16, MLA_ROPE).astype(BF16)

    return _rowcall(name, body, n,
                    [(kv, _rs(1024, 0)), (kv, _rs(1024, 1)), (qp, _rs(2048)), (u, _rs(128, C_MISC // 128)), (qg, _ps((1, 256))),
                     (kg, _ps((1, 256))), (cos, _rs(128)), (sin, _rs(128))],
                    [(_sd((n, 2048), BF16), _rs(2048)), (_sd((n, 2048), BF16), _rs(2048)), (_sd((n, 1024), BF16), _rs(1024))])


def mla_prep_bwd(name, kv, qp, u, qg, kg, cos, sin, dkm, dqm, dv):
    n = kv.shape[0]

    def body(kv_ref, q_ref, m_ref, qg_ref, kg_ref, cos_ref, sin_ref, dkm_ref, dqm_ref, dv_ref,
             dkv_ref, dq_ref, dkr_ref, dqg_ref, dkg_ref):
        i = pl.program_id(0)
        cs, sn = cos_ref[...], sin_ref[...]
        fr = lambda x, g: _normrope(x, g, cs, sn, 16, MLA_ROPE)
        dkg_n = jnp.zeros((1, 128), F32)
        dqg_n = jnp.zeros((1, 128), F32)
        dqg_r = jnp.zeros((1, 128), F32)
        dkr_sum = jnp.zeros((RT, 128), F32)
        for h in range(MLA_H):
            _, vjp = jax.vjp(_rms, kv_ref[:, 128 * h:128 * h + 128], kg_ref[:, 0:128])
            dx, dg = vjp(dkm_ref[:, 256 * h:256 * h + 128])
            dkv_ref[:, 128 * h:128 * h + 128] = dx.astype(BF16)
            dkg_n = dkg_n + dg
            dkr_sum = dkr_sum + dkm_ref[:, 256 * h + 128:256 * h + 256]
            _, vjp = jax.vjp(_rms, q_ref[:, 256 * h:256 * h + 128], qg_ref[:, 0:128])
            dx, dg = vjp(dqm_ref[:, 256 * h:256 * h + 128])
            dq_ref[:, 256 * h:256 * h + 128] = dx.astype(BF16)
            dqg_n = dqg_n + dg
            _, vjp = jax.vjp(fr, q_ref[:, 256 * h + 128:256 * h + 256], qg_ref[:, 128:256])
            dx, dg = vjp(dqm_ref[:, 256 * h + 128:256 * h + 256])
            dq_ref[:, 256 * h + 128:256 * h + 256] = dx.astype(BF16)
            dqg_r = dqg_r + dg
        _, vjp = jax.vjp(lambda m, g: _mla_krope(m, g, cs, sn), m_ref[...], kg_ref[:, 128:256])
        dm, dkg_r = vjp(dkr_sum)
        dkr_ref[...] = dm
        dkv_ref[:, 1024:2048] = dv_ref[...].astype(BF16)
        _acc(dqg_ref.at[:, 0:128], dqg_n, i == 0)
        _acc(dqg_ref.at[:, 128:256], dqg_r, i == 0)
        _acc(dkg_ref.at[:, 0:128], dkg_n, i == 0)
        _acc(dkg_ref.at[:, 128:256], dkg_r, i == 0)

    return _rowcall(name, body, n,
                    [(kv, _rs(1024, 0)), (qp, _rs(2048)), (u, _rs(128, C_MISC // 128)), (qg, _ps((1, 256))), (kg, _ps((1, 256))),
                     (cos, _rs(128)), (sin, _rs(128)), (dkm, _rs(2048)), (dqm, _rs(2048)), (dv, _rs(1024))],
                    [(_sd((n, 2048), BF16), _rs(2048)), (_sd((n, 2048), BF16), _rs(2048)), (_sd((n, 128), F32), _rs(128)),
                     (_sd((1, 256), F32), _ps((1, 256))), (_sd((1, 256), F32), _ps((1, 256)))])


def misc_combine(name, dkr, dm_f, dm_b, drow_t):
    n = dkr.shape[0]

    def body(a_ref, f_ref, b_ref, r_ref, o_ref):
        o_ref[...] = (a_ref[...] + f_ref[0] + f_ref[1] + b_ref[0] + b_ref[1] + r_ref[...]).astype(BF16)

    g2 = pl.BlockSpec((2, RT, 128), lambda i: (0, i, 0))
    return _rowcall(name, body, n, [(dkr, _rs(128)), (dm_f, g2), (dm_b, g2), (drow_t, _rs(128))],
                    [(_sd((n, 128), BF16), _rs(128))])[0]


def _f32(ref):
    return ref[...].astype(F32)


def _merge(g1, g2, g3, p1, p2, p3):
    return jax.nn.sigmoid(g1) * p1 + jax.nn.sigmoid(g2) * p2 + jax.nn.sigmoid(g3) * p3


def merge_fwd(name, u, p1, p2, p3):
    n = u.shape[0]

    def body(g1, g2, g3, a, b, c, o_ref):
        o_ref[...] = _merge(g1[...], g2[...], g3[...], _f32(a), _f32(b), _f32(c)).astype(BF16)

    return _rowcall(name, body, n, [(u, _rs(D, 0)), (u, _rs(D, 1)), (u, _rs(D, 2)), (p1, _rs(D)), (p2, _rs(D)), (p3, _rs(D))],
                    [(_sd((n, D), BF16), _rs(D))])[0]


def merge_bwd(name, u, p1, p2, p3, dm):
    n = u.shape[0]

    def body(g1, g2, g3, a, b, c, dm_ref, d1, d2, d3, dg_ref):
        _, vjp = jax.vjp(_merge, g1[...], g2[...], g3[...], _f32(a), _f32(b), _f32(c))
        r = vjp(dm_ref[...])
        for k in range(3):
            dg_ref[:, D * k:D * k + D] = r[k].astype(BF16)
        d1[...] = r[3].astype(BF16)
        d2[...] = r[4].astype(BF16)
        d3[...] = r[5].astype(BF16)

    return _rowcall(name, body, n,
                    [(u, _rs(D, 0)), (u, _rs(D, 1)), (u, _rs(D, 2)), (p1, _rs(D)), (p2, _rs(D)), (p3, _rs(D)), (dm, _rs(D))],
                    [(_sd((n, D), BF16), _rs(D))] * 3 + [(_sd((n, 3 * D), BF16), _rs(3 * D))])


def _swiglu(g, u):
    return _silu(g) * u


def swiglu_fwd(name, gu):
    n = gu.shape[0]

    def body(g_ref, u_ref, o_ref):
        o_ref[...] = _swiglu(_f32(g_ref), _f32(u_ref)).astype(BF16)

    return _rowcall(name, body, n, [(gu, _rs(FFN, 0)), (gu, _rs(FFN, 1))], [(_sd((n, FFN), BF16), _rs(FFN))])[0]


def swiglu_bwd(name, gu, da):
    n = gu.shape[0]

    def body(g_ref, u_ref, da_ref, o_ref):
        _, vjp = jax.vjp(_swiglu, _f32(g_ref), _f32(u_ref))
        dg, du = vjp(da_ref[...])
        o_ref[:, 0:FFN] = dg.astype(BF16)
        o_ref[:, FFN:2 * FFN] = du.astype(BF16)

    return _rowcall(name, body, n, [(gu, _rs(FFN, 0)), (gu, _rs(FFN, 1)), (da, _rs(FFN))],
                    [(_sd((n, 2 * FFN), BF16), _rs(2 * FFN))])[0]


FLASH_ROWS = 256


def _fold_lanes(x, op):
    acc = x[:, 0:128]
    for b in range(1, x.shape[1] // 128):
        acc = op(acc, x[:, 128 * b:128 * b + 128])
    return acc


def _band_mask(tq, tk, i, kb):
    qp = i * tq + lax.broadcasted_iota(jnp.int32, (tq, tk), 0)
    kp = kb * tk + lax.broadcasted_iota(jnp.int32, (tq, tk), 1)
    return jnp.abs(qp - kp) <= SWA_WIN


def flash_fwd(name, qa, ka, va, *, w, vw, hq, grp, vcol0, scale, nlat, tq, tk, band, sink, ctx_q, prev=None):
    n = qa.shape[0]
    cblk = nlat // NCTX
    band = band and not ctx_q
    assert not band, "latent rows of a banded attention go through swa_fwd_lat"
    if ctx_q:
        tq = tk = NCTX
        grid = (hq, 1, 1)
        qmap = lambda h, i, kk: (cblk, h)
        kmap = lambda h, i, kk: (cblk, h // grp)
        vmap = lambda h, i, kk: (cblk, vcol0 + h // grp)
        omap = lambda h, i, kk: (cblk, h)
        lmap = lambda h, i, kk: (h, cblk, 0)
    else:
        nb = nlat // tk
        nk = 3 if band else nb
        grid = (hq, nlat // tq, nk)
        kb_of = (lambda i, kk: jnp.clip(i + kk - 1, 0, nb - 1)) if band else (lambda i, kk: kk)
        qmap = lambda h, i, kk: (i, h)
        kmap = lambda h, i, kk: (kb_of(i, kk), h // grp)
        vmap = lambda h, i, kk: (kb_of(i, kk), vcol0 + h // grp)
        omap = lambda h, i, kk: (i, h)
        lmap = lambda h, i, kk: (h, i, 0)
    nk = grid[2]
    extra = not ctx_q
    has_sink = sink is not None

    def body(*refs):
        refs = list(refs)
        q_ref, k_ref, v_ref = refs[:3]
        pos = 3
        if extra:
            ke_ref, ve_ref = refs[pos:pos + 2]
            pos += 2
        if has_sink:
            s_ref = refs[pos]
            pos += 1
        if prev is not None:
            pos += 2
        o_ref, l_ref, m_s, l_s, a_s = refs[pos:pos + 5]
        kk = pl.program_id(2)
        tr = min(tq, FLASH_ROWS)

        def step(kblk, vblk):
            for r in range(tq // tr):
                rows = slice(r * tr, (r + 1) * tr)
                s = _d(q_ref[rows, :], kblk, ((1,), (1,))) * (scale * LOG2E)
                m_prev = m_s[rows, :]
                m_new = jnp.maximum(m_prev, jnp.max(_fold_lanes(s, jnp.maximum), axis=1, keepdims=True))
                alpha = jnp.exp2(m_prev - m_new)
                p = jnp.exp2(s - m_new)
                l_s[rows, :] = alpha * l_s[rows, :] + _fold_lanes(p, jnp.add)
                a_s[rows, :] = alpha * a_s[rows, :] + _d(p, vblk, ((1,), (0,)))
                m_s[rows, :] = m_new

        @pl.when(kk == 0)
        def _():
            if has_sink:
                sv = jnp.max(s_ref[0], axis=1, keepdims=True) * LOG2E
                m_s[...] = jnp.zeros((tq, 1), F32) + sv
                l_s[...] = (lax.broadcasted_iota(jnp.int32, (tq, 128), 1) == 0).astype(F32)
            else:
                m_s[...] = jnp.full((tq, 1), NEG, F32)
                l_s[...] = jnp.zeros((tq, 128), F32)
            a_s[...] = jnp.zeros((tq, vw), F32)
            if extra:
                step(ke_ref[...], ve_ref[...])

        step(k_ref[...], v_ref[...])

        @pl.when(kk == nk - 1)
        def _():
            l = jnp.sum(l_s[...], axis=1, keepdims=True)
            o_ref[...] = (a_s[...] / l).astype(BF16)
            l_ref[0] = m_s[...] + jnp.log2(l)

    ins = [(qa, pl.BlockSpec((tq, w), qmap)), (ka, pl.BlockSpec((tk, w), kmap)), (va, pl.BlockSpec((tk, vw), vmap))]
    if extra:
        ins += [(ka, pl.BlockSpec((NCTX, w), lambda h, i, kk: (cblk, h // grp))),
                (va, pl.BlockSpec((NCTX, vw), lambda h, i, kk: (cblk, vcol0 + h // grp)))]
    if has_sink:
        ins += [(sink, pl.BlockSpec((1, 1, 128), lambda h, i, kk: (h, 0, 0)))]
    aliases = {}
    if prev is not None:
        any_spec = pl.BlockSpec(memory_space=pl.ANY)
        aliases = {len(ins): 0, len(ins) + 1: 1}
        ins += [(prev[0], any_spec), (prev[1], any_spec)]
    return pl.pallas_call(
        body, out_shape=[_sd((n, hq * vw), BF16), _sd((hq, n, 1), F32)], grid=grid,
        in_specs=[s for _, s in ins],
        out_specs=[pl.BlockSpec((tq, vw), omap), pl.BlockSpec((1, tq, 1), lmap)],
        scratch_shapes=[pltpu.VMEM((tq, 1), F32), pltpu.VMEM((tq, 128), F32), pltpu.VMEM((tq, vw), F32)],
        input_output_aliases=aliases, name=name,
        compiler_params=_cp(("parallel", "parallel", "arbitrary")))(*[a for a, _ in ins])


def flash_dq(name, qa, ka, va, oa, doa, lse, *, w, vw, hq, grp, vcol0, scale, nlat, tq, tk, band, sink, ctx_q, prev=None):
    n = qa.shape[0]
    cblk = nlat // NCTX
    band = band and not ctx_q
    if ctx_q:
        tq = tk = NCTX
        grid = (hq, 1, 1)
        qmap = lambda h, i, kk: (cblk, h)
        kmap = lambda h, i, kk: (cblk, h // grp)
        vmap = lambda h, i, kk: (cblk, vcol0 + h // grp)
        lmap = lambda h, i, kk: (h, cblk, 0)
    else:
        nb = nlat // tk
        grid = (hq, nlat // tq, 3 if band else nb)
        kb_of = (lambda i, kk: jnp.clip(i + kk - 1, 0, nb - 1)) if band else (lambda i, kk: kk)
        qmap = lambda h, i, kk: (i, h)
        kmap = lambda h, i, kk: (kb_of(i, kk), h // grp)
        vmap = lambda h, i, kk: (kb_of(i, kk), vcol0 + h // grp)
        lmap = lambda h, i, kk: (h, i, 0)
    nk = grid[2]
    nq = grid[1]
    extra = not ctx_q
    has_sink = sink is not None

    def body(*refs):
        refs = list(refs)
        q_ref, k_ref, v_ref, o_ref, do_ref, l_ref = refs[:6]
        pos = 6
        if extra:
            ke_ref, ve_ref = refs[pos:pos + 2]
            pos += 2
        if has_sink:
            s_ref = refs[pos]
            pos += 1
        if prev is not None:
            pos += 2
        dq_ref, dl_ref, ds_ref, acc_s, dl_s = refs[pos:pos + 5]
        i = pl.program_id(1)
        kk = pl.program_id(2)
        q = q_ref[...]
        do = do_ref[...]
        lse_v = l_ref[0]

        def step(kblk, vblk, mask):
            s = _d(q, kblk, ((1,), (1,))) * (scale * LOG2E)
            if mask is not None:
                s = jnp.where(mask, s, NEG)
            p = jnp.exp2(s - lse_v)
            dp = _d(do, vblk, ((1,), (1,)))
            ds = p * (dp - dl_s[...]) * scale
            acc_s[...] += _d(ds, kblk, ((1,), (0,)))

        @pl.when(kk == 0)
        def _():
            delta = jnp.sum(do * o_ref[...].astype(F32), axis=1, keepdims=True)
            dl_s[...] = delta
            acc_s[...] = jnp.zeros((tq, w), F32)
            if has_sink:
                sv = jnp.max(s_ref[0], axis=1, keepdims=True) * LOG2E
                dsk = jnp.sum(-jnp.exp2(sv - lse_v) * delta, axis=0, keepdims=True)
                _acc(ds_ref, jnp.zeros((1, 1, 128), F32) + dsk, i == 0)
            else:
                ds_ref[...] = jnp.zeros((1, 1, 128), F32)
            if extra:
                step(ke_ref[...], ve_ref[...], None)

        if band:
            kb = i + kk - 1

            @pl.when((kb >= 0) & (kb < nlat // tk))
            def _():
                step(k_ref[...], v_ref[...], _band_mask(tq, tk, i, kb))
        else:
            step(k_ref[...], v_ref[...], None)

        @pl.when(kk == nk - 1)
        def _():
            dq_ref[...] = acc_s[...]
            dl_ref[0] = dl_s[...]

    ins = [(qa, pl.BlockSpec((tq, w), qmap)), (ka, pl.BlockSpec((tk, w), kmap)), (va, pl.BlockSpec((tk, vw), vmap)),
           (oa, pl.BlockSpec((tq, vw), qmap)), (doa, pl.BlockSpec((tq, vw), qmap)), (lse, pl.BlockSpec((1, tq, 1), lmap))]
    if extra:
        ins += [(ka, pl.BlockSpec((NCTX, w), lambda h, i, kk: (cblk, h // grp))),
                (va, pl.BlockSpec((NCTX, vw), lambda h, i, kk: (cblk, vcol0 + h // grp)))]
    if has_sink:
        ins += [(sink, pl.BlockSpec((1, 1, 128), lambda h, i, kk: (h, 0, 0)))]
    aliases = {}
    if prev is not None:
        any_spec = pl.BlockSpec(memory_space=pl.ANY)
        aliases = {len(ins): 0, len(ins) + 1: 1}
        ins += [(prev[0], any_spec), (prev[1], any_spec)]
    del nq
    return pl.pallas_call(
        body, out_shape=[_sd((n, hq * w), F32), _sd((hq, n, 1), F32), _sd((hq, 1, 128), F32)], grid=grid,
        in_specs=[s for _, s in ins],
        out_specs=[pl.BlockSpec((tq, w), qmap), pl.BlockSpec((1, tq, 1), lmap),
                   pl.BlockSpec((1, 1, 128), lambda h, i, kk: (h, 0, 0))],
        scratch_shapes=[pltpu.VMEM((tq, w), F32), pltpu.VMEM((tq, 1), F32)],
        input_output_aliases=aliases, name=name,
        compiler_params=_cp(("parallel", "arbitrary", "arbitrary")))(*[a for a, _ in ins])


def flash_dkv(name, qa, ka, va, doa, lse, delta, *, w, vw, hkv, grp, vcol0, scale, nlat, tq, tk, band, ctx_k, prev=None):
    n = qa.shape[0]
    cblk = nlat // NCTX
    nqb = nlat // tq
    band = band and not ctx_k
    if ctx_k:
        tk = NCTX
        nqs = nqb
        grid = (hkv, 1, grp * nqs)
        kmap = lambda hk, j, t: (cblk, hk)
        vmap = lambda hk, j, t: (cblk, vcol0 + hk)
        dvmap = lambda hk, j, t: (cblk, hk)
        qb_of = lambda j, t: t % nqs
    else:
        nqs = 3 if band else nqb
        grid = (hkv, nlat // tk, grp * nqs)
        kmap = lambda hk, j, t: (j, hk)
        vmap = lambda hk, j, t: (j, vcol0 + hk)
        dvmap = lambda hk, j, t: (j, hk)
        qb_of = (lambda j, t: jnp.clip(j + t % nqs - 1, 0, nqb - 1)) if band else (lambda j, t: t % nqs)
    qmap = lambda hk, j, t: (qb_of(j, t), hk * grp + t // nqs)
    lmap = lambda hk, j, t: (hk * grp + t // nqs, qb_of(j, t), 0)

    def body(*refs):
        refs = list(refs)
        q_ref, k_ref, v_ref, do_ref, l_ref, dl_ref = refs[:6]
        pos = 6
        if ctx_k:
            qe_ref, doe_ref, le_ref, dle_ref = refs[pos:pos + 4]
            pos += 4
        if prev is not None:
            pos += 2
        dk_ref, dv_ref = refs[pos:pos + 2]
        j = pl.program_id(1)
        t = pl.program_id(2)
        kblk = k_ref[...]
        vblk = v_ref[...]

        def contrib(q, do, lse_v, dl_v, mask):
            s = _d(q, kblk, ((1,), (1,))) * (scale * LOG2E)
            if mask is not None:
                s = jnp.where(mask, s, NEG)
            p = jnp.exp2(s - lse_v)
            dp = _d(do, vblk, ((1,), (1,)))
            ds = p * (dp - dl_v) * scale
            return _d(ds, q, ((0,), (0,))), _d(p, do, ((0,), (0,)))

        @pl.when(t == 0)
        def _():
            dk = jnp.zeros((tk, w), F32)
            dv = jnp.zeros((tk, vw), F32)
            if ctx_k:
                for gi in range(grp):
                    a, b = contrib(qe_ref[:, w * gi:w * gi + w], doe_ref[:, vw * gi:vw * gi + vw], le_ref[gi], dle_ref[gi], None)
                    dk = dk + a
                    dv = dv + b
            dk_ref[...] = dk
            dv_ref[...] = dv

        def add(mask):
            a, b = contrib(q_ref[...], do_ref[...], l_ref[0], dl_ref[0], mask)
            dk_ref[...] += a
            dv_ref[...] += b

        if band:
            qb = j + t % nqs - 1

            @pl.when((qb >= 0) & (qb < nqb))
            def _():
                add(_band_mask(tq, tk, qb, j))
        else:
            add(None)

    ins = [(qa, pl.BlockSpec((tq, w), qmap)), (ka, pl.BlockSpec((tk, w), kmap)), (va, pl.BlockSpec((tk, vw), vmap)),
           (doa, pl.BlockSpec((tq, vw), qmap)), (lse, pl.BlockSpec((1, tq, 1), lmap)), (delta, pl.BlockSpec((1, tq, 1), lmap))]
    if ctx_k:
        ins += [(qa, pl.BlockSpec((NCTX, grp * w), lambda hk, j, t: (cblk, hk))),
                (doa, pl.BlockSpec((NCTX, grp * vw), lambda hk, j, t: (cblk, hk))),
                (lse, pl.BlockSpec((grp, NCTX, 1), lambda hk, j, t: (hk, cblk, 0))),
                (delta, pl.BlockSpec((grp, NCTX, 1), lambda hk, j, t: (hk, cblk, 0)))]
    aliases = {}
    if prev is not None:
        any_spec = pl.BlockSpec(memory_space=pl.ANY)
        aliases = {len(ins): 0, len(ins) + 1: 1}
        ins += [(prev[0], any_spec), (prev[1], any_spec)]
    return pl.pallas_call(
        body, out_shape=[_sd((n, hkv * w), F32), _sd((n, hkv * vw), F32)], grid=grid,
        in_specs=[s for _, s in ins],
        out_specs=[pl.BlockSpec((tk, w), kmap), pl.BlockSpec((tk, vw), dvmap)],
        input_output_aliases=aliases, name=name,
        compiler_params=_cp(("parallel", "parallel", "arbitrary")))(*[a for a, _ in ins])


def mla_fwd(name, qm, km, vm, nlat):
    n = qm.shape[0]
    t = NCTX
    nlt = nlat // t
    c = (MLA_NOPE + MLA_ROPE) ** -0.5 * LOG2E

    def body(q_ref, k_ref, v_ref, o_ref, l_ref):
        i = pl.program_id(1)

        def run(k, v):
            s = _d(q_ref[...], k, ((1,), (1,))) * c
            m = jnp.max(_fold_lanes(s, jnp.maximum), axis=1, keepdims=True)
            p = jnp.exp2(s - m)
            l = jnp.sum(_fold_lanes(p, jnp.add), axis=1, keepdims=True)
            o_ref[...] = (_d(p, v, ((1,), (0,))) / l).astype(BF16)
            l_ref[0] = m + jnp.log2(l)

        @pl.when(i < nlt)
        def _():
            run(k_ref[...], v_ref[...])

        @pl.when(i == nlt)
        def _():
            run(k_ref[nlat:n, :], v_ref[nlat:n, :])

    return pl.pallas_call(
        body, out_shape=[_sd((n, MLA_H * 128), BF16), _sd((MLA_H, n, 1), F32)], grid=(MLA_H, n // t),
        in_specs=[pl.BlockSpec((t, 256), lambda h, i: (i, h)), pl.BlockSpec((n, 256), lambda h, i: (0, h)),
                  pl.BlockSpec((n, 128), lambda h, i: (0, h))],
        out_specs=[pl.BlockSpec((t, 128), lambda h, i: (i, h)), pl.BlockSpec((1, t, 1), lambda h, i: (h, i, 0))],
        name=name, compiler_params=_cp(("parallel", "arbitrary")))(qm, km, vm)


def mla_bwd(name, qm, km, vm, o, do, lse, nlat):
    n = qm.shape[0]
    t = NCTX
    nlt = nlat // t
    scale = (MLA_NOPE + MLA_ROPE) ** -0.5
    nchunk = 2 if (n // 128) % 2 == 0 else 1
    cw = n // nchunk

    def body(q_ref, k_ref, v_ref, o_ref, do_ref, l_ref, dq_ref, dkt_ref, dvt_ref):
        i = pl.program_id(1)

        @pl.when(i == 0)
        def _():
            dkt_ref[...] = jnp.zeros((256, n), F32)
            dvt_ref[...] = jnp.zeros((128, n), F32)

        q = q_ref[...]
        do = do_ref[...]
        delta = jnp.sum(do.astype(F32) * o_ref[...].astype(F32), axis=1, keepdims=True)

        def run(spans):
            dq = jnp.zeros((t, 256), F32)
            for a, b in spans:
                kc = k_ref[a:b, :]
                s = _d(q, kc, ((1,), (1,))) * (scale * LOG2E)
                p = jnp.exp2(s - l_ref[0])
                ds = (p * (_d(do, v_ref[a:b, :], ((1,), (1,))) - delta) * scale).astype(BF16)
                dq = dq + _d(ds, kc, ((1,), (0,)))
                dkt_ref[:, a:b] += _d(q, ds, ((0,), (0,)))
                dvt_ref[:, a:b] += _d(do, p, ((0,), (0,)))
            dq_ref[...] = dq

        @pl.when(i < nlt)
        def _():
            run([(c * cw, (c + 1) * cw) for c in range(nchunk)])

        @pl.when(i == nlt)
        def _():
            run([(nlat, n)])

    qspec = pl.BlockSpec((t, 256), lambda h, i: (i, h))
    ospec = pl.BlockSpec((t, 128), lambda h, i: (i, h))
    return pl.pallas_call(
        body, out_shape=[_sd((n, MLA_H * 256), F32), _sd((MLA_H * 256, n), F32), _sd((MLA_H * 128, n), F32)],
        grid=(MLA_H, n // t),
        in_specs=[qspec, pl.BlockSpec((n, 256), lambda h, i: (0, h)), pl.BlockSpec((n, 128), lambda h, i: (0, h)),
                  ospec, ospec, pl.BlockSpec((1, t, 1), lambda h, i: (h, i, 0))],
        out_specs=[qspec, pl.BlockSpec((256, n), lambda h, i: (h, 0)), pl.BlockSpec((128, n), lambda h, i: (h, 0))],
        name=name, compiler_params=_cp(("parallel", "arbitrary")))(qm, km, vm, o, do, lse)


def mla_attention_bwd(tag, qm, km, vm, o, do, lse, nlat):
    dq, dkt, dvt = mla_bwd(tag + "_bwd", qm, km, vm, o, do, lse, nlat)
    return dq, jnp.transpose(dkt), jnp.transpose(dvt)


SWA_T = 512


def _swa_window(t, nlat):
    t = min(t, nlat)
    return t, min(t + 2 * SWA_WIN, nlat)


def _win_start(i, t, wlen, nlat):
    return pl.multiple_of(jnp.clip(i * t - SWA_WIN, 0, nlat - wlen), 128)


def _win_mask(rows, cols, row0, col0):
    rp = row0 + lax.broadcasted_iota(jnp.int32, (rows, cols), 0)
    cp = col0 + lax.broadcasted_iota(jnp.int32, (rows, cols), 1)
    return jnp.abs(rp - cp) <= SWA_WIN


def swa_fwd_lat(name, qs, ks, u, sink, nlat):
    n = qs.shape[0]
    tq, wlen = _swa_window(SWA_T, nlat)
    grp = SWA_HQ // SWA_HKV
    scale = SWA_DH ** -0.5
    vcol0 = C_V // 128

    def body(q_ref, k_ref, v_ref, s_ref, o_ref, l_ref):
        i = pl.program_id(1)
        ws = _win_start(i, tq, wlen, nlat)
        q = q_ref[...]
        s1 = _d(q, k_ref[pl.ds(ws, wlen), :], ((1,), (1,))) * (scale * LOG2E)
        s1 = jnp.where(_win_mask(tq, wlen, i * tq, ws), s1, NEG)
        s2 = _d(q, k_ref[pl.ds(nlat, NCTX), :], ((1,), (1,))) * (scale * LOG2E)
        sv = jnp.max(s_ref[0], axis=1, keepdims=True) * LOG2E
        m = jnp.maximum(jnp.maximum(jnp.max(s1, axis=1, keepdims=True), jnp.max(s2, axis=1, keepdims=True)), sv)
        p1 = jnp.exp2(s1 - m)
        p2 = jnp.exp2(s2 - m)
        l = jnp.sum(p1, axis=1, keepdims=True) + jnp.sum(p2, axis=1, keepdims=True) + jnp.exp2(sv - m)
        acc = _d(p1, v_ref[pl.ds(ws, wlen), :], ((1,), (0,))) + _d(p2, v_ref[pl.ds(nlat, NCTX), :], ((1,), (0,)))
        o_ref[...] = (acc / l).astype(BF16)
        l_ref[0] = m + jnp.log2(l)

    return pl.pallas_call(
        body, out_shape=[_sd((n, SWA_HQ * 128), BF16), _sd((SWA_HQ, n, 1), F32)], grid=(SWA_HQ, nlat // tq),
        in_specs=[pl.BlockSpec((tq, 128), lambda h, i: (i, h)), pl.BlockSpec((n, 128), lambda h, i: (0, h // grp)),
                  pl.BlockSpec((n, 128), lambda h, i: (0, vcol0 + h // grp)), pl.BlockSpec((1, 1, 128), lambda h, i: (h, 0, 0))],
        out_specs=[pl.BlockSpec((tq, 128), lambda h, i: (i, h)), pl.BlockSpec((1, tq, 1), lambda h, i: (h, i, 0))],
        name=name, compiler_params=_cp(("parallel", "arbitrary")))(qs, ks, u, sink)


def swa_dq_lat(name, qs, ks, u, o, do, lse, sink, nlat):
    n = qs.shape[0]
    tq, wlen = _swa_window(SWA_T, nlat)
    grp = SWA_HQ // SWA_HKV
    scale = SWA_DH ** -0.5
    vcol0 = C_V // 128

    def body(q_ref, k_ref, v_ref, s_ref, o_ref, do_ref, l_ref, dq_ref, dl_ref, ds_ref):
        i = pl.program_id(1)
        ws = _win_start(i, tq, wlen, nlat)
        q = q_ref[...]
        do = do_ref[...]
        lse_v = l_ref[0]
        delta = jnp.sum(do.astype(F32) * o_ref[...].astype(F32), axis=1, keepdims=True)
        kw = k_ref[pl.ds(ws, wlen), :]
        kc = k_ref[pl.ds(nlat, NCTX), :]
        s1 = _d(q, kw, ((1,), (1,))) * (scale * LOG2E)
        s1 = jnp.where(_win_mask(tq, wlen, i * tq, ws), s1, NEG)
        s2 = _d(q, kc, ((1,), (1,))) * (scale * LOG2E)
        ds1 = jnp.exp2(s1 - lse_v) * (_d(do, v_ref[pl.ds(ws, wlen), :], ((1,), (1,))) - delta) * scale
        ds2 = jnp.exp2(s2 - lse_v) * (_d(do, v_ref[pl.ds(nlat, NCTX), :], ((1,), (1,))) - delta) * scale
        dq_ref[...] = _d(ds1, kw, ((1,), (0,))) + _d(ds2, kc, ((1,), (0,)))
        dl_ref[0] = delta
        sv = jnp.max(s_ref[0], axis=1, keepdims=True) * LOG2E
        dsk = jnp.sum(-jnp.exp2(sv - lse_v) * delta, axis=0, keepdims=True)
        _acc(ds_ref, jnp.zeros((1, 1, 128), F32) + dsk, i == 0)

    qspec = pl.BlockSpec((tq, 128), lambda h, i: (i, h))
    lspec = pl.BlockSpec((1, tq, 1), lambda h, i: (h, i, 0))
    return pl.pallas_call(
        body, out_shape=[_sd((n, SWA_HQ * 128), F32), _sd((SWA_HQ, n, 1), F32), _sd((SWA_HQ, 1, 128), F32)],
        grid=(SWA_HQ, nlat // tq),
        in_specs=[qspec, pl.BlockSpec((n, 128), lambda h, i: (0, h // grp)),
                  pl.BlockSpec((n, 128), lambda h, i: (0, vcol0 + h // grp)), pl.BlockSpec((1, 1, 128), lambda h, i: (h, 0, 0)),
                  qspec, qspec, lspec],
        out_specs=[qspec, lspec, pl.BlockSpec((1, 1, 128), lambda h, i: (h, 0, 0))],
        name=name, compiler_params=_cp(("parallel", "arbitrary")))(qs, ks, u, sink, o, do, lse)


def swa_dkv_lat(name, qs, ks, u, do, lse_row, delta_row, nlat):
    n = qs.shape[0]
    tk, wlen = _swa_window(SWA_T, nlat)
    grp = SWA_HQ // SWA_HKV
    scale = SWA_DH ** -0.5
    vcol0 = C_V // 128

    def body(q_ref, k_ref, v_ref, do_ref, l_ref, dl_ref, dk_ref, dv_ref):
        j = pl.program_id(1)
        ws = _win_start(j, tk, wlen, nlat)
        k = k_ref[...]
        v = v_ref[...]
        mask = _win_mask(tk, wlen, j * tk, ws)
        dk = jnp.zeros((tk, 128), F32)
        dv = jnp.zeros((tk, 128), F32)
        for gi in range(grp):
            qw = q_ref[pl.ds(ws, wlen), 128 * gi:128 * gi + 128]
            dow = do_ref[pl.ds(ws, wlen), 128 * gi:128 * gi + 128]
            st = jnp.where(mask, _d(k, qw, ((1,), (1,))) * (scale * LOG2E), NEG)
            pt = jnp.exp2(st - l_ref[gi, :, pl.ds(ws, wlen)])
            dv = dv + _d(pt, dow, ((1,), (0,)))
            dst = pt * (_d(v, dow, ((1,), (1,))) - dl_ref[gi, :, pl.ds(ws, wlen)]) * scale
            dk = dk + _d(dst, qw, ((1,), (0,)))
        dk_ref[...] = dk
        dv_ref[...] = dv

    rspec = pl.BlockSpec((grp, 1, n), lambda hk, j: (hk, 0, 0))
    return pl.pallas_call(
        body, out_shape=[_sd((n, SWA_HKV * 128), F32), _sd((n, SWA_HKV * 128), F32)], grid=(SWA_HKV, nlat // tk),
        in_specs=[pl.BlockSpec((n, grp * 128), lambda hk, j: (0, hk)), pl.BlockSpec((tk, 128), lambda hk, j: (j, hk)),
                  pl.BlockSpec((tk, 128), lambda hk, j: (j, vcol0 + hk)), pl.BlockSpec((n, grp * 128), lambda hk, j: (0, hk)),
                  rspec, rspec],
        out_specs=[pl.BlockSpec((tk, 128), lambda hk, j: (j, hk)), pl.BlockSpec((tk, 128), lambda hk, j: (j, hk))],
        name=name, compiler_params=_cp(("parallel", "arbitrary")))(qs, ks, u, do, lse_row, delta_row)


def swa_attention_fwd(tag, qs, ks, u, sink, cfg, nlat):
    o, lse = swa_fwd_lat(tag + "_fwd_lat", qs, ks, u, sink, nlat)
    return flash_fwd(tag + "_fwd_ctx", qs, ks, u, sink=sink, ctx_q=True, nlat=nlat, prev=(o, lse), **cfg)


def swa_attention_bwd(tag, qs, ks, u, o, do, lse, sink, cfg, nlat):
    n = qs.shape[0]
    dq, delta, ds1 = swa_dq_lat(tag + "_dq_lat", qs, ks, u, o, do, lse, sink, nlat)
    dq, delta, ds2 = flash_dq(tag + "_dq_ctx", qs, ks, u, o, do, lse, sink=sink, ctx_q=True, nlat=nlat, prev=(dq, delta), **cfg)
    dk, dv = swa_dkv_lat(tag + "_dkv_lat", qs, ks, u, do, lse.reshape(SWA_HQ, 1, n), delta.reshape(SWA_HQ, 1, n), nlat)
    kc = {k: v for k, v in cfg.items() if k != "hq"}
    kc["hkv"] = SWA_HKV
    kc["tq"] = min(1024, nlat)
    dk, dv = flash_dkv(tag + "_dkv_ctx", qs, ks, u, do, lse, delta, ctx_k=True, nlat=nlat, prev=(dk, dv), **kc)
    return dq, dk, dv, ds1 + ds2


def adamw(name, w, g, m, v):
    r, c = w.shape
    tr = _pick(r, (256, 128, 64, 32, 16, 8))
    bc1 = 1.0 - ADAM_B1 ** ADAM_STEP
    bc2 = 1.0 - ADAM_B2 ** ADAM_STEP

    def body(w_ref, g_ref, m_ref, v_ref, d_ref, nm_ref, nv_ref):
        gv = g_ref[...]
        nm = ADAM_B1 * m_ref[...] + (1.0 - ADAM_B1) * gv
        nv = ADAM_B2 * v_ref[...] + (1.0 - ADAM_B2) * (gv * gv)
        d_ref[...] = -ADAM_LR * ((nm / bc1) / (jnp.sqrt(nv / bc2) + ADAM_EPS) + ADAM_WD * w_ref[...])
        nm_ref[...] = nm
        nv_ref[...] = nv

    spec = pl.BlockSpec((tr, c), lambda i: (i, 0))
    return pl.pallas_call(body, out_shape=[_sd((r, c), F32)] * 3, grid=(r // tr,), in_specs=[spec] * 4, out_specs=[spec] * 3,
                          name=name, compiler_params=_cp(("parallel",)))(w, g, m, v)


def _coords():
    return lax.axis_index("x"), lax.axis_index("y"), lax.axis_index("c")


_ANY = pl.BlockSpec(memory_space=pl.ANY)


def _chip():
    return 2 * lax.axis_index("x") + lax.axis_index("y")


def _per_core(fn):
    c = lax.axis_index("c")
    for cs in (0, 1):
        pl.when(c == cs)(functools.partial(fn, cs))


def gather_chips_one(name, a):
    r = a.shape[0]
    half = r // 2

    def body(a_ref, o_ref, ici_send, ici_recv, d2d_send, d2d_recv):
        _per_core(functools.partial(run, a_ref, o_ref, ici_send, ici_recv, d2d_send, d2d_recv))

    def run(a_ref, o_ref, ici_send, ici_recv, d2d_send, d2d_recv, c):
        x, y, _ = _coords()
        me = 2 * x + y
        peers = [(1 - x, y), (x, 1 - y), (1 - x, 1 - y)]
        my_rows = pl.ds(c * half, half)
        sib_rows = pl.ds((1 - c) * half, half)
        sends = [pltpu.make_async_remote_copy(a_ref.at[my_rows], o_ref.at[me, my_rows], ici_send.at[k], ici_recv.at[k],
                                              device_id=(px, py, c), device_id_type=MESH)
                 for k, (px, py) in enumerate(peers)]
        for cp in sends:
            cp.start()
        passed = []
        for k, (px, py) in enumerate(peers):
            s = 2 * px + py
            pltpu.make_async_remote_copy(a_ref.at[my_rows], o_ref.at[s, my_rows], ici_send.at[k], ici_recv.at[k],
                                         device_id=(px, py, c), device_id_type=MESH).wait_recv()
            fw = pltpu.make_async_remote_copy(o_ref.at[s, my_rows], o_ref.at[s, my_rows], d2d_send.at[k], d2d_recv.at[k],
                                              device_id=(x, y, 1 - c), device_id_type=MESH)
            fw.start()
            passed.append(fw)
        for k, (px, py) in enumerate(peers):
            s = 2 * px + py
            pltpu.make_async_remote_copy(o_ref.at[s, sib_rows], o_ref.at[s, sib_rows], d2d_send.at[k], d2d_recv.at[k],
                                         device_id=(x, y, 1 - c), device_id_type=MESH).wait_recv()
        for cp in sends + passed:
            cp.wait_send()

    out = pl.pallas_call(
        body, out_shape=_sd((4,) + a.shape, a.dtype), in_specs=[_ANY], out_specs=_ANY,
        scratch_shapes=[pltpu.SemaphoreType.DMA((3,)), pltpu.SemaphoreType.DMA((3,)), pltpu.SemaphoreType.DMA((3,)),
                        pltpu.SemaphoreType.DMA((3,))],
        name=name, compiler_params=pltpu.CompilerParams(has_side_effects=True))(a)
    return lax.dynamic_update_index_in_dim(out, a, _chip(), 0)


def pair_split_one(name, a):
    k4, r, cdim = a.shape
    half = r // 2

    def body(a_ref, got_ref, send_sem, recv_sem):
        _per_core(functools.partial(run, a_ref, got_ref, send_sem, recv_sem))

    def run(a_ref, got_ref, send_sem, recv_sem, c):
        x, y, _ = _coords()
        sib_rows = pl.ds((1 - c) * half, half)
        cp = pltpu.make_async_remote_copy(a_ref.at[:, sib_rows], got_ref, send_sem, recv_sem,
                                          device_id=(x, y, 1 - c), device_id_type=MESH)
        cp.start()
        cp.wait()

    got = pl.pallas_call(
        body, out_shape=_sd((k4, half, cdim), a.dtype), in_specs=[_ANY], out_specs=_ANY,
        scratch_shapes=[pltpu.SemaphoreType.DMA, pltpu.SemaphoreType.DMA],
        name=name, compiler_params=pltpu.CompilerParams(has_side_effects=True))(a)
    return lax.dynamic_slice_in_dim(a, lax.axis_index("c") * half, half, axis=1), got


def scatter_chips_one(name, a):
    def body(a_ref, o_ref, send_sems, recv_sems):
        x, y, c = _coords()
        me = 2 * x + y
        peers = [(1 - x, y), (x, 1 - y), (1 - x, 1 - y)]
        sends = [pltpu.make_async_remote_copy(a_ref.at[2 * px + py], o_ref.at[me], send_sems.at[k], recv_sems.at[k],
                                              device_id=(px, py, c), device_id_type=MESH)
                 for k, (px, py) in enumerate(peers)]
        for cp in sends:
            cp.start()
        for k, (px, py) in enumerate(peers):
            pltpu.make_async_remote_copy(a_ref.at[me], o_ref.at[2 * px + py], send_sems.at[k], recv_sems.at[k],
                                         device_id=(px, py, c), device_id_type=MESH).wait_recv()
        for cp in sends:
            cp.wait_send()

    out = pl.pallas_call(
        body, out_shape=_sd(a.shape, a.dtype), in_specs=[_ANY], out_specs=_ANY,
        scratch_shapes=[pltpu.SemaphoreType.DMA((3,)), pltpu.SemaphoreType.DMA((3,))],
        name=name, compiler_params=pltpu.CompilerParams(has_side_effects=True))(a)
    return lax.dynamic_update_index_in_dim(out, lax.dynamic_index_in_dim(a, _chip(), 0, keepdims=False), _chip(), 0)


def pair_join_one(name, a):
    half, cdim = a.shape

    def body(a_ref, o_ref, send_sem, recv_sem):
        _per_core(functools.partial(run, a_ref, o_ref, send_sem, recv_sem))

    def run(a_ref, o_ref, send_sem, recv_sem, c):
        x, y, _ = _coords()
        my_rows = pl.ds(c * half, half)
        sib_rows = pl.ds((1 - c) * half, half)
        cp = pltpu.make_async_remote_copy(a_ref, o_ref.at[my_rows], send_sem, recv_sem, device_id=(x, y, 1 - c),
                                          device_id_type=MESH)
        cp.start()
        cp.wait_send()
        pltpu.make_async_remote_copy(a_ref, o_ref.at[sib_rows], send_sem, recv_sem, device_id=(x, y, 1 - c),
                                     device_id_type=MESH).wait_recv()

    out = pl.pallas_call(
        body, out_shape=_sd((2 * half, cdim), a.dtype), in_specs=[_ANY], out_specs=_ANY,
        scratch_shapes=[pltpu.SemaphoreType.DMA, pltpu.SemaphoreType.DMA],
        name=name, compiler_params=pltpu.CompilerParams(has_side_effects=True))(a)
    return lax.dynamic_update_slice_in_dim(out, a, lax.axis_index("c") * half, axis=0)


def gather_chips(name, arrs):
    nj = len(arrs)
    halves = [a.shape[0] // 2 for a in arrs]

    def body(*refs):
        _per_core(functools.partial(run, refs[:nj], refs[nj:2 * nj], *refs[2 * nj:]))

    def run(a_refs, o_refs, ici_send, ici_recv, d2d_send, d2d_recv, c):
        x, y, _ = _coords()
        me = 2 * x + y
        peers = [(1 - x, y), (x, 1 - y), (1 - x, 1 - y)]
        mine = [pl.ds(c * h, h) for h in halves]
        sibs = [pl.ds((1 - c) * h, h) for h in halves]

        def ici(k, j, blk):
            return pltpu.make_async_remote_copy(a_refs[j].at[mine[j]], o_refs[j].at[blk, mine[j]], ici_send.at[k * nj + j],
                                                ici_recv.at[k * nj + j], device_id=(*peers[k], c), device_id_type=MESH)

        def d2d(k, j, rows):
            blk = 2 * peers[k][0] + peers[k][1]
            return pltpu.make_async_remote_copy(o_refs[j].at[blk, rows[j]], o_refs[j].at[blk, rows[j]], d2d_send.at[k * nj + j],
                                                d2d_recv.at[k * nj + j], device_id=(x, y, 1 - c), device_id_type=MESH)

        sends = [ici(k, j, me) for k in range(3) for j in range(nj)]
        for cp in sends:
            cp.start()
        passed = []
        for k in range(3):
            for j in range(nj):
                ici(k, j, 2 * peers[k][0] + peers[k][1]).wait_recv()
                fw = d2d(k, j, mine)
                fw.start()
                passed.append(fw)
        for k in range(3):
            for j in range(nj):
                d2d(k, j, sibs).wait_recv()
        for cp in sends + passed:
            cp.wait_send()

    outs = pl.pallas_call(
        body, out_shape=[_sd((4,) + a.shape, a.dtype) for a in arrs], in_specs=[_ANY] * nj, out_specs=[_ANY] * nj,
        scratch_shapes=[pltpu.SemaphoreType.DMA((3 * nj,))] * 4,
        name=name, compiler_params=pltpu.CompilerParams(has_side_effects=True))(*arrs)
    return [lax.dynamic_update_index_in_dim(o, a, _chip(), 0) for o, a in zip(outs, arrs)]


def pair_split(name, arrs):
    nj = len(arrs)
    halves = [a.shape[1] // 2 for a in arrs]

    def body(*refs):
        _per_core(functools.partial(run, refs[:nj], refs[nj:2 * nj], *refs[2 * nj:]))

    def run(a_refs, got_refs, send_sems, recv_sems, c):
        x, y, _ = _coords()
        cps = [pltpu.make_async_remote_copy(a_refs[j].at[:, pl.ds((1 - c) * halves[j], halves[j])], got_refs[j],
                                            send_sems.at[j], recv_sems.at[j], device_id=(x, y, 1 - c), device_id_type=MESH)
               for j in range(nj)]
        for cp in cps:
            cp.start()
        for cp in cps:
            cp.wait()

    got = pl.pallas_call(
        body, out_shape=[_sd((4, h, a.shape[2]), a.dtype) for a, h in zip(arrs, halves)], in_specs=[_ANY] * nj,
        out_specs=[_ANY] * nj, scratch_shapes=[pltpu.SemaphoreType.DMA((nj,))] * 2,
        name=name, compiler_params=pltpu.CompilerParams(has_side_effects=True))(*arrs)
    own = [lax.dynamic_slice_in_dim(a, lax.axis_index("c") * h, h, axis=1) for a, h in zip(arrs, halves)]
    return own, got


def scatter_chips(name, arrs):
    nj = len(arrs)

    def body(*refs):
        a_refs, o_refs = refs[:nj], refs[nj:2 * nj]
        send_sems, recv_sems = refs[2 * nj:]
        x, y, c = _coords()
        me = 2 * x + y
        peers = [(1 - x, y), (x, 1 - y), (1 - x, 1 - y)]

        def cp(k, j, src_blk, dst_blk):
            return pltpu.make_async_remote_copy(a_refs[j].at[src_blk], o_refs[j].at[dst_blk], send_sems.at[k * nj + j],
                                                recv_sems.at[k * nj + j], device_id=(*peers[k], c), device_id_type=MESH)

        sends = [cp(k, j, 2 * peers[k][0] + peers[k][1], me) for k in range(3) for j in range(nj)]
        for s in sends:
            s.start()
        for k in range(3):
            for j in range(nj):
                cp(k, j, me, 2 * peers[k][0] + peers[k][1]).wait_recv()
        for s in sends:
            s.wait_send()

    outs = pl.pallas_call(
        body, out_shape=[_sd(a.shape, a.dtype) for a in arrs], in_specs=[_ANY] * nj, out_specs=[_ANY] * nj,
        scratch_shapes=[pltpu.SemaphoreType.DMA((3 * nj,))] * 2,
        name=name, compiler_params=pltpu.CompilerParams(has_side_effects=True))(*arrs)
    return [lax.dynamic_update_index_in_dim(o, lax.dynamic_index_in_dim(a, _chip(), 0, keepdims=False), _chip(), 0)
            for o, a in zip(outs, arrs)]


def pair_join(name, arrs):
    nj = len(arrs)
    halves = [a.shape[0] for a in arrs]

    def body(*refs):
        _per_core(functools.partial(run, refs[:nj], refs[nj:2 * nj], *refs[2 * nj:]))

    def run(a_refs, o_refs, send_sems, recv_sems, c):
        x, y, _ = _coords()

        def cp(j, rows_of):
            return pltpu.make_async_remote_copy(a_refs[j], o_refs[j].at[pl.ds(rows_of * halves[j], halves[j])], send_sems.at[j],
                                                recv_sems.at[j], device_id=(x, y, 1 - c), device_id_type=MESH)

        sends = [cp(j, c) for j in range(nj)]
        for s in sends:
            s.start()
        for s in sends:
            s.wait_send()
        for j in range(nj):
            cp(j, 1 - c).wait_recv()

    outs = pl.pallas_call(
        body, out_shape=[_sd((2 * a.shape[0], a.shape[1]), a.dtype) for a in arrs], in_specs=[_ANY] * nj, out_specs=[_ANY] * nj,
        scratch_shapes=[pltpu.SemaphoreType.DMA((nj,))] * 2,
        name=name, compiler_params=pltpu.CompilerParams(has_side_effects=True))(*arrs)
    return [lax.dynamic_update_slice_in_dim(o, a, lax.axis_index("c") * a.shape[0], axis=0) for o, a in zip(outs, arrs)]


def add_cast(name, a, b, dtype):
    k, r, c = a.shape
    tr = _pick(r, (512, 256, 128, 64, 32, 16, 8))

    def body(a_ref, b_ref, o_ref):
        o_ref[...] = (a_ref[...].astype(F32) + b_ref[...].astype(F32)).astype(dtype)

    spec = pl.BlockSpec((1, tr, c), lambda s, i: (s, i, 0))
    return pl.pallas_call(body, out_shape=_sd((k, r, c), dtype), grid=(k, r // tr), in_specs=[spec, spec], out_specs=spec,
                          name=name, compiler_params=_cp(("parallel", "parallel")))(a, b)


def gather_all(name, a):
    def body(a_ref, o_ref, send_sems, recv_sems, loc_sem):
        x, y, c = _coords()
        me = 4 * x + 2 * y + c
        flips = [(fx, fy, fc) for fx in (0, 1) for fy in (0, 1) for fc in (0, 1) if fx + fy + fc > 0]
        peers = [(x ^ fx, y ^ fy, c ^ fc) for fx, fy, fc in flips]
        mine = pltpu.make_async_copy(a_ref, o_ref.at[me], loc_sem)
        mine.start()
        sends = [pltpu.make_async_remote_copy(a_ref, o_ref.at[me], send_sems.at[k], recv_sems.at[k],
                                              device_id=p, device_id_type=MESH) for k, p in enumerate(peers)]
        for cp in sends:
            cp.start()
        for k, (px, py, pc) in enumerate(peers):
            pltpu.make_async_remote_copy(a_ref, o_ref.at[4 * px + 2 * py + pc], send_sems.at[k], recv_sems.at[k],
                                         device_id=(px, py, pc), device_id_type=MESH).wait_recv()
        for cp in sends:
            cp.wait_send()
        mine.wait()

    return pl.pallas_call(
        body, out_shape=_sd((8,) + a.shape, a.dtype), in_specs=[_ANY], out_specs=_ANY,
        scratch_shapes=[pltpu.SemaphoreType.DMA((7,)), pltpu.SemaphoreType.DMA((7,)), pltpu.SemaphoreType.DMA],
        name=name, compiler_params=pltpu.CompilerParams(has_side_effects=True))(a)


def sum_blocks(name, a):
    k, r, c = a.shape
    tr = _pick(r, (256, 128, 64, 32, 16, 8))

    def body(a_ref, o_ref):
        acc = a_ref[0].astype(F32)
        for s in range(1, k):
            acc = acc + a_ref[s].astype(F32)
        o_ref[...] = acc

    return pl.pallas_call(body, out_shape=_sd((r, c), F32), grid=(r // tr,),
                          in_specs=[pl.BlockSpec((k, tr, c), lambda i: (0, i, 0))], out_specs=pl.BlockSpec((tr, c), lambda i: (i, 0)),
                          name=name, compiler_params=_cp(("parallel",)))(a)


BIG = ("w_mod", "w_in", "w_mla_uq", "w_mla_ukv", "w_p_ssm", "w_p_swa", "w_p_mla", "w_out", "w_ffn_in", "w_ffn_out")
COL_SHARDED = ("w_mod", "w_in", "w_mla_uq", "w_mla_ukv", "w_ffn_in")
SMALL = ("c_ctx", "b_mod", "norm1_g", "norm2_g", "ssm_conv_w", "ssm_conv_b", "ssm_dt_bias", "ssm_a_log", "ssm_d",
         "ssm_norm_g", "swa_q_norm_g", "swa_k_norm_g", "swa_sink", "mla_q_lat_g", "mla_kv_lat_g", "mla_q_norm_g",
         "mla_k_norm_g")
WEIGHTS = ("c_ctx", "w_mod", "b_mod", "norm1_g", "norm2_g", "w_in", "ssm_conv_w", "ssm_conv_b", "ssm_dt_bias", "ssm_a_log",
           "ssm_d", "ssm_norm_g", "swa_q_norm_g", "swa_k_norm_g", "swa_sink", "mla_q_lat_g", "mla_kv_lat_g", "w_mla_uq",
           "w_mla_ukv", "mla_q_norm_g", "mla_k_norm_g", "w_p_ssm", "w_p_swa", "w_p_mla", "w_out", "w_ffn_in", "w_ffn_out")


def pack_w_in(w):
    z = lambda k: jnp.zeros((w.shape[0], k), w.dtype)
    return jnp.concatenate([w[:, 4832:7904], w[:, 2400:3424], w[:, 3424:4448], w[:, 0:1536], w[:, 1568:1824], w[:, 1824:2080],
                            w[:, 2080:2336], w[:, 2336:2400], w[:, 1536:1568], z(32), z(128), w[:, 4448:4832]], axis=1)


def unpack_w_in(g):
    return jnp.concatenate([g[:, 5120:6656], g[:, 7488:7520], g[:, 6656:6912], g[:, 6912:7168], g[:, 7168:7424], g[:, 7424:7488],
                            g[:, 3072:4096], g[:, 4096:5120], g[:, 7680:8064], g[:, 0:3072]], axis=1)


def pack_ukv(w):
    return w.reshape(MLA_KVRANK, MLA_H, 2, 128).transpose(0, 2, 1, 3).reshape(MLA_KVRANK, 2048)


def unpack_ukv(g):
    return g.reshape(MLA_KVRANK, 2, MLA_H, 128).transpose(0, 2, 1, 3).reshape(MLA_KVRANK, 2048)


def pack_uq(w):
    return jnp.pad(w.reshape(MLA_QRANK, MLA_H, 192), ((0, 0), (0, 0), (0, 64))).reshape(MLA_QRANK, 2048)


def unpack_uq(g):
    return g.reshape(MLA_QRANK, MLA_H, 256)[:, :, :192].reshape(MLA_QRANK, 1536)


def rope_tables(nlat):
    t = jnp.arange(nlat, dtype=jnp.int32)
    r = (t // GRID_W).astype(F32)[:, None]
    col = (t % GRID_W).astype(F32)[:, None]

    def tab(nf, pad):
        inv = jnp.power(ROPE_BASE, -jnp.arange(nf, dtype=F32) / nf)
        ar, ac = r * inv, col * inv
        cos = jnp.concatenate([jnp.cos(ar), jnp.cos(ar), jnp.cos(ac), jnp.cos(ac), jnp.ones((nlat, pad), F32)], axis=1)
        sin = jnp.concatenate([-jnp.sin(ar), jnp.sin(ar), -jnp.sin(ac), jnp.sin(ac), jnp.zeros((nlat, pad), F32)], axis=1)
        cos = jnp.concatenate([cos, jnp.ones((NCTX, 128), F32)], axis=0)
        sin = jnp.concatenate([sin, jnp.zeros((NCTX, 128), F32)], axis=0)
        return cos, sin

    return tab(32, 0), tab(16, 64)


def _lanes(v, start, width=128):
    return jnp.zeros((1, width), F32).at[0, start:start + v.shape[0]].set(v)


def layer_fwd(i, xin, h, mod, p, tabs, nlat):
    t = "l%d_" % i
    n = xin.shape[0]
    (cos_s, sin_s), (cos_m, sin_m) = tabs
    u = mm(h, p["w_in"], F32, t + "in_proj")
    xbc = conv_fwd(t + "conv", u, p["conv_w"], p["conv_b"], nlat)
    dtrow = jnp.transpose(u[:, C_MISC + DT_LANE:C_MISC + DT_LANE + 32])
    nlc = nlat // Q
    yf, hs_f = ssd_fwd(t + "ssd_f", xbc, u, dtrow, p["bias_c"], p["alog_c"], p["bias_r"], p["alog_r"], nlc, False, 0)
    yb, hs_b = ssd_fwd(t + "ssd_b", xbc, u, dtrow, p["bias_c"], p["alog_c"], p["bias_r"], p["alog_r"], nlc, True, 1)
    ys = ssd_out_fwd(t + "ssd_out", yf, yb, xbc, u, p["ssm_norm_g"], p["d_exp"])
    qs, ks = swa_prep_fwd(t + "swa_prep", u, p["swa_q_g"], p["swa_k_g"], cos_s, sin_s)
    o_swa, lse_swa = swa_attention_fwd(t + "swa", qs, ks, u, p["sink"], p["swa_cfg"], nlat)
    ckv_n, cq_n = lat_norm_fwd(t + "lat_norm", u, p["kv_lat_g"], p["q_lat_g"])
    kv = mm(ckv_n, p["w_ukv"], F32, t + "ukv")
    qp = mm(cq_n, p["w_uq"], F32, t + "uq")
    km, qm, vm = mla_prep_fwd(t + "mla_prep", kv, qp, u, p["mla_q_g"], p["mla_k_g"], cos_m, sin_m)
    o_mla, lse_mla = mla_fwd(t + "mla_fwd", qm, km, vm, nlat)
    p1 = mm(ys, p["w_p_ssm"], BF16, t + "p_ssm")
    p2 = mm(o_swa, p["w_p_swa"], BF16, t + "p_swa")
    p3 = mm(o_mla, p["w_p_mla"], BF16, t + "p_mla")
    merged = merge_fwd(t + "merge", u, p1, p2, p3)
    o = mm(merged, p["w_out"], F32, t + "out_proj")
    x1, h2 = resid_mod_fwd(t + "res1", xin, o, mod, 2, mod, 3, 4, p["norm2_g"], nlat // RT)
    gu = mm(h2, p["w_ffn_in"], BF16, t + "ffn_in")
    a = swiglu_fwd(t + "swiglu", gu)
    f = mm(a, p["w_ffn_out"], F32, t + "ffn_out")
    saved = dict(xin=xin, h=h, u=u, xbc=xbc, dtrow=dtrow, yf=yf, yb=yb, hs_f=hs_f, hs_b=hs_b, ys=ys, qs=qs, ks=ks,
                 o_swa=o_swa, lse_swa=lse_swa, ckv_n=ckv_n, cq_n=cq_n, kv=kv, qp=qp, km=km, qm=qm, vm=vm, o_mla=o_mla,
                 lse_mla=lse_mla, p1=p1, p2=p2, p3=p3, merged=merged, o=o, x1=x1, h2=h2, gu=gu, a=a, f=f)
    del n
    return x1, f, saved


def layer_bwd(i, dx2, df, dgt2, sv, mod, p, tabs, nlat):
    t = "l%db_" % i
    (cos_s, sin_s), (cos_m, sin_m) = tabs
    g = {}
    nt = nlat // RT
    nlc = nlat // Q
    g["w_ffn_out"] = mm_tn(sv["a"], df, t + "wg_ffn_out")
    da = mm(df, p["w_ffn_out"], F32, t + "dg_ffn_out", trans_b=True)
    dgu = swiglu_bwd(t + "swiglu", sv["gu"], da)
    g["w_ffn_in"] = mm_tn(sv["h2"], dgu, t + "wg_ffn_in")
    dh2 = mm(dgu, p["w_ffn_in"], F32, t + "dg_ffn_in", trans_b=True)
    dx1, do, dgt1, dsh2, dsc2, g["norm2_g"] = resid_mod_bwd(t + "res1", sv["x1"], dx2, dh2, sv["o"], mod, 2, mod, 3, 4,
                                                              p["norm2_g"], nt)
    g["w_out"] = mm_tn(sv["merged"], do, t + "wg_out")
    dmerged = mm(do, p["w_out"], F32, t + "dg_out", trans_b=True)
    dp1, dp2, dp3, dgates = merge_bwd(t + "merge", sv["u"], sv["p1"], sv["p2"], sv["p3"], dmerged)
    g["w_p_ssm"] = mm_tn(sv["ys"], dp1, t + "wg_p_ssm")
    g["w_p_swa"] = mm_tn(sv["o_swa"], dp2, t + "wg_p_swa")
    g["w_p_mla"] = mm_tn(sv["o_mla"], dp3, t + "wg_p_mla")
    dys = mm(dp1, p["w_p_ssm"], F32, t + "dg_p_ssm", trans_b=True)
    do_swa = mm(dp2, p["w_p_swa"], BF16, t + "dg_p_swa", trans_b=True)
    do_mla = mm(dp3, p["w_p_mla"], BF16, t + "dg_p_mla", trans_b=True)
    dqm, dkm, dv_mla = mla_attention_bwd(t + "mla", sv["qm"], sv["km"], sv["vm"], sv["o_mla"], do_mla, sv["lse_mla"], nlat)
    dkv, dqp, dkr, g["mla_q_g"], g["mla_k_g"] = mla_prep_bwd(t + "mla_prep", sv["kv"], sv["qp"], sv["u"], p["mla_q_g"],
                                                             p["mla_k_g"], cos_m, sin_m, dkm, dqm, dv_mla)
    g["w_ukv"] = mm_tn(sv["ckv_n"], dkv, t + "wg_ukv")
    g["w_uq"] = mm_tn(sv["cq_n"], dqp, t + "wg_uq")
    dckv_n = mm(dkv, p["w_ukv"], F32, t + "dg_ukv", trans_b=True)
    dcq_n = mm(dqp, p["w_uq"], F32, t + "dg_uq", trans_b=True)
    dckv, dcq, g["kv_lat_g"], g["q_lat_g"] = lat_norm_bwd(t + "lat_norm", sv["u"], p["kv_lat_g"], p["q_lat_g"], dckv_n, dcq_n)
    dqs, dks, dv_swa, g["sink"] = swa_attention_bwd(t + "swa", sv["qs"], sv["ks"], sv["u"], sv["o_swa"], do_swa, sv["lse_swa"],
                                                p["sink"], p["swa_cfg"], nlat)
    dq, dk, dv, g["swa_q_g"], g["swa_k_g"] = swa_prep_bwd(t + "swa_prep", sv["u"], p["swa_q_g"], p["swa_k_g"], cos_s, sin_s,
                                                          dqs, dks, dv_swa)
    dy, dxs_skip, dz, g["ssm_norm_g"], g["d_exp"] = ssd_out_bwd(t + "ssd_out", sv["yf"], sv["yb"], sv["xbc"], sv["u"],
                                                                 p["ssm_norm_g"], p["d_exp"], dys)
    n = dy.shape[0]
    zbc = jnp.zeros((n, 256), F32)
    r_f = ssd_bwd(t + "ssd_f", sv["xbc"], sv["u"], sv["dtrow"], p["bias_c"], p["alog_c"], p["bias_r"], p["alog_r"],
                  sv["hs_f"], dy, (dxs_skip, zbc, zbc), nlc, False, 0)
    r_b = ssd_bwd(t + "ssd_b", sv["xbc"], sv["u"], sv["dtrow"], p["bias_c"], p["alog_c"], p["bias_r"], p["alog_r"],
                  sv["hs_b"], dy, (r_f[0], r_f[1], r_f[2]), nlc, True, 1)
    dact = jnp.concatenate([r_b[0], r_b[1], r_b[2]], axis=1)
    dxbc, g["conv_w"], g["conv_b"] = conv_bwd(t + "conv", sv["u"], dact, p["conv_w"], p["conv_b"], nlat)
    drow = jnp.concatenate([r_f[4][0] + r_f[4][1], r_b[4][0] + r_b[4][1]], axis=0)
    drow_t = jnp.pad(jnp.transpose(drow), ((0, 0), (DT_LANE, 128 - DT_LANE - 32)))
    dmisc = misc_combine(t + "misc", dkr, r_f[3], r_b[3], drow_t)
    g["bias_c"] = r_f[5] + r_b[5]
    g["alog_c"] = r_f[6] + r_b[6]
    g["bias_r"] = jnp.concatenate([r_f[7], r_b[7]], axis=0)
    g["alog_r"] = jnp.concatenate([r_f[8], r_b[8]], axis=0)
    du = jnp.concatenate([dgates, dz, dq, dxbc, dk, dv, dckv, dmisc, jnp.zeros((n, 128), BF16), dcq], axis=1)
    g["w_in"] = mm_tn(sv["h"], du, t + "wg_in")
    dh = mm(du, p["w_in"], F32, t + "dg_in", trans_b=True)
    g["mod"] = (dgt1, dsh2, dsc2, dgt2)
    return dx1, dh, g


def local_step(x, c, ctx, target, c_ctx, W, nlat):
    xin = jnp.concatenate([x, ctx], axis=0)
    n = xin.shape[0]
    nt = nlat // RT
    tabs = rope_tables(nlat)
    c8 = jnp.zeros((8, D), F32).at[0].set(c[0]).at[1].set(c_ctx)
    mods, silus = [], []
    for i in range(DEPTH):
        m8, s8 = mod_fwd("l%d_mod" % i, c8, W[i]["w_mod"], W[i]["b_mod"])
        mods.append(m8[0:2].reshape(2, 1, 6 * D))
        silus.append(s8)
    saved = []
    _, h = resid_mod_fwd("l0_norm1", xin, None, None, 0, mods[0], 0, 1, W[0]["norm1_g"], nt)
    xcur = xin
    for i in range(DEPTH):
        x1, f, sv = layer_fwd(i, xcur, h, mods[i], W[i], tabs, nlat)
        saved.append(sv)
        if i + 1 < DEPTH:
            xcur, h = resid_mod_fwd("l%d_res2" % i, x1, f, mods[i], 5, mods[i + 1], 0, 1, W[i + 1]["norm1_g"], nt)
    loss_v, dx2, df, dgt2 = resid_loss("loss", x1, f, mods[DEPTH - 1], 5, target, nt)
    grads = [None] * DEPTH
    for i in reversed(range(DEPTH)):
        dx1, dh, g = layer_bwd(i, dx2, df, dgt2, saved[i], mods[i], W[i], tabs, nlat)
        if i > 0:
            sv = saved[i]
            dx2, df, dgt2, dsh1, dsc1, g["norm1_g"] = resid_mod_bwd(
                "l%db_res2" % (i - 1), sv["xin"], dx1, dh, saved[i - 1]["f"], mods[i - 1], 5, mods[i], 0, 1,
                W[i]["norm1_g"], nt)
        else:
            dxin, _, _, dsh1, dsc1, g["norm1_g"] = resid_mod_bwd("l0b_norm1", saved[0]["xin"], dx1, dh, None, None, 0,
                                                                  mods[0], 0, 1, W[0]["norm1_g"], nt)
        dgt1, dsh2, dsc2, dgt2_i = g.pop("mod")
        dmod = jnp.concatenate([dsh1, dsc1, dgt1, dsh2, dsc2, dgt2_i], axis=2).reshape(2, 6 * D)
        dmod8 = jnp.zeros((8, 6 * D), F32).at[0:2].set(dmod)
        g["w_mod"] = mm_tn(silus[i], dmod8, "l%db_wg_mod" % i)
        dsilu = mm(dmod8, W[i]["w_mod"], F32, "l%db_dg_mod" % i, trans_b=True)
        dc8, g["b_mod"] = mod_small_bwd("l%db_mod_small" % i, c8, dsilu, dmod8)
        g["c8"] = dc8
        grads[i] = g
    del n
    return loss_v[0, 0], dxin, grads


def _big_shapes():
    return dict(w_mod=(2, 1024, 1536), w_in=(2, 1024, 1976), w_mla_uq=(2, 384, 384), w_mla_ukv=(2, 256, 512),
                w_p_ssm=(2, 256, 1024), w_p_swa=(2, 256, 1024), w_p_mla=(2, 256, 1024), w_out=(2, 256, 1024),
                w_ffn_in=(2, 1024, 1408), w_ffn_out=(2, 704, 1024))


PACK_ROWS = 14336


def _pack_big(d, dtype):
    parts = [d[k].astype(dtype).reshape(-1, 1024) for k in BIG]
    used = sum(p.shape[0] for p in parts)
    return jnp.concatenate(parts + [jnp.zeros((PACK_ROWS - used, 1024), dtype)], axis=0)


def _unpack_big(buf, lead):
    out = {}
    r0 = 0
    for k in BIG:
        sh = _big_shapes()[k]
        rows = sh[0] * sh[1] * sh[2] // 1024
        out[k] = buf[..., r0:r0 + rows, :].reshape(lead + sh)
        r0 += rows
    return out


def _full_from_chips(k, a):
    if k in COL_SHARDED:
        return a.transpose(1, 2, 0, 3).reshape(2, a.shape[2], 4 * a.shape[3])
    return a.transpose(1, 0, 2, 3).reshape(2, 4 * a.shape[2], a.shape[3])


def _chips_from_full(k, a):
    if k in COL_SHARDED:
        return a.reshape(a.shape[0], 4, a.shape[1] // 4).transpose(1, 0, 2)
    return a.reshape(4, a.shape[0] // 4, a.shape[1])


def _small_sizes():
    return dict(c_ctx=1024, b_mod=2 * 6144, norm1_g=2048, norm2_g=2048, ssm_conv_w=2 * 5 * 1536, ssm_conv_b=2 * 1536,
                ssm_dt_bias=64, ssm_a_log=64, ssm_d=32, ssm_norm_g=2048, swa_q_norm_g=256, swa_k_norm_g=256, swa_sink=16,
                mla_q_lat_g=768, mla_kv_lat_g=512, mla_q_norm_g=384, mla_k_norm_g=384)


def _pack_small(d):
    parts = []
    for k in SMALL:
        v = d[k].astype(F32).reshape(-1)
        parts.append(jnp.pad(v, (0, (-v.shape[0]) % 1024)))
    return jnp.concatenate(parts).reshape(-1, 128)


def _unpack_small(buf, shapes):
    flat = buf.reshape(-1)
    out = {}
    o = 0
    for k in SMALL:
        sz = _small_sizes()[k]
        out[k] = flat[o:o + sz].reshape(shapes[k])
        o += sz + (-sz) % 1024
    return out


def big_grads(grads):
    gfull = {k: [] for k in BIG}
    for i in range(DEPTH):
        g = grads[i]
        gfull["w_mod"].append(g["w_mod"])
        gfull["w_in"].append(unpack_w_in(g["w_in"]))
        gfull["w_mla_uq"].append(unpack_uq(g["w_uq"]))
        gfull["w_mla_ukv"].append(unpack_ukv(g["w_ukv"]))
        for k in ("w_p_ssm", "w_p_swa", "w_p_mla", "w_out", "w_ffn_in", "w_ffn_out"):
            gfull[k].append(g[k])
    return gfull


def small_grads(grads):
    gs = {}
    gs["c_ctx"] = sum(grads[i]["c8"][1] for i in range(DEPTH))
    st = lambda f: jnp.stack([f(grads[i]) for i in range(DEPTH)])
    gs["b_mod"] = st(lambda g: g["b_mod"][0])
    gs["norm1_g"] = st(lambda g: g["norm1_g"][0])
    gs["norm2_g"] = st(lambda g: g["norm2_g"][0])
    gs["ssm_conv_w"] = st(lambda g: g["conv_w"])
    gs["ssm_conv_b"] = st(lambda g: g["conv_b"][0])
    gs["ssm_dt_bias"] = st(lambda g: (g["bias_c"][0, DT_LANE:DT_LANE + 32] + g["bias_r"][:, 0]).reshape(2, 16))
    gs["ssm_a_log"] = st(lambda g: (g["alog_c"][0, DT_LANE:DT_LANE + 32] + g["alog_r"][:, 0]).reshape(2, 16))
    gs["ssm_d"] = st(lambda g: g["d_exp"].reshape(16, 64).sum(axis=1))
    gs["ssm_norm_g"] = st(lambda g: g["ssm_norm_g"][0])
    gs["swa_q_norm_g"] = st(lambda g: g["swa_q_g"][0])
    gs["swa_k_norm_g"] = st(lambda g: g["swa_k_g"][0])
    gs["swa_sink"] = st(lambda g: g["sink"][:, 0, 0])
    gs["mla_q_lat_g"] = st(lambda g: g["q_lat_g"][0])
    gs["mla_kv_lat_g"] = st(lambda g: g["kv_lat_g"][0])
    gs["mla_q_norm_g"] = st(lambda g: g["mla_q_g"][0, :192])
    gs["mla_k_norm_g"] = st(lambda g: g["mla_k_g"][0, :192])
    return gs


def layer_params(i, full, conv_full, sm, nlat):
    p = {}
    p["w_mod"] = full["w_mod"][i]
    p["w_in"] = pack_w_in(full["w_in"][i])
    p["w_uq"] = pack_uq(full["w_mla_uq"][i])
    p["w_ukv"] = pack_ukv(full["w_mla_ukv"][i])
    for k in ("w_p_ssm", "w_p_swa", "w_p_mla", "w_out", "w_ffn_in", "w_ffn_out"):
        p[k] = full[k][i]
    p["b_mod"] = sm["b_mod"][i][None]
    p["norm1_g"] = sm["norm1_g"][i][None]
    p["norm2_g"] = sm["norm2_g"][i][None]
    p["conv_w"] = conv_full[i]
    p["conv_b"] = sm["ssm_conv_b"][i][None]
    bias = sm["ssm_dt_bias"][i].reshape(32)
    alog = sm["ssm_a_log"][i].reshape(32)
    p["bias_c"] = _lanes(bias, DT_LANE)
    p["alog_c"] = _lanes(alog, DT_LANE)
    p["bias_r"] = bias[:, None]
    p["alog_r"] = alog[:, None]
    p["d_exp"] = jnp.repeat(sm["ssm_d"][i], 64)[None]
    p["ssm_norm_g"] = sm["ssm_norm_g"][i][None]
    p["swa_q_g"] = sm["swa_q_norm_g"][i][None]
    p["swa_k_g"] = sm["swa_k_norm_g"][i][None]
    p["sink"] = jnp.broadcast_to(sm["swa_sink"][i][:, None, None], (SWA_HQ, 1, 128))
    p["q_lat_g"] = sm["mla_q_lat_g"][i][None]
    p["kv_lat_g"] = sm["mla_kv_lat_g"][i][None]
    p["mla_q_g"] = _lanes(sm["mla_q_norm_g"][i], 0, 256)
    p["mla_k_g"] = _lanes(sm["mla_k_norm_g"][i], 0, 256)
    p["swa_cfg"] = dict(w=128, vw=128, hq=SWA_HQ, grp=SWA_HQ // SWA_HKV, vcol0=C_V // 128, scale=SWA_DH ** -0.5,
                        tq=256, tk=256, band=True)
    return p


def kernel(x, c, ctx, c_ctx, w_mod, b_mod, norm1_g, norm2_g, w_in, ssm_conv_w, ssm_conv_b, ssm_dt_bias, ssm_a_log, ssm_d, ssm_norm_g, swa_q_norm_g, swa_k_norm_g, swa_sink, mla_q_lat_g, mla_kv_lat_g, w_mla_uq, w_mla_ukv, mla_q_norm_g, mla_k_norm_g, w_p_ssm, w_p_swa, w_p_mla, w_out, w_ffn_in, w_ffn_out, loss_target, m_c_ctx, m_w_mod, m_b_mod, m_norm1_g, m_norm2_g, m_w_in, m_ssm_conv_w, m_ssm_conv_b, m_ssm_dt_bias, m_ssm_a_log, m_ssm_d, m_ssm_norm_g, m_swa_q_norm_g, m_swa_k_norm_g, m_swa_sink, m_mla_q_lat_g, m_mla_kv_lat_g, m_w_mla_uq, m_w_mla_ukv, m_mla_q_norm_g, m_mla_k_norm_g, m_w_p_ssm, m_w_p_swa, m_w_p_mla, m_w_out, m_w_ffn_in, m_w_ffn_out, v_c_ctx, v_w_mod, v_b_mod, v_norm1_g, v_norm2_g, v_w_in, v_ssm_conv_w, v_ssm_conv_b, v_ssm_dt_bias, v_ssm_a_log, v_ssm_d, v_ssm_norm_g, v_swa_q_norm_g, v_swa_k_norm_g, v_swa_sink, v_mla_q_lat_g, v_mla_kv_lat_g, v_w_mla_uq, v_w_mla_ukv, v_mla_q_norm_g, v_mla_k_norm_g, v_w_p_ssm, v_w_p_swa, v_w_p_mla, v_w_out, v_w_ffn_in, v_w_ffn_out):
    loc = dict(locals())
    w = {k: loc[k] for k in WEIGHTS}
    m = {k: loc["m_" + k] for k in WEIGHTS}
    v = {k: loc["v_" + k] for k in WEIGHTS}
    nlat = x.shape[1]

    sh2 = {k: (w[k].shape[0] * w[k].shape[1], w[k].shape[2]) for k in BIG}
    conv_sh = jnp.pad(ssm_conv_w.reshape(10, 384), ((0, 6), (0, 0)))
    gathered = gather_chips("gather_weights", [w[k].astype(BF16).reshape(sh2[k]) for k in BIG] + [conv_sh])
    full = {k: _full_from_chips(k, g.reshape((4,) + w[k].shape)) for k, g in zip(BIG, gathered)}
    conv_full = gathered[-1][:, :10].reshape(4, 2, 5, 384).transpose(1, 2, 0, 3).reshape(2, 5, 1536)

    W = [layer_params(i, full, conv_full, w, nlat) for i in range(DEPTH)]

    loss_loc, dx, grads = local_step(x[0], c, ctx[0], loss_target[0], c_ctx, W, nlat)

    gfull = big_grads(grads)
    by_chip = {k: jnp.stack([_chips_from_full(k, a) for a in gfull[k]], axis=1) for k in BIG}
    send = [by_chip[k].astype(BF16).reshape((4,) + sh2[k]) for k in BIG]
    own, got = pair_split("pair_split", send)
    pair = [add_cast("pair_sum_" + k, o, g, BF16) for k, o, g in zip(BIG, own, got)]
    recv = scatter_chips("scatter_grads", pair)
    mine = [sum_blocks("sum_chips_" + k, r) for k, r in zip(BIG, recv)]
    gbig = {k: g.reshape(w[k].shape) for k, g in zip(BIG, pair_join("join_cores", mine))}

    gs = small_grads(grads)
    small_all = gather_all("gather_small", _pack_small(gs))
    small_sum = sum_blocks("sum_small", small_all)
    full_shapes = {k: (w[k].shape if k != "ssm_conv_w" else (2, 5, 1536)) for k in SMALL}
    gsmall = _unpack_small(small_sum, full_shapes)
    chip = 2 * lax.axis_index("x") + lax.axis_index("y")
    gsmall["ssm_conv_w"] = lax.dynamic_slice_in_dim(gsmall["ssm_conv_w"], chip * 384, 384, axis=2)

    grad = {**gbig, **gsmall}
    delta, new_m, new_v = {}, {}, {}
    sm = {k: _pack_small_local(d) for k, d in (("w", w), ("g", grad), ("m", m), ("v", v))}
    r = adamw("adamw_small", sm["w"], sm["g"], sm["m"], sm["v"])
    shapes = {k: w[k].shape for k in SMALL}
    for dst, buf in zip((delta, new_m, new_v), r):
        dst.update(_unpack_small_local(buf, shapes))
    for k in BIG:
        sh = w[k].shape
        r = adamw("adamw_" + k, *[a[k].reshape(sh[0] * sh[1], sh[2]) for a in (w, grad, m, v)])
        for dst, buf in zip((delta, new_m, new_v), r):
            dst[k] = buf.reshape(sh)

    loss = lax.psum(loss_loc, ("x", "y", "c"))
    return (loss, dx[None, :nlat], *[grad[k] for k in WEIGHTS], *[delta[k] for k in WEIGHTS],
            *[new_m[k] for k in WEIGHTS], *[new_v[k] for k in WEIGHTS])


def _pack_small_local(d):
    parts = []
    for k in SMALL:
        a = d[k].astype(F32).reshape(-1)
        parts.append(jnp.pad(a, (0, (-a.shape[0]) % 1024)))
    return jnp.concatenate(parts).reshape(-1, 128)


def _unpack_small_local(buf, shapes):
    flat = buf.reshape(-1)
    out = {}
    o = 0
    for k in SMALL:
        sz = math.prod(shapes[k])
        out[k] = flat[o:o + sz].reshape(shapes[k])
        o += sz + (-sz) % 1024
    return out
```

```python
import functools
import math

import jax
import jax.numpy as jnp
from jax import lax
from jax.experimental import pallas as pl
from jax.experimental.pallas import tpu as pltpu

F32 = jnp.float32
BF16 = jnp.bfloat16
MESH = pl.DeviceIdType.MESH

D = 1024
NCTX = 256
EPS = 1e-6
ROPE_BASE = 10000.0
GRID_W = 64
DEPTH = 2
Q = 128
N_HEADS_SSM = 16
SWA_HQ, SWA_HKV, SWA_DH, SWA_WIN = 8, 2, 128, 128
MLA_H, MLA_NOPE, MLA_ROPE, MLA_V = 8, 128, 64, 128
MLA_QRANK, MLA_KVRANK = 384, 256
FFN = 2816
RT = 256
VMEM_LIMIT = 56 << 20
NEG = -1e30
LOG2E = 1.4426950408889634

C_G1, C_G2, C_G3, C_Z, C_Q, C_XS, C_B, C_C, C_K, C_V, C_CKV, C_MISC, C_PAD, C_CQ = (
    0, 1024, 2048, 3072, 4096, 5120, 6144, 6400, 6656, 6912, 7168, 7424, 7552, 7680)
UW = 8064
DT_LANE = 64

ADAM_LR, ADAM_B1, ADAM_B2, ADAM_EPS, ADAM_WD, ADAM_STEP = 0.001, 0.9, 0.999, 1e-08, 0.01, 10


def _cp(sem):
    return pltpu.CompilerParams(dimension_semantics=sem, vmem_limit_bytes=VMEM_LIMIT)


def _pick(n, cands):
    for c in cands:
        if n % c == 0:
            return c
    return n


_TN = (1536, 1408, 1152, 1024, 896, 768, 512, 384, 256, 128)


def mm(a, b, out_dtype, name, trans_b=False):
    m, k = a.shape
    n = b.shape[0] if trans_b else b.shape[1]
    tm = _pick(m, (768, 512, 256, 128, 8))
    tn = _pick(n, _TN)
    tk = k if k <= 2048 else _pick(k, (1408, 1152, 1024, 896, 768, 512))
    nk = k // tk
    b_spec = (pl.BlockSpec((tn, tk), lambda i, j, kk: (j, kk)) if trans_b
              else pl.BlockSpec((tk, tn), lambda i, j, kk: (kk, j)))

    def body(a_ref, b_ref, o_ref, *acc):
        p = _d(a_ref[...], b_ref[...], ((1,), (1 if trans_b else 0,)))
        if nk == 1:
            o_ref[...] = p.astype(out_dtype)
        else:
            kk = pl.program_id(2)

            @pl.when(kk == 0)
            def _():
                acc[0][...] = p

            @pl.when(kk > 0)
            def _():
                acc[0][...] += p

            @pl.when(kk == nk - 1)
            def _():
                o_ref[...] = acc[0][...].astype(out_dtype)

    return pl.pallas_call(
        body, out_shape=jax.ShapeDtypeStruct((m, n), out_dtype), grid=(m // tm, n // tn, nk),
        in_specs=[pl.BlockSpec((tm, tk), lambda i, j, kk: (i, kk)), b_spec],
        out_specs=pl.BlockSpec((tm, tn), lambda i, j, kk: (i, j)),
        scratch_shapes=[] if nk == 1 else [pltpu.VMEM((tm, tn), F32)],
        name=name, compiler_params=_cp(("parallel", "parallel", "arbitrary")))(a, b)


def mm_tn(a, b, name, out_dtype=BF16):
    t, ka = a.shape
    _, nb = b.shape
    ta = _pick(ka, (1024, 1408, 768, 512, 384, 256, 128))
    tb = _pick(nb, _TN)
    tt = _pick(t, (768, 512, 256, 128, 8))
    nt = t // tt

    def body(a_ref, b_ref, o_ref, acc):
        p = _d(a_ref[...], b_ref[...], ((0,), (0,)))
        s = pl.program_id(2)

        @pl.when(s == 0)
        def _():
            acc[...] = p

        @pl.when(s > 0)
        def _():
            acc[...] += p

        @pl.when(s == nt - 1)
        def _():
            o_ref[...] = acc[...].astype(out_dtype)

    return pl.pallas_call(
        body, out_shape=jax.ShapeDtypeStruct((ka, nb), out_dtype), grid=(ka // ta, nb // tb, nt),
        in_specs=[pl.BlockSpec((tt, ta), lambda i, j, s: (s, i)), pl.BlockSpec((tt, tb), lambda i, j, s: (s, j))],
        out_specs=pl.BlockSpec((ta, tb), lambda i, j, s: (i, j)), scratch_shapes=[pltpu.VMEM((ta, tb), F32)],
        name=name, compiler_params=_cp(("parallel", "parallel", "arbitrary")))(a, b)


def _rms(x, g, n=None):
    n = x.shape[-1] if n is None else n
    r = lax.rsqrt(jnp.sum(x * x, axis=-1, keepdims=True) * (1.0 / n) + EPS)
    return x * r * g


def _silu(x):
    return x * jax.nn.sigmoid(x)


def _modulate(x, g, sc, sh):
    return _rms(x, g) * (1.0 + sc) + sh


def _swap(x, s):
    ax = x.ndim - 1
    w = x.shape[ax]
    lane = lax.broadcasted_iota(jnp.int32, x.shape, ax)
    lo = (lane & s) == 0
    return jnp.where(lo, pltpu.roll(x, w - s, ax), pltpu.roll(x, s, ax))


@functools.partial(jax.custom_vjp, nondiff_argnums=(3,))
def _rope(x, cos, sin, s):
    return x * cos + _swap(x, s) * sin


def _rope_fwd(x, cos, sin, s):
    return _rope(x, cos, sin, s), (cos, sin)


def _rope_bwd(s, res, g):
    cos, sin = res
    return g * cos - _swap(g, s) * sin, jnp.zeros_like(cos), jnp.zeros_like(sin)


_rope.defvjp(_rope_fwd, _rope_bwd)


@jax.custom_vjp
def _softplus(x):
    return jnp.maximum(x, 0.0) + jnp.log(1.0 + jnp.exp(-jnp.abs(x)))


def _softplus_fwd(x):
    return _softplus(x), x


def _softplus_bwd(x, g):
    return (g * jax.nn.sigmoid(x),)


_softplus.defvjp(_softplus_fwd, _softplus_bwd)


def _d(a, b, dims):
    return lax.dot_general(a.astype(BF16), b.astype(BF16), (dims, ((), ())), preferred_element_type=F32)


@jax.custom_vjp
def bdot(a, b):
    return _d(a, b, ((1,), (0,)))


bdot.defvjp(lambda a, b: (bdot(a, b), (a, b)),
            lambda r, g: (_d(g, r[1], ((1,), (1,))), _d(r[0], g, ((0,), (0,)))))


@jax.custom_vjp
def bdot_nt(a, b):
    return _d(a, b, ((1,), (1,)))


bdot_nt.defvjp(lambda a, b: (bdot_nt(a, b), (a, b)),
               lambda r, g: (_d(g, r[1], ((1,), (0,))), _d(g, r[0], ((0,), (0,)))))


@jax.custom_vjp
def bdot_tn(a, b):
    return _d(a, b, ((0,), (0,)))


bdot_tn.defvjp(lambda a, b: (bdot_tn(a, b), (a, b)),
               lambda r, g: (_d(r[1], g, ((1,), (1,))), _d(r[0], g, ((1,), (0,)))))


def _tri(rev):
    i = lax.broadcasted_iota(jnp.int32, (Q, Q), 0)
    j = lax.broadcasted_iota(jnp.int32, (Q, Q), 1)
    return (i <= j) if rev else (i >= j)


def _split3(a):
    hi = a.astype(BF16)
    r = a - hi.astype(F32)
    mid = r.astype(BF16)
    lo = (r - mid.astype(F32)).astype(BF16)
    return hi, mid, lo


def _cum_cols_impl(a, rev):
    t = _tri(rev).astype(BF16)
    return sum(jnp.dot(t, p, preferred_element_type=F32) for p in _split3(a))


def _cum_rows_impl(a, rev):
    t = _tri(not rev).astype(BF16)
    return sum(jnp.dot(p, t, preferred_element_type=F32) for p in _split3(a))


@functools.partial(jax.custom_vjp, nondiff_argnums=(1,))
def cum_cols(a, rev):
    return _cum_cols_impl(a, rev)


cum_cols.defvjp(lambda a, rev: (_cum_cols_impl(a, rev), None), lambda rev, _, g: (_cum_cols_impl(g, not rev),))


@functools.partial(jax.custom_vjp, nondiff_argnums=(1,))
def cum_rows(a, rev):
    return _cum_rows_impl(a, rev)


cum_rows.defvjp(lambda a, rev: (_cum_rows_impl(a, rev), None), lambda rev, _, g: (_cum_rows_impl(g, not rev),))


def _rs(w, cb=0):
    return pl.BlockSpec((RT, w), lambda i: (i, cb))


def _ps(shape):
    nd = len(shape)
    return pl.BlockSpec(shape, lambda i: (0,) * nd)


def _gs(w, cb, nlat):
    return pl.BlockSpec((1, 1, w), lambda i: (i // nlat, 0, cb))


def _rowcall(name, body, n, ins, outs, scratch=()):
    return pl.pallas_call(
        body, out_shape=[o[0] for o in outs], grid=(n // RT,), in_specs=[s for _, s in ins],
        out_specs=[s for _, s in outs], scratch_shapes=list(scratch), name=name,
        compiler_params=_cp(("arbitrary",)))(*[a for a, _ in ins])


def _acc(ref, val, first):
    @pl.when(first)
    def _():
        ref[...] = val

    @pl.when(jnp.logical_not(first))
    def _():
        ref[...] += val


def _sd(shape, dt):
    return jax.ShapeDtypeStruct(shape, dt)


def resid_mod_fwd(name, xp, o, mod_gt, gt_i, mod_n, sh_i, sc_i, norm_g, nlat):
    n = xp.shape[0]
    has_res = o is not None

    def body(*refs):
        if has_res:
            xp_ref, o_ref, gt_ref, sh_ref, sc_ref, g_ref, xn_ref, h_ref = refs
            xn = xp_ref[...] + gt_ref[0] * o_ref[...]
            xn_ref[...] = xn
        else:
            xp_ref, sh_ref, sc_ref, g_ref, h_ref = refs
            xn = xp_ref[...]
        h_ref[...] = _modulate(xn, g_ref[...], sc_ref[0], sh_ref[0]).astype(BF16)

    ins = [(xp, _rs(D))]
    if has_res:
        ins += [(o, _rs(D)), (mod_gt, _gs(D, gt_i, nlat))]
    ins += [(mod_n, _gs(D, sh_i, nlat)), (mod_n, _gs(D, sc_i, nlat)), (norm_g, _ps((1, D)))]
    outs = ([(_sd((n, D), F32), _rs(D))] if has_res else []) + [(_sd((n, D), BF16), _rs(D))]
    r = _rowcall(name, body, n, ins, outs)
    return (r[0], r[1]) if has_res else (xp, r[0])


def resid_mod_bwd(name, xn, dxn, dh, o, mod_gt, gt_i, mod_n, sh_i, sc_i, norm_g, nlat):
    n = xn.shape[0]
    has_res = o is not None

    def body(*refs):
        i = pl.program_id(0)
        if has_res:
            (xn_ref, dxn_ref, dh_ref, o_ref, gt_ref, sh_ref, sc_ref, g_ref,
             dx_ref, do_ref, dgt_ref, dsh_ref, dsc_ref, dg_ref) = refs
        else:
            xn_ref, dxn_ref, dh_ref, sh_ref, sc_ref, g_ref, dx_ref, dsh_ref, dsc_ref, dg_ref = refs
        _, vjp = jax.vjp(_modulate, xn_ref[...], g_ref[...], sc_ref[0], sh_ref[0])
        dx, dg, dsc, dsh = vjp(dh_ref[...])
        dx = dx + dxn_ref[...]
        dx_ref[...] = dx
        gfirst = (i == 0) | (i == nlat)
        _acc(dg_ref, dg, i == 0)
        _acc(dsh_ref, dsh[None], gfirst)
        _acc(dsc_ref, dsc[None], gfirst)
        if has_res:
            do_ref[...] = (gt_ref[0] * dx).astype(BF16)
            _acc(dgt_ref, jnp.sum(dx * o_ref[...], axis=0, keepdims=True)[None], gfirst)

    ins = [(xn, _rs(D)), (dxn, _rs(D)), (dh, _rs(D))]
    if has_res:
        ins += [(o, _rs(D)), (mod_gt, _gs(D, gt_i, nlat))]
    ins += [(mod_n, _gs(D, sh_i, nlat)), (mod_n, _gs(D, sc_i, nlat)), (norm_g, _ps((1, D)))]
    gacc = (_sd((2, 1, D), F32), _gs(D, 0, nlat))
    outs = [(_sd((n, D), F32), _rs(D))]
    if has_res:
        outs += [(_sd((n, D), BF16), _rs(D)), gacc]
    outs += [gacc, gacc, (_sd((1, D), F32), _ps((1, D)))]
    r = _rowcall(name, body, n, ins, outs)
    if has_res:
        return r
    return r[0], None, None, r[1], r[2], r[3]


def resid_loss(name, xp, o, mod_gt, gt_i, target, nlat):
    n = xp.shape[0]

    def body(xp_ref, o_ref, gt_ref, t_ref, loss_ref, dx_ref, do_ref, dgt_ref):
        i = pl.program_id(0)
        gt = gt_ref[0]

        @pl.when(i < nlat)
        def _():
            err = xp_ref[...] + gt * o_ref[...] - t_ref[...]
            dx = err * (1.0 / D)
            dx_ref[...] = dx
            do_ref[...] = (gt * dx).astype(BF16)
            _acc(loss_ref, jnp.full((1, 128), 0.5 / D, F32) * jnp.sum(err * err), i == 0)
            _acc(dgt_ref, jnp.sum(dx * o_ref[...], axis=0, keepdims=True)[None], i == 0)

        @pl.when(i >= nlat)
        def _():
            dx_ref[...] = jnp.zeros((RT, D), F32)
            do_ref[...] = jnp.zeros((RT, D), BF16)
            dgt_ref[...] = jnp.zeros((1, 1, D), F32)

    tgt_spec = pl.BlockSpec((RT, D), lambda i: (jnp.minimum(i, nlat - 1), 0))
    ins = [(xp, _rs(D)), (o, _rs(D)), (mod_gt, _gs(D, gt_i, nlat)), (target, tgt_spec)]
    outs = [(_sd((1, 128), F32), _ps((1, 128))), (_sd((n, D), F32), _rs(D)), (_sd((n, D), BF16), _rs(D)),
            (_sd((2, 1, D), F32), _gs(D, 0, nlat))]
    return _rowcall(name, body, n, ins, outs)


def mod_fwd(name, c8, w_mod, b_mod):
    tn = 1536

    def body(c_ref, w_ref, b_ref, o_ref, s_ref):
        s = _silu(c_ref[...]).astype(BF16)
        s_ref[...] = s
        o_ref[...] = jnp.dot(s, w_ref[...], preferred_element_type=F32) + b_ref[...]

    return pl.pallas_call(
        body, out_shape=[_sd((8, 6 * D), F32), _sd((8, D), BF16)], grid=(6 * D // tn,),
        in_specs=[pl.BlockSpec((8, D), lambda j: (0, 0)), pl.BlockSpec((D, tn), lambda j: (0, j)),
                  pl.BlockSpec((1, tn), lambda j: (0, j))],
        out_specs=[pl.BlockSpec((8, tn), lambda j: (0, j)), pl.BlockSpec((8, D), lambda j: (0, 0))],
        name=name, compiler_params=_cp(("arbitrary",)))(c8, w_mod, b_mod)


def mod_small_bwd(name, c8, dsilu, dmod8):
    def body(c_ref, ds_ref, dm_ref, dc_ref, db_ref):
        _, vjp = jax.vjp(_silu, c_ref[...])
        dc_ref[...] = vjp(ds_ref[...])[0]
        db_ref[...] = jnp.sum(dm_ref[...], axis=0, keepdims=True)

    return pl.pallas_call(
        body, out_shape=[_sd((8, D), F32), _sd((1, 6 * D), F32)], grid=(1,),
        in_specs=[pl.BlockSpec((8, D), lambda j: (0, 0)), pl.BlockSpec((8, D), lambda j: (0, 0)),
                  pl.BlockSpec((8, 6 * D), lambda j: (0, 0))],
        out_specs=[pl.BlockSpec((8, D), lambda j: (0, 0)), pl.BlockSpec((1, 6 * D), lambda j: (0, 0))],
        name=name, compiler_params=_cp(("arbitrary",)))(c8, dsilu, dmod8)


def _conv_taps(x, nlat):
    n = x.shape[0]
    r = lax.broadcasted_iota(jnp.int32, x.shape, 0)
    lo = jnp.where(r < nlat, 0, nlat)
    hi = jnp.where(r < nlat, nlat, n)
    taps = []
    for o in (-2, -1, 0, 1, 2):
        xs = x if o == 0 else pltpu.roll(x, (-o) % n, 0)
        t = r + o
        taps.append(jnp.where((t >= lo) & (t < hi), xs, 0.0))
    return taps


def conv_fwd(name, u, w, b, nlat_rows):
    n = u.shape[0]

    def body(x_ref, w_ref, b_ref, o_ref):
        taps = _conv_taps(x_ref[...], nlat_rows)
        wv = w_ref[...]
        pre = b_ref[...] + sum(taps[k] * wv[k:k + 1, :] for k in range(5))
        o_ref[...] = _silu(pre)

    return pl.pallas_call(
        body, out_shape=_sd((n, 1536), F32), grid=(12,),
        in_specs=[pl.BlockSpec((n, 128), lambda j: (0, C_XS // 128 + j)), pl.BlockSpec((5, 128), lambda j: (0, j)),
                  pl.BlockSpec((1, 128), lambda j: (0, j))],
        out_specs=pl.BlockSpec((n, 128), lambda j: (0, j)),
        name=name, compiler_params=_cp(("parallel",)))(u, w, b)


def conv_bwd(name, u, dact, w, b, nlat_rows):
    n = u.shape[0]

    def body(x_ref, da_ref, w_ref, b_ref, dx_ref, dw_ref, db_ref):
        taps = _conv_taps(x_ref[...], nlat_rows)
        wv = w_ref[...]
        pre = b_ref[...] + sum(taps[k] * wv[k:k + 1, :] for k in range(5))
        s = jax.nn.sigmoid(pre)
        dpre = da_ref[...] * (s * (1.0 + pre * (1.0 - s)))
        db_ref[...] = jnp.sum(dpre, axis=0, keepdims=True)
        rows = lax.broadcasted_iota(jnp.int32, (5, 128), 0)
        dw = jnp.zeros((5, 128), F32)
        for k in range(5):
            dw = dw + jnp.where(rows == k, jnp.sum(dpre * taps[k], axis=0, keepdims=True), 0.0)
        dw_ref[...] = dw
        r = lax.broadcasted_iota(jnp.int32, dpre.shape, 0)
        lo = jnp.where(r < nlat_rows, 0, nlat_rows)
        hi = jnp.where(r < nlat_rows, nlat_rows, n)
        dx = jnp.zeros_like(dpre)
        for k in range(5):
            o = k - 2
            ds = dpre if o == 0 else pltpu.roll(dpre, o % n, 0)
            t = r - o
            dx = dx + jnp.where((t >= lo) & (t < hi), ds, 0.0) * wv[k:k + 1, :]
        dx_ref[...] = dx.astype(BF16)

    return pl.pallas_call(
        body, out_shape=[_sd((n, 1536), BF16), _sd((5, 1536), F32), _sd((1, 1536), F32)], grid=(12,),
        in_specs=[pl.BlockSpec((n, 128), lambda j: (0, C_XS // 128 + j)), pl.BlockSpec((n, 128), lambda j: (0, j)),
                  pl.BlockSpec((5, 128), lambda j: (0, j)), pl.BlockSpec((1, 128), lambda j: (0, j))],
        out_specs=[pl.BlockSpec((n, 128), lambda j: (0, j)), pl.BlockSpec((5, 128), lambda j: (0, j)),
                   pl.BlockSpec((1, 128), lambda j: (0, j))],
        name=name, compiler_params=_cp(("parallel",)))(u, dact, w, b)


def _ssd_chunk(rev, dirn, g, x4, bm, cm, misc, dtrow, bias_c, alog_c, bias_r, alog_r, h4):
    dt_c = _softplus(misc + bias_c)
    a_c = dt_c * (-jnp.exp(alog_c))
    dt_r = _softplus(dtrow + bias_r)
    a_r = dt_r * (-jnp.exp(alog_r))
    cs_c = cum_cols(a_c, rev)
    cs_r = cum_rows(a_r, rev)
    tot_c = jnp.sum(a_c, axis=0, keepdims=True)
    cb = bdot_nt(cm, bm)
    tri = _tri(rev)
    lane = lax.broadcasted_iota(jnp.int32, (1, 128), 1)
    row16 = lax.broadcasted_iota(jnp.int32, (16, 1), 0)
    prow = lax.broadcasted_iota(jnp.int32, (128, 1), 0)
    ys, hs = [], []
    for p in range(4):
        ydiag = 0.0
        wst = 0.0
        eoff = 0.0
        hscale = 0.0
        for e in range(2):
            hg = 8 * g + 2 * p + e
            oh_c = (lane == DT_LANE + 16 * dirn + hg).astype(F32)
            dt_h = jnp.sum(dt_c * oh_c, axis=1, keepdims=True)
            cs_h = jnp.sum(cs_c * oh_c, axis=1, keepdims=True)
            tot_h = jnp.sum(tot_c * oh_c, axis=1, keepdims=True)
            csr_h = jnp.sum(cs_r * (row16 == hg).astype(F32), axis=0, keepdims=True)
            seg = jnp.exp(jnp.where(tri, cs_h - csr_h, -jnp.inf))
            hm = ((lane < 64) if e == 0 else (lane >= 64)).astype(F32)
            ydiag = ydiag + bdot(cb * seg, x4[p] * (dt_h * hm))
            wst = wst + (dt_h * jnp.exp(tot_h - cs_h)) * hm
            eoff = eoff + jnp.exp(cs_h) * hm
            hscale = hscale + jnp.exp(tot_h) * ((prow < 64) if e == 0 else (prow >= 64)).astype(F32)
        ys.append(ydiag + bdot_nt(cm, h4[p]) * eoff)
        hs.append(h4[p] * hscale + bdot_tn(x4[p] * wst, bm))
    return ys, hs


def _ssd_specs(nlat_chunks, rev, dirn, bwd):
    nc = nlat_chunks + 2

    def chunk(s):
        if bwd:
            s = nc - 1 - s
        return (nlat_chunks + 1 - s) if rev else (s + nlat_chunks) % nc

    def step(s):
        return (nc - 1 - s) if bwd else s

    return dict(
        x=pl.BlockSpec((Q, 512), lambda g, s: (chunk(s), g)),
        b=pl.BlockSpec((Q, 128), lambda g, s: (chunk(s), 8 + g)),
        c=pl.BlockSpec((Q, 128), lambda g, s: (chunk(s), 10 + g)),
        misc=pl.BlockSpec((Q, 128), lambda g, s: (chunk(s), C_MISC // 128)),
        dtrow=pl.BlockSpec((16, Q), lambda g, s: (dirn, chunk(s))),
        p_c=pl.BlockSpec((1, 128), lambda g, s: (0, 0)),
        p_r=pl.BlockSpec((16, 1), lambda g, s: (dirn, 0)),
        y=pl.BlockSpec((Q, 512), lambda g, s: (chunk(s), g)),
        hsave=pl.BlockSpec((1, 1, 512, 128), lambda g, s: (g, step(s), 0, 0)),
        bc_out=pl.BlockSpec((Q, 128), lambda g, s: (chunk(s), g)),
        misc_out=pl.BlockSpec((1, Q, 128), lambda g, s: (g, chunk(s), 0)),
        dtrow_out=pl.BlockSpec((1, 16, Q), lambda g, s: (g, 0, chunk(s))),
        pacc_c=pl.BlockSpec((1, 128), lambda g, s: (0, 0)),
        pacc_r=pl.BlockSpec((16, 1), lambda g, s: (0, 0)),
    )


def ssd_fwd(name, xbc, u, dtrow, bias_c, alog_c, bias_r, alog_r, nlat_chunks, rev, dirn):
    n = xbc.shape[0]
    nc = nlat_chunks + 2
    sp = _ssd_specs(nlat_chunks, rev, dirn, False)

    def body(x_ref, b_ref, c_ref, m_ref, r_ref, bc_ref, ac_ref, br_ref, ar_ref, y_ref, hs_ref, h_s):
        g = pl.program_id(0)
        s = pl.program_id(1)

        @pl.when(s == 0)
        def _():
            h_s[...] = jnp.zeros((512, 128), F32)

        hs_ref[0, 0] = h_s[...]
        x4 = [x_ref[:, 128 * p:128 * p + 128] for p in range(4)]
        h4 = [h_s[128 * p:128 * p + 128, :] for p in range(4)]
        ys, hs = _ssd_chunk(rev, dirn, g, x4, b_ref[...], c_ref[...], m_ref[...], r_ref[...],
                            bc_ref[...], ac_ref[...], br_ref[...], ar_ref[...], h4)
        for p in range(4):
            y_ref[:, 128 * p:128 * p + 128] = ys[p]
            h_s[128 * p:128 * p + 128, :] = hs[p]

    return pl.pallas_call(
        body, out_shape=[_sd((n, 1024), F32), _sd((2, nc, 512, 128), F32)], grid=(2, nc),
        in_specs=[sp["x"], sp["b"], sp["c"], sp["misc"], sp["dtrow"], sp["p_c"], sp["p_c"], sp["p_r"], sp["p_r"]],
        out_specs=[sp["y"], sp["hsave"]], scratch_shapes=[pltpu.VMEM((512, 128), F32)],
        name=name, compiler_params=_cp(("arbitrary", "arbitrary")))(
            xbc, xbc, xbc, u, dtrow, bias_c, alog_c, bias_r, alog_r)


def ssd_bwd(name, xbc, u, dtrow, bias_c, alog_c, bias_r, alog_r, hsave, dy, acc, nlat_chunks, rev, dirn):
    n = xbc.shape[0]
    sp = _ssd_specs(nlat_chunks, rev, dirn, True)

    def body(x_ref, b_ref, c_ref, m_ref, r_ref, bc_ref, ac_ref, br_ref, ar_ref, hs_ref, dy_ref, ax_ref, ab_ref, acc_ref,
             dx_ref, db_ref, dc_ref, dm_ref, dr_ref, dbc_ref, dac_ref, dbr_ref, dar_ref, dh_s):
        g = pl.program_id(0)
        s = pl.program_id(1)

        @pl.when(s == 0)
        def _():
            dh_s[...] = jnp.zeros((512, 128), F32)

        x4 = [x_ref[:, 128 * p:128 * p + 128] for p in range(4)]
        h4 = [hs_ref[0, 0, 128 * p:128 * p + 128, :] for p in range(4)]
        fn = functools.partial(_ssd_chunk, rev, dirn, g)
        _, vjp = jax.vjp(fn, x4, b_ref[...], c_ref[...], m_ref[...], r_ref[...],
                         bc_ref[...], ac_ref[...], br_ref[...], ar_ref[...], h4)
        dys = [dy_ref[:, 128 * p:128 * p + 128] for p in range(4)]
        dhs = [dh_s[128 * p:128 * p + 128, :] for p in range(4)]
        dx4, db, dc, dm, dr, dbc, dac, dbr, dar, dh4 = vjp((dys, dhs))
        for p in range(4):
            dx_ref[:, 128 * p:128 * p + 128] = dx4[p] + ax_ref[:, 128 * p:128 * p + 128]
            dh_s[128 * p:128 * p + 128, :] = dh4[p]
        db_ref[...] = db + ab_ref[...]
        dc_ref[...] = dc + acc_ref[...]
        dm_ref[0] = dm
        dr_ref[0] = dr
        first = (g == 0) & (s == 0)
        _acc(dbc_ref, dbc, first)
        _acc(dac_ref, dac, first)
        _acc(dbr_ref, dbr, first)
        _acc(dar_ref, dar, first)

    ax, ab, ac = acc
    return pl.pallas_call(
        body,
        out_shape=[_sd((n, 1024), F32), _sd((n, 256), F32), _sd((n, 256), F32), _sd((2, n, 128), F32),
                   _sd((2, 16, n), F32), _sd((1, 128), F32), _sd((1, 128), F32), _sd((16, 1), F32), _sd((16, 1), F32)],
        grid=(2, nlat_chunks + 2),
        in_specs=[sp["x"], sp["b"], sp["c"], sp["misc"], sp["dtrow"], sp["p_c"], sp["p_c"], sp["p_r"], sp["p_r"],
                  sp["hsave"], sp["y"], sp["y"], sp["bc_out"], sp["bc_out"]],
        out_specs=[sp["y"], sp["bc_out"], sp["bc_out"], sp["misc_out"], sp["dtrow_out"],
                   sp["pacc_c"], sp["pacc_c"], sp["pacc_r"], sp["pacc_r"]],
        scratch_shapes=[pltpu.VMEM((512, 128), F32)],
        name=name, compiler_params=_cp(("arbitrary", "arbitrary")))(
            xbc, xbc, xbc, u, dtrow, bias_c, alog_c, bias_r, alog_r, hsave, dy, ax, ab, ac)


def _ssd_out(yf, yb, xs, z, g, dexp):
    return _rms((yf + yb + dexp * xs) * _silu(z), g)


def ssd_out_fwd(name, yf, yb, xbc, u, g, dexp):
    n = yf.shape[0]

    def body(yf_ref, yb_ref, xs_ref, z_ref, g_ref, d_ref, o_ref):
        o_ref[...] = _ssd_out(yf_ref[...], yb_ref[...], xs_ref[...], z_ref[...], g_ref[...], d_ref[...]).astype(BF16)

    return _rowcall(name, body, n,
                    [(yf, _rs(D)), (yb, _rs(D)), (xbc, _rs(D, 0)), (u, _rs(D, C_Z // D)), (g, _ps((1, D))), (dexp, _ps((1, D)))],
                    [(_sd((n, D), BF16), _rs(D))])[0]


def ssd_out_bwd(name, yf, yb, xbc, u, g, dexp, dys):
    n = yf.shape[0]

    def body(yf_ref, yb_ref, xs_ref, z_ref, g_ref, d_ref, dys_ref, dy_ref, dxs_ref, dz_ref, dg_ref, dd_ref):
        i = pl.program_id(0)
        _, vjp = jax.vjp(_ssd_out, yf_ref[...], yb_ref[...], xs_ref[...], z_ref[...], g_ref[...], d_ref[...])
        dyf, _, dxs, dz, dg, dd = vjp(dys_ref[...])
        dy_ref[...] = dyf
        dxs_ref[...] = dxs
        dz_ref[...] = dz.astype(BF16)
        _acc(dg_ref, dg, i == 0)
        _acc(dd_ref, dd, i == 0)

    return _rowcall(name, body, n,
                    [(yf, _rs(D)), (yb, _rs(D)), (xbc, _rs(D, 0)), (u, _rs(D, C_Z // D)), (g, _ps((1, D))), (dexp, _ps((1, D))),
                     (dys, _rs(D))],
                    [(_sd((n, D), F32), _rs(D)), (_sd((n, D), F32), _rs(D)), (_sd((n, D), BF16), _rs(D)),
                     (_sd((1, D), F32), _ps((1, D))), (_sd((1, D), F32), _ps((1, D)))])


def _normrope(x, g, cos, sin, s, n=None):
    return _rope(_rms(x, g, n), cos, sin, s)


def swa_prep_fwd(name, u, gq, gk, cos, sin):
    n = u.shape[0]

    def body(q_ref, k_ref, gq_ref, gk_ref, cos_ref, sin_ref, qs_ref, ks_ref):
        cs, sn = cos_ref[...], sin_ref[...]
        for h in range(SWA_HQ):
            sl = slice(128 * h, 128 * h + 128)
            qs_ref[:, sl] = _normrope(q_ref[:, sl], gq_ref[...], cs, sn, 32).astype(BF16)
        for h in range(SWA_HKV):
            sl = slice(128 * h, 128 * h + 128)
            ks_ref[:, sl] = _normrope(k_ref[:, sl], gk_ref[...], cs, sn, 32).astype(BF16)

    return _rowcall(name, body, n,
                    [(u, _rs(1024, C_Q // 1024)), (u, _rs(256, C_K // 256)), (gq, _ps((1, 128))), (gk, _ps((1, 128))),
                     (cos, _rs(128)), (sin, _rs(128))],
                    [(_sd((n, 1024), BF16), _rs(1024)), (_sd((n, 256), BF16), _rs(256))])


def swa_prep_bwd(name, u, gq, gk, cos, sin, dqs, dks, dv):
    n = u.shape[0]

    def body(q_ref, k_ref, gq_ref, gk_ref, cos_ref, sin_ref, dqs_ref, dks_ref, dv_ref,
             dq_ref, dk_ref, dvo_ref, dgq_ref, dgk_ref):
        i = pl.program_id(0)
        cs, sn = cos_ref[...], sin_ref[...]
        fn = lambda x, g: _normrope(x, g, cs, sn, 32)
        dgq = jnp.zeros((1, 128), F32)
        dgk = jnp.zeros((1, 128), F32)
        for h in range(SWA_HQ):
            sl = slice(128 * h, 128 * h + 128)
            _, vjp = jax.vjp(fn, q_ref[:, sl], gq_ref[...])
            dx, dg = vjp(dqs_ref[:, sl])
            dq_ref[:, sl] = dx.astype(BF16)
            dgq = dgq + dg
        for h in range(SWA_HKV):
            sl = slice(128 * h, 128 * h + 128)
            _, vjp = jax.vjp(fn, k_ref[:, sl], gk_ref[...])
            dx, dg = vjp(dks_ref[:, sl])
            dk_ref[:, sl] = dx.astype(BF16)
            dgk = dgk + dg
        dvo_ref[...] = dv_ref[...].astype(BF16)
        _acc(dgq_ref, dgq, i == 0)
        _acc(dgk_ref, dgk, i == 0)

    return _rowcall(name, body, n,
                    [(u, _rs(1024, C_Q // 1024)), (u, _rs(256, C_K // 256)), (gq, _ps((1, 128))), (gk, _ps((1, 128))),
                     (cos, _rs(128)), (sin, _rs(128)), (dqs, _rs(1024)), (dks, _rs(256)), (dv, _rs(256))],
                    [(_sd((n, 1024), BF16), _rs(1024)), (_sd((n, 256), BF16), _rs(256)), (_sd((n, 256), BF16), _rs(256)),
                     (_sd((1, 128), F32), _ps((1, 128))), (_sd((1, 128), F32), _ps((1, 128)))])


def lat_norm_fwd(name, u, g_kv, g_q):
    n = u.shape[0]

    def body(ckv_ref, cq_ref, gkv_ref, gq_ref, okv_ref, oq_ref):
        okv_ref[...] = _rms(ckv_ref[...], gkv_ref[...]).astype(BF16)
        oq_ref[...] = _rms(cq_ref[...], gq_ref[...]).astype(BF16)

    return _rowcall(name, body, n,
                    [(u, _rs(256, C_CKV // 256)), (u, _rs(384, C_CQ // 384)), (g_kv, _ps((1, 256))), (g_q, _ps((1, 384)))],
                    [(_sd((n, 256), BF16), _rs(256)), (_sd((n, 384), BF16), _rs(384))])


def lat_norm_bwd(name, u, g_kv, g_q, dkvn, dqn):
    n = u.shape[0]

    def body(ckv_ref, cq_ref, gkv_ref, gq_ref, dkvn_ref, dqn_ref, dckv_ref, dcq_ref, dgkv_ref, dgq_ref):
        i = pl.program_id(0)
        _, vjp = jax.vjp(_rms, ckv_ref[...], gkv_ref[...])
        dx, dg = vjp(dkvn_ref[...])
        dckv_ref[...] = dx.astype(BF16)
        _acc(dgkv_ref, dg, i == 0)
        _, vjp = jax.vjp(_rms, cq_ref[...], gq_ref[...])
        dx, dg = vjp(dqn_ref[...])
        dcq_ref[...] = dx.astype(BF16)
        _acc(dgq_ref, dg, i == 0)

    return _rowcall(name, body, n,
                    [(u, _rs(256, C_CKV // 256)), (u, _rs(384, C_CQ // 384)), (g_kv, _ps((1, 256))), (g_q, _ps((1, 384))),
                     (dkvn, _rs(256)), (dqn, _rs(384))],
                    [(_sd((n, 256), BF16), _rs(256)), (_sd((n, 384), BF16), _rs(384)),
                     (_sd((1, 256), F32), _ps((1, 256))), (_sd((1, 384), F32), _ps((1, 384)))])


def _lane_lt64(x):
    return (lax.broadcasted_iota(jnp.int32, (1, 128), 1) < 64).astype(F32) * x


def _mla_krope(misc, g, cos, sin):
    return _normrope(_lane_lt64(misc), g, cos, sin, 16, MLA_ROPE)


def mla_prep_fwd(name, kv, qp, u, qg, kg, cos, sin):
    n = kv.shape[0]

    def body(kv_ref, v_ref, q_ref, m_ref, qg_ref, kg_ref, cos_ref, sin_ref, km_ref, qm_ref, vm_ref):
        cs, sn = cos_ref[...], sin_ref[...]
        vm_ref[...] = v_ref[...].astype(BF16)
        kr = _mla_krope(m_ref[...], kg_ref[:, 128:256], cs, sn).astype(BF16)
        for h in range(MLA_H):
            km_ref[:, 256 * h:256 * h + 128] = _rms(kv_ref[:, 128 * h:128 * h + 128], kg_ref[:, 0:128]).astype(BF16)
            km_ref[:, 256 * h + 128:256 * h + 256] = kr
            qm_ref[:, 256 * h:256 * h + 128] = _rms(q_ref[:, 256 * h:256 * h + 128], qg_ref[:, 0:128]).astype(BF16)
            qm_ref[:, 256 * h + 128:256 * h + 256] = _normrope(
                q_ref[:, 256 * h + 128:256 * h + 256], qg_ref[:, 128:256], cs, sn, 16, MLA_ROPE).astype(BF16)

    return _rowcall(name, body, n,
                    [(kv, _rs(1024, 0)), (kv, _rs(1024, 1)), (qp, _rs(2048)), (u, _rs(128, C_MISC // 128)), (qg, _ps((1, 256))),
                     (kg, _ps((1, 256))), (cos, _rs(128)), (sin, _rs(128))],
                    [(_sd((n, 2048), BF16), _rs(2048)), (_sd((n, 2048), BF16), _rs(2048)), (_sd((n, 1024), BF16), _rs(1024))])


def mla_prep_bwd(name, kv, qp, u, qg, kg, cos, sin, dkm, dqm, dv):
    n = kv.shape[0]

    def body(kv_ref, q_ref, m_ref, qg_ref, kg_ref, cos_ref, sin_ref, dkm_ref, dqm_ref, dv_ref,
             dkv_ref, dq_ref, dkr_ref, dqg_ref, dkg_ref):
        i = pl.program_id(0)
        cs, sn = cos_ref[...], sin_ref[...]
        fr = lambda x, g: _normrope(x, g, cs, sn, 16, MLA_ROPE)
        dkg_n = jnp.zeros((1, 128), F32)
        dqg_n = jnp.zeros((1, 128), F32)
        dqg_r = jnp.zeros((1, 128), F32)
        dkr_sum = jnp.zeros((RT, 128), F32)
        for h in range(MLA_H):
            dk_h = jnp.transpose(dkm_ref[256 * h:256 * h + 256, :])
            dkv_ref[:, 1024 + 128 * h:1024 + 128 * h + 128] = jnp.transpose(dv_ref[128 * h:128 * h + 128, :]).astype(BF16)
            _, vjp = jax.vjp(_rms, kv_ref[:, 128 * h:128 * h + 128], kg_ref[:, 0:128])
            dx, dg = vjp(dk_h[:, 0:128])
            dkv_ref[:, 128 * h:128 * h + 128] = dx.astype(BF16)
            dkg_n = dkg_n + dg
            dkr_sum = dkr_sum + dk_h[:, 128:256]
            _, vjp = jax.vjp(_rms, q_ref[:, 256 * h:256 * h + 128], qg_ref[:, 0:128])
            dx, dg = vjp(dqm_ref[:, 256 * h:256 * h + 128])
            dq_ref[:, 256 * h:256 * h + 128] = dx.astype(BF16)
            dqg_n = dqg_n + dg
            _, vjp = jax.vjp(fr, q_ref[:, 256 * h + 128:256 * h + 256], qg_ref[:, 128:256])
            dx, dg = vjp(dqm_ref[:, 256 * h + 128:256 * h + 256])
            dq_ref[:, 256 * h + 128:256 * h + 256] = dx.astype(BF16)
            dqg_r = dqg_r + dg
        _, vjp = jax.vjp(lambda m, g: _mla_krope(m, g, cs, sn), m_ref[...], kg_ref[:, 128:256])
        dm, dkg_r = vjp(dkr_sum)
        dkr_ref[...] = dm
        _acc(dqg_ref.at[:, 0:128], dqg_n, i == 0)
        _acc(dqg_ref.at[:, 128:256], dqg_r, i == 0)
        _acc(dkg_ref.at[:, 0:128], dkg_n, i == 0)
        _acc(dkg_ref.at[:, 128:256], dkg_r, i == 0)

    return _rowcall(name, body, n,
                    [(kv, _rs(1024, 0)), (qp, _rs(2048)), (u, _rs(128, C_MISC // 128)), (qg, _ps((1, 256))), (kg, _ps((1, 256))),
                     (cos, _rs(128)), (sin, _rs(128)), (dkm, pl.BlockSpec((2048, RT), lambda i: (0, i))), (dqm, _rs(2048)),
                     (dv, pl.BlockSpec((1024, RT), lambda i: (0, i)))],
                    [(_sd((n, 2048), BF16), _rs(2048)), (_sd((n, 2048), BF16), _rs(2048)), (_sd((n, 128), F32), _rs(128)),
                     (_sd((1, 256), F32), _ps((1, 256))), (_sd((1, 256), F32), _ps((1, 256)))])


def misc_combine(name, dkr, dm_f, dm_b, drow_t):
    n = dkr.shape[0]

    def body(a_ref, f_ref, b_ref, r_ref, o_ref):
        o_ref[...] = (a_ref[...] + f_ref[0] + f_ref[1] + b_ref[0] + b_ref[1] + r_ref[...]).astype(BF16)

    g2 = pl.BlockSpec((2, RT, 128), lambda i: (0, i, 0))
    return _rowcall(name, body, n, [(dkr, _rs(128)), (dm_f, g2), (dm_b, g2), (drow_t, _rs(128))],
                    [(_sd((n, 128), BF16), _rs(128))])[0]


def _f32(ref):
    return ref[...].astype(F32)


def _merge(g1, g2, g3, p1, p2, p3):
    return jax.nn.sigmoid(g1) * p1 + jax.nn.sigmoid(g2) * p2 + jax.nn.sigmoid(g3) * p3


def merge_fwd(name, u, p1, p2, p3):
    n = u.shape[0]

    def body(g1, g2, g3, a, b, c, o_ref):
        o_ref[...] = _merge(g1[...], g2[...], g3[...], _f32(a), _f32(b), _f32(c)).astype(BF16)

    return _rowcall(name, body, n, [(u, _rs(D, 0)), (u, _rs(D, 1)), (u, _rs(D, 2)), (p1, _rs(D)), (p2, _rs(D)), (p3, _rs(D))],
                    [(_sd((n, D), BF16), _rs(D))])[0]


def merge_bwd(name, u, p1, p2, p3, dm):
    n = u.shape[0]

    def body(g1, g2, g3, a, b, c, dm_ref, d1, d2, d3, dg_ref):
        _, vjp = jax.vjp(_merge, g1[...], g2[...], g3[...], _f32(a), _f32(b), _f32(c))
        r = vjp(dm_ref[...])
        for k in range(3):
            dg_ref[:, D * k:D * k + D] = r[k].astype(BF16)
        d1[...] = r[3].astype(BF16)
        d2[...] = r[4].astype(BF16)
        d3[...] = r[5].astype(BF16)

    return _rowcall(name, body, n,
                    [(u, _rs(D, 0)), (u, _rs(D, 1)), (u, _rs(D, 2)), (p1, _rs(D)), (p2, _rs(D)), (p3, _rs(D)), (dm, _rs(D))],
                    [(_sd((n, D), BF16), _rs(D))] * 3 + [(_sd((n, 3 * D), BF16), _rs(3 * D))])


def _swiglu(g, u):
    return _silu(g) * u


def swiglu_fwd(name, gu):
    n = gu.shape[0]

    def body(g_ref, u_ref, o_ref):
        o_ref[...] = _swiglu(_f32(g_ref), _f32(u_ref)).astype(BF16)

    return _rowcall(name, body, n, [(gu, _rs(FFN, 0)), (gu, _rs(FFN, 1))], [(_sd((n, FFN), BF16), _rs(FFN))])[0]


def swiglu_bwd(name, gu, da):
    n = gu.shape[0]

    def body(g_ref, u_ref, da_ref, o_ref):
        _, vjp = jax.vjp(_swiglu, _f32(g_ref), _f32(u_ref))
        dg, du = vjp(da_ref[...])
        o_ref[:, 0:FFN] = dg.astype(BF16)
        o_ref[:, FFN:2 * FFN] = du.astype(BF16)

    return _rowcall(name, body, n, [(gu, _rs(FFN, 0)), (gu, _rs(FFN, 1)), (da, _rs(FFN))],
                    [(_sd((n, 2 * FFN), BF16), _rs(2 * FFN))])[0]


FLASH_ROWS = 256


def _fold_lanes(x, op):
    acc = x[:, 0:128]
    for b in range(1, x.shape[1] // 128):
        acc = op(acc, x[:, 128 * b:128 * b + 128])
    return acc


def _band_mask(tq, tk, i, kb):
    qp = i * tq + lax.broadcasted_iota(jnp.int32, (tq, tk), 0)
    kp = kb * tk + lax.broadcasted_iota(jnp.int32, (tq, tk), 1)
    return jnp.abs(qp - kp) <= SWA_WIN


def flash_fwd(name, qa, ka, va, *, w, vw, hq, grp, vcol0, scale, nlat, tq, tk, band, sink, ctx_q, prev=None):
    n = qa.shape[0]
    cblk = nlat // NCTX
    band = band and not ctx_q
    assert not band, "latent rows of a banded attention go through swa_fwd_lat"
    if ctx_q:
        tq = tk = NCTX
        grid = (hq, 1, 1)
        qmap = lambda h, i, kk: (cblk, h)
        kmap = lambda h, i, kk: (cblk, h // grp)
        vmap = lambda h, i, kk: (cblk, vcol0 + h // grp)
        omap = lambda h, i, kk: (cblk, h)
        lmap = lambda h, i, kk: (h, cblk, 0)
    else:
        nb = nlat // tk
        nk = 3 if band else nb
        grid = (hq, nlat // tq, nk)
        kb_of = (lambda i, kk: jnp.clip(i + kk - 1, 0, nb - 1)) if band else (lambda i, kk: kk)
        qmap = lambda h, i, kk: (i, h)
        kmap = lambda h, i, kk: (kb_of(i, kk), h // grp)
        vmap = lambda h, i, kk: (kb_of(i, kk), vcol0 + h // grp)
        omap = lambda h, i, kk: (i, h)
        lmap = lambda h, i, kk: (h, i, 0)
    nk = grid[2]
    extra = not ctx_q
    has_sink = sink is not None

    def body(*refs):
        refs = list(refs)
        q_ref, k_ref, v_ref = refs[:3]
        pos = 3
        if extra:
            ke_ref, ve_ref = refs[pos:pos + 2]
            pos += 2
        if has_sink:
            s_ref = refs[pos]
            pos += 1
        if prev is not None:
            pos += 2
        o_ref, l_ref, m_s, l_s, a_s = refs[pos:pos + 5]
        kk = pl.program_id(2)
        tr = min(tq, FLASH_ROWS)

        def step(kblk, vblk):
            for r in range(tq // tr):
                rows = slice(r * tr, (r + 1) * tr)
                s = _d(q_ref[rows, :], kblk, ((1,), (1,))) * (scale * LOG2E)
                m_prev = m_s[rows, :]
                m_new = jnp.maximum(m_prev, jnp.max(_fold_lanes(s, jnp.maximum), axis=1, keepdims=True))
                alpha = jnp.exp2(m_prev - m_new)
                p = jnp.exp2(s - m_new)
                l_s[rows, :] = alpha * l_s[rows, :] + _fold_lanes(p, jnp.add)
                a_s[rows, :] = alpha * a_s[rows, :] + _d(p, vblk, ((1,), (0,)))
                m_s[rows, :] = m_new

        @pl.when(kk == 0)
        def _():
            if has_sink:
                sv = jnp.max(s_ref[0], axis=1, keepdims=True) * LOG2E
                m_s[...] = jnp.zeros((tq, 1), F32) + sv
                l_s[...] = (lax.broadcasted_iota(jnp.int32, (tq, 128), 1) == 0).astype(F32)
            else:
                m_s[...] = jnp.full((tq, 1), NEG, F32)
                l_s[...] = jnp.zeros((tq, 128), F32)
            a_s[...] = jnp.zeros((tq, vw), F32)
            if extra:
                step(ke_ref[...], ve_ref[...])

        step(k_ref[...], v_ref[...])

        @pl.when(kk == nk - 1)
        def _():
            l = jnp.sum(l_s[...], axis=1, keepdims=True)
            o_ref[...] = (a_s[...] / l).astype(BF16)
            l_ref[0] = m_s[...] + jnp.log2(l)

    ins = [(qa, pl.BlockSpec((tq, w), qmap)), (ka, pl.BlockSpec((tk, w), kmap)), (va, pl.BlockSpec((tk, vw), vmap))]
    if extra:
        ins += [(ka, pl.BlockSpec((NCTX, w), lambda h, i, kk: (cblk, h // grp))),
                (va, pl.BlockSpec((NCTX, vw), lambda h, i, kk: (cblk, vcol0 + h // grp)))]
    if has_sink:
        ins += [(sink, pl.BlockSpec((1, 1, 128), lambda h, i, kk: (h, 0, 0)))]
    aliases = {}
    if prev is not None:
        any_spec = pl.BlockSpec(memory_space=pl.ANY)
        aliases = {len(ins): 0, len(ins) + 1: 1}
        ins += [(prev[0], any_spec), (prev[1], any_spec)]
    return pl.pallas_call(
        body, out_shape=[_sd((n, hq * vw), BF16), _sd((hq, n, 1), F32)], grid=grid,
        in_specs=[s for _, s in ins],
        out_specs=[pl.BlockSpec((tq, vw), omap), pl.BlockSpec((1, tq, 1), lmap)],
        scratch_shapes=[pltpu.VMEM((tq, 1), F32), pltpu.VMEM((tq, 128), F32), pltpu.VMEM((tq, vw), F32)],
        input_output_aliases=aliases, name=name,
        compiler_params=_cp(("parallel", "parallel", "arbitrary")))(*[a for a, _ in ins])


def flash_dq(name, qa, ka, va, oa, doa, lse, *, w, vw, hq, grp, vcol0, scale, nlat, tq, tk, band, sink, ctx_q, prev=None):
    n = qa.shape[0]
    cblk = nlat // NCTX
    band = band and not ctx_q
    if ctx_q:
        tq = tk = NCTX
        grid = (hq, 1, 1)
        qmap = lambda h, i, kk: (cblk, h)
        kmap = lambda h, i, kk: (cblk, h // grp)
        vmap = lambda h, i, kk: (cblk, vcol0 + h // grp)
        lmap = lambda h, i, kk: (h, cblk, 0)
    else:
        nb = nlat // tk
        grid = (hq, nlat // tq, 3 if band else nb)
        kb_of = (lambda i, kk: jnp.clip(i + kk - 1, 0, nb - 1)) if band else (lambda i, kk: kk)
        qmap = lambda h, i, kk: (i, h)
        kmap = lambda h, i, kk: (kb_of(i, kk), h // grp)
        vmap = lambda h, i, kk: (kb_of(i, kk), vcol0 + h // grp)
        lmap = lambda h, i, kk: (h, i, 0)
    nk = grid[2]
    nq = grid[1]
    extra = not ctx_q
    has_sink = sink is not None

    def body(*refs):
        refs = list(refs)
        q_ref, k_ref, v_ref, o_ref, do_ref, l_ref = refs[:6]
        pos = 6
        if extra:
            ke_ref, ve_ref = refs[pos:pos + 2]
            pos += 2
        if has_sink:
            s_ref = refs[pos]
            pos += 1
        if prev is not None:
            pos += 2
        dq_ref, dl_ref, ds_ref, acc_s, dl_s = refs[pos:pos + 5]
        i = pl.program_id(1)
        kk = pl.program_id(2)
        q = q_ref[...]
        do = do_ref[...]
        lse_v = l_ref[0]

        def step(kblk, vblk, mask):
            s = _d(q, kblk, ((1,), (1,))) * (scale * LOG2E)
            if mask is not None:
                s = jnp.where(mask, s, NEG)
            p = jnp.exp2(s - lse_v)
            dp = _d(do, vblk, ((1,), (1,)))
            ds = p * (dp - dl_s[...]) * scale
            acc_s[...] += _d(ds, kblk, ((1,), (0,)))

        @pl.when(kk == 0)
        def _():
            delta = jnp.sum(do * o_ref[...].astype(F32), axis=1, keepdims=True)
            dl_s[...] = delta
            acc_s[...] = jnp.zeros((tq, w), F32)
            if has_sink:
                sv = jnp.max(s_ref[0], axis=1, keepdims=True) * LOG2E
                dsk = jnp.sum(-jnp.exp2(sv - lse_v) * delta, axis=0, keepdims=True)
                _acc(ds_ref, jnp.zeros((1, 1, 128), F32) + dsk, i == 0)
            else:
                ds_ref[...] = jnp.zeros((1, 1, 128), F32)
            if extra:
                step(ke_ref[...], ve_ref[...], None)

        if band:
            kb = i + kk - 1

            @pl.when((kb >= 0) & (kb < nlat // tk))
            def _():
                step(k_ref[...], v_ref[...], _band_mask(tq, tk, i, kb))
        else:
            step(k_ref[...], v_ref[...], None)

        @pl.when(kk == nk - 1)
        def _():
            dq_ref[...] = acc_s[...]
            dl_ref[0] = dl_s[...]

    ins = [(qa, pl.BlockSpec((tq, w), qmap)), (ka, pl.BlockSpec((tk, w), kmap)), (va, pl.BlockSpec((tk, vw), vmap)),
           (oa, pl.BlockSpec((tq, vw), qmap)), (doa, pl.BlockSpec((tq, vw), qmap)), (lse, pl.BlockSpec((1, tq, 1), lmap))]
    if extra:
        ins += [(ka, pl.BlockSpec((NCTX, w), lambda h, i, kk: (cblk, h // grp))),
                (va, pl.BlockSpec((NCTX, vw), lambda h, i, kk: (cblk, vcol0 + h // grp)))]
    if has_sink:
        ins += [(sink, pl.BlockSpec((1, 1, 128), lambda h, i, kk: (h, 0, 0)))]
    aliases = {}
    if prev is not None:
        any_spec = pl.BlockSpec(memory_space=pl.ANY)
        aliases = {len(ins): 0, len(ins) + 1: 1}
        ins += [(prev[0], any_spec), (prev[1], any_spec)]
    del nq
    return pl.pallas_call(
        body, out_shape=[_sd((n, hq * w), F32), _sd((hq, n, 1), F32), _sd((hq, 1, 128), F32)], grid=grid,
        in_specs=[s for _, s in ins],
        out_specs=[pl.BlockSpec((tq, w), qmap), pl.BlockSpec((1, tq, 1), lmap),
                   pl.BlockSpec((1, 1, 128), lambda h, i, kk: (h, 0, 0))],
        scratch_shapes=[pltpu.VMEM((tq, w), F32), pltpu.VMEM((tq, 1), F32)],
        input_output_aliases=aliases, name=name,
        compiler_params=_cp(("parallel", "arbitrary", "arbitrary")))(*[a for a, _ in ins])


def flash_dkv(name, qa, ka, va, doa, lse, delta, *, w, vw, hkv, grp, vcol0, scale, nlat, tq, tk, band, ctx_k, prev=None):
    n = qa.shape[0]
    cblk = nlat // NCTX
    nqb = nlat // tq
    band = band and not ctx_k
    if ctx_k:
        tk = NCTX
        nqs = nqb
        grid = (hkv, 1, grp * nqs)
        kmap = lambda hk, j, t: (cblk, hk)
        vmap = lambda hk, j, t: (cblk, vcol0 + hk)
        dvmap = lambda hk, j, t: (cblk, hk)
        qb_of = lambda j, t: t % nqs
    else:
        nqs = 3 if band else nqb
        grid = (hkv, nlat // tk, grp * nqs)
        kmap = lambda hk, j, t: (j, hk)
        vmap = lambda hk, j, t: (j, vcol0 + hk)
        dvmap = lambda hk, j, t: (j, hk)
        qb_of = (lambda j, t: jnp.clip(j + t % nqs - 1, 0, nqb - 1)) if band else (lambda j, t: t % nqs)
    qmap = lambda hk, j, t: (qb_of(j, t), hk * grp + t // nqs)
    lmap = lambda hk, j, t: (hk * grp + t // nqs, qb_of(j, t), 0)

    def body(*refs):
        refs = list(refs)
        q_ref, k_ref, v_ref, do_ref, l_ref, dl_ref = refs[:6]
        pos = 6
        if ctx_k:
            qe_ref, doe_ref, le_ref, dle_ref = refs[pos:pos + 4]
            pos += 4
        if prev is not None:
            pos += 2
        dk_ref, dv_ref = refs[pos:pos + 2]
        j = pl.program_id(1)
        t = pl.program_id(2)
        kblk = k_ref[...]
        vblk = v_ref[...]

        def contrib(q, do, lse_v, dl_v, mask):
            s = _d(q, kblk, ((1,), (1,))) * (scale * LOG2E)
            if mask is not None:
                s = jnp.where(mask, s, NEG)
            p = jnp.exp2(s - lse_v)
            dp = _d(do, vblk, ((1,), (1,)))
            ds = p * (dp - dl_v) * scale
            return _d(ds, q, ((0,), (0,))), _d(p, do, ((0,), (0,)))

        @pl.when(t == 0)
        def _():
            dk = jnp.zeros((tk, w), F32)
            dv = jnp.zeros((tk, vw), F32)
            if ctx_k:
                for gi in range(grp):
                    a, b = contrib(qe_ref[:, w * gi:w * gi + w], doe_ref[:, vw * gi:vw * gi + vw], le_ref[gi], dle_ref[gi], None)
                    dk = dk + a
                    dv = dv + b
            dk_ref[...] = dk
            dv_ref[...] = dv

        def add(mask):
            a, b = contrib(q_ref[...], do_ref[...], l_ref[0], dl_ref[0], mask)
            dk_ref[...] += a
            dv_ref[...] += b

        if band:
            qb = j + t % nqs - 1

            @pl.when((qb >= 0) & (qb < nqb))
            def _():
                add(_band_mask(tq, tk, qb, j))
        else:
            add(None)

    ins = [(qa, pl.BlockSpec((tq, w), qmap)), (ka, pl.BlockSpec((tk, w), kmap)), (va, pl.BlockSpec((tk, vw), vmap)),
           (doa, pl.BlockSpec((tq, vw), qmap)), (lse, pl.BlockSpec((1, tq, 1), lmap)), (delta, pl.BlockSpec((1, tq, 1), lmap))]
    if ctx_k:
        ins += [(qa, pl.BlockSpec((NCTX, grp * w), lambda hk, j, t: (cblk, hk))),
                (doa, pl.BlockSpec((NCTX, grp * vw), lambda hk, j, t: (cblk, hk))),
                (lse, pl.BlockSpec((grp, NCTX, 1), lambda hk, j, t: (hk, cblk, 0))),
                (delta, pl.BlockSpec((grp, NCTX, 1), lambda hk, j, t: (hk, cblk, 0)))]
    aliases = {}
    if prev is not None:
        any_spec = pl.BlockSpec(memory_space=pl.ANY)
        aliases = {len(ins): 0, len(ins) + 1: 1}
        ins += [(prev[0], any_spec), (prev[1], any_spec)]
    return pl.pallas_call(
        body, out_shape=[_sd((n, hkv * w), F32), _sd((n, hkv * vw), F32)], grid=grid,
        in_specs=[s for _, s in ins],
        out_specs=[pl.BlockSpec((tk, w), kmap), pl.BlockSpec((tk, vw), dvmap)],
        input_output_aliases=aliases, name=name,
        compiler_params=_cp(("parallel", "parallel", "arbitrary")))(*[a for a, _ in ins])


MLA_FWD_CHUNKS = 3
MLA_BWD_CHUNKS = 2


def mla_fwd(name, qm, km, vm, nlat):
    n = qm.shape[0]
    t = NCTX
    nlt = nlat // t
    c = (MLA_NOPE + MLA_ROPE) ** -0.5 * LOG2E

    nchunk = MLA_FWD_CHUNKS if (n // 128) % MLA_FWD_CHUNKS == 0 else 1
    cw = n // nchunk

    def body(q_ref, k_ref, v_ref, o_ref, l_ref):
        i = pl.program_id(1)

        def run(spans):
            parts = []
            for a, b in spans:
                s = _d(q_ref[...], k_ref[a:b, :], ((1,), (1,))) * c
                m = jnp.max(_fold_lanes(s, jnp.maximum), axis=1, keepdims=True)
                p = jnp.exp2(s - m)
                parts.append((m, jnp.sum(_fold_lanes(p, jnp.add), axis=1, keepdims=True), _d(p, v_ref[a:b, :], ((1,), (0,)))))
            m = functools.reduce(jnp.maximum, [pt[0] for pt in parts])
            l = sum(jnp.exp2(pm - m) * pl_ for pm, pl_, _ in parts)
            acc = sum(jnp.exp2(pm - m) * pa for pm, _, pa in parts)
            o_ref[...] = (acc / l).astype(BF16)
            l_ref[0] = m + jnp.log2(l)

        @pl.when(i < nlt)
        def _():
            run([(j * cw, (j + 1) * cw) for j in range(nchunk)])

        @pl.when(i == nlt)
        def _():
            run([(nlat, n)])

    return pl.pallas_call(
        body, out_shape=[_sd((n, MLA_H * 128), BF16), _sd((MLA_H, n, 1), F32)], grid=(MLA_H, n // t),
        in_specs=[pl.BlockSpec((t, 256), lambda h, i: (i, h)), pl.BlockSpec((n, 256), lambda h, i: (0, h)),
                  pl.BlockSpec((n, 128), lambda h, i: (0, h))],
        out_specs=[pl.BlockSpec((t, 128), lambda h, i: (i, h)), pl.BlockSpec((1, t, 1), lambda h, i: (h, i, 0))],
        name=name, compiler_params=_cp(("parallel", "arbitrary")))(qm, km, vm)


def mla_bwd(name, qm, km, vm, o, do, lse, nlat):
    n = qm.shape[0]
    t = NCTX
    nlt = nlat // t
    scale = (MLA_NOPE + MLA_ROPE) ** -0.5
    nchunk = MLA_BWD_CHUNKS if (n // 128) % MLA_BWD_CHUNKS == 0 else 1
    cw = n // nchunk

    def body(q_ref, k_ref, v_ref, o_ref, do_ref, l_ref, dq_ref, dkt_ref, dvt_ref):
        i = pl.program_id(1)

        @pl.when(i == 0)
        def _():
            dkt_ref[...] = jnp.zeros((256, n), F32)
            dvt_ref[...] = jnp.zeros((128, n), F32)

        q = q_ref[...]
        do = do_ref[...]
        delta = jnp.sum(do.astype(F32) * o_ref[...].astype(F32), axis=1, keepdims=True)

        def run(spans):
            dq = jnp.zeros((t, 256), F32)
            for a, b in spans:
                kc = k_ref[a:b, :]
                s = _d(q, kc, ((1,), (1,))) * (scale * LOG2E)
                p = jnp.exp2(s - l_ref[0])
                ds = (p * (_d(do, v_ref[a:b, :], ((1,), (1,))) - delta) * scale).astype(BF16)
                dq = dq + _d(ds, kc, ((1,), (0,)))
                dkt_ref[:, a:b] += _d(q, ds, ((0,), (0,)))
                dvt_ref[:, a:b] += _d(do, p, ((0,), (0,)))
            dq_ref[...] = dq

        @pl.when(i < nlt)
        def _():
            run([(c * cw, (c + 1) * cw) for c in range(nchunk)])

        @pl.when(i == nlt)
        def _():
            run([(nlat, n)])

    qspec = pl.BlockSpec((t, 256), lambda h, i: (i, h))
    ospec = pl.BlockSpec((t, 128), lambda h, i: (i, h))
    return pl.pallas_call(
        body, out_shape=[_sd((n, MLA_H * 256), F32), _sd((MLA_H * 256, n), F32), _sd((MLA_H * 128, n), F32)],
        grid=(MLA_H, n // t),
        in_specs=[qspec, pl.BlockSpec((n, 256), lambda h, i: (0, h)), pl.BlockSpec((n, 128), lambda h, i: (0, h)),
                  ospec, ospec, pl.BlockSpec((1, t, 1), lambda h, i: (h, i, 0))],
        out_specs=[qspec, pl.BlockSpec((256, n), lambda h, i: (h, 0)), pl.BlockSpec((128, n), lambda h, i: (h, 0))],
        name=name, compiler_params=_cp(("parallel", "arbitrary")))(qm, km, vm, o, do, lse)


def mla_attention_bwd(tag, qm, km, vm, o, do, lse, nlat):
    return mla_bwd(tag + "_bwd", qm, km, vm, o, do, lse, nlat)


SWA_T = 512


def _swa_window(t, nlat):
    t = min(t, nlat)
    return t, min(t + 2 * SWA_WIN, nlat)


def _win_start(i, t, wlen, nlat):
    return pl.multiple_of(jnp.clip(i * t - SWA_WIN, 0, nlat - wlen), 128)


def _win_mask(rows, cols, row0, col0):
    rp = row0 + lax.broadcasted_iota(jnp.int32, (rows, cols), 0)
    cp = col0 + lax.broadcasted_iota(jnp.int32, (rows, cols), 1)
    return jnp.abs(rp - cp) <= SWA_WIN


def swa_fwd_lat(name, qs, ks, u, sink, nlat):
    n = qs.shape[0]
    tq, wlen = _swa_window(SWA_T, nlat)
    grp = SWA_HQ // SWA_HKV
    scale = SWA_DH ** -0.5
    vcol0 = C_V // 128

    def body(q_ref, k_ref, v_ref, s_ref, o_ref, l_ref):
        i = pl.program_id(1)
        ws = _win_start(i, tq, wlen, nlat)
        q = q_ref[...]
        s1 = _d(q, k_ref[pl.ds(ws, wlen), :], ((1,), (1,))) * (scale * LOG2E)
        s1 = jnp.where(_win_mask(tq, wlen, i * tq, ws), s1, NEG)
        s2 = _d(q, k_ref[pl.ds(nlat, NCTX), :], ((1,), (1,))) * (scale * LOG2E)
        sv = jnp.max(s_ref[0], axis=1, keepdims=True) * LOG2E
        m = jnp.maximum(jnp.maximum(jnp.max(s1, axis=1, keepdims=True), jnp.max(s2, axis=1, keepdims=True)), sv)
        p1 = jnp.exp2(s1 - m)
        p2 = jnp.exp2(s2 - m)
        l = jnp.sum(p1, axis=1, keepdims=True) + jnp.sum(p2, axis=1, keepdims=True) + jnp.exp2(sv - m)
        acc = _d(p1, v_ref[pl.ds(ws, wlen), :], ((1,), (0,))) + _d(p2, v_ref[pl.ds(nlat, NCTX), :], ((1,), (0,)))
        o_ref[...] = (acc / l).astype(BF16)
        l_ref[0] = m + jnp.log2(l)

    return pl.pallas_call(
        body, out_shape=[_sd((n, SWA_HQ * 128), BF16), _sd((SWA_HQ, n, 1), F32)], grid=(SWA_HQ, nlat // tq),
        in_specs=[pl.BlockSpec((tq, 128), lambda h, i: (i, h)), pl.BlockSpec((n, 128), lambda h, i: (0, h // grp)),
                  pl.BlockSpec((n, 128), lambda h, i: (0, vcol0 + h // grp)), pl.BlockSpec((1, 1, 128), lambda h, i: (h, 0, 0))],
        out_specs=[pl.BlockSpec((tq, 128), lambda h, i: (i, h)), pl.BlockSpec((1, tq, 1), lambda h, i: (h, i, 0))],
        name=name, compiler_params=_cp(("parallel", "arbitrary")))(qs, ks, u, sink)


def swa_dq_lat(name, qs, ks, u, o, do, lse, sink, nlat):
    n = qs.shape[0]
    tq, wlen = _swa_window(SWA_T, nlat)
    grp = SWA_HQ // SWA_HKV
    scale = SWA_DH ** -0.5
    vcol0 = C_V // 128

    def body(q_ref, k_ref, v_ref, s_ref, o_ref, do_ref, l_ref, dq_ref, dl_ref, ds_ref):
        i = pl.program_id(1)
        ws = _win_start(i, tq, wlen, nlat)
        q = q_ref[...]
        do = do_ref[...]
        lse_v = l_ref[0]
        delta = jnp.sum(do.astype(F32) * o_ref[...].astype(F32), axis=1, keepdims=True)
        kw = k_ref[pl.ds(ws, wlen), :]
        kc = k_ref[pl.ds(nlat, NCTX), :]
        s1 = _d(q, kw, ((1,), (1,))) * (scale * LOG2E)
        s1 = jnp.where(_win_mask(tq, wlen, i * tq, ws), s1, NEG)
        s2 = _d(q, kc, ((1,), (1,))) * (scale * LOG2E)
        ds1 = jnp.exp2(s1 - lse_v) * (_d(do, v_ref[pl.ds(ws, wlen), :], ((1,), (1,))) - delta) * scale
        ds2 = jnp.exp2(s2 - lse_v) * (_d(do, v_ref[pl.ds(nlat, NCTX), :], ((1,), (1,))) - delta) * scale
        dq_ref[...] = _d(ds1, kw, ((1,), (0,))) + _d(ds2, kc, ((1,), (0,)))
        dl_ref[0] = delta
        sv = jnp.max(s_ref[0], axis=1, keepdims=True) * LOG2E
        dsk = jnp.sum(-jnp.exp2(sv - lse_v) * delta, axis=0, keepdims=True)
        _acc(ds_ref, jnp.zeros((1, 1, 128), F32) + dsk, i == 0)

    qspec = pl.BlockSpec((tq, 128), lambda h, i: (i, h))
    lspec = pl.BlockSpec((1, tq, 1), lambda h, i: (h, i, 0))
    return pl.pallas_call(
        body, out_shape=[_sd((n, SWA_HQ * 128), F32), _sd((SWA_HQ, n, 1), F32), _sd((SWA_HQ, 1, 128), F32)],
        grid=(SWA_HQ, nlat // tq),
        in_specs=[qspec, pl.BlockSpec((n, 128), lambda h, i: (0, h // grp)),
                  pl.BlockSpec((n, 128), lambda h, i: (0, vcol0 + h // grp)), pl.BlockSpec((1, 1, 128), lambda h, i: (h, 0, 0)),
                  qspec, qspec, lspec],
        out_specs=[qspec, lspec, pl.BlockSpec((1, 1, 128), lambda h, i: (h, 0, 0))],
        name=name, compiler_params=_cp(("parallel", "arbitrary")))(qs, ks, u, sink, o, do, lse)


def swa_dkv_lat(name, qs, ks, u, do, lse_row, delta_row, nlat):
    n = qs.shape[0]
    tk, wlen = _swa_window(SWA_T, nlat)
    grp = SWA_HQ // SWA_HKV
    scale = SWA_DH ** -0.5
    vcol0 = C_V // 128

    def body(q_ref, k_ref, v_ref, do_ref, l_ref, dl_ref, dk_ref, dv_ref):
        j = pl.program_id(1)
        ws = _win_start(j, tk, wlen, nlat)
        k = k_ref[...]
        v = v_ref[...]
        mask = _win_mask(tk, wlen, j * tk, ws)
        dk = jnp.zeros((tk, 128), F32)
        dv = jnp.zeros((tk, 128), F32)
        for gi in range(grp):
            qw = q_ref[pl.ds(ws, wlen), 128 * gi:128 * gi + 128]
            dow = do_ref[pl.ds(ws, wlen), 128 * gi:128 * gi + 128]
            st = jnp.where(mask, _d(k, qw, ((1,), (1,))) * (scale * LOG2E), NEG)
            pt = jnp.exp2(st - l_ref[gi, :, pl.ds(ws, wlen)])
            dv = dv + _d(pt, dow, ((1,), (0,)))
            dst = pt * (_d(v, dow, ((1,), (1,))) - dl_ref[gi, :, pl.ds(ws, wlen)]) * scale
            dk = dk + _d(dst, qw, ((1,), (0,)))
        dk_ref[...] = dk
        dv_ref[...] = dv

    rspec = pl.BlockSpec((grp, 1, n), lambda hk, j: (hk, 0, 0))
    return pl.pallas_call(
        body, out_shape=[_sd((n, SWA_HKV * 128), F32), _sd((n, SWA_HKV * 128), F32)], grid=(SWA_HKV, nlat // tk),
        in_specs=[pl.BlockSpec((n, grp * 128), lambda hk, j: (0, hk)), pl.BlockSpec((tk, 128), lambda hk, j: (j, hk)),
                  pl.BlockSpec((tk, 128), lambda hk, j: (j, vcol0 + hk)), pl.BlockSpec((n, grp * 128), lambda hk, j: (0, hk)),
                  rspec, rspec],
        out_specs=[pl.BlockSpec((tk, 128), lambda hk, j: (j, hk)), pl.BlockSpec((tk, 128), lambda hk, j: (j, hk))],
        name=name, compiler_params=_cp(("parallel", "arbitrary")))(qs, ks, u, do, lse_row, delta_row)


def swa_attention_fwd(tag, qs, ks, u, sink, cfg, nlat):
    o, lse = swa_fwd_lat(tag + "_fwd_lat", qs, ks, u, sink, nlat)
    return flash_fwd(tag + "_fwd_ctx", qs, ks, u, sink=sink, ctx_q=True, nlat=nlat, prev=(o, lse), **cfg)


def swa_attention_bwd(tag, qs, ks, u, o, do, lse, sink, cfg, nlat):
    n = qs.shape[0]
    dq, delta, ds1 = swa_dq_lat(tag + "_dq_lat", qs, ks, u, o, do, lse, sink, nlat)
    dq, delta, ds2 = flash_dq(tag + "_dq_ctx", qs, ks, u, o, do, lse, sink=sink, ctx_q=True, nlat=nlat, prev=(dq, delta), **cfg)
    dk, dv = swa_dkv_lat(tag + "_dkv_lat", qs, ks, u, do, lse.reshape(SWA_HQ, 1, n), delta.reshape(SWA_HQ, 1, n), nlat)
    kc = {k: v for k, v in cfg.items() if k != "hq"}
    kc["hkv"] = SWA_HKV
    kc["tq"] = min(1024, nlat)
    dk, dv = flash_dkv(tag + "_dkv_ctx", qs, ks, u, do, lse, delta, ctx_k=True, nlat=nlat, prev=(dk, dv), **kc)
    return dq, dk, dv, ds1 + ds2


def adamw(name, w, g, m, v):
    r, c = w.shape
    tr = _pick(r, (256, 128, 64, 32, 16, 8))
    bc1 = 1.0 - ADAM_B1 ** ADAM_STEP
    bc2 = 1.0 - ADAM_B2 ** ADAM_STEP

    def body(w_ref, g_ref, m_ref, v_ref, d_ref, nm_ref, nv_ref):
        gv = g_ref[...]
        nm = ADAM_B1 * m_ref[...] + (1.0 - ADAM_B1) * gv
        nv = ADAM_B2 * v_ref[...] + (1.0 - ADAM_B2) * (gv * gv)
        d_ref[...] = -ADAM_LR * ((nm / bc1) / (jnp.sqrt(nv / bc2) + ADAM_EPS) + ADAM_WD * w_ref[...])
        nm_ref[...] = nm
        nv_ref[...] = nv

    spec = pl.BlockSpec((tr, c), lambda i: (i, 0))
    return pl.pallas_call(body, out_shape=[_sd((r, c), F32)] * 3, grid=(r // tr,), in_specs=[spec] * 4, out_specs=[spec] * 3,
                          name=name, compiler_params=_cp(("parallel",)))(w, g, m, v)


def _coords():
    return lax.axis_index("x"), lax.axis_index("y"), lax.axis_index("c")


_ANY = pl.BlockSpec(memory_space=pl.ANY)


def _chip():
    return 2 * lax.axis_index("x") + lax.axis_index("y")


def _per_core(fn):
    c = lax.axis_index("c")
    for cs in (0, 1):
        pl.when(c == cs)(functools.partial(fn, cs))


def gather_chips_one(name, a):
    r = a.shape[0]
    half = r // 2

    def body(a_ref, o_ref, ici_send, ici_recv, d2d_send, d2d_recv):
        _per_core(functools.partial(run, a_ref, o_ref, ici_send, ici_recv, d2d_send, d2d_recv))

    def run(a_ref, o_ref, ici_send, ici_recv, d2d_send, d2d_recv, c):
        x, y, _ = _coords()
        me = 2 * x + y
        peers = [(1 - x, y), (x, 1 - y), (1 - x, 1 - y)]
        my_rows = pl.ds(c * half, half)
        sib_rows = pl.ds((1 - c) * half, half)
        sends = [pltpu.make_async_remote_copy(a_ref.at[my_rows], o_ref.at[me, my_rows], ici_send.at[k], ici_recv.at[k],
                                              device_id=(px, py, c), device_id_type=MESH)
                 for k, (px, py) in enumerate(peers)]
        for cp in sends:
            cp.start()
        passed = []
        for k, (px, py) in enumerate(peers):
            s = 2 * px + py
            pltpu.make_async_remote_copy(a_ref.at[my_rows], o_ref.at[s, my_rows], ici_send.at[k], ici_recv.at[k],
                                         device_id=(px, py, c), device_id_type=MESH).wait_recv()
            fw = pltpu.make_async_remote_copy(o_ref.at[s, my_rows], o_ref.at[s, my_rows], d2d_send.at[k], d2d_recv.at[k],
                                              device_id=(x, y, 1 - c), device_id_type=MESH)
            fw.start()
            passed.append(fw)
        for k, (px, py) in enumerate(peers):
            s = 2 * px + py
            pltpu.make_async_remote_copy(o_ref.at[s, sib_rows], o_ref.at[s, sib_rows], d2d_send.at[k], d2d_recv.at[k],
                                         device_id=(x, y, 1 - c), device_id_type=MESH).wait_recv()
        for cp in sends + passed:
            cp.wait_send()

    out = pl.pallas_call(
        body, out_shape=_sd((4,) + a.shape, a.dtype), in_specs=[_ANY], out_specs=_ANY,
        scratch_shapes=[pltpu.SemaphoreType.DMA((3,)), pltpu.SemaphoreType.DMA((3,)), pltpu.SemaphoreType.DMA((3,)),
                        pltpu.SemaphoreType.DMA((3,))],
        name=name, compiler_params=pltpu.CompilerParams(has_side_effects=True))(a)
    return lax.dynamic_update_index_in_dim(out, a, _chip(), 0)


def pair_split_one(name, a):
    k4, r, cdim = a.shape
    half = r // 2

    def body(a_ref, got_ref, send_sem, recv_sem):
        _per_core(functools.partial(run, a_ref, got_ref, send_sem, recv_sem))

    def run(a_ref, got_ref, send_sem, recv_sem, c):
        x, y, _ = _coords()
        sib_rows = pl.ds((1 - c) * half, half)
        cp = pltpu.make_async_remote_copy(a_ref.at[:, sib_rows], got_ref, send_sem, recv_sem,
                                          device_id=(x, y, 1 - c), device_id_type=MESH)
        cp.start()
        cp.wait()

    got = pl.pallas_call(
        body, out_shape=_sd((k4, half, cdim), a.dtype), in_specs=[_ANY], out_specs=_ANY,
        scratch_shapes=[pltpu.SemaphoreType.DMA, pltpu.SemaphoreType.DMA],
        name=name, compiler_params=pltpu.CompilerParams(has_side_effects=True))(a)
    return lax.dynamic_slice_in_dim(a, lax.axis_index("c") * half, half, axis=1), got


def scatter_chips_one(name, a):
    def body(a_ref, o_ref, send_sems, recv_sems):
        x, y, c = _coords()
        me = 2 * x + y
        peers = [(1 - x, y), (x, 1 - y), (1 - x, 1 - y)]
        sends = [pltpu.make_async_remote_copy(a_ref.at[2 * px + py], o_ref.at[me], send_sems.at[k], recv_sems.at[k],
                                              device_id=(px, py, c), device_id_type=MESH)
                 for k, (px, py) in enumerate(peers)]
        for cp in sends:
            cp.start()
        for k, (px, py) in enumerate(peers):
            pltpu.make_async_remote_copy(a_ref.at[me], o_ref.at[2 * px + py], send_sems.at[k], recv_sems.at[k],
                                         device_id=(px, py, c), device_id_type=MESH).wait_recv()
        for cp in sends:
            cp.wait_send()

    out = pl.pallas_call(
        body, out_shape=_sd(a.shape, a.dtype), in_specs=[_ANY], out_specs=_ANY,
        scratch_shapes=[pltpu.SemaphoreType.DMA((3,)), pltpu.SemaphoreType.DMA((3,))],
        name=name, compiler_params=pltpu.CompilerParams(has_side_effects=True))(a)
    return lax.dynamic_update_index_in_dim(out, lax.dynamic_index_in_dim(a, _chip(), 0, keepdims=False), _chip(), 0)


def pair_join_one(name, a):
    half, cdim = a.shape

    def body(a_ref, o_ref, send_sem, recv_sem):
        _per_core(functools.partial(run, a_ref, o_ref, send_sem, recv_sem))

    def run(a_ref, o_ref, send_sem, recv_sem, c):
        x, y, _ = _coords()
        my_rows = pl.ds(c * half, half)
        sib_rows = pl.ds((1 - c) * half, half)
        cp = pltpu.make_async_remote_copy(a_ref, o_ref.at[my_rows], send_sem, recv_sem, device_id=(x, y, 1 - c),
                                          device_id_type=MESH)
        cp.start()
        cp.wait_send()
        pltpu.make_async_remote_copy(a_ref, o_ref.at[sib_rows], send_sem, recv_sem, device_id=(x, y, 1 - c),
                                     device_id_type=MESH).wait_recv()

    out = pl.pallas_call(
        body, out_shape=_sd((2 * half, cdim), a.dtype), in_specs=[_ANY], out_specs=_ANY,
        scratch_shapes=[pltpu.SemaphoreType.DMA, pltpu.SemaphoreType.DMA],
        name=name, compiler_params=pltpu.CompilerParams(has_side_effects=True))(a)
    return lax.dynamic_update_slice_in_dim(out, a, lax.axis_index("c") * half, axis=0)


def gather_chips(name, arrs):
    nj = len(arrs)
    halves = [a.shape[0] // 2 for a in arrs]

    def body(*refs):
        _per_core(functools.partial(run, refs[:nj], refs[nj:2 * nj], *refs[2 * nj:]))

    def run(a_refs, o_refs, ici_send, ici_recv, d2d_send, d2d_recv, c):
        x, y, _ = _coords()
        me = 2 * x + y
        peers = [(1 - x, y), (x, 1 - y), (1 - x, 1 - y)]
        mine = [pl.ds(c * h, h) for h in halves]
        sibs = [pl.ds((1 - c) * h, h) for h in halves]

        def ici(k, j, blk):
            return pltpu.make_async_remote_copy(a_refs[j].at[mine[j]], o_refs[j].at[blk, mine[j]], ici_send.at[k * nj + j],
                                                ici_recv.at[k * nj + j], device_id=(*peers[k], c), device_id_type=MESH)

        def d2d(k, j, rows):
            blk = 2 * peers[k][0] + peers[k][1]
            return pltpu.make_async_remote_copy(o_refs[j].at[blk, rows[j]], o_refs[j].at[blk, rows[j]], d2d_send.at[k * nj + j],
                                                d2d_recv.at[k * nj + j], device_id=(x, y, 1 - c), device_id_type=MESH)

        sends = [ici(k, j, me) for k in range(3) for j in range(nj)]
        for cp in sends:
            cp.start()
        passed = []
        for k in range(3):
            for j in range(nj):
                ici(k, j, 2 * peers[k][0] + peers[k][1]).wait_recv()
                fw = d2d(k, j, mine)
                fw.start()
                passed.append(fw)
        for k in range(3):
            for j in range(nj):
                d2d(k, j, sibs).wait_recv()
        for cp in sends + passed:
            cp.wait_send()

    outs = pl.pallas_call(
        body, out_shape=[_sd((4,) + a.shape, a.dtype) for a in arrs], in_specs=[_ANY] * nj, out_specs=[_ANY] * nj,
        scratch_shapes=[pltpu.SemaphoreType.DMA((3 * nj,))] * 4,
        name=name, compiler_params=pltpu.CompilerParams(has_side_effects=True))(*arrs)
    return [lax.dynamic_update_index_in_dim(o, a, _chip(), 0) for o, a in zip(outs, arrs)]


def pair_split(name, arrs):
    nj = len(arrs)
    halves = [a.shape[1] // 2 for a in arrs]

    def body(*refs):
        _per_core(functools.partial(run, refs[:nj], refs[nj:2 * nj], *refs[2 * nj:]))

    def run(a_refs, got_refs, send_sems, recv_sems, c):
        x, y, _ = _coords()
        cps = [pltpu.make_async_remote_copy(a_refs[j].at[:, pl.ds((1 - c) * halves[j], halves[j])], got_refs[j],
                                            send_sems.at[j], recv_sems.at[j], device_id=(x, y, 1 - c), device_id_type=MESH)
               for j in range(nj)]
        for cp in cps:
            cp.start()
        for cp in cps:
            cp.wait()

    got = pl.pallas_call(
        body, out_shape=[_sd((4, h, a.shape[2]), a.dtype) for a, h in zip(arrs, halves)], in_specs=[_ANY] * nj,
        out_specs=[_ANY] * nj, scratch_shapes=[pltpu.SemaphoreType.DMA((nj,))] * 2,
        name=name, compiler_params=pltpu.CompilerParams(has_side_effects=True))(*arrs)
    own = [lax.dynamic_slice_in_dim(a, lax.axis_index("c") * h, h, axis=1) for a, h in zip(arrs, halves)]
    return own, got


def scatter_chips(name, arrs):
    nj = len(arrs)

    def body(*refs):
        a_refs, o_refs = refs[:nj], refs[nj:2 * nj]
        send_sems, recv_sems = refs[2 * nj:]
        x, y, c = _coords()
        me = 2 * x + y
        peers = [(1 - x, y), (x, 1 - y), (1 - x, 1 - y)]

        def cp(k, j, src_blk, dst_blk):
            return pltpu.make_async_remote_copy(a_refs[j].at[src_blk], o_refs[j].at[dst_blk], send_sems.at[k * nj + j],
                                                recv_sems.at[k * nj + j], device_id=(*peers[k], c), device_id_type=MESH)

        sends = [cp(k, j, 2 * peers[k][0] + peers[k][1], me) for k in range(3) for j in range(nj)]
        for s in sends:
            s.start()
        for k in range(3):
            for j in range(nj):
                cp(k, j, me, 2 * peers[k][0] + peers[k][1]).wait_recv()
        for s in sends:
            s.wait_send()

    outs = pl.pallas_call(
        body, out_shape=[_sd(a.shape, a.dtype) for a in arrs], in_specs=[_ANY] * nj, out_specs=[_ANY] * nj,
        scratch_shapes=[pltpu.SemaphoreType.DMA((3 * nj,))] * 2,
        name=name, compiler_params=pltpu.CompilerParams(has_side_effects=True))(*arrs)
    return [lax.dynamic_update_index_in_dim(o, lax.dynamic_index_in_dim(a, _chip(), 0, keepdims=False), _chip(), 0)
            for o, a in zip(outs, arrs)]


def pair_join(name, arrs):
    nj = len(arrs)
    halves = [a.shape[0] for a in arrs]

    def body(*refs):
        _per_core(functools.partial(run, refs[:nj], refs[nj:2 * nj], *refs[2 * nj:]))

    def run(a_refs, o_refs, send_sems, recv_sems, c):
        x, y, _ = _coords()

        def cp(j, rows_of):
            return pltpu.make_async_remote_copy(a_refs[j], o_refs[j].at[pl.ds(rows_of * halves[j], halves[j])], send_sems.at[j],
                                                recv_sems.at[j], device_id=(x, y, 1 - c), device_id_type=MESH)

        sends = [cp(j, c) for j in range(nj)]
        for s in sends:
            s.start()
        for s in sends:
            s.wait_send()
        for j in range(nj):
            cp(j, 1 - c).wait_recv()

    outs = pl.pallas_call(
        body, out_shape=[_sd((2 * a.shape[0], a.shape[1]), a.dtype) for a in arrs], in_specs=[_ANY] * nj, out_specs=[_ANY] * nj,
        scratch_shapes=[pltpu.SemaphoreType.DMA((nj,))] * 2,
        name=name, compiler_params=pltpu.CompilerParams(has_side_effects=True))(*arrs)
    return [lax.dynamic_update_slice_in_dim(o, a, lax.axis_index("c") * a.shape[0], axis=0) for o, a in zip(outs, arrs)]


def add_cast(name, a, b, dtype):
    k, r, c = a.shape
    tr = _pick(r, (512, 256, 128, 64, 32, 16, 8))

    def body(a_ref, b_ref, o_ref):
        o_ref[...] = (a_ref[...].astype(F32) + b_ref[...].astype(F32)).astype(dtype)

    spec = pl.BlockSpec((1, tr, c), lambda s, i: (s, i, 0))
    return pl.pallas_call(body, out_shape=_sd((k, r, c), dtype), grid=(k, r // tr), in_specs=[spec, spec], out_specs=spec,
                          name=name, compiler_params=_cp(("parallel", "parallel")))(a, b)


def gather_all(name, a):
    def body(a_ref, o_ref, send_sems, recv_sems, loc_sem):
        x, y, c = _coords()
        me = 4 * x + 2 * y + c
        flips = [(fx, fy, fc) for fx in (0, 1) for fy in (0, 1) for fc in (0, 1) if fx + fy + fc > 0]
        peers = [(x ^ fx, y ^ fy, c ^ fc) for fx, fy, fc in flips]
        mine = pltpu.make_async_copy(a_ref, o_ref.at[me], loc_sem)
        mine.start()
        sends = [pltpu.make_async_remote_copy(a_ref, o_ref.at[me], send_sems.at[k], recv_sems.at[k],
                                              device_id=p, device_id_type=MESH) for k, p in enumerate(peers)]
        for cp in sends:
            cp.start()
        for k, (px, py, pc) in enumerate(peers):
            pltpu.make_async_remote_copy(a_ref, o_ref.at[4 * px + 2 * py + pc], send_sems.at[k], recv_sems.at[k],
                                         device_id=(px, py, pc), device_id_type=MESH).wait_recv()
        for cp in sends:
            cp.wait_send()
        mine.wait()

    return pl.pallas_call(
        body, out_shape=_sd((8,) + a.shape, a.dtype), in_specs=[_ANY], out_specs=_ANY,
        scratch_shapes=[pltpu.SemaphoreType.DMA((7,)), pltpu.SemaphoreType.DMA((7,)), pltpu.SemaphoreType.DMA],
        name=name, compiler_params=pltpu.CompilerParams(has_side_effects=True))(a)


def sum_blocks(name, a):
    k, r, c = a.shape
    tr = _pick(r, (256, 128, 64, 32, 16, 8))

    def body(a_ref, o_ref):
        acc = a_ref[0].astype(F32)
        for s in range(1, k):
            acc = acc + a_ref[s].astype(F32)
        o_ref[...] = acc

    return pl.pallas_call(body, out_shape=_sd((r, c), F32), grid=(r // tr,),
                          in_specs=[pl.BlockSpec((k, tr, c), lambda i: (0, i, 0))], out_specs=pl.BlockSpec((tr, c), lambda i: (i, 0)),
                          name=name, compiler_params=_cp(("parallel",)))(a)


BIG = ("w_mod", "w_in", "w_mla_uq", "w_mla_ukv", "w_p_ssm", "w_p_swa", "w_p_mla", "w_out", "w_ffn_in", "w_ffn_out")
COL_SHARDED = ("w_mod", "w_in", "w_mla_uq", "w_mla_ukv", "w_ffn_in")
SMALL = ("c_ctx", "b_mod", "norm1_g", "norm2_g", "ssm_conv_w", "ssm_conv_b", "ssm_dt_bias", "ssm_a_log", "ssm_d",
         "ssm_norm_g", "swa_q_norm_g", "swa_k_norm_g", "swa_sink", "mla_q_lat_g", "mla_kv_lat_g", "mla_q_norm_g",
         "mla_k_norm_g")
WEIGHTS = ("c_ctx", "w_mod", "b_mod", "norm1_g", "norm2_g", "w_in", "ssm_conv_w", "ssm_conv_b", "ssm_dt_bias", "ssm_a_log",
           "ssm_d", "ssm_norm_g", "swa_q_norm_g", "swa_k_norm_g", "swa_sink", "mla_q_lat_g", "mla_kv_lat_g", "w_mla_uq",
           "w_mla_ukv", "mla_q_norm_g", "mla_k_norm_g", "w_p_ssm", "w_p_swa", "w_p_mla", "w_out", "w_ffn_in", "w_ffn_out")


def pack_w_in(w):
    z = lambda k: jnp.zeros((w.shape[0], k), w.dtype)
    return jnp.concatenate([w[:, 4832:7904], w[:, 2400:3424], w[:, 3424:4448], w[:, 0:1536], w[:, 1568:1824], w[:, 1824:2080],
                            w[:, 2080:2336], w[:, 2336:2400], w[:, 1536:1568], z(32), z(128), w[:, 4448:4832]], axis=1)


def unpack_w_in(g):
    return jnp.concatenate([g[:, 5120:6656], g[:, 7488:7520], g[:, 6656:6912], g[:, 6912:7168], g[:, 7168:7424], g[:, 7424:7488],
                            g[:, 3072:4096], g[:, 4096:5120], g[:, 7680:8064], g[:, 0:3072]], axis=1)


def pack_ukv(w):
    return w.reshape(MLA_KVRANK, MLA_H, 2, 128).transpose(0, 2, 1, 3).reshape(MLA_KVRANK, 2048)


def unpack_ukv(g):
    return g.reshape(MLA_KVRANK, 2, MLA_H, 128).transpose(0, 2, 1, 3).reshape(MLA_KVRANK, 2048)


def pack_uq(w):
    return jnp.pad(w.reshape(MLA_QRANK, MLA_H, 192), ((0, 0), (0, 0), (0, 64))).reshape(MLA_QRANK, 2048)


def unpack_uq(g):
    return g.reshape(MLA_QRANK, MLA_H, 256)[:, :, :192].reshape(MLA_QRANK, 1536)


def rope_tables(nlat):
    t = jnp.arange(nlat, dtype=jnp.int32)
    r = (t // GRID_W).astype(F32)[:, None]
    col = (t % GRID_W).astype(F32)[:, None]

    def tab(nf, pad):
        inv = jnp.power(ROPE_BASE, -jnp.arange(nf, dtype=F32) / nf)
        ar, ac = r * inv, col * inv
        cos = jnp.concatenate([jnp.cos(ar), jnp.cos(ar), jnp.cos(ac), jnp.cos(ac), jnp.ones((nlat, pad), F32)], axis=1)
        sin = jnp.concatenate([-jnp.sin(ar), jnp.sin(ar), -jnp.sin(ac), jnp.sin(ac), jnp.zeros((nlat, pad), F32)], axis=1)
        cos = jnp.concatenate([cos, jnp.ones((NCTX, 128), F32)], axis=0)
        sin = jnp.concatenate([sin, jnp.zeros((NCTX, 128), F32)], axis=0)
        return cos, sin

    return tab(32, 0), tab(16, 64)


def _lanes(v, start, width=128):
    return jnp.zeros((1, width), F32).at[0, start:start + v.shape[0]].set(v)


def layer_fwd(i, xin, h, mod, p, tabs, nlat):
    t = "l%d_" % i
    n = xin.shape[0]
    (cos_s, sin_s), (cos_m, sin_m) = tabs
    u = mm(h, p["w_in"], F32, t + "in_proj")
    xbc = conv_fwd(t + "conv", u, p["conv_w"], p["conv_b"], nlat)
    dtrow = jnp.transpose(u[:, C_MISC + DT_LANE:C_MISC + DT_LANE + 32])
    nlc = nlat // Q
    yf, hs_f = ssd_fwd(t + "ssd_f", xbc, u, dtrow, p["bias_c"], p["alog_c"], p["bias_r"], p["alog_r"], nlc, False, 0)
    yb, hs_b = ssd_fwd(t + "ssd_b", xbc, u, dtrow, p["bias_c"], p["alog_c"], p["bias_r"], p["alog_r"], nlc, True, 1)
    ys = ssd_out_fwd(t + "ssd_out", yf, yb, xbc, u, p["ssm_norm_g"], p["d_exp"])
    qs, ks = swa_prep_fwd(t + "swa_prep", u, p["swa_q_g"], p["swa_k_g"], cos_s, sin_s)
    o_swa, lse_swa = swa_attention_fwd(t + "swa", qs, ks, u, p["sink"], p["swa_cfg"], nlat)
    ckv_n, cq_n = lat_norm_fwd(t + "lat_norm", u, p["kv_lat_g"], p["q_lat_g"])
    kv = mm(ckv_n, p["w_ukv"], F32, t + "ukv")
    qp = mm(cq_n, p["w_uq"], F32, t + "uq")
    km, qm, vm = mla_prep_fwd(t + "mla_prep", kv, qp, u, p["mla_q_g"], p["mla_k_g"], cos_m, sin_m)
    o_mla, lse_mla = mla_fwd(t + "mla_fwd", qm, km, vm, nlat)
    p1 = mm(ys, p["w_p_ssm"], BF16, t + "p_ssm")
    p2 = mm(o_swa, p["w_p_swa"], BF16, t + "p_swa")
    p3 = mm(o_mla, p["w_p_mla"], BF16, t + "p_mla")
    merged = merge_fwd(t + "merge", u, p1, p2, p3)
    o = mm(merged, p["w_out"], F32, t + "out_proj")
    x1, h2 = resid_mod_fwd(t + "res1", xin, o, mod, 2, mod, 3, 4, p["norm2_g"], nlat // RT)
    gu = mm(h2, p["w_ffn_in"], BF16, t + "ffn_in")
    a = swiglu_fwd(t + "swiglu", gu)
    f = mm(a, p["w_ffn_out"], F32, t + "ffn_out")
    saved = dict(xin=xin, h=h, u=u, xbc=xbc, dtrow=dtrow, yf=yf, yb=yb, hs_f=hs_f, hs_b=hs_b, ys=ys, qs=qs, ks=ks,
                 o_swa=o_swa, lse_swa=lse_swa, ckv_n=ckv_n, cq_n=cq_n, kv=kv, qp=qp, km=km, qm=qm, vm=vm, o_mla=o_mla,
                 lse_mla=lse_mla, p1=p1, p2=p2, p3=p3, merged=merged, o=o, x1=x1, h2=h2, gu=gu, a=a, f=f)
    del n
    return x1, f, saved


def layer_bwd(i, dx2, df, dgt2, sv, mod, p, tabs, nlat):
    t = "l%db_" % i
    (cos_s, sin_s), (cos_m, sin_m) = tabs
    g = {}
    nt = nlat // RT
    nlc = nlat // Q
    g["w_ffn_out"] = mm_tn(sv["a"], df, t + "wg_ffn_out")
    da = mm(df, p["w_ffn_out"], F32, t + "dg_ffn_out", trans_b=True)
    dgu = swiglu_bwd(t + "swiglu", sv["gu"], da)
    g["w_ffn_in"] = mm_tn(sv["h2"], dgu, t + "wg_ffn_in")
    dh2 = mm(dgu, p["w_ffn_in"], F32, t + "dg_ffn_in", trans_b=True)
    dx1, do, dgt1, dsh2, dsc2, g["norm2_g"] = resid_mod_bwd(t + "res1", sv["x1"], dx2, dh2, sv["o"], mod, 2, mod, 3, 4,
                                                              p["norm2_g"], nt)
    g["w_out"] = mm_tn(sv["merged"], do, t + "wg_out")
    dmerged = mm(do, p["w_out"], F32, t + "dg_out", trans_b=True)
    dp1, dp2, dp3, dgates = merge_bwd(t + "merge", sv["u"], sv["p1"], sv["p2"], sv["p3"], dmerged)
    g["w_p_ssm"] = mm_tn(sv["ys"], dp1, t + "wg_p_ssm")
    g["w_p_swa"] = mm_tn(sv["o_swa"], dp2, t + "wg_p_swa")
    g["w_p_mla"] = mm_tn(sv["o_mla"], dp3, t + "wg_p_mla")
    dys = mm(dp1, p["w_p_ssm"], F32, t + "dg_p_ssm", trans_b=True)
    do_swa = mm(dp2, p["w_p_swa"], BF16, t + "dg_p_swa", trans_b=True)
    do_mla = mm(dp3, p["w_p_mla"], BF16, t + "dg_p_mla", trans_b=True)
    dqm, dkm, dv_mla = mla_attention_bwd(t + "mla", sv["qm"], sv["km"], sv["vm"], sv["o_mla"], do_mla, sv["lse_mla"], nlat)
    dkv, dqp, dkr, g["mla_q_g"], g["mla_k_g"] = mla_prep_bwd(t + "mla_prep", sv["kv"], sv["qp"], sv["u"], p["mla_q_g"],
                                                             p["mla_k_g"], cos_m, sin_m, dkm, dqm, dv_mla)
    g["w_ukv"] = mm_tn(sv["ckv_n"], dkv, t + "wg_ukv")
    g["w_uq"] = mm_tn(sv["cq_n"], dqp, t + "wg_uq")
    dckv_n = mm(dkv, p["w_ukv"], F32, t + "dg_ukv", trans_b=True)
    dcq_n = mm(dqp, p["w_uq"], F32, t + "dg_uq", trans_b=True)
    dckv, dcq, g["kv_lat_g"], g["q_lat_g"] = lat_norm_bwd(t + "lat_norm", sv["u"], p["kv_lat_g"], p["q_lat_g"], dckv_n, dcq_n)
    dqs, dks, dv_swa, g["sink"] = swa_attention_bwd(t + "swa", sv["qs"], sv["ks"], sv["u"], sv["o_swa"], do_swa, sv["lse_swa"],
                                                p["sink"], p["swa_cfg"], nlat)
    dq, dk, dv, g["swa_q_g"], g["swa_k_g"] = swa_prep_bwd(t + "swa_prep", sv["u"], p["swa_q_g"], p["swa_k_g"], cos_s, sin_s,
                                                          dqs, dks, dv_swa)
    dy, dxs_skip, dz, g["ssm_norm_g"], g["d_exp"] = ssd_out_bwd(t + "ssd_out", sv["yf"], sv["yb"], sv["xbc"], sv["u"],
                                                                 p["ssm_norm_g"], p["d_exp"], dys)
    n = dy.shape[0]
    zbc = jnp.zeros((n, 256), F32)
    r_f = ssd_bwd(t + "ssd_f", sv["xbc"], sv["u"], sv["dtrow"], p["bias_c"], p["alog_c"], p["bias_r"], p["alog_r"],
                  sv["hs_f"], dy, (dxs_skip, zbc, zbc), nlc, False, 0)
    r_b = ssd_bwd(t + "ssd_b", sv["xbc"], sv["u"], sv["dtrow"], p["bias_c"], p["alog_c"], p["bias_r"], p["alog_r"],
                  sv["hs_b"], dy, (r_f[0], r_f[1], r_f[2]), nlc, True, 1)
    dact = jnp.concatenate([r_b[0], r_b[1], r_b[2]], axis=1)
    dxbc, g["conv_w"], g["conv_b"] = conv_bwd(t + "conv", sv["u"], dact, p["conv_w"], p["conv_b"], nlat)
    drow = jnp.concatenate([r_f[4][0] + r_f[4][1], r_b[4][0] + r_b[4][1]], axis=0)
    drow_t = jnp.pad(jnp.transpose(drow), ((0, 0), (DT_LANE, 128 - DT_LANE - 32)))
    dmisc = misc_combine(t + "misc", dkr, r_f[3], r_b[3], drow_t)
    g["bias_c"] = r_f[5] + r_b[5]
    g["alog_c"] = r_f[6] + r_b[6]
    g["bias_r"] = jnp.concatenate([r_f[7], r_b[7]], axis=0)
    g["alog_r"] = jnp.concatenate([r_f[8], r_b[8]], axis=0)
    du = jnp.concatenate([dgates, dz, dq, dxbc, dk, dv, dckv, dmisc, jnp.zeros((n, 128), BF16), dcq], axis=1)
    g["w_in"] = mm_tn(sv["h"], du, t + "wg_in")
    dh = mm(du, p["w_in"], F32, t + "dg_in", trans_b=True)
    g["mod"] = (dgt1, dsh2, dsc2, dgt2)
    return dx1, dh, g


def local_step(x, c, ctx, target, c_ctx, W, nlat):
    xin = jnp.concatenate([x, ctx], axis=0)
    n = xin.shape[0]
    nt = nlat // RT
    tabs = rope_tables(nlat)
    c8 = jnp.zeros((8, D), F32).at[0].set(c[0]).at[1].set(c_ctx)
    mods, silus = [], []
    for i in range(DEPTH):
        m8, s8 = mod_fwd("l%d_mod" % i, c8, W[i]["w_mod"], W[i]["b_mod"])
        mods.append(m8[0:2].reshape(2, 1, 6 * D))
        silus.append(s8)
    saved = []
    _, h = resid_mod_fwd("l0_norm1", xin, None, None, 0, mods[0], 0, 1, W[0]["norm1_g"], nt)
    xcur = xin
    for i in range(DEPTH):
        x1, f, sv = layer_fwd(i, xcur, h, mods[i], W[i], tabs, nlat)
        saved.append(sv)
        if i + 1 < DEPTH:
            xcur, h = resid_mod_fwd("l%d_res2" % i, x1, f, mods[i], 5, mods[i + 1], 0, 1, W[i + 1]["norm1_g"], nt)
    loss_v, dx2, df, dgt2 = resid_loss("loss", x1, f, mods[DEPTH - 1], 5, target, nt)
    grads = [None] * DEPTH
    for i in reversed(range(DEPTH)):
        dx1, dh, g = layer_bwd(i, dx2, df, dgt2, saved[i], mods[i], W[i], tabs, nlat)
        if i > 0:
            sv = saved[i]
            dx2, df, dgt2, dsh1, dsc1, g["norm1_g"] = resid_mod_bwd(
                "l%db_res2" % (i - 1), sv["xin"], dx1, dh, saved[i - 1]["f"], mods[i - 1], 5, mods[i], 0, 1,
                W[i]["norm1_g"], nt)
        else:
            dxin, _, _, dsh1, dsc1, g["norm1_g"] = resid_mod_bwd("l0b_norm1", saved[0]["xin"], dx1, dh, None, None, 0,
                                                                  mods[0], 0, 1, W[0]["norm1_g"], nt)
        dgt1, dsh2, dsc2, dgt2_i = g.pop("mod")
        dmod = jnp.concatenate([dsh1, dsc1, dgt1, dsh2, dsc2, dgt2_i], axis=2).reshape(2, 6 * D)
        dmod8 = jnp.zeros((8, 6 * D), F32).at[0:2].set(dmod)
        g["w_mod"] = mm_tn(silus[i], dmod8, "l%db_wg_mod" % i)
        dsilu = mm(dmod8, W[i]["w_mod"], F32, "l%db_dg_mod" % i, trans_b=True)
        dc8, g["b_mod"] = mod_small_bwd("l%db_mod_small" % i, c8, dsilu, dmod8)
        g["c8"] = dc8
        grads[i] = g
    del n
    return loss_v[0, 0], dxin, grads


def _big_shapes():
    return dict(w_mod=(2, 1024, 1536), w_in=(2, 1024, 1976), w_mla_uq=(2, 384, 384), w_mla_ukv=(2, 256, 512),
                w_p_ssm=(2, 256, 1024), w_p_swa=(2, 256, 1024), w_p_mla=(2, 256, 1024), w_out=(2, 256, 1024),
                w_ffn_in=(2, 1024, 1408), w_ffn_out=(2, 704, 1024))


PACK_ROWS = 14336


def _pack_big(d, dtype):
    parts = [d[k].astype(dtype).reshape(-1, 1024) for k in BIG]
    used = sum(p.shape[0] for p in parts)
    return jnp.concatenate(parts + [jnp.zeros((PACK_ROWS - used, 1024), dtype)], axis=0)


def _unpack_big(buf, lead):
    out = {}
    r0 = 0
    for k in BIG:
        sh = _big_shapes()[k]
        rows = sh[0] * sh[1] * sh[2] // 1024
        out[k] = buf[..., r0:r0 + rows, :].reshape(lead + sh)
        r0 += rows
    return out


def _full_from_chips(k, a):
    if k in COL_SHARDED:
        return a.transpose(1, 2, 0, 3).reshape(2, a.shape[2], 4 * a.shape[3])
    return a.transpose(1, 0, 2, 3).reshape(2, 4 * a.shape[2], a.shape[3])


def _chips_from_full(k, a):
    if k in COL_SHARDED:
        return a.reshape(a.shape[0], 4, a.shape[1] // 4).transpose(1, 0, 2)
    return a.reshape(4, a.shape[0] // 4, a.shape[1])


def _small_sizes():
    return dict(c_ctx=1024, b_mod=2 * 6144, norm1_g=2048, norm2_g=2048, ssm_conv_w=2 * 5 * 1536, ssm_conv_b=2 * 1536,
                ssm_dt_bias=64, ssm_a_log=64, ssm_d=32, ssm_norm_g=2048, swa_q_norm_g=256, swa_k_norm_g=256, swa_sink=16,
                mla_q_lat_g=768, mla_kv_lat_g=512, mla_q_norm_g=384, mla_k_norm_g=384)


def _pack_small(d):
    parts = []
    for k in SMALL:
        v = d[k].astype(F32).reshape(-1)
        parts.append(jnp.pad(v, (0, (-v.shape[0]) % 1024)))
    return jnp.concatenate(parts).reshape(-1, 128)


def _unpack_small(buf, shapes):
    flat = buf.reshape(-1)
    out = {}
    o = 0
    for k in SMALL:
        sz = _small_sizes()[k]
        out[k] = flat[o:o + sz].reshape(shapes[k])
        o += sz + (-sz) % 1024
    return out


def big_grads(grads):
    gfull = {k: [] for k in BIG}
    for i in range(DEPTH):
        g = grads[i]
        gfull["w_mod"].append(g["w_mod"])
        gfull["w_in"].append(unpack_w_in(g["w_in"]))
        gfull["w_mla_uq"].append(unpack_uq(g["w_uq"]))
        gfull["w_mla_ukv"].append(unpack_ukv(g["w_ukv"]))
        for k in ("w_p_ssm", "w_p_swa", "w_p_mla", "w_out", "w_ffn_in", "w_ffn_out"):
            gfull[k].append(g[k])
    return gfull


def small_grads(grads):
    gs = {}
    gs["c_ctx"] = sum(grads[i]["c8"][1] for i in range(DEPTH))
    st = lambda f: jnp.stack([f(grads[i]) for i in range(DEPTH)])
    gs["b_mod"] = st(lambda g: g["b_mod"][0])
    gs["norm1_g"] = st(lambda g: g["norm1_g"][0])
    gs["norm2_g"] = st(lambda g: g["norm2_g"][0])
    gs["ssm_conv_w"] = st(lambda g: g["conv_w"])
    gs["ssm_conv_b"] = st(lambda g: g["conv_b"][0])
    gs["ssm_dt_bias"] = st(lambda g: (g["bias_c"][0, DT_LANE:DT_LANE + 32] + g["bias_r"][:, 0]).reshape(2, 16))
    gs["ssm_a_log"] = st(lambda g: (g["alog_c"][0, DT_LANE:DT_LANE + 32] + g["alog_r"][:, 0]).reshape(2, 16))
    gs["ssm_d"] = st(lambda g: g["d_exp"].reshape(16, 64).sum(axis=1))
    gs["ssm_norm_g"] = st(lambda g: g["ssm_norm_g"][0])
    gs["swa_q_norm_g"] = st(lambda g: g["swa_q_g"][0])
    gs["swa_k_norm_g"] = st(lambda g: g["swa_k_g"][0])
    gs["swa_sink"] = st(lambda g: g["sink"][:, 0, 0])
    gs["mla_q_lat_g"] = st(lambda g: g["q_lat_g"][0])
    gs["mla_kv_lat_g"] = st(lambda g: g["kv_lat_g"][0])
    gs["mla_q_norm_g"] = st(lambda g: g["mla_q_g"][0, :192])
    gs["mla_k_norm_g"] = st(lambda g: g["mla_k_g"][0, :192])
    return gs


def layer_params(i, full, conv_full, sm, nlat):
    p = {}
    p["w_mod"] = full["w_mod"][i]
    p["w_in"] = pack_w_in(full["w_in"][i])
    p["w_uq"] = pack_uq(full["w_mla_uq"][i])
    p["w_ukv"] = pack_ukv(full["w_mla_ukv"][i])
    for k in ("w_p_ssm", "w_p_swa", "w_p_mla", "w_out", "w_ffn_in", "w_ffn_out"):
        p[k] = full[k][i]
    p["b_mod"] = sm["b_mod"][i][None]
    p["norm1_g"] = sm["norm1_g"][i][None]
    p["norm2_g"] = sm["norm2_g"][i][None]
    p["conv_w"] = conv_full[i]
    p["conv_b"] = sm["ssm_conv_b"][i][None]
    bias = sm["ssm_dt_bias"][i].reshape(32)
    alog = sm["ssm_a_log"][i].reshape(32)
    p["bias_c"] = _lanes(bias, DT_LANE)
    p["alog_c"] = _lanes(alog, DT_LANE)
    p["bias_r"] = bias[:, None]
    p["alog_r"] = alog[:, None]
    p["d_exp"] = jnp.repeat(sm["ssm_d"][i], 64)[None]
    p["ssm_norm_g"] = sm["ssm_norm_g"][i][None]
    p["swa_q_g"] = sm["swa_q_norm_g"][i][None]
    p["swa_k_g"] = sm["swa_k_norm_g"][i][None]
    p["sink"] = jnp.broadcast_to(sm["swa_sink"][i][:, None, None], (SWA_HQ, 1, 128))
    p["q_lat_g"] = sm["mla_q_lat_g"][i][None]
    p["kv_lat_g"] = sm["mla_kv_lat_g"][i][None]
    p["mla_q_g"] = _lanes(sm["mla_q_norm_g"][i], 0, 256)
    p["mla_k_g"] = _lanes(sm["mla_k_norm_g"][i], 0, 256)
    p["swa_cfg"] = dict(w=128, vw=128, hq=SWA_HQ, grp=SWA_HQ // SWA_HKV, vcol0=C_V // 128, scale=SWA_DH ** -0.5,
                        tq=256, tk=256, band=True)
    return p


def kernel(x, c, ctx, c_ctx, w_mod, b_mod, norm1_g, norm2_g, w_in, ssm_conv_w, ssm_conv_b, ssm_dt_bias, ssm_a_log, ssm_d, ssm_norm_g, swa_q_norm_g, swa_k_norm_g, swa_sink, mla_q_lat_g, mla_kv_lat_g, w_mla_uq, w_mla_ukv, mla_q_norm_g, mla_k_norm_g, w_p_ssm, w_p_swa, w_p_mla, w_out, w_ffn_in, w_ffn_out, loss_target, m_c_ctx, m_w_mod, m_b_mod, m_norm1_g, m_norm2_g, m_w_in, m_ssm_conv_w, m_ssm_conv_b, m_ssm_dt_bias, m_ssm_a_log, m_ssm_d, m_ssm_norm_g, m_swa_q_norm_g, m_swa_k_norm_g, m_swa_sink, m_mla_q_lat_g, m_mla_kv_lat_g, m_w_mla_uq, m_w_mla_ukv, m_mla_q_norm_g, m_mla_k_norm_g, m_w_p_ssm, m_w_p_swa, m_w_p_mla, m_w_out, m_w_ffn_in, m_w_ffn_out, v_c_ctx, v_w_mod, v_b_mod, v_norm1_g, v_norm2_g, v_w_in, v_ssm_conv_w, v_ssm_conv_b, v_ssm_dt_bias, v_ssm_a_log, v_ssm_d, v_ssm_norm_g, v_swa_q_norm_g, v_swa_k_norm_g, v_swa_sink, v_mla_q_lat_g, v_mla_kv_lat_g, v_w_mla_uq, v_w_mla_ukv, v_mla_q_norm_g, v_mla_k_norm_g, v_w_p_ssm, v_w_p_swa, v_w_p_mla, v_w_out, v_w_ffn_in, v_w_ffn_out):
    loc = dict(locals())
    w = {k: loc[k] for k in WEIGHTS}
    m = {k: loc["m_" + k] for k in WEIGHTS}
    v = {k: loc["v_" + k] for k in WEIGHTS}
    nlat = x.shape[1]

    sh2 = {k: (w[k].shape[0] * w[k].shape[1], w[k].shape[2]) for k in BIG}
    conv_sh = jnp.pad(ssm_conv_w.reshape(10, 384), ((0, 6), (0, 0)))
    gathered = gather_chips("gather_weights", [w[k].astype(BF16).reshape(sh2[k]) for k in BIG] + [conv_sh])
    full = {k: _full_from_chips(k, g.reshape((4,) + w[k].shape)) for k, g in zip(BIG, gathered)}
    conv_full = gathered[-1][:, :10].reshape(4, 2, 5, 384).transpose(1, 2, 0, 3).reshape(2, 5, 1536)

    W = [layer_params(i, full, conv_full, w, nlat) for i in range(DEPTH)]

    loss_loc, dx, grads = local_step(x[0], c, ctx[0], loss_target[0], c_ctx, W, nlat)

    gfull = big_grads(grads)
    by_chip = {k: jnp.stack([_chips_from_full(k, a) for a in gfull[k]], axis=1) for k in BIG}
    send = [by_chip[k].astype(BF16).reshape((4,) + sh2[k]) for k in BIG]
    own, got = pair_split("pair_split", send)
    pair = [add_cast("pair_sum_" + k, o, g, BF16) for k, o, g in zip(BIG, own, got)]
    recv = scatter_chips("scatter_grads", pair)
    mine = [sum_blocks("sum_chips_" + k, r) for k, r in zip(BIG, recv)]
    gbig = {k: g.reshape(w[k].shape) for k, g in zip(BIG, pair_join("join_cores", mine))}

    gs = small_grads(grads)
    small_all = gather_all("gather_small", _pack_small(gs))
    small_sum = sum_blocks("sum_small", small_all)
    full_shapes = {k: (w[k].shape if k != "ssm_conv_w" else (2, 5, 1536)) for k in SMALL}
    gsmall = _unpack_small(small_sum, full_shapes)
    chip = 2 * lax.axis_index("x") + lax.axis_index("y")
    gsmall["ssm_conv_w"] = lax.dynamic_slice_in_dim(gsmall["ssm_conv_w"], chip * 384, 384, axis=2)

    grad = {**gbig, **gsmall}
    delta, new_m, new_v = {}, {}, {}
    sm = {k: _pack_small_local(d) for k, d in (("w", w), ("g", grad), ("m", m), ("v", v))}
    r = adamw("adamw_small", sm["w"], sm["g"], sm["m"], sm["v"])
    shapes = {k: w[k].shape for k in SMALL}
    for dst, buf in zip((delta, new_m, new_v), r):
        dst.update(_unpack_small_local(buf, shapes))
    for k in BIG:
        sh = w[k].shape
        r = adamw("adamw_" + k, *[a[k].reshape(sh[0] * sh[1], sh[2]) for a in (w, grad, m, v)])
        for dst, buf in zip((delta, new_m, new_v), r):
            dst[k] = buf.reshape(sh)

    loss = lax.psum(loss_loc, ("x", "y", "c"))
    return (loss, dx[None, :nlat], *[grad[k] for k in WEIGHTS], *[delta[k] for k in WEIGHTS],
            *[new_m[k] for k in WEIGHTS], *[new_v[k] for k in WEIGHTS])


def _pack_small_local(d):
    parts = []
    for k in SMALL:
        a = d[k].astype(F32).reshape(-1)
        parts.append(jnp.pad(a, (0, (-a.shape[0]) % 1024)))
    return jnp.concatenate(parts).reshape(-1, 128)


def _unpack_small_local(buf, shapes):
    flat = buf.reshape(-1)
    out = {}
    o = 0
    for k in SMALL:
        sz = math.prod(shapes[k])
        out[k] = flat[o:o + sz].reshape(shapes[k])
        o += sz + (-sz) % 1024
    return out
```

```python
import functools
import math

import jax
import jax.numpy as jnp
from jax import lax
from jax.experimental import pallas as pl
from jax.experimental.pallas import tpu as pltpu

F32 = jnp.float32
BF16 = jnp.bfloat16
MESH = pl.DeviceIdType.MESH

D = 1024
NCTX = 256
EPS = 1e-6
ROPE_BASE = 10000.0
GRID_W = 64
DEPTH = 2
Q = 128
N_HEADS_SSM = 16
SWA_HQ, SWA_HKV, SWA_DH, SWA_WIN = 8, 2, 128, 128
MLA_H, MLA_NOPE, MLA_ROPE, MLA_V = 8, 128, 64, 128
MLA_QRANK, MLA_KVRANK = 384, 256
FFN = 2816
RT = 256
VMEM_LIMIT = 56 << 20
NEG = -1e30
LOG2E = 1.4426950408889634

C_G1, C_G2, C_G3, C_Z, C_Q, C_XS, C_B, C_C, C_K, C_V, C_CKV, C_MISC, C_PAD, C_CQ = (
    0, 1024, 2048, 3072, 4096, 5120, 6144, 6400, 6656, 6912, 7168, 7424, 7552, 7680)
UW = 8064
DT_LANE = 64

ADAM_LR, ADAM_B1, ADAM_B2, ADAM_EPS, ADAM_WD, ADAM_STEP = 0.001, 0.9, 0.999, 1e-08, 0.01, 10


def _cp(sem):
    return pltpu.CompilerParams(dimension_semantics=sem, vmem_limit_bytes=VMEM_LIMIT)


def _pick(n, cands):
    for c in cands:
        if n % c == 0:
            return c
    return n


_TN = (1536, 1408, 1152, 1024, 896, 768, 512, 384, 256, 128)


def mm(a, b, out_dtype, name, trans_b=False):
    m, k = a.shape
    n = b.shape[0] if trans_b else b.shape[1]
    tm = _pick(m, (768, 512, 256, 128, 8))
    tn = _pick(n, _TN)
    tk = k if k <= 2048 else _pick(k, (1408, 1152, 1024, 896, 768, 512))
    nk = k // tk
    b_spec = (pl.BlockSpec((tn, tk), lambda i, j, kk: (j, kk)) if trans_b
              else pl.BlockSpec((tk, tn), lambda i, j, kk: (kk, j)))

    def body(a_ref, b_ref, o_ref, *acc):
        p = _d(a_ref[...], b_ref[...], ((1,), (1 if trans_b else 0,)))
        if nk == 1:
            o_ref[...] = p.astype(out_dtype)
        else:
            kk = pl.program_id(2)

            @pl.when(kk == 0)
            def _():
                acc[0][...] = p

            @pl.when(kk > 0)
            def _():
                acc[0][...] += p

            @pl.when(kk == nk - 1)
            def _():
                o_ref[...] = acc[0][...].astype(out_dtype)

    return pl.pallas_call(
        body, out_shape=jax.ShapeDtypeStruct((m, n), out_dtype), grid=(m // tm, n // tn, nk),
        in_specs=[pl.BlockSpec((tm, tk), lambda i, j, kk: (i, kk)), b_spec],
        out_specs=pl.BlockSpec((tm, tn), lambda i, j, kk: (i, j)),
        scratch_shapes=[] if nk == 1 else [pltpu.VMEM((tm, tn), F32)],
        name=name, compiler_params=_cp(("parallel", "parallel", "arbitrary")))(a, b)


def mm_tn(a, b, name, out_dtype=BF16):
    t, ka = a.shape
    _, nb = b.shape
    ta = _pick(ka, (1024, 1408, 768, 512, 384, 256, 128))
    tb = _pick(nb, _TN)
    tt = _pick(t, (768, 512, 256, 128, 8))
    nt = t // tt

    def body(a_ref, b_ref, o_ref, acc):
        p = _d(a_ref[...], b_ref[...], ((0,), (0,)))
        s = pl.program_id(2)

        @pl.when(s == 0)
        def _():
            acc[...] = p

        @pl.when(s > 0)
        def _():
            acc[...] += p

        @pl.when(s == nt - 1)
        def _():
            o_ref[...] = acc[...].astype(out_dtype)

    return pl.pallas_call(
        body, out_shape=jax.ShapeDtypeStruct((ka, nb), out_dtype), grid=(ka // ta, nb // tb, nt),
        in_specs=[pl.BlockSpec((tt, ta), lambda i, j, s: (s, i)), pl.BlockSpec((tt, tb), lambda i, j, s: (s, j))],
        out_specs=pl.BlockSpec((ta, tb), lambda i, j, s: (i, j)), scratch_shapes=[pltpu.VMEM((ta, tb), F32)],
        name=name, compiler_params=_cp(("parallel", "parallel", "arbitrary")))(a, b)


def _rms(x, g, n=None):
    n = x.shape[-1] if n is None else n
    r = lax.rsqrt(jnp.sum(x * x, axis=-1, keepdims=True) * (1.0 / n) + EPS)
    return x * r * g


def _silu(x):
    return x * jax.nn.sigmoid(x)


def _modulate(x, g, sc, sh):
    return _rms(x, g) * (1.0 + sc) + sh


def _swap(x, s):
    ax = x.ndim - 1
    w = x.shape[ax]
    lane = lax.broadcasted_iota(jnp.int32, x.shape, ax)
    lo = (lane & s) == 0
    return jnp.where(lo, pltpu.roll(x, w - s, ax), pltpu.roll(x, s, ax))


@functools.partial(jax.custom_vjp, nondiff_argnums=(3,))
def _rope(x, cos, sin, s):
    return x * cos + _swap(x, s) * sin


def _rope_fwd(x, cos, sin, s):
    return _rope(x, cos, sin, s), (cos, sin)


def _rope_bwd(s, res, g):
    cos, sin = res
    return g * cos - _swap(g, s) * sin, jnp.zeros_like(cos), jnp.zeros_like(sin)


_rope.defvjp(_rope_fwd, _rope_bwd)


@jax.custom_vjp
def _softplus(x):
    return jnp.maximum(x, 0.0) + jnp.log(1.0 + jnp.exp(-jnp.abs(x)))


def _softplus_fwd(x):
    return _softplus(x), x


def _softplus_bwd(x, g):
    return (g * jax.nn.sigmoid(x),)


_softplus.defvjp(_softplus_fwd, _softplus_bwd)


def _d(a, b, dims):
    return lax.dot_general(a.astype(BF16), b.astype(BF16), (dims, ((), ())), preferred_element_type=F32)


@jax.custom_vjp
def bdot(a, b):
    return _d(a, b, ((1,), (0,)))


bdot.defvjp(lambda a, b: (bdot(a, b), (a, b)),
            lambda r, g: (_d(g, r[1], ((1,), (1,))), _d(r[0], g, ((0,), (0,)))))


@jax.custom_vjp
def bdot_nt(a, b):
    return _d(a, b, ((1,), (1,)))


bdot_nt.defvjp(lambda a, b: (bdot_nt(a, b), (a, b)),
               lambda r, g: (_d(g, r[1], ((1,), (0,))), _d(g, r[0], ((0,), (0,)))))


@jax.custom_vjp
def bdot_tn(a, b):
    return _d(a, b, ((0,), (0,)))


bdot_tn.defvjp(lambda a, b: (bdot_tn(a, b), (a, b)),
               lambda r, g: (_d(r[1], g, ((1,), (1,))), _d(r[0], g, ((1,), (0,)))))


def _tri(rev):
    i = lax.broadcasted_iota(jnp.int32, (Q, Q), 0)
    j = lax.broadcasted_iota(jnp.int32, (Q, Q), 1)
    return (i <= j) if rev else (i >= j)


def _split3(a):
    hi = a.astype(BF16)
    r = a - hi.astype(F32)
    mid = r.astype(BF16)
    lo = (r - mid.astype(F32)).astype(BF16)
    return hi, mid, lo


def _cum_cols_impl(a, rev):
    t = _tri(rev).astype(BF16)
    return sum(jnp.dot(t, p, preferred_element_type=F32) for p in _split3(a))


def _cum_rows_impl(a, rev):
    t = _tri(not rev).astype(BF16)
    return sum(jnp.dot(p, t, preferred_element_type=F32) for p in _split3(a))


@functools.partial(jax.custom_vjp, nondiff_argnums=(1,))
def cum_cols(a, rev):
    return _cum_cols_impl(a, rev)


cum_cols.defvjp(lambda a, rev: (_cum_cols_impl(a, rev), None), lambda rev, _, g: (_cum_cols_impl(g, not rev),))


@functools.partial(jax.custom_vjp, nondiff_argnums=(1,))
def cum_rows(a, rev):
    return _cum_rows_impl(a, rev)


cum_rows.defvjp(lambda a, rev: (_cum_rows_impl(a, rev), None), lambda rev, _, g: (_cum_rows_impl(g, not rev),))


def _rs(w, cb=0):
    return pl.BlockSpec((RT, w), lambda i: (i, cb))


def _ps(shape):
    nd = len(shape)
    return pl.BlockSpec(shape, lambda i: (0,) * nd)


def _gs(w, cb, nlat):
    return pl.BlockSpec((1, 1, w), lambda i: (i // nlat, 0, cb))


def _rowcall(name, body, n, ins, outs, scratch=()):
    return pl.pallas_call(
        body, out_shape=[o[0] for o in outs], grid=(n // RT,), in_specs=[s for _, s in ins],
        out_specs=[s for _, s in outs], scratch_shapes=list(scratch), name=name,
        compiler_params=_cp(("arbitrary",)))(*[a for a, _ in ins])


def _acc(ref, val, first):
    @pl.when(first)
    def _():
        ref[...] = val

    @pl.when(jnp.logical_not(first))
    def _():
        ref[...] += val


def _sd(shape, dt):
    return jax.ShapeDtypeStruct(shape, dt)


def resid_mod_fwd(name, xp, o, mod_gt, gt_i, mod_n, sh_i, sc_i, norm_g, nlat):
    n = xp.shape[0]
    has_res = o is not None

    def body(*refs):
        if has_res:
            xp_ref, o_ref, gt_ref, sh_ref, sc_ref, g_ref, xn_ref, h_ref = refs
            xn = xp_ref[...] + gt_ref[0] * o_ref[...]
            xn_ref[...] = xn
        else:
            xp_ref, sh_ref, sc_ref, g_ref, h_ref = refs
            xn = xp_ref[...]
        h_ref[...] = _modulate(xn, g_ref[...], sc_ref[0], sh_ref[0]).astype(BF16)

    ins = [(xp, _rs(D))]
    if has_res:
        ins += [(o, _rs(D)), (mod_gt, _gs(D, gt_i, nlat))]
    ins += [(mod_n, _gs(D, sh_i, nlat)), (mod_n, _gs(D, sc_i, nlat)), (norm_g, _ps((1, D)))]
    outs = ([(_sd((n, D), F32), _rs(D))] if has_res else []) + [(_sd((n, D), BF16), _rs(D))]
    r = _rowcall(name, body, n, ins, outs)
    return (r[0], r[1]) if has_res else (xp, r[0])


def resid_mod_bwd(name, xn, dxn, dh, o, mod_gt, gt_i, mod_n, sh_i, sc_i, norm_g, nlat):
    n = xn.shape[0]
    has_res = o is not None

    def body(*refs):
        i = pl.program_id(0)
        if has_res:
            (xn_ref, dxn_ref, dh_ref, o_ref, gt_ref, sh_ref, sc_ref, g_ref,
             dx_ref, do_ref, dgt_ref, dsh_ref, dsc_ref, dg_ref) = refs
        else:
            xn_ref, dxn_ref, dh_ref, sh_ref, sc_ref, g_ref, dx_ref, dsh_ref, dsc_ref, dg_ref = refs
        _, vjp = jax.vjp(_modulate, xn_ref[...], g_ref[...], sc_ref[0], sh_ref[0])
        dx, dg, dsc, dsh = vjp(dh_ref[...])
        dx = dx + dxn_ref[...]
        dx_ref[...] = dx
        gfirst = (i == 0) | (i == nlat)
        _acc(dg_ref, dg, i == 0)
        _acc(dsh_ref, dsh[None], gfirst)
        _acc(dsc_ref, dsc[None], gfirst)
        if has_res:
            do_ref[...] = (gt_ref[0] * dx).astype(BF16)
            _acc(dgt_ref, jnp.sum(dx * o_ref[...], axis=0, keepdims=True)[None], gfirst)

    ins = [(xn, _rs(D)), (dxn, _rs(D)), (dh, _rs(D))]
    if has_res:
        ins += [(o, _rs(D)), (mod_gt, _gs(D, gt_i, nlat))]
    ins += [(mod_n, _gs(D, sh_i, nlat)), (mod_n, _gs(D, sc_i, nlat)), (norm_g, _ps((1, D)))]
    gacc = (_sd((2, 1, D), F32), _gs(D, 0, nlat))
    outs = [(_sd((n, D), F32), _rs(D))]
    if has_res:
        outs += [(_sd((n, D), BF16), _rs(D)), gacc]
    outs += [gacc, gacc, (_sd((1, D), F32), _ps((1, D)))]
    r = _rowcall(name, body, n, ins, outs)
    if has_res:
        return r
    return r[0], None, None, r[1], r[2], r[3]


def resid_loss(name, xp, o, mod_gt, gt_i, target, nlat):
    n = xp.shape[0]

    def body(xp_ref, o_ref, gt_ref, t_ref, loss_ref, dx_ref, do_ref, dgt_ref):
        i = pl.program_id(0)
        gt = gt_ref[0]

        @pl.when(i < nlat)
        def _():
            err = xp_ref[...] + gt * o_ref[...] - t_ref[...]
            dx = err * (1.0 / D)
            dx_ref[...] = dx
            do_ref[...] = (gt * dx).astype(BF16)
            _acc(loss_ref, jnp.full((1, 128), 0.5 / D, F32) * jnp.sum(err * err), i == 0)
            _acc(dgt_ref, jnp.sum(dx * o_ref[...], axis=0, keepdims=True)[None], i == 0)

        @pl.when(i >= nlat)
        def _():
            dx_ref[...] = jnp.zeros((RT, D), F32)
            do_ref[...] = jnp.zeros((RT, D), BF16)
            dgt_ref[...] = jnp.zeros((1, 1, D), F32)

    tgt_spec = pl.BlockSpec((RT, D), lambda i: (jnp.minimum(i, nlat - 1), 0))
    ins = [(xp, _rs(D)), (o, _rs(D)), (mod_gt, _gs(D, gt_i, nlat)), (target, tgt_spec)]
    outs = [(_sd((1, 128), F32), _ps((1, 128))), (_sd((n, D), F32), _rs(D)), (_sd((n, D), BF16), _rs(D)),
            (_sd((2, 1, D), F32), _gs(D, 0, nlat))]
    return _rowcall(name, body, n, ins, outs)


def mod_fwd(name, c8, w_mod, b_mod):
    tn = 1536

    def body(c_ref, w_ref, b_ref, o_ref, s_ref):
        s = _silu(c_ref[...]).astype(BF16)
        s_ref[...] = s
        o_ref[...] = jnp.dot(s, w_ref[...], preferred_element_type=F32) + b_ref[...]

    return pl.pallas_call(
        body, out_shape=[_sd((8, 6 * D), F32), _sd((8, D), BF16)], grid=(6 * D // tn,),
        in_specs=[pl.BlockSpec((8, D), lambda j: (0, 0)), pl.BlockSpec((D, tn), lambda j: (0, j)),
                  pl.BlockSpec((1, tn), lambda j: (0, j))],
        out_specs=[pl.BlockSpec((8, tn), lambda j: (0, j)), pl.BlockSpec((8, D), lambda j: (0, 0))],
        name=name, compiler_params=_cp(("arbitrary",)))(c8, w_mod, b_mod)


def mod_small_bwd(name, c8, dsilu, dmod8):
    def body(c_ref, ds_ref, dm_ref, dc_ref, db_ref):
        _, vjp = jax.vjp(_silu, c_ref[...])
        dc_ref[...] = vjp(ds_ref[...])[0]
        db_ref[...] = jnp.sum(dm_ref[...], axis=0, keepdims=True)

    return pl.pallas_call(
        body, out_shape=[_sd((8, D), F32), _sd((1, 6 * D), F32)], grid=(1,),
        in_specs=[pl.BlockSpec((8, D), lambda j: (0, 0)), pl.BlockSpec((8, D), lambda j: (0, 0)),
                  pl.BlockSpec((8, 6 * D), lambda j: (0, 0))],
        out_specs=[pl.BlockSpec((8, D), lambda j: (0, 0)), pl.BlockSpec((1, 6 * D), lambda j: (0, 0))],
        name=name, compiler_params=_cp(("arbitrary",)))(c8, dsilu, dmod8)


def _conv_taps(x, nlat):
    n = x.shape[0]
    r = lax.broadcasted_iota(jnp.int32, x.shape, 0)
    lo = jnp.where(r < nlat, 0, nlat)
    hi = jnp.where(r < nlat, nlat, n)
    taps = []
    for o in (-2, -1, 0, 1, 2):
        xs = x if o == 0 else pltpu.roll(x, (-o) % n, 0)
        t = r + o
        taps.append(jnp.where((t >= lo) & (t < hi), xs, 0.0))
    return taps


def conv_fwd(name, u, w, b, nlat_rows):
    n = u.shape[0]

    def body(x_ref, w_ref, b_ref, o_ref):
        taps = _conv_taps(x_ref[...], nlat_rows)
        wv = w_ref[...]
        pre = b_ref[...] + sum(taps[k] * wv[k:k + 1, :] for k in range(5))
        o_ref[...] = _silu(pre)

    return pl.pallas_call(
        body, out_shape=_sd((n, 1536), F32), grid=(12,),
        in_specs=[pl.BlockSpec((n, 128), lambda j: (0, C_XS // 128 + j)), pl.BlockSpec((5, 128), lambda j: (0, j)),
                  pl.BlockSpec((1, 128), lambda j: (0, j))],
        out_specs=pl.BlockSpec((n, 128), lambda j: (0, j)),
        name=name, compiler_params=_cp(("parallel",)))(u, w, b)


def conv_bwd(name, u, dact, w, b, nlat_rows, chan0):
    n, nch = dact.shape
    t0 = chan0 // 128

    def body(x_ref, da_ref, w_ref, b_ref, dx_ref, dw_ref, db_ref):
        taps = _conv_taps(x_ref[...], nlat_rows)
        wv = w_ref[...]
        pre = b_ref[...] + sum(taps[k] * wv[k:k + 1, :] for k in range(5))
        s = jax.nn.sigmoid(pre)
        dpre = da_ref[...] * (s * (1.0 + pre * (1.0 - s)))
        db_ref[...] = jnp.sum(dpre, axis=0, keepdims=True)
        rows = lax.broadcasted_iota(jnp.int32, (5, 128), 0)
        dw = jnp.zeros((5, 128), F32)
        for k in range(5):
            dw = dw + jnp.where(rows == k, jnp.sum(dpre * taps[k], axis=0, keepdims=True), 0.0)
        dw_ref[...] = dw
        r = lax.broadcasted_iota(jnp.int32, dpre.shape, 0)
        lo = jnp.where(r < nlat_rows, 0, nlat_rows)
        hi = jnp.where(r < nlat_rows, nlat_rows, n)
        dx = jnp.zeros_like(dpre)
        for k in range(5):
            o = k - 2
            ds = dpre if o == 0 else pltpu.roll(dpre, o % n, 0)
            t = r - o
            dx = dx + jnp.where((t >= lo) & (t < hi), ds, 0.0) * wv[k:k + 1, :]
        dx_ref[...] = dx.astype(BF16)

    return pl.pallas_call(
        body, out_shape=[_sd((n, nch), BF16), _sd((5, nch), F32), _sd((1, nch), F32)], grid=(nch // 128,),
        in_specs=[pl.BlockSpec((n, 128), lambda j: (0, C_XS // 128 + t0 + j)), pl.BlockSpec((n, 128), lambda j: (0, j)),
                  pl.BlockSpec((5, 128), lambda j: (0, t0 + j)), pl.BlockSpec((1, 128), lambda j: (0, t0 + j))],
        out_specs=[pl.BlockSpec((n, 128), lambda j: (0, j)), pl.BlockSpec((5, 128), lambda j: (0, j)),
                   pl.BlockSpec((1, 128), lambda j: (0, j))],
        name=name, compiler_params=_cp(("parallel",)))(u, dact, w, b)


def _ssd_chunk(rev, dirn, g, x4, bm, cm, misc, dtrow, bias_c, alog_c, bias_r, alog_r, h4):
    dt_c = _softplus(misc + bias_c)
    a_c = dt_c * (-jnp.exp(alog_c))
    dt_r = _softplus(dtrow + bias_r)
    a_r = dt_r * (-jnp.exp(alog_r))
    cs_c = cum_cols(a_c, rev)
    cs_r = cum_rows(a_r, rev)
    tot_c = jnp.sum(a_c, axis=0, keepdims=True)
    cb = bdot_nt(cm, bm)
    tri = _tri(rev)
    lane = lax.broadcasted_iota(jnp.int32, (1, 128), 1)
    row16 = lax.broadcasted_iota(jnp.int32, (16, 1), 0)
    prow = lax.broadcasted_iota(jnp.int32, (128, 1), 0)
    ys, hs = [], []
    for p in range(4):
        ydiag = 0.0
        wst = 0.0
        eoff = 0.0
        hscale = 0.0
        for e in range(2):
            hg = 8 * g + 2 * p + e
            oh_c = (lane == DT_LANE + 16 * dirn + hg).astype(F32)
            dt_h = jnp.sum(dt_c * oh_c, axis=1, keepdims=True)
            cs_h = jnp.sum(cs_c * oh_c, axis=1, keepdims=True)
            tot_h = jnp.sum(tot_c * oh_c, axis=1, keepdims=True)
            csr_h = jnp.sum(cs_r * (row16 == hg).astype(F32), axis=0, keepdims=True)
            seg = jnp.exp(jnp.where(tri, cs_h - csr_h, -jnp.inf))
            hm = ((lane < 64) if e == 0 else (lane >= 64)).astype(F32)
            ydiag = ydiag + bdot(cb * seg, x4[p] * (dt_h * hm))
            wst = wst + (dt_h * jnp.exp(tot_h - cs_h)) * hm
            eoff = eoff + jnp.exp(cs_h) * hm
            hscale = hscale + jnp.exp(tot_h) * ((prow < 64) if e == 0 else (prow >= 64)).astype(F32)
        ys.append(ydiag + bdot_nt(cm, h4[p]) * eoff)
        hs.append(h4[p] * hscale + bdot_tn(x4[p] * wst, bm))
    return ys, hs


def _ssd_specs(nlat_chunks, rev, dirn, bwd):
    nc = nlat_chunks + 2

    def chunk(s):
        if bwd:
            s = nc - 1 - s
        return (nlat_chunks + 1 - s) if rev else (s + nlat_chunks) % nc

    def step(s):
        return (nc - 1 - s) if bwd else s

    return dict(
        x=pl.BlockSpec((Q, 512), lambda g, s: (chunk(s), g)),
        b=pl.BlockSpec((Q, 128), lambda g, s: (chunk(s), 8 + g)),
        c=pl.BlockSpec((Q, 128), lambda g, s: (chunk(s), 10 + g)),
        misc=pl.BlockSpec((Q, 128), lambda g, s: (chunk(s), C_MISC // 128)),
        dtrow=pl.BlockSpec((16, Q), lambda g, s: (dirn, chunk(s))),
        p_c=pl.BlockSpec((1, 128), lambda g, s: (0, 0)),
        p_r=pl.BlockSpec((16, 1), lambda g, s: (dirn, 0)),
        y=pl.BlockSpec((Q, 512), lambda g, s: (chunk(s), g)),
        hsave=pl.BlockSpec((1, 1, 512, 128), lambda g, s: (g, step(s), 0, 0)),
        bc_out=pl.BlockSpec((Q, 128), lambda g, s: (chunk(s), g)),
        misc_out=pl.BlockSpec((1, Q, 128), lambda g, s: (g, chunk(s), 0)),
        dtrow_out=pl.BlockSpec((1, 16, Q), lambda g, s: (g, 0, chunk(s))),
        pacc_c=pl.BlockSpec((1, 128), lambda g, s: (0, 0)),
        pacc_r=pl.BlockSpec((16, 1), lambda g, s: (0, 0)),
    )


def ssd_fwd(name, xbc, u, dtrow, bias_c, alog_c, bias_r, alog_r, nlat_chunks, rev, dirn):
    n = xbc.shape[0]
    nc = nlat_chunks + 2
    sp = _ssd_specs(nlat_chunks, rev, dirn, False)

    def body(x_ref, b_ref, c_ref, m_ref, r_ref, bc_ref, ac_ref, br_ref, ar_ref, y_ref, hs_ref, h_s):
        g = pl.program_id(0)
        s = pl.program_id(1)

        @pl.when(s == 0)
        def _():
            h_s[...] = jnp.zeros((512, 128), F32)

        hs_ref[0, 0] = h_s[...]
        x4 = [x_ref[:, 128 * p:128 * p + 128] for p in range(4)]
        h4 = [h_s[128 * p:128 * p + 128, :] for p in range(4)]
        ys, hs = _ssd_chunk(rev, dirn, g, x4, b_ref[...], c_ref[...], m_ref[...], r_ref[...],
                            bc_ref[...], ac_ref[...], br_ref[...], ar_ref[...], h4)
        for p in range(4):
            y_ref[:, 128 * p:128 * p + 128] = ys[p]
            h_s[128 * p:128 * p + 128, :] = hs[p]

    return pl.pallas_call(
        body, out_shape=[_sd((n, 1024), F32), _sd((2, nc, 512, 128), F32)], grid=(2, nc),
        in_specs=[sp["x"], sp["b"], sp["c"], sp["misc"], sp["dtrow"], sp["p_c"], sp["p_c"], sp["p_r"], sp["p_r"]],
        out_specs=[sp["y"], sp["hsave"]], scratch_shapes=[pltpu.VMEM((512, 128), F32)],
        name=name, compiler_params=_cp(("arbitrary", "arbitrary")))(
            xbc, xbc, xbc, u, dtrow, bias_c, alog_c, bias_r, alog_r)


def ssd_bwd(name, xbc, u, dtrow, bias_c, alog_c, bias_r, alog_r, hsave, dy, acc, nlat_chunks, rev, dirn):
    n = xbc.shape[0]
    sp = _ssd_specs(nlat_chunks, rev, dirn, True)

    def body(x_ref, b_ref, c_ref, m_ref, r_ref, bc_ref, ac_ref, br_ref, ar_ref, hs_ref, dy_ref, ax_ref, ab_ref, acc_ref,
             dx_ref, db_ref, dc_ref, dm_ref, dr_ref, dbc_ref, dac_ref, dbr_ref, dar_ref, dh_s):
        g = pl.program_id(0)
        s = pl.program_id(1)

        @pl.when(s == 0)
        def _():
            dh_s[...] = jnp.zeros((512, 128), F32)

        x4 = [x_ref[:, 128 * p:128 * p + 128] for p in range(4)]
        h4 = [hs_ref[0, 0, 128 * p:128 * p + 128, :] for p in range(4)]
        fn = functools.partial(_ssd_chunk, rev, dirn, g)
        _, vjp = jax.vjp(fn, x4, b_ref[...], c_ref[...], m_ref[...], r_ref[...],
                         bc_ref[...], ac_ref[...], br_ref[...], ar_ref[...], h4)
        dys = [dy_ref[:, 128 * p:128 * p + 128] for p in range(4)]
        dhs = [dh_s[128 * p:128 * p + 128, :] for p in range(4)]
        dx4, db, dc, dm, dr, dbc, dac, dbr, dar, dh4 = vjp((dys, dhs))
        for p in range(4):
            dx_ref[:, 128 * p:128 * p + 128] = dx4[p] + ax_ref[:, 128 * p:128 * p + 128]
            dh_s[128 * p:128 * p + 128, :] = dh4[p]
        db_ref[...] = db + ab_ref[...]
        dc_ref[...] = dc + acc_ref[...]
        dm_ref[0] = dm
        dr_ref[0] = dr
        first = (g == 0) & (s == 0)
        _acc(dbc_ref, dbc, first)
        _acc(dac_ref, dac, first)
        _acc(dbr_ref, dbr, first)
        _acc(dar_ref, dar, first)

    ax, ab, ac = acc
    return pl.pallas_call(
        body,
        out_shape=[_sd((n, 1024), F32), _sd((n, 256), F32), _sd((n, 256), F32), _sd((2, n, 128), F32),
                   _sd((2, 16, n), F32), _sd((1, 128), F32), _sd((1, 128), F32), _sd((16, 1), F32), _sd((16, 1), F32)],
        grid=(2, nlat_chunks + 2),
        in_specs=[sp["x"], sp["b"], sp["c"], sp["misc"], sp["dtrow"], sp["p_c"], sp["p_c"], sp["p_r"], sp["p_r"],
                  sp["hsave"], sp["y"], sp["y"], sp["bc_out"], sp["bc_out"]],
        out_specs=[sp["y"], sp["bc_out"], sp["bc_out"], sp["misc_out"], sp["dtrow_out"],
                   sp["pacc_c"], sp["pacc_c"], sp["pacc_r"], sp["pacc_r"]],
        scratch_shapes=[pltpu.VMEM((512, 128), F32)],
        name=name, compiler_params=_cp(("arbitrary", "arbitrary")))(
            xbc, xbc, xbc, u, dtrow, bias_c, alog_c, bias_r, alog_r, hsave, dy, ax, ab, ac)


def _ssd_out(yf, yb, xs, z, g, dexp):
    return _rms((yf + yb + dexp * xs) * _silu(z), g)


def ssd_out_fwd(name, yf, yb, xbc, u, g, dexp):
    n = yf.shape[0]

    def body(yf_ref, yb_ref, xs_ref, z_ref, g_ref, d_ref, o_ref):
        o_ref[...] = _ssd_out(yf_ref[...], yb_ref[...], xs_ref[...], z_ref[...], g_ref[...], d_ref[...]).astype(BF16)

    return _rowcall(name, body, n,
                    [(yf, _rs(D)), (yb, _rs(D)), (xbc, _rs(D, 0)), (u, _rs(D, C_Z // D)), (g, _ps((1, D))), (dexp, _ps((1, D)))],
                    [(_sd((n, D), BF16), _rs(D))])[0]


def ssd_out_bwd(name, yf, yb, xbc, u, g, dexp, dys):
    n = yf.shape[0]

    def body(yf_ref, yb_ref, xs_ref, z_ref, g_ref, d_ref, dys_ref, dy_ref, dxs_ref, dz_ref, dg_ref, dd_ref):
        i = pl.program_id(0)
        _, vjp = jax.vjp(_ssd_out, yf_ref[...], yb_ref[...], xs_ref[...], z_ref[...], g_ref[...], d_ref[...])
        dyf, _, dxs, dz, dg, dd = vjp(dys_ref[...])
        dy_ref[...] = dyf
        dxs_ref[...] = dxs
        dz_ref[...] = dz.astype(BF16)
        _acc(dg_ref, dg, i == 0)
        _acc(dd_ref, dd, i == 0)

    return _rowcall(name, body, n,
                    [(yf, _rs(D)), (yb, _rs(D)), (xbc, _rs(D, 0)), (u, _rs(D, C_Z // D)), (g, _ps((1, D))), (dexp, _ps((1, D))),
                     (dys, _rs(D))],
                    [(_sd((n, D), F32), _rs(D)), (_sd((n, D), F32), _rs(D)), (_sd((n, D), BF16), _rs(D)),
                     (_sd((1, D), F32), _ps((1, D))), (_sd((1, D), F32), _ps((1, D)))])


def _normrope(x, g, cos, sin, s, n=None):
    return _rope(_rms(x, g, n), cos, sin, s)


def swa_prep_fwd(name, u, gq, gk, cos, sin):
    n = u.shape[0]

    def body(q_ref, k_ref, gq_ref, gk_ref, cos_ref, sin_ref, qs_ref, ks_ref):
        cs, sn = cos_ref[...], sin_ref[...]
        for h in range(SWA_HQ):
            sl = slice(128 * h, 128 * h + 128)
            qs_ref[:, sl] = _normrope(q_ref[:, sl], gq_ref[...], cs, sn, 32).astype(BF16)
        for h in range(SWA_HKV):
            sl = slice(128 * h, 128 * h + 128)
            ks_ref[:, sl] = _normrope(k_ref[:, sl], gk_ref[...], cs, sn, 32).astype(BF16)

    return _rowcall(name, body, n,
                    [(u, _rs(1024, C_Q // 1024)), (u, _rs(256, C_K // 256)), (gq, _ps((1, 128))), (gk, _ps((1, 128))),
                     (cos, _rs(128)), (sin, _rs(128))],
                    [(_sd((n, 1024), BF16), _rs(1024)), (_sd((n, 256), BF16), _rs(256))])


def swa_prep_bwd(name, u, gq, gk, cos, sin, dqs, dks, dv):
    n = u.shape[0]

    def body(q_ref, k_ref, gq_ref, gk_ref, cos_ref, sin_ref, dqs_ref, dks_ref, dv_ref,
             dq_ref, dk_ref, dvo_ref, dgq_ref, dgk_ref):
        i = pl.program_id(0)
        cs, sn = cos_ref[...], sin_ref[...]
        fn = lambda x, g: _normrope(x, g, cs, sn, 32)
        dgq = jnp.zeros((1, 128), F32)
        dgk = jnp.zeros((1, 128), F32)
        for h in range(SWA_HQ):
            sl = slice(128 * h, 128 * h + 128)
            _, vjp = jax.vjp(fn, q_ref[:, sl], gq_ref[...])
            dx, dg = vjp(dqs_ref[:, sl])
            dq_ref[:, sl] = dx.astype(BF16)
            dgq = dgq + dg
        for h in range(SWA_HKV):
            sl = slice(128 * h, 128 * h + 128)
            _, vjp = jax.vjp(fn, k_ref[:, sl], gk_ref[...])
            dx, dg = vjp(dks_ref[:, sl])
            dk_ref[:, sl] = dx.astype(BF16)
            dgk = dgk + dg
        dvo_ref[...] = dv_ref[...].astype(BF16)
        _acc(dgq_ref, dgq, i == 0)
        _acc(dgk_ref, dgk, i == 0)

    return _rowcall(name, body, n,
                    [(u, _rs(1024, C_Q // 1024)), (u, _rs(256, C_K // 256)), (gq, _ps((1, 128))), (gk, _ps((1, 128))),
                     (cos, _rs(128)), (sin, _rs(128)), (dqs, _rs(1024)), (dks, _rs(256)), (dv, _rs(256))],
                    [(_sd((n, 1024), BF16), _rs(1024)), (_sd((n, 256), BF16), _rs(256)), (_sd((n, 256), BF16), _rs(256)),
                     (_sd((1, 128), F32), _ps((1, 128))), (_sd((1, 128), F32), _ps((1, 128)))])


def lat_norm_fwd(name, u, g_kv, g_q):
    n = u.shape[0]

    def body(ckv_ref, cq_ref, gkv_ref, gq_ref, okv_ref, oq_ref):
        okv_ref[...] = _rms(ckv_ref[...], gkv_ref[...]).astype(BF16)
        oq_ref[...] = _rms(cq_ref[...], gq_ref[...]).astype(BF16)

    return _rowcall(name, body, n,
                    [(u, _rs(256, C_CKV // 256)), (u, _rs(384, C_CQ // 384)), (g_kv, _ps((1, 256))), (g_q, _ps((1, 384)))],
                    [(_sd((n, 256), BF16), _rs(256)), (_sd((n, 384), BF16), _rs(384))])


def lat_norm_bwd(name, u, g_kv, g_q, dkvn, dqn):
    n = u.shape[0]

    def body(ckv_ref, cq_ref, gkv_ref, gq_ref, dkvn_ref, dqn_ref, dckv_ref, dcq_ref, dgkv_ref, dgq_ref):
        i = pl.program_id(0)
        _, vjp = jax.vjp(_rms, ckv_ref[...], gkv_ref[...])
        dx, dg = vjp(dkvn_ref[...])
        dckv_ref[...] = dx.astype(BF16)
        _acc(dgkv_ref, dg, i == 0)
        _, vjp = jax.vjp(_rms, cq_ref[...], gq_ref[...])
        dx, dg = vjp(dqn_ref[...])
        dcq_ref[...] = dx.astype(BF16)
        _acc(dgq_ref, dg, i == 0)

    return _rowcall(name, body, n,
                    [(u, _rs(256, C_CKV // 256)), (u, _rs(384, C_CQ // 384)), (g_kv, _ps((1, 256))), (g_q, _ps((1, 384))),
                     (dkvn, _rs(256)), (dqn, _rs(384))],
                    [(_sd((n, 256), BF16), _rs(256)), (_sd((n, 384), BF16), _rs(384)),
                     (_sd((1, 256), F32), _ps((1, 256))), (_sd((1, 384), F32), _ps((1, 384)))])


def _lane_lt64(x):
    return (lax.broadcasted_iota(jnp.int32, (1, 128), 1) < 64).astype(F32) * x


def _mla_krope(misc, g, cos, sin):
    return _normrope(_lane_lt64(misc), g, cos, sin, 16, MLA_ROPE)


def mla_prep_fwd(name, kv, qp, u, qg, kg, cos, sin):
    n = kv.shape[0]

    def body(kv_ref, v_ref, q_ref, m_ref, qg_ref, kg_ref, cos_ref, sin_ref, km_ref, qm_ref, vm_ref):
        cs, sn = cos_ref[...], sin_ref[...]
        vm_ref[...] = v_ref[...].astype(BF16)
        kr = _mla_krope(m_ref[...], kg_ref[:, 128:256], cs, sn).astype(BF16)
        for h in range(MLA_H):
            km_ref[:, 256 * h:256 * h + 128] = _rms(kv_ref[:, 128 * h:128 * h + 128], kg_ref[:, 0:128]).astype(BF16)
            km_ref[:, 256 * h + 128:256 * h + 256] = kr
            qm_ref[:, 256 * h:256 * h + 128] = _rms(q_ref[:, 256 * h:256 * h + 128], qg_ref[:, 0:128]).astype(BF16)
            qm_ref[:, 256 * h + 128:256 * h + 256] = _normrope(
                q_ref[:, 256 * h + 128:256 * h + 256], qg_ref[:, 128:256], cs, sn, 16, MLA_ROPE).astype(BF16)

    return _rowcall(name, body, n,
                    [(kv, _rs(1024, 0)), (kv, _rs(1024, 1)), (qp, _rs(2048)), (u, _rs(128, C_MISC // 128)), (qg, _ps((1, 256))),
                     (kg, _ps((1, 256))), (cos, _rs(128)), (sin, _rs(128))],
                    [(_sd((n, 2048), BF16), _rs(2048)), (_sd((n, 2048), BF16), _rs(2048)), (_sd((n, 1024), BF16), _rs(1024))])


def mla_prep_bwd(name, kv, qp, u, qg, kg, cos, sin, dkm, dqm, dv):
    n = kv.shape[0]

    def body(kv_ref, q_ref, m_ref, qg_ref, kg_ref, cos_ref, sin_ref, dkm_ref, dqm_ref, dv_ref,
             dkv_ref, dq_ref, dkr_ref, dqg_ref, dkg_ref):
        i = pl.program_id(0)
        cs, sn = cos_ref[...], sin_ref[...]
        fr = lambda x, g: _normrope(x, g, cs, sn, 16, MLA_ROPE)
        dkg_n = jnp.zeros((1, 128), F32)
        dqg_n = jnp.zeros((1, 128), F32)
        dqg_r = jnp.zeros((1, 128), F32)
        dkr_sum = jnp.zeros((RT, 128), F32)
        for h in range(MLA_H):
            dk_h = jnp.transpose(dkm_ref[256 * h:256 * h + 256, :])
            dkv_ref[:, 1024 + 128 * h:1024 + 128 * h + 128] = jnp.transpose(dv_ref[128 * h:128 * h + 128, :]).astype(BF16)
            _, vjp = jax.vjp(_rms, kv_ref[:, 128 * h:128 * h + 128], kg_ref[:, 0:128])
            dx, dg = vjp(dk_h[:, 0:128])
            dkv_ref[:, 128 * h:128 * h + 128] = dx.astype(BF16)
            dkg_n = dkg_n + dg
            dkr_sum = dkr_sum + dk_h[:, 128:256]
            _, vjp = jax.vjp(_rms, q_ref[:, 256 * h:256 * h + 128], qg_ref[:, 0:128])
            dx, dg = vjp(dqm_ref[:, 256 * h:256 * h + 128])
            dq_ref[:, 256 * h:256 * h + 128] = dx.astype(BF16)
            dqg_n = dqg_n + dg
            _, vjp = jax.vjp(fr, q_ref[:, 256 * h + 128:256 * h + 256], qg_ref[:, 128:256])
            dx, dg = vjp(dqm_ref[:, 256 * h + 128:256 * h + 256])
            dq_ref[:, 256 * h + 128:256 * h + 256] = dx.astype(BF16)
            dqg_r = dqg_r + dg
        _, vjp = jax.vjp(lambda m, g: _mla_krope(m, g, cs, sn), m_ref[...], kg_ref[:, 128:256])
        dm, dkg_r = vjp(dkr_sum)
        dkr_ref[...] = dm
        _acc(dqg_ref.at[:, 0:128], dqg_n, i == 0)
        _acc(dqg_ref.at[:, 128:256], dqg_r, i == 0)
        _acc(dkg_ref.at[:, 0:128], dkg_n, i == 0)
        _acc(dkg_ref.at[:, 128:256], dkg_r, i == 0)

    return _rowcall(name, body, n,
                    [(kv, _rs(1024, 0)), (qp, _rs(2048)), (u, _rs(128, C_MISC // 128)), (qg, _ps((1, 256))), (kg, _ps((1, 256))),
                     (cos, _rs(128)), (sin, _rs(128)), (dkm, pl.BlockSpec((2048, RT), lambda i: (0, i))), (dqm, _rs(2048)),
                     (dv, pl.BlockSpec((1024, RT), lambda i: (0, i)))],
                    [(_sd((n, 2048), BF16), _rs(2048)), (_sd((n, 2048), BF16), _rs(2048)), (_sd((n, 128), F32), _rs(128)),
                     (_sd((1, 256), F32), _ps((1, 256))), (_sd((1, 256), F32), _ps((1, 256)))])


def misc_combine(name, dkr, dm_f, dm_b, drow_t):
    n = dkr.shape[0]

    def body(a_ref, f_ref, b_ref, r_ref, o_ref):
        o_ref[...] = (a_ref[...] + f_ref[0] + f_ref[1] + b_ref[0] + b_ref[1] + r_ref[...]).astype(BF16)

    g2 = pl.BlockSpec((2, RT, 128), lambda i: (0, i, 0))
    return _rowcall(name, body, n, [(dkr, _rs(128)), (dm_f, g2), (dm_b, g2), (drow_t, _rs(128))],
                    [(_sd((n, 128), BF16), _rs(128))])[0]


def _f32(ref):
    return ref[...].astype(F32)


def _merge(g1, g2, g3, p1, p2, p3):
    return jax.nn.sigmoid(g1) * p1 + jax.nn.sigmoid(g2) * p2 + jax.nn.sigmoid(g3) * p3


def merge_fwd(name, u, p1, p2, p3):
    n = u.shape[0]

    def body(g1, g2, g3, a, b, c, o_ref):
        o_ref[...] = _merge(g1[...], g2[...], g3[...], _f32(a), _f32(b), _f32(c)).astype(BF16)

    return _rowcall(name, body, n, [(u, _rs(D, 0)), (u, _rs(D, 1)), (u, _rs(D, 2)), (p1, _rs(D)), (p2, _rs(D)), (p3, _rs(D))],
                    [(_sd((n, D), BF16), _rs(D))])[0]


def merge_bwd(name, u, p1, p2, p3, dm):
    n = u.shape[0]

    def body(g1, g2, g3, a, b, c, dm_ref, d1, d2, d3, dg_ref):
        _, vjp = jax.vjp(_merge, g1[...], g2[...], g3[...], _f32(a), _f32(b), _f32(c))
        r = vjp(dm_ref[...])
        for k in range(3):
            dg_ref[:, D * k:D * k + D] = r[k].astype(BF16)
        d1[...] = r[3].astype(BF16)
        d2[...] = r[4].astype(BF16)
        d3[...] = r[5].astype(BF16)

    return _rowcall(name, body, n,
                    [(u, _rs(D, 0)), (u, _rs(D, 1)), (u, _rs(D, 2)), (p1, _rs(D)), (p2, _rs(D)), (p3, _rs(D)), (dm, _rs(D))],
                    [(_sd((n, D), BF16), _rs(D))] * 3 + [(_sd((n, 3 * D), BF16), _rs(3 * D))])


def _swiglu(g, u):
    return _silu(g) * u


def swiglu_fwd(name, gu):
    n = gu.shape[0]

    def body(g_ref, u_ref, o_ref):
        o_ref[...] = _swiglu(_f32(g_ref), _f32(u_ref)).astype(BF16)

    return _rowcall(name, body, n, [(gu, _rs(FFN, 0)), (gu, _rs(FFN, 1))], [(_sd((n, FFN), BF16), _rs(FFN))])[0]


def swiglu_bwd(name, gu, da):
    n = gu.shape[0]

    def body(g_ref, u_ref, da_ref, o_ref):
        _, vjp = jax.vjp(_swiglu, _f32(g_ref), _f32(u_ref))
        dg, du = vjp(da_ref[...])
        o_ref[:, 0:FFN] = dg.astype(BF16)
        o_ref[:, FFN:2 * FFN] = du.astype(BF16)

    return _rowcall(name, body, n, [(gu, _rs(FFN, 0)), (gu, _rs(FFN, 1)), (da, _rs(FFN))],
                    [(_sd((n, 2 * FFN), BF16), _rs(2 * FFN))])[0]


FLASH_ROWS = 256


def _fold_lanes(x, op):
    acc = x[:, 0:128]
    for b in range(1, x.shape[1] // 128):
        acc = op(acc, x[:, 128 * b:128 * b + 128])
    return acc


def _band_mask(tq, tk, i, kb):
    qp = i * tq + lax.broadcasted_iota(jnp.int32, (tq, tk), 0)
    kp = kb * tk + lax.broadcasted_iota(jnp.int32, (tq, tk), 1)
    return jnp.abs(qp - kp) <= SWA_WIN


def flash_fwd(name, qa, ka, va, *, w, vw, hq, grp, vcol0, scale, nlat, tq, tk, band, sink, ctx_q, prev=None):
    n = qa.shape[0]
    cblk = nlat // NCTX
    band = band and not ctx_q
    assert not band, "latent rows of a banded attention go through swa_fwd_lat"
    if ctx_q:
        tq = tk = NCTX
        grid = (hq, 1, 1)
        qmap = lambda h, i, kk: (cblk, h)
        kmap = lambda h, i, kk: (cblk, h // grp)
        vmap = lambda h, i, kk: (cblk, vcol0 + h // grp)
        omap = lambda h, i, kk: (cblk, h)
        lmap = lambda h, i, kk: (h, cblk, 0)
    else:
        nb = nlat // tk
        nk = 3 if band else nb
        grid = (hq, nlat // tq, nk)
        kb_of = (lambda i, kk: jnp.clip(i + kk - 1, 0, nb - 1)) if band else (lambda i, kk: kk)
        qmap = lambda h, i, kk: (i, h)
        kmap = lambda h, i, kk: (kb_of(i, kk), h // grp)
        vmap = lambda h, i, kk: (kb_of(i, kk), vcol0 + h // grp)
        omap = lambda h, i, kk: (i, h)
        lmap = lambda h, i, kk: (h, i, 0)
    nk = grid[2]
    extra = not ctx_q
    has_sink = sink is not None

    def body(*refs):
        refs = list(refs)
        q_ref, k_ref, v_ref = refs[:3]
        pos = 3
        if extra:
            ke_ref, ve_ref = refs[pos:pos + 2]
            pos += 2
        if has_sink:
            s_ref = refs[pos]
            pos += 1
        if prev is not None:
            pos += 2
        o_ref, l_ref, m_s, l_s, a_s = refs[pos:pos + 5]
        kk = pl.program_id(2)
        tr = min(tq, FLASH_ROWS)

        def step(kblk, vblk):
            for r in range(tq // tr):
                rows = slice(r * tr, (r + 1) * tr)
                s = _d(q_ref[rows, :], kblk, ((1,), (1,))) * (scale * LOG2E)
                m_prev = m_s[rows, :]
                m_new = jnp.maximum(m_prev, jnp.max(_fold_lanes(s, jnp.maximum), axis=1, keepdims=True))
                alpha = jnp.exp2(m_prev - m_new)
                p = jnp.exp2(s - m_new)
                l_s[rows, :] = alpha * l_s[rows, :] + _fold_lanes(p, jnp.add)
                a_s[rows, :] = alpha * a_s[rows, :] + _d(p, vblk, ((1,), (0,)))
                m_s[rows, :] = m_new

        @pl.when(kk == 0)
        def _():
            if has_sink:
                sv = jnp.max(s_ref[0], axis=1, keepdims=True) * LOG2E
                m_s[...] = jnp.zeros((tq, 1), F32) + sv
                l_s[...] = (lax.broadcasted_iota(jnp.int32, (tq, 128), 1) == 0).astype(F32)
            else:
                m_s[...] = jnp.full((tq, 1), NEG, F32)
                l_s[...] = jnp.zeros((tq, 128), F32)
            a_s[...] = jnp.zeros((tq, vw), F32)
            if extra:
                step(ke_ref[...], ve_ref[...])

        step(k_ref[...], v_ref[...])

        @pl.when(kk == nk - 1)
        def _():
            l = jnp.sum(l_s[...], axis=1, keepdims=True)
            o_ref[...] = (a_s[...] / l).astype(BF16)
            l_ref[0] = m_s[...] + jnp.log2(l)

    ins = [(qa, pl.BlockSpec((tq, w), qmap)), (ka, pl.BlockSpec((tk, w), kmap)), (va, pl.BlockSpec((tk, vw), vmap))]
    if extra:
        ins += [(ka, pl.BlockSpec((NCTX, w), lambda h, i, kk: (cblk, h // grp))),
                (va, pl.BlockSpec((NCTX, vw), lambda h, i, kk: (cblk, vcol0 + h // grp)))]
    if has_sink:
        ins += [(sink, pl.BlockSpec((1, 1, 128), lambda h, i, kk: (h, 0, 0)))]
    aliases = {}
    if prev is not None:
        any_spec = pl.BlockSpec(memory_space=pl.ANY)
        aliases = {len(ins): 0, len(ins) + 1: 1}
        ins += [(prev[0], any_spec), (prev[1], any_spec)]
    return pl.pallas_call(
        body, out_shape=[_sd((n, hq * vw), BF16), _sd((hq, n, 1), F32)], grid=grid,
        in_specs=[s for _, s in ins],
        out_specs=[pl.BlockSpec((tq, vw), omap), pl.BlockSpec((1, tq, 1), lmap)],
        scratch_shapes=[pltpu.VMEM((tq, 1), F32), pltpu.VMEM((tq, 128), F32), pltpu.VMEM((tq, vw), F32)],
        input_output_aliases=aliases, name=name,
        compiler_params=_cp(("parallel", "parallel", "arbitrary")))(*[a for a, _ in ins])


def flash_dq(name, qa, ka, va, oa, doa, lse, *, w, vw, hq, grp, vcol0, scale, nlat, tq, tk, band, sink, ctx_q, prev=None):
    n = qa.shape[0]
    cblk = nlat // NCTX
    band = band and not ctx_q
    if ctx_q:
        tq = tk = NCTX
        grid = (hq, 1, 1)
        qmap = lambda h, i, kk: (cblk, h)
        kmap = lambda h, i, kk: (cblk, h // grp)
        vmap = lambda h, i, kk: (cblk, vcol0 + h // grp)
        lmap = lambda h, i, kk: (h, cblk, 0)
    else:
        nb = nlat // tk
        grid = (hq, nlat // tq, 3 if band else nb)
        kb_of = (lambda i, kk: jnp.clip(i + kk - 1, 0, nb - 1)) if band else (lambda i, kk: kk)
        qmap = lambda h, i, kk: (i, h)
        kmap = lambda h, i, kk: (kb_of(i, kk), h // grp)
        vmap = lambda h, i, kk: (kb_of(i, kk), vcol0 + h // grp)
        lmap = lambda h, i, kk: (h, i, 0)
    nk = grid[2]
    nq = grid[1]
    extra = not ctx_q
    has_sink = sink is not None

    def body(*refs):
        refs = list(refs)
        q_ref, k_ref, v_ref, o_ref, do_ref, l_ref = refs[:6]
        pos = 6
        if extra:
            ke_ref, ve_ref = refs[pos:pos + 2]
            pos += 2
        if has_sink:
            s_ref = refs[pos]
            pos += 1
        if prev is not None:
            pos += 2
        dq_ref, dl_ref, ds_ref, acc_s, dl_s = refs[pos:pos + 5]
        i = pl.program_id(1)
        kk = pl.program_id(2)
        q = q_ref[...]
        do = do_ref[...]
        lse_v = l_ref[0]

        def step(kblk, vblk, mask):
            s = _d(q, kblk, ((1,), (1,))) * (scale * LOG2E)
            if mask is not None:
                s = jnp.where(mask, s, NEG)
            p = jnp.exp2(s - lse_v)
            dp = _d(do, vblk, ((1,), (1,)))
            ds = p * (dp - dl_s[...]) * scale
            acc_s[...] += _d(ds, kblk, ((1,), (0,)))

        @pl.when(kk == 0)
        def _():
            delta = jnp.sum(do * o_ref[...].astype(F32), axis=1, keepdims=True)
            dl_s[...] = delta
            acc_s[...] = jnp.zeros((tq, w), F32)
            if has_sink:
                sv = jnp.max(s_ref[0], axis=1, keepdims=True) * LOG2E
                dsk = jnp.sum(-jnp.exp2(sv - lse_v) * delta, axis=0, keepdims=True)
                _acc(ds_ref, jnp.zeros((1, 1, 128), F32) + dsk, i == 0)
            else:
                ds_ref[...] = jnp.zeros((1, 1, 128), F32)
            if extra:
                step(ke_ref[...], ve_ref[...], None)

        if band:
            kb = i + kk - 1

            @pl.when((kb >= 0) & (kb < nlat // tk))
            def _():
                step(k_ref[...], v_ref[...], _band_mask(tq, tk, i, kb))
        else:
            step(k_ref[...], v_ref[...], None)

        @pl.when(kk == nk - 1)
        def _():
            dq_ref[...] = acc_s[...]
            dl_ref[0] = dl_s[...]

    ins = [(qa, pl.BlockSpec((tq, w), qmap)), (ka, pl.BlockSpec((tk, w), kmap)), (va, pl.BlockSpec((tk, vw), vmap)),
           (oa, pl.BlockSpec((tq, vw), qmap)), (doa, pl.BlockSpec((tq, vw), qmap)), (lse, pl.BlockSpec((1, tq, 1), lmap))]
    if extra:
        ins += [(ka, pl.BlockSpec((NCTX, w), lambda h, i, kk: (cblk, h // grp))),
                (va, pl.BlockSpec((NCTX, vw), lambda h, i, kk: (cblk, vcol0 + h // grp)))]
    if has_sink:
        ins += [(sink, pl.BlockSpec((1, 1, 128), lambda h, i, kk: (h, 0, 0)))]
    aliases = {}
    if prev is not None:
        any_spec = pl.BlockSpec(memory_space=pl.ANY)
        aliases = {len(ins): 0, len(ins) + 1: 1}
        ins += [(prev[0], any_spec), (prev[1], any_spec)]
    del nq
    return pl.pallas_call(
        body, out_shape=[_sd((n, hq * w), F32), _sd((hq, n, 1), F32), _sd((hq, 1, 128), F32)], grid=grid,
        in_specs=[s for _, s in ins],
        out_specs=[pl.BlockSpec((tq, w), qmap), pl.BlockSpec((1, tq, 1), lmap),
                   pl.BlockSpec((1, 1, 128), lambda h, i, kk: (h, 0, 0))],
        scratch_shapes=[pltpu.VMEM((tq, w), F32), pltpu.VMEM((tq, 1), F32)],
        input_output_aliases=aliases, name=name,
        compiler_params=_cp(("parallel", "arbitrary", "arbitrary")))(*[a for a, _ in ins])


def flash_dkv(name, qa, ka, va, doa, lse, delta, *, w, vw, hkv, grp, vcol0, scale, nlat, tq, tk, band, ctx_k, prev=None):
    n = qa.shape[0]
    cblk = nlat // NCTX
    nqb = nlat // tq
    band = band and not ctx_k
    if ctx_k:
        tk = NCTX
        nqs = nqb
        grid = (hkv, 1, grp * nqs)
        kmap = lambda hk, j, t: (cblk, hk)
        vmap = lambda hk, j, t: (cblk, vcol0 + hk)
        dvmap = lambda hk, j, t: (cblk, hk)
        qb_of = lambda j, t: t % nqs
    else:
        nqs = 3 if band else nqb
        grid = (hkv, nlat // tk, grp * nqs)
        kmap = lambda hk, j, t: (j, hk)
        vmap = lambda hk, j, t: (j, vcol0 + hk)
        dvmap = lambda hk, j, t: (j, hk)
        qb_of = (lambda j, t: jnp.clip(j + t % nqs - 1, 0, nqb - 1)) if band else (lambda j, t: t % nqs)
    qmap = lambda hk, j, t: (qb_of(j, t), hk * grp + t // nqs)
    lmap = lambda hk, j, t: (hk * grp + t // nqs, qb_of(j, t), 0)

    def body(*refs):
        refs = list(refs)
        q_ref, k_ref, v_ref, do_ref, l_ref, dl_ref = refs[:6]
        pos = 6
        if ctx_k:
            qe_ref, doe_ref, le_ref, dle_ref = refs[pos:pos + 4]
            pos += 4
        if prev is not None:
            pos += 2
        dk_ref, dv_ref = refs[pos:pos + 2]
        j = pl.program_id(1)
        t = pl.program_id(2)
        kblk = k_ref[...]
        vblk = v_ref[...]

        def contrib(q, do, lse_v, dl_v, mask):
            s = _d(q, kblk, ((1,), (1,))) * (scale * LOG2E)
            if mask is not None:
                s = jnp.where(mask, s, NEG)
            p = jnp.exp2(s - lse_v)
            dp = _d(do, vblk, ((1,), (1,)))
            ds = p * (dp - dl_v) * scale
            return _d(ds, q, ((0,), (0,))), _d(p, do, ((0,), (0,)))

        @pl.when(t == 0)
        def _():
            dk = jnp.zeros((tk, w), F32)
            dv = jnp.zeros((tk, vw), F32)
            if ctx_k:
                for gi in range(grp):
                    a, b = contrib(qe_ref[:, w * gi:w * gi + w], doe_ref[:, vw * gi:vw * gi + vw], le_ref[gi], dle_ref[gi], None)
                    dk = dk + a
                    dv = dv + b
            dk_ref[...] = dk
            dv_ref[...] = dv

        def add(mask):
            a, b = contrib(q_ref[...], do_ref[...], l_ref[0], dl_ref[0], mask)
            dk_ref[...] += a
            dv_ref[...] += b

        if band:
            qb = j + t % nqs - 1

            @pl.when((qb >= 0) & (qb < nqb))
            def _():
                add(_band_mask(tq, tk, qb, j))
        else:
            add(None)

    ins = [(qa, pl.BlockSpec((tq, w), qmap)), (ka, pl.BlockSpec((tk, w), kmap)), (va, pl.BlockSpec((tk, vw), vmap)),
           (doa, pl.BlockSpec((tq, vw), qmap)), (lse, pl.BlockSpec((1, tq, 1), lmap)), (delta, pl.BlockSpec((1, tq, 1), lmap))]
    if ctx_k:
        ins += [(qa, pl.BlockSpec((NCTX, grp * w), lambda hk, j, t: (cblk, hk))),
                (doa, pl.BlockSpec((NCTX, grp * vw), lambda hk, j, t: (cblk, hk))),
                (lse, pl.BlockSpec((grp, NCTX, 1), lambda hk, j, t: (hk, cblk, 0))),
                (delta, pl.BlockSpec((grp, NCTX, 1), lambda hk, j, t: (hk, cblk, 0)))]
    aliases = {}
    if prev is not None:
        any_spec = pl.BlockSpec(memory_space=pl.ANY)
        aliases = {len(ins): 0, len(ins) + 1: 1}
        ins += [(prev[0], any_spec), (prev[1], any_spec)]
    return pl.pallas_call(
        body, out_shape=[_sd((n, hkv * w), F32), _sd((n, hkv * vw), F32)], grid=grid,
        in_specs=[s for _, s in ins],
        out_specs=[pl.BlockSpec((tk, w), kmap), pl.BlockSpec((tk, vw), dvmap)],
        input_output_aliases=aliases, name=name,
        compiler_params=_cp(("parallel", "parallel", "arbitrary")))(*[a for a, _ in ins])


MLA_FWD_CHUNKS = 3
MLA_BWD_CHUNKS = 2


def mla_fwd(name, qm, km, vm, nlat):
    n = qm.shape[0]
    t = NCTX
    nlt = nlat // t
    c = (MLA_NOPE + MLA_ROPE) ** -0.5 * LOG2E

    nchunk = MLA_FWD_CHUNKS if (n // 128) % MLA_FWD_CHUNKS == 0 else 1
    cw = n // nchunk

    def body(q_ref, k_ref, v_ref, o_ref, l_ref):
        i = pl.program_id(1)

        def run(spans):
            parts = []
            for a, b in spans:
                s = _d(q_ref[...], k_ref[a:b, :], ((1,), (1,))) * c
                m = jnp.max(_fold_lanes(s, jnp.maximum), axis=1, keepdims=True)
                p = jnp.exp2(s - m)
                parts.append((m, jnp.sum(_fold_lanes(p, jnp.add), axis=1, keepdims=True), _d(p, v_ref[a:b, :], ((1,), (0,)))))
            m = functools.reduce(jnp.maximum, [pt[0] for pt in parts])
            l = sum(jnp.exp2(pm - m) * pl_ for pm, pl_, _ in parts)
            acc = sum(jnp.exp2(pm - m) * pa for pm, _, pa in parts)
            o_ref[...] = (acc / l).astype(BF16)
            l_ref[0] = m + jnp.log2(l)

        @pl.when(i < nlt)
        def _():
            run([(j * cw, (j + 1) * cw) for j in range(nchunk)])

        @pl.when(i == nlt)
        def _():
            run([(nlat, n)])

    return pl.pallas_call(
        body, out_shape=[_sd((n, MLA_H * 128), BF16), _sd((MLA_H, n, 1), F32)], grid=(MLA_H, n // t),
        in_specs=[pl.BlockSpec((t, 256), lambda h, i: (i, h)), pl.BlockSpec((n, 256), lambda h, i: (0, h)),
                  pl.BlockSpec((n, 128), lambda h, i: (0, h))],
        out_specs=[pl.BlockSpec((t, 128), lambda h, i: (i, h)), pl.BlockSpec((1, t, 1), lambda h, i: (h, i, 0))],
        name=name, compiler_params=_cp(("parallel", "arbitrary")))(qm, km, vm)


def mla_bwd(name, qm, km, vm, o, do, lse, nlat):
    n = qm.shape[0]
    t = NCTX
    nlt = nlat // t
    scale = (MLA_NOPE + MLA_ROPE) ** -0.5
    nchunk = MLA_BWD_CHUNKS if (n // 128) % MLA_BWD_CHUNKS == 0 else 1
    cw = n // nchunk

    def body(q_ref, k_ref, v_ref, o_ref, do_ref, l_ref, dq_ref, dkt_ref, dvt_ref):
        i = pl.program_id(1)

        @pl.when(i == 0)
        def _():
            dkt_ref[...] = jnp.zeros((256, n), F32)
            dvt_ref[...] = jnp.zeros((128, n), F32)

        q = q_ref[...]
        do = do_ref[...]
        delta = jnp.sum(do.astype(F32) * o_ref[...].astype(F32), axis=1, keepdims=True)

        def run(spans):
            dq = jnp.zeros((t, 256), F32)
            for a, b in spans:
                kc = k_ref[a:b, :]
                s = _d(q, kc, ((1,), (1,))) * (scale * LOG2E)
                p = jnp.exp2(s - l_ref[0])
                ds = (p * (_d(do, v_ref[a:b, :], ((1,), (1,))) - delta) * scale).astype(BF16)
                dq = dq + _d(ds, kc, ((1,), (0,)))
                dkt_ref[:, a:b] += _d(q, ds, ((0,), (0,)))
                dvt_ref[:, a:b] += _d(do, p, ((0,), (0,)))
            dq_ref[...] = dq

        @pl.when(i < nlt)
        def _():
            run([(c * cw, (c + 1) * cw) for c in range(nchunk)])

        @pl.when(i == nlt)
        def _():
            run([(nlat, n)])

    qspec = pl.BlockSpec((t, 256), lambda h, i: (i, h))
    ospec = pl.BlockSpec((t, 128), lambda h, i: (i, h))
    return pl.pallas_call(
        body, out_shape=[_sd((n, MLA_H * 256), F32), _sd((MLA_H * 256, n), F32), _sd((MLA_H * 128, n), F32)],
        grid=(MLA_H, n // t),
        in_specs=[qspec, pl.BlockSpec((n, 256), lambda h, i: (0, h)), pl.BlockSpec((n, 128), lambda h, i: (0, h)),
                  ospec, ospec, pl.BlockSpec((1, t, 1), lambda h, i: (h, i, 0))],
        out_specs=[qspec, pl.BlockSpec((256, n), lambda h, i: (h, 0)), pl.BlockSpec((128, n), lambda h, i: (h, 0))],
        name=name, compiler_params=_cp(("parallel", "arbitrary")))(qm, km, vm, o, do, lse)


def mla_attention_bwd(tag, qm, km, vm, o, do, lse, nlat):
    return mla_bwd(tag + "_bwd", qm, km, vm, o, do, lse, nlat)


SWA_T = 512


def _swa_window(t, nlat):
    t = min(t, nlat)
    return t, min(t + 2 * SWA_WIN, nlat)


def _win_start(i, t, wlen, nlat):
    return pl.multiple_of(jnp.clip(i * t - SWA_WIN, 0, nlat - wlen), 128)


def _win_mask(rows, cols, row0, col0):
    rp = row0 + lax.broadcasted_iota(jnp.int32, (rows, cols), 0)
    cp = col0 + lax.broadcasted_iota(jnp.int32, (rows, cols), 1)
    return jnp.abs(rp - cp) <= SWA_WIN


def swa_fwd_lat(name, qs, ks, u, sink, nlat):
    n = qs.shape[0]
    tq, wlen = _swa_window(SWA_T, nlat)
    grp = SWA_HQ // SWA_HKV
    scale = SWA_DH ** -0.5
    vcol0 = C_V // 128

    def body(q_ref, k_ref, v_ref, s_ref, o_ref, l_ref):
        i = pl.program_id(1)
        ws = _win_start(i, tq, wlen, nlat)
        q = q_ref[...]
        s1 = _d(q, k_ref[pl.ds(ws, wlen), :], ((1,), (1,))) * (scale * LOG2E)
        s1 = jnp.where(_win_mask(tq, wlen, i * tq, ws), s1, NEG)
        s2 = _d(q, k_ref[pl.ds(nlat, NCTX), :], ((1,), (1,))) * (scale * LOG2E)
        sv = jnp.max(s_ref[0], axis=1, keepdims=True) * LOG2E
        m = jnp.maximum(jnp.maximum(jnp.max(s1, axis=1, keepdims=True), jnp.max(s2, axis=1, keepdims=True)), sv)
        p1 = jnp.exp2(s1 - m)
        p2 = jnp.exp2(s2 - m)
        l = jnp.sum(p1, axis=1, keepdims=True) + jnp.sum(p2, axis=1, keepdims=True) + jnp.exp2(sv - m)
        acc = _d(p1, v_ref[pl.ds(ws, wlen), :], ((1,), (0,))) + _d(p2, v_ref[pl.ds(nlat, NCTX), :], ((1,), (0,)))
        o_ref[...] = (acc / l).astype(BF16)
        l_ref[0] = m + jnp.log2(l)

    return pl.pallas_call(
        body, out_shape=[_sd((n, SWA_HQ * 128), BF16), _sd((SWA_HQ, n, 1), F32)], grid=(SWA_HQ, nlat // tq),
        in_specs=[pl.BlockSpec((tq, 128), lambda h, i: (i, h)), pl.BlockSpec((n, 128), lambda h, i: (0, h // grp)),
                  pl.BlockSpec((n, 128), lambda h, i: (0, vcol0 + h // grp)), pl.BlockSpec((1, 1, 128), lambda h, i: (h, 0, 0))],
        out_specs=[pl.BlockSpec((tq, 128), lambda h, i: (i, h)), pl.BlockSpec((1, tq, 1), lambda h, i: (h, i, 0))],
        name=name, compiler_params=_cp(("parallel", "arbitrary")))(qs, ks, u, sink)


def swa_dq_lat(name, qs, ks, u, o, do, lse, sink, nlat):
    n = qs.shape[0]
    tq, wlen = _swa_window(SWA_T, nlat)
    grp = SWA_HQ // SWA_HKV
    scale = SWA_DH ** -0.5
    vcol0 = C_V // 128

    def body(q_ref, k_ref, v_ref, s_ref, o_ref, do_ref, l_ref, dq_ref, dl_ref, ds_ref):
        i = pl.program_id(1)
        ws = _win_start(i, tq, wlen, nlat)
        q = q_ref[...]
        do = do_ref[...]
        lse_v = l_ref[0]
        delta = jnp.sum(do.astype(F32) * o_ref[...].astype(F32), axis=1, keepdims=True)
        kw = k_ref[pl.ds(ws, wlen), :]
        kc = k_ref[pl.ds(nlat, NCTX), :]
        s1 = _d(q, kw, ((1,), (1,))) * (scale * LOG2E)
        s1 = jnp.where(_win_mask(tq, wlen, i * tq, ws), s1, NEG)
        s2 = _d(q, kc, ((1,), (1,))) * (scale * LOG2E)
        ds1 = jnp.exp2(s1 - lse_v) * (_d(do, v_ref[pl.ds(ws, wlen), :], ((1,), (1,))) - delta) * scale
        ds2 = jnp.exp2(s2 - lse_v) * (_d(do, v_ref[pl.ds(nlat, NCTX), :], ((1,), (1,))) - delta) * scale
        dq_ref[...] = _d(ds1, kw, ((1,), (0,))) + _d(ds2, kc, ((1,), (0,)))
        dl_ref[0] = delta
        sv = jnp.max(s_ref[0], axis=1, keepdims=True) * LOG2E
        dsk = jnp.sum(-jnp.exp2(sv - lse_v) * delta, axis=0, keepdims=True)
        _acc(ds_ref, jnp.zeros((1, 1, 128), F32) + dsk, i == 0)

    qspec = pl.BlockSpec((tq, 128), lambda h, i: (i, h))
    lspec = pl.BlockSpec((1, tq, 1), lambda h, i: (h, i, 0))
    return pl.pallas_call(
        body, out_shape=[_sd((n, SWA_HQ * 128), F32), _sd((SWA_HQ, n, 1), F32), _sd((SWA_HQ, 1, 128), F32)],
        grid=(SWA_HQ, nlat // tq),
        in_specs=[qspec, pl.BlockSpec((n, 128), lambda h, i: (0, h // grp)),
                  pl.BlockSpec((n, 128), lambda h, i: (0, vcol0 + h // grp)), pl.BlockSpec((1, 1, 128), lambda h, i: (h, 0, 0)),
                  qspec, qspec, lspec],
        out_specs=[qspec, lspec, pl.BlockSpec((1, 1, 128), lambda h, i: (h, 0, 0))],
        name=name, compiler_params=_cp(("parallel", "arbitrary")))(qs, ks, u, sink, o, do, lse)


def swa_dkv_lat(name, qs, ks, u, do, lse_row, delta_row, nlat):
    n = qs.shape[0]
    tk, wlen = _swa_window(SWA_T, nlat)
    grp = SWA_HQ // SWA_HKV
    scale = SWA_DH ** -0.5
    vcol0 = C_V // 128

    def body(q_ref, k_ref, v_ref, do_ref, l_ref, dl_ref, dk_ref, dv_ref):
        j = pl.program_id(1)
        ws = _win_start(j, tk, wlen, nlat)
        k = k_ref[...]
        v = v_ref[...]
        mask = _win_mask(tk, wlen, j * tk, ws)
        dk = jnp.zeros((tk, 128), F32)
        dv = jnp.zeros((tk, 128), F32)
        for gi in range(grp):
            qw = q_ref[pl.ds(ws, wlen), 128 * gi:128 * gi + 128]
            dow = do_ref[pl.ds(ws, wlen), 128 * gi:128 * gi + 128]
            st = jnp.where(mask, _d(k, qw, ((1,), (1,))) * (scale * LOG2E), NEG)
            pt = jnp.exp2(st - l_ref[gi, :, pl.ds(ws, wlen)])
            dv = dv + _d(pt, dow, ((1,), (0,)))
            dst = pt * (_d(v, dow, ((1,), (1,))) - dl_ref[gi, :, pl.ds(ws, wlen)]) * scale
            dk = dk + _d(dst, qw, ((1,), (0,)))
        dk_ref[...] = dk
        dv_ref[...] = dv

    rspec = pl.BlockSpec((grp, 1, n), lambda hk, j: (hk, 0, 0))
    return pl.pallas_call(
        body, out_shape=[_sd((n, SWA_HKV * 128), F32), _sd((n, SWA_HKV * 128), F32)], grid=(SWA_HKV, nlat // tk),
        in_specs=[pl.BlockSpec((n, grp * 128), lambda hk, j: (0, hk)), pl.BlockSpec((tk, 128), lambda hk, j: (j, hk)),
                  pl.BlockSpec((tk, 128), lambda hk, j: (j, vcol0 + hk)), pl.BlockSpec((n, grp * 128), lambda hk, j: (0, hk)),
                  rspec, rspec],
        out_specs=[pl.BlockSpec((tk, 128), lambda hk, j: (j, hk)), pl.BlockSpec((tk, 128), lambda hk, j: (j, hk))],
        name=name, compiler_params=_cp(("parallel", "arbitrary")))(qs, ks, u, do, lse_row, delta_row)


def swa_attention_fwd(tag, qs, ks, u, sink, cfg, nlat):
    o, lse = swa_fwd_lat(tag + "_fwd_lat", qs, ks, u, sink, nlat)
    return flash_fwd(tag + "_fwd_ctx", qs, ks, u, sink=sink, ctx_q=True, nlat=nlat, prev=(o, lse), **cfg)


def swa_attention_bwd(tag, qs, ks, u, o, do, lse, sink, cfg, nlat):
    n = qs.shape[0]
    dq, delta, ds1 = swa_dq_lat(tag + "_dq_lat", qs, ks, u, o, do, lse, sink, nlat)
    dq, delta, ds2 = flash_dq(tag + "_dq_ctx", qs, ks, u, o, do, lse, sink=sink, ctx_q=True, nlat=nlat, prev=(dq, delta), **cfg)
    dk, dv = swa_dkv_lat(tag + "_dkv_lat", qs, ks, u, do, lse.reshape(SWA_HQ, 1, n), delta.reshape(SWA_HQ, 1, n), nlat)
    kc = {k: v for k, v in cfg.items() if k != "hq"}
    kc["hkv"] = SWA_HKV
    kc["tq"] = min(1024, nlat)
    dk, dv = flash_dkv(tag + "_dkv_ctx", qs, ks, u, do, lse, delta, ctx_k=True, nlat=nlat, prev=(dk, dv), **kc)
    return dq, dk, dv, ds1 + ds2


def adamw(name, w, g, m, v):
    r, c = w.shape
    tr = _pick(r, (256, 128, 64, 32, 16, 8))
    bc1 = 1.0 - ADAM_B1 ** ADAM_STEP
    bc2 = 1.0 - ADAM_B2 ** ADAM_STEP

    def body(w_ref, g_ref, m_ref, v_ref, d_ref, nm_ref, nv_ref):
        gv = g_ref[...]
        nm = ADAM_B1 * m_ref[...] + (1.0 - ADAM_B1) * gv
        nv = ADAM_B2 * v_ref[...] + (1.0 - ADAM_B2) * (gv * gv)
        d_ref[...] = -ADAM_LR * ((nm / bc1) / (jnp.sqrt(nv / bc2) + ADAM_EPS) + ADAM_WD * w_ref[...])
        nm_ref[...] = nm
        nv_ref[...] = nv

    spec = pl.BlockSpec((tr, c), lambda i: (i, 0))
    return pl.pallas_call(body, out_shape=[_sd((r, c), F32)] * 3, grid=(r // tr,), in_specs=[spec] * 4, out_specs=[spec] * 3,
                          name=name, compiler_params=_cp(("parallel",)))(w, g, m, v)


def _coords():
    return lax.axis_index("x"), lax.axis_index("y"), lax.axis_index("c")


_ANY = pl.BlockSpec(memory_space=pl.ANY)


def _chip():
    return 2 * lax.axis_index("x") + lax.axis_index("y")


def _per_core(fn):
    c = lax.axis_index("c")
    for cs in (0, 1):
        pl.when(c == cs)(functools.partial(fn, cs))


def gather_chips(name, arrs):
    nj = len(arrs)
    halves = [a.shape[0] // 2 for a in arrs]

    def body(*refs):
        _per_core(functools.partial(run, refs[:nj], refs[nj:2 * nj], *refs[2 * nj:]))

    def run(a_refs, o_refs, ici_send, ici_recv, d2d_send, d2d_recv, c):
        x, y, _ = _coords()
        me = 2 * x + y
        peers = [(1 - x, y), (x, 1 - y), (1 - x, 1 - y)]
        mine = [pl.ds(c * h, h) for h in halves]
        sibs = [pl.ds((1 - c) * h, h) for h in halves]

        def ici(k, j, blk):
            return pltpu.make_async_remote_copy(a_refs[j].at[mine[j]], o_refs[j].at[blk, mine[j]], ici_send.at[k * nj + j],
                                                ici_recv.at[k * nj + j], device_id=(*peers[k], c), device_id_type=MESH)

        def d2d(k, j, rows):
            blk = 2 * peers[k][0] + peers[k][1]
            return pltpu.make_async_remote_copy(o_refs[j].at[blk, rows[j]], o_refs[j].at[blk, rows[j]], d2d_send.at[k * nj + j],
                                                d2d_recv.at[k * nj + j], device_id=(x, y, 1 - c), device_id_type=MESH)

        sends = [ici(k, j, me) for k in range(3) for j in range(nj)]
        for cp in sends:
            cp.start()
        passed = []
        for k in range(3):
            for j in range(nj):
                ici(k, j, 2 * peers[k][0] + peers[k][1]).wait_recv()
                fw = d2d(k, j, mine)
                fw.start()
                passed.append(fw)
        for k in range(3):
            for j in range(nj):
                d2d(k, j, sibs).wait_recv()
        for cp in sends + passed:
            cp.wait_send()

    outs = pl.pallas_call(
        body, out_shape=[_sd((4,) + a.shape, a.dtype) for a in arrs], in_specs=[_ANY] * nj, out_specs=[_ANY] * nj,
        scratch_shapes=[pltpu.SemaphoreType.DMA((3 * nj,))] * 4,
        name=name, compiler_params=pltpu.CompilerParams(has_side_effects=True))(*arrs)
    return [lax.dynamic_update_index_in_dim(o, a, _chip(), 0) for o, a in zip(outs, arrs)]


def pair_split(name, arrs):
    nj = len(arrs)
    halves = [a.shape[1] // 2 for a in arrs]

    def body(*refs):
        _per_core(functools.partial(run, refs[:nj], refs[nj:2 * nj], *refs[2 * nj:]))

    def run(a_refs, got_refs, send_sems, recv_sems, c):
        x, y, _ = _coords()
        cps = [pltpu.make_async_remote_copy(a_refs[j].at[:, pl.ds((1 - c) * halves[j], halves[j])], got_refs[j],
                                            send_sems.at[j], recv_sems.at[j], device_id=(x, y, 1 - c), device_id_type=MESH)
               for j in range(nj)]
        for cp in cps:
            cp.start()
        for cp in cps:
            cp.wait()

    got = pl.pallas_call(
        body, out_shape=[_sd((4, h, a.shape[2]), a.dtype) for a, h in zip(arrs, halves)], in_specs=[_ANY] * nj,
        out_specs=[_ANY] * nj, scratch_shapes=[pltpu.SemaphoreType.DMA((nj,))] * 2,
        name=name, compiler_params=pltpu.CompilerParams(has_side_effects=True))(*arrs)
    own = [lax.dynamic_slice_in_dim(a, lax.axis_index("c") * h, h, axis=1) for a, h in zip(arrs, halves)]
    return own, got


def scatter_chips(name, arrs):
    nj = len(arrs)

    def body(*refs):
        a_refs, o_refs = refs[:nj], refs[nj:2 * nj]
        send_sems, recv_sems = refs[2 * nj:]
        x, y, c = _coords()
        me = 2 * x + y
        peers = [(1 - x, y), (x, 1 - y), (1 - x, 1 - y)]

        def cp(k, j, src_blk, dst_blk):
            return pltpu.make_async_remote_copy(a_refs[j].at[src_blk], o_refs[j].at[dst_blk], send_sems.at[k * nj + j],
                                                recv_sems.at[k * nj + j], device_id=(*peers[k], c), device_id_type=MESH)

        sends = [cp(k, j, 2 * peers[k][0] + peers[k][1], me) for k in range(3) for j in range(nj)]
        for s in sends:
            s.start()
        for k in range(3):
            for j in range(nj):
                cp(k, j, me, 2 * peers[k][0] + peers[k][1]).wait_recv()
        for s in sends:
            s.wait_send()

    outs = pl.pallas_call(
        body, out_shape=[_sd(a.shape, a.dtype) for a in arrs], in_specs=[_ANY] * nj, out_specs=[_ANY] * nj,
        scratch_shapes=[pltpu.SemaphoreType.DMA((3 * nj,))] * 2,
        name=name, compiler_params=pltpu.CompilerParams(has_side_effects=True))(*arrs)
    return [lax.dynamic_update_index_in_dim(o, lax.dynamic_index_in_dim(a, _chip(), 0, keepdims=False), _chip(), 0)
            for o, a in zip(outs, arrs)]


def pair_join(name, arrs):
    nj = len(arrs)
    halves = [a.shape[0] for a in arrs]

    def body(*refs):
        _per_core(functools.partial(run, refs[:nj], refs[nj:2 * nj], *refs[2 * nj:]))

    def run(a_refs, o_refs, send_sems, recv_sems, c):
        x, y, _ = _coords()

        def cp(j, rows_of):
            return pltpu.make_async_remote_copy(a_refs[j], o_refs[j].at[pl.ds(rows_of * halves[j], halves[j])], send_sems.at[j],
                                                recv_sems.at[j], device_id=(x, y, 1 - c), device_id_type=MESH)

        sends = [cp(j, c) for j in range(nj)]
        for s in sends:
            s.start()
        for s in sends:
            s.wait_send()
        for j in range(nj):
            cp(j, 1 - c).wait_recv()

    outs = pl.pallas_call(
        body, out_shape=[_sd((2 * a.shape[0], a.shape[1]), a.dtype) for a in arrs], in_specs=[_ANY] * nj, out_specs=[_ANY] * nj,
        scratch_shapes=[pltpu.SemaphoreType.DMA((nj,))] * 2,
        name=name, compiler_params=pltpu.CompilerParams(has_side_effects=True))(*arrs)
    return [lax.dynamic_update_slice_in_dim(o, a, lax.axis_index("c") * a.shape[0], axis=0) for o, a in zip(outs, arrs)]


def add_cast(name, a, b, dtype):
    k, r, c = a.shape
    tr = _pick(r, (512, 256, 128, 64, 32, 16, 8))

    def body(a_ref, b_ref, o_ref):
        o_ref[...] = (a_ref[...].astype(F32) + b_ref[...].astype(F32)).astype(dtype)

    spec = pl.BlockSpec((1, tr, c), lambda s, i: (s, i, 0))
    return pl.pallas_call(body, out_shape=_sd((k, r, c), dtype), grid=(k, r // tr), in_specs=[spec, spec], out_specs=spec,
                          name=name, compiler_params=_cp(("parallel", "parallel")))(a, b)


def gather_all(name, a):
    def body(a_ref, o_ref, send_sems, recv_sems, loc_sem):
        x, y, c = _coords()
        me = 4 * x + 2 * y + c
        flips = [(fx, fy, fc) for fx in (0, 1) for fy in (0, 1) for fc in (0, 1) if fx + fy + fc > 0]
        peers = [(x ^ fx, y ^ fy, c ^ fc) for fx, fy, fc in flips]
        mine = pltpu.make_async_copy(a_ref, o_ref.at[me], loc_sem)
        mine.start()
        sends = [pltpu.make_async_remote_copy(a_ref, o_ref.at[me], send_sems.at[k], recv_sems.at[k],
                                              device_id=p, device_id_type=MESH) for k, p in enumerate(peers)]
        for cp in sends:
            cp.start()
        for k, (px, py, pc) in enumerate(peers):
            pltpu.make_async_remote_copy(a_ref, o_ref.at[4 * px + 2 * py + pc], send_sems.at[k], recv_sems.at[k],
                                         device_id=(px, py, pc), device_id_type=MESH).wait_recv()
        for cp in sends:
            cp.wait_send()
        mine.wait()

    return pl.pallas_call(
        body, out_shape=_sd((8,) + a.shape, a.dtype), in_specs=[_ANY], out_specs=_ANY,
        scratch_shapes=[pltpu.SemaphoreType.DMA((7,)), pltpu.SemaphoreType.DMA((7,)), pltpu.SemaphoreType.DMA],
        name=name, compiler_params=pltpu.CompilerParams(has_side_effects=True))(a)


def sum_blocks(name, a):
    k, r, c = a.shape
    tr = _pick(r, (256, 128, 64, 32, 16, 8))

    def body(a_ref, o_ref):
        acc = a_ref[0].astype(F32)
        for s in range(1, k):
            acc = acc + a_ref[s].astype(F32)
        o_ref[...] = acc

    return pl.pallas_call(body, out_shape=_sd((r, c), F32), grid=(r // tr,),
                          in_specs=[pl.BlockSpec((k, tr, c), lambda i: (0, i, 0))], out_specs=pl.BlockSpec((tr, c), lambda i: (i, 0)),
                          name=name, compiler_params=_cp(("parallel",)))(a)


BIG = ("w_mod", "w_in", "w_mla_uq", "w_mla_ukv", "w_p_ssm", "w_p_swa", "w_p_mla", "w_out", "w_ffn_in", "w_ffn_out")
COL_SHARDED = ("w_mod", "w_in", "w_mla_uq", "w_mla_ukv", "w_ffn_in")
SMALL = ("c_ctx", "b_mod", "norm1_g", "norm2_g", "ssm_conv_w", "ssm_conv_b", "ssm_dt_bias", "ssm_a_log", "ssm_d",
         "ssm_norm_g", "swa_q_norm_g", "swa_k_norm_g", "swa_sink", "mla_q_lat_g", "mla_kv_lat_g", "mla_q_norm_g",
         "mla_k_norm_g")
WEIGHTS = ("c_ctx", "w_mod", "b_mod", "norm1_g", "norm2_g", "w_in", "ssm_conv_w", "ssm_conv_b", "ssm_dt_bias", "ssm_a_log",
           "ssm_d", "ssm_norm_g", "swa_q_norm_g", "swa_k_norm_g", "swa_sink", "mla_q_lat_g", "mla_kv_lat_g", "w_mla_uq",
           "w_mla_ukv", "mla_q_norm_g", "mla_k_norm_g", "w_p_ssm", "w_p_swa", "w_p_mla", "w_out", "w_ffn_in", "w_ffn_out")


def pack_w_in(w):
    z = lambda k: jnp.zeros((w.shape[0], k), w.dtype)
    return jnp.concatenate([w[:, 4832:7904], w[:, 2400:3424], w[:, 3424:4448], w[:, 0:1536], w[:, 1568:1824], w[:, 1824:2080],
                            w[:, 2080:2336], w[:, 2336:2400], w[:, 1536:1568], z(32), z(128), w[:, 4448:4832]], axis=1)


def unpack_w_in(g):
    return jnp.concatenate([g[:, 5120:6656], g[:, 7488:7520], g[:, 6656:6912], g[:, 6912:7168], g[:, 7168:7424], g[:, 7424:7488],
                            g[:, 3072:4096], g[:, 4096:5120], g[:, 7680:8064], g[:, 0:3072]], axis=1)


def pack_ukv(w):
    return w.reshape(MLA_KVRANK, MLA_H, 2, 128).transpose(0, 2, 1, 3).reshape(MLA_KVRANK, 2048)


def unpack_ukv(g):
    return g.reshape(MLA_KVRANK, 2, MLA_H, 128).transpose(0, 2, 1, 3).reshape(MLA_KVRANK, 2048)


def pack_uq(w):
    return jnp.pad(w.reshape(MLA_QRANK, MLA_H, 192), ((0, 0), (0, 0), (0, 64))).reshape(MLA_QRANK, 2048)


def unpack_uq(g):
    return g.reshape(MLA_QRANK, MLA_H, 256)[:, :, :192].reshape(MLA_QRANK, 1536)


def rope_tables(nlat):
    t = jnp.arange(nlat, dtype=jnp.int32)
    r = (t // GRID_W).astype(F32)[:, None]
    col = (t % GRID_W).astype(F32)[:, None]

    def tab(nf, pad):
        inv = jnp.power(ROPE_BASE, -jnp.arange(nf, dtype=F32) / nf)
        ar, ac = r * inv, col * inv
        cos = jnp.concatenate([jnp.cos(ar), jnp.cos(ar), jnp.cos(ac), jnp.cos(ac), jnp.ones((nlat, pad), F32)], axis=1)
        sin = jnp.concatenate([-jnp.sin(ar), jnp.sin(ar), -jnp.sin(ac), jnp.sin(ac), jnp.zeros((nlat, pad), F32)], axis=1)
        cos = jnp.concatenate([cos, jnp.ones((NCTX, 128), F32)], axis=0)
        sin = jnp.concatenate([sin, jnp.zeros((NCTX, 128), F32)], axis=0)
        return cos, sin

    return tab(32, 0), tab(16, 64)


def _lanes(v, start, width=128):
    return jnp.zeros((1, width), F32).at[0, start:start + v.shape[0]].set(v)


def layer_fwd(i, xin, h, mod, p, tabs, nlat):
    t = "l%d_" % i
    n = xin.shape[0]
    (cos_s, sin_s), (cos_m, sin_m) = tabs
    u = mm(h, p["w_in"], F32, t + "in_proj")
    xbc = conv_fwd(t + "conv", u, p["conv_w"], p["conv_b"], nlat)
    dtrow = jnp.transpose(u[:, C_MISC + DT_LANE:C_MISC + DT_LANE + 32])
    nlc = nlat // Q
    yf, hs_f = ssd_fwd(t + "ssd_f", xbc, u, dtrow, p["bias_c"], p["alog_c"], p["bias_r"], p["alog_r"], nlc, False, 0)
    yb, hs_b = ssd_fwd(t + "ssd_b", xbc, u, dtrow, p["bias_c"], p["alog_c"], p["bias_r"], p["alog_r"], nlc, True, 1)
    ys = ssd_out_fwd(t + "ssd_out", yf, yb, xbc, u, p["ssm_norm_g"], p["d_exp"])
    qs, ks = swa_prep_fwd(t + "swa_prep", u, p["swa_q_g"], p["swa_k_g"], cos_s, sin_s)
    o_swa, lse_swa = swa_attention_fwd(t + "swa", qs, ks, u, p["sink"], p["swa_cfg"], nlat)
    ckv_n, cq_n = lat_norm_fwd(t + "lat_norm", u, p["kv_lat_g"], p["q_lat_g"])
    kv = mm(ckv_n, p["w_ukv"], F32, t + "ukv")
    qp = mm(cq_n, p["w_uq"], F32, t + "uq")
    km, qm, vm = mla_prep_fwd(t + "mla_prep", kv, qp, u, p["mla_q_g"], p["mla_k_g"], cos_m, sin_m)
    o_mla, lse_mla = mla_fwd(t + "mla_fwd", qm, km, vm, nlat)
    p1 = mm(ys, p["w_p_ssm"], BF16, t + "p_ssm")
    p2 = mm(o_swa, p["w_p_swa"], BF16, t + "p_swa")
    p3 = mm(o_mla, p["w_p_mla"], BF16, t + "p_mla")
    merged = merge_fwd(t + "merge", u, p1, p2, p3)
    o = mm(merged, p["w_out"], F32, t + "out_proj")
    x1, h2 = resid_mod_fwd(t + "res1", xin, o, mod, 2, mod, 3, 4, p["norm2_g"], nlat // RT)
    gu = mm(h2, p["w_ffn_in"], BF16, t + "ffn_in")
    a = swiglu_fwd(t + "swiglu", gu)
    f = mm(a, p["w_ffn_out"], F32, t + "ffn_out")
    saved = dict(xin=xin, h=h, u=u, xbc=xbc, dtrow=dtrow, yf=yf, yb=yb, hs_f=hs_f, hs_b=hs_b, ys=ys, qs=qs, ks=ks,
                 o_swa=o_swa, lse_swa=lse_swa, ckv_n=ckv_n, cq_n=cq_n, kv=kv, qp=qp, km=km, qm=qm, vm=vm, o_mla=o_mla,
                 lse_mla=lse_mla, p1=p1, p2=p2, p3=p3, merged=merged, o=o, x1=x1, h2=h2, gu=gu, a=a, f=f)
    del n
    return x1, f, saved


def layer_bwd(i, dx2, df, dgt2, sv, mod, p, tabs, nlat):
    t = "l%db_" % i
    (cos_s, sin_s), (cos_m, sin_m) = tabs
    g = {}
    nt = nlat // RT
    nlc = nlat // Q
    g["w_ffn_out"] = mm_tn(sv["a"], df, t + "wg_ffn_out")
    da = mm(df, p["w_ffn_out"], F32, t + "dg_ffn_out", trans_b=True)
    dgu = swiglu_bwd(t + "swiglu", sv["gu"], da)
    g["w_ffn_in"] = mm_tn(sv["h2"], dgu, t + "wg_ffn_in")
    dh2 = mm(dgu, p["w_ffn_in"], F32, t + "dg_ffn_in", trans_b=True)
    dx1, do, dgt1, dsh2, dsc2, g["norm2_g"] = resid_mod_bwd(t + "res1", sv["x1"], dx2, dh2, sv["o"], mod, 2, mod, 3, 4,
                                                              p["norm2_g"], nt)
    g["w_out"] = mm_tn(sv["merged"], do, t + "wg_out")
    dmerged = mm(do, p["w_out"], F32, t + "dg_out", trans_b=True)
    dp1, dp2, dp3, dgates = merge_bwd(t + "merge", sv["u"], sv["p1"], sv["p2"], sv["p3"], dmerged)
    g["w_p_ssm"] = mm_tn(sv["ys"], dp1, t + "wg_p_ssm")
    g["w_p_swa"] = mm_tn(sv["o_swa"], dp2, t + "wg_p_swa")
    g["w_p_mla"] = mm_tn(sv["o_mla"], dp3, t + "wg_p_mla")
    dys = mm(dp1, p["w_p_ssm"], F32, t + "dg_p_ssm", trans_b=True)
    do_swa = mm(dp2, p["w_p_swa"], BF16, t + "dg_p_swa", trans_b=True)
    do_mla = mm(dp3, p["w_p_mla"], BF16, t + "dg_p_mla", trans_b=True)
    dqm, dkm, dv_mla = mla_attention_bwd(t + "mla", sv["qm"], sv["km"], sv["vm"], sv["o_mla"], do_mla, sv["lse_mla"], nlat)
    dkv, dqp, dkr, g["mla_q_g"], g["mla_k_g"] = mla_prep_bwd(t + "mla_prep", sv["kv"], sv["qp"], sv["u"], p["mla_q_g"],
                                                             p["mla_k_g"], cos_m, sin_m, dkm, dqm, dv_mla)
    g["w_ukv"] = mm_tn(sv["ckv_n"], dkv, t + "wg_ukv")
    g["w_uq"] = mm_tn(sv["cq_n"], dqp, t + "wg_uq")
    dckv_n = mm(dkv, p["w_ukv"], F32, t + "dg_ukv", trans_b=True)
    dcq_n = mm(dqp, p["w_uq"], F32, t + "dg_uq", trans_b=True)
    dckv, dcq, g["kv_lat_g"], g["q_lat_g"] = lat_norm_bwd(t + "lat_norm", sv["u"], p["kv_lat_g"], p["q_lat_g"], dckv_n, dcq_n)
    dqs, dks, dv_swa, g["sink"] = swa_attention_bwd(t + "swa", sv["qs"], sv["ks"], sv["u"], sv["o_swa"], do_swa, sv["lse_swa"],
                                                p["sink"], p["swa_cfg"], nlat)
    dq, dk, dv, g["swa_q_g"], g["swa_k_g"] = swa_prep_bwd(t + "swa_prep", sv["u"], p["swa_q_g"], p["swa_k_g"], cos_s, sin_s,
                                                          dqs, dks, dv_swa)
    dy, dxs_skip, dz, g["ssm_norm_g"], g["d_exp"] = ssd_out_bwd(t + "ssd_out", sv["yf"], sv["yb"], sv["xbc"], sv["u"],
                                                                 p["ssm_norm_g"], p["d_exp"], dys)
    n = dy.shape[0]
    zbc = jnp.zeros((n, 256), F32)
    r_f = ssd_bwd(t + "ssd_f", sv["xbc"], sv["u"], sv["dtrow"], p["bias_c"], p["alog_c"], p["bias_r"], p["alog_r"],
                  sv["hs_f"], dy, (dxs_skip, zbc, zbc), nlc, False, 0)
    r_b = ssd_bwd(t + "ssd_b", sv["xbc"], sv["u"], sv["dtrow"], p["bias_c"], p["alog_c"], p["bias_r"], p["alog_r"],
                  sv["hs_b"], dy, (r_f[0], r_f[1], r_f[2]), nlc, True, 1)
    cv = [conv_bwd(t + "conv_" + nm, sv["u"], r_b[k], p["conv_w"], p["conv_b"], nlat, c0)
          for k, (nm, c0) in enumerate((("x", 0), ("b", 1024), ("c", 1280)))]
    dxbc = [r[0] for r in cv]
    g["conv_w"] = jnp.concatenate([r[1] for r in cv], axis=1)
    g["conv_b"] = jnp.concatenate([r[2] for r in cv], axis=1)
    drow = jnp.concatenate([r_f[4][0] + r_f[4][1], r_b[4][0] + r_b[4][1]], axis=0)
    drow_t = jnp.pad(jnp.transpose(drow), ((0, 0), (DT_LANE, 128 - DT_LANE - 32)))
    dmisc = misc_combine(t + "misc", dkr, r_f[3], r_b[3], drow_t)
    g["bias_c"] = r_f[5] + r_b[5]
    g["alog_c"] = r_f[6] + r_b[6]
    g["bias_r"] = jnp.concatenate([r_f[7], r_b[7]], axis=0)
    g["alog_r"] = jnp.concatenate([r_f[8], r_b[8]], axis=0)
    du = jnp.concatenate([dgates, dz, dq, *dxbc, dk, dv, dckv, dmisc, jnp.zeros((n, 128), BF16), dcq], axis=1)
    g["w_in"] = mm_tn(sv["h"], du, t + "wg_in")
    dh = mm(du, p["w_in"], F32, t + "dg_in", trans_b=True)
    g["mod"] = (dgt1, dsh2, dsc2, dgt2)
    return dx1, dh, g


def local_step(x, c, ctx, target, c_ctx, W, nlat):
    xin = jnp.concatenate([x, ctx], axis=0)
    n = xin.shape[0]
    nt = nlat // RT
    tabs = rope_tables(nlat)
    c8 = jnp.zeros((8, D), F32).at[0].set(c[0]).at[1].set(c_ctx)
    mods, silus = [], []
    for i in range(DEPTH):
        m8, s8 = mod_fwd("l%d_mod" % i, c8, W[i]["w_mod"], W[i]["b_mod"])
        mods.append(m8[0:2].reshape(2, 1, 6 * D))
        silus.append(s8)
    saved = []
    _, h = resid_mod_fwd("l0_norm1", xin, None, None, 0, mods[0], 0, 1, W[0]["norm1_g"], nt)
    xcur = xin
    for i in range(DEPTH):
        x1, f, sv = layer_fwd(i, xcur, h, mods[i], W[i], tabs, nlat)
        saved.append(sv)
        if i + 1 < DEPTH:
            xcur, h = resid_mod_fwd("l%d_res2" % i, x1, f, mods[i], 5, mods[i + 1], 0, 1, W[i + 1]["norm1_g"], nt)
    loss_v, dx2, df, dgt2 = resid_loss("loss", x1, f, mods[DEPTH - 1], 5, target, nt)
    grads = [None] * DEPTH
    for i in reversed(range(DEPTH)):
        dx1, dh, g = layer_bwd(i, dx2, df, dgt2, saved[i], mods[i], W[i], tabs, nlat)
        if i > 0:
            sv = saved[i]
            dx2, df, dgt2, dsh1, dsc1, g["norm1_g"] = resid_mod_bwd(
                "l%db_res2" % (i - 1), sv["xin"], dx1, dh, saved[i - 1]["f"], mods[i - 1], 5, mods[i], 0, 1,
                W[i]["norm1_g"], nt)
        else:
            dxin, _, _, dsh1, dsc1, g["norm1_g"] = resid_mod_bwd("l0b_norm1", saved[0]["xin"], dx1, dh, None, None, 0,
                                                                  mods[0], 0, 1, W[0]["norm1_g"], nt)
        dgt1, dsh2, dsc2, dgt2_i = g.pop("mod")
        dmod = jnp.concatenate([dsh1, dsc1, dgt1, dsh2, dsc2, dgt2_i], axis=2).reshape(2, 6 * D)
        dmod8 = jnp.zeros((8, 6 * D), F32).at[0:2].set(dmod)
        g["w_mod"] = mm_tn(silus[i], dmod8, "l%db_wg_mod" % i)
        dsilu = mm(dmod8, W[i]["w_mod"], F32, "l%db_dg_mod" % i, trans_b=True)
        dc8, g["b_mod"] = mod_small_bwd("l%db_mod_small" % i, c8, dsilu, dmod8)
        g["c8"] = dc8
        grads[i] = g
    del n
    return loss_v[0, 0], dxin, grads


def _full_from_chips(k, a):
    if k in COL_SHARDED:
        return a.transpose(1, 2, 0, 3).reshape(2, a.shape[2], 4 * a.shape[3])
    return a.transpose(1, 0, 2, 3).reshape(2, 4 * a.shape[2], a.shape[3])


def _chips_from_full(k, a):
    if k in COL_SHARDED:
        return a.reshape(a.shape[0], 4, a.shape[1] // 4).transpose(1, 0, 2)
    return a.reshape(4, a.shape[0] // 4, a.shape[1])


def _small_sizes():
    return dict(c_ctx=1024, b_mod=2 * 6144, norm1_g=2048, norm2_g=2048, ssm_conv_w=2 * 5 * 1536, ssm_conv_b=2 * 1536,
                ssm_dt_bias=64, ssm_a_log=64, ssm_d=32, ssm_norm_g=2048, swa_q_norm_g=256, swa_k_norm_g=256, swa_sink=16,
                mla_q_lat_g=768, mla_kv_lat_g=512, mla_q_norm_g=384, mla_k_norm_g=384)


def _pack_small(d):
    parts = []
    for k in SMALL:
        v = d[k].astype(F32).reshape(-1)
        parts.append(jnp.pad(v, (0, (-v.shape[0]) % 1024)))
    return jnp.concatenate(parts).reshape(-1, 128)


def _unpack_small(buf, shapes):
    flat = buf.reshape(-1)
    out = {}
    o = 0
    for k in SMALL:
        sz = _small_sizes()[k]
        out[k] = flat[o:o + sz].reshape(shapes[k])
        o += sz + (-sz) % 1024
    return out


def big_grads(grads):
    gfull = {k: [] for k in BIG}
    for i in range(DEPTH):
        g = grads[i]
        gfull["w_mod"].append(g["w_mod"])
        gfull["w_in"].append(unpack_w_in(g["w_in"]))
        gfull["w_mla_uq"].append(unpack_uq(g["w_uq"]))
        gfull["w_mla_ukv"].append(unpack_ukv(g["w_ukv"]))
        for k in ("w_p_ssm", "w_p_swa", "w_p_mla", "w_out", "w_ffn_in", "w_ffn_out"):
            gfull[k].append(g[k])
    return gfull


def small_grads(grads):
    gs = {}
    gs["c_ctx"] = sum(grads[i]["c8"][1] for i in range(DEPTH))
    st = lambda f: jnp.stack([f(grads[i]) for i in range(DEPTH)])
    gs["b_mod"] = st(lambda g: g["b_mod"][0])
    gs["norm1_g"] = st(lambda g: g["norm1_g"][0])
    gs["norm2_g"] = st(lambda g: g["norm2_g"][0])
    gs["ssm_conv_w"] = st(lambda g: g["conv_w"])
    gs["ssm_conv_b"] = st(lambda g: g["conv_b"][0])
    gs["ssm_dt_bias"] = st(lambda g: (g["bias_c"][0, DT_LANE:DT_LANE + 32] + g["bias_r"][:, 0]).reshape(2, 16))
    gs["ssm_a_log"] = st(lambda g: (g["alog_c"][0, DT_LANE:DT_LANE + 32] + g["alog_r"][:, 0]).reshape(2, 16))
    gs["ssm_d"] = st(lambda g: g["d_exp"].reshape(16, 64).sum(axis=1))
    gs["ssm_norm_g"] = st(lambda g: g["ssm_norm_g"][0])
    gs["swa_q_norm_g"] = st(lambda g: g["swa_q_g"][0])
    gs["swa_k_norm_g"] = st(lambda g: g["swa_k_g"][0])
    gs["swa_sink"] = st(lambda g: g["sink"][:, 0, 0])
    gs["mla_q_lat_g"] = st(lambda g: g["q_lat_g"][0])
    gs["mla_kv_lat_g"] = st(lambda g: g["kv_lat_g"][0])
    gs["mla_q_norm_g"] = st(lambda g: g["mla_q_g"][0, :192])
    gs["mla_k_norm_g"] = st(lambda g: g["mla_k_g"][0, :192])
    return gs


def layer_params(i, full, conv_full, sm, nlat):
    p = {}
    p["w_mod"] = full["w_mod"][i]
    p["w_in"] = pack_w_in(full["w_in"][i])
    p["w_uq"] = pack_uq(full["w_mla_uq"][i])
    p["w_ukv"] = pack_ukv(full["w_mla_ukv"][i])
    for k in ("w_p_ssm", "w_p_swa", "w_p_mla", "w_out", "w_ffn_in", "w_ffn_out"):
        p[k] = full[k][i]
    p["b_mod"] = sm["b_mod"][i][None]
    p["norm1_g"] = sm["norm1_g"][i][None]
    p["norm2_g"] = sm["norm2_g"][i][None]
    p["conv_w"] = conv_full[i]
    p["conv_b"] = sm["ssm_conv_b"][i][None]
    bias = sm["ssm_dt_bias"][i].reshape(32)
    alog = sm["ssm_a_log"][i].reshape(32)
    p["bias_c"] = _lanes(bias, DT_LANE)
    p["alog_c"] = _lanes(alog, DT_LANE)
    p["bias_r"] = bias[:, None]
    p["alog_r"] = alog[:, None]
    p["d_exp"] = jnp.repeat(sm["ssm_d"][i], 64)[None]
    p["ssm_norm_g"] = sm["ssm_norm_g"][i][None]
    p["swa_q_g"] = sm["swa_q_norm_g"][i][None]
    p["swa_k_g"] = sm["swa_k_norm_g"][i][None]
    p["sink"] = jnp.broadcast_to(sm["swa_sink"][i][:, None, None], (SWA_HQ, 1, 128))
    p["q_lat_g"] = sm["mla_q_lat_g"][i][None]
    p["kv_lat_g"] = sm["mla_kv_lat_g"][i][None]
    p["mla_q_g"] = _lanes(sm["mla_q_norm_g"][i], 0, 256)
    p["mla_k_g"] = _lanes(sm["mla_k_norm_g"][i], 0, 256)
    p["swa_cfg"] = dict(w=128, vw=128, hq=SWA_HQ, grp=SWA_HQ // SWA_HKV, vcol0=C_V // 128, scale=SWA_DH ** -0.5,
                        tq=256, tk=256, band=True)
    return p


def kernel(x, c, ctx, c_ctx, w_mod, b_mod, norm1_g, norm2_g, w_in, ssm_conv_w, ssm_conv_b, ssm_dt_bias, ssm_a_log, ssm_d, ssm_norm_g, swa_q_norm_g, swa_k_norm_g, swa_sink, mla_q_lat_g, mla_kv_lat_g, w_mla_uq, w_mla_ukv, mla_q_norm_g, mla_k_norm_g, w_p_ssm, w_p_swa, w_p_mla, w_out, w_ffn_in, w_ffn_out, loss_target, m_c_ctx, m_w_mod, m_b_mod, m_norm1_g, m_norm2_g, m_w_in, m_ssm_conv_w, m_ssm_conv_b, m_ssm_dt_bias, m_ssm_a_log, m_ssm_d, m_ssm_norm_g, m_swa_q_norm_g, m_swa_k_norm_g, m_swa_sink, m_mla_q_lat_g, m_mla_kv_lat_g, m_w_mla_uq, m_w_mla_ukv, m_mla_q_norm_g, m_mla_k_norm_g, m_w_p_ssm, m_w_p_swa, m_w_p_mla, m_w_out, m_w_ffn_in, m_w_ffn_out, v_c_ctx, v_w_mod, v_b_mod, v_norm1_g, v_norm2_g, v_w_in, v_ssm_conv_w, v_ssm_conv_b, v_ssm_dt_bias, v_ssm_a_log, v_ssm_d, v_ssm_norm_g, v_swa_q_norm_g, v_swa_k_norm_g, v_swa_sink, v_mla_q_lat_g, v_mla_kv_lat_g, v_w_mla_uq, v_w_mla_ukv, v_mla_q_norm_g, v_mla_k_norm_g, v_w_p_ssm, v_w_p_swa, v_w_p_mla, v_w_out, v_w_ffn_in, v_w_ffn_out):
    loc = dict(locals())
    w = {k: loc[k] for k in WEIGHTS}
    m = {k: loc["m_" + k] for k in WEIGHTS}
    v = {k: loc["v_" + k] for k in WEIGHTS}
    nlat = x.shape[1]

    sh2 = {k: (w[k].shape[0] * w[k].shape[1], w[k].shape[2]) for k in BIG}
    conv_sh = jnp.pad(ssm_conv_w.reshape(10, 384), ((0, 6), (0, 0)))
    gathered = gather_chips("gather_weights", [w[k].astype(BF16).reshape(sh2[k]) for k in BIG] + [conv_sh])
    full = {k: _full_from_chips(k, g.reshape((4,) + w[k].shape)) for k, g in zip(BIG, gathered)}
    conv_full = gathered[-1][:, :10].reshape(4, 2, 5, 384).transpose(1, 2, 0, 3).reshape(2, 5, 1536)

    W = [layer_params(i, full, conv_full, w, nlat) for i in range(DEPTH)]

    loss_loc, dx, grads = local_step(x[0], c, ctx[0], loss_target[0], c_ctx, W, nlat)

    gfull = big_grads(grads)
    by_chip = {k: jnp.stack([_chips_from_full(k, a) for a in gfull[k]], axis=1) for k in BIG}
    send = [by_chip[k].astype(BF16).reshape((4,) + sh2[k]) for k in BIG]
    own, got = pair_split("pair_split", send)
    pair = [add_cast("pair_sum_" + k, o, g, BF16) for k, o, g in zip(BIG, own, got)]
    recv = scatter_chips("scatter_grads", pair)
    mine = [sum_blocks("sum_chips_" + k, r) for k, r in zip(BIG, recv)]
    gbig = {k: g.reshape(w[k].shape) for k, g in zip(BIG, pair_join("join_cores", mine))}

    gs = small_grads(grads)
    small_all = gather_all("gather_small", _pack_small(gs))
    small_sum = sum_blocks("sum_small", small_all)
    full_shapes = {k: (w[k].shape if k != "ssm_conv_w" else (2, 5, 1536)) for k in SMALL}
    gsmall = _unpack_small(small_sum, full_shapes)
    chip = 2 * lax.axis_index("x") + lax.axis_index("y")
    gsmall["ssm_conv_w"] = lax.dynamic_slice_in_dim(gsmall["ssm_conv_w"], chip * 384, 384, axis=2)

    grad = {**gbig, **gsmall}
    delta, new_m, new_v = {}, {}, {}
    sm = {k: _pack_small_local(d) for k, d in (("w", w), ("g", grad), ("m", m), ("v", v))}
    r = adamw("adamw_small", sm["w"], sm["g"], sm["m"], sm["v"])
    shapes = {k: w[k].shape for k in SMALL}
    for dst, buf in zip((delta, new_m, new_v), r):
        dst.update(_unpack_small_local(buf, shapes))
    for k in BIG:
        sh = w[k].shape
        r = adamw("adamw_" + k, *[a[k].reshape(sh[0] * sh[1], sh[2]) for a in (w, grad, m, v)])
        for dst, buf in zip((delta, new_m, new_v), r):
            dst[k] = buf.reshape(sh)

    loss = lax.psum(loss_loc, ("x", "y", "c"))
    return (loss, dx[None, :nlat], *[grad[k] for k in WEIGHTS], *[delta[k] for k in WEIGHTS],
            *[new_m[k] for k in WEIGHTS], *[new_v[k] for k in WEIGHTS])


def _pack_small_local(d):
    parts = []
    for k in SMALL:
        a = d[k].astype(F32).reshape(-1)
        parts.append(jnp.pad(a, (0, (-a.shape[0]) % 1024)))
    return jnp.concatenate(parts).reshape(-1, 128)


def _unpack_small_local(buf, shapes):
    flat = buf.reshape(-1)
    out = {}
    o = 0
    for k in SMALL:
        sz = math.prod(shapes[k])
        out[k] = flat[o:o + sz].reshape(shapes[k])
        o += sz + (-sz) % 1024
    return out
```

```python
import functools
import math

import jax
import jax.numpy as jnp
from jax import lax
from jax.experimental import pallas as pl
from jax.experimental.pallas import tpu as pltpu

F32 = jnp.float32
BF16 = jnp.bfloat16
MESH = pl.DeviceIdType.MESH

D = 1024
NCTX = 256
EPS = 1e-6
ROPE_BASE = 10000.0
GRID_W = 64
DEPTH = 2
Q = 128
N_HEADS_SSM = 16
SWA_HQ, SWA_HKV, SWA_DH, SWA_WIN = 8, 2, 128, 128
MLA_H, MLA_NOPE, MLA_ROPE, MLA_V = 8, 128, 64, 128
MLA_QRANK, MLA_KVRANK = 384, 256
FFN = 2816
RT = 256
VMEM_LIMIT = 56 << 20
NEG = -1e30
LOG2E = 1.4426950408889634

C_G1, C_G2, C_G3, C_Z, C_Q, C_XS, C_B, C_C, C_K, C_V, C_CKV, C_MISC, C_PAD, C_CQ = (
    0, 1024, 2048, 3072, 4096, 5120, 6144, 6400, 6656, 6912, 7168, 7424, 7552, 7680)
UW = 8064
DT_LANE = 64

ADAM_LR, ADAM_B1, ADAM_B2, ADAM_EPS, ADAM_WD, ADAM_STEP = 0.001, 0.9, 0.999, 1e-08, 0.01, 10


def _cp(sem):
    return pltpu.CompilerParams(dimension_semantics=sem, vmem_limit_bytes=VMEM_LIMIT)


def _pick(n, cands):
    for c in cands:
        if n % c == 0:
            return c
    return n


_TN = (1536, 1408, 1152, 1024, 896, 768, 512, 384, 256, 128)


def mm(a, b, out_dtype, name, trans_b=False):
    m, k = a.shape
    n = b.shape[0] if trans_b else b.shape[1]
    tm = _pick(m, (1408, 768, 512, 256, 128, 8))
    tn = _pick(n, _TN)
    tk = k if k <= 2048 else _pick(k, (1408, 1152, 1024, 896, 768, 512))
    nk = k // tk
    b_spec = (pl.BlockSpec((tn, tk), lambda i, j, kk: (j, kk)) if trans_b
              else pl.BlockSpec((tk, tn), lambda i, j, kk: (kk, j)))

    def body(a_ref, b_ref, o_ref, *acc):
        p = _d(a_ref[...], b_ref[...], ((1,), (1 if trans_b else 0,)))
        if nk == 1:
            o_ref[...] = p.astype(out_dtype)
        else:
            kk = pl.program_id(2)

            @pl.when(kk == 0)
            def _():
                acc[0][...] = p

            @pl.when(kk > 0)
            def _():
                acc[0][...] += p

            @pl.when(kk == nk - 1)
            def _():
                o_ref[...] = acc[0][...].astype(out_dtype)

    return pl.pallas_call(
        body, out_shape=jax.ShapeDtypeStruct((m, n), out_dtype), grid=(m // tm, n // tn, nk),
        in_specs=[pl.BlockSpec((tm, tk), lambda i, j, kk: (i, kk)), b_spec],
        out_specs=pl.BlockSpec((tm, tn), lambda i, j, kk: (i, j)),
        scratch_shapes=[] if nk == 1 else [pltpu.VMEM((tm, tn), F32)],
        name=name, compiler_params=_cp(("parallel", "parallel", "arbitrary")))(a, b)


def mm_tn(a, b, name, out_dtype=BF16):
    t, ka = a.shape
    _, nb = b.shape
    ta = _pick(ka, (1024, 1408, 768, 512, 384, 256, 128))
    tb = _pick(nb, _TN)
    tt = _pick(t, (768, 512, 256, 128, 8))
    nt = t // tt

    def body(a_ref, b_ref, o_ref, acc):
        p = _d(a_ref[...], b_ref[...], ((0,), (0,)))
        s = pl.program_id(2)

        @pl.when(s == 0)
        def _():
            acc[...] = p

        @pl.when(s > 0)
        def _():
            acc[...] += p

        @pl.when(s == nt - 1)
        def _():
            o_ref[...] = acc[...].astype(out_dtype)

    return pl.pallas_call(
        body, out_shape=jax.ShapeDtypeStruct((ka, nb), out_dtype), grid=(ka // ta, nb // tb, nt),
        in_specs=[pl.BlockSpec((tt, ta), lambda i, j, s: (s, i)), pl.BlockSpec((tt, tb), lambda i, j, s: (s, j))],
        out_specs=pl.BlockSpec((ta, tb), lambda i, j, s: (i, j)), scratch_shapes=[pltpu.VMEM((ta, tb), F32)],
        name=name, compiler_params=_cp(("parallel", "parallel", "arbitrary")))(a, b)


def _rms(x, g, n=None):
    n = x.shape[-1] if n is None else n
    r = lax.rsqrt(jnp.sum(x * x, axis=-1, keepdims=True) * (1.0 / n) + EPS)
    return x * r * g


def _silu(x):
    return x * jax.nn.sigmoid(x)


def _modulate(x, g, sc, sh):
    return _rms(x, g) * (1.0 + sc) + sh


def _swap(x, s):
    ax = x.ndim - 1
    w = x.shape[ax]
    lane = lax.broadcasted_iota(jnp.int32, x.shape, ax)
    lo = (lane & s) == 0
    return jnp.where(lo, pltpu.roll(x, w - s, ax), pltpu.roll(x, s, ax))


@functools.partial(jax.custom_vjp, nondiff_argnums=(3,))
def _rope(x, cos, sin, s):
    return x * cos + _swap(x, s) * sin


def _rope_fwd(x, cos, sin, s):
    return _rope(x, cos, sin, s), (cos, sin)


def _rope_bwd(s, res, g):
    cos, sin = res
    return g * cos - _swap(g, s) * sin, jnp.zeros_like(cos), jnp.zeros_like(sin)


_rope.defvjp(_rope_fwd, _rope_bwd)


@jax.custom_vjp
def _softplus(x):
    return jnp.maximum(x, 0.0) + jnp.log(1.0 + jnp.exp(-jnp.abs(x)))


def _softplus_fwd(x):
    return _softplus(x), x


def _softplus_bwd(x, g):
    return (g * jax.nn.sigmoid(x),)


_softplus.defvjp(_softplus_fwd, _softplus_bwd)


def _d(a, b, dims):
    return lax.dot_general(a.astype(BF16), b.astype(BF16), (dims, ((), ())), preferred_element_type=F32)


@jax.custom_vjp
def bdot(a, b):
    return _d(a, b, ((1,), (0,)))


bdot.defvjp(lambda a, b: (bdot(a, b), (a, b)),
            lambda r, g: (_d(g, r[1], ((1,), (1,))), _d(r[0], g, ((0,), (0,)))))


@jax.custom_vjp
def bdot_nt(a, b):
    return _d(a, b, ((1,), (1,)))


bdot_nt.defvjp(lambda a, b: (bdot_nt(a, b), (a, b)),
               lambda r, g: (_d(g, r[1], ((1,), (0,))), _d(g, r[0], ((0,), (0,)))))


@jax.custom_vjp
def bdot_tn(a, b):
    return _d(a, b, ((0,), (0,)))


bdot_tn.defvjp(lambda a, b: (bdot_tn(a, b), (a, b)),
               lambda r, g: (_d(r[1], g, ((1,), (1,))), _d(r[0], g, ((1,), (0,)))))


def _tri(rev):
    i = lax.broadcasted_iota(jnp.int32, (Q, Q), 0)
    j = lax.broadcasted_iota(jnp.int32, (Q, Q), 1)
    return (i <= j) if rev else (i >= j)


def _split3(a):
    hi = a.astype(BF16)
    r = a - hi.astype(F32)
    mid = r.astype(BF16)
    lo = (r - mid.astype(F32)).astype(BF16)
    return hi, mid, lo


def _cum_cols_impl(a, rev):
    t = _tri(rev).astype(BF16)
    return sum(jnp.dot(t, p, preferred_element_type=F32) for p in _split3(a))


def _cum_rows_impl(a, rev):
    t = _tri(not rev).astype(BF16)
    return sum(jnp.dot(p, t, preferred_element_type=F32) for p in _split3(a))


@functools.partial(jax.custom_vjp, nondiff_argnums=(1,))
def cum_cols(a, rev):
    return _cum_cols_impl(a, rev)


cum_cols.defvjp(lambda a, rev: (_cum_cols_impl(a, rev), None), lambda rev, _, g: (_cum_cols_impl(g, not rev),))


@functools.partial(jax.custom_vjp, nondiff_argnums=(1,))
def cum_rows(a, rev):
    return _cum_rows_impl(a, rev)


cum_rows.defvjp(lambda a, rev: (_cum_rows_impl(a, rev), None), lambda rev, _, g: (_cum_rows_impl(g, not rev),))


def _rs(w, cb=0):
    return pl.BlockSpec((RT, w), lambda i: (i, cb))


def _ps(shape):
    nd = len(shape)
    return pl.BlockSpec(shape, lambda i: (0,) * nd)


def _gs(w, cb, nlat):
    return pl.BlockSpec((1, 1, w), lambda i: (i // nlat, 0, cb))


def _rowcall(name, body, n, ins, outs, scratch=()):
    return pl.pallas_call(
        body, out_shape=[o[0] for o in outs], grid=(n // RT,), in_specs=[s for _, s in ins],
        out_specs=[s for _, s in outs], scratch_shapes=list(scratch), name=name,
        compiler_params=_cp(("arbitrary",)))(*[a for a, _ in ins])


def _acc(ref, val, first):
    @pl.when(first)
    def _():
        ref[...] = val

    @pl.when(jnp.logical_not(first))
    def _():
        ref[...] += val


def _sd(shape, dt):
    return jax.ShapeDtypeStruct(shape, dt)


def resid_mod_fwd(name, xp, o, mod_gt, gt_i, mod_n, sh_i, sc_i, norm_g, nlat):
    n = xp.shape[0]
    has_res = o is not None

    def body(*refs):
        if has_res:
            xp_ref, o_ref, gt_ref, sh_ref, sc_ref, g_ref, xn_ref, h_ref = refs
            xn = xp_ref[...] + gt_ref[0] * o_ref[...]
            xn_ref[...] = xn
        else:
            xp_ref, sh_ref, sc_ref, g_ref, h_ref = refs
            xn = xp_ref[...]
        h_ref[...] = _modulate(xn, g_ref[...], sc_ref[0], sh_ref[0]).astype(BF16)

    ins = [(xp, _rs(D))]
    if has_res:
        ins += [(o, _rs(D)), (mod_gt, _gs(D, gt_i, nlat))]
    ins += [(mod_n, _gs(D, sh_i, nlat)), (mod_n, _gs(D, sc_i, nlat)), (norm_g, _ps((1, D)))]
    outs = ([(_sd((n, D), F32), _rs(D))] if has_res else []) + [(_sd((n, D), BF16), _rs(D))]
    r = _rowcall(name, body, n, ins, outs)
    return (r[0], r[1]) if has_res else (xp, r[0])


def resid_mod_bwd(name, xn, dxn, dh, o, mod_gt, gt_i, mod_n, sh_i, sc_i, norm_g, nlat):
    n = xn.shape[0]
    has_res = o is not None

    def body(*refs):
        i = pl.program_id(0)
        if has_res:
            (xn_ref, dxn_ref, dh_ref, o_ref, gt_ref, sh_ref, sc_ref, g_ref,
             dx_ref, do_ref, dgt_ref, dsh_ref, dsc_ref, dg_ref) = refs
        else:
            xn_ref, dxn_ref, dh_ref, sh_ref, sc_ref, g_ref, dx_ref, dsh_ref, dsc_ref, dg_ref = refs
        _, vjp = jax.vjp(_modulate, xn_ref[...], g_ref[...], sc_ref[0], sh_ref[0])
        dx, dg, dsc, dsh = vjp(dh_ref[...])
        dx = dx + dxn_ref[...]
        dx_ref[...] = dx
        gfirst = (i == 0) | (i == nlat)
        _acc(dg_ref, dg, i == 0)
        _acc(dsh_ref, dsh[None], gfirst)
        _acc(dsc_ref, dsc[None], gfirst)
        if has_res:
            do_ref[...] = (gt_ref[0] * dx).astype(BF16)
            _acc(dgt_ref, jnp.sum(dx * o_ref[...], axis=0, keepdims=True)[None], gfirst)

    ins = [(xn, _rs(D)), (dxn, _rs(D)), (dh, _rs(D))]
    if has_res:
        ins += [(o, _rs(D)), (mod_gt, _gs(D, gt_i, nlat))]
    ins += [(mod_n, _gs(D, sh_i, nlat)), (mod_n, _gs(D, sc_i, nlat)), (norm_g, _ps((1, D)))]
    gacc = (_sd((2, 1, D), F32), _gs(D, 0, nlat))
    outs = [(_sd((n, D), F32), _rs(D))]
    if has_res:
        outs += [(_sd((n, D), BF16), _rs(D)), gacc]
    outs += [gacc, gacc, (_sd((1, D), F32), _ps((1, D)))]
    r = _rowcall(name, body, n, ins, outs)
    if has_res:
        return r
    return r[0], None, None, r[1], r[2], r[3]


def resid_loss(name, xp, o, mod_gt, gt_i, target, nlat):
    n = xp.shape[0]

    def body(xp_ref, o_ref, gt_ref, t_ref, loss_ref, dx_ref, do_ref, dgt_ref):
        i = pl.program_id(0)
        gt = gt_ref[0]

        @pl.when(i < nlat)
        def _():
            err = xp_ref[...] + gt * o_ref[...] - t_ref[...]
            dx = err * (1.0 / D)
            dx_ref[...] = dx
            do_ref[...] = (gt * dx).astype(BF16)
            _acc(loss_ref, jnp.full((1, 128), 0.5 / D, F32) * jnp.sum(err * err), i == 0)
            _acc(dgt_ref, jnp.sum(dx * o_ref[...], axis=0, keepdims=True)[None], i == 0)

        @pl.when(i >= nlat)
        def _():
            dx_ref[...] = jnp.zeros((RT, D), F32)
            do_ref[...] = jnp.zeros((RT, D), BF16)
            dgt_ref[...] = jnp.zeros((1, 1, D), F32)

    tgt_spec = pl.BlockSpec((RT, D), lambda i: (jnp.minimum(i, nlat - 1), 0))
    ins = [(xp, _rs(D)), (o, _rs(D)), (mod_gt, _gs(D, gt_i, nlat)), (target, tgt_spec)]
    outs = [(_sd((1, 128), F32), _ps((1, 128))), (_sd((n, D), F32), _rs(D)), (_sd((n, D), BF16), _rs(D)),
            (_sd((2, 1, D), F32), _gs(D, 0, nlat))]
    return _rowcall(name, body, n, ins, outs)


def mod_fwd(name, c8, w_mod, b_mod):
    tn = 1536

    def body(c_ref, w_ref, b_ref, o_ref, s_ref):
        s = _silu(c_ref[...]).astype(BF16)
        s_ref[...] = s
        o_ref[...] = jnp.dot(s, w_ref[...], preferred_element_type=F32) + b_ref[...]

    return pl.pallas_call(
        body, out_shape=[_sd((8, 6 * D), F32), _sd((8, D), BF16)], grid=(6 * D // tn,),
        in_specs=[pl.BlockSpec((8, D), lambda j: (0, 0)), pl.BlockSpec((D, tn), lambda j: (0, j)),
                  pl.BlockSpec((1, tn), lambda j: (0, j))],
        out_specs=[pl.BlockSpec((8, tn), lambda j: (0, j)), pl.BlockSpec((8, D), lambda j: (0, 0))],
        name=name, compiler_params=_cp(("arbitrary",)))(c8, w_mod, b_mod)


def mod_small_bwd(name, c8, dsilu, dmod8):
    def body(c_ref, ds_ref, dm_ref, dc_ref, db_ref):
        _, vjp = jax.vjp(_silu, c_ref[...])
        dc_ref[...] = vjp(ds_ref[...])[0]
        db_ref[...] = jnp.sum(dm_ref[...], axis=0, keepdims=True)

    return pl.pallas_call(
        body, out_shape=[_sd((8, D), F32), _sd((1, 6 * D), F32)], grid=(1,),
        in_specs=[pl.BlockSpec((8, D), lambda j: (0, 0)), pl.BlockSpec((8, D), lambda j: (0, 0)),
                  pl.BlockSpec((8, 6 * D), lambda j: (0, 0))],
        out_specs=[pl.BlockSpec((8, D), lambda j: (0, 0)), pl.BlockSpec((1, 6 * D), lambda j: (0, 0))],
        name=name, compiler_params=_cp(("arbitrary",)))(c8, dsilu, dmod8)


def _conv_taps(x, nlat):
    n = x.shape[0]
    r = lax.broadcasted_iota(jnp.int32, x.shape, 0)
    lo = jnp.where(r < nlat, 0, nlat)
    hi = jnp.where(r < nlat, nlat, n)
    taps = []
    for o in (-2, -1, 0, 1, 2):
        xs = x if o == 0 else pltpu.roll(x, (-o) % n, 0)
        t = r + o
        taps.append(jnp.where((t >= lo) & (t < hi), xs, 0.0))
    return taps


def conv_fwd(name, u, w, b, nlat_rows):
    n = u.shape[0]

    def body(x_ref, w_ref, b_ref, o_ref):
        taps = _conv_taps(x_ref[...], nlat_rows)
        wv = w_ref[...]
        pre = b_ref[...] + sum(taps[k] * wv[k:k + 1, :] for k in range(5))
        o_ref[...] = _silu(pre)

    return pl.pallas_call(
        body, out_shape=_sd((n, 1536), F32), grid=(12,),
        in_specs=[pl.BlockSpec((n, 128), lambda j: (0, C_XS // 128 + j)), pl.BlockSpec((5, 128), lambda j: (0, j)),
                  pl.BlockSpec((1, 128), lambda j: (0, j))],
        out_specs=pl.BlockSpec((n, 128), lambda j: (0, j)),
        name=name, compiler_params=_cp(("parallel",)))(u, w, b)


def conv_bwd(name, u, dact, w, b, nlat_rows, chan0):
    n, nch = dact.shape
    t0 = chan0 // 128

    def body(x_ref, da_ref, w_ref, b_ref, dx_ref, dw_ref, db_ref):
        taps = _conv_taps(x_ref[...], nlat_rows)
        wv = w_ref[...]
        pre = b_ref[...] + sum(taps[k] * wv[k:k + 1, :] for k in range(5))
        s = jax.nn.sigmoid(pre)
        dpre = da_ref[...] * (s * (1.0 + pre * (1.0 - s)))
        db_ref[...] = jnp.sum(dpre, axis=0, keepdims=True)
        rows = lax.broadcasted_iota(jnp.int32, (5, 128), 0)
        dw = jnp.zeros((5, 128), F32)
        for k in range(5):
            dw = dw + jnp.where(rows == k, jnp.sum(dpre * taps[k], axis=0, keepdims=True), 0.0)
        dw_ref[...] = dw
        r = lax.broadcasted_iota(jnp.int32, dpre.shape, 0)
        lo = jnp.where(r < nlat_rows, 0, nlat_rows)
        hi = jnp.where(r < nlat_rows, nlat_rows, n)
        dx = jnp.zeros_like(dpre)
        for k in range(5):
            o = k - 2
            ds = dpre if o == 0 else pltpu.roll(dpre, o % n, 0)
            t = r - o
            dx = dx + jnp.where((t >= lo) & (t < hi), ds, 0.0) * wv[k:k + 1, :]
        dx_ref[...] = dx.astype(BF16)

    return pl.pallas_call(
        body, out_shape=[_sd((n, nch), BF16), _sd((5, nch), F32), _sd((1, nch), F32)], grid=(nch // 128,),
        in_specs=[pl.BlockSpec((n, 128), lambda j: (0, C_XS // 128 + t0 + j)), pl.BlockSpec((n, 128), lambda j: (0, j)),
                  pl.BlockSpec((5, 128), lambda j: (0, t0 + j)), pl.BlockSpec((1, 128), lambda j: (0, t0 + j))],
        out_specs=[pl.BlockSpec((n, 128), lambda j: (0, j)), pl.BlockSpec((5, 128), lambda j: (0, j)),
                   pl.BlockSpec((1, 128), lambda j: (0, j))],
        name=name, compiler_params=_cp(("parallel",)))(u, dact, w, b)


def _ssd_chunk(rev, dirn, g, x4, bm, cm, misc, dtrow, bias_c, alog_c, bias_r, alog_r, h4):
    dt_c = _softplus(misc + bias_c)
    a_c = dt_c * (-jnp.exp(alog_c))
    dt_r = _softplus(dtrow + bias_r)
    a_r = dt_r * (-jnp.exp(alog_r))
    cs_c = cum_cols(a_c, rev)
    cs_r = cum_rows(a_r, rev)
    tot_c = jnp.sum(a_c, axis=0, keepdims=True)
    cb = bdot_nt(cm, bm)
    tri = _tri(rev)
    lane = lax.broadcasted_iota(jnp.int32, (1, 128), 1)
    row16 = lax.broadcasted_iota(jnp.int32, (16, 1), 0)
    prow = lax.broadcasted_iota(jnp.int32, (128, 1), 0)
    ys, hs = [], []
    for p in range(4):
        ydiag = 0.0
        wst = 0.0
        eoff = 0.0
        hscale = 0.0
        for e in range(2):
            hg = 8 * g + 2 * p + e
            oh_c = (lane == DT_LANE + 16 * dirn + hg).astype(F32)
            dt_h = jnp.sum(dt_c * oh_c, axis=1, keepdims=True)
            cs_h = jnp.sum(cs_c * oh_c, axis=1, keepdims=True)
            tot_h = jnp.sum(tot_c * oh_c, axis=1, keepdims=True)
            csr_h = jnp.sum(cs_r * (row16 == hg).astype(F32), axis=0, keepdims=True)
            seg = jnp.exp(jnp.where(tri, cs_h - csr_h, -jnp.inf))
            hm = ((lane < 64) if e == 0 else (lane >= 64)).astype(F32)
            ydiag = ydiag + bdot(cb * seg, x4[p] * (dt_h * hm))
            wst = wst + (dt_h * jnp.exp(tot_h - cs_h)) * hm
            eoff = eoff + jnp.exp(cs_h) * hm
            hscale = hscale + jnp.exp(tot_h) * ((prow < 64) if e == 0 else (prow >= 64)).astype(F32)
        ys.append(ydiag + bdot_nt(cm, h4[p]) * eoff)
        hs.append(h4[p] * hscale + bdot_tn(x4[p] * wst, bm))
    return ys, hs


def _ssd_specs(nlat_chunks, rev, dirn, bwd):
    nc = nlat_chunks + 2

    def chunk(s):
        if bwd:
            s = nc - 1 - s
        return (nlat_chunks + 1 - s) if rev else (s + nlat_chunks) % nc

    def step(s):
        return (nc - 1 - s) if bwd else s

    return dict(
        x=pl.BlockSpec((Q, 512), lambda g, s: (chunk(s), g)),
        b=pl.BlockSpec((Q, 128), lambda g, s: (chunk(s), 8 + g)),
        c=pl.BlockSpec((Q, 128), lambda g, s: (chunk(s), 10 + g)),
        misc=pl.BlockSpec((Q, 128), lambda g, s: (chunk(s), C_MISC // 128)),
        dtrow=pl.BlockSpec((16, Q), lambda g, s: (dirn, chunk(s))),
        p_c=pl.BlockSpec((1, 128), lambda g, s: (0, 0)),
        p_r=pl.BlockSpec((16, 1), lambda g, s: (dirn, 0)),
        y=pl.BlockSpec((Q, 512), lambda g, s: (chunk(s), g)),
        hsave=pl.BlockSpec((1, 1, 512, 128), lambda g, s: (g, step(s), 0, 0)),
        bc_out=pl.BlockSpec((Q, 128), lambda g, s: (chunk(s), g)),
        misc_out=pl.BlockSpec((1, Q, 128), lambda g, s: (g, chunk(s), 0)),
        dtrow_out=pl.BlockSpec((1, 16, Q), lambda g, s: (g, 0, chunk(s))),
        pacc_c=pl.BlockSpec((1, 128), lambda g, s: (0, 0)),
        pacc_r=pl.BlockSpec((16, 1), lambda g, s: (0, 0)),
    )


def ssd_fwd(name, xbc, u, dtrow, bias_c, alog_c, bias_r, alog_r, nlat_chunks, rev, dirn):
    n = xbc.shape[0]
    nc = nlat_chunks + 2
    sp = _ssd_specs(nlat_chunks, rev, dirn, False)

    def body(x_ref, b_ref, c_ref, m_ref, r_ref, bc_ref, ac_ref, br_ref, ar_ref, y_ref, hs_ref, h_s):
        g = pl.program_id(0)
        s = pl.program_id(1)

        @pl.when(s == 0)
        def _():
            h_s[...] = jnp.zeros((512, 128), F32)

        hs_ref[0, 0] = h_s[...]
        x4 = [x_ref[:, 128 * p:128 * p + 128] for p in range(4)]
        h4 = [h_s[128 * p:128 * p + 128, :] for p in range(4)]
        ys, hs = _ssd_chunk(rev, dirn, g, x4, b_ref[...], c_ref[...], m_ref[...], r_ref[...],
                            bc_ref[...], ac_ref[...], br_ref[...], ar_ref[...], h4)
        for p in range(4):
            y_ref[:, 128 * p:128 * p + 128] = ys[p]
            h_s[128 * p:128 * p + 128, :] = hs[p]

    return pl.pallas_call(
        body, out_shape=[_sd((n, 1024), F32), _sd((2, nc, 512, 128), F32)], grid=(2, nc),
        in_specs=[sp["x"], sp["b"], sp["c"], sp["misc"], sp["dtrow"], sp["p_c"], sp["p_c"], sp["p_r"], sp["p_r"]],
        out_specs=[sp["y"], sp["hsave"]], scratch_shapes=[pltpu.VMEM((512, 128), F32)],
        name=name, compiler_params=_cp(("arbitrary", "arbitrary")))(
            xbc, xbc, xbc, u, dtrow, bias_c, alog_c, bias_r, alog_r)


def ssd_bwd(name, xbc, u, dtrow, bias_c, alog_c, bias_r, alog_r, hsave, dy, acc, nlat_chunks, rev, dirn):
    n = xbc.shape[0]
    sp = _ssd_specs(nlat_chunks, rev, dirn, True)

    def body(x_ref, b_ref, c_ref, m_ref, r_ref, bc_ref, ac_ref, br_ref, ar_ref, hs_ref, dy_ref, ax_ref, ab_ref, acc_ref,
             dx_ref, db_ref, dc_ref, dm_ref, dr_ref, dbc_ref, dac_ref, dbr_ref, dar_ref, dh_s):
        g = pl.program_id(0)
        s = pl.program_id(1)

        @pl.when(s == 0)
        def _():
            dh_s[...] = jnp.zeros((512, 128), F32)

        x4 = [x_ref[:, 128 * p:128 * p + 128] for p in range(4)]
        h4 = [hs_ref[0, 0, 128 * p:128 * p + 128, :] for p in range(4)]
        fn = functools.partial(_ssd_chunk, rev, dirn, g)
        _, vjp = jax.vjp(fn, x4, b_ref[...], c_ref[...], m_ref[...], r_ref[...],
                         bc_ref[...], ac_ref[...], br_ref[...], ar_ref[...], h4)
        dys = [dy_ref[:, 128 * p:128 * p + 128] for p in range(4)]
        dhs = [dh_s[128 * p:128 * p + 128, :] for p in range(4)]
        dx4, db, dc, dm, dr, dbc, dac, dbr, dar, dh4 = vjp((dys, dhs))
        for p in range(4):
            dx_ref[:, 128 * p:128 * p + 128] = dx4[p] + ax_ref[:, 128 * p:128 * p + 128]
            dh_s[128 * p:128 * p + 128, :] = dh4[p]
        db_ref[...] = db + ab_ref[...]
        dc_ref[...] = dc + acc_ref[...]
        dm_ref[0] = dm
        dr_ref[0] = dr
        first = (g == 0) & (s == 0)
        _acc(dbc_ref, dbc, first)
        _acc(dac_ref, dac, first)
        _acc(dbr_ref, dbr, first)
        _acc(dar_ref, dar, first)

    ax, ab, ac = acc
    return pl.pallas_call(
        body,
        out_shape=[_sd((n, 1024), F32), _sd((n, 256), F32), _sd((n, 256), F32), _sd((2, n, 128), F32),
                   _sd((2, 16, n), F32), _sd((1, 128), F32), _sd((1, 128), F32), _sd((16, 1), F32), _sd((16, 1), F32)],
        grid=(2, nlat_chunks + 2),
        in_specs=[sp["x"], sp["b"], sp["c"], sp["misc"], sp["dtrow"], sp["p_c"], sp["p_c"], sp["p_r"], sp["p_r"],
                  sp["hsave"], sp["y"], sp["y"], sp["bc_out"], sp["bc_out"]],
        out_specs=[sp["y"], sp["bc_out"], sp["bc_out"], sp["misc_out"], sp["dtrow_out"],
                   sp["pacc_c"], sp["pacc_c"], sp["pacc_r"], sp["pacc_r"]],
        scratch_shapes=[pltpu.VMEM((512, 128), F32)],
        name=name, compiler_params=_cp(("arbitrary", "arbitrary")))(
            xbc, xbc, xbc, u, dtrow, bias_c, alog_c, bias_r, alog_r, hsave, dy, ax, ab, ac)


def _ssd_out(yf, yb, xs, z, g, dexp):
    return _rms((yf + yb + dexp * xs) * _silu(z), g)


def ssd_out_fwd(name, yf, yb, xbc, u, g, dexp):
    n = yf.shape[0]

    def body(yf_ref, yb_ref, xs_ref, z_ref, g_ref, d_ref, o_ref):
        o_ref[...] = _ssd_out(yf_ref[...], yb_ref[...], xs_ref[...], z_ref[...], g_ref[...], d_ref[...]).astype(BF16)

    return _rowcall(name, body, n,
                    [(yf, _rs(D)), (yb, _rs(D)), (xbc, _rs(D, 0)), (u, _rs(D, C_Z // D)), (g, _ps((1, D))), (dexp, _ps((1, D)))],
                    [(_sd((n, D), BF16), _rs(D))])[0]


def ssd_out_bwd(name, yf, yb, xbc, u, g, dexp, dys):
    n = yf.shape[0]

    def body(yf_ref, yb_ref, xs_ref, z_ref, g_ref, d_ref, dys_ref, dy_ref, dxs_ref, dz_ref, dg_ref, dd_ref):
        i = pl.program_id(0)
        _, vjp = jax.vjp(_ssd_out, yf_ref[...], yb_ref[...], xs_ref[...], z_ref[...], g_ref[...], d_ref[...])
        dyf, _, dxs, dz, dg, dd = vjp(dys_ref[...])
        dy_ref[...] = dyf
        dxs_ref[...] = dxs
        dz_ref[...] = dz.astype(BF16)
        _acc(dg_ref, dg, i == 0)
        _acc(dd_ref, dd, i == 0)

    return _rowcall(name, body, n,
                    [(yf, _rs(D)), (yb, _rs(D)), (xbc, _rs(D, 0)), (u, _rs(D, C_Z // D)), (g, _ps((1, D))), (dexp, _ps((1, D))),
                     (dys, _rs(D))],
                    [(_sd((n, D), F32), _rs(D)), (_sd((n, D), F32), _rs(D)), (_sd((n, D), BF16), _rs(D)),
                     (_sd((1, D), F32), _ps((1, D))), (_sd((1, D), F32), _ps((1, D)))])


def _normrope(x, g, cos, sin, s, n=None):
    return _rope(_rms(x, g, n), cos, sin, s)


def swa_prep_fwd(name, u, gq, gk, cos, sin):
    n = u.shape[0]

    def body(q_ref, k_ref, gq_ref, gk_ref, cos_ref, sin_ref, qs_ref, ks_ref):
        cs, sn = cos_ref[...], sin_ref[...]
        for h in range(SWA_HQ):
            sl = slice(128 * h, 128 * h + 128)
            qs_ref[:, sl] = _normrope(q_ref[:, sl], gq_ref[...], cs, sn, 32).astype(BF16)
        for h in range(SWA_HKV):
            sl = slice(128 * h, 128 * h + 128)
            ks_ref[:, sl] = _normrope(k_ref[:, sl], gk_ref[...], cs, sn, 32).astype(BF16)

    return _rowcall(name, body, n,
                    [(u, _rs(1024, C_Q // 1024)), (u, _rs(256, C_K // 256)), (gq, _ps((1, 128))), (gk, _ps((1, 128))),
                     (cos, _rs(128)), (sin, _rs(128))],
                    [(_sd((n, 1024), BF16), _rs(1024)), (_sd((n, 256), BF16), _rs(256))])


def swa_prep_bwd(name, u, gq, gk, cos, sin, dqs, dks, dv):
    n = u.shape[0]

    def body(q_ref, k_ref, gq_ref, gk_ref, cos_ref, sin_ref, dqs_ref, dks_ref, dv_ref,
             dq_ref, dk_ref, dvo_ref, dgq_ref, dgk_ref):
        i = pl.program_id(0)
        cs, sn = cos_ref[...], sin_ref[...]
        fn = lambda x, g: _normrope(x, g, cs, sn, 32)
        dgq = jnp.zeros((1, 128), F32)
        dgk = jnp.zeros((1, 128), F32)
        for h in range(SWA_HQ):
            sl = slice(128 * h, 128 * h + 128)
            _, vjp = jax.vjp(fn, q_ref[:, sl], gq_ref[...])
            dx, dg = vjp(dqs_ref[:, sl])
            dq_ref[:, sl] = dx.astype(BF16)
            dgq = dgq + dg
        for h in range(SWA_HKV):
            sl = slice(128 * h, 128 * h + 128)
            _, vjp = jax.vjp(fn, k_ref[:, sl], gk_ref[...])
            dx, dg = vjp(dks_ref[:, sl])
            dk_ref[:, sl] = dx.astype(BF16)
            dgk = dgk + dg
        dvo_ref[...] = dv_ref[...].astype(BF16)
        _acc(dgq_ref, dgq, i == 0)
        _acc(dgk_ref, dgk, i == 0)

    return _rowcall(name, body, n,
                    [(u, _rs(1024, C_Q // 1024)), (u, _rs(256, C_K // 256)), (gq, _ps((1, 128))), (gk, _ps((1, 128))),
                     (cos, _rs(128)), (sin, _rs(128)), (dqs, _rs(1024)), (dks, _rs(256)), (dv, _rs(256))],
                    [(_sd((n, 1024), BF16), _rs(1024)), (_sd((n, 256), BF16), _rs(256)), (_sd((n, 256), BF16), _rs(256)),
                     (_sd((1, 128), F32), _ps((1, 128))), (_sd((1, 128), F32), _ps((1, 128)))])


def lat_norm_fwd(name, u, g_kv, g_q):
    n = u.shape[0]

    def body(ckv_ref, cq_ref, gkv_ref, gq_ref, okv_ref, oq_ref):
        okv_ref[...] = _rms(ckv_ref[...], gkv_ref[...]).astype(BF16)
        oq_ref[...] = _rms(cq_ref[...], gq_ref[...]).astype(BF16)

    return _rowcall(name, body, n,
                    [(u, _rs(256, C_CKV // 256)), (u, _rs(384, C_CQ // 384)), (g_kv, _ps((1, 256))), (g_q, _ps((1, 384)))],
                    [(_sd((n, 256), BF16), _rs(256)), (_sd((n, 384), BF16), _rs(384))])


def lat_norm_bwd(name, u, g_kv, g_q, dkvn, dqn):
    n = u.shape[0]

    def body(ckv_ref, cq_ref, gkv_ref, gq_ref, dkvn_ref, dqn_ref, dckv_ref, dcq_ref, dgkv_ref, dgq_ref):
        i = pl.program_id(0)
        _, vjp = jax.vjp(_rms, ckv_ref[...], gkv_ref[...])
        dx, dg = vjp(dkvn_ref[...])
        dckv_ref[...] = dx.astype(BF16)
        _acc(dgkv_ref, dg, i == 0)
        _, vjp = jax.vjp(_rms, cq_ref[...], gq_ref[...])
        dx, dg = vjp(dqn_ref[...])
        dcq_ref[...] = dx.astype(BF16)
        _acc(dgq_ref, dg, i == 0)

    return _rowcall(name, body, n,
                    [(u, _rs(256, C_CKV // 256)), (u, _rs(384, C_CQ // 384)), (g_kv, _ps((1, 256))), (g_q, _ps((1, 384))),
                     (dkvn, _rs(256)), (dqn, _rs(384))],
                    [(_sd((n, 256), BF16), _rs(256)), (_sd((n, 384), BF16), _rs(384)),
                     (_sd((1, 256), F32), _ps((1, 256))), (_sd((1, 384), F32), _ps((1, 384)))])


def _lane_lt64(x):
    return (lax.broadcasted_iota(jnp.int32, (1, 128), 1) < 64).astype(F32) * x


def _mla_krope(misc, g, cos, sin):
    return _normrope(_lane_lt64(misc), g, cos, sin, 16, MLA_ROPE)


def mla_prep_fwd(name, kv, qp, u, qg, kg, cos, sin):
    n = kv.shape[0]

    def body(kv_ref, v_ref, q_ref, m_ref, qg_ref, kg_ref, cos_ref, sin_ref, km_ref, qm_ref, vm_ref):
        cs, sn = cos_ref[...], sin_ref[...]
        vm_ref[...] = v_ref[...].astype(BF16)
        kr = _mla_krope(m_ref[...], kg_ref[:, 128:256], cs, sn).astype(BF16)
        for h in range(MLA_H):
            km_ref[:, 256 * h:256 * h + 128] = _rms(kv_ref[:, 128 * h:128 * h + 128], kg_ref[:, 0:128]).astype(BF16)
            km_ref[:, 256 * h + 128:256 * h + 256] = kr
            qm_ref[:, 256 * h:256 * h + 128] = _rms(q_ref[:, 256 * h:256 * h + 128], qg_ref[:, 0:128]).astype(BF16)
            qm_ref[:, 256 * h + 128:256 * h + 256] = _normrope(
                q_ref[:, 256 * h + 128:256 * h + 256], qg_ref[:, 128:256], cs, sn, 16, MLA_ROPE).astype(BF16)

    return _rowcall(name, body, n,
                    [(kv, _rs(1024, 0)), (kv, _rs(1024, 1)), (qp, _rs(2048)), (u, _rs(128, C_MISC // 128)), (qg, _ps((1, 256))),
                     (kg, _ps((1, 256))), (cos, _rs(128)), (sin, _rs(128))],
                    [(_sd((n, 2048), BF16), _rs(2048)), (_sd((n, 2048), BF16), _rs(2048)), (_sd((n, 1024), BF16), _rs(1024))])


def mla_prep_bwd(name, kv, qp, u, qg, kg, cos, sin, dkm, dqm, dv):
    n = kv.shape[0]

    def body(kv_ref, q_ref, m_ref, qg_ref, kg_ref, cos_ref, sin_ref, dkm_ref, dqm_ref, dv_ref,
             dkv_ref, dq_ref, dkr_ref, dqg_ref, dkg_ref):
        i = pl.program_id(0)
        cs, sn = cos_ref[...], sin_ref[...]
        fr = lambda x, g: _normrope(x, g, cs, sn, 16, MLA_ROPE)
        dkg_n = jnp.zeros((1, 128), F32)
        dqg_n = jnp.zeros((1, 128), F32)
        dqg_r = jnp.zeros((1, 128), F32)
        dkr_sum = jnp.zeros((RT, 128), F32)
        for h in range(MLA_H):
            dk_h = jnp.transpose(dkm_ref[256 * h:256 * h + 256, :])
            dkv_ref[:, 1024 + 128 * h:1024 + 128 * h + 128] = jnp.transpose(dv_ref[128 * h:128 * h + 128, :]).astype(BF16)
            _, vjp = jax.vjp(_rms, kv_ref[:, 128 * h:128 * h + 128], kg_ref[:, 0:128])
            dx, dg = vjp(dk_h[:, 0:128])
            dkv_ref[:, 128 * h:128 * h + 128] = dx.astype(BF16)
            dkg_n = dkg_n + dg
            dkr_sum = dkr_sum + dk_h[:, 128:256]
            _, vjp = jax.vjp(_rms, q_ref[:, 256 * h:256 * h + 128], qg_ref[:, 0:128])
            dx, dg = vjp(dqm_ref[:, 256 * h:256 * h + 128])
            dq_ref[:, 256 * h:256 * h + 128] = dx.astype(BF16)
            dqg_n = dqg_n + dg
            _, vjp = jax.vjp(fr, q_ref[:, 256 * h + 128:256 * h + 256], qg_ref[:, 128:256])
            dx, dg = vjp(dqm_ref[:, 256 * h + 128:256 * h + 256])
            dq_ref[:, 256 * h + 128:256 * h + 256] = dx.astype(BF16)
            dqg_r = dqg_r + dg
        _, vjp = jax.vjp(lambda m, g: _mla_krope(m, g, cs, sn), m_ref[...], kg_ref[:, 128:256])
        dm, dkg_r = vjp(dkr_sum)
        dkr_ref[...] = dm
        _acc(dqg_ref.at[:, 0:128], dqg_n, i == 0)
        _acc(dqg_ref.at[:, 128:256], dqg_r, i == 0)
        _acc(dkg_ref.at[:, 0:128], dkg_n, i == 0)
        _acc(dkg_ref.at[:, 128:256], dkg_r, i == 0)

    return _rowcall(name, body, n,
                    [(kv, _rs(1024, 0)), (qp, _rs(2048)), (u, _rs(128, C_MISC // 128)), (qg, _ps((1, 256))), (kg, _ps((1, 256))),
                     (cos, _rs(128)), (sin, _rs(128)), (dkm, pl.BlockSpec((2048, RT), lambda i: (0, i))), (dqm, _rs(2048)),
                     (dv, pl.BlockSpec((1024, RT), lambda i: (0, i)))],
                    [(_sd((n, 2048), BF16), _rs(2048)), (_sd((n, 2048), BF16), _rs(2048)), (_sd((n, 128), F32), _rs(128)),
                     (_sd((1, 256), F32), _ps((1, 256))), (_sd((1, 256), F32), _ps((1, 256)))])


def misc_combine(name, dkr, dm_f, dm_b, drow_t):
    n = dkr.shape[0]

    def body(a_ref, f_ref, b_ref, r_ref, o_ref):
        o_ref[...] = (a_ref[...] + f_ref[0] + f_ref[1] + b_ref[0] + b_ref[1] + r_ref[...]).astype(BF16)

    g2 = pl.BlockSpec((2, RT, 128), lambda i: (0, i, 0))
    return _rowcall(name, body, n, [(dkr, _rs(128)), (dm_f, g2), (dm_b, g2), (drow_t, _rs(128))],
                    [(_sd((n, 128), BF16), _rs(128))])[0]


def _f32(ref):
    return ref[...].astype(F32)


def _merge(g1, g2, g3, p1, p2, p3):
    return jax.nn.sigmoid(g1) * p1 + jax.nn.sigmoid(g2) * p2 + jax.nn.sigmoid(g3) * p3


def merge_fwd(name, u, p1, p2, p3):
    n = u.shape[0]

    def body(g1, g2, g3, a, b, c, o_ref):
        o_ref[...] = _merge(g1[...], g2[...], g3[...], _f32(a), _f32(b), _f32(c)).astype(BF16)

    return _rowcall(name, body, n, [(u, _rs(D, 0)), (u, _rs(D, 1)), (u, _rs(D, 2)), (p1, _rs(D)), (p2, _rs(D)), (p3, _rs(D))],
                    [(_sd((n, D), BF16), _rs(D))])[0]


def merge_bwd(name, u, p1, p2, p3, dm):
    n = u.shape[0]

    def body(g1, g2, g3, a, b, c, dm_ref, d1, d2, d3, dg_ref):
        _, vjp = jax.vjp(_merge, g1[...], g2[...], g3[...], _f32(a), _f32(b), _f32(c))
        r = vjp(dm_ref[...])
        for k in range(3):
            dg_ref[:, D * k:D * k + D] = r[k].astype(BF16)
        d1[...] = r[3].astype(BF16)
        d2[...] = r[4].astype(BF16)
        d3[...] = r[5].astype(BF16)

    return _rowcall(name, body, n,
                    [(u, _rs(D, 0)), (u, _rs(D, 1)), (u, _rs(D, 2)), (p1, _rs(D)), (p2, _rs(D)), (p3, _rs(D)), (dm, _rs(D))],
                    [(_sd((n, D), BF16), _rs(D))] * 3 + [(_sd((n, 3 * D), BF16), _rs(3 * D))])


def _swiglu(g, u):
    return _silu(g) * u


def swiglu_fwd(name, gu):
    n = gu.shape[0]

    def body(g_ref, u_ref, o_ref):
        o_ref[...] = _swiglu(_f32(g_ref), _f32(u_ref)).astype(BF16)

    return _rowcall(name, body, n, [(gu, _rs(FFN, 0)), (gu, _rs(FFN, 1))], [(_sd((n, FFN), BF16), _rs(FFN))])[0]


def swiglu_bwd(name, gu, da):
    n = gu.shape[0]

    def body(g_ref, u_ref, da_ref, o_ref):
        _, vjp = jax.vjp(_swiglu, _f32(g_ref), _f32(u_ref))
        dg, du = vjp(da_ref[...])
        o_ref[:, 0:FFN] = dg.astype(BF16)
        o_ref[:, FFN:2 * FFN] = du.astype(BF16)

    return _rowcall(name, body, n, [(gu, _rs(FFN, 0)), (gu, _rs(FFN, 1)), (da, _rs(FFN))],
                    [(_sd((n, 2 * FFN), BF16), _rs(2 * FFN))])[0]


FLASH_ROWS = 256


def _fold_lanes(x, op):
    acc = x[:, 0:128]
    for b in range(1, x.shape[1] // 128):
        acc = op(acc, x[:, 128 * b:128 * b + 128])
    return acc


def _band_mask(tq, tk, i, kb):
    qp = i * tq + lax.broadcasted_iota(jnp.int32, (tq, tk), 0)
    kp = kb * tk + lax.broadcasted_iota(jnp.int32, (tq, tk), 1)
    return jnp.abs(qp - kp) <= SWA_WIN


def flash_fwd(name, qa, ka, va, *, w, vw, hq, grp, vcol0, scale, nlat, tq, tk, band, sink, ctx_q, prev=None):
    n = qa.shape[0]
    cblk = nlat // NCTX
    band = band and not ctx_q
    assert not band, "latent rows of a banded attention go through swa_fwd_lat"
    if ctx_q:
        tq = tk = NCTX
        grid = (hq, 1, 1)
        qmap = lambda h, i, kk: (cblk, h)
        kmap = lambda h, i, kk: (cblk, h // grp)
        vmap = lambda h, i, kk: (cblk, vcol0 + h // grp)
        omap = lambda h, i, kk: (cblk, h)
        lmap = lambda h, i, kk: (h, cblk, 0)
    else:
        nb = nlat // tk
        nk = 3 if band else nb
        grid = (hq, nlat // tq, nk)
        kb_of = (lambda i, kk: jnp.clip(i + kk - 1, 0, nb - 1)) if band else (lambda i, kk: kk)
        qmap = lambda h, i, kk: (i, h)
        kmap = lambda h, i, kk: (kb_of(i, kk), h // grp)
        vmap = lambda h, i, kk: (kb_of(i, kk), vcol0 + h // grp)
        omap = lambda h, i, kk: (i, h)
        lmap = lambda h, i, kk: (h, i, 0)
    nk = grid[2]
    extra = not ctx_q
    has_sink = sink is not None

    def body(*refs):
        refs = list(refs)
        q_ref, k_ref, v_ref = refs[:3]
        pos = 3
        if extra:
            ke_ref, ve_ref = refs[pos:pos + 2]
            pos += 2
        if has_sink:
            s_ref = refs[pos]
            pos += 1
        if prev is not None:
            pos += 2
        o_ref, l_ref, m_s, l_s, a_s = refs[pos:pos + 5]
        kk = pl.program_id(2)
        tr = min(tq, FLASH_ROWS)

        def step(kblk, vblk):
            for r in range(tq // tr):
                rows = slice(r * tr, (r + 1) * tr)
                s = _d(q_ref[rows, :], kblk, ((1,), (1,))) * (scale * LOG2E)
                m_prev = m_s[rows, :]
                m_new = jnp.maximum(m_prev, jnp.max(_fold_lanes(s, jnp.maximum), axis=1, keepdims=True))
                alpha = jnp.exp2(m_prev - m_new)
                p = jnp.exp2(s - m_new)
                l_s[rows, :] = alpha * l_s[rows, :] + _fold_lanes(p, jnp.add)
                a_s[rows, :] = alpha * a_s[rows, :] + _d(p, vblk, ((1,), (0,)))
                m_s[rows, :] = m_new

        @pl.when(kk == 0)
        def _():
            if has_sink:
                sv = jnp.max(s_ref[0], axis=1, keepdims=True) * LOG2E
                m_s[...] = jnp.zeros((tq, 1), F32) + sv
                l_s[...] = (lax.broadcasted_iota(jnp.int32, (tq, 128), 1) == 0).astype(F32)
            else:
                m_s[...] = jnp.full((tq, 1), NEG, F32)
                l_s[...] = jnp.zeros((tq, 128), F32)
            a_s[...] = jnp.zeros((tq, vw), F32)
            if extra:
                step(ke_ref[...], ve_ref[...])

        step(k_ref[...], v_ref[...])

        @pl.when(kk == nk - 1)
        def _():
            l = jnp.sum(l_s[...], axis=1, keepdims=True)
            o_ref[...] = (a_s[...] / l).astype(BF16)
            l_ref[0] = m_s[...] + jnp.log2(l)

    ins = [(qa, pl.BlockSpec((tq, w), qmap)), (ka, pl.BlockSpec((tk, w), kmap)), (va, pl.BlockSpec((tk, vw), vmap))]
    if extra:
        ins += [(ka, pl.BlockSpec((NCTX, w), lambda h, i, kk: (cblk, h // grp))),
                (va, pl.BlockSpec((NCTX, vw), lambda h, i, kk: (cblk, vcol0 + h // grp)))]
    if has_sink:
        ins += [(sink, pl.BlockSpec((1, 1, 128), lambda h, i, kk: (h, 0, 0)))]
    aliases = {}
    if prev is not None:
        any_spec = pl.BlockSpec(memory_space=pl.ANY)
        aliases = {len(ins): 0, len(ins) + 1: 1}
        ins += [(prev[0], any_spec), (prev[1], any_spec)]
    return pl.pallas_call(
        body, out_shape=[_sd((n, hq * vw), BF16), _sd((hq, n, 1), F32)], grid=grid,
        in_specs=[s for _, s in ins],
        out_specs=[pl.BlockSpec((tq, vw), omap), pl.BlockSpec((1, tq, 1), lmap)],
        scratch_shapes=[pltpu.VMEM((tq, 1), F32), pltpu.VMEM((tq, 128), F32), pltpu.VMEM((tq, vw), F32)],
        input_output_aliases=aliases, name=name,
        compiler_params=_cp(("parallel", "parallel", "arbitrary")))(*[a for a, _ in ins])


def flash_dq(name, qa, ka, va, oa, doa, lse, *, w, vw, hq, grp, vcol0, scale, nlat, tq, tk, band, sink, ctx_q, prev=None):
    n = qa.shape[0]
    cblk = nlat // NCTX
    band = band and not ctx_q
    if ctx_q:
        tq = tk = NCTX
        grid = (hq, 1, 1)
        qmap = lambda h, i, kk: (cblk, h)
        kmap = lambda h, i, kk: (cblk, h // grp)
        vmap = lambda h, i, kk: (cblk, vcol0 + h // grp)
        lmap = lambda h, i, kk: (h, cblk, 0)
    else:
        nb = nlat // tk
        grid = (hq, nlat // tq, 3 if band else nb)
        kb_of = (lambda i, kk: jnp.clip(i + kk - 1, 0, nb - 1)) if band else (lambda i, kk: kk)
        qmap = lambda h, i, kk: (i, h)
        kmap = lambda h, i, kk: (kb_of(i, kk), h // grp)
        vmap = lambda h, i, kk: (kb_of(i, kk), vcol0 + h // grp)
        lmap = lambda h, i, kk: (h, i, 0)
    nk = grid[2]
    nq = grid[1]
    extra = not ctx_q
    has_sink = sink is not None

    def body(*refs):
        refs = list(refs)
        q_ref, k_ref, v_ref, o_ref, do_ref, l_ref = refs[:6]
        pos = 6
        if extra:
            ke_ref, ve_ref = refs[pos:pos + 2]
            pos += 2
        if has_sink:
            s_ref = refs[pos]
            pos += 1
        if prev is not None:
            pos += 2
        dq_ref, dl_ref, ds_ref, acc_s, dl_s = refs[pos:pos + 5]
        i = pl.program_id(1)
        kk = pl.program_id(2)
        q = q_ref[...]
        do = do_ref[...]
        lse_v = l_ref[0]

        def step(kblk, vblk, mask):
            s = _d(q, kblk, ((1,), (1,))) * (scale * LOG2E)
            if mask is not None:
                s = jnp.where(mask, s, NEG)
            p = jnp.exp2(s - lse_v)
            dp = _d(do, vblk, ((1,), (1,)))
            ds = p * (dp - dl_s[...]) * scale
            acc_s[...] += _d(ds, kblk, ((1,), (0,)))

        @pl.when(kk == 0)
        def _():
            delta = jnp.sum(do * o_ref[...].astype(F32), axis=1, keepdims=True)
            dl_s[...] = delta
            acc_s[...] = jnp.zeros((tq, w), F32)
            if has_sink:
                sv = jnp.max(s_ref[0], axis=1, keepdims=True) * LOG2E
                dsk = jnp.sum(-jnp.exp2(sv - lse_v) * delta, axis=0, keepdims=True)
                _acc(ds_ref, jnp.zeros((1, 1, 128), F32) + dsk, i == 0)
            else:
                ds_ref[...] = jnp.zeros((1, 1, 128), F32)
            if extra:
                step(ke_ref[...], ve_ref[...], None)

        if band:
            kb = i + kk - 1

            @pl.when((kb >= 0) & (kb < nlat // tk))
            def _():
                step(k_ref[...], v_ref[...], _band_mask(tq, tk, i, kb))
        else:
            step(k_ref[...], v_ref[...], None)

        @pl.when(kk == nk - 1)
        def _():
            dq_ref[...] = acc_s[...]
            dl_ref[0] = dl_s[...]

    ins = [(qa, pl.BlockSpec((tq, w), qmap)), (ka, pl.BlockSpec((tk, w), kmap)), (va, pl.BlockSpec((tk, vw), vmap)),
           (oa, pl.BlockSpec((tq, vw), qmap)), (doa, pl.BlockSpec((tq, vw), qmap)), (lse, pl.BlockSpec((1, tq, 1), lmap))]
    if extra:
        ins += [(ka, pl.BlockSpec((NCTX, w), lambda h, i, kk: (cblk, h // grp))),
                (va, pl.BlockSpec((NCTX, vw), lambda h, i, kk: (cblk, vcol0 + h // grp)))]
    if has_sink:
        ins += [(sink, pl.BlockSpec((1, 1, 128), lambda h, i, kk: (h, 0, 0)))]
    aliases = {}
    if prev is not None:
        any_spec = pl.BlockSpec(memory_space=pl.ANY)
        aliases = {len(ins): 0, len(ins) + 1: 1}
        ins += [(prev[0], any_spec), (prev[1], any_spec)]
    del nq
    return pl.pallas_call(
        body, out_shape=[_sd((n, hq * w), F32), _sd((hq, n, 1), F32), _sd((hq, 1, 128), F32)], grid=grid,
        in_specs=[s for _, s in ins],
        out_specs=[pl.BlockSpec((tq, w), qmap), pl.BlockSpec((1, tq, 1), lmap),
                   pl.BlockSpec((1, 1, 128), lambda h, i, kk: (h, 0, 0))],
        scratch_shapes=[pltpu.VMEM((tq, w), F32), pltpu.VMEM((tq, 1), F32)],
        input_output_aliases=aliases, name=name,
        compiler_params=_cp(("parallel", "arbitrary", "arbitrary")))(*[a for a, _ in ins])


def flash_dkv(name, qa, ka, va, doa, lse, delta, *, w, vw, hkv, grp, vcol0, scale, nlat, tq, tk, band, ctx_k, prev=None):
    n = qa.shape[0]
    cblk = nlat // NCTX
    nqb = nlat // tq
    band = band and not ctx_k
    if ctx_k:
        tk = NCTX
        nqs = nqb
        grid = (hkv, 1, grp * nqs)
        kmap = lambda hk, j, t: (cblk, hk)
        vmap = lambda hk, j, t: (cblk, vcol0 + hk)
        dvmap = lambda hk, j, t: (cblk, hk)
        qb_of = lambda j, t: t % nqs
    else:
        nqs = 3 if band else nqb
        grid = (hkv, nlat // tk, grp * nqs)
        kmap = lambda hk, j, t: (j, hk)
        vmap = lambda hk, j, t: (j, vcol0 + hk)
        dvmap = lambda hk, j, t: (j, hk)
        qb_of = (lambda j, t: jnp.clip(j + t % nqs - 1, 0, nqb - 1)) if band else (lambda j, t: t % nqs)
    qmap = lambda hk, j, t: (qb_of(j, t), hk * grp + t // nqs)
    lmap = lambda hk, j, t: (hk * grp + t // nqs, qb_of(j, t), 0)

    def body(*refs):
        refs = list(refs)
        q_ref, k_ref, v_ref, do_ref, l_ref, dl_ref = refs[:6]
        pos = 6
        if ctx_k:
            qe_ref, doe_ref, le_ref, dle_ref = refs[pos:pos + 4]
            pos += 4
        if prev is not None:
            pos += 2
        dk_ref, dv_ref = refs[pos:pos + 2]
        j = pl.program_id(1)
        t = pl.program_id(2)
        kblk = k_ref[...]
        vblk = v_ref[...]

        def contrib(q, do, lse_v, dl_v, mask):
            s = _d(q, kblk, ((1,), (1,))) * (scale * LOG2E)
            if mask is not None:
                s = jnp.where(mask, s, NEG)
            p = jnp.exp2(s - lse_v)
            dp = _d(do, vblk, ((1,), (1,)))
            ds = p * (dp - dl_v) * scale
            return _d(ds, q, ((0,), (0,))), _d(p, do, ((0,), (0,)))

        @pl.when(t == 0)
        def _():
            dk = jnp.zeros((tk, w), F32)
            dv = jnp.zeros((tk, vw), F32)
            if ctx_k:
                for gi in range(grp):
                    a, b = contrib(qe_ref[:, w * gi:w * gi + w], doe_ref[:, vw * gi:vw * gi + vw], le_ref[gi], dle_ref[gi], None)
                    dk = dk + a
                    dv = dv + b
            dk_ref[...] = dk
            dv_ref[...] = dv

        def add(mask):
            a, b = contrib(q_ref[...], do_ref[...], l_ref[0], dl_ref[0], mask)
            dk_ref[...] += a
            dv_ref[...] += b

        if band:
            qb = j + t % nqs - 1

            @pl.when((qb >= 0) & (qb < nqb))
            def _():
                add(_band_mask(tq, tk, qb, j))
        else:
            add(None)

    ins = [(qa, pl.BlockSpec((tq, w), qmap)), (ka, pl.BlockSpec((tk, w), kmap)), (va, pl.BlockSpec((tk, vw), vmap)),
           (doa, pl.BlockSpec((tq, vw), qmap)), (lse, pl.BlockSpec((1, tq, 1), lmap)), (delta, pl.BlockSpec((1, tq, 1), lmap))]
    if ctx_k:
        ins += [(qa, pl.BlockSpec((NCTX, grp * w), lambda hk, j, t: (cblk, hk))),
                (doa, pl.BlockSpec((NCTX, grp * vw), lambda hk, j, t: (cblk, hk))),
                (lse, pl.BlockSpec((grp, NCTX, 1), lambda hk, j, t: (hk, cblk, 0))),
                (delta, pl.BlockSpec((grp, NCTX, 1), lambda hk, j, t: (hk, cblk, 0)))]
    aliases = {}
    if prev is not None:
        any_spec = pl.BlockSpec(memory_space=pl.ANY)
        aliases = {len(ins): 0, len(ins) + 1: 1}
        ins += [(prev[0], any_spec), (prev[1], any_spec)]
    return pl.pallas_call(
        body, out_shape=[_sd((n, hkv * w), F32), _sd((n, hkv * vw), F32)], grid=grid,
        in_specs=[s for _, s in ins],
        out_specs=[pl.BlockSpec((tk, w), kmap), pl.BlockSpec((tk, vw), dvmap)],
        input_output_aliases=aliases, name=name,
        compiler_params=_cp(("parallel", "parallel", "arbitrary")))(*[a for a, _ in ins])


MLA_FWD_CHUNKS = 3
MLA_BWD_CHUNKS = 2


def mla_fwd(name, qm, km, vm, nlat):
    n = qm.shape[0]
    t = NCTX
    nlt = nlat // t
    c = (MLA_NOPE + MLA_ROPE) ** -0.5 * LOG2E

    nchunk = MLA_FWD_CHUNKS if (n // 128) % MLA_FWD_CHUNKS == 0 else 1
    cw = n // nchunk

    def body(q_ref, k_ref, v_ref, o_ref, l_ref):
        i = pl.program_id(1)

        def run(spans):
            parts = []
            for a, b in spans:
                s = _d(q_ref[...], k_ref[a:b, :], ((1,), (1,))) * c
                m = jnp.max(_fold_lanes(s, jnp.maximum), axis=1, keepdims=True)
                p = jnp.exp2(s - m)
                parts.append((m, jnp.sum(_fold_lanes(p, jnp.add), axis=1, keepdims=True), _d(p, v_ref[a:b, :], ((1,), (0,)))))
            m = functools.reduce(jnp.maximum, [pt[0] for pt in parts])
            l = sum(jnp.exp2(pm - m) * pl_ for pm, pl_, _ in parts)
            acc = sum(jnp.exp2(pm - m) * pa for pm, _, pa in parts)
            o_ref[...] = (acc / l).astype(BF16)
            l_ref[0] = m + jnp.log2(l)

        @pl.when(i < nlt)
        def _():
            run([(j * cw, (j + 1) * cw) for j in range(nchunk)])

        @pl.when(i == nlt)
        def _():
            run([(nlat, n)])

    return pl.pallas_call(
        body, out_shape=[_sd((n, MLA_H * 128), BF16), _sd((MLA_H, n, 1), F32)], grid=(MLA_H, n // t),
        in_specs=[pl.BlockSpec((t, 256), lambda h, i: (i, h)), pl.BlockSpec((n, 256), lambda h, i: (0, h)),
                  pl.BlockSpec((n, 128), lambda h, i: (0, h))],
        out_specs=[pl.BlockSpec((t, 128), lambda h, i: (i, h)), pl.BlockSpec((1, t, 1), lambda h, i: (h, i, 0))],
        name=name, compiler_params=_cp(("parallel", "arbitrary")))(qm, km, vm)


def mla_bwd(name, qm, km, vm, o, do, lse, nlat):
    n = qm.shape[0]
    t = NCTX
    nlt = nlat // t
    scale = (MLA_NOPE + MLA_ROPE) ** -0.5
    nchunk = MLA_BWD_CHUNKS if (n // 128) % MLA_BWD_CHUNKS == 0 else 1
    cw = n // nchunk

    def body(q_ref, k_ref, v_ref, o_ref, do_ref, l_ref, dq_ref, dkt_ref, dvt_ref):
        i = pl.program_id(1)

        @pl.when(i == 0)
        def _():
            dkt_ref[...] = jnp.zeros((256, n), F32)
            dvt_ref[...] = jnp.zeros((128, n), F32)

        q = q_ref[...]
        do = do_ref[...]
        delta = jnp.sum(do.astype(F32) * o_ref[...].astype(F32), axis=1, keepdims=True)

        def run(spans):
            dq = jnp.zeros((t, 256), F32)
            for a, b in spans:
                kc = k_ref[a:b, :]
                s = _d(q, kc, ((1,), (1,))) * (scale * LOG2E)
                p = jnp.exp2(s - l_ref[0])
                ds = (p * (_d(do, v_ref[a:b, :], ((1,), (1,))) - delta) * scale).astype(BF16)
                dq = dq + _d(ds, kc, ((1,), (0,)))
                dkt_ref[:, a:b] += _d(q, ds, ((0,), (0,)))
                dvt_ref[:, a:b] += _d(do, p, ((0,), (0,)))
            dq_ref[...] = dq

        @pl.when(i < nlt)
        def _():
            run([(c * cw, (c + 1) * cw) for c in range(nchunk)])

        @pl.when(i == nlt)
        def _():
            run([(nlat, n)])

    qspec = pl.BlockSpec((t, 256), lambda h, i: (i, h))
    ospec = pl.BlockSpec((t, 128), lambda h, i: (i, h))
    return pl.pallas_call(
        body, out_shape=[_sd((n, MLA_H * 256), F32), _sd((MLA_H * 256, n), F32), _sd((MLA_H * 128, n), F32)],
        grid=(MLA_H, n // t),
        in_specs=[qspec, pl.BlockSpec((n, 256), lambda h, i: (0, h)), pl.BlockSpec((n, 128), lambda h, i: (0, h)),
                  ospec, ospec, pl.BlockSpec((1, t, 1), lambda h, i: (h, i, 0))],
        out_specs=[qspec, pl.BlockSpec((256, n), lambda h, i: (h, 0)), pl.BlockSpec((128, n), lambda h, i: (h, 0))],
        name=name, compiler_params=_cp(("parallel", "arbitrary")))(qm, km, vm, o, do, lse)


def mla_attention_bwd(tag, qm, km, vm, o, do, lse, nlat):
    return mla_bwd(tag + "_bwd", qm, km, vm, o, do, lse, nlat)


SWA_T = 512


def _swa_window(t, nlat):
    t = min(t, nlat)
    return t, min(t + 2 * SWA_WIN, nlat)


def _win_start(i, t, wlen, nlat):
    return pl.multiple_of(jnp.clip(i * t - SWA_WIN, 0, nlat - wlen), 128)


def _win_mask(rows, cols, row0, col0):
    rp = row0 + lax.broadcasted_iota(jnp.int32, (rows, cols), 0)
    cp = col0 + lax.broadcasted_iota(jnp.int32, (rows, cols), 1)
    return jnp.abs(rp - cp) <= SWA_WIN


def swa_fwd_lat(name, qs, ks, u, sink, nlat):
    n = qs.shape[0]
    tq, wlen = _swa_window(SWA_T, nlat)
    grp = SWA_HQ // SWA_HKV
    scale = SWA_DH ** -0.5
    vcol0 = C_V // 128

    def body(q_ref, k_ref, v_ref, s_ref, o_ref, l_ref):
        i = pl.program_id(1)
        ws = _win_start(i, tq, wlen, nlat)
        q = q_ref[...]
        s1 = _d(q, k_ref[pl.ds(ws, wlen), :], ((1,), (1,))) * (scale * LOG2E)
        s1 = jnp.where(_win_mask(tq, wlen, i * tq, ws), s1, NEG)
        s2 = _d(q, k_ref[pl.ds(nlat, NCTX), :], ((1,), (1,))) * (scale * LOG2E)
        sv = jnp.max(s_ref[0], axis=1, keepdims=True) * LOG2E
        m = jnp.maximum(jnp.maximum(jnp.max(s1, axis=1, keepdims=True), jnp.max(s2, axis=1, keepdims=True)), sv)
        p1 = jnp.exp2(s1 - m)
        p2 = jnp.exp2(s2 - m)
        l = jnp.sum(p1, axis=1, keepdims=True) + jnp.sum(p2, axis=1, keepdims=True) + jnp.exp2(sv - m)
        acc = _d(p1, v_ref[pl.ds(ws, wlen), :], ((1,), (0,))) + _d(p2, v_ref[pl.ds(nlat, NCTX), :], ((1,), (0,)))
        o_ref[...] = (acc / l).astype(BF16)
        l_ref[0] = m + jnp.log2(l)

    return pl.pallas_call(
        body, out_shape=[_sd((n, SWA_HQ * 128), BF16), _sd((SWA_HQ, n, 1), F32)], grid=(SWA_HQ, nlat // tq),
        in_specs=[pl.BlockSpec((tq, 128), lambda h, i: (i, h)), pl.BlockSpec((n, 128), lambda h, i: (0, h // grp)),
                  pl.BlockSpec((n, 128), lambda h, i: (0, vcol0 + h // grp)), pl.BlockSpec((1, 1, 128), lambda h, i: (h, 0, 0))],
        out_specs=[pl.BlockSpec((tq, 128), lambda h, i: (i, h)), pl.BlockSpec((1, tq, 1), lambda h, i: (h, i, 0))],
        name=name, compiler_params=_cp(("parallel", "arbitrary")))(qs, ks, u, sink)


def swa_dq_lat(name, qs, ks, u, o, do, lse, sink, nlat):
    n = qs.shape[0]
    tq, wlen = _swa_window(SWA_T, nlat)
    grp = SWA_HQ // SWA_HKV
    scale = SWA_DH ** -0.5
    vcol0 = C_V // 128

    def body(q_ref, k_ref, v_ref, s_ref, o_ref, do_ref, l_ref, dq_ref, dl_ref, ds_ref):
        i = pl.program_id(1)
        ws = _win_start(i, tq, wlen, nlat)
        q = q_ref[...]
        do = do_ref[...]
        lse_v = l_ref[0]
        delta = jnp.sum(do.astype(F32) * o_ref[...].astype(F32), axis=1, keepdims=True)
        kw = k_ref[pl.ds(ws, wlen), :]
        kc = k_ref[pl.ds(nlat, NCTX), :]
        s1 = _d(q, kw, ((1,), (1,))) * (scale * LOG2E)
        s1 = jnp.where(_win_mask(tq, wlen, i * tq, ws), s1, NEG)
        s2 = _d(q, kc, ((1,), (1,))) * (scale * LOG2E)
        ds1 = jnp.exp2(s1 - lse_v) * (_d(do, v_ref[pl.ds(ws, wlen), :], ((1,), (1,))) - delta) * scale
        ds2 = jnp.exp2(s2 - lse_v) * (_d(do, v_ref[pl.ds(nlat, NCTX), :], ((1,), (1,))) - delta) * scale
        dq_ref[...] = _d(ds1, kw, ((1,), (0,))) + _d(ds2, kc, ((1,), (0,)))
        dl_ref[0] = delta
        sv = jnp.max(s_ref[0], axis=1, keepdims=True) * LOG2E
        dsk = jnp.sum(-jnp.exp2(sv - lse_v) * delta, axis=0, keepdims=True)
        _acc(ds_ref, jnp.zeros((1, 1, 128), F32) + dsk, i == 0)

    qspec = pl.BlockSpec((tq, 128), lambda h, i: (i, h))
    lspec = pl.BlockSpec((1, tq, 1), lambda h, i: (h, i, 0))
    return pl.pallas_call(
        body, out_shape=[_sd((n, SWA_HQ * 128), F32), _sd((SWA_HQ, n, 1), F32), _sd((SWA_HQ, 1, 128), F32)],
        grid=(SWA_HQ, nlat // tq),
        in_specs=[qspec, pl.BlockSpec((n, 128), lambda h, i: (0, h // grp)),
                  pl.BlockSpec((n, 128), lambda h, i: (0, vcol0 + h // grp)), pl.BlockSpec((1, 1, 128), lambda h, i: (h, 0, 0)),
                  qspec, qspec, lspec],
        out_specs=[qspec, lspec, pl.BlockSpec((1, 1, 128), lambda h, i: (h, 0, 0))],
        name=name, compiler_params=_cp(("parallel", "arbitrary")))(qs, ks, u, sink, o, do, lse)


def swa_dkv_lat(name, qs, ks, u, do, lse_row, delta_row, nlat):
    n = qs.shape[0]
    tk, wlen = _swa_window(SWA_T, nlat)
    grp = SWA_HQ // SWA_HKV
    scale = SWA_DH ** -0.5
    vcol0 = C_V // 128

    def body(q_ref, k_ref, v_ref, do_ref, l_ref, dl_ref, dk_ref, dv_ref):
        j = pl.program_id(1)
        ws = _win_start(j, tk, wlen, nlat)
        k = k_ref[...]
        v = v_ref[...]
        mask = _win_mask(tk, wlen, j * tk, ws)
        dk = jnp.zeros((tk, 128), F32)
        dv = jnp.zeros((tk, 128), F32)
        for gi in range(grp):
            qw = q_ref[pl.ds(ws, wlen), 128 * gi:128 * gi + 128]
            dow = do_ref[pl.ds(ws, wlen), 128 * gi:128 * gi + 128]
            st = jnp.where(mask, _d(k, qw, ((1,), (1,))) * (scale * LOG2E), NEG)
            pt = jnp.exp2(st - l_ref[gi, :, pl.ds(ws, wlen)])
            dv = dv + _d(pt, dow, ((1,), (0,)))
            dst = pt * (_d(v, dow, ((1,), (1,))) - dl_ref[gi, :, pl.ds(ws, wlen)]) * scale
            dk = dk + _d(dst, qw, ((1,), (0,)))
        dk_ref[...] = dk
        dv_ref[...] = dv

    rspec = pl.BlockSpec((grp, 1, n), lambda hk, j: (hk, 0, 0))
    return pl.pallas_call(
        body, out_shape=[_sd((n, SWA_HKV * 128), F32), _sd((n, SWA_HKV * 128), F32)], grid=(SWA_HKV, nlat // tk),
        in_specs=[pl.BlockSpec((n, grp * 128), lambda hk, j: (0, hk)), pl.BlockSpec((tk, 128), lambda hk, j: (j, hk)),
                  pl.BlockSpec((tk, 128), lambda hk, j: (j, vcol0 + hk)), pl.BlockSpec((n, grp * 128), lambda hk, j: (0, hk)),
                  rspec, rspec],
        out_specs=[pl.BlockSpec((tk, 128), lambda hk, j: (j, hk)), pl.BlockSpec((tk, 128), lambda hk, j: (j, hk))],
        name=name, compiler_params=_cp(("parallel", "arbitrary")))(qs, ks, u, do, lse_row, delta_row)


def swa_attention_fwd(tag, qs, ks, u, sink, cfg, nlat):
    o, lse = swa_fwd_lat(tag + "_fwd_lat", qs, ks, u, sink, nlat)
    return flash_fwd(tag + "_fwd_ctx", qs, ks, u, sink=sink, ctx_q=True, nlat=nlat, prev=(o, lse), **cfg)


def swa_attention_bwd(tag, qs, ks, u, o, do, lse, sink, cfg, nlat):
    n = qs.shape[0]
    dq, delta, ds1 = swa_dq_lat(tag + "_dq_lat", qs, ks, u, o, do, lse, sink, nlat)
    dq, delta, ds2 = flash_dq(tag + "_dq_ctx", qs, ks, u, o, do, lse, sink=sink, ctx_q=True, nlat=nlat, prev=(dq, delta), **cfg)
    dk, dv = swa_dkv_lat(tag + "_dkv_lat", qs, ks, u, do, lse.reshape(SWA_HQ, 1, n), delta.reshape(SWA_HQ, 1, n), nlat)
    kc = {k: v for k, v in cfg.items() if k != "hq"}
    kc["hkv"] = SWA_HKV
    kc["tq"] = min(1024, nlat)
    dk, dv = flash_dkv(tag + "_dkv_ctx", qs, ks, u, do, lse, delta, ctx_k=True, nlat=nlat, prev=(dk, dv), **kc)
    return dq, dk, dv, ds1 + ds2


def adamw(name, w, g, m, v):
    r, c = w.shape
    tr = _pick(r, (256, 128, 64, 32, 16, 8))
    bc1 = 1.0 - ADAM_B1 ** ADAM_STEP
    bc2 = 1.0 - ADAM_B2 ** ADAM_STEP

    def body(w_ref, g_ref, m_ref, v_ref, d_ref, nm_ref, nv_ref):
        gv = g_ref[...]
        nm = ADAM_B1 * m_ref[...] + (1.0 - ADAM_B1) * gv
        nv = ADAM_B2 * v_ref[...] + (1.0 - ADAM_B2) * (gv * gv)
        d_ref[...] = -ADAM_LR * ((nm / bc1) / (jnp.sqrt(nv / bc2) + ADAM_EPS) + ADAM_WD * w_ref[...])
        nm_ref[...] = nm
        nv_ref[...] = nv

    spec = pl.BlockSpec((tr, c), lambda i: (i, 0))
    return pl.pallas_call(body, out_shape=[_sd((r, c), F32)] * 3, grid=(r // tr,), in_specs=[spec] * 4, out_specs=[spec] * 3,
                          name=name, compiler_params=_cp(("parallel",)))(w, g, m, v)


def _coords():
    return lax.axis_index("x"), lax.axis_index("y"), lax.axis_index("c")


_ANY = pl.BlockSpec(memory_space=pl.ANY)


def _chip():
    return 2 * lax.axis_index("x") + lax.axis_index("y")


def _per_core(fn):
    c = lax.axis_index("c")
    for cs in (0, 1):
        pl.when(c == cs)(functools.partial(fn, cs))


def gather_chips(name, arrs):
    nj = len(arrs)
    halves = [a.shape[0] // 2 for a in arrs]

    def body(*refs):
        _per_core(functools.partial(run, refs[:nj], refs[nj:2 * nj], *refs[2 * nj:]))

    def run(a_refs, o_refs, ici_send, ici_recv, d2d_send, d2d_recv, c):
        x, y, _ = _coords()
        me = 2 * x + y
        peers = [(1 - x, y), (x, 1 - y), (1 - x, 1 - y)]
        mine = [pl.ds(c * h, h) for h in halves]
        sibs = [pl.ds((1 - c) * h, h) for h in halves]

        def ici(k, j, blk):
            return pltpu.make_async_remote_copy(a_refs[j].at[mine[j]], o_refs[j].at[blk, mine[j]], ici_send.at[k * nj + j],
                                                ici_recv.at[k * nj + j], device_id=(*peers[k], c), device_id_type=MESH)

        def d2d(k, j, rows):
            blk = 2 * peers[k][0] + peers[k][1]
            return pltpu.make_async_remote_copy(o_refs[j].at[blk, rows[j]], o_refs[j].at[blk, rows[j]], d2d_send.at[k * nj + j],
                                                d2d_recv.at[k * nj + j], device_id=(x, y, 1 - c), device_id_type=MESH)

        sends = [ici(k, j, me) for k in range(3) for j in range(nj)]
        for cp in sends:
            cp.start()
        passed = []
        for k in range(3):
            for j in range(nj):
                ici(k, j, 2 * peers[k][0] + peers[k][1]).wait_recv()
                fw = d2d(k, j, mine)
                fw.start()
                passed.append(fw)
        for k in range(3):
            for j in range(nj):
                d2d(k, j, sibs).wait_recv()
        for cp in sends + passed:
            cp.wait_send()

    outs = pl.pallas_call(
        body, out_shape=[_sd((4,) + a.shape, a.dtype) for a in arrs], in_specs=[_ANY] * nj, out_specs=[_ANY] * nj,
        scratch_shapes=[pltpu.SemaphoreType.DMA((3 * nj,))] * 4,
        name=name, compiler_params=pltpu.CompilerParams(has_side_effects=True))(*arrs)
    return [lax.dynamic_update_index_in_dim(o, a, _chip(), 0) for o, a in zip(outs, arrs)]


def pair_split(name, arrs):
    nj = len(arrs)
    halves = [a.shape[1] // 2 for a in arrs]

    def body(*refs):
        _per_core(functools.partial(run, refs[:nj], refs[nj:2 * nj], *refs[2 * nj:]))

    def run(a_refs, got_refs, send_sems, recv_sems, c):
        x, y, _ = _coords()
        cps = [pltpu.make_async_remote_copy(a_refs[j].at[:, pl.ds((1 - c) * halves[j], halves[j])], got_refs[j],
                                            send_sems.at[j], recv_sems.at[j], device_id=(x, y, 1 - c), device_id_type=MESH)
               for j in range(nj)]
        for cp in cps:
            cp.start()
        for cp in cps:
            cp.wait()

    got = pl.pallas_call(
        body, out_shape=[_sd((4, h, a.shape[2]), a.dtype) for a, h in zip(arrs, halves)], in_specs=[_ANY] * nj,
        out_specs=[_ANY] * nj, scratch_shapes=[pltpu.SemaphoreType.DMA((nj,))] * 2,
        name=name, compiler_params=pltpu.CompilerParams(has_side_effects=True))(*arrs)
    own = [lax.dynamic_slice_in_dim(a, lax.axis_index("c") * h, h, axis=1) for a, h in zip(arrs, halves)]
    return own, got


def scatter_chips(name, arrs):
    nj = len(arrs)

    def body(*refs):
        a_refs, o_refs = refs[:nj], refs[nj:2 * nj]
        send_sems, recv_sems = refs[2 * nj:]
        x, y, c = _coords()
        me = 2 * x + y
        peers = [(1 - x, y), (x, 1 - y), (1 - x, 1 - y)]

        def cp(k, j, src_blk, dst_blk):
            return pltpu.make_async_remote_copy(a_refs[j].at[src_blk], o_refs[j].at[dst_blk], send_sems.at[k * nj + j],
                                                recv_sems.at[k * nj + j], device_id=(*peers[k], c), device_id_type=MESH)

        sends = [cp(k, j, 2 * peers[k][0] + peers[k][1], me) for k in range(3) for j in range(nj)]
        for s in sends:
            s.start()
        for k in range(3):
            for j in range(nj):
                cp(k, j, me, 2 * peers[k][0] + peers[k][1]).wait_recv()
        for s in sends:
            s.wait_send()

    outs = pl.pallas_call(
        body, out_shape=[_sd(a.shape, a.dtype) for a in arrs], in_specs=[_ANY] * nj, out_specs=[_ANY] * nj,
        scratch_shapes=[pltpu.SemaphoreType.DMA((3 * nj,))] * 2,
        name=name, compiler_params=pltpu.CompilerParams(has_side_effects=True))(*arrs)
    return [lax.dynamic_update_index_in_dim(o, lax.dynamic_index_in_dim(a, _chip(), 0, keepdims=False), _chip(), 0)
            for o, a in zip(outs, arrs)]


def pair_join(name, arrs):
    nj = len(arrs)
    halves = [a.shape[0] for a in arrs]

    def body(*refs):
        _per_core(functools.partial(run, refs[:nj], refs[nj:2 * nj], *refs[2 * nj:]))

    def run(a_refs, o_refs, send_sems, recv_sems, c):
        x, y, _ = _coords()

        def cp(j, rows_of):
            return pltpu.make_async_remote_copy(a_refs[j], o_refs[j].at[pl.ds(rows_of * halves[j], halves[j])], send_sems.at[j],
                                                recv_sems.at[j], device_id=(x, y, 1 - c), device_id_type=MESH)

        sends = [cp(j, c) for j in range(nj)]
        for s in sends:
            s.start()
        for s in sends:
            s.wait_send()
        for j in range(nj):
            cp(j, 1 - c).wait_recv()

    outs = pl.pallas_call(
        body, out_shape=[_sd((2 * a.shape[0], a.shape[1]), a.dtype) for a in arrs], in_specs=[_ANY] * nj, out_specs=[_ANY] * nj,
        scratch_shapes=[pltpu.SemaphoreType.DMA((nj,))] * 2,
        name=name, compiler_params=pltpu.CompilerParams(has_side_effects=True))(*arrs)
    return [lax.dynamic_update_slice_in_dim(o, a, lax.axis_index("c") * a.shape[0], axis=0) for o, a in zip(outs, arrs)]


def add_cast(name, a, b, dtype):
    k, r, c = a.shape
    tr = _pick(r, (512, 256, 128, 64, 32, 16, 8))

    def body(a_ref, b_ref, o_ref):
        o_ref[...] = (a_ref[...].astype(F32) + b_ref[...].astype(F32)).astype(dtype)

    spec = pl.BlockSpec((1, tr, c), lambda s, i: (s, i, 0))
    return pl.pallas_call(body, out_shape=_sd((k, r, c), dtype), grid=(k, r // tr), in_specs=[spec, spec], out_specs=spec,
                          name=name, compiler_params=_cp(("parallel", "parallel")))(a, b)


def gather_all(name, a):
    def body(a_ref, o_ref, send_sems, recv_sems, loc_sem):
        x, y, c = _coords()
        me = 4 * x + 2 * y + c
        flips = [(fx, fy, fc) for fx in (0, 1) for fy in (0, 1) for fc in (0, 1) if fx + fy + fc > 0]
        peers = [(x ^ fx, y ^ fy, c ^ fc) for fx, fy, fc in flips]
        mine = pltpu.make_async_copy(a_ref, o_ref.at[me], loc_sem)
        mine.start()
        sends = [pltpu.make_async_remote_copy(a_ref, o_ref.at[me], send_sems.at[k], recv_sems.at[k],
                                              device_id=p, device_id_type=MESH) for k, p in enumerate(peers)]
        for cp in sends:
            cp.start()
        for k, (px, py, pc) in enumerate(peers):
            pltpu.make_async_remote_copy(a_ref, o_ref.at[4 * px + 2 * py + pc], send_sems.at[k], recv_sems.at[k],
                                         device_id=(px, py, pc), device_id_type=MESH).wait_recv()
        for cp in sends:
            cp.wait_send()
        mine.wait()

    return pl.pallas_call(
        body, out_shape=_sd((8,) + a.shape, a.dtype), in_specs=[_ANY], out_specs=_ANY,
        scratch_shapes=[pltpu.SemaphoreType.DMA((7,)), pltpu.SemaphoreType.DMA((7,)), pltpu.SemaphoreType.DMA],
        name=name, compiler_params=pltpu.CompilerParams(has_side_effects=True))(a)


def sum_blocks(name, a):
    k, r, c = a.shape
    tr = _pick(r, (256, 128, 64, 32, 16, 8))

    def body(a_ref, o_ref):
        acc = a_ref[0].astype(F32)
        for s in range(1, k):
            acc = acc + a_ref[s].astype(F32)
        o_ref[...] = acc

    return pl.pallas_call(body, out_shape=_sd((r, c), F32), grid=(r // tr,),
                          in_specs=[pl.BlockSpec((k, tr, c), lambda i: (0, i, 0))], out_specs=pl.BlockSpec((tr, c), lambda i: (i, 0)),
                          name=name, compiler_params=_cp(("parallel",)))(a)


BIG = ("w_mod", "w_in", "w_mla_uq", "w_mla_ukv", "w_p_ssm", "w_p_swa", "w_p_mla", "w_out", "w_ffn_in", "w_ffn_out")
COL_SHARDED = ("w_mod", "w_in", "w_mla_uq", "w_mla_ukv", "w_ffn_in")
SMALL = ("c_ctx", "b_mod", "norm1_g", "norm2_g", "ssm_conv_w", "ssm_conv_b", "ssm_dt_bias", "ssm_a_log", "ssm_d",
         "ssm_norm_g", "swa_q_norm_g", "swa_k_norm_g", "swa_sink", "mla_q_lat_g", "mla_kv_lat_g", "mla_q_norm_g",
         "mla_k_norm_g")
WEIGHTS = ("c_ctx", "w_mod", "b_mod", "norm1_g", "norm2_g", "w_in", "ssm_conv_w", "ssm_conv_b", "ssm_dt_bias", "ssm_a_log",
           "ssm_d", "ssm_norm_g", "swa_q_norm_g", "swa_k_norm_g", "swa_sink", "mla_q_lat_g", "mla_kv_lat_g", "w_mla_uq",
           "w_mla_ukv", "mla_q_norm_g", "mla_k_norm_g", "w_p_ssm", "w_p_swa", "w_p_mla", "w_out", "w_ffn_in", "w_ffn_out")


def pack_w_in(w):
    z = lambda k: jnp.zeros((w.shape[0], k), w.dtype)
    return jnp.concatenate([w[:, 4832:7904], w[:, 2400:3424], w[:, 3424:4448], w[:, 0:1536], w[:, 1568:1824], w[:, 1824:2080],
                            w[:, 2080:2336], w[:, 2336:2400], w[:, 1536:1568], z(32), z(128), w[:, 4448:4832]], axis=1)


def unpack_w_in(g):
    return jnp.concatenate([g[:, 5120:6656], g[:, 7488:7520], g[:, 6656:6912], g[:, 6912:7168], g[:, 7168:7424], g[:, 7424:7488],
                            g[:, 3072:4096], g[:, 4096:5120], g[:, 7680:8064], g[:, 0:3072]], axis=1)


def pack_ukv(w):
    return w.reshape(MLA_KVRANK, MLA_H, 2, 128).transpose(0, 2, 1, 3).reshape(MLA_KVRANK, 2048)


def unpack_ukv(g):
    return g.reshape(MLA_KVRANK, 2, MLA_H, 128).transpose(0, 2, 1, 3).reshape(MLA_KVRANK, 2048)


def pack_uq(w):
    return jnp.pad(w.reshape(MLA_QRANK, MLA_H, 192), ((0, 0), (0, 0), (0, 64))).reshape(MLA_QRANK, 2048)


def unpack_uq(g):
    return g.reshape(MLA_QRANK, MLA_H, 256)[:, :, :192].reshape(MLA_QRANK, 1536)


def rope_tables(nlat):
    t = jnp.arange(nlat, dtype=jnp.int32)
    r = (t // GRID_W).astype(F32)[:, None]
    col = (t % GRID_W).astype(F32)[:, None]

    def tab(nf, pad):
        inv = jnp.power(ROPE_BASE, -jnp.arange(nf, dtype=F32) / nf)
        ar, ac = r * inv, col * inv
        cos = jnp.concatenate([jnp.cos(ar), jnp.cos(ar), jnp.cos(ac), jnp.cos(ac), jnp.ones((nlat, pad), F32)], axis=1)
        sin = jnp.concatenate([-jnp.sin(ar), jnp.sin(ar), -jnp.sin(ac), jnp.sin(ac), jnp.zeros((nlat, pad), F32)], axis=1)
        cos = jnp.concatenate([cos, jnp.ones((NCTX, 128), F32)], axis=0)
        sin = jnp.concatenate([sin, jnp.zeros((NCTX, 128), F32)], axis=0)
        return cos, sin

    return tab(32, 0), tab(16, 64)


def _lanes(v, start, width=128):
    return jnp.zeros((1, width), F32).at[0, start:start + v.shape[0]].set(v)


def layer_fwd(i, xin, h, mod, p, tabs, nlat):
    t = "l%d_" % i
    n = xin.shape[0]
    (cos_s, sin_s), (cos_m, sin_m) = tabs
    u = mm(h, p["w_in"], F32, t + "in_proj")
    xbc = conv_fwd(t + "conv", u, p["conv_w"], p["conv_b"], nlat)
    dtrow = jnp.transpose(u[:, C_MISC + DT_LANE:C_MISC + DT_LANE + 32])
    nlc = nlat // Q
    yf, hs_f = ssd_fwd(t + "ssd_f", xbc, u, dtrow, p["bias_c"], p["alog_c"], p["bias_r"], p["alog_r"], nlc, False, 0)
    yb, hs_b = ssd_fwd(t + "ssd_b", xbc, u, dtrow, p["bias_c"], p["alog_c"], p["bias_r"], p["alog_r"], nlc, True, 1)
    ys = ssd_out_fwd(t + "ssd_out", yf, yb, xbc, u, p["ssm_norm_g"], p["d_exp"])
    qs, ks = swa_prep_fwd(t + "swa_prep", u, p["swa_q_g"], p["swa_k_g"], cos_s, sin_s)
    o_swa, lse_swa = swa_attention_fwd(t + "swa", qs, ks, u, p["sink"], p["swa_cfg"], nlat)
    ckv_n, cq_n = lat_norm_fwd(t + "lat_norm", u, p["kv_lat_g"], p["q_lat_g"])
    kv = mm(ckv_n, p["w_ukv"], F32, t + "ukv")
    qp = mm(cq_n, p["w_uq"], F32, t + "uq")
    km, qm, vm = mla_prep_fwd(t + "mla_prep", kv, qp, u, p["mla_q_g"], p["mla_k_g"], cos_m, sin_m)
    o_mla, lse_mla = mla_fwd(t + "mla_fwd", qm, km, vm, nlat)
    p1 = mm(ys, p["w_p_ssm"], BF16, t + "p_ssm")
    p2 = mm(o_swa, p["w_p_swa"], BF16, t + "p_swa")
    p3 = mm(o_mla, p["w_p_mla"], BF16, t + "p_mla")
    merged = merge_fwd(t + "merge", u, p1, p2, p3)
    o = mm(merged, p["w_out"], F32, t + "out_proj")
    x1, h2 = resid_mod_fwd(t + "res1", xin, o, mod, 2, mod, 3, 4, p["norm2_g"], nlat // RT)
    gu = mm(h2, p["w_ffn_in"], BF16, t + "ffn_in")
    a = swiglu_fwd(t + "swiglu", gu)
    f = mm(a, p["w_ffn_out"], F32, t + "ffn_out")
    saved = dict(xin=xin, h=h, u=u, xbc=xbc, dtrow=dtrow, yf=yf, yb=yb, hs_f=hs_f, hs_b=hs_b, ys=ys, qs=qs, ks=ks,
                 o_swa=o_swa, lse_swa=lse_swa, ckv_n=ckv_n, cq_n=cq_n, kv=kv, qp=qp, km=km, qm=qm, vm=vm, o_mla=o_mla,
                 lse_mla=lse_mla, p1=p1, p2=p2, p3=p3, merged=merged, o=o, x1=x1, h2=h2, gu=gu, a=a, f=f)
    del n
    return x1, f, saved


def layer_bwd(i, dx2, df, dgt2, sv, mod, p, tabs, nlat):
    t = "l%db_" % i
    (cos_s, sin_s), (cos_m, sin_m) = tabs
    g = {}
    nt = nlat // RT
    nlc = nlat // Q
    g["w_ffn_out"] = mm_tn(sv["a"], df, t + "wg_ffn_out")
    da = mm(df, p["w_ffn_out"], F32, t + "dg_ffn_out", trans_b=True)
    dgu = swiglu_bwd(t + "swiglu", sv["gu"], da)
    g["w_ffn_in"] = mm_tn(sv["h2"], dgu, t + "wg_ffn_in")
    dh2 = mm(dgu, p["w_ffn_in"], F32, t + "dg_ffn_in", trans_b=True)
    dx1, do, dgt1, dsh2, dsc2, g["norm2_g"] = resid_mod_bwd(t + "res1", sv["x1"], dx2, dh2, sv["o"], mod, 2, mod, 3, 4,
                                                              p["norm2_g"], nt)
    g["w_out"] = mm_tn(sv["merged"], do, t + "wg_out")
    dmerged = mm(do, p["w_out"], F32, t + "dg_out", trans_b=True)
    dp1, dp2, dp3, dgates = merge_bwd(t + "merge", sv["u"], sv["p1"], sv["p2"], sv["p3"], dmerged)
    g["w_p_ssm"] = mm_tn(sv["ys"], dp1, t + "wg_p_ssm")
    g["w_p_swa"] = mm_tn(sv["o_swa"], dp2, t + "wg_p_swa")
    g["w_p_mla"] = mm_tn(sv["o_mla"], dp3, t + "wg_p_mla")
    dys = mm(dp1, p["w_p_ssm"], F32, t + "dg_p_ssm", trans_b=True)
    do_swa = mm(dp2, p["w_p_swa"], BF16, t + "dg_p_swa", trans_b=True)
    do_mla = mm(dp3, p["w_p_mla"], BF16, t + "dg_p_mla", trans_b=True)
    dqm, dkm, dv_mla = mla_attention_bwd(t + "mla", sv["qm"], sv["km"], sv["vm"], sv["o_mla"], do_mla, sv["lse_mla"], nlat)
    dkv, dqp, dkr, g["mla_q_g"], g["mla_k_g"] = mla_prep_bwd(t + "mla_prep", sv["kv"], sv["qp"], sv["u"], p["mla_q_g"],
                                                             p["mla_k_g"], cos_m, sin_m, dkm, dqm, dv_mla)
    g["w_ukv"] = mm_tn(sv["ckv_n"], dkv, t + "wg_ukv")
    g["w_uq"] = mm_tn(sv["cq_n"], dqp, t + "wg_uq")
    dckv_n = mm(dkv, p["w_ukv"], F32, t + "dg_ukv", trans_b=True)
    dcq_n = mm(dqp, p["w_uq"], F32, t + "dg_uq", trans_b=True)
    dckv, dcq, g["kv_lat_g"], g["q_lat_g"] = lat_norm_bwd(t + "lat_norm", sv["u"], p["kv_lat_g"], p["q_lat_g"], dckv_n, dcq_n)
    dqs, dks, dv_swa, g["sink"] = swa_attention_bwd(t + "swa", sv["qs"], sv["ks"], sv["u"], sv["o_swa"], do_swa, sv["lse_swa"],
                                                p["sink"], p["swa_cfg"], nlat)
    dq, dk, dv, g["swa_q_g"], g["swa_k_g"] = swa_prep_bwd(t + "swa_prep", sv["u"], p["swa_q_g"], p["swa_k_g"], cos_s, sin_s,
                                                          dqs, dks, dv_swa)
    dy, dxs_skip, dz, g["ssm_norm_g"], g["d_exp"] = ssd_out_bwd(t + "ssd_out", sv["yf"], sv["yb"], sv["xbc"], sv["u"],
                                                                 p["ssm_norm_g"], p["d_exp"], dys)
    n = dy.shape[0]
    zbc = jnp.zeros((n, 256), F32)
    r_f = ssd_bwd(t + "ssd_f", sv["xbc"], sv["u"], sv["dtrow"], p["bias_c"], p["alog_c"], p["bias_r"], p["alog_r"],
                  sv["hs_f"], dy, (dxs_skip, zbc, zbc), nlc, False, 0)
    r_b = ssd_bwd(t + "ssd_b", sv["xbc"], sv["u"], sv["dtrow"], p["bias_c"], p["alog_c"], p["bias_r"], p["alog_r"],
                  sv["hs_b"], dy, (r_f[0], r_f[1], r_f[2]), nlc, True, 1)
    cv = [conv_bwd(t + "conv_" + nm, sv["u"], r_b[k], p["conv_w"], p["conv_b"], nlat, c0)
          for k, (nm, c0) in enumerate((("x", 0), ("b", 1024), ("c", 1280)))]
    dxbc = [r[0] for r in cv]
    g["conv_w"] = jnp.concatenate([r[1] for r in cv], axis=1)
    g["conv_b"] = jnp.concatenate([r[2] for r in cv], axis=1)
    drow = jnp.concatenate([r_f[4][0] + r_f[4][1], r_b[4][0] + r_b[4][1]], axis=0)
    drow_t = jnp.pad(jnp.transpose(drow), ((0, 0), (DT_LANE, 128 - DT_LANE - 32)))
    dmisc = misc_combine(t + "misc", dkr, r_f[3], r_b[3], drow_t)
    g["bias_c"] = r_f[5] + r_b[5]
    g["alog_c"] = r_f[6] + r_b[6]
    g["bias_r"] = jnp.concatenate([r_f[7], r_b[7]], axis=0)
    g["alog_r"] = jnp.concatenate([r_f[8], r_b[8]], axis=0)
    du = jnp.concatenate([dgates, dz, dq, *dxbc, dk, dv, dckv, dmisc, jnp.zeros((n, 128), BF16), dcq], axis=1)
    g["w_in"] = mm_tn(sv["h"], du, t + "wg_in")
    dh = mm(du, p["w_in"], F32, t + "dg_in", trans_b=True)
    g["mod"] = (dgt1, dsh2, dsc2, dgt2)
    return dx1, dh, g


def local_step(x, c, ctx, target, c_ctx, W, nlat):
    xin = jnp.concatenate([x, ctx], axis=0)
    n = xin.shape[0]
    nt = nlat // RT
    tabs = rope_tables(nlat)
    c8 = jnp.zeros((8, D), F32).at[0].set(c[0]).at[1].set(c_ctx)
    mods, silus = [], []
    for i in range(DEPTH):
        m8, s8 = mod_fwd("l%d_mod" % i, c8, W[i]["w_mod"], W[i]["b_mod"])
        mods.append(m8[0:2].reshape(2, 1, 6 * D))
        silus.append(s8)
    saved = []
    _, h = resid_mod_fwd("l0_norm1", xin, None, None, 0, mods[0], 0, 1, W[0]["norm1_g"], nt)
    xcur = xin
    for i in range(DEPTH):
        x1, f, sv = layer_fwd(i, xcur, h, mods[i], W[i], tabs, nlat)
        saved.append(sv)
        if i + 1 < DEPTH:
            xcur, h = resid_mod_fwd("l%d_res2" % i, x1, f, mods[i], 5, mods[i + 1], 0, 1, W[i + 1]["norm1_g"], nt)
    loss_v, dx2, df, dgt2 = resid_loss("loss", x1, f, mods[DEPTH - 1], 5, target, nt)
    grads = [None] * DEPTH
    for i in reversed(range(DEPTH)):
        dx1, dh, g = layer_bwd(i, dx2, df, dgt2, saved[i], mods[i], W[i], tabs, nlat)
        if i > 0:
            sv = saved[i]
            dx2, df, dgt2, dsh1, dsc1, g["norm1_g"] = resid_mod_bwd(
                "l%db_res2" % (i - 1), sv["xin"], dx1, dh, saved[i - 1]["f"], mods[i - 1], 5, mods[i], 0, 1,
                W[i]["norm1_g"], nt)
        else:
            dxin, _, _, dsh1, dsc1, g["norm1_g"] = resid_mod_bwd("l0b_norm1", saved[0]["xin"], dx1, dh, None, None, 0,
                                                                  mods[0], 0, 1, W[0]["norm1_g"], nt)
        dgt1, dsh2, dsc2, dgt2_i = g.pop("mod")
        dmod = jnp.concatenate([dsh1, dsc1, dgt1, dsh2, dsc2, dgt2_i], axis=2).reshape(2, 6 * D)
        dmod8 = jnp.zeros((8, 6 * D), F32).at[0:2].set(dmod)
        g["w_mod"] = mm_tn(silus[i], dmod8, "l%db_wg_mod" % i)
        dsilu = mm(dmod8, W[i]["w_mod"], F32, "l%db_dg_mod" % i, trans_b=True)
        dc8, g["b_mod"] = mod_small_bwd("l%db_mod_small" % i, c8, dsilu, dmod8)
        g["c8"] = dc8
        grads[i] = g
    del n
    return loss_v[0, 0], dxin, grads


def _full_from_chips(k, a):
    if k in COL_SHARDED:
        return a.transpose(1, 2, 0, 3).reshape(2, a.shape[2], 4 * a.shape[3])
    return a.transpose(1, 0, 2, 3).reshape(2, 4 * a.shape[2], a.shape[3])


def _chips_from_full(k, a):
    if k in COL_SHARDED:
        return a.reshape(a.shape[0], 4, a.shape[1] // 4).transpose(1, 0, 2)
    return a.reshape(4, a.shape[0] // 4, a.shape[1])


def _small_sizes():
    return dict(c_ctx=1024, b_mod=2 * 6144, norm1_g=2048, norm2_g=2048, ssm_conv_w=2 * 5 * 1536, ssm_conv_b=2 * 1536,
                ssm_dt_bias=64, ssm_a_log=64, ssm_d=32, ssm_norm_g=2048, swa_q_norm_g=256, swa_k_norm_g=256, swa_sink=16,
                mla_q_lat_g=768, mla_kv_lat_g=512, mla_q_norm_g=384, mla_k_norm_g=384)


def _pack_small(d):
    parts = []
    for k in SMALL:
        v = d[k].astype(F32).reshape(-1)
        parts.append(jnp.pad(v, (0, (-v.shape[0]) % 1024)))
    return jnp.concatenate(parts).reshape(-1, 128)


def _unpack_small(buf, shapes):
    flat = buf.reshape(-1)
    out = {}
    o = 0
    for k in SMALL:
        sz = _small_sizes()[k]
        out[k] = flat[o:o + sz].reshape(shapes[k])
        o += sz + (-sz) % 1024
    return out


def big_grads(grads):
    gfull = {k: [] for k in BIG}
    for i in range(DEPTH):
        g = grads[i]
        gfull["w_mod"].append(g["w_mod"])
        gfull["w_in"].append(unpack_w_in(g["w_in"]))
        gfull["w_mla_uq"].append(unpack_uq(g["w_uq"]))
        gfull["w_mla_ukv"].append(unpack_ukv(g["w_ukv"]))
        for k in ("w_p_ssm", "w_p_swa", "w_p_mla", "w_out", "w_ffn_in", "w_ffn_out"):
            gfull[k].append(g[k])
    return gfull


def small_grads(grads):
    gs = {}
    gs["c_ctx"] = sum(grads[i]["c8"][1] for i in range(DEPTH))
    st = lambda f: jnp.stack([f(grads[i]) for i in range(DEPTH)])
    gs["b_mod"] = st(lambda g: g["b_mod"][0])
    gs["norm1_g"] = st(lambda g: g["norm1_g"][0])
    gs["norm2_g"] = st(lambda g: g["norm2_g"][0])
    gs["ssm_conv_w"] = st(lambda g: g["conv_w"])
    gs["ssm_conv_b"] = st(lambda g: g["conv_b"][0])
    gs["ssm_dt_bias"] = st(lambda g: (g["bias_c"][0, DT_LANE:DT_LANE + 32] + g["bias_r"][:, 0]).reshape(2, 16))
    gs["ssm_a_log"] = st(lambda g: (g["alog_c"][0, DT_LANE:DT_LANE + 32] + g["alog_r"][:, 0]).reshape(2, 16))
    gs["ssm_d"] = st(lambda g: g["d_exp"].reshape(16, 64).sum(axis=1))
    gs["ssm_norm_g"] = st(lambda g: g["ssm_norm_g"][0])
    gs["swa_q_norm_g"] = st(lambda g: g["swa_q_g"][0])
    gs["swa_k_norm_g"] = st(lambda g: g["swa_k_g"][0])
    gs["swa_sink"] = st(lambda g: g["sink"][:, 0, 0])
    gs["mla_q_lat_g"] = st(lambda g: g["q_lat_g"][0])
    gs["mla_kv_lat_g"] = st(lambda g: g["kv_lat_g"][0])
    gs["mla_q_norm_g"] = st(lambda g: g["mla_q_g"][0, :192])
    gs["mla_k_norm_g"] = st(lambda g: g["mla_k_g"][0, :192])
    return gs


def layer_params(i, full, conv_full, sm, nlat):
    p = {}
    p["w_mod"] = full["w_mod"][i]
    p["w_in"] = pack_w_in(full["w_in"][i])
    p["w_uq"] = pack_uq(full["w_mla_uq"][i])
    p["w_ukv"] = pack_ukv(full["w_mla_ukv"][i])
    for k in ("w_p_ssm", "w_p_swa", "w_p_mla", "w_out", "w_ffn_in", "w_ffn_out"):
        p[k] = full[k][i]
    p["b_mod"] = sm["b_mod"][i][None]
    p["norm1_g"] = sm["norm1_g"][i][None]
    p["norm2_g"] = sm["norm2_g"][i][None]
    p["conv_w"] = conv_full[i]
    p["conv_b"] = sm["ssm_conv_b"][i][None]
    bias = sm["ssm_dt_bias"][i].reshape(32)
    alog = sm["ssm_a_log"][i].reshape(32)
    p["bias_c"] = _lanes(bias, DT_LANE)
    p["alog_c"] = _lanes(alog, DT_LANE)
    p["bias_r"] = bias[:, None]
    p["alog_r"] = alog[:, None]
    p["d_exp"] = jnp.repeat(sm["ssm_d"][i], 64)[None]
    p["ssm_norm_g"] = sm["ssm_norm_g"][i][None]
    p["swa_q_g"] = sm["swa_q_norm_g"][i][None]
    p["swa_k_g"] = sm["swa_k_norm_g"][i][None]
    p["sink"] = jnp.broadcast_to(sm["swa_sink"][i][:, None, None], (SWA_HQ, 1, 128))
    p["q_lat_g"] = sm["mla_q_lat_g"][i][None]
    p["kv_lat_g"] = sm["mla_kv_lat_g"][i][None]
    p["mla_q_g"] = _lanes(sm["mla_q_norm_g"][i], 0, 256)
    p["mla_k_g"] = _lanes(sm["mla_k_norm_g"][i], 0, 256)
    p["swa_cfg"] = dict(w=128, vw=128, hq=SWA_HQ, grp=SWA_HQ // SWA_HKV, vcol0=C_V // 128, scale=SWA_DH ** -0.5,
                        tq=256, tk=256, band=True)
    return p


def kernel(x, c, ctx, c_ctx, w_mod, b_mod, norm1_g, norm2_g, w_in, ssm_conv_w, ssm_conv_b, ssm_dt_bias, ssm_a_log, ssm_d, ssm_norm_g, swa_q_norm_g, swa_k_norm_g, swa_sink, mla_q_lat_g, mla_kv_lat_g, w_mla_uq, w_mla_ukv, mla_q_norm_g, mla_k_norm_g, w_p_ssm, w_p_swa, w_p_mla, w_out, w_ffn_in, w_ffn_out, loss_target, m_c_ctx, m_w_mod, m_b_mod, m_norm1_g, m_norm2_g, m_w_in, m_ssm_conv_w, m_ssm_conv_b, m_ssm_dt_bias, m_ssm_a_log, m_ssm_d, m_ssm_norm_g, m_swa_q_norm_g, m_swa_k_norm_g, m_swa_sink, m_mla_q_lat_g, m_mla_kv_lat_g, m_w_mla_uq, m_w_mla_ukv, m_mla_q_norm_g, m_mla_k_norm_g, m_w_p_ssm, m_w_p_swa, m_w_p_mla, m_w_out, m_w_ffn_in, m_w_ffn_out, v_c_ctx, v_w_mod, v_b_mod, v_norm1_g, v_norm2_g, v_w_in, v_ssm_conv_w, v_ssm_conv_b, v_ssm_dt_bias, v_ssm_a_log, v_ssm_d, v_ssm_norm_g, v_swa_q_norm_g, v_swa_k_norm_g, v_swa_sink, v_mla_q_lat_g, v_mla_kv_lat_g, v_w_mla_uq, v_w_mla_ukv, v_mla_q_norm_g, v_mla_k_norm_g, v_w_p_ssm, v_w_p_swa, v_w_p_mla, v_w_out, v_w_ffn_in, v_w_ffn_out):
    loc = dict(locals())
    w = {k: loc[k] for k in WEIGHTS}
    m = {k: loc["m_" + k] for k in WEIGHTS}
    v = {k: loc["v_" + k] for k in WEIGHTS}
    nlat = x.shape[1]

    sh2 = {k: (w[k].shape[0] * w[k].shape[1], w[k].shape[2]) for k in BIG}
    conv_sh = jnp.pad(ssm_conv_w.reshape(10, 384), ((0, 6), (0, 0)))
    gathered = gather_chips("gather_weights", [w[k].astype(BF16).reshape(sh2[k]) for k in BIG] + [conv_sh])
    full = {k: _full_from_chips(k, g.reshape((4,) + w[k].shape)) for k, g in zip(BIG, gathered)}
    conv_full = gathered[-1][:, :10].reshape(4, 2, 5, 384).transpose(1, 2, 0, 3).reshape(2, 5, 1536)

    W = [layer_params(i, full, conv_full, w, nlat) for i in range(DEPTH)]

    loss_loc, dx, grads = local_step(x[0], c, ctx[0], loss_target[0], c_ctx, W, nlat)

    gfull = big_grads(grads)
    by_chip = {k: jnp.stack([_chips_from_full(k, a) for a in gfull[k]], axis=1) for k in BIG}
    send = [by_chip[k].astype(BF16).reshape((4,) + sh2[k]) for k in BIG]
    own, got = pair_split("pair_split", send)
    pair = [add_cast("pair_sum_" + k, o, g, BF16) for k, o, g in zip(BIG, own, got)]
    recv = scatter_chips("scatter_grads", pair)
    mine = [sum_blocks("sum_chips_" + k, r) for k, r in zip(BIG, recv)]
    gbig = {k: g.reshape(w[k].shape) for k, g in zip(BIG, pair_join("join_cores", mine))}

    gs = small_grads(grads)
    small_all = gather_all("gather_small", _pack_small(gs))
    small_sum = sum_blocks("sum_small", small_all)
    full_shapes = {k: (w[k].shape if k != "ssm_conv_w" else (2, 5, 1536)) for k in SMALL}
    gsmall = _unpack_small(small_sum, full_shapes)
    chip = 2 * lax.axis_index("x") + lax.axis_index("y")
    gsmall["ssm_conv_w"] = lax.dynamic_slice_in_dim(gsmall["ssm_conv_w"], chip * 384, 384, axis=2)

    grad = {**gbig, **gsmall}
    delta, new_m, new_v = {}, {}, {}
    sm = {k: _pack_small_local(d) for k, d in (("w", w), ("g", grad), ("m", m), ("v", v))}
    r = adamw("adamw_small", sm["w"], sm["g"], sm["m"], sm["v"])
    shapes = {k: w[k].shape for k in SMALL}
    for dst, buf in zip((delta, new_m, new_v), r):
        dst.update(_unpack_small_local(buf, shapes))
    for k in BIG:
        sh = w[k].shape
        r = adamw("adamw_" + k, *[a[k].reshape(sh[0] * sh[1], sh[2]) for a in (w, grad, m, v)])
        for dst, buf in zip((delta, new_m, new_v), r):
            dst[k] = buf.reshape(sh)

    loss = lax.psum(loss_loc, ("x", "y", "c"))
    return (loss, dx[None, :nlat], *[grad[k] for k in WEIGHTS], *[delta[k] for k in WEIGHTS],
            *[new_m[k] for k in WEIGHTS], *[new_v[k] for k in WEIGHTS])


def _pack_small_local(d):
    parts = []
    for k in SMALL:
        a = d[k].astype(F32).reshape(-1)
        parts.append(jnp.pad(a, (0, (-a.shape[0]) % 1024)))
    return jnp.concatenate(parts).reshape(-1, 128)


def _unpack_small_local(buf, shapes):
    flat = buf.reshape(-1)
    out = {}
    o = 0
    for k in SMALL:
        sz = math.prod(shapes[k])
        out[k] = flat[o:o + sz].reshape(shapes[k])
        o += sz + (-sz) % 1024
    return out
```

```python
import functools
import math

import jax
import jax.numpy as jnp
from jax import lax
from jax.experimental import pallas as pl
from jax.experimental.pallas import tpu as pltpu

F32 = jnp.float32
BF16 = jnp.bfloat16
MESH = pl.DeviceIdType.MESH

D = 1024
NCTX = 256
EPS = 1e-6
ROPE_BASE = 10000.0
GRID_W = 64
DEPTH = 2
Q = 128
N_HEADS_SSM = 16
SWA_HQ, SWA_HKV, SWA_DH, SWA_WIN = 8, 2, 128, 128
MLA_H, MLA_NOPE, MLA_ROPE, MLA_V = 8, 128, 64, 128
MLA_QRANK, MLA_KVRANK = 384, 256
FFN = 2816
RT = 256
VMEM_LIMIT = 56 << 20
NEG = -1e30
LOG2E = 1.4426950408889634

C_G1, C_G2, C_G3, C_Z, C_Q, C_XS, C_B, C_C, C_K, C_V, C_CKV, C_MISC, C_PAD, C_CQ = (
    0, 1024, 2048, 3072, 4096, 5120, 6144, 6400, 6656, 6912, 7168, 7424, 7552, 7680)
UW = 8064
DT_LANE = 64

ADAM_LR, ADAM_B1, ADAM_B2, ADAM_EPS, ADAM_WD, ADAM_STEP = 0.001, 0.9, 0.999, 1e-08, 0.01, 10


def _cp(sem):
    return pltpu.CompilerParams(dimension_semantics=sem, vmem_limit_bytes=VMEM_LIMIT)


def _pick(n, cands):
    for c in cands:
        if n % c == 0:
            return c
    return n


_TN = (1536, 1408, 1152, 1024, 896, 768, 512, 384, 256, 128)


def mm(a, b, out_dtype, name, trans_b=False):
    m, k = a.shape
    n = b.shape[0] if trans_b else b.shape[1]
    tm = _pick(m, (1408, 768, 512, 256, 128, 8))
    tn = _pick(n, _TN)
    tk = k if k <= 2048 else _pick(k, (1408, 1152, 1024, 896, 768, 512))
    nk = k // tk
    b_spec = (pl.BlockSpec((tn, tk), lambda i, j, kk: (j, kk)) if trans_b
              else pl.BlockSpec((tk, tn), lambda i, j, kk: (kk, j)))

    def body(a_ref, b_ref, o_ref, *acc):
        p = _d(a_ref[...], b_ref[...], ((1,), (1 if trans_b else 0,)))
        if nk == 1:
            o_ref[...] = p.astype(out_dtype)
        else:
            kk = pl.program_id(2)

            @pl.when(kk == 0)
            def _():
                acc[0][...] = p

            @pl.when(kk > 0)
            def _():
                acc[0][...] += p

            @pl.when(kk == nk - 1)
            def _():
                o_ref[...] = acc[0][...].astype(out_dtype)

    return pl.pallas_call(
        body, out_shape=jax.ShapeDtypeStruct((m, n), out_dtype), grid=(m // tm, n // tn, nk),
        in_specs=[pl.BlockSpec((tm, tk), lambda i, j, kk: (i, kk)), b_spec],
        out_specs=pl.BlockSpec((tm, tn), lambda i, j, kk: (i, j)),
        scratch_shapes=[] if nk == 1 else [pltpu.VMEM((tm, tn), F32)],
        name=name, compiler_params=_cp(("parallel", "parallel", "arbitrary")))(a, b)


def mm_tn(a, b, name, out_dtype=BF16):
    t, ka = a.shape
    _, nb = b.shape
    ta = _pick(ka, (1024, 1408, 768, 512, 384, 256, 128))
    tb = _pick(nb, _TN)
    tt = _pick(t, (1408, 768, 512, 256, 128, 8))
    nt = t // tt

    def body(a_ref, b_ref, o_ref, acc):
        p = _d(a_ref[...], b_ref[...], ((0,), (0,)))
        s = pl.program_id(2)

        @pl.when(s == 0)
        def _():
            acc[...] = p

        @pl.when(s > 0)
        def _():
            acc[...] += p

        @pl.when(s == nt - 1)
        def _():
            o_ref[...] = acc[...].astype(out_dtype)

    return pl.pallas_call(
        body, out_shape=jax.ShapeDtypeStruct((ka, nb), out_dtype), grid=(ka // ta, nb // tb, nt),
        in_specs=[pl.BlockSpec((tt, ta), lambda i, j, s: (s, i)), pl.BlockSpec((tt, tb), lambda i, j, s: (s, j))],
        out_specs=pl.BlockSpec((ta, tb), lambda i, j, s: (i, j)), scratch_shapes=[pltpu.VMEM((ta, tb), F32)],
        name=name, compiler_params=_cp(("parallel", "parallel", "arbitrary")))(a, b)


def _rms(x, g, n=None):
    n = x.shape[-1] if n is None else n
    r = lax.rsqrt(jnp.sum(x * x, axis=-1, keepdims=True) * (1.0 / n) + EPS)
    return x * r * g


def _silu(x):
    return x * jax.nn.sigmoid(x)


def _modulate(x, g, sc, sh):
    return _rms(x, g) * (1.0 + sc) + sh


def _swap(x, s):
    ax = x.ndim - 1
    w = x.shape[ax]
    lane = lax.broadcasted_iota(jnp.int32, x.shape, ax)
    lo = (lane & s) == 0
    return jnp.where(lo, pltpu.roll(x, w - s, ax), pltpu.roll(x, s, ax))


@functools.partial(jax.custom_vjp, nondiff_argnums=(3,))
def _rope(x, cos, sin, s):
    return x * cos + _swap(x, s) * sin


def _rope_fwd(x, cos, sin, s):
    return _rope(x, cos, sin, s), (cos, sin)


def _rope_bwd(s, res, g):
    cos, sin = res
    return g * cos - _swap(g, s) * sin, jnp.zeros_like(cos), jnp.zeros_like(sin)


_rope.defvjp(_rope_fwd, _rope_bwd)


@jax.custom_vjp
def _softplus(x):
    return jnp.maximum(x, 0.0) + jnp.log(1.0 + jnp.exp(-jnp.abs(x)))


def _softplus_fwd(x):
    return _softplus(x), x


def _softplus_bwd(x, g):
    return (g * jax.nn.sigmoid(x),)


_softplus.defvjp(_softplus_fwd, _softplus_bwd)


def _d(a, b, dims):
    return lax.dot_general(a.astype(BF16), b.astype(BF16), (dims, ((), ())), preferred_element_type=F32)


@jax.custom_vjp
def bdot(a, b):
    return _d(a, b, ((1,), (0,)))


bdot.defvjp(lambda a, b: (bdot(a, b), (a, b)),
            lambda r, g: (_d(g, r[1], ((1,), (1,))), _d(r[0], g, ((0,), (0,)))))


@jax.custom_vjp
def bdot_nt(a, b):
    return _d(a, b, ((1,), (1,)))


bdot_nt.defvjp(lambda a, b: (bdot_nt(a, b), (a, b)),
               lambda r, g: (_d(g, r[1], ((1,), (0,))), _d(g, r[0], ((0,), (0,)))))


@jax.custom_vjp
def bdot_tn(a, b):
    return _d(a, b, ((0,), (0,)))


bdot_tn.defvjp(lambda a, b: (bdot_tn(a, b), (a, b)),
               lambda r, g: (_d(r[1], g, ((1,), (1,))), _d(r[0], g, ((1,), (0,)))))


def _tri(rev):
    i = lax.broadcasted_iota(jnp.int32, (Q, Q), 0)
    j = lax.broadcasted_iota(jnp.int32, (Q, Q), 1)
    return (i <= j) if rev else (i >= j)


def _split3(a):
    hi = a.astype(BF16)
    r = a - hi.astype(F32)
    mid = r.astype(BF16)
    lo = (r - mid.astype(F32)).astype(BF16)
    return hi, mid, lo


def _cum_cols_impl(a, rev):
    t = _tri(rev).astype(BF16)
    return sum(jnp.dot(t, p, preferred_element_type=F32) for p in _split3(a))


def _cum_rows_impl(a, rev):
    t = _tri(not rev).astype(BF16)
    return sum(jnp.dot(p, t, preferred_element_type=F32) for p in _split3(a))


@functools.partial(jax.custom_vjp, nondiff_argnums=(1,))
def cum_cols(a, rev):
    return _cum_cols_impl(a, rev)


cum_cols.defvjp(lambda a, rev: (_cum_cols_impl(a, rev), None), lambda rev, _, g: (_cum_cols_impl(g, not rev),))


@functools.partial(jax.custom_vjp, nondiff_argnums=(1,))
def cum_rows(a, rev):
    return _cum_rows_impl(a, rev)


cum_rows.defvjp(lambda a, rev: (_cum_rows_impl(a, rev), None), lambda rev, _, g: (_cum_rows_impl(g, not rev),))


def _rs(w, cb=0):
    return pl.BlockSpec((RT, w), lambda i: (i, cb))


def _ps(shape):
    nd = len(shape)
    return pl.BlockSpec(shape, lambda i: (0,) * nd)


def _gs(w, cb, nlat):
    return pl.BlockSpec((1, 1, w), lambda i: (i // nlat, 0, cb))


def _rowcall(name, body, n, ins, outs, scratch=()):
    return pl.pallas_call(
        body, out_shape=[o[0] for o in outs], grid=(n // RT,), in_specs=[s for _, s in ins],
        out_specs=[s for _, s in outs], scratch_shapes=list(scratch), name=name,
        compiler_params=_cp(("arbitrary",)))(*[a for a, _ in ins])


def _acc(ref, val, first):
    @pl.when(first)
    def _():
        ref[...] = val

    @pl.when(jnp.logical_not(first))
    def _():
        ref[...] += val


def _sd(shape, dt):
    return jax.ShapeDtypeStruct(shape, dt)


def resid_mod_fwd(name, xp, o, mod_gt, gt_i, mod_n, sh_i, sc_i, norm_g, nlat):
    n = xp.shape[0]
    has_res = o is not None

    def body(*refs):
        if has_res:
            xp_ref, o_ref, gt_ref, sh_ref, sc_ref, g_ref, xn_ref, h_ref = refs
            xn = xp_ref[...] + gt_ref[0] * o_ref[...]
            xn_ref[...] = xn
        else:
            xp_ref, sh_ref, sc_ref, g_ref, h_ref = refs
            xn = xp_ref[...]
        h_ref[...] = _modulate(xn, g_ref[...], sc_ref[0], sh_ref[0]).astype(BF16)

    ins = [(xp, _rs(D))]
    if has_res:
        ins += [(o, _rs(D)), (mod_gt, _gs(D, gt_i, nlat))]
    ins += [(mod_n, _gs(D, sh_i, nlat)), (mod_n, _gs(D, sc_i, nlat)), (norm_g, _ps((1, D)))]
    outs = ([(_sd((n, D), F32), _rs(D))] if has_res else []) + [(_sd((n, D), BF16), _rs(D))]
    r = _rowcall(name, body, n, ins, outs)
    return (r[0], r[1]) if has_res else (xp, r[0])


def resid_mod_bwd(name, xn, dxn, dh, o, mod_gt, gt_i, mod_n, sh_i, sc_i, norm_g, nlat):
    n = xn.shape[0]
    has_res = o is not None

    def body(*refs):
        i = pl.program_id(0)
        if has_res:
            (xn_ref, dxn_ref, dh_ref, o_ref, gt_ref, sh_ref, sc_ref, g_ref,
             dx_ref, do_ref, dgt_ref, dsh_ref, dsc_ref, dg_ref) = refs
        else:
            xn_ref, dxn_ref, dh_ref, sh_ref, sc_ref, g_ref, dx_ref, dsh_ref, dsc_ref, dg_ref = refs
        _, vjp = jax.vjp(_modulate, xn_ref[...], g_ref[...], sc_ref[0], sh_ref[0])
        dx, dg, dsc, dsh = vjp(dh_ref[...])
        dx = dx + dxn_ref[...]
        dx_ref[...] = dx
        gfirst = (i == 0) | (i == nlat)
        _acc(dg_ref, dg, i == 0)
        _acc(dsh_ref, dsh[None], gfirst)
        _acc(dsc_ref, dsc[None], gfirst)
        if has_res:
            do_ref[...] = (gt_ref[0] * dx).astype(BF16)
            _acc(dgt_ref, jnp.sum(dx * o_ref[...], axis=0, keepdims=True)[None], gfirst)

    ins = [(xn, _rs(D)), (dxn, _rs(D)), (dh, _rs(D))]
    if has_res:
        ins += [(o, _rs(D)), (mod_gt, _gs(D, gt_i, nlat))]
    ins += [(mod_n, _gs(D, sh_i, nlat)), (mod_n, _gs(D, sc_i, nlat)), (norm_g, _ps((1, D)))]
    gacc = (_sd((2, 1, D), F32), _gs(D, 0, nlat))
    outs = [(_sd((n, D), F32), _rs(D))]
    if has_res:
        outs += [(_sd((n, D), BF16), _rs(D)), gacc]
    outs += [gacc, gacc, (_sd((1, D), F32), _ps((1, D)))]
    r = _rowcall(name, body, n, ins, outs)
    if has_res:
        return r
    return r[0], None, None, r[1], r[2], r[3]


def resid_loss(name, xp, o, mod_gt, gt_i, target, nlat):
    n = xp.shape[0]

    def body(xp_ref, o_ref, gt_ref, t_ref, loss_ref, dx_ref, do_ref, dgt_ref):
        i = pl.program_id(0)
        gt = gt_ref[0]

        @pl.when(i < nlat)
        def _():
            err = xp_ref[...] + gt * o_ref[...] - t_ref[...]
            dx = err * (1.0 / D)
            dx_ref[...] = dx
            do_ref[...] = (gt * dx).astype(BF16)
            _acc(loss_ref, jnp.full((1, 128), 0.5 / D, F32) * jnp.sum(err * err), i == 0)
            _acc(dgt_ref, jnp.sum(dx * o_ref[...], axis=0, keepdims=True)[None], i == 0)

        @pl.when(i >= nlat)
        def _():
            dx_ref[...] = jnp.zeros((RT, D), F32)
            do_ref[...] = jnp.zeros((RT, D), BF16)
            dgt_ref[...] = jnp.zeros((1, 1, D), F32)

    tgt_spec = pl.BlockSpec((RT, D), lambda i: (jnp.minimum(i, nlat - 1), 0))
    ins = [(xp, _rs(D)), (o, _rs(D)), (mod_gt, _gs(D, gt_i, nlat)), (target, tgt_spec)]
    outs = [(_sd((1, 128), F32), _ps((1, 128))), (_sd((n, D), F32), _rs(D)), (_sd((n, D), BF16), _rs(D)),
            (_sd((2, 1, D), F32), _gs(D, 0, nlat))]
    return _rowcall(name, body, n, ins, outs)


def mod_fwd(name, c8, w_mod, b_mod):
    tn = 1536

    def body(c_ref, w_ref, b_ref, o_ref, s_ref):
        s = _silu(c_ref[...]).astype(BF16)
        s_ref[...] = s
        o_ref[...] = jnp.dot(s, w_ref[...], preferred_element_type=F32) + b_ref[...]

    return pl.pallas_call(
        body, out_shape=[_sd((8, 6 * D), F32), _sd((8, D), BF16)], grid=(6 * D // tn,),
        in_specs=[pl.BlockSpec((8, D), lambda j: (0, 0)), pl.BlockSpec((D, tn), lambda j: (0, j)),
                  pl.BlockSpec((1, tn), lambda j: (0, j))],
        out_specs=[pl.BlockSpec((8, tn), lambda j: (0, j)), pl.BlockSpec((8, D), lambda j: (0, 0))],
        name=name, compiler_params=_cp(("arbitrary",)))(c8, w_mod, b_mod)


def mod_small_bwd(name, c8, dsilu, dmod8):
    def body(c_ref, ds_ref, dm_ref, dc_ref, db_ref):
        _, vjp = jax.vjp(_silu, c_ref[...])
        dc_ref[...] = vjp(ds_ref[...])[0]
        db_ref[...] = jnp.sum(dm_ref[...], axis=0, keepdims=True)

    return pl.pallas_call(
        body, out_shape=[_sd((8, D), F32), _sd((1, 6 * D), F32)], grid=(1,),
        in_specs=[pl.BlockSpec((8, D), lambda j: (0, 0)), pl.BlockSpec((8, D), lambda j: (0, 0)),
                  pl.BlockSpec((8, 6 * D), lambda j: (0, 0))],
        out_specs=[pl.BlockSpec((8, D), lambda j: (0, 0)), pl.BlockSpec((1, 6 * D), lambda j: (0, 0))],
        name=name, compiler_params=_cp(("arbitrary",)))(c8, dsilu, dmod8)


def _conv_taps(x, nlat):
    n = x.shape[0]
    r = lax.broadcasted_iota(jnp.int32, x.shape, 0)
    lo = jnp.where(r < nlat, 0, nlat)
    hi = jnp.where(r < nlat, nlat, n)
    taps = []
    for o in (-2, -1, 0, 1, 2):
        xs = x if o == 0 else pltpu.roll(x, (-o) % n, 0)
        t = r + o
        taps.append(jnp.where((t >= lo) & (t < hi), xs, 0.0))
    return taps


def conv_fwd(name, u, w, b, nlat_rows):
    n = u.shape[0]

    def body(x_ref, w_ref, b_ref, o_ref):
        taps = _conv_taps(x_ref[...], nlat_rows)
        wv = w_ref[...]
        pre = b_ref[...] + sum(taps[k] * wv[k:k + 1, :] for k in range(5))
        o_ref[...] = _silu(pre)

    return pl.pallas_call(
        body, out_shape=_sd((n, 1536), F32), grid=(12,),
        in_specs=[pl.BlockSpec((n, 128), lambda j: (0, C_XS // 128 + j)), pl.BlockSpec((5, 128), lambda j: (0, j)),
                  pl.BlockSpec((1, 128), lambda j: (0, j))],
        out_specs=pl.BlockSpec((n, 128), lambda j: (0, j)),
        name=name, compiler_params=_cp(("parallel",)))(u, w, b)


def conv_bwd(name, u, dact, w, b, nlat_rows, chan0):
    n, nch = dact.shape
    t0 = chan0 // 128

    def body(x_ref, da_ref, w_ref, b_ref, dx_ref, dw_ref, db_ref):
        taps = _conv_taps(x_ref[...], nlat_rows)
        wv = w_ref[...]
        pre = b_ref[...] + sum(taps[k] * wv[k:k + 1, :] for k in range(5))
        s = jax.nn.sigmoid(pre)
        dpre = da_ref[...] * (s * (1.0 + pre * (1.0 - s)))
        db_ref[...] = jnp.sum(dpre, axis=0, keepdims=True)
        rows = lax.broadcasted_iota(jnp.int32, (5, 128), 0)
        dw = jnp.zeros((5, 128), F32)
        for k in range(5):
            dw = dw + jnp.where(rows == k, jnp.sum(dpre * taps[k], axis=0, keepdims=True), 0.0)
        dw_ref[...] = dw
        r = lax.broadcasted_iota(jnp.int32, dpre.shape, 0)
        lo = jnp.where(r < nlat_rows, 0, nlat_rows)
        hi = jnp.where(r < nlat_rows, nlat_rows, n)
        dx = jnp.zeros_like(dpre)
        for k in range(5):
            o = k - 2
            ds = dpre if o == 0 else pltpu.roll(dpre, o % n, 0)
            t = r - o
            dx = dx + jnp.where((t >= lo) & (t < hi), ds, 0.0) * wv[k:k + 1, :]
        dx_ref[...] = dx.astype(BF16)

    return pl.pallas_call(
        body, out_shape=[_sd((n, nch), BF16), _sd((5, nch), F32), _sd((1, nch), F32)], grid=(nch // 128,),
        in_specs=[pl.BlockSpec((n, 128), lambda j: (0, C_XS // 128 + t0 + j)), pl.BlockSpec((n, 128), lambda j: (0, j)),
                  pl.BlockSpec((5, 128), lambda j: (0, t0 + j)), pl.BlockSpec((1, 128), lambda j: (0, t0 + j))],
        out_specs=[pl.BlockSpec((n, 128), lambda j: (0, j)), pl.BlockSpec((5, 128), lambda j: (0, j)),
                   pl.BlockSpec((1, 128), lambda j: (0, j))],
        name=name, compiler_params=_cp(("parallel",)))(u, dact, w, b)


def _ssd_chunk(rev, dirn, g, x4, bm, cm, misc, dtrow, bias_c, alog_c, bias_r, alog_r, h4):
    dt_c = _softplus(misc + bias_c)
    a_c = dt_c * (-jnp.exp(alog_c))
    dt_r = _softplus(dtrow + bias_r)
    a_r = dt_r * (-jnp.exp(alog_r))
    cs_c = cum_cols(a_c, rev)
    cs_r = cum_rows(a_r, rev)
    tot_c = jnp.sum(a_c, axis=0, keepdims=True)
    cb = bdot_nt(cm, bm)
    tri = _tri(rev)
    lane = lax.broadcasted_iota(jnp.int32, (1, 128), 1)
    row16 = lax.broadcasted_iota(jnp.int32, (16, 1), 0)
    prow = lax.broadcasted_iota(jnp.int32, (128, 1), 0)
    ys, hs = [], []
    for p in range(4):
        ydiag = 0.0
        wst = 0.0
        eoff = 0.0
        hscale = 0.0
        for e in range(2):
            hg = 8 * g + 2 * p + e
            oh_c = (lane == DT_LANE + 16 * dirn + hg).astype(F32)
            dt_h = jnp.sum(dt_c * oh_c, axis=1, keepdims=True)
            cs_h = jnp.sum(cs_c * oh_c, axis=1, keepdims=True)
            tot_h = jnp.sum(tot_c * oh_c, axis=1, keepdims=True)
            csr_h = jnp.sum(cs_r * (row16 == hg).astype(F32), axis=0, keepdims=True)
            seg = jnp.exp(jnp.where(tri, cs_h - csr_h, -jnp.inf))
            hm = ((lane < 64) if e == 0 else (lane >= 64)).astype(F32)
            ydiag = ydiag + bdot(cb * seg, x4[p] * (dt_h * hm))
            wst = wst + (dt_h * jnp.exp(tot_h - cs_h)) * hm
            eoff = eoff + jnp.exp(cs_h) * hm
            hscale = hscale + jnp.exp(tot_h) * ((prow < 64) if e == 0 else (prow >= 64)).astype(F32)
        ys.append(ydiag + bdot_nt(cm, h4[p]) * eoff)
        hs.append(h4[p] * hscale + bdot_tn(x4[p] * wst, bm))
    return ys, hs


def _ssd_specs(nlat_chunks, rev, dirn, bwd):
    nc = nlat_chunks + 2

    def chunk(s):
        if bwd:
            s = nc - 1 - s
        return (nlat_chunks + 1 - s) if rev else (s + nlat_chunks) % nc

    def step(s):
        return (nc - 1 - s) if bwd else s

    return dict(
        x=pl.BlockSpec((Q, 512), lambda g, s: (chunk(s), g)),
        b=pl.BlockSpec((Q, 128), lambda g, s: (chunk(s), 8 + g)),
        c=pl.BlockSpec((Q, 128), lambda g, s: (chunk(s), 10 + g)),
        misc=pl.BlockSpec((Q, 128), lambda g, s: (chunk(s), C_MISC // 128)),
        dtrow=pl.BlockSpec((16, Q), lambda g, s: (dirn, chunk(s))),
        p_c=pl.BlockSpec((1, 128), lambda g, s: (0, 0)),
        p_r=pl.BlockSpec((16, 1), lambda g, s: (dirn, 0)),
        y=pl.BlockSpec((Q, 512), lambda g, s: (chunk(s), g)),
        hsave=pl.BlockSpec((1, 1, 512, 128), lambda g, s: (g, step(s), 0, 0)),
        bc_out=pl.BlockSpec((Q, 128), lambda g, s: (chunk(s), g)),
        misc_out=pl.BlockSpec((1, Q, 128), lambda g, s: (g, chunk(s), 0)),
        dtrow_out=pl.BlockSpec((1, 16, Q), lambda g, s: (g, 0, chunk(s))),
        pacc_c=pl.BlockSpec((1, 128), lambda g, s: (0, 0)),
        pacc_r=pl.BlockSpec((16, 1), lambda g, s: (0, 0)),
    )


def ssd_fwd(name, xbc, u, dtrow, bias_c, alog_c, bias_r, alog_r, nlat_chunks, rev, dirn):
    n = xbc.shape[0]
    nc = nlat_chunks + 2
    sp = _ssd_specs(nlat_chunks, rev, dirn, False)

    def body(x_ref, b_ref, c_ref, m_ref, r_ref, bc_ref, ac_ref, br_ref, ar_ref, y_ref, hs_ref, h_s):
        g = pl.program_id(0)
        s = pl.program_id(1)

        @pl.when(s == 0)
        def _():
            h_s[...] = jnp.zeros((512, 128), F32)

        hs_ref[0, 0] = h_s[...]
        x4 = [x_ref[:, 128 * p:128 * p + 128] for p in range(4)]
        h4 = [h_s[128 * p:128 * p + 128, :] for p in range(4)]
        ys, hs = _ssd_chunk(rev, dirn, g, x4, b_ref[...], c_ref[...], m_ref[...], r_ref[...],
                            bc_ref[...], ac_ref[...], br_ref[...], ar_ref[...], h4)
        for p in range(4):
            y_ref[:, 128 * p:128 * p + 128] = ys[p]
            h_s[128 * p:128 * p + 128, :] = hs[p]

    return pl.pallas_call(
        body, out_shape=[_sd((n, 1024), F32), _sd((2, nc, 512, 128), F32)], grid=(2, nc),
        in_specs=[sp["x"], sp["b"], sp["c"], sp["misc"], sp["dtrow"], sp["p_c"], sp["p_c"], sp["p_r"], sp["p_r"]],
        out_specs=[sp["y"], sp["hsave"]], scratch_shapes=[pltpu.VMEM((512, 128), F32)],
        name=name, compiler_params=_cp(("arbitrary", "arbitrary")))(
            xbc, xbc, xbc, u, dtrow, bias_c, alog_c, bias_r, alog_r)


def ssd_bwd(name, xbc, u, dtrow, bias_c, alog_c, bias_r, alog_r, hsave, dy, acc, nlat_chunks, rev, dirn):
    n = xbc.shape[0]
    sp = _ssd_specs(nlat_chunks, rev, dirn, True)

    def body(x_ref, b_ref, c_ref, m_ref, r_ref, bc_ref, ac_ref, br_ref, ar_ref, hs_ref, dy_ref, ax_ref, ab_ref, acc_ref,
             dx_ref, db_ref, dc_ref, dm_ref, dr_ref, dbc_ref, dac_ref, dbr_ref, dar_ref, dh_s):
        g = pl.program_id(0)
        s = pl.program_id(1)

        @pl.when(s == 0)
        def _():
            dh_s[...] = jnp.zeros((512, 128), F32)

        x4 = [x_ref[:, 128 * p:128 * p + 128] for p in range(4)]
        h4 = [hs_ref[0, 0, 128 * p:128 * p + 128, :] for p in range(4)]
        fn = functools.partial(_ssd_chunk, rev, dirn, g)
        _, vjp = jax.vjp(fn, x4, b_ref[...], c_ref[...], m_ref[...], r_ref[...],
                         bc_ref[...], ac_ref[...], br_ref[...], ar_ref[...], h4)
        dys = [dy_ref[:, 128 * p:128 * p + 128] for p in range(4)]
        dhs = [dh_s[128 * p:128 * p + 128, :] for p in range(4)]
        dx4, db, dc, dm, dr, dbc, dac, dbr, dar, dh4 = vjp((dys, dhs))
        for p in range(4):
            dx_ref[:, 128 * p:128 * p + 128] = dx4[p] + ax_ref[:, 128 * p:128 * p + 128]
            dh_s[128 * p:128 * p + 128, :] = dh4[p]
        db_ref[...] = db + ab_ref[...]
        dc_ref[...] = dc + acc_ref[...]
        dm_ref[0] = dm
        dr_ref[0] = dr
        first = (g == 0) & (s == 0)
        _acc(dbc_ref, dbc, first)
        _acc(dac_ref, dac, first)
        _acc(dbr_ref, dbr, first)
        _acc(dar_ref, dar, first)

    ax, ab, ac = acc
    return pl.pallas_call(
        body,
        out_shape=[_sd((n, 1024), F32), _sd((n, 256), F32), _sd((n, 256), F32), _sd((2, n, 128), F32),
                   _sd((2, 16, n), F32), _sd((1, 128), F32), _sd((1, 128), F32), _sd((16, 1), F32), _sd((16, 1), F32)],
        grid=(2, nlat_chunks + 2),
        in_specs=[sp["x"], sp["b"], sp["c"], sp["misc"], sp["dtrow"], sp["p_c"], sp["p_c"], sp["p_r"], sp["p_r"],
                  sp["hsave"], sp["y"], sp["y"], sp["bc_out"], sp["bc_out"]],
        out_specs=[sp["y"], sp["bc_out"], sp["bc_out"], sp["misc_out"], sp["dtrow_out"],
                   sp["pacc_c"], sp["pacc_c"], sp["pacc_r"], sp["pacc_r"]],
        scratch_shapes=[pltpu.VMEM((512, 128), F32)],
        name=name, compiler_params=_cp(("arbitrary", "arbitrary")))(
            xbc, xbc, xbc, u, dtrow, bias_c, alog_c, bias_r, alog_r, hsave, dy, ax, ab, ac)


def _ssd_out(yf, yb, xs, z, g, dexp):
    return _rms((yf + yb + dexp * xs) * _silu(z), g)


def ssd_out_fwd(name, yf, yb, xbc, u, g, dexp):
    n = yf.shape[0]

    def body(yf_ref, yb_ref, xs_ref, z_ref, g_ref, d_ref, o_ref):
        o_ref[...] = _ssd_out(yf_ref[...], yb_ref[...], xs_ref[...], z_ref[...], g_ref[...], d_ref[...]).astype(BF16)

    return _rowcall(name, body, n,
                    [(yf, _rs(D)), (yb, _rs(D)), (xbc, _rs(D, 0)), (u, _rs(D, C_Z // D)), (g, _ps((1, D))), (dexp, _ps((1, D)))],
                    [(_sd((n, D), BF16), _rs(D))])[0]


def ssd_out_bwd(name, yf, yb, xbc, u, g, dexp, dys):
    n = yf.shape[0]

    def body(yf_ref, yb_ref, xs_ref, z_ref, g_ref, d_ref, dys_ref, dy_ref, dxs_ref, dz_ref, dg_ref, dd_ref):
        i = pl.program_id(0)
        _, vjp = jax.vjp(_ssd_out, yf_ref[...], yb_ref[...], xs_ref[...], z_ref[...], g_ref[...], d_ref[...])
        dyf, _, dxs, dz, dg, dd = vjp(dys_ref[...])
        dy_ref[...] = dyf
        dxs_ref[...] = dxs
        dz_ref[...] = dz.astype(BF16)
        _acc(dg_ref, dg, i == 0)
        _acc(dd_ref, dd, i == 0)

    return _rowcall(name, body, n,
                    [(yf, _rs(D)), (yb, _rs(D)), (xbc, _rs(D, 0)), (u, _rs(D, C_Z // D)), (g, _ps((1, D))), (dexp, _ps((1, D))),
                     (dys, _rs(D))],
                    [(_sd((n, D), F32), _rs(D)), (_sd((n, D), F32), _rs(D)), (_sd((n, D), BF16), _rs(D)),
                     (_sd((1, D), F32), _ps((1, D))), (_sd((1, D), F32), _ps((1, D)))])


def _normrope(x, g, cos, sin, s, n=None):
    return _rope(_rms(x, g, n), cos, sin, s)


def swa_prep_fwd(name, u, gq, gk, cos, sin):
    n = u.shape[0]

    def body(q_ref, k_ref, gq_ref, gk_ref, cos_ref, sin_ref, qs_ref, ks_ref):
        cs, sn = cos_ref[...], sin_ref[...]
        for h in range(SWA_HQ):
            sl = slice(128 * h, 128 * h + 128)
            qs_ref[:, sl] = _normrope(q_ref[:, sl], gq_ref[...], cs, sn, 32).astype(BF16)
        for h in range(SWA_HKV):
            sl = slice(128 * h, 128 * h + 128)
            ks_ref[:, sl] = _normrope(k_ref[:, sl], gk_ref[...], cs, sn, 32).astype(BF16)

    return _rowcall(name, body, n,
                    [(u, _rs(1024, C_Q // 1024)), (u, _rs(256, C_K // 256)), (gq, _ps((1, 128))), (gk, _ps((1, 128))),
                     (cos, _rs(128)), (sin, _rs(128))],
                    [(_sd((n, 1024), BF16), _rs(1024)), (_sd((n, 256), BF16), _rs(256))])


def swa_prep_bwd(name, u, gq, gk, cos, sin, dqs, dks, dv):
    n = u.shape[0]

    def body(q_ref, k_ref, gq_ref, gk_ref, cos_ref, sin_ref, dqs_ref, dks_ref, dv_ref,
             dq_ref, dk_ref, dvo_ref, dgq_ref, dgk_ref):
        i = pl.program_id(0)
        cs, sn = cos_ref[...], sin_ref[...]
        fn = lambda x, g: _normrope(x, g, cs, sn, 32)
        dgq = jnp.zeros((1, 128), F32)
        dgk = jnp.zeros((1, 128), F32)
        for h in range(SWA_HQ):
            sl = slice(128 * h, 128 * h + 128)
            _, vjp = jax.vjp(fn, q_ref[:, sl], gq_ref[...])
            dx, dg = vjp(dqs_ref[:, sl])
            dq_ref[:, sl] = dx.astype(BF16)
            dgq = dgq + dg
        for h in range(SWA_HKV):
            sl = slice(128 * h, 128 * h + 128)
            _, vjp = jax.vjp(fn, k_ref[:, sl], gk_ref[...])
            dx, dg = vjp(dks_ref[:, sl])
            dk_ref[:, sl] = dx.astype(BF16)
            dgk = dgk + dg
        dvo_ref[...] = dv_ref[...].astype(BF16)
        _acc(dgq_ref, dgq, i == 0)
        _acc(dgk_ref, dgk, i == 0)

    return _rowcall(name, body, n,
                    [(u, _rs(1024, C_Q // 1024)), (u, _rs(256, C_K // 256)), (gq, _ps((1, 128))), (gk, _ps((1, 128))),
                     (cos, _rs(128)), (sin, _rs(128)), (dqs, _rs(1024)), (dks, _rs(256)), (dv, _rs(256))],
                    [(_sd((n, 1024), BF16), _rs(1024)), (_sd((n, 256), BF16), _rs(256)), (_sd((n, 256), BF16), _rs(256)),
                     (_sd((1, 128), F32), _ps((1, 128))), (_sd((1, 128), F32), _ps((1, 128)))])


def lat_norm_fwd(name, u, g_kv, g_q):
    n = u.shape[0]

    def body(ckv_ref, cq_ref, gkv_ref, gq_ref, okv_ref, oq_ref):
        okv_ref[...] = _rms(ckv_ref[...], gkv_ref[...]).astype(BF16)
        oq_ref[...] = _rms(cq_ref[...], gq_ref[...]).astype(BF16)

    return _rowcall(name, body, n,
                    [(u, _rs(256, C_CKV // 256)), (u, _rs(384, C_CQ // 384)), (g_kv, _ps((1, 256))), (g_q, _ps((1, 384)))],
                    [(_sd((n, 256), BF16), _rs(256)), (_sd((n, 384), BF16), _rs(384))])


def lat_norm_bwd(name, u, g_kv, g_q, dkvn, dqn):
    n = u.shape[0]

    def body(ckv_ref, cq_ref, gkv_ref, gq_ref, dkvn_ref, dqn_ref, dckv_ref, dcq_ref, dgkv_ref, dgq_ref):
        i = pl.program_id(0)
        _, vjp = jax.vjp(_rms, ckv_ref[...], gkv_ref[...])
        dx, dg = vjp(dkvn_ref[...])
        dckv_ref[...] = dx.astype(BF16)
        _acc(dgkv_ref, dg, i == 0)
        _, vjp = jax.vjp(_rms, cq_ref[...], gq_ref[...])
        dx, dg = vjp(dqn_ref[...])
        dcq_ref[...] = dx.astype(BF16)
        _acc(dgq_ref, dg, i == 0)

    return _rowcall(name, body, n,
                    [(u, _rs(256, C_CKV // 256)), (u, _rs(384, C_CQ // 384)), (g_kv, _ps((1, 256))), (g_q, _ps((1, 384))),
                     (dkvn, _rs(256)), (dqn, _rs(384))],
                    [(_sd((n, 256), BF16), _rs(256)), (_sd((n, 384), BF16), _rs(384)),
                     (_sd((1, 256), F32), _ps((1, 256))), (_sd((1, 384), F32), _ps((1, 384)))])


def _lane_lt64(x):
    return (lax.broadcasted_iota(jnp.int32, (1, 128), 1) < 64).astype(F32) * x


def _mla_krope(misc, g, cos, sin):
    return _normrope(_lane_lt64(misc), g, cos, sin, 16, MLA_ROPE)


def mla_prep_fwd(name, kv, qp, u, qg, kg, cos, sin):
    n = kv.shape[0]

    def body(kv_ref, v_ref, q_ref, m_ref, qg_ref, kg_ref, cos_ref, sin_ref, km_ref, qm_ref, vm_ref):
        cs, sn = cos_ref[...], sin_ref[...]
        vm_ref[...] = v_ref[...].astype(BF16)
        kr = _mla_krope(m_ref[...], kg_ref[:, 128:256], cs, sn).astype(BF16)
        for h in range(MLA_H):
            km_ref[:, 256 * h:256 * h + 128] = _rms(kv_ref[:, 128 * h:128 * h + 128], kg_ref[:, 0:128]).astype(BF16)
            km_ref[:, 256 * h + 128:256 * h + 256] = kr
            qm_ref[:, 256 * h:256 * h + 128] = _rms(q_ref[:, 256 * h:256 * h + 128], qg_ref[:, 0:128]).astype(BF16)
            qm_ref[:, 256 * h + 128:256 * h + 256] = _normrope(
                q_ref[:, 256 * h + 128:256 * h + 256], qg_ref[:, 128:256], cs, sn, 16, MLA_ROPE).astype(BF16)

    return _rowcall(name, body, n,
                    [(kv, _rs(1024, 0)), (kv, _rs(1024, 1)), (qp, _rs(2048)), (u, _rs(128, C_MISC // 128)), (qg, _ps((1, 256))),
                     (kg, _ps((1, 256))), (cos, _rs(128)), (sin, _rs(128))],
                    [(_sd((n, 2048), BF16), _rs(2048)), (_sd((n, 2048), BF16), _rs(2048)), (_sd((n, 1024), BF16), _rs(1024))])


def mla_prep_bwd(name, kv, qp, u, qg, kg, cos, sin, dkm, dqm, dv):
    n = kv.shape[0]

    def body(kv_ref, q_ref, m_ref, qg_ref, kg_ref, cos_ref, sin_ref, dkm_ref, dqm_ref, dv_ref,
             dkv_ref, dq_ref, dkr_ref, dqg_ref, dkg_ref):
        i = pl.program_id(0)
        cs, sn = cos_ref[...], sin_ref[...]
        fr = lambda x, g: _normrope(x, g, cs, sn, 16, MLA_ROPE)
        dkg_n = jnp.zeros((1, 128), F32)
        dqg_n = jnp.zeros((1, 128), F32)
        dqg_r = jnp.zeros((1, 128), F32)
        dkr_sum = jnp.zeros((RT, 128), F32)
        for h in range(MLA_H):
            dk_h = jnp.transpose(dkm_ref[256 * h:256 * h + 256, :])
            dkv_ref[:, 1024 + 128 * h:1024 + 128 * h + 128] = jnp.transpose(dv_ref[128 * h:128 * h + 128, :]).astype(BF16)
            _, vjp = jax.vjp(_rms, kv_ref[:, 128 * h:128 * h + 128], kg_ref[:, 0:128])
            dx, dg = vjp(dk_h[:, 0:128])
            dkv_ref[:, 128 * h:128 * h + 128] = dx.astype(BF16)
            dkg_n = dkg_n + dg
            dkr_sum = dkr_sum + dk_h[:, 128:256]
            _, vjp = jax.vjp(_rms, q_ref[:, 256 * h:256 * h + 128], qg_ref[:, 0:128])
            dx, dg = vjp(dqm_ref[:, 256 * h:256 * h + 128])
            dq_ref[:, 256 * h:256 * h + 128] = dx.astype(BF16)
            dqg_n = dqg_n + dg
            _, vjp = jax.vjp(fr, q_ref[:, 256 * h + 128:256 * h + 256], qg_ref[:, 128:256])
            dx, dg = vjp(dqm_ref[:, 256 * h + 128:256 * h + 256])
            dq_ref[:, 256 * h + 128:256 * h + 256] = dx.astype(BF16)
            dqg_r = dqg_r + dg
        _, vjp = jax.vjp(lambda m, g: _mla_krope(m, g, cs, sn), m_ref[...], kg_ref[:, 128:256])
        dm, dkg_r = vjp(dkr_sum)
        dkr_ref[...] = dm
        _acc(dqg_ref.at[:, 0:128], dqg_n, i == 0)
        _acc(dqg_ref.at[:, 128:256], dqg_r, i == 0)
        _acc(dkg_ref.at[:, 0:128], dkg_n, i == 0)
        _acc(dkg_ref.at[:, 128:256], dkg_r, i == 0)

    return _rowcall(name, body, n,
                    [(kv, _rs(1024, 0)), (qp, _rs(2048)), (u, _rs(128, C_MISC // 128)), (qg, _ps((1, 256))), (kg, _ps((1, 256))),
                     (cos, _rs(128)), (sin, _rs(128)), (dkm, pl.BlockSpec((2048, RT), lambda i: (0, i))), (dqm, _rs(2048)),
                     (dv, pl.BlockSpec((1024, RT), lambda i: (0, i)))],
                    [(_sd((n, 2048), BF16), _rs(2048)), (_sd((n, 2048), BF16), _rs(2048)), (_sd((n, 128), F32), _rs(128)),
                     (_sd((1, 256), F32), _ps((1, 256))), (_sd((1, 256), F32), _ps((1, 256)))])


def misc_combine(name, dkr, dm_f, dm_b, drow_t):
    n = dkr.shape[0]

    def body(a_ref, f_ref, b_ref, r_ref, o_ref):
        o_ref[...] = (a_ref[...] + f_ref[0] + f_ref[1] + b_ref[0] + b_ref[1] + r_ref[...]).astype(BF16)

    g2 = pl.BlockSpec((2, RT, 128), lambda i: (0, i, 0))
    return _rowcall(name, body, n, [(dkr, _rs(128)), (dm_f, g2), (dm_b, g2), (drow_t, _rs(128))],
                    [(_sd((n, 128), BF16), _rs(128))])[0]


def _f32(ref):
    return ref[...].astype(F32)


def _merge(g1, g2, g3, p1, p2, p3):
    return jax.nn.sigmoid(g1) * p1 + jax.nn.sigmoid(g2) * p2 + jax.nn.sigmoid(g3) * p3


def merge_fwd(name, u, p1, p2, p3):
    n = u.shape[0]

    def body(g1, g2, g3, a, b, c, o_ref):
        o_ref[...] = _merge(g1[...], g2[...], g3[...], _f32(a), _f32(b), _f32(c)).astype(BF16)

    return _rowcall(name, body, n, [(u, _rs(D, 0)), (u, _rs(D, 1)), (u, _rs(D, 2)), (p1, _rs(D)), (p2, _rs(D)), (p3, _rs(D))],
                    [(_sd((n, D), BF16), _rs(D))])[0]


def merge_bwd(name, u, p1, p2, p3, dm):
    n = u.shape[0]

    def body(g1, g2, g3, a, b, c, dm_ref, d1, d2, d3, dg_ref):
        _, vjp = jax.vjp(_merge, g1[...], g2[...], g3[...], _f32(a), _f32(b), _f32(c))
        r = vjp(dm_ref[...])
        for k in range(3):
            dg_ref[:, D * k:D * k + D] = r[k].astype(BF16)
        d1[...] = r[3].astype(BF16)
        d2[...] = r[4].astype(BF16)
        d3[...] = r[5].astype(BF16)

    return _rowcall(name, body, n,
                    [(u, _rs(D, 0)), (u, _rs(D, 1)), (u, _rs(D, 2)), (p1, _rs(D)), (p2, _rs(D)), (p3, _rs(D)), (dm, _rs(D))],
                    [(_sd((n, D), BF16), _rs(D))] * 3 + [(_sd((n, 3 * D), BF16), _rs(3 * D))])


def _swiglu(g, u):
    return _silu(g) * u


def swiglu_fwd(name, gu):
    n = gu.shape[0]

    def body(g_ref, u_ref, o_ref):
        o_ref[...] = _swiglu(_f32(g_ref), _f32(u_ref)).astype(BF16)

    return _rowcall(name, body, n, [(gu, _rs(FFN, 0)), (gu, _rs(FFN, 1))], [(_sd((n, FFN), BF16), _rs(FFN))])[0]


def swiglu_bwd(name, gu, da):
    n = gu.shape[0]

    def body(g_ref, u_ref, da_ref, o_ref):
        _, vjp = jax.vjp(_swiglu, _f32(g_ref), _f32(u_ref))
        dg, du = vjp(da_ref[...])
        o_ref[:, 0:FFN] = dg.astype(BF16)
        o_ref[:, FFN:2 * FFN] = du.astype(BF16)

    return _rowcall(name, body, n, [(gu, _rs(FFN, 0)), (gu, _rs(FFN, 1)), (da, _rs(FFN))],
                    [(_sd((n, 2 * FFN), BF16), _rs(2 * FFN))])[0]


FLASH_ROWS = 256


def _fold_lanes(x, op):
    acc = x[:, 0:128]
    for b in range(1, x.shape[1] // 128):
        acc = op(acc, x[:, 128 * b:128 * b + 128])
    return acc


def _band_mask(tq, tk, i, kb):
    qp = i * tq + lax.broadcasted_iota(jnp.int32, (tq, tk), 0)
    kp = kb * tk + lax.broadcasted_iota(jnp.int32, (tq, tk), 1)
    return jnp.abs(qp - kp) <= SWA_WIN


def flash_fwd(name, qa, ka, va, *, w, vw, hq, grp, vcol0, scale, nlat, tq, tk, band, sink, ctx_q, prev=None):
    n = qa.shape[0]
    cblk = nlat // NCTX
    band = band and not ctx_q
    assert not band, "latent rows of a banded attention go through swa_fwd_lat"
    if ctx_q:
        tq = tk = NCTX
        grid = (hq, 1, 1)
        qmap = lambda h, i, kk: (cblk, h)
        kmap = lambda h, i, kk: (cblk, h // grp)
        vmap = lambda h, i, kk: (cblk, vcol0 + h // grp)
        omap = lambda h, i, kk: (cblk, h)
        lmap = lambda h, i, kk: (h, cblk, 0)
    else:
        nb = nlat // tk
        nk = 3 if band else nb
        grid = (hq, nlat // tq, nk)
        kb_of = (lambda i, kk: jnp.clip(i + kk - 1, 0, nb - 1)) if band else (lambda i, kk: kk)
        qmap = lambda h, i, kk: (i, h)
        kmap = lambda h, i, kk: (kb_of(i, kk), h // grp)
        vmap = lambda h, i, kk: (kb_of(i, kk), vcol0 + h // grp)
        omap = lambda h, i, kk: (i, h)
        lmap = lambda h, i, kk: (h, i, 0)
    nk = grid[2]
    extra = not ctx_q
    has_sink = sink is not None

    def body(*refs):
        refs = list(refs)
        q_ref, k_ref, v_ref = refs[:3]
        pos = 3
        if extra:
            ke_ref, ve_ref = refs[pos:pos + 2]
            pos += 2
        if has_sink:
            s_ref = refs[pos]
            pos += 1
        if prev is not None:
            pos += 2
        o_ref, l_ref, m_s, l_s, a_s = refs[pos:pos + 5]
        kk = pl.program_id(2)
        tr = min(tq, FLASH_ROWS)

        def step(kblk, vblk):
            for r in range(tq // tr):
                rows = slice(r * tr, (r + 1) * tr)
                s = _d(q_ref[rows, :], kblk, ((1,), (1,))) * (scale * LOG2E)
                m_prev = m_s[rows, :]
                m_new = jnp.maximum(m_prev, jnp.max(_fold_lanes(s, jnp.maximum), axis=1, keepdims=True))
                alpha = jnp.exp2(m_prev - m_new)
                p = jnp.exp2(s - m_new)
                l_s[rows, :] = alpha * l_s[rows, :] + _fold_lanes(p, jnp.add)
                a_s[rows, :] = alpha * a_s[rows, :] + _d(p, vblk, ((1,), (0,)))
                m_s[rows, :] = m_new

        @pl.when(kk == 0)
        def _():
            if has_sink:
                sv = jnp.max(s_ref[0], axis=1, keepdims=True) * LOG2E
                m_s[...] = jnp.zeros((tq, 1), F32) + sv
                l_s[...] = (lax.broadcasted_iota(jnp.int32, (tq, 128), 1) == 0).astype(F32)
            else:
                m_s[...] = jnp.full((tq, 1), NEG, F32)
                l_s[...] = jnp.zeros((tq, 128), F32)
            a_s[...] = jnp.zeros((tq, vw), F32)
            if extra:
                step(ke_ref[...], ve_ref[...])

        step(k_ref[...], v_ref[...])

        @pl.when(kk == nk - 1)
        def _():
            l = jnp.sum(l_s[...], axis=1, keepdims=True)
            o_ref[...] = (a_s[...] / l).astype(BF16)
            l_ref[0] = m_s[...] + jnp.log2(l)

    ins = [(qa, pl.BlockSpec((tq, w), qmap)), (ka, pl.BlockSpec((tk, w), kmap)), (va, pl.BlockSpec((tk, vw), vmap))]
    if extra:
        ins += [(ka, pl.BlockSpec((NCTX, w), lambda h, i, kk: (cblk, h // grp))),
                (va, pl.BlockSpec((NCTX, vw), lambda h, i, kk: (cblk, vcol0 + h // grp)))]
    if has_sink:
        ins += [(sink, pl.BlockSpec((1, 1, 128), lambda h, i, kk: (h, 0, 0)))]
    aliases = {}
    if prev is not None:
        any_spec = pl.BlockSpec(memory_space=pl.ANY)
        aliases = {len(ins): 0, len(ins) + 1: 1}
        ins += [(prev[0], any_spec), (prev[1], any_spec)]
    return pl.pallas_call(
        body, out_shape=[_sd((n, hq * vw), BF16), _sd((hq, n, 1), F32)], grid=grid,
        in_specs=[s for _, s in ins],
        out_specs=[pl.BlockSpec((tq, vw), omap), pl.BlockSpec((1, tq, 1), lmap)],
        scratch_shapes=[pltpu.VMEM((tq, 1), F32), pltpu.VMEM((tq, 128), F32), pltpu.VMEM((tq, vw), F32)],
        input_output_aliases=aliases, name=name,
        compiler_params=_cp(("parallel", "parallel", "arbitrary")))(*[a for a, _ in ins])


def flash_dq(name, qa, ka, va, oa, doa, lse, *, w, vw, hq, grp, vcol0, scale, nlat, tq, tk, band, sink, ctx_q, prev=None):
    n = qa.shape[0]
    cblk = nlat // NCTX
    band = band and not ctx_q
    if ctx_q:
        tq = tk = NCTX
        grid = (hq, 1, 1)
        qmap = lambda h, i, kk: (cblk, h)
        kmap = lambda h, i, kk: (cblk, h // grp)
        vmap = lambda h, i, kk: (cblk, vcol0 + h // grp)
        lmap = lambda h, i, kk: (h, cblk, 0)
    else:
        nb = nlat // tk
        grid = (hq, nlat // tq, 3 if band else nb)
        kb_of = (lambda i, kk: jnp.clip(i + kk - 1, 0, nb - 1)) if band else (lambda i, kk: kk)
        qmap = lambda h, i, kk: (i, h)
        kmap = lambda h, i, kk: (kb_of(i, kk), h // grp)
        vmap = lambda h, i, kk: (kb_of(i, kk), vcol0 + h // grp)
        lmap = lambda h, i, kk: (h, i, 0)
    nk = grid[2]
    nq = grid[1]
    extra = not ctx_q
    has_sink = sink is not None

    def body(*refs):
        refs = list(refs)
        q_ref, k_ref, v_ref, o_ref, do_ref, l_ref = refs[:6]
        pos = 6
        if extra:
            ke_ref, ve_ref = refs[pos:pos + 2]
            pos += 2
        if has_sink:
            s_ref = refs[pos]
            pos += 1
        if prev is not None:
            pos += 2
        dq_ref, dl_ref, ds_ref, acc_s, dl_s = refs[pos:pos + 5]
        i = pl.program_id(1)
        kk = pl.program_id(2)
        q = q_ref[...]
        do = do_ref[...]
        lse_v = l_ref[0]

        def step(kblk, vblk, mask):
            s = _d(q, kblk, ((1,), (1,))) * (scale * LOG2E)
            if mask is not None:
                s = jnp.where(mask, s, NEG)
            p = jnp.exp2(s - lse_v)
            dp = _d(do, vblk, ((1,), (1,)))
            ds = p * (dp - dl_s[...]) * scale
            acc_s[...] += _d(ds, kblk, ((1,), (0,)))

        @pl.when(kk == 0)
        def _():
            delta = jnp.sum(do * o_ref[...].astype(F32), axis=1, keepdims=True)
            dl_s[...] = delta
            acc_s[...] = jnp.zeros((tq, w), F32)
            if has_sink:
                sv = jnp.max(s_ref[0], axis=1, keepdims=True) * LOG2E
                dsk = jnp.sum(-jnp.exp2(sv - lse_v) * delta, axis=0, keepdims=True)
                _acc(ds_ref, jnp.zeros((1, 1, 128), F32) + dsk, i == 0)
            else:
                ds_ref[...] = jnp.zeros((1, 1, 128), F32)
            if extra:
                step(ke_ref[...], ve_ref[...], None)

        if band:
            kb = i + kk - 1

            @pl.when((kb >= 0) & (kb < nlat // tk))
            def _():
                step(k_ref[...], v_ref[...], _band_mask(tq, tk, i, kb))
        else:
            step(k_ref[...], v_ref[...], None)

        @pl.when(kk == nk - 1)
        def _():
            dq_ref[...] = acc_s[...]
            dl_ref[0] = dl_s[...]

    ins = [(qa, pl.BlockSpec((tq, w), qmap)), (ka, pl.BlockSpec((tk, w), kmap)), (va, pl.BlockSpec((tk, vw), vmap)),
           (oa, pl.BlockSpec((tq, vw), qmap)), (doa, pl.BlockSpec((tq, vw), qmap)), (lse, pl.BlockSpec((1, tq, 1), lmap))]
    if extra:
        ins += [(ka, pl.BlockSpec((NCTX, w), lambda h, i, kk: (cblk, h // grp))),
                (va, pl.BlockSpec((NCTX, vw), lambda h, i, kk: (cblk, vcol0 + h // grp)))]
    if has_sink:
        ins += [(sink, pl.BlockSpec((1, 1, 128), lambda h, i, kk: (h, 0, 0)))]
    aliases = {}
    if prev is not None:
        any_spec = pl.BlockSpec(memory_space=pl.ANY)
        aliases = {len(ins): 0, len(ins) + 1: 1}
        ins += [(prev[0], any_spec), (prev[1], any_spec)]
    del nq
    return pl.pallas_call(
        body, out_shape=[_sd((n, hq * w), F32), _sd((hq, n, 1), F32), _sd((hq, 1, 128), F32)], grid=grid,
        in_specs=[s for _, s in ins],
        out_specs=[pl.BlockSpec((tq, w), qmap), pl.BlockSpec((1, tq, 1), lmap),
                   pl.BlockSpec((1, 1, 128), lambda h, i, kk: (h, 0, 0))],
        scratch_shapes=[pltpu.VMEM((tq, w), F32), pltpu.VMEM((tq, 1), F32)],
        input_output_aliases=aliases, name=name,
        compiler_params=_cp(("parallel", "arbitrary", "arbitrary")))(*[a for a, _ in ins])


def flash_dkv(name, qa, ka, va, doa, lse, delta, *, w, vw, hkv, grp, vcol0, scale, nlat, tq, tk, band, ctx_k, prev=None):
    n = qa.shape[0]
    cblk = nlat // NCTX
    nqb = nlat // tq
    band = band and not ctx_k
    if ctx_k:
        tk = NCTX
        nqs = nqb
        grid = (hkv, 1, grp * nqs)
        kmap = lambda hk, j, t: (cblk, hk)
        vmap = lambda hk, j, t: (cblk, vcol0 + hk)
        dvmap = lambda hk, j, t: (cblk, hk)
        qb_of = lambda j, t: t % nqs
    else:
        nqs = 3 if band else nqb
        grid = (hkv, nlat // tk, grp * nqs)
        kmap = lambda hk, j, t: (j, hk)
        vmap = lambda hk, j, t: (j, vcol0 + hk)
        dvmap = lambda hk, j, t: (j, hk)
        qb_of = (lambda j, t: jnp.clip(j + t % nqs - 1, 0, nqb - 1)) if band else (lambda j, t: t % nqs)
    qmap = lambda hk, j, t: (qb_of(j, t), hk * grp + t // nqs)
    lmap = lambda hk, j, t: (hk * grp + t // nqs, qb_of(j, t), 0)

    def body(*refs):
        refs = list(refs)
        q_ref, k_ref, v_ref, do_ref, l_ref, dl_ref = refs[:6]
        pos = 6
        if ctx_k:
            qe_ref, doe_ref, le_ref, dle_ref = refs[pos:pos + 4]
            pos += 4
        if prev is not None:
            pos += 2
        dk_ref, dv_ref = refs[pos:pos + 2]
        j = pl.program_id(1)
        t = pl.program_id(2)
        kblk = k_ref[...]
        vblk = v_ref[...]

        def contrib(q, do, lse_v, dl_v, mask):
            s = _d(q, kblk, ((1,), (1,))) * (scale * LOG2E)
            if mask is not None:
                s = jnp.where(mask, s, NEG)
            p = jnp.exp2(s - lse_v)
            dp = _d(do, vblk, ((1,), (1,)))
            ds = p * (dp - dl_v) * scale
            return _d(ds, q, ((0,), (0,))), _d(p, do, ((0,), (0,)))

        @pl.when(t == 0)
        def _():
            dk = jnp.zeros((tk, w), F32)
            dv = jnp.zeros((tk, vw), F32)
            if ctx_k:
                for gi in range(grp):
                    a, b = contrib(qe_ref[:, w * gi:w * gi + w], doe_ref[:, vw * gi:vw * gi + vw], le_ref[gi], dle_ref[gi], None)
                    dk = dk + a
                    dv = dv + b
            dk_ref[...] = dk
            dv_ref[...] = dv

        def add(mask):
            a, b = contrib(q_ref[...], do_ref[...], l_ref[0], dl_ref[0], mask)
            dk_ref[...] += a
            dv_ref[...] += b

        if band:
            qb = j + t % nqs - 1

            @pl.when((qb >= 0) & (qb < nqb))
            def _():
                add(_band_mask(tq, tk, qb, j))
        else:
            add(None)

    ins = [(qa, pl.BlockSpec((tq, w), qmap)), (ka, pl.BlockSpec((tk, w), kmap)), (va, pl.BlockSpec((tk, vw), vmap)),
           (doa, pl.BlockSpec((tq, vw), qmap)), (lse, pl.BlockSpec((1, tq, 1), lmap)), (delta, pl.BlockSpec((1, tq, 1), lmap))]
    if ctx_k:
        ins += [(qa, pl.BlockSpec((NCTX, grp * w), lambda hk, j, t: (cblk, hk))),
                (doa, pl.BlockSpec((NCTX, grp * vw), lambda hk, j, t: (cblk, hk))),
                (lse, pl.BlockSpec((grp, NCTX, 1), lambda hk, j, t: (hk, cblk, 0))),
                (delta, pl.BlockSpec((grp, NCTX, 1), lambda hk, j, t: (hk, cblk, 0)))]
    aliases = {}
    if prev is not None:
        any_spec = pl.BlockSpec(memory_space=pl.ANY)
        aliases = {len(ins): 0, len(ins) + 1: 1}
        ins += [(prev[0], any_spec), (prev[1], any_spec)]
    return pl.pallas_call(
        body, out_shape=[_sd((n, hkv * w), F32), _sd((n, hkv * vw), F32)], grid=grid,
        in_specs=[s for _, s in ins],
        out_specs=[pl.BlockSpec((tk, w), kmap), pl.BlockSpec((tk, vw), dvmap)],
        input_output_aliases=aliases, name=name,
        compiler_params=_cp(("parallel", "parallel", "arbitrary")))(*[a for a, _ in ins])


MLA_FWD_CHUNKS = 3
MLA_BWD_CHUNKS = 2


def mla_fwd(name, qm, km, vm, nlat):
    n = qm.shape[0]
    t = NCTX
    nlt = nlat // t
    c = (MLA_NOPE + MLA_ROPE) ** -0.5 * LOG2E

    nchunk = MLA_FWD_CHUNKS if (n // 128) % MLA_FWD_CHUNKS == 0 else 1
    cw = n // nchunk

    def body(q_ref, k_ref, v_ref, o_ref, l_ref):
        i = pl.program_id(1)

        def run(spans):
            parts = []
            for a, b in spans:
                s = _d(q_ref[...], k_ref[a:b, :], ((1,), (1,))) * c
                m = jnp.max(_fold_lanes(s, jnp.maximum), axis=1, keepdims=True)
                p = jnp.exp2(s - m)
                parts.append((m, jnp.sum(_fold_lanes(p, jnp.add), axis=1, keepdims=True), _d(p, v_ref[a:b, :], ((1,), (0,)))))
            m = functools.reduce(jnp.maximum, [pt[0] for pt in parts])
            l = sum(jnp.exp2(pm - m) * pl_ for pm, pl_, _ in parts)
            acc = sum(jnp.exp2(pm - m) * pa for pm, _, pa in parts)
            o_ref[...] = (acc / l).astype(BF16)
            l_ref[0] = m + jnp.log2(l)

        @pl.when(i < nlt)
        def _():
            run([(j * cw, (j + 1) * cw) for j in range(nchunk)])

        @pl.when(i == nlt)
        def _():
            run([(nlat, n)])

    return pl.pallas_call(
        body, out_shape=[_sd((n, MLA_H * 128), BF16), _sd((MLA_H, n, 1), F32)], grid=(MLA_H, n // t),
        in_specs=[pl.BlockSpec((t, 256), lambda h, i: (i, h)), pl.BlockSpec((n, 256), lambda h, i: (0, h)),
                  pl.BlockSpec((n, 128), lambda h, i: (0, h))],
        out_specs=[pl.BlockSpec((t, 128), lambda h, i: (i, h)), pl.BlockSpec((1, t, 1), lambda h, i: (h, i, 0))],
        name=name, compiler_params=_cp(("parallel", "arbitrary")))(qm, km, vm)


def mla_bwd(name, qm, km, vm, o, do, lse, nlat):
    n = qm.shape[0]
    t = NCTX
    nlt = nlat // t
    scale = (MLA_NOPE + MLA_ROPE) ** -0.5
    nchunk = MLA_BWD_CHUNKS if (n // 128) % MLA_BWD_CHUNKS == 0 else 1
    cw = n // nchunk

    def body(q_ref, k_ref, v_ref, o_ref, do_ref, l_ref, dq_ref, dkt_ref, dvt_ref):
        i = pl.program_id(1)

        @pl.when(i == 0)
        def _():
            dkt_ref[...] = jnp.zeros((256, n), F32)
            dvt_ref[...] = jnp.zeros((128, n), F32)

        q = q_ref[...]
        do = do_ref[...]
        delta = jnp.sum(do.astype(F32) * o_ref[...].astype(F32), axis=1, keepdims=True)

        def run(spans):
            dq = jnp.zeros((t, 256), F32)
            for a, b in spans:
                kc = k_ref[a:b, :]
                s = _d(q, kc, ((1,), (1,))) * (scale * LOG2E)
                p = jnp.exp2(s - l_ref[0])
                ds = (p * (_d(do, v_ref[a:b, :], ((1,), (1,))) - delta) * scale).astype(BF16)
                dq = dq + _d(ds, kc, ((1,), (0,)))
                dkt_ref[:, a:b] += _d(q, ds, ((0,), (0,)))
                dvt_ref[:, a:b] += _d(do, p, ((0,), (0,)))
            dq_ref[...] = dq

        @pl.when(i < nlt)
        def _():
            run([(c * cw, (c + 1) * cw) for c in range(nchunk)])

        @pl.when(i == nlt)
        def _():
            run([(nlat, n)])

    qspec = pl.BlockSpec((t, 256), lambda h, i: (i, h))
    ospec = pl.BlockSpec((t, 128), lambda h, i: (i, h))
    return pl.pallas_call(
        body, out_shape=[_sd((n, MLA_H * 256), F32), _sd((MLA_H * 256, n), F32), _sd((MLA_H * 128, n), F32)],
        grid=(MLA_H, n // t),
        in_specs=[qspec, pl.BlockSpec((n, 256), lambda h, i: (0, h)), pl.BlockSpec((n, 128), lambda h, i: (0, h)),
                  ospec, ospec, pl.BlockSpec((1, t, 1), lambda h, i: (h, i, 0))],
        out_specs=[qspec, pl.BlockSpec((256, n), lambda h, i: (h, 0)), pl.BlockSpec((128, n), lambda h, i: (h, 0))],
        name=name, compiler_params=_cp(("parallel", "arbitrary")))(qm, km, vm, o, do, lse)


def mla_attention_bwd(tag, qm, km, vm, o, do, lse, nlat):
    return mla_bwd(tag + "_bwd", qm, km, vm, o, do, lse, nlat)


SWA_T = 512


def _swa_window(t, nlat):
    t = min(t, nlat)
    return t, min(t + 2 * SWA_WIN, nlat)


def _win_start(i, t, wlen, nlat):
    return pl.multiple_of(jnp.clip(i * t - SWA_WIN, 0, nlat - wlen), 128)


def _win_mask(rows, cols, row0, col0):
    rp = row0 + lax.broadcasted_iota(jnp.int32, (rows, cols), 0)
    cp = col0 + lax.broadcasted_iota(jnp.int32, (rows, cols), 1)
    return jnp.abs(rp - cp) <= SWA_WIN


def swa_fwd_lat(name, qs, ks, u, sink, nlat):
    n = qs.shape[0]
    tq, wlen = _swa_window(SWA_T, nlat)
    grp = SWA_HQ // SWA_HKV
    scale = SWA_DH ** -0.5
    vcol0 = C_V // 128

    def body(q_ref, k_ref, v_ref, s_ref, o_ref, l_ref):
        i = pl.program_id(1)
        ws = _win_start(i, tq, wlen, nlat)
        q = q_ref[...]
        s1 = _d(q, k_ref[pl.ds(ws, wlen), :], ((1,), (1,))) * (scale * LOG2E)
        s1 = jnp.where(_win_mask(tq, wlen, i * tq, ws), s1, NEG)
        s2 = _d(q, k_ref[pl.ds(nlat, NCTX), :], ((1,), (1,))) * (scale * LOG2E)
        sv = jnp.max(s_ref[0], axis=1, keepdims=True) * LOG2E
        m = jnp.maximum(jnp.maximum(jnp.max(s1, axis=1, keepdims=True), jnp.max(s2, axis=1, keepdims=True)), sv)
        p1 = jnp.exp2(s1 - m)
        p2 = jnp.exp2(s2 - m)
        l = jnp.sum(p1, axis=1, keepdims=True) + jnp.sum(p2, axis=1, keepdims=True) + jnp.exp2(sv - m)
        acc = _d(p1, v_ref[pl.ds(ws, wlen), :], ((1,), (0,))) + _d(p2, v_ref[pl.ds(nlat, NCTX), :], ((1,), (0,)))
        o_ref[...] = (acc / l).astype(BF16)
        l_ref[0] = m + jnp.log2(l)

    return pl.pallas_call(
        body, out_shape=[_sd((n, SWA_HQ * 128), BF16), _sd((SWA_HQ, n, 1), F32)], grid=(SWA_HQ, nlat // tq),
        in_specs=[pl.BlockSpec((tq, 128), lambda h, i: (i, h)), pl.BlockSpec((n, 128), lambda h, i: (0, h // grp)),
                  pl.BlockSpec((n, 128), lambda h, i: (0, vcol0 + h // grp)), pl.BlockSpec((1, 1, 128), lambda h, i: (h, 0, 0))],
        out_specs=[pl.BlockSpec((tq, 128), lambda h, i: (i, h)), pl.BlockSpec((1, tq, 1), lambda h, i: (h, i, 0))],
        name=name, compiler_params=_cp(("parallel", "arbitrary")))(qs, ks, u, sink)


def swa_dq_lat(name, qs, ks, u, o, do, lse, sink, nlat):
    n = qs.shape[0]
    tq, wlen = _swa_window(SWA_T, nlat)
    grp = SWA_HQ // SWA_HKV
    scale = SWA_DH ** -0.5
    vcol0 = C_V // 128

    def body(q_ref, k_ref, v_ref, s_ref, o_ref, do_ref, l_ref, dq_ref, dl_ref, ds_ref):
        i = pl.program_id(1)
        ws = _win_start(i, tq, wlen, nlat)
        q = q_ref[...]
        do = do_ref[...]
        lse_v = l_ref[0]
        delta = jnp.sum(do.astype(F32) * o_ref[...].astype(F32), axis=1, keepdims=True)
        kw = k_ref[pl.ds(ws, wlen), :]
        kc = k_ref[pl.ds(nlat, NCTX), :]
        s1 = _d(q, kw, ((1,), (1,))) * (scale * LOG2E)
        s1 = jnp.where(_win_mask(tq, wlen, i * tq, ws), s1, NEG)
        s2 = _d(q, kc, ((1,), (1,))) * (scale * LOG2E)
        ds1 = jnp.exp2(s1 - lse_v) * (_d(do, v_ref[pl.ds(ws, wlen), :], ((1,), (1,))) - delta) * scale
        ds2 = jnp.exp2(s2 - lse_v) * (_d(do, v_ref[pl.ds(nlat, NCTX), :], ((1,), (1,))) - delta) * scale
        dq_ref[...] = _d(ds1, kw, ((1,), (0,))) + _d(ds2, kc, ((1,), (0,)))
        dl_ref[0] = delta
        sv = jnp.max(s_ref[0], axis=1, keepdims=True) * LOG2E
        dsk = jnp.sum(-jnp.exp2(sv - lse_v) * delta, axis=0, keepdims=True)
        _acc(ds_ref, jnp.zeros((1, 1, 128), F32) + dsk, i == 0)

    qspec = pl.BlockSpec((tq, 128), lambda h, i: (i, h))
    lspec = pl.BlockSpec((1, tq, 1), lambda h, i: (h, i, 0))
    return pl.pallas_call(
        body, out_shape=[_sd((n, SWA_HQ * 128), F32), _sd((SWA_HQ, n, 1), F32), _sd((SWA_HQ, 1, 128), F32)],
        grid=(SWA_HQ, nlat // tq),
        in_specs=[qspec, pl.BlockSpec((n, 128), lambda h, i: (0, h // grp)),
                  pl.BlockSpec((n, 128), lambda h, i: (0, vcol0 + h // grp)), pl.BlockSpec((1, 1, 128), lambda h, i: (h, 0, 0)),
                  qspec, qspec, lspec],
        out_specs=[qspec, lspec, pl.BlockSpec((1, 1, 128), lambda h, i: (h, 0, 0))],
        name=name, compiler_params=_cp(("parallel", "arbitrary")))(qs, ks, u, sink, o, do, lse)


def swa_dkv_lat(name, qs, ks, u, do, lse_row, delta_row, nlat):
    n = qs.shape[0]
    tk, wlen = _swa_window(SWA_T, nlat)
    grp = SWA_HQ // SWA_HKV
    scale = SWA_DH ** -0.5
    vcol0 = C_V // 128

    def body(q_ref, k_ref, v_ref, do_ref, l_ref, dl_ref, dk_ref, dv_ref):
        j = pl.program_id(1)
        ws = _win_start(j, tk, wlen, nlat)
        k = k_ref[...]
        v = v_ref[...]
        mask = _win_mask(tk, wlen, j * tk, ws)
        dk = jnp.zeros((tk, 128), F32)
        dv = jnp.zeros((tk, 128), F32)
        for gi in range(grp):
            qw = q_ref[pl.ds(ws, wlen), 128 * gi:128 * gi + 128]
            dow = do_ref[pl.ds(ws, wlen), 128 * gi:128 * gi + 128]
            st = jnp.where(mask, _d(k, qw, ((1,), (1,))) * (scale * LOG2E), NEG)
            pt = jnp.exp2(st - l_ref[gi, :, pl.ds(ws, wlen)])
            dv = dv + _d(pt, dow, ((1,), (0,)))
            dst = pt * (_d(v, dow, ((1,), (1,))) - dl_ref[gi, :, pl.ds(ws, wlen)]) * scale
            dk = dk + _d(dst, qw, ((1,), (0,)))
        dk_ref[...] = dk
        dv_ref[...] = dv

    rspec = pl.BlockSpec((grp, 1, n), lambda hk, j: (hk, 0, 0))
    return pl.pallas_call(
        body, out_shape=[_sd((n, SWA_HKV * 128), F32), _sd((n, SWA_HKV * 128), F32)], grid=(SWA_HKV, nlat // tk),
        in_specs=[pl.BlockSpec((n, grp * 128), lambda hk, j: (0, hk)), pl.BlockSpec((tk, 128), lambda hk, j: (j, hk)),
                  pl.BlockSpec((tk, 128), lambda hk, j: (j, vcol0 + hk)), pl.BlockSpec((n, grp * 128), lambda hk, j: (0, hk)),
                  rspec, rspec],
        out_specs=[pl.BlockSpec((tk, 128), lambda hk, j: (j, hk)), pl.BlockSpec((tk, 128), lambda hk, j: (j, hk))],
        name=name, compiler_params=_cp(("parallel", "arbitrary")))(qs, ks, u, do, lse_row, delta_row)


def swa_attention_fwd(tag, qs, ks, u, sink, cfg, nlat):
    o, lse = swa_fwd_lat(tag + "_fwd_lat", qs, ks, u, sink, nlat)
    return flash_fwd(tag + "_fwd_ctx", qs, ks, u, sink=sink, ctx_q=True, nlat=nlat, prev=(o, lse), **cfg)


def swa_attention_bwd(tag, qs, ks, u, o, do, lse, sink, cfg, nlat):
    n = qs.shape[0]
    dq, delta, ds1 = swa_dq_lat(tag + "_dq_lat", qs, ks, u, o, do, lse, sink, nlat)
    dq, delta, ds2 = flash_dq(tag + "_dq_ctx", qs, ks, u, o, do, lse, sink=sink, ctx_q=True, nlat=nlat, prev=(dq, delta), **cfg)
    dk, dv = swa_dkv_lat(tag + "_dkv_lat", qs, ks, u, do, lse.reshape(SWA_HQ, 1, n), delta.reshape(SWA_HQ, 1, n), nlat)
    kc = {k: v for k, v in cfg.items() if k != "hq"}
    kc["hkv"] = SWA_HKV
    kc["tq"] = min(1024, nlat)
    dk, dv = flash_dkv(tag + "_dkv_ctx", qs, ks, u, do, lse, delta, ctx_k=True, nlat=nlat, prev=(dk, dv), **kc)
    return dq, dk, dv, ds1 + ds2


def adamw(name, w, g, m, v):
    r, c = w.shape
    tr = _pick(r, (256, 128, 64, 32, 16, 8))
    bc1 = 1.0 - ADAM_B1 ** ADAM_STEP
    bc2 = 1.0 - ADAM_B2 ** ADAM_STEP

    def body(w_ref, g_ref, m_ref, v_ref, d_ref, nm_ref, nv_ref):
        gv = g_ref[...]
        nm = ADAM_B1 * m_ref[...] + (1.0 - ADAM_B1) * gv
        nv = ADAM_B2 * v_ref[...] + (1.0 - ADAM_B2) * (gv * gv)
        d_ref[...] = -ADAM_LR * ((nm / bc1) / (jnp.sqrt(nv / bc2) + ADAM_EPS) + ADAM_WD * w_ref[...])
        nm_ref[...] = nm
        nv_ref[...] = nv

    spec = pl.BlockSpec((tr, c), lambda i: (i, 0))
    return pl.pallas_call(body, out_shape=[_sd((r, c), F32)] * 3, grid=(r // tr,), in_specs=[spec] * 4, out_specs=[spec] * 3,
                          name=name, compiler_params=_cp(("parallel",)))(w, g, m, v)


def _coords():
    return lax.axis_index("x"), lax.axis_index("y"), lax.axis_index("c")


_ANY = pl.BlockSpec(memory_space=pl.ANY)


def _chip():
    return 2 * lax.axis_index("x") + lax.axis_index("y")


def _per_core(fn):
    c = lax.axis_index("c")
    for cs in (0, 1):
        pl.when(c == cs)(functools.partial(fn, cs))


def gather_chips(name, arrs):
    nj = len(arrs)
    halves = [a.shape[0] // 2 for a in arrs]

    def body(*refs):
        _per_core(functools.partial(run, refs[:nj], refs[nj:2 * nj], *refs[2 * nj:]))

    def run(a_refs, o_refs, ici_send, ici_recv, d2d_send, d2d_recv, c):
        x, y, _ = _coords()
        me = 2 * x + y
        peers = [(1 - x, y), (x, 1 - y), (1 - x, 1 - y)]
        mine = [pl.ds(c * h, h) for h in halves]
        sibs = [pl.ds((1 - c) * h, h) for h in halves]

        def ici(k, j, blk):
            return pltpu.make_async_remote_copy(a_refs[j].at[mine[j]], o_refs[j].at[blk, mine[j]], ici_send.at[k * nj + j],
                                                ici_recv.at[k * nj + j], device_id=(*peers[k], c), device_id_type=MESH)

        def d2d(k, j, rows):
            blk = 2 * peers[k][0] + peers[k][1]
            return pltpu.make_async_remote_copy(o_refs[j].at[blk, rows[j]], o_refs[j].at[blk, rows[j]], d2d_send.at[k * nj + j],
                                                d2d_recv.at[k * nj + j], device_id=(x, y, 1 - c), device_id_type=MESH)

        sends = [ici(k, j, me) for k in range(3) for j in range(nj)]
        for cp in sends:
            cp.start()
        passed = []
        for k in range(3):
            for j in range(nj):
                ici(k, j, 2 * peers[k][0] + peers[k][1]).wait_recv()
                fw = d2d(k, j, mine)
                fw.start()
                passed.append(fw)
        for k in range(3):
            for j in range(nj):
                d2d(k, j, sibs).wait_recv()
        for cp in sends + passed:
            cp.wait_send()

    outs = pl.pallas_call(
        body, out_shape=[_sd((4,) + a.shape, a.dtype) for a in arrs], in_specs=[_ANY] * nj, out_specs=[_ANY] * nj,
        scratch_shapes=[pltpu.SemaphoreType.DMA((3 * nj,))] * 4,
        name=name, compiler_params=pltpu.CompilerParams(has_side_effects=True))(*arrs)
    return [lax.dynamic_update_index_in_dim(o, a, _chip(), 0) for o, a in zip(outs, arrs)]


def pair_split(name, arrs):
    nj = len(arrs)
    halves = [a.shape[1] // 2 for a in arrs]

    def body(*refs):
        _per_core(functools.partial(run, refs[:nj], refs[nj:2 * nj], *refs[2 * nj:]))

    def run(a_refs, got_refs, send_sems, recv_sems, c):
        x, y, _ = _coords()
        cps = [pltpu.make_async_remote_copy(a_refs[j].at[:, pl.ds((1 - c) * halves[j], halves[j])], got_refs[j],
                                            send_sems.at[j], recv_sems.at[j], device_id=(x, y, 1 - c), device_id_type=MESH)
               for j in range(nj)]
        for cp in cps:
            cp.start()
        for cp in cps:
            cp.wait()

    got = pl.pallas_call(
        body, out_shape=[_sd((4, h, a.shape[2]), a.dtype) for a, h in zip(arrs, halves)], in_specs=[_ANY] * nj,
        out_specs=[_ANY] * nj, scratch_shapes=[pltpu.SemaphoreType.DMA((nj,))] * 2,
        name=name, compiler_params=pltpu.CompilerParams(has_side_effects=True))(*arrs)
    own = [lax.dynamic_slice_in_dim(a, lax.axis_index("c") * h, h, axis=1) for a, h in zip(arrs, halves)]
    return own, got


def scatter_chips(name, arrs):
    nj = len(arrs)

    def body(*refs):
        a_refs, o_refs = refs[:nj], refs[nj:2 * nj]
        send_sems, recv_sems = refs[2 * nj:]
        x, y, c = _coords()
        me = 2 * x + y
        peers = [(1 - x, y), (x, 1 - y), (1 - x, 1 - y)]

        def cp(k, j, src_blk, dst_blk):
            return pltpu.make_async_remote_copy(a_refs[j].at[src_blk], o_refs[j].at[dst_blk], send_sems.at[k * nj + j],
                                                recv_sems.at[k * nj + j], device_id=(*peers[k], c), device_id_type=MESH)

        sends = [cp(k, j, 2 * peers[k][0] + peers[k][1], me) for k in range(3) for j in range(nj)]
        for s in sends:
            s.start()
        for k in range(3):
            for j in range(nj):
                cp(k, j, me, 2 * peers[k][0] + peers[k][1]).wait_recv()
        for s in sends:
            s.wait_send()

    outs = pl.pallas_call(
        body, out_shape=[_sd(a.shape, a.dtype) for a in arrs], in_specs=[_ANY] * nj, out_specs=[_ANY] * nj,
        scratch_shapes=[pltpu.SemaphoreType.DMA((3 * nj,))] * 2,
        name=name, compiler_params=pltpu.CompilerParams(has_side_effects=True))(*arrs)
    return [lax.dynamic_update_index_in_dim(o, lax.dynamic_index_in_dim(a, _chip(), 0, keepdims=False), _chip(), 0)
            for o, a in zip(outs, arrs)]


def pair_join(name, arrs):
    nj = len(arrs)
    halves = [a.shape[0] for a in arrs]

    def body(*refs):
        _per_core(functools.partial(run, refs[:nj], refs[nj:2 * nj], *refs[2 * nj:]))

    def run(a_refs, o_refs, send_sems, recv_sems, c):
        x, y, _ = _coords()

        def cp(j, rows_of):
            return pltpu.make_async_remote_copy(a_refs[j], o_refs[j].at[pl.ds(rows_of * halves[j], halves[j])], send_sems.at[j],
                                                recv_sems.at[j], device_id=(x, y, 1 - c), device_id_type=MESH)

        sends = [cp(j, c) for j in range(nj)]
        for s in sends:
            s.start()
        for s in sends:
            s.wait_send()
        for j in range(nj):
            cp(j, 1 - c).wait_recv()

    outs = pl.pallas_call(
        body, out_shape=[_sd((2 * a.shape[0], a.shape[1]), a.dtype) for a in arrs], in_specs=[_ANY] * nj, out_specs=[_ANY] * nj,
        scratch_shapes=[pltpu.SemaphoreType.DMA((nj,))] * 2,
        name=name, compiler_params=pltpu.CompilerParams(has_side_effects=True))(*arrs)
    return [lax.dynamic_update_slice_in_dim(o, a, lax.axis_index("c") * a.shape[0], axis=0) for o, a in zip(outs, arrs)]


def add_cast(name, a, b, dtype):
    k, r, c = a.shape
    tr = _pick(r, (512, 256, 128, 64, 32, 16, 8))

    def body(a_ref, b_ref, o_ref):
        o_ref[...] = (a_ref[...].astype(F32) + b_ref[...].astype(F32)).astype(dtype)

    spec = pl.BlockSpec((1, tr, c), lambda s, i: (s, i, 0))
    return pl.pallas_call(body, out_shape=_sd((k, r, c), dtype), grid=(k, r // tr), in_specs=[spec, spec], out_specs=spec,
                          name=name, compiler_params=_cp(("parallel", "parallel")))(a, b)


def gather_all(name, a):
    def body(a_ref, o_ref, send_sems, recv_sems, loc_sem):
        x, y, c = _coords()
        me = 4 * x + 2 * y + c
        flips = [(fx, fy, fc) for fx in (0, 1) for fy in (0, 1) for fc in (0, 1) if fx + fy + fc > 0]
        peers = [(x ^ fx, y ^ fy, c ^ fc) for fx, fy, fc in flips]
        mine = pltpu.make_async_copy(a_ref, o_ref.at[me], loc_sem)
        mine.start()
        sends = [pltpu.make_async_remote_copy(a_ref, o_ref.at[me], send_sems.at[k], recv_sems.at[k],
                                              device_id=p, device_id_type=MESH) for k, p in enumerate(peers)]
        for cp in sends:
            cp.start()
        for k, (px, py, pc) in enumerate(peers):
            pltpu.make_async_remote_copy(a_ref, o_ref.at[4 * px + 2 * py + pc], send_sems.at[k], recv_sems.at[k],
                                         device_id=(px, py, pc), device_id_type=MESH).wait_recv()
        for cp in sends:
            cp.wait_send()
        mine.wait()

    return pl.pallas_call(
        body, out_shape=_sd((8,) + a.shape, a.dtype), in_specs=[_ANY], out_specs=_ANY,
        scratch_shapes=[pltpu.SemaphoreType.DMA((7,)), pltpu.SemaphoreType.DMA((7,)), pltpu.SemaphoreType.DMA],
        name=name, compiler_params=pltpu.CompilerParams(has_side_effects=True))(a)


def sum_blocks(name, a):
    k, r, c = a.shape
    tr = _pick(r, (256, 128, 64, 32, 16, 8))

    def body(a_ref, o_ref):
        acc = a_ref[0].astype(F32)
        for s in range(1, k):
            acc = acc + a_ref[s].astype(F32)
        o_ref[...] = acc

    return pl.pallas_call(body, out_shape=_sd((r, c), F32), grid=(r // tr,),
                          in_specs=[pl.BlockSpec((k, tr, c), lambda i: (0, i, 0))], out_specs=pl.BlockSpec((tr, c), lambda i: (i, 0)),
                          name=name, compiler_params=_cp(("parallel",)))(a)


BIG = ("w_mod", "w_in", "w_mla_uq", "w_mla_ukv", "w_p_ssm", "w_p_swa", "w_p_mla", "w_out", "w_ffn_in", "w_ffn_out")
COL_SHARDED = ("w_mod", "w_in", "w_mla_uq", "w_mla_ukv", "w_ffn_in")
SMALL = ("c_ctx", "b_mod", "norm1_g", "norm2_g", "ssm_conv_w", "ssm_conv_b", "ssm_dt_bias", "ssm_a_log", "ssm_d",
         "ssm_norm_g", "swa_q_norm_g", "swa_k_norm_g", "swa_sink", "mla_q_lat_g", "mla_kv_lat_g", "mla_q_norm_g",
         "mla_k_norm_g")
WEIGHTS = ("c_ctx", "w_mod", "b_mod", "norm1_g", "norm2_g", "w_in", "ssm_conv_w", "ssm_conv_b", "ssm_dt_bias", "ssm_a_log",
           "ssm_d", "ssm_norm_g", "swa_q_norm_g", "swa_k_norm_g", "swa_sink", "mla_q_lat_g", "mla_kv_lat_g", "w_mla_uq",
           "w_mla_ukv", "mla_q_norm_g", "mla_k_norm_g", "w_p_ssm", "w_p_swa", "w_p_mla", "w_out", "w_ffn_in", "w_ffn_out")


def pack_w_in(w):
    z = lambda k: jnp.zeros((w.shape[0], k), w.dtype)
    return jnp.concatenate([w[:, 4832:7904], w[:, 2400:3424], w[:, 3424:4448], w[:, 0:1536], w[:, 1568:1824], w[:, 1824:2080],
                            w[:, 2080:2336], w[:, 2336:2400], w[:, 1536:1568], z(32), z(128), w[:, 4448:4832]], axis=1)


def unpack_w_in(g):
    return jnp.concatenate([g[:, 5120:6656], g[:, 7488:7520], g[:, 6656:6912], g[:, 6912:7168], g[:, 7168:7424], g[:, 7424:7488],
                            g[:, 3072:4096], g[:, 4096:5120], g[:, 7680:8064], g[:, 0:3072]], axis=1)


def pack_ukv(w):
    return w.reshape(MLA_KVRANK, MLA_H, 2, 128).transpose(0, 2, 1, 3).reshape(MLA_KVRANK, 2048)


def unpack_ukv(g):
    return g.reshape(MLA_KVRANK, 2, MLA_H, 128).transpose(0, 2, 1, 3).reshape(MLA_KVRANK, 2048)


def pack_uq(w):
    return jnp.pad(w.reshape(MLA_QRANK, MLA_H, 192), ((0, 0), (0, 0), (0, 64))).reshape(MLA_QRANK, 2048)


def unpack_uq(g):
    return g.reshape(MLA_QRANK, MLA_H, 256)[:, :, :192].reshape(MLA_QRANK, 1536)


def rope_tables(nlat):
    t = jnp.arange(nlat, dtype=jnp.int32)
    r = (t // GRID_W).astype(F32)[:, None]
    col = (t % GRID_W).astype(F32)[:, None]

    def tab(nf, pad):
        inv = jnp.power(ROPE_BASE, -jnp.arange(nf, dtype=F32) / nf)
        ar, ac = r * inv, col * inv
        cos = jnp.concatenate([jnp.cos(ar), jnp.cos(ar), jnp.cos(ac), jnp.cos(ac), jnp.ones((nlat, pad), F32)], axis=1)
        sin = jnp.concatenate([-jnp.sin(ar), jnp.sin(ar), -jnp.sin(ac), jnp.sin(ac), jnp.zeros((nlat, pad), F32)], axis=1)
        cos = jnp.concatenate([cos, jnp.ones((NCTX, 128), F32)], axis=0)
        sin = jnp.concatenate([sin, jnp.zeros((NCTX, 128), F32)], axis=0)
        return cos, sin

    return tab(32, 0), tab(16, 64)


def _lanes(v, start, width=128):
    return jnp.zeros((1, width), F32).at[0, start:start + v.shape[0]].set(v)


def layer_fwd(i, xin, h, mod, p, tabs, nlat):
    t = "l%d_" % i
    n = xin.shape[0]
    (cos_s, sin_s), (cos_m, sin_m) = tabs
    u = mm(h, p["w_in"], F32, t + "in_proj")
    xbc = conv_fwd(t + "conv", u, p["conv_w"], p["conv_b"], nlat)
    dtrow = jnp.transpose(u[:, C_MISC + DT_LANE:C_MISC + DT_LANE + 32])
    nlc = nlat // Q
    yf, hs_f = ssd_fwd(t + "ssd_f", xbc, u, dtrow, p["bias_c"], p["alog_c"], p["bias_r"], p["alog_r"], nlc, False, 0)
    yb, hs_b = ssd_fwd(t + "ssd_b", xbc, u, dtrow, p["bias_c"], p["alog_c"], p["bias_r"], p["alog_r"], nlc, True, 1)
    ys = ssd_out_fwd(t + "ssd_out", yf, yb, xbc, u, p["ssm_norm_g"], p["d_exp"])
    qs, ks = swa_prep_fwd(t + "swa_prep", u, p["swa_q_g"], p["swa_k_g"], cos_s, sin_s)
    o_swa, lse_swa = swa_attention_fwd(t + "swa", qs, ks, u, p["sink"], p["swa_cfg"], nlat)
    ckv_n, cq_n = lat_norm_fwd(t + "lat_norm", u, p["kv_lat_g"], p["q_lat_g"])
    kv = mm(ckv_n, p["w_ukv"], F32, t + "ukv")
    qp = mm(cq_n, p["w_uq"], F32, t + "uq")
    km, qm, vm = mla_prep_fwd(t + "mla_prep", kv, qp, u, p["mla_q_g"], p["mla_k_g"], cos_m, sin_m)
    o_mla, lse_mla = mla_fwd(t + "mla_fwd", qm, km, vm, nlat)
    p1 = mm(ys, p["w_p_ssm"], BF16, t + "p_ssm")
    p2 = mm(o_swa, p["w_p_swa"], BF16, t + "p_swa")
    p3 = mm(o_mla, p["w_p_mla"], BF16, t + "p_mla")
    merged = merge_fwd(t + "merge", u, p1, p2, p3)
    o = mm(merged, p["w_out"], F32, t + "out_proj")
    x1, h2 = resid_mod_fwd(t + "res1", xin, o, mod, 2, mod, 3, 4, p["norm2_g"], nlat // RT)
    gu = mm(h2, p["w_ffn_in"], BF16, t + "ffn_in")
    a = swiglu_fwd(t + "swiglu", gu)
    f = mm(a, p["w_ffn_out"], F32, t + "ffn_out")
    saved = dict(xin=xin, h=h, u=u, xbc=xbc, dtrow=dtrow, yf=yf, yb=yb, hs_f=hs_f, hs_b=hs_b, ys=ys, qs=qs, ks=ks,
                 o_swa=o_swa, lse_swa=lse_swa, ckv_n=ckv_n, cq_n=cq_n, kv=kv, qp=qp, km=km, qm=qm, vm=vm, o_mla=o_mla,
                 lse_mla=lse_mla, p1=p1, p2=p2, p3=p3, merged=merged, o=o, x1=x1, h2=h2, gu=gu, a=a, f=f)
    del n
    return x1, f, saved


def layer_bwd(i, dx2, df, dgt2, sv, mod, p, tabs, nlat):
    t = "l%db_" % i
    (cos_s, sin_s), (cos_m, sin_m) = tabs
    g = {}
    nt = nlat // RT
    nlc = nlat // Q
    g["w_ffn_out"] = mm_tn(sv["a"], df, t + "wg_ffn_out")
    da = mm(df, p["w_ffn_out"], F32, t + "dg_ffn_out", trans_b=True)
    dgu = swiglu_bwd(t + "swiglu", sv["gu"], da)
    g["w_ffn_in"] = mm_tn(sv["h2"], dgu, t + "wg_ffn_in")
    dh2 = mm(dgu, p["w_ffn_in"], F32, t + "dg_ffn_in", trans_b=True)
    dx1, do, dgt1, dsh2, dsc2, g["norm2_g"] = resid_mod_bwd(t + "res1", sv["x1"], dx2, dh2, sv["o"], mod, 2, mod, 3, 4,
                                                              p["norm2_g"], nt)
    g["w_out"] = mm_tn(sv["merged"], do, t + "wg_out")
    dmerged = mm(do, p["w_out"], F32, t + "dg_out", trans_b=True)
    dp1, dp2, dp3, dgates = merge_bwd(t + "merge", sv["u"], sv["p1"], sv["p2"], sv["p3"], dmerged)
    g["w_p_ssm"] = mm_tn(sv["ys"], dp1, t + "wg_p_ssm")
    g["w_p_swa"] = mm_tn(sv["o_swa"], dp2, t + "wg_p_swa")
    g["w_p_mla"] = mm_tn(sv["o_mla"], dp3, t + "wg_p_mla")
    dys = mm(dp1, p["w_p_ssm"], F32, t + "dg_p_ssm", trans_b=True)
    do_swa = mm(dp2, p["w_p_swa"], BF16, t + "dg_p_swa", trans_b=True)
    do_mla = mm(dp3, p["w_p_mla"], BF16, t + "dg_p_mla", trans_b=True)
    dqm, dkm, dv_mla = mla_attention_bwd(t + "mla", sv["qm"], sv["km"], sv["vm"], sv["o_mla"], do_mla, sv["lse_mla"], nlat)
    dkv, dqp, dkr, g["mla_q_g"], g["mla_k_g"] = mla_prep_bwd(t + "mla_prep", sv["kv"], sv["qp"], sv["u"], p["mla_q_g"],
                                                             p["mla_k_g"], cos_m, sin_m, dkm, dqm, dv_mla)
    g["w_ukv"] = mm_tn(sv["ckv_n"], dkv, t + "wg_ukv")
    g["w_uq"] = mm_tn(sv["cq_n"], dqp, t + "wg_uq")
    dckv_n = mm(dkv, p["w_ukv"], F32, t + "dg_ukv", trans_b=True)
    dcq_n = mm(dqp, p["w_uq"], F32, t + "dg_uq", trans_b=True)
    dckv, dcq, g["kv_lat_g"], g["q_lat_g"] = lat_norm_bwd(t + "lat_norm", sv["u"], p["kv_lat_g"], p["q_lat_g"], dckv_n, dcq_n)
    dqs, dks, dv_swa, g["sink"] = swa_attention_bwd(t + "swa", sv["qs"], sv["ks"], sv["u"], sv["o_swa"], do_swa, sv["lse_swa"],
                                                p["sink"], p["swa_cfg"], nlat)
    dq, dk, dv, g["swa_q_g"], g["swa_k_g"] = swa_prep_bwd(t + "swa_prep", sv["u"], p["swa_q_g"], p["swa_k_g"], cos_s, sin_s,
                                                          dqs, dks, dv_swa)
    dy, dxs_skip, dz, g["ssm_norm_g"], g["d_exp"] = ssd_out_bwd(t + "ssd_out", sv["yf"], sv["yb"], sv["xbc"], sv["u"],
                                                                 p["ssm_norm_g"], p["d_exp"], dys)
    n = dy.shape[0]
    zbc = jnp.zeros((n, 256), F32)
    r_f = ssd_bwd(t + "ssd_f", sv["xbc"], sv["u"], sv["dtrow"], p["bias_c"], p["alog_c"], p["bias_r"], p["alog_r"],
                  sv["hs_f"], dy, (dxs_skip, zbc, zbc), nlc, False, 0)
    r_b = ssd_bwd(t + "ssd_b", sv["xbc"], sv["u"], sv["dtrow"], p["bias_c"], p["alog_c"], p["bias_r"], p["alog_r"],
                  sv["hs_b"], dy, (r_f[0], r_f[1], r_f[2]), nlc, True, 1)
    cv = [conv_bwd(t + "conv_" + nm, sv["u"], r_b[k], p["conv_w"], p["conv_b"], nlat, c0)
          for k, (nm, c0) in enumerate((("x", 0), ("b", 1024), ("c", 1280)))]
    dxbc = [r[0] for r in cv]
    g["conv_w"] = jnp.concatenate([r[1] for r in cv], axis=1)
    g["conv_b"] = jnp.concatenate([r[2] for r in cv], axis=1)
    drow = jnp.concatenate([r_f[4][0] + r_f[4][1], r_b[4][0] + r_b[4][1]], axis=0)
    drow_t = jnp.pad(jnp.transpose(drow), ((0, 0), (DT_LANE, 128 - DT_LANE - 32)))
    dmisc = misc_combine(t + "misc", dkr, r_f[3], r_b[3], drow_t)
    g["bias_c"] = r_f[5] + r_b[5]
    g["alog_c"] = r_f[6] + r_b[6]
    g["bias_r"] = jnp.concatenate([r_f[7], r_b[7]], axis=0)
    g["alog_r"] = jnp.concatenate([r_f[8], r_b[8]], axis=0)
    du = jnp.concatenate([dgates, dz, dq, *dxbc, dk, dv, dckv, dmisc, jnp.zeros((n, 128), BF16), dcq], axis=1)
    g["w_in"] = mm_tn(sv["h"], du, t + "wg_in")
    dh = mm(du, p["w_in"], F32, t + "dg_in", trans_b=True)
    g["mod"] = (dgt1, dsh2, dsc2, dgt2)
    return dx1, dh, g


def local_step(x, c, ctx, target, c_ctx, W, nlat):
    xin = jnp.concatenate([x, ctx], axis=0)
    n = xin.shape[0]
    nt = nlat // RT
    tabs = rope_tables(nlat)
    c8 = jnp.zeros((8, D), F32).at[0].set(c[0]).at[1].set(c_ctx)
    mods, silus = [], []
    for i in range(DEPTH):
        m8, s8 = mod_fwd("l%d_mod" % i, c8, W[i]["w_mod"], W[i]["b_mod"])
        mods.append(m8[0:2].reshape(2, 1, 6 * D))
        silus.append(s8)
    saved = []
    _, h = resid_mod_fwd("l0_norm1", xin, None, None, 0, mods[0], 0, 1, W[0]["norm1_g"], nt)
    xcur = xin
    for i in range(DEPTH):
        x1, f, sv = layer_fwd(i, xcur, h, mods[i], W[i], tabs, nlat)
        saved.append(sv)
        if i + 1 < DEPTH:
            xcur, h = resid_mod_fwd("l%d_res2" % i, x1, f, mods[i], 5, mods[i + 1], 0, 1, W[i + 1]["norm1_g"], nt)
    loss_v, dx2, df, dgt2 = resid_loss("loss", x1, f, mods[DEPTH - 1], 5, target, nt)
    grads = [None] * DEPTH
    for i in reversed(range(DEPTH)):
        dx1, dh, g = layer_bwd(i, dx2, df, dgt2, saved[i], mods[i], W[i], tabs, nlat)
        if i > 0:
            sv = saved[i]
            dx2, df, dgt2, dsh1, dsc1, g["norm1_g"] = resid_mod_bwd(
                "l%db_res2" % (i - 1), sv["xin"], dx1, dh, saved[i - 1]["f"], mods[i - 1], 5, mods[i], 0, 1,
                W[i]["norm1_g"], nt)
        else:
            dxin, _, _, dsh1, dsc1, g["norm1_g"] = resid_mod_bwd("l0b_norm1", saved[0]["xin"], dx1, dh, None, None, 0,
                                                                  mods[0], 0, 1, W[0]["norm1_g"], nt)
        dgt1, dsh2, dsc2, dgt2_i = g.pop("mod")
        dmod = jnp.concatenate([dsh1, dsc1, dgt1, dsh2, dsc2, dgt2_i], axis=2).reshape(2, 6 * D)
        dmod8 = jnp.zeros((8, 6 * D), F32).at[0:2].set(dmod)
        g["w_mod"] = mm_tn(silus[i], dmod8, "l%db_wg_mod" % i)
        dsilu = mm(dmod8, W[i]["w_mod"], F32, "l%db_dg_mod" % i, trans_b=True)
        dc8, g["b_mod"] = mod_small_bwd("l%db_mod_small" % i, c8, dsilu, dmod8)
        g["c8"] = dc8
        grads[i] = g
    del n
    return loss_v[0, 0], dxin, grads


def _full_from_chips(k, a):
    if k in COL_SHARDED:
        return a.transpose(1, 2, 0, 3).reshape(2, a.shape[2], 4 * a.shape[3])
    return a.transpose(1, 0, 2, 3).reshape(2, 4 * a.shape[2], a.shape[3])


def _chips_from_full(k, a):
    if k in COL_SHARDED:
        return a.reshape(a.shape[0], 4, a.shape[1] // 4).transpose(1, 0, 2)
    return a.reshape(4, a.shape[0] // 4, a.shape[1])


def _small_sizes():
    return dict(c_ctx=1024, b_mod=2 * 6144, norm1_g=2048, norm2_g=2048, ssm_conv_w=2 * 5 * 1536, ssm_conv_b=2 * 1536,
                ssm_dt_bias=64, ssm_a_log=64, ssm_d=32, ssm_norm_g=2048, swa_q_norm_g=256, swa_k_norm_g=256, swa_sink=16,
                mla_q_lat_g=768, mla_kv_lat_g=512, mla_q_norm_g=384, mla_k_norm_g=384)


def _pack_small(d):
    parts = []
    for k in SMALL:
        v = d[k].astype(F32).reshape(-1)
        parts.append(jnp.pad(v, (0, (-v.shape[0]) % 1024)))
    return jnp.concatenate(parts).reshape(-1, 128)


def _unpack_small(buf, shapes):
    flat = buf.reshape(-1)
    out = {}
    o = 0
    for k in SMALL:
        sz = _small_sizes()[k]
        out[k] = flat[o:o + sz].reshape(shapes[k])
        o += sz + (-sz) % 1024
    return out


def big_grads(grads):
    gfull = {k: [] for k in BIG}
    for i in range(DEPTH):
        g = grads[i]
        gfull["w_mod"].append(g["w_mod"])
        gfull["w_in"].append(unpack_w_in(g["w_in"]))
        gfull["w_mla_uq"].append(unpack_uq(g["w_uq"]))
        gfull["w_mla_ukv"].append(unpack_ukv(g["w_ukv"]))
        for k in ("w_p_ssm", "w_p_swa", "w_p_mla", "w_out", "w_ffn_in", "w_ffn_out"):
            gfull[k].append(g[k])
    return gfull


def small_grads(grads):
    gs = {}
    gs["c_ctx"] = sum(grads[i]["c8"][1] for i in range(DEPTH))
    st = lambda f: jnp.stack([f(grads[i]) for i in range(DEPTH)])
    gs["b_mod"] = st(lambda g: g["b_mod"][0])
    gs["norm1_g"] = st(lambda g: g["norm1_g"][0])
    gs["norm2_g"] = st(lambda g: g["norm2_g"][0])
    gs["ssm_conv_w"] = st(lambda g: g["conv_w"])
    gs["ssm_conv_b"] = st(lambda g: g["conv_b"][0])
    gs["ssm_dt_bias"] = st(lambda g: (g["bias_c"][0, DT_LANE:DT_LANE + 32] + g["bias_r"][:, 0]).reshape(2, 16))
    gs["ssm_a_log"] = st(lambda g: (g["alog_c"][0, DT_LANE:DT_LANE + 32] + g["alog_r"][:, 0]).reshape(2, 16))
    gs["ssm_d"] = st(lambda g: g["d_exp"].reshape(16, 64).sum(axis=1))
    gs["ssm_norm_g"] = st(lambda g: g["ssm_norm_g"][0])
    gs["swa_q_norm_g"] = st(lambda g: g["swa_q_g"][0])
    gs["swa_k_norm_g"] = st(lambda g: g["swa_k_g"][0])
    gs["swa_sink"] = st(lambda g: g["sink"][:, 0, 0])
    gs["mla_q_lat_g"] = st(lambda g: g["q_lat_g"][0])
    gs["mla_kv_lat_g"] = st(lambda g: g["kv_lat_g"][0])
    gs["mla_q_norm_g"] = st(lambda g: g["mla_q_g"][0, :192])
    gs["mla_k_norm_g"] = st(lambda g: g["mla_k_g"][0, :192])
    return gs


def layer_params(i, full, conv_full, sm, nlat):
    p = {}
    p["w_mod"] = full["w_mod"][i]
    p["w_in"] = pack_w_in(full["w_in"][i])
    p["w_uq"] = pack_uq(full["w_mla_uq"][i])
    p["w_ukv"] = pack_ukv(full["w_mla_ukv"][i])
    for k in ("w_p_ssm", "w_p_swa", "w_p_mla", "w_out", "w_ffn_in", "w_ffn_out"):
        p[k] = full[k][i]
    p["b_mod"] = sm["b_mod"][i][None]
    p["norm1_g"] = sm["norm1_g"][i][None]
    p["norm2_g"] = sm["norm2_g"][i][None]
    p["conv_w"] = conv_full[i]
    p["conv_b"] = sm["ssm_conv_b"][i][None]
    bias = sm["ssm_dt_bias"][i].reshape(32)
    alog = sm["ssm_a_log"][i].reshape(32)
    p["bias_c"] = _lanes(bias, DT_LANE)
    p["alog_c"] = _lanes(alog, DT_LANE)
    p["bias_r"] = bias[:, None]
    p["alog_r"] = alog[:, None]
    p["d_exp"] = jnp.repeat(sm["ssm_d"][i], 64)[None]
    p["ssm_norm_g"] = sm["ssm_norm_g"][i][None]
    p["swa_q_g"] = sm["swa_q_norm_g"][i][None]
    p["swa_k_g"] = sm["swa_k_norm_g"][i][None]
    p["sink"] = jnp.broadcast_to(sm["swa_sink"][i][:, None, None], (SWA_HQ, 1, 128))
    p["q_lat_g"] = sm["mla_q_lat_g"][i][None]
    p["kv_lat_g"] = sm["mla_kv_lat_g"][i][None]
    p["mla_q_g"] = _lanes(sm["mla_q_norm_g"][i], 0, 256)
    p["mla_k_g"] = _lanes(sm["mla_k_norm_g"][i], 0, 256)
    p["swa_cfg"] = dict(w=128, vw=128, hq=SWA_HQ, grp=SWA_HQ // SWA_HKV, vcol0=C_V // 128, scale=SWA_DH ** -0.5,
                        tq=256, tk=256, band=True)
    return p


def kernel(x, c, ctx, c_ctx, w_mod, b_mod, norm1_g, norm2_g, w_in, ssm_conv_w, ssm_conv_b, ssm_dt_bias, ssm_a_log, ssm_d, ssm_norm_g, swa_q_norm_g, swa_k_norm_g, swa_sink, mla_q_lat_g, mla_kv_lat_g, w_mla_uq, w_mla_ukv, mla_q_norm_g, mla_k_norm_g, w_p_ssm, w_p_swa, w_p_mla, w_out, w_ffn_in, w_ffn_out, loss_target, m_c_ctx, m_w_mod, m_b_mod, m_norm1_g, m_norm2_g, m_w_in, m_ssm_conv_w, m_ssm_conv_b, m_ssm_dt_bias, m_ssm_a_log, m_ssm_d, m_ssm_norm_g, m_swa_q_norm_g, m_swa_k_norm_g, m_swa_sink, m_mla_q_lat_g, m_mla_kv_lat_g, m_w_mla_uq, m_w_mla_ukv, m_mla_q_norm_g, m_mla_k_norm_g, m_w_p_ssm, m_w_p_swa, m_w_p_mla, m_w_out, m_w_ffn_in, m_w_ffn_out, v_c_ctx, v_w_mod, v_b_mod, v_norm1_g, v_norm2_g, v_w_in, v_ssm_conv_w, v_ssm_conv_b, v_ssm_dt_bias, v_ssm_a_log, v_ssm_d, v_ssm_norm_g, v_swa_q_norm_g, v_swa_k_norm_g, v_swa_sink, v_mla_q_lat_g, v_mla_kv_lat_g, v_w_mla_uq, v_w_mla_ukv, v_mla_q_norm_g, v_mla_k_norm_g, v_w_p_ssm, v_w_p_swa, v_w_p_mla, v_w_out, v_w_ffn_in, v_w_ffn_out):
    loc = dict(locals())
    w = {k: loc[k] for k in WEIGHTS}
    m = {k: loc["m_" + k] for k in WEIGHTS}
    v = {k: loc["v_" + k] for k in WEIGHTS}
    nlat = x.shape[1]

    sh2 = {k: (w[k].shape[0] * w[k].shape[1], w[k].shape[2]) for k in BIG}
    conv_sh = jnp.pad(ssm_conv_w.reshape(10, 384), ((0, 6), (0, 0)))
    gathered = gather_chips("gather_weights", [w[k].astype(BF16).reshape(sh2[k]) for k in BIG] + [conv_sh])
    full = {k: _full_from_chips(k, g.reshape((4,) + w[k].shape)) for k, g in zip(BIG, gathered)}
    conv_full = gathered[-1][:, :10].reshape(4, 2, 5, 384).transpose(1, 2, 0, 3).reshape(2, 5, 1536)

    W = [layer_params(i, full, conv_full, w, nlat) for i in range(DEPTH)]

    loss_loc, dx, grads = local_step(x[0], c, ctx[0], loss_target[0], c_ctx, W, nlat)

    gfull = big_grads(grads)
    by_chip = {k: jnp.stack([_chips_from_full(k, a) for a in gfull[k]], axis=1) for k in BIG}
    send = [by_chip[k].astype(BF16).reshape((4,) + sh2[k]) for k in BIG]
    own, got = pair_split("pair_split", send)
    pair = [add_cast("pair_sum_" + k, o, g, BF16) for k, o, g in zip(BIG, own, got)]
    recv = scatter_chips("scatter_grads", pair)
    mine = [sum_blocks("sum_chips_" + k, r) for k, r in zip(BIG, recv)]
    gbig = {k: g.reshape(w[k].shape) for k, g in zip(BIG, pair_join("join_cores", mine))}

    gs = small_grads(grads)
    small_all = gather_all("gather_small", _pack_small(gs))
    small_sum = sum_blocks("sum_small", small_all)
    full_shapes = {k: (w[k].shape if k != "ssm_conv_w" else (2, 5, 1536)) for k in SMALL}
    gsmall = _unpack_small(small_sum, full_shapes)
    chip = 2 * lax.axis_index("x") + lax.axis_index("y")
    gsmall["ssm_conv_w"] = lax.dynamic_slice_in_dim(gsmall["ssm_conv_w"], chip * 384, 384, axis=2)

    grad = {**gbig, **gsmall}
    delta, new_m, new_v = {}, {}, {}
    sm = {k: _pack_small_local(d) for k, d in (("w", w), ("g", grad), ("m", m), ("v", v))}
    r = adamw("adamw_small", sm["w"], sm["g"], sm["m"], sm["v"])
    shapes = {k: w[k].shape for k in SMALL}
    for dst, buf in zip((delta, new_m, new_v), r):
        dst.update(_unpack_small_local(buf, shapes))
    for k in BIG:
        sh = w[k].shape
        r = adamw("adamw_" + k, *[a[k].reshape(sh[0] * sh[1], sh[2]) for a in (w, grad, m, v)])
        for dst, buf in zip((delta, new_m, new_v), r):
            dst[k] = buf.reshape(sh)

    loss = lax.psum(loss_loc, ("x", "y", "c"))
    return (loss, dx[None, :nlat], *[grad[k] for k in WEIGHTS], *[delta[k] for k in WEIGHTS],
            *[new_m[k] for k in WEIGHTS], *[new_v[k] for k in WEIGHTS])


def _pack_small_local(d):
    parts = []
    for k in SMALL:
        a = d[k].astype(F32).reshape(-1)
        parts.append(jnp.pad(a, (0, (-a.shape[0]) % 1024)))
    return jnp.concatenate(parts).reshape(-1, 128)


def _unpack_small_local(buf, shapes):
    flat = buf.reshape(-1)
    out = {}
    o = 0
    for k in SMALL:
        sz = math.prod(shapes[k])
        out[k] = flat[o:o + sz].reshape(shapes[k])
        o += sz + (-sz) % 1024
    return out
```
